```python
import jax, jax.numpy as jnp
from jax import lax
import numpy as np

D_MODEL = 1024
BATCH = 8
SEQ = 2048
DEPTH = 1

HEAD_DIM = 64
DIL_CONFIGS = ((128, 1), (512, 4), (2048, 16))
DIL_HEADS = 8
DIL_WIDTH = DIL_HEADS * HEAD_DIM
DIL_BLOCK = 64
NA_HEADS = 8
NA_WIDTH = NA_HEADS * HEAD_DIM
NA_ROWS_MAX = 8
NA_COLS = 16
GRID_W = 64
MEM_LEN = 256
MEM_HEADS = 4
MEM_HEAD_DIM = 128
MEM_WIDTH = MEM_HEADS * MEM_HEAD_DIM
ROPE_THETA = 500000.0
ROPE_DIM = HEAD_DIM // 4
N_BRANCH = 3
BRANCH_WIDTH = 512
EPS = 1e-6
NEG = -1e30

kernel_name = 'hybrid_dilated_neighbourhood_memory_block'


def _in_sizes():
    return ([DIL_WIDTH] * (3 * len(DIL_CONFIGS))
            + [NA_WIDTH] * 3
            + [MEM_WIDTH]
            + [BRANCH_WIDTH] * N_BRANCH
            + [N_BRANCH * D_MODEL])


def _rmsnorm(x, g):
    xf = x.astype(jnp.float32)
    y = xf * lax.rsqrt(jnp.mean(xf * xf, axis=-1, keepdims=True) + EPS)
    return (y * g.astype(jnp.float32)).astype(x.dtype)


def _heads(t, n_heads):
    b, s, w = t.shape
    return t.reshape(b, s, n_heads, w // n_heads).transpose(0, 2, 1, 3)


def _merge_heads(t):
    b, h, s, d = t.shape
    return t.transpose(0, 2, 1, 3).reshape(b, s, h * d)


def _rope_partial(t, pos):
    half = ROPE_DIM // 2
    inv = ROPE_THETA ** (-jnp.arange(half, dtype=jnp.float32) * 2.0 / ROPE_DIM)
    ang = pos[:, None] * inv[None, :]
    cos, sin = jnp.cos(ang), jnp.sin(ang)
    tf = t[..., :ROPE_DIM].astype(jnp.float32)
    t1, t2 = tf[..., :half], tf[..., half:]
    rot = jnp.concatenate([t1 * cos - t2 * sin, t2 * cos + t1 * sin], axis=-1).astype(t.dtype)
    return jnp.concatenate([rot, t[..., ROPE_DIM:]], axis=-1)


def _banded_attention(q, k, v, reach):
    n, L, hd = q.shape
    bq = DIL_BLOCK
    nb = -(-L // bq)
    lp = nb * bq
    qb = jnp.pad(q, ((0, 0), (0, lp - L), (0, 0))).reshape(n, nb, bq, hd)
    kpad = jnp.pad(k, ((0, 0), (bq, lp - L + bq), (0, 0)))
    vpad = jnp.pad(v, ((0, 0), (bq, lp - L + bq), (0, 0)))
    win = jnp.arange(nb)[:, None] * bq + jnp.arange(3 * bq)[None, :]
    kw = kpad[:, win]
    vw = vpad[:, win]
    kpos = win - bq
    qpos = jnp.arange(nb)[:, None] * bq + jnp.arange(bq)[None, :]
    valid = ((kpos[:, None, :] >= 0) & (kpos[:, None, :] < L)
             & (jnp.abs(qpos[:, :, None] - kpos[:, None, :]) <= reach))
    s = jnp.einsum('nbqd,nbkd->nbqk', qb, kw, preferred_element_type=jnp.float32) * (hd ** -0.5)
    s = jnp.where(valid[None], s, NEG)
    m = jnp.max(s, axis=-1, keepdims=True)
    p = jnp.exp(s - m)
    den = jnp.sum(p, axis=-1)
    out = jnp.einsum('nbqk,nbkd->nbqd', p, vw.astype(jnp.float32)) / den[..., None]
    lse = m[..., 0] + jnp.log(den)
    return out.reshape(n, lp, hd)[:, :L], lse.reshape(n, lp)[:, :L]


def _dilated_attention(q, k, v, dilation, reach):
    b, h, s, hd = q.shape
    mlen = s // dilation

    def fold(t):
        return t.reshape(b, h, mlen, dilation, hd).transpose(0, 1, 3, 2, 4).reshape(b * h * dilation, mlen, hd)

    out, lse = _banded_attention(fold(q), fold(k), fold(v), reach)
    out = out.reshape(b, h, dilation, mlen, hd).transpose(0, 1, 3, 2, 4).reshape(b, h, s, hd)
    lse = lse.reshape(b, h, dilation, mlen).transpose(0, 1, 3, 2).reshape(b, h, s)
    return out, lse


def _neighbourhood_attention(q, k, v, rpb):
    b, h, s, hd = q.shape
    rows = s // GRID_W
    kr = min(NA_ROWS_MAX, rows)
    q5 = q.reshape(b, h, rows, GRID_W, hd)
    k5 = k.reshape(b, h, rows, GRID_W, hd)
    v5 = v.reshape(b, h, rows, GRID_W, hd)
    r_ids = jnp.arange(rows)
    r_start = jnp.clip(r_ids - kr // 2, 0, rows - kr)
    row_idx = r_start[:, None] + jnp.arange(kr)[None, :]
    k_rows = k5[:, :, row_idx]
    v_rows = v5[:, :, row_idx]
    c_ids = jnp.arange(GRID_W)
    c_start = jnp.clip(c_ids - NA_COLS // 2, 0, GRID_W - NA_COLS)
    col_mask = (c_ids[None, :] >= c_start[:, None]) & (c_ids[None, :] < c_start[:, None] + NA_COLS)
    dr = row_idx - r_ids[:, None]
    dc = jnp.clip(c_ids[None, :] - c_ids[:, None], -(NA_COLS - 1), NA_COLS - 1)
    bias = rpb[:, dr + NA_ROWS_MAX - 1][..., dc + NA_COLS - 1]
    bias = bias.transpose(0, 1, 3, 2, 4).astype(jnp.float32)
    sc = jnp.einsum('bhrqd,bhrjkd->bhrqjk', q5, k_rows, preferred_element_type=jnp.float32) * (hd ** -0.5)
    sc = jnp.where(col_mask[:, None, :], sc + bias[None], NEG)
    p = jax.nn.softmax(sc, axis=(-2, -1))
    out = jnp.einsum('bhrqjk,bhrjkd->bhrqd', p, v_rows.astype(jnp.float32))
    return out.reshape(b, h, s, hd)


def _cross_attention(q, k, v):
    sc = jnp.einsum('bhqd,bhkd->bhqk', q, k, preferred_element_type=jnp.float32) * (q.shape[-1] ** -0.5)
    p = jax.nn.softmax(sc, axis=-1)
    return jnp.einsum('bhqk,bhkd->bhqd', p, v.astype(jnp.float32))


def _fwd_setup_inputs(seed: int = 0) -> dict:
    key = jax.random.key(seed)
    ks = jax.random.split(key, 14)
    f32 = jnp.float32
    n_in = int(sum(_in_sizes()))

    def nrm(k, shape, scale):
        return jax.random.normal(k, shape, f32) * scale

    return {
        'x': nrm(ks[0], (BATCH, SEQ, D_MODEL), 1.0),
        'mem': nrm(ks[1], (BATCH, MEM_LEN, D_MODEL), 1.0),
        'pre_norm': 1.0 + nrm(ks[2], (DEPTH, D_MODEL), 0.05),
        'w_in': nrm(ks[3], (DEPTH, D_MODEL, n_in), D_MODEL ** -0.5),
        'merge_bias': nrm(ks[4], (DEPTH, N_BRANCH, D_MODEL), 0.1),
        'na_rpb': nrm(ks[5], (DEPTH, NA_HEADS, 2 * NA_ROWS_MAX - 1, 2 * NA_COLS - 1), 0.1),
        'mem_norm': 1.0 + nrm(ks[6], (DEPTH, D_MODEL), 0.05),
        'w_mem_kv': nrm(ks[7], (DEPTH, D_MODEL, 2 * MEM_WIDTH), D_MODEL ** -0.5),
        'w_branch_a': nrm(ks[8], (DEPTH, BRANCH_WIDTH, D_MODEL), BRANCH_WIDTH ** -0.5),
        'w_branch_b': nrm(ks[9], (DEPTH, BRANCH_WIDTH, D_MODEL), BRANCH_WIDTH ** -0.5),
        'w_branch_c': nrm(ks[10], (DEPTH, BRANCH_WIDTH, D_MODEL), BRANCH_WIDTH ** -0.5),
        'w_out': nrm(ks[11], (DEPTH, D_MODEL, D_MODEL), D_MODEL ** -0.5),
        'post_norm': 1.0 + nrm(ks[12], (DEPTH, D_MODEL), 0.05),
    }


def _fwd_reference(x, mem, pre_norm, w_in, merge_bias, na_rpb, mem_norm, w_mem_kv,
              w_branch_a, w_branch_b, w_branch_c, w_out, post_norm):
    b, s, _ = x.shape
    pos = jnp.arange(s, dtype=jnp.float32)
    split_at = np.cumsum(_in_sizes())[:-1].tolist()
    n_dil = len(DIL_CONFIGS)
    off = 3 * n_dil
    for layer in range(DEPTH):
        h = _rmsnorm(x, pre_norm[layer])
        parts = jnp.split(h @ w_in[layer], split_at, axis=-1)

        outs, lses = [], []
        for g, (window, dilation) in enumerate(DIL_CONFIGS):
            q = _rope_partial(_heads(parts[3 * g], DIL_HEADS), pos)
            k = _rope_partial(_heads(parts[3 * g + 1], DIL_HEADS), pos)
            v = _heads(parts[3 * g + 2], DIL_HEADS)
            o, l = _dilated_attention(q, k, v, dilation, (window // 2) // dilation)
            outs.append(o)
            lses.append(l)
        wts = jax.nn.softmax(jnp.stack(lses, axis=0), axis=0)
        out_a = _merge_heads(jnp.sum(wts[..., None] * jnp.stack(outs, axis=0), axis=0).astype(x.dtype))

        out_b = _merge_heads(_neighbourhood_attention(
            _heads(parts[off], NA_HEADS), _heads(parts[off + 1], NA_HEADS),
            _heads(parts[off + 2], NA_HEADS), na_rpb[layer]).astype(x.dtype))

        kv_m = _rmsnorm(mem, mem_norm[layer]) @ w_mem_kv[layer]
        k_m, v_m = jnp.split(kv_m, 2, axis=-1)
        out_c = _merge_heads(_cross_attention(
            _heads(parts[off + 3], MEM_HEADS), _heads(k_m, MEM_HEADS),
            _heads(v_m, MEM_HEADS)).astype(x.dtype))

        g_a, g_b, g_c = parts[off + 4], parts[off + 5], parts[off + 6]
        gate_logits = parts[off + 7].reshape(b, s, N_BRANCH, D_MODEL) + merge_bias[layer]
        gates = jax.nn.sigmoid(gate_logits.astype(jnp.float32)).astype(x.dtype)
        y = (gates[:, :, 0] * ((out_a * jax.nn.silu(g_a)) @ w_branch_a[layer])
             + gates[:, :, 1] * ((out_b * jax.nn.silu(g_b)) @ w_branch_b[layer])
             + gates[:, :, 2] * ((out_c * jax.nn.silu(g_c)) @ w_branch_c[layer]))
        y = y @ w_out[layer]
        x = x + _rmsnorm(y, post_norm[layer])
    return x


import jax as _jax
import jax.numpy as _jnp

TWIN_FORMAT = 'train_step'
FWD_PARAMS = ['x', 'mem', 'pre_norm', 'w_in', 'merge_bias', 'na_rpb', 'mem_norm', 'w_mem_kv', 'w_branch_a', 'w_branch_b', 'w_branch_c', 'w_out', 'post_norm']
TWIN_WEIGHTS = ['pre_norm', 'w_in', 'merge_bias', 'na_rpb', 'mem_norm', 'w_mem_kv', 'w_branch_a', 'w_branch_b', 'w_branch_c', 'w_out', 'post_norm']
TWIN_DIFF_INPUT = 'x'
TWIN_INPUTS = ['x', 'mem', 'pre_norm', 'w_in', 'merge_bias', 'na_rpb', 'mem_norm', 'w_mem_kv', 'w_branch_a', 'w_branch_b', 'w_branch_c', 'w_out', 'post_norm', 'loss_target', 'm_pre_norm', 'm_w_in', 'm_merge_bias', 'm_na_rpb', 'm_mem_norm', 'm_w_mem_kv', 'm_w_branch_a', 'm_w_branch_b', 'm_w_branch_c', 'm_w_out', 'm_post_norm', 'v_pre_norm', 'v_w_in', 'v_merge_bias', 'v_na_rpb', 'v_mem_norm', 'v_w_mem_kv', 'v_w_branch_a', 'v_w_branch_b', 'v_w_branch_c', 'v_w_out', 'v_post_norm']
TWIN_OUTPUTS = ['loss', 'grad_x', 'grad_pre_norm', 'grad_w_in', 'grad_merge_bias', 'grad_na_rpb', 'grad_mem_norm', 'grad_w_mem_kv', 'grad_w_branch_a', 'grad_w_branch_b', 'grad_w_branch_c', 'grad_w_out', 'grad_post_norm', 'delta_pre_norm', 'delta_w_in', 'delta_merge_bias', 'delta_na_rpb', 'delta_mem_norm', 'delta_w_mem_kv', 'delta_w_branch_a', 'delta_w_branch_b', 'delta_w_branch_c', 'delta_w_out', 'delta_post_norm', 'new_m_pre_norm', 'new_m_w_in', 'new_m_merge_bias', 'new_m_na_rpb', 'new_m_mem_norm', 'new_m_w_mem_kv', 'new_m_w_branch_a', 'new_m_w_branch_b', 'new_m_w_branch_c', 'new_m_w_out', 'new_m_post_norm', 'new_v_pre_norm', 'new_v_w_in', 'new_v_merge_bias', 'new_v_na_rpb', 'new_v_mem_norm', 'new_v_w_mem_kv', 'new_v_w_branch_a', 'new_v_w_branch_b', 'new_v_w_branch_c', 'new_v_w_out', 'new_v_post_norm']
TWIN_LEAF_KINDS = {'loss': 'loss', 'grad_x': 'grad_x', 'grad_pre_norm': 'grad_w', 'grad_w_in': 'grad_w', 'grad_merge_bias': 'grad_w', 'grad_na_rpb': 'grad_w', 'grad_mem_norm': 'grad_w', 'grad_w_mem_kv': 'grad_w', 'grad_w_branch_a': 'grad_w', 'grad_w_branch_b': 'grad_w', 'grad_w_branch_c': 'grad_w', 'grad_w_out': 'grad_w', 'grad_post_norm': 'grad_w', 'delta_pre_norm': 'delta_w', 'delta_w_in': 'delta_w', 'delta_merge_bias': 'delta_w', 'delta_na_rpb': 'delta_w', 'delta_mem_norm': 'delta_w', 'delta_w_mem_kv': 'delta_w', 'delta_w_branch_a': 'delta_w', 'delta_w_branch_b': 'delta_w', 'delta_w_branch_c': 'delta_w', 'delta_w_out': 'delta_w', 'delta_post_norm': 'delta_w', 'new_m_pre_norm': 'new_m', 'new_m_w_in': 'new_m', 'new_m_merge_bias': 'new_m', 'new_m_na_rpb': 'new_m', 'new_m_mem_norm': 'new_m', 'new_m_w_mem_kv': 'new_m', 'new_m_w_branch_a': 'new_m', 'new_m_w_branch_b': 'new_m', 'new_m_w_branch_c': 'new_m', 'new_m_w_out': 'new_m', 'new_m_post_norm': 'new_m', 'new_v_pre_norm': 'new_v', 'new_v_w_in': 'new_v', 'new_v_merge_bias': 'new_v', 'new_v_na_rpb': 'new_v', 'new_v_mem_norm': 'new_v', 'new_v_w_mem_kv': 'new_v', 'new_v_w_branch_a': 'new_v', 'new_v_w_branch_b': 'new_v', 'new_v_w_branch_c': 'new_v', 'new_v_w_out': 'new_v', 'new_v_post_norm': 'new_v'}


def _forward(args):
    return _fwd_reference(*[args[k] for k in FWD_PARAMS])


def _output_shape():
    out = _jax.eval_shape(lambda: _forward(_fwd_setup_inputs(0)))
    return out.shape, out.dtype

N_MICROBATCH = 1
ADAM_LR = 0.001
ADAM_B1 = 0.9
ADAM_B2 = 0.999
ADAM_EPS = 1e-08
ADAM_WD = 0.01
ADAM_STEP = 10
PER_EXAMPLE_BATCH_AXIS = {'x': 0, 'mem': 0, 'loss_target': 0}
SHARED_INPUTS = []
_WEIGHT_DTYPES = {'pre_norm': _jnp.float32, 'w_in': _jnp.float32, 'merge_bias': _jnp.float32, 'na_rpb': _jnp.float32, 'mem_norm': _jnp.float32, 'w_mem_kv': _jnp.float32, 'w_branch_a': _jnp.float32, 'w_branch_b': _jnp.float32, 'w_branch_c': _jnp.float32, 'w_out': _jnp.float32, 'post_norm': _jnp.float32}
MOMENT_SCALE = {'pre_norm': 3.410659e-01, 'w_in': 1.040152e-01, 'merge_bias': 4.366236e-02, 'na_rpb': 7.112901e-02, 'mem_norm': 1.814897e-01, 'w_mem_kv': 1.731544e-01, 'w_branch_a': 8.739068e-02, 'w_branch_b': 1.394814e-01, 'w_branch_c': 1.293181e-01, 'w_out': 1.915749e-01, 'post_norm': 1.601303e+01}


def _to_microbatches(a, axis):
    t = _jnp.moveaxis(a, axis, 0)
    t = t.reshape((N_MICROBATCH, t.shape[0] // N_MICROBATCH) + t.shape[1:])
    return _jnp.moveaxis(t, 1, axis + 1)


def setup_inputs(seed: int = 0) -> dict:
    inp = _fwd_setup_inputs(seed)
    key = _jax.random.fold_in(_jax.random.key(seed), 7919)
    shape, _ = _output_shape()
    out = dict(inp)
    out["loss_target"] = _jax.random.normal(_jax.random.fold_in(key, 0), shape, _jnp.float32)
    for i, name in enumerate(TWIN_WEIGHTS):
        w = inp[name].astype(_jnp.float32)
        if MOMENT_SCALE is None:
            s = _jnp.sqrt(_jnp.mean(_jnp.square(w)) + 1e-30)
        else:
            s = MOMENT_SCALE[name]
        km, kv = _jax.random.split(_jax.random.fold_in(key, i + 1))
        out[name] = w
        out["m_" + name] = s * _jax.random.normal(km, w.shape, _jnp.float32)
        out["v_" + name] = (s * s) * _jax.random.uniform(kv, w.shape, _jnp.float32, 0.5, 1.5)
    if N_MICROBATCH > 1:
        for name, axis in PER_EXAMPLE_BATCH_AXIS.items():
            out[name] = _to_microbatches(out[name], axis)
    return {'x': out['x'], 'mem': out['mem'], 'pre_norm': out['pre_norm'], 'w_in': out['w_in'], 'merge_bias': out['merge_bias'], 'na_rpb': out['na_rpb'], 'mem_norm': out['mem_norm'], 'w_mem_kv': out['w_mem_kv'], 'w_branch_a': out['w_branch_a'], 'w_branch_b': out['w_branch_b'], 'w_branch_c': out['w_branch_c'], 'w_out': out['w_out'], 'post_norm': out['post_norm'], 'loss_target': out['loss_target'], 'm_pre_norm': out['m_pre_norm'], 'm_w_in': out['m_w_in'], 'm_merge_bias': out['m_merge_bias'], 'm_na_rpb': out['m_na_rpb'], 'm_mem_norm': out['m_mem_norm'], 'm_w_mem_kv': out['m_w_mem_kv'], 'm_w_branch_a': out['m_w_branch_a'], 'm_w_branch_b': out['m_w_branch_b'], 'm_w_branch_c': out['m_w_branch_c'], 'm_w_out': out['m_w_out'], 'm_post_norm': out['m_post_norm'], 'v_pre_norm': out['v_pre_norm'], 'v_w_in': out['v_w_in'], 'v_merge_bias': out['v_merge_bias'], 'v_na_rpb': out['v_na_rpb'], 'v_mem_norm': out['v_mem_norm'], 'v_w_mem_kv': out['v_w_mem_kv'], 'v_w_branch_a': out['v_w_branch_a'], 'v_w_branch_b': out['v_w_branch_b'], 'v_w_branch_c': out['v_w_branch_c'], 'v_w_out': out['v_w_out'], 'v_post_norm': out['v_post_norm']}


def _loss(weights, diff, rest, loss_target):
    with _jax.named_scope("forward"):
        args = {**rest, TWIN_DIFF_INPUT: diff, **{k: w.astype(_WEIGHT_DTYPES[k]) for k, w in weights.items()}}
        y = _forward(args)
    with _jax.named_scope("loss_head"):
        err = _jnp.square(y.astype(_jnp.float32) - loss_target)
        return 0.5 * _jnp.sum(_jnp.mean(err, axis=-1)) if err.ndim else 0.5 * err


def _adamw(w, g, m, v):
    m = ADAM_B1 * m + (1.0 - ADAM_B1) * g
    v = ADAM_B2 * v + (1.0 - ADAM_B2) * _jnp.square(g)
    m_hat = m / (1.0 - ADAM_B1 ** ADAM_STEP)
    v_hat = v / (1.0 - ADAM_B2 ** ADAM_STEP)
    delta = -ADAM_LR * (m_hat / (_jnp.sqrt(v_hat) + ADAM_EPS) + ADAM_WD * w)
    return delta, m, v


def reference(x, mem, pre_norm, w_in, merge_bias, na_rpb, mem_norm, w_mem_kv, w_branch_a, w_branch_b, w_branch_c, w_out, post_norm, loss_target, m_pre_norm, m_w_in, m_merge_bias, m_na_rpb, m_mem_norm, m_w_mem_kv, m_w_branch_a, m_w_branch_b, m_w_branch_c, m_w_out, m_post_norm, v_pre_norm, v_w_in, v_merge_bias, v_na_rpb, v_mem_norm, v_w_mem_kv, v_w_branch_a, v_w_branch_b, v_w_branch_c, v_w_out, v_post_norm):
    given = dict(x=x, mem=mem, pre_norm=pre_norm, w_in=w_in, merge_bias=merge_bias, na_rpb=na_rpb, mem_norm=mem_norm, w_mem_kv=w_mem_kv, w_branch_a=w_branch_a, w_branch_b=w_branch_b, w_branch_c=w_branch_c, w_out=w_out, post_norm=post_norm, loss_target=loss_target, m_pre_norm=m_pre_norm, m_w_in=m_w_in, m_merge_bias=m_merge_bias, m_na_rpb=m_na_rpb, m_mem_norm=m_mem_norm, m_w_mem_kv=m_w_mem_kv, m_w_branch_a=m_w_branch_a, m_w_branch_b=m_w_branch_b, m_w_branch_c=m_w_branch_c, m_w_out=m_w_out, m_post_norm=m_post_norm, v_pre_norm=v_pre_norm, v_w_in=v_w_in, v_merge_bias=v_merge_bias, v_na_rpb=v_na_rpb, v_mem_norm=v_mem_norm, v_w_mem_kv=v_w_mem_kv, v_w_branch_a=v_w_branch_a, v_w_branch_b=v_w_branch_b, v_w_branch_c=v_w_branch_c, v_w_out=v_w_out, v_post_norm=v_post_norm)
    weights = {n: given[n] for n in TWIN_WEIGHTS}
    shared = {n: given[n] for n in SHARED_INPUTS}
    per_example = {n: given[n] for n in ['x', 'mem']}
    grad_fn = _jax.value_and_grad(_loss, argnums=(0, 1))

    def one_microbatch(ex, loss_target):
        ex = dict(ex)
        diff = ex.pop(TWIN_DIFF_INPUT)
        return grad_fn(weights, diff, {**shared, **ex}, loss_target)

    if N_MICROBATCH == 1:
        loss, (grad_w, grad_x) = one_microbatch(per_example, given["loss_target"])
    else:
        def body(carry, xs):
            loss_sum, grad_sum = carry
            l_k, (gw_k, gx_k) = one_microbatch(xs[0], xs[1])
            with _jax.named_scope("update"):
                return (loss_sum + l_k, _jax.tree.map(_jnp.add, grad_sum, gw_k)), gx_k

        init = (_jnp.zeros((), _jnp.float32), _jax.tree.map(_jnp.zeros_like, weights))
        (loss, grad_w), grad_x = _jax.lax.scan(body, init, (per_example, given["loss_target"]))
    with _jax.named_scope("update"):
        delta_w, new_m, new_v = {}, {}, {}
        for n in TWIN_WEIGHTS:
            delta_w[n], new_m[n], new_v[n] = _adamw(weights[n], grad_w[n], given["m_" + n], given["v_" + n])
    return (loss, grad_x, *[grad_w[n] for n in TWIN_WEIGHTS], *[delta_w[n] for n in TWIN_WEIGHTS],
            *[new_m[n] for n in TWIN_WEIGHTS], *[new_v[n] for n in TWIN_WEIGHTS])
```

```python
import functools

import jax
import jax.numpy as jnp
from jax import lax
from jax.experimental import pallas as pl
from jax.experimental.pallas import tpu as pltpu

F32 = jnp.float32
BF16 = jnp.bfloat16

SEQ = 2048
D_MODEL = 1024
N_IN = 11264
N_DEV = 8
SHARD_IN = N_IN // N_DEV
HEAD_DIM = 64
GRID_W = 64
NA_ROWS = 8
MEM_LEN = 256
DILATIONS = (1, 4, 16)
REACH = 64
ROPE_THETA = 500000.0
ROPE_DIM = 16
EPS = 1e-6
NEG = -1e30
ADAM_LR = 0.001
ADAM_B1 = 0.9
ADAM_B2 = 0.999
ADAM_EPS = 1e-08
ADAM_WD = 0.01
ADAM_STEP = 10

VMEM_LIMIT_BYTES = 56 * 1024 * 1024
MESH_ID = pl.DeviceIdType.MESH

NN = (((1,), (0,)), ((), ()))
NT = (((1,), (1,)), ((), ()))
TN = (((0,), (0,)), ((), ()))


def _params(sem=None):
    return pltpu.CompilerParams(dimension_semantics=sem, vmem_limit_bytes=VMEM_LIMIT_BYTES)


def _iota(shape, dim):
    return lax.broadcasted_iota(jnp.int32, shape, dim)


def _sigmoid(x):
    return 1.0 / (1.0 + jnp.exp(-x))


def _fold(a, d):
    if d == 1:
        return a
    n, w = a.shape
    return a.reshape(n // d, d, w).transpose(1, 0, 2).reshape(n, w)


def _unfold(a, d):
    if d == 1:
        return a
    n, w = a.shape
    return a.reshape(d, n // d, w).transpose(1, 0, 2).reshape(n, w)


def _rope_tables():
    half = ROPE_DIM // 2
    inv = ROPE_THETA ** (-jnp.arange(half, dtype=F32) * 2.0 / ROPE_DIM)
    pos = jnp.arange(SEQ, dtype=F32)
    ang = pos[:, None] * inv[None, :]
    cos, sin = jnp.cos(ang), jnp.sin(ang)
    zeros = jnp.zeros_like(cos)
    rest = HEAD_DIM - ROPE_DIM
    c64 = jnp.concatenate([cos, cos, jnp.ones((SEQ, rest), F32)], axis=1)
    s1 = jnp.concatenate([zeros, sin, jnp.zeros((SEQ, rest), F32)], axis=1)
    s2 = jnp.concatenate([-sin, zeros, jnp.zeros((SEQ, rest), F32)], axis=1)
    tabs = []
    for d in DILATIONS:
        tabs.append(jnp.stack([jnp.tile(_fold(t, d), (1, 2)) for t in (c64, s1, s2)], axis=0))
    return jnp.stack(tabs, axis=0)


def _rope(a, c, s1, s2):
    return a * c + pltpu.roll(a, 8, 1) * s1 + pltpu.roll(a, 120, 1) * s2


def _rope_t(a, c, s1, s2):
    return a * c + pltpu.roll(a * s1, 120, 1) + pltpu.roll(a * s2, 8, 1)


def _perm_of_block(j):
    return jnp.where(j < 3, 0, jnp.where(j < 6, 1, jnp.where(j < 9, 2, 0)))


def _mm(name, a, b, out_shape, out_dtype, grid, a_spec, b_spec, o_spec, acc_shape, dims, k_axis, nk):
    def body(a_ref, b_ref, o_ref, acc_ref):
        k = pl.program_id(k_axis)

        @pl.when(k == 0)
        def _():
            acc_ref[...] = jnp.zeros(acc_shape, F32)

        acc_ref[...] += lax.dot_general(a_ref[...], b_ref[...], dims, preferred_element_type=F32)

        @pl.when(k == nk - 1)
        def _():
            o_ref[...] = acc_ref[...].astype(out_dtype)

    sem = tuple("arbitrary" if ax == k_axis else "parallel" for ax in range(len(grid)))
    return pl.pallas_call(
        body, name=name, grid=grid, in_specs=[a_spec, b_spec], out_specs=o_spec,
        out_shape=jax.ShapeDtypeStruct(out_shape, out_dtype),
        scratch_shapes=[pltpu.VMEM(acc_shape, F32)], compiler_params=_params(sem))(a, b)


def _mm_simple(name, a, b, dims, out_dtype, tm, tn, tk):
    if dims is NN:
        m, kk = a.shape
        n = b.shape[1]
        a_spec = pl.BlockSpec((tm, tk), lambda i, j, k: (i, k))
        b_spec = pl.BlockSpec((tk, tn), lambda i, j, k: (k, j))
    elif dims is NT:
        m, kk = a.shape
        n = b.shape[0]
        a_spec = pl.BlockSpec((tm, tk), lambda i, j, k: (i, k))
        b_spec = pl.BlockSpec((tn, tk), lambda i, j, k: (j, k))
    else:
        kk, m = a.shape
        n = b.shape[1]
        a_spec = pl.BlockSpec((tk, tm), lambda i, j, k: (k, i))
        b_spec = pl.BlockSpec((tk, tn), lambda i, j, k: (k, j))
    grid = (m // tm, n // tn, kk // tk)
    o_spec = pl.BlockSpec((tm, tn), lambda i, j, k: (i, j))
    return _mm(name, a, b, (m, n), out_dtype, grid, a_spec, b_spec, o_spec, (tm, tn), dims, 2, kk // tk)


def _rmsnorm_fwd(name, x, gain, rows):
    n, d = x.shape

    def body(x_ref, g_ref, o_ref):
        xv = x_ref[...]
        rstd = lax.rsqrt(jnp.mean(xv * xv, axis=1, keepdims=True) + EPS)
        o_ref[...] = (xv * rstd * g_ref[...]).astype(BF16)

    return pl.pallas_call(
        body, name=name, grid=(n // rows,),
        in_specs=[pl.BlockSpec((rows, d), lambda i: (i, 0)), pl.BlockSpec((1, d), lambda i: (0, 0))],
        out_specs=pl.BlockSpec((rows, d), lambda i: (i, 0)),
        out_shape=jax.ShapeDtypeStruct((n, d), BF16), compiler_params=_params(("parallel",)))(x, gain)


def _prenorm_bwd(x, gain, dh0, dh1, dh2, dout):
    rows = 256

    def body(x_ref, g_ref, a_ref, b_ref, c_ref, do_ref, dx_ref, gg_ref):
        xv = x_ref[...]
        rstd = lax.rsqrt(jnp.mean(xv * xv, axis=1, keepdims=True) + EPS)
        xn = xv * rstd
        dh = a_ref[...] + b_ref[...] + c_ref[...]
        gdh = dh * g_ref[...]
        dx_ref[...] = rstd * (gdh - xn * jnp.mean(gdh * xn, axis=1, keepdims=True)) + do_ref[...]

        @pl.when(pl.program_id(0) == 0)
        def _():
            gg_ref[...] = jnp.zeros((1, D_MODEL), F32)

        gg_ref[...] += jnp.sum(dh * xn, axis=0, keepdims=True)

    row = pl.BlockSpec((rows, D_MODEL), lambda i: (i, 0))
    vec = pl.BlockSpec((1, D_MODEL), lambda i: (0, 0))
    return pl.pallas_call(
        body, name="prenorm_bwd", grid=(SEQ // rows,),
        in_specs=[row, vec, row, row, row, row], out_specs=[row, vec],
        out_shape=[jax.ShapeDtypeStruct((SEQ, D_MODEL), F32), jax.ShapeDtypeStruct((1, D_MODEL), F32)],
        compiler_params=_params(("arbitrary",)))(x, gain, dh0, dh1, dh2, dout)


def _memnorm_bwd(mem, dmemn):
    def body(m_ref, d_ref, gg_ref):
        mv = m_ref[...]
        rstd = lax.rsqrt(jnp.mean(mv * mv, axis=1, keepdims=True) + EPS)
        gg_ref[...] = jnp.sum(d_ref[...] * mv * rstd, axis=0, keepdims=True)

    return pl.pallas_call(
        body, name="memnorm_bwd", out_shape=jax.ShapeDtypeStruct((1, D_MODEL), F32),
        compiler_params=_params())(mem, dmemn)


def _in_proj(hs, wt, tabs):
    tm, tn = 512, 512

    def body(h_ref, w_ref, t_ref, o_ref):
        j = pl.program_id(0)
        acc = lax.dot_general(h_ref[...], w_ref[...], NT, preferred_element_type=F32)
        is_rope = jnp.logical_and(j < 9, j % 3 != 2)

        @pl.when(is_rope)
        def _():
            c, s1, s2 = t_ref[0], t_ref[1], t_ref[2]
            for q in range(tn // 128):
                a = acc[:, q * 128:(q + 1) * 128]
                o_ref[:, q * 128:(q + 1) * 128] = _rope(a, c, s1, s2).astype(BF16)

        @pl.when(jnp.logical_not(is_rope))
        def _():
            o_ref[...] = acc.astype(BF16)

    return pl.pallas_call(
        body, name="in_proj", grid=(N_IN // tn, SEQ // tm),
        in_specs=[pl.BlockSpec((None, tm, D_MODEL), lambda j, i: (_perm_of_block(j), i, 0)),
                  pl.BlockSpec((tn, D_MODEL), lambda j, i: (j, 0)),
                  pl.BlockSpec((None, 3, tm, 128), lambda j, i: (_perm_of_block(j), 0, i, 0))],
        out_specs=pl.BlockSpec((tm, tn), lambda j, i: (i, j)),
        out_shape=jax.ShapeDtypeStruct((SEQ, N_IN), BF16),
        compiler_params=_params(("parallel", "parallel")))(hs, wt, tabs)


def _in_proj_dw(dparts, hs):
    tn, tk = 512, 512
    return _mm("in_proj_dw", dparts, hs, (N_IN, D_MODEL), BF16, (N_IN // tn, SEQ // tk),
               pl.BlockSpec((tk, tn), lambda j, k: (k, j)),
               pl.BlockSpec((None, tk, D_MODEL), lambda j, k: (_perm_of_block(j), k, 0)),
               pl.BlockSpec((tn, D_MODEL), lambda j, k: (j, 0)), (tn, D_MODEL), TN, 1, SEQ // tk)


def _in_proj_dh(dparts, wt):
    tm, tk = 512, 512
    nat_blocks = N_IN // tk - 6

    def nat_col(k):
        return jnp.where(k < 3, k, k + 6)

    nat = _mm("in_proj_dh_nat", dparts, wt, (SEQ, D_MODEL), F32, (SEQ // tm, nat_blocks),
              pl.BlockSpec((tm, tk), lambda i, k: (i, nat_col(k))),
              pl.BlockSpec((tk, D_MODEL), lambda i, k: (nat_col(k), 0)),
              pl.BlockSpec((tm, D_MODEL), lambda i, k: (i, 0)), (tm, D_MODEL), NN, 1, nat_blocks)
    fold = _mm("in_proj_dh_fold", dparts, wt, (2, SEQ, D_MODEL), F32, (2, SEQ // tm, 3),
               pl.BlockSpec((tm, tk), lambda g, i, k: (i, 3 + 3 * g + k)),
               pl.BlockSpec((tk, D_MODEL), lambda g, i, k: (3 + 3 * g + k, 0)),
               pl.BlockSpec((None, tm, D_MODEL), lambda g, i, k: (g, i, 0)), (tm, D_MODEL), NN, 2, 3)
    return nat, fold


def _head_lanes(lanes, hh):
    return lanes >= 64 if hh == 1 else lanes < 64


def _head_rows(x, lanes, hh, pair):
    if not pair:
        return jnp.max(x, axis=1, keepdims=True)
    return jnp.max(jnp.where(_head_lanes(lanes, hh), x, -jnp.inf), axis=1, keepdims=True)


def _mask_head(x, lanes, hh, pair):
    if not pair:
        return x
    return jnp.where(_head_lanes(lanes, hh), x.astype(F32), 0.0).astype(BF16)


def _merge_heads(parts, lanes, pair):
    if not pair:
        return parts[0]
    return jnp.where(lanes < 64, parts[0], parts[1])


def _window(mode, qi, tq, mlen, tk):
    if mode == "dil":
        q0 = qi * tq
        seg = (q0 // mlen) * mlen
        ks = jnp.clip(q0 - REACH, seg, seg + mlen - tk)
        return pl.multiple_of(ks, 64)
    if mode == "na":
        r_start = jnp.clip(qi - NA_ROWS // 2, 0, SEQ // GRID_W - NA_ROWS)
        return pl.multiple_of(r_start * GRID_W, 64)
    return 0


def _scores(mode, qh, k, scale, qi, tq, tk, ks, bias_ref, hh):
    s = lax.dot_general(qh, k, NT, preferred_element_type=F32) * scale
    if mode == "dil":
        qpos = qi * tq + _iota((tq, tk), 0)
        kpos = ks + _iota((tq, tk), 1)
        s = jnp.where(jnp.abs(qpos - kpos) <= REACH, s, NEG)
    elif mode == "na":
        off = qi - jnp.clip(qi - NA_ROWS // 2, 0, SEQ // GRID_W - NA_ROWS)
        s = s + bias_ref[hh, off]
    return s


def _attn_cfg(mode, d):
    if mode == "dil":
        mlen = SEQ // d
        return dict(pair=True, tq=128, tk=min(256, mlen), mlen=mlen, lk=SEQ, scale=HEAD_DIM ** -0.5, units=4)
    if mode == "na":
        return dict(pair=True, tq=GRID_W, tk=NA_ROWS * GRID_W, mlen=SEQ, lk=SEQ, scale=HEAD_DIM ** -0.5, units=4)
    return dict(pair=False, tq=256, tk=MEM_LEN, mlen=SEQ, lk=MEM_LEN, scale=128 ** -0.5, units=4)


def _attn_fwd(name, mode, q_arr, k_arr, v_arr, qcol, kcol, vcol, d=1, bias=None):
    cfg = _attn_cfg(mode, d)
    pair, tq, tk, mlen, lk, scale = cfg["pair"], cfg["tq"], cfg["tk"], cfg["mlen"], cfg["lk"], cfg["scale"]
    nh = 2 if pair else 1

    def body(*refs):
        if mode == "na":
            q_ref, k_ref, v_ref, bias_ref, o_ref, l_ref = refs
        else:
            q_ref, k_ref, v_ref, o_ref, l_ref = refs
            bias_ref = None
        qi = pl.program_id(1)
        ks = _window(mode, qi, tq, mlen, tk)
        q = q_ref[...]
        k = k_ref[pl.ds(ks, tk), :]
        v = v_ref[pl.ds(ks, tk), :]
        lanes = _iota((tq, 128), 1)
        outs, lses = [], []
        for hh in range(nh):
            s = _scores(mode, _mask_head(q, lanes, hh, pair), k, scale, qi, tq, tk, ks, bias_ref, hh)
            m = jnp.max(s, axis=1, keepdims=True)
            p = jnp.exp(s - m)
            l = jnp.sum(p, axis=1, keepdims=True)
            o = jnp.dot(p.astype(BF16), v, preferred_element_type=F32)
            outs.append(o / l)
            lses.append(jnp.broadcast_to(m + jnp.log(l), (tq, 128)))
        o_ref[...] = _merge_heads(outs, lanes, pair)
        l_ref[...] = _merge_heads(lses, lanes, pair)

    in_specs = [pl.BlockSpec((tq, 128), lambda u, i: (i, qcol + u)),
                pl.BlockSpec((lk, 128), lambda u, i: (0, kcol + u)),
                pl.BlockSpec((lk, 128), lambda u, i: (0, vcol + u))]
    args = [q_arr, k_arr, v_arr]
    if mode == "na":
        in_specs.append(pl.BlockSpec((2, NA_ROWS, GRID_W, NA_ROWS * GRID_W), lambda u, i: (u, 0, 0, 0)))
        args.append(bias)
    out_spec = pl.BlockSpec((tq, 128), lambda u, i: (i, u))
    return pl.pallas_call(
        body, name=name, grid=(cfg["units"], SEQ // tq), in_specs=in_specs, out_specs=[out_spec, out_spec],
        out_shape=[jax.ShapeDtypeStruct((SEQ, 512), F32), jax.ShapeDtypeStruct((SEQ, 512), F32)],
        compiler_params=_params(("parallel", "parallel")))(*args)


def _attn_bwd(name, mode, q_arr, k_arr, v_arr, qcol, kcol, vcol, do, lse, dp=None, o=None, d=1, bias=None,
              tabs=None):
    cfg = _attn_cfg(mode, d)
    pair, tq, tk, mlen, lk, scale = cfg["pair"], cfg["tq"], cfg["tk"], cfg["mlen"], cfg["lk"], cfg["scale"]
    nh = 2 if pair else 1
    nq = SEQ // tq
    kv_dtype = F32 if mode == "mem" else BF16

    def body(*refs):
        refs = list(refs)
        q_ref, k_ref, v_ref, do_ref, l_ref = refs[:5]
        rest = refs[5:]
        bias_ref = tq_ref = tk_ref = db_ref = None
        if mode == "dil":
            dp_ref, tq_ref, tk_ref, dq_ref, dk_ref, dv_ref, dk_acc, dv_acc = rest
        elif mode == "na":
            o_ref, bias_ref, dq_ref, dk_ref, dv_ref, db_ref, dk_acc, dv_acc = rest
        else:
            o_ref, dq_ref, dk_ref, dv_ref, dk_acc, dv_acc = rest
        qi = pl.program_id(1)
        ks = _window(mode, qi, tq, mlen, tk)

        @pl.when(qi == 0)
        def _():
            dk_acc[...] = jnp.zeros((lk, 128), F32)
            dv_acc[...] = jnp.zeros((lk, 128), F32)
            if mode == "na":
                db_ref[...] = jnp.zeros(db_ref.shape, F32)

        q = q_ref[...]
        k = k_ref[pl.ds(ks, tk), :]
        v = v_ref[pl.ds(ks, tk), :]
        dov = do_ref[...]
        lsev = l_ref[...]
        lanes = _iota((tq, 128), 1)
        lanes_k = _iota((tk, 128), 1)
        if mode == "dil":
            dpv = dp_ref[...]
        else:
            dpv = dov.astype(F32) * o_ref[...]
        dqs, dks, dvs = [], [], []
        for hh in range(nh):
            qh = _mask_head(q, lanes, hh, pair)
            doh = _mask_head(dov, lanes, hh, pair)
            s = _scores(mode, qh, k, scale, qi, tq, tk, ks, bias_ref, hh)
            p = jnp.exp(s - _head_rows(lsev, lanes, hh, pair))
            if mode == "dil":
                dph = _head_rows(dpv, lanes, hh, pair)
            elif pair:
                dph = jnp.sum(jnp.where(_head_lanes(lanes, hh), dpv, 0.0), axis=1, keepdims=True)
            else:
                dph = jnp.sum(dpv, axis=1, keepdims=True)
            dpm = lax.dot_general(doh, v, NT, preferred_element_type=F32)
            ds = p * (dpm - dph)
            if mode == "na":
                off = qi - jnp.clip(qi - NA_ROWS // 2, 0, SEQ // GRID_W - NA_ROWS)
                db_ref[hh, off] += ds
            dsb = ds.astype(BF16)
            dvs.append(lax.dot_general(p.astype(BF16), dov, TN, preferred_element_type=F32))
            dqs.append(jnp.dot(dsb, k, preferred_element_type=F32) * scale)
            dks.append(lax.dot_general(dsb, q, TN, preferred_element_type=F32) * scale)
        dq = _merge_heads(dqs, lanes, pair)
        if mode == "dil":
            dq = _rope_t(dq, tq_ref[0], tq_ref[1], tq_ref[2])
        dq_ref[...] = dq.astype(BF16)
        dk_acc[pl.ds(ks, tk), :] += _merge_heads(dks, lanes_k, pair)
        dv_acc[pl.ds(ks, tk), :] += _merge_heads(dvs, lanes_k, pair)

        @pl.when(qi == nq - 1)
        def _():
            dkv = dk_acc[...]
            if mode == "dil":
                dkv = _rope_t(dkv, tk_ref[0], tk_ref[1], tk_ref[2])
            dk_ref[...] = dkv.astype(kv_dtype)
            dv_ref[...] = dv_acc[...].astype(kv_dtype)

    q_spec = pl.BlockSpec((tq, 128), lambda u, i: (i, qcol + u))
    row_spec = pl.BlockSpec((tq, 128), lambda u, i: (i, u))
    kv_out = pl.BlockSpec((lk, 128), lambda u, i: (0, u))
    in_specs = [q_spec,
                pl.BlockSpec((lk, 128), lambda u, i: (0, kcol + u)),
                pl.BlockSpec((lk, 128), lambda u, i: (0, vcol + u)),
                row_spec, row_spec]
    args = [q_arr, k_arr, v_arr, do, lse]
    out_specs = [row_spec, kv_out, kv_out]
    out_shape = [jax.ShapeDtypeStruct((SEQ, 512), BF16), jax.ShapeDtypeStruct((lk, 512), kv_dtype),
                 jax.ShapeDtypeStruct((lk, 512), kv_dtype)]
    if mode == "dil":
        in_specs += [row_spec, pl.BlockSpec((3, tq, 128), lambda u, i: (0, i, 0)),
                     pl.BlockSpec((3, SEQ, 128), lambda u, i: (0, 0, 0))]
        args += [dp, tabs, tabs]
    elif mode == "na":
        b_spec = pl.BlockSpec((2, NA_ROWS, GRID_W, NA_ROWS * GRID_W), lambda u, i: (u, 0, 0, 0))
        in_specs += [row_spec, b_spec]
        args += [o, bias]
        out_specs.append(b_spec)
        out_shape.append(jax.ShapeDtypeStruct((8, NA_ROWS, GRID_W, NA_ROWS * GRID_W), F32))
    else:
        in_specs.append(row_spec)
        args.append(o)
    return pl.pallas_call(
        body, name=name, grid=(cfg["units"], nq), in_specs=in_specs, out_specs=out_specs, out_shape=out_shape,
        scratch_shapes=[pltpu.VMEM((lk, 128), F32), pltpu.VMEM((lk, 128), F32)],
        compiler_params=_params(("parallel", "arbitrary")))(*args)


def _na_geometry():
    qc = _iota((GRID_W, 128), 0)
    lane = _iota((GRID_W, 128), 1)
    kc = lane & 63
    c_start = jnp.clip(qc - 8, 0, GRID_W - 16)
    valid = jnp.logical_and(kc >= c_start, kc < c_start + 16)
    return lane, kc - qc + 15, valid


def _na_bias(rpb_flat):
    def body(r_ref, o_ref, t_ref):
        h = pl.program_id(0)
        lane, dci, valid = _na_geometry()
        for dd in range(14):
            base = (h * 15 + dd) * 31

            def step(i, t):
                val = jnp.where(lane < 64, r_ref[base + i], r_ref[base + 31 + i])
                return jnp.where(dci == i, val, t)

            t = lax.fori_loop(0, 31, step, jnp.zeros((GRID_W, 128), F32))
            t_ref[dd] = jnp.where(valid, t, NEG)
        for off in range(NA_ROWS):
            for p in range(4):
                o_ref[off, :, p * 128:(p + 1) * 128] = t_ref[2 * p - off + 7]

    return pl.pallas_call(
        body, name="na_bias", grid=(8,),
        in_specs=[pl.BlockSpec(memory_space=pltpu.SMEM)],
        out_specs=pl.BlockSpec((None, NA_ROWS, GRID_W, NA_ROWS * GRID_W), lambda h: (h, 0, 0, 0)),
        out_shape=jax.ShapeDtypeStruct((8, NA_ROWS, GRID_W, NA_ROWS * GRID_W), F32),
        scratch_shapes=[pltpu.VMEM((14, GRID_W, 128), F32)],
        compiler_params=_params(("parallel",)))(rpb_flat)


def _na_bias_bwd(dbias):
    def body(d_ref, o_ref, m_ref):
        lane, dci, valid = _na_geometry()
        sel_r = _iota((128, 128), 0)
        sel_c = _iota((128, 128), 1)
        sel = jnp.where(jnp.logical_or(jnp.logical_and(sel_c == 0, sel_r < 64),
                                       jnp.logical_and(sel_c == 1, sel_r >= 64)), 1.0, 0.0).astype(F32)
        for dd in range(14):
            t = jnp.zeros((GRID_W, 128), F32)
            for off in range(NA_ROWS):
                for p in range(4):
                    if 2 * p - off + 7 == dd:
                        t = t + d_ref[off, :, p * 128:(p + 1) * 128]
            t = jnp.where(valid, t, 0.0)
            m_ref[...] = jnp.zeros((32, 128), F32)

            def step(i, carry):
                m_ref[pl.ds(i, 1), :] = jnp.sum(jnp.where(dci == i, t, 0.0), axis=0, keepdims=True)
                return carry

            lax.fori_loop(0, 31, step, 0)
            o_ref[dd] = jnp.dot(m_ref[...], sel, precision=lax.Precision.HIGHEST, preferred_element_type=F32)

    return pl.pallas_call(
        body, name="na_bias_bwd", grid=(8,),
        in_specs=[pl.BlockSpec((None, NA_ROWS, GRID_W, NA_ROWS * GRID_W), lambda h: (h, 0, 0, 0))],
        out_specs=pl.BlockSpec((None, 14, 32, 128), lambda h: (h, 0, 0, 0)),
        out_shape=jax.ShapeDtypeStruct((8, 14, 32, 128), F32),
        scratch_shapes=[pltpu.VMEM((32, 128), F32)],
        compiler_params=_params(("parallel",)))(dbias)


GATE_ROWS = 128


def _group_weights(l0, l1, l2):
    m = jnp.maximum(jnp.maximum(l0, l1), l2)
    e0, e1, e2 = jnp.exp(l0 - m), jnp.exp(l1 - m), jnp.exp(l2 - m)
    inv = 1.0 / (e0 + e1 + e2)
    return e0 * inv, e1 * inv, e2 * inv


def _gate_specs():
    r512 = pl.BlockSpec((GATE_ROWS, 512), lambda i: (i, 0))
    r1024 = pl.BlockSpec((GATE_ROWS, D_MODEL), lambda i: (i, 0))
    silu_cols = [pl.BlockSpec((GATE_ROWS, 512), functools.partial(lambda b, i: (i, b), 13 + b)) for b in range(3)]
    logit_cols = [pl.BlockSpec((GATE_ROWS, D_MODEL), functools.partial(lambda b, i: (i, b), 8 + b)) for b in range(3)]
    return r512, r1024, silu_cols, logit_cols


def _gate_fwd(o_grp, l_grp, out_b, out_c, parts, merge_bias, wts):
    r512, r1024, silu_cols, logit_cols = _gate_specs()

    def body(o0, o1, o2, l0, l1, l2, ob, oc, ga, gb, gc, la, lb, lc, mb, wa, wb, wc,
             oa_ref, ua, ub, uc, za, zb, zc, y_ref):
        w0, w1, w2 = _group_weights(l0[...], l1[...], l2[...])
        out_a = w0 * o0[...] + w1 * o1[...] + w2 * o2[...]
        oa_ref[...] = out_a
        y = jnp.zeros((GATE_ROWS, D_MODEL), F32)
        for b, (ov, g_ref, l_ref, w_ref, u_ref, z_ref) in enumerate(
                ((out_a, ga, la, wa, ua, za), (ob[...], gb, lb, wb, ub, zb), (oc[...], gc, lc, wc, uc, zc))):
            g = g_ref[...].astype(F32)
            u = (ov * (g * _sigmoid(g))).astype(BF16)
            u_ref[...] = u
            z = lax.dot_general(u, w_ref[...], NT, preferred_element_type=F32)
            z_ref[...] = z.astype(BF16)
            gate = _sigmoid(l_ref[...].astype(F32) + mb[b:b + 1, :])
            y = y + gate * z
        y_ref[...] = y.astype(BF16)

    full = lambda shape: pl.BlockSpec(shape, lambda i: (0,) * len(shape))
    in_specs = ([r512] * 8 + silu_cols + logit_cols
                + [full((3, D_MODEL))] + [full((D_MODEL, 512))] * 3)
    out_specs = [r512] * 4 + [r1024] * 4
    out_shape = ([jax.ShapeDtypeStruct((SEQ, 512), F32)] + [jax.ShapeDtypeStruct((SEQ, 512), BF16)] * 3
                 + [jax.ShapeDtypeStruct((SEQ, D_MODEL), BF16)] * 4)
    res = pl.pallas_call(
        body, name="gate_fwd", grid=(SEQ // GATE_ROWS,), in_specs=in_specs, out_specs=out_specs,
        out_shape=out_shape, compiler_params=_params(("parallel",)))(
            *o_grp, *l_grp, out_b, out_c, parts, parts, parts, parts, parts, parts, merge_bias, *wts)
    return res[0], res[1:4], res[4:7], res[7]


def _gate_bwd(dy, z, parts, merge_bias, outs, o_grp, l_grp, wts, head_sum):
    r512, r1024, silu_cols, logit_cols = _gate_specs()

    def body(dy_ref, za, zb, zc, la, lb, lc, mb, oa, ob, oc, ga, gb, gc, o0, o1, o2, l0, l1, l2, wa, wb, wc, hs_ref,
             dla, dlb, dlc, gmb, dza, dzb, dzc, dga, dgb, dgc, do0, do1, do2, dp0, dp1, dp2, dob, doc):
        dyv = dy_ref[...].astype(F32)
        rows = []
        dos = []
        for b, (z_ref, l_ref, ov_ref, g_ref, w_ref, dl_ref, dz_ref, dg_ref) in enumerate(
                ((za, la, oa, ga, wa, dla, dza, dga), (zb, lb, ob, gb, wb, dlb, dzb, dgb),
                 (zc, lc, oc, gc, wc, dlc, dzc, dgc))):
            gate = _sigmoid(l_ref[...].astype(F32) + mb[b:b + 1, :])
            dl = dyv * z_ref[...].astype(F32) * gate * (1.0 - gate)
            dl_ref[...] = dl.astype(BF16)
            rows.append(jnp.sum(dl, axis=0, keepdims=True))
            dz = (dyv * gate).astype(BF16)
            dz_ref[...] = dz
            du = jnp.dot(dz, w_ref[...], preferred_element_type=F32)
            g = g_ref[...].astype(F32)
            sg = _sigmoid(g)
            dos.append(du * (g * sg))
            dg_ref[...] = (du * ov_ref[...] * (sg * (1.0 + g * (1.0 - sg)))).astype(BF16)

        @pl.when(pl.program_id(0) == 0)
        def _():
            gmb[...] = jnp.zeros((3, D_MODEL), F32)

        for b in range(3):
            gmb[b:b + 1, :] += rows[b]
        dob[...] = dos[1].astype(BF16)
        doc[...] = dos[2].astype(BF16)
        doa = dos[0]
        row_term = jnp.dot(doa * oa[...], hs_ref[...], precision=lax.Precision.HIGHEST, preferred_element_type=F32)
        ws = _group_weights(l0[...], l1[...], l2[...])
        for wg, do_ref, dp_ref in zip(ws, (do0, do1, do2), (dp0, dp1, dp2)):
            do_ref[...] = (wg * doa).astype(BF16)
            dp_ref[...] = wg * row_term

    full = lambda shape: pl.BlockSpec(shape, lambda i: (0,) * len(shape))
    acc = pl.BlockSpec((3, D_MODEL), lambda i: (0, 0))
    in_specs = ([r1024] * 4 + logit_cols + [full((3, D_MODEL))] + [r512] * 3 + silu_cols + [r512] * 6
                + [full((D_MODEL, 512))] * 3 + [full((512, 512))])
    out_specs = [r1024] * 3 + [acc] + [r1024] * 3 + [r512] * 11
    out_shape = ([jax.ShapeDtypeStruct((SEQ, D_MODEL), BF16)] * 3 + [jax.ShapeDtypeStruct((3, D_MODEL), F32)]
                 + [jax.ShapeDtypeStruct((SEQ, D_MODEL), BF16)] * 3 + [jax.ShapeDtypeStruct((SEQ, 512), BF16)] * 6
                 + [jax.ShapeDtypeStruct((SEQ, 512), F32)] * 3 + [jax.ShapeDtypeStruct((SEQ, 512), BF16)] * 2)
    res = pl.pallas_call(
        body, name="gate_bwd", grid=(SEQ // GATE_ROWS,), in_specs=in_specs, out_specs=out_specs,
        out_shape=out_shape, compiler_params=_params(("arbitrary",)))(
            dy, *z, parts, parts, parts, merge_bias, *outs, parts, parts, parts, *o_grp, *l_grp, *wts, head_sum)
    return res[0:3], res[3], res[4:7], res[7:10], res[10:13], res[13:16], res[16], res[17]


def _post(y2, x, target, gain):
    rows = 256

    def body(y_ref, x_ref, t_ref, g_ref, do_ref, dy_ref, l_ref, gg_ref):
        yv = y_ref[...]
        rstd = lax.rsqrt(jnp.mean(yv * yv, axis=1, keepdims=True) + EPS)
        yn = yv * rstd
        gv = g_ref[...]
        err = x_ref[...] + yn * gv - t_ref[...]
        dout = err * (1.0 / D_MODEL)
        do_ref[...] = dout
        dn = dout * gv
        dy_ref[...] = (rstd * (dn - yn * jnp.mean(dn * yn, axis=1, keepdims=True))).astype(BF16)

        @pl.when(pl.program_id(0) == 0)
        def _():
            l_ref[...] = jnp.zeros((1, D_MODEL), F32)
            gg_ref[...] = jnp.zeros((1, D_MODEL), F32)

        l_ref[...] += jnp.sum(err * err, axis=0, keepdims=True)
        gg_ref[...] += jnp.sum(dout * yn, axis=0, keepdims=True)

    row = pl.BlockSpec((rows, D_MODEL), lambda i: (i, 0))
    vec = pl.BlockSpec((1, D_MODEL), lambda i: (0, 0))
    return pl.pallas_call(
        body, name="post", grid=(SEQ // rows,), in_specs=[row, row, row, vec], out_specs=[row, row, vec, vec],
        out_shape=[jax.ShapeDtypeStruct((SEQ, D_MODEL), F32), jax.ShapeDtypeStruct((SEQ, D_MODEL), BF16),
                   jax.ShapeDtypeStruct((1, D_MODEL), F32), jax.ShapeDtypeStruct((1, D_MODEL), F32)],
        compiler_params=_params(("arbitrary",)))(y2, x, target, gain)


def _local_step(x, mem, target, pre_norm, mem_norm, post_norm, na_rpb, merge_bias, wt_in, w_kv, wt_a, wt_b, wt_c,
                w_out):
    tabs = _rope_tables()
    h = _rmsnorm_fwd("prenorm", x, pre_norm, 256)
    hs = jnp.stack([_fold(h, d) for d in DILATIONS], axis=0)
    parts = _in_proj(hs, wt_in, tabs)

    o_grp, l_grp = [], []
    for g, d in enumerate(DILATIONS):
        o, l = _attn_fwd("dil_fwd_%d" % g, "dil", parts, parts, parts, 12 * g, 12 * g + 4, 12 * g + 8, d=d)
        o_grp.append(_unfold(o, d))
        l_grp.append(_unfold(l, d))
    bias = _na_bias(na_rpb.reshape(-1))
    out_b, lse_b = _attn_fwd("na_fwd", "na", parts, parts, parts, 36, 40, 44, bias=bias)
    memn = _rmsnorm_fwd("memnorm", mem, mem_norm, MEM_LEN)
    kv_m = _mm_simple("mem_kv", memn, w_kv, NN, BF16, MEM_LEN, 512, D_MODEL)
    out_c, lse_c = _attn_fwd("mem_fwd", "mem", parts, kv_m, kv_m, 48, 0, 4)

    wts = (wt_a, wt_b, wt_c)
    out_a, u, z, y = _gate_fwd(o_grp, l_grp, out_b, out_c, parts, merge_bias, wts)
    y2 = _mm_simple("out_proj", y, w_out, NN, F32, 512, D_MODEL, D_MODEL)
    dout, dy2, err_sq, g_post = _post(y2, x, target, post_norm)
    loss = 0.5 * jnp.sum(err_sq) / D_MODEL

    dy = _mm_simple("out_proj_dx", dy2, w_out, NT, BF16, 512, D_MODEL, D_MODEL)
    g_w_out = _mm_simple("out_proj_dw", y, dy2, TN, BF16, D_MODEL, 512, 512)

    rr = _iota((512, 512), 0) // HEAD_DIM
    cc = _iota((512, 512), 1) // HEAD_DIM
    head_sum = (rr == cc).astype(F32)
    dlog, g_mb, dz, dg, do_grp, dp_grp, do_b, do_c = _gate_bwd(
        dy, z, parts, merge_bias, (out_a, out_b, out_c), o_grp, l_grp, wts, head_sum)
    g_wt = [_mm_simple("branch_dw_%d" % b, dz[b], u[b], TN, BF16, D_MODEL, 512, 512) for b in range(3)]

    dqkv = []
    for g, d in enumerate(DILATIONS):
        dq, dk, dv = _attn_bwd("dil_bwd_%d" % g, "dil", parts, parts, parts, 12 * g, 12 * g + 4, 12 * g + 8,
                               _fold(do_grp[g], d), _fold(l_grp[g], d), dp=_fold(dp_grp[g], d), d=d, tabs=tabs[g])
        dqkv += [dq, dk, dv]
    dq_b, dk_b, dv_b, dbias = _attn_bwd("na_bwd", "na", parts, parts, parts, 36, 40, 44, do_b, lse_b, o=out_b,
                                        bias=bias)
    g_rpb_t = _na_bias_bwd(dbias)
    lo = jnp.pad(g_rpb_t[:, :, :31, 0], ((0, 0), (0, 1), (0, 0)))
    hi = jnp.pad(g_rpb_t[:, :, :31, 1], ((0, 0), (1, 0), (0, 0)))
    g_rpb = lo + hi
    dq_c, dk_m, dv_m = _attn_bwd("mem_bwd", "mem", parts, kv_m, kv_m, 48, 0, 4, do_c, lse_c, o=out_c)

    dkv = jnp.concatenate([dk_m, dv_m], axis=1).astype(BF16)
    g_w_kv = _mm_simple("mem_kv_dw", memn, dkv, TN, BF16, D_MODEL, 512, MEM_LEN)
    dmemn = _mm_simple("mem_kv_dx", dkv, w_kv, NT, F32, MEM_LEN, 512, D_MODEL)
    g_mem_norm = _memnorm_bwd(mem, dmemn)

    dparts = jnp.concatenate(dqkv + [dq_b, dk_b, dv_b, dq_c] + list(dg) + list(dlog), axis=1)
    g_wt_in = _in_proj_dw(dparts, hs)
    dh_nat, dh_fold = _in_proj_dh(dparts, wt_in)
    grad_x, g_pre = _prenorm_bwd(x, pre_norm, dh_nat, _unfold(dh_fold[0], 4), _unfold(dh_fold[1], 16), dout)

    grads = dict(wt_in=g_wt_in, w_kv=g_w_kv, wt_a=g_wt[0], wt_b=g_wt[1], wt_c=g_wt[2], w_out=g_w_out,
                 merge_bias=g_mb, pre_norm=g_pre, mem_norm=g_mem_norm, post_norm=g_post, na_rpb=g_rpb)
    return loss, grad_x, grads


ANY = pl.BlockSpec(memory_space=pl.ANY)


def _place():
    return lax.axis_index("x"), lax.axis_index("y"), lax.axis_index("c")


def _all_gather(shards):
    nt = len(shards)

    def body(*refs):
        srcs, outs = refs[:nt], refs[nt:2 * nt]
        send_sems, recv_sems, local_sems = refs[2 * nt:]
        x, y, c = _place()
        me, sibling = (x, y, c), (x, y, 1 - c)
        chips = [(1 - x, y), (x, 1 - y), (1 - x, 1 - y)]

        def block(t, px, py, pc):
            return outs[t].at[4 * px + 2 * py + pc]

        def copy(k, t, blk, to, src=None):
            return pltpu.make_async_remote_copy(
                src_ref=block(t, *blk) if src is None else src, dst_ref=block(t, *blk),
                send_sem=send_sems.at[k * nt + t], recv_sem=recv_sems.at[k * nt + t],
                device_id=to, device_id_type=MESH_ID)

        mine = [pltpu.make_async_copy(srcs[t], block(t, *me), local_sems.at[t]) for t in range(nt)]
        for cp in mine:
            cp.start()
        first = [copy(0, t, me, sibling, src=srcs[t]) for t in range(nt)]
        for j, chip in enumerate(chips):
            first += [copy(1 + j, t, me, (*chip, c), src=srcs[t]) for t in range(nt)]
        for cp in first:
            cp.start()
        passed = []
        for j, chip in enumerate(chips):
            for t in range(nt):
                copy(1 + j, t, (*chip, c), me).wait_recv()
                fwd = copy(4 + j, t, (*chip, c), sibling)
                fwd.start()
                passed.append(fwd)
        for t in range(nt):
            copy(0, t, sibling, me).wait_recv()
        for j, chip in enumerate(chips):
            for t in range(nt):
                copy(4 + j, t, (*chip, 1 - c), me).wait_recv()
        for cp in first + passed:
            cp.wait_send()
        for cp in mine:
            cp.wait()

    return pl.pallas_call(
        body, name="all_gather", in_specs=[ANY] * nt, out_specs=[ANY] * nt,
        out_shape=[jax.ShapeDtypeStruct((N_DEV,) + s.shape, s.dtype) for s in shards],
        scratch_shapes=[pltpu.SemaphoreType.DMA((7 * nt,)), pltpu.SemaphoreType.DMA((7 * nt,)),
                        pltpu.SemaphoreType.DMA((nt,))])(*shards)


def _exchange_sibling(terms):
    nt = len(terms)

    def body(*refs):
        srcs, outs = refs[:nt], refs[nt:2 * nt]
        send_sems, recv_sems = refs[2 * nt:]
        x, y, c = _place()
        copies = []
        for q in range(4):
            for t in range(nt):
                copies.append(pltpu.make_async_remote_copy(
                    src_ref=srcs[t].at[2 * q + 1 - c], dst_ref=outs[t].at[q],
                    send_sem=send_sems.at[q * nt + t], recv_sem=recv_sems.at[q * nt + t],
                    device_id=(x, y, 1 - c), device_id_type=MESH_ID))
        for cp in copies:
            cp.start()
        for cp in copies:
            cp.wait()

    return pl.pallas_call(
        body, name="exchange_sibling", in_specs=[ANY] * nt, out_specs=[ANY] * nt,
        out_shape=[jax.ShapeDtypeStruct((4,) + s.shape[1:], s.dtype) for s in terms],
        scratch_shapes=[pltpu.SemaphoreType.DMA((4 * nt,)), pltpu.SemaphoreType.DMA((4 * nt,))])(*terms)


def _exchange_chips(sums):
    nt = len(sums)

    def body(*refs):
        srcs, outs = refs[:nt], refs[nt:2 * nt]
        send_sems, recv_sems = refs[2 * nt:]
        x, y, c = _place()
        chips = [(1 - x, y), (x, 1 - y), (1 - x, 1 - y)]
        copies = []
        for s, (tx, ty) in enumerate(chips):
            for t in range(nt):
                copies.append(pltpu.make_async_remote_copy(
                    src_ref=srcs[t].at[2 * tx + ty], dst_ref=outs[t].at[s],
                    send_sem=send_sems.at[s * nt + t], recv_sem=recv_sems.at[s * nt + t],
                    device_id=(tx, ty, c), device_id_type=MESH_ID))
        for cp in copies:
            cp.start()
        for cp in copies:
            cp.wait()

    return pl.pallas_call(
        body, name="exchange_chips", in_specs=[ANY] * nt, out_specs=[ANY] * nt,
        out_shape=[jax.ShapeDtypeStruct((3,) + s.shape[1:], s.dtype) for s in sums],
        scratch_shapes=[pltpu.SemaphoreType.DMA((3 * nt,)), pltpu.SemaphoreType.DMA((3 * nt,))])(*sums)


def _gather_small(block):
    def body(src, out, send_sems, recv_sems, local_sem):
        x, y, c = _place()
        me = 4 * x + 2 * y + c
        mine = pltpu.make_async_copy(src, out.at[me], local_sem)
        mine.start()
        copies = []
        for mask in range(1, 8):
            fx, fy, fc = (mask >> 2) & 1, (mask >> 1) & 1, mask & 1
            to = (jnp.where(fx, 1 - x, x), jnp.where(fy, 1 - y, y), jnp.where(fc, 1 - c, c))
            copies.append(pltpu.make_async_remote_copy(
                src_ref=src, dst_ref=out.at[me], send_sem=send_sems.at[mask - 1], recv_sem=recv_sems.at[mask - 1],
                device_id=to, device_id_type=MESH_ID))
        for cp in copies:
            cp.start()
        for cp in copies:
            cp.wait()
        mine.wait()

    return pl.pallas_call(
        body, name="gather_small", in_specs=[ANY], out_specs=ANY,
        out_shape=jax.ShapeDtypeStruct((N_DEV,) + block.shape, block.dtype),
        scratch_shapes=[pltpu.SemaphoreType.DMA((7,)), pltpu.SemaphoreType.DMA((7,)), pltpu.SemaphoreType.DMA])(block)


def _add_sibling(name, term, recv, rows):
    _, r, w = term.shape
    cidx = lax.axis_index("c").astype(jnp.int32).reshape(1)

    def body(c_ref, a_ref, b_ref, o_ref):
        o_ref[...] = (a_ref[...].astype(F32) + b_ref[...].astype(F32)).astype(o_ref.dtype)

    grid_spec = pltpu.PrefetchScalarGridSpec(
        num_scalar_prefetch=1, grid=(4, r // rows),
        in_specs=[pl.BlockSpec((None, rows, w), lambda q, i, c_ref: (2 * q + c_ref[0], i, 0)),
                  pl.BlockSpec((None, rows, w), lambda q, i, c_ref: (q, i, 0))],
        out_specs=pl.BlockSpec((None, rows, w), lambda q, i, c_ref: (q, i, 0)))
    return pl.pallas_call(
        body, name=name, grid_spec=grid_spec, out_shape=jax.ShapeDtypeStruct((4, r, w), term.dtype),
        compiler_params=_params(("parallel", "parallel")))(cidx, term, recv)


def _add_chips(name, sums, recv, rows):
    _, r, w = sums.shape
    qidx = (2 * lax.axis_index("x") + lax.axis_index("y")).astype(jnp.int32).reshape(1)

    def body(q_ref, a_ref, b_ref, o_ref):
        o_ref[...] = ((a_ref[...].astype(F32) + b_ref[0].astype(F32))
                      + (b_ref[1].astype(F32) + b_ref[2].astype(F32)))

    grid_spec = pltpu.PrefetchScalarGridSpec(
        num_scalar_prefetch=1, grid=(r // rows,),
        in_specs=[pl.BlockSpec((None, rows, w), lambda i, q_ref: (q_ref[0], i, 0)),
                  pl.BlockSpec((3, rows, w), lambda i, q_ref: (0, i, 0))],
        out_specs=pl.BlockSpec((rows, w), lambda i, q_ref: (i, 0)))
    return pl.pallas_call(
        body, name=name, grid_spec=grid_spec, out_shape=jax.ShapeDtypeStruct((r, w), F32),
        compiler_params=_params(("parallel",)))(qidx, sums, recv)


def _reduce_scatter(names, terms):
    def rows_of(a):
        return SHARD_IN // 4 if a.shape[1] == SHARD_IN else a.shape[1]

    recv1 = _exchange_sibling(terms)
    sums = [_add_sibling("add_sibling_" + n, t, r, rows_of(t)) for n, t, r in zip(names, terms, recv1)]
    recv2 = _exchange_chips(sums)
    return [_add_chips("add_chips_" + n, s, r, rows_of(s)) for n, s, r in zip(names, sums, recv2)]


def _adamw(name, w, g, m, v, rows=None):
    r, c = w.shape
    rows = r if rows is None else rows
    c1 = 1.0 - ADAM_B1 ** ADAM_STEP
    c2 = 1.0 - ADAM_B2 ** ADAM_STEP

    def body(w_ref, g_ref, m_ref, v_ref, d_ref, nm_ref, nv_ref):
        gv = g_ref[...]
        nm = ADAM_B1 * m_ref[...] + (1.0 - ADAM_B1) * gv
        nv = ADAM_B2 * v_ref[...] + (1.0 - ADAM_B2) * (gv * gv)
        nm_ref[...] = nm
        nv_ref[...] = nv
        d_ref[...] = -ADAM_LR * ((nm / c1) / (jnp.sqrt(nv / c2) + ADAM_EPS) + ADAM_WD * w_ref[...])

    spec = pl.BlockSpec((rows, c), lambda i: (i, 0))
    return pl.pallas_call(
        body, name=name, grid=(r // rows,), in_specs=[spec] * 4, out_specs=[spec] * 3,
        out_shape=[jax.ShapeDtypeStruct((r, c), F32)] * 3, compiler_params=_params(("parallel",)))(w, g, m, v)


def _sum_devices(gathered):
    def body(g_ref, o_ref):
        acc = g_ref[0]
        for j in range(1, N_DEV):
            acc = acc + g_ref[j]
        o_ref[...] = acc

    return pl.pallas_call(
        body, name="sum_devices", out_shape=jax.ShapeDtypeStruct(gathered.shape[1:], F32),
        compiler_params=_params())(gathered)


def _rows128(a, rows):
    flat = a.reshape(-1)
    return jnp.pad(flat, (0, rows * 128 - flat.shape[0])).reshape(rows, 128)


def kernel(x, mem, pre_norm, w_in, merge_bias, na_rpb, mem_norm, w_mem_kv, w_branch_a, w_branch_b, w_branch_c, w_out, post_norm, loss_target, m_pre_norm, m_w_in, m_merge_bias, m_na_rpb, m_mem_norm, m_w_mem_kv, m_w_branch_a, m_w_branch_b, m_w_branch_c, m_w_out, m_post_norm, v_pre_norm, v_w_in, v_merge_bias, v_na_rpb, v_mem_norm, v_w_mem_kv, v_w_branch_a, v_w_branch_b, v_w_branch_c, v_w_out, v_post_norm):
    wt_in_s = w_in[0].T.astype(BF16)
    rows_s = jnp.concatenate([w_mem_kv[0], w_out[0]], axis=0).astype(BF16)
    cols_s = jnp.concatenate([w_branch_a[0].T, w_branch_b[0].T, w_branch_c[0].T], axis=0).astype(BF16)
    mb_s = jnp.pad(merge_bias[0], ((0, 5), (0, 0)))
    g_in, g_rows, g_cols, g_mb = _all_gather([wt_in_s, rows_s, cols_s, mb_s])
    wt_in = g_in.reshape(N_IN, D_MODEL)
    w_kv = g_rows[:, :128].reshape(D_MODEL, D_MODEL)
    w_o = g_rows[:, 128:].reshape(D_MODEL, D_MODEL)
    wt_a = g_cols[:, 0:128].reshape(D_MODEL, 512)
    wt_b = g_cols[:, 128:256].reshape(D_MODEL, 512)
    wt_c = g_cols[:, 256:384].reshape(D_MODEL, 512)
    mb_full = g_mb[:, :3].transpose(1, 0, 2).reshape(3, D_MODEL)

    loss_term, grad_x, grads = _local_step(
        x[0], mem[0], loss_target[0], pre_norm, mem_norm, post_norm, na_rpb[0], mb_full,
        wt_in, w_kv, wt_a, wt_b, wt_c, w_o)
    loss = lax.psum(loss_term, ("x", "y", "c"))

    gmb_t = jnp.pad(grads["merge_bias"].reshape(3, N_DEV, 128).transpose(1, 0, 2), ((0, 0), (0, 5), (0, 0)))
    names = ["w_in", "w_kv", "w_out", "a", "b", "c", "mb"]
    terms = [grads["wt_in"].reshape(N_DEV, SHARD_IN, D_MODEL), grads["w_kv"].reshape(N_DEV, 128, D_MODEL),
             grads["w_out"].reshape(N_DEV, 128, D_MODEL), grads["wt_a"].reshape(N_DEV, 128, 512),
             grads["wt_b"].reshape(N_DEV, 128, 512), grads["wt_c"].reshape(N_DEV, 128, 512), gmb_t]
    gt_in, g_kv, g_out, gt_a, gt_b, gt_c, g_mb8 = _reduce_scatter(names, terms)

    small = jnp.concatenate([_rows128(grads["pre_norm"], 8), _rows128(grads["mem_norm"], 8),
                             _rows128(grads["post_norm"], 8), _rows128(grads["na_rpb"], 32)], axis=0)
    total = _sum_devices(_gather_small(small))
    g_pre = total[0:8].reshape(1, D_MODEL)
    g_memn = total[8:16].reshape(1, D_MODEL)
    g_post = total[16:24].reshape(1, D_MODEL)
    g_rpb = total[24:56].reshape(-1)[:8 * 15 * 31].reshape(1, 8, 15, 31)

    grad = {
        "pre_norm": g_pre, "w_in": gt_in.T[None], "merge_bias": g_mb8[:3][None], "na_rpb": g_rpb,
        "mem_norm": g_memn, "w_mem_kv": g_kv[None], "w_branch_a": gt_a.T[None], "w_branch_b": gt_b.T[None],
        "w_branch_c": gt_c.T[None], "w_out": g_out[None], "post_norm": g_post}
    weights = {
        "pre_norm": (pre_norm, m_pre_norm, v_pre_norm), "w_in": (w_in, m_w_in, v_w_in),
        "merge_bias": (merge_bias, m_merge_bias, v_merge_bias), "na_rpb": (na_rpb, m_na_rpb, v_na_rpb),
        "mem_norm": (mem_norm, m_mem_norm, v_mem_norm), "w_mem_kv": (w_mem_kv, m_w_mem_kv, v_w_mem_kv),
        "w_branch_a": (w_branch_a, m_w_branch_a, v_w_branch_a), "w_branch_b": (w_branch_b, m_w_branch_b, v_w_branch_b),
        "w_branch_c": (w_branch_c, m_w_branch_c, v_w_branch_c), "w_out": (w_out, m_w_out, v_w_out),
        "post_norm": (post_norm, m_post_norm, v_post_norm)}
    order = ["pre_norm", "w_in", "merge_bias", "na_rpb", "mem_norm", "w_mem_kv", "w_branch_a", "w_branch_b",
             "w_branch_c", "w_out", "post_norm"]
    delta, new_m, new_v = {}, {}, {}
    for n in order:
        w, m, v = weights[n]
        shape = w.shape
        two_d = (-1, shape[-1])
        rows = 256 if n == "w_in" else None
        dl, nm, nv = _adamw("adamw_" + n, w.reshape(two_d), grad[n].reshape(two_d), m.reshape(two_d),
                            v.reshape(two_d), rows)
        delta[n], new_m[n], new_v[n] = dl.reshape(shape), nm.reshape(shape), nv.reshape(shape)

    return (loss, grad_x[None], *[grad[n] for n in order], *[delta[n] for n in order],
            *[new_m[n] for n in order], *[new_v[n] for n in order])
```

```python
import functools

import jax
import jax.numpy as jnp
from jax import lax
from jax.experimental import pallas as pl
from jax.experimental.pallas import tpu as pltpu

F32 = jnp.float32
BF16 = jnp.bfloat16

SEQ = 2048
D_MODEL = 1024
N_IN = 11264
N_DEV = 8
SHARD_IN = N_IN // N_DEV
HEAD_DIM = 64
GRID_W = 64
NA_ROWS = 8
MEM_LEN = 256
DILATIONS = (1, 4, 16)
REACH = 64
ROPE_THETA = 500000.0
ROPE_DIM = 16
EPS = 1e-6
NEG = -1e30
ADAM_LR = 0.001
ADAM_B1 = 0.9
ADAM_B2 = 0.999
ADAM_EPS = 1e-08
ADAM_WD = 0.01
ADAM_STEP = 10

VMEM_LIMIT_BYTES = 56 * 1024 * 1024
MESH_ID = pl.DeviceIdType.MESH

NN = (((1,), (0,)), ((), ()))
NT = (((1,), (1,)), ((), ()))
TN = (((0,), (0,)), ((), ()))


def _params(sem=None):
    return pltpu.CompilerParams(dimension_semantics=sem, vmem_limit_bytes=VMEM_LIMIT_BYTES)


def _iota(shape, dim):
    return lax.broadcasted_iota(jnp.int32, shape, dim)


def _sigmoid(x):
    return 1.0 / (1.0 + jnp.exp(-x))


def _fold(a, d):
    if d == 1:
        return a
    n, w = a.shape
    return a.reshape(n // d, d, w).transpose(1, 0, 2).reshape(n, w)


def _unfold(a, d):
    if d == 1:
        return a
    n, w = a.shape
    return a.reshape(d, n // d, w).transpose(1, 0, 2).reshape(n, w)


def _rope_tables():
    half = ROPE_DIM // 2
    inv = ROPE_THETA ** (-jnp.arange(half, dtype=F32) * 2.0 / ROPE_DIM)
    pos = jnp.arange(SEQ, dtype=F32)
    ang = pos[:, None] * inv[None, :]
    cos, sin = jnp.cos(ang), jnp.sin(ang)
    zeros = jnp.zeros_like(cos)
    rest = HEAD_DIM - ROPE_DIM
    c64 = jnp.concatenate([cos, cos, jnp.ones((SEQ, rest), F32)], axis=1)
    s1 = jnp.concatenate([zeros, sin, jnp.zeros((SEQ, rest), F32)], axis=1)
    s2 = jnp.concatenate([-sin, zeros, jnp.zeros((SEQ, rest), F32)], axis=1)
    tabs = []
    for d in DILATIONS:
        tabs.append(jnp.stack([jnp.tile(_fold(t, d), (1, 2)) for t in (c64, s1, s2)], axis=0))
    return jnp.stack(tabs, axis=0)


def _rope(a, c, s1, s2):
    return a * c + pltpu.roll(a, 8, 1) * s1 + pltpu.roll(a, 120, 1) * s2


def _rope_t(a, c, s1, s2):
    return a * c + pltpu.roll(a * s1, 120, 1) + pltpu.roll(a * s2, 8, 1)


def _perm_of_block(j):
    return jnp.where(j < 3, 0, jnp.where(j < 6, 1, jnp.where(j < 9, 2, 0)))


def _mm(name, a, b, out_shape, out_dtype, grid, a_spec, b_spec, o_spec, acc_shape, dims, k_axis, nk):
    def body(a_ref, b_ref, o_ref, acc_ref):
        k = pl.program_id(k_axis)

        @pl.when(k == 0)
        def _():
            acc_ref[...] = jnp.zeros(acc_shape, F32)

        acc_ref[...] += lax.dot_general(a_ref[...], b_ref[...], dims, preferred_element_type=F32)

        @pl.when(k == nk - 1)
        def _():
            o_ref[...] = acc_ref[...].astype(out_dtype)

    sem = tuple("arbitrary" if ax == k_axis else "parallel" for ax in range(len(grid)))
    return pl.pallas_call(
        body, name=name, grid=grid, in_specs=[a_spec, b_spec], out_specs=o_spec,
        out_shape=jax.ShapeDtypeStruct(out_shape, out_dtype),
        scratch_shapes=[pltpu.VMEM(acc_shape, F32)], compiler_params=_params(sem))(a, b)


def _mm_simple(name, a, b, dims, out_dtype, tm, tn, tk):
    if dims is NN:
        m, kk = a.shape
        n = b.shape[1]
        a_spec = pl.BlockSpec((tm, tk), lambda i, j, k: (i, k))
        b_spec = pl.BlockSpec((tk, tn), lambda i, j, k: (k, j))
    elif dims is NT:
        m, kk = a.shape
        n = b.shape[0]
        a_spec = pl.BlockSpec((tm, tk), lambda i, j, k: (i, k))
        b_spec = pl.BlockSpec((tn, tk), lambda i, j, k: (j, k))
    else:
        kk, m = a.shape
        n = b.shape[1]
        a_spec = pl.BlockSpec((tk, tm), lambda i, j, k: (k, i))
        b_spec = pl.BlockSpec((tk, tn), lambda i, j, k: (k, j))
    grid = (m // tm, n // tn, kk // tk)
    o_spec = pl.BlockSpec((tm, tn), lambda i, j, k: (i, j))
    return _mm(name, a, b, (m, n), out_dtype, grid, a_spec, b_spec, o_spec, (tm, tn), dims, 2, kk // tk)


def _rmsnorm_fwd(name, x, gain, rows):
    n, d = x.shape

    def body(x_ref, g_ref, o_ref):
        xv = x_ref[...]
        rstd = lax.rsqrt(jnp.mean(xv * xv, axis=1, keepdims=True) + EPS)
        o_ref[...] = (xv * rstd * g_ref[...]).astype(BF16)

    return pl.pallas_call(
        body, name=name, grid=(n // rows,),
        in_specs=[pl.BlockSpec((rows, d), lambda i: (i, 0)), pl.BlockSpec((1, d), lambda i: (0, 0))],
        out_specs=pl.BlockSpec((rows, d), lambda i: (i, 0)),
        out_shape=jax.ShapeDtypeStruct((n, d), BF16), compiler_params=_params(("parallel",)))(x, gain)


def _prenorm_bwd(x, gain, dh0, dh1, dh2, dout):
    rows = 256

    def body(x_ref, g_ref, a_ref, b_ref, c_ref, do_ref, dx_ref, gg_ref):
        xv = x_ref[...]
        rstd = lax.rsqrt(jnp.mean(xv * xv, axis=1, keepdims=True) + EPS)
        xn = xv * rstd
        dh = a_ref[...] + b_ref[...] + c_ref[...]
        gdh = dh * g_ref[...]
        dx_ref[...] = rstd * (gdh - xn * jnp.mean(gdh * xn, axis=1, keepdims=True)) + do_ref[...]

        @pl.when(pl.program_id(0) == 0)
        def _():
            gg_ref[...] = jnp.zeros((1, D_MODEL), F32)

        gg_ref[...] += jnp.sum(dh * xn, axis=0, keepdims=True)

    row = pl.BlockSpec((rows, D_MODEL), lambda i: (i, 0))
    vec = pl.BlockSpec((1, D_MODEL), lambda i: (0, 0))
    return pl.pallas_call(
        body, name="prenorm_bwd", grid=(SEQ // rows,),
        in_specs=[row, vec, row, row, row, row], out_specs=[row, vec],
        out_shape=[jax.ShapeDtypeStruct((SEQ, D_MODEL), F32), jax.ShapeDtypeStruct((1, D_MODEL), F32)],
        compiler_params=_params(("arbitrary",)))(x, gain, dh0, dh1, dh2, dout)


def _memnorm_bwd(mem, dmemn):
    def body(m_ref, d_ref, gg_ref):
        mv = m_ref[...]
        rstd = lax.rsqrt(jnp.mean(mv * mv, axis=1, keepdims=True) + EPS)
        gg_ref[...] = jnp.sum(d_ref[...] * mv * rstd, axis=0, keepdims=True)

    return pl.pallas_call(
        body, name="memnorm_bwd", out_shape=jax.ShapeDtypeStruct((1, D_MODEL), F32),
        compiler_params=_params())(mem, dmemn)


def _in_proj(hs, wt, tabs):
    tm, tn = 512, 512

    def body(h_ref, w_ref, t_ref, o_ref):
        j = pl.program_id(0)
        is_rope = jnp.logical_and(j < 9, j % 3 != 2)
        row_slices = [slice(r * tm, (r + 1) * tm) for r in range(SEQ // tm)]

        def product(rs):
            return lax.dot_general(h_ref[rs, :], w_ref[...], NT, preferred_element_type=F32)

        @pl.when(is_rope)
        def _():
            for rs in row_slices:
                acc = product(rs)
                c, s1, s2 = t_ref[0, rs, :], t_ref[1, rs, :], t_ref[2, rs, :]
                for q in range(tn // 128):
                    a = acc[:, q * 128:(q + 1) * 128]
                    o_ref[rs, q * 128:(q + 1) * 128] = _rope(a, c, s1, s2).astype(BF16)

        @pl.when(jnp.logical_not(is_rope))
        def _():
            for rs in row_slices:
                o_ref[rs, :] = product(rs).astype(BF16)

    return pl.pallas_call(
        body, name="in_proj", grid=(N_IN // tn,),
        in_specs=[pl.BlockSpec((None, SEQ, D_MODEL), lambda j: (_perm_of_block(j), 0, 0)),
                  pl.BlockSpec((tn, D_MODEL), lambda j: (j, 0)),
                  pl.BlockSpec((None, 3, SEQ, 128), lambda j: (_perm_of_block(j), 0, 0, 0))],
        out_specs=pl.BlockSpec((SEQ, tn), lambda j: (0, j)),
        out_shape=jax.ShapeDtypeStruct((SEQ, N_IN), BF16),
        compiler_params=_params(("parallel",)))(hs, wt, tabs)


def _in_proj_dw(dparts, hst):
    tn = 512

    def body(h_ref, d_ref, o_ref):
        acc = jnp.dot(h_ref[...], d_ref[...], preferred_element_type=F32)
        o_ref[...] = acc.T.astype(BF16)

    return pl.pallas_call(
        body, name="in_proj_dw", grid=(N_IN // tn,),
        in_specs=[pl.BlockSpec((None, D_MODEL, SEQ), lambda j: (_perm_of_block(j), 0, 0)),
                  pl.BlockSpec((SEQ, tn), lambda j: (0, j))],
        out_specs=pl.BlockSpec((tn, D_MODEL), lambda j: (j, 0)),
        out_shape=jax.ShapeDtypeStruct((N_IN, D_MODEL), BF16), compiler_params=_params(("parallel",)))(hst, dparts)


def _in_proj_dh(dparts, wt):
    tk = 512
    nblk = N_IN // tk

    def col(s):
        return jnp.where(s < 3, s, jnp.where(s < 16, s + 6, s - 13))

    def order(s):
        return jnp.where(s < 16, 0, jnp.where(s < 19, 1, 2))

    def body(d_ref, w_ref, o_ref):
        s = pl.program_id(0)
        first = jnp.logical_or(s == 0, jnp.logical_or(s == 16, s == 19))
        row_slices = [slice(r * 512, (r + 1) * 512) for r in range(SEQ // 512)]

        @pl.when(first)
        def _():
            for rs in row_slices:
                o_ref[rs, :] = jnp.dot(d_ref[rs, :], w_ref[...], preferred_element_type=F32)

        @pl.when(jnp.logical_not(first))
        def _():
            for rs in row_slices:
                o_ref[rs, :] += jnp.dot(d_ref[rs, :], w_ref[...], preferred_element_type=F32)

    return pl.pallas_call(
        body, name="in_proj_dh", grid=(nblk,),
        in_specs=[pl.BlockSpec((SEQ, tk), lambda s: (0, col(s))),
                  pl.BlockSpec((tk, D_MODEL), lambda s: (col(s), 0))],
        out_specs=pl.BlockSpec((None, SEQ, D_MODEL), lambda s: (order(s), 0, 0)),
        out_shape=jax.ShapeDtypeStruct((3, SEQ, D_MODEL), F32), compiler_params=_params(("arbitrary",)))(dparts, wt)


def _head_lanes(lanes, hh):
    return lanes >= 64 if hh == 1 else lanes < 64


def _head_rows(x, lanes, hh, pair):
    if not pair:
        return jnp.max(x, axis=1, keepdims=True)
    return jnp.max(jnp.where(_head_lanes(lanes, hh), x, -jnp.inf), axis=1, keepdims=True)


def _mask_head(x, lanes, hh, pair):
    if not pair:
        return x
    return jnp.where(_head_lanes(lanes, hh), x.astype(F32), 0.0).astype(BF16)


def _merge_heads(parts, lanes, pair):
    if not pair:
        return parts[0]
    return jnp.where(lanes < 64, parts[0], parts[1])


def _window(mode, qi, tq, mlen, tk):
    if mode == "dil":
        q0 = qi * tq
        seg = (q0 // mlen) * mlen
        ks = jnp.clip(q0 - REACH, seg, seg + mlen - tk)
        return pl.multiple_of(ks, 64)
    if mode == "na":
        r_start = jnp.clip(qi - NA_ROWS // 2, 0, SEQ // GRID_W - NA_ROWS)
        return pl.multiple_of(r_start * GRID_W, 64)
    return 0


def _scores(mode, qh, k, scale, qi, tq, tk, ks, bias_ref, hh):
    s = lax.dot_general(qh, k, NT, preferred_element_type=F32) * scale
    if mode == "dil":
        qpos = qi * tq + _iota((tq, tk), 0)
        kpos = ks + _iota((tq, tk), 1)
        s = jnp.where(jnp.abs(qpos - kpos) <= REACH, s, NEG)
    elif mode == "na":
        off = qi - jnp.clip(qi - NA_ROWS // 2, 0, SEQ // GRID_W - NA_ROWS)
        s = s + bias_ref[hh, off]
    return s


def _attn_cfg(mode, d):
    if mode == "dil":
        mlen = SEQ // d
        return dict(pair=True, tq=128, tk=min(256, mlen), mlen=mlen, lk=SEQ, scale=HEAD_DIM ** -0.5, units=4,
                    nsub=ATTN_SUBTILES)
    if mode == "na":
        return dict(pair=True, tq=GRID_W, tk=NA_ROWS * GRID_W, mlen=SEQ, lk=SEQ, scale=HEAD_DIM ** -0.5, units=4,
                    nsub=ATTN_SUBTILES)
    return dict(pair=False, tq=128, tk=MEM_LEN, mlen=SEQ, lk=MEM_LEN, scale=128 ** -0.5, units=4,
                nsub=ATTN_SUBTILES)


ATTN_SUBTILES = 4


def _attn_fwd(name, mode, q_arr, k_arr, v_arr, qcol, kcol, vcol, d=1, bias=None):
    cfg = _attn_cfg(mode, d)
    pair, tq, tk, mlen, lk, scale = cfg["pair"], cfg["tq"], cfg["tk"], cfg["mlen"], cfg["lk"], cfg["scale"]
    nh = 2 if pair else 1
    nsub = cfg["nsub"]
    rows = nsub * tq

    def body(*refs):
        if mode == "na":
            q_ref, k_ref, v_ref, bias_ref, o_ref, l_ref = refs
        else:
            q_ref, k_ref, v_ref, o_ref, l_ref = refs
            bias_ref = None
        lanes = _iota((tq, 128), 1)
        for sub in range(nsub):
            qi = pl.program_id(1) * nsub + sub
            sl = slice(sub * tq, (sub + 1) * tq)
            ks = _window(mode, qi, tq, mlen, tk)
            q = q_ref[sl, :]
            k = k_ref[pl.ds(ks, tk), :]
            v = v_ref[pl.ds(ks, tk), :]
            outs, lses = [], []
            for hh in range(nh):
                s = _scores(mode, _mask_head(q, lanes, hh, pair), k, scale, qi, tq, tk, ks, bias_ref, hh)
                m = jnp.max(s, axis=1, keepdims=True)
                p = jnp.exp(s - m)
                l = jnp.sum(p, axis=1, keepdims=True)
                o = jnp.dot(p.astype(BF16), v, preferred_element_type=F32)
                outs.append(o / l)
                lses.append(jnp.broadcast_to(m + jnp.log(l), (tq, 128)))
            o_ref[sl, :] = _merge_heads(outs, lanes, pair)
            l_ref[sl, :] = _merge_heads(lses, lanes, pair)

    in_specs = [pl.BlockSpec((rows, 128), lambda u, i: (i, qcol + u)),
                pl.BlockSpec((lk, 128), lambda u, i: (0, kcol + u)),
                pl.BlockSpec((lk, 128), lambda u, i: (0, vcol + u))]
    args = [q_arr, k_arr, v_arr]
    if mode == "na":
        in_specs.append(pl.BlockSpec((2, NA_ROWS, GRID_W, NA_ROWS * GRID_W), lambda u, i: (u, 0, 0, 0)))
        args.append(bias)
    out_spec = pl.BlockSpec((rows, 128), lambda u, i: (i, u))
    return pl.pallas_call(
        body, name=name, grid=(cfg["units"], SEQ // rows), in_specs=in_specs, out_specs=[out_spec, out_spec],
        out_shape=[jax.ShapeDtypeStruct((SEQ, 512), F32), jax.ShapeDtypeStruct((SEQ, 512), F32)],
        compiler_params=_params(("parallel", "parallel")))(*args)


def _attn_bwd(name, mode, q_arr, k_arr, v_arr, qcol, kcol, vcol, do, lse, dp=None, o=None, d=1, bias=None,
              tabs=None):
    cfg = _attn_cfg(mode, d)
    pair, tq, tk, mlen, lk, scale = cfg["pair"], cfg["tq"], cfg["tk"], cfg["mlen"], cfg["lk"], cfg["scale"]
    nh = 2 if pair else 1
    nsub = cfg["nsub"]
    rows = nsub * tq
    nq = SEQ // rows
    kv_dtype = F32 if mode == "mem" else BF16

    def body(*refs):
        refs = list(refs)
        q_ref, k_ref, v_ref, do_ref, l_ref = refs[:5]
        rest = refs[5:]
        bias_ref = tq_ref = tk_ref = db_ref = None
        if mode == "dil":
            dp_ref, tq_ref, tk_ref, dq_ref, dk_ref, dv_ref, dk_acc, dv_acc = rest
        elif mode == "na":
            o_ref, bias_ref, dq_ref, dk_ref, dv_ref, db_ref, dk_acc, dv_acc = rest
        else:
            o_ref, dq_ref, dk_ref, dv_ref, dk_acc, dv_acc = rest
        step = pl.program_id(1)

        @pl.when(step == 0)
        def _():
            dk_acc[...] = jnp.zeros((lk, 128), F32)
            dv_acc[...] = jnp.zeros((lk, 128), F32)
            if mode == "na":
                db_ref[...] = jnp.zeros(db_ref.shape, F32)

        lanes = _iota((tq, 128), 1)
        lanes_k = _iota((tk, 128), 1)
        for sub in range(nsub):
            qi = step * nsub + sub
            sl = slice(sub * tq, (sub + 1) * tq)
            ks = _window(mode, qi, tq, mlen, tk)
            q = q_ref[sl, :]
            k = k_ref[pl.ds(ks, tk), :]
            v = v_ref[pl.ds(ks, tk), :]
            dov = do_ref[sl, :]
            lsev = l_ref[sl, :]
            if mode == "dil":
                dpv = dp_ref[sl, :]
            else:
                dpv = dov.astype(F32) * o_ref[sl, :]
            dqs, dks, dvs = [], [], []
            for hh in range(nh):
                qh = _mask_head(q, lanes, hh, pair)
                doh = _mask_head(dov, lanes, hh, pair)
                s = _scores(mode, qh, k, scale, qi, tq, tk, ks, bias_ref, hh)
                p = jnp.exp(s - _head_rows(lsev, lanes, hh, pair))
                if mode == "dil":
                    dph = _head_rows(dpv, lanes, hh, pair)
                elif pair:
                    dph = jnp.sum(jnp.where(_head_lanes(lanes, hh), dpv, 0.0), axis=1, keepdims=True)
                else:
                    dph = jnp.sum(dpv, axis=1, keepdims=True)
                dpm = lax.dot_general(doh, v, NT, preferred_element_type=F32)
                ds = p * (dpm - dph)
                if mode == "na":
                    off = qi - jnp.clip(qi - NA_ROWS // 2, 0, SEQ // GRID_W - NA_ROWS)
                    db_ref[hh, off] += ds
                dsb = ds.astype(BF16)
                dvs.append(lax.dot_general(p.astype(BF16), dov, TN, preferred_element_type=F32))
                dqs.append(jnp.dot(dsb, k, preferred_element_type=F32) * scale)
                dks.append(lax.dot_general(dsb, q, TN, preferred_element_type=F32) * scale)
            dq = _merge_heads(dqs, lanes, pair)
            if mode == "dil":
                dq = _rope_t(dq, tq_ref[0, sl, :], tq_ref[1, sl, :], tq_ref[2, sl, :])
            dq_ref[sl, :] = dq.astype(BF16)
            dk_acc[pl.ds(ks, tk), :] += _merge_heads(dks, lanes_k, pair)
            dv_acc[pl.ds(ks, tk), :] += _merge_heads(dvs, lanes_k, pair)

        @pl.when(step == nq - 1)
        def _():
            dkv = dk_acc[...]
            if mode == "dil":
                dkv = _rope_t(dkv, tk_ref[0], tk_ref[1], tk_ref[2])
            dk_ref[...] = dkv.astype(kv_dtype)
            dv_ref[...] = dv_acc[...].astype(kv_dtype)

    q_spec = pl.BlockSpec((rows, 128), lambda u, i: (i, qcol + u))
    row_spec = pl.BlockSpec((rows, 128), lambda u, i: (i, u))
    kv_out = pl.BlockSpec((lk, 128), lambda u, i: (0, u))
    in_specs = [q_spec,
                pl.BlockSpec((lk, 128), lambda u, i: (0, kcol + u)),
                pl.BlockSpec((lk, 128), lambda u, i: (0, vcol + u)),
                row_spec, row_spec]
    args = [q_arr, k_arr, v_arr, do, lse]
    out_specs = [row_spec, kv_out, kv_out]
    out_shape = [jax.ShapeDtypeStruct((SEQ, 512), BF16), jax.ShapeDtypeStruct((lk, 512), kv_dtype),
                 jax.ShapeDtypeStruct((lk, 512), kv_dtype)]
    if mode == "dil":
        in_specs += [row_spec, pl.BlockSpec((3, rows, 128), lambda u, i: (0, i, 0)),
                     pl.BlockSpec((3, SEQ, 128), lambda u, i: (0, 0, 0))]
        args += [dp, tabs, tabs]
    elif mode == "na":
        b_spec = pl.BlockSpec((2, NA_ROWS, GRID_W, NA_ROWS * GRID_W), lambda u, i: (u, 0, 0, 0))
        in_specs += [row_spec, b_spec]
        args += [o, bias]
        out_specs.append(b_spec)
        out_shape.append(jax.ShapeDtypeStruct((8, NA_ROWS, GRID_W, NA_ROWS * GRID_W), F32))
    else:
        in_specs.append(row_spec)
        args.append(o)
    return pl.pallas_call(
        body, name=name, grid=(cfg["units"], nq), in_specs=in_specs, out_specs=out_specs, out_shape=out_shape,
        scratch_shapes=[pltpu.VMEM((lk, 128), F32), pltpu.VMEM((lk, 128), F32)],
        compiler_params=_params(("parallel", "arbitrary")))(*args)


def _na_geometry():
    qc = _iota((GRID_W, 128), 0)
    lane = _iota((GRID_W, 128), 1)
    kc = lane & 63
    c_start = jnp.clip(qc - 8, 0, GRID_W - 16)
    valid = jnp.logical_and(kc >= c_start, kc < c_start + 16)
    return lane, valid


def _na_bias(rpb_rows):
    def body(r_ref, o_ref, t_ref):
        lane, valid = _na_geometry()
        for dd in range(14):
            row_a = jnp.broadcast_to(r_ref[dd:dd + 1, :], (GRID_W, 128))
            row_b = jnp.broadcast_to(r_ref[dd + 1:dd + 2, :], (GRID_W, 128))
            both = jnp.where(lane < 64, row_a, pltpu.roll(row_b, 64, 1))
            t = pltpu.roll(both, 128 - 15, 1, stride=1, stride_axis=0)
            t_ref[dd] = jnp.where(valid, t, NEG)
        for off in range(NA_ROWS):
            for p in range(4):
                o_ref[off, :, p * 128:(p + 1) * 128] = t_ref[2 * p - off + 7]

    return pl.pallas_call(
        body, name="na_bias", grid=(8,),
        in_specs=[pl.BlockSpec((None, 16, 128), lambda h: (h, 0, 0))],
        out_specs=pl.BlockSpec((None, NA_ROWS, GRID_W, NA_ROWS * GRID_W), lambda h: (h, 0, 0, 0)),
        out_shape=jax.ShapeDtypeStruct((8, NA_ROWS, GRID_W, NA_ROWS * GRID_W), F32),
        scratch_shapes=[pltpu.VMEM((14, GRID_W, 128), F32)],
        compiler_params=_params(("parallel",)))(rpb_rows)


def _na_bias_bwd(dbias):
    def body(d_ref, o_ref):
        lane, valid = _na_geometry()
        reverse = (_iota((GRID_W, GRID_W), 0) + _iota((GRID_W, GRID_W), 1) == GRID_W - 1).astype(F32)
        o_ref[...] = jnp.zeros((16, 128), F32)
        for dd in range(14):
            t = jnp.zeros((GRID_W, 128), F32)
            for off in range(NA_ROWS):
                for p in range(4):
                    if 2 * p - off + 7 == dd:
                        t = t + d_ref[off, :, p * 128:(p + 1) * 128]
            t = jnp.dot(reverse, jnp.where(valid, t, 0.0), precision=lax.Precision.HIGHEST,
                        preferred_element_type=F32)
            t = pltpu.roll(t, 128 - (GRID_W - 16), 1, stride=1, stride_axis=0)
            o_ref[dd:dd + 1, :] = jnp.sum(t, axis=0, keepdims=True)

    return pl.pallas_call(
        body, name="na_bias_bwd", grid=(8,),
        in_specs=[pl.BlockSpec((None, NA_ROWS, GRID_W, NA_ROWS * GRID_W), lambda h: (h, 0, 0, 0))],
        out_specs=pl.BlockSpec((None, 16, 128), lambda h: (h, 0, 0)),
        out_shape=jax.ShapeDtypeStruct((8, 16, 128), F32),
        compiler_params=_params(("parallel",)))(dbias)


GATE_ROWS = 128


def _group_weights(l0, l1, l2):
    m = jnp.maximum(jnp.maximum(l0, l1), l2)
    e0, e1, e2 = jnp.exp(l0 - m), jnp.exp(l1 - m), jnp.exp(l2 - m)
    inv = 1.0 / (e0 + e1 + e2)
    return e0 * inv, e1 * inv, e2 * inv


def _gate_specs():
    r512 = pl.BlockSpec((GATE_ROWS, 512), lambda i: (i, 0))
    r1024 = pl.BlockSpec((GATE_ROWS, D_MODEL), lambda i: (i, 0))
    silu_cols = [pl.BlockSpec((GATE_ROWS, 512), functools.partial(lambda b, i: (i, b), 13 + b)) for b in range(3)]
    logit_cols = [pl.BlockSpec((GATE_ROWS, D_MODEL), functools.partial(lambda b, i: (i, b), 8 + b)) for b in range(3)]
    return r512, r1024, silu_cols, logit_cols


def _gate_fwd(o_grp, l_grp, out_b, out_c, parts, merge_bias, wts):
    r512, r1024, silu_cols, logit_cols = _gate_specs()

    def body(o0, o1, o2, l0, l1, l2, ob, oc, ga, gb, gc, la, lb, lc, mb, wa, wb, wc,
             oa_ref, ua, ub, uc, za, zb, zc, y_ref):
        w0, w1, w2 = _group_weights(l0[...], l1[...], l2[...])
        out_a = w0 * o0[...] + w1 * o1[...] + w2 * o2[...]
        oa_ref[...] = out_a
        y = jnp.zeros((GATE_ROWS, D_MODEL), F32)
        for b, (ov, g_ref, l_ref, w_ref, u_ref, z_ref) in enumerate(
                ((out_a, ga, la, wa, ua, za), (ob[...], gb, lb, wb, ub, zb), (oc[...], gc, lc, wc, uc, zc))):
            g = g_ref[...].astype(F32)
            u = (ov * (g * _sigmoid(g))).astype(BF16)
            u_ref[...] = u
            z = lax.dot_general(u, w_ref[...], NT, preferred_element_type=F32)
            z_ref[...] = z.astype(BF16)
            gate = _sigmoid(l_ref[...].astype(F32) + mb[b:b + 1, :])
            y = y + gate * z
        y_ref[...] = y.astype(BF16)

    full = lambda shape: pl.BlockSpec(shape, lambda i: (0,) * len(shape))
    in_specs = ([r512] * 8 + silu_cols + logit_cols
                + [full((3, D_MODEL))] + [full((D_MODEL, 512))] * 3)
    out_specs = [r512] * 4 + [r1024] * 4
    out_shape = ([jax.ShapeDtypeStruct((SEQ, 512), F32)] + [jax.ShapeDtypeStruct((SEQ, 512), BF16)] * 3
                 + [jax.ShapeDtypeStruct((SEQ, D_MODEL), BF16)] * 4)
    res = pl.pallas_call(
        body, name="gate_fwd", grid=(SEQ // GATE_ROWS,), in_specs=in_specs, out_specs=out_specs,
        out_shape=out_shape, compiler_params=_params(("parallel",)))(
            *o_grp, *l_grp, out_b, out_c, parts, parts, parts, parts, parts, parts, merge_bias, *wts)
    return res[0], res[1:4], res[4:7], res[7]


def _gate_bwd(dy, z, parts, merge_bias, outs, o_grp, l_grp, wts, head_sum):
    r512, r1024, silu_cols, logit_cols = _gate_specs()

    def body(dy_ref, za, zb, zc, la, lb, lc, mb, oa, ob, oc, ga, gb, gc, o0, o1, o2, l0, l1, l2, wa, wb, wc, hs_ref,
             dla, dlb, dlc, gmb, dza, dzb, dzc, dga, dgb, dgc, do0, do1, do2, dp0, dp1, dp2, dob, doc):
        dyv = dy_ref[...].astype(F32)
        rows = []
        dos = []
        for b, (z_ref, l_ref, ov_ref, g_ref, w_ref, dl_ref, dz_ref, dg_ref) in enumerate(
                ((za, la, oa, ga, wa, dla, dza, dga), (zb, lb, ob, gb, wb, dlb, dzb, dgb),
                 (zc, lc, oc, gc, wc, dlc, dzc, dgc))):
            gate = _sigmoid(l_ref[...].astype(F32) + mb[b:b + 1, :])
            dl = dyv * z_ref[...].astype(F32) * gate * (1.0 - gate)
            dl_ref[...] = dl.astype(BF16)
            rows.append(jnp.sum(dl, axis=0, keepdims=True))
            dz = (dyv * gate).astype(BF16)
            dz_ref[...] = dz
            du = jnp.dot(dz, w_ref[...], preferred_element_type=F32)
            g = g_ref[...].astype(F32)
            sg = _sigmoid(g)
            dos.append(du * (g * sg))
            dg_ref[...] = (du * ov_ref[...] * (sg * (1.0 + g * (1.0 - sg)))).astype(BF16)

        @pl.when(pl.program_id(0) == 0)
        def _():
            gmb[...] = jnp.zeros((3, D_MODEL), F32)

        for b in range(3):
            gmb[b:b + 1, :] += rows[b]
        dob[...] = dos[1].astype(BF16)
        doc[...] = dos[2].astype(BF16)
        doa = dos[0]
        row_term = jnp.dot(doa * oa[...], hs_ref[...], precision=lax.Precision.HIGHEST, preferred_element_type=F32)
        ws = _group_weights(l0[...], l1[...], l2[...])
        for wg, do_ref, dp_ref in zip(ws, (do0, do1, do2), (dp0, dp1, dp2)):
            do_ref[...] = (wg * doa).astype(BF16)
            dp_ref[...] = wg * row_term

    full = lambda shape: pl.BlockSpec(shape, lambda i: (0,) * len(shape))
    acc = pl.BlockSpec((3, D_MODEL), lambda i: (0, 0))
    in_specs = ([r1024] * 4 + logit_cols + [full((3, D_MODEL))] + [r512] * 3 + silu_cols + [r512] * 6
                + [full((D_MODEL, 512))] * 3 + [full((512, 512))])
    out_specs = [r1024] * 3 + [acc] + [r1024] * 3 + [r512] * 11
    out_shape = ([jax.ShapeDtypeStruct((SEQ, D_MODEL), BF16)] * 3 + [jax.ShapeDtypeStruct((3, D_MODEL), F32)]
                 + [jax.ShapeDtypeStruct((SEQ, D_MODEL), BF16)] * 3 + [jax.ShapeDtypeStruct((SEQ, 512), BF16)] * 6
                 + [jax.ShapeDtypeStruct((SEQ, 512), F32)] * 3 + [jax.ShapeDtypeStruct((SEQ, 512), BF16)] * 2)
    res = pl.pallas_call(
        body, name="gate_bwd", grid=(SEQ // GATE_ROWS,), in_specs=in_specs, out_specs=out_specs,
        out_shape=out_shape, compiler_params=_params(("arbitrary",)))(
            dy, *z, parts, parts, parts, merge_bias, *outs, parts, parts, parts, *o_grp, *l_grp, *wts, head_sum)
    return res[0:3], res[3], res[4:7], res[7:10], res[10:13], res[13:16], res[16], res[17]


def _post(y2, x, target, gain):
    rows = 256

    def body(y_ref, x_ref, t_ref, g_ref, do_ref, dy_ref, l_ref, gg_ref):
        yv = y_ref[...]
        rstd = lax.rsqrt(jnp.mean(yv * yv, axis=1, keepdims=True) + EPS)
        yn = yv * rstd
        gv = g_ref[...]
        err = x_ref[...] + yn * gv - t_ref[...]
        dout = err * (1.0 / D_MODEL)
        do_ref[...] = dout
        dn = dout * gv
        dy_ref[...] = (rstd * (dn - yn * jnp.mean(dn * yn, axis=1, keepdims=True))).astype(BF16)

        @pl.when(pl.program_id(0) == 0)
        def _():
            l_ref[...] = jnp.zeros((1, D_MODEL), F32)
            gg_ref[...] = jnp.zeros((1, D_MODEL), F32)

        l_ref[...] += jnp.sum(err * err, axis=0, keepdims=True)
        gg_ref[...] += jnp.sum(dout * yn, axis=0, keepdims=True)

    row = pl.BlockSpec((rows, D_MODEL), lambda i: (i, 0))
    vec = pl.BlockSpec((1, D_MODEL), lambda i: (0, 0))
    return pl.pallas_call(
        body, name="post", grid=(SEQ // rows,), in_specs=[row, row, row, vec], out_specs=[row, row, vec, vec],
        out_shape=[jax.ShapeDtypeStruct((SEQ, D_MODEL), F32), jax.ShapeDtypeStruct((SEQ, D_MODEL), BF16),
                   jax.ShapeDtypeStruct((1, D_MODEL), F32), jax.ShapeDtypeStruct((1, D_MODEL), F32)],
        compiler_params=_params(("arbitrary",)))(y2, x, target, gain)


def _local_step(x, mem, target, pre_norm, mem_norm, post_norm, na_rpb, merge_bias, wt_in, w_kv, wt_a, wt_b, wt_c,
                w_out):
    tabs = _rope_tables()
    h = _rmsnorm_fwd("prenorm", x, pre_norm, 256)
    hs = jnp.stack([_fold(h, d) for d in DILATIONS], axis=0)
    parts = _in_proj(hs, wt_in, tabs)

    o_grp, l_grp = [], []
    for g, d in enumerate(DILATIONS):
        o, l = _attn_fwd("dil_fwd_%d" % g, "dil", parts, parts, parts, 12 * g, 12 * g + 4, 12 * g + 8, d=d)
        o_grp.append(_unfold(o, d))
        l_grp.append(_unfold(l, d))
    bias = _na_bias(jnp.pad(na_rpb, ((0, 0), (0, 1), (0, 128 - 31))))
    out_b, lse_b = _attn_fwd("na_fwd", "na", parts, parts, parts, 36, 40, 44, bias=bias)
    memn = _rmsnorm_fwd("memnorm", mem, mem_norm, MEM_LEN)
    kv_m = _mm_simple("mem_kv", memn, w_kv, NN, BF16, MEM_LEN, 512, D_MODEL)
    out_c, lse_c = _attn_fwd("mem_fwd", "mem", parts, kv_m, kv_m, 48, 0, 4)

    wts = (wt_a, wt_b, wt_c)
    out_a, u, z, y = _gate_fwd(o_grp, l_grp, out_b, out_c, parts, merge_bias, wts)
    y2 = _mm_simple("out_proj", y, w_out, NN, F32, 512, D_MODEL, D_MODEL)
    dout, dy2, err_sq, g_post = _post(y2, x, target, post_norm)
    loss = 0.5 * jnp.sum(err_sq) / D_MODEL

    dy = _mm_simple("out_proj_dx", dy2, w_out, NT, BF16, 512, D_MODEL, D_MODEL)
    g_w_out = _mm_simple("out_proj_dw", y, dy2, TN, BF16, D_MODEL, 512, 512)

    rr = _iota((512, 512), 0) // HEAD_DIM
    cc = _iota((512, 512), 1) // HEAD_DIM
    head_sum = (rr == cc).astype(F32)
    dlog, g_mb, dz, dg, do_grp, dp_grp, do_b, do_c = _gate_bwd(
        dy, z, parts, merge_bias, (out_a, out_b, out_c), o_grp, l_grp, wts, head_sum)
    g_wt = [_mm_simple("branch_dw_%d" % b, dz[b], u[b], TN, BF16, D_MODEL, 512, 512) for b in range(3)]

    dqkv = []
    for g, d in enumerate(DILATIONS):
        dq, dk, dv = _attn_bwd("dil_bwd_%d" % g, "dil", parts, parts, parts, 12 * g, 12 * g + 4, 12 * g + 8,
                               _fold(do_grp[g], d), _fold(l_grp[g], d), dp=_fold(dp_grp[g], d), d=d, tabs=tabs[g])
        dqkv += [dq, dk, dv]
    dq_b, dk_b, dv_b, dbias = _attn_bwd("na_bwd", "na", parts, parts, parts, 36, 40, 44, do_b, lse_b, o=out_b,
                                        bias=bias)
    g_rpb_t = _na_bias_bwd(dbias)
    g_rpb = g_rpb_t[:, :15, :31] + jnp.pad(g_rpb_t[:, :14, 64:95], ((0, 0), (1, 0), (0, 0)))
    dq_c, dk_m, dv_m = _attn_bwd("mem_bwd", "mem", parts, kv_m, kv_m, 48, 0, 4, do_c, lse_c, o=out_c)

    dkv = jnp.concatenate([dk_m, dv_m], axis=1).astype(BF16)
    g_w_kv = _mm_simple("mem_kv_dw", memn, dkv, TN, BF16, D_MODEL, 512, MEM_LEN)
    dmemn = _mm_simple("mem_kv_dx", dkv, w_kv, NT, F32, MEM_LEN, 512, D_MODEL)
    g_mem_norm = _memnorm_bwd(mem, dmemn)

    dparts = jnp.concatenate(dqkv + [dq_b, dk_b, dv_b, dq_c] + list(dg) + list(dlog), axis=1)
    g_wt_in = _in_proj_dw(dparts, jnp.swapaxes(hs, 1, 2))
    dh = _in_proj_dh(dparts, wt_in)
    grad_x, g_pre = _prenorm_bwd(x, pre_norm, dh[0], _unfold(dh[1], 4), _unfold(dh[2], 16), dout)

    grads = dict(wt_in=g_wt_in, w_kv=g_w_kv, wt_a=g_wt[0], wt_b=g_wt[1], wt_c=g_wt[2], w_out=g_w_out,
                 merge_bias=g_mb, pre_norm=g_pre, mem_norm=g_mem_norm, post_norm=g_post, na_rpb=g_rpb)
    return loss, grad_x, grads


ANY = pl.BlockSpec(memory_space=pl.ANY)


def _place():
    return lax.axis_index("x"), lax.axis_index("y"), lax.axis_index("c")


def _all_gather(shards):
    nt = len(shards)

    def body(*refs):
        srcs, outs = refs[:nt], refs[nt:2 * nt]
        send_sems, recv_sems, local_sems = refs[2 * nt:]
        x, y, c = _place()
        me, sibling = (x, y, c), (x, y, 1 - c)
        chips = [(1 - x, y), (x, 1 - y), (1 - x, 1 - y)]

        def block(t, px, py, pc):
            return outs[t].at[4 * px + 2 * py + pc]

        def copy(k, t, blk, to, src=None):
            return pltpu.make_async_remote_copy(
                src_ref=block(t, *blk) if src is None else src, dst_ref=block(t, *blk),
                send_sem=send_sems.at[k * nt + t], recv_sem=recv_sems.at[k * nt + t],
                device_id=to, device_id_type=MESH_ID)

        mine = [pltpu.make_async_copy(srcs[t], block(t, *me), local_sems.at[t]) for t in range(nt)]
        for cp in mine:
            cp.start()
        first = [copy(0, t, me, sibling, src=srcs[t]) for t in range(nt)]
        for j, chip in enumerate(chips):
            first += [copy(1 + j, t, me, (*chip, c), src=srcs[t]) for t in range(nt)]
        for cp in first:
            cp.start()
        passed = []
        for j, chip in enumerate(chips):
            for t in range(nt):
                copy(1 + j, t, (*chip, c), me).wait_recv()
                fwd = copy(4 + j, t, (*chip, c), sibling)
                fwd.start()
                passed.append(fwd)
        for t in range(nt):
            copy(0, t, sibling, me).wait_recv()
        for j, chip in enumerate(chips):
            for t in range(nt):
                copy(4 + j, t, (*chip, 1 - c), me).wait_recv()
        for cp in first + passed:
            cp.wait_send()
        for cp in mine:
            cp.wait()

    return pl.pallas_call(
        body, name="all_gather", in_specs=[ANY] * nt, out_specs=[ANY] * nt,
        out_shape=[jax.ShapeDtypeStruct((N_DEV,) + s.shape, s.dtype) for s in shards],
        scratch_shapes=[pltpu.SemaphoreType.DMA((7 * nt,)), pltpu.SemaphoreType.DMA((7 * nt,)),
                        pltpu.SemaphoreType.DMA((nt,))])(*shards)


def _exchange_sibling(terms):
    nt = len(terms)

    def body(*refs):
        srcs, outs = refs[:nt], refs[nt:2 * nt]
        send_sems, recv_sems = refs[2 * nt:]
        x, y, c = _place()
        copies = []
        for q in range(4):
            for t in range(nt):
                copies.append(pltpu.make_async_remote_copy(
                    src_ref=srcs[t].at[2 * q + 1 - c], dst_ref=outs[t].at[q],
                    send_sem=send_sems.at[q * nt + t], recv_sem=recv_sems.at[q * nt + t],
                    device_id=(x, y, 1 - c), device_id_type=MESH_ID))
        for cp in copies:
            cp.start()
        for cp in copies:
            cp.wait()

    return pl.pallas_call(
        body, name="exchange_sibling", in_specs=[ANY] * nt, out_specs=[ANY] * nt,
        out_shape=[jax.ShapeDtypeStruct((4,) + s.shape[1:], s.dtype) for s in terms],
        scratch_shapes=[pltpu.SemaphoreType.DMA((4 * nt,)), pltpu.SemaphoreType.DMA((4 * nt,))])(*terms)


def _exchange_chips(sums):
    nt = len(sums)

    def body(*refs):
        srcs, outs = refs[:nt], refs[nt:2 * nt]
        send_sems, recv_sems = refs[2 * nt:]
        x, y, c = _place()
        chips = [(1 - x, y), (x, 1 - y), (1 - x, 1 - y)]
        copies = []
        for s, (tx, ty) in enumerate(chips):
            for t in range(nt):
                copies.append(pltpu.make_async_remote_copy(
                    src_ref=srcs[t].at[2 * tx + ty], dst_ref=outs[t].at[s],
                    send_sem=send_sems.at[s * nt + t], recv_sem=recv_sems.at[s * nt + t],
                    device_id=(tx, ty, c), device_id_type=MESH_ID))
        for cp in copies:
            cp.start()
        for cp in copies:
            cp.wait()

    return pl.pallas_call(
        body, name="exchange_chips", in_specs=[ANY] * nt, out_specs=[ANY] * nt,
        out_shape=[jax.ShapeDtypeStruct((3,) + s.shape[1:], s.dtype) for s in sums],
        scratch_shapes=[pltpu.SemaphoreType.DMA((3 * nt,)), pltpu.SemaphoreType.DMA((3 * nt,))])(*sums)


def _gather_small(block):
    def body(src, out, send_sems, recv_sems, local_sem):
        x, y, c = _place()
        me = 4 * x + 2 * y + c
        mine = pltpu.make_async_copy(src, out.at[me], local_sem)
        mine.start()
        copies = []
        for mask in range(1, 8):
            fx, fy, fc = (mask >> 2) & 1, (mask >> 1) & 1, mask & 1
            to = (jnp.where(fx, 1 - x, x), jnp.where(fy, 1 - y, y), jnp.where(fc, 1 - c, c))
            copies.append(pltpu.make_async_remote_copy(
                src_ref=src, dst_ref=out.at[me], send_sem=send_sems.at[mask - 1], recv_sem=recv_sems.at[mask - 1],
                device_id=to, device_id_type=MESH_ID))
        for cp in copies:
            cp.start()
        for cp in copies:
            cp.wait()
        mine.wait()

    return pl.pallas_call(
        body, name="gather_small", in_specs=[ANY], out_specs=ANY,
        out_shape=jax.ShapeDtypeStruct((N_DEV,) + block.shape, block.dtype),
        scratch_shapes=[pltpu.SemaphoreType.DMA((7,)), pltpu.SemaphoreType.DMA((7,)), pltpu.SemaphoreType.DMA])(block)


def _add_sibling(name, term, recv, rows):
    _, r, w = term.shape
    cidx = lax.axis_index("c").astype(jnp.int32).reshape(1)

    def body(c_ref, a_ref, b_ref, o_ref):
        o_ref[...] = (a_ref[...].astype(F32) + b_ref[...].astype(F32)).astype(o_ref.dtype)

    grid_spec = pltpu.PrefetchScalarGridSpec(
        num_scalar_prefetch=1, grid=(4, r // rows),
        in_specs=[pl.BlockSpec((None, rows, w), lambda q, i, c_ref: (2 * q + c_ref[0], i, 0)),
                  pl.BlockSpec((None, rows, w), lambda q, i, c_ref: (q, i, 0))],
        out_specs=pl.BlockSpec((None, rows, w), lambda q, i, c_ref: (q, i, 0)))
    return pl.pallas_call(
        body, name=name, grid_spec=grid_spec, out_shape=jax.ShapeDtypeStruct((4, r, w), term.dtype),
        compiler_params=_params(("parallel", "parallel")))(cidx, term, recv)


def _add_chips(name, sums, recv, rows):
    _, r, w = sums.shape
    qidx = (2 * lax.axis_index("x") + lax.axis_index("y")).astype(jnp.int32).reshape(1)

    def body(q_ref, a_ref, b_ref, o_ref):
        o_ref[...] = ((a_ref[...].astype(F32) + b_ref[0].astype(F32))
                      + (b_ref[1].astype(F32) + b_ref[2].astype(F32)))

    grid_spec = pltpu.PrefetchScalarGridSpec(
        num_scalar_prefetch=1, grid=(r // rows,),
        in_specs=[pl.BlockSpec((None, rows, w), lambda i, q_ref: (q_ref[0], i, 0)),
                  pl.BlockSpec((3, rows, w), lambda i, q_ref: (0, i, 0))],
        out_specs=pl.BlockSpec((rows, w), lambda i, q_ref: (i, 0)))
    return pl.pallas_call(
        body, name=name, grid_spec=grid_spec, out_shape=jax.ShapeDtypeStruct((r, w), F32),
        compiler_params=_params(("parallel",)))(qidx, sums, recv)


def _reduce_scatter(names, terms):
    def rows_of(a):
        return SHARD_IN // 4 if a.shape[1] == SHARD_IN else a.shape[1]

    recv1 = _exchange_sibling(terms)
    sums = [_add_sibling("add_sibling_" + n, t, r, rows_of(t)) for n, t, r in zip(names, terms, recv1)]
    recv2 = _exchange_chips(sums)
    return [_add_chips("add_chips_" + n, s, r, rows_of(s)) for n, s, r in zip(names, sums, recv2)]


def _adamw(name, w, g, m, v, rows=None):
    r, c = w.shape
    rows = r if rows is None else rows
    c1 = 1.0 - ADAM_B1 ** ADAM_STEP
    c2 = 1.0 - ADAM_B2 ** ADAM_STEP

    def body(w_ref, g_ref, m_ref, v_ref, d_ref, nm_ref, nv_ref):
        gv = g_ref[...]
        nm = ADAM_B1 * m_ref[...] + (1.0 - ADAM_B1) * gv
        nv = ADAM_B2 * v_ref[...] + (1.0 - ADAM_B2) * (gv * gv)
        nm_ref[...] = nm
        nv_ref[...] = nv
        d_ref[...] = -ADAM_LR * ((nm / c1) / (jnp.sqrt(nv / c2) + ADAM_EPS) + ADAM_WD * w_ref[...])

    spec = pl.BlockSpec((rows, c), lambda i: (i, 0))
    return pl.pallas_call(
        body, name=name, grid=(r // rows,), in_specs=[spec] * 4, out_specs=[spec] * 3,
        out_shape=[jax.ShapeDtypeStruct((r, c), F32)] * 3, compiler_params=_params(("parallel",)))(w, g, m, v)


def _sum_devices(gathered):
    def body(g_ref, o_ref):
        acc = g_ref[0]
        for j in range(1, N_DEV):
            acc = acc + g_ref[j]
        o_ref[...] = acc

    return pl.pallas_call(
        body, name="sum_devices", out_shape=jax.ShapeDtypeStruct(gathered.shape[1:], F32),
        compiler_params=_params())(gathered)


def _rows128(a, rows):
    flat = a.reshape(-1)
    return jnp.pad(flat, (0, rows * 128 - flat.shape[0])).reshape(rows, 128)


def kernel(x, mem, pre_norm, w_in, merge_bias, na_rpb, mem_norm, w_mem_kv, w_branch_a, w_branch_b, w_branch_c, w_out, post_norm, loss_target, m_pre_norm, m_w_in, m_merge_bias, m_na_rpb, m_mem_norm, m_w_mem_kv, m_w_branch_a, m_w_branch_b, m_w_branch_c, m_w_out, m_post_norm, v_pre_norm, v_w_in, v_merge_bias, v_na_rpb, v_mem_norm, v_w_mem_kv, v_w_branch_a, v_w_branch_b, v_w_branch_c, v_w_out, v_post_norm):
    wt_in_s = w_in[0].T.astype(BF16)
    rows_s = jnp.concatenate([w_mem_kv[0], w_out[0]], axis=0).astype(BF16)
    cols_s = jnp.concatenate([w_branch_a[0].T, w_branch_b[0].T, w_branch_c[0].T], axis=0).astype(BF16)
    mb_s = jnp.pad(merge_bias[0], ((0, 5), (0, 0)))
    g_in, g_rows, g_cols, g_mb = _all_gather([wt_in_s, rows_s, cols_s, mb_s])
    wt_in = g_in.reshape(N_IN, D_MODEL)
    w_kv = g_rows[:, :128].reshape(D_MODEL, D_MODEL)
    w_o = g_rows[:, 128:].reshape(D_MODEL, D_MODEL)
    wt_a = g_cols[:, 0:128].reshape(D_MODEL, 512)
    wt_b = g_cols[:, 128:256].reshape(D_MODEL, 512)
    wt_c = g_cols[:, 256:384].reshape(D_MODEL, 512)
    mb_full = g_mb[:, :3].transpose(1, 0, 2).reshape(3, D_MODEL)

    loss_term, grad_x, grads = _local_step(
        x[0], mem[0], loss_target[0], pre_norm, mem_norm, post_norm, na_rpb[0], mb_full,
        wt_in, w_kv, wt_a, wt_b, wt_c, w_o)
    loss = lax.psum(loss_term, ("x", "y", "c"))

    gmb_t = jnp.pad(grads["merge_bias"].reshape(3, N_DEV, 128).transpose(1, 0, 2), ((0, 0), (0, 5), (0, 0)))
    names = ["w_in", "w_kv", "w_out", "a", "b", "c", "mb"]
    terms = [grads["wt_in"].reshape(N_DEV, SHARD_IN, D_MODEL), grads["w_kv"].reshape(N_DEV, 128, D_MODEL),
             grads["w_out"].reshape(N_DEV, 128, D_MODEL), grads["wt_a"].reshape(N_DEV, 128, 512),
             grads["wt_b"].reshape(N_DEV, 128, 512), grads["wt_c"].reshape(N_DEV, 128, 512), gmb_t]
    gt_in, g_kv, g_out, gt_a, gt_b, gt_c, g_mb8 = _reduce_scatter(names, terms)

    small = jnp.concatenate([_rows128(grads["pre_norm"], 8), _rows128(grads["mem_norm"], 8),
                             _rows128(grads["post_norm"], 8), _rows128(grads["na_rpb"], 32)], axis=0)
    total = _sum_devices(_gather_small(small))
    g_pre = total[0:8].reshape(1, D_MODEL)
    g_memn = total[8:16].reshape(1, D_MODEL)
    g_post = total[16:24].reshape(1, D_MODEL)
    g_rpb = total[24:56].reshape(-1)[:8 * 15 * 31].reshape(1, 8, 15, 31)

    grad = {
        "pre_norm": g_pre, "w_in": gt_in.T[None], "merge_bias": g_mb8[:3][None], "na_rpb": g_rpb,
        "mem_norm": g_memn, "w_mem_kv": g_kv[None], "w_branch_a": gt_a.T[None], "w_branch_b": gt_b.T[None],
        "w_branch_c": gt_c.T[None], "w_out": g_out[None], "post_norm": g_post}
    weights = {
        "pre_norm": (pre_norm, m_pre_norm, v_pre_norm), "w_in": (w_in, m_w_in, v_w_in),
        "merge_bias": (merge_bias, m_merge_bias, v_merge_bias), "na_rpb": (na_rpb, m_na_rpb, v_na_rpb),
        "mem_norm": (mem_norm, m_mem_norm, v_mem_norm), "w_mem_kv": (w_mem_kv, m_w_mem_kv, v_w_mem_kv),
        "w_branch_a": (w_branch_a, m_w_branch_a, v_w_branch_a), "w_branch_b": (w_branch_b, m_w_branch_b, v_w_branch_b),
        "w_branch_c": (w_branch_c, m_w_branch_c, v_w_branch_c), "w_out": (w_out, m_w_out, v_w_out),
        "post_norm": (post_norm, m_post_norm, v_post_norm)}
    order = ["pre_norm", "w_in", "merge_bias", "na_rpb", "mem_norm", "w_mem_kv", "w_branch_a", "w_branch_b",
             "w_branch_c", "w_out", "post_norm"]
    delta, new_m, new_v = {}, {}, {}
    for n in order:
        w, m, v = weights[n]
        shape = w.shape
        two_d = (-1, shape[-1])
        rows = 256 if n == "w_in" else None
        dl, nm, nv = _adamw("adamw_" + n, w.reshape(two_d), grad[n].reshape(two_d), m.reshape(two_d),
                            v.reshape(two_d), rows)
        delta[n], new_m[n], new_v[n] = dl.reshape(shape), nm.reshape(shape), nv.reshape(shape)

    return (loss, grad_x[None], *[grad[n] for n in order], *[delta[n] for n in order],
            *[new_m[n] for n in order], *[new_v[n] for n in order])
```

```python
import functools

import numpy as np
import jax
import jax.numpy as jnp
from jax import lax
from jax.experimental import pallas as pl
from jax.experimental.pallas import tpu as pltpu

F32 = jnp.float32
BF16 = jnp.bfloat16

SEQ = 2048
D_MODEL = 1024
N_IN = 11264
N_DEV = 8
SHARD_IN = N_IN // N_DEV
HEAD_DIM = 64
GRID_W = 64
NA_ROWS = 8
MEM_LEN = 256
DILATIONS = (1, 4, 16)
REACH = 64
ROPE_THETA = 500000.0
ROPE_DIM = 16
EPS = 1e-6
NEG = -1e30
ADAM_LR = 0.001
ADAM_B1 = 0.9
ADAM_B2 = 0.999
ADAM_EPS = 1e-08
ADAM_WD = 0.01
ADAM_STEP = 10

VMEM_LIMIT_BYTES = 56 * 1024 * 1024
MESH_ID = pl.DeviceIdType.MESH

NN = (((1,), (0,)), ((), ()))
NT = (((1,), (1,)), ((), ()))
TN = (((0,), (0,)), ((), ()))


def _params(sem=None):
    return pltpu.CompilerParams(dimension_semantics=sem, vmem_limit_bytes=VMEM_LIMIT_BYTES)


def _iota(shape, dim):
    return lax.broadcasted_iota(jnp.int32, shape, dim)


def _sigmoid(x):
    return 1.0 / (1.0 + jnp.exp(-x))


def _fold(a, d):
    if d == 1:
        return a
    n, w = a.shape
    return a.reshape(n // d, d, w).transpose(1, 0, 2).reshape(n, w)


def _unfold(a, d):
    if d == 1:
        return a
    n, w = a.shape
    return a.reshape(d, n // d, w).transpose(1, 0, 2).reshape(n, w)


def _rope_tables():
    half = ROPE_DIM // 2
    inv = (ROPE_THETA ** (-np.arange(half, dtype=np.float64) * 2.0 / ROPE_DIM)).astype(np.float32)
    pos = np.arange(SEQ, dtype=np.float32)
    ang = pos[:, None] * inv[None, :]
    cos, sin = np.cos(ang), np.sin(ang)
    zeros = np.zeros_like(cos)
    rest = HEAD_DIM - ROPE_DIM
    c64 = np.concatenate([cos, cos, np.ones((SEQ, rest), np.float32)], axis=1)
    s1 = np.concatenate([zeros, sin, np.zeros((SEQ, rest), np.float32)], axis=1)
    s2 = np.concatenate([-sin, zeros, np.zeros((SEQ, rest), np.float32)], axis=1)

    def fold(t, d):
        return t.reshape(SEQ // d, d, t.shape[1]).transpose(1, 0, 2).reshape(SEQ, t.shape[1])

    tabs = [np.stack([np.tile(fold(t, d), (1, 2)) for t in (c64, s1, s2)], axis=0) for d in DILATIONS]
    return jnp.asarray(np.stack(tabs, axis=0), dtype=F32)


def _rope(a, c, s1, s2):
    return a * c + pltpu.roll(a, 8, 1) * s1 + pltpu.roll(a, 120, 1) * s2


def _rope_t(a, c, s1, s2):
    return a * c + pltpu.roll(a * s1, 120, 1) + pltpu.roll(a * s2, 8, 1)


def _perm_of_block(j):
    return jnp.where(j < 3, 0, jnp.where(j < 6, 1, jnp.where(j < 9, 2, 0)))


def _mm(name, a, b, out_shape, out_dtype, grid, a_spec, b_spec, o_spec, acc_shape, dims, k_axis, nk):
    def body(a_ref, b_ref, o_ref, acc_ref):
        k = pl.program_id(k_axis)

        @pl.when(k == 0)
        def _():
            acc_ref[...] = jnp.zeros(acc_shape, F32)

        acc_ref[...] += lax.dot_general(a_ref[...], b_ref[...], dims, preferred_element_type=F32)

        @pl.when(k == nk - 1)
        def _():
            o_ref[...] = acc_ref[...].astype(out_dtype)

    sem = tuple("arbitrary" if ax == k_axis else "parallel" for ax in range(len(grid)))
    return pl.pallas_call(
        body, name=name, grid=grid, in_specs=[a_spec, b_spec], out_specs=o_spec,
        out_shape=jax.ShapeDtypeStruct(out_shape, out_dtype),
        scratch_shapes=[pltpu.VMEM(acc_shape, F32)], compiler_params=_params(sem))(a, b)


def _mm_simple(name, a, b, dims, out_dtype, tm, tn, tk):
    if dims is NN:
        m, kk = a.shape
        n = b.shape[1]
        a_spec = pl.BlockSpec((tm, tk), lambda i, j, k: (i, k))
        b_spec = pl.BlockSpec((tk, tn), lambda i, j, k: (k, j))
    elif dims is NT:
        m, kk = a.shape
        n = b.shape[0]
        a_spec = pl.BlockSpec((tm, tk), lambda i, j, k: (i, k))
        b_spec = pl.BlockSpec((tn, tk), lambda i, j, k: (j, k))
    else:
        kk, m = a.shape
        n = b.shape[1]
        a_spec = pl.BlockSpec((tk, tm), lambda i, j, k: (k, i))
        b_spec = pl.BlockSpec((tk, tn), lambda i, j, k: (k, j))
    grid = (m // tm, n // tn, kk // tk)
    o_spec = pl.BlockSpec((tm, tn), lambda i, j, k: (i, j))
    return _mm(name, a, b, (m, n), out_dtype, grid, a_spec, b_spec, o_spec, (tm, tn), dims, 2, kk // tk)


def _rmsnorm_fwd(name, x, gain, rows):
    n, d = x.shape

    def body(x_ref, g_ref, o_ref):
        xv = x_ref[...]
        rstd = lax.rsqrt(jnp.mean(xv * xv, axis=1, keepdims=True) + EPS)
        o_ref[...] = (xv * rstd * g_ref[...]).astype(BF16)

    return pl.pallas_call(
        body, name=name, grid=(n // rows,),
        in_specs=[pl.BlockSpec((rows, d), lambda i: (i, 0)), pl.BlockSpec((1, d), lambda i: (0, 0))],
        out_specs=pl.BlockSpec((rows, d), lambda i: (i, 0)),
        out_shape=jax.ShapeDtypeStruct((n, d), BF16), compiler_params=_params(("parallel",)))(x, gain)


def _folded_rows(first, rows, d):
    if d == 1:
        return pl.ds(pl.multiple_of(first, rows), rows)
    mlen = SEQ // d
    return pl.ds((first % mlen) * d + first // mlen, rows, stride=d)


def _prenorm_fold(x, gain):
    rows = 128

    nchunk = D_MODEL // 128

    def body(*refs):
        x_refs, g_ref, hs_ref, hst_ref = refs[:nchunk], refs[nchunk], refs[nchunk + 1], refs[nchunk + 2]
        first = pl.program_id(0) * rows
        for p, d in enumerate(DILATIONS):
            idx = _folded_rows(first, rows, d)
            xv = jnp.concatenate([r[idx, :] for r in x_refs], axis=1)
            rstd = lax.rsqrt(jnp.mean(xv * xv, axis=1, keepdims=True) + EPS)
            h = xv * rstd * g_ref[...]
            hs_ref[p] = h.astype(BF16)
            hst_ref[p] = h.T.astype(BF16)

    x_specs = [pl.BlockSpec((SEQ, 128), functools.partial(lambda c, i: (0, c), c)) for c in range(nchunk)]
    return pl.pallas_call(
        body, name="prenorm", grid=(SEQ // rows,),
        in_specs=x_specs + [pl.BlockSpec((1, D_MODEL), lambda i: (0, 0))],
        out_specs=[pl.BlockSpec((3, rows, D_MODEL), lambda i: (0, i, 0)),
                   pl.BlockSpec((3, D_MODEL, rows), lambda i: (0, 0, i))],
        out_shape=[jax.ShapeDtypeStruct((3, SEQ, D_MODEL), BF16), jax.ShapeDtypeStruct((3, D_MODEL, SEQ), BF16)],
        compiler_params=_params(("parallel",)))(*([x] * nchunk), gain)


def _prenorm_bwd(x, gain, dh, dout):
    rows = 256

    def body(x_ref, g_ref, a_ref, do_ref, dx_ref, gg_ref):
        xv = x_ref[...]
        rstd = lax.rsqrt(jnp.mean(xv * xv, axis=1, keepdims=True) + EPS)
        xn = xv * rstd
        dh = jnp.concatenate([a_ref[c] for c in range(D_MODEL // 128)], axis=1)
        gdh = dh * g_ref[...]
        dx_ref[...] = rstd * (gdh - xn * jnp.mean(gdh * xn, axis=1, keepdims=True)) + do_ref[...]

        @pl.when(pl.program_id(0) == 0)
        def _():
            gg_ref[...] = jnp.zeros((1, D_MODEL), F32)

        gg_ref[...] += jnp.sum(dh * xn, axis=0, keepdims=True)

    row = pl.BlockSpec((rows, D_MODEL), lambda i: (i, 0))
    vec = pl.BlockSpec((1, D_MODEL), lambda i: (0, 0))
    return pl.pallas_call(
        body, name="prenorm_bwd", grid=(SEQ // rows,),
        in_specs=[row, vec, pl.BlockSpec((D_MODEL // 128, rows, 128), lambda i: (0, i, 0)), row], out_specs=[row, vec],
        out_shape=[jax.ShapeDtypeStruct((SEQ, D_MODEL), F32), jax.ShapeDtypeStruct((1, D_MODEL), F32)],
        compiler_params=_params(("arbitrary",)))(x, gain, dh, dout)


def _memnorm_bwd(mem, dmemn):
    def body(m_ref, d_ref, gg_ref):
        mv = m_ref[...]
        rstd = lax.rsqrt(jnp.mean(mv * mv, axis=1, keepdims=True) + EPS)
        gg_ref[...] = jnp.sum(d_ref[...] * mv * rstd, axis=0, keepdims=True)

    return pl.pallas_call(
        body, name="memnorm_bwd", out_shape=jax.ShapeDtypeStruct((1, D_MODEL), F32),
        compiler_params=_params())(mem, dmemn)


def _in_proj(hs, wt, tabs):
    tm, tn = 512, 512

    def body(h_ref, w_ref, t_ref, o_ref):
        j = pl.program_id(0)
        is_rope = jnp.logical_and(j < 9, j % 3 != 2)
        row_slices = [slice(r * tm, (r + 1) * tm) for r in range(SEQ // tm)]

        def product(rs):
            return lax.dot_general(h_ref[rs, :], w_ref[...], NT, preferred_element_type=F32)

        @pl.when(is_rope)
        def _():
            for rs in row_slices:
                acc = product(rs)
                c, s1, s2 = t_ref[0, rs, :], t_ref[1, rs, :], t_ref[2, rs, :]
                for q in range(tn // 128):
                    a = acc[:, q * 128:(q + 1) * 128]
                    o_ref[rs, q * 128:(q + 1) * 128] = _rope(a, c, s1, s2).astype(BF16)

        @pl.when(jnp.logical_not(is_rope))
        def _():
            for rs in row_slices:
                o_ref[rs, :] = product(rs).astype(BF16)

    return pl.pallas_call(
        body, name="in_proj", grid=(N_IN // tn,),
        in_specs=[pl.BlockSpec((None, SEQ, D_MODEL), lambda j: (_perm_of_block(j), 0, 0)),
                  pl.BlockSpec((tn, D_MODEL), lambda j: (j, 0)),
                  pl.BlockSpec((None, 3, SEQ, 128), lambda j: (_perm_of_block(j), 0, 0, 0))],
        out_specs=pl.BlockSpec((SEQ, tn), lambda j: (0, j)),
        out_shape=jax.ShapeDtypeStruct((SEQ, N_IN), BF16),
        compiler_params=_params(("parallel",)))(hs, wt, tabs)


def _in_proj_dw(dparts, hst):
    tn = 512

    def body(h_ref, d_ref, o_ref):
        acc = jnp.dot(h_ref[...], d_ref[...], preferred_element_type=F32)
        o_ref[...] = acc.T.astype(BF16)

    return pl.pallas_call(
        body, name="in_proj_dw", grid=(N_IN // tn,),
        in_specs=[pl.BlockSpec((None, D_MODEL, SEQ), lambda j: (_perm_of_block(j), 0, 0)),
                  pl.BlockSpec((SEQ, tn), lambda j: (0, j))],
        out_specs=pl.BlockSpec((tn, D_MODEL), lambda j: (j, 0)),
        out_shape=jax.ShapeDtypeStruct((N_IN, D_MODEL), BF16), compiler_params=_params(("parallel",)))(hst, dparts)


def _in_proj_dh(dparts, wt):
    tk = 512
    nblk = N_IN // tk
    nchunk = D_MODEL // 128

    def col(s):
        return jnp.where(s < 3, s, jnp.where(s < 16, s + 6, s - 13))

    def body(d_ref, w_ref, o_ref, acc_ref):
        s = pl.program_id(0)
        row_slices = [slice(r * 512, (r + 1) * 512) for r in range(SEQ // 512)]

        def product(rs):
            return jnp.dot(d_ref[rs, :], w_ref[...], preferred_element_type=F32)

        def accumulate(cond, to_out, init):
            @pl.when(cond)
            def _():
                for rs in row_slices:
                    prod = product(rs)
                    if not to_out:
                        if init:
                            acc_ref[rs, :] = prod
                        else:
                            acc_ref[rs, :] += prod
                        continue
                    for c in range(nchunk):
                        if init:
                            o_ref[c, rs, :] = prod[:, c * 128:(c + 1) * 128]
                        else:
                            o_ref[c, rs, :] += prod[:, c * 128:(c + 1) * 128]

        accumulate(s == 0, True, True)
        accumulate(jnp.logical_and(s > 0, s < 16), True, False)
        accumulate(jnp.logical_or(s == 16, s == 19), False, True)
        accumulate(jnp.logical_and(s > 16, s != 19), False, False)
        for last, d in ((18, 4), (21, 16)):
            @pl.when(s == last)
            def _():
                mlen = SEQ // d
                for r in range(d):
                    for c in range(nchunk):
                        o_ref[c, pl.ds(r, mlen, stride=d), :] += acc_ref[r * mlen:(r + 1) * mlen,
                                                                         c * 128:(c + 1) * 128]

    return pl.pallas_call(
        body, name="in_proj_dh", grid=(nblk,),
        in_specs=[pl.BlockSpec((SEQ, tk), lambda s: (0, col(s))),
                  pl.BlockSpec((tk, D_MODEL), lambda s: (col(s), 0))],
        out_specs=pl.BlockSpec((nchunk, SEQ, 128), lambda s: (0, 0, 0)),
        out_shape=jax.ShapeDtypeStruct((nchunk, SEQ, 128), F32),
        scratch_shapes=[pltpu.VMEM((SEQ, D_MODEL), F32)], compiler_params=_params(("arbitrary",)))(dparts, wt)


def _head_lanes(lanes, hh):
    return lanes >= 64 if hh == 1 else lanes < 64


def _head_rows(x, lanes, hh, pair):
    if not pair:
        return jnp.max(x, axis=1, keepdims=True)
    return jnp.max(jnp.where(_head_lanes(lanes, hh), x, -jnp.inf), axis=1, keepdims=True)


def _mask_head(x, lanes, hh, pair):
    if not pair:
        return x
    return jnp.where(_head_lanes(lanes, hh), x.astype(F32), 0.0).astype(BF16)


def _merge_heads(parts, lanes, pair):
    if not pair:
        return parts[0]
    return jnp.where(lanes < 64, parts[0], parts[1])


def _window(mode, qi, tq, mlen, tk):
    if mode == "dil":
        q0 = qi * tq
        seg = (q0 // mlen) * mlen
        ks = jnp.clip(q0 - REACH, seg, seg + mlen - tk)
        return pl.multiple_of(ks, 64)
    if mode == "na":
        r_start = jnp.clip(qi - NA_ROWS // 2, 0, SEQ // GRID_W - NA_ROWS)
        return pl.multiple_of(r_start * GRID_W, 64)
    return 0


def _scores(mode, qh, k, scale, qi, tq, tk, ks, bias_ref, hh):
    s = lax.dot_general(qh, k, NT, preferred_element_type=F32) * scale
    if mode == "dil":
        qpos = qi * tq + _iota((tq, tk), 0)
        kpos = ks + _iota((tq, tk), 1)
        s = jnp.where(jnp.abs(qpos - kpos) <= REACH, s, NEG)
    elif mode == "na":
        off = qi - jnp.clip(qi - NA_ROWS // 2, 0, SEQ // GRID_W - NA_ROWS)
        s = s + bias_ref[hh, off]
    return s


def _attn_cfg(mode, d):
    if mode == "dil":
        mlen = SEQ // d
        return dict(pair=True, tq=128, tk=min(256, mlen), mlen=mlen, lk=SEQ, scale=HEAD_DIM ** -0.5, units=4,
                    nsub=ATTN_SUBTILES)
    if mode == "na":
        return dict(pair=True, tq=GRID_W, tk=NA_ROWS * GRID_W, mlen=SEQ, lk=SEQ, scale=HEAD_DIM ** -0.5, units=4,
                    nsub=ATTN_SUBTILES)
    return dict(pair=False, tq=128, tk=MEM_LEN, mlen=SEQ, lk=MEM_LEN, scale=128 ** -0.5, units=4,
                nsub=ATTN_SUBTILES)


ATTN_SUBTILES = 4


def _attn_fwd(name, mode, q_arr, k_arr, v_arr, qcol, kcol, vcol, d=1, bias=None):
    cfg = _attn_cfg(mode, d)
    pair, tq, tk, mlen, lk, scale = cfg["pair"], cfg["tq"], cfg["tk"], cfg["mlen"], cfg["lk"], cfg["scale"]
    nh = 2 if pair else 1
    nsub = cfg["nsub"]
    rows = nsub * tq

    def body(*refs):
        if mode == "na":
            q_ref, k_ref, v_ref, bias_ref, o_ref, l_ref = refs
        else:
            q_ref, k_ref, v_ref, o_ref, l_ref = refs
            bias_ref = None
        lanes = _iota((tq, 128), 1)
        chains = [(sub, hh) for sub in range(nsub) for hh in range(nh)]
        qis = [pl.program_id(1) * nsub + sub for sub in range(nsub)]
        kss = [_window(mode, qi, tq, mlen, tk) for qi in qis]
        vs = [v_ref[pl.ds(ks, tk), :] for ks in kss]
        ss = []
        for sub, hh in chains:
            q = q_ref[sub * tq:(sub + 1) * tq, :]
            k = k_ref[pl.ds(kss[sub], tk), :]
            ss.append(_scores(mode, _mask_head(q, lanes, hh, pair), k, scale, qis[sub], tq, tk, kss[sub], bias_ref, hh))
        ms = [jnp.max(s, axis=1, keepdims=True) for s in ss]
        ps = [jnp.exp(s - m) for s, m in zip(ss, ms)]
        ls = [jnp.sum(p, axis=1, keepdims=True) for p in ps]
        os_ = [jnp.dot(p.astype(BF16), vs[sub], preferred_element_type=F32) for p, (sub, hh) in zip(ps, chains)]
        for sub in range(nsub):
            sel = [i for i, (s_, hh) in enumerate(chains) if s_ == sub]
            outs = [os_[i] / ls[i] for i in sel]
            lses = [jnp.broadcast_to(ms[i] + jnp.log(ls[i]), (tq, 128)) for i in sel]
            dst = _folded_rows(qis[sub] * tq, tq, d) if mode == "dil" else slice(sub * tq, (sub + 1) * tq)
            o_ref[dst, :] = _merge_heads(outs, lanes, pair)
            l_ref[dst, :] = _merge_heads(lses, lanes, pair)

    in_specs = [pl.BlockSpec((rows, 128), lambda u, i: (i, qcol + u)),
                pl.BlockSpec((lk, 128), lambda u, i: (0, kcol + u)),
                pl.BlockSpec((lk, 128), lambda u, i: (0, vcol + u))]
    args = [q_arr, k_arr, v_arr]
    if mode == "na":
        in_specs.append(pl.BlockSpec((2, NA_ROWS, GRID_W, NA_ROWS * GRID_W), lambda u, i: (u, 0, 0, 0)))
        args.append(bias)
    if mode == "dil":
        out_spec = pl.BlockSpec((SEQ, 128), lambda u, i: (0, u))
    else:
        out_spec = pl.BlockSpec((rows, 128), lambda u, i: (i, u))
    return pl.pallas_call(
        body, name=name, grid=(cfg["units"], SEQ // rows), in_specs=in_specs, out_specs=[out_spec, out_spec],
        out_shape=[jax.ShapeDtypeStruct((SEQ, 512), F32), jax.ShapeDtypeStruct((SEQ, 512), F32)],
        compiler_params=_params(("parallel", "arbitrary")))(*args)


def _attn_bwd(name, mode, q_arr, k_arr, v_arr, qcol, kcol, vcol, do, lse, dp=None, o=None, d=1, bias=None,
              tabs=None):
    cfg = _attn_cfg(mode, d)
    pair, tq, tk, mlen, lk, scale = cfg["pair"], cfg["tq"], cfg["tk"], cfg["mlen"], cfg["lk"], cfg["scale"]
    nh = 2 if pair else 1
    nsub = cfg["nsub"]
    rows = nsub * tq
    nq = SEQ // rows
    kv_dtype = F32 if mode == "mem" else BF16

    def body(*refs):
        refs = list(refs)
        q_ref, k_ref, v_ref, do_ref, l_ref = refs[:5]
        rest = refs[5:]
        bias_ref = tq_ref = tk_ref = db_ref = None
        if mode == "dil":
            dp_ref, tq_ref, tk_ref, dq_ref, dk_ref, dv_ref, dk_acc, dv_acc = rest
        elif mode == "na":
            o_ref, bias_ref, dq_ref, dk_ref, dv_ref, db_ref, dk_acc, dv_acc = rest
        else:
            o_ref, dq_ref, dk_ref, dv_ref, dk_acc, dv_acc = rest
        step = pl.program_id(1)

        @pl.when(step == 0)
        def _():
            dk_acc[...] = jnp.zeros((lk, 128), F32)
            dv_acc[...] = jnp.zeros((lk, 128), F32)
            if mode == "na":
                db_ref[...] = jnp.zeros(db_ref.shape, F32)

        lanes = _iota((tq, 128), 1)
        lanes_k = _iota((tk, 128), 1)
        chains = [(sub, hh) for sub in range(nsub) for hh in range(nh)]
        qis = [step * nsub + sub for sub in range(nsub)]
        sls = [slice(sub * tq, (sub + 1) * tq) for sub in range(nsub)]
        kss = [_window(mode, qi, tq, mlen, tk) for qi in qis]
        qs = [q_ref[sl, :] for sl in sls]
        ks_ = [k_ref[pl.ds(ks, tk), :] for ks in kss]
        vs = [v_ref[pl.ds(ks, tk), :] for ks in kss]
        dovs, lsevs, dpvs = [], [], []
        for sub in range(nsub):
            if mode == "dil":
                src = _folded_rows(qis[sub] * tq, tq, d)
                dovs.append(do_ref[src, :].astype(BF16))
                lsevs.append(l_ref[src, :])
                dpvs.append(dp_ref[src, :])
            else:
                dovs.append(do_ref[sls[sub], :])
                lsevs.append(l_ref[sls[sub], :])
                dpvs.append(dovs[sub].astype(F32) * o_ref[sls[sub], :])
        ss = [_scores(mode, _mask_head(qs[sub], lanes, hh, pair), ks_[sub], scale, qis[sub], tq, tk, kss[sub],
                      bias_ref, hh) for sub, hh in chains]
        dpms = [lax.dot_general(_mask_head(dovs[sub], lanes, hh, pair), vs[sub], NT, preferred_element_type=F32)
                for sub, hh in chains]
        ps = [jnp.exp(s - _head_rows(lsevs[sub], lanes, hh, pair)) for s, (sub, hh) in zip(ss, chains)]
        dphs = []
        for sub, hh in chains:
            if mode == "dil":
                dphs.append(_head_rows(dpvs[sub], lanes, hh, pair))
            elif pair:
                dphs.append(jnp.sum(jnp.where(_head_lanes(lanes, hh), dpvs[sub], 0.0), axis=1, keepdims=True))
            else:
                dphs.append(jnp.sum(dpvs[sub], axis=1, keepdims=True))
        dss = [p * (dpm - dph) for p, dpm, dph in zip(ps, dpms, dphs)]
        if mode == "na":
            for ds, (sub, hh) in zip(dss, chains):
                off = qis[sub] - jnp.clip(qis[sub] - NA_ROWS // 2, 0, SEQ // GRID_W - NA_ROWS)
                db_ref[hh, off] += ds
        dsbs = [ds.astype(BF16) for ds in dss]
        dvs = [lax.dot_general(p.astype(BF16), dovs[sub], TN, preferred_element_type=F32)
               for p, (sub, hh) in zip(ps, chains)]
        dqs = [jnp.dot(dsb, ks_[sub], preferred_element_type=F32) * scale for dsb, (sub, hh) in zip(dsbs, chains)]
        dks = [lax.dot_general(dsb, qs[sub], TN, preferred_element_type=F32) * scale
               for dsb, (sub, hh) in zip(dsbs, chains)]
        for sub in range(nsub):
            sel = [i for i, (s_, hh) in enumerate(chains) if s_ == sub]
            sl = sls[sub]
            dq = _merge_heads([dqs[i] for i in sel], lanes, pair)
            if mode == "dil":
                dq = _rope_t(dq, tq_ref[0, sl, :], tq_ref[1, sl, :], tq_ref[2, sl, :])
            dq_ref[sl, :] = dq.astype(BF16)
            dk_acc[pl.ds(kss[sub], tk), :] += _merge_heads([dks[i] for i in sel], lanes_k, pair)
            dv_acc[pl.ds(kss[sub], tk), :] += _merge_heads([dvs[i] for i in sel], lanes_k, pair)

        @pl.when(step == nq - 1)
        def _():
            dkv = dk_acc[...]
            if mode == "dil":
                dkv = _rope_t(dkv, tk_ref[0], tk_ref[1], tk_ref[2])
            dk_ref[...] = dkv.astype(kv_dtype)
            dv_ref[...] = dv_acc[...].astype(kv_dtype)

    q_spec = pl.BlockSpec((rows, 128), lambda u, i: (i, qcol + u))
    row_spec = pl.BlockSpec((rows, 128), lambda u, i: (i, u))
    kv_out = pl.BlockSpec((lk, 128), lambda u, i: (0, u))
    whole = pl.BlockSpec((SEQ, 128), lambda u, i: (0, u))
    nat_spec = whole if mode == "dil" else row_spec
    in_specs = [q_spec,
                pl.BlockSpec((lk, 128), lambda u, i: (0, kcol + u)),
                pl.BlockSpec((lk, 128), lambda u, i: (0, vcol + u)),
                nat_spec, nat_spec]
    args = [q_arr, k_arr, v_arr, do, lse]
    out_specs = [row_spec, kv_out, kv_out]
    out_shape = [jax.ShapeDtypeStruct((SEQ, 512), BF16), jax.ShapeDtypeStruct((lk, 512), kv_dtype),
                 jax.ShapeDtypeStruct((lk, 512), kv_dtype)]
    if mode == "dil":
        in_specs += [whole, pl.BlockSpec((3, rows, 128), lambda u, i: (0, i, 0)),
                     pl.BlockSpec((3, SEQ, 128), lambda u, i: (0, 0, 0))]
        args += [dp, tabs, tabs]
    elif mode == "na":
        b_spec = pl.BlockSpec((2, NA_ROWS, GRID_W, NA_ROWS * GRID_W), lambda u, i: (u, 0, 0, 0))
        in_specs += [row_spec, b_spec]
        args += [o, bias]
        out_specs.append(b_spec)
        out_shape.append(jax.ShapeDtypeStruct((8, NA_ROWS, GRID_W, NA_ROWS * GRID_W), F32))
    else:
        in_specs.append(row_spec)
        args.append(o)
    return pl.pallas_call(
        body, name=name, grid=(cfg["units"], nq), in_specs=in_specs, out_specs=out_specs, out_shape=out_shape,
        scratch_shapes=[pltpu.VMEM((lk, 128), F32), pltpu.VMEM((lk, 128), F32)],
        compiler_params=_params(("parallel", "arbitrary")))(*args)


def _na_geometry():
    qc = _iota((GRID_W, 128), 0)
    lane = _iota((GRID_W, 128), 1)
    kc = lane & 63
    c_start = jnp.clip(qc - 8, 0, GRID_W - 16)
    valid = jnp.logical_and(kc >= c_start, kc < c_start + 16)
    return lane, valid


def _na_bias(rpb_rows):
    def body(r_ref, o_ref, t_ref):
        lane, valid = _na_geometry()
        for dd in range(14):
            row_a = jnp.broadcast_to(r_ref[dd:dd + 1, :], (GRID_W, 128))
            row_b = jnp.broadcast_to(r_ref[dd + 1:dd + 2, :], (GRID_W, 128))
            both = jnp.where(lane < 64, row_a, pltpu.roll(row_b, 64, 1))
            t = pltpu.roll(both, 128 - 15, 1, stride=1, stride_axis=0)
            t_ref[dd] = jnp.where(valid, t, NEG)
        for off in range(NA_ROWS):
            for p in range(4):
                o_ref[off, :, p * 128:(p + 1) * 128] = t_ref[2 * p - off + 7]

    return pl.pallas_call(
        body, name="na_bias", grid=(8,),
        in_specs=[pl.BlockSpec((None, 16, 128), lambda h: (h, 0, 0))],
        out_specs=pl.BlockSpec((None, NA_ROWS, GRID_W, NA_ROWS * GRID_W), lambda h: (h, 0, 0, 0)),
        out_shape=jax.ShapeDtypeStruct((8, NA_ROWS, GRID_W, NA_ROWS * GRID_W), F32),
        scratch_shapes=[pltpu.VMEM((14, GRID_W, 128), F32)],
        compiler_params=_params(("parallel",)))(rpb_rows)


def _na_bias_bwd(dbias):
    def body(d_ref, o_ref):
        lane, valid = _na_geometry()
        reverse = (_iota((GRID_W, GRID_W), 0) + _iota((GRID_W, GRID_W), 1) == GRID_W - 1).astype(F32)
        o_ref[...] = jnp.zeros((16, 128), F32)
        for dd in range(14):
            t = jnp.zeros((GRID_W, 128), F32)
            for off in range(NA_ROWS):
                for p in range(4):
                    if 2 * p - off + 7 == dd:
                        t = t + d_ref[off, :, p * 128:(p + 1) * 128]
            t = jnp.dot(reverse, jnp.where(valid, t, 0.0), precision=lax.Precision.HIGHEST,
                        preferred_element_type=F32)
            t = pltpu.roll(t, 128 - (GRID_W - 16), 1, stride=1, stride_axis=0)
            o_ref[dd:dd + 1, :] = jnp.sum(t, axis=0, keepdims=True)

    return pl.pallas_call(
        body, name="na_bias_bwd", grid=(8,),
        in_specs=[pl.BlockSpec((None, NA_ROWS, GRID_W, NA_ROWS * GRID_W), lambda h: (h, 0, 0, 0))],
        out_specs=pl.BlockSpec((None, 16, 128), lambda h: (h, 0, 0)),
        out_shape=jax.ShapeDtypeStruct((8, 16, 128), F32),
        compiler_params=_params(("parallel",)))(dbias)


GATE_ROWS = 128


def _group_weights(l0, l1, l2):
    m = jnp.maximum(jnp.maximum(l0, l1), l2)
    e0, e1, e2 = jnp.exp(l0 - m), jnp.exp(l1 - m), jnp.exp(l2 - m)
    inv = 1.0 / (e0 + e1 + e2)
    return e0 * inv, e1 * inv, e2 * inv


def _gate_specs():
    r512 = pl.BlockSpec((GATE_ROWS, 512), lambda i: (i, 0))
    r1024 = pl.BlockSpec((GATE_ROWS, D_MODEL), lambda i: (i, 0))
    silu_cols = [pl.BlockSpec((GATE_ROWS, 512), functools.partial(lambda b, i: (i, b), 13 + b)) for b in range(3)]
    logit_cols = [pl.BlockSpec((GATE_ROWS, D_MODEL), functools.partial(lambda b, i: (i, b), 8 + b)) for b in range(3)]
    return r512, r1024, silu_cols, logit_cols


def _gate_fwd(o_grp, l_grp, out_b, out_c, parts, merge_bias, wts):
    r512, r1024, silu_cols, logit_cols = _gate_specs()

    def body(o0, o1, o2, l0, l1, l2, ob, oc, ga, gb, gc, la, lb, lc, mb, wa, wb, wc,
             oa_ref, ua, ub, uc, za, zb, zc, y_ref):
        w0, w1, w2 = _group_weights(l0[...], l1[...], l2[...])
        out_a = w0 * o0[...] + w1 * o1[...] + w2 * o2[...]
        oa_ref[...] = out_a
        y = jnp.zeros((GATE_ROWS, D_MODEL), F32)
        for b, (ov, g_ref, l_ref, w_ref, u_ref, z_ref) in enumerate(
                ((out_a, ga, la, wa, ua, za), (ob[...], gb, lb, wb, ub, zb), (oc[...], gc, lc, wc, uc, zc))):
            g = g_ref[...].astype(F32)
            u = (ov * (g * _sigmoid(g))).astype(BF16)
            u_ref[...] = u
            z = lax.dot_general(u, w_ref[...], NT, preferred_element_type=F32)
            z_ref[...] = z.astype(BF16)
            gate = _sigmoid(l_ref[...].astype(F32) + mb[b:b + 1, :])
            y = y + gate * z
        y_ref[...] = y.astype(BF16)

    full = lambda shape: pl.BlockSpec(shape, lambda i: (0,) * len(shape))
    in_specs = ([r512] * 8 + silu_cols + logit_cols
                + [full((3, D_MODEL))] + [full((D_MODEL, 512))] * 3)
    out_specs = [r512] * 4 + [r1024] * 4
    out_shape = ([jax.ShapeDtypeStruct((SEQ, 512), F32)] + [jax.ShapeDtypeStruct((SEQ, 512), BF16)] * 3
                 + [jax.ShapeDtypeStruct((SEQ, D_MODEL), BF16)] * 4)
    res = pl.pallas_call(
        body, name="gate_fwd", grid=(SEQ // GATE_ROWS,), in_specs=in_specs, out_specs=out_specs,
        out_shape=out_shape, compiler_params=_params(("parallel",)))(
            *o_grp, *l_grp, out_b, out_c, parts, parts, parts, parts, parts, parts, merge_bias, *wts)
    return res[0], res[1:4], res[4:7], res[7]


def _gate_bwd(dy, z, parts, merge_bias, outs, o_grp, l_grp, wts, head_sum):
    r512, r1024, silu_cols, logit_cols = _gate_specs()

    def body(dy_ref, za, zb, zc, la, lb, lc, mb, oa, ob, oc, ga, gb, gc, o0, o1, o2, l0, l1, l2, wa, wb, wc, hs_ref,
             dla, dlb, dlc, gmb, dza, dzb, dzc, dga, dgb, dgc, do0, do1, do2, dp0, dp1, dp2, dob, doc):
        dyv = dy_ref[...].astype(F32)
        rows = []
        dos = []
        for b, (z_ref, l_ref, ov_ref, g_ref, w_ref, dl_ref, dz_ref, dg_ref) in enumerate(
                ((za, la, oa, ga, wa, dla, dza, dga), (zb, lb, ob, gb, wb, dlb, dzb, dgb),
                 (zc, lc, oc, gc, wc, dlc, dzc, dgc))):
            gate = _sigmoid(l_ref[...].astype(F32) + mb[b:b + 1, :])
            dl = dyv * z_ref[...].astype(F32) * gate * (1.0 - gate)
            dl_ref[...] = dl.astype(BF16)
            rows.append(jnp.sum(dl, axis=0, keepdims=True))
            dz = (dyv * gate).astype(BF16)
            dz_ref[...] = dz
            du = jnp.dot(dz, w_ref[...], preferred_element_type=F32)
            g = g_ref[...].astype(F32)
            sg = _sigmoid(g)
            dos.append(du * (g * sg))
            dg_ref[...] = (du * ov_ref[...] * (sg * (1.0 + g * (1.0 - sg)))).astype(BF16)

        @pl.when(pl.program_id(0) == 0)
        def _():
            gmb[...] = jnp.zeros((3, D_MODEL), F32)

        for b in range(3):
            gmb[b:b + 1, :] += rows[b]
        dob[...] = dos[1].astype(BF16)
        doc[...] = dos[2].astype(BF16)
        doa = dos[0]
        row_term = jnp.dot(doa * oa[...], hs_ref[...], precision=lax.Precision.HIGHEST, preferred_element_type=F32)
        ws = _group_weights(l0[...], l1[...], l2[...])
        for wg, do_ref, dp_ref in zip(ws, (do0, do1, do2), (dp0, dp1, dp2)):
            do_ref[...] = wg * doa
            dp_ref[...] = wg * row_term

    full = lambda shape: pl.BlockSpec(shape, lambda i: (0,) * len(shape))
    acc = pl.BlockSpec((3, D_MODEL), lambda i: (0, 0))
    in_specs = ([r1024] * 4 + logit_cols + [full((3, D_MODEL))] + [r512] * 3 + silu_cols + [r512] * 6
                + [full((D_MODEL, 512))] * 3 + [full((512, 512))])
    out_specs = [r1024] * 3 + [acc] + [r1024] * 3 + [r512] * 11
    out_shape = ([jax.ShapeDtypeStruct((SEQ, D_MODEL), BF16)] * 3 + [jax.ShapeDtypeStruct((3, D_MODEL), F32)]
                 + [jax.ShapeDtypeStruct((SEQ, D_MODEL), BF16)] * 3 + [jax.ShapeDtypeStruct((SEQ, 512), BF16)] * 3
                 + [jax.ShapeDtypeStruct((SEQ, 512), F32)] * 6 + [jax.ShapeDtypeStruct((SEQ, 512), BF16)] * 2)
    res = pl.pallas_call(
        body, name="gate_bwd", grid=(SEQ // GATE_ROWS,), in_specs=in_specs, out_specs=out_specs,
        out_shape=out_shape, compiler_params=_params(("arbitrary",)))(
            dy, *z, parts, parts, parts, merge_bias, *outs, parts, parts, parts, *o_grp, *l_grp, *wts, head_sum)
    return res[0:3], res[3], res[4:7], res[7:10], res[10:13], res[13:16], res[16], res[17]


def _post(y2, x, target, gain):
    rows = 256

    def body(y_ref, x_ref, t_ref, g_ref, do_ref, dy_ref, l_ref, gg_ref):
        yv = y_ref[...]
        rstd = lax.rsqrt(jnp.mean(yv * yv, axis=1, keepdims=True) + EPS)
        yn = yv * rstd
        gv = g_ref[...]
        err = x_ref[...] + yn * gv - t_ref[...]
        dout = err * (1.0 / D_MODEL)
        do_ref[...] = dout
        dn = dout * gv
        dy_ref[...] = (rstd * (dn - yn * jnp.mean(dn * yn, axis=1, keepdims=True))).astype(BF16)

        @pl.when(pl.program_id(0) == 0)
        def _():
            l_ref[...] = jnp.zeros((1, D_MODEL), F32)
            gg_ref[...] = jnp.zeros((1, D_MODEL), F32)

        l_ref[...] += jnp.sum(err * err, axis=0, keepdims=True)
        gg_ref[...] += jnp.sum(dout * yn, axis=0, keepdims=True)

    row = pl.BlockSpec((rows, D_MODEL), lambda i: (i, 0))
    vec = pl.BlockSpec((1, D_MODEL), lambda i: (0, 0))
    return pl.pallas_call(
        body, name="post", grid=(SEQ // rows,), in_specs=[row, row, row, vec], out_specs=[row, row, vec, vec],
        out_shape=[jax.ShapeDtypeStruct((SEQ, D_MODEL), F32), jax.ShapeDtypeStruct((SEQ, D_MODEL), BF16),
                   jax.ShapeDtypeStruct((1, D_MODEL), F32), jax.ShapeDtypeStruct((1, D_MODEL), F32)],
        compiler_params=_params(("arbitrary",)))(y2, x, target, gain)


def _local_step(x, mem, target, pre_norm, mem_norm, post_norm, na_rpb, merge_bias, wt_in, w_kv, wt_a, wt_b, wt_c,
                w_out):
    tabs = _rope_tables()
    hs, hst = _prenorm_fold(x, pre_norm)
    parts = _in_proj(hs, wt_in, tabs)

    o_grp, l_grp = [], []
    for g, d in enumerate(DILATIONS):
        o, l = _attn_fwd("dil_fwd_%d" % g, "dil", parts, parts, parts, 12 * g, 12 * g + 4, 12 * g + 8, d=d)
        o_grp.append(o)
        l_grp.append(l)
    bias = _na_bias(jnp.pad(na_rpb, ((0, 0), (0, 1), (0, 128 - 31))))
    out_b, lse_b = _attn_fwd("na_fwd", "na", parts, parts, parts, 36, 40, 44, bias=bias)
    memn = _rmsnorm_fwd("memnorm", mem, mem_norm, MEM_LEN)
    kv_m = _mm_simple("mem_kv", memn, w_kv, NN, BF16, MEM_LEN, 512, D_MODEL)
    out_c, lse_c = _attn_fwd("mem_fwd", "mem", parts, kv_m, kv_m, 48, 0, 4)

    wts = (wt_a, wt_b, wt_c)
    out_a, u, z, y = _gate_fwd(o_grp, l_grp, out_b, out_c, parts, merge_bias, wts)
    y2 = _mm_simple("out_proj", y, w_out, NN, F32, 512, D_MODEL, D_MODEL)
    dout, dy2, err_sq, g_post = _post(y2, x, target, post_norm)
    loss = 0.5 * jnp.sum(err_sq) / D_MODEL

    dy = _mm_simple("out_proj_dx", dy2, w_out, NT, BF16, 512, D_MODEL, D_MODEL)
    g_w_out = _mm_simple("out_proj_dw", y, dy2, TN, BF16, D_MODEL, 512, 512)

    rr = _iota((512, 512), 0) // HEAD_DIM
    cc = _iota((512, 512), 1) // HEAD_DIM
    head_sum = (rr == cc).astype(F32)
    dlog, g_mb, dz, dg, do_grp, dp_grp, do_b, do_c = _gate_bwd(
        dy, z, parts, merge_bias, (out_a, out_b, out_c), o_grp, l_grp, wts, head_sum)
    g_wt = [_mm_simple("branch_dw_%d" % b, dz[b], u[b], TN, BF16, D_MODEL, 512, 512) for b in range(3)]

    dqkv = []
    for g, d in enumerate(DILATIONS):
        dq, dk, dv = _attn_bwd("dil_bwd_%d" % g, "dil", parts, parts, parts, 12 * g, 12 * g + 4, 12 * g + 8,
                               do_grp[g], l_grp[g], dp=dp_grp[g], d=d, tabs=tabs[g])
        dqkv += [dq, dk, dv]
    dq_b, dk_b, dv_b, dbias = _attn_bwd("na_bwd", "na", parts, parts, parts, 36, 40, 44, do_b, lse_b, o=out_b,
                                        bias=bias)
    g_rpb_t = _na_bias_bwd(dbias)
    g_rpb = g_rpb_t[:, :15, :31] + jnp.pad(g_rpb_t[:, :14, 64:95], ((0, 0), (1, 0), (0, 0)))
    dq_c, dk_m, dv_m = _attn_bwd("mem_bwd", "mem", parts, kv_m, kv_m, 48, 0, 4, do_c, lse_c, o=out_c)

    dkv = jnp.concatenate([dk_m, dv_m], axis=1).astype(BF16)
    g_w_kv = _mm_simple("mem_kv_dw", memn, dkv, TN, BF16, D_MODEL, 512, MEM_LEN)
    dmemn = _mm_simple("mem_kv_dx", dkv, w_kv, NT, F32, MEM_LEN, 512, D_MODEL)
    g_mem_norm = _memnorm_bwd(mem, dmemn)

    dparts = jnp.concatenate(dqkv + [dq_b, dk_b, dv_b, dq_c] + list(dg) + list(dlog), axis=1)
    g_wt_in = _in_proj_dw(dparts, hst)
    dh = _in_proj_dh(dparts, wt_in)
    grad_x, g_pre = _prenorm_bwd(x, pre_norm, dh, dout)

    grads = dict(wt_in=g_wt_in, w_kv=g_w_kv, wt_a=g_wt[0], wt_b=g_wt[1], wt_c=g_wt[2], w_out=g_w_out,
                 merge_bias=g_mb, pre_norm=g_pre, mem_norm=g_mem_norm, post_norm=g_post, na_rpb=g_rpb)
    return loss, grad_x, grads


ANY = pl.BlockSpec(memory_space=pl.ANY)


def _place():
    return lax.axis_index("x"), lax.axis_index("y"), lax.axis_index("c")


def _all_gather(shards):
    nt = len(shards)

    def body(*refs):
        srcs, outs = refs[:nt], refs[nt:2 * nt]
        send_sems, recv_sems, local_sems = refs[2 * nt:]
        x, y, c = _place()
        me, sibling = (x, y, c), (x, y, 1 - c)
        chips = [(1 - x, y), (x, 1 - y), (1 - x, 1 - y)]

        def block(t, px, py, pc):
            return outs[t].at[4 * px + 2 * py + pc]

        def copy(k, t, blk, to, src=None):
            return pltpu.make_async_remote_copy(
                src_ref=block(t, *blk) if src is None else src, dst_ref=block(t, *blk),
                send_sem=send_sems.at[k * nt + t], recv_sem=recv_sems.at[k * nt + t],
                device_id=to, device_id_type=MESH_ID)

        mine = [pltpu.make_async_copy(srcs[t], block(t, *me), local_sems.at[t]) for t in range(nt)]
        for cp in mine:
            cp.start()
        first = [copy(0, t, me, sibling, src=srcs[t]) for t in range(nt)]
        for j, chip in enumerate(chips):
            first += [copy(1 + j, t, me, (*chip, c), src=srcs[t]) for t in range(nt)]
        for cp in first:
            cp.start()
        passed = []
        for j, chip in enumerate(chips):
            for t in range(nt):
                copy(1 + j, t, (*chip, c), me).wait_recv()
                fwd = copy(4 + j, t, (*chip, c), sibling)
                fwd.start()
                passed.append(fwd)
        for t in range(nt):
            copy(0, t, sibling, me).wait_recv()
        for j, chip in enumerate(chips):
            for t in range(nt):
                copy(4 + j, t, (*chip, 1 - c), me).wait_recv()
        for cp in first + passed:
            cp.wait_send()
        for cp in mine:
            cp.wait()

    return pl.pallas_call(
        body, name="all_gather", in_specs=[ANY] * nt, out_specs=[ANY] * nt,
        out_shape=[jax.ShapeDtypeStruct((N_DEV,) + s.shape, s.dtype) for s in shards],
        scratch_shapes=[pltpu.SemaphoreType.DMA((7 * nt,)), pltpu.SemaphoreType.DMA((7 * nt,)),
                        pltpu.SemaphoreType.DMA((nt,))])(*shards)


def _exchange_sibling(terms):
    nt = len(terms)

    def body(*refs):
        srcs, outs = refs[:nt], refs[nt:2 * nt]
        send_sems, recv_sems = refs[2 * nt:]
        x, y, c = _place()
        copies = []
        for q in range(4):
            for t in range(nt):
                copies.append(pltpu.make_async_remote_copy(
                    src_ref=srcs[t].at[2 * q + 1 - c], dst_ref=outs[t].at[q],
                    send_sem=send_sems.at[q * nt + t], recv_sem=recv_sems.at[q * nt + t],
                    device_id=(x, y, 1 - c), device_id_type=MESH_ID))
        for cp in copies:
            cp.start()
        for cp in copies:
            cp.wait()

    return pl.pallas_call(
        body, name="exchange_sibling", in_specs=[ANY] * nt, out_specs=[ANY] * nt,
        out_shape=[jax.ShapeDtypeStruct((4,) + s.shape[1:], s.dtype) for s in terms],
        scratch_shapes=[pltpu.SemaphoreType.DMA((4 * nt,)), pltpu.SemaphoreType.DMA((4 * nt,))])(*terms)


def _exchange_chips(sums):
    nt = len(sums)

    def body(*refs):
        srcs, outs = refs[:nt], refs[nt:2 * nt]
        send_sems, recv_sems = refs[2 * nt:]
        x, y, c = _place()
        chips = [(1 - x, y), (x, 1 - y), (1 - x, 1 - y)]
        copies = []
        for s, (tx, ty) in enumerate(chips):
            for t in range(nt):
                copies.append(pltpu.make_async_remote_copy(
                    src_ref=srcs[t].at[2 * tx + ty], dst_ref=outs[t].at[s],
                    send_sem=send_sems.at[s * nt + t], recv_sem=recv_sems.at[s * nt + t],
                    device_id=(tx, ty, c), device_id_type=MESH_ID))
        for cp in copies:
            cp.start()
        for cp in copies:
            cp.wait()

    return pl.pallas_call(
        body, name="exchange_chips", in_specs=[ANY] * nt, out_specs=[ANY] * nt,
        out_shape=[jax.ShapeDtypeStruct((3,) + s.shape[1:], s.dtype) for s in sums],
        scratch_shapes=[pltpu.SemaphoreType.DMA((3 * nt,)), pltpu.SemaphoreType.DMA((3 * nt,))])(*sums)


def _gather_small(block):
    def body(src, out, send_sems, recv_sems, local_sem):
        x, y, c = _place()
        me = 4 * x + 2 * y + c
        mine = pltpu.make_async_copy(src, out.at[me], local_sem)
        mine.start()
        copies = []
        for mask in range(1, 8):
            fx, fy, fc = (mask >> 2) & 1, (mask >> 1) & 1, mask & 1
            to = (jnp.where(fx, 1 - x, x), jnp.where(fy, 1 - y, y), jnp.where(fc, 1 - c, c))
            copies.append(pltpu.make_async_remote_copy(
                src_ref=src, dst_ref=out.at[me], send_sem=send_sems.at[mask - 1], recv_sem=recv_sems.at[mask - 1],
                device_id=to, device_id_type=MESH_ID))
        for cp in copies:
            cp.start()
        for cp in copies:
            cp.wait()
        mine.wait()

    return pl.pallas_call(
        body, name="gather_small", in_specs=[ANY], out_specs=ANY,
        out_shape=jax.ShapeDtypeStruct((N_DEV,) + block.shape, block.dtype),
        scratch_shapes=[pltpu.SemaphoreType.DMA((7,)), pltpu.SemaphoreType.DMA((7,)), pltpu.SemaphoreType.DMA])(block)


def _add_sibling(name, term, recv, rows):
    _, r, w = term.shape
    cidx = lax.axis_index("c").astype(jnp.int32).reshape(1)

    def body(c_ref, a_ref, b_ref, o_ref):
        o_ref[...] = (a_ref[...].astype(F32) + b_ref[...].astype(F32)).astype(o_ref.dtype)

    grid_spec = pltpu.PrefetchScalarGridSpec(
        num_scalar_prefetch=1, grid=(4, r // rows),
        in_specs=[pl.BlockSpec((None, rows, w), lambda q, i, c_ref: (2 * q + c_ref[0], i, 0)),
                  pl.BlockSpec((None, rows, w), lambda q, i, c_ref: (q, i, 0))],
        out_specs=pl.BlockSpec((None, rows, w), lambda q, i, c_ref: (q, i, 0)))
    return pl.pallas_call(
        body, name=name, grid_spec=grid_spec, out_shape=jax.ShapeDtypeStruct((4, r, w), term.dtype),
        compiler_params=_params(("parallel", "parallel")))(cidx, term, recv)


def _add_chips(name, sums, recv, rows):
    _, r, w = sums.shape
    qidx = (2 * lax.axis_index("x") + lax.axis_index("y")).astype(jnp.int32).reshape(1)

    def body(q_ref, a_ref, b_ref, o_ref):
        o_ref[...] = ((a_ref[...].astype(F32) + b_ref[0].astype(F32))
                      + (b_ref[1].astype(F32) + b_ref[2].astype(F32)))

    grid_spec = pltpu.PrefetchScalarGridSpec(
        num_scalar_prefetch=1, grid=(r // rows,),
        in_specs=[pl.BlockSpec((None, rows, w), lambda i, q_ref: (q_ref[0], i, 0)),
                  pl.BlockSpec((3, rows, w), lambda i, q_ref: (0, i, 0))],
        out_specs=pl.BlockSpec((rows, w), lambda i, q_ref: (i, 0)))
    return pl.pallas_call(
        body, name=name, grid_spec=grid_spec, out_shape=jax.ShapeDtypeStruct((r, w), F32),
        compiler_params=_params(("parallel",)))(qidx, sums, recv)


def _reduce_scatter(names, terms):
    def rows_of(a):
        return SHARD_IN // 4 if a.shape[1] == SHARD_IN else a.shape[1]

    recv1 = _exchange_sibling(terms)
    sums = [_add_sibling("add_sibling_" + n, t, r, rows_of(t)) for n, t, r in zip(names, terms, recv1)]
    recv2 = _exchange_chips(sums)
    return [_add_chips("add_chips_" + n, s, r, rows_of(s)) for n, s, r in zip(names, sums, recv2)]


def _adamw(name, w, g, m, v, rows=None):
    r, c = w.shape
    rows = r if rows is None else rows
    c1 = 1.0 - ADAM_B1 ** ADAM_STEP
    c2 = 1.0 - ADAM_B2 ** ADAM_STEP

    def body(w_ref, g_ref, m_ref, v_ref, d_ref, nm_ref, nv_ref):
        gv = g_ref[...]
        nm = ADAM_B1 * m_ref[...] + (1.0 - ADAM_B1) * gv
        nv = ADAM_B2 * v_ref[...] + (1.0 - ADAM_B2) * (gv * gv)
        nm_ref[...] = nm
        nv_ref[...] = nv
        d_ref[...] = -ADAM_LR * ((nm / c1) / (jnp.sqrt(nv / c2) + ADAM_EPS) + ADAM_WD * w_ref[...])

    spec = pl.BlockSpec((rows, c), lambda i: (i, 0))
    return pl.pallas_call(
        body, name=name, grid=(r // rows,), in_specs=[spec] * 4, out_specs=[spec] * 3,
        out_shape=[jax.ShapeDtypeStruct((r, c), F32)] * 3, compiler_params=_params(("parallel",)))(w, g, m, v)


def _sum_devices(gathered):
    def body(g_ref, o_ref):
        acc = g_ref[0]
        for j in range(1, N_DEV):
            acc = acc + g_ref[j]
        o_ref[...] = acc

    return pl.pallas_call(
        body, name="sum_devices", out_shape=jax.ShapeDtypeStruct(gathered.shape[1:], F32),
        compiler_params=_params())(gathered)


def _rows128(a, rows):
    flat = a.reshape(-1)
    return jnp.pad(flat, (0, rows * 128 - flat.shape[0])).reshape(rows, 128)


def kernel(x, mem, pre_norm, w_in, merge_bias, na_rpb, mem_norm, w_mem_kv, w_branch_a, w_branch_b, w_branch_c, w_out, post_norm, loss_target, m_pre_norm, m_w_in, m_merge_bias, m_na_rpb, m_mem_norm, m_w_mem_kv, m_w_branch_a, m_w_branch_b, m_w_branch_c, m_w_out, m_post_norm, v_pre_norm, v_w_in, v_merge_bias, v_na_rpb, v_mem_norm, v_w_mem_kv, v_w_branch_a, v_w_branch_b, v_w_branch_c, v_w_out, v_post_norm):
    wt_in_s = w_in[0].T.astype(BF16)
    rows_s = jnp.concatenate([w_mem_kv[0], w_out[0]], axis=0).astype(BF16)
    cols_s = jnp.concatenate([w_branch_a[0].T, w_branch_b[0].T, w_branch_c[0].T], axis=0).astype(BF16)
    mb_s = jnp.pad(merge_bias[0], ((0, 5), (0, 0)))
    g_in, g_rows, g_cols, g_mb = _all_gather([wt_in_s, rows_s, cols_s, mb_s])
    wt_in = g_in.reshape(N_IN, D_MODEL)
    w_kv = g_rows[:, :128].reshape(D_MODEL, D_MODEL)
    w_o = g_rows[:, 128:].reshape(D_MODEL, D_MODEL)
    wt_a = g_cols[:, 0:128].reshape(D_MODEL, 512)
    wt_b = g_cols[:, 128:256].reshape(D_MODEL, 512)
    wt_c = g_cols[:, 256:384].reshape(D_MODEL, 512)
    mb_full = g_mb[:, :3].transpose(1, 0, 2).reshape(3, D_MODEL)

    loss_term, grad_x, grads = _local_step(
        x[0], mem[0], loss_target[0], pre_norm, mem_norm, post_norm, na_rpb[0], mb_full,
        wt_in, w_kv, wt_a, wt_b, wt_c, w_o)
    gmb_t = jnp.pad(grads["merge_bias"].reshape(3, N_DEV, 128).transpose(1, 0, 2), ((0, 0), (0, 5), (0, 0)))
    names = ["w_in", "w_kv", "w_out", "a", "b", "c", "mb"]
    terms = [grads["wt_in"].reshape(N_DEV, SHARD_IN, D_MODEL), grads["w_kv"].reshape(N_DEV, 128, D_MODEL),
             grads["w_out"].reshape(N_DEV, 128, D_MODEL), grads["wt_a"].reshape(N_DEV, 128, 512),
             grads["wt_b"].reshape(N_DEV, 128, 512), grads["wt_c"].reshape(N_DEV, 128, 512), gmb_t]
    gt_in, g_kv, g_out, gt_a, gt_b, gt_c, g_mb8 = _reduce_scatter(names, terms)

    small = jnp.concatenate([_rows128(grads["pre_norm"], 8), _rows128(grads["mem_norm"], 8),
                             _rows128(grads["post_norm"], 8), _rows128(grads["na_rpb"], 32),
                             _rows128(loss_term, 8)], axis=0)
    total = _sum_devices(_gather_small(small))
    loss = total[56, 0]
    g_pre = total[0:8].reshape(1, D_MODEL)
    g_memn = total[8:16].reshape(1, D_MODEL)
    g_post = total[16:24].reshape(1, D_MODEL)
    g_rpb = total[24:56].reshape(-1)[:8 * 15 * 31].reshape(1, 8, 15, 31)

    grad = {
        "pre_norm": g_pre, "w_in": gt_in.T[None], "merge_bias": g_mb8[:3][None], "na_rpb": g_rpb,
        "mem_norm": g_memn, "w_mem_kv": g_kv[None], "w_branch_a": gt_a.T[None], "w_branch_b": gt_b.T[None],
        "w_branch_c": gt_c.T[None], "w_out": g_out[None], "post_norm": g_post}
    weights = {
        "pre_norm": (pre_norm, m_pre_norm, v_pre_norm), "w_in": (w_in, m_w_in, v_w_in),
        "merge_bias": (merge_bias, m_merge_bias, v_merge_bias), "na_rpb": (na_rpb, m_na_rpb, v_na_rpb),
        "mem_norm": (mem_norm, m_mem_norm, v_mem_norm), "w_mem_kv": (w_mem_kv, m_w_mem_kv, v_w_mem_kv),
        "w_branch_a": (w_branch_a, m_w_branch_a, v_w_branch_a), "w_branch_b": (w_branch_b, m_w_branch_b, v_w_branch_b),
        "w_branch_c": (w_branch_c, m_w_branch_c, v_w_branch_c), "w_out": (w_out, m_w_out, v_w_out),
        "post_norm": (post_norm, m_post_norm, v_post_norm)}
    order = ["pre_norm", "w_in", "merge_bias", "na_rpb", "mem_norm", "w_mem_kv", "w_branch_a", "w_branch_b",
             "w_branch_c", "w_out", "post_norm"]
    delta, new_m, new_v = {}, {}, {}
    for n in order:
        w, m, v = weights[n]
        shape = w.shape
        two_d = (-1, shape[-1])
        rows = 256 if n == "w_in" else None
        dl, nm, nv = _adamw("adamw_" + n, w.reshape(two_d), grad[n].reshape(two_d), m.reshape(two_d),
                            v.reshape(two_d), rows)
        delta[n], new_m[n], new_v[n] = dl.reshape(shape), nm.reshape(shape), nv.reshape(shape)

    return (loss, grad_x[None], *[grad[n] for n in order], *[delta[n] for n in order],
            *[new_m[n] for n in order], *[new_v[n] for n in order])
```

```python
import functools

import numpy as np
import jax
import jax.numpy as jnp
from jax import lax
from jax.experimental import pallas as pl
from jax.experimental.pallas import tpu as pltpu

F32 = jnp.float32
BF16 = jnp.bfloat16

SEQ = 2048
D_MODEL = 1024
N_IN = 11264
N_DEV = 8
SHARD_IN = N_IN // N_DEV
HEAD_DIM = 64
GRID_W = 64
NA_ROWS = 8
MEM_LEN = 256
DILATIONS = (1, 4, 16)
REACH = 64
ROPE_THETA = 500000.0
ROPE_DIM = 16
EPS = 1e-6
NEG = -1e30
ADAM_LR = 0.001
ADAM_B1 = 0.9
ADAM_B2 = 0.999
ADAM_EPS = 1e-08
ADAM_WD = 0.01
ADAM_STEP = 10

VMEM_LIMIT_BYTES = 56 * 1024 * 1024
MESH_ID = pl.DeviceIdType.MESH

NN = (((1,), (0,)), ((), ()))
NT = (((1,), (1,)), ((), ()))
TN = (((0,), (0,)), ((), ()))


def _params(sem=None):
    return pltpu.CompilerParams(dimension_semantics=sem, vmem_limit_bytes=VMEM_LIMIT_BYTES)


def _iota(shape, dim):
    return lax.broadcasted_iota(jnp.int32, shape, dim)


def _sigmoid(x):
    return 1.0 / (1.0 + jnp.exp(-x))


def _fold(a, d):
    if d == 1:
        return a
    n, w = a.shape
    return a.reshape(n // d, d, w).transpose(1, 0, 2).reshape(n, w)


def _unfold(a, d):
    if d == 1:
        return a
    n, w = a.shape
    return a.reshape(d, n // d, w).transpose(1, 0, 2).reshape(n, w)


def _rope_tables():
    half = ROPE_DIM // 2
    inv = (ROPE_THETA ** (-np.arange(half, dtype=np.float64) * 2.0 / ROPE_DIM)).astype(np.float32)
    pos = np.arange(SEQ, dtype=np.float32)
    ang = pos[:, None] * inv[None, :]
    cos, sin = np.cos(ang), np.sin(ang)
    zeros = np.zeros_like(cos)
    rest = HEAD_DIM - ROPE_DIM
    c64 = np.concatenate([cos, cos, np.ones((SEQ, rest), np.float32)], axis=1)
    s1 = np.concatenate([zeros, sin, np.zeros((SEQ, rest), np.float32)], axis=1)
    s2 = np.concatenate([-sin, zeros, np.zeros((SEQ, rest), np.float32)], axis=1)

    def fold(t, d):
        return t.reshape(SEQ // d, d, t.shape[1]).transpose(1, 0, 2).reshape(SEQ, t.shape[1])

    tabs = [np.stack([np.tile(fold(t, d), (1, 2)) for t in (c64, s1, s2)], axis=0) for d in DILATIONS]
    return jnp.asarray(np.stack(tabs, axis=0), dtype=F32)


def _rope(a, c, s1, s2):
    return a * c + pltpu.roll(a, 8, 1) * s1 + pltpu.roll(a, 120, 1) * s2


def _rope_t(a, c, s1, s2):
    return a * c + pltpu.roll(a * s1, 120, 1) + pltpu.roll(a * s2, 8, 1)


def _perm_of_block(j):
    return jnp.where(j < 3, 0, jnp.where(j < 6, 1, jnp.where(j < 9, 2, 0)))


def _mm(name, a, b, out_shape, out_dtype, grid, a_spec, b_spec, o_spec, acc_shape, dims, k_axis, nk):
    def body(a_ref, b_ref, o_ref, acc_ref):
        k = pl.program_id(k_axis)

        @pl.when(k == 0)
        def _():
            acc_ref[...] = jnp.zeros(acc_shape, F32)

        acc_ref[...] += lax.dot_general(a_ref[...], b_ref[...], dims, preferred_element_type=F32)

        @pl.when(k == nk - 1)
        def _():
            o_ref[...] = acc_ref[...].astype(out_dtype)

    sem = tuple("arbitrary" if ax == k_axis else "parallel" for ax in range(len(grid)))
    return pl.pallas_call(
        body, name=name, grid=grid, in_specs=[a_spec, b_spec], out_specs=o_spec,
        out_shape=jax.ShapeDtypeStruct(out_shape, out_dtype),
        scratch_shapes=[pltpu.VMEM(acc_shape, F32)], compiler_params=_params(sem))(a, b)


def _mm_simple(name, a, b, dims, out_dtype, tm, tn, tk):
    if dims is NN:
        m, kk = a.shape
        n = b.shape[1]
        a_spec = pl.BlockSpec((tm, tk), lambda i, j, k: (i, k))
        b_spec = pl.BlockSpec((tk, tn), lambda i, j, k: (k, j))
    elif dims is NT:
        m, kk = a.shape
        n = b.shape[0]
        a_spec = pl.BlockSpec((tm, tk), lambda i, j, k: (i, k))
        b_spec = pl.BlockSpec((tn, tk), lambda i, j, k: (j, k))
    else:
        kk, m = a.shape
        n = b.shape[1]
        a_spec = pl.BlockSpec((tk, tm), lambda i, j, k: (k, i))
        b_spec = pl.BlockSpec((tk, tn), lambda i, j, k: (k, j))
    grid = (m // tm, n // tn, kk // tk)
    o_spec = pl.BlockSpec((tm, tn), lambda i, j, k: (i, j))
    return _mm(name, a, b, (m, n), out_dtype, grid, a_spec, b_spec, o_spec, (tm, tn), dims, 2, kk // tk)


def _rmsnorm_fwd(name, x, gain, rows):
    n, d = x.shape

    def body(x_ref, g_ref, o_ref):
        xv = x_ref[...]
        rstd = lax.rsqrt(jnp.mean(xv * xv, axis=1, keepdims=True) + EPS)
        o_ref[...] = (xv * rstd * g_ref[...]).astype(BF16)

    return pl.pallas_call(
        body, name=name, grid=(n // rows,),
        in_specs=[pl.BlockSpec((rows, d), lambda i: (i, 0)), pl.BlockSpec((1, d), lambda i: (0, 0))],
        out_specs=pl.BlockSpec((rows, d), lambda i: (i, 0)),
        out_shape=jax.ShapeDtypeStruct((n, d), BF16), compiler_params=_params(("parallel",)))(x, gain)


def _folded_rows(first, rows, d):
    if d == 1:
        return pl.ds(pl.multiple_of(first, rows), rows)
    mlen = SEQ // d
    return pl.ds((first % mlen) * d + first // mlen, rows, stride=d)


def _prenorm_fold(x, gain):
    rows = 128

    nchunk = D_MODEL // 128

    def body(*refs):
        x_refs, g_ref, hs_ref, hst_ref = refs[:nchunk], refs[nchunk], refs[nchunk + 1], refs[nchunk + 2]
        first = pl.program_id(0) * rows
        for p, d in enumerate(DILATIONS):
            idx = _folded_rows(first, rows, d)
            xv = jnp.concatenate([r[idx, :] for r in x_refs], axis=1)
            rstd = lax.rsqrt(jnp.mean(xv * xv, axis=1, keepdims=True) + EPS)
            h = xv * rstd * g_ref[...]
            hs_ref[p] = h.astype(BF16)
            hst_ref[p] = h.T.astype(BF16)

    x_specs = [pl.BlockSpec((SEQ, 128), functools.partial(lambda c, i: (0, c), c)) for c in range(nchunk)]
    return pl.pallas_call(
        body, name="prenorm", grid=(SEQ // rows,),
        in_specs=x_specs + [pl.BlockSpec((1, D_MODEL), lambda i: (0, 0))],
        out_specs=[pl.BlockSpec((3, rows, D_MODEL), lambda i: (0, i, 0)),
                   pl.BlockSpec((3, D_MODEL, rows), lambda i: (0, 0, i))],
        out_shape=[jax.ShapeDtypeStruct((3, SEQ, D_MODEL), BF16), jax.ShapeDtypeStruct((3, D_MODEL, SEQ), BF16)],
        compiler_params=_params(("parallel",)))(*([x] * nchunk), gain)


def _prenorm_bwd(x, gain, dh, dout):
    rows = 256

    def body(x_ref, g_ref, a_ref, do_ref, dx_ref, gg_ref):
        xv = x_ref[...]
        rstd = lax.rsqrt(jnp.mean(xv * xv, axis=1, keepdims=True) + EPS)
        xn = xv * rstd
        dh = jnp.concatenate([a_ref[c] for c in range(D_MODEL // 128)], axis=1)
        gdh = dh * g_ref[...]
        dx_ref[...] = rstd * (gdh - xn * jnp.mean(gdh * xn, axis=1, keepdims=True)) + do_ref[...]

        @pl.when(pl.program_id(0) == 0)
        def _():
            gg_ref[...] = jnp.zeros((1, D_MODEL), F32)

        gg_ref[...] += jnp.sum(dh * xn, axis=0, keepdims=True)

    row = pl.BlockSpec((rows, D_MODEL), lambda i: (i, 0))
    vec = pl.BlockSpec((1, D_MODEL), lambda i: (0, 0))
    return pl.pallas_call(
        body, name="prenorm_bwd", grid=(SEQ // rows,),
        in_specs=[row, vec, pl.BlockSpec((D_MODEL // 128, rows, 128), lambda i: (0, i, 0)), row], out_specs=[row, vec],
        out_shape=[jax.ShapeDtypeStruct((SEQ, D_MODEL), F32), jax.ShapeDtypeStruct((1, D_MODEL), F32)],
        compiler_params=_params(("arbitrary",)))(x, gain, dh, dout)


def _memnorm_bwd(mem, dmemn):
    def body(m_ref, d_ref, gg_ref):
        mv = m_ref[...]
        rstd = lax.rsqrt(jnp.mean(mv * mv, axis=1, keepdims=True) + EPS)
        gg_ref[...] = jnp.sum(d_ref[...] * mv * rstd, axis=0, keepdims=True)

    return pl.pallas_call(
        body, name="memnorm_bwd", out_shape=jax.ShapeDtypeStruct((1, D_MODEL), F32),
        compiler_params=_params())(mem, dmemn)


def _dep_operand(dep):
    return ([], []) if dep is None else ([pl.BlockSpec(memory_space=pl.ANY)], [dep])


def _in_proj(hs, wt, tabs, dep=None):
    tm, tn = 512, 512
    dep_specs, dep_args = _dep_operand(dep)

    def body(h_ref, w_ref, t_ref, *rest):
        o_ref = rest[-1]
        j = pl.program_id(0)
        is_rope = jnp.logical_and(j < 9, j % 3 != 2)
        row_slices = [slice(r * tm, (r + 1) * tm) for r in range(SEQ // tm)]

        def product(rs):
            return lax.dot_general(h_ref[rs, :], w_ref[...], NT, preferred_element_type=F32)

        @pl.when(is_rope)
        def _():
            for rs in row_slices:
                acc = product(rs)
                c, s1, s2 = t_ref[0, rs, :], t_ref[1, rs, :], t_ref[2, rs, :]
                for q in range(tn // 128):
                    a = acc[:, q * 128:(q + 1) * 128]
                    o_ref[rs, q * 128:(q + 1) * 128] = _rope(a, c, s1, s2).astype(BF16)

        @pl.when(jnp.logical_not(is_rope))
        def _():
            for rs in row_slices:
                o_ref[rs, :] = product(rs).astype(BF16)

    return pl.pallas_call(
        body, name="in_proj", grid=(N_IN // tn,),
        in_specs=[pl.BlockSpec((None, SEQ, D_MODEL), lambda j: (_perm_of_block(j), 0, 0)),
                  pl.BlockSpec((tn, D_MODEL), lambda j: (j, 0)),
                  pl.BlockSpec((None, 3, SEQ, 128), lambda j: (_perm_of_block(j), 0, 0, 0))] + dep_specs,
        out_specs=pl.BlockSpec((SEQ, tn), lambda j: (0, j)),
        out_shape=jax.ShapeDtypeStruct((SEQ, N_IN), BF16),
        compiler_params=_params(("parallel",)))(hs, wt, tabs, *dep_args)


def _in_proj_dw(dparts, hst):
    tn = 512

    def body(h_ref, d_ref, o_ref):
        acc = jnp.dot(h_ref[...], d_ref[...], preferred_element_type=F32)
        o_ref[...] = acc.T.astype(BF16)

    return pl.pallas_call(
        body, name="in_proj_dw", grid=(N_IN // tn,),
        in_specs=[pl.BlockSpec((None, D_MODEL, SEQ), lambda j: (_perm_of_block(j), 0, 0)),
                  pl.BlockSpec((SEQ, tn), lambda j: (0, j))],
        out_specs=pl.BlockSpec((tn, D_MODEL), lambda j: (j, 0)),
        out_shape=jax.ShapeDtypeStruct((N_IN, D_MODEL), BF16), compiler_params=_params(("parallel",)))(hst, dparts)


def _in_proj_dh(dparts, wt, dep=None):
    tk = 512
    nblk = N_IN // tk
    nchunk = D_MODEL // 128

    def col(s):
        return jnp.where(s < 3, s, jnp.where(s < 16, s + 6, s - 13))

    dep_specs, dep_args = _dep_operand(dep)

    def body(d_ref, w_ref, *rest):
        o_ref, acc_ref = rest[-2:]
        s = pl.program_id(0)
        row_slices = [slice(r * 512, (r + 1) * 512) for r in range(SEQ // 512)]

        def product(rs):
            return jnp.dot(d_ref[rs, :], w_ref[...], preferred_element_type=F32)

        def accumulate(cond, to_out, init):
            @pl.when(cond)
            def _():
                for rs in row_slices:
                    prod = product(rs)
                    if not to_out:
                        if init:
                            acc_ref[rs, :] = prod
                        else:
                            acc_ref[rs, :] += prod
                        continue
                    for c in range(nchunk):
                        if init:
                            o_ref[c, rs, :] = prod[:, c * 128:(c + 1) * 128]
                        else:
                            o_ref[c, rs, :] += prod[:, c * 128:(c + 1) * 128]

        accumulate(s == 0, True, True)
        accumulate(jnp.logical_and(s > 0, s < 16), True, False)
        accumulate(jnp.logical_or(s == 16, s == 19), False, True)
        accumulate(jnp.logical_and(s > 16, s != 19), False, False)
        for last, d in ((18, 4), (21, 16)):
            @pl.when(s == last)
            def _():
                mlen = SEQ // d
                for r in range(d):
                    for c in range(nchunk):
                        o_ref[c, pl.ds(r, mlen, stride=d), :] += acc_ref[r * mlen:(r + 1) * mlen,
                                                                         c * 128:(c + 1) * 128]

    return pl.pallas_call(
        body, name="in_proj_dh", grid=(nblk,),
        in_specs=[pl.BlockSpec((SEQ, tk), lambda s: (0, col(s))),
                  pl.BlockSpec((tk, D_MODEL), lambda s: (col(s), 0))] + dep_specs,
        out_specs=pl.BlockSpec((nchunk, SEQ, 128), lambda s: (0, 0, 0)),
        out_shape=jax.ShapeDtypeStruct((nchunk, SEQ, 128), F32),
        scratch_shapes=[pltpu.VMEM((SEQ, D_MODEL), F32)],
        compiler_params=_params(("arbitrary",)))(dparts, wt, *dep_args)


def _head_lanes(lanes, hh):
    return lanes >= 64 if hh == 1 else lanes < 64


def _head_rows(x, lanes, hh, pair):
    if not pair:
        return jnp.max(x, axis=1, keepdims=True)
    return jnp.max(jnp.where(_head_lanes(lanes, hh), x, -jnp.inf), axis=1, keepdims=True)


def _mask_head(x, lanes, hh, pair):
    if not pair:
        return x
    return jnp.where(_head_lanes(lanes, hh), x.astype(F32), 0.0).astype(BF16)


def _merge_heads(parts, lanes, pair):
    if not pair:
        return parts[0]
    return jnp.where(lanes < 64, parts[0], parts[1])


def _window(mode, qi, tq, mlen, tk):
    if mode == "dil":
        q0 = qi * tq
        seg = (q0 // mlen) * mlen
        ks = jnp.clip(q0 - REACH, seg, seg + mlen - tk)
        return pl.multiple_of(ks, 64)
    if mode == "na":
        r_start = jnp.clip(qi - NA_ROWS // 2, 0, SEQ // GRID_W - NA_ROWS)
        return pl.multiple_of(r_start * GRID_W, 64)
    return 0


def _scores(mode, qh, k, scale, qi, tq, tk, ks, bias_ref, hh):
    s = lax.dot_general(qh, k, NT, preferred_element_type=F32) * scale
    if mode == "dil":
        qpos = qi * tq + _iota((tq, tk), 0)
        kpos = ks + _iota((tq, tk), 1)
        s = jnp.where(jnp.abs(qpos - kpos) <= REACH, s, NEG)
    elif mode == "na":
        off = qi - jnp.clip(qi - NA_ROWS // 2, 0, SEQ // GRID_W - NA_ROWS)
        s = s + bias_ref[hh, off]
    return s


def _attn_cfg(mode, d):
    if mode == "dil":
        mlen = SEQ // d
        return dict(pair=True, tq=128, tk=min(256, mlen), mlen=mlen, lk=SEQ, scale=HEAD_DIM ** -0.5, units=4,
                    nsub=ATTN_SUBTILES)
    if mode == "na":
        return dict(pair=True, tq=GRID_W, tk=NA_ROWS * GRID_W, mlen=SEQ, lk=SEQ, scale=HEAD_DIM ** -0.5, units=4,
                    nsub=ATTN_SUBTILES)
    return dict(pair=False, tq=128, tk=MEM_LEN, mlen=SEQ, lk=MEM_LEN, scale=128 ** -0.5, units=4,
                nsub=ATTN_SUBTILES)


ATTN_SUBTILES = 4


def _attn_fwd(name, mode, q_arr, k_arr, v_arr, qcol, kcol, vcol, d=1, bias=None):
    cfg = _attn_cfg(mode, d)
    pair, tq, tk, mlen, lk, scale = cfg["pair"], cfg["tq"], cfg["tk"], cfg["mlen"], cfg["lk"], cfg["scale"]
    nh = 2 if pair else 1
    nsub = cfg["nsub"]
    rows = nsub * tq

    def body(*refs):
        if mode == "na":
            q_ref, k_ref, v_ref, bias_ref, o_ref, l_ref = refs
        else:
            q_ref, k_ref, v_ref, o_ref, l_ref = refs
            bias_ref = None
        lanes = _iota((tq, 128), 1)
        chains = [(sub, hh) for sub in range(nsub) for hh in range(nh)]
        qis = [pl.program_id(1) * nsub + sub for sub in range(nsub)]
        kss = [_window(mode, qi, tq, mlen, tk) for qi in qis]
        vs = [v_ref[pl.ds(ks, tk), :] for ks in kss]
        ss = []
        for sub, hh in chains:
            q = q_ref[sub * tq:(sub + 1) * tq, :]
            k = k_ref[pl.ds(kss[sub], tk), :]
            ss.append(_scores(mode, _mask_head(q, lanes, hh, pair), k, scale, qis[sub], tq, tk, kss[sub], bias_ref, hh))
        ms = [jnp.max(s, axis=1, keepdims=True) for s in ss]
        ps = [jnp.exp(s - m) for s, m in zip(ss, ms)]
        ls = [jnp.sum(p, axis=1, keepdims=True) for p in ps]
        os_ = [jnp.dot(p.astype(BF16), vs[sub], preferred_element_type=F32) for p, (sub, hh) in zip(ps, chains)]
        for sub in range(nsub):
            sel = [i for i, (s_, hh) in enumerate(chains) if s_ == sub]
            outs = [os_[i] / ls[i] for i in sel]
            lses = [jnp.broadcast_to(ms[i] + jnp.log(ls[i]), (tq, 128)) for i in sel]
            dst = _folded_rows(qis[sub] * tq, tq, d) if mode == "dil" else slice(sub * tq, (sub + 1) * tq)
            o_ref[dst, :] = _merge_heads(outs, lanes, pair)
            l_ref[dst, :] = _merge_heads(lses, lanes, pair)

    in_specs = [pl.BlockSpec((rows, 128), lambda u, i: (i, qcol + u)),
                pl.BlockSpec((lk, 128), lambda u, i: (0, kcol + u)),
                pl.BlockSpec((lk, 128), lambda u, i: (0, vcol + u))]
    args = [q_arr, k_arr, v_arr]
    if mode == "na":
        in_specs.append(pl.BlockSpec((2, NA_ROWS, GRID_W, NA_ROWS * GRID_W), lambda u, i: (u, 0, 0, 0)))
        args.append(bias)
    if mode == "dil":
        out_spec = pl.BlockSpec((SEQ, 128), lambda u, i: (0, u))
    else:
        out_spec = pl.BlockSpec((rows, 128), lambda u, i: (i, u))
    return pl.pallas_call(
        body, name=name, grid=(cfg["units"], SEQ // rows), in_specs=in_specs, out_specs=[out_spec, out_spec],
        out_shape=[jax.ShapeDtypeStruct((SEQ, 512), F32), jax.ShapeDtypeStruct((SEQ, 512), F32)],
        compiler_params=_params(("parallel", "arbitrary")))(*args)


def _attn_bwd(name, mode, q_arr, k_arr, v_arr, qcol, kcol, vcol, do, lse, dp=None, o=None, d=1, bias=None,
              tabs=None):
    cfg = _attn_cfg(mode, d)
    pair, tq, tk, mlen, lk, scale = cfg["pair"], cfg["tq"], cfg["tk"], cfg["mlen"], cfg["lk"], cfg["scale"]
    nh = 2 if pair else 1
    nsub = cfg["nsub"]
    rows = nsub * tq
    nq = SEQ // rows
    kv_dtype = F32 if mode == "mem" else BF16

    def body(*refs):
        refs = list(refs)
        q_ref, k_ref, v_ref, do_ref, l_ref = refs[:5]
        rest = refs[5:]
        bias_ref = tq_ref = tk_ref = db_ref = None
        if mode == "dil":
            dp_ref, tq_ref, tk_ref, dq_ref, dk_ref, dv_ref, dk_acc, dv_acc = rest
        elif mode == "na":
            o_ref, bias_ref, dq_ref, dk_ref, dv_ref, db_ref, dk_acc, dv_acc = rest
        else:
            o_ref, dq_ref, dk_ref, dv_ref, dk_acc, dv_acc = rest
        step = pl.program_id(1)

        @pl.when(step == 0)
        def _():
            dk_acc[...] = jnp.zeros((lk, 128), F32)
            dv_acc[...] = jnp.zeros((lk, 128), F32)
            if mode == "na":
                db_ref[...] = jnp.zeros(db_ref.shape, F32)

        lanes = _iota((tq, 128), 1)
        lanes_k = _iota((tk, 128), 1)
        chains = [(sub, hh) for sub in range(nsub) for hh in range(nh)]
        qis = [step * nsub + sub for sub in range(nsub)]
        sls = [slice(sub * tq, (sub + 1) * tq) for sub in range(nsub)]
        kss = [_window(mode, qi, tq, mlen, tk) for qi in qis]
        qs = [q_ref[sl, :] for sl in sls]
        ks_ = [k_ref[pl.ds(ks, tk), :] for ks in kss]
        vs = [v_ref[pl.ds(ks, tk), :] for ks in kss]
        dovs, lsevs, dpvs = [], [], []
        for sub in range(nsub):
            if mode == "dil":
                src = _folded_rows(qis[sub] * tq, tq, d)
                dovs.append(do_ref[src, :].astype(BF16))
                lsevs.append(l_ref[src, :])
                dpvs.append(dp_ref[src, :])
            else:
                dovs.append(do_ref[sls[sub], :])
                lsevs.append(l_ref[sls[sub], :])
                dpvs.append(dovs[sub].astype(F32) * o_ref[sls[sub], :])
        ss = [_scores(mode, _mask_head(qs[sub], lanes, hh, pair), ks_[sub], scale, qis[sub], tq, tk, kss[sub],
                      bias_ref, hh) for sub, hh in chains]
        dpms = [lax.dot_general(_mask_head(dovs[sub], lanes, hh, pair), vs[sub], NT, preferred_element_type=F32)
                for sub, hh in chains]
        ps = [jnp.exp(s - _head_rows(lsevs[sub], lanes, hh, pair)) for s, (sub, hh) in zip(ss, chains)]
        dphs = []
        for sub, hh in chains:
            if mode == "dil":
                dphs.append(_head_rows(dpvs[sub], lanes, hh, pair))
            elif pair:
                dphs.append(jnp.sum(jnp.where(_head_lanes(lanes, hh), dpvs[sub], 0.0), axis=1, keepdims=True))
            else:
                dphs.append(jnp.sum(dpvs[sub], axis=1, keepdims=True))
        dss = [p * (dpm - dph) for p, dpm, dph in zip(ps, dpms, dphs)]
        if mode == "na":
            for ds, (sub, hh) in zip(dss, chains):
                off = qis[sub] - jnp.clip(qis[sub] - NA_ROWS // 2, 0, SEQ // GRID_W - NA_ROWS)
                db_ref[hh, off] += ds
        dsbs = [ds.astype(BF16) for ds in dss]
        dvs = [lax.dot_general(p.astype(BF16), dovs[sub], TN, preferred_element_type=F32)
               for p, (sub, hh) in zip(ps, chains)]
        dqs = [jnp.dot(dsb, ks_[sub], preferred_element_type=F32) * scale for dsb, (sub, hh) in zip(dsbs, chains)]
        dks = [lax.dot_general(dsb, qs[sub], TN, preferred_element_type=F32) * scale
               for dsb, (sub, hh) in zip(dsbs, chains)]
        for sub in range(nsub):
            sel = [i for i, (s_, hh) in enumerate(chains) if s_ == sub]
            sl = sls[sub]
            dq = _merge_heads([dqs[i] for i in sel], lanes, pair)
            if mode == "dil":
                dq = _rope_t(dq, tq_ref[0, sl, :], tq_ref[1, sl, :], tq_ref[2, sl, :])
            dq_ref[sl, :] = dq.astype(BF16)
            dk_acc[pl.ds(kss[sub], tk), :] += _merge_heads([dks[i] for i in sel], lanes_k, pair)
            dv_acc[pl.ds(kss[sub], tk), :] += _merge_heads([dvs[i] for i in sel], lanes_k, pair)

        @pl.when(step == nq - 1)
        def _():
            dkv = dk_acc[...]
            if mode == "dil":
                dkv = _rope_t(dkv, tk_ref[0], tk_ref[1], tk_ref[2])
            dk_ref[...] = dkv.astype(kv_dtype)
            dv_ref[...] = dv_acc[...].astype(kv_dtype)

    q_spec = pl.BlockSpec((rows, 128), lambda u, i: (i, qcol + u))
    row_spec = pl.BlockSpec((rows, 128), lambda u, i: (i, u))
    kv_out = pl.BlockSpec((lk, 128), lambda u, i: (0, u))
    whole = pl.BlockSpec((SEQ, 128), lambda u, i: (0, u))
    nat_spec = whole if mode == "dil" else row_spec
    in_specs = [q_spec,
                pl.BlockSpec((lk, 128), lambda u, i: (0, kcol + u)),
                pl.BlockSpec((lk, 128), lambda u, i: (0, vcol + u)),
                nat_spec, nat_spec]
    args = [q_arr, k_arr, v_arr, do, lse]
    out_specs = [row_spec, kv_out, kv_out]
    out_shape = [jax.ShapeDtypeStruct((SEQ, 512), BF16), jax.ShapeDtypeStruct((lk, 512), kv_dtype),
                 jax.ShapeDtypeStruct((lk, 512), kv_dtype)]
    if mode == "dil":
        in_specs += [whole, pl.BlockSpec((3, rows, 128), lambda u, i: (0, i, 0)),
                     pl.BlockSpec((3, SEQ, 128), lambda u, i: (0, 0, 0))]
        args += [dp, tabs, tabs]
    elif mode == "na":
        b_spec = pl.BlockSpec((2, NA_ROWS, GRID_W, NA_ROWS * GRID_W), lambda u, i: (u, 0, 0, 0))
        in_specs += [row_spec, b_spec]
        args += [o, bias]
        out_specs.append(b_spec)
        out_shape.append(jax.ShapeDtypeStruct((8, NA_ROWS, GRID_W, NA_ROWS * GRID_W), F32))
    else:
        in_specs.append(row_spec)
        args.append(o)
    return pl.pallas_call(
        body, name=name, grid=(cfg["units"], nq), in_specs=in_specs, out_specs=out_specs, out_shape=out_shape,
        scratch_shapes=[pltpu.VMEM((lk, 128), F32), pltpu.VMEM((lk, 128), F32)],
        compiler_params=_params(("parallel", "arbitrary")))(*args)


def _na_geometry():
    qc = _iota((GRID_W, 128), 0)
    lane = _iota((GRID_W, 128), 1)
    kc = lane & 63
    c_start = jnp.clip(qc - 8, 0, GRID_W - 16)
    valid = jnp.logical_and(kc >= c_start, kc < c_start + 16)
    return lane, valid


def _na_bias(rpb_rows):
    def body(r_ref, o_ref, t_ref):
        lane, valid = _na_geometry()
        for dd in range(14):
            row_a = jnp.broadcast_to(r_ref[dd:dd + 1, :], (GRID_W, 128))
            row_b = jnp.broadcast_to(r_ref[dd + 1:dd + 2, :], (GRID_W, 128))
            both = jnp.where(lane < 64, row_a, pltpu.roll(row_b, 64, 1))
            t = pltpu.roll(both, 128 - 15, 1, stride=1, stride_axis=0)
            t_ref[dd] = jnp.where(valid, t, NEG)
        for off in range(NA_ROWS):
            for p in range(4):
                o_ref[off, :, p * 128:(p + 1) * 128] = t_ref[2 * p - off + 7]

    return pl.pallas_call(
        body, name="na_bias", grid=(8,),
        in_specs=[pl.BlockSpec((None, 16, 128), lambda h: (h, 0, 0))],
        out_specs=pl.BlockSpec((None, NA_ROWS, GRID_W, NA_ROWS * GRID_W), lambda h: (h, 0, 0, 0)),
        out_shape=jax.ShapeDtypeStruct((8, NA_ROWS, GRID_W, NA_ROWS * GRID_W), F32),
        scratch_shapes=[pltpu.VMEM((14, GRID_W, 128), F32)],
        compiler_params=_params(("parallel",)))(rpb_rows)


def _na_bias_bwd(dbias):
    def body(d_ref, o_ref):
        lane, valid = _na_geometry()
        reverse = (_iota((GRID_W, GRID_W), 0) + _iota((GRID_W, GRID_W), 1) == GRID_W - 1).astype(F32)
        o_ref[...] = jnp.zeros((16, 128), F32)
        for dd in range(14):
            t = jnp.zeros((GRID_W, 128), F32)
            for off in range(NA_ROWS):
                for p in range(4):
                    if 2 * p - off + 7 == dd:
                        t = t + d_ref[off, :, p * 128:(p + 1) * 128]
            t = jnp.dot(reverse, jnp.where(valid, t, 0.0), precision=lax.Precision.HIGHEST,
                        preferred_element_type=F32)
            t = pltpu.roll(t, 128 - (GRID_W - 16), 1, stride=1, stride_axis=0)
            o_ref[dd:dd + 1, :] = jnp.sum(t, axis=0, keepdims=True)

    return pl.pallas_call(
        body, name="na_bias_bwd", grid=(8,),
        in_specs=[pl.BlockSpec((None, NA_ROWS, GRID_W, NA_ROWS * GRID_W), lambda h: (h, 0, 0, 0))],
        out_specs=pl.BlockSpec((None, 16, 128), lambda h: (h, 0, 0)),
        out_shape=jax.ShapeDtypeStruct((8, 16, 128), F32),
        compiler_params=_params(("parallel",)))(dbias)


GATE_ROWS = 128


def _group_weights(l0, l1, l2):
    m = jnp.maximum(jnp.maximum(l0, l1), l2)
    e0, e1, e2 = jnp.exp(l0 - m), jnp.exp(l1 - m), jnp.exp(l2 - m)
    inv = 1.0 / (e0 + e1 + e2)
    return e0 * inv, e1 * inv, e2 * inv


def _gate_specs():
    r512 = pl.BlockSpec((GATE_ROWS, 512), lambda i: (i, 0))
    r1024 = pl.BlockSpec((GATE_ROWS, D_MODEL), lambda i: (i, 0))
    silu_cols = [pl.BlockSpec((GATE_ROWS, 512), functools.partial(lambda b, i: (i, b), 13 + b)) for b in range(3)]
    logit_cols = [pl.BlockSpec((GATE_ROWS, D_MODEL), functools.partial(lambda b, i: (i, b), 8 + b)) for b in range(3)]
    return r512, r1024, silu_cols, logit_cols


def _gate_fwd(o_grp, l_grp, out_b, out_c, parts, merge_bias, wts):
    r512, r1024, silu_cols, logit_cols = _gate_specs()

    def body(o0, o1, o2, l0, l1, l2, ob, oc, ga, gb, gc, la, lb, lc, mb, wa, wb, wc,
             oa_ref, ua, ub, uc, za, zb, zc, y_ref):
        w0, w1, w2 = _group_weights(l0[...], l1[...], l2[...])
        out_a = w0 * o0[...] + w1 * o1[...] + w2 * o2[...]
        oa_ref[...] = out_a
        y = jnp.zeros((GATE_ROWS, D_MODEL), F32)
        for b, (ov, g_ref, l_ref, w_ref, u_ref, z_ref) in enumerate(
                ((out_a, ga, la, wa, ua, za), (ob[...], gb, lb, wb, ub, zb), (oc[...], gc, lc, wc, uc, zc))):
            g = g_ref[...].astype(F32)
            u = (ov * (g * _sigmoid(g))).astype(BF16)
            u_ref[...] = u
            z = lax.dot_general(u, w_ref[...], NT, preferred_element_type=F32)
            z_ref[...] = z.astype(BF16)
            gate = _sigmoid(l_ref[...].astype(F32) + mb[b:b + 1, :])
            y = y + gate * z
        y_ref[...] = y.astype(BF16)

    full = lambda shape: pl.BlockSpec(shape, lambda i: (0,) * len(shape))
    in_specs = ([r512] * 8 + silu_cols + logit_cols
                + [full((3, D_MODEL))] + [full((D_MODEL, 512))] * 3)
    out_specs = [r512] * 4 + [r1024] * 4
    out_shape = ([jax.ShapeDtypeStruct((SEQ, 512), F32)] + [jax.ShapeDtypeStruct((SEQ, 512), BF16)] * 3
                 + [jax.ShapeDtypeStruct((SEQ, D_MODEL), BF16)] * 4)
    res = pl.pallas_call(
        body, name="gate_fwd", grid=(SEQ // GATE_ROWS,), in_specs=in_specs, out_specs=out_specs,
        out_shape=out_shape, compiler_params=_params(("parallel",)))(
            *o_grp, *l_grp, out_b, out_c, parts, parts, parts, parts, parts, parts, merge_bias, *wts)
    return res[0], res[1:4], res[4:7], res[7]


def _gate_bwd(dy, z, parts, merge_bias, outs, o_grp, l_grp, wts, head_sum):
    r512, r1024, silu_cols, logit_cols = _gate_specs()

    def body(dy_ref, za, zb, zc, la, lb, lc, mb, oa, ob, oc, ga, gb, gc, o0, o1, o2, l0, l1, l2, wa, wb, wc, hs_ref,
             dla, dlb, dlc, gmb, dza, dzb, dzc, dga, dgb, dgc, do0, do1, do2, dp0, dp1, dp2, dob, doc):
        dyv = dy_ref[...].astype(F32)
        rows = []
        dos = []
        for b, (z_ref, l_ref, ov_ref, g_ref, w_ref, dl_ref, dz_ref, dg_ref) in enumerate(
                ((za, la, oa, ga, wa, dla, dza, dga), (zb, lb, ob, gb, wb, dlb, dzb, dgb),
                 (zc, lc, oc, gc, wc, dlc, dzc, dgc))):
            gate = _sigmoid(l_ref[...].astype(F32) + mb[b:b + 1, :])
            dl = dyv * z_ref[...].astype(F32) * gate * (1.0 - gate)
            dl_ref[...] = dl.astype(BF16)
            rows.append(jnp.sum(dl, axis=0, keepdims=True))
            dz = (dyv * gate).astype(BF16)
            dz_ref[...] = dz
            du = jnp.dot(dz, w_ref[...], preferred_element_type=F32)
            g = g_ref[...].astype(F32)
            sg = _sigmoid(g)
            dos.append(du * (g * sg))
            dg_ref[...] = (du * ov_ref[...] * (sg * (1.0 + g * (1.0 - sg)))).astype(BF16)

        @pl.when(pl.program_id(0) == 0)
        def _():
            gmb[...] = jnp.zeros((3, D_MODEL), F32)

        for b in range(3):
            gmb[b:b + 1, :] += rows[b]
        dob[...] = dos[1].astype(BF16)
        doc[...] = dos[2].astype(BF16)
        doa = dos[0]
        row_term = jnp.dot(doa * oa[...], hs_ref[...], precision=lax.Precision.HIGHEST, preferred_element_type=F32)
        ws = _group_weights(l0[...], l1[...], l2[...])
        for wg, do_ref, dp_ref in zip(ws, (do0, do1, do2), (dp0, dp1, dp2)):
            do_ref[...] = wg * doa
            dp_ref[...] = wg * row_term

    full = lambda shape: pl.BlockSpec(shape, lambda i: (0,) * len(shape))
    acc = pl.BlockSpec((3, D_MODEL), lambda i: (0, 0))
    in_specs = ([r1024] * 4 + logit_cols + [full((3, D_MODEL))] + [r512] * 3 + silu_cols + [r512] * 6
                + [full((D_MODEL, 512))] * 3 + [full((512, 512))])
    out_specs = [r1024] * 3 + [acc] + [r1024] * 3 + [r512] * 11
    out_shape = ([jax.ShapeDtypeStruct((SEQ, D_MODEL), BF16)] * 3 + [jax.ShapeDtypeStruct((3, D_MODEL), F32)]
                 + [jax.ShapeDtypeStruct((SEQ, D_MODEL), BF16)] * 3 + [jax.ShapeDtypeStruct((SEQ, 512), BF16)] * 3
                 + [jax.ShapeDtypeStruct((SEQ, 512), F32)] * 6 + [jax.ShapeDtypeStruct((SEQ, 512), BF16)] * 2)
    res = pl.pallas_call(
        body, name="gate_bwd", grid=(SEQ // GATE_ROWS,), in_specs=in_specs, out_specs=out_specs,
        out_shape=out_shape, compiler_params=_params(("arbitrary",)))(
            dy, *z, parts, parts, parts, merge_bias, *outs, parts, parts, parts, *o_grp, *l_grp, *wts, head_sum)
    return res[0:3], res[3], res[4:7], res[7:10], res[10:13], res[13:16], res[16], res[17]


def _post(y2, x, target, gain):
    rows = 256

    def body(y_ref, x_ref, t_ref, g_ref, do_ref, dy_ref, l_ref, gg_ref):
        yv = y_ref[...]
        rstd = lax.rsqrt(jnp.mean(yv * yv, axis=1, keepdims=True) + EPS)
        yn = yv * rstd
        gv = g_ref[...]
        err = x_ref[...] + yn * gv - t_ref[...]
        dout = err * (1.0 / D_MODEL)
        do_ref[...] = dout
        dn = dout * gv
        dy_ref[...] = (rstd * (dn - yn * jnp.mean(dn * yn, axis=1, keepdims=True))).astype(BF16)

        @pl.when(pl.program_id(0) == 0)
        def _():
            l_ref[...] = jnp.zeros((1, D_MODEL), F32)
            gg_ref[...] = jnp.zeros((1, D_MODEL), F32)

        l_ref[...] += jnp.sum(err * err, axis=0, keepdims=True)
        gg_ref[...] += jnp.sum(dout * yn, axis=0, keepdims=True)

    row = pl.BlockSpec((rows, D_MODEL), lambda i: (i, 0))
    vec = pl.BlockSpec((1, D_MODEL), lambda i: (0, 0))
    return pl.pallas_call(
        body, name="post", grid=(SEQ // rows,), in_specs=[row, row, row, vec], out_specs=[row, row, vec, vec],
        out_shape=[jax.ShapeDtypeStruct((SEQ, D_MODEL), F32), jax.ShapeDtypeStruct((SEQ, D_MODEL), BF16),
                   jax.ShapeDtypeStruct((1, D_MODEL), F32), jax.ShapeDtypeStruct((1, D_MODEL), F32)],
        compiler_params=_params(("arbitrary",)))(y2, x, target, gain)


def _local_step(x, mem, target, pre_norm, mem_norm, post_norm, na_rpb, wt_in, late_weights, dep_in=None,
                reduce_start=None):
    tabs = _rope_tables()
    hs, hst = _prenorm_fold(x, pre_norm)
    parts = _in_proj(hs, wt_in, tabs, dep_in)

    o_grp, l_grp = [], []
    for g, d in enumerate(DILATIONS):
        o, l = _attn_fwd("dil_fwd_%d" % g, "dil", parts, parts, parts, 12 * g, 12 * g + 4, 12 * g + 8, d=d)
        o_grp.append(o)
        l_grp.append(l)
    bias = _na_bias(jnp.pad(na_rpb, ((0, 0), (0, 1), (0, 128 - 31))))
    out_b, lse_b = _attn_fwd("na_fwd", "na", parts, parts, parts, 36, 40, 44, bias=bias)
    merge_bias, w_kv, wt_a, wt_b, wt_c, w_out = late_weights(out_b)
    memn = _rmsnorm_fwd("memnorm", mem, mem_norm, MEM_LEN)
    kv_m = _mm_simple("mem_kv", memn, w_kv, NN, BF16, MEM_LEN, 512, D_MODEL)
    out_c, lse_c = _attn_fwd("mem_fwd", "mem", parts, kv_m, kv_m, 48, 0, 4)

    wts = (wt_a, wt_b, wt_c)
    out_a, u, z, y = _gate_fwd(o_grp, l_grp, out_b, out_c, parts, merge_bias, wts)
    y2 = _mm_simple("out_proj", y, w_out, NN, F32, 512, D_MODEL, D_MODEL)
    dout, dy2, err_sq, g_post = _post(y2, x, target, post_norm)
    loss = 0.5 * jnp.sum(err_sq) / D_MODEL

    dy = _mm_simple("out_proj_dx", dy2, w_out, NT, BF16, 512, D_MODEL, D_MODEL)
    g_w_out = _mm_simple("out_proj_dw", y, dy2, TN, BF16, D_MODEL, 512, 512)

    rr = _iota((512, 512), 0) // HEAD_DIM
    cc = _iota((512, 512), 1) // HEAD_DIM
    head_sum = (rr == cc).astype(F32)
    dlog, g_mb, dz, dg, do_grp, dp_grp, do_b, do_c = _gate_bwd(
        dy, z, parts, merge_bias, (out_a, out_b, out_c), o_grp, l_grp, wts, head_sum)
    g_wt = [_mm_simple("branch_dw_%d" % b, dz[b], u[b], TN, BF16, D_MODEL, 512, 512) for b in range(3)]

    dqkv = []
    for g, d in enumerate(DILATIONS):
        dq, dk, dv = _attn_bwd("dil_bwd_%d" % g, "dil", parts, parts, parts, 12 * g, 12 * g + 4, 12 * g + 8,
                               do_grp[g], l_grp[g], dp=dp_grp[g], d=d, tabs=tabs[g])
        dqkv += [dq, dk, dv]
    dq_b, dk_b, dv_b, dbias = _attn_bwd("na_bwd", "na", parts, parts, parts, 36, 40, 44, do_b, lse_b, o=out_b,
                                        bias=bias)
    g_rpb_t = _na_bias_bwd(dbias)
    g_rpb = g_rpb_t[:, :15, :31] + jnp.pad(g_rpb_t[:, :14, 64:95], ((0, 0), (1, 0), (0, 0)))
    dq_c, dk_m, dv_m = _attn_bwd("mem_bwd", "mem", parts, kv_m, kv_m, 48, 0, 4, do_c, lse_c, o=out_c)

    dkv = jnp.concatenate([dk_m, dv_m], axis=1).astype(BF16)
    g_w_kv = _mm_simple("mem_kv_dw", memn, dkv, TN, BF16, D_MODEL, 512, MEM_LEN)
    dmemn = _mm_simple("mem_kv_dx", dkv, w_kv, NT, F32, MEM_LEN, 512, D_MODEL)
    g_mem_norm = _memnorm_bwd(mem, dmemn)

    dparts = jnp.concatenate(dqkv + [dq_b, dk_b, dv_b, dq_c] + list(dg) + list(dlog), axis=1)
    g_wt_in = _in_proj_dw(dparts, hst)
    grads = dict(wt_in=g_wt_in, w_kv=g_w_kv, wt_a=g_wt[0], wt_b=g_wt[1], wt_c=g_wt[2], w_out=g_w_out,
                 merge_bias=g_mb, mem_norm=g_mem_norm, post_norm=g_post, na_rpb=g_rpb)
    dep = reduce_start(grads) if reduce_start is not None else None
    dh = _in_proj_dh(dparts, wt_in, dep)
    grad_x, grads["pre_norm"] = _prenorm_bwd(x, pre_norm, dh, dout)
    return loss, grad_x, grads


ANY = pl.BlockSpec(memory_space=pl.ANY)


def _place():
    return lax.axis_index("x"), lax.axis_index("y"), lax.axis_index("c")


def _all_gather(shards):
    nt = len(shards)

    def body(*refs):
        srcs, outs = refs[:nt], refs[nt:2 * nt]
        send_sems, recv_sems, local_sems = refs[2 * nt:]
        x, y, c = _place()
        me, sibling = (x, y, c), (x, y, 1 - c)
        chips = [(1 - x, y), (x, 1 - y), (1 - x, 1 - y)]

        def block(t, px, py, pc):
            return outs[t].at[4 * px + 2 * py + pc]

        def copy(k, t, blk, to, src=None):
            return pltpu.make_async_remote_copy(
                src_ref=block(t, *blk) if src is None else src, dst_ref=block(t, *blk),
                send_sem=send_sems.at[k * nt + t], recv_sem=recv_sems.at[k * nt + t],
                device_id=to, device_id_type=MESH_ID)

        mine = [pltpu.make_async_copy(srcs[t], block(t, *me), local_sems.at[t]) for t in range(nt)]
        for cp in mine:
            cp.start()
        first = [copy(0, t, me, sibling, src=srcs[t]) for t in range(nt)]
        for j, chip in enumerate(chips):
            first += [copy(1 + j, t, me, (*chip, c), src=srcs[t]) for t in range(nt)]
        for cp in first:
            cp.start()
        passed = []
        for j, chip in enumerate(chips):
            for t in range(nt):
                copy(1 + j, t, (*chip, c), me).wait_recv()
                fwd = copy(4 + j, t, (*chip, c), sibling)
                fwd.start()
                passed.append(fwd)
        for t in range(nt):
            copy(0, t, sibling, me).wait_recv()
        for j, chip in enumerate(chips):
            for t in range(nt):
                copy(4 + j, t, (*chip, 1 - c), me).wait_recv()
        for cp in first + passed:
            cp.wait_send()
        for cp in mine:
            cp.wait()

    return pl.pallas_call(
        body, name="all_gather", in_specs=[ANY] * nt, out_specs=[ANY] * nt,
        out_shape=[jax.ShapeDtypeStruct((N_DEV,) + s.shape, s.dtype) for s in shards],
        scratch_shapes=[pltpu.SemaphoreType.DMA((7 * nt,)), pltpu.SemaphoreType.DMA((7 * nt,)),
                        pltpu.SemaphoreType.DMA((nt,))])(*shards)


def _exchange_sibling(terms):
    nt = len(terms)

    def body(*refs):
        srcs, outs = refs[:nt], refs[nt:2 * nt]
        send_sems, recv_sems = refs[2 * nt:]
        x, y, c = _place()
        copies = []
        for q in range(4):
            for t in range(nt):
                copies.append(pltpu.make_async_remote_copy(
                    src_ref=srcs[t].at[2 * q + 1 - c], dst_ref=outs[t].at[q],
                    send_sem=send_sems.at[q * nt + t], recv_sem=recv_sems.at[q * nt + t],
                    device_id=(x, y, 1 - c), device_id_type=MESH_ID))
        for cp in copies:
            cp.start()
        for cp in copies:
            cp.wait()

    return pl.pallas_call(
        body, name="exchange_sibling", in_specs=[ANY] * nt, out_specs=[ANY] * nt,
        out_shape=[jax.ShapeDtypeStruct((4,) + s.shape[1:], s.dtype) for s in terms],
        scratch_shapes=[pltpu.SemaphoreType.DMA((4 * nt,)), pltpu.SemaphoreType.DMA((4 * nt,))])(*terms)


def _gather_small(block):
    def body(src, out, send_sems, recv_sems, local_sem):
        x, y, c = _place()
        me = 4 * x + 2 * y + c
        mine = pltpu.make_async_copy(src, out.at[me], local_sem)
        mine.start()
        copies = []
        for mask in range(1, 8):
            fx, fy, fc = (mask >> 2) & 1, (mask >> 1) & 1, mask & 1
            to = (jnp.where(fx, 1 - x, x), jnp.where(fy, 1 - y, y), jnp.where(fc, 1 - c, c))
            copies.append(pltpu.make_async_remote_copy(
                src_ref=src, dst_ref=out.at[me], send_sem=send_sems.at[mask - 1], recv_sem=recv_sems.at[mask - 1],
                device_id=to, device_id_type=MESH_ID))
        for cp in copies:
            cp.start()
        for cp in copies:
            cp.wait()
        mine.wait()

    return pl.pallas_call(
        body, name="gather_small", in_specs=[ANY], out_specs=ANY,
        out_shape=jax.ShapeDtypeStruct((N_DEV,) + block.shape, block.dtype),
        scratch_shapes=[pltpu.SemaphoreType.DMA((7,)), pltpu.SemaphoreType.DMA((7,)), pltpu.SemaphoreType.DMA])(block)


HBM = pl.BlockSpec(memory_space=pltpu.HBM)
SEM = pl.BlockSpec(memory_space=pltpu.SEMAPHORE)
DATAFLOW = pltpu.SideEffectType.DATAFLOW_SIDE_EFFECTING


def _split_copies(kind, srcs, lands, send_sems, recv_sems):
    nt = len(srcs)
    x, y, c = _place()
    copies = []
    if kind == "gather":
        me = 4 * x + 2 * y + c
        for mask in range(1, 8):
            fx, fy, fc = (mask >> 2) & 1, (mask >> 1) & 1, mask & 1
            to = (1 - x if fx else x, 1 - y if fy else y, 1 - c if fc else c)
            for t in range(nt):
                k = (mask - 1) * nt + t
                copies.append(pltpu.make_async_remote_copy(
                    src_ref=srcs[t], dst_ref=lands[t].at[me], send_sem=send_sems.at[k], recv_sem=recv_sems.at[k],
                    device_id=to, device_id_type=MESH_ID))
    else:
        for s, (tx, ty) in enumerate([(1 - x, y), (x, 1 - y), (1 - x, 1 - y)]):
            for t in range(nt):
                k = s * nt + t
                copies.append(pltpu.make_async_remote_copy(
                    src_ref=srcs[t].at[2 * tx + ty], dst_ref=lands[t].at[s], send_sem=send_sems.at[k],
                    recv_sem=recv_sems.at[k], device_id=(tx, ty, c), device_id_type=MESH_ID))
    return copies


def _split_count(kind, nt):
    return (7 if kind == "gather" else 3) * nt


def _exchange_start(name, kind, srcs, land_shapes):
    nt = len(srcs)
    n = _split_count(kind, nt)

    def body(*refs):
        src_refs, land_refs = refs[:nt], refs[nt:2 * nt]
        send_sems, recv_sems = refs[2 * nt], refs[2 * nt + 1]
        token = refs[-1]
        for cp in _split_copies(kind, src_refs, land_refs, send_sems, recv_sems):
            cp.start()
        token[...] = jnp.zeros_like(token)

    lands = [pltpu.with_memory_space_constraint(lax.empty(s.shape, s.dtype), pltpu.HBM) for s in land_shapes]
    res = pl.pallas_call(
        body, name=name,
        out_shape=(pltpu.SemaphoreType.DMA((n,)), pltpu.SemaphoreType.DMA((n,)),
                   *[pltpu.HBM(s.shape, s.dtype) for s in srcs], *[pltpu.HBM(s.shape, s.dtype) for s in land_shapes],
                   jax.ShapeDtypeStruct((8, 128), F32)),
        in_specs=[HBM] * (2 * nt),
        out_specs=(SEM, SEM, *([HBM] * (2 * nt)), pl.BlockSpec(memory_space=pltpu.VMEM)),
        input_output_aliases={i: 2 + i for i in range(2 * nt)},
        compiler_params=pltpu.CompilerParams(has_side_effects=DATAFLOW))(
            *[pltpu.with_memory_space_constraint(s, pltpu.HBM) for s in srcs], *lands)
    return res[0], res[1], list(res[2:2 + nt]), list(res[2 + nt:2 + 2 * nt]), res[-1]


def _exchange_wait(name, kind, send_sems, recv_sems, srcs, lands, after):
    nt = len(srcs)

    def body(*refs):
        src_refs, land_refs = refs[:nt], refs[nt:2 * nt]
        s_sems, r_sems = refs[2 * nt], refs[2 * nt + 1]
        for cp in _split_copies(kind, src_refs, land_refs, s_sems, r_sems):
            cp.wait_send()
            cp.wait_recv()

    res = pl.pallas_call(
        body, name=name,
        out_shape=tuple(pltpu.HBM(s.shape, s.dtype) for s in list(srcs) + list(lands)),
        in_specs=[HBM] * (2 * nt) + [SEM, SEM, pl.BlockSpec(memory_space=pl.ANY)],
        out_specs=tuple([HBM] * (2 * nt)),
        input_output_aliases={i: i for i in range(2 * nt)},
        compiler_params=pltpu.CompilerParams(has_side_effects=DATAFLOW))(
            *srcs, *lands, send_sems, recv_sems, after)
    return list(res[:nt]), list(res[nt:])


def _add_sibling(name, term, recv, rows):
    _, r, w = term.shape
    cidx = lax.axis_index("c").astype(jnp.int32).reshape(1)

    def body(c_ref, a_ref, b_ref, o_ref):
        o_ref[...] = (a_ref[...].astype(F32) + b_ref[...].astype(F32)).astype(o_ref.dtype)

    grid_spec = pltpu.PrefetchScalarGridSpec(
        num_scalar_prefetch=1, grid=(4, r // rows),
        in_specs=[pl.BlockSpec((None, rows, w), lambda q, i, c_ref: (2 * q + c_ref[0], i, 0)),
                  pl.BlockSpec((None, rows, w), lambda q, i, c_ref: (q, i, 0))],
        out_specs=pl.BlockSpec((None, rows, w), lambda q, i, c_ref: (q, i, 0)))
    return pl.pallas_call(
        body, name=name, grid_spec=grid_spec, out_shape=jax.ShapeDtypeStruct((4, r, w), term.dtype),
        compiler_params=_params(("parallel", "parallel")))(cidx, term, recv)


def _add_chips(name, sums, recv, rows):
    _, r, w = sums.shape
    qidx = (2 * lax.axis_index("x") + lax.axis_index("y")).astype(jnp.int32).reshape(1)

    def body(q_ref, a_ref, b_ref, o_ref):
        o_ref[...] = ((a_ref[...].astype(F32) + b_ref[0].astype(F32))
                      + (b_ref[1].astype(F32) + b_ref[2].astype(F32)))

    grid_spec = pltpu.PrefetchScalarGridSpec(
        num_scalar_prefetch=1, grid=(r // rows,),
        in_specs=[pl.BlockSpec((None, rows, w), lambda i, q_ref: (q_ref[0], i, 0)),
                  pl.BlockSpec((3, rows, w), lambda i, q_ref: (0, i, 0))],
        out_specs=pl.BlockSpec((rows, w), lambda i, q_ref: (i, 0)))
    return pl.pallas_call(
        body, name=name, grid_spec=grid_spec, out_shape=jax.ShapeDtypeStruct((r, w), F32),
        compiler_params=_params(("parallel",)))(qidx, sums, recv)


def _rs_rows(a):
    return SHARD_IN // 4 if a.shape[1] == SHARD_IN else a.shape[1]


def _reduce_scatter_start(names, terms):
    recv1 = _exchange_sibling(terms)
    sums = [_add_sibling("add_sibling_" + n, t, r, _rs_rows(t)) for n, t, r in zip(names, terms, recv1)]
    lands = [jax.ShapeDtypeStruct((3,) + s.shape[1:], s.dtype) for s in sums]
    send_sems, recv_sems, sums, lands, token = _exchange_start("exchange_chips_start", "chips", sums, lands)
    return (names, send_sems, recv_sems, sums, lands), token


def _reduce_scatter_finish(state, after):
    names, send_sems, recv_sems, sums, lands = state
    sums, recv2 = _exchange_wait("exchange_chips_wait", "chips", send_sems, recv_sems, sums, lands, after)
    return [_add_chips("add_chips_" + n, s, r, _rs_rows(s)) for n, s, r in zip(names, sums, recv2)]


def _adamw(name, w, g, m, v, rows=None):
    r, c = w.shape
    rows = r if rows is None else rows
    c1 = 1.0 - ADAM_B1 ** ADAM_STEP
    c2 = 1.0 - ADAM_B2 ** ADAM_STEP

    def body(w_ref, g_ref, m_ref, v_ref, d_ref, nm_ref, nv_ref):
        gv = g_ref[...]
        nm = ADAM_B1 * m_ref[...] + (1.0 - ADAM_B1) * gv
        nv = ADAM_B2 * v_ref[...] + (1.0 - ADAM_B2) * (gv * gv)
        nm_ref[...] = nm
        nv_ref[...] = nv
        d_ref[...] = -ADAM_LR * ((nm / c1) / (jnp.sqrt(nv / c2) + ADAM_EPS) + ADAM_WD * w_ref[...])

    spec = pl.BlockSpec((rows, c), lambda i: (i, 0))
    return pl.pallas_call(
        body, name=name, grid=(r // rows,), in_specs=[spec] * 4, out_specs=[spec] * 3,
        out_shape=[jax.ShapeDtypeStruct((r, c), F32)] * 3, compiler_params=_params(("parallel",)))(w, g, m, v)


def _sum_devices(gathered):
    def body(g_ref, o_ref):
        acc = g_ref[0]
        for j in range(1, N_DEV):
            acc = acc + g_ref[j]
        o_ref[...] = acc

    return pl.pallas_call(
        body, name="sum_devices", out_shape=jax.ShapeDtypeStruct(gathered.shape[1:], F32),
        compiler_params=_params())(gathered)


def _rows128(a, rows):
    flat = a.reshape(-1)
    return jnp.pad(flat, (0, rows * 128 - flat.shape[0])).reshape(rows, 128)


def kernel(x, mem, pre_norm, w_in, merge_bias, na_rpb, mem_norm, w_mem_kv, w_branch_a, w_branch_b, w_branch_c, w_out, post_norm, loss_target, m_pre_norm, m_w_in, m_merge_bias, m_na_rpb, m_mem_norm, m_w_mem_kv, m_w_branch_a, m_w_branch_b, m_w_branch_c, m_w_out, m_post_norm, v_pre_norm, v_w_in, v_merge_bias, v_na_rpb, v_mem_norm, v_w_mem_kv, v_w_branch_a, v_w_branch_b, v_w_branch_c, v_w_out, v_post_norm):
    wt_in_s = w_in[0].T.astype(BF16)
    rows_s = jnp.concatenate([w_mem_kv[0], w_out[0]], axis=0).astype(BF16)
    cols_s = jnp.concatenate([w_branch_a[0].T, w_branch_b[0].T, w_branch_c[0].T], axis=0).astype(BF16)
    mb_s = jnp.pad(merge_bias[0], ((0, 5), (0, 0)))
    wt_in = _all_gather([wt_in_s])[0].reshape(N_IN, D_MODEL)

    late_own = [rows_s, cols_s, mb_s]
    late_lands = [jax.ShapeDtypeStruct((N_DEV,) + s.shape, s.dtype) for s in late_own]
    l_send, l_recv, late_own, late_lands, late_token = _exchange_start("gather_late_start", "gather", late_own,
                                                                       late_lands)
    me = 4 * lax.axis_index("x") + 2 * lax.axis_index("y") + lax.axis_index("c")

    def late_weights(after):
        own, lands = _exchange_wait("gather_late_wait", "gather", l_send, l_recv, late_own, late_lands, after)
        g_rows, g_cols, g_mb = [lax.dynamic_update_slice(land, o[None], (me, 0, 0)) for land, o in zip(lands, own)]
        return (g_mb[:, :3].transpose(1, 0, 2).reshape(3, D_MODEL),
                g_rows[:, :128].reshape(D_MODEL, D_MODEL), g_cols[:, 0:128].reshape(D_MODEL, 512),
                g_cols[:, 128:256].reshape(D_MODEL, 512), g_cols[:, 256:384].reshape(D_MODEL, 512),
                g_rows[:, 128:].reshape(D_MODEL, D_MODEL))

    rs_state = []

    def reduce_start(grads):
        gmb_t = jnp.pad(grads["merge_bias"].reshape(3, N_DEV, 128).transpose(1, 0, 2), ((0, 0), (0, 5), (0, 0)))
        names = ["w_in", "w_kv", "w_out", "a", "b", "c", "mb"]
        terms = [grads["wt_in"].reshape(N_DEV, SHARD_IN, D_MODEL), grads["w_kv"].reshape(N_DEV, 128, D_MODEL),
                 grads["w_out"].reshape(N_DEV, 128, D_MODEL), grads["wt_a"].reshape(N_DEV, 128, 512),
                 grads["wt_b"].reshape(N_DEV, 128, 512), grads["wt_c"].reshape(N_DEV, 128, 512), gmb_t]
        state, token = _reduce_scatter_start(names, terms)
        rs_state.append(state)
        return token

    loss_term, grad_x, grads = _local_step(
        x[0], mem[0], loss_target[0], pre_norm, mem_norm, post_norm, na_rpb[0], wt_in, late_weights,
        dep_in=late_token, reduce_start=reduce_start)
    gt_in, g_kv, g_out, gt_a, gt_b, gt_c, g_mb8 = _reduce_scatter_finish(rs_state[0], grad_x)

    small = jnp.concatenate([_rows128(grads["pre_norm"], 8), _rows128(grads["mem_norm"], 8),
                             _rows128(grads["post_norm"], 8), _rows128(grads["na_rpb"], 32),
                             _rows128(loss_term, 8)], axis=0)
    total = _sum_devices(_gather_small(small))
    loss = total[56, 0]
    g_pre = total[0:8].reshape(1, D_MODEL)
    g_memn = total[8:16].reshape(1, D_MODEL)
    g_post = total[16:24].reshape(1, D_MODEL)
    g_rpb = total[24:56].reshape(-1)[:8 * 15 * 31].reshape(1, 8, 15, 31)

    grad = {
        "pre_norm": g_pre, "w_in": gt_in.T[None], "merge_bias": g_mb8[:3][None], "na_rpb": g_rpb,
        "mem_norm": g_memn, "w_mem_kv": g_kv[None], "w_branch_a": gt_a.T[None], "w_branch_b": gt_b.T[None],
        "w_branch_c": gt_c.T[None], "w_out": g_out[None], "post_norm": g_post}
    weights = {
        "pre_norm": (pre_norm, m_pre_norm, v_pre_norm), "w_in": (w_in, m_w_in, v_w_in),
        "merge_bias": (merge_bias, m_merge_bias, v_merge_bias), "na_rpb": (na_rpb, m_na_rpb, v_na_rpb),
        "mem_norm": (mem_norm, m_mem_norm, v_mem_norm), "w_mem_kv": (w_mem_kv, m_w_mem_kv, v_w_mem_kv),
        "w_branch_a": (w_branch_a, m_w_branch_a, v_w_branch_a), "w_branch_b": (w_branch_b, m_w_branch_b, v_w_branch_b),
        "w_branch_c": (w_branch_c, m_w_branch_c, v_w_branch_c), "w_out": (w_out, m_w_out, v_w_out),
        "post_norm": (post_norm, m_post_norm, v_post_norm)}
    order = ["pre_norm", "w_in", "merge_bias", "na_rpb", "mem_norm", "w_mem_kv", "w_branch_a", "w_branch_b",
             "w_branch_c", "w_out", "post_norm"]
    delta, new_m, new_v = {}, {}, {}
    for n in order:
        w, m, v = weights[n]
        shape = w.shape
        two_d = (-1, shape[-1])
        rows = 256 if n == "w_in" else None
        dl, nm, nv = _adamw("adamw_" + n, w.reshape(two_d), grad[n].reshape(two_d), m.reshape(two_d),
                            v.reshape(two_d), rows)
        delta[n], new_m[n], new_v[n] = dl.reshape(shape), nm.reshape(shape), nv.reshape(shape)

    return (loss, grad_x[None], *[grad[n] for n in order], *[delta[n] for n in order],
            *[new_m[n] for n in order], *[new_v[n] for n in order])
```

```python
import functools

import numpy as np
import jax
import jax.numpy as jnp
from jax import lax
from jax.experimental import pallas as pl
from jax.experimental.pallas import tpu as pltpu

F32 = jnp.float32
BF16 = jnp.bfloat16

SEQ = 2048
D_MODEL = 1024
N_IN = 11264
N_DEV = 8
SHARD_IN = N_IN // N_DEV
HEAD_DIM = 64
GRID_W = 64
NA_ROWS = 8
MEM_LEN = 256
DILATIONS = (1, 4, 16)
REACH = 64
ROPE_THETA = 500000.0
ROPE_DIM = 16
EPS = 1e-6
NEG = -1e30
ADAM_LR = 0.001
ADAM_B1 = 0.9
ADAM_B2 = 0.999
ADAM_EPS = 1e-08
ADAM_WD = 0.01
ADAM_STEP = 10

VMEM_LIMIT_BYTES = 56 * 1024 * 1024
MESH_ID = pl.DeviceIdType.MESH

NN = (((1,), (0,)), ((), ()))
NT = (((1,), (1,)), ((), ()))
TN = (((0,), (0,)), ((), ()))


def _params(sem=None):
    return pltpu.CompilerParams(dimension_semantics=sem, vmem_limit_bytes=VMEM_LIMIT_BYTES)


def _iota(shape, dim):
    return lax.broadcasted_iota(jnp.int32, shape, dim)


def _sigmoid(x):
    return 1.0 / (1.0 + jnp.exp(-x))


def _fold(a, d):
    if d == 1:
        return a
    n, w = a.shape
    return a.reshape(n // d, d, w).transpose(1, 0, 2).reshape(n, w)


def _unfold(a, d):
    if d == 1:
        return a
    n, w = a.shape
    return a.reshape(d, n // d, w).transpose(1, 0, 2).reshape(n, w)


def _rope_tables():
    half = ROPE_DIM // 2
    inv = (ROPE_THETA ** (-np.arange(half, dtype=np.float64) * 2.0 / ROPE_DIM)).astype(np.float32)
    pos = np.arange(SEQ, dtype=np.float32)
    ang = pos[:, None] * inv[None, :]
    cos, sin = np.cos(ang), np.sin(ang)
    zeros = np.zeros_like(cos)
    rest = HEAD_DIM - ROPE_DIM
    c64 = np.concatenate([cos, cos, np.ones((SEQ, rest), np.float32)], axis=1)
    s1 = np.concatenate([zeros, sin, np.zeros((SEQ, rest), np.float32)], axis=1)
    s2 = np.concatenate([-sin, zeros, np.zeros((SEQ, rest), np.float32)], axis=1)

    def fold(t, d):
        return t.reshape(SEQ // d, d, t.shape[1]).transpose(1, 0, 2).reshape(SEQ, t.shape[1])

    tabs = [np.stack([np.tile(fold(t, d), (1, 2)) for t in (c64, s1, s2)], axis=0) for d in DILATIONS]
    return jnp.asarray(np.stack(tabs, axis=0), dtype=F32)


def _rope(a, c, s1, s2):
    return a * c + pltpu.roll(a, 8, 1) * s1 + pltpu.roll(a, 120, 1) * s2


def _rope_t(a, c, s1, s2):
    return a * c + pltpu.roll(a * s1, 120, 1) + pltpu.roll(a * s2, 8, 1)


def _perm_of_block(j):
    return jnp.where(j < 3, 0, jnp.where(j < 6, 1, jnp.where(j < 9, 2, 0)))


def _mm(name, a, b, out_shape, out_dtype, grid, a_spec, b_spec, o_spec, acc_shape, dims, k_axis, nk):
    def body(a_ref, b_ref, o_ref, acc_ref):
        k = pl.program_id(k_axis)

        @pl.when(k == 0)
        def _():
            acc_ref[...] = jnp.zeros(acc_shape, F32)

        acc_ref[...] += lax.dot_general(a_ref[...], b_ref[...], dims, preferred_element_type=F32)

        @pl.when(k == nk - 1)
        def _():
            o_ref[...] = acc_ref[...].astype(out_dtype)

    sem = tuple("arbitrary" if ax == k_axis else "parallel" for ax in range(len(grid)))
    return pl.pallas_call(
        body, name=name, grid=grid, in_specs=[a_spec, b_spec], out_specs=o_spec,
        out_shape=jax.ShapeDtypeStruct(out_shape, out_dtype),
        scratch_shapes=[pltpu.VMEM(acc_shape, F32)], compiler_params=_params(sem))(a, b)


def _mm_simple(name, a, b, dims, out_dtype, tm, tn, tk):
    if dims is NN:
        m, kk = a.shape
        n = b.shape[1]
        a_spec = pl.BlockSpec((tm, tk), lambda i, j, k: (i, k))
        b_spec = pl.BlockSpec((tk, tn), lambda i, j, k: (k, j))
    elif dims is NT:
        m, kk = a.shape
        n = b.shape[0]
        a_spec = pl.BlockSpec((tm, tk), lambda i, j, k: (i, k))
        b_spec = pl.BlockSpec((tn, tk), lambda i, j, k: (j, k))
    else:
        kk, m = a.shape
        n = b.shape[1]
        a_spec = pl.BlockSpec((tk, tm), lambda i, j, k: (k, i))
        b_spec = pl.BlockSpec((tk, tn), lambda i, j, k: (k, j))
    grid = (m // tm, n // tn, kk // tk)
    o_spec = pl.BlockSpec((tm, tn), lambda i, j, k: (i, j))
    return _mm(name, a, b, (m, n), out_dtype, grid, a_spec, b_spec, o_spec, (tm, tn), dims, 2, kk // tk)


def _rmsnorm_fwd(name, x, gain, rows):
    n, d = x.shape

    def body(x_ref, g_ref, o_ref):
        xv = x_ref[...]
        rstd = lax.rsqrt(jnp.mean(xv * xv, axis=1, keepdims=True) + EPS)
        o_ref[...] = (xv * rstd * g_ref[...]).astype(BF16)

    return pl.pallas_call(
        body, name=name, grid=(n // rows,),
        in_specs=[pl.BlockSpec((rows, d), lambda i: (i, 0)), pl.BlockSpec((1, d), lambda i: (0, 0))],
        out_specs=pl.BlockSpec((rows, d), lambda i: (i, 0)),
        out_shape=jax.ShapeDtypeStruct((n, d), BF16), compiler_params=_params(("parallel",)))(x, gain)


def _folded_rows(first, rows, d):
    if d == 1:
        return pl.ds(pl.multiple_of(first, rows), rows)
    mlen = SEQ // d
    return pl.ds((first % mlen) * d + first // mlen, rows, stride=d)


def _prenorm_fold(x, gain):
    rows = 128

    nchunk = D_MODEL // 128

    def body(*refs):
        x_refs, g_ref, hs_ref, hst_ref = refs[:nchunk], refs[nchunk], refs[nchunk + 1], refs[nchunk + 2]
        first = pl.program_id(0) * rows
        for p, d in enumerate(DILATIONS):
            idx = _folded_rows(first, rows, d)
            xv = jnp.concatenate([r[idx, :] for r in x_refs], axis=1)
            rstd = lax.rsqrt(jnp.mean(xv * xv, axis=1, keepdims=True) + EPS)
            h = xv * rstd * g_ref[...]
            hs_ref[p] = h.astype(BF16)
            hst_ref[p] = h.T.astype(BF16)

    x_specs = [pl.BlockSpec((SEQ, 128), functools.partial(lambda c, i: (0, c), c)) for c in range(nchunk)]
    return pl.pallas_call(
        body, name="prenorm", grid=(SEQ // rows,),
        in_specs=x_specs + [pl.BlockSpec((1, D_MODEL), lambda i: (0, 0))],
        out_specs=[pl.BlockSpec((3, rows, D_MODEL), lambda i: (0, i, 0)),
                   pl.BlockSpec((3, D_MODEL, rows), lambda i: (0, 0, i))],
        out_shape=[jax.ShapeDtypeStruct((3, SEQ, D_MODEL), BF16), jax.ShapeDtypeStruct((3, D_MODEL, SEQ), BF16)],
        compiler_params=_params(("parallel",)))(*([x] * nchunk), gain)


def _prenorm_bwd(x, gain, dh, dout):
    rows = 256

    def body(x_ref, g_ref, a_ref, do_ref, dx_ref, gg_ref):
        xv = x_ref[...]
        rstd = lax.rsqrt(jnp.mean(xv * xv, axis=1, keepdims=True) + EPS)
        xn = xv * rstd
        dh = jnp.concatenate([a_ref[c] for c in range(D_MODEL // 128)], axis=1)
        gdh = dh * g_ref[...]
        dx_ref[...] = rstd * (gdh - xn * jnp.mean(gdh * xn, axis=1, keepdims=True)) + do_ref[...]

        @pl.when(pl.program_id(0) == 0)
        def _():
            gg_ref[...] = jnp.zeros((1, D_MODEL), F32)

        gg_ref[...] += jnp.sum(dh * xn, axis=0, keepdims=True)

    row = pl.BlockSpec((rows, D_MODEL), lambda i: (i, 0))
    vec = pl.BlockSpec((1, D_MODEL), lambda i: (0, 0))
    return pl.pallas_call(
        body, name="prenorm_bwd", grid=(SEQ // rows,),
        in_specs=[row, vec, pl.BlockSpec((D_MODEL // 128, rows, 128), lambda i: (0, i, 0)), row], out_specs=[row, vec],
        out_shape=[jax.ShapeDtypeStruct((SEQ, D_MODEL), F32), jax.ShapeDtypeStruct((1, D_MODEL), F32)],
        compiler_params=_params(("arbitrary",)))(x, gain, dh, dout)


def _memnorm_bwd(mem, dmemn):
    def body(m_ref, d_ref, gg_ref):
        mv = m_ref[...]
        rstd = lax.rsqrt(jnp.mean(mv * mv, axis=1, keepdims=True) + EPS)
        gg_ref[...] = jnp.sum(d_ref[...] * mv * rstd, axis=0, keepdims=True)

    return pl.pallas_call(
        body, name="memnorm_bwd", out_shape=jax.ShapeDtypeStruct((1, D_MODEL), F32),
        compiler_params=_params())(mem, dmemn)


def _dep_operand(dep):
    return ([], []) if dep is None else ([pl.BlockSpec(memory_space=pl.ANY)], [dep])


def _in_proj(hs, wt, tabs, dep=None):
    tm, tn = 512, 512
    dep_specs, dep_args = _dep_operand(dep)

    def body(h_ref, w_ref, t_ref, *rest):
        o_ref = rest[-1]
        j = pl.program_id(0)
        is_rope = jnp.logical_and(j < 9, j % 3 != 2)
        row_slices = [slice(r * tm, (r + 1) * tm) for r in range(SEQ // tm)]

        def product(rs):
            return lax.dot_general(h_ref[rs, :], w_ref[...], NT, preferred_element_type=F32)

        @pl.when(is_rope)
        def _():
            for rs in row_slices:
                acc = product(rs)
                c, s1, s2 = t_ref[0, rs, :], t_ref[1, rs, :], t_ref[2, rs, :]
                for q in range(tn // 128):
                    a = acc[:, q * 128:(q + 1) * 128]
                    o_ref[rs, q * 128:(q + 1) * 128] = _rope(a, c, s1, s2).astype(BF16)

        @pl.when(jnp.logical_not(is_rope))
        def _():
            for rs in row_slices:
                o_ref[rs, :] = product(rs).astype(BF16)

    return pl.pallas_call(
        body, name="in_proj", grid=(N_IN // tn,),
        in_specs=[pl.BlockSpec((None, SEQ, D_MODEL), lambda j: (_perm_of_block(j), 0, 0)),
                  pl.BlockSpec((tn, D_MODEL), lambda j: (j, 0)),
                  pl.BlockSpec((None, 3, SEQ, 128), lambda j: (_perm_of_block(j), 0, 0, 0))] + dep_specs,
        out_specs=pl.BlockSpec((SEQ, tn), lambda j: (0, j)),
        out_shape=jax.ShapeDtypeStruct((SEQ, N_IN), BF16),
        compiler_params=_params(("parallel",)))(hs, wt, tabs, *dep_args)


def _in_proj_dw(dparts, hst):
    tn = 512

    def body(h_ref, d_ref, o_ref):
        acc = jnp.dot(h_ref[...], d_ref[...], preferred_element_type=F32)
        o_ref[...] = acc.T.astype(BF16)

    return pl.pallas_call(
        body, name="in_proj_dw", grid=(N_IN // tn,),
        in_specs=[pl.BlockSpec((None, D_MODEL, SEQ), lambda j: (_perm_of_block(j), 0, 0)),
                  pl.BlockSpec((SEQ, tn), lambda j: (0, j))],
        out_specs=pl.BlockSpec((tn, D_MODEL), lambda j: (j, 0)),
        out_shape=jax.ShapeDtypeStruct((N_IN, D_MODEL), BF16), compiler_params=_params(("parallel",)))(hst, dparts)


def _in_proj_dh(dparts, wt, dep=None):
    tk = 512
    nblk = N_IN // tk
    nchunk = D_MODEL // 128

    def col(s):
        return jnp.where(s < 3, s, jnp.where(s < 16, s + 6, s - 13))

    dep_specs, dep_args = _dep_operand(dep)

    def body(d_ref, w_ref, *rest):
        o_ref, acc_ref = rest[-2:]
        s = pl.program_id(0)
        row_slices = [slice(r * 512, (r + 1) * 512) for r in range(SEQ // 512)]

        def product(rs):
            return jnp.dot(d_ref[rs, :], w_ref[...], preferred_element_type=F32)

        def accumulate(cond, to_out, init):
            @pl.when(cond)
            def _():
                for rs in row_slices:
                    prod = product(rs)
                    if not to_out:
                        if init:
                            acc_ref[rs, :] = prod
                        else:
                            acc_ref[rs, :] += prod
                        continue
                    for c in range(nchunk):
                        if init:
                            o_ref[c, rs, :] = prod[:, c * 128:(c + 1) * 128]
                        else:
                            o_ref[c, rs, :] += prod[:, c * 128:(c + 1) * 128]

        accumulate(s == 0, True, True)
        accumulate(jnp.logical_and(s > 0, s < 16), True, False)
        accumulate(jnp.logical_or(s == 16, s == 19), False, True)
        accumulate(jnp.logical_and(s > 16, s != 19), False, False)
        for last, d in ((18, 4), (21, 16)):
            @pl.when(s == last)
            def _():
                mlen = SEQ // d
                for r in range(d):
                    for c in range(nchunk):
                        o_ref[c, pl.ds(r, mlen, stride=d), :] += acc_ref[r * mlen:(r + 1) * mlen,
                                                                         c * 128:(c + 1) * 128]

    return pl.pallas_call(
        body, name="in_proj_dh", grid=(nblk,),
        in_specs=[pl.BlockSpec((SEQ, tk), lambda s: (0, col(s))),
                  pl.BlockSpec((tk, D_MODEL), lambda s: (col(s), 0))] + dep_specs,
        out_specs=pl.BlockSpec((nchunk, SEQ, 128), lambda s: (0, 0, 0)),
        out_shape=jax.ShapeDtypeStruct((nchunk, SEQ, 128), F32),
        scratch_shapes=[pltpu.VMEM((SEQ, D_MODEL), F32)],
        compiler_params=_params(("arbitrary",)))(dparts, wt, *dep_args)


def _head_lanes(lanes, hh):
    return lanes >= 64 if hh == 1 else lanes < 64


def _head_rows(x, lanes, hh, pair):
    if not pair:
        return jnp.max(x, axis=1, keepdims=True)
    return jnp.max(jnp.where(_head_lanes(lanes, hh), x, -jnp.inf), axis=1, keepdims=True)


def _mask_head(x, lanes, hh, pair):
    if not pair:
        return x
    return jnp.where(_head_lanes(lanes, hh), x.astype(F32), 0.0).astype(BF16)


def _merge_heads(parts, lanes, pair):
    if not pair:
        return parts[0]
    return jnp.where(lanes < 64, parts[0], parts[1])


def _window(mode, qi, tq, mlen, tk):
    if mode == "dil":
        q0 = qi * tq
        seg = (q0 // mlen) * mlen
        ks = jnp.clip(q0 - REACH, seg, seg + mlen - tk)
        return pl.multiple_of(ks, 64)
    if mode == "na":
        r_start = jnp.clip(qi - NA_ROWS // 2, 0, SEQ // GRID_W - NA_ROWS)
        return pl.multiple_of(r_start * GRID_W, 64)
    return 0


def _scores(mode, qh, k, scale, qi, tq, tk, ks, bias_ref, hh):
    s = lax.dot_general(qh, k, NT, preferred_element_type=F32) * scale
    if mode == "dil":
        qpos = qi * tq + _iota((tq, tk), 0)
        kpos = ks + _iota((tq, tk), 1)
        s = jnp.where(jnp.abs(qpos - kpos) <= REACH, s, NEG)
    elif mode == "na":
        off = qi - jnp.clip(qi - NA_ROWS // 2, 0, SEQ // GRID_W - NA_ROWS)
        s = s + bias_ref[hh, off]
    return s


def _attn_cfg(mode, d):
    if mode == "dil":
        mlen = SEQ // d
        return dict(pair=True, tq=128, tk=min(256, mlen), mlen=mlen, lk=SEQ, scale=HEAD_DIM ** -0.5, units=4,
                    nsub=ATTN_SUBTILES)
    if mode == "na":
        return dict(pair=True, tq=GRID_W, tk=NA_ROWS * GRID_W, mlen=SEQ, lk=SEQ, scale=HEAD_DIM ** -0.5, units=4,
                    nsub=ATTN_SUBTILES)
    return dict(pair=False, tq=128, tk=MEM_LEN, mlen=SEQ, lk=MEM_LEN, scale=128 ** -0.5, units=4,
                nsub=ATTN_SUBTILES)


ATTN_SUBTILES = 4


def _attn_fwd(name, mode, q_arr, k_arr, v_arr, qcol, kcol, vcol, d=1, bias=None):
    cfg = _attn_cfg(mode, d)
    pair, tq, tk, mlen, lk, scale = cfg["pair"], cfg["tq"], cfg["tk"], cfg["mlen"], cfg["lk"], cfg["scale"]
    nh = 2 if pair else 1
    nsub = cfg["nsub"]
    rows = nsub * tq

    def body(*refs):
        if mode == "na":
            q_ref, k_ref, v_ref, bias_ref, o_ref, l_ref = refs
        else:
            q_ref, k_ref, v_ref, o_ref, l_ref = refs
            bias_ref = None
        lanes = _iota((tq, 128), 1)
        chains = [(sub, hh) for sub in range(nsub) for hh in range(nh)]
        qis = [pl.program_id(1) * nsub + sub for sub in range(nsub)]
        kss = [_window(mode, qi, tq, mlen, tk) for qi in qis]
        vs = [v_ref[pl.ds(ks, tk), :] for ks in kss]
        ss = []
        for sub, hh in chains:
            q = q_ref[sub * tq:(sub + 1) * tq, :]
            k = k_ref[pl.ds(kss[sub], tk), :]
            ss.append(_scores(mode, _mask_head(q, lanes, hh, pair), k, scale, qis[sub], tq, tk, kss[sub], bias_ref, hh))
        ms = [jnp.max(s, axis=1, keepdims=True) for s in ss]
        ps = [jnp.exp(s - m) for s, m in zip(ss, ms)]
        ls = [jnp.sum(p, axis=1, keepdims=True) for p in ps]
        os_ = [jnp.dot(p.astype(BF16), vs[sub], preferred_element_type=F32) for p, (sub, hh) in zip(ps, chains)]
        for sub in range(nsub):
            sel = [i for i, (s_, hh) in enumerate(chains) if s_ == sub]
            outs = [os_[i] / ls[i] for i in sel]
            lses = [jnp.broadcast_to(ms[i] + jnp.log(ls[i]), (tq, 128)) for i in sel]
            dst = _folded_rows(qis[sub] * tq, tq, d) if mode == "dil" else slice(sub * tq, (sub + 1) * tq)
            o_ref[dst, :] = _merge_heads(outs, lanes, pair)
            l_ref[dst, :] = _merge_heads(lses, lanes, pair)

    in_specs = [pl.BlockSpec((rows, 128), lambda u, i: (i, qcol + u)),
                pl.BlockSpec((lk, 128), lambda u, i: (0, kcol + u)),
                pl.BlockSpec((lk, 128), lambda u, i: (0, vcol + u))]
    args = [q_arr, k_arr, v_arr]
    if mode == "na":
        in_specs.append(pl.BlockSpec((2, NA_ROWS, GRID_W, NA_ROWS * GRID_W), lambda u, i: (u, 0, 0, 0)))
        args.append(bias)
    if mode == "dil":
        out_spec = pl.BlockSpec((SEQ, 128), lambda u, i: (0, u))
    else:
        out_spec = pl.BlockSpec((rows, 128), lambda u, i: (i, u))
    return pl.pallas_call(
        body, name=name, grid=(cfg["units"], SEQ // rows), in_specs=in_specs, out_specs=[out_spec, out_spec],
        out_shape=[jax.ShapeDtypeStruct((SEQ, 512), F32), jax.ShapeDtypeStruct((SEQ, 512), F32)],
        compiler_params=_params(("parallel", "arbitrary")))(*args)


def _attn_bwd(name, mode, q_arr, k_arr, v_arr, qcol, kcol, vcol, do, lse, dp=None, o=None, d=1, bias=None,
              tabs=None):
    cfg = _attn_cfg(mode, d)
    pair, tq, tk, mlen, lk, scale = cfg["pair"], cfg["tq"], cfg["tk"], cfg["mlen"], cfg["lk"], cfg["scale"]
    nh = 2 if pair else 1
    nsub = cfg["nsub"]
    rows = nsub * tq
    nq = SEQ // rows
    kv_dtype = F32 if mode == "mem" else BF16

    def body(*refs):
        refs = list(refs)
        q_ref, k_ref, v_ref, do_ref, l_ref = refs[:5]
        rest = refs[5:]
        bias_ref = tq_ref = tk_ref = db_ref = None
        if mode == "dil":
            dp_ref, tq_ref, tk_ref, dq_ref, dk_ref, dv_ref, dk_acc, dv_acc = rest
        elif mode == "na":
            o_ref, bias_ref, dq_ref, dk_ref, dv_ref, db_ref, dk_acc, dv_acc = rest
        else:
            o_ref, dq_ref, dk_ref, dv_ref, dk_acc, dv_acc = rest
        step = pl.program_id(1)

        @pl.when(step == 0)
        def _():
            dk_acc[...] = jnp.zeros((lk, 128), F32)
            dv_acc[...] = jnp.zeros((lk, 128), F32)
            if mode == "na":
                db_ref[...] = jnp.zeros(db_ref.shape, F32)

        lanes = _iota((tq, 128), 1)
        lanes_k = _iota((tk, 128), 1)
        chains = [(sub, hh) for sub in range(nsub) for hh in range(nh)]
        qis = [step * nsub + sub for sub in range(nsub)]
        sls = [slice(sub * tq, (sub + 1) * tq) for sub in range(nsub)]
        kss = [_window(mode, qi, tq, mlen, tk) for qi in qis]
        qs = [q_ref[sl, :] for sl in sls]
        ks_ = [k_ref[pl.ds(ks, tk), :] for ks in kss]
        vs = [v_ref[pl.ds(ks, tk), :] for ks in kss]
        dovs, lsevs, dpvs = [], [], []
        for sub in range(nsub):
            if mode == "dil":
                src = _folded_rows(qis[sub] * tq, tq, d)
                dovs.append(do_ref[src, :].astype(BF16))
                lsevs.append(l_ref[src, :])
                dpvs.append(dp_ref[src, :])
            else:
                dovs.append(do_ref[sls[sub], :])
                lsevs.append(l_ref[sls[sub], :])
                dpvs.append(dovs[sub].astype(F32) * o_ref[sls[sub], :])
        ss = [_scores(mode, _mask_head(qs[sub], lanes, hh, pair), ks_[sub], scale, qis[sub], tq, tk, kss[sub],
                      bias_ref, hh) for sub, hh in chains]
        dpms = [lax.dot_general(_mask_head(dovs[sub], lanes, hh, pair), vs[sub], NT, preferred_element_type=F32)
                for sub, hh in chains]
        ps = [jnp.exp(s - _head_rows(lsevs[sub], lanes, hh, pair)) for s, (sub, hh) in zip(ss, chains)]
        dphs = []
        for sub, hh in chains:
            if mode == "dil":
                dphs.append(_head_rows(dpvs[sub], lanes, hh, pair))
            elif pair:
                dphs.append(jnp.sum(jnp.where(_head_lanes(lanes, hh), dpvs[sub], 0.0), axis=1, keepdims=True))
            else:
                dphs.append(jnp.sum(dpvs[sub], axis=1, keepdims=True))
        dss = [p * (dpm - dph) for p, dpm, dph in zip(ps, dpms, dphs)]
        if mode == "na":
            for ds, (sub, hh) in zip(dss, chains):
                off = qis[sub] - jnp.clip(qis[sub] - NA_ROWS // 2, 0, SEQ // GRID_W - NA_ROWS)
                db_ref[hh, off] += ds
        dsbs = [ds.astype(BF16) for ds in dss]
        dvs = [lax.dot_general(p.astype(BF16), dovs[sub], TN, preferred_element_type=F32)
               for p, (sub, hh) in zip(ps, chains)]
        dqs = [jnp.dot(dsb, ks_[sub], preferred_element_type=F32) * scale for dsb, (sub, hh) in zip(dsbs, chains)]
        dks = [lax.dot_general(dsb, qs[sub], TN, preferred_element_type=F32) * scale
               for dsb, (sub, hh) in zip(dsbs, chains)]
        for sub in range(nsub):
            sel = [i for i, (s_, hh) in enumerate(chains) if s_ == sub]
            sl = sls[sub]
            dq = _merge_heads([dqs[i] for i in sel], lanes, pair)
            if mode == "dil":
                dq = _rope_t(dq, tq_ref[0, sl, :], tq_ref[1, sl, :], tq_ref[2, sl, :])
            dq_ref[sl, :] = dq.astype(BF16)
            dk_acc[pl.ds(kss[sub], tk), :] += _merge_heads([dks[i] for i in sel], lanes_k, pair)
            dv_acc[pl.ds(kss[sub], tk), :] += _merge_heads([dvs[i] for i in sel], lanes_k, pair)

        @pl.when(step == nq - 1)
        def _():
            dkv = dk_acc[...]
            if mode == "dil":
                dkv = _rope_t(dkv, tk_ref[0], tk_ref[1], tk_ref[2])
            dk_ref[...] = dkv.astype(kv_dtype)
            dv_ref[...] = dv_acc[...].astype(kv_dtype)

    q_spec = pl.BlockSpec((rows, 128), lambda u, i: (i, qcol + u))
    row_spec = pl.BlockSpec((rows, 128), lambda u, i: (i, u))
    kv_out = pl.BlockSpec((lk, 128), lambda u, i: (0, u))
    whole = pl.BlockSpec((SEQ, 128), lambda u, i: (0, u))
    nat_spec = whole if mode == "dil" else row_spec
    in_specs = [q_spec,
                pl.BlockSpec((lk, 128), lambda u, i: (0, kcol + u)),
                pl.BlockSpec((lk, 128), lambda u, i: (0, vcol + u)),
                nat_spec, nat_spec]
    args = [q_arr, k_arr, v_arr, do, lse]
    out_specs = [row_spec, kv_out, kv_out]
    out_shape = [jax.ShapeDtypeStruct((SEQ, 512), BF16), jax.ShapeDtypeStruct((lk, 512), kv_dtype),
                 jax.ShapeDtypeStruct((lk, 512), kv_dtype)]
    if mode == "dil":
        in_specs += [whole, pl.BlockSpec((3, rows, 128), lambda u, i: (0, i, 0)),
                     pl.BlockSpec((3, SEQ, 128), lambda u, i: (0, 0, 0))]
        args += [dp, tabs, tabs]
    elif mode == "na":
        b_spec = pl.BlockSpec((2, NA_ROWS, GRID_W, NA_ROWS * GRID_W), lambda u, i: (u, 0, 0, 0))
        in_specs += [row_spec, b_spec]
        args += [o, bias]
        out_specs.append(b_spec)
        out_shape.append(jax.ShapeDtypeStruct((8, NA_ROWS, GRID_W, NA_ROWS * GRID_W), F32))
    else:
        in_specs.append(row_spec)
        args.append(o)
    return pl.pallas_call(
        body, name=name, grid=(cfg["units"], nq), in_specs=in_specs, out_specs=out_specs, out_shape=out_shape,
        scratch_shapes=[pltpu.VMEM((lk, 128), F32), pltpu.VMEM((lk, 128), F32)],
        compiler_params=_params(("parallel", "arbitrary")))(*args)


def _na_geometry():
    qc = _iota((GRID_W, 128), 0)
    lane = _iota((GRID_W, 128), 1)
    kc = lane & 63
    c_start = jnp.clip(qc - 8, 0, GRID_W - 16)
    valid = jnp.logical_and(kc >= c_start, kc < c_start + 16)
    return lane, valid


def _na_bias(rpb_rows):
    def body(r_ref, o_ref, t_ref):
        lane, valid = _na_geometry()
        for dd in range(14):
            row_a = jnp.broadcast_to(r_ref[dd:dd + 1, :], (GRID_W, 128))
            row_b = jnp.broadcast_to(r_ref[dd + 1:dd + 2, :], (GRID_W, 128))
            both = jnp.where(lane < 64, row_a, pltpu.roll(row_b, 64, 1))
            t = pltpu.roll(both, 128 - 15, 1, stride=1, stride_axis=0)
            t_ref[dd] = jnp.where(valid, t, NEG)
        for off in range(NA_ROWS):
            for p in range(4):
                o_ref[off, :, p * 128:(p + 1) * 128] = t_ref[2 * p - off + 7]

    return pl.pallas_call(
        body, name="na_bias", grid=(8,),
        in_specs=[pl.BlockSpec((None, 16, 128), lambda h: (h, 0, 0))],
        out_specs=pl.BlockSpec((None, NA_ROWS, GRID_W, NA_ROWS * GRID_W), lambda h: (h, 0, 0, 0)),
        out_shape=jax.ShapeDtypeStruct((8, NA_ROWS, GRID_W, NA_ROWS * GRID_W), F32),
        scratch_shapes=[pltpu.VMEM((14, GRID_W, 128), F32)],
        compiler_params=_params(("parallel",)))(rpb_rows)


def _na_bias_bwd(dbias):
    def body(d_ref, o_ref):
        lane, valid = _na_geometry()
        reverse = (_iota((GRID_W, GRID_W), 0) + _iota((GRID_W, GRID_W), 1) == GRID_W - 1).astype(F32)
        o_ref[...] = jnp.zeros((16, 128), F32)
        for dd in range(14):
            t = jnp.zeros((GRID_W, 128), F32)
            for off in range(NA_ROWS):
                for p in range(4):
                    if 2 * p - off + 7 == dd:
                        t = t + d_ref[off, :, p * 128:(p + 1) * 128]
            t = jnp.dot(reverse, jnp.where(valid, t, 0.0), precision=lax.Precision.HIGHEST,
                        preferred_element_type=F32)
            t = pltpu.roll(t, 128 - (GRID_W - 16), 1, stride=1, stride_axis=0)
            o_ref[dd:dd + 1, :] = jnp.sum(t, axis=0, keepdims=True)

    return pl.pallas_call(
        body, name="na_bias_bwd", grid=(8,),
        in_specs=[pl.BlockSpec((None, NA_ROWS, GRID_W, NA_ROWS * GRID_W), lambda h: (h, 0, 0, 0))],
        out_specs=pl.BlockSpec((None, 16, 128), lambda h: (h, 0, 0)),
        out_shape=jax.ShapeDtypeStruct((8, 16, 128), F32),
        compiler_params=_params(("parallel",)))(dbias)


GATE_ROWS = 128


def _group_weights(l0, l1, l2):
    m = jnp.maximum(jnp.maximum(l0, l1), l2)
    e0, e1, e2 = jnp.exp(l0 - m), jnp.exp(l1 - m), jnp.exp(l2 - m)
    inv = 1.0 / (e0 + e1 + e2)
    return e0 * inv, e1 * inv, e2 * inv


def _gate_specs():
    r512 = pl.BlockSpec((GATE_ROWS, 512), lambda i: (i, 0))
    r1024 = pl.BlockSpec((GATE_ROWS, D_MODEL), lambda i: (i, 0))
    silu_cols = [pl.BlockSpec((GATE_ROWS, 512), functools.partial(lambda b, i: (i, b), 13 + b)) for b in range(3)]
    logit_cols = [pl.BlockSpec((GATE_ROWS, D_MODEL), functools.partial(lambda b, i: (i, b), 8 + b)) for b in range(3)]
    return r512, r1024, silu_cols, logit_cols


def _gate_fwd(o_grp, l_grp, out_b, out_c, parts, merge_bias, wts):
    r512, r1024, silu_cols, logit_cols = _gate_specs()

    def body(o0, o1, o2, l0, l1, l2, ob, oc, ga, gb, gc, la, lb, lc, mb, wa, wb, wc,
             oa_ref, ua, ub, uc, za, zb, zc, y_ref):
        w0, w1, w2 = _group_weights(l0[...], l1[...], l2[...])
        out_a = w0 * o0[...] + w1 * o1[...] + w2 * o2[...]
        oa_ref[...] = out_a
        y = jnp.zeros((GATE_ROWS, D_MODEL), F32)
        for b, (ov, g_ref, l_ref, w_ref, u_ref, z_ref) in enumerate(
                ((out_a, ga, la, wa, ua, za), (ob[...], gb, lb, wb, ub, zb), (oc[...], gc, lc, wc, uc, zc))):
            g = g_ref[...].astype(F32)
            u = (ov * (g * _sigmoid(g))).astype(BF16)
            u_ref[...] = u
            z = lax.dot_general(u, w_ref[...], NT, preferred_element_type=F32)
            z_ref[...] = z.astype(BF16)
            gate = _sigmoid(l_ref[...].astype(F32) + mb[b:b + 1, :])
            y = y + gate * z
        y_ref[...] = y.astype(BF16)

    full = lambda shape: pl.BlockSpec(shape, lambda i: (0,) * len(shape))
    in_specs = ([r512] * 8 + silu_cols + logit_cols
                + [full((3, D_MODEL))] + [full((D_MODEL, 512))] * 3)
    out_specs = [r512] * 4 + [r1024] * 4
    out_shape = ([jax.ShapeDtypeStruct((SEQ, 512), F32)] + [jax.ShapeDtypeStruct((SEQ, 512), BF16)] * 3
                 + [jax.ShapeDtypeStruct((SEQ, D_MODEL), BF16)] * 4)
    res = pl.pallas_call(
        body, name="gate_fwd", grid=(SEQ // GATE_ROWS,), in_specs=in_specs, out_specs=out_specs,
        out_shape=out_shape, compiler_params=_params(("parallel",)))(
            *o_grp, *l_grp, out_b, out_c, parts, parts, parts, parts, parts, parts, merge_bias, *wts)
    return res[0], res[1:4], res[4:7], res[7]


def _gate_bwd(dy, z, parts, merge_bias, outs, o_grp, l_grp, wts, head_sum):
    r512, r1024, silu_cols, logit_cols = _gate_specs()

    def body(dy_ref, za, zb, zc, la, lb, lc, mb, oa, ob, oc, ga, gb, gc, o0, o1, o2, l0, l1, l2, wa, wb, wc, hs_ref,
             dla, dlb, dlc, gmb, dza, dzb, dzc, dga, dgb, dgc, do0, do1, do2, dp0, dp1, dp2, dob, doc):
        dyv = dy_ref[...].astype(F32)
        rows = []
        dos = []
        for b, (z_ref, l_ref, ov_ref, g_ref, w_ref, dl_ref, dz_ref, dg_ref) in enumerate(
                ((za, la, oa, ga, wa, dla, dza, dga), (zb, lb, ob, gb, wb, dlb, dzb, dgb),
                 (zc, lc, oc, gc, wc, dlc, dzc, dgc))):
            gate = _sigmoid(l_ref[...].astype(F32) + mb[b:b + 1, :])
            dl = dyv * z_ref[...].astype(F32) * gate * (1.0 - gate)
            dl_ref[...] = dl.astype(BF16)
            rows.append(jnp.sum(dl, axis=0, keepdims=True))
            dz = (dyv * gate).astype(BF16)
            dz_ref[...] = dz
            du = jnp.dot(dz, w_ref[...], preferred_element_type=F32)
            g = g_ref[...].astype(F32)
            sg = _sigmoid(g)
            dos.append(du * (g * sg))
            dg_ref[...] = (du * ov_ref[...] * (sg * (1.0 + g * (1.0 - sg)))).astype(BF16)

        @pl.when(pl.program_id(0) == 0)
        def _():
            gmb[...] = jnp.zeros((3, D_MODEL), F32)

        for b in range(3):
            gmb[b:b + 1, :] += rows[b]
        dob[...] = dos[1].astype(BF16)
        doc[...] = dos[2].astype(BF16)
        doa = dos[0]
        row_term = jnp.dot(doa * oa[...], hs_ref[...], precision=lax.Precision.HIGHEST, preferred_element_type=F32)
        ws = _group_weights(l0[...], l1[...], l2[...])
        for wg, do_ref, dp_ref in zip(ws, (do0, do1, do2), (dp0, dp1, dp2)):
            do_ref[...] = wg * doa
            dp_ref[...] = wg * row_term

    full = lambda shape: pl.BlockSpec(shape, lambda i: (0,) * len(shape))
    acc = pl.BlockSpec((3, D_MODEL), lambda i: (0, 0))
    in_specs = ([r1024] * 4 + logit_cols + [full((3, D_MODEL))] + [r512] * 3 + silu_cols + [r512] * 6
                + [full((D_MODEL, 512))] * 3 + [full((512, 512))])
    out_specs = [r1024] * 3 + [acc] + [r1024] * 3 + [r512] * 11
    out_shape = ([jax.ShapeDtypeStruct((SEQ, D_MODEL), BF16)] * 3 + [jax.ShapeDtypeStruct((3, D_MODEL), F32)]
                 + [jax.ShapeDtypeStruct((SEQ, D_MODEL), BF16)] * 3 + [jax.ShapeDtypeStruct((SEQ, 512), BF16)] * 3
                 + [jax.ShapeDtypeStruct((SEQ, 512), F32)] * 6 + [jax.ShapeDtypeStruct((SEQ, 512), BF16)] * 2)
    res = pl.pallas_call(
        body, name="gate_bwd", grid=(SEQ // GATE_ROWS,), in_specs=in_specs, out_specs=out_specs,
        out_shape=out_shape, compiler_params=_params(("arbitrary",)))(
            dy, *z, parts, parts, parts, merge_bias, *outs, parts, parts, parts, *o_grp, *l_grp, *wts, head_sum)
    return res[0:3], res[3], res[4:7], res[7:10], res[10:13], res[13:16], res[16], res[17]


def _post(y2, x, target, gain):
    rows = 256

    def body(y_ref, x_ref, t_ref, g_ref, do_ref, dy_ref, l_ref, gg_ref):
        yv = y_ref[...]
        rstd = lax.rsqrt(jnp.mean(yv * yv, axis=1, keepdims=True) + EPS)
        yn = yv * rstd
        gv = g_ref[...]
        err = x_ref[...] + yn * gv - t_ref[...]
        dout = err * (1.0 / D_MODEL)
        do_ref[...] = dout
        dn = dout * gv
        dy_ref[...] = (rstd * (dn - yn * jnp.mean(dn * yn, axis=1, keepdims=True))).astype(BF16)

        @pl.when(pl.program_id(0) == 0)
        def _():
            l_ref[...] = jnp.zeros((1, D_MODEL), F32)
            gg_ref[...] = jnp.zeros((1, D_MODEL), F32)

        l_ref[...] += jnp.sum(err * err, axis=0, keepdims=True)
        gg_ref[...] += jnp.sum(dout * yn, axis=0, keepdims=True)

    row = pl.BlockSpec((rows, D_MODEL), lambda i: (i, 0))
    vec = pl.BlockSpec((1, D_MODEL), lambda i: (0, 0))
    return pl.pallas_call(
        body, name="post", grid=(SEQ // rows,), in_specs=[row, row, row, vec], out_specs=[row, row, vec, vec],
        out_shape=[jax.ShapeDtypeStruct((SEQ, D_MODEL), F32), jax.ShapeDtypeStruct((SEQ, D_MODEL), BF16),
                   jax.ShapeDtypeStruct((1, D_MODEL), F32), jax.ShapeDtypeStruct((1, D_MODEL), F32)],
        compiler_params=_params(("arbitrary",)))(y2, x, target, gain)


def _local_step(x, mem, target, pre_norm, mem_norm, post_norm, na_rpb, wt_in, late_weights, dep_in=None,
                reduce_start=None):
    tabs = _rope_tables()
    hs, hst = _prenorm_fold(x, pre_norm)
    parts = _in_proj(hs, wt_in, tabs, dep_in)

    o_grp, l_grp = [], []
    for g, d in enumerate(DILATIONS):
        o, l = _attn_fwd("dil_fwd_%d" % g, "dil", parts, parts, parts, 12 * g, 12 * g + 4, 12 * g + 8, d=d)
        o_grp.append(o)
        l_grp.append(l)
    bias = _na_bias(jnp.pad(na_rpb, ((0, 0), (0, 1), (0, 128 - 31))))
    out_b, lse_b = _attn_fwd("na_fwd", "na", parts, parts, parts, 36, 40, 44, bias=bias)
    merge_bias, w_kv, wt_a, wt_b, wt_c, w_out = late_weights(out_b)
    memn = _rmsnorm_fwd("memnorm", mem, mem_norm, MEM_LEN)
    kv_m = _mm_simple("mem_kv", memn, w_kv, NN, BF16, MEM_LEN, 512, D_MODEL)
    out_c, lse_c = _attn_fwd("mem_fwd", "mem", parts, kv_m, kv_m, 48, 0, 4)

    wts = (wt_a, wt_b, wt_c)
    out_a, u, z, y = _gate_fwd(o_grp, l_grp, out_b, out_c, parts, merge_bias, wts)
    y2 = _mm_simple("out_proj", y, w_out, NN, F32, 512, D_MODEL, D_MODEL)
    dout, dy2, err_sq, g_post = _post(y2, x, target, post_norm)
    loss = 0.5 * jnp.sum(err_sq) / D_MODEL

    dy = _mm_simple("out_proj_dx", dy2, w_out, NT, BF16, 512, D_MODEL, D_MODEL)
    g_w_out = _mm_simple("out_proj_dw", y, dy2, TN, BF16, D_MODEL, 512, 512)

    rr = _iota((512, 512), 0) // HEAD_DIM
    cc = _iota((512, 512), 1) // HEAD_DIM
    head_sum = (rr == cc).astype(F32)
    dlog, g_mb, dz, dg, do_grp, dp_grp, do_b, do_c = _gate_bwd(
        dy, z, parts, merge_bias, (out_a, out_b, out_c), o_grp, l_grp, wts, head_sum)
    g_wt = [_mm_simple("branch_dw_%d" % b, dz[b], u[b], TN, BF16, D_MODEL, 512, 512) for b in range(3)]

    dqkv = []
    for g, d in enumerate(DILATIONS):
        dq, dk, dv = _attn_bwd("dil_bwd_%d" % g, "dil", parts, parts, parts, 12 * g, 12 * g + 4, 12 * g + 8,
                               do_grp[g], l_grp[g], dp=dp_grp[g], d=d, tabs=tabs[g])
        dqkv += [dq, dk, dv]
    dq_b, dk_b, dv_b, dbias = _attn_bwd("na_bwd", "na", parts, parts, parts, 36, 40, 44, do_b, lse_b, o=out_b,
                                        bias=bias)
    g_rpb_t = _na_bias_bwd(dbias)
    g_rpb = g_rpb_t[:, :15, :31] + jnp.pad(g_rpb_t[:, :14, 64:95], ((0, 0), (1, 0), (0, 0)))
    dq_c, dk_m, dv_m = _attn_bwd("mem_bwd", "mem", parts, kv_m, kv_m, 48, 0, 4, do_c, lse_c, o=out_c)

    dkv = jnp.concatenate([dk_m, dv_m], axis=1).astype(BF16)
    g_w_kv = _mm_simple("mem_kv_dw", memn, dkv, TN, BF16, D_MODEL, 512, MEM_LEN)
    dmemn = _mm_simple("mem_kv_dx", dkv, w_kv, NT, F32, MEM_LEN, 512, D_MODEL)
    g_mem_norm = _memnorm_bwd(mem, dmemn)

    dparts = jnp.concatenate(dqkv + [dq_b, dk_b, dv_b, dq_c] + list(dg) + list(dlog), axis=1)
    g_wt_in = _in_proj_dw(dparts, hst)
    grads = dict(wt_in=g_wt_in, w_kv=g_w_kv, wt_a=g_wt[0], wt_b=g_wt[1], wt_c=g_wt[2], w_out=g_w_out,
                 merge_bias=g_mb, mem_norm=g_mem_norm, post_norm=g_post, na_rpb=g_rpb)
    dep = reduce_start(grads) if reduce_start is not None else None
    dh = _in_proj_dh(dparts, wt_in, dep)
    grad_x, grads["pre_norm"] = _prenorm_bwd(x, pre_norm, dh, dout)
    return loss, grad_x, grads


ANY = pl.BlockSpec(memory_space=pl.ANY)


def _place():
    return lax.axis_index("x"), lax.axis_index("y"), lax.axis_index("c")


def _all_gather(shard):
    r = shard.shape[0]
    half = r // 2

    def body(src, out, send_sems, recv_sems, local_sem):
        x, y, c = _place()
        me, sib = (x, y, c), (x, y, 1 - c)
        xn, yn, dg = (1 - x, y, c), (x, 1 - y, c), (1 - x, 1 - y, c)

        def rows(dev, part=None):
            blk = out.at[4 * dev[0] + 2 * dev[1] + dev[2]]
            return blk if part is None else blk.at[pl.ds(part * half, half)]

        def copy(k, dev, part, to, own=False):
            return pltpu.make_async_remote_copy(
                src_ref=src if own else rows(dev, part), dst_ref=rows(dev, part),
                send_sem=send_sems.at[k], recv_sem=recv_sems.at[k], device_id=to, device_id_type=MESH_ID)

        def other(dev):
            return (dev[0], dev[1], 1 - dev[2])

        mine = pltpu.make_async_copy(src, rows(me), local_sem)
        mine.start()
        sent = [copy(0, me, None, sib, own=True), copy(1, me, None, xn, own=True), copy(2, me, None, yn, own=True)]
        for cp in sent:
            cp.start()
        copy(1, xn, None, me).wait_recv()
        sent += [copy(3, xn, 0, yn), copy(5, xn, None, sib)]
        sent[-2].start()
        sent[-1].start()
        copy(2, yn, None, me).wait_recv()
        sent += [copy(4, yn, 1, xn), copy(6, yn, None, sib)]
        sent[-2].start()
        sent[-1].start()
        copy(3, dg, 0, me).wait_recv()
        sent.append(copy(7, dg, 0, sib))
        sent[-1].start()
        copy(4, dg, 1, me).wait_recv()
        sent.append(copy(8, dg, 1, sib))
        sent[-1].start()
        copy(0, sib, None, me).wait_recv()
        copy(5, other(xn), None, me).wait_recv()
        copy(6, other(yn), None, me).wait_recv()
        copy(7, other(dg), 0, me).wait_recv()
        copy(8, other(dg), 1, me).wait_recv()
        for cp in sent:
            cp.wait_send()
        mine.wait()

    return pl.pallas_call(
        body, name="all_gather", in_specs=[ANY], out_specs=ANY,
        out_shape=jax.ShapeDtypeStruct((N_DEV,) + shard.shape, shard.dtype),
        scratch_shapes=[pltpu.SemaphoreType.DMA((9,)), pltpu.SemaphoreType.DMA((9,)), pltpu.SemaphoreType.DMA])(shard)


def _exchange_sibling(terms):
    nt = len(terms)

    def body(*refs):
        srcs, outs = refs[:nt], refs[nt:2 * nt]
        send_sems, recv_sems = refs[2 * nt:]
        x, y, c = _place()
        copies = []
        for q in range(4):
            for t in range(nt):
                copies.append(pltpu.make_async_remote_copy(
                    src_ref=srcs[t].at[2 * q + 1 - c], dst_ref=outs[t].at[q],
                    send_sem=send_sems.at[q * nt + t], recv_sem=recv_sems.at[q * nt + t],
                    device_id=(x, y, 1 - c), device_id_type=MESH_ID))
        for cp in copies:
            cp.start()
        for cp in copies:
            cp.wait()

    return pl.pallas_call(
        body, name="exchange_sibling", in_specs=[ANY] * nt, out_specs=[ANY] * nt,
        out_shape=[jax.ShapeDtypeStruct((4,) + s.shape[1:], s.dtype) for s in terms],
        scratch_shapes=[pltpu.SemaphoreType.DMA((4 * nt,)), pltpu.SemaphoreType.DMA((4 * nt,))])(*terms)


def _gather_small(block):
    def body(src, out, send_sems, recv_sems, local_sem):
        x, y, c = _place()
        me = 4 * x + 2 * y + c
        mine = pltpu.make_async_copy(src, out.at[me], local_sem)
        mine.start()
        copies = []
        for mask in range(1, 8):
            fx, fy, fc = (mask >> 2) & 1, (mask >> 1) & 1, mask & 1
            to = (jnp.where(fx, 1 - x, x), jnp.where(fy, 1 - y, y), jnp.where(fc, 1 - c, c))
            copies.append(pltpu.make_async_remote_copy(
                src_ref=src, dst_ref=out.at[me], send_sem=send_sems.at[mask - 1], recv_sem=recv_sems.at[mask - 1],
                device_id=to, device_id_type=MESH_ID))
        for cp in copies:
            cp.start()
        for cp in copies:
            cp.wait()
        mine.wait()

    return pl.pallas_call(
        body, name="gather_small", in_specs=[ANY], out_specs=ANY,
        out_shape=jax.ShapeDtypeStruct((N_DEV,) + block.shape, block.dtype),
        scratch_shapes=[pltpu.SemaphoreType.DMA((7,)), pltpu.SemaphoreType.DMA((7,)), pltpu.SemaphoreType.DMA])(block)


HBM = pl.BlockSpec(memory_space=pltpu.HBM)
SEM = pl.BlockSpec(memory_space=pltpu.SEMAPHORE)
DATAFLOW = pltpu.SideEffectType.DATAFLOW_SIDE_EFFECTING


def _split_copies(kind, srcs, lands, send_sems, recv_sems):
    nt = len(srcs)
    x, y, c = _place()
    copies = []
    if kind == "gather":
        me = 4 * x + 2 * y + c
        for mask in range(1, 8):
            fx, fy, fc = (mask >> 2) & 1, (mask >> 1) & 1, mask & 1
            to = (1 - x if fx else x, 1 - y if fy else y, 1 - c if fc else c)
            for t in range(nt):
                k = (mask - 1) * nt + t
                copies.append(pltpu.make_async_remote_copy(
                    src_ref=srcs[t], dst_ref=lands[t].at[me], send_sem=send_sems.at[k], recv_sem=recv_sems.at[k],
                    device_id=to, device_id_type=MESH_ID))
    else:
        for s, (tx, ty) in enumerate([(1 - x, y), (x, 1 - y), (1 - x, 1 - y)]):
            for t in range(nt):
                k = s * nt + t
                copies.append(pltpu.make_async_remote_copy(
                    src_ref=srcs[t].at[2 * tx + ty], dst_ref=lands[t].at[s], send_sem=send_sems.at[k],
                    recv_sem=recv_sems.at[k], device_id=(tx, ty, c), device_id_type=MESH_ID))
    return copies


def _split_count(kind, nt):
    return (7 if kind == "gather" else 3) * nt


def _exchange_start(name, kind, srcs, land_shapes, after=None):
    nt = len(srcs)
    n = _split_count(kind, nt)
    dep_specs, dep_args = _dep_operand(after)
    nd = len(dep_args)

    def body(*refs):
        src_refs, land_refs = refs[:nt], refs[nt:2 * nt]
        send_sems, recv_sems = refs[2 * nt + nd], refs[2 * nt + nd + 1]
        token = refs[-1]
        for cp in _split_copies(kind, src_refs, land_refs, send_sems, recv_sems):
            cp.start()
        token[...] = jnp.zeros_like(token)

    lands = [pltpu.with_memory_space_constraint(lax.empty(s.shape, s.dtype), pltpu.HBM) for s in land_shapes]
    res = pl.pallas_call(
        body, name=name,
        out_shape=(pltpu.SemaphoreType.DMA((n,)), pltpu.SemaphoreType.DMA((n,)),
                   *[pltpu.HBM(s.shape, s.dtype) for s in srcs], *[pltpu.HBM(s.shape, s.dtype) for s in land_shapes],
                   jax.ShapeDtypeStruct((8, 128), F32)),
        in_specs=[HBM] * (2 * nt) + dep_specs,
        out_specs=(SEM, SEM, *([HBM] * (2 * nt)), pl.BlockSpec(memory_space=pltpu.VMEM)),
        input_output_aliases={i: 2 + i for i in range(2 * nt)},
        compiler_params=pltpu.CompilerParams(has_side_effects=DATAFLOW))(
            *[pltpu.with_memory_space_constraint(s, pltpu.HBM) for s in srcs], *lands, *dep_args)
    return res[0], res[1], list(res[2:2 + nt]), list(res[2 + nt:2 + 2 * nt]), res[-1]


def _exchange_wait(name, kind, send_sems, recv_sems, srcs, lands, after):
    nt = len(srcs)

    def body(*refs):
        src_refs, land_refs = refs[:nt], refs[nt:2 * nt]
        s_sems, r_sems = refs[2 * nt], refs[2 * nt + 1]
        for cp in _split_copies(kind, src_refs, land_refs, s_sems, r_sems):
            cp.wait_send()
            cp.wait_recv()

    res = pl.pallas_call(
        body, name=name,
        out_shape=tuple(pltpu.HBM(s.shape, s.dtype) for s in list(srcs) + list(lands)),
        in_specs=[HBM] * (2 * nt) + [SEM, SEM, pl.BlockSpec(memory_space=pl.ANY)],
        out_specs=tuple([HBM] * (2 * nt)),
        input_output_aliases={i: i for i in range(2 * nt)},
        compiler_params=pltpu.CompilerParams(has_side_effects=DATAFLOW))(
            *srcs, *lands, send_sems, recv_sems, after)
    return list(res[:nt]), list(res[nt:])


def _add_sibling(name, term, recv, rows):
    _, r, w = term.shape
    cidx = lax.axis_index("c").astype(jnp.int32).reshape(1)

    def body(c_ref, a_ref, b_ref, o_ref):
        o_ref[...] = (a_ref[...].astype(F32) + b_ref[...].astype(F32)).astype(o_ref.dtype)

    grid_spec = pltpu.PrefetchScalarGridSpec(
        num_scalar_prefetch=1, grid=(4, r // rows),
        in_specs=[pl.BlockSpec((None, rows, w), lambda q, i, c_ref: (2 * q + c_ref[0], i, 0)),
                  pl.BlockSpec((None, rows, w), lambda q, i, c_ref: (q, i, 0))],
        out_specs=pl.BlockSpec((None, rows, w), lambda q, i, c_ref: (q, i, 0)))
    return pl.pallas_call(
        body, name=name, grid_spec=grid_spec, out_shape=jax.ShapeDtypeStruct((4, r, w), term.dtype),
        compiler_params=_params(("parallel", "parallel")))(cidx, term, recv)


def _add_chips(name, sums, recv, rows):
    _, r, w = sums.shape
    qidx = (2 * lax.axis_index("x") + lax.axis_index("y")).astype(jnp.int32).reshape(1)

    def body(q_ref, a_ref, b_ref, o_ref):
        o_ref[...] = ((a_ref[...].astype(F32) + b_ref[0].astype(F32))
                      + (b_ref[1].astype(F32) + b_ref[2].astype(F32)))

    grid_spec = pltpu.PrefetchScalarGridSpec(
        num_scalar_prefetch=1, grid=(r // rows,),
        in_specs=[pl.BlockSpec((None, rows, w), lambda i, q_ref: (q_ref[0], i, 0)),
                  pl.BlockSpec((3, rows, w), lambda i, q_ref: (0, i, 0))],
        out_specs=pl.BlockSpec((rows, w), lambda i, q_ref: (i, 0)))
    return pl.pallas_call(
        body, name=name, grid_spec=grid_spec, out_shape=jax.ShapeDtypeStruct((r, w), F32),
        compiler_params=_params(("parallel",)))(qidx, sums, recv)


def _rs_rows(a):
    return SHARD_IN // 4 if a.shape[1] == SHARD_IN else a.shape[1]


def _reduce_scatter_start(names, terms):
    recv1 = _exchange_sibling(terms)
    sums = [_add_sibling("add_sibling_" + n, t, r, _rs_rows(t)) for n, t, r in zip(names, terms, recv1)]
    lands = [jax.ShapeDtypeStruct((3,) + s.shape[1:], s.dtype) for s in sums]
    send_sems, recv_sems, sums, lands, token = _exchange_start("exchange_chips_start", "chips", sums, lands)
    return (names, send_sems, recv_sems, sums, lands), token


def _reduce_scatter_finish(state, after):
    names, send_sems, recv_sems, sums, lands = state
    sums, recv2 = _exchange_wait("exchange_chips_wait", "chips", send_sems, recv_sems, sums, lands, after)
    return [_add_chips("add_chips_" + n, s, r, _rs_rows(s)) for n, s, r in zip(names, sums, recv2)]


def _adamw(name, w, g, m, v, rows=None):
    r, c = w.shape
    rows = r if rows is None else rows
    c1 = 1.0 - ADAM_B1 ** ADAM_STEP
    c2 = 1.0 - ADAM_B2 ** ADAM_STEP

    def body(w_ref, g_ref, m_ref, v_ref, d_ref, nm_ref, nv_ref):
        gv = g_ref[...]
        nm = ADAM_B1 * m_ref[...] + (1.0 - ADAM_B1) * gv
        nv = ADAM_B2 * v_ref[...] + (1.0 - ADAM_B2) * (gv * gv)
        nm_ref[...] = nm
        nv_ref[...] = nv
        d_ref[...] = -ADAM_LR * ((nm / c1) / (jnp.sqrt(nv / c2) + ADAM_EPS) + ADAM_WD * w_ref[...])

    spec = pl.BlockSpec((rows, c), lambda i: (i, 0))
    return pl.pallas_call(
        body, name=name, grid=(r // rows,), in_specs=[spec] * 4, out_specs=[spec] * 3,
        out_shape=[jax.ShapeDtypeStruct((r, c), F32)] * 3, compiler_params=_params(("parallel",)))(w, g, m, v)


def _sum_devices(gathered):
    def body(g_ref, o_ref):
        acc = g_ref[0]
        for j in range(1, N_DEV):
            acc = acc + g_ref[j]
        o_ref[...] = acc

    return pl.pallas_call(
        body, name="sum_devices", out_shape=jax.ShapeDtypeStruct(gathered.shape[1:], F32),
        compiler_params=_params())(gathered)


def _rows128(a, rows):
    flat = a.reshape(-1)
    return jnp.pad(flat, (0, rows * 128 - flat.shape[0])).reshape(rows, 128)


def kernel(x, mem, pre_norm, w_in, merge_bias, na_rpb, mem_norm, w_mem_kv, w_branch_a, w_branch_b, w_branch_c, w_out, post_norm, loss_target, m_pre_norm, m_w_in, m_merge_bias, m_na_rpb, m_mem_norm, m_w_mem_kv, m_w_branch_a, m_w_branch_b, m_w_branch_c, m_w_out, m_post_norm, v_pre_norm, v_w_in, v_merge_bias, v_na_rpb, v_mem_norm, v_w_mem_kv, v_w_branch_a, v_w_branch_b, v_w_branch_c, v_w_out, v_post_norm):
    wt_in_s = w_in[0].T.astype(BF16)
    rows_s = jnp.concatenate([w_mem_kv[0], w_out[0]], axis=0).astype(BF16)
    cols_s = jnp.concatenate([w_branch_a[0].T, w_branch_b[0].T, w_branch_c[0].T], axis=0).astype(BF16)
    mb_s = jnp.pad(merge_bias[0], ((0, 5), (0, 0)))
    wt_in = _all_gather(wt_in_s).reshape(N_IN, D_MODEL)

    late_own = [rows_s, cols_s, mb_s]
    late_lands = [jax.ShapeDtypeStruct((N_DEV,) + s.shape, s.dtype) for s in late_own]
    l_send, l_recv, late_own, late_lands, late_token = _exchange_start("gather_late_start", "gather", late_own,
                                                                       late_lands, after=wt_in)
    me = 4 * lax.axis_index("x") + 2 * lax.axis_index("y") + lax.axis_index("c")

    def late_weights(after):
        own, lands = _exchange_wait("gather_late_wait", "gather", l_send, l_recv, late_own, late_lands, after)
        g_rows, g_cols, g_mb = [lax.dynamic_update_slice(land, o[None], (me, 0, 0)) for land, o in zip(lands, own)]
        return (g_mb[:, :3].transpose(1, 0, 2).reshape(3, D_MODEL),
                g_rows[:, :128].reshape(D_MODEL, D_MODEL), g_cols[:, 0:128].reshape(D_MODEL, 512),
                g_cols[:, 128:256].reshape(D_MODEL, 512), g_cols[:, 256:384].reshape(D_MODEL, 512),
                g_rows[:, 128:].reshape(D_MODEL, D_MODEL))

    rs_state = []

    def reduce_start(grads):
        gmb_t = jnp.pad(grads["merge_bias"].reshape(3, N_DEV, 128).transpose(1, 0, 2), ((0, 0), (0, 5), (0, 0)))
        names = ["w_in", "w_kv", "w_out", "a", "b", "c", "mb"]
        terms = [grads["wt_in"].reshape(N_DEV, SHARD_IN, D_MODEL), grads["w_kv"].reshape(N_DEV, 128, D_MODEL),
                 grads["w_out"].reshape(N_DEV, 128, D_MODEL), grads["wt_a"].reshape(N_DEV, 128, 512),
                 grads["wt_b"].reshape(N_DEV, 128, 512), grads["wt_c"].reshape(N_DEV, 128, 512), gmb_t]
        state, token = _reduce_scatter_start(names, terms)
        rs_state.append(state)
        return token

    loss_term, grad_x, grads = _local_step(
        x[0], mem[0], loss_target[0], pre_norm, mem_norm, post_norm, na_rpb[0], wt_in, late_weights,
        dep_in=late_token, reduce_start=reduce_start)
    gt_in, g_kv, g_out, gt_a, gt_b, gt_c, g_mb8 = _reduce_scatter_finish(rs_state[0], grad_x)

    small = jnp.concatenate([_rows128(grads["pre_norm"], 8), _rows128(grads["mem_norm"], 8),
                             _rows128(grads["post_norm"], 8), _rows128(grads["na_rpb"], 32),
                             _rows128(loss_term, 8)], axis=0)
    total = _sum_devices(_gather_small(small))
    loss = total[56, 0]
    g_pre = total[0:8].reshape(1, D_MODEL)
    g_memn = total[8:16].reshape(1, D_MODEL)
    g_post = total[16:24].reshape(1, D_MODEL)
    g_rpb = total[24:56].reshape(-1)[:8 * 15 * 31].reshape(1, 8, 15, 31)

    grad = {
        "pre_norm": g_pre, "w_in": gt_in.T[None], "merge_bias": g_mb8[:3][None], "na_rpb": g_rpb,
        "mem_norm": g_memn, "w_mem_kv": g_kv[None], "w_branch_a": gt_a.T[None], "w_branch_b": gt_b.T[None],
        "w_branch_c": gt_c.T[None], "w_out": g_out[None], "post_norm": g_post}
    weights = {
        "pre_norm": (pre_norm, m_pre_norm, v_pre_norm), "w_in": (w_in, m_w_in, v_w_in),
        "merge_bias": (merge_bias, m_merge_bias, v_merge_bias), "na_rpb": (na_rpb, m_na_rpb, v_na_rpb),
        "mem_norm": (mem_norm, m_mem_norm, v_mem_norm), "w_mem_kv": (w_mem_kv, m_w_mem_kv, v_w_mem_kv),
        "w_branch_a": (w_branch_a, m_w_branch_a, v_w_branch_a), "w_branch_b": (w_branch_b, m_w_branch_b, v_w_branch_b),
        "w_branch_c": (w_branch_c, m_w_branch_c, v_w_branch_c), "w_out": (w_out, m_w_out, v_w_out),
        "post_norm": (post_norm, m_post_norm, v_post_norm)}
    order = ["pre_norm", "w_in", "merge_bias", "na_rpb", "mem_norm", "w_mem_kv", "w_branch_a", "w_branch_b",
             "w_branch_c", "w_out", "post_norm"]
    delta, new_m, new_v = {}, {}, {}
    for n in order:
        w, m, v = weights[n]
        shape = w.shape
        two_d = (-1, shape[-1])
        rows = 256 if n == "w_in" else None
        dl, nm, nv = _adamw("adamw_" + n, w.reshape(two_d), grad[n].reshape(two_d), m.reshape(two_d),
                            v.reshape(two_d), rows)
        delta[n], new_m[n], new_v[n] = dl.reshape(shape), nm.reshape(shape), nv.reshape(shape)

    return (loss, grad_x[None], *[grad[n] for n in order], *[delta[n] for n in order],
            *[new_m[n] for n in order], *[new_v[n] for n in order])
```

```python
import functools

import numpy as np
import jax
import jax.numpy as jnp
from jax import lax
from jax.experimental import pallas as pl
from jax.experimental.pallas import tpu as pltpu

F32 = jnp.float32
BF16 = jnp.bfloat16

SEQ = 2048
D_MODEL = 1024
N_IN = 11264
N_DEV = 8
SHARD_IN = N_IN // N_DEV
HEAD_DIM = 64
GRID_W = 64
NA_ROWS = 8
MEM_LEN = 256
DILATIONS = (1, 4, 16)
REACH = 64
ROPE_THETA = 500000.0
ROPE_DIM = 16
EPS = 1e-6
NEG = -1e30
ADAM_LR = 0.001
ADAM_B1 = 0.9
ADAM_B2 = 0.999
ADAM_EPS = 1e-08
ADAM_WD = 0.01
ADAM_STEP = 10

VMEM_LIMIT_BYTES = 56 * 1024 * 1024
MESH_ID = pl.DeviceIdType.MESH

NN = (((1,), (0,)), ((), ()))
NT = (((1,), (1,)), ((), ()))
TN = (((0,), (0,)), ((), ()))


def _params(sem=None):
    return pltpu.CompilerParams(dimension_semantics=sem, vmem_limit_bytes=VMEM_LIMIT_BYTES)


def _iota(shape, dim):
    return lax.broadcasted_iota(jnp.int32, shape, dim)


def _sigmoid(x):
    return 1.0 / (1.0 + jnp.exp(-x))


def _fold(a, d):
    if d == 1:
        return a
    n, w = a.shape
    return a.reshape(n // d, d, w).transpose(1, 0, 2).reshape(n, w)


def _unfold(a, d):
    if d == 1:
        return a
    n, w = a.shape
    return a.reshape(d, n // d, w).transpose(1, 0, 2).reshape(n, w)


def _rope_tables():
    half = ROPE_DIM // 2
    inv = (ROPE_THETA ** (-np.arange(half, dtype=np.float64) * 2.0 / ROPE_DIM)).astype(np.float32)
    pos = np.arange(SEQ, dtype=np.float32)
    ang = pos[:, None] * inv[None, :]
    cos, sin = np.cos(ang), np.sin(ang)
    zeros = np.zeros_like(cos)
    rest = HEAD_DIM - ROPE_DIM
    c64 = np.concatenate([cos, cos, np.ones((SEQ, rest), np.float32)], axis=1)
    s1 = np.concatenate([zeros, sin, np.zeros((SEQ, rest), np.float32)], axis=1)
    s2 = np.concatenate([-sin, zeros, np.zeros((SEQ, rest), np.float32)], axis=1)

    def fold(t, d):
        return t.reshape(SEQ // d, d, t.shape[1]).transpose(1, 0, 2).reshape(SEQ, t.shape[1])

    tabs = [np.stack([np.tile(fold(t, d), (1, 2)) for t in (c64, s1, s2)], axis=0) for d in DILATIONS]
    return jnp.asarray(np.stack(tabs, axis=0), dtype=F32)


def _rope(a, c, s1, s2):
    return a * c + pltpu.roll(a, 8, 1) * s1 + pltpu.roll(a, 120, 1) * s2


def _rope_t(a, c, s1, s2):
    return a * c + pltpu.roll(a * s1, 120, 1) + pltpu.roll(a * s2, 8, 1)


def _perm_of_block(j):
    return jnp.where(j < 3, 0, jnp.where(j < 6, 1, jnp.where(j < 9, 2, 0)))


def _mm(name, a, b, out_shape, out_dtype, grid, a_spec, b_spec, o_spec, acc_shape, dims, k_axis, nk):
    def body(a_ref, b_ref, o_ref, acc_ref):
        k = pl.program_id(k_axis)

        @pl.when(k == 0)
        def _():
            acc_ref[...] = jnp.zeros(acc_shape, F32)

        acc_ref[...] += lax.dot_general(a_ref[...], b_ref[...], dims, preferred_element_type=F32)

        @pl.when(k == nk - 1)
        def _():
            o_ref[...] = acc_ref[...].astype(out_dtype)

    sem = tuple("arbitrary" if ax == k_axis else "parallel" for ax in range(len(grid)))
    return pl.pallas_call(
        body, name=name, grid=grid, in_specs=[a_spec, b_spec], out_specs=o_spec,
        out_shape=jax.ShapeDtypeStruct(out_shape, out_dtype),
        scratch_shapes=[pltpu.VMEM(acc_shape, F32)], compiler_params=_params(sem))(a, b)


def _mm_simple(name, a, b, dims, out_dtype, tm, tn, tk):
    if dims is NN:
        m, kk = a.shape
        n = b.shape[1]
        a_spec = pl.BlockSpec((tm, tk), lambda i, j, k: (i, k))
        b_spec = pl.BlockSpec((tk, tn), lambda i, j, k: (k, j))
    elif dims is NT:
        m, kk = a.shape
        n = b.shape[0]
        a_spec = pl.BlockSpec((tm, tk), lambda i, j, k: (i, k))
        b_spec = pl.BlockSpec((tn, tk), lambda i, j, k: (j, k))
    else:
        kk, m = a.shape
        n = b.shape[1]
        a_spec = pl.BlockSpec((tk, tm), lambda i, j, k: (k, i))
        b_spec = pl.BlockSpec((tk, tn), lambda i, j, k: (k, j))
    grid = (m // tm, n // tn, kk // tk)
    o_spec = pl.BlockSpec((tm, tn), lambda i, j, k: (i, j))
    return _mm(name, a, b, (m, n), out_dtype, grid, a_spec, b_spec, o_spec, (tm, tn), dims, 2, kk // tk)


def _rmsnorm_fwd(name, x, gain, rows):
    n, d = x.shape

    def body(x_ref, g_ref, o_ref):
        xv = x_ref[...]
        rstd = lax.rsqrt(jnp.mean(xv * xv, axis=1, keepdims=True) + EPS)
        o_ref[...] = (xv * rstd * g_ref[...]).astype(BF16)

    return pl.pallas_call(
        body, name=name, grid=(n // rows,),
        in_specs=[pl.BlockSpec((rows, d), lambda i: (i, 0)), pl.BlockSpec((1, d), lambda i: (0, 0))],
        out_specs=pl.BlockSpec((rows, d), lambda i: (i, 0)),
        out_shape=jax.ShapeDtypeStruct((n, d), BF16), compiler_params=_params(("parallel",)))(x, gain)


def _folded_rows(first, rows, d):
    if d == 1:
        return pl.ds(pl.multiple_of(first, rows), rows)
    mlen = SEQ // d
    return pl.ds((first % mlen) * d + first // mlen, rows, stride=d)


def _prenorm_fold(x, gain):
    rows = 128

    nchunk = D_MODEL // 128

    def body(*refs):
        x_refs, g_ref, hs_ref, hst_ref = refs[:nchunk], refs[nchunk], refs[nchunk + 1], refs[nchunk + 2]
        first = pl.program_id(0) * rows
        for p, d in enumerate(DILATIONS):
            idx = _folded_rows(first, rows, d)
            xv = jnp.concatenate([r[idx, :] for r in x_refs], axis=1)
            rstd = lax.rsqrt(jnp.mean(xv * xv, axis=1, keepdims=True) + EPS)
            h = xv * rstd * g_ref[...]
            hs_ref[p] = h.astype(BF16)
            hst_ref[p] = h.T.astype(BF16)

    x_specs = [pl.BlockSpec((SEQ, 128), functools.partial(lambda c, i: (0, c), c)) for c in range(nchunk)]
    return pl.pallas_call(
        body, name="prenorm", grid=(SEQ // rows,),
        in_specs=x_specs + [pl.BlockSpec((1, D_MODEL), lambda i: (0, 0))],
        out_specs=[pl.BlockSpec((3, rows, D_MODEL), lambda i: (0, i, 0)),
                   pl.BlockSpec((3, D_MODEL, rows), lambda i: (0, 0, i))],
        out_shape=[jax.ShapeDtypeStruct((3, SEQ, D_MODEL), BF16), jax.ShapeDtypeStruct((3, D_MODEL, SEQ), BF16)],
        compiler_params=_params(("parallel",)))(*([x] * nchunk), gain)


def _prenorm_bwd(x, gain, dh, dout):
    rows = 256

    def body(x_ref, g_ref, a_ref, do_ref, dx_ref, gg_ref):
        xv = x_ref[...]
        rstd = lax.rsqrt(jnp.mean(xv * xv, axis=1, keepdims=True) + EPS)
        xn = xv * rstd
        dh = jnp.concatenate([a_ref[c] for c in range(D_MODEL // 128)], axis=1)
        gdh = dh * g_ref[...]
        dx_ref[...] = rstd * (gdh - xn * jnp.mean(gdh * xn, axis=1, keepdims=True)) + do_ref[...]

        @pl.when(pl.program_id(0) == 0)
        def _():
            gg_ref[...] = jnp.zeros((1, D_MODEL), F32)

        gg_ref[...] += jnp.sum(dh * xn, axis=0, keepdims=True)

    row = pl.BlockSpec((rows, D_MODEL), lambda i: (i, 0))
    vec = pl.BlockSpec((1, D_MODEL), lambda i: (0, 0))
    return pl.pallas_call(
        body, name="prenorm_bwd", grid=(SEQ // rows,),
        in_specs=[row, vec, pl.BlockSpec((D_MODEL // 128, rows, 128), lambda i: (0, i, 0)), row], out_specs=[row, vec],
        out_shape=[jax.ShapeDtypeStruct((SEQ, D_MODEL), F32), jax.ShapeDtypeStruct((1, D_MODEL), F32)],
        compiler_params=_params(("arbitrary",)))(x, gain, dh, dout)


def _memnorm_bwd(mem, dmemn):
    def body(m_ref, d_ref, gg_ref):
        mv = m_ref[...]
        rstd = lax.rsqrt(jnp.mean(mv * mv, axis=1, keepdims=True) + EPS)
        gg_ref[...] = jnp.sum(d_ref[...] * mv * rstd, axis=0, keepdims=True)

    return pl.pallas_call(
        body, name="memnorm_bwd", out_shape=jax.ShapeDtypeStruct((1, D_MODEL), F32),
        compiler_params=_params())(mem, dmemn)


def _dep_operand(dep):
    return ([], []) if dep is None else ([pl.BlockSpec(memory_space=pl.ANY)], [dep])


def _in_proj(hs, wt, tabs, dep=None):
    tm, tn = 512, 512
    dep_specs, dep_args = _dep_operand(dep)

    def body(h_ref, w_ref, t_ref, *rest):
        o_ref = rest[-1]
        j = pl.program_id(0)
        is_rope = jnp.logical_and(j < 9, j % 3 != 2)
        row_slices = [slice(r * tm, (r + 1) * tm) for r in range(SEQ // tm)]

        def product(rs):
            return lax.dot_general(h_ref[rs, :], w_ref[...], NT, preferred_element_type=F32)

        @pl.when(is_rope)
        def _():
            for rs in row_slices:
                acc = product(rs)
                c, s1, s2 = t_ref[0, rs, :], t_ref[1, rs, :], t_ref[2, rs, :]
                for q in range(tn // 128):
                    a = acc[:, q * 128:(q + 1) * 128]
                    o_ref[rs, q * 128:(q + 1) * 128] = _rope(a, c, s1, s2).astype(BF16)

        @pl.when(jnp.logical_not(is_rope))
        def _():
            for rs in row_slices:
                o_ref[rs, :] = product(rs).astype(BF16)

    return pl.pallas_call(
        body, name="in_proj", grid=(N_IN // tn,),
        in_specs=[pl.BlockSpec((None, SEQ, D_MODEL), lambda j: (_perm_of_block(j), 0, 0)),
                  pl.BlockSpec((tn, D_MODEL), lambda j: (j, 0)),
                  pl.BlockSpec((None, 3, SEQ, 128), lambda j: (_perm_of_block(j), 0, 0, 0))] + dep_specs,
        out_specs=pl.BlockSpec((SEQ, tn), lambda j: (0, j)),
        out_shape=jax.ShapeDtypeStruct((SEQ, N_IN), BF16),
        compiler_params=_params(("parallel",)))(hs, wt, tabs, *dep_args)


def _piece_blocks(pieces):
    return [(a, h * 512) for a, p in enumerate(pieces) for h in range(p.shape[1] // 512)]


def _block_fetch(piece_refs, blocks, buf, sem):
    def start(block, slot):
        for b, (a, col) in enumerate(blocks):
            @pl.when(block == b)
            def _():
                pltpu.make_async_copy(piece_refs[a].at[:, pl.ds(col, 512)], buf.at[slot], sem.at[slot]).start()

    def wait(slot):
        pltpu.make_async_copy(piece_refs[0].at[:, pl.ds(0, 512)], buf.at[slot], sem.at[slot]).wait()

    return start, wait


def _in_proj_dw(pieces, hst, dep=None):
    tn = 512
    blocks = _piece_blocks(pieces)
    nblk = len(blocks)
    npc = len(pieces)
    dep_specs, dep_args = _dep_operand(dep)

    def body(h_ref, *rest):
        piece_refs = rest[:npc]
        o_ref, buf, sem = rest[-3:]
        j = pl.program_id(0)
        slot = j % 2
        start, wait = _block_fetch(piece_refs, blocks, buf, sem)

        @pl.when(j == 0)
        def _():
            start(j, slot)

        wait(slot)

        @pl.when(j + 1 < nblk)
        def _():
            start(j + 1, 1 - slot)

        acc = jnp.dot(h_ref[...], buf[slot], preferred_element_type=F32)
        o_ref[...] = acc.T.astype(BF16)

    return pl.pallas_call(
        body, name="in_proj_dw", grid=(nblk,),
        in_specs=[pl.BlockSpec((None, D_MODEL, SEQ), lambda j: (_perm_of_block(j), 0, 0))] + [ANY] * npc + dep_specs,
        out_specs=pl.BlockSpec((tn, D_MODEL), lambda j: (j, 0)),
        out_shape=jax.ShapeDtypeStruct((N_IN, D_MODEL), BF16),
        scratch_shapes=[pltpu.VMEM((2, SEQ, tn), BF16), pltpu.SemaphoreType.DMA((2,))],
        compiler_params=_params(("arbitrary",)))(hst, *pieces, *dep_args)


def _in_proj_dh(pieces, wt, dep=None):
    tk = 512
    blocks = _piece_blocks(pieces)
    nblk = len(blocks)
    npc = len(pieces)
    nchunk = D_MODEL // 128

    def col(s):
        return jnp.where(s < 3, s, jnp.where(s < 16, s + 6, s - 13))

    dep_specs, dep_args = _dep_operand(dep)

    def body(w_ref, *rest):
        piece_refs = rest[:npc]
        o_ref, acc_ref, buf, sem = rest[-4:]
        s = pl.program_id(0)
        slot = s % 2
        start, wait = _block_fetch(piece_refs, blocks, buf, sem)

        @pl.when(s == 0)
        def _():
            start(col(s), slot)

        wait(slot)

        @pl.when(s + 1 < nblk)
        def _():
            start(col(s + 1), 1 - slot)

        row_slices = [slice(r * 512, (r + 1) * 512) for r in range(SEQ // 512)]

        def product(rs):
            return jnp.dot(buf[slot, rs, :], w_ref[...], preferred_element_type=F32)

        def accumulate(cond, to_out, init):
            @pl.when(cond)
            def _():
                for rs in row_slices:
                    prod = product(rs)
                    if not to_out:
                        if init:
                            acc_ref[rs, :] = prod
                        else:
                            acc_ref[rs, :] += prod
                        continue
                    for c in range(nchunk):
                        if init:
                            o_ref[c, rs, :] = prod[:, c * 128:(c + 1) * 128]
                        else:
                            o_ref[c, rs, :] += prod[:, c * 128:(c + 1) * 128]

        accumulate(s == 0, True, True)
        accumulate(jnp.logical_and(s > 0, s < 16), True, False)
        accumulate(jnp.logical_or(s == 16, s == 19), False, True)
        accumulate(jnp.logical_and(s > 16, s != 19), False, False)
        for last, d in ((18, 4), (21, 16)):
            @pl.when(s == last)
            def _():
                mlen = SEQ // d
                for r in range(d):
                    for c in range(nchunk):
                        o_ref[c, pl.ds(r, mlen, stride=d), :] += acc_ref[r * mlen:(r + 1) * mlen,
                                                                         c * 128:(c + 1) * 128]

    return pl.pallas_call(
        body, name="in_proj_dh", grid=(nblk,),
        in_specs=[pl.BlockSpec((tk, D_MODEL), lambda s: (col(s), 0))] + [ANY] * npc + dep_specs,
        out_specs=pl.BlockSpec((nchunk, SEQ, 128), lambda s: (0, 0, 0)),
        out_shape=jax.ShapeDtypeStruct((nchunk, SEQ, 128), F32),
        scratch_shapes=[pltpu.VMEM((SEQ, D_MODEL), F32), pltpu.VMEM((2, SEQ, tk), BF16),
                        pltpu.SemaphoreType.DMA((2,))],
        compiler_params=_params(("arbitrary",)))(wt, *pieces, *dep_args)


def _head_lanes(lanes, hh):
    return lanes >= 64 if hh == 1 else lanes < 64


def _head_rows(x, lanes, hh, pair):
    if not pair:
        return jnp.max(x, axis=1, keepdims=True)
    return jnp.max(jnp.where(_head_lanes(lanes, hh), x, -jnp.inf), axis=1, keepdims=True)


def _mask_head(x, lanes, hh, pair):
    if not pair:
        return x
    return jnp.where(_head_lanes(lanes, hh), x.astype(F32), 0.0).astype(BF16)


def _merge_heads(parts, lanes, pair):
    if not pair:
        return parts[0]
    return jnp.where(lanes < 64, parts[0], parts[1])


def _window(mode, qi, tq, mlen, tk):
    if mode == "dil":
        q0 = qi * tq
        seg = (q0 // mlen) * mlen
        ks = jnp.clip(q0 - REACH, seg, seg + mlen - tk)
        return pl.multiple_of(ks, 64)
    if mode == "na":
        r_start = jnp.clip(qi - NA_ROWS // 2, 0, SEQ // GRID_W - NA_ROWS)
        return pl.multiple_of(r_start * GRID_W, 64)
    return 0


def _scores(mode, qh, k, scale, qi, tq, tk, ks, bias_ref, hh):
    s = lax.dot_general(qh, k, NT, preferred_element_type=F32) * scale
    if mode == "dil":
        qpos = qi * tq + _iota((tq, tk), 0)
        kpos = ks + _iota((tq, tk), 1)
        s = jnp.where(jnp.abs(qpos - kpos) <= REACH, s, NEG)
    elif mode == "na":
        off = qi - jnp.clip(qi - NA_ROWS // 2, 0, SEQ // GRID_W - NA_ROWS)
        s = s + bias_ref[hh, off]
    return s


def _attn_cfg(mode, d):
    if mode == "dil":
        mlen = SEQ // d
        return dict(pair=True, tq=128, tk=min(256, mlen), mlen=mlen, lk=SEQ, scale=HEAD_DIM ** -0.5, units=4,
                    nsub=ATTN_SUBTILES)
    if mode == "na":
        return dict(pair=True, tq=GRID_W, tk=NA_ROWS * GRID_W, mlen=SEQ, lk=SEQ, scale=HEAD_DIM ** -0.5, units=4,
                    nsub=ATTN_SUBTILES)
    return dict(pair=False, tq=128, tk=MEM_LEN, mlen=SEQ, lk=MEM_LEN, scale=128 ** -0.5, units=4,
                nsub=ATTN_SUBTILES)


ATTN_SUBTILES = 4


def _attn_fwd(name, mode, q_arr, k_arr, v_arr, qcol, kcol, vcol, d=1, bias=None):
    cfg = _attn_cfg(mode, d)
    pair, tq, tk, mlen, lk, scale = cfg["pair"], cfg["tq"], cfg["tk"], cfg["mlen"], cfg["lk"], cfg["scale"]
    nh = 2 if pair else 1
    nsub = cfg["nsub"]
    rows = nsub * tq

    def body(*refs):
        if mode == "na":
            q_ref, k_ref, v_ref, bias_ref, o_ref, l_ref = refs
        else:
            q_ref, k_ref, v_ref, o_ref, l_ref = refs
            bias_ref = None
        lanes = _iota((tq, 128), 1)
        chains = [(sub, hh) for sub in range(nsub) for hh in range(nh)]
        qis = [pl.program_id(1) * nsub + sub for sub in range(nsub)]
        kss = [_window(mode, qi, tq, mlen, tk) for qi in qis]
        vs = [v_ref[pl.ds(ks, tk), :] for ks in kss]
        ss = []
        for sub, hh in chains:
            q = q_ref[sub * tq:(sub + 1) * tq, :]
            k = k_ref[pl.ds(kss[sub], tk), :]
            ss.append(_scores(mode, _mask_head(q, lanes, hh, pair), k, scale, qis[sub], tq, tk, kss[sub], bias_ref, hh))
        ms = [jnp.max(s, axis=1, keepdims=True) for s in ss]
        ps = [jnp.exp(s - m) for s, m in zip(ss, ms)]
        ls = [jnp.sum(p, axis=1, keepdims=True) for p in ps]
        os_ = [jnp.dot(p.astype(BF16), vs[sub], preferred_element_type=F32) for p, (sub, hh) in zip(ps, chains)]
        for sub in range(nsub):
            sel = [i for i, (s_, hh) in enumerate(chains) if s_ == sub]
            outs = [os_[i] / ls[i] for i in sel]
            lses = [jnp.broadcast_to(ms[i] + jnp.log(ls[i]), (tq, 128)) for i in sel]
            dst = _folded_rows(qis[sub] * tq, tq, d) if mode == "dil" else slice(sub * tq, (sub + 1) * tq)
            o_ref[dst, :] = _merge_heads(outs, lanes, pair)
            l_ref[dst, :] = _merge_heads(lses, lanes, pair)

    in_specs = [pl.BlockSpec((rows, 128), lambda u, i: (i, qcol + u)),
                pl.BlockSpec((lk, 128), lambda u, i: (0, kcol + u)),
                pl.BlockSpec((lk, 128), lambda u, i: (0, vcol + u))]
    args = [q_arr, k_arr, v_arr]
    if mode == "na":
        in_specs.append(pl.BlockSpec((2, NA_ROWS, GRID_W, NA_ROWS * GRID_W), lambda u, i: (u, 0, 0, 0)))
        args.append(bias)
    if mode == "dil":
        out_spec = pl.BlockSpec((SEQ, 128), lambda u, i: (0, u))
    else:
        out_spec = pl.BlockSpec((rows, 128), lambda u, i: (i, u))
    return pl.pallas_call(
        body, name=name, grid=(cfg["units"], SEQ // rows), in_specs=in_specs, out_specs=[out_spec, out_spec],
        out_shape=[jax.ShapeDtypeStruct((SEQ, 512), F32), jax.ShapeDtypeStruct((SEQ, 512), F32)],
        compiler_params=_params(("parallel", "arbitrary")))(*args)


def _attn_bwd(name, mode, q_arr, k_arr, v_arr, qcol, kcol, vcol, do, lse, dp=None, o=None, d=1, bias=None,
              tabs=None):
    cfg = _attn_cfg(mode, d)
    pair, tq, tk, mlen, lk, scale = cfg["pair"], cfg["tq"], cfg["tk"], cfg["mlen"], cfg["lk"], cfg["scale"]
    nh = 2 if pair else 1
    nsub = cfg["nsub"]
    rows = nsub * tq
    nq = SEQ // rows
    kv_dtype = F32 if mode == "mem" else BF16

    def body(*refs):
        refs = list(refs)
        q_ref, k_ref, v_ref, do_ref, l_ref = refs[:5]
        rest = refs[5:]
        bias_ref = tq_ref = tk_ref = db_ref = None
        if mode == "dil":
            dp_ref, tq_ref, tk_ref, dq_ref, dk_ref, dv_ref, dk_acc, dv_acc = rest
        elif mode == "na":
            o_ref, bias_ref, dq_ref, dk_ref, dv_ref, db_ref, dk_acc, dv_acc = rest
        else:
            o_ref, dq_ref, dk_ref, dv_ref, dk_acc, dv_acc = rest
        step = pl.program_id(1)

        @pl.when(step == 0)
        def _():
            dk_acc[...] = jnp.zeros((lk, 128), F32)
            dv_acc[...] = jnp.zeros((lk, 128), F32)
            if mode == "na":
                db_ref[...] = jnp.zeros(db_ref.shape, F32)

        lanes = _iota((tq, 128), 1)
        lanes_k = _iota((tk, 128), 1)
        chains = [(sub, hh) for sub in range(nsub) for hh in range(nh)]
        qis = [step * nsub + sub for sub in range(nsub)]
        sls = [slice(sub * tq, (sub + 1) * tq) for sub in range(nsub)]
        kss = [_window(mode, qi, tq, mlen, tk) for qi in qis]
        qs = [q_ref[sl, :] for sl in sls]
        ks_ = [k_ref[pl.ds(ks, tk), :] for ks in kss]
        vs = [v_ref[pl.ds(ks, tk), :] for ks in kss]
        dovs, lsevs, dpvs = [], [], []
        for sub in range(nsub):
            if mode == "dil":
                src = _folded_rows(qis[sub] * tq, tq, d)
                dovs.append(do_ref[src, :].astype(BF16))
                lsevs.append(l_ref[src, :])
                dpvs.append(dp_ref[src, :])
            else:
                dovs.append(do_ref[sls[sub], :])
                lsevs.append(l_ref[sls[sub], :])
                dpvs.append(dovs[sub].astype(F32) * o_ref[sls[sub], :])
        ss = [_scores(mode, _mask_head(qs[sub], lanes, hh, pair), ks_[sub], scale, qis[sub], tq, tk, kss[sub],
                      bias_ref, hh) for sub, hh in chains]
        dpms = [lax.dot_general(_mask_head(dovs[sub], lanes, hh, pair), vs[sub], NT, preferred_element_type=F32)
                for sub, hh in chains]
        ps = [jnp.exp(s - _head_rows(lsevs[sub], lanes, hh, pair)) for s, (sub, hh) in zip(ss, chains)]
        dphs = []
        for sub, hh in chains:
            if mode == "dil":
                dphs.append(_head_rows(dpvs[sub], lanes, hh, pair))
            elif pair:
                dphs.append(jnp.sum(jnp.where(_head_lanes(lanes, hh), dpvs[sub], 0.0), axis=1, keepdims=True))
            else:
                dphs.append(jnp.sum(dpvs[sub], axis=1, keepdims=True))
        dss = [p * (dpm - dph) for p, dpm, dph in zip(ps, dpms, dphs)]
        if mode == "na":
            for ds, (sub, hh) in zip(dss, chains):
                off = qis[sub] - jnp.clip(qis[sub] - NA_ROWS // 2, 0, SEQ // GRID_W - NA_ROWS)
                db_ref[hh, off] += ds
        dsbs = [ds.astype(BF16) for ds in dss]
        dvs = [lax.dot_general(p.astype(BF16), dovs[sub], TN, preferred_element_type=F32)
               for p, (sub, hh) in zip(ps, chains)]
        dqs = [jnp.dot(dsb, ks_[sub], preferred_element_type=F32) * scale for dsb, (sub, hh) in zip(dsbs, chains)]
        dks = [lax.dot_general(dsb, qs[sub], TN, preferred_element_type=F32) * scale
               for dsb, (sub, hh) in zip(dsbs, chains)]
        for sub in range(nsub):
            sel = [i for i, (s_, hh) in enumerate(chains) if s_ == sub]
            sl = sls[sub]
            dq = _merge_heads([dqs[i] for i in sel], lanes, pair)
            if mode == "dil":
                dq = _rope_t(dq, tq_ref[0, sl, :], tq_ref[1, sl, :], tq_ref[2, sl, :])
            dq_ref[sl, :] = dq.astype(BF16)
            dk_acc[pl.ds(kss[sub], tk), :] += _merge_heads([dks[i] for i in sel], lanes_k, pair)
            dv_acc[pl.ds(kss[sub], tk), :] += _merge_heads([dvs[i] for i in sel], lanes_k, pair)

        @pl.when(step == nq - 1)
        def _():
            dkv = dk_acc[...]
            if mode == "dil":
                dkv = _rope_t(dkv, tk_ref[0], tk_ref[1], tk_ref[2])
            dk_ref[...] = dkv.astype(kv_dtype)
            dv_ref[...] = dv_acc[...].astype(kv_dtype)

    q_spec = pl.BlockSpec((rows, 128), lambda u, i: (i, qcol + u))
    row_spec = pl.BlockSpec((rows, 128), lambda u, i: (i, u))
    kv_out = pl.BlockSpec((lk, 128), lambda u, i: (0, u))
    whole = pl.BlockSpec((SEQ, 128), lambda u, i: (0, u))
    nat_spec = whole if mode == "dil" else row_spec
    in_specs = [q_spec,
                pl.BlockSpec((lk, 128), lambda u, i: (0, kcol + u)),
                pl.BlockSpec((lk, 128), lambda u, i: (0, vcol + u)),
                nat_spec, nat_spec]
    args = [q_arr, k_arr, v_arr, do, lse]
    out_specs = [row_spec, kv_out, kv_out]
    out_shape = [jax.ShapeDtypeStruct((SEQ, 512), BF16), jax.ShapeDtypeStruct((lk, 512), kv_dtype),
                 jax.ShapeDtypeStruct((lk, 512), kv_dtype)]
    if mode == "dil":
        in_specs += [whole, pl.BlockSpec((3, rows, 128), lambda u, i: (0, i, 0)),
                     pl.BlockSpec((3, SEQ, 128), lambda u, i: (0, 0, 0))]
        args += [dp, tabs, tabs]
    elif mode == "na":
        b_spec = pl.BlockSpec((2, NA_ROWS, GRID_W, NA_ROWS * GRID_W), lambda u, i: (u, 0, 0, 0))
        in_specs += [row_spec, b_spec]
        args += [o, bias]
        out_specs.append(b_spec)
        out_shape.append(jax.ShapeDtypeStruct((8, NA_ROWS, GRID_W, NA_ROWS * GRID_W), F32))
    else:
        in_specs.append(row_spec)
        args.append(o)
    return pl.pallas_call(
        body, name=name, grid=(cfg["units"], nq), in_specs=in_specs, out_specs=out_specs, out_shape=out_shape,
        scratch_shapes=[pltpu.VMEM((lk, 128), F32), pltpu.VMEM((lk, 128), F32)],
        compiler_params=_params(("parallel", "arbitrary")))(*args)


def _na_geometry():
    qc = _iota((GRID_W, 128), 0)
    lane = _iota((GRID_W, 128), 1)
    kc = lane & 63
    c_start = jnp.clip(qc - 8, 0, GRID_W - 16)
    valid = jnp.logical_and(kc >= c_start, kc < c_start + 16)
    return lane, valid


def _na_bias(rpb_rows):
    def body(r_ref, o_ref, t_ref):
        lane, valid = _na_geometry()
        for dd in range(14):
            row_a = jnp.broadcast_to(r_ref[dd:dd + 1, :], (GRID_W, 128))
            row_b = jnp.broadcast_to(r_ref[dd + 1:dd + 2, :], (GRID_W, 128))
            both = jnp.where(lane < 64, row_a, pltpu.roll(row_b, 64, 1))
            t = pltpu.roll(both, 128 - 15, 1, stride=1, stride_axis=0)
            t_ref[dd] = jnp.where(valid, t, NEG)
        for off in range(NA_ROWS):
            for p in range(4):
                o_ref[off, :, p * 128:(p + 1) * 128] = t_ref[2 * p - off + 7]

    return pl.pallas_call(
        body, name="na_bias", grid=(8,),
        in_specs=[pl.BlockSpec((None, 16, 128), lambda h: (h, 0, 0))],
        out_specs=pl.BlockSpec((None, NA_ROWS, GRID_W, NA_ROWS * GRID_W), lambda h: (h, 0, 0, 0)),
        out_shape=jax.ShapeDtypeStruct((8, NA_ROWS, GRID_W, NA_ROWS * GRID_W), F32),
        scratch_shapes=[pltpu.VMEM((14, GRID_W, 128), F32)],
        compiler_params=_params(("parallel",)))(rpb_rows)


def _na_bias_bwd(dbias):
    def body(d_ref, o_ref):
        lane, valid = _na_geometry()
        reverse = (_iota((GRID_W, GRID_W), 0) + _iota((GRID_W, GRID_W), 1) == GRID_W - 1).astype(F32)
        o_ref[...] = jnp.zeros((16, 128), F32)
        for dd in range(14):
            t = jnp.zeros((GRID_W, 128), F32)
            for off in range(NA_ROWS):
                for p in range(4):
                    if 2 * p - off + 7 == dd:
                        t = t + d_ref[off, :, p * 128:(p + 1) * 128]
            t = jnp.dot(reverse, jnp.where(valid, t, 0.0), precision=lax.Precision.HIGHEST,
                        preferred_element_type=F32)
            t = pltpu.roll(t, 128 - (GRID_W - 16), 1, stride=1, stride_axis=0)
            o_ref[dd:dd + 1, :] = jnp.sum(t, axis=0, keepdims=True)

    return pl.pallas_call(
        body, name="na_bias_bwd", grid=(8,),
        in_specs=[pl.BlockSpec((None, NA_ROWS, GRID_W, NA_ROWS * GRID_W), lambda h: (h, 0, 0, 0))],
        out_specs=pl.BlockSpec((None, 16, 128), lambda h: (h, 0, 0)),
        out_shape=jax.ShapeDtypeStruct((8, 16, 128), F32),
        compiler_params=_params(("parallel",)))(dbias)


GATE_ROWS = 128


def _group_weights(l0, l1, l2):
    m = jnp.maximum(jnp.maximum(l0, l1), l2)
    e0, e1, e2 = jnp.exp(l0 - m), jnp.exp(l1 - m), jnp.exp(l2 - m)
    inv = 1.0 / (e0 + e1 + e2)
    return e0 * inv, e1 * inv, e2 * inv


def _gate_specs():
    r512 = pl.BlockSpec((GATE_ROWS, 512), lambda i: (i, 0))
    r1024 = pl.BlockSpec((GATE_ROWS, D_MODEL), lambda i: (i, 0))
    silu_cols = [pl.BlockSpec((GATE_ROWS, 512), functools.partial(lambda b, i: (i, b), 13 + b)) for b in range(3)]
    logit_cols = [pl.BlockSpec((GATE_ROWS, D_MODEL), functools.partial(lambda b, i: (i, b), 8 + b)) for b in range(3)]
    return r512, r1024, silu_cols, logit_cols


def _gate_fwd(o_grp, l_grp, out_b, out_c, parts, merge_bias, wts):
    r512, r1024, silu_cols, logit_cols = _gate_specs()

    def body(o0, o1, o2, l0, l1, l2, ob, oc, ga, gb, gc, la, lb, lc, mb, wa, wb, wc,
             oa_ref, ua, ub, uc, za, zb, zc, y_ref):
        w0, w1, w2 = _group_weights(l0[...], l1[...], l2[...])
        out_a = w0 * o0[...] + w1 * o1[...] + w2 * o2[...]
        oa_ref[...] = out_a
        y = jnp.zeros((GATE_ROWS, D_MODEL), F32)
        for b, (ov, g_ref, l_ref, w_ref, u_ref, z_ref) in enumerate(
                ((out_a, ga, la, wa, ua, za), (ob[...], gb, lb, wb, ub, zb), (oc[...], gc, lc, wc, uc, zc))):
            g = g_ref[...].astype(F32)
            u = (ov * (g * _sigmoid(g))).astype(BF16)
            u_ref[...] = u
            z = lax.dot_general(u, w_ref[...], NT, preferred_element_type=F32)
            z_ref[...] = z.astype(BF16)
            gate = _sigmoid(l_ref[...].astype(F32) + mb[b:b + 1, :])
            y = y + gate * z
        y_ref[...] = y.astype(BF16)

    full = lambda shape: pl.BlockSpec(shape, lambda i: (0,) * len(shape))
    in_specs = ([r512] * 8 + silu_cols + logit_cols
                + [full((3, D_MODEL))] + [full((D_MODEL, 512))] * 3)
    out_specs = [r512] * 4 + [r1024] * 4
    out_shape = ([jax.ShapeDtypeStruct((SEQ, 512), F32)] + [jax.ShapeDtypeStruct((SEQ, 512), BF16)] * 3
                 + [jax.ShapeDtypeStruct((SEQ, D_MODEL), BF16)] * 4)
    res = pl.pallas_call(
        body, name="gate_fwd", grid=(SEQ // GATE_ROWS,), in_specs=in_specs, out_specs=out_specs,
        out_shape=out_shape, compiler_params=_params(("parallel",)))(
            *o_grp, *l_grp, out_b, out_c, parts, parts, parts, parts, parts, parts, merge_bias, *wts)
    return res[0], res[1:4], res[4:7], res[7]


def _gate_bwd(dy, z, parts, merge_bias, outs, o_grp, l_grp, wts, head_sum):
    r512, r1024, silu_cols, logit_cols = _gate_specs()

    def body(dy_ref, za, zb, zc, la, lb, lc, mb, oa, ob, oc, ga, gb, gc, o0, o1, o2, l0, l1, l2, wa, wb, wc, hs_ref,
             dla, dlb, dlc, gmb, dza, dzb, dzc, dga, dgb, dgc, do0, do1, do2, dp0, dp1, dp2, dob, doc):
        dyv = dy_ref[...].astype(F32)
        rows = []
        dos = []
        for b, (z_ref, l_ref, ov_ref, g_ref, w_ref, dl_ref, dz_ref, dg_ref) in enumerate(
                ((za, la, oa, ga, wa, dla, dza, dga), (zb, lb, ob, gb, wb, dlb, dzb, dgb),
                 (zc, lc, oc, gc, wc, dlc, dzc, dgc))):
            gate = _sigmoid(l_ref[...].astype(F32) + mb[b:b + 1, :])
            dl = dyv * z_ref[...].astype(F32) * gate * (1.0 - gate)
            dl_ref[...] = dl.astype(BF16)
            rows.append(jnp.sum(dl, axis=0, keepdims=True))
            dz = (dyv * gate).astype(BF16)
            dz_ref[...] = dz
            du = jnp.dot(dz, w_ref[...], preferred_element_type=F32)
            g = g_ref[...].astype(F32)
            sg = _sigmoid(g)
            dos.append(du * (g * sg))
            dg_ref[...] = (du * ov_ref[...] * (sg * (1.0 + g * (1.0 - sg)))).astype(BF16)

        @pl.when(pl.program_id(0) == 0)
        def _():
            gmb[...] = jnp.zeros((3, D_MODEL), F32)

        for b in range(3):
            gmb[b:b + 1, :] += rows[b]
        dob[...] = dos[1].astype(BF16)
        doc[...] = dos[2].astype(BF16)
        doa = dos[0]
        row_term = jnp.dot(doa * oa[...], hs_ref[...], precision=lax.Precision.HIGHEST, preferred_element_type=F32)
        ws = _group_weights(l0[...], l1[...], l2[...])
        for wg, do_ref, dp_ref in zip(ws, (do0, do1, do2), (dp0, dp1, dp2)):
            do_ref[...] = wg * doa
            dp_ref[...] = wg * row_term

    full = lambda shape: pl.BlockSpec(shape, lambda i: (0,) * len(shape))
    acc = pl.BlockSpec((3, D_MODEL), lambda i: (0, 0))
    in_specs = ([r1024] * 4 + logit_cols + [full((3, D_MODEL))] + [r512] * 3 + silu_cols + [r512] * 6
                + [full((D_MODEL, 512))] * 3 + [full((512, 512))])
    out_specs = [r1024] * 3 + [acc] + [r1024] * 3 + [r512] * 11
    out_shape = ([jax.ShapeDtypeStruct((SEQ, D_MODEL), BF16)] * 3 + [jax.ShapeDtypeStruct((3, D_MODEL), F32)]
                 + [jax.ShapeDtypeStruct((SEQ, D_MODEL), BF16)] * 3 + [jax.ShapeDtypeStruct((SEQ, 512), BF16)] * 3
                 + [jax.ShapeDtypeStruct((SEQ, 512), F32)] * 6 + [jax.ShapeDtypeStruct((SEQ, 512), BF16)] * 2)
    res = pl.pallas_call(
        body, name="gate_bwd", grid=(SEQ // GATE_ROWS,), in_specs=in_specs, out_specs=out_specs,
        out_shape=out_shape, compiler_params=_params(("arbitrary",)))(
            dy, *z, parts, parts, parts, merge_bias, *outs, parts, parts, parts, *o_grp, *l_grp, *wts, head_sum)
    return res[0:3], res[3], res[4:7], res[7:10], res[10:13], res[13:16], res[16], res[17]


def _post(y2, x, target, gain):
    rows = 256

    def body(y_ref, x_ref, t_ref, g_ref, do_ref, dy_ref, l_ref, gg_ref):
        yv = y_ref[...]
        rstd = lax.rsqrt(jnp.mean(yv * yv, axis=1, keepdims=True) + EPS)
        yn = yv * rstd
        gv = g_ref[...]
        err = x_ref[...] + yn * gv - t_ref[...]
        dout = err * (1.0 / D_MODEL)
        do_ref[...] = dout
        dn = dout * gv
        dy_ref[...] = (rstd * (dn - yn * jnp.mean(dn * yn, axis=1, keepdims=True))).astype(BF16)

        @pl.when(pl.program_id(0) == 0)
        def _():
            l_ref[...] = jnp.zeros((1, D_MODEL), F32)
            gg_ref[...] = jnp.zeros((1, D_MODEL), F32)

        l_ref[...] += jnp.sum(err * err, axis=0, keepdims=True)
        gg_ref[...] += jnp.sum(dout * yn, axis=0, keepdims=True)

    row = pl.BlockSpec((rows, D_MODEL), lambda i: (i, 0))
    vec = pl.BlockSpec((1, D_MODEL), lambda i: (0, 0))
    return pl.pallas_call(
        body, name="post", grid=(SEQ // rows,), in_specs=[row, row, row, vec], out_specs=[row, row, vec, vec],
        out_shape=[jax.ShapeDtypeStruct((SEQ, D_MODEL), F32), jax.ShapeDtypeStruct((SEQ, D_MODEL), BF16),
                   jax.ShapeDtypeStruct((1, D_MODEL), F32), jax.ShapeDtypeStruct((1, D_MODEL), F32)],
        compiler_params=_params(("arbitrary",)))(y2, x, target, gain)


def _local_step(x, mem, target, pre_norm, mem_norm, post_norm, na_rpb, wt_in, late_weights, dep_in=None,
                reduce_start=None):
    tabs = _rope_tables()
    hs, hst = _prenorm_fold(x, pre_norm)
    parts = _in_proj(hs, wt_in, tabs, dep_in)

    o_grp, l_grp = [], []
    for g, d in enumerate(DILATIONS):
        o, l = _attn_fwd("dil_fwd_%d" % g, "dil", parts, parts, parts, 12 * g, 12 * g + 4, 12 * g + 8, d=d)
        o_grp.append(o)
        l_grp.append(l)
    bias = _na_bias(jnp.pad(na_rpb, ((0, 0), (0, 1), (0, 128 - 31))))
    out_b, lse_b = _attn_fwd("na_fwd", "na", parts, parts, parts, 36, 40, 44, bias=bias)
    merge_bias, w_kv, wt_a, wt_b, wt_c, w_out = late_weights(out_b)
    memn = _rmsnorm_fwd("memnorm", mem, mem_norm, MEM_LEN)
    kv_m = _mm_simple("mem_kv", memn, w_kv, NN, BF16, MEM_LEN, 512, D_MODEL)
    out_c, lse_c = _attn_fwd("mem_fwd", "mem", parts, kv_m, kv_m, 48, 0, 4)

    wts = (wt_a, wt_b, wt_c)
    out_a, u, z, y = _gate_fwd(o_grp, l_grp, out_b, out_c, parts, merge_bias, wts)
    y2 = _mm_simple("out_proj", y, w_out, NN, F32, 512, D_MODEL, D_MODEL)
    dout, dy2, err_sq, g_post = _post(y2, x, target, post_norm)
    loss = 0.5 * jnp.sum(err_sq) / D_MODEL

    dy = _mm_simple("out_proj_dx", dy2, w_out, NT, BF16, 512, D_MODEL, D_MODEL)
    g_w_out = _mm_simple("out_proj_dw", y, dy2, TN, BF16, D_MODEL, 512, 512)

    rr = _iota((512, 512), 0) // HEAD_DIM
    cc = _iota((512, 512), 1) // HEAD_DIM
    head_sum = (rr == cc).astype(F32)
    dlog, g_mb, dz, dg, do_grp, dp_grp, do_b, do_c = _gate_bwd(
        dy, z, parts, merge_bias, (out_a, out_b, out_c), o_grp, l_grp, wts, head_sum)
    g_wt = [_mm_simple("branch_dw_%d" % b, dz[b], u[b], TN, BF16, D_MODEL, 512, 512) for b in range(3)]

    dqkv = []
    for g, d in enumerate(DILATIONS):
        dq, dk, dv = _attn_bwd("dil_bwd_%d" % g, "dil", parts, parts, parts, 12 * g, 12 * g + 4, 12 * g + 8,
                               do_grp[g], l_grp[g], dp=dp_grp[g], d=d, tabs=tabs[g])
        dqkv += [dq, dk, dv]
    dq_b, dk_b, dv_b, dbias = _attn_bwd("na_bwd", "na", parts, parts, parts, 36, 40, 44, do_b, lse_b, o=out_b,
                                        bias=bias)
    g_rpb_t = _na_bias_bwd(dbias)
    g_rpb = g_rpb_t[:, :15, :31] + jnp.pad(g_rpb_t[:, :14, 64:95], ((0, 0), (1, 0), (0, 0)))
    dq_c, dk_m, dv_m = _attn_bwd("mem_bwd", "mem", parts, kv_m, kv_m, 48, 0, 4, do_c, lse_c, o=out_c)

    dkv = jnp.concatenate([dk_m, dv_m], axis=1).astype(BF16)
    g_w_kv = _mm_simple("mem_kv_dw", memn, dkv, TN, BF16, D_MODEL, 512, MEM_LEN)
    dmemn = _mm_simple("mem_kv_dx", dkv, w_kv, NT, F32, MEM_LEN, 512, D_MODEL)
    g_mem_norm = _memnorm_bwd(mem, dmemn)

    grads = dict(w_kv=g_w_kv, wt_a=g_wt[0], wt_b=g_wt[1], wt_c=g_wt[2], w_out=g_w_out, merge_bias=g_mb,
                 mem_norm=g_mem_norm, post_norm=g_post, na_rpb=g_rpb)
    dep = reduce_start(grads) if reduce_start is not None else None
    dparts = dqkv + [dq_b, dk_b, dv_b, dq_c] + list(dg) + list(dlog)
    grads["wt_in"] = _in_proj_dw(dparts, hst, dep)
    dep = reduce_start(grads) if reduce_start is not None else None
    dh = _in_proj_dh(dparts, wt_in, dep)
    grad_x, grads["pre_norm"] = _prenorm_bwd(x, pre_norm, dh, dout)
    return loss, grad_x, grads


ANY = pl.BlockSpec(memory_space=pl.ANY)


def _place():
    return lax.axis_index("x"), lax.axis_index("y"), lax.axis_index("c")


def _all_gather(shard):
    r = shard.shape[0]
    half = r // 2

    def body(src, out, send_sems, recv_sems, local_sem):
        x, y, c = _place()
        me, sib = (x, y, c), (x, y, 1 - c)
        xn, yn, dg = (1 - x, y, c), (x, 1 - y, c), (1 - x, 1 - y, c)

        def rows(dev, part=None):
            blk = out.at[4 * dev[0] + 2 * dev[1] + dev[2]]
            return blk if part is None else blk.at[pl.ds(part * half, half)]

        def copy(k, dev, part, to, own=False):
            return pltpu.make_async_remote_copy(
                src_ref=src if own else rows(dev, part), dst_ref=rows(dev, part),
                send_sem=send_sems.at[k], recv_sem=recv_sems.at[k], device_id=to, device_id_type=MESH_ID)

        def other(dev):
            return (dev[0], dev[1], 1 - dev[2])

        mine = pltpu.make_async_copy(src, rows(me), local_sem)
        mine.start()
        sent = [copy(0, me, None, sib, own=True), copy(1, me, None, xn, own=True), copy(2, me, None, yn, own=True)]
        for cp in sent:
            cp.start()
        copy(1, xn, None, me).wait_recv()
        sent += [copy(3, xn, 0, yn), copy(5, xn, None, sib)]
        sent[-2].start()
        sent[-1].start()
        copy(2, yn, None, me).wait_recv()
        sent += [copy(4, yn, 1, xn), copy(6, yn, None, sib)]
        sent[-2].start()
        sent[-1].start()
        copy(3, dg, 0, me).wait_recv()
        sent.append(copy(7, dg, 0, sib))
        sent[-1].start()
        copy(4, dg, 1, me).wait_recv()
        sent.append(copy(8, dg, 1, sib))
        sent[-1].start()
        copy(0, sib, None, me).wait_recv()
        copy(5, other(xn), None, me).wait_recv()
        copy(6, other(yn), None, me).wait_recv()
        copy(7, other(dg), 0, me).wait_recv()
        copy(8, other(dg), 1, me).wait_recv()
        for cp in sent:
            cp.wait_send()
        mine.wait()

    return pl.pallas_call(
        body, name="all_gather", in_specs=[ANY], out_specs=ANY,
        out_shape=jax.ShapeDtypeStruct((N_DEV,) + shard.shape, shard.dtype),
        scratch_shapes=[pltpu.SemaphoreType.DMA((9,)), pltpu.SemaphoreType.DMA((9,)), pltpu.SemaphoreType.DMA])(shard)


def _exchange_sibling(name, terms):
    nt = len(terms)

    def body(*refs):
        srcs, outs = refs[:nt], refs[nt:2 * nt]
        send_sems, recv_sems = refs[2 * nt:]
        x, y, c = _place()
        copies = []
        for q in range(4):
            for t in range(nt):
                copies.append(pltpu.make_async_remote_copy(
                    src_ref=srcs[t].at[2 * q + 1 - c], dst_ref=outs[t].at[q],
                    send_sem=send_sems.at[q * nt + t], recv_sem=recv_sems.at[q * nt + t],
                    device_id=(x, y, 1 - c), device_id_type=MESH_ID))
        for cp in copies:
            cp.start()
        for cp in copies:
            cp.wait()

    return pl.pallas_call(
        body, name=name, in_specs=[ANY] * nt, out_specs=[ANY] * nt,
        out_shape=[jax.ShapeDtypeStruct((4,) + s.shape[1:], s.dtype) for s in terms],
        scratch_shapes=[pltpu.SemaphoreType.DMA((4 * nt,)), pltpu.SemaphoreType.DMA((4 * nt,))])(*terms)


def _gather_small(block):
    def body(src, out, send_sems, recv_sems, local_sem):
        x, y, c = _place()
        me = 4 * x + 2 * y + c
        mine = pltpu.make_async_copy(src, out.at[me], local_sem)
        mine.start()
        copies = []
        for mask in range(1, 8):
            fx, fy, fc = (mask >> 2) & 1, (mask >> 1) & 1, mask & 1
            to = (jnp.where(fx, 1 - x, x), jnp.where(fy, 1 - y, y), jnp.where(fc, 1 - c, c))
            copies.append(pltpu.make_async_remote_copy(
                src_ref=src, dst_ref=out.at[me], send_sem=send_sems.at[mask - 1], recv_sem=recv_sems.at[mask - 1],
                device_id=to, device_id_type=MESH_ID))
        for cp in copies:
            cp.start()
        for cp in copies:
            cp.wait()
        mine.wait()

    return pl.pallas_call(
        body, name="gather_small", in_specs=[ANY], out_specs=ANY,
        out_shape=jax.ShapeDtypeStruct((N_DEV,) + block.shape, block.dtype),
        scratch_shapes=[pltpu.SemaphoreType.DMA((7,)), pltpu.SemaphoreType.DMA((7,)), pltpu.SemaphoreType.DMA])(block)


HBM = pl.BlockSpec(memory_space=pltpu.HBM)
SEM = pl.BlockSpec(memory_space=pltpu.SEMAPHORE)
DATAFLOW = pltpu.SideEffectType.DATAFLOW_SIDE_EFFECTING


def _split_copies(kind, srcs, lands, send_sems, recv_sems):
    nt = len(srcs)
    x, y, c = _place()
    copies = []
    if kind == "gather":
        me = 4 * x + 2 * y + c
        for mask in range(1, 8):
            fx, fy, fc = (mask >> 2) & 1, (mask >> 1) & 1, mask & 1
            to = (1 - x if fx else x, 1 - y if fy else y, 1 - c if fc else c)
            for t in range(nt):
                k = (mask - 1) * nt + t
                copies.append(pltpu.make_async_remote_copy(
                    src_ref=srcs[t], dst_ref=lands[t].at[me], send_sem=send_sems.at[k], recv_sem=recv_sems.at[k],
                    device_id=to, device_id_type=MESH_ID))
    else:
        for s, (tx, ty) in enumerate([(1 - x, y), (x, 1 - y), (1 - x, 1 - y)]):
            for t in range(nt):
                k = s * nt + t
                copies.append(pltpu.make_async_remote_copy(
                    src_ref=srcs[t].at[2 * tx + ty], dst_ref=lands[t].at[s], send_sem=send_sems.at[k],
                    recv_sem=recv_sems.at[k], device_id=(tx, ty, c), device_id_type=MESH_ID))
    return copies


def _split_count(kind, nt):
    return (7 if kind == "gather" else 3) * nt


def _exchange_start(name, kind, srcs, land_shapes, after=None):
    nt = len(srcs)
    n = _split_count(kind, nt)
    dep_specs, dep_args = _dep_operand(after)
    nd = len(dep_args)

    def body(*refs):
        src_refs, land_refs = refs[:nt], refs[nt:2 * nt]
        send_sems, recv_sems = refs[2 * nt + nd], refs[2 * nt + nd + 1]
        token = refs[-1]
        for cp in _split_copies(kind, src_refs, land_refs, send_sems, recv_sems):
            cp.start()
        token[...] = jnp.zeros_like(token)

    lands = [pltpu.with_memory_space_constraint(lax.empty(s.shape, s.dtype), pltpu.HBM) for s in land_shapes]
    res = pl.pallas_call(
        body, name=name,
        out_shape=(pltpu.SemaphoreType.DMA((n,)), pltpu.SemaphoreType.DMA((n,)),
                   *[pltpu.HBM(s.shape, s.dtype) for s in srcs], *[pltpu.HBM(s.shape, s.dtype) for s in land_shapes],
                   jax.ShapeDtypeStruct((8, 128), F32)),
        in_specs=[HBM] * (2 * nt) + dep_specs,
        out_specs=(SEM, SEM, *([HBM] * (2 * nt)), pl.BlockSpec(memory_space=pltpu.VMEM)),
        input_output_aliases={i: 2 + i for i in range(2 * nt)},
        compiler_params=pltpu.CompilerParams(has_side_effects=DATAFLOW))(
            *[pltpu.with_memory_space_constraint(s, pltpu.HBM) for s in srcs], *lands, *dep_args)
    return res[0], res[1], list(res[2:2 + nt]), list(res[2 + nt:2 + 2 * nt]), res[-1]


def _exchange_wait(name, kind, send_sems, recv_sems, srcs, lands, after):
    nt = len(srcs)

    def body(*refs):
        src_refs, land_refs = refs[:nt], refs[nt:2 * nt]
        s_sems, r_sems = refs[2 * nt], refs[2 * nt + 1]
        for cp in _split_copies(kind, src_refs, land_refs, s_sems, r_sems):
            cp.wait_send()
            cp.wait_recv()

    res = pl.pallas_call(
        body, name=name,
        out_shape=tuple(pltpu.HBM(s.shape, s.dtype) for s in list(srcs) + list(lands)),
        in_specs=[HBM] * (2 * nt) + [SEM, SEM, pl.BlockSpec(memory_space=pl.ANY)],
        out_specs=tuple([HBM] * (2 * nt)),
        input_output_aliases={i: i for i in range(2 * nt)},
        compiler_params=pltpu.CompilerParams(has_side_effects=DATAFLOW))(
            *srcs, *lands, send_sems, recv_sems, after)
    return list(res[:nt]), list(res[nt:])


def _add_sibling(name, term, recv, rows):
    _, r, w = term.shape
    cidx = lax.axis_index("c").astype(jnp.int32).reshape(1)

    def body(c_ref, a_ref, b_ref, o_ref):
        o_ref[...] = (a_ref[...].astype(F32) + b_ref[...].astype(F32)).astype(o_ref.dtype)

    grid_spec = pltpu.PrefetchScalarGridSpec(
        num_scalar_prefetch=1, grid=(4, r // rows),
        in_specs=[pl.BlockSpec((None, rows, w), lambda q, i, c_ref: (2 * q + c_ref[0], i, 0)),
                  pl.BlockSpec((None, rows, w), lambda q, i, c_ref: (q, i, 0))],
        out_specs=pl.BlockSpec((None, rows, w), lambda q, i, c_ref: (q, i, 0)))
    return pl.pallas_call(
        body, name=name, grid_spec=grid_spec, out_shape=jax.ShapeDtypeStruct((4, r, w), term.dtype),
        compiler_params=_params(("parallel", "parallel")))(cidx, term, recv)


def _add_chips(name, sums, recv, rows):
    _, r, w = sums.shape
    qidx = (2 * lax.axis_index("x") + lax.axis_index("y")).astype(jnp.int32).reshape(1)

    def body(q_ref, a_ref, b_ref, o_ref):
        o_ref[...] = ((a_ref[...].astype(F32) + b_ref[0].astype(F32))
                      + (b_ref[1].astype(F32) + b_ref[2].astype(F32)))

    grid_spec = pltpu.PrefetchScalarGridSpec(
        num_scalar_prefetch=1, grid=(r // rows,),
        in_specs=[pl.BlockSpec((None, rows, w), lambda i, q_ref: (q_ref[0], i, 0)),
                  pl.BlockSpec((3, rows, w), lambda i, q_ref: (0, i, 0))],
        out_specs=pl.BlockSpec((rows, w), lambda i, q_ref: (i, 0)))
    return pl.pallas_call(
        body, name=name, grid_spec=grid_spec, out_shape=jax.ShapeDtypeStruct((r, w), F32),
        compiler_params=_params(("parallel",)))(qidx, sums, recv)


def _rs_rows(a):
    return SHARD_IN // 4 if a.shape[1] == SHARD_IN else a.shape[1]


def _reduce_scatter_start(tag, names, terms):
    recv1 = _exchange_sibling("exchange_sibling_" + tag, terms)
    sums = [_add_sibling("add_sibling_" + n, t, r, _rs_rows(t)) for n, t, r in zip(names, terms, recv1)]
    lands = [jax.ShapeDtypeStruct((3,) + s.shape[1:], s.dtype) for s in sums]
    send_sems, recv_sems, sums, lands, token = _exchange_start("exchange_chips_start_" + tag, "chips", sums, lands)
    return (tag, names, send_sems, recv_sems, sums, lands), token


def _reduce_scatter_finish(state, after):
    tag, names, send_sems, recv_sems, sums, lands = state
    sums, recv2 = _exchange_wait("exchange_chips_wait_" + tag, "chips", send_sems, recv_sems, sums, lands, after)
    return [_add_chips("add_chips_" + n, s, r, _rs_rows(s)) for n, s, r in zip(names, sums, recv2)]


def _adamw(name, w, g, m, v, rows=None):
    r, c = w.shape
    rows = r if rows is None else rows
    c1 = 1.0 - ADAM_B1 ** ADAM_STEP
    c2 = 1.0 - ADAM_B2 ** ADAM_STEP

    def body(w_ref, g_ref, m_ref, v_ref, d_ref, nm_ref, nv_ref):
        gv = g_ref[...]
        nm = ADAM_B1 * m_ref[...] + (1.0 - ADAM_B1) * gv
        nv = ADAM_B2 * v_ref[...] + (1.0 - ADAM_B2) * (gv * gv)
        nm_ref[...] = nm
        nv_ref[...] = nv
        d_ref[...] = -ADAM_LR * ((nm / c1) / (jnp.sqrt(nv / c2) + ADAM_EPS) + ADAM_WD * w_ref[...])

    spec = pl.BlockSpec((rows, c), lambda i: (i, 0))
    return pl.pallas_call(
        body, name=name, grid=(r // rows,), in_specs=[spec] * 4, out_specs=[spec] * 3,
        out_shape=[jax.ShapeDtypeStruct((r, c), F32)] * 3, compiler_params=_params(("parallel",)))(w, g, m, v)


def _sum_devices(gathered):
    def body(g_ref, o_ref):
        acc = g_ref[0]
        for j in range(1, N_DEV):
            acc = acc + g_ref[j]
        o_ref[...] = acc

    return pl.pallas_call(
        body, name="sum_devices", out_shape=jax.ShapeDtypeStruct(gathered.shape[1:], F32),
        compiler_params=_params())(gathered)


def _rows128(a, rows):
    flat = a.reshape(-1)
    return jnp.pad(flat, (0, rows * 128 - flat.shape[0])).reshape(rows, 128)


def kernel(x, mem, pre_norm, w_in, merge_bias, na_rpb, mem_norm, w_mem_kv, w_branch_a, w_branch_b, w_branch_c, w_out, post_norm, loss_target, m_pre_norm, m_w_in, m_merge_bias, m_na_rpb, m_mem_norm, m_w_mem_kv, m_w_branch_a, m_w_branch_b, m_w_branch_c, m_w_out, m_post_norm, v_pre_norm, v_w_in, v_merge_bias, v_na_rpb, v_mem_norm, v_w_mem_kv, v_w_branch_a, v_w_branch_b, v_w_branch_c, v_w_out, v_post_norm):
    wt_in_s = w_in[0].T.astype(BF16)
    rows_s = jnp.concatenate([w_mem_kv[0], w_out[0]], axis=0).astype(BF16)
    cols_s = jnp.concatenate([w_branch_a[0].T, w_branch_b[0].T, w_branch_c[0].T], axis=0).astype(BF16)
    mb_s = jnp.pad(merge_bias[0], ((0, 5), (0, 0)))
    wt_in = _all_gather(wt_in_s).reshape(N_IN, D_MODEL)

    late_own = [rows_s, cols_s, mb_s]
    late_lands = [jax.ShapeDtypeStruct((N_DEV,) + s.shape, s.dtype) for s in late_own]
    l_send, l_recv, late_own, late_lands, late_token = _exchange_start("gather_late_start", "gather", late_own,
                                                                       late_lands, after=wt_in)
    me = 4 * lax.axis_index("x") + 2 * lax.axis_index("y") + lax.axis_index("c")

    def late_weights(after):
        own, lands = _exchange_wait("gather_late_wait", "gather", l_send, l_recv, late_own, late_lands, after)
        g_rows, g_cols, g_mb = [lax.dynamic_update_slice(land, o[None], (me, 0, 0)) for land, o in zip(lands, own)]
        return (g_mb[:, :3].transpose(1, 0, 2).reshape(3, D_MODEL),
                g_rows[:, :128].reshape(D_MODEL, D_MODEL), g_cols[:, 0:128].reshape(D_MODEL, 512),
                g_cols[:, 128:256].reshape(D_MODEL, 512), g_cols[:, 256:384].reshape(D_MODEL, 512),
                g_rows[:, 128:].reshape(D_MODEL, D_MODEL))

    rs_state = []

    def reduce_start(grads):
        if "wt_in" in grads:
            state, token = _reduce_scatter_start("w_in", ["w_in"],
                                                 [grads["wt_in"].reshape(N_DEV, SHARD_IN, D_MODEL)])
        else:
            gmb_t = jnp.pad(grads["merge_bias"].reshape(3, N_DEV, 128).transpose(1, 0, 2), ((0, 0), (0, 5), (0, 0)))
            names = ["w_kv", "w_out", "a", "b", "c", "mb"]
            terms = [grads["w_kv"].reshape(N_DEV, 128, D_MODEL), grads["w_out"].reshape(N_DEV, 128, D_MODEL),
                     grads["wt_a"].reshape(N_DEV, 128, 512), grads["wt_b"].reshape(N_DEV, 128, 512),
                     grads["wt_c"].reshape(N_DEV, 128, 512), gmb_t]
            state, token = _reduce_scatter_start("rest", names, terms)
        rs_state.append(state)
        return token

    loss_term, grad_x, grads = _local_step(
        x[0], mem[0], loss_target[0], pre_norm, mem_norm, post_norm, na_rpb[0], wt_in, late_weights,
        dep_in=late_token, reduce_start=reduce_start)

    small = jnp.concatenate([_rows128(grads["pre_norm"], 8), _rows128(grads["mem_norm"], 8),
                             _rows128(grads["post_norm"], 8), _rows128(grads["na_rpb"], 32),
                             _rows128(loss_term, 8)], axis=0)
    total = _sum_devices(_gather_small(small))
    loss = total[56, 0]
    grad = {"pre_norm": total[0:8].reshape(1, D_MODEL), "mem_norm": total[8:16].reshape(1, D_MODEL),
            "post_norm": total[16:24].reshape(1, D_MODEL),
            "na_rpb": total[24:56].reshape(-1)[:8 * 15 * 31].reshape(1, 8, 15, 31)}
    weights = {
        "pre_norm": (pre_norm, m_pre_norm, v_pre_norm), "w_in": (w_in, m_w_in, v_w_in),
        "merge_bias": (merge_bias, m_merge_bias, v_merge_bias), "na_rpb": (na_rpb, m_na_rpb, v_na_rpb),
        "mem_norm": (mem_norm, m_mem_norm, v_mem_norm), "w_mem_kv": (w_mem_kv, m_w_mem_kv, v_w_mem_kv),
        "w_branch_a": (w_branch_a, m_w_branch_a, v_w_branch_a), "w_branch_b": (w_branch_b, m_w_branch_b, v_w_branch_b),
        "w_branch_c": (w_branch_c, m_w_branch_c, v_w_branch_c), "w_out": (w_out, m_w_out, v_w_out),
        "post_norm": (post_norm, m_post_norm, v_post_norm)}
    order = ["pre_norm", "w_in", "merge_bias", "na_rpb", "mem_norm", "w_mem_kv", "w_branch_a", "w_branch_b",
             "w_branch_c", "w_out", "post_norm"]
    delta, new_m, new_v = {}, {}, {}

    def update(n):
        w, m, v = weights[n]
        shape = w.shape
        two_d = (-1, shape[-1])
        rows = 256 if n == "w_in" else None
        dl, nm, nv = _adamw("adamw_" + n, w.reshape(two_d), grad[n].reshape(two_d), m.reshape(two_d),
                            v.reshape(two_d), rows)
        delta[n], new_m[n], new_v[n] = dl.reshape(shape), nm.reshape(shape), nv.reshape(shape)

    for n in ("pre_norm", "na_rpb", "mem_norm", "post_norm"):
        update(n)
    g_kv, g_out, gt_a, gt_b, gt_c, g_mb8 = _reduce_scatter_finish(rs_state[0], delta["post_norm"])
    grad.update({"merge_bias": g_mb8[:3][None], "w_mem_kv": g_kv[None], "w_branch_a": gt_a.T[None],
                 "w_branch_b": gt_b.T[None], "w_branch_c": gt_c.T[None], "w_out": g_out[None]})
    for n in ("merge_bias", "w_mem_kv", "w_branch_a", "w_branch_b", "w_branch_c", "w_out"):
        update(n)
    (gt_in,) = _reduce_scatter_finish(rs_state[1], delta["w_out"])
    grad["w_in"] = gt_in.T[None]
    update("w_in")

    return (loss, grad_x[None], *[grad[n] for n in order], *[delta[n] for n in order],
            *[new_m[n] for n in order], *[new_v[n] for n in order])
```

```python
import functools

import numpy as np
import jax
import jax.numpy as jnp
from jax import lax
from jax.experimental import pallas as pl
from jax.experimental.pallas import tpu as pltpu

F32 = jnp.float32
BF16 = jnp.bfloat16

SEQ = 2048
D_MODEL = 1024
N_IN = 11264
N_DEV = 8
SHARD_IN = N_IN // N_DEV
HEAD_DIM = 64
GRID_W = 64
NA_ROWS = 8
MEM_LEN = 256
DILATIONS = (1, 4, 16)
REACH = 64
ROPE_THETA = 500000.0
ROPE_DIM = 16
EPS = 1e-6
NEG = -1e30
ADAM_LR = 0.001
ADAM_B1 = 0.9
ADAM_B2 = 0.999
ADAM_EPS = 1e-08
ADAM_WD = 0.01
ADAM_STEP = 10

VMEM_LIMIT_BYTES = 56 * 1024 * 1024
MESH_ID = pl.DeviceIdType.MESH

NN = (((1,), (0,)), ((), ()))
NT = (((1,), (1,)), ((), ()))
TN = (((0,), (0,)), ((), ()))


def _params(sem=None):
    return pltpu.CompilerParams(dimension_semantics=sem, vmem_limit_bytes=VMEM_LIMIT_BYTES)


def _iota(shape, dim):
    return lax.broadcasted_iota(jnp.int32, shape, dim)


def _sigmoid(x):
    return 1.0 / (1.0 + jnp.exp(-x))


def _rope_tables():
    half = ROPE_DIM // 2
    inv = (ROPE_THETA ** (-np.arange(half, dtype=np.float64) * 2.0 / ROPE_DIM)).astype(np.float32)
    pos = np.arange(SEQ, dtype=np.float32)
    ang = pos[:, None] * inv[None, :]
    cos, sin = np.cos(ang), np.sin(ang)
    zeros = np.zeros_like(cos)
    rest = HEAD_DIM - ROPE_DIM
    c64 = np.concatenate([cos, cos, np.ones((SEQ, rest), np.float32)], axis=1)
    s1 = np.concatenate([zeros, sin, np.zeros((SEQ, rest), np.float32)], axis=1)
    s2 = np.concatenate([-sin, zeros, np.zeros((SEQ, rest), np.float32)], axis=1)

    def fold(t, d):
        return t.reshape(SEQ // d, d, t.shape[1]).transpose(1, 0, 2).reshape(SEQ, t.shape[1])

    tabs = [np.stack([np.tile(fold(t, d), (1, 2)) for t in (c64, s1, s2)], axis=0) for d in DILATIONS]
    return jnp.asarray(np.stack(tabs, axis=0), dtype=F32)


def _rope(a, c, s1, s2):
    return a * c + pltpu.roll(a, 8, 1) * s1 + pltpu.roll(a, 120, 1) * s2


def _rope_t(a, c, s1, s2):
    return a * c + pltpu.roll(a * s1, 120, 1) + pltpu.roll(a * s2, 8, 1)


def _perm_of_block(j):
    return jnp.where(j < 3, 0, jnp.where(j < 6, 1, jnp.where(j < 9, 2, 0)))


def _mm(name, a, b, out_shape, out_dtype, grid, a_spec, b_spec, o_spec, acc_shape, dims, k_axis, nk):
    def body(a_ref, b_ref, o_ref, acc_ref):
        k = pl.program_id(k_axis)

        @pl.when(k == 0)
        def _():
            acc_ref[...] = jnp.zeros(acc_shape, F32)

        acc_ref[...] += lax.dot_general(a_ref[...], b_ref[...], dims, preferred_element_type=F32)

        @pl.when(k == nk - 1)
        def _():
            o_ref[...] = acc_ref[...].astype(out_dtype)

    sem = tuple("arbitrary" if ax == k_axis else "parallel" for ax in range(len(grid)))
    return pl.pallas_call(
        body, name=name, grid=grid, in_specs=[a_spec, b_spec], out_specs=o_spec,
        out_shape=jax.ShapeDtypeStruct(out_shape, out_dtype),
        scratch_shapes=[pltpu.VMEM(acc_shape, F32)], compiler_params=_params(sem))(a, b)


def _mm_simple(name, a, b, dims, out_dtype, tm, tn, tk):
    if dims is NN:
        m, kk = a.shape
        n = b.shape[1]
        a_spec = pl.BlockSpec((tm, tk), lambda i, j, k: (i, k))
        b_spec = pl.BlockSpec((tk, tn), lambda i, j, k: (k, j))
    elif dims is NT:
        m, kk = a.shape
        n = b.shape[0]
        a_spec = pl.BlockSpec((tm, tk), lambda i, j, k: (i, k))
        b_spec = pl.BlockSpec((tn, tk), lambda i, j, k: (j, k))
    else:
        kk, m = a.shape
        n = b.shape[1]
        a_spec = pl.BlockSpec((tk, tm), lambda i, j, k: (k, i))
        b_spec = pl.BlockSpec((tk, tn), lambda i, j, k: (k, j))
    grid = (m // tm, n // tn, kk // tk)
    o_spec = pl.BlockSpec((tm, tn), lambda i, j, k: (i, j))
    return _mm(name, a, b, (m, n), out_dtype, grid, a_spec, b_spec, o_spec, (tm, tn), dims, 2, kk // tk)


def _rmsnorm_fwd(name, x, gain, rows):
    n, d = x.shape

    def body(x_ref, g_ref, o_ref):
        xv = x_ref[...]
        rstd = lax.rsqrt(jnp.mean(xv * xv, axis=1, keepdims=True) + EPS)
        o_ref[...] = (xv * rstd * g_ref[...]).astype(BF16)

    return pl.pallas_call(
        body, name=name, grid=(n // rows,),
        in_specs=[pl.BlockSpec((rows, d), lambda i: (i, 0)), pl.BlockSpec((1, d), lambda i: (0, 0))],
        out_specs=pl.BlockSpec((rows, d), lambda i: (i, 0)),
        out_shape=jax.ShapeDtypeStruct((n, d), BF16), compiler_params=_params(("parallel",)))(x, gain)


def _folded_rows(first, rows, d):
    if d == 1:
        return pl.ds(pl.multiple_of(first, rows), rows)
    mlen = SEQ // d
    return pl.ds((first % mlen) * d + first // mlen, rows, stride=d)


def _prenorm_fold(x, gain):
    rows = 128

    nchunk = D_MODEL // 128

    def body(*refs):
        x_refs, g_ref, hs_ref, hst_ref = refs[:nchunk], refs[nchunk], refs[nchunk + 1], refs[nchunk + 2]
        first = pl.program_id(0) * rows
        for p, d in enumerate(DILATIONS):
            idx = _folded_rows(first, rows, d)
            xv = jnp.concatenate([r[idx, :] for r in x_refs], axis=1)
            rstd = lax.rsqrt(jnp.mean(xv * xv, axis=1, keepdims=True) + EPS)
            h = xv * rstd * g_ref[...]
            hs_ref[p] = h.astype(BF16)
            hst_ref[p] = h.T.astype(BF16)

    x_specs = [pl.BlockSpec((SEQ, 128), functools.partial(lambda c, i: (0, c), c)) for c in range(nchunk)]
    return pl.pallas_call(
        body, name="prenorm", grid=(SEQ // rows,),
        in_specs=x_specs + [pl.BlockSpec((1, D_MODEL), lambda i: (0, 0))],
        out_specs=[pl.BlockSpec((3, rows, D_MODEL), lambda i: (0, i, 0)),
                   pl.BlockSpec((3, D_MODEL, rows), lambda i: (0, 0, i))],
        out_shape=[jax.ShapeDtypeStruct((3, SEQ, D_MODEL), BF16), jax.ShapeDtypeStruct((3, D_MODEL, SEQ), BF16)],
        compiler_params=_params(("parallel",)))(*([x] * nchunk), gain)


def _prenorm_bwd(x, gain, dh, dout):
    rows = 256

    def body(x_ref, g_ref, a_ref, do_ref, dx_ref, gg_ref):
        xv = x_ref[...]
        rstd = lax.rsqrt(jnp.mean(xv * xv, axis=1, keepdims=True) + EPS)
        xn = xv * rstd
        dh = jnp.concatenate([a_ref[c] for c in range(D_MODEL // 128)], axis=1)
        gdh = dh * g_ref[...]
        dx_ref[...] = rstd * (gdh - xn * jnp.mean(gdh * xn, axis=1, keepdims=True)) + do_ref[...]

        @pl.when(pl.program_id(0) == 0)
        def _():
            gg_ref[...] = jnp.zeros((1, D_MODEL), F32)

        gg_ref[...] += jnp.sum(dh * xn, axis=0, keepdims=True)

    row = pl.BlockSpec((rows, D_MODEL), lambda i: (i, 0))
    vec = pl.BlockSpec((1, D_MODEL), lambda i: (0, 0))
    return pl.pallas_call(
        body, name="prenorm_bwd", grid=(SEQ // rows,),
        in_specs=[row, vec, pl.BlockSpec((D_MODEL // 128, rows, 128), lambda i: (0, i, 0)), row], out_specs=[row, vec],
        out_shape=[jax.ShapeDtypeStruct((SEQ, D_MODEL), F32), jax.ShapeDtypeStruct((1, D_MODEL), F32)],
        compiler_params=_params(("arbitrary",)))(x, gain, dh, dout)


def _memnorm_bwd(mem, dmemn):
    def body(m_ref, d_ref, gg_ref):
        mv = m_ref[...]
        rstd = lax.rsqrt(jnp.mean(mv * mv, axis=1, keepdims=True) + EPS)
        gg_ref[...] = jnp.sum(d_ref[...] * mv * rstd, axis=0, keepdims=True)

    return pl.pallas_call(
        body, name="memnorm_bwd", out_shape=jax.ShapeDtypeStruct((1, D_MODEL), F32),
        compiler_params=_params())(mem, dmemn)


def _dep_operand(dep):
    return ([], []) if dep is None else ([pl.BlockSpec(memory_space=pl.ANY)], [dep])


def _in_proj(hs, wt, tabs, dep=None):
    tm, tn = 1024, 512
    dep_specs, dep_args = _dep_operand(dep)

    def body(h_ref, w_ref, t_ref, *rest):
        o_ref = rest[-1]
        j = pl.program_id(0)
        is_rope = jnp.logical_and(j < 9, j % 3 != 2)
        row_slices = [slice(r * tm, (r + 1) * tm) for r in range(SEQ // tm)]

        def product(rs):
            return lax.dot_general(h_ref[rs, :], w_ref[...], NT, preferred_element_type=F32)

        @pl.when(is_rope)
        def _():
            for rs in row_slices:
                acc = product(rs)
                c, s1, s2 = t_ref[0, rs, :], t_ref[1, rs, :], t_ref[2, rs, :]
                for q in range(tn // 128):
                    a = acc[:, q * 128:(q + 1) * 128]
                    o_ref[rs, q * 128:(q + 1) * 128] = _rope(a, c, s1, s2).astype(BF16)

        @pl.when(jnp.logical_not(is_rope))
        def _():
            for rs in row_slices:
                o_ref[rs, :] = product(rs).astype(BF16)

    return pl.pallas_call(
        body, name="in_proj", grid=(N_IN // tn,),
        in_specs=[pl.BlockSpec((None, SEQ, D_MODEL), lambda j: (_perm_of_block(j), 0, 0)),
                  pl.BlockSpec((tn, D_MODEL), lambda j: (j, 0)),
                  pl.BlockSpec((None, 3, SEQ, 128), lambda j: (_perm_of_block(j), 0, 0, 0))] + dep_specs,
        out_specs=pl.BlockSpec((SEQ, tn), lambda j: (0, j)),
        out_shape=jax.ShapeDtypeStruct((SEQ, N_IN), BF16),
        compiler_params=_params(("parallel",)))(hs, wt, tabs, *dep_args)


def _piece_blocks(pieces):
    return [(a, h * 512) for a, p in enumerate(pieces) for h in range(p.shape[1] // 512)]


def _block_fetch(piece_refs, blocks, buf, sem):
    def start(block, slot):
        for b, (a, col) in enumerate(blocks):
            @pl.when(block == b)
            def _():
                pltpu.make_async_copy(piece_refs[a].at[:, pl.ds(col, 512)], buf.at[slot], sem.at[slot]).start()

    def wait(slot):
        pltpu.make_async_copy(piece_refs[0].at[:, pl.ds(0, 512)], buf.at[slot], sem.at[slot]).wait()

    return start, wait


def _in_proj_dw(pieces, hst, dep=None):
    tn = 512
    blocks = _piece_blocks(pieces)
    nblk = len(blocks)
    npc = len(pieces)
    dep_specs, dep_args = _dep_operand(dep)

    def body(h_ref, *rest):
        piece_refs = rest[:npc]
        o_ref, buf, sem = rest[-3:]
        j = pl.program_id(0)
        slot = j % 2
        start, wait = _block_fetch(piece_refs, blocks, buf, sem)

        @pl.when(j == 0)
        def _():
            start(j, slot)

        wait(slot)

        @pl.when(j + 1 < nblk)
        def _():
            start(j + 1, 1 - slot)

        acc = jnp.dot(h_ref[...], buf[slot], preferred_element_type=F32)
        o_ref[...] = acc.T.astype(BF16)

    return pl.pallas_call(
        body, name="in_proj_dw", grid=(nblk,),
        in_specs=[pl.BlockSpec((None, D_MODEL, SEQ), lambda j: (_perm_of_block(j), 0, 0))] + [ANY] * npc + dep_specs,
        out_specs=pl.BlockSpec((tn, D_MODEL), lambda j: (j, 0)),
        out_shape=jax.ShapeDtypeStruct((N_IN, D_MODEL), BF16),
        scratch_shapes=[pltpu.VMEM((2, SEQ, tn), BF16), pltpu.SemaphoreType.DMA((2,))],
        compiler_params=_params(("arbitrary",)))(hst, *pieces, *dep_args)


def _in_proj_dh(pieces, wt, dep=None):
    tk = 512
    blocks = _piece_blocks(pieces)
    nblk = len(blocks)
    npc = len(pieces)
    nchunk = D_MODEL // 128

    def col(s):
        return jnp.where(s < 3, s, jnp.where(s < 16, s + 6, s - 13))

    dep_specs, dep_args = _dep_operand(dep)

    def body(w_ref, *rest):
        piece_refs = rest[:npc]
        o_ref, acc_ref, buf, sem = rest[-4:]
        s = pl.program_id(0)
        slot = s % 2
        start, wait = _block_fetch(piece_refs, blocks, buf, sem)

        @pl.when(s == 0)
        def _():
            start(col(s), slot)

        wait(slot)

        @pl.when(s + 1 < nblk)
        def _():
            start(col(s + 1), 1 - slot)

        row_slices = [slice(r * 1024, (r + 1) * 1024) for r in range(SEQ // 1024)]

        def product(rs):
            return jnp.dot(buf[slot, rs, :], w_ref[...], preferred_element_type=F32)

        def accumulate(cond, to_out, init):
            @pl.when(cond)
            def _():
                for rs in row_slices:
                    prod = product(rs)
                    if not to_out:
                        if init:
                            acc_ref[rs, :] = prod
                        else:
                            acc_ref[rs, :] += prod
                        continue
                    for c in range(nchunk):
                        if init:
                            o_ref[c, rs, :] = prod[:, c * 128:(c + 1) * 128]
                        else:
                            o_ref[c, rs, :] += prod[:, c * 128:(c + 1) * 128]

        accumulate(s == 0, True, True)
        accumulate(jnp.logical_and(s > 0, s < 16), True, False)
        accumulate(jnp.logical_or(s == 16, s == 19), False, True)
        accumulate(jnp.logical_and(s > 16, s != 19), False, False)
        for last, d in ((18, 4), (21, 16)):
            @pl.when(s == last)
            def _():
                mlen = SEQ // d
                for r in range(d):
                    for c in range(nchunk):
                        o_ref[c, pl.ds(r, mlen, stride=d), :] += acc_ref[r * mlen:(r + 1) * mlen,
                                                                         c * 128:(c + 1) * 128]

    return pl.pallas_call(
        body, name="in_proj_dh", grid=(nblk,),
        in_specs=[pl.BlockSpec((tk, D_MODEL), lambda s: (col(s), 0))] + [ANY] * npc + dep_specs,
        out_specs=pl.BlockSpec((nchunk, SEQ, 128), lambda s: (0, 0, 0)),
        out_shape=jax.ShapeDtypeStruct((nchunk, SEQ, 128), F32),
        scratch_shapes=[pltpu.VMEM((SEQ, D_MODEL), F32), pltpu.VMEM((2, SEQ, tk), BF16),
                        pltpu.SemaphoreType.DMA((2,))],
        compiler_params=_params(("arbitrary",)))(wt, *pieces, *dep_args)


def _head_lanes(lanes, hh):
    return lanes >= 64 if hh == 1 else lanes < 64


def _head_rows(x, lanes, hh, pair):
    if not pair:
        return jnp.max(x, axis=1, keepdims=True)
    return jnp.max(jnp.where(_head_lanes(lanes, hh), x, -jnp.inf), axis=1, keepdims=True)


def _mask_head(x, lanes, hh, pair):
    if not pair:
        return x
    return jnp.where(_head_lanes(lanes, hh), x.astype(F32), 0.0).astype(BF16)


def _merge_heads(parts, lanes, pair):
    if not pair:
        return parts[0]
    return jnp.where(lanes < 64, parts[0], parts[1])


def _window(mode, qi, tq, mlen, tk):
    if mode == "dil":
        q0 = qi * tq
        seg = (q0 // mlen) * mlen
        ks = jnp.clip(q0 - REACH, seg, seg + mlen - tk)
        return pl.multiple_of(ks, 64)
    if mode == "na":
        r_start = jnp.clip(qi - NA_ROWS // 2, 0, SEQ // GRID_W - NA_ROWS)
        return pl.multiple_of(r_start * GRID_W, 64)
    return 0


def _scores(mode, qh, k, scale, qi, tq, tk, ks, bias_ref, hh):
    s = lax.dot_general(qh, k, NT, preferred_element_type=F32) * scale
    if mode == "dil":
        qpos = qi * tq + _iota((tq, tk), 0)
        kpos = ks + _iota((tq, tk), 1)
        s = jnp.where(jnp.abs(qpos - kpos) <= REACH, s, NEG)
    elif mode == "na":
        off = qi - jnp.clip(qi - NA_ROWS // 2, 0, SEQ // GRID_W - NA_ROWS)
        s = s + bias_ref[hh, off]
    return s


def _attn_cfg(mode, d):
    if mode == "dil":
        mlen = SEQ // d
        return dict(pair=True, tq=128, tk=min(256, mlen), mlen=mlen, lk=SEQ, scale=HEAD_DIM ** -0.5, units=4,
                    nsub=ATTN_SUBTILES)
    if mode == "na":
        return dict(pair=True, tq=GRID_W, tk=NA_ROWS * GRID_W, mlen=SEQ, lk=SEQ, scale=HEAD_DIM ** -0.5, units=4,
                    nsub=ATTN_SUBTILES)
    return dict(pair=False, tq=128, tk=MEM_LEN, mlen=SEQ, lk=MEM_LEN, scale=128 ** -0.5, units=4,
                nsub=ATTN_SUBTILES)


ATTN_SUBTILES = 4


def _attn_fwd(name, mode, q_arr, k_arr, v_arr, qcol, kcol, vcol, d=1, bias=None):
    cfg = _attn_cfg(mode, d)
    pair, tq, tk, mlen, lk, scale = cfg["pair"], cfg["tq"], cfg["tk"], cfg["mlen"], cfg["lk"], cfg["scale"]
    nh = 2 if pair else 1
    nsub = cfg["nsub"]
    rows = nsub * tq

    def body(*refs):
        if mode == "na":
            q_ref, k_ref, v_ref, bias_ref, o_ref, l_ref = refs
        else:
            q_ref, k_ref, v_ref, o_ref, l_ref = refs
            bias_ref = None
        lanes = _iota((tq, 128), 1)
        chains = [(sub, hh) for sub in range(nsub) for hh in range(nh)]
        qis = [pl.program_id(1) * nsub + sub for sub in range(nsub)]
        kss = [_window(mode, qi, tq, mlen, tk) for qi in qis]
        vs = [v_ref[pl.ds(ks, tk), :] for ks in kss]
        ss = []
        for sub, hh in chains:
            q = q_ref[sub * tq:(sub + 1) * tq, :]
            k = k_ref[pl.ds(kss[sub], tk), :]
            ss.append(_scores(mode, _mask_head(q, lanes, hh, pair), k, scale, qis[sub], tq, tk, kss[sub], bias_ref, hh))
        ms = [jnp.max(s, axis=1, keepdims=True) for s in ss]
        ps = [jnp.exp(s - m) for s, m in zip(ss, ms)]
        ls = [jnp.sum(p, axis=1, keepdims=True) for p in ps]
        os_ = [jnp.dot(p.astype(BF16), vs[sub], preferred_element_type=F32) for p, (sub, hh) in zip(ps, chains)]
        for sub in range(nsub):
            sel = [i for i, (s_, hh) in enumerate(chains) if s_ == sub]
            outs = [os_[i] / ls[i] for i in sel]
            lses = [jnp.broadcast_to(ms[i] + jnp.log(ls[i]), (tq, 128)) for i in sel]
            dst = _folded_rows(qis[sub] * tq, tq, d) if mode == "dil" else slice(sub * tq, (sub + 1) * tq)
            o_ref[dst, :] = _merge_heads(outs, lanes, pair)
            l_ref[dst, :] = _merge_heads(lses, lanes, pair)

    in_specs = [pl.BlockSpec((rows, 128), lambda u, i: (i, qcol + u)),
                pl.BlockSpec((lk, 128), lambda u, i: (0, kcol + u)),
                pl.BlockSpec((lk, 128), lambda u, i: (0, vcol + u))]
    args = [q_arr, k_arr, v_arr]
    if mode == "na":
        in_specs.append(pl.BlockSpec((2, NA_ROWS, GRID_W, NA_ROWS * GRID_W), lambda u, i: (u, 0, 0, 0)))
        args.append(bias)
    if mode == "dil":
        out_spec = pl.BlockSpec((SEQ, 128), lambda u, i: (0, u))
    else:
        out_spec = pl.BlockSpec((rows, 128), lambda u, i: (i, u))
    return pl.pallas_call(
        body, name=name, grid=(cfg["units"], SEQ // rows), in_specs=in_specs, out_specs=[out_spec, out_spec],
        out_shape=[jax.ShapeDtypeStruct((SEQ, 512), F32), jax.ShapeDtypeStruct((SEQ, 512), F32)],
        compiler_params=_params(("parallel", "arbitrary")))(*args)


def _attn_bwd(name, mode, q_arr, k_arr, v_arr, qcol, kcol, vcol, do, lse, dp=None, o=None, d=1, bias=None,
              tabs=None):
    cfg = _attn_cfg(mode, d)
    pair, tq, tk, mlen, lk, scale = cfg["pair"], cfg["tq"], cfg["tk"], cfg["mlen"], cfg["lk"], cfg["scale"]
    nh = 2 if pair else 1
    nsub = cfg["nsub"]
    rows = nsub * tq
    nq = SEQ // rows
    kv_dtype = F32 if mode == "mem" else BF16

    def body(*refs):
        refs = list(refs)
        q_ref, k_ref, v_ref, do_ref, l_ref = refs[:5]
        rest = refs[5:]
        bias_ref = tq_ref = tk_ref = db_ref = None
        if mode == "dil":
            dp_ref, tq_ref, tk_ref, dq_ref, dk_ref, dv_ref, dk_acc, dv_acc = rest
        elif mode == "na":
            o_ref, bias_ref, dq_ref, dk_ref, dv_ref, db_ref, dk_acc, dv_acc = rest
        else:
            o_ref, dq_ref, dk_ref, dv_ref, dk_acc, dv_acc = rest
        step = pl.program_id(1)

        @pl.when(step == 0)
        def _():
            dk_acc[...] = jnp.zeros((lk, 128), F32)
            dv_acc[...] = jnp.zeros((lk, 128), F32)
            if mode == "na":
                db_ref[...] = jnp.zeros(db_ref.shape, F32)

        lanes = _iota((tq, 128), 1)
        lanes_k = _iota((tk, 128), 1)
        chains = [(sub, hh) for sub in range(nsub) for hh in range(nh)]
        qis = [step * nsub + sub for sub in range(nsub)]
        sls = [slice(sub * tq, (sub + 1) * tq) for sub in range(nsub)]
        kss = [_window(mode, qi, tq, mlen, tk) for qi in qis]
        qs = [q_ref[sl, :] for sl in sls]
        ks_ = [k_ref[pl.ds(ks, tk), :] for ks in kss]
        vs = [v_ref[pl.ds(ks, tk), :] for ks in kss]
        dovs, lsevs, dpvs = [], [], []
        for sub in range(nsub):
            if mode == "dil":
                src = _folded_rows(qis[sub] * tq, tq, d)
                dovs.append(do_ref[src, :].astype(BF16))
                lsevs.append(l_ref[src, :])
                dpvs.append(dp_ref[src, :])
            else:
                dovs.append(do_ref[sls[sub], :])
                lsevs.append(l_ref[sls[sub], :])
                dpvs.append(dovs[sub].astype(F32) * o_ref[sls[sub], :])
        ss = [_scores(mode, _mask_head(qs[sub], lanes, hh, pair), ks_[sub], scale, qis[sub], tq, tk, kss[sub],
                      bias_ref, hh) for sub, hh in chains]
        dpms = [lax.dot_general(_mask_head(dovs[sub], lanes, hh, pair), vs[sub], NT, preferred_element_type=F32)
                for sub, hh in chains]
        ps = [jnp.exp(s - _head_rows(lsevs[sub], lanes, hh, pair)) for s, (sub, hh) in zip(ss, chains)]
        dphs = []
        for sub, hh in chains:
            if mode == "dil":
                dphs.append(_head_rows(dpvs[sub], lanes, hh, pair))
            elif pair:
                dphs.append(jnp.sum(jnp.where(_head_lanes(lanes, hh), dpvs[sub], 0.0), axis=1, keepdims=True))
            else:
                dphs.append(jnp.sum(dpvs[sub], axis=1, keepdims=True))
        dss = [p * (dpm - dph) for p, dpm, dph in zip(ps, dpms, dphs)]
        if mode == "na":
            for ds, (sub, hh) in zip(dss, chains):
                off = qis[sub] - jnp.clip(qis[sub] - NA_ROWS // 2, 0, SEQ // GRID_W - NA_ROWS)
                db_ref[hh, off] += ds
        dsbs = [ds.astype(BF16) for ds in dss]
        dvs = [lax.dot_general(p.astype(BF16), dovs[sub], TN, preferred_element_type=F32)
               for p, (sub, hh) in zip(ps, chains)]
        dqs = [jnp.dot(dsb, ks_[sub], preferred_element_type=F32) * scale for dsb, (sub, hh) in zip(dsbs, chains)]
        dks = [lax.dot_general(dsb, qs[sub], TN, preferred_element_type=F32) * scale
               for dsb, (sub, hh) in zip(dsbs, chains)]
        for sub in range(nsub):
            sel = [i for i, (s_, hh) in enumerate(chains) if s_ == sub]
            sl = sls[sub]
            dq = _merge_heads([dqs[i] for i in sel], lanes, pair)
            if mode == "dil":
                dq = _rope_t(dq, tq_ref[0, sl, :], tq_ref[1, sl, :], tq_ref[2, sl, :])
            dq_ref[sl, :] = dq.astype(BF16)
            dk_acc[pl.ds(kss[sub], tk), :] += _merge_heads([dks[i] for i in sel], lanes_k, pair)
            dv_acc[pl.ds(kss[sub], tk), :] += _merge_heads([dvs[i] for i in sel], lanes_k, pair)

        @pl.when(step == nq - 1)
        def _():
            dkv = dk_acc[...]
            if mode == "dil":
                dkv = _rope_t(dkv, tk_ref[0], tk_ref[1], tk_ref[2])
            dk_ref[...] = dkv.astype(kv_dtype)
            dv_ref[...] = dv_acc[...].astype(kv_dtype)

    q_spec = pl.BlockSpec((rows, 128), lambda u, i: (i, qcol + u))
    row_spec = pl.BlockSpec((rows, 128), lambda u, i: (i, u))
    kv_out = pl.BlockSpec((lk, 128), lambda u, i: (0, u))
    whole = pl.BlockSpec((SEQ, 128), lambda u, i: (0, u))
    nat_spec = whole if mode == "dil" else row_spec
    in_specs = [q_spec,
                pl.BlockSpec((lk, 128), lambda u, i: (0, kcol + u)),
                pl.BlockSpec((lk, 128), lambda u, i: (0, vcol + u)),
                nat_spec, nat_spec]
    args = [q_arr, k_arr, v_arr, do, lse]
    out_specs = [row_spec, kv_out, kv_out]
    out_shape = [jax.ShapeDtypeStruct((SEQ, 512), BF16), jax.ShapeDtypeStruct((lk, 512), kv_dtype),
                 jax.ShapeDtypeStruct((lk, 512), kv_dtype)]
    if mode == "dil":
        in_specs += [whole, pl.BlockSpec((3, rows, 128), lambda u, i: (0, i, 0)),
                     pl.BlockSpec((3, SEQ, 128), lambda u, i: (0, 0, 0))]
        args += [dp, tabs, tabs]
    elif mode == "na":
        b_spec = pl.BlockSpec((2, NA_ROWS, GRID_W, NA_ROWS * GRID_W), lambda u, i: (u, 0, 0, 0))
        in_specs += [row_spec, b_spec]
        args += [o, bias]
        out_specs.append(b_spec)
        out_shape.append(jax.ShapeDtypeStruct((8, NA_ROWS, GRID_W, NA_ROWS * GRID_W), F32))
    else:
        in_specs.append(row_spec)
        args.append(o)
    return pl.pallas_call(
        body, name=name, grid=(cfg["units"], nq), in_specs=in_specs, out_specs=out_specs, out_shape=out_shape,
        scratch_shapes=[pltpu.VMEM((lk, 128), F32), pltpu.VMEM((lk, 128), F32)],
        compiler_params=_params(("parallel", "arbitrary")))(*args)


def _na_geometry():
    qc = _iota((GRID_W, 128), 0)
    lane = _iota((GRID_W, 128), 1)
    kc = lane & 63
    c_start = jnp.clip(qc - 8, 0, GRID_W - 16)
    valid = jnp.logical_and(kc >= c_start, kc < c_start + 16)
    return lane, valid


def _na_bias(rpb_rows):
    def body(r_ref, o_ref, t_ref):
        lane, valid = _na_geometry()
        for dd in range(14):
            row_a = jnp.broadcast_to(r_ref[dd:dd + 1, :], (GRID_W, 128))
            row_b = jnp.broadcast_to(r_ref[dd + 1:dd + 2, :], (GRID_W, 128))
            both = jnp.where(lane < 64, row_a, pltpu.roll(row_b, 64, 1))
            t = pltpu.roll(both, 128 - 15, 1, stride=1, stride_axis=0)
            t_ref[dd] = jnp.where(valid, t, NEG)
        for off in range(NA_ROWS):
            for p in range(4):
                o_ref[off, :, p * 128:(p + 1) * 128] = t_ref[2 * p - off + 7]

    return pl.pallas_call(
        body, name="na_bias", grid=(8,),
        in_specs=[pl.BlockSpec((None, 16, 128), lambda h: (h, 0, 0))],
        out_specs=pl.BlockSpec((None, NA_ROWS, GRID_W, NA_ROWS * GRID_W), lambda h: (h, 0, 0, 0)),
        out_shape=jax.ShapeDtypeStruct((8, NA_ROWS, GRID_W, NA_ROWS * GRID_W), F32),
        scratch_shapes=[pltpu.VMEM((14, GRID_W, 128), F32)],
        compiler_params=_params(("parallel",)))(rpb_rows)


def _na_bias_bwd(dbias):
    def body(d_ref, o_ref):
        lane, valid = _na_geometry()
        reverse = (_iota((GRID_W, GRID_W), 0) + _iota((GRID_W, GRID_W), 1) == GRID_W - 1).astype(F32)
        o_ref[...] = jnp.zeros((16, 128), F32)
        for dd in range(14):
            t = jnp.zeros((GRID_W, 128), F32)
            for off in range(NA_ROWS):
                for p in range(4):
                    if 2 * p - off + 7 == dd:
                        t = t + d_ref[off, :, p * 128:(p + 1) * 128]
            t = jnp.dot(reverse, jnp.where(valid, t, 0.0), precision=lax.Precision.HIGHEST,
                        preferred_element_type=F32)
            t = pltpu.roll(t, 128 - (GRID_W - 16), 1, stride=1, stride_axis=0)
            o_ref[dd:dd + 1, :] = jnp.sum(t, axis=0, keepdims=True)

    return pl.pallas_call(
        body, name="na_bias_bwd", grid=(8,),
        in_specs=[pl.BlockSpec((None, NA_ROWS, GRID_W, NA_ROWS * GRID_W), lambda h: (h, 0, 0, 0))],
        out_specs=pl.BlockSpec((None, 16, 128), lambda h: (h, 0, 0)),
        out_shape=jax.ShapeDtypeStruct((8, 16, 128), F32),
        compiler_params=_params(("parallel",)))(dbias)


GATE_ROWS = 128


def _group_weights(l0, l1, l2):
    m = jnp.maximum(jnp.maximum(l0, l1), l2)
    e0, e1, e2 = jnp.exp(l0 - m), jnp.exp(l1 - m), jnp.exp(l2 - m)
    inv = 1.0 / (e0 + e1 + e2)
    return e0 * inv, e1 * inv, e2 * inv


def _gate_specs():
    r512 = pl.BlockSpec((GATE_ROWS, 512), lambda i: (i, 0))
    r1024 = pl.BlockSpec((GATE_ROWS, D_MODEL), lambda i: (i, 0))
    silu_cols = [pl.BlockSpec((GATE_ROWS, 512), functools.partial(lambda b, i: (i, b), 13 + b)) for b in range(3)]
    logit_cols = [pl.BlockSpec((GATE_ROWS, D_MODEL), functools.partial(lambda b, i: (i, b), 8 + b)) for b in range(3)]
    return r512, r1024, silu_cols, logit_cols


def _gate_fwd(o_grp, l_grp, out_b, out_c, parts, merge_bias, wts):
    r512, r1024, silu_cols, logit_cols = _gate_specs()

    def body(o0, o1, o2, l0, l1, l2, ob, oc, ga, gb, gc, la, lb, lc, mb, wa, wb, wc,
             oa_ref, ua, ub, uc, za, zb, zc, y_ref):
        w0, w1, w2 = _group_weights(l0[...], l1[...], l2[...])
        out_a = w0 * o0[...] + w1 * o1[...] + w2 * o2[...]
        oa_ref[...] = out_a
        y = jnp.zeros((GATE_ROWS, D_MODEL), F32)
        for b, (ov, g_ref, l_ref, w_ref, u_ref, z_ref) in enumerate(
                ((out_a, ga, la, wa, ua, za), (ob[...], gb, lb, wb, ub, zb), (oc[...], gc, lc, wc, uc, zc))):
            g = g_ref[...].astype(F32)
            u = (ov * (g * _sigmoid(g))).astype(BF16)
            u_ref[...] = u
            z = lax.dot_general(u, w_ref[...], NT, preferred_element_type=F32)
            z_ref[...] = z.astype(BF16)
            gate = _sigmoid(l_ref[...].astype(F32) + mb[b:b + 1, :])
            y = y + gate * z
        y_ref[...] = y.astype(BF16)

    full = lambda shape: pl.BlockSpec(shape, lambda i: (0,) * len(shape))
    in_specs = ([r512] * 8 + silu_cols + logit_cols
                + [full((3, D_MODEL))] + [full((D_MODEL, 512))] * 3)
    out_specs = [r512] * 4 + [r1024] * 4
    out_shape = ([jax.ShapeDtypeStruct((SEQ, 512), F32)] + [jax.ShapeDtypeStruct((SEQ, 512), BF16)] * 3
                 + [jax.ShapeDtypeStruct((SEQ, D_MODEL), BF16)] * 4)
    res = pl.pallas_call(
        body, name="gate_fwd", grid=(SEQ // GATE_ROWS,), in_specs=in_specs, out_specs=out_specs,
        out_shape=out_shape, compiler_params=_params(("parallel",)))(
            *o_grp, *l_grp, out_b, out_c, parts, parts, parts, parts, parts, parts, merge_bias, *wts)
    return res[0], res[1:4], res[4:7], res[7]


def _gate_bwd(dy, z, parts, merge_bias, outs, o_grp, l_grp, wts, head_sum):
    r512, r1024, silu_cols, logit_cols = _gate_specs()

    def body(dy_ref, za, zb, zc, la, lb, lc, mb, oa, ob, oc, ga, gb, gc, o0, o1, o2, l0, l1, l2, wa, wb, wc, hs_ref,
             dla, dlb, dlc, gmb, dza, dzb, dzc, dga, dgb, dgc, do0, do1, do2, dp0, dp1, dp2, dob, doc):
        dyv = dy_ref[...].astype(F32)
        rows = []
        dos = []
        for b, (z_ref, l_ref, ov_ref, g_ref, w_ref, dl_ref, dz_ref, dg_ref) in enumerate(
                ((za, la, oa, ga, wa, dla, dza, dga), (zb, lb, ob, gb, wb, dlb, dzb, dgb),
                 (zc, lc, oc, gc, wc, dlc, dzc, dgc))):
            gate = _sigmoid(l_ref[...].astype(F32) + mb[b:b + 1, :])
            dl = dyv * z_ref[...].astype(F32) * gate * (1.0 - gate)
            dl_ref[...] = dl.astype(BF16)
            rows.append(jnp.sum(dl, axis=0, keepdims=True))
            dz = (dyv * gate).astype(BF16)
            dz_ref[...] = dz
            du = jnp.dot(dz, w_ref[...], preferred_element_type=F32)
            g = g_ref[...].astype(F32)
            sg = _sigmoid(g)
            dos.append(du * (g * sg))
            dg_ref[...] = (du * ov_ref[...] * (sg * (1.0 + g * (1.0 - sg)))).astype(BF16)

        @pl.when(pl.program_id(0) == 0)
        def _():
            gmb[...] = jnp.zeros((3, D_MODEL), F32)

        for b in range(3):
            gmb[b:b + 1, :] += rows[b]
        dob[...] = dos[1].astype(BF16)
        doc[...] = dos[2].astype(BF16)
        doa = dos[0]
        row_term = jnp.dot(doa * oa[...], hs_ref[...], precision=lax.Precision.HIGHEST, preferred_element_type=F32)
        ws = _group_weights(l0[...], l1[...], l2[...])
        for wg, do_ref, dp_ref in zip(ws, (do0, do1, do2), (dp0, dp1, dp2)):
            do_ref[...] = wg * doa
            dp_ref[...] = wg * row_term

    full = lambda shape: pl.BlockSpec(shape, lambda i: (0,) * len(shape))
    acc = pl.BlockSpec((3, D_MODEL), lambda i: (0, 0))
    in_specs = ([r1024] * 4 + logit_cols + [full((3, D_MODEL))] + [r512] * 3 + silu_cols + [r512] * 6
                + [full((D_MODEL, 512))] * 3 + [full((512, 512))])
    out_specs = [r1024] * 3 + [acc] + [r1024] * 3 + [r512] * 11
    out_shape = ([jax.ShapeDtypeStruct((SEQ, D_MODEL), BF16)] * 3 + [jax.ShapeDtypeStruct((3, D_MODEL), F32)]
                 + [jax.ShapeDtypeStruct((SEQ, D_MODEL), BF16)] * 3 + [jax.ShapeDtypeStruct((SEQ, 512), BF16)] * 3
                 + [jax.ShapeDtypeStruct((SEQ, 512), F32)] * 6 + [jax.ShapeDtypeStruct((SEQ, 512), BF16)] * 2)
    res = pl.pallas_call(
        body, name="gate_bwd", grid=(SEQ // GATE_ROWS,), in_specs=in_specs, out_specs=out_specs,
        out_shape=out_shape, compiler_params=_params(("arbitrary",)))(
            dy, *z, parts, parts, parts, merge_bias, *outs, parts, parts, parts, *o_grp, *l_grp, *wts, head_sum)
    return res[0:3], res[3], res[4:7], res[7:10], res[10:13], res[13:16], res[16], res[17]


def _post(y2, x, target, gain):
    rows = 256

    def body(y_ref, x_ref, t_ref, g_ref, do_ref, dy_ref, l_ref, gg_ref):
        yv = y_ref[...]
        rstd = lax.rsqrt(jnp.mean(yv * yv, axis=1, keepdims=True) + EPS)
        yn = yv * rstd
        gv = g_ref[...]
        err = x_ref[...] + yn * gv - t_ref[...]
        dout = err * (1.0 / D_MODEL)
        do_ref[...] = dout
        dn = dout * gv
        dy_ref[...] = (rstd * (dn - yn * jnp.mean(dn * yn, axis=1, keepdims=True))).astype(BF16)

        @pl.when(pl.program_id(0) == 0)
        def _():
            l_ref[...] = jnp.zeros((1, D_MODEL), F32)
            gg_ref[...] = jnp.zeros((1, D_MODEL), F32)

        l_ref[...] += jnp.sum(err * err, axis=0, keepdims=True)
        gg_ref[...] += jnp.sum(dout * yn, axis=0, keepdims=True)

    row = pl.BlockSpec((rows, D_MODEL), lambda i: (i, 0))
    vec = pl.BlockSpec((1, D_MODEL), lambda i: (0, 0))
    return pl.pallas_call(
        body, name="post", grid=(SEQ // rows,), in_specs=[row, row, row, vec], out_specs=[row, row, vec, vec],
        out_shape=[jax.ShapeDtypeStruct((SEQ, D_MODEL), F32), jax.ShapeDtypeStruct((SEQ, D_MODEL), BF16),
                   jax.ShapeDtypeStruct((1, D_MODEL), F32), jax.ShapeDtypeStruct((1, D_MODEL), F32)],
        compiler_params=_params(("arbitrary",)))(y2, x, target, gain)


def _local_step(x, mem, target, pre_norm, mem_norm, post_norm, na_rpb, wt_in, late_weights, dep_in=None,
                reduce_start=None):
    tabs = _rope_tables()
    hs, hst = _prenorm_fold(x, pre_norm)
    parts = _in_proj(hs, wt_in, tabs, dep_in)

    o_grp, l_grp = [], []
    for g, d in enumerate(DILATIONS):
        o, l = _attn_fwd("dil_fwd_%d" % g, "dil", parts, parts, parts, 12 * g, 12 * g + 4, 12 * g + 8, d=d)
        o_grp.append(o)
        l_grp.append(l)
    bias = _na_bias(jnp.pad(na_rpb, ((0, 0), (0, 1), (0, 128 - 31))))
    out_b, lse_b = _attn_fwd("na_fwd", "na", parts, parts, parts, 36, 40, 44, bias=bias)
    merge_bias, w_kv, wt_a, wt_b, wt_c, w_out = late_weights(out_b)
    memn = _rmsnorm_fwd("memnorm", mem, mem_norm, MEM_LEN)
    kv_m = _mm_simple("mem_kv", memn, w_kv, NN, BF16, MEM_LEN, 512, D_MODEL)
    out_c, lse_c = _attn_fwd("mem_fwd", "mem", parts, kv_m, kv_m, 48, 0, 4)

    wts = (wt_a, wt_b, wt_c)
    out_a, u, z, y = _gate_fwd(o_grp, l_grp, out_b, out_c, parts, merge_bias, wts)
    y2 = _mm_simple("out_proj", y, w_out, NN, F32, 512, D_MODEL, D_MODEL)
    dout, dy2, err_sq, g_post = _post(y2, x, target, post_norm)
    loss = 0.5 * jnp.sum(err_sq) / D_MODEL

    dy = _mm_simple("out_proj_dx", dy2, w_out, NT, BF16, 512, D_MODEL, D_MODEL)
    g_w_out = _mm_simple("out_proj_dw", y, dy2, TN, BF16, D_MODEL, 512, 512)

    rr = _iota((512, 512), 0) // HEAD_DIM
    cc = _iota((512, 512), 1) // HEAD_DIM
    head_sum = (rr == cc).astype(F32)
    dlog, g_mb, dz, dg, do_grp, dp_grp, do_b, do_c = _gate_bwd(
        dy, z, parts, merge_bias, (out_a, out_b, out_c), o_grp, l_grp, wts, head_sum)
    g_wt = [_mm_simple("branch_dw_%d" % b, dz[b], u[b], TN, BF16, D_MODEL, 512, 512) for b in range(3)]

    dqkv = []
    for g, d in enumerate(DILATIONS):
        dq, dk, dv = _attn_bwd("dil_bwd_%d" % g, "dil", parts, parts, parts, 12 * g, 12 * g + 4, 12 * g + 8,
                               do_grp[g], l_grp[g], dp=dp_grp[g], d=d, tabs=tabs[g])
        dqkv += [dq, dk, dv]
    dq_b, dk_b, dv_b, dbias = _attn_bwd("na_bwd", "na", parts, parts, parts, 36, 40, 44, do_b, lse_b, o=out_b,
                                        bias=bias)
    g_rpb_t = _na_bias_bwd(dbias)
    g_rpb = g_rpb_t[:, :15, :31] + jnp.pad(g_rpb_t[:, :14, 64:95], ((0, 0), (1, 0), (0, 0)))
    dq_c, dk_m, dv_m = _attn_bwd("mem_bwd", "mem", parts, kv_m, kv_m, 48, 0, 4, do_c, lse_c, o=out_c)

    dkv = jnp.concatenate([dk_m, dv_m], axis=1).astype(BF16)
    g_w_kv = _mm_simple("mem_kv_dw", memn, dkv, TN, BF16, D_MODEL, 512, MEM_LEN)
    dmemn = _mm_simple("mem_kv_dx", dkv, w_kv, NT, F32, MEM_LEN, 512, D_MODEL)
    g_mem_norm = _memnorm_bwd(mem, dmemn)

    grads = dict(w_kv=g_w_kv, wt_a=g_wt[0], wt_b=g_wt[1], wt_c=g_wt[2], w_out=g_w_out, merge_bias=g_mb,
                 mem_norm=g_mem_norm, post_norm=g_post, na_rpb=g_rpb)
    dep = reduce_start(grads) if reduce_start is not None else None
    dparts = dqkv + [dq_b, dk_b, dv_b, dq_c] + list(dg) + list(dlog)
    grads["wt_in"] = _in_proj_dw(dparts, hst, dep)
    dep = reduce_start(grads) if reduce_start is not None else None
    dh = _in_proj_dh(dparts, wt_in, dep)
    grad_x, grads["pre_norm"] = _prenorm_bwd(x, pre_norm, dh, dout)
    return loss, grad_x, grads


ANY = pl.BlockSpec(memory_space=pl.ANY)


def _place():
    return lax.axis_index("x"), lax.axis_index("y"), lax.axis_index("c")


def _all_gather(shard):
    r = shard.shape[0]
    half = r // 2

    def body(src, out, send_sems, recv_sems, local_sem):
        x, y, c = _place()
        me, sib = (x, y, c), (x, y, 1 - c)
        xn, yn, dg = (1 - x, y, c), (x, 1 - y, c), (1 - x, 1 - y, c)

        def rows(dev, part=None):
            blk = out.at[4 * dev[0] + 2 * dev[1] + dev[2]]
            return blk if part is None else blk.at[pl.ds(part * half, half)]

        def copy(k, dev, part, to, own=False):
            return pltpu.make_async_remote_copy(
                src_ref=src if own else rows(dev, part), dst_ref=rows(dev, part),
                send_sem=send_sems.at[k], recv_sem=recv_sems.at[k], device_id=to, device_id_type=MESH_ID)

        def other(dev):
            return (dev[0], dev[1], 1 - dev[2])

        mine = pltpu.make_async_copy(src, rows(me), local_sem)
        mine.start()
        sent = [copy(0, me, None, sib, own=True), copy(1, me, None, xn, own=True), copy(2, me, None, yn, own=True)]
        for cp in sent:
            cp.start()
        copy(1, xn, None, me).wait_recv()
        sent += [copy(3, xn, 0, yn), copy(5, xn, None, sib)]
        sent[-2].start()
        sent[-1].start()
        copy(2, yn, None, me).wait_recv()
        sent += [copy(4, yn, 1, xn), copy(6, yn, None, sib)]
        sent[-2].start()
        sent[-1].start()
        copy(3, dg, 0, me).wait_recv()
        sent.append(copy(7, dg, 0, sib))
        sent[-1].start()
        copy(4, dg, 1, me).wait_recv()
        sent.append(copy(8, dg, 1, sib))
        sent[-1].start()
        copy(0, sib, None, me).wait_recv()
        copy(5, other(xn), None, me).wait_recv()
        copy(6, other(yn), None, me).wait_recv()
        copy(7, other(dg), 0, me).wait_recv()
        copy(8, other(dg), 1, me).wait_recv()
        for cp in sent:
            cp.wait_send()
        mine.wait()

    return pl.pallas_call(
        body, name="all_gather", in_specs=[ANY], out_specs=ANY,
        out_shape=jax.ShapeDtypeStruct((N_DEV,) + shard.shape, shard.dtype),
        scratch_shapes=[pltpu.SemaphoreType.DMA((9,)), pltpu.SemaphoreType.DMA((9,)), pltpu.SemaphoreType.DMA])(shard)


def _exchange_sibling(name, terms):
    nt = len(terms)

    def body(*refs):
        srcs, outs = refs[:nt], refs[nt:2 * nt]
        send_sems, recv_sems = refs[2 * nt:]
        x, y, c = _place()
        copies = []
        for q in range(4):
            for t in range(nt):
                copies.append(pltpu.make_async_remote_copy(
                    src_ref=srcs[t].at[2 * q + 1 - c], dst_ref=outs[t].at[q],
                    send_sem=send_sems.at[q * nt + t], recv_sem=recv_sems.at[q * nt + t],
                    device_id=(x, y, 1 - c), device_id_type=MESH_ID))
        for cp in copies:
            cp.start()
        for cp in copies:
            cp.wait()

    return pl.pallas_call(
        body, name=name, in_specs=[ANY] * nt, out_specs=[ANY] * nt,
        out_shape=[jax.ShapeDtypeStruct((4,) + s.shape[1:], s.dtype) for s in terms],
        scratch_shapes=[pltpu.SemaphoreType.DMA((4 * nt,)), pltpu.SemaphoreType.DMA((4 * nt,))])(*terms)


HBM = pl.BlockSpec(memory_space=pltpu.HBM)
SEM = pl.BlockSpec(memory_space=pltpu.SEMAPHORE)
DATAFLOW = pltpu.SideEffectType.DATAFLOW_SIDE_EFFECTING


def _split_copies(kind, srcs, lands, send_sems, recv_sems):
    nt = len(srcs)
    x, y, c = _place()
    copies = []
    if kind == "gather":
        me = 4 * x + 2 * y + c
        for mask in range(1, 8):
            fx, fy, fc = (mask >> 2) & 1, (mask >> 1) & 1, mask & 1
            to = (1 - x if fx else x, 1 - y if fy else y, 1 - c if fc else c)
            for t in range(nt):
                k = (mask - 1) * nt + t
                copies.append(pltpu.make_async_remote_copy(
                    src_ref=srcs[t], dst_ref=lands[t].at[me], send_sem=send_sems.at[k], recv_sem=recv_sems.at[k],
                    device_id=to, device_id_type=MESH_ID))
    else:
        for s, (tx, ty) in enumerate([(1 - x, y), (x, 1 - y), (1 - x, 1 - y)]):
            for t in range(nt):
                k = s * nt + t
                copies.append(pltpu.make_async_remote_copy(
                    src_ref=srcs[t].at[2 * tx + ty], dst_ref=lands[t].at[s], send_sem=send_sems.at[k],
                    recv_sem=recv_sems.at[k], device_id=(tx, ty, c), device_id_type=MESH_ID))
    return copies


def _split_count(kind, nt):
    return (7 if kind == "gather" else 3) * nt


def _exchange_start(name, kind, srcs, land_shapes, after=None):
    nt = len(srcs)
    n = _split_count(kind, nt)
    dep_specs, dep_args = _dep_operand(after)
    nd = len(dep_args)

    def body(*refs):
        src_refs, land_refs = refs[:nt], refs[nt:2 * nt]
        send_sems, recv_sems = refs[2 * nt + nd], refs[2 * nt + nd + 1]
        token = refs[-1]
        for cp in _split_copies(kind, src_refs, land_refs, send_sems, recv_sems):
            cp.start()
        token[...] = jnp.zeros_like(token)

    lands = [pltpu.with_memory_space_constraint(lax.empty(s.shape, s.dtype), pltpu.HBM) for s in land_shapes]
    res = pl.pallas_call(
        body, name=name,
        out_shape=(pltpu.SemaphoreType.DMA((n,)), pltpu.SemaphoreType.DMA((n,)),
                   *[pltpu.HBM(s.shape, s.dtype) for s in srcs], *[pltpu.HBM(s.shape, s.dtype) for s in land_shapes],
                   jax.ShapeDtypeStruct((8, 128), F32)),
        in_specs=[HBM] * (2 * nt) + dep_specs,
        out_specs=(SEM, SEM, *([HBM] * (2 * nt)), pl.BlockSpec(memory_space=pltpu.VMEM)),
        input_output_aliases={i: 2 + i for i in range(2 * nt)},
        compiler_params=pltpu.CompilerParams(has_side_effects=DATAFLOW))(
            *[pltpu.with_memory_space_constraint(s, pltpu.HBM) for s in srcs], *lands, *dep_args)
    return res[0], res[1], list(res[2:2 + nt]), list(res[2 + nt:2 + 2 * nt]), res[-1]


def _exchange_wait(name, kind, send_sems, recv_sems, srcs, lands, after):
    nt = len(srcs)

    def body(*refs):
        src_refs, land_refs = refs[:nt], refs[nt:2 * nt]
        s_sems, r_sems = refs[2 * nt], refs[2 * nt + 1]
        for cp in _split_copies(kind, src_refs, land_refs, s_sems, r_sems):
            cp.wait_send()
            cp.wait_recv()

    res = pl.pallas_call(
        body, name=name,
        out_shape=tuple(pltpu.HBM(s.shape, s.dtype) for s in list(srcs) + list(lands)),
        in_specs=[HBM] * (2 * nt) + [SEM, SEM, pl.BlockSpec(memory_space=pl.ANY)],
        out_specs=tuple([HBM] * (2 * nt)),
        input_output_aliases={i: i for i in range(2 * nt)},
        compiler_params=pltpu.CompilerParams(has_side_effects=DATAFLOW))(
            *srcs, *lands, send_sems, recv_sems, after)
    return list(res[:nt]), list(res[nt:])


def _add_sibling(name, term, recv, rows):
    _, r, w = term.shape
    cidx = lax.axis_index("c").astype(jnp.int32).reshape(1)

    def body(c_ref, a_ref, b_ref, o_ref):
        o_ref[...] = (a_ref[...].astype(F32) + b_ref[...].astype(F32)).astype(o_ref.dtype)

    grid_spec = pltpu.PrefetchScalarGridSpec(
        num_scalar_prefetch=1, grid=(4, r // rows),
        in_specs=[pl.BlockSpec((None, rows, w), lambda q, i, c_ref: (2 * q + c_ref[0], i, 0)),
                  pl.BlockSpec((None, rows, w), lambda q, i, c_ref: (q, i, 0))],
        out_specs=pl.BlockSpec((None, rows, w), lambda q, i, c_ref: (q, i, 0)))
    return pl.pallas_call(
        body, name=name, grid_spec=grid_spec, out_shape=jax.ShapeDtypeStruct((4, r, w), term.dtype),
        compiler_params=_params(("parallel", "parallel")))(cidx, term, recv)


def _add_chips(name, sums, recv, rows):
    _, r, w = sums.shape
    qidx = (2 * lax.axis_index("x") + lax.axis_index("y")).astype(jnp.int32).reshape(1)

    def body(q_ref, a_ref, b_ref, o_ref):
        o_ref[...] = ((a_ref[...].astype(F32) + b_ref[0].astype(F32))
                      + (b_ref[1].astype(F32) + b_ref[2].astype(F32)))

    grid_spec = pltpu.PrefetchScalarGridSpec(
        num_scalar_prefetch=1, grid=(r // rows,),
        in_specs=[pl.BlockSpec((None, rows, w), lambda i, q_ref: (q_ref[0], i, 0)),
                  pl.BlockSpec((3, rows, w), lambda i, q_ref: (0, i, 0))],
        out_specs=pl.BlockSpec((rows, w), lambda i, q_ref: (i, 0)))
    return pl.pallas_call(
        body, name=name, grid_spec=grid_spec, out_shape=jax.ShapeDtypeStruct((r, w), F32),
        compiler_params=_params(("parallel",)))(qidx, sums, recv)


def _rs_rows(a):
    return SHARD_IN // 4 if a.shape[1] == SHARD_IN else a.shape[1]


def _reduce_scatter_start(tag, names, terms):
    recv1 = _exchange_sibling("exchange_sibling_" + tag, terms)
    sums = [_add_sibling("add_sibling_" + n, t, r, _rs_rows(t)) for n, t, r in zip(names, terms, recv1)]
    lands = [jax.ShapeDtypeStruct((3,) + s.shape[1:], s.dtype) for s in sums]
    send_sems, recv_sems, sums, lands, token = _exchange_start("exchange_chips_start_" + tag, "chips", sums, lands)
    return (tag, names, send_sems, recv_sems, sums, lands), token


def _reduce_scatter_finish(state, after):
    tag, names, send_sems, recv_sems, sums, lands = state
    sums, recv2 = _exchange_wait("exchange_chips_wait_" + tag, "chips", send_sems, recv_sems, sums, lands, after)
    return [_add_chips("add_chips_" + n, s, r, _rs_rows(s)) for n, s, r in zip(names, sums, recv2)]


def _adamw(name, w, g, m, v, rows=None):
    r, c = w.shape
    rows = r if rows is None else rows
    c1 = 1.0 - ADAM_B1 ** ADAM_STEP
    c2 = 1.0 - ADAM_B2 ** ADAM_STEP

    def body(w_ref, g_ref, m_ref, v_ref, d_ref, nm_ref, nv_ref):
        gv = g_ref[...]
        nm = ADAM_B1 * m_ref[...] + (1.0 - ADAM_B1) * gv
        nv = ADAM_B2 * v_ref[...] + (1.0 - ADAM_B2) * (gv * gv)
        nm_ref[...] = nm
        nv_ref[...] = nv
        d_ref[...] = -ADAM_LR * ((nm / c1) / (jnp.sqrt(nv / c2) + ADAM_EPS) + ADAM_WD * w_ref[...])

    spec = pl.BlockSpec((rows, c), lambda i: (i, 0))
    return pl.pallas_call(
        body, name=name, grid=(r // rows,), in_specs=[spec] * 4, out_specs=[spec] * 3,
        out_shape=[jax.ShapeDtypeStruct((r, c), F32)] * 3, compiler_params=_params(("parallel",)))(w, g, m, v)


def _sum_devices(gathered):
    def body(g_ref, o_ref):
        acc = g_ref[0]
        for j in range(1, N_DEV):
            acc = acc + g_ref[j]
        o_ref[...] = acc

    return pl.pallas_call(
        body, name="sum_devices", out_shape=jax.ShapeDtypeStruct(gathered.shape[1:], F32),
        compiler_params=_params())(gathered)


def _rows128(a, rows):
    flat = a.reshape(-1)
    return jnp.pad(flat, (0, rows * 128 - flat.shape[0])).reshape(rows, 128)


def kernel(x, mem, pre_norm, w_in, merge_bias, na_rpb, mem_norm, w_mem_kv, w_branch_a, w_branch_b, w_branch_c, w_out, post_norm, loss_target, m_pre_norm, m_w_in, m_merge_bias, m_na_rpb, m_mem_norm, m_w_mem_kv, m_w_branch_a, m_w_branch_b, m_w_branch_c, m_w_out, m_post_norm, v_pre_norm, v_w_in, v_merge_bias, v_na_rpb, v_mem_norm, v_w_mem_kv, v_w_branch_a, v_w_branch_b, v_w_branch_c, v_w_out, v_post_norm):
    wt_in_s = w_in[0].T.astype(BF16)
    rows_s = jnp.concatenate([w_mem_kv[0], w_out[0]], axis=0).astype(BF16)
    cols_s = jnp.concatenate([w_branch_a[0].T, w_branch_b[0].T, w_branch_c[0].T], axis=0).astype(BF16)
    mb_s = jnp.pad(merge_bias[0], ((0, 5), (0, 0)))
    wt_in = _all_gather(wt_in_s).reshape(N_IN, D_MODEL)

    late_own = [rows_s, cols_s, mb_s]
    late_lands = [jax.ShapeDtypeStruct((N_DEV,) + s.shape, s.dtype) for s in late_own]
    l_send, l_recv, late_own, late_lands, late_token = _exchange_start("gather_late_start", "gather", late_own,
                                                                       late_lands, after=wt_in)
    me = 4 * lax.axis_index("x") + 2 * lax.axis_index("y") + lax.axis_index("c")

    def late_weights(after):
        own, lands = _exchange_wait("gather_late_wait", "gather", l_send, l_recv, late_own, late_lands, after)
        g_rows, g_cols, g_mb = [lax.dynamic_update_slice(land, o[None], (me, 0, 0)) for land, o in zip(lands, own)]
        return (g_mb[:, :3].transpose(1, 0, 2).reshape(3, D_MODEL),
                g_rows[:, :128].reshape(D_MODEL, D_MODEL), g_cols[:, 0:128].reshape(D_MODEL, 512),
                g_cols[:, 128:256].reshape(D_MODEL, 512), g_cols[:, 256:384].reshape(D_MODEL, 512),
                g_rows[:, 128:].reshape(D_MODEL, D_MODEL))

    rs_state = []

    def reduce_start(grads):
        if "wt_in" in grads:
            state, token = _reduce_scatter_start("w_in", ["w_in"],
                                                 [grads["wt_in"].reshape(N_DEV, SHARD_IN, D_MODEL)])
        else:
            gmb_t = jnp.pad(grads["merge_bias"].reshape(3, N_DEV, 128).transpose(1, 0, 2), ((0, 0), (0, 5), (0, 0)))
            names = ["w_kv", "w_out", "a", "b", "c", "mb"]
            terms = [grads["w_kv"].reshape(N_DEV, 128, D_MODEL), grads["w_out"].reshape(N_DEV, 128, D_MODEL),
                     grads["wt_a"].reshape(N_DEV, 128, 512), grads["wt_b"].reshape(N_DEV, 128, 512),
                     grads["wt_c"].reshape(N_DEV, 128, 512), gmb_t]
            state, token = _reduce_scatter_start("rest", names, terms)
        rs_state.append(state)
        return token

    loss_term, grad_x, grads = _local_step(
        x[0], mem[0], loss_target[0], pre_norm, mem_norm, post_norm, na_rpb[0], wt_in, late_weights,
        dep_in=late_token, reduce_start=reduce_start)

    small = jnp.concatenate([_rows128(grads["pre_norm"], 8), _rows128(grads["mem_norm"], 8),
                             _rows128(grads["post_norm"], 8), _rows128(grads["na_rpb"], 32),
                             _rows128(loss_term, 8)], axis=0)
    s_send, s_recv, s_own, s_land, s_token = _exchange_start(
        "gather_small_start", "gather", [small], [jax.ShapeDtypeStruct((N_DEV,) + small.shape, F32)])
    grad = {}
    weights = {
        "pre_norm": (pre_norm, m_pre_norm, v_pre_norm), "w_in": (w_in, m_w_in, v_w_in),
        "merge_bias": (merge_bias, m_merge_bias, v_merge_bias), "na_rpb": (na_rpb, m_na_rpb, v_na_rpb),
        "mem_norm": (mem_norm, m_mem_norm, v_mem_norm), "w_mem_kv": (w_mem_kv, m_w_mem_kv, v_w_mem_kv),
        "w_branch_a": (w_branch_a, m_w_branch_a, v_w_branch_a), "w_branch_b": (w_branch_b, m_w_branch_b, v_w_branch_b),
        "w_branch_c": (w_branch_c, m_w_branch_c, v_w_branch_c), "w_out": (w_out, m_w_out, v_w_out),
        "post_norm": (post_norm, m_post_norm, v_post_norm)}
    order = ["pre_norm", "w_in", "merge_bias", "na_rpb", "mem_norm", "w_mem_kv", "w_branch_a", "w_branch_b",
             "w_branch_c", "w_out", "post_norm"]
    delta, new_m, new_v = {}, {}, {}

    def update(n):
        w, m, v = weights[n]
        shape = w.shape
        two_d = (-1, shape[-1])
        rows = 256 if n == "w_in" else None
        dl, nm, nv = _adamw("adamw_" + n, w.reshape(two_d), grad[n].reshape(two_d), m.reshape(two_d),
                            v.reshape(two_d), rows)
        delta[n], new_m[n], new_v[n] = dl.reshape(shape), nm.reshape(shape), nv.reshape(shape)

    g_kv, g_out, gt_a, gt_b, gt_c, g_mb8 = _reduce_scatter_finish(rs_state[0], s_token)
    grad.update({"merge_bias": g_mb8[:3][None], "w_mem_kv": g_kv[None], "w_branch_a": gt_a.T[None],
                 "w_branch_b": gt_b.T[None], "w_branch_c": gt_c.T[None], "w_out": g_out[None]})
    for n in ("merge_bias", "w_mem_kv", "w_branch_a", "w_branch_b", "w_branch_c", "w_out"):
        update(n)
    s_own, s_land = _exchange_wait("gather_small_wait", "gather", s_send, s_recv, s_own, s_land, delta["w_out"])
    total = _sum_devices(lax.dynamic_update_slice(s_land[0], s_own[0][None], (me, 0, 0)))
    loss = total[56, 0]
    grad.update({"pre_norm": total[0:8].reshape(1, D_MODEL), "mem_norm": total[8:16].reshape(1, D_MODEL),
                 "post_norm": total[16:24].reshape(1, D_MODEL),
                 "na_rpb": total[24:56].reshape(-1)[:8 * 15 * 31].reshape(1, 8, 15, 31)})
    for n in ("pre_norm", "na_rpb", "mem_norm", "post_norm"):
        update(n)
    (gt_in,) = _reduce_scatter_finish(rs_state[1], delta["post_norm"])
    grad["w_in"] = gt_in.T[None]
    update("w_in")

    return (loss, grad_x[None], *[grad[n] for n in order], *[delta[n] for n in order],
            *[new_m[n] for n in order], *[new_v[n] for n in order])
```

```python
import functools

import numpy as np
import jax
import jax.numpy as jnp
from jax import lax
from jax.experimental import pallas as pl
from jax.experimental.pallas import tpu as pltpu

F32 = jnp.float32
BF16 = jnp.bfloat16

SEQ = 2048
D_MODEL = 1024
N_IN = 11264
N_DEV = 8
SHARD_IN = N_IN // N_DEV
HEAD_DIM = 64
GRID_W = 64
NA_ROWS = 8
MEM_LEN = 256
DILATIONS = (1, 4, 16)
REACH = 64
ROPE_THETA = 500000.0
ROPE_DIM = 16
EPS = 1e-6
NEG = -1e30
ADAM_LR = 0.001
ADAM_B1 = 0.9
ADAM_B2 = 0.999
ADAM_EPS = 1e-08
ADAM_WD = 0.01
ADAM_STEP = 10

VMEM_LIMIT_BYTES = 56 * 1024 * 1024
MESH_ID = pl.DeviceIdType.MESH

NN = (((1,), (0,)), ((), ()))
NT = (((1,), (1,)), ((), ()))
TN = (((0,), (0,)), ((), ()))


def _params(sem=None):
    return pltpu.CompilerParams(dimension_semantics=sem, vmem_limit_bytes=VMEM_LIMIT_BYTES)


def _iota(shape, dim):
    return lax.broadcasted_iota(jnp.int32, shape, dim)


def _sigmoid(x):
    return 1.0 / (1.0 + jnp.exp(-x))


def _rope_tables():
    half = ROPE_DIM // 2
    inv = (ROPE_THETA ** (-np.arange(half, dtype=np.float64) * 2.0 / ROPE_DIM)).astype(np.float32)
    pos = np.arange(SEQ, dtype=np.float32)
    ang = pos[:, None] * inv[None, :]
    cos, sin = np.cos(ang), np.sin(ang)
    zeros = np.zeros_like(cos)
    rest = HEAD_DIM - ROPE_DIM
    c64 = np.concatenate([cos, cos, np.ones((SEQ, rest), np.float32)], axis=1)
    s1 = np.concatenate([zeros, sin, np.zeros((SEQ, rest), np.float32)], axis=1)
    s2 = np.concatenate([-sin, zeros, np.zeros((SEQ, rest), np.float32)], axis=1)

    def fold(t, d):
        return t.reshape(SEQ // d, d, t.shape[1]).transpose(1, 0, 2).reshape(SEQ, t.shape[1])

    tabs = [np.stack([np.tile(fold(t, d), (1, 2)) for t in (c64, s1, s2)], axis=0) for d in DILATIONS]
    return jnp.asarray(np.stack(tabs, axis=0), dtype=F32)


def _rope(a, c, s1, s2):
    return a * c + pltpu.roll(a, 8, 1) * s1 + pltpu.roll(a, 120, 1) * s2


def _rope_t(a, c, s1, s2):
    return a * c + pltpu.roll(a * s1, 120, 1) + pltpu.roll(a * s2, 8, 1)


def _perm_of_block(j):
    return jnp.where(j < 3, 0, jnp.where(j < 6, 1, jnp.where(j < 9, 2, 0)))


def _mm(name, a, b, out_shape, out_dtype, grid, a_spec, b_spec, o_spec, acc_shape, dims, k_axis, nk):
    def body(a_ref, b_ref, o_ref, acc_ref):
        k = pl.program_id(k_axis)

        @pl.when(k == 0)
        def _():
            acc_ref[...] = jnp.zeros(acc_shape, F32)

        acc_ref[...] += lax.dot_general(a_ref[...], b_ref[...], dims, preferred_element_type=F32)

        @pl.when(k == nk - 1)
        def _():
            o_ref[...] = acc_ref[...].astype(out_dtype)

    sem = tuple("arbitrary" if ax == k_axis else "parallel" for ax in range(len(grid)))
    return pl.pallas_call(
        body, name=name, grid=grid, in_specs=[a_spec, b_spec], out_specs=o_spec,
        out_shape=jax.ShapeDtypeStruct(out_shape, out_dtype),
        scratch_shapes=[pltpu.VMEM(acc_shape, F32)], compiler_params=_params(sem))(a, b)


def _mm_simple(name, a, b, dims, out_dtype, tm, tn, tk):
    if dims is NN:
        m, kk = a.shape
        n = b.shape[1]
        a_spec = pl.BlockSpec((tm, tk), lambda i, j, k: (i, k))
        b_spec = pl.BlockSpec((tk, tn), lambda i, j, k: (k, j))
    elif dims is NT:
        m, kk = a.shape
        n = b.shape[0]
        a_spec = pl.BlockSpec((tm, tk), lambda i, j, k: (i, k))
        b_spec = pl.BlockSpec((tn, tk), lambda i, j, k: (j, k))
    else:
        kk, m = a.shape
        n = b.shape[1]
        a_spec = pl.BlockSpec((tk, tm), lambda i, j, k: (k, i))
        b_spec = pl.BlockSpec((tk, tn), lambda i, j, k: (k, j))
    grid = (m // tm, n // tn, kk // tk)
    o_spec = pl.BlockSpec((tm, tn), lambda i, j, k: (i, j))
    return _mm(name, a, b, (m, n), out_dtype, grid, a_spec, b_spec, o_spec, (tm, tn), dims, 2, kk // tk)


def _rmsnorm_fwd(name, x, gain, rows):
    n, d = x.shape

    def body(x_ref, g_ref, o_ref):
        xv = x_ref[...]
        rstd = lax.rsqrt(jnp.mean(xv * xv, axis=1, keepdims=True) + EPS)
        o_ref[...] = (xv * rstd * g_ref[...]).astype(BF16)

    return pl.pallas_call(
        body, name=name, grid=(n // rows,),
        in_specs=[pl.BlockSpec((rows, d), lambda i: (i, 0)), pl.BlockSpec((1, d), lambda i: (0, 0))],
        out_specs=pl.BlockSpec((rows, d), lambda i: (i, 0)),
        out_shape=jax.ShapeDtypeStruct((n, d), BF16), compiler_params=_params(("parallel",)))(x, gain)


def _folded_rows(first, rows, d):
    if d == 1:
        return pl.ds(pl.multiple_of(first, rows), rows)
    mlen = SEQ // d
    return pl.ds((first % mlen) * d + first // mlen, rows, stride=d)


def _prenorm_fold(x, gain):
    rows = 128

    nchunk = D_MODEL // 128

    def body(*refs):
        x_refs, g_ref, hs_ref, hst_ref = refs[:nchunk], refs[nchunk], refs[nchunk + 1], refs[nchunk + 2]
        first = pl.program_id(0) * rows
        for p, d in enumerate(DILATIONS):
            idx = _folded_rows(first, rows, d)
            xv = jnp.concatenate([r[idx, :] for r in x_refs], axis=1)
            rstd = lax.rsqrt(jnp.mean(xv * xv, axis=1, keepdims=True) + EPS)
            h = xv * rstd * g_ref[...]
            hs_ref[p] = h.astype(BF16)
            hst_ref[p] = h.T.astype(BF16)

    x_specs = [pl.BlockSpec((SEQ, 128), functools.partial(lambda c, i: (0, c), c)) for c in range(nchunk)]
    return pl.pallas_call(
        body, name="prenorm", grid=(SEQ // rows,),
        in_specs=x_specs + [pl.BlockSpec((1, D_MODEL), lambda i: (0, 0))],
        out_specs=[pl.BlockSpec((3, rows, D_MODEL), lambda i: (0, i, 0)),
                   pl.BlockSpec((3, D_MODEL, rows), lambda i: (0, 0, i))],
        out_shape=[jax.ShapeDtypeStruct((3, SEQ, D_MODEL), BF16), jax.ShapeDtypeStruct((3, D_MODEL, SEQ), BF16)],
        compiler_params=_params(("parallel",)))(*([x] * nchunk), gain)


def _prenorm_bwd(x, gain, dh, dout):
    rows = 256

    def body(x_ref, g_ref, a_ref, do_ref, dx_ref, gg_ref):
        xv = x_ref[...]
        rstd = lax.rsqrt(jnp.mean(xv * xv, axis=1, keepdims=True) + EPS)
        xn = xv * rstd
        dh = jnp.concatenate([a_ref[c] for c in range(D_MODEL // 128)], axis=1)
        gdh = dh * g_ref[...]
        dx_ref[...] = rstd * (gdh - xn * jnp.mean(gdh * xn, axis=1, keepdims=True)) + do_ref[...]

        @pl.when(pl.program_id(0) == 0)
        def _():
            gg_ref[...] = jnp.zeros((1, D_MODEL), F32)

        gg_ref[...] += jnp.sum(dh * xn, axis=0, keepdims=True)

    row = pl.BlockSpec((rows, D_MODEL), lambda i: (i, 0))
    vec = pl.BlockSpec((1, D_MODEL), lambda i: (0, 0))
    return pl.pallas_call(
        body, name="prenorm_bwd", grid=(SEQ // rows,),
        in_specs=[row, vec, pl.BlockSpec((D_MODEL // 128, rows, 128), lambda i: (0, i, 0)), row], out_specs=[row, vec],
        out_shape=[jax.ShapeDtypeStruct((SEQ, D_MODEL), F32), jax.ShapeDtypeStruct((1, D_MODEL), F32)],
        compiler_params=_params(("arbitrary",)))(x, gain, dh, dout)


def _memnorm_bwd(mem, dmemn):
    def body(m_ref, d_ref, gg_ref):
        mv = m_ref[...]
        rstd = lax.rsqrt(jnp.mean(mv * mv, axis=1, keepdims=True) + EPS)
        gg_ref[...] = jnp.sum(d_ref[...] * mv * rstd, axis=0, keepdims=True)

    return pl.pallas_call(
        body, name="memnorm_bwd", out_shape=jax.ShapeDtypeStruct((1, D_MODEL), F32),
        compiler_params=_params())(mem, dmemn)


def _dep_operand(dep):
    return ([], []) if dep is None else ([pl.BlockSpec(memory_space=pl.ANY)], [dep])


def _in_proj(hs, wt, tabs, dep=None):
    tm, tn = 512, 512
    dep_specs, dep_args = _dep_operand(dep)

    def body(h_ref, w_ref, t_ref, *rest):
        o_ref = rest[-1]
        j = pl.program_id(0)
        is_rope = jnp.logical_and(j < 9, j % 3 != 2)
        row_slices = [slice(r * tm, (r + 1) * tm) for r in range(SEQ // tm)]

        def product(rs):
            return lax.dot_general(h_ref[rs, :], w_ref[...], NT, preferred_element_type=F32)

        @pl.when(is_rope)
        def _():
            for rs in row_slices:
                acc = product(rs)
                c, s1, s2 = t_ref[0, rs, :], t_ref[1, rs, :], t_ref[2, rs, :]
                for q in range(tn // 128):
                    a = acc[:, q * 128:(q + 1) * 128]
                    o_ref[rs, q * 128:(q + 1) * 128] = _rope(a, c, s1, s2).astype(BF16)

        @pl.when(jnp.logical_not(is_rope))
        def _():
            for rs in row_slices:
                o_ref[rs, :] = product(rs).astype(BF16)

    return pl.pallas_call(
        body, name="in_proj", grid=(N_IN // tn,),
        in_specs=[pl.BlockSpec((None, SEQ, D_MODEL), lambda j: (_perm_of_block(j), 0, 0)),
                  pl.BlockSpec((tn, D_MODEL), lambda j: (j, 0)),
                  pl.BlockSpec((None, 3, SEQ, 128), lambda j: (_perm_of_block(j), 0, 0, 0))] + dep_specs,
        out_specs=pl.BlockSpec((SEQ, tn), lambda j: (0, j)),
        out_shape=jax.ShapeDtypeStruct((SEQ, N_IN), BF16),
        compiler_params=_params(("parallel",)))(hs, wt, tabs, *dep_args)


def _piece_blocks(pieces):
    return [(a, h * 512) for a, p in enumerate(pieces) for h in range(p.shape[1] // 512)]


def _block_fetch(piece_refs, blocks, buf, sem):
    def start(block, slot):
        for b, (a, col) in enumerate(blocks):
            @pl.when(block == b)
            def _():
                pltpu.make_async_copy(piece_refs[a].at[:, pl.ds(col, 512)], buf.at[slot], sem.at[slot]).start()

    def wait(slot):
        pltpu.make_async_copy(piece_refs[0].at[:, pl.ds(0, 512)], buf.at[slot], sem.at[slot]).wait()

    return start, wait


def _in_proj_dw(pieces, hst, dep=None):
    tn = 512
    blocks = _piece_blocks(pieces)
    nblk = len(blocks)
    npc = len(pieces)
    dep_specs, dep_args = _dep_operand(dep)

    def body(h_ref, *rest):
        piece_refs = rest[:npc]
        o_ref, buf, sem = rest[-3:]
        j = pl.program_id(0)
        slot = j % 2
        start, wait = _block_fetch(piece_refs, blocks, buf, sem)

        @pl.when(j == 0)
        def _():
            start(j, slot)

        wait(slot)

        @pl.when(j + 1 < nblk)
        def _():
            start(j + 1, 1 - slot)

        acc = jnp.dot(h_ref[...], buf[slot], preferred_element_type=F32)
        o_ref[...] = acc.T.astype(BF16)

    return pl.pallas_call(
        body, name="in_proj_dw", grid=(nblk,),
        in_specs=[pl.BlockSpec((None, D_MODEL, SEQ), lambda j: (_perm_of_block(j), 0, 0))] + [ANY] * npc + dep_specs,
        out_specs=pl.BlockSpec((tn, D_MODEL), lambda j: (j, 0)),
        out_shape=jax.ShapeDtypeStruct((N_IN, D_MODEL), BF16),
        scratch_shapes=[pltpu.VMEM((2, SEQ, tn), BF16), pltpu.SemaphoreType.DMA((2,))],
        compiler_params=_params(("arbitrary",)))(hst, *pieces, *dep_args)


def _in_proj_dh(pieces, wt, dep=None):
    tk = 512
    blocks = _piece_blocks(pieces)
    nblk = len(blocks)
    npc = len(pieces)
    nchunk = D_MODEL // 128

    def col(s):
        return jnp.where(s < 3, s, jnp.where(s < 16, s + 6, s - 13))

    dep_specs, dep_args = _dep_operand(dep)

    def body(w_ref, *rest):
        piece_refs = rest[:npc]
        o_ref, acc_ref, buf, sem = rest[-4:]
        s = pl.program_id(0)
        slot = s % 2
        start, wait = _block_fetch(piece_refs, blocks, buf, sem)

        @pl.when(s == 0)
        def _():
            start(col(s), slot)

        wait(slot)

        @pl.when(s + 1 < nblk)
        def _():
            start(col(s + 1), 1 - slot)

        row_slices = [slice(r * 512, (r + 1) * 512) for r in range(SEQ // 512)]

        def product(rs):
            return jnp.dot(buf[slot, rs, :], w_ref[...], preferred_element_type=F32)

        def accumulate(cond, to_out, init):
            @pl.when(cond)
            def _():
                for rs in row_slices:
                    prod = product(rs)
                    if not to_out:
                        if init:
                            acc_ref[rs, :] = prod
                        else:
                            acc_ref[rs, :] += prod
                        continue
                    for c in range(nchunk):
                        if init:
                            o_ref[c, rs, :] = prod[:, c * 128:(c + 1) * 128]
                        else:
                            o_ref[c, rs, :] += prod[:, c * 128:(c + 1) * 128]

        accumulate(s == 0, True, True)
        accumulate(jnp.logical_and(s > 0, s < 16), True, False)
        accumulate(jnp.logical_or(s == 16, s == 19), False, True)
        accumulate(jnp.logical_and(s > 16, s != 19), False, False)
        for last, d in ((18, 4), (21, 16)):
            @pl.when(s == last)
            def _():
                mlen = SEQ // d
                for r in range(d):
                    for c in range(nchunk):
                        o_ref[c, pl.ds(r, mlen, stride=d), :] += acc_ref[r * mlen:(r + 1) * mlen,
                                                                         c * 128:(c + 1) * 128]

    return pl.pallas_call(
        body, name="in_proj_dh", grid=(nblk,),
        in_specs=[pl.BlockSpec((tk, D_MODEL), lambda s: (col(s), 0))] + [ANY] * npc + dep_specs,
        out_specs=pl.BlockSpec((nchunk, SEQ, 128), lambda s: (0, 0, 0)),
        out_shape=jax.ShapeDtypeStruct((nchunk, SEQ, 128), F32),
        scratch_shapes=[pltpu.VMEM((SEQ, D_MODEL), F32), pltpu.VMEM((2, SEQ, tk), BF16),
                        pltpu.SemaphoreType.DMA((2,))],
        compiler_params=_params(("arbitrary",)))(wt, *pieces, *dep_args)


def _head_lanes(lanes, hh):
    return lanes >= 64 if hh == 1 else lanes < 64


def _head_rows(x, lanes, hh, pair):
    if not pair:
        return jnp.max(x, axis=1, keepdims=True)
    return jnp.max(jnp.where(_head_lanes(lanes, hh), x, -jnp.inf), axis=1, keepdims=True)


def _mask_head(x, lanes, hh, pair, scale=1.0):
    if not pair:
        return x
    xf = x.astype(F32) if scale == 1.0 else x.astype(F32) * scale
    return jnp.where(_head_lanes(lanes, hh), xf, 0.0).astype(BF16)


def _merge_heads(parts, lanes, pair):
    if not pair:
        return parts[0]
    return jnp.where(lanes < 64, parts[0], parts[1])


def _window(mode, qi, tq, mlen, tk):
    if mode == "dil":
        q0 = qi * tq
        seg = (q0 // mlen) * mlen
        ks = jnp.clip(q0 - REACH, seg, seg + mlen - tk)
        return pl.multiple_of(ks, 64)
    if mode == "na":
        r_start = jnp.clip(qi - NA_ROWS // 2, 0, SEQ // GRID_W - NA_ROWS)
        return pl.multiple_of(r_start * GRID_W, 64)
    return 0


def _band_mask(qi, tq, tk, ks):
    qpos = qi * tq + _iota((tq, tk), 0)
    kpos = ks + _iota((tq, tk), 1)
    return jnp.where(jnp.abs(qpos - kpos) <= REACH, 0.0, NEG).astype(F32)


def _scores(mode, qh, k, sscale, band, qi, bias_ref, hh):
    s = lax.dot_general(qh, k, NT, preferred_element_type=F32)
    if sscale != 1.0:
        s = s * sscale
    if mode == "dil":
        s = s + band
    elif mode == "na":
        off = qi - jnp.clip(qi - NA_ROWS // 2, 0, SEQ // GRID_W - NA_ROWS)
        s = s + bias_ref[hh, off]
    return s


def _attn_cfg(mode, d):
    if mode == "dil":
        mlen = SEQ // d
        return dict(pair=True, tq=128, tk=min(256, mlen), mlen=mlen, lk=SEQ, scale=HEAD_DIM ** -0.5, units=4,
                    nsub=ATTN_SUBTILES)
    if mode == "na":
        return dict(pair=True, tq=GRID_W, tk=NA_ROWS * GRID_W, mlen=SEQ, lk=SEQ, scale=HEAD_DIM ** -0.5, units=4,
                    nsub=ATTN_SUBTILES)
    return dict(pair=False, tq=128, tk=MEM_LEN, mlen=SEQ, lk=MEM_LEN, scale=128 ** -0.5, units=4,
                nsub=ATTN_SUBTILES)


ATTN_SUBTILES = 4


def _attn_fwd(name, mode, q_arr, k_arr, v_arr, qcol, kcol, vcol, d=1, bias=None):
    cfg = _attn_cfg(mode, d)
    pair, tq, tk, mlen, lk, scale = cfg["pair"], cfg["tq"], cfg["tk"], cfg["mlen"], cfg["lk"], cfg["scale"]
    qscale, sscale = (scale, 1.0) if pair else (1.0, scale)
    nh = 2 if pair else 1
    nsub = cfg["nsub"]
    rows = nsub * tq

    def body(*refs):
        if mode == "na":
            q_ref, k_ref, v_ref, bias_ref, o_ref, l_ref = refs
        else:
            q_ref, k_ref, v_ref, o_ref, l_ref = refs
            bias_ref = None
        lanes = _iota((tq, 128), 1)
        chains = [(sub, hh) for sub in range(nsub) for hh in range(nh)]
        qis = [pl.program_id(1) * nsub + sub for sub in range(nsub)]
        kss = [_window(mode, qi, tq, mlen, tk) for qi in qis]
        vs = [v_ref[pl.ds(ks, tk), :] for ks in kss]
        bands = [_band_mask(qi, tq, tk, ks) if mode == "dil" else None for qi, ks in zip(qis, kss)]
        ss = []
        for sub, hh in chains:
            q = q_ref[sub * tq:(sub + 1) * tq, :]
            k = k_ref[pl.ds(kss[sub], tk), :]
            ss.append(_scores(mode, _mask_head(q, lanes, hh, pair, qscale), k, sscale, bands[sub], qis[sub], bias_ref,
                              hh))
        ms = [jnp.max(s, axis=1, keepdims=True) for s in ss]
        ps = [jnp.exp(s - m) for s, m in zip(ss, ms)]
        ls = [jnp.sum(p, axis=1, keepdims=True) for p in ps]
        os_ = [jnp.dot(p.astype(BF16), vs[sub], preferred_element_type=F32) for p, (sub, hh) in zip(ps, chains)]
        for sub in range(nsub):
            sel = [i for i, (s_, hh) in enumerate(chains) if s_ == sub]
            outs = [os_[i] / ls[i] for i in sel]
            lses = [jnp.broadcast_to(ms[i] + jnp.log(ls[i]), (tq, 128)) for i in sel]
            dst = _folded_rows(qis[sub] * tq, tq, d) if mode == "dil" else slice(sub * tq, (sub + 1) * tq)
            o_ref[dst, :] = _merge_heads(outs, lanes, pair)
            l_ref[dst, :] = _merge_heads(lses, lanes, pair)

    in_specs = [pl.BlockSpec((rows, 128), lambda u, i: (i, qcol + u)),
                pl.BlockSpec((lk, 128), lambda u, i: (0, kcol + u)),
                pl.BlockSpec((lk, 128), lambda u, i: (0, vcol + u))]
    args = [q_arr, k_arr, v_arr]
    if mode == "na":
        in_specs.append(pl.BlockSpec((2, NA_ROWS, GRID_W, NA_ROWS * GRID_W), lambda u, i: (u, 0, 0, 0)))
        args.append(bias)
    if mode == "dil":
        out_spec = pl.BlockSpec((SEQ, 128), lambda u, i: (0, u))
    else:
        out_spec = pl.BlockSpec((rows, 128), lambda u, i: (i, u))
    return pl.pallas_call(
        body, name=name, grid=(cfg["units"], SEQ // rows), in_specs=in_specs, out_specs=[out_spec, out_spec],
        out_shape=[jax.ShapeDtypeStruct((SEQ, 512), F32), jax.ShapeDtypeStruct((SEQ, 512), F32)],
        compiler_params=_params(("parallel", "arbitrary")))(*args)


def _attn_bwd(name, mode, q_arr, k_arr, v_arr, qcol, kcol, vcol, do, lse, dp=None, o=None, d=1, bias=None,
              tabs=None):
    cfg = _attn_cfg(mode, d)
    pair, tq, tk, mlen, lk, scale = cfg["pair"], cfg["tq"], cfg["tk"], cfg["mlen"], cfg["lk"], cfg["scale"]
    qscale, sscale = (scale, 1.0) if pair else (1.0, scale)
    nh = 2 if pair else 1
    nsub = cfg["nsub"]
    rows = nsub * tq
    nq = SEQ // rows
    kv_dtype = F32 if mode == "mem" else BF16

    def body(*refs):
        refs = list(refs)
        q_ref, k_ref, v_ref, do_ref, l_ref = refs[:5]
        rest = refs[5:]
        bias_ref = tq_ref = tk_ref = db_ref = None
        if mode == "dil":
            dp_ref, tq_ref, tk_ref, dq_ref, dk_ref, dv_ref, dk_acc, dv_acc = rest
        elif mode == "na":
            o_ref, bias_ref, dq_ref, dk_ref, dv_ref, db_ref, dk_acc, dv_acc = rest
        else:
            o_ref, dq_ref, dk_ref, dv_ref, dk_acc, dv_acc = rest
        step = pl.program_id(1)

        @pl.when(step == 0)
        def _():
            dk_acc[...] = jnp.zeros((lk, 128), F32)
            dv_acc[...] = jnp.zeros((lk, 128), F32)
            if mode == "na":
                db_ref[...] = jnp.zeros(db_ref.shape, F32)

        lanes = _iota((tq, 128), 1)
        lanes_k = _iota((tk, 128), 1)
        chains = [(sub, hh) for sub in range(nsub) for hh in range(nh)]
        qis = [step * nsub + sub for sub in range(nsub)]
        sls = [slice(sub * tq, (sub + 1) * tq) for sub in range(nsub)]
        kss = [_window(mode, qi, tq, mlen, tk) for qi in qis]
        qs = [q_ref[sl, :] for sl in sls]
        ks_ = [k_ref[pl.ds(ks, tk), :] for ks in kss]
        vs = [v_ref[pl.ds(ks, tk), :] for ks in kss]
        dovs, lsevs, dpvs = [], [], []
        for sub in range(nsub):
            if mode == "dil":
                src = _folded_rows(qis[sub] * tq, tq, d)
                dovs.append(do_ref[src, :].astype(BF16))
                lsevs.append(l_ref[src, :])
                dpvs.append(dp_ref[src, :])
            else:
                dovs.append(do_ref[sls[sub], :])
                lsevs.append(l_ref[sls[sub], :])
                dpvs.append(dovs[sub].astype(F32) * o_ref[sls[sub], :])
        bands = [_band_mask(qi, tq, tk, ks) if mode == "dil" else None for qi, ks in zip(qis, kss)]
        ss = [_scores(mode, _mask_head(qs[sub], lanes, hh, pair, qscale), ks_[sub], sscale, bands[sub], qis[sub],
                      bias_ref, hh) for sub, hh in chains]
        dpms = [lax.dot_general(_mask_head(dovs[sub], lanes, hh, pair), vs[sub], NT, preferred_element_type=F32)
                for sub, hh in chains]
        ps = [jnp.exp(s - _head_rows(lsevs[sub], lanes, hh, pair)) for s, (sub, hh) in zip(ss, chains)]
        dphs = []
        for sub, hh in chains:
            if mode == "dil":
                dphs.append(_head_rows(dpvs[sub], lanes, hh, pair))
            elif pair:
                dphs.append(jnp.sum(jnp.where(_head_lanes(lanes, hh), dpvs[sub], 0.0), axis=1, keepdims=True))
            else:
                dphs.append(jnp.sum(dpvs[sub], axis=1, keepdims=True))
        dss = [p * (dpm - dph) for p, dpm, dph in zip(ps, dpms, dphs)]
        if mode == "na":
            for ds, (sub, hh) in zip(dss, chains):
                off = qis[sub] - jnp.clip(qis[sub] - NA_ROWS // 2, 0, SEQ // GRID_W - NA_ROWS)
                db_ref[hh, off] += ds
        dsbs = [ds.astype(BF16) for ds in dss]
        dvs = [lax.dot_general(p.astype(BF16), dovs[sub], TN, preferred_element_type=F32)
               for p, (sub, hh) in zip(ps, chains)]
        dqs = [jnp.dot(dsb, ks_[sub], preferred_element_type=F32) * scale for dsb, (sub, hh) in zip(dsbs, chains)]
        dks = [lax.dot_general(dsb, qs[sub], TN, preferred_element_type=F32) * scale
               for dsb, (sub, hh) in zip(dsbs, chains)]
        for sub in range(nsub):
            sel = [i for i, (s_, hh) in enumerate(chains) if s_ == sub]
            sl = sls[sub]
            dq = _merge_heads([dqs[i] for i in sel], lanes, pair)
            if mode == "dil":
                dq = _rope_t(dq, tq_ref[0, sl, :], tq_ref[1, sl, :], tq_ref[2, sl, :])
            dq_ref[sl, :] = dq.astype(BF16)
            dk_acc[pl.ds(kss[sub], tk), :] += _merge_heads([dks[i] for i in sel], lanes_k, pair)
            dv_acc[pl.ds(kss[sub], tk), :] += _merge_heads([dvs[i] for i in sel], lanes_k, pair)

        @pl.when(step == nq - 1)
        def _():
            dkv = dk_acc[...]
            if mode == "dil":
                dkv = _rope_t(dkv, tk_ref[0], tk_ref[1], tk_ref[2])
            dk_ref[...] = dkv.astype(kv_dtype)
            dv_ref[...] = dv_acc[...].astype(kv_dtype)

    q_spec = pl.BlockSpec((rows, 128), lambda u, i: (i, qcol + u))
    row_spec = pl.BlockSpec((rows, 128), lambda u, i: (i, u))
    kv_out = pl.BlockSpec((lk, 128), lambda u, i: (0, u))
    whole = pl.BlockSpec((SEQ, 128), lambda u, i: (0, u))
    nat_spec = whole if mode == "dil" else row_spec
    in_specs = [q_spec,
                pl.BlockSpec((lk, 128), lambda u, i: (0, kcol + u)),
                pl.BlockSpec((lk, 128), lambda u, i: (0, vcol + u)),
                nat_spec, nat_spec]
    args = [q_arr, k_arr, v_arr, do, lse]
    out_specs = [row_spec, kv_out, kv_out]
    out_shape = [jax.ShapeDtypeStruct((SEQ, 512), BF16), jax.ShapeDtypeStruct((lk, 512), kv_dtype),
                 jax.ShapeDtypeStruct((lk, 512), kv_dtype)]
    if mode == "dil":
        in_specs += [whole, pl.BlockSpec((3, rows, 128), lambda u, i: (0, i, 0)),
                     pl.BlockSpec((3, SEQ, 128), lambda u, i: (0, 0, 0))]
        args += [dp, tabs, tabs]
    elif mode == "na":
        b_spec = pl.BlockSpec((2, NA_ROWS, GRID_W, NA_ROWS * GRID_W), lambda u, i: (u, 0, 0, 0))
        in_specs += [row_spec, b_spec]
        args += [o, bias]
        out_specs.append(b_spec)
        out_shape.append(jax.ShapeDtypeStruct((8, NA_ROWS, GRID_W, NA_ROWS * GRID_W), F32))
    else:
        in_specs.append(row_spec)
        args.append(o)
    return pl.pallas_call(
        body, name=name, grid=(cfg["units"], nq), in_specs=in_specs, out_specs=out_specs, out_shape=out_shape,
        scratch_shapes=[pltpu.VMEM((lk, 128), F32), pltpu.VMEM((lk, 128), F32)],
        compiler_params=_params(("parallel", "arbitrary")))(*args)


def _na_geometry():
    qc = _iota((GRID_W, 128), 0)
    lane = _iota((GRID_W, 128), 1)
    kc = lane & 63
    c_start = jnp.clip(qc - 8, 0, GRID_W - 16)
    valid = jnp.logical_and(kc >= c_start, kc < c_start + 16)
    return lane, valid


def _na_bias(rpb_rows):
    def body(r_ref, o_ref, t_ref):
        lane, valid = _na_geometry()
        for dd in range(14):
            row_a = jnp.broadcast_to(r_ref[dd:dd + 1, :], (GRID_W, 128))
            row_b = jnp.broadcast_to(r_ref[dd + 1:dd + 2, :], (GRID_W, 128))
            both = jnp.where(lane < 64, row_a, pltpu.roll(row_b, 64, 1))
            t = pltpu.roll(both, 128 - 15, 1, stride=1, stride_axis=0)
            t_ref[dd] = jnp.where(valid, t, NEG)
        for off in range(NA_ROWS):
            for p in range(4):
                o_ref[off, :, p * 128:(p + 1) * 128] = t_ref[2 * p - off + 7]

    return pl.pallas_call(
        body, name="na_bias", grid=(8,),
        in_specs=[pl.BlockSpec((None, 16, 128), lambda h: (h, 0, 0))],
        out_specs=pl.BlockSpec((None, NA_ROWS, GRID_W, NA_ROWS * GRID_W), lambda h: (h, 0, 0, 0)),
        out_shape=jax.ShapeDtypeStruct((8, NA_ROWS, GRID_W, NA_ROWS * GRID_W), F32),
        scratch_shapes=[pltpu.VMEM((14, GRID_W, 128), F32)],
        compiler_params=_params(("parallel",)))(rpb_rows)


def _na_bias_bwd(dbias):
    def body(d_ref, o_ref):
        lane, valid = _na_geometry()
        reverse = (_iota((GRID_W, GRID_W), 0) + _iota((GRID_W, GRID_W), 1) == GRID_W - 1).astype(F32)
        o_ref[...] = jnp.zeros((16, 128), F32)
        for dd in range(14):
            t = jnp.zeros((GRID_W, 128), F32)
            for off in range(NA_ROWS):
                for p in range(4):
                    if 2 * p - off + 7 == dd:
                        t = t + d_ref[off, :, p * 128:(p + 1) * 128]
            t = jnp.dot(reverse, jnp.where(valid, t, 0.0), precision=lax.Precision.HIGHEST,
                        preferred_element_type=F32)
            t = pltpu.roll(t, 128 - (GRID_W - 16), 1, stride=1, stride_axis=0)
            o_ref[dd:dd + 1, :] = jnp.sum(t, axis=0, keepdims=True)

    return pl.pallas_call(
        body, name="na_bias_bwd", grid=(8,),
        in_specs=[pl.BlockSpec((None, NA_ROWS, GRID_W, NA_ROWS * GRID_W), lambda h: (h, 0, 0, 0))],
        out_specs=pl.BlockSpec((None, 16, 128), lambda h: (h, 0, 0)),
        out_shape=jax.ShapeDtypeStruct((8, 16, 128), F32),
        compiler_params=_params(("parallel",)))(dbias)


GATE_ROWS = 128


def _group_weights(l0, l1, l2):
    m = jnp.maximum(jnp.maximum(l0, l1), l2)
    e0, e1, e2 = jnp.exp(l0 - m), jnp.exp(l1 - m), jnp.exp(l2 - m)
    inv = 1.0 / (e0 + e1 + e2)
    return e0 * inv, e1 * inv, e2 * inv


def _gate_specs():
    r512 = pl.BlockSpec((GATE_ROWS, 512), lambda i: (i, 0))
    r1024 = pl.BlockSpec((GATE_ROWS, D_MODEL), lambda i: (i, 0))
    silu_cols = [pl.BlockSpec((GATE_ROWS, 512), functools.partial(lambda b, i: (i, b), 13 + b)) for b in range(3)]
    logit_cols = [pl.BlockSpec((GATE_ROWS, D_MODEL), functools.partial(lambda b, i: (i, b), 8 + b)) for b in range(3)]
    return r512, r1024, silu_cols, logit_cols


def _gate_fwd(o_grp, l_grp, out_b, out_c, parts, merge_bias, wts):
    r512, r1024, silu_cols, logit_cols = _gate_specs()

    def body(o0, o1, o2, l0, l1, l2, ob, oc, ga, gb, gc, la, lb, lc, mb, wa, wb, wc,
             oa_ref, ua, ub, uc, za, zb, zc, y_ref):
        w0, w1, w2 = _group_weights(l0[...], l1[...], l2[...])
        out_a = w0 * o0[...] + w1 * o1[...] + w2 * o2[...]
        oa_ref[...] = out_a
        y = jnp.zeros((GATE_ROWS, D_MODEL), F32)
        for b, (ov, g_ref, l_ref, w_ref, u_ref, z_ref) in enumerate(
                ((out_a, ga, la, wa, ua, za), (ob[...], gb, lb, wb, ub, zb), (oc[...], gc, lc, wc, uc, zc))):
            g = g_ref[...].astype(F32)
            u = (ov * (g * _sigmoid(g))).astype(BF16)
            u_ref[...] = u
            z = lax.dot_general(u, w_ref[...], NT, preferred_element_type=F32)
            z_ref[...] = z.astype(BF16)
            gate = _sigmoid(l_ref[...].astype(F32) + mb[b:b + 1, :])
            y = y + gate * z
        y_ref[...] = y.astype(BF16)

    full = lambda shape: pl.BlockSpec(shape, lambda i: (0,) * len(shape))
    in_specs = ([r512] * 8 + silu_cols + logit_cols
                + [full((3, D_MODEL))] + [full((D_MODEL, 512))] * 3)
    out_specs = [r512] * 4 + [r1024] * 4
    out_shape = ([jax.ShapeDtypeStruct((SEQ, 512), F32)] + [jax.ShapeDtypeStruct((SEQ, 512), BF16)] * 3
                 + [jax.ShapeDtypeStruct((SEQ, D_MODEL), BF16)] * 4)
    res = pl.pallas_call(
        body, name="gate_fwd", grid=(SEQ // GATE_ROWS,), in_specs=in_specs, out_specs=out_specs,
        out_shape=out_shape, compiler_params=_params(("parallel",)))(
            *o_grp, *l_grp, out_b, out_c, parts, parts, parts, parts, parts, parts, merge_bias, *wts)
    return res[0], res[1:4], res[4:7], res[7]


def _gate_bwd(dy, z, parts, merge_bias, outs, o_grp, l_grp, wts, head_sum):
    r512, r1024, silu_cols, logit_cols = _gate_specs()

    def body(dy_ref, za, zb, zc, la, lb, lc, mb, oa, ob, oc, ga, gb, gc, o0, o1, o2, l0, l1, l2, wa, wb, wc, hs_ref,
             dla, dlb, dlc, gmb, dza, dzb, dzc, dga, dgb, dgc, do0, do1, do2, dp0, dp1, dp2, dob, doc):
        dyv = dy_ref[...].astype(F32)
        rows = []
        dos = []
        for b, (z_ref, l_ref, ov_ref, g_ref, w_ref, dl_ref, dz_ref, dg_ref) in enumerate(
                ((za, la, oa, ga, wa, dla, dza, dga), (zb, lb, ob, gb, wb, dlb, dzb, dgb),
                 (zc, lc, oc, gc, wc, dlc, dzc, dgc))):
            gate = _sigmoid(l_ref[...].astype(F32) + mb[b:b + 1, :])
            dl = dyv * z_ref[...].astype(F32) * gate * (1.0 - gate)
            dl_ref[...] = dl.astype(BF16)
            rows.append(jnp.sum(dl, axis=0, keepdims=True))
            dz = (dyv * gate).astype(BF16)
            dz_ref[...] = dz
            du = jnp.dot(dz, w_ref[...], preferred_element_type=F32)
            g = g_ref[...].astype(F32)
            sg = _sigmoid(g)
            dos.append(du * (g * sg))
            dg_ref[...] = (du * ov_ref[...] * (sg * (1.0 + g * (1.0 - sg)))).astype(BF16)

        @pl.when(pl.program_id(0) == 0)
        def _():
            gmb[...] = jnp.zeros((3, D_MODEL), F32)

        for b in range(3):
            gmb[b:b + 1, :] += rows[b]
        dob[...] = dos[1].astype(BF16)
        doc[...] = dos[2].astype(BF16)
        doa = dos[0]
        row_term = jnp.dot(doa * oa[...], hs_ref[...], precision=lax.Precision.HIGHEST, preferred_element_type=F32)
        ws = _group_weights(l0[...], l1[...], l2[...])
        for wg, do_ref, dp_ref in zip(ws, (do0, do1, do2), (dp0, dp1, dp2)):
            do_ref[...] = wg * doa
            dp_ref[...] = wg * row_term

    full = lambda shape: pl.BlockSpec(shape, lambda i: (0,) * len(shape))
    acc = pl.BlockSpec((3, D_MODEL), lambda i: (0, 0))
    in_specs = ([r1024] * 4 + logit_cols + [full((3, D_MODEL))] + [r512] * 3 + silu_cols + [r512] * 6
                + [full((D_MODEL, 512))] * 3 + [full((512, 512))])
    out_specs = [r1024] * 3 + [acc] + [r1024] * 3 + [r512] * 11
    out_shape = ([jax.ShapeDtypeStruct((SEQ, D_MODEL), BF16)] * 3 + [jax.ShapeDtypeStruct((3, D_MODEL), F32)]
                 + [jax.ShapeDtypeStruct((SEQ, D_MODEL), BF16)] * 3 + [jax.ShapeDtypeStruct((SEQ, 512), BF16)] * 3
                 + [jax.ShapeDtypeStruct((SEQ, 512), F32)] * 6 + [jax.ShapeDtypeStruct((SEQ, 512), BF16)] * 2)
    res = pl.pallas_call(
        body, name="gate_bwd", grid=(SEQ // GATE_ROWS,), in_specs=in_specs, out_specs=out_specs,
        out_shape=out_shape, compiler_params=_params(("arbitrary",)))(
            dy, *z, parts, parts, parts, merge_bias, *outs, parts, parts, parts, *o_grp, *l_grp, *wts, head_sum)
    return res[0:3], res[3], res[4:7], res[7:10], res[10:13], res[13:16], res[16], res[17]


def _post(y2, x, target, gain):
    rows = 256

    def body(y_ref, x_ref, t_ref, g_ref, do_ref, dy_ref, l_ref, gg_ref):
        yv = y_ref[...]
        rstd = lax.rsqrt(jnp.mean(yv * yv, axis=1, keepdims=True) + EPS)
        yn = yv * rstd
        gv = g_ref[...]
        err = x_ref[...] + yn * gv - t_ref[...]
        dout = err * (1.0 / D_MODEL)
        do_ref[...] = dout
        dn = dout * gv
        dy_ref[...] = (rstd * (dn - yn * jnp.mean(dn * yn, axis=1, keepdims=True))).astype(BF16)

        @pl.when(pl.program_id(0) == 0)
        def _():
            l_ref[...] = jnp.zeros((1, D_MODEL), F32)
            gg_ref[...] = jnp.zeros((1, D_MODEL), F32)

        l_ref[...] += jnp.sum(err * err, axis=0, keepdims=True)
        gg_ref[...] += jnp.sum(dout * yn, axis=0, keepdims=True)

    row = pl.BlockSpec((rows, D_MODEL), lambda i: (i, 0))
    vec = pl.BlockSpec((1, D_MODEL), lambda i: (0, 0))
    return pl.pallas_call(
        body, name="post", grid=(SEQ // rows,), in_specs=[row, row, row, vec], out_specs=[row, row, vec, vec],
        out_shape=[jax.ShapeDtypeStruct((SEQ, D_MODEL), F32), jax.ShapeDtypeStruct((SEQ, D_MODEL), BF16),
                   jax.ShapeDtypeStruct((1, D_MODEL), F32), jax.ShapeDtypeStruct((1, D_MODEL), F32)],
        compiler_params=_params(("arbitrary",)))(y2, x, target, gain)


def _local_step(x, mem, target, pre_norm, mem_norm, post_norm, na_rpb, wt_in, late_weights, dep_in=None,
                reduce_start=None):
    tabs = _rope_tables()
    hs, hst = _prenorm_fold(x, pre_norm)
    parts = _in_proj(hs, wt_in, tabs, dep_in)

    o_grp, l_grp = [], []
    for g, d in enumerate(DILATIONS):
        o, l = _attn_fwd("dil_fwd_%d" % g, "dil", parts, parts, parts, 12 * g, 12 * g + 4, 12 * g + 8, d=d)
        o_grp.append(o)
        l_grp.append(l)
    bias = _na_bias(jnp.pad(na_rpb, ((0, 0), (0, 1), (0, 128 - 31))))
    out_b, lse_b = _attn_fwd("na_fwd", "na", parts, parts, parts, 36, 40, 44, bias=bias)
    merge_bias, w_kv, wt_a, wt_b, wt_c, w_out = late_weights(out_b)
    memn = _rmsnorm_fwd("memnorm", mem, mem_norm, MEM_LEN)
    kv_m = _mm_simple("mem_kv", memn, w_kv, NN, BF16, MEM_LEN, 512, D_MODEL)
    out_c, lse_c = _attn_fwd("mem_fwd", "mem", parts, kv_m, kv_m, 48, 0, 4)

    wts = (wt_a, wt_b, wt_c)
    out_a, u, z, y = _gate_fwd(o_grp, l_grp, out_b, out_c, parts, merge_bias, wts)
    y2 = _mm_simple("out_proj", y, w_out, NN, F32, 512, D_MODEL, D_MODEL)
    dout, dy2, err_sq, g_post = _post(y2, x, target, post_norm)
    loss = 0.5 * jnp.sum(err_sq) / D_MODEL

    dy = _mm_simple("out_proj_dx", dy2, w_out, NT, BF16, 512, D_MODEL, D_MODEL)
    g_w_out = _mm_simple("out_proj_dw", y, dy2, TN, BF16, D_MODEL, 512, 512)

    rr = _iota((512, 512), 0) // HEAD_DIM
    cc = _iota((512, 512), 1) // HEAD_DIM
    head_sum = (rr == cc).astype(F32)
    dlog, g_mb, dz, dg, do_grp, dp_grp, do_b, do_c = _gate_bwd(
        dy, z, parts, merge_bias, (out_a, out_b, out_c), o_grp, l_grp, wts, head_sum)
    g_wt = [_mm_simple("branch_dw_%d" % b, dz[b], u[b], TN, BF16, D_MODEL, 512, 512) for b in range(3)]

    dqkv = []
    for g, d in enumerate(DILATIONS):
        dq, dk, dv = _attn_bwd("dil_bwd_%d" % g, "dil", parts, parts, parts, 12 * g, 12 * g + 4, 12 * g + 8,
                               do_grp[g], l_grp[g], dp=dp_grp[g], d=d, tabs=tabs[g])
        dqkv += [dq, dk, dv]
    dq_b, dk_b, dv_b, dbias = _attn_bwd("na_bwd", "na", parts, parts, parts, 36, 40, 44, do_b, lse_b, o=out_b,
                                        bias=bias)
    g_rpb_t = _na_bias_bwd(dbias)
    g_rpb = g_rpb_t[:, :15, :31] + jnp.pad(g_rpb_t[:, :14, 64:95], ((0, 0), (1, 0), (0, 0)))
    dq_c, dk_m, dv_m = _attn_bwd("mem_bwd", "mem", parts, kv_m, kv_m, 48, 0, 4, do_c, lse_c, o=out_c)

    dkv = jnp.concatenate([dk_m, dv_m], axis=1).astype(BF16)
    g_w_kv = _mm_simple("mem_kv_dw", memn, dkv, TN, BF16, D_MODEL, 512, MEM_LEN)
    dmemn = _mm_simple("mem_kv_dx", dkv, w_kv, NT, F32, MEM_LEN, 512, D_MODEL)
    g_mem_norm = _memnorm_bwd(mem, dmemn)

    grads = dict(w_kv=g_w_kv, wt_a=g_wt[0], wt_b=g_wt[1], wt_c=g_wt[2], w_out=g_w_out, merge_bias=g_mb,
                 mem_norm=g_mem_norm, post_norm=g_post, na_rpb=g_rpb)
    dep = reduce_start(grads) if reduce_start is not None else None
    dparts = dqkv + [dq_b, dk_b, dv_b, dq_c] + list(dg) + list(dlog)
    grads["wt_in"] = _in_proj_dw(dparts, hst, dep)
    dep = reduce_start(grads) if reduce_start is not None else None
    dh = _in_proj_dh(dparts, wt_in, dep)
    grad_x, grads["pre_norm"] = _prenorm_bwd(x, pre_norm, dh, dout)
    return loss, grad_x, grads


ANY = pl.BlockSpec(memory_space=pl.ANY)


def _place():
    return lax.axis_index("x"), lax.axis_index("y"), lax.axis_index("c")


def _all_gather(shard):
    r = shard.shape[0]
    half = r // 2

    def body(src, out, send_sems, recv_sems, local_sem):
        x, y, c = _place()
        me, sib = (x, y, c), (x, y, 1 - c)
        xn, yn, dg = (1 - x, y, c), (x, 1 - y, c), (1 - x, 1 - y, c)

        def rows(dev, part=None):
            blk = out.at[4 * dev[0] + 2 * dev[1] + dev[2]]
            return blk if part is None else blk.at[pl.ds(part * half, half)]

        def copy(k, dev, part, to, own=False):
            return pltpu.make_async_remote_copy(
                src_ref=src if own else rows(dev, part), dst_ref=rows(dev, part),
                send_sem=send_sems.at[k], recv_sem=recv_sems.at[k], device_id=to, device_id_type=MESH_ID)

        def other(dev):
            return (dev[0], dev[1], 1 - dev[2])

        mine = pltpu.make_async_copy(src, rows(me), local_sem)
        mine.start()
        sent = [copy(0, me, None, sib, own=True), copy(1, me, None, xn, own=True), copy(2, me, None, yn, own=True)]
        for cp in sent:
            cp.start()
        copy(1, xn, None, me).wait_recv()
        sent += [copy(3, xn, 0, yn), copy(5, xn, None, sib)]
        sent[-2].start()
        sent[-1].start()
        copy(2, yn, None, me).wait_recv()
        sent += [copy(4, yn, 1, xn), copy(6, yn, None, sib)]
        sent[-2].start()
        sent[-1].start()
        copy(3, dg, 0, me).wait_recv()
        sent.append(copy(7, dg, 0, sib))
        sent[-1].start()
        copy(4, dg, 1, me).wait_recv()
        sent.append(copy(8, dg, 1, sib))
        sent[-1].start()
        copy(0, sib, None, me).wait_recv()
        copy(5, other(xn), None, me).wait_recv()
        copy(6, other(yn), None, me).wait_recv()
        copy(7, other(dg), 0, me).wait_recv()
        copy(8, other(dg), 1, me).wait_recv()
        for cp in sent:
            cp.wait_send()
        mine.wait()

    return pl.pallas_call(
        body, name="all_gather", in_specs=[ANY], out_specs=ANY,
        out_shape=jax.ShapeDtypeStruct((N_DEV,) + shard.shape, shard.dtype),
        scratch_shapes=[pltpu.SemaphoreType.DMA((9,)), pltpu.SemaphoreType.DMA((9,)), pltpu.SemaphoreType.DMA])(shard)


def _exchange_sibling(name, terms):
    nt = len(terms)

    def body(*refs):
        srcs, outs = refs[:nt], refs[nt:2 * nt]
        send_sems, recv_sems = refs[2 * nt:]
        x, y, c = _place()
        copies = []
        for q in range(4):
            for t in range(nt):
                copies.append(pltpu.make_async_remote_copy(
                    src_ref=srcs[t].at[2 * q + 1 - c], dst_ref=outs[t].at[q],
                    send_sem=send_sems.at[q * nt + t], recv_sem=recv_sems.at[q * nt + t],
                    device_id=(x, y, 1 - c), device_id_type=MESH_ID))
        for cp in copies:
            cp.start()
        for cp in copies:
            cp.wait()

    return pl.pallas_call(
        body, name=name, in_specs=[ANY] * nt, out_specs=[ANY] * nt,
        out_shape=[jax.ShapeDtypeStruct((4,) + s.shape[1:], s.dtype) for s in terms],
        scratch_shapes=[pltpu.SemaphoreType.DMA((4 * nt,)), pltpu.SemaphoreType.DMA((4 * nt,))])(*terms)


HBM = pl.BlockSpec(memory_space=pltpu.HBM)
SEM = pl.BlockSpec(memory_space=pltpu.SEMAPHORE)
DATAFLOW = pltpu.SideEffectType.DATAFLOW_SIDE_EFFECTING


def _split_copies(kind, srcs, lands, send_sems, recv_sems):
    nt = len(srcs)
    x, y, c = _place()
    copies = []
    if kind == "gather":
        me = 4 * x + 2 * y + c
        for mask in range(1, 8):
            fx, fy, fc = (mask >> 2) & 1, (mask >> 1) & 1, mask & 1
            to = (1 - x if fx else x, 1 - y if fy else y, 1 - c if fc else c)
            for t in range(nt):
                k = (mask - 1) * nt + t
                copies.append(pltpu.make_async_remote_copy(
                    src_ref=srcs[t], dst_ref=lands[t].at[me], send_sem=send_sems.at[k], recv_sem=recv_sems.at[k],
                    device_id=to, device_id_type=MESH_ID))
    else:
        for s, (tx, ty) in enumerate([(1 - x, y), (x, 1 - y), (1 - x, 1 - y)]):
            for t in range(nt):
                k = s * nt + t
                copies.append(pltpu.make_async_remote_copy(
                    src_ref=srcs[t].at[2 * tx + ty], dst_ref=lands[t].at[s], send_sem=send_sems.at[k],
                    recv_sem=recv_sems.at[k], device_id=(tx, ty, c), device_id_type=MESH_ID))
    return copies


def _split_count(kind, nt):
    return (7 if kind == "gather" else 3) * nt


def _exchange_start(name, kind, srcs, land_shapes, after=None):
    nt = len(srcs)
    n = _split_count(kind, nt)
    dep_specs, dep_args = _dep_operand(after)
    nd = len(dep_args)

    def body(*refs):
        src_refs, land_refs = refs[:nt], refs[nt:2 * nt]
        send_sems, recv_sems = refs[2 * nt + nd], refs[2 * nt + nd + 1]
        token = refs[-1]
        for cp in _split_copies(kind, src_refs, land_refs, send_sems, recv_sems):
            cp.start()
        token[...] = jnp.zeros_like(token)

    lands = [pltpu.with_memory_space_constraint(lax.empty(s.shape, s.dtype), pltpu.HBM) for s in land_shapes]
    res = pl.pallas_call(
        body, name=name,
        out_shape=(pltpu.SemaphoreType.DMA((n,)), pltpu.SemaphoreType.DMA((n,)),
                   *[pltpu.HBM(s.shape, s.dtype) for s in srcs], *[pltpu.HBM(s.shape, s.dtype) for s in land_shapes],
                   jax.ShapeDtypeStruct((8, 128), F32)),
        in_specs=[HBM] * (2 * nt) + dep_specs,
        out_specs=(SEM, SEM, *([HBM] * (2 * nt)), pl.BlockSpec(memory_space=pltpu.VMEM)),
        input_output_aliases={i: 2 + i for i in range(2 * nt)},
        compiler_params=pltpu.CompilerParams(has_side_effects=DATAFLOW))(
            *[pltpu.with_memory_space_constraint(s, pltpu.HBM) for s in srcs], *lands, *dep_args)
    return res[0], res[1], list(res[2:2 + nt]), list(res[2 + nt:2 + 2 * nt]), res[-1]


def _exchange_wait(name, kind, send_sems, recv_sems, srcs, lands, after):
    nt = len(srcs)

    def body(*refs):
        src_refs, land_refs = refs[:nt], refs[nt:2 * nt]
        s_sems, r_sems = refs[2 * nt], refs[2 * nt + 1]
        for cp in _split_copies(kind, src_refs, land_refs, s_sems, r_sems):
            cp.wait_send()
            cp.wait_recv()

    res = pl.pallas_call(
        body, name=name,
        out_shape=tuple(pltpu.HBM(s.shape, s.dtype) for s in list(srcs) + list(lands)),
        in_specs=[HBM] * (2 * nt) + [SEM, SEM, pl.BlockSpec(memory_space=pl.ANY)],
        out_specs=tuple([HBM] * (2 * nt)),
        input_output_aliases={i: i for i in range(2 * nt)},
        compiler_params=pltpu.CompilerParams(has_side_effects=DATAFLOW))(
            *srcs, *lands, send_sems, recv_sems, after)
    return list(res[:nt]), list(res[nt:])


def _add_sibling(name, term, recv, rows):
    _, r, w = term.shape
    cidx = lax.axis_index("c").astype(jnp.int32).reshape(1)

    def body(c_ref, a_ref, b_ref, o_ref):
        o_ref[...] = (a_ref[...].astype(F32) + b_ref[...].astype(F32)).astype(o_ref.dtype)

    grid_spec = pltpu.PrefetchScalarGridSpec(
        num_scalar_prefetch=1, grid=(4, r // rows),
        in_specs=[pl.BlockSpec((None, rows, w), lambda q, i, c_ref: (2 * q + c_ref[0], i, 0)),
                  pl.BlockSpec((None, rows, w), lambda q, i, c_ref: (q, i, 0))],
        out_specs=pl.BlockSpec((None, rows, w), lambda q, i, c_ref: (q, i, 0)))
    return pl.pallas_call(
        body, name=name, grid_spec=grid_spec, out_shape=jax.ShapeDtypeStruct((4, r, w), term.dtype),
        compiler_params=_params(("parallel", "parallel")))(cidx, term, recv)


def _add_chips(name, sums, recv, rows):
    _, r, w = sums.shape
    qidx = (2 * lax.axis_index("x") + lax.axis_index("y")).astype(jnp.int32).reshape(1)

    def body(q_ref, a_ref, b_ref, o_ref):
        o_ref[...] = ((a_ref[...].astype(F32) + b_ref[0].astype(F32))
                      + (b_ref[1].astype(F32) + b_ref[2].astype(F32)))

    grid_spec = pltpu.PrefetchScalarGridSpec(
        num_scalar_prefetch=1, grid=(r // rows,),
        in_specs=[pl.BlockSpec((None, rows, w), lambda i, q_ref: (q_ref[0], i, 0)),
                  pl.BlockSpec((3, rows, w), lambda i, q_ref: (0, i, 0))],
        out_specs=pl.BlockSpec((rows, w), lambda i, q_ref: (i, 0)))
    return pl.pallas_call(
        body, name=name, grid_spec=grid_spec, out_shape=jax.ShapeDtypeStruct((r, w), F32),
        compiler_params=_params(("parallel",)))(qidx, sums, recv)


def _rs_rows(a):
    return SHARD_IN // 4 if a.shape[1] == SHARD_IN else a.shape[1]


def _reduce_scatter_start(tag, names, terms):
    recv1 = _exchange_sibling("exchange_sibling_" + tag, terms)
    sums = [_add_sibling("add_sibling_" + n, t, r, _rs_rows(t)) for n, t, r in zip(names, terms, recv1)]
    lands = [jax.ShapeDtypeStruct((3,) + s.shape[1:], s.dtype) for s in sums]
    send_sems, recv_sems, sums, lands, token = _exchange_start("exchange_chips_start_" + tag, "chips", sums, lands)
    return (tag, names, send_sems, recv_sems, sums, lands), token


def _reduce_scatter_wait(state, after):
    tag, names, send_sems, recv_sems, sums, lands = state
    sums, recv2 = _exchange_wait("exchange_chips_wait_" + tag, "chips", send_sems, recv_sems, sums, lands, after)
    return names, sums, recv2


def _reduce_scatter_finish(state, after):
    names, sums, recv2 = _reduce_scatter_wait(state, after)
    return [_add_chips("add_chips_" + n, s, r, _rs_rows(s)) for n, s, r in zip(names, sums, recv2)]


def _adamw(name, w, g, m, v):
    def body(w_ref, g_ref, m_ref, v_ref, d_ref, nm_ref, nv_ref):
        d_ref[...], nm_ref[...], nv_ref[...] = _adam_math(w_ref[...], g_ref[...], m_ref[...], v_ref[...])

    return pl.pallas_call(
        body, name=name, out_shape=[jax.ShapeDtypeStruct(w.shape, F32)] * 3, compiler_params=_params())(w, g, m, v)


def _adam_math(w, g, m, v):
    nm = ADAM_B1 * m + (1.0 - ADAM_B1) * g
    nv = ADAM_B2 * v + (1.0 - ADAM_B2) * (g * g)
    c1 = 1.0 - ADAM_B1 ** ADAM_STEP
    c2 = 1.0 - ADAM_B2 ** ADAM_STEP
    return -ADAM_LR * ((nm / c1) / (jnp.sqrt(nv / c2) + ADAM_EPS) + ADAM_WD * w), nm, nv


def _adamw_chips_t(name, sums, recv, w, m, v, cols):
    r, c = w.shape
    qidx = (2 * lax.axis_index("x") + lax.axis_index("y")).astype(jnp.int32).reshape(1)

    def body(q_ref, a_ref, b_ref, w_ref, m_ref, v_ref, g_ref, d_ref, nm_ref, nv_ref):
        gt = (a_ref[...].astype(F32) + b_ref[0].astype(F32)) + (b_ref[1].astype(F32) + b_ref[2].astype(F32))
        g = gt.T
        g_ref[...] = g
        d_ref[...], nm_ref[...], nv_ref[...] = _adam_math(w_ref[...], g, m_ref[...], v_ref[...])

    row = pl.BlockSpec((cols, c), lambda i, q_ref: (i, 0))
    grid_spec = pltpu.PrefetchScalarGridSpec(
        num_scalar_prefetch=1, grid=(r // cols,),
        in_specs=[pl.BlockSpec((None, c, cols), lambda i, q_ref: (q_ref[0], 0, i)),
                  pl.BlockSpec((3, c, cols), lambda i, q_ref: (0, 0, i)), row, row, row],
        out_specs=[row] * 4)
    return pl.pallas_call(
        body, name=name, grid_spec=grid_spec, out_shape=[jax.ShapeDtypeStruct((r, c), F32)] * 4,
        compiler_params=_params(("parallel",)))(qidx, sums, recv, w, m, v)


def _sum_devices(gathered):
    def body(g_ref, o_ref):
        acc = g_ref[0]
        for j in range(1, N_DEV):
            acc = acc + g_ref[j]
        o_ref[...] = acc

    return pl.pallas_call(
        body, name="sum_devices", out_shape=jax.ShapeDtypeStruct(gathered.shape[1:], F32),
        compiler_params=_params())(gathered)


def _rows128(a, rows):
    flat = a.reshape(-1)
    return jnp.pad(flat, (0, rows * 128 - flat.shape[0])).reshape(rows, 128)


def kernel(x, mem, pre_norm, w_in, merge_bias, na_rpb, mem_norm, w_mem_kv, w_branch_a, w_branch_b, w_branch_c, w_out, post_norm, loss_target, m_pre_norm, m_w_in, m_merge_bias, m_na_rpb, m_mem_norm, m_w_mem_kv, m_w_branch_a, m_w_branch_b, m_w_branch_c, m_w_out, m_post_norm, v_pre_norm, v_w_in, v_merge_bias, v_na_rpb, v_mem_norm, v_w_mem_kv, v_w_branch_a, v_w_branch_b, v_w_branch_c, v_w_out, v_post_norm):
    wt_in_s = w_in[0].T.astype(BF16)
    rows_s = jnp.concatenate([w_mem_kv[0], w_out[0]], axis=0).astype(BF16)
    cols_s = jnp.concatenate([w_branch_a[0].T, w_branch_b[0].T, w_branch_c[0].T], axis=0).astype(BF16)
    mb_s = jnp.pad(merge_bias[0], ((0, 5), (0, 0)))
    wt_in = _all_gather(wt_in_s).reshape(N_IN, D_MODEL)

    late_own = [rows_s, cols_s, mb_s]
    late_lands = [jax.ShapeDtypeStruct((N_DEV,) + s.shape, s.dtype) for s in late_own]
    l_send, l_recv, late_own, late_lands, late_token = _exchange_start("gather_late_start", "gather", late_own,
                                                                       late_lands, after=wt_in)
    me = 4 * lax.axis_index("x") + 2 * lax.axis_index("y") + lax.axis_index("c")

    def late_weights(after):
        own, lands = _exchange_wait("gather_late_wait", "gather", l_send, l_recv, late_own, late_lands, after)
        g_rows, g_cols, g_mb = [lax.dynamic_update_slice(land, o[None], (me, 0, 0)) for land, o in zip(lands, own)]
        return (g_mb[:, :3].transpose(1, 0, 2).reshape(3, D_MODEL),
                g_rows[:, :128].reshape(D_MODEL, D_MODEL), g_cols[:, 0:128].reshape(D_MODEL, 512),
                g_cols[:, 128:256].reshape(D_MODEL, 512), g_cols[:, 256:384].reshape(D_MODEL, 512),
                g_rows[:, 128:].reshape(D_MODEL, D_MODEL))

    rs_state = []

    def reduce_start(grads):
        if "wt_in" in grads:
            state, token = _reduce_scatter_start("w_in", ["w_in"],
                                                 [grads["wt_in"].reshape(N_DEV, SHARD_IN, D_MODEL)])
        else:
            gmb_t = jnp.pad(grads["merge_bias"].reshape(3, N_DEV, 128).transpose(1, 0, 2), ((0, 0), (0, 5), (0, 0)))
            names = ["w_kv", "w_out", "a", "b", "c", "mb"]
            terms = [grads["w_kv"].reshape(N_DEV, 128, D_MODEL), grads["w_out"].reshape(N_DEV, 128, D_MODEL),
                     grads["wt_a"].reshape(N_DEV, 128, 512), grads["wt_b"].reshape(N_DEV, 128, 512),
                     grads["wt_c"].reshape(N_DEV, 128, 512), gmb_t]
            state, token = _reduce_scatter_start("rest", names, terms)
        rs_state.append(state)
        return token

    loss_term, grad_x, grads = _local_step(
        x[0], mem[0], loss_target[0], pre_norm, mem_norm, post_norm, na_rpb[0], wt_in, late_weights,
        dep_in=late_token, reduce_start=reduce_start)

    small = jnp.concatenate([_rows128(grads["pre_norm"], 8), _rows128(grads["mem_norm"], 8),
                             _rows128(grads["post_norm"], 8), _rows128(grads["na_rpb"], 32),
                             _rows128(loss_term, 8)], axis=0)
    s_send, s_recv, s_own, s_land, s_token = _exchange_start(
        "gather_small_start", "gather", [small], [jax.ShapeDtypeStruct((N_DEV,) + small.shape, F32)])
    grad = {}
    weights = {
        "pre_norm": (pre_norm, m_pre_norm, v_pre_norm), "w_in": (w_in, m_w_in, v_w_in),
        "merge_bias": (merge_bias, m_merge_bias, v_merge_bias), "na_rpb": (na_rpb, m_na_rpb, v_na_rpb),
        "mem_norm": (mem_norm, m_mem_norm, v_mem_norm), "w_mem_kv": (w_mem_kv, m_w_mem_kv, v_w_mem_kv),
        "w_branch_a": (w_branch_a, m_w_branch_a, v_w_branch_a), "w_branch_b": (w_branch_b, m_w_branch_b, v_w_branch_b),
        "w_branch_c": (w_branch_c, m_w_branch_c, v_w_branch_c), "w_out": (w_out, m_w_out, v_w_out),
        "post_norm": (post_norm, m_post_norm, v_post_norm)}
    order = ["pre_norm", "w_in", "merge_bias", "na_rpb", "mem_norm", "w_mem_kv", "w_branch_a", "w_branch_b",
             "w_branch_c", "w_out", "post_norm"]
    delta, new_m, new_v = {}, {}, {}

    def update(n):
        w, m, v = weights[n]
        shape = w.shape
        two_d = (-1, shape[-1])
        dl, nm, nv = _adamw("adamw_" + n, w.reshape(two_d), grad[n].reshape(two_d), m.reshape(two_d),
                            v.reshape(two_d))
        delta[n], new_m[n], new_v[n] = dl.reshape(shape), nm.reshape(shape), nv.reshape(shape)

    g_kv, g_out, gt_a, gt_b, gt_c, g_mb8 = _reduce_scatter_finish(rs_state[0], s_token)
    grad.update({"merge_bias": g_mb8[:3][None], "w_mem_kv": g_kv[None], "w_branch_a": gt_a.T[None],
                 "w_branch_b": gt_b.T[None], "w_branch_c": gt_c.T[None], "w_out": g_out[None]})
    rest = ("merge_bias", "w_mem_kv", "w_branch_a", "w_branch_b", "w_branch_c", "w_out")
    for n in rest:
        update(n)
    all_rest = jnp.stack([a[n].reshape(-1)[0] for n in rest for a in (delta, new_m, new_v)]).sum(keepdims=True)
    s_own, s_land = _exchange_wait("gather_small_wait", "gather", s_send, s_recv, s_own, s_land, all_rest)
    total = _sum_devices(lax.dynamic_update_slice(s_land[0], s_own[0][None], (me, 0, 0)))
    loss = total[56, 0]
    grad.update({"pre_norm": total[0:8].reshape(1, D_MODEL), "mem_norm": total[8:16].reshape(1, D_MODEL),
                 "post_norm": total[16:24].reshape(1, D_MODEL),
                 "na_rpb": total[24:56].reshape(-1)[:8 * 15 * 31].reshape(1, 8, 15, 31)})
    for n in ("pre_norm", "na_rpb", "mem_norm", "post_norm"):
        update(n)
    _, sums_in, recv_in = _reduce_scatter_wait(rs_state[1], delta["post_norm"])
    g, dl, nm, nv = _adamw_chips_t("adamw_w_in", sums_in[0], recv_in[0], w_in[0], m_w_in[0], v_w_in[0], 256)
    grad["w_in"], delta["w_in"], new_m["w_in"], new_v["w_in"] = g[None], dl[None], nm[None], nv[None]

    return (loss, grad_x[None], *[grad[n] for n in order], *[delta[n] for n in order],
            *[new_m[n] for n in order], *[new_v[n] for n in order])
```

```python
import functools

import numpy as np
import jax
import jax.numpy as jnp
from jax import lax
from jax.experimental import pallas as pl
from jax.experimental.pallas import tpu as pltpu

F32 = jnp.float32
BF16 = jnp.bfloat16

SEQ = 2048
D_MODEL = 1024
N_IN = 11264
N_DEV = 8
SHARD_IN = N_IN // N_DEV
HEAD_DIM = 64
GRID_W = 64
NA_ROWS = 8
MEM_LEN = 256
DILATIONS = (1, 4, 16)
REACH = 64
ROPE_THETA = 500000.0
ROPE_DIM = 16
EPS = 1e-6
NEG = -1e30
ADAM_LR = 0.001
ADAM_B1 = 0.9
ADAM_B2 = 0.999
ADAM_EPS = 1e-08
ADAM_WD = 0.01
ADAM_STEP = 10

VMEM_LIMIT_BYTES = 56 * 1024 * 1024
MESH_ID = pl.DeviceIdType.MESH

NN = (((1,), (0,)), ((), ()))
NT = (((1,), (1,)), ((), ()))
TN = (((0,), (0,)), ((), ()))


def _params(sem=None):
    return pltpu.CompilerParams(dimension_semantics=sem, vmem_limit_bytes=VMEM_LIMIT_BYTES)


def _iota(shape, dim):
    return lax.broadcasted_iota(jnp.int32, shape, dim)


def _sigmoid(x):
    return 1.0 / (1.0 + jnp.exp(-x))


def _rope_tables():
    half = ROPE_DIM // 2
    inv = (ROPE_THETA ** (-np.arange(half, dtype=np.float64) * 2.0 / ROPE_DIM)).astype(np.float32)
    pos = np.arange(SEQ, dtype=np.float32)
    ang = pos[:, None] * inv[None, :]
    cos, sin = np.cos(ang), np.sin(ang)
    zeros = np.zeros_like(cos)
    rest = HEAD_DIM - ROPE_DIM
    c64 = np.concatenate([cos, cos, np.ones((SEQ, rest), np.float32)], axis=1)
    s1 = np.concatenate([zeros, sin, np.zeros((SEQ, rest), np.float32)], axis=1)
    s2 = np.concatenate([-sin, zeros, np.zeros((SEQ, rest), np.float32)], axis=1)

    def fold(t, d):
        return t.reshape(SEQ // d, d, t.shape[1]).transpose(1, 0, 2).reshape(SEQ, t.shape[1])

    tabs = [np.stack([np.tile(fold(t, d), (1, 2)) for t in (c64, s1, s2)], axis=0) for d in DILATIONS]
    return jnp.asarray(np.stack(tabs, axis=0), dtype=F32)


def _rope(a, c, s1, s2):
    return a * c + pltpu.roll(a, 8, 1) * s1 + pltpu.roll(a, 120, 1) * s2


def _rope_t(a, c, s1, s2):
    return a * c + pltpu.roll(a * s1, 120, 1) + pltpu.roll(a * s2, 8, 1)


def _perm_of_block(j):
    return jnp.where(j < 3, 0, jnp.where(j < 6, 1, jnp.where(j < 9, 2, 0)))


def _mm(name, a, b, out_shape, out_dtype, grid, a_spec, b_spec, o_spec, acc_shape, dims, k_axis, nk):
    def body(a_ref, b_ref, o_ref, acc_ref):
        k = pl.program_id(k_axis)

        @pl.when(k == 0)
        def _():
            acc_ref[...] = jnp.zeros(acc_shape, F32)

        acc_ref[...] += lax.dot_general(a_ref[...], b_ref[...], dims, preferred_element_type=F32)

        @pl.when(k == nk - 1)
        def _():
            o_ref[...] = acc_ref[...].astype(out_dtype)

    sem = tuple("arbitrary" if ax == k_axis else "parallel" for ax in range(len(grid)))
    return pl.pallas_call(
        body, name=name, grid=grid, in_specs=[a_spec, b_spec], out_specs=o_spec,
        out_shape=jax.ShapeDtypeStruct(out_shape, out_dtype),
        scratch_shapes=[pltpu.VMEM(acc_shape, F32)], compiler_params=_params(sem))(a, b)


def _mm_simple(name, a, b, dims, out_dtype, tm, tn, tk):
    if dims is NN:
        m, kk = a.shape
        n = b.shape[1]
        a_spec = pl.BlockSpec((tm, tk), lambda i, j, k: (i, k))
        b_spec = pl.BlockSpec((tk, tn), lambda i, j, k: (k, j))
    elif dims is NT:
        m, kk = a.shape
        n = b.shape[0]
        a_spec = pl.BlockSpec((tm, tk), lambda i, j, k: (i, k))
        b_spec = pl.BlockSpec((tn, tk), lambda i, j, k: (j, k))
    else:
        kk, m = a.shape
        n = b.shape[1]
        a_spec = pl.BlockSpec((tk, tm), lambda i, j, k: (k, i))
        b_spec = pl.BlockSpec((tk, tn), lambda i, j, k: (k, j))
    grid = (m // tm, n // tn, kk // tk)
    o_spec = pl.BlockSpec((tm, tn), lambda i, j, k: (i, j))
    return _mm(name, a, b, (m, n), out_dtype, grid, a_spec, b_spec, o_spec, (tm, tn), dims, 2, kk // tk)


def _rmsnorm_fwd(name, x, gain, rows):
    n, d = x.shape

    def body(x_ref, g_ref, o_ref):
        xv = x_ref[...]
        rstd = lax.rsqrt(jnp.mean(xv * xv, axis=1, keepdims=True) + EPS)
        o_ref[...] = (xv * rstd * g_ref[...]).astype(BF16)

    return pl.pallas_call(
        body, name=name, grid=(n // rows,),
        in_specs=[pl.BlockSpec((rows, d), lambda i: (i, 0)), pl.BlockSpec((1, d), lambda i: (0, 0))],
        out_specs=pl.BlockSpec((rows, d), lambda i: (i, 0)),
        out_shape=jax.ShapeDtypeStruct((n, d), BF16), compiler_params=_params(("parallel",)))(x, gain)


def _folded_rows(first, rows, d):
    if d == 1:
        return pl.ds(pl.multiple_of(first, rows), rows)
    mlen = SEQ // d
    return pl.ds((first % mlen) * d + first // mlen, rows, stride=d)


def _prenorm_fold(x, gain):
    rows = 128

    nchunk = D_MODEL // 128

    def body(*refs):
        x_refs, g_ref, hs_ref, hst_ref = refs[:nchunk], refs[nchunk], refs[nchunk + 1], refs[nchunk + 2]
        first = pl.program_id(0) * rows
        for p, d in enumerate(DILATIONS):
            idx = _folded_rows(first, rows, d)
            xv = jnp.concatenate([r[idx, :] for r in x_refs], axis=1)
            rstd = lax.rsqrt(jnp.mean(xv * xv, axis=1, keepdims=True) + EPS)
            h = xv * rstd * g_ref[...]
            hs_ref[p] = h.astype(BF16)
            hst_ref[p] = h.T.astype(BF16)

    x_specs = [pl.BlockSpec((SEQ, 128), functools.partial(lambda c, i: (0, c), c)) for c in range(nchunk)]
    return pl.pallas_call(
        body, name="prenorm", grid=(SEQ // rows,),
        in_specs=x_specs + [pl.BlockSpec((1, D_MODEL), lambda i: (0, 0))],
        out_specs=[pl.BlockSpec((3, rows, D_MODEL), lambda i: (0, i, 0)),
                   pl.BlockSpec((3, D_MODEL, rows), lambda i: (0, 0, i))],
        out_shape=[jax.ShapeDtypeStruct((3, SEQ, D_MODEL), BF16), jax.ShapeDtypeStruct((3, D_MODEL, SEQ), BF16)],
        compiler_params=_params(("parallel",)))(*([x] * nchunk), gain)


def _prenorm_bwd(x, gain, dh, dout):
    rows = 256

    def body(x_ref, g_ref, a_ref, do_ref, dx_ref, gg_ref):
        xv = x_ref[...]
        rstd = lax.rsqrt(jnp.mean(xv * xv, axis=1, keepdims=True) + EPS)
        xn = xv * rstd
        dh = jnp.concatenate([a_ref[c] for c in range(D_MODEL // 128)], axis=1)
        gdh = dh * g_ref[...]
        dx_ref[...] = rstd * (gdh - xn * jnp.mean(gdh * xn, axis=1, keepdims=True)) + do_ref[...]

        @pl.when(pl.program_id(0) == 0)
        def _():
            gg_ref[...] = jnp.zeros((1, D_MODEL), F32)

        gg_ref[...] += jnp.sum(dh * xn, axis=0, keepdims=True)

    row = pl.BlockSpec((rows, D_MODEL), lambda i: (i, 0))
    vec = pl.BlockSpec((1, D_MODEL), lambda i: (0, 0))
    return pl.pallas_call(
        body, name="prenorm_bwd", grid=(SEQ // rows,),
        in_specs=[row, vec, pl.BlockSpec((D_MODEL // 128, rows, 128), lambda i: (0, i, 0)), row], out_specs=[row, vec],
        out_shape=[jax.ShapeDtypeStruct((SEQ, D_MODEL), F32), jax.ShapeDtypeStruct((1, D_MODEL), F32)],
        compiler_params=_params(("arbitrary",)))(x, gain, dh, dout)


def _memnorm_bwd(mem, dmemn):
    def body(m_ref, d_ref, gg_ref):
        mv = m_ref[...]
        rstd = lax.rsqrt(jnp.mean(mv * mv, axis=1, keepdims=True) + EPS)
        gg_ref[...] = jnp.sum(d_ref[...] * mv * rstd, axis=0, keepdims=True)

    return pl.pallas_call(
        body, name="memnorm_bwd", out_shape=jax.ShapeDtypeStruct((1, D_MODEL), F32),
        compiler_params=_params())(mem, dmemn)


def _dep_operand(dep):
    return ([], []) if dep is None else ([pl.BlockSpec(memory_space=pl.ANY)], [dep])


def _in_proj(hs, wt, tabs, dep=None):
    tm, tn = 512, 512
    dep_specs, dep_args = _dep_operand(dep)

    def body(h_ref, w_ref, t_ref, *rest):
        o_ref = rest[-1]
        j = pl.program_id(0)
        is_rope = jnp.logical_and(j < 9, j % 3 != 2)
        row_slices = [slice(r * tm, (r + 1) * tm) for r in range(SEQ // tm)]

        def product(rs):
            return lax.dot_general(h_ref[rs, :], w_ref[...], NT, preferred_element_type=F32)

        @pl.when(is_rope)
        def _():
            for rs in row_slices:
                acc = product(rs)
                c, s1, s2 = t_ref[0, rs, :], t_ref[1, rs, :], t_ref[2, rs, :]
                for q in range(tn // 128):
                    a = acc[:, q * 128:(q + 1) * 128]
                    o_ref[rs, q * 128:(q + 1) * 128] = _rope(a, c, s1, s2).astype(BF16)

        @pl.when(jnp.logical_not(is_rope))
        def _():
            for rs in row_slices:
                o_ref[rs, :] = product(rs).astype(BF16)

    return pl.pallas_call(
        body, name="in_proj", grid=(N_IN // tn,),
        in_specs=[pl.BlockSpec((None, SEQ, D_MODEL), lambda j: (_perm_of_block(j), 0, 0)),
                  pl.BlockSpec((tn, D_MODEL), lambda j: (j, 0)),
                  pl.BlockSpec((None, 3, SEQ, 128), lambda j: (_perm_of_block(j), 0, 0, 0))] + dep_specs,
        out_specs=pl.BlockSpec((SEQ, tn), lambda j: (0, j)),
        out_shape=jax.ShapeDtypeStruct((SEQ, N_IN), BF16),
        compiler_params=_params(("parallel",)))(hs, wt, tabs, *dep_args)


def _piece_blocks(pieces):
    return [(a, h * 512) for a, p in enumerate(pieces) for h in range(p.shape[1] // 512)]


def _block_fetch(piece_refs, blocks, buf, sem):
    def start(block, slot):
        for b, (a, col) in enumerate(blocks):
            @pl.when(block == b)
            def _():
                pltpu.make_async_copy(piece_refs[a].at[:, pl.ds(col, 512)], buf.at[slot], sem.at[slot]).start()

    def wait(slot):
        pltpu.make_async_copy(piece_refs[0].at[:, pl.ds(0, 512)], buf.at[slot], sem.at[slot]).wait()

    return start, wait


def _in_proj_dw(pieces, hst, dep=None):
    tn = 512
    blocks = _piece_blocks(pieces)
    nblk = len(blocks)
    npc = len(pieces)
    dep_specs, dep_args = _dep_operand(dep)

    def body(h_ref, *rest):
        piece_refs = rest[:npc]
        o_ref, buf, sem = rest[-3:]
        j = pl.program_id(0)
        slot = j % 2
        start, wait = _block_fetch(piece_refs, blocks, buf, sem)

        @pl.when(j == 0)
        def _():
            start(j, slot)

        wait(slot)

        @pl.when(j + 1 < nblk)
        def _():
            start(j + 1, 1 - slot)

        acc = jnp.dot(h_ref[...], buf[slot], preferred_element_type=F32)
        o_ref[...] = acc.T.astype(BF16)

    return pl.pallas_call(
        body, name="in_proj_dw", grid=(nblk,),
        in_specs=[pl.BlockSpec((None, D_MODEL, SEQ), lambda j: (_perm_of_block(j), 0, 0))] + [ANY] * npc + dep_specs,
        out_specs=pl.BlockSpec((tn, D_MODEL), lambda j: (j, 0)),
        out_shape=jax.ShapeDtypeStruct((N_IN, D_MODEL), BF16),
        scratch_shapes=[pltpu.VMEM((2, SEQ, tn), BF16), pltpu.SemaphoreType.DMA((2,))],
        compiler_params=_params(("arbitrary",)))(hst, *pieces, *dep_args)


def _in_proj_dh(pieces, wt, dep=None):
    tk = 512
    blocks = _piece_blocks(pieces)
    nblk = len(blocks)
    npc = len(pieces)
    nchunk = D_MODEL // 128

    def col(s):
        return jnp.where(s < 3, s, jnp.where(s < 16, s + 6, s - 13))

    dep_specs, dep_args = _dep_operand(dep)

    def body(w_ref, *rest):
        piece_refs = rest[:npc]
        o_ref, acc_ref, buf, sem = rest[-4:]
        s = pl.program_id(0)
        slot = s % 2
        start, wait = _block_fetch(piece_refs, blocks, buf, sem)

        @pl.when(s == 0)
        def _():
            start(col(s), slot)

        wait(slot)

        @pl.when(s + 1 < nblk)
        def _():
            start(col(s + 1), 1 - slot)

        row_slices = [slice(r * 512, (r + 1) * 512) for r in range(SEQ // 512)]

        def product(rs):
            return jnp.dot(buf[slot, rs, :], w_ref[...], preferred_element_type=F32)

        def accumulate(cond, to_out, init):
            @pl.when(cond)
            def _():
                for rs in row_slices:
                    prod = product(rs)
                    if not to_out:
                        if init:
                            acc_ref[rs, :] = prod
                        else:
                            acc_ref[rs, :] += prod
                        continue
                    for c in range(nchunk):
                        if init:
                            o_ref[c, rs, :] = prod[:, c * 128:(c + 1) * 128]
                        else:
                            o_ref[c, rs, :] += prod[:, c * 128:(c + 1) * 128]

        accumulate(s == 0, True, True)
        accumulate(jnp.logical_and(s > 0, s < 16), True, False)
        accumulate(jnp.logical_or(s == 16, s == 19), False, True)
        accumulate(jnp.logical_and(s > 16, s != 19), False, False)
        for last, d in ((18, 4), (21, 16)):
            @pl.when(s == last)
            def _():
                mlen = SEQ // d
                for r in range(d):
                    for c in range(nchunk):
                        o_ref[c, pl.ds(r, mlen, stride=d), :] += acc_ref[r * mlen:(r + 1) * mlen,
                                                                         c * 128:(c + 1) * 128]

    return pl.pallas_call(
        body, name="in_proj_dh", grid=(nblk,),
        in_specs=[pl.BlockSpec((tk, D_MODEL), lambda s: (col(s), 0))] + [ANY] * npc + dep_specs,
        out_specs=pl.BlockSpec((nchunk, SEQ, 128), lambda s: (0, 0, 0)),
        out_shape=jax.ShapeDtypeStruct((nchunk, SEQ, 128), F32),
        scratch_shapes=[pltpu.VMEM((SEQ, D_MODEL), F32), pltpu.VMEM((2, SEQ, tk), BF16),
                        pltpu.SemaphoreType.DMA((2,))],
        compiler_params=_params(("arbitrary",)))(wt, *pieces, *dep_args)


def _head_lanes(lanes, hh):
    return lanes >= 64 if hh == 1 else lanes < 64


def _head_rows(x, lanes, hh, pair):
    if not pair:
        return jnp.max(x, axis=1, keepdims=True)
    return jnp.max(jnp.where(_head_lanes(lanes, hh), x, -jnp.inf), axis=1, keepdims=True)


def _mask_head(x, lanes, hh, pair, scale=1.0):
    if not pair:
        return x
    xf = x.astype(F32) if scale == 1.0 else x.astype(F32) * scale
    return jnp.where(_head_lanes(lanes, hh), xf, 0.0).astype(BF16)


def _merge_heads(parts, lanes, pair):
    if not pair:
        return parts[0]
    return jnp.where(lanes < 64, parts[0], parts[1])


def _window(mode, qi, tq, mlen, tk):
    if mode == "dil":
        q0 = qi * tq
        seg = (q0 // mlen) * mlen
        ks = jnp.clip(q0 - REACH, seg, seg + mlen - tk)
        return pl.multiple_of(ks, 64)
    if mode == "na":
        r_start = jnp.clip(qi - NA_ROWS // 2, 0, SEQ // GRID_W - NA_ROWS)
        return pl.multiple_of(r_start * GRID_W, 64)
    return 0


def _band_mask(qi, tq, tk, ks):
    qpos = qi * tq + _iota((tq, tk), 0)
    kpos = ks + _iota((tq, tk), 1)
    return jnp.where(jnp.abs(qpos - kpos) <= REACH, 0.0, NEG).astype(F32)


def _scores(mode, qh, k, sscale, band, qi, bias_ref, hh):
    s = lax.dot_general(qh, k, NT, preferred_element_type=F32)
    if sscale != 1.0:
        s = s * sscale
    if mode == "dil":
        s = s + band
    elif mode == "na":
        off = qi - jnp.clip(qi - NA_ROWS // 2, 0, SEQ // GRID_W - NA_ROWS)
        s = s + bias_ref[hh, off]
    return s


def _attn_cfg(mode, d):
    if mode == "dil":
        mlen = SEQ // d
        return dict(pair=True, tq=128, tk=min(256, mlen), mlen=mlen, lk=SEQ, scale=HEAD_DIM ** -0.5, units=4,
                    nsub=ATTN_SUBTILES)
    if mode == "na":
        return dict(pair=True, tq=GRID_W, tk=NA_ROWS * GRID_W, mlen=SEQ, lk=SEQ, scale=HEAD_DIM ** -0.5, units=4,
                    nsub=ATTN_SUBTILES)
    return dict(pair=False, tq=128, tk=MEM_LEN, mlen=SEQ, lk=MEM_LEN, scale=128 ** -0.5, units=4,
                nsub=ATTN_SUBTILES)


ATTN_SUBTILES = 4


def _attn_fwd(name, mode, q_arr, k_arr, v_arr, qcol, kcol, vcol, d=1, bias=None):
    cfg = _attn_cfg(mode, d)
    pair, tq, tk, mlen, lk, scale = cfg["pair"], cfg["tq"], cfg["tk"], cfg["mlen"], cfg["lk"], cfg["scale"]
    qscale, sscale = (scale, 1.0) if pair else (1.0, scale)
    nh = 2 if pair else 1
    nsub = cfg["nsub"]
    rows = nsub * tq

    def body(*refs):
        if mode == "na":
            q_ref, k_ref, v_ref, bias_ref, o_ref, l_ref = refs
        else:
            q_ref, k_ref, v_ref, o_ref, l_ref = refs
            bias_ref = None
        lanes = _iota((tq, 128), 1)
        chains = [(sub, hh) for sub in range(nsub) for hh in range(nh)]
        qis = [pl.program_id(1) * nsub + sub for sub in range(nsub)]
        kss = [_window(mode, qi, tq, mlen, tk) for qi in qis]
        vs = [v_ref[pl.ds(ks, tk), :] for ks in kss]
        bands = [_band_mask(qi, tq, tk, ks) if mode == "dil" else None for qi, ks in zip(qis, kss)]
        ss = []
        for sub, hh in chains:
            q = q_ref[sub * tq:(sub + 1) * tq, :]
            k = k_ref[pl.ds(kss[sub], tk), :]
            ss.append(_scores(mode, _mask_head(q, lanes, hh, pair, qscale), k, sscale, bands[sub], qis[sub], bias_ref,
                              hh))
        ms = [jnp.max(s, axis=1, keepdims=True) for s in ss]
        ps = [jnp.exp(s - m) for s, m in zip(ss, ms)]
        ls = [jnp.sum(p, axis=1, keepdims=True) for p in ps]
        os_ = [jnp.dot(p.astype(BF16), vs[sub], preferred_element_type=F32) for p, (sub, hh) in zip(ps, chains)]
        for sub in range(nsub):
            sel = [i for i, (s_, hh) in enumerate(chains) if s_ == sub]
            outs = [os_[i] / ls[i] for i in sel]
            lses = [jnp.broadcast_to(ms[i] + jnp.log(ls[i]), (tq, 128)) for i in sel]
            dst = _folded_rows(qis[sub] * tq, tq, d) if mode == "dil" else slice(sub * tq, (sub + 1) * tq)
            o_ref[dst, :] = _merge_heads(outs, lanes, pair)
            l_ref[dst, :] = _merge_heads(lses, lanes, pair)

    in_specs = [pl.BlockSpec((rows, 128), lambda u, i: (i, qcol + u)),
                pl.BlockSpec((lk, 128), lambda u, i: (0, kcol + u)),
                pl.BlockSpec((lk, 128), lambda u, i: (0, vcol + u))]
    args = [q_arr, k_arr, v_arr]
    if mode == "na":
        in_specs.append(pl.BlockSpec((2, NA_ROWS, GRID_W, NA_ROWS * GRID_W), lambda u, i: (u, 0, 0, 0)))
        args.append(bias)
    if mode == "dil":
        out_spec = pl.BlockSpec((SEQ, 128), lambda u, i: (0, u))
    else:
        out_spec = pl.BlockSpec((rows, 128), lambda u, i: (i, u))
    return pl.pallas_call(
        body, name=name, grid=(cfg["units"], SEQ // rows), in_specs=in_specs, out_specs=[out_spec, out_spec],
        out_shape=[jax.ShapeDtypeStruct((SEQ, 512), F32), jax.ShapeDtypeStruct((SEQ, 512), F32)],
        compiler_params=_params(("parallel", "arbitrary")))(*args)


def _attn_bwd(name, mode, q_arr, k_arr, v_arr, qcol, kcol, vcol, do, lse, dp=None, o=None, d=1, bias=None,
              tabs=None):
    cfg = _attn_cfg(mode, d)
    pair, tq, tk, mlen, lk, scale = cfg["pair"], cfg["tq"], cfg["tk"], cfg["mlen"], cfg["lk"], cfg["scale"]
    qscale, sscale = (scale, 1.0) if pair else (1.0, scale)
    nh = 2 if pair else 1
    nsub = cfg["nsub"]
    rows = nsub * tq
    nq = SEQ // rows
    kv_dtype = F32 if mode == "mem" else BF16

    def body(*refs):
        refs = list(refs)
        q_ref, k_ref, v_ref, do_ref, l_ref = refs[:5]
        rest = refs[5:]
        bias_ref = tq_ref = tk_ref = db_ref = None
        if mode == "dil":
            dp_ref, tq_ref, tk_ref, dq_ref, dk_ref, dv_ref, dk_acc, dv_acc = rest
        elif mode == "na":
            o_ref, bias_ref, dq_ref, dk_ref, dv_ref, db_ref, dk_acc, dv_acc = rest
        else:
            o_ref, dq_ref, dk_ref, dv_ref, dk_acc, dv_acc = rest
        step = pl.program_id(1)

        @pl.when(step == 0)
        def _():
            dk_acc[...] = jnp.zeros((lk, 128), F32)
            dv_acc[...] = jnp.zeros((lk, 128), F32)
            if mode == "na":
                db_ref[...] = jnp.zeros(db_ref.shape, F32)

        lanes = _iota((tq, 128), 1)
        lanes_k = _iota((tk, 128), 1)
        chains = [(sub, hh) for sub in range(nsub) for hh in range(nh)]
        qis = [step * nsub + sub for sub in range(nsub)]
        sls = [slice(sub * tq, (sub + 1) * tq) for sub in range(nsub)]
        kss = [_window(mode, qi, tq, mlen, tk) for qi in qis]
        qs = [q_ref[sl, :] for sl in sls]
        ks_ = [k_ref[pl.ds(ks, tk), :] for ks in kss]
        vs = [v_ref[pl.ds(ks, tk), :] for ks in kss]
        dovs, lsevs, dpvs = [], [], []
        for sub in range(nsub):
            if mode == "dil":
                src = _folded_rows(qis[sub] * tq, tq, d)
                dovs.append(do_ref[src, :].astype(BF16))
                lsevs.append(l_ref[src, :])
                dpvs.append(dp_ref[src, :])
            else:
                dovs.append(do_ref[sls[sub], :])
                lsevs.append(l_ref[sls[sub], :])
                dpvs.append(dovs[sub].astype(F32) * o_ref[sls[sub], :])
        bands = [_band_mask(qi, tq, tk, ks) if mode == "dil" else None for qi, ks in zip(qis, kss)]
        ss = [_scores(mode, _mask_head(qs[sub], lanes, hh, pair, qscale), ks_[sub], sscale, bands[sub], qis[sub],
                      bias_ref, hh) for sub, hh in chains]
        dpms = [lax.dot_general(_mask_head(dovs[sub], lanes, hh, pair), vs[sub], NT, preferred_element_type=F32)
                for sub, hh in chains]
        ps = [jnp.exp(s - _head_rows(lsevs[sub], lanes, hh, pair)) for s, (sub, hh) in zip(ss, chains)]
        dphs = []
        for sub, hh in chains:
            if mode == "dil":
                dphs.append(_head_rows(dpvs[sub], lanes, hh, pair))
            elif pair:
                dphs.append(jnp.sum(jnp.where(_head_lanes(lanes, hh), dpvs[sub], 0.0), axis=1, keepdims=True))
            else:
                dphs.append(jnp.sum(dpvs[sub], axis=1, keepdims=True))
        dss = [p * (dpm - dph) for p, dpm, dph in zip(ps, dpms, dphs)]
        if mode == "na":
            for ds, (sub, hh) in zip(dss, chains):
                off = qis[sub] - jnp.clip(qis[sub] - NA_ROWS // 2, 0, SEQ // GRID_W - NA_ROWS)
                db_ref[hh, off] += ds
        dsbs = [ds.astype(BF16) for ds in dss]
        dvs = [lax.dot_general(p.astype(BF16), dovs[sub], TN, preferred_element_type=F32)
               for p, (sub, hh) in zip(ps, chains)]
        dqs = [jnp.dot(dsb, ks_[sub], preferred_element_type=F32) * scale for dsb, (sub, hh) in zip(dsbs, chains)]
        dks = [lax.dot_general(dsb, qs[sub], TN, preferred_element_type=F32) * scale
               for dsb, (sub, hh) in zip(dsbs, chains)]
        for sub in range(nsub):
            sel = [i for i, (s_, hh) in enumerate(chains) if s_ == sub]
            sl = sls[sub]
            dq = _merge_heads([dqs[i] for i in sel], lanes, pair)
            if mode == "dil":
                dq = _rope_t(dq, tq_ref[0, sl, :], tq_ref[1, sl, :], tq_ref[2, sl, :])
            dq_ref[sl, :] = dq.astype(BF16)
            dk_acc[pl.ds(kss[sub], tk), :] += _merge_heads([dks[i] for i in sel], lanes_k, pair)
            dv_acc[pl.ds(kss[sub], tk), :] += _merge_heads([dvs[i] for i in sel], lanes_k, pair)

        @pl.when(step == nq - 1)
        def _():
            dkv = dk_acc[...]
            if mode == "dil":
                dkv = _rope_t(dkv, tk_ref[0], tk_ref[1], tk_ref[2])
            dk_ref[...] = dkv.astype(kv_dtype)
            dv_ref[...] = dv_acc[...].astype(kv_dtype)

    q_spec = pl.BlockSpec((rows, 128), lambda u, i: (i, qcol + u))
    row_spec = pl.BlockSpec((rows, 128), lambda u, i: (i, u))
    kv_out = pl.BlockSpec((lk, 128), lambda u, i: (0, u))
    whole = pl.BlockSpec((SEQ, 128), lambda u, i: (0, u))
    nat_spec = whole if mode == "dil" else row_spec
    in_specs = [q_spec,
                pl.BlockSpec((lk, 128), lambda u, i: (0, kcol + u)),
                pl.BlockSpec((lk, 128), lambda u, i: (0, vcol + u)),
                nat_spec, nat_spec]
    args = [q_arr, k_arr, v_arr, do, lse]
    out_specs = [row_spec, kv_out, kv_out]
    out_shape = [jax.ShapeDtypeStruct((SEQ, 512), BF16), jax.ShapeDtypeStruct((lk, 512), kv_dtype),
                 jax.ShapeDtypeStruct((lk, 512), kv_dtype)]
    if mode == "dil":
        in_specs += [whole, pl.BlockSpec((3, rows, 128), lambda u, i: (0, i, 0)),
                     pl.BlockSpec((3, SEQ, 128), lambda u, i: (0, 0, 0))]
        args += [dp, tabs, tabs]
    elif mode == "na":
        b_spec = pl.BlockSpec((2, NA_ROWS, GRID_W, NA_ROWS * GRID_W), lambda u, i: (u, 0, 0, 0))
        in_specs += [row_spec, b_spec]
        args += [o, bias]
        out_specs.append(b_spec)
        out_shape.append(jax.ShapeDtypeStruct((8, NA_ROWS, GRID_W, NA_ROWS * GRID_W), F32))
    else:
        in_specs.append(row_spec)
        args.append(o)
    return pl.pallas_call(
        body, name=name, grid=(cfg["units"], nq), in_specs=in_specs, out_specs=out_specs, out_shape=out_shape,
        scratch_shapes=[pltpu.VMEM((lk, 128), F32), pltpu.VMEM((lk, 128), F32)],
        compiler_params=_params(("parallel", "arbitrary")))(*args)


def _na_geometry():
    qc = _iota((GRID_W, 128), 0)
    lane = _iota((GRID_W, 128), 1)
    kc = lane & 63
    c_start = jnp.clip(qc - 8, 0, GRID_W - 16)
    valid = jnp.logical_and(kc >= c_start, kc < c_start + 16)
    return lane, valid


def _na_bias(rpb_rows):
    def body(r_ref, o_ref, t_ref):
        lane, valid = _na_geometry()
        for dd in range(14):
            row_a = jnp.broadcast_to(r_ref[dd:dd + 1, :], (GRID_W, 128))
            row_b = jnp.broadcast_to(r_ref[dd + 1:dd + 2, :], (GRID_W, 128))
            both = jnp.where(lane < 64, row_a, pltpu.roll(row_b, 64, 1))
            t = pltpu.roll(both, 128 - 15, 1, stride=1, stride_axis=0)
            t_ref[dd] = jnp.where(valid, t, NEG)
        for off in range(NA_ROWS):
            for p in range(4):
                o_ref[off, :, p * 128:(p + 1) * 128] = t_ref[2 * p - off + 7]

    return pl.pallas_call(
        body, name="na_bias", grid=(8,),
        in_specs=[pl.BlockSpec((None, 16, 128), lambda h: (h, 0, 0))],
        out_specs=pl.BlockSpec((None, NA_ROWS, GRID_W, NA_ROWS * GRID_W), lambda h: (h, 0, 0, 0)),
        out_shape=jax.ShapeDtypeStruct((8, NA_ROWS, GRID_W, NA_ROWS * GRID_W), F32),
        scratch_shapes=[pltpu.VMEM((14, GRID_W, 128), F32)],
        compiler_params=_params(("parallel",)))(rpb_rows)


def _na_bias_bwd(dbias):
    def body(d_ref, o_ref):
        lane, valid = _na_geometry()
        reverse = (_iota((GRID_W, GRID_W), 0) + _iota((GRID_W, GRID_W), 1) == GRID_W - 1).astype(F32)
        o_ref[...] = jnp.zeros((16, 128), F32)
        for dd in range(14):
            t = jnp.zeros((GRID_W, 128), F32)
            for off in range(NA_ROWS):
                for p in range(4):
                    if 2 * p - off + 7 == dd:
                        t = t + d_ref[off, :, p * 128:(p + 1) * 128]
            t = jnp.dot(reverse, jnp.where(valid, t, 0.0), precision=lax.Precision.HIGHEST,
                        preferred_element_type=F32)
            t = pltpu.roll(t, 128 - (GRID_W - 16), 1, stride=1, stride_axis=0)
            o_ref[dd:dd + 1, :] = jnp.sum(t, axis=0, keepdims=True)

    return pl.pallas_call(
        body, name="na_bias_bwd", grid=(8,),
        in_specs=[pl.BlockSpec((None, NA_ROWS, GRID_W, NA_ROWS * GRID_W), lambda h: (h, 0, 0, 0))],
        out_specs=pl.BlockSpec((None, 16, 128), lambda h: (h, 0, 0)),
        out_shape=jax.ShapeDtypeStruct((8, 16, 128), F32),
        compiler_params=_params(("parallel",)))(dbias)


GATE_ROWS = 128


def _group_weights(l0, l1, l2):
    m = jnp.maximum(jnp.maximum(l0, l1), l2)
    e0, e1, e2 = jnp.exp(l0 - m), jnp.exp(l1 - m), jnp.exp(l2 - m)
    inv = 1.0 / (e0 + e1 + e2)
    return e0 * inv, e1 * inv, e2 * inv


def _gate_specs():
    r512 = pl.BlockSpec((GATE_ROWS, 512), lambda i: (i, 0))
    r1024 = pl.BlockSpec((GATE_ROWS, D_MODEL), lambda i: (i, 0))
    silu_cols = [pl.BlockSpec((GATE_ROWS, 512), functools.partial(lambda b, i: (i, b), 13 + b)) for b in range(3)]
    logit_cols = [pl.BlockSpec((GATE_ROWS, D_MODEL), functools.partial(lambda b, i: (i, b), 8 + b)) for b in range(3)]
    return r512, r1024, silu_cols, logit_cols


def _gate_fwd(o_grp, l_grp, out_b, out_c, parts, merge_bias, wts):
    r512, r1024, silu_cols, logit_cols = _gate_specs()

    def body(o0, o1, o2, l0, l1, l2, ob, oc, ga, gb, gc, la, lb, lc, mb, wa, wb, wc,
             oa_ref, ua, ub, uc, za, zb, zc, y_ref):
        w0, w1, w2 = _group_weights(l0[...], l1[...], l2[...])
        out_a = w0 * o0[...] + w1 * o1[...] + w2 * o2[...]
        oa_ref[...] = out_a
        y = jnp.zeros((GATE_ROWS, D_MODEL), F32)
        for b, (ov, g_ref, l_ref, w_ref, u_ref, z_ref) in enumerate(
                ((out_a, ga, la, wa, ua, za), (ob[...], gb, lb, wb, ub, zb), (oc[...], gc, lc, wc, uc, zc))):
            g = g_ref[...].astype(F32)
            u = (ov * (g * _sigmoid(g))).astype(BF16)
            u_ref[...] = u
            z = lax.dot_general(u, w_ref[...], NT, preferred_element_type=F32)
            z_ref[...] = z.astype(BF16)
            gate = _sigmoid(l_ref[...].astype(F32) + mb[b:b + 1, :])
            y = y + gate * z
        y_ref[...] = y.astype(BF16)

    full = lambda shape: pl.BlockSpec(shape, lambda i: (0,) * len(shape))
    in_specs = ([r512] * 8 + silu_cols + logit_cols
                + [full((3, D_MODEL))] + [full((D_MODEL, 512))] * 3)
    out_specs = [r512] * 4 + [r1024] * 4
    out_shape = ([jax.ShapeDtypeStruct((SEQ, 512), F32)] + [jax.ShapeDtypeStruct((SEQ, 512), BF16)] * 3
                 + [jax.ShapeDtypeStruct((SEQ, D_MODEL), BF16)] * 4)
    res = pl.pallas_call(
        body, name="gate_fwd", grid=(SEQ // GATE_ROWS,), in_specs=in_specs, out_specs=out_specs,
        out_shape=out_shape, compiler_params=_params(("parallel",)))(
            *o_grp, *l_grp, out_b, out_c, parts, parts, parts, parts, parts, parts, merge_bias, *wts)
    return res[0], res[1:4], res[4:7], res[7]


def _gate_bwd(dy, z, parts, merge_bias, outs, o_grp, l_grp, wts, head_sum):
    r512, r1024, silu_cols, logit_cols = _gate_specs()

    def body(dy_ref, za, zb, zc, la, lb, lc, mb, oa, ob, oc, ga, gb, gc, o0, o1, o2, l0, l1, l2, wa, wb, wc, hs_ref,
             dla, dlb, dlc, gmb, dza, dzb, dzc, dga, dgb, dgc, do0, do1, do2, dp0, dp1, dp2, dob, doc):
        dyv = dy_ref[...].astype(F32)
        rows = []
        dos = []
        for b, (z_ref, l_ref, ov_ref, g_ref, w_ref, dl_ref, dz_ref, dg_ref) in enumerate(
                ((za, la, oa, ga, wa, dla, dza, dga), (zb, lb, ob, gb, wb, dlb, dzb, dgb),
                 (zc, lc, oc, gc, wc, dlc, dzc, dgc))):
            gate = _sigmoid(l_ref[...].astype(F32) + mb[b:b + 1, :])
            dl = dyv * z_ref[...].astype(F32) * gate * (1.0 - gate)
            dl_ref[...] = dl.astype(BF16)
            rows.append(jnp.sum(dl, axis=0, keepdims=True))
            dz = (dyv * gate).astype(BF16)
            dz_ref[...] = dz
            du = jnp.dot(dz, w_ref[...], preferred_element_type=F32)
            g = g_ref[...].astype(F32)
            sg = _sigmoid(g)
            dos.append(du * (g * sg))
            dg_ref[...] = (du * ov_ref[...] * (sg * (1.0 + g * (1.0 - sg)))).astype(BF16)

        @pl.when(pl.program_id(0) == 0)
        def _():
            gmb[...] = jnp.zeros((3, D_MODEL), F32)

        for b in range(3):
            gmb[b:b + 1, :] += rows[b]
        dob[...] = dos[1].astype(BF16)
        doc[...] = dos[2].astype(BF16)
        doa = dos[0]
        row_term = jnp.dot(doa * oa[...], hs_ref[...], precision=lax.Precision.HIGHEST, preferred_element_type=F32)
        ws = _group_weights(l0[...], l1[...], l2[...])
        for wg, do_ref, dp_ref in zip(ws, (do0, do1, do2), (dp0, dp1, dp2)):
            do_ref[...] = wg * doa
            dp_ref[...] = wg * row_term

    full = lambda shape: pl.BlockSpec(shape, lambda i: (0,) * len(shape))
    acc = pl.BlockSpec((3, D_MODEL), lambda i: (0, 0))
    in_specs = ([r1024] * 4 + logit_cols + [full((3, D_MODEL))] + [r512] * 3 + silu_cols + [r512] * 6
                + [full((D_MODEL, 512))] * 3 + [full((512, 512))])
    out_specs = [r1024] * 3 + [acc] + [r1024] * 3 + [r512] * 11
    out_shape = ([jax.ShapeDtypeStruct((SEQ, D_MODEL), BF16)] * 3 + [jax.ShapeDtypeStruct((3, D_MODEL), F32)]
                 + [jax.ShapeDtypeStruct((SEQ, D_MODEL), BF16)] * 3 + [jax.ShapeDtypeStruct((SEQ, 512), BF16)] * 3
                 + [jax.ShapeDtypeStruct((SEQ, 512), F32)] * 6 + [jax.ShapeDtypeStruct((SEQ, 512), BF16)] * 2)
    res = pl.pallas_call(
        body, name="gate_bwd", grid=(SEQ // GATE_ROWS,), in_specs=in_specs, out_specs=out_specs,
        out_shape=out_shape, compiler_params=_params(("arbitrary",)))(
            dy, *z, parts, parts, parts, merge_bias, *outs, parts, parts, parts, *o_grp, *l_grp, *wts, head_sum)
    return res[0:3], res[3], res[4:7], res[7:10], res[10:13], res[13:16], res[16], res[17]


def _post(y2, x, target, gain):
    rows = 256

    def body(y_ref, x_ref, t_ref, g_ref, do_ref, dy_ref, l_ref, gg_ref):
        yv = y_ref[...]
        rstd = lax.rsqrt(jnp.mean(yv * yv, axis=1, keepdims=True) + EPS)
        yn = yv * rstd
        gv = g_ref[...]
        err = x_ref[...] + yn * gv - t_ref[...]
        dout = err * (1.0 / D_MODEL)
        do_ref[...] = dout
        dn = dout * gv
        dy_ref[...] = (rstd * (dn - yn * jnp.mean(dn * yn, axis=1, keepdims=True))).astype(BF16)

        @pl.when(pl.program_id(0) == 0)
        def _():
            l_ref[...] = jnp.zeros((1, D_MODEL), F32)
            gg_ref[...] = jnp.zeros((1, D_MODEL), F32)

        l_ref[...] += jnp.sum(err * err, axis=0, keepdims=True)
        gg_ref[...] += jnp.sum(dout * yn, axis=0, keepdims=True)

    row = pl.BlockSpec((rows, D_MODEL), lambda i: (i, 0))
    vec = pl.BlockSpec((1, D_MODEL), lambda i: (0, 0))
    return pl.pallas_call(
        body, name="post", grid=(SEQ // rows,), in_specs=[row, row, row, vec], out_specs=[row, row, vec, vec],
        out_shape=[jax.ShapeDtypeStruct((SEQ, D_MODEL), F32), jax.ShapeDtypeStruct((SEQ, D_MODEL), BF16),
                   jax.ShapeDtypeStruct((1, D_MODEL), F32), jax.ShapeDtypeStruct((1, D_MODEL), F32)],
        compiler_params=_params(("arbitrary",)))(y2, x, target, gain)


def _local_step(x, mem, target, pre_norm, mem_norm, post_norm, na_rpb, wt_in, late_weights, dep_in=None,
                reduce_start=None):
    tabs = _rope_tables()
    hs, hst = _prenorm_fold(x, pre_norm)
    parts = _in_proj(hs, wt_in, tabs, dep_in)

    o_grp, l_grp = [], []
    for g, d in enumerate(DILATIONS):
        o, l = _attn_fwd("dil_fwd_%d" % g, "dil", parts, parts, parts, 12 * g, 12 * g + 4, 12 * g + 8, d=d)
        o_grp.append(o)
        l_grp.append(l)
    bias = _na_bias(jnp.pad(na_rpb, ((0, 0), (0, 1), (0, 128 - 31))))
    out_b, lse_b = _attn_fwd("na_fwd", "na", parts, parts, parts, 36, 40, 44, bias=bias)
    merge_bias, w_kv, wt_a, wt_b, wt_c, w_out = late_weights(out_b)
    memn = _rmsnorm_fwd("memnorm", mem, mem_norm, MEM_LEN)
    kv_m = _mm_simple("mem_kv", memn, w_kv, NN, BF16, MEM_LEN, 512, D_MODEL)
    out_c, lse_c = _attn_fwd("mem_fwd", "mem", parts, kv_m, kv_m, 48, 0, 4)

    wts = (wt_a, wt_b, wt_c)
    out_a, u, z, y = _gate_fwd(o_grp, l_grp, out_b, out_c, parts, merge_bias, wts)
    y2 = _mm_simple("out_proj", y, w_out, NN, F32, 512, D_MODEL, D_MODEL)
    dout, dy2, err_sq, g_post = _post(y2, x, target, post_norm)
    loss = 0.5 * jnp.sum(err_sq) / D_MODEL

    dy = _mm_simple("out_proj_dx", dy2, w_out, NT, BF16, 512, D_MODEL, D_MODEL)
    g_w_out = _mm_simple("out_proj_dw", y, dy2, TN, BF16, D_MODEL, 512, 512)

    rr = _iota((512, 512), 0) // HEAD_DIM
    cc = _iota((512, 512), 1) // HEAD_DIM
    head_sum = (rr == cc).astype(F32)
    dlog, g_mb, dz, dg, do_grp, dp_grp, do_b, do_c = _gate_bwd(
        dy, z, parts, merge_bias, (out_a, out_b, out_c), o_grp, l_grp, wts, head_sum)
    g_wt = [_mm_simple("branch_dw_%d" % b, dz[b], u[b], TN, BF16, D_MODEL, 512, 512) for b in range(3)]

    dqkv = []
    for g, d in enumerate(DILATIONS):
        dq, dk, dv = _attn_bwd("dil_bwd_%d" % g, "dil", parts, parts, parts, 12 * g, 12 * g + 4, 12 * g + 8,
                               do_grp[g], l_grp[g], dp=dp_grp[g], d=d, tabs=tabs[g])
        dqkv += [dq, dk, dv]
    dq_b, dk_b, dv_b, dbias = _attn_bwd("na_bwd", "na", parts, parts, parts, 36, 40, 44, do_b, lse_b, o=out_b,
                                        bias=bias)
    g_rpb_t = _na_bias_bwd(dbias)
    g_rpb = g_rpb_t[:, :15, :31] + jnp.pad(g_rpb_t[:, :14, 64:95], ((0, 0), (1, 0), (0, 0)))
    dq_c, dk_m, dv_m = _attn_bwd("mem_bwd", "mem", parts, kv_m, kv_m, 48, 0, 4, do_c, lse_c, o=out_c)

    dkv = jnp.concatenate([dk_m, dv_m], axis=1).astype(BF16)
    g_w_kv = _mm_simple("mem_kv_dw", memn, dkv, TN, BF16, D_MODEL, 512, MEM_LEN)
    dmemn = _mm_simple("mem_kv_dx", dkv, w_kv, NT, F32, MEM_LEN, 512, D_MODEL)
    g_mem_norm = _memnorm_bwd(mem, dmemn)

    grads = dict(w_kv=g_w_kv, wt_a=g_wt[0], wt_b=g_wt[1], wt_c=g_wt[2], w_out=g_w_out, merge_bias=g_mb,
                 mem_norm=g_mem_norm, post_norm=g_post, na_rpb=g_rpb)
    dep = reduce_start(grads) if reduce_start is not None else None
    dparts = dqkv + [dq_b, dk_b, dv_b, dq_c] + list(dg) + list(dlog)
    grads["wt_in"] = _in_proj_dw(dparts, hst, dep)
    dep = reduce_start(grads) if reduce_start is not None else None
    dh = _in_proj_dh(dparts, wt_in, dep)
    grad_x, grads["pre_norm"] = _prenorm_bwd(x, pre_norm, dh, dout)
    return loss, grad_x, grads


ANY = pl.BlockSpec(memory_space=pl.ANY)


def _place():
    return lax.axis_index("x"), lax.axis_index("y"), lax.axis_index("c")


def _all_gather(shard):
    r = shard.shape[0]
    half = r // 2

    def body(src, out, send_sems, recv_sems, local_sem):
        x, y, c = _place()
        me, sib = (x, y, c), (x, y, 1 - c)
        xn, yn, dg = (1 - x, y, c), (x, 1 - y, c), (1 - x, 1 - y, c)

        def rows(dev, part=None):
            blk = out.at[4 * dev[0] + 2 * dev[1] + dev[2]]
            return blk if part is None else blk.at[pl.ds(part * half, half)]

        def copy(k, dev, part, to, own=False):
            return pltpu.make_async_remote_copy(
                src_ref=src if own else rows(dev, part), dst_ref=rows(dev, part),
                send_sem=send_sems.at[k], recv_sem=recv_sems.at[k], device_id=to, device_id_type=MESH_ID)

        def other(dev):
            return (dev[0], dev[1], 1 - dev[2])

        mine = pltpu.make_async_copy(src, rows(me), local_sem)
        mine.start()
        sent = [copy(0, me, None, sib, own=True), copy(1, me, None, xn, own=True), copy(2, me, None, yn, own=True)]
        for cp in sent:
            cp.start()
        copy(1, xn, None, me).wait_recv()
        sent += [copy(3, xn, 0, yn), copy(5, xn, None, sib)]
        sent[-2].start()
        sent[-1].start()
        copy(2, yn, None, me).wait_recv()
        sent += [copy(4, yn, 1, xn), copy(6, yn, None, sib)]
        sent[-2].start()
        sent[-1].start()
        copy(3, dg, 0, me).wait_recv()
        sent.append(copy(7, dg, 0, sib))
        sent[-1].start()
        copy(4, dg, 1, me).wait_recv()
        sent.append(copy(8, dg, 1, sib))
        sent[-1].start()
        copy(0, sib, None, me).wait_recv()
        copy(5, other(xn), None, me).wait_recv()
        copy(6, other(yn), None, me).wait_recv()
        copy(7, other(dg), 0, me).wait_recv()
        copy(8, other(dg), 1, me).wait_recv()
        for cp in sent:
            cp.wait_send()
        mine.wait()

    return pl.pallas_call(
        body, name="all_gather", in_specs=[ANY], out_specs=ANY,
        out_shape=jax.ShapeDtypeStruct((N_DEV,) + shard.shape, shard.dtype),
        scratch_shapes=[pltpu.SemaphoreType.DMA((9,)), pltpu.SemaphoreType.DMA((9,)), pltpu.SemaphoreType.DMA])(shard)


def _exchange_sibling(name, terms):
    nt = len(terms)

    def body(*refs):
        srcs, outs = refs[:nt], refs[nt:2 * nt]
        send_sems, recv_sems = refs[2 * nt:]
        x, y, c = _place()
        copies = []
        for q in range(4):
            for t in range(nt):
                copies.append(pltpu.make_async_remote_copy(
                    src_ref=srcs[t].at[2 * q + 1 - c], dst_ref=outs[t].at[q],
                    send_sem=send_sems.at[q * nt + t], recv_sem=recv_sems.at[q * nt + t],
                    device_id=(x, y, 1 - c), device_id_type=MESH_ID))
        for cp in copies:
            cp.start()
        for cp in copies:
            cp.wait()

    return pl.pallas_call(
        body, name=name, in_specs=[ANY] * nt, out_specs=[ANY] * nt,
        out_shape=[jax.ShapeDtypeStruct((4,) + s.shape[1:], s.dtype) for s in terms],
        scratch_shapes=[pltpu.SemaphoreType.DMA((4 * nt,)), pltpu.SemaphoreType.DMA((4 * nt,))])(*terms)


HBM = pl.BlockSpec(memory_space=pltpu.HBM)
SEM = pl.BlockSpec(memory_space=pltpu.SEMAPHORE)
DATAFLOW = pltpu.SideEffectType.DATAFLOW_SIDE_EFFECTING


def _split_copies(kind, srcs, lands, send_sems, recv_sems):
    nt = len(srcs)
    x, y, c = _place()
    copies = []
    if kind == "gather":
        me = 4 * x + 2 * y + c
        for mask in range(1, 8):
            fx, fy, fc = (mask >> 2) & 1, (mask >> 1) & 1, mask & 1
            to = (1 - x if fx else x, 1 - y if fy else y, 1 - c if fc else c)
            for t in range(nt):
                k = (mask - 1) * nt + t
                copies.append(pltpu.make_async_remote_copy(
                    src_ref=srcs[t], dst_ref=lands[t].at[me], send_sem=send_sems.at[k], recv_sem=recv_sems.at[k],
                    device_id=to, device_id_type=MESH_ID))
    else:
        for s, (tx, ty) in enumerate([(1 - x, y), (x, 1 - y), (1 - x, 1 - y)]):
            for t in range(nt):
                k = s * nt + t
                copies.append(pltpu.make_async_remote_copy(
                    src_ref=srcs[t].at[2 * tx + ty], dst_ref=lands[t].at[s], send_sem=send_sems.at[k],
                    recv_sem=recv_sems.at[k], device_id=(tx, ty, c), device_id_type=MESH_ID))
    return copies


def _split_count(kind, nt):
    return (7 if kind == "gather" else 3) * nt


def _exchange_start(name, kind, srcs, land_shapes, after=None):
    nt = len(srcs)
    n = _split_count(kind, nt)
    dep_specs, dep_args = _dep_operand(after)
    nd = len(dep_args)

    def body(*refs):
        src_refs, land_refs = refs[:nt], refs[nt:2 * nt]
        send_sems, recv_sems = refs[2 * nt + nd], refs[2 * nt + nd + 1]
        token = refs[-1]
        for cp in _split_copies(kind, src_refs, land_refs, send_sems, recv_sems):
            cp.start()
        token[...] = jnp.zeros_like(token)

    lands = [pltpu.with_memory_space_constraint(lax.empty(s.shape, s.dtype), pltpu.HBM) for s in land_shapes]
    res = pl.pallas_call(
        body, name=name,
        out_shape=(pltpu.SemaphoreType.DMA((n,)), pltpu.SemaphoreType.DMA((n,)),
                   *[pltpu.HBM(s.shape, s.dtype) for s in srcs], *[pltpu.HBM(s.shape, s.dtype) for s in land_shapes],
                   jax.ShapeDtypeStruct((8, 128), F32)),
        in_specs=[HBM] * (2 * nt) + dep_specs,
        out_specs=(SEM, SEM, *([HBM] * (2 * nt)), pl.BlockSpec(memory_space=pltpu.VMEM)),
        input_output_aliases={i: 2 + i for i in range(2 * nt)},
        compiler_params=pltpu.CompilerParams(has_side_effects=DATAFLOW))(
            *[pltpu.with_memory_space_constraint(s, pltpu.HBM) for s in srcs], *lands, *dep_args)
    return res[0], res[1], list(res[2:2 + nt]), list(res[2 + nt:2 + 2 * nt]), res[-1]


def _exchange_wait(name, kind, send_sems, recv_sems, srcs, lands, after):
    nt = len(srcs)

    def body(*refs):
        src_refs, land_refs = refs[:nt], refs[nt:2 * nt]
        s_sems, r_sems = refs[2 * nt], refs[2 * nt + 1]
        for cp in _split_copies(kind, src_refs, land_refs, s_sems, r_sems):
            cp.wait_send()
            cp.wait_recv()

    res = pl.pallas_call(
        body, name=name,
        out_shape=tuple(pltpu.HBM(s.shape, s.dtype) for s in list(srcs) + list(lands)),
        in_specs=[HBM] * (2 * nt) + [SEM, SEM, pl.BlockSpec(memory_space=pl.ANY)],
        out_specs=tuple([HBM] * (2 * nt)),
        input_output_aliases={i: i for i in range(2 * nt)},
        compiler_params=pltpu.CompilerParams(has_side_effects=DATAFLOW))(
            *srcs, *lands, send_sems, recv_sems, after)
    return list(res[:nt]), list(res[nt:])


def _add_sibling(name, term, recv, rows):
    _, r, w = term.shape
    cidx = lax.axis_index("c").astype(jnp.int32).reshape(1)

    def body(c_ref, a_ref, b_ref, o_ref):
        o_ref[...] = (a_ref[...].astype(F32) + b_ref[...].astype(F32)).astype(o_ref.dtype)

    grid_spec = pltpu.PrefetchScalarGridSpec(
        num_scalar_prefetch=1, grid=(4, r // rows),
        in_specs=[pl.BlockSpec((None, rows, w), lambda q, i, c_ref: (2 * q + c_ref[0], i, 0)),
                  pl.BlockSpec((None, rows, w), lambda q, i, c_ref: (q, i, 0))],
        out_specs=pl.BlockSpec((None, rows, w), lambda q, i, c_ref: (q, i, 0)))
    return pl.pallas_call(
        body, name=name, grid_spec=grid_spec, out_shape=jax.ShapeDtypeStruct((4, r, w), term.dtype),
        compiler_params=_params(("parallel", "parallel")))(cidx, term, recv)


def _add_sibling_small(name, terms, recvs):
    nt = len(terms)

    def body(*refs):
        c = lax.axis_index("c")
        for t_ref, r_ref, o_ref in zip(refs[:nt], refs[nt:2 * nt], refs[2 * nt:]):
            for q in range(4):
                o_ref[q] = (t_ref[2 * q + c].astype(F32) + r_ref[q].astype(F32)).astype(o_ref.dtype)

    return pl.pallas_call(
        body, name=name, out_shape=[jax.ShapeDtypeStruct((4,) + t.shape[1:], t.dtype) for t in terms],
        compiler_params=_params())(*terms, *recvs)


def _add_chips(name, sums, recv, rows):
    _, r, w = sums.shape
    qidx = (2 * lax.axis_index("x") + lax.axis_index("y")).astype(jnp.int32).reshape(1)

    def body(q_ref, a_ref, b_ref, o_ref):
        o_ref[...] = ((a_ref[...].astype(F32) + b_ref[0].astype(F32))
                      + (b_ref[1].astype(F32) + b_ref[2].astype(F32)))

    grid_spec = pltpu.PrefetchScalarGridSpec(
        num_scalar_prefetch=1, grid=(r // rows,),
        in_specs=[pl.BlockSpec((None, rows, w), lambda i, q_ref: (q_ref[0], i, 0)),
                  pl.BlockSpec((3, rows, w), lambda i, q_ref: (0, i, 0))],
        out_specs=pl.BlockSpec((rows, w), lambda i, q_ref: (i, 0)))
    return pl.pallas_call(
        body, name=name, grid_spec=grid_spec, out_shape=jax.ShapeDtypeStruct((r, w), F32),
        compiler_params=_params(("parallel",)))(qidx, sums, recv)


def _rs_rows(a):
    return SHARD_IN // 4 if a.shape[1] == SHARD_IN else a.shape[1]


def _reduce_scatter_start(tag, names, terms):
    recv1 = _exchange_sibling("exchange_sibling_" + tag, terms)
    if len(terms) == 1:
        sums = [_add_sibling("add_sibling_" + names[0], terms[0], recv1[0], _rs_rows(terms[0]))]
    else:
        sums = _add_sibling_small("add_sibling_" + tag, terms, recv1)
    lands =[jax.ShapeDtypeStruct((3,) + s.shape[1:], s.dtype) for s in sums]
    send_sems, recv_sems, sums, lands, token = _exchange_start("exchange_chips_start_" + tag, "chips", sums, lands)
    return (tag, names, send_sems, recv_sems, sums, lands), token


def _reduce_scatter_wait(state, after):
    tag, names, send_sems, recv_sems, sums, lands = state
    sums, recv2 = _exchange_wait("exchange_chips_wait_" + tag, "chips", send_sems, recv_sems, sums, lands, after)
    return names, sums, recv2


def _adamw(name, w, g, m, v):
    def body(w_ref, g_ref, m_ref, v_ref, d_ref, nm_ref, nv_ref):
        d_ref[...], nm_ref[...], nv_ref[...] = _adam_math(w_ref[...], g_ref[...], m_ref[...], v_ref[...])

    return pl.pallas_call(
        body, name=name, out_shape=[jax.ShapeDtypeStruct(w.shape, F32)] * 3, compiler_params=_params())(w, g, m, v)


def _adam_math(w, g, m, v):
    nm = ADAM_B1 * m + (1.0 - ADAM_B1) * g
    nv = ADAM_B2 * v + (1.0 - ADAM_B2) * (g * g)
    c1 = 1.0 - ADAM_B1 ** ADAM_STEP
    c2 = 1.0 - ADAM_B2 ** ADAM_STEP
    return -ADAM_LR * ((nm / c1) / (jnp.sqrt(nv / c2) + ADAM_EPS) + ADAM_WD * w), nm, nv


def _adamw_chips(name, sums, recv, w, m, v, transposed, rows=None, dep=None):
    r, c = w.shape
    rows = r if rows is None else rows
    qidx = (2 * lax.axis_index("x") + lax.axis_index("y")).astype(jnp.int32).reshape(1)
    dep_specs, dep_args = _dep_operand(dep)

    def body(q_ref, a_ref, b_ref, w_ref, m_ref, v_ref, *rest):
        g_ref, d_ref, nm_ref, nv_ref = rest[-4:]
        g = (a_ref[...].astype(F32) + b_ref[0].astype(F32)) + (b_ref[1].astype(F32) + b_ref[2].astype(F32))
        if transposed:
            g = g.T
        g_ref[...] = g
        d_ref[...], nm_ref[...], nv_ref[...] = _adam_math(w_ref[...], g, m_ref[...], v_ref[...])

    row = pl.BlockSpec((rows, c), lambda i, q_ref: (i, 0))
    if transposed:
        term_specs = [pl.BlockSpec((None, c, rows), lambda i, q_ref: (q_ref[0], 0, i)),
                      pl.BlockSpec((3, c, rows), lambda i, q_ref: (0, 0, i))]
    else:
        term_specs = [pl.BlockSpec((None, rows, c), lambda i, q_ref: (q_ref[0], i, 0)),
                      pl.BlockSpec((3, rows, c), lambda i, q_ref: (0, i, 0))]
    grid_spec = pltpu.PrefetchScalarGridSpec(
        num_scalar_prefetch=1, grid=(r // rows,), in_specs=term_specs + [row, row, row] + dep_specs,
        out_specs=[row] * 4)
    return pl.pallas_call(
        body, name=name, grid_spec=grid_spec, out_shape=[jax.ShapeDtypeStruct((r, c), F32)] * 4,
        compiler_params=_params(("parallel",)))(qidx, sums, recv, w, m, v, *dep_args)


def _sum_devices(gathered):
    def body(g_ref, o_ref):
        acc = g_ref[0]
        for j in range(1, N_DEV):
            acc = acc + g_ref[j]
        o_ref[...] = acc

    return pl.pallas_call(
        body, name="sum_devices", out_shape=jax.ShapeDtypeStruct(gathered.shape[1:], F32),
        compiler_params=_params())(gathered)


def _rows128(a, rows):
    flat = a.reshape(-1)
    return jnp.pad(flat, (0, rows * 128 - flat.shape[0])).reshape(rows, 128)


def kernel(x, mem, pre_norm, w_in, merge_bias, na_rpb, mem_norm, w_mem_kv, w_branch_a, w_branch_b, w_branch_c, w_out, post_norm, loss_target, m_pre_norm, m_w_in, m_merge_bias, m_na_rpb, m_mem_norm, m_w_mem_kv, m_w_branch_a, m_w_branch_b, m_w_branch_c, m_w_out, m_post_norm, v_pre_norm, v_w_in, v_merge_bias, v_na_rpb, v_mem_norm, v_w_mem_kv, v_w_branch_a, v_w_branch_b, v_w_branch_c, v_w_out, v_post_norm):
    wt_in_s = w_in[0].T.astype(BF16)
    rows_s = jnp.concatenate([w_mem_kv[0], w_out[0]], axis=0).astype(BF16)
    cols_s = jnp.concatenate([w_branch_a[0].T, w_branch_b[0].T, w_branch_c[0].T], axis=0).astype(BF16)
    mb_s = jnp.pad(merge_bias[0], ((0, 5), (0, 0)))
    wt_in = _all_gather(wt_in_s).reshape(N_IN, D_MODEL)

    late_own = [rows_s, cols_s, mb_s]
    late_lands = [jax.ShapeDtypeStruct((N_DEV,) + s.shape, s.dtype) for s in late_own]
    l_send, l_recv, late_own, late_lands, late_token = _exchange_start("gather_late_start", "gather", late_own,
                                                                       late_lands, after=wt_in)
    me = 4 * lax.axis_index("x") + 2 * lax.axis_index("y") + lax.axis_index("c")

    def late_weights(after):
        own, lands = _exchange_wait("gather_late_wait", "gather", l_send, l_recv, late_own, late_lands, after)
        g_rows, g_cols, g_mb = [lax.dynamic_update_slice(land, o[None], (me, 0, 0)) for land, o in zip(lands, own)]
        return (g_mb[:, :3].transpose(1, 0, 2).reshape(3, D_MODEL),
                g_rows[:, :128].reshape(D_MODEL, D_MODEL), g_cols[:, 0:128].reshape(D_MODEL, 512),
                g_cols[:, 128:256].reshape(D_MODEL, 512), g_cols[:, 256:384].reshape(D_MODEL, 512),
                g_rows[:, 128:].reshape(D_MODEL, D_MODEL))

    rs_state = []

    def reduce_start(grads):
        if "wt_in" in grads:
            state, token = _reduce_scatter_start("w_in", ["w_in"],
                                                 [grads["wt_in"].reshape(N_DEV, SHARD_IN, D_MODEL)])
        else:
            gmb_t = jnp.pad(grads["merge_bias"].reshape(3, N_DEV, 128).transpose(1, 0, 2), ((0, 0), (0, 5), (0, 0)))
            names = ["w_kv", "w_out", "a", "b", "c", "mb"]
            terms = [grads["w_kv"].reshape(N_DEV, 128, D_MODEL), grads["w_out"].reshape(N_DEV, 128, D_MODEL),
                     grads["wt_a"].reshape(N_DEV, 128, 512), grads["wt_b"].reshape(N_DEV, 128, 512),
                     grads["wt_c"].reshape(N_DEV, 128, 512), gmb_t]
            state, token = _reduce_scatter_start("rest", names, terms)
        rs_state.append(state)
        return token

    loss_term, grad_x, grads = _local_step(
        x[0], mem[0], loss_target[0], pre_norm, mem_norm, post_norm, na_rpb[0], wt_in, late_weights,
        dep_in=late_token, reduce_start=reduce_start)

    small = jnp.concatenate([_rows128(grads["pre_norm"], 8), _rows128(grads["mem_norm"], 8),
                             _rows128(grads["post_norm"], 8), _rows128(grads["na_rpb"], 32),
                             _rows128(loss_term, 8)], axis=0)
    s_send, s_recv, s_own, s_land, s_token = _exchange_start(
        "gather_small_start", "gather", [small], [jax.ShapeDtypeStruct((N_DEV,) + small.shape, F32)])
    grad = {}
    weights = {
        "pre_norm": (pre_norm, m_pre_norm, v_pre_norm), "w_in": (w_in, m_w_in, v_w_in),
        "merge_bias": (merge_bias, m_merge_bias, v_merge_bias), "na_rpb": (na_rpb, m_na_rpb, v_na_rpb),
        "mem_norm": (mem_norm, m_mem_norm, v_mem_norm), "w_mem_kv": (w_mem_kv, m_w_mem_kv, v_w_mem_kv),
        "w_branch_a": (w_branch_a, m_w_branch_a, v_w_branch_a), "w_branch_b": (w_branch_b, m_w_branch_b, v_w_branch_b),
        "w_branch_c": (w_branch_c, m_w_branch_c, v_w_branch_c), "w_out": (w_out, m_w_out, v_w_out),
        "post_norm": (post_norm, m_post_norm, v_post_norm)}
    order = ["pre_norm", "w_in", "merge_bias", "na_rpb", "mem_norm", "w_mem_kv", "w_branch_a", "w_branch_b",
             "w_branch_c", "w_out", "post_norm"]
    delta, new_m, new_v = {}, {}, {}

    def update(n):
        w, m, v = weights[n]
        shape = w.shape
        two_d = (-1, shape[-1])
        dl, nm, nv = _adamw("adamw_" + n, w.reshape(two_d), grad[n].reshape(two_d), m.reshape(two_d),
                            v.reshape(two_d))
        delta[n], new_m[n], new_v[n] = dl.reshape(shape), nm.reshape(shape), nv.reshape(shape)

    def update_sharded(n, sums, recv, transposed, rows=None, dep=None):
        w, m, v = weights[n]
        g, dl, nm, nv = _adamw_chips("adamw_" + n, sums, recv, w[0], m[0], v[0], transposed, rows, dep)
        grad[n], delta[n], new_m[n], new_v[n] = g[None], dl[None], nm[None], nv[None]
        return dl

    _, sums, recv2 = _reduce_scatter_wait(rs_state[0], s_token)
    dep = None
    for i, (n, transposed) in enumerate((("w_mem_kv", False), ("w_out", False), ("w_branch_a", True),
                                         ("w_branch_b", True), ("w_branch_c", True))):
        dep = update_sharded(n, sums[i], recv2[i], transposed, dep=dep)
    grad["merge_bias"] = _add_chips("add_chips_mb", sums[5], recv2[5], 8)[:3][None]
    update("merge_bias")
    s_own, s_land = _exchange_wait("gather_small_wait", "gather", s_send, s_recv, s_own, s_land, dep)
    total = _sum_devices(lax.dynamic_update_slice(s_land[0], s_own[0][None], (me, 0, 0)))
    loss = total[56, 0]
    grad.update({"pre_norm": total[0:8].reshape(1, D_MODEL), "mem_norm": total[8:16].reshape(1, D_MODEL),
                 "post_norm": total[16:24].reshape(1, D_MODEL),
                 "na_rpb": total[24:56].reshape(-1)[:8 * 15 * 31].reshape(1, 8, 15, 31)})
    for n in ("pre_norm", "na_rpb", "mem_norm", "post_norm"):
        update(n)
    _, sums_in, recv_in = _reduce_scatter_wait(rs_state[1], delta["post_norm"])
    update_sharded("w_in", sums_in[0], recv_in[0], True, 256)

    return (loss, grad_x[None], *[grad[n] for n in order], *[delta[n] for n in order],
            *[new_m[n] for n in order], *[new_v[n] for n in order])
```

```python
import functools

import numpy as np
import jax
import jax.numpy as jnp
from jax import lax
from jax.experimental import pallas as pl
from jax.experimental.pallas import tpu as pltpu

F32 = jnp.float32
BF16 = jnp.bfloat16

SEQ = 2048
D_MODEL = 1024
N_IN = 11264
N_DEV = 8
SHARD_IN = N_IN // N_DEV
HEAD_DIM = 64
GRID_W = 64
NA_ROWS = 8
MEM_LEN = 256
DILATIONS = (1, 4, 16)
REACH = 64
ROPE_THETA = 500000.0
ROPE_DIM = 16
EPS = 1e-6
NEG = -1e30
ADAM_LR = 0.001
ADAM_B1 = 0.9
ADAM_B2 = 0.999
ADAM_EPS = 1e-08
ADAM_WD = 0.01
ADAM_STEP = 10

VMEM_LIMIT_BYTES = 56 * 1024 * 1024
MESH_ID = pl.DeviceIdType.MESH

NN = (((1,), (0,)), ((), ()))
NT = (((1,), (1,)), ((), ()))
TN = (((0,), (0,)), ((), ()))


def _params(sem=None):
    return pltpu.CompilerParams(dimension_semantics=sem, vmem_limit_bytes=VMEM_LIMIT_BYTES)


def _iota(shape, dim):
    return lax.broadcasted_iota(jnp.int32, shape, dim)


def _sigmoid(x):
    return 1.0 / (1.0 + jnp.exp(-x))


def _rope_tables():
    half = ROPE_DIM // 2
    inv = (ROPE_THETA ** (-np.arange(half, dtype=np.float64) * 2.0 / ROPE_DIM)).astype(np.float32)
    pos = np.arange(SEQ, dtype=np.float32)
    ang = pos[:, None] * inv[None, :]
    cos, sin = np.cos(ang), np.sin(ang)
    zeros = np.zeros_like(cos)
    rest = HEAD_DIM - ROPE_DIM
    c64 = np.concatenate([cos, cos, np.ones((SEQ, rest), np.float32)], axis=1)
    s1 = np.concatenate([zeros, sin, np.zeros((SEQ, rest), np.float32)], axis=1)
    s2 = np.concatenate([-sin, zeros, np.zeros((SEQ, rest), np.float32)], axis=1)

    def fold(t, d):
        return t.reshape(SEQ // d, d, t.shape[1]).transpose(1, 0, 2).reshape(SEQ, t.shape[1])

    tabs = [np.stack([np.tile(fold(t, d), (1, 2)) for t in (c64, s1, s2)], axis=0) for d in DILATIONS]
    return jnp.asarray(np.stack(tabs, axis=0), dtype=F32)


def _rope(a, c, s1, s2):
    return a * c + pltpu.roll(a, 8, 1) * s1 + pltpu.roll(a, 120, 1) * s2


def _rope_t(a, c, s1, s2):
    return a * c + pltpu.roll(a * s1, 120, 1) + pltpu.roll(a * s2, 8, 1)


def _perm_of_block(j):
    return jnp.where(j < 3, 0, jnp.where(j < 6, 1, jnp.where(j < 9, 2, 0)))


def _mm(name, a, b, out_shape, out_dtype, grid, a_spec, b_spec, o_spec, acc_shape, dims, k_axis, nk):
    def body(a_ref, b_ref, o_ref, acc_ref):
        k = pl.program_id(k_axis)

        @pl.when(k == 0)
        def _():
            acc_ref[...] = jnp.zeros(acc_shape, F32)

        acc_ref[...] += lax.dot_general(a_ref[...], b_ref[...], dims, preferred_element_type=F32)

        @pl.when(k == nk - 1)
        def _():
            o_ref[...] = acc_ref[...].astype(out_dtype)

    sem = tuple("arbitrary" if ax == k_axis else "parallel" for ax in range(len(grid)))
    return pl.pallas_call(
        body, name=name, grid=grid, in_specs=[a_spec, b_spec], out_specs=o_spec,
        out_shape=jax.ShapeDtypeStruct(out_shape, out_dtype),
        scratch_shapes=[pltpu.VMEM(acc_shape, F32)], compiler_params=_params(sem))(a, b)


def _mm_simple(name, a, b, dims, out_dtype, tm, tn, tk):
    if dims is NN:
        m, kk = a.shape
        n = b.shape[1]
        a_spec = pl.BlockSpec((tm, tk), lambda i, j, k: (i, k))
        b_spec = pl.BlockSpec((tk, tn), lambda i, j, k: (k, j))
    elif dims is NT:
        m, kk = a.shape
        n = b.shape[0]
        a_spec = pl.BlockSpec((tm, tk), lambda i, j, k: (i, k))
        b_spec = pl.BlockSpec((tn, tk), lambda i, j, k: (j, k))
    else:
        kk, m = a.shape
        n = b.shape[1]
        a_spec = pl.BlockSpec((tk, tm), lambda i, j, k: (k, i))
        b_spec = pl.BlockSpec((tk, tn), lambda i, j, k: (k, j))
    grid = (m // tm, n // tn, kk // tk)
    o_spec = pl.BlockSpec((tm, tn), lambda i, j, k: (i, j))
    return _mm(name, a, b, (m, n), out_dtype, grid, a_spec, b_spec, o_spec, (tm, tn), dims, 2, kk // tk)


def _rmsnorm_fwd(name, x, gain, rows):
    n, d = x.shape

    def body(x_ref, g_ref, o_ref):
        xv = x_ref[...]
        rstd = lax.rsqrt(jnp.mean(xv * xv, axis=1, keepdims=True) + EPS)
        o_ref[...] = (xv * rstd * g_ref[...]).astype(BF16)

    return pl.pallas_call(
        body, name=name, grid=(n // rows,),
        in_specs=[pl.BlockSpec((rows, d), lambda i: (i, 0)), pl.BlockSpec((1, d), lambda i: (0, 0))],
        out_specs=pl.BlockSpec((rows, d), lambda i: (i, 0)),
        out_shape=jax.ShapeDtypeStruct((n, d), BF16), compiler_params=_params(("parallel",)))(x, gain)


def _folded_rows(first, rows, d):
    if d == 1:
        return pl.ds(pl.multiple_of(first, rows), rows)
    mlen = SEQ // d
    return pl.ds((first % mlen) * d + first // mlen, rows, stride=d)


def _prenorm_fold(x, gain):
    rows = 128

    nchunk = D_MODEL // 128

    def body(*refs):
        x_refs, g_ref, hs_ref, hst_ref = refs[:nchunk], refs[nchunk], refs[nchunk + 1], refs[nchunk + 2]
        first = pl.program_id(0) * rows
        for p, d in enumerate(DILATIONS):
            idx = _folded_rows(first, rows, d)
            xv = jnp.concatenate([r[idx, :] for r in x_refs], axis=1)
            rstd = lax.rsqrt(jnp.mean(xv * xv, axis=1, keepdims=True) + EPS)
            h = xv * rstd * g_ref[...]
            hs_ref[p] = h.astype(BF16)
            hst_ref[p] = h.T.astype(BF16)

    x_specs = [pl.BlockSpec((SEQ, 128), functools.partial(lambda c, i: (0, c), c)) for c in range(nchunk)]
    return pl.pallas_call(
        body, name="prenorm", grid=(SEQ // rows,),
        in_specs=x_specs + [pl.BlockSpec((1, D_MODEL), lambda i: (0, 0))],
        out_specs=[pl.BlockSpec((3, rows, D_MODEL), lambda i: (0, i, 0)),
                   pl.BlockSpec((3, D_MODEL, rows), lambda i: (0, 0, i))],
        out_shape=[jax.ShapeDtypeStruct((3, SEQ, D_MODEL), BF16), jax.ShapeDtypeStruct((3, D_MODEL, SEQ), BF16)],
        compiler_params=_params(("parallel",)))(*([x] * nchunk), gain)


def _prenorm_bwd(x, gain, dh, dout):
    rows = 256

    def body(x_ref, g_ref, a_ref, do_ref, dx_ref, gg_ref):
        xv = x_ref[...]
        rstd = lax.rsqrt(jnp.mean(xv * xv, axis=1, keepdims=True) + EPS)
        xn = xv * rstd
        dh = jnp.concatenate([a_ref[c] for c in range(D_MODEL // 128)], axis=1)
        gdh = dh * g_ref[...]
        dx_ref[...] = rstd * (gdh - xn * jnp.mean(gdh * xn, axis=1, keepdims=True)) + do_ref[...]

        @pl.when(pl.program_id(0) == 0)
        def _():
            gg_ref[...] = jnp.zeros((1, D_MODEL), F32)

        gg_ref[...] += jnp.sum(dh * xn, axis=0, keepdims=True)

    row = pl.BlockSpec((rows, D_MODEL), lambda i: (i, 0))
    vec = pl.BlockSpec((1, D_MODEL), lambda i: (0, 0))
    return pl.pallas_call(
        body, name="prenorm_bwd", grid=(SEQ // rows,),
        in_specs=[row, vec, pl.BlockSpec((D_MODEL // 128, rows, 128), lambda i: (0, i, 0)), row], out_specs=[row, vec],
        out_shape=[jax.ShapeDtypeStruct((SEQ, D_MODEL), F32), jax.ShapeDtypeStruct((1, D_MODEL), F32)],
        compiler_params=_params(("arbitrary",)))(x, gain, dh, dout)


def _memnorm_bwd(mem, dmemn):
    def body(m_ref, d_ref, gg_ref):
        mv = m_ref[...]
        rstd = lax.rsqrt(jnp.mean(mv * mv, axis=1, keepdims=True) + EPS)
        gg_ref[...] = jnp.sum(d_ref[...] * mv * rstd, axis=0, keepdims=True)

    return pl.pallas_call(
        body, name="memnorm_bwd", out_shape=jax.ShapeDtypeStruct((1, D_MODEL), F32),
        compiler_params=_params())(mem, dmemn)


def _dep_operand(dep):
    return ([], []) if dep is None else ([pl.BlockSpec(memory_space=pl.ANY)], [dep])


def _in_proj(hs, wt, tabs, dep=None):
    tm, tn = 512, 512
    dep_specs, dep_args = _dep_operand(dep)

    def body(h_ref, w_ref, t_ref, *rest):
        o_ref = rest[-1]
        j = pl.program_id(0)
        is_rope = jnp.logical_and(j < 9, j % 3 != 2)
        row_slices = [slice(r * tm, (r + 1) * tm) for r in range(SEQ // tm)]

        def product(rs):
            return lax.dot_general(h_ref[rs, :], w_ref[...], NT, preferred_element_type=F32)

        @pl.when(is_rope)
        def _():
            for rs in row_slices:
                acc = product(rs)
                c, s1, s2 = t_ref[0, rs, :], t_ref[1, rs, :], t_ref[2, rs, :]
                for q in range(tn // 128):
                    a = acc[:, q * 128:(q + 1) * 128]
                    o_ref[rs, q * 128:(q + 1) * 128] = _rope(a, c, s1, s2).astype(BF16)

        @pl.when(jnp.logical_not(is_rope))
        def _():
            for rs in row_slices:
                o_ref[rs, :] = product(rs).astype(BF16)

    return pl.pallas_call(
        body, name="in_proj", grid=(N_IN // tn,),
        in_specs=[pl.BlockSpec((None, SEQ, D_MODEL), lambda j: (_perm_of_block(j), 0, 0)),
                  pl.BlockSpec((tn, D_MODEL), lambda j: (j, 0)),
                  pl.BlockSpec((None, 3, SEQ, 128), lambda j: (_perm_of_block(j), 0, 0, 0))] + dep_specs,
        out_specs=pl.BlockSpec((SEQ, tn), lambda j: (0, j)),
        out_shape=jax.ShapeDtypeStruct((SEQ, N_IN), BF16),
        compiler_params=_params(("parallel",)))(hs, wt, tabs, *dep_args)


def _piece_blocks(pieces):
    return [(a, h * 512) for a, p in enumerate(pieces) for h in range(p.shape[1] // 512)]


def _block_fetch(piece_refs, blocks, buf, sem):
    def start(block, slot):
        for b, (a, col) in enumerate(blocks):
            @pl.when(block == b)
            def _():
                pltpu.make_async_copy(piece_refs[a].at[:, pl.ds(col, 512)], buf.at[slot], sem.at[slot]).start()

    def wait(slot):
        pltpu.make_async_copy(piece_refs[0].at[:, pl.ds(0, 512)], buf.at[slot], sem.at[slot]).wait()

    return start, wait


def _in_proj_dw(pieces, hst, dep=None):
    tn = 512
    blocks = _piece_blocks(pieces)
    nblk = len(blocks)
    npc = len(pieces)
    dep_specs, dep_args = _dep_operand(dep)

    def body(h_ref, *rest):
        piece_refs = rest[:npc]
        o_ref, buf, sem = rest[-3:]
        j = pl.program_id(0)
        slot = j % 2
        start, wait = _block_fetch(piece_refs, blocks, buf, sem)

        @pl.when(j == 0)
        def _():
            start(j, slot)

        wait(slot)

        @pl.when(j + 1 < nblk)
        def _():
            start(j + 1, 1 - slot)

        acc = jnp.dot(h_ref[...], buf[slot], preferred_element_type=F32)
        o_ref[...] = acc.T.astype(BF16)

    return pl.pallas_call(
        body, name="in_proj_dw", grid=(nblk,),
        in_specs=[pl.BlockSpec((None, D_MODEL, SEQ), lambda j: (_perm_of_block(j), 0, 0))] + [ANY] * npc + dep_specs,
        out_specs=pl.BlockSpec((tn, D_MODEL), lambda j: (j, 0)),
        out_shape=jax.ShapeDtypeStruct((N_IN, D_MODEL), BF16),
        scratch_shapes=[pltpu.VMEM((2, SEQ, tn), BF16), pltpu.SemaphoreType.DMA((2,))],
        compiler_params=_params(("arbitrary",)))(hst, *pieces, *dep_args)


def _in_proj_dh(pieces, wt, dep=None):
    tk = 512
    blocks = _piece_blocks(pieces)
    nblk = len(blocks)
    npc = len(pieces)
    nchunk = D_MODEL // 128

    def col(s):
        return jnp.where(s < 3, s, jnp.where(s < 16, s + 6, s - 13))

    dep_specs, dep_args = _dep_operand(dep)

    def body(w_ref, *rest):
        piece_refs = rest[:npc]
        o_ref, acc_ref, buf, sem = rest[-4:]
        s = pl.program_id(0)
        slot = s % 2
        start, wait = _block_fetch(piece_refs, blocks, buf, sem)

        @pl.when(s == 0)
        def _():
            start(col(s), slot)

        wait(slot)

        @pl.when(s + 1 < nblk)
        def _():
            start(col(s + 1), 1 - slot)

        row_slices = [slice(r * 512, (r + 1) * 512) for r in range(SEQ // 512)]

        def product(rs):
            return jnp.dot(buf[slot, rs, :], w_ref[...], preferred_element_type=F32)

        def accumulate(cond, to_out, init):
            @pl.when(cond)
            def _():
                for rs in row_slices:
                    prod = product(rs)
                    if not to_out:
                        if init:
                            acc_ref[rs, :] = prod
                        else:
                            acc_ref[rs, :] += prod
                        continue
                    for c in range(nchunk):
                        if init:
                            o_ref[c, rs, :] = prod[:, c * 128:(c + 1) * 128]
                        else:
                            o_ref[c, rs, :] += prod[:, c * 128:(c + 1) * 128]

        accumulate(s == 0, True, True)
        accumulate(jnp.logical_and(s > 0, s < 16), True, False)
        accumulate(jnp.logical_or(s == 16, s == 19), False, True)
        accumulate(jnp.logical_and(s > 16, s != 19), False, False)
        for last, d in ((18, 4), (21, 16)):
            @pl.when(s == last)
            def _():
                mlen = SEQ // d
                for r in range(d):
                    for c in range(nchunk):
                        o_ref[c, pl.ds(r, mlen, stride=d), :] += acc_ref[r * mlen:(r + 1) * mlen,
                                                                         c * 128:(c + 1) * 128]

    return pl.pallas_call(
        body, name="in_proj_dh", grid=(nblk,),
        in_specs=[pl.BlockSpec((tk, D_MODEL), lambda s: (col(s), 0))] + [ANY] * npc + dep_specs,
        out_specs=pl.BlockSpec((nchunk, SEQ, 128), lambda s: (0, 0, 0)),
        out_shape=jax.ShapeDtypeStruct((nchunk, SEQ, 128), F32),
        scratch_shapes=[pltpu.VMEM((SEQ, D_MODEL), F32), pltpu.VMEM((2, SEQ, tk), BF16),
                        pltpu.SemaphoreType.DMA((2,))],
        compiler_params=_params(("arbitrary",)))(wt, *pieces, *dep_args)


def _head_lanes(lanes, hh):
    return lanes >= 64 if hh == 1 else lanes < 64


def _head_rows(x, lanes, hh, pair):
    if not pair:
        return jnp.max(x, axis=1, keepdims=True)
    return jnp.max(jnp.where(_head_lanes(lanes, hh), x, -jnp.inf), axis=1, keepdims=True)


def _mask_head(x, lanes, hh, pair, scale=1.0):
    if not pair:
        return x
    xf = x.astype(F32) if scale == 1.0 else x.astype(F32) * scale
    return jnp.where(_head_lanes(lanes, hh), xf, 0.0).astype(BF16)


def _window(mode, qi, tq, mlen, tk):
    if mode == "dil":
        q0 = qi * tq
        seg = (q0 // mlen) * mlen
        ks = jnp.clip(q0 - REACH, seg, seg + mlen - tk)
        return pl.multiple_of(ks, 64)
    if mode == "na":
        r_start = jnp.clip(qi - NA_ROWS // 2, 0, SEQ // GRID_W - NA_ROWS)
        return pl.multiple_of(r_start * GRID_W, 64)
    return 0


def _band_mask(qi, tq, tk, ks):
    qpos = qi * tq + _iota((tq, tk), 0)
    kpos = ks + _iota((tq, tk), 1)
    return jnp.where(jnp.abs(qpos - kpos) <= REACH, 0.0, NEG).astype(F32)


def _stack_heads(x, lanes, pair, scale=1.0):
    if not pair:
        return x
    return jnp.concatenate([_mask_head(x, lanes, hh, pair, scale) for hh in range(2)], axis=0)


def _stack_rows(x, lanes, pair):
    if not pair:
        return _head_rows(x, lanes, 0, pair)
    return jnp.concatenate([_head_rows(x, lanes, hh, pair) for hh in range(2)], axis=0)


def _unstack_heads(x, lanes, pair, tq):
    if not pair:
        return x
    return jnp.where(lanes < 64, x[:tq], x[tq:])


def _scores(mode, qst, k, sscale, band, qi, bias_ref, pair):
    s = lax.dot_general(qst, k, NT, preferred_element_type=F32)
    if sscale != 1.0:
        s = s * sscale
    if mode == "dil":
        s = s + jnp.concatenate([band, band], axis=0)
    elif mode == "na":
        off = qi - jnp.clip(qi - NA_ROWS // 2, 0, SEQ // GRID_W - NA_ROWS)
        s = s + jnp.concatenate([bias_ref[0, off], bias_ref[1, off]], axis=0)
    return s


def _attn_cfg(mode, d):
    if mode == "dil":
        mlen = SEQ // d
        return dict(pair=True, tq=128, tk=min(256, mlen), mlen=mlen, lk=SEQ, scale=HEAD_DIM ** -0.5, units=4,
                    nsub=ATTN_SUBTILES)
    if mode == "na":
        return dict(pair=True, tq=GRID_W, tk=NA_ROWS * GRID_W, mlen=SEQ, lk=SEQ, scale=HEAD_DIM ** -0.5, units=4,
                    nsub=ATTN_SUBTILES)
    return dict(pair=False, tq=128, tk=MEM_LEN, mlen=SEQ, lk=MEM_LEN, scale=128 ** -0.5, units=4,
                nsub=ATTN_SUBTILES)


ATTN_SUBTILES = 4


def _attn_fwd(name, mode, q_arr, k_arr, v_arr, qcol, kcol, vcol, d=1, bias=None):
    cfg = _attn_cfg(mode, d)
    pair, tq, tk, mlen, lk, scale = cfg["pair"], cfg["tq"], cfg["tk"], cfg["mlen"], cfg["lk"], cfg["scale"]
    qscale, sscale = (scale, 1.0) if pair else (1.0, scale)
    nsub = cfg["nsub"]
    rows = nsub * tq

    def body(*refs):
        if mode == "na":
            q_ref, k_ref, v_ref, bias_ref, o_ref, l_ref = refs
        else:
            q_ref, k_ref, v_ref, o_ref, l_ref = refs
            bias_ref = None
        lanes = _iota((tq, 128), 1)
        qis = [pl.program_id(1) * nsub + sub for sub in range(nsub)]
        kss = [_window(mode, qi, tq, mlen, tk) for qi in qis]
        vs = [v_ref[pl.ds(ks, tk), :] for ks in kss]
        bands = [_band_mask(qi, tq, tk, ks) if mode == "dil" else None for qi, ks in zip(qis, kss)]
        ss = []
        for sub in range(nsub):
            qst = _stack_heads(q_ref[sub * tq:(sub + 1) * tq, :], lanes, pair, qscale)
            k = k_ref[pl.ds(kss[sub], tk), :]
            ss.append(_scores(mode, qst, k, sscale, bands[sub], qis[sub], bias_ref, pair))
        ms = [jnp.max(s_, axis=1, keepdims=True) for s_ in ss]
        ps = [jnp.exp(s_ - m) for s_, m in zip(ss, ms)]
        ls = [jnp.sum(p, axis=1, keepdims=True) for p in ps]
        os_ = [jnp.dot(p.astype(BF16), v, preferred_element_type=F32) for p, v in zip(ps, vs)]
        for sub in range(nsub):
            out = _unstack_heads(os_[sub] / ls[sub], lanes, pair, tq)
            lse = ms[sub] + jnp.log(ls[sub])
            lse = _unstack_heads(jnp.broadcast_to(lse, (lse.shape[0], 128)), lanes, pair, tq)
            dst = _folded_rows(qis[sub] * tq, tq, d) if mode == "dil" else slice(sub * tq, (sub + 1) * tq)
            o_ref[dst, :] = out
            l_ref[dst, :] = lse

    in_specs = [pl.BlockSpec((rows, 128), lambda u, i: (i, qcol + u)),
                pl.BlockSpec((lk, 128), lambda u, i: (0, kcol + u)),
                pl.BlockSpec((lk, 128), lambda u, i: (0, vcol + u))]
    args = [q_arr, k_arr, v_arr]
    if mode == "na":
        in_specs.append(pl.BlockSpec((2, NA_ROWS, GRID_W, NA_ROWS * GRID_W), lambda u, i: (u, 0, 0, 0)))
        args.append(bias)
    if mode == "dil":
        out_spec = pl.BlockSpec((SEQ, 128), lambda u, i: (0, u))
    else:
        out_spec = pl.BlockSpec((rows, 128), lambda u, i: (i, u))
    return pl.pallas_call(
        body, name=name, grid=(cfg["units"], SEQ // rows), in_specs=in_specs, out_specs=[out_spec, out_spec],
        out_shape=[jax.ShapeDtypeStruct((SEQ, 512), F32), jax.ShapeDtypeStruct((SEQ, 512), F32)],
        compiler_params=_params(("parallel", "arbitrary")))(*args)


def _attn_bwd(name, mode, q_arr, k_arr, v_arr, qcol, kcol, vcol, do, lse, dp=None, o=None, d=1, bias=None,
              tabs=None):
    cfg = _attn_cfg(mode, d)
    pair, tq, tk, mlen, lk, scale = cfg["pair"], cfg["tq"], cfg["tk"], cfg["mlen"], cfg["lk"], cfg["scale"]
    qscale, sscale = (scale, 1.0) if pair else (1.0, scale)
    nsub = cfg["nsub"]
    rows = nsub * tq
    nq = SEQ // rows
    kv_dtype = F32 if mode == "mem" else BF16

    def body(*refs):
        refs = list(refs)
        q_ref, k_ref, v_ref, do_ref, l_ref = refs[:5]
        rest = refs[5:]
        bias_ref = tq_ref = tk_ref = db_ref = None
        if mode == "dil":
            dp_ref, tq_ref, tk_ref, dq_ref, dk_ref, dv_ref, dk_acc, dv_acc = rest
        elif mode == "na":
            o_ref, bias_ref, dq_ref, dk_ref, dv_ref, db_ref, dk_acc, dv_acc = rest
        else:
            o_ref, dq_ref, dk_ref, dv_ref, dk_acc, dv_acc = rest
        step = pl.program_id(1)

        @pl.when(step == 0)
        def _():
            dk_acc[...] = jnp.zeros((lk, 128), F32)
            dv_acc[...] = jnp.zeros((lk, 128), F32)
            if mode == "na":
                db_ref[...] = jnp.zeros(db_ref.shape, F32)

        lanes = _iota((tq, 128), 1)
        qis = [step * nsub + sub for sub in range(nsub)]
        sls = [slice(sub * tq, (sub + 1) * tq) for sub in range(nsub)]
        kss = [_window(mode, qi, tq, mlen, tk) for qi in qis]
        ks_ = [k_ref[pl.ds(ks, tk), :] for ks in kss]
        vs = [v_ref[pl.ds(ks, tk), :] for ks in kss]
        qsts, dosts, lses, dphs = [], [], [], []
        for sub in range(nsub):
            if mode == "dil":
                src = _folded_rows(qis[sub] * tq, tq, d)
                dov = do_ref[src, :].astype(BF16)
                lsev = l_ref[src, :]
                dphs.append(_stack_rows(dp_ref[src, :], lanes, pair))
            else:
                dov = do_ref[sls[sub], :]
                lsev = l_ref[sls[sub], :]
                dpv = dov.astype(F32) * o_ref[sls[sub], :]
                if pair:
                    dphs.append(jnp.concatenate(
                        [jnp.sum(jnp.where(_head_lanes(lanes, hh), dpv, 0.0), axis=1, keepdims=True)
                         for hh in range(2)], axis=0))
                else:
                    dphs.append(jnp.sum(dpv, axis=1, keepdims=True))
            qsts.append(_stack_heads(q_ref[sls[sub], :], lanes, pair, qscale))
            dosts.append(_stack_heads(dov, lanes, pair))
            lses.append(_stack_rows(lsev, lanes, pair))
        bands = [_band_mask(qi, tq, tk, ks) if mode == "dil" else None for qi, ks in zip(qis, kss)]
        ss = [_scores(mode, qsts[sub], ks_[sub], sscale, bands[sub], qis[sub], bias_ref, pair) for sub in range(nsub)]
        dpms = [lax.dot_general(dosts[sub], vs[sub], NT, preferred_element_type=F32) for sub in range(nsub)]
        ps = [jnp.exp(s_ - lse) for s_, lse in zip(ss, lses)]
        dss = [p * (dpm - dph) for p, dpm, dph in zip(ps, dpms, dphs)]
        if mode == "na":
            for sub, ds in enumerate(dss):
                off = qis[sub] - jnp.clip(qis[sub] - NA_ROWS // 2, 0, SEQ // GRID_W - NA_ROWS)
                db_ref[0, off] += ds[:tq]
                db_ref[1, off] += ds[tq:]
        dsbs = [ds.astype(BF16) for ds in dss]
        dvs = [lax.dot_general(p.astype(BF16), dosts[sub], TN, preferred_element_type=F32)
               for sub, p in enumerate(ps)]
        dqs = [jnp.dot(dsb, ks_[sub], preferred_element_type=F32) * scale for sub, dsb in enumerate(dsbs)]
        dks = [lax.dot_general(dsb, qsts[sub], TN, preferred_element_type=F32) for sub, dsb in enumerate(dsbs)]
        for sub in range(nsub):
            sl = sls[sub]
            dq = _unstack_heads(dqs[sub], lanes, pair, tq)
            if mode == "dil":
                dq = _rope_t(dq, tq_ref[0, sl, :], tq_ref[1, sl, :], tq_ref[2, sl, :])
            dq_ref[sl, :] = dq.astype(BF16)
            dk_acc[pl.ds(kss[sub], tk), :] += dks[sub] if pair else dks[sub] * scale
            dv_acc[pl.ds(kss[sub], tk), :] += dvs[sub]

        @pl.when(step == nq - 1)
        def _():
            dkv = dk_acc[...]
            if mode == "dil":
                dkv = _rope_t(dkv, tk_ref[0], tk_ref[1], tk_ref[2])
            dk_ref[...] = dkv.astype(kv_dtype)
            dv_ref[...] = dv_acc[...].astype(kv_dtype)

    q_spec = pl.BlockSpec((rows, 128), lambda u, i: (i, qcol + u))
    row_spec = pl.BlockSpec((rows, 128), lambda u, i: (i, u))
    kv_out = pl.BlockSpec((lk, 128), lambda u, i: (0, u))
    whole = pl.BlockSpec((SEQ, 128), lambda u, i: (0, u))
    nat_spec = whole if mode == "dil" else row_spec
    in_specs = [q_spec,
                pl.BlockSpec((lk, 128), lambda u, i: (0, kcol + u)),
                pl.BlockSpec((lk, 128), lambda u, i: (0, vcol + u)),
                nat_spec, nat_spec]
    args = [q_arr, k_arr, v_arr, do, lse]
    out_specs = [row_spec, kv_out, kv_out]
    out_shape = [jax.ShapeDtypeStruct((SEQ, 512), BF16), jax.ShapeDtypeStruct((lk, 512), kv_dtype),
                 jax.ShapeDtypeStruct((lk, 512), kv_dtype)]
    if mode == "dil":
        in_specs += [whole, pl.BlockSpec((3, rows, 128), lambda u, i: (0, i, 0)),
                     pl.BlockSpec((3, SEQ, 128), lambda u, i: (0, 0, 0))]
        args += [dp, tabs, tabs]
    elif mode == "na":
        b_spec = pl.BlockSpec((2, NA_ROWS, GRID_W, NA_ROWS * GRID_W), lambda u, i: (u, 0, 0, 0))
        in_specs += [row_spec, b_spec]
        args += [o, bias]
        out_specs.append(b_spec)
        out_shape.append(jax.ShapeDtypeStruct((8, NA_ROWS, GRID_W, NA_ROWS * GRID_W), F32))
    else:
        in_specs.append(row_spec)
        args.append(o)
    return pl.pallas_call(
        body, name=name, grid=(cfg["units"], nq), in_specs=in_specs, out_specs=out_specs, out_shape=out_shape,
        scratch_shapes=[pltpu.VMEM((lk, 128), F32), pltpu.VMEM((lk, 128), F32)],
        compiler_params=_params(("parallel", "arbitrary")))(*args)


def _na_geometry():
    qc = _iota((GRID_W, 128), 0)
    lane = _iota((GRID_W, 128), 1)
    kc = lane & 63
    c_start = jnp.clip(qc - 8, 0, GRID_W - 16)
    valid = jnp.logical_and(kc >= c_start, kc < c_start + 16)
    return lane, valid


def _na_bias(rpb_rows):
    def body(r_ref, o_ref, t_ref):
        lane, valid = _na_geometry()
        for dd in range(14):
            row_a = jnp.broadcast_to(r_ref[dd:dd + 1, :], (GRID_W, 128))
            row_b = jnp.broadcast_to(r_ref[dd + 1:dd + 2, :], (GRID_W, 128))
            both = jnp.where(lane < 64, row_a, pltpu.roll(row_b, 64, 1))
            t = pltpu.roll(both, 128 - 15, 1, stride=1, stride_axis=0)
            t_ref[dd] = jnp.where(valid, t, NEG)
        for off in range(NA_ROWS):
            for p in range(4):
                o_ref[off, :, p * 128:(p + 1) * 128] = t_ref[2 * p - off + 7]

    return pl.pallas_call(
        body, name="na_bias", grid=(8,),
        in_specs=[pl.BlockSpec((None, 16, 128), lambda h: (h, 0, 0))],
        out_specs=pl.BlockSpec((None, NA_ROWS, GRID_W, NA_ROWS * GRID_W), lambda h: (h, 0, 0, 0)),
        out_shape=jax.ShapeDtypeStruct((8, NA_ROWS, GRID_W, NA_ROWS * GRID_W), F32),
        scratch_shapes=[pltpu.VMEM((14, GRID_W, 128), F32)],
        compiler_params=_params(("parallel",)))(rpb_rows)


def _na_bias_bwd(dbias):
    def body(d_ref, o_ref):
        lane, valid = _na_geometry()
        reverse = (_iota((GRID_W, GRID_W), 0) + _iota((GRID_W, GRID_W), 1) == GRID_W - 1).astype(F32)
        o_ref[...] = jnp.zeros((16, 128), F32)
        for dd in range(14):
            t = jnp.zeros((GRID_W, 128), F32)
            for off in range(NA_ROWS):
                for p in range(4):
                    if 2 * p - off + 7 == dd:
                        t = t + d_ref[off, :, p * 128:(p + 1) * 128]
            t = jnp.dot(reverse, jnp.where(valid, t, 0.0), precision=lax.Precision.HIGHEST,
                        preferred_element_type=F32)
            t = pltpu.roll(t, 128 - (GRID_W - 16), 1, stride=1, stride_axis=0)
            o_ref[dd:dd + 1, :] = jnp.sum(t, axis=0, keepdims=True)

    return pl.pallas_call(
        body, name="na_bias_bwd", grid=(8,),
        in_specs=[pl.BlockSpec((None, NA_ROWS, GRID_W, NA_ROWS * GRID_W), lambda h: (h, 0, 0, 0))],
        out_specs=pl.BlockSpec((None, 16, 128), lambda h: (h, 0, 0)),
        out_shape=jax.ShapeDtypeStruct((8, 16, 128), F32),
        compiler_params=_params(("parallel",)))(dbias)


GATE_ROWS = 128


def _group_weights(l0, l1, l2):
    m = jnp.maximum(jnp.maximum(l0, l1), l2)
    e0, e1, e2 = jnp.exp(l0 - m), jnp.exp(l1 - m), jnp.exp(l2 - m)
    inv = 1.0 / (e0 + e1 + e2)
    return e0 * inv, e1 * inv, e2 * inv


def _gate_specs():
    r512 = pl.BlockSpec((GATE_ROWS, 512), lambda i: (i, 0))
    r1024 = pl.BlockSpec((GATE_ROWS, D_MODEL), lambda i: (i, 0))
    silu_cols = [pl.BlockSpec((GATE_ROWS, 512), functools.partial(lambda b, i: (i, b), 13 + b)) for b in range(3)]
    logit_cols = [pl.BlockSpec((GATE_ROWS, D_MODEL), functools.partial(lambda b, i: (i, b), 8 + b)) for b in range(3)]
    return r512, r1024, silu_cols, logit_cols


def _gate_fwd(o_grp, l_grp, out_b, out_c, parts, merge_bias, wts):
    r512, r1024, silu_cols, logit_cols = _gate_specs()

    def body(o0, o1, o2, l0, l1, l2, ob, oc, ga, gb, gc, la, lb, lc, mb, wa, wb, wc,
             oa_ref, ua, ub, uc, za, zb, zc, y_ref):
        w0, w1, w2 = _group_weights(l0[...], l1[...], l2[...])
        out_a = w0 * o0[...] + w1 * o1[...] + w2 * o2[...]
        oa_ref[...] = out_a
        y = jnp.zeros((GATE_ROWS, D_MODEL), F32)
        for b, (ov, g_ref, l_ref, w_ref, u_ref, z_ref) in enumerate(
                ((out_a, ga, la, wa, ua, za), (ob[...], gb, lb, wb, ub, zb), (oc[...], gc, lc, wc, uc, zc))):
            g = g_ref[...].astype(F32)
            u = (ov * (g * _sigmoid(g))).astype(BF16)
            u_ref[...] = u
            z = lax.dot_general(u, w_ref[...], NT, preferred_element_type=F32)
            z_ref[...] = z.astype(BF16)
            gate = _sigmoid(l_ref[...].astype(F32) + mb[b:b + 1, :])
            y = y + gate * z
        y_ref[...] = y.astype(BF16)

    full = lambda shape: pl.BlockSpec(shape, lambda i: (0,) * len(shape))
    in_specs = ([r512] * 8 + silu_cols + logit_cols
                + [full((3, D_MODEL))] + [full((D_MODEL, 512))] * 3)
    out_specs = [r512] * 4 + [r1024] * 4
    out_shape = ([jax.ShapeDtypeStruct((SEQ, 512), F32)] + [jax.ShapeDtypeStruct((SEQ, 512), BF16)] * 3
                 + [jax.ShapeDtypeStruct((SEQ, D_MODEL), BF16)] * 4)
    res = pl.pallas_call(
        body, name="gate_fwd", grid=(SEQ // GATE_ROWS,), in_specs=in_specs, out_specs=out_specs,
        out_shape=out_shape, compiler_params=_params(("parallel",)))(
            *o_grp, *l_grp, out_b, out_c, parts, parts, parts, parts, parts, parts, merge_bias, *wts)
    return res[0], res[1:4], res[4:7], res[7]


def _gate_bwd(dy, z, parts, merge_bias, outs, o_grp, l_grp, wts, head_sum):
    r512, r1024, silu_cols, logit_cols = _gate_specs()

    def body(dy_ref, za, zb, zc, la, lb, lc, mb, oa, ob, oc, ga, gb, gc, o0, o1, o2, l0, l1, l2, wa, wb, wc, hs_ref,
             dla, dlb, dlc, gmb, dza, dzb, dzc, dga, dgb, dgc, do0, do1, do2, dp0, dp1, dp2, dob, doc):
        dyv = dy_ref[...].astype(F32)
        rows = []
        dos = []
        for b, (z_ref, l_ref, ov_ref, g_ref, w_ref, dl_ref, dz_ref, dg_ref) in enumerate(
                ((za, la, oa, ga, wa, dla, dza, dga), (zb, lb, ob, gb, wb, dlb, dzb, dgb),
                 (zc, lc, oc, gc, wc, dlc, dzc, dgc))):
            gate = _sigmoid(l_ref[...].astype(F32) + mb[b:b + 1, :])
            dl = dyv * z_ref[...].astype(F32) * gate * (1.0 - gate)
            dl_ref[...] = dl.astype(BF16)
            rows.append(jnp.sum(dl, axis=0, keepdims=True))
            dz = (dyv * gate).astype(BF16)
            dz_ref[...] = dz
            du = jnp.dot(dz, w_ref[...], preferred_element_type=F32)
            g = g_ref[...].astype(F32)
            sg = _sigmoid(g)
            dos.append(du * (g * sg))
            dg_ref[...] = (du * ov_ref[...] * (sg * (1.0 + g * (1.0 - sg)))).astype(BF16)

        @pl.when(pl.program_id(0) == 0)
        def _():
            gmb[...] = jnp.zeros((3, D_MODEL), F32)

        for b in range(3):
            gmb[b:b + 1, :] += rows[b]
        dob[...] = dos[1].astype(BF16)
        doc[...] = dos[2].astype(BF16)
        doa = dos[0]
        row_term = jnp.dot(doa * oa[...], hs_ref[...], precision=lax.Precision.HIGHEST, preferred_element_type=F32)
        ws = _group_weights(l0[...], l1[...], l2[...])
        for wg, do_ref, dp_ref in zip(ws, (do0, do1, do2), (dp0, dp1, dp2)):
            do_ref[...] = wg * doa
            dp_ref[...] = wg * row_term

    full = lambda shape: pl.BlockSpec(shape, lambda i: (0,) * len(shape))
    acc = pl.BlockSpec((3, D_MODEL), lambda i: (0, 0))
    in_specs = ([r1024] * 4 + logit_cols + [full((3, D_MODEL))] + [r512] * 3 + silu_cols + [r512] * 6
                + [full((D_MODEL, 512))] * 3 + [full((512, 512))])
    out_specs = [r1024] * 3 + [acc] + [r1024] * 3 + [r512] * 11
    out_shape = ([jax.ShapeDtypeStruct((SEQ, D_MODEL), BF16)] * 3 + [jax.ShapeDtypeStruct((3, D_MODEL), F32)]
                 + [jax.ShapeDtypeStruct((SEQ, D_MODEL), BF16)] * 3 + [jax.ShapeDtypeStruct((SEQ, 512), BF16)] * 3
                 + [jax.ShapeDtypeStruct((SEQ, 512), F32)] * 6 + [jax.ShapeDtypeStruct((SEQ, 512), BF16)] * 2)
    res = pl.pallas_call(
        body, name="gate_bwd", grid=(SEQ // GATE_ROWS,), in_specs=in_specs, out_specs=out_specs,
        out_shape=out_shape, compiler_params=_params(("arbitrary",)))(
            dy, *z, parts, parts, parts, merge_bias, *outs, parts, parts, parts, *o_grp, *l_grp, *wts, head_sum)
    return res[0:3], res[3], res[4:7], res[7:10], res[10:13], res[13:16], res[16], res[17]


def _post(y2, x, target, gain):
    rows = 256

    def body(y_ref, x_ref, t_ref, g_ref, do_ref, dy_ref, l_ref, gg_ref):
        yv = y_ref[...]
        rstd = lax.rsqrt(jnp.mean(yv * yv, axis=1, keepdims=True) + EPS)
        yn = yv * rstd
        gv = g_ref[...]
        err = x_ref[...] + yn * gv - t_ref[...]
        dout = err * (1.0 / D_MODEL)
        do_ref[...] = dout
        dn = dout * gv
        dy_ref[...] = (rstd * (dn - yn * jnp.mean(dn * yn, axis=1, keepdims=True))).astype(BF16)

        @pl.when(pl.program_id(0) == 0)
        def _():
            l_ref[...] = jnp.zeros((1, D_MODEL), F32)
            gg_ref[...] = jnp.zeros((1, D_MODEL), F32)

        l_ref[...] += jnp.sum(err * err, axis=0, keepdims=True)
        gg_ref[...] += jnp.sum(dout * yn, axis=0, keepdims=True)

    row = pl.BlockSpec((rows, D_MODEL), lambda i: (i, 0))
    vec = pl.BlockSpec((1, D_MODEL), lambda i: (0, 0))
    return pl.pallas_call(
        body, name="post", grid=(SEQ // rows,), in_specs=[row, row, row, vec], out_specs=[row, row, vec, vec],
        out_shape=[jax.ShapeDtypeStruct((SEQ, D_MODEL), F32), jax.ShapeDtypeStruct((SEQ, D_MODEL), BF16),
                   jax.ShapeDtypeStruct((1, D_MODEL), F32), jax.ShapeDtypeStruct((1, D_MODEL), F32)],
        compiler_params=_params(("arbitrary",)))(y2, x, target, gain)


def _local_step(x, mem, target, pre_norm, mem_norm, post_norm, na_rpb, wt_in, late_weights, dep_in=None,
                reduce_start=None):
    tabs = _rope_tables()
    hs, hst = _prenorm_fold(x, pre_norm)
    parts = _in_proj(hs, wt_in, tabs, dep_in)

    o_grp, l_grp = [], []
    for g, d in enumerate(DILATIONS):
        o, l = _attn_fwd("dil_fwd_%d" % g, "dil", parts, parts, parts, 12 * g, 12 * g + 4, 12 * g + 8, d=d)
        o_grp.append(o)
        l_grp.append(l)
    bias = _na_bias(jnp.pad(na_rpb, ((0, 0), (0, 1), (0, 128 - 31))))
    out_b, lse_b = _attn_fwd("na_fwd", "na", parts, parts, parts, 36, 40, 44, bias=bias)
    merge_bias, w_kv, wt_a, wt_b, wt_c, w_out = late_weights(out_b)
    memn = _rmsnorm_fwd("memnorm", mem, mem_norm, MEM_LEN)
    kv_m = _mm_simple("mem_kv", memn, w_kv, NN, BF16, MEM_LEN, 512, D_MODEL)
    out_c, lse_c = _attn_fwd("mem_fwd", "mem", parts, kv_m, kv_m, 48, 0, 4)

    wts = (wt_a, wt_b, wt_c)
    out_a, u, z, y = _gate_fwd(o_grp, l_grp, out_b, out_c, parts, merge_bias, wts)
    y2 = _mm_simple("out_proj", y, w_out, NN, F32, 512, D_MODEL, D_MODEL)
    dout, dy2, err_sq, g_post = _post(y2, x, target, post_norm)
    loss = 0.5 * jnp.sum(err_sq) / D_MODEL

    dy = _mm_simple("out_proj_dx", dy2, w_out, NT, BF16, 512, D_MODEL, D_MODEL)
    g_w_out = _mm_simple("out_proj_dw", y, dy2, TN, BF16, D_MODEL, 512, 512)

    rr = _iota((512, 512), 0) // HEAD_DIM
    cc = _iota((512, 512), 1) // HEAD_DIM
    head_sum = (rr == cc).astype(F32)
    dlog, g_mb, dz, dg, do_grp, dp_grp, do_b, do_c = _gate_bwd(
        dy, z, parts, merge_bias, (out_a, out_b, out_c), o_grp, l_grp, wts, head_sum)
    g_wt = [_mm_simple("branch_dw_%d" % b, dz[b], u[b], TN, BF16, D_MODEL, 512, 512) for b in range(3)]

    dqkv = []
    for g, d in enumerate(DILATIONS):
        dq, dk, dv = _attn_bwd("dil_bwd_%d" % g, "dil", parts, parts, parts, 12 * g, 12 * g + 4, 12 * g + 8,
                               do_grp[g], l_grp[g], dp=dp_grp[g], d=d, tabs=tabs[g])
        dqkv += [dq, dk, dv]
    dq_b, dk_b, dv_b, dbias = _attn_bwd("na_bwd", "na", parts, parts, parts, 36, 40, 44, do_b, lse_b, o=out_b,
                                        bias=bias)
    g_rpb_t = _na_bias_bwd(dbias)
    g_rpb = g_rpb_t[:, :15, :31] + jnp.pad(g_rpb_t[:, :14, 64:95], ((0, 0), (1, 0), (0, 0)))
    dq_c, dk_m, dv_m = _attn_bwd("mem_bwd", "mem", parts, kv_m, kv_m, 48, 0, 4, do_c, lse_c, o=out_c)

    dkv = jnp.concatenate([dk_m, dv_m], axis=1).astype(BF16)
    g_w_kv = _mm_simple("mem_kv_dw", memn, dkv, TN, BF16, D_MODEL, 512, MEM_LEN)
    dmemn = _mm_simple("mem_kv_dx", dkv, w_kv, NT, F32, MEM_LEN, 512, D_MODEL)
    g_mem_norm = _memnorm_bwd(mem, dmemn)

    grads = dict(w_kv=g_w_kv, wt_a=g_wt[0], wt_b=g_wt[1], wt_c=g_wt[2], w_out=g_w_out, merge_bias=g_mb,
                 mem_norm=g_mem_norm, post_norm=g_post, na_rpb=g_rpb)
    dep = reduce_start(grads) if reduce_start is not None else None
    dparts = dqkv + [dq_b, dk_b, dv_b, dq_c] + list(dg) + list(dlog)
    grads["wt_in"] = _in_proj_dw(dparts, hst, dep)
    dep = reduce_start(grads) if reduce_start is not None else None
    dh = _in_proj_dh(dparts, wt_in, dep)
    grad_x, grads["pre_norm"] = _prenorm_bwd(x, pre_norm, dh, dout)
    return loss, grad_x, grads


ANY = pl.BlockSpec(memory_space=pl.ANY)


def _place():
    return lax.axis_index("x"), lax.axis_index("y"), lax.axis_index("c")


def _all_gather(shard):
    r = shard.shape[0]
    half = r // 2

    def body(src, out, send_sems, recv_sems, local_sem):
        x, y, c = _place()
        me, sib = (x, y, c), (x, y, 1 - c)
        xn, yn, dg = (1 - x, y, c), (x, 1 - y, c), (1 - x, 1 - y, c)

        def rows(dev, part=None):
            blk = out.at[4 * dev[0] + 2 * dev[1] + dev[2]]
            return blk if part is None else blk.at[pl.ds(part * half, half)]

        def copy(k, dev, part, to, own=False):
            return pltpu.make_async_remote_copy(
                src_ref=src if own else rows(dev, part), dst_ref=rows(dev, part),
                send_sem=send_sems.at[k], recv_sem=recv_sems.at[k], device_id=to, device_id_type=MESH_ID)

        def other(dev):
            return (dev[0], dev[1], 1 - dev[2])

        mine = pltpu.make_async_copy(src, rows(me), local_sem)
        mine.start()
        sent = [copy(0, me, None, sib, own=True), copy(1, me, None, xn, own=True), copy(2, me, None, yn, own=True)]
        for cp in sent:
            cp.start()
        copy(1, xn, None, me).wait_recv()
        sent += [copy(3, xn, 0, yn), copy(5, xn, None, sib)]
        sent[-2].start()
        sent[-1].start()
        copy(2, yn, None, me).wait_recv()
        sent += [copy(4, yn, 1, xn), copy(6, yn, None, sib)]
        sent[-2].start()
        sent[-1].start()
        copy(3, dg, 0, me).wait_recv()
        sent.append(copy(7, dg, 0, sib))
        sent[-1].start()
        copy(4, dg, 1, me).wait_recv()
        sent.append(copy(8, dg, 1, sib))
        sent[-1].start()
        copy(0, sib, None, me).wait_recv()
        copy(5, other(xn), None, me).wait_recv()
        copy(6, other(yn), None, me).wait_recv()
        copy(7, other(dg), 0, me).wait_recv()
        copy(8, other(dg), 1, me).wait_recv()
        for cp in sent:
            cp.wait_send()
        mine.wait()

    return pl.pallas_call(
        body, name="all_gather", in_specs=[ANY], out_specs=ANY,
        out_shape=jax.ShapeDtypeStruct((N_DEV,) + shard.shape, shard.dtype),
        scratch_shapes=[pltpu.SemaphoreType.DMA((9,)), pltpu.SemaphoreType.DMA((9,)), pltpu.SemaphoreType.DMA])(shard)


def _exchange_sibling(name, terms):
    nt = len(terms)

    def body(*refs):
        srcs, outs = refs[:nt], refs[nt:2 * nt]
        send_sems, recv_sems = refs[2 * nt:]
        x, y, c = _place()
        copies = []
        for q in range(4):
            for t in range(nt):
                copies.append(pltpu.make_async_remote_copy(
                    src_ref=srcs[t].at[2 * q + 1 - c], dst_ref=outs[t].at[q],
                    send_sem=send_sems.at[q * nt + t], recv_sem=recv_sems.at[q * nt + t],
                    device_id=(x, y, 1 - c), device_id_type=MESH_ID))
        for cp in copies:
            cp.start()
        for cp in copies:
            cp.wait()

    return pl.pallas_call(
        body, name=name, in_specs=[ANY] * nt, out_specs=[ANY] * nt,
        out_shape=[jax.ShapeDtypeStruct((4,) + s.shape[1:], s.dtype) for s in terms],
        scratch_shapes=[pltpu.SemaphoreType.DMA((4 * nt,)), pltpu.SemaphoreType.DMA((4 * nt,))])(*terms)


HBM = pl.BlockSpec(memory_space=pltpu.HBM)
SEM = pl.BlockSpec(memory_space=pltpu.SEMAPHORE)
DATAFLOW = pltpu.SideEffectType.DATAFLOW_SIDE_EFFECTING


def _split_copies(kind, srcs, lands, send_sems, recv_sems):
    nt = len(srcs)
    x, y, c = _place()
    copies = []
    if kind == "gather":
        me = 4 * x + 2 * y + c
        for mask in range(1, 8):
            fx, fy, fc = (mask >> 2) & 1, (mask >> 1) & 1, mask & 1
            to = (1 - x if fx else x, 1 - y if fy else y, 1 - c if fc else c)
            for t in range(nt):
                k = (mask - 1) * nt + t
                copies.append(pltpu.make_async_remote_copy(
                    src_ref=srcs[t], dst_ref=lands[t].at[me], send_sem=send_sems.at[k], recv_sem=recv_sems.at[k],
                    device_id=to, device_id_type=MESH_ID))
    else:
        for s, (tx, ty) in enumerate([(1 - x, y), (x, 1 - y), (1 - x, 1 - y)]):
            for t in range(nt):
                k = s * nt + t
                copies.append(pltpu.make_async_remote_copy(
                    src_ref=srcs[t].at[2 * tx + ty], dst_ref=lands[t].at[s], send_sem=send_sems.at[k],
                    recv_sem=recv_sems.at[k], device_id=(tx, ty, c), device_id_type=MESH_ID))
    return copies


def _split_count(kind, nt):
    return (7 if kind == "gather" else 3) * nt


def _exchange_start(name, kind, srcs, land_shapes, after=None):
    nt = len(srcs)
    n = _split_count(kind, nt)
    dep_specs, dep_args = _dep_operand(after)
    nd = len(dep_args)

    def body(*refs):
        src_refs, land_refs = refs[:nt], refs[nt:2 * nt]
        send_sems, recv_sems = refs[2 * nt + nd], refs[2 * nt + nd + 1]
        token = refs[-1]
        for cp in _split_copies(kind, src_refs, land_refs, send_sems, recv_sems):
            cp.start()
        token[...] = jnp.zeros_like(token)

    lands = [pltpu.with_memory_space_constraint(lax.empty(s.shape, s.dtype), pltpu.HBM) for s in land_shapes]
    res = pl.pallas_call(
        body, name=name,
        out_shape=(pltpu.SemaphoreType.DMA((n,)), pltpu.SemaphoreType.DMA((n,)),
                   *[pltpu.HBM(s.shape, s.dtype) for s in srcs], *[pltpu.HBM(s.shape, s.dtype) for s in land_shapes],
                   jax.ShapeDtypeStruct((8, 128), F32)),
        in_specs=[HBM] * (2 * nt) + dep_specs,
        out_specs=(SEM, SEM, *([HBM] * (2 * nt)), pl.BlockSpec(memory_space=pltpu.VMEM)),
        input_output_aliases={i: 2 + i for i in range(2 * nt)},
        compiler_params=pltpu.CompilerParams(has_side_effects=DATAFLOW))(
            *[pltpu.with_memory_space_constraint(s, pltpu.HBM) for s in srcs], *lands, *dep_args)
    return res[0], res[1], list(res[2:2 + nt]), list(res[2 + nt:2 + 2 * nt]), res[-1]


def _exchange_wait(name, kind, send_sems, recv_sems, srcs, lands, after):
    nt = len(srcs)

    def body(*refs):
        src_refs, land_refs = refs[:nt], refs[nt:2 * nt]
        s_sems, r_sems = refs[2 * nt], refs[2 * nt + 1]
        for cp in _split_copies(kind, src_refs, land_refs, s_sems, r_sems):
            cp.wait_send()
            cp.wait_recv()

    res = pl.pallas_call(
        body, name=name,
        out_shape=tuple(pltpu.HBM(s.shape, s.dtype) for s in list(srcs) + list(lands)),
        in_specs=[HBM] * (2 * nt) + [SEM, SEM, pl.BlockSpec(memory_space=pl.ANY)],
        out_specs=tuple([HBM] * (2 * nt)),
        input_output_aliases={i: i for i in range(2 * nt)},
        compiler_params=pltpu.CompilerParams(has_side_effects=DATAFLOW))(
            *srcs, *lands, send_sems, recv_sems, after)
    return list(res[:nt]), list(res[nt:])


def _add_sibling(name, term, recv, rows):
    _, r, w = term.shape
    cidx = lax.axis_index("c").astype(jnp.int32).reshape(1)

    def body(c_ref, a_ref, b_ref, o_ref):
        o_ref[...] = (a_ref[...].astype(F32) + b_ref[...].astype(F32)).astype(o_ref.dtype)

    grid_spec = pltpu.PrefetchScalarGridSpec(
        num_scalar_prefetch=1, grid=(4, r // rows),
        in_specs=[pl.BlockSpec((None, rows, w), lambda q, i, c_ref: (2 * q + c_ref[0], i, 0)),
                  pl.BlockSpec((None, rows, w), lambda q, i, c_ref: (q, i, 0))],
        out_specs=pl.BlockSpec((None, rows, w), lambda q, i, c_ref: (q, i, 0)))
    return pl.pallas_call(
        body, name=name, grid_spec=grid_spec, out_shape=jax.ShapeDtypeStruct((4, r, w), term.dtype),
        compiler_params=_params(("parallel", "parallel")))(cidx, term, recv)


def _add_sibling_small(name, terms, recvs):
    nt = len(terms)

    def body(*refs):
        c = lax.axis_index("c")
        for t_ref, r_ref, o_ref in zip(refs[:nt], refs[nt:2 * nt], refs[2 * nt:]):
            for q in range(4):
                o_ref[q] = (t_ref[2 * q + c].astype(F32) + r_ref[q].astype(F32)).astype(o_ref.dtype)

    return pl.pallas_call(
        body, name=name, out_shape=[jax.ShapeDtypeStruct((4,) + t.shape[1:], t.dtype) for t in terms],
        compiler_params=_params())(*terms, *recvs)


def _add_chips(name, sums, recv, rows):
    _, r, w = sums.shape
    qidx = (2 * lax.axis_index("x") + lax.axis_index("y")).astype(jnp.int32).reshape(1)

    def body(q_ref, a_ref, b_ref, o_ref):
        o_ref[...] = ((a_ref[...].astype(F32) + b_ref[0].astype(F32))
                      + (b_ref[1].astype(F32) + b_ref[2].astype(F32)))

    grid_spec = pltpu.PrefetchScalarGridSpec(
        num_scalar_prefetch=1, grid=(r // rows,),
        in_specs=[pl.BlockSpec((None, rows, w), lambda i, q_ref: (q_ref[0], i, 0)),
                  pl.BlockSpec((3, rows, w), lambda i, q_ref: (0, i, 0))],
        out_specs=pl.BlockSpec((rows, w), lambda i, q_ref: (i, 0)))
    return pl.pallas_call(
        body, name=name, grid_spec=grid_spec, out_shape=jax.ShapeDtypeStruct((r, w), F32),
        compiler_params=_params(("parallel",)))(qidx, sums, recv)


def _rs_rows(a):
    return SHARD_IN // 4 if a.shape[1] == SHARD_IN else a.shape[1]


def _reduce_scatter_start(tag, names, terms):
    recv1 = _exchange_sibling("exchange_sibling_" + tag, terms)
    if len(terms) == 1:
        sums = [_add_sibling("add_sibling_" + names[0], terms[0], recv1[0], _rs_rows(terms[0]))]
    else:
        sums = _add_sibling_small("add_sibling_" + tag, terms, recv1)
    lands =[jax.ShapeDtypeStruct((3,) + s.shape[1:], s.dtype) for s in sums]
    send_sems, recv_sems, sums, lands, token = _exchange_start("exchange_chips_start_" + tag, "chips", sums, lands)
    return (tag, names, send_sems, recv_sems, sums, lands), token


def _reduce_scatter_wait(state, after):
    tag, names, send_sems, recv_sems, sums, lands = state
    sums, recv2 = _exchange_wait("exchange_chips_wait_" + tag, "chips", send_sems, recv_sems, sums, lands, after)
    return names, sums, recv2


def _adamw(name, w, g, m, v):
    def body(w_ref, g_ref, m_ref, v_ref, d_ref, nm_ref, nv_ref):
        d_ref[...], nm_ref[...], nv_ref[...] = _adam_math(w_ref[...], g_ref[...], m_ref[...], v_ref[...])

    return pl.pallas_call(
        body, name=name, out_shape=[jax.ShapeDtypeStruct(w.shape, F32)] * 3, compiler_params=_params())(w, g, m, v)


def _adam_math(w, g, m, v):
    nm = ADAM_B1 * m + (1.0 - ADAM_B1) * g
    nv = ADAM_B2 * v + (1.0 - ADAM_B2) * (g * g)
    c1 = 1.0 - ADAM_B1 ** ADAM_STEP
    c2 = 1.0 - ADAM_B2 ** ADAM_STEP
    return -ADAM_LR * ((nm / c1) / (jnp.sqrt(nv / c2) + ADAM_EPS) + ADAM_WD * w), nm, nv


def _adamw_chips(name, sums, recv, w, m, v, transposed, rows=None, dep=None):
    r, c = w.shape
    rows = r if rows is None else rows
    qidx = (2 * lax.axis_index("x") + lax.axis_index("y")).astype(jnp.int32).reshape(1)
    dep_specs, dep_args = _dep_operand(dep)

    def body(q_ref, a_ref, b_ref, w_ref, m_ref, v_ref, *rest):
        g_ref, d_ref, nm_ref, nv_ref = rest[-4:]
        g = (a_ref[...].astype(F32) + b_ref[0].astype(F32)) + (b_ref[1].astype(F32) + b_ref[2].astype(F32))
        if transposed:
            g = g.T
        g_ref[...] = g
        d_ref[...], nm_ref[...], nv_ref[...] = _adam_math(w_ref[...], g, m_ref[...], v_ref[...])

    row = pl.BlockSpec((rows, c), lambda i, q_ref: (i, 0))
    if transposed:
        term_specs = [pl.BlockSpec((None, c, rows), lambda i, q_ref: (q_ref[0], 0, i)),
                      pl.BlockSpec((3, c, rows), lambda i, q_ref: (0, 0, i))]
    else:
        term_specs = [pl.BlockSpec((None, rows, c), lambda i, q_ref: (q_ref[0], i, 0)),
                      pl.BlockSpec((3, rows, c), lambda i, q_ref: (0, i, 0))]
    grid_spec = pltpu.PrefetchScalarGridSpec(
        num_scalar_prefetch=1, grid=(r // rows,), in_specs=term_specs + [row, row, row] + dep_specs,
        out_specs=[row] * 4)
    return pl.pallas_call(
        body, name=name, grid_spec=grid_spec, out_shape=[jax.ShapeDtypeStruct((r, c), F32)] * 4,
        compiler_params=_params(("parallel",)))(qidx, sums, recv, w, m, v, *dep_args)


def _sum_devices(gathered):
    def body(g_ref, o_ref):
        acc = g_ref[0]
        for j in range(1, N_DEV):
            acc = acc + g_ref[j]
        o_ref[...] = acc

    return pl.pallas_call(
        body, name="sum_devices", out_shape=jax.ShapeDtypeStruct(gathered.shape[1:], F32),
        compiler_params=_params())(gathered)


def _rows128(a, rows):
    flat = a.reshape(-1)
    return jnp.pad(flat, (0, rows * 128 - flat.shape[0])).reshape(rows, 128)


def kernel(x, mem, pre_norm, w_in, merge_bias, na_rpb, mem_norm, w_mem_kv, w_branch_a, w_branch_b, w_branch_c, w_out, post_norm, loss_target, m_pre_norm, m_w_in, m_merge_bias, m_na_rpb, m_mem_norm, m_w_mem_kv, m_w_branch_a, m_w_branch_b, m_w_branch_c, m_w_out, m_post_norm, v_pre_norm, v_w_in, v_merge_bias, v_na_rpb, v_mem_norm, v_w_mem_kv, v_w_branch_a, v_w_branch_b, v_w_branch_c, v_w_out, v_post_norm):
    wt_in_s = w_in[0].T.astype(BF16)
    rows_s = jnp.concatenate([w_mem_kv[0], w_out[0]], axis=0).astype(BF16)
    cols_s = jnp.concatenate([w_branch_a[0].T, w_branch_b[0].T, w_branch_c[0].T], axis=0).astype(BF16)
    mb_s = jnp.pad(merge_bias[0], ((0, 5), (0, 0)))
    wt_in = _all_gather(wt_in_s).reshape(N_IN, D_MODEL)

    late_own = [rows_s, cols_s, mb_s]
    late_lands = [jax.ShapeDtypeStruct((N_DEV,) + s.shape, s.dtype) for s in late_own]
    l_send, l_recv, late_own, late_lands, late_token = _exchange_start("gather_late_start", "gather", late_own,
                                                                       late_lands, after=wt_in)
    me = 4 * lax.axis_index("x") + 2 * lax.axis_index("y") + lax.axis_index("c")

    def late_weights(after):
        own, lands = _exchange_wait("gather_late_wait", "gather", l_send, l_recv, late_own, late_lands, after)
        g_rows, g_cols, g_mb = [lax.dynamic_update_slice(land, o[None], (me, 0, 0)) for land, o in zip(lands, own)]
        return (g_mb[:, :3].transpose(1, 0, 2).reshape(3, D_MODEL),
                g_rows[:, :128].reshape(D_MODEL, D_MODEL), g_cols[:, 0:128].reshape(D_MODEL, 512),
                g_cols[:, 128:256].reshape(D_MODEL, 512), g_cols[:, 256:384].reshape(D_MODEL, 512),
                g_rows[:, 128:].reshape(D_MODEL, D_MODEL))

    rs_state = []

    def reduce_start(grads):
        if "wt_in" in grads:
            state, token = _reduce_scatter_start("w_in", ["w_in"],
                                                 [grads["wt_in"].reshape(N_DEV, SHARD_IN, D_MODEL)])
        else:
            gmb_t = jnp.pad(grads["merge_bias"].reshape(3, N_DEV, 128).transpose(1, 0, 2), ((0, 0), (0, 5), (0, 0)))
            names = ["w_kv", "w_out", "a", "b", "c", "mb"]
            terms = [grads["w_kv"].reshape(N_DEV, 128, D_MODEL), grads["w_out"].reshape(N_DEV, 128, D_MODEL),
                     grads["wt_a"].reshape(N_DEV, 128, 512), grads["wt_b"].reshape(N_DEV, 128, 512),
                     grads["wt_c"].reshape(N_DEV, 128, 512), gmb_t]
            state, token = _reduce_scatter_start("rest", names, terms)
        rs_state.append(state)
        return token

    loss_term, grad_x, grads = _local_step(
        x[0], mem[0], loss_target[0], pre_norm, mem_norm, post_norm, na_rpb[0], wt_in, late_weights,
        dep_in=late_token, reduce_start=reduce_start)

    small = jnp.concatenate([_rows128(grads["pre_norm"], 8), _rows128(grads["mem_norm"], 8),
                             _rows128(grads["post_norm"], 8), _rows128(grads["na_rpb"], 32),
                             _rows128(loss_term, 8)], axis=0)
    s_send, s_recv, s_own, s_land, s_token = _exchange_start(
        "gather_small_start", "gather", [small], [jax.ShapeDtypeStruct((N_DEV,) + small.shape, F32)])
    grad = {}
    weights = {
        "pre_norm": (pre_norm, m_pre_norm, v_pre_norm), "w_in": (w_in, m_w_in, v_w_in),
        "merge_bias": (merge_bias, m_merge_bias, v_merge_bias), "na_rpb": (na_rpb, m_na_rpb, v_na_rpb),
        "mem_norm": (mem_norm, m_mem_norm, v_mem_norm), "w_mem_kv": (w_mem_kv, m_w_mem_kv, v_w_mem_kv),
        "w_branch_a": (w_branch_a, m_w_branch_a, v_w_branch_a), "w_branch_b": (w_branch_b, m_w_branch_b, v_w_branch_b),
        "w_branch_c": (w_branch_c, m_w_branch_c, v_w_branch_c), "w_out": (w_out, m_w_out, v_w_out),
        "post_norm": (post_norm, m_post_norm, v_post_norm)}
    order = ["pre_norm", "w_in", "merge_bias", "na_rpb", "mem_norm", "w_mem_kv", "w_branch_a", "w_branch_b",
             "w_branch_c", "w_out", "post_norm"]
    delta, new_m, new_v = {}, {}, {}

    def update(n):
        w, m, v = weights[n]
        shape = w.shape
        two_d = (-1, shape[-1])
        dl, nm, nv = _adamw("adamw_" + n, w.reshape(two_d), grad[n].reshape(two_d), m.reshape(two_d),
                            v.reshape(two_d))
        delta[n], new_m[n], new_v[n] = dl.reshape(shape), nm.reshape(shape), nv.reshape(shape)

    def update_sharded(n, sums, recv, transposed, rows=None, dep=None):
        w, m, v = weights[n]
        g, dl, nm, nv = _adamw_chips("adamw_" + n, sums, recv, w[0], m[0], v[0], transposed, rows, dep)
        grad[n], delta[n], new_m[n], new_v[n] = g[None], dl[None], nm[None], nv[None]
        return dl

    _, sums, recv2 = _reduce_scatter_wait(rs_state[0], s_token)
    dep = None
    for i, (n, transposed) in enumerate((("w_mem_kv", False), ("w_out", False), ("w_branch_a", True),
                                         ("w_branch_b", True), ("w_branch_c", True))):
        dep = update_sharded(n, sums[i], recv2[i], transposed, dep=dep)
    grad["merge_bias"] = _add_chips("add_chips_mb", sums[5], recv2[5], 8)[:3][None]
    update("merge_bias")
    s_own, s_land = _exchange_wait("gather_small_wait", "gather", s_send, s_recv, s_own, s_land, dep)
    total = _sum_devices(lax.dynamic_update_slice(s_land[0], s_own[0][None], (me, 0, 0)))
    loss = total[56, 0]
    grad.update({"pre_norm": total[0:8].reshape(1, D_MODEL), "mem_norm": total[8:16].reshape(1, D_MODEL),
                 "post_norm": total[16:24].reshape(1, D_MODEL),
                 "na_rpb": total[24:56].reshape(-1)[:8 * 15 * 31].reshape(1, 8, 15, 31)})
    for n in ("pre_norm", "na_rpb", "mem_norm", "post_norm"):
        update(n)
    _, sums_in, recv_in = _reduce_scatter_wait(rs_state[1], delta["post_norm"])
    update_sharded("w_in", sums_in[0], recv_in[0], True, 256)

    return (loss, grad_x[None], *[grad[n] for n in order], *[delta[n] for n in order],
            *[new_m[n] for n in order], *[new_v[n] for n in order])
```

```python
import functools

import numpy as np
import jax
import jax.numpy as jnp
from jax import lax
from jax.experimental import pallas as pl
from jax.experimental.pallas import tpu as pltpu

F32 = jnp.float32
BF16 = jnp.bfloat16

SEQ = 2048
D_MODEL = 1024
N_IN = 11264
N_DEV = 8
SHARD_IN = N_IN // N_DEV
HEAD_DIM = 64
GRID_W = 64
NA_ROWS = 8
MEM_LEN = 256
DILATIONS = (1, 4, 16)
REACH = 64
ROPE_THETA = 500000.0
ROPE_DIM = 16
EPS = 1e-6
NEG = -1e30
ADAM_LR = 0.001
ADAM_B1 = 0.9
ADAM_B2 = 0.999
ADAM_EPS = 1e-08
ADAM_WD = 0.01
ADAM_STEP = 10

VMEM_LIMIT_BYTES = 56 * 1024 * 1024
MESH_ID = pl.DeviceIdType.MESH

NN = (((1,), (0,)), ((), ()))
NT = (((1,), (1,)), ((), ()))
TN = (((0,), (0,)), ((), ()))


def _params(sem=None):
    return pltpu.CompilerParams(dimension_semantics=sem, vmem_limit_bytes=VMEM_LIMIT_BYTES)


def _iota(shape, dim):
    return lax.broadcasted_iota(jnp.int32, shape, dim)


def _sigmoid(x):
    return 1.0 / (1.0 + jnp.exp(-x))


def _rope_tables():
    half = ROPE_DIM // 2
    inv = (ROPE_THETA ** (-np.arange(half, dtype=np.float64) * 2.0 / ROPE_DIM)).astype(np.float32)
    pos = np.arange(SEQ, dtype=np.float32)
    ang = pos[:, None] * inv[None, :]
    cos, sin = np.cos(ang), np.sin(ang)
    zeros = np.zeros_like(cos)
    rest = HEAD_DIM - ROPE_DIM
    c64 = np.concatenate([cos, cos, np.ones((SEQ, rest), np.float32)], axis=1)
    s1 = np.concatenate([zeros, sin, np.zeros((SEQ, rest), np.float32)], axis=1)
    s2 = np.concatenate([-sin, zeros, np.zeros((SEQ, rest), np.float32)], axis=1)

    def fold(t, d):
        return t.reshape(SEQ // d, d, t.shape[1]).transpose(1, 0, 2).reshape(SEQ, t.shape[1])

    tabs = [np.stack([np.tile(fold(t, d), (1, 2)) for t in (c64, s1, s2)], axis=0) for d in DILATIONS]
    return jnp.asarray(np.stack(tabs, axis=0), dtype=F32)


def _rope(a, c, s1, s2):
    return a * c + pltpu.roll(a, 8, 1) * s1 + pltpu.roll(a, 120, 1) * s2


def _rope_t(a, c, s1, s2):
    return a * c + pltpu.roll(a * s1, 120, 1) + pltpu.roll(a * s2, 8, 1)


def _perm_of_block(j):
    return jnp.where(j < 3, 0, jnp.where(j < 6, 1, jnp.where(j < 9, 2, 0)))


def _mm(name, a, b, out_shape, out_dtype, grid, a_spec, b_spec, o_spec, acc_shape, dims, k_axis, nk):
    def body(a_ref, b_ref, o_ref, acc_ref):
        k = pl.program_id(k_axis)

        @pl.when(k == 0)
        def _():
            acc_ref[...] = jnp.zeros(acc_shape, F32)

        acc_ref[...] += lax.dot_general(a_ref[...], b_ref[...], dims, preferred_element_type=F32)

        @pl.when(k == nk - 1)
        def _():
            o_ref[...] = acc_ref[...].astype(out_dtype)

    sem = tuple("arbitrary" if ax == k_axis else "parallel" for ax in range(len(grid)))
    return pl.pallas_call(
        body, name=name, grid=grid, in_specs=[a_spec, b_spec], out_specs=o_spec,
        out_shape=jax.ShapeDtypeStruct(out_shape, out_dtype),
        scratch_shapes=[pltpu.VMEM(acc_shape, F32)], compiler_params=_params(sem))(a, b)


def _mm_simple(name, a, b, dims, out_dtype, tm, tn, tk):
    if dims is NN:
        m, kk = a.shape
        n = b.shape[1]
        a_spec = pl.BlockSpec((tm, tk), lambda i, j, k: (i, k))
        b_spec = pl.BlockSpec((tk, tn), lambda i, j, k: (k, j))
    elif dims is NT:
        m, kk = a.shape
        n = b.shape[0]
        a_spec = pl.BlockSpec((tm, tk), lambda i, j, k: (i, k))
        b_spec = pl.BlockSpec((tn, tk), lambda i, j, k: (j, k))
    else:
        kk, m = a.shape
        n = b.shape[1]
        a_spec = pl.BlockSpec((tk, tm), lambda i, j, k: (k, i))
        b_spec = pl.BlockSpec((tk, tn), lambda i, j, k: (k, j))
    grid = (m // tm, n // tn, kk // tk)
    o_spec = pl.BlockSpec((tm, tn), lambda i, j, k: (i, j))
    return _mm(name, a, b, (m, n), out_dtype, grid, a_spec, b_spec, o_spec, (tm, tn), dims, 2, kk // tk)


def _rmsnorm_fwd(name, x, gain, rows):
    n, d = x.shape

    def body(x_ref, g_ref, o_ref):
        xv = x_ref[...]
        rstd = lax.rsqrt(jnp.mean(xv * xv, axis=1, keepdims=True) + EPS)
        o_ref[...] = (xv * rstd * g_ref[...]).astype(BF16)

    return pl.pallas_call(
        body, name=name, grid=(n // rows,),
        in_specs=[pl.BlockSpec((rows, d), lambda i: (i, 0)), pl.BlockSpec((1, d), lambda i: (0, 0))],
        out_specs=pl.BlockSpec((rows, d), lambda i: (i, 0)),
        out_shape=jax.ShapeDtypeStruct((n, d), BF16), compiler_params=_params(("parallel",)))(x, gain)


def _folded_rows(first, rows, d):
    if d == 1:
        return pl.ds(pl.multiple_of(first, rows), rows)
    mlen = SEQ // d
    return pl.ds((first % mlen) * d + first // mlen, rows, stride=d)


def _prenorm_fold(x, gain):
    rows = 128

    nchunk = D_MODEL // 128

    def body(*refs):
        x_refs, g_ref, hs_ref, hst_ref = refs[:nchunk], refs[nchunk], refs[nchunk + 1], refs[nchunk + 2]
        first = pl.program_id(0) * rows
        for p, d in enumerate(DILATIONS):
            idx = _folded_rows(first, rows, d)
            xv = jnp.concatenate([r[idx, :] for r in x_refs], axis=1)
            rstd = lax.rsqrt(jnp.mean(xv * xv, axis=1, keepdims=True) + EPS)
            h = xv * rstd * g_ref[...]
            hs_ref[p] = h.astype(BF16)
            hst_ref[p] = h.T.astype(BF16)

    x_specs = [pl.BlockSpec((SEQ, 128), functools.partial(lambda c, i: (0, c), c)) for c in range(nchunk)]
    return pl.pallas_call(
        body, name="prenorm", grid=(SEQ // rows,),
        in_specs=x_specs + [pl.BlockSpec((1, D_MODEL), lambda i: (0, 0))],
        out_specs=[pl.BlockSpec((3, rows, D_MODEL), lambda i: (0, i, 0)),
                   pl.BlockSpec((3, D_MODEL, rows), lambda i: (0, 0, i))],
        out_shape=[jax.ShapeDtypeStruct((3, SEQ, D_MODEL), BF16), jax.ShapeDtypeStruct((3, D_MODEL, SEQ), BF16)],
        compiler_params=_params(("parallel",)))(*([x] * nchunk), gain)


def _prenorm_bwd(x, gain, dh, dout):
    rows = 256

    def body(x_ref, g_ref, a_ref, do_ref, dx_ref, gg_ref):
        xv = x_ref[...]
        rstd = lax.rsqrt(jnp.mean(xv * xv, axis=1, keepdims=True) + EPS)
        xn = xv * rstd
        dh = jnp.concatenate([a_ref[c] for c in range(D_MODEL // 128)], axis=1)
        gdh = dh * g_ref[...]
        dx_ref[...] = rstd * (gdh - xn * jnp.mean(gdh * xn, axis=1, keepdims=True)) + do_ref[...]

        @pl.when(pl.program_id(0) == 0)
        def _():
            gg_ref[...] = jnp.zeros((1, D_MODEL), F32)

        gg_ref[...] += jnp.sum(dh * xn, axis=0, keepdims=True)

    row = pl.BlockSpec((rows, D_MODEL), lambda i: (i, 0))
    vec = pl.BlockSpec((1, D_MODEL), lambda i: (0, 0))
    return pl.pallas_call(
        body, name="prenorm_bwd", grid=(SEQ // rows,),
        in_specs=[row, vec, pl.BlockSpec((D_MODEL // 128, rows, 128), lambda i: (0, i, 0)), row], out_specs=[row, vec],
        out_shape=[jax.ShapeDtypeStruct((SEQ, D_MODEL), F32), jax.ShapeDtypeStruct((1, D_MODEL), F32)],
        compiler_params=_params(("arbitrary",)))(x, gain, dh, dout)


def _memnorm_bwd(mem, dmemn):
    def body(m_ref, d_ref, gg_ref):
        mv = m_ref[...]
        rstd = lax.rsqrt(jnp.mean(mv * mv, axis=1, keepdims=True) + EPS)
        gg_ref[...] = jnp.sum(d_ref[...] * mv * rstd, axis=0, keepdims=True)

    return pl.pallas_call(
        body, name="memnorm_bwd", out_shape=jax.ShapeDtypeStruct((1, D_MODEL), F32),
        compiler_params=_params())(mem, dmemn)


def _dep_operand(dep):
    return ([], []) if dep is None else ([pl.BlockSpec(memory_space=pl.ANY)], [dep])


def _in_proj(hs, wt, tabs, dep=None):
    tm, tn = 512, 512
    dep_specs, dep_args = _dep_operand(dep)

    def body(h_ref, w_ref, t_ref, *rest):
        o_ref = rest[-1]
        j = pl.program_id(0)
        is_rope = jnp.logical_and(j < 9, j % 3 != 2)
        row_slices = [slice(r * tm, (r + 1) * tm) for r in range(SEQ // tm)]

        def product(rs):
            return lax.dot_general(h_ref[rs, :], w_ref[...], NT, preferred_element_type=F32)

        @pl.when(is_rope)
        def _():
            for rs in row_slices:
                acc = product(rs)
                c, s1, s2 = t_ref[0, rs, :], t_ref[1, rs, :], t_ref[2, rs, :]
                for q in range(tn // 128):
                    a = acc[:, q * 128:(q + 1) * 128]
                    o_ref[rs, q * 128:(q + 1) * 128] = _rope(a, c, s1, s2).astype(BF16)

        @pl.when(jnp.logical_not(is_rope))
        def _():
            for rs in row_slices:
                o_ref[rs, :] = product(rs).astype(BF16)

    return pl.pallas_call(
        body, name="in_proj", grid=(N_IN // tn,),
        in_specs=[pl.BlockSpec((None, SEQ, D_MODEL), lambda j: (_perm_of_block(j), 0, 0)),
                  pl.BlockSpec((tn, D_MODEL), lambda j: (j, 0)),
                  pl.BlockSpec((None, 3, SEQ, 128), lambda j: (_perm_of_block(j), 0, 0, 0))] + dep_specs,
        out_specs=pl.BlockSpec((SEQ, tn), lambda j: (0, j)),
        out_shape=jax.ShapeDtypeStruct((SEQ, N_IN), BF16),
        compiler_params=_params(("parallel",)))(hs, wt, tabs, *dep_args)


def _piece_blocks(pieces):
    return [(a, h * 512) for a, p in enumerate(pieces) for h in range(p.shape[1] // 512)]


def _block_fetch(piece_refs, blocks, buf, sem):
    def start(block, slot):
        for b, (a, col) in enumerate(blocks):
            @pl.when(block == b)
            def _():
                pltpu.make_async_copy(piece_refs[a].at[:, pl.ds(col, 512)], buf.at[slot], sem.at[slot]).start()

    def wait(slot):
        pltpu.make_async_copy(piece_refs[0].at[:, pl.ds(0, 512)], buf.at[slot], sem.at[slot]).wait()

    return start, wait


def _in_proj_dw(pieces, hst, dep=None):
    tn = 512
    blocks = _piece_blocks(pieces)
    nblk = len(blocks)
    npc = len(pieces)
    dep_specs, dep_args = _dep_operand(dep)

    def body(h_ref, *rest):
        piece_refs = rest[:npc]
        o_ref, buf, sem = rest[-3:]
        j = pl.program_id(0)
        slot = j % 2
        start, wait = _block_fetch(piece_refs, blocks, buf, sem)

        @pl.when(j == 0)
        def _():
            start(j, slot)

        wait(slot)

        @pl.when(j + 1 < nblk)
        def _():
            start(j + 1, 1 - slot)

        acc = jnp.dot(h_ref[...], buf[slot], preferred_element_type=F32)
        o_ref[...] = acc.T.astype(BF16)

    return pl.pallas_call(
        body, name="in_proj_dw", grid=(nblk,),
        in_specs=[pl.BlockSpec((None, D_MODEL, SEQ), lambda j: (_perm_of_block(j), 0, 0))] + [ANY] * npc + dep_specs,
        out_specs=pl.BlockSpec((tn, D_MODEL), lambda j: (j, 0)),
        out_shape=jax.ShapeDtypeStruct((N_IN, D_MODEL), BF16),
        scratch_shapes=[pltpu.VMEM((2, SEQ, tn), BF16), pltpu.SemaphoreType.DMA((2,))],
        compiler_params=_params(("arbitrary",)))(hst, *pieces, *dep_args)


def _in_proj_dh(pieces, wt, dep=None):
    tk = 512
    blocks = _piece_blocks(pieces)
    nblk = len(blocks)
    npc = len(pieces)
    nchunk = D_MODEL // 128

    def col(s):
        return jnp.where(s < 3, s, jnp.where(s < 16, s + 6, s - 13))

    dep_specs, dep_args = _dep_operand(dep)

    def body(w_ref, *rest):
        piece_refs = rest[:npc]
        o_ref, acc_ref, buf, sem = rest[-4:]
        s = pl.program_id(0)
        slot = s % 2
        start, wait = _block_fetch(piece_refs, blocks, buf, sem)

        @pl.when(s == 0)
        def _():
            start(col(s), slot)

        wait(slot)

        @pl.when(s + 1 < nblk)
        def _():
            start(col(s + 1), 1 - slot)

        row_slices = [slice(r * 512, (r + 1) * 512) for r in range(SEQ // 512)]

        def product(rs):
            return jnp.dot(buf[slot, rs, :], w_ref[...], preferred_element_type=F32)

        def accumulate(cond, to_out, init):
            @pl.when(cond)
            def _():
                for rs in row_slices:
                    prod = product(rs)
                    if not to_out:
                        if init:
                            acc_ref[rs, :] = prod
                        else:
                            acc_ref[rs, :] += prod
                        continue
                    for c in range(nchunk):
                        if init:
                            o_ref[c, rs, :] = prod[:, c * 128:(c + 1) * 128]
                        else:
                            o_ref[c, rs, :] += prod[:, c * 128:(c + 1) * 128]

        accumulate(s == 0, True, True)
        accumulate(jnp.logical_and(s > 0, s < 16), True, False)
        accumulate(jnp.logical_or(s == 16, s == 19), False, True)
        accumulate(jnp.logical_and(s > 16, s != 19), False, False)
        for last, d in ((18, 4), (21, 16)):
            @pl.when(s == last)
            def _():
                mlen = SEQ // d
                for r in range(d):
                    for c in range(nchunk):
                        o_ref[c, pl.ds(r, mlen, stride=d), :] += acc_ref[r * mlen:(r + 1) * mlen,
                                                                         c * 128:(c + 1) * 128]

    return pl.pallas_call(
        body, name="in_proj_dh", grid=(nblk,),
        in_specs=[pl.BlockSpec((tk, D_MODEL), lambda s: (col(s), 0))] + [ANY] * npc + dep_specs,
        out_specs=pl.BlockSpec((nchunk, SEQ, 128), lambda s: (0, 0, 0)),
        out_shape=jax.ShapeDtypeStruct((nchunk, SEQ, 128), F32),
        scratch_shapes=[pltpu.VMEM((SEQ, D_MODEL), F32), pltpu.VMEM((2, SEQ, tk), BF16),
                        pltpu.SemaphoreType.DMA((2,))],
        compiler_params=_params(("arbitrary",)))(wt, *pieces, *dep_args)


def _head_lanes(lanes, hh):
    return lanes >= 64 if hh == 1 else lanes < 64


def _head_rows(x, lanes, hh, pair):
    if not pair:
        return jnp.max(x, axis=1, keepdims=True)
    return jnp.max(jnp.where(_head_lanes(lanes, hh), x, -jnp.inf), axis=1, keepdims=True)


def _mask_head(x, lanes, hh, pair, scale=1.0):
    if not pair:
        return x
    xf = x.astype(F32) if scale == 1.0 else x.astype(F32) * scale
    return jnp.where(_head_lanes(lanes, hh), xf, 0.0).astype(BF16)


def _window(mode, qi, tq, mlen, tk):
    if mode == "dil":
        q0 = qi * tq
        seg = (q0 // mlen) * mlen
        ks = jnp.clip(q0 - REACH, seg, seg + mlen - tk)
        return pl.multiple_of(ks, 64)
    if mode == "na":
        r_start = jnp.clip(qi - NA_ROWS // 2, 0, SEQ // GRID_W - NA_ROWS)
        return pl.multiple_of(r_start * GRID_W, 64)
    return 0


def _band_mask(qi, tq, tk, ks):
    qpos = qi * tq + _iota((tq, tk), 0)
    kpos = ks + _iota((tq, tk), 1)
    return jnp.where(jnp.abs(qpos - kpos) <= REACH, 0.0, NEG).astype(F32)


def _stack_heads(x, lanes, pair, scale=1.0):
    if not pair:
        return x
    return jnp.concatenate([_mask_head(x, lanes, hh, pair, scale) for hh in range(2)], axis=0)


def _stack_rows(x, lanes, pair):
    if not pair:
        return _head_rows(x, lanes, 0, pair)
    return jnp.concatenate([_head_rows(x, lanes, hh, pair) for hh in range(2)], axis=0)


def _unstack_heads(x, lanes, pair, tq):
    if not pair:
        return x
    return jnp.where(lanes < 64, x[:tq], x[tq:])


def _scores(mode, qst, k, sscale, band, qi, bias_ref, pair):
    s = lax.dot_general(qst, k, NT, preferred_element_type=F32)
    if sscale != 1.0:
        s = s * sscale
    if mode == "dil":
        s = s + jnp.concatenate([band, band], axis=0)
    elif mode == "na":
        off = qi - jnp.clip(qi - NA_ROWS // 2, 0, SEQ // GRID_W - NA_ROWS)
        s = s + jnp.concatenate([bias_ref[0, off], bias_ref[1, off]], axis=0)
    return s


def _attn_cfg(mode, d):
    if mode == "dil":
        mlen = SEQ // d
        return dict(pair=True, tq=128, tk=min(256, mlen), mlen=mlen, lk=SEQ, scale=HEAD_DIM ** -0.5, units=4,
                    nsub=ATTN_SUBTILES)
    if mode == "na":
        return dict(pair=True, tq=GRID_W, tk=NA_ROWS * GRID_W, mlen=SEQ, lk=SEQ, scale=HEAD_DIM ** -0.5, units=4,
                    nsub=ATTN_SUBTILES)
    return dict(pair=False, tq=128, tk=MEM_LEN, mlen=SEQ, lk=MEM_LEN, scale=128 ** -0.5, units=4,
                nsub=ATTN_SUBTILES)


ATTN_SUBTILES = 8


def _attn_fwd(name, mode, q_arr, k_arr, v_arr, qcol, kcol, vcol, d=1, bias=None):
    cfg = _attn_cfg(mode, d)
    pair, tq, tk, mlen, lk, scale = cfg["pair"], cfg["tq"], cfg["tk"], cfg["mlen"], cfg["lk"], cfg["scale"]
    qscale, sscale = (scale, 1.0) if pair else (1.0, scale)
    nsub = cfg["nsub"]
    rows = nsub * tq

    def body(*refs):
        if mode == "na":
            q_ref, k_ref, v_ref, bias_ref, o_ref, l_ref = refs
        else:
            q_ref, k_ref, v_ref, o_ref, l_ref = refs
            bias_ref = None
        lanes = _iota((tq, 128), 1)
        qis = [pl.program_id(1) * nsub + sub for sub in range(nsub)]
        kss = [_window(mode, qi, tq, mlen, tk) for qi in qis]
        vs = [v_ref[pl.ds(ks, tk), :] for ks in kss]
        bands = [_band_mask(qi, tq, tk, ks) if mode == "dil" else None for qi, ks in zip(qis, kss)]
        ss = []
        for sub in range(nsub):
            qst = _stack_heads(q_ref[sub * tq:(sub + 1) * tq, :], lanes, pair, qscale)
            k = k_ref[pl.ds(kss[sub], tk), :]
            ss.append(_scores(mode, qst, k, sscale, bands[sub], qis[sub], bias_ref, pair))
        ms = [jnp.max(s_, axis=1, keepdims=True) for s_ in ss]
        ps = [jnp.exp(s_ - m) for s_, m in zip(ss, ms)]
        ls = [jnp.sum(p, axis=1, keepdims=True) for p in ps]
        os_ = [jnp.dot(p.astype(BF16), v, preferred_element_type=F32) for p, v in zip(ps, vs)]
        for sub in range(nsub):
            out = _unstack_heads(os_[sub] / ls[sub], lanes, pair, tq)
            lse = ms[sub] + jnp.log(ls[sub])
            lse = _unstack_heads(jnp.broadcast_to(lse, (lse.shape[0], 128)), lanes, pair, tq)
            dst = _folded_rows(qis[sub] * tq, tq, d) if mode == "dil" else slice(sub * tq, (sub + 1) * tq)
            o_ref[dst, :] = out
            l_ref[dst, :] = lse

    in_specs = [pl.BlockSpec((rows, 128), lambda u, i: (i, qcol + u)),
                pl.BlockSpec((lk, 128), lambda u, i: (0, kcol + u)),
                pl.BlockSpec((lk, 128), lambda u, i: (0, vcol + u))]
    args = [q_arr, k_arr, v_arr]
    if mode == "na":
        in_specs.append(pl.BlockSpec((2, NA_ROWS, GRID_W, NA_ROWS * GRID_W), lambda u, i: (u, 0, 0, 0)))
        args.append(bias)
    if mode == "dil":
        out_spec = pl.BlockSpec((SEQ, 128), lambda u, i: (0, u))
    else:
        out_spec = pl.BlockSpec((rows, 128), lambda u, i: (i, u))
    return pl.pallas_call(
        body, name=name, grid=(cfg["units"], SEQ // rows), in_specs=in_specs, out_specs=[out_spec, out_spec],
        out_shape=[jax.ShapeDtypeStruct((SEQ, 512), F32), jax.ShapeDtypeStruct((SEQ, 512), F32)],
        compiler_params=_params(("parallel", "arbitrary")))(*args)


def _attn_bwd(name, mode, q_arr, k_arr, v_arr, qcol, kcol, vcol, do, lse, dp=None, o=None, d=1, bias=None,
              tabs=None):
    cfg = _attn_cfg(mode, d)
    pair, tq, tk, mlen, lk, scale = cfg["pair"], cfg["tq"], cfg["tk"], cfg["mlen"], cfg["lk"], cfg["scale"]
    qscale, sscale = (scale, 1.0) if pair else (1.0, scale)
    nsub = cfg["nsub"]
    rows = nsub * tq
    nq = SEQ // rows
    kv_dtype = F32 if mode == "mem" else BF16

    def body(*refs):
        refs = list(refs)
        q_ref, k_ref, v_ref, do_ref, l_ref = refs[:5]
        rest = refs[5:]
        bias_ref = tq_ref = tk_ref = db_ref = None
        if mode == "dil":
            dp_ref, tq_ref, tk_ref, dq_ref, dk_ref, dv_ref, dk_acc, dv_acc = rest
        elif mode == "na":
            o_ref, bias_ref, dq_ref, dk_ref, dv_ref, db_ref, dk_acc, dv_acc = rest
        else:
            o_ref, dq_ref, dk_ref, dv_ref, dk_acc, dv_acc = rest
        step = pl.program_id(1)

        @pl.when(step == 0)
        def _():
            dk_acc[...] = jnp.zeros((lk, 128), F32)
            dv_acc[...] = jnp.zeros((lk, 128), F32)
            if mode == "na":
                db_ref[...] = jnp.zeros(db_ref.shape, F32)

        lanes = _iota((tq, 128), 1)
        qis = [step * nsub + sub for sub in range(nsub)]
        sls = [slice(sub * tq, (sub + 1) * tq) for sub in range(nsub)]
        kss = [_window(mode, qi, tq, mlen, tk) for qi in qis]
        ks_ = [k_ref[pl.ds(ks, tk), :] for ks in kss]
        vs = [v_ref[pl.ds(ks, tk), :] for ks in kss]
        qsts, dosts, lses, dphs = [], [], [], []
        for sub in range(nsub):
            if mode == "dil":
                src = _folded_rows(qis[sub] * tq, tq, d)
                dov = do_ref[src, :].astype(BF16)
                lsev = l_ref[src, :]
                dphs.append(_stack_rows(dp_ref[src, :], lanes, pair))
            else:
                dov = do_ref[sls[sub], :]
                lsev = l_ref[sls[sub], :]
                dpv = dov.astype(F32) * o_ref[sls[sub], :]
                if pair:
                    dphs.append(jnp.concatenate(
                        [jnp.sum(jnp.where(_head_lanes(lanes, hh), dpv, 0.0), axis=1, keepdims=True)
                         for hh in range(2)], axis=0))
                else:
                    dphs.append(jnp.sum(dpv, axis=1, keepdims=True))
            qsts.append(_stack_heads(q_ref[sls[sub], :], lanes, pair, qscale))
            dosts.append(_stack_heads(dov, lanes, pair))
            lses.append(_stack_rows(lsev, lanes, pair))
        bands = [_band_mask(qi, tq, tk, ks) if mode == "dil" else None for qi, ks in zip(qis, kss)]
        ss = [_scores(mode, qsts[sub], ks_[sub], sscale, bands[sub], qis[sub], bias_ref, pair) for sub in range(nsub)]
        dpms = [lax.dot_general(dosts[sub], vs[sub], NT, preferred_element_type=F32) for sub in range(nsub)]
        ps = [jnp.exp(s_ - lse) for s_, lse in zip(ss, lses)]
        dss = [p * (dpm - dph) for p, dpm, dph in zip(ps, dpms, dphs)]
        if mode == "na":
            for sub, ds in enumerate(dss):
                off = qis[sub] - jnp.clip(qis[sub] - NA_ROWS // 2, 0, SEQ // GRID_W - NA_ROWS)
                db_ref[0, off] += ds[:tq]
                db_ref[1, off] += ds[tq:]
        dsbs = [ds.astype(BF16) for ds in dss]
        dvs = [lax.dot_general(p.astype(BF16), dosts[sub], TN, preferred_element_type=F32)
               for sub, p in enumerate(ps)]
        dqs = [jnp.dot(dsb, ks_[sub], preferred_element_type=F32) * scale for sub, dsb in enumerate(dsbs)]
        dks = [lax.dot_general(dsb, qsts[sub], TN, preferred_element_type=F32) for sub, dsb in enumerate(dsbs)]
        for sub in range(nsub):
            sl = sls[sub]
            dq = _unstack_heads(dqs[sub], lanes, pair, tq)
            if mode == "dil":
                dq = _rope_t(dq, tq_ref[0, sl, :], tq_ref[1, sl, :], tq_ref[2, sl, :])
            dq_ref[sl, :] = dq.astype(BF16)
            dk_acc[pl.ds(kss[sub], tk), :] += dks[sub] if pair else dks[sub] * scale
            dv_acc[pl.ds(kss[sub], tk), :] += dvs[sub]

        @pl.when(step == nq - 1)
        def _():
            dkv = dk_acc[...]
            if mode == "dil":
                dkv = _rope_t(dkv, tk_ref[0], tk_ref[1], tk_ref[2])
            dk_ref[...] = dkv.astype(kv_dtype)
            dv_ref[...] = dv_acc[...].astype(kv_dtype)

    q_spec = pl.BlockSpec((rows, 128), lambda u, i: (i, qcol + u))
    row_spec = pl.BlockSpec((rows, 128), lambda u, i: (i, u))
    kv_out = pl.BlockSpec((lk, 128), lambda u, i: (0, u))
    whole = pl.BlockSpec((SEQ, 128), lambda u, i: (0, u))
    nat_spec = whole if mode == "dil" else row_spec
    in_specs = [q_spec,
                pl.BlockSpec((lk, 128), lambda u, i: (0, kcol + u)),
                pl.BlockSpec((lk, 128), lambda u, i: (0, vcol + u)),
                nat_spec, nat_spec]
    args = [q_arr, k_arr, v_arr, do, lse]
    out_specs = [row_spec, kv_out, kv_out]
    out_shape = [jax.ShapeDtypeStruct((SEQ, 512), BF16), jax.ShapeDtypeStruct((lk, 512), kv_dtype),
                 jax.ShapeDtypeStruct((lk, 512), kv_dtype)]
    if mode == "dil":
        in_specs += [whole, pl.BlockSpec((3, rows, 128), lambda u, i: (0, i, 0)),
                     pl.BlockSpec((3, SEQ, 128), lambda u, i: (0, 0, 0))]
        args += [dp, tabs, tabs]
    elif mode == "na":
        b_spec = pl.BlockSpec((2, NA_ROWS, GRID_W, NA_ROWS * GRID_W), lambda u, i: (u, 0, 0, 0))
        in_specs += [row_spec, b_spec]
        args += [o, bias]
        out_specs.append(b_spec)
        out_shape.append(jax.ShapeDtypeStruct((8, NA_ROWS, GRID_W, NA_ROWS * GRID_W), F32))
    else:
        in_specs.append(row_spec)
        args.append(o)
    return pl.pallas_call(
        body, name=name, grid=(cfg["units"], nq), in_specs=in_specs, out_specs=out_specs, out_shape=out_shape,
        scratch_shapes=[pltpu.VMEM((lk, 128), F32), pltpu.VMEM((lk, 128), F32)],
        compiler_params=_params(("parallel", "arbitrary")))(*args)


def _na_geometry():
    qc = _iota((GRID_W, 128), 0)
    lane = _iota((GRID_W, 128), 1)
    kc = lane & 63
    c_start = jnp.clip(qc - 8, 0, GRID_W - 16)
    valid = jnp.logical_and(kc >= c_start, kc < c_start + 16)
    return lane, valid


def _na_bias(rpb_rows):
    def body(r_ref, o_ref, t_ref):
        lane, valid = _na_geometry()
        for dd in range(14):
            row_a = jnp.broadcast_to(r_ref[dd:dd + 1, :], (GRID_W, 128))
            row_b = jnp.broadcast_to(r_ref[dd + 1:dd + 2, :], (GRID_W, 128))
            both = jnp.where(lane < 64, row_a, pltpu.roll(row_b, 64, 1))
            t = pltpu.roll(both, 128 - 15, 1, stride=1, stride_axis=0)
            t_ref[dd] = jnp.where(valid, t, NEG)
        for off in range(NA_ROWS):
            for p in range(4):
                o_ref[off, :, p * 128:(p + 1) * 128] = t_ref[2 * p - off + 7]

    return pl.pallas_call(
        body, name="na_bias", grid=(8,),
        in_specs=[pl.BlockSpec((None, 16, 128), lambda h: (h, 0, 0))],
        out_specs=pl.BlockSpec((None, NA_ROWS, GRID_W, NA_ROWS * GRID_W), lambda h: (h, 0, 0, 0)),
        out_shape=jax.ShapeDtypeStruct((8, NA_ROWS, GRID_W, NA_ROWS * GRID_W), F32),
        scratch_shapes=[pltpu.VMEM((14, GRID_W, 128), F32)],
        compiler_params=_params(("parallel",)))(rpb_rows)


def _na_bias_bwd(dbias):
    def body(d_ref, o_ref):
        lane, valid = _na_geometry()
        reverse = (_iota((GRID_W, GRID_W), 0) + _iota((GRID_W, GRID_W), 1) == GRID_W - 1).astype(F32)
        o_ref[...] = jnp.zeros((16, 128), F32)
        for dd in range(14):
            t = jnp.zeros((GRID_W, 128), F32)
            for off in range(NA_ROWS):
                for p in range(4):
                    if 2 * p - off + 7 == dd:
                        t = t + d_ref[off, :, p * 128:(p + 1) * 128]
            t = jnp.dot(reverse, jnp.where(valid, t, 0.0), precision=lax.Precision.HIGHEST,
                        preferred_element_type=F32)
            t = pltpu.roll(t, 128 - (GRID_W - 16), 1, stride=1, stride_axis=0)
            o_ref[dd:dd + 1, :] = jnp.sum(t, axis=0, keepdims=True)

    return pl.pallas_call(
        body, name="na_bias_bwd", grid=(8,),
        in_specs=[pl.BlockSpec((None, NA_ROWS, GRID_W, NA_ROWS * GRID_W), lambda h: (h, 0, 0, 0))],
        out_specs=pl.BlockSpec((None, 16, 128), lambda h: (h, 0, 0)),
        out_shape=jax.ShapeDtypeStruct((8, 16, 128), F32),
        compiler_params=_params(("parallel",)))(dbias)


GATE_ROWS = 128


def _group_weights(l0, l1, l2):
    m = jnp.maximum(jnp.maximum(l0, l1), l2)
    e0, e1, e2 = jnp.exp(l0 - m), jnp.exp(l1 - m), jnp.exp(l2 - m)
    inv = 1.0 / (e0 + e1 + e2)
    return e0 * inv, e1 * inv, e2 * inv


def _gate_specs():
    r512 = pl.BlockSpec((GATE_ROWS, 512), lambda i: (i, 0))
    r1024 = pl.BlockSpec((GATE_ROWS, D_MODEL), lambda i: (i, 0))
    silu_cols = [pl.BlockSpec((GATE_ROWS, 512), functools.partial(lambda b, i: (i, b), 13 + b)) for b in range(3)]
    logit_cols = [pl.BlockSpec((GATE_ROWS, D_MODEL), functools.partial(lambda b, i: (i, b), 8 + b)) for b in range(3)]
    return r512, r1024, silu_cols, logit_cols


def _gate_fwd(o_grp, l_grp, out_b, out_c, parts, merge_bias, wts):
    r512, r1024, silu_cols, logit_cols = _gate_specs()

    def body(o0, o1, o2, l0, l1, l2, ob, oc, ga, gb, gc, la, lb, lc, mb, wa, wb, wc,
             oa_ref, ua, ub, uc, za, zb, zc, y_ref):
        w0, w1, w2 = _group_weights(l0[...], l1[...], l2[...])
        out_a = w0 * o0[...] + w1 * o1[...] + w2 * o2[...]
        oa_ref[...] = out_a
        y = jnp.zeros((GATE_ROWS, D_MODEL), F32)
        for b, (ov, g_ref, l_ref, w_ref, u_ref, z_ref) in enumerate(
                ((out_a, ga, la, wa, ua, za), (ob[...], gb, lb, wb, ub, zb), (oc[...], gc, lc, wc, uc, zc))):
            g = g_ref[...].astype(F32)
            u = (ov * (g * _sigmoid(g))).astype(BF16)
            u_ref[...] = u
            z = lax.dot_general(u, w_ref[...], NT, preferred_element_type=F32)
            z_ref[...] = z.astype(BF16)
            gate = _sigmoid(l_ref[...].astype(F32) + mb[b:b + 1, :])
            y = y + gate * z
        y_ref[...] = y.astype(BF16)

    full = lambda shape: pl.BlockSpec(shape, lambda i: (0,) * len(shape))
    in_specs = ([r512] * 8 + silu_cols + logit_cols
                + [full((3, D_MODEL))] + [full((D_MODEL, 512))] * 3)
    out_specs = [r512] * 4 + [r1024] * 4
    out_shape = ([jax.ShapeDtypeStruct((SEQ, 512), F32)] + [jax.ShapeDtypeStruct((SEQ, 512), BF16)] * 3
                 + [jax.ShapeDtypeStruct((SEQ, D_MODEL), BF16)] * 4)
    res = pl.pallas_call(
        body, name="gate_fwd", grid=(SEQ // GATE_ROWS,), in_specs=in_specs, out_specs=out_specs,
        out_shape=out_shape, compiler_params=_params(("parallel",)))(
            *o_grp, *l_grp, out_b, out_c, parts, parts, parts, parts, parts, parts, merge_bias, *wts)
    return res[0], res[1:4], res[4:7], res[7]


def _gate_bwd(dy, z, parts, merge_bias, outs, o_grp, l_grp, wts, head_sum):
    r512, r1024, silu_cols, logit_cols = _gate_specs()

    def body(dy_ref, za, zb, zc, la, lb, lc, mb, oa, ob, oc, ga, gb, gc, o0, o1, o2, l0, l1, l2, wa, wb, wc, hs_ref,
             dla, dlb, dlc, gmb, dza, dzb, dzc, dga, dgb, dgc, do0, do1, do2, dp0, dp1, dp2, dob, doc):
        dyv = dy_ref[...].astype(F32)
        rows = []
        dos = []
        for b, (z_ref, l_ref, ov_ref, g_ref, w_ref, dl_ref, dz_ref, dg_ref) in enumerate(
                ((za, la, oa, ga, wa, dla, dza, dga), (zb, lb, ob, gb, wb, dlb, dzb, dgb),
                 (zc, lc, oc, gc, wc, dlc, dzc, dgc))):
            gate = _sigmoid(l_ref[...].astype(F32) + mb[b:b + 1, :])
            dl = dyv * z_ref[...].astype(F32) * gate * (1.0 - gate)
            dl_ref[...] = dl.astype(BF16)
            rows.append(jnp.sum(dl, axis=0, keepdims=True))
            dz = (dyv * gate).astype(BF16)
            dz_ref[...] = dz
            du = jnp.dot(dz, w_ref[...], preferred_element_type=F32)
            g = g_ref[...].astype(F32)
            sg = _sigmoid(g)
            dos.append(du * (g * sg))
            dg_ref[...] = (du * ov_ref[...] * (sg * (1.0 + g * (1.0 - sg)))).astype(BF16)

        @pl.when(pl.program_id(0) == 0)
        def _():
            gmb[...] = jnp.zeros((3, D_MODEL), F32)

        for b in range(3):
            gmb[b:b + 1, :] += rows[b]
        dob[...] = dos[1].astype(BF16)
        doc[...] = dos[2].astype(BF16)
        doa = dos[0]
        row_term = jnp.dot(doa * oa[...], hs_ref[...], precision=lax.Precision.HIGHEST, preferred_element_type=F32)
        ws = _group_weights(l0[...], l1[...], l2[...])
        for wg, do_ref, dp_ref in zip(ws, (do0, do1, do2), (dp0, dp1, dp2)):
            do_ref[...] = wg * doa
            dp_ref[...] = wg * row_term

    full = lambda shape: pl.BlockSpec(shape, lambda i: (0,) * len(shape))
    acc = pl.BlockSpec((3, D_MODEL), lambda i: (0, 0))
    in_specs = ([r1024] * 4 + logit_cols + [full((3, D_MODEL))] + [r512] * 3 + silu_cols + [r512] * 6
                + [full((D_MODEL, 512))] * 3 + [full((512, 512))])
    out_specs = [r1024] * 3 + [acc] + [r1024] * 3 + [r512] * 11
    out_shape = ([jax.ShapeDtypeStruct((SEQ, D_MODEL), BF16)] * 3 + [jax.ShapeDtypeStruct((3, D_MODEL), F32)]
                 + [jax.ShapeDtypeStruct((SEQ, D_MODEL), BF16)] * 3 + [jax.ShapeDtypeStruct((SEQ, 512), BF16)] * 3
                 + [jax.ShapeDtypeStruct((SEQ, 512), F32)] * 6 + [jax.ShapeDtypeStruct((SEQ, 512), BF16)] * 2)
    res = pl.pallas_call(
        body, name="gate_bwd", grid=(SEQ // GATE_ROWS,), in_specs=in_specs, out_specs=out_specs,
        out_shape=out_shape, compiler_params=_params(("arbitrary",)))(
            dy, *z, parts, parts, parts, merge_bias, *outs, parts, parts, parts, *o_grp, *l_grp, *wts, head_sum)
    return res[0:3], res[3], res[4:7], res[7:10], res[10:13], res[13:16], res[16], res[17]


def _post(y2, x, target, gain):
    rows = 256

    def body(y_ref, x_ref, t_ref, g_ref, do_ref, dy_ref, l_ref, gg_ref):
        yv = y_ref[...]
        rstd = lax.rsqrt(jnp.mean(yv * yv, axis=1, keepdims=True) + EPS)
        yn = yv * rstd
        gv = g_ref[...]
        err = x_ref[...] + yn * gv - t_ref[...]
        dout = err * (1.0 / D_MODEL)
        do_ref[...] = dout
        dn = dout * gv
        dy_ref[...] = (rstd * (dn - yn * jnp.mean(dn * yn, axis=1, keepdims=True))).astype(BF16)

        @pl.when(pl.program_id(0) == 0)
        def _():
            l_ref[...] = jnp.zeros((1, D_MODEL), F32)
            gg_ref[...] = jnp.zeros((1, D_MODEL), F32)

        l_ref[...] += jnp.sum(err * err, axis=0, keepdims=True)
        gg_ref[...] += jnp.sum(dout * yn, axis=0, keepdims=True)

    row = pl.BlockSpec((rows, D_MODEL), lambda i: (i, 0))
    vec = pl.BlockSpec((1, D_MODEL), lambda i: (0, 0))
    return pl.pallas_call(
        body, name="post", grid=(SEQ // rows,), in_specs=[row, row, row, vec], out_specs=[row, row, vec, vec],
        out_shape=[jax.ShapeDtypeStruct((SEQ, D_MODEL), F32), jax.ShapeDtypeStruct((SEQ, D_MODEL), BF16),
                   jax.ShapeDtypeStruct((1, D_MODEL), F32), jax.ShapeDtypeStruct((1, D_MODEL), F32)],
        compiler_params=_params(("arbitrary",)))(y2, x, target, gain)


def _local_step(x, mem, target, pre_norm, mem_norm, post_norm, na_rpb, wt_in, late_weights, dep_in=None,
                reduce_start=None):
    tabs = _rope_tables()
    hs, hst = _prenorm_fold(x, pre_norm)
    parts = _in_proj(hs, wt_in, tabs, dep_in)

    o_grp, l_grp = [], []
    for g, d in enumerate(DILATIONS):
        o, l = _attn_fwd("dil_fwd_%d" % g, "dil", parts, parts, parts, 12 * g, 12 * g + 4, 12 * g + 8, d=d)
        o_grp.append(o)
        l_grp.append(l)
    bias = _na_bias(jnp.pad(na_rpb, ((0, 0), (0, 1), (0, 128 - 31))))
    out_b, lse_b = _attn_fwd("na_fwd", "na", parts, parts, parts, 36, 40, 44, bias=bias)
    merge_bias, w_kv, wt_a, wt_b, wt_c, w_out = late_weights(out_b)
    memn = _rmsnorm_fwd("memnorm", mem, mem_norm, MEM_LEN)
    kv_m = _mm_simple("mem_kv", memn, w_kv, NN, BF16, MEM_LEN, 512, D_MODEL)
    out_c, lse_c = _attn_fwd("mem_fwd", "mem", parts, kv_m, kv_m, 48, 0, 4)

    wts = (wt_a, wt_b, wt_c)
    out_a, u, z, y = _gate_fwd(o_grp, l_grp, out_b, out_c, parts, merge_bias, wts)
    y2 = _mm_simple("out_proj", y, w_out, NN, F32, 512, D_MODEL, D_MODEL)
    dout, dy2, err_sq, g_post = _post(y2, x, target, post_norm)
    loss = 0.5 * jnp.sum(err_sq) / D_MODEL

    dy = _mm_simple("out_proj_dx", dy2, w_out, NT, BF16, 512, D_MODEL, D_MODEL)
    g_w_out = _mm_simple("out_proj_dw", y, dy2, TN, BF16, D_MODEL, 512, 512)

    rr = _iota((512, 512), 0) // HEAD_DIM
    cc = _iota((512, 512), 1) // HEAD_DIM
    head_sum = (rr == cc).astype(F32)
    dlog, g_mb, dz, dg, do_grp, dp_grp, do_b, do_c = _gate_bwd(
        dy, z, parts, merge_bias, (out_a, out_b, out_c), o_grp, l_grp, wts, head_sum)
    g_wt = [_mm_simple("branch_dw_%d" % b, dz[b], u[b], TN, BF16, D_MODEL, 512, 512) for b in range(3)]

    dqkv = []
    for g, d in enumerate(DILATIONS):
        dq, dk, dv = _attn_bwd("dil_bwd_%d" % g, "dil", parts, parts, parts, 12 * g, 12 * g + 4, 12 * g + 8,
                               do_grp[g], l_grp[g], dp=dp_grp[g], d=d, tabs=tabs[g])
        dqkv += [dq, dk, dv]
    dq_b, dk_b, dv_b, dbias = _attn_bwd("na_bwd", "na", parts, parts, parts, 36, 40, 44, do_b, lse_b, o=out_b,
                                        bias=bias)
    g_rpb_t = _na_bias_bwd(dbias)
    g_rpb = g_rpb_t[:, :15, :31] + jnp.pad(g_rpb_t[:, :14, 64:95], ((0, 0), (1, 0), (0, 0)))
    dq_c, dk_m, dv_m = _attn_bwd("mem_bwd", "mem", parts, kv_m, kv_m, 48, 0, 4, do_c, lse_c, o=out_c)

    dkv = jnp.concatenate([dk_m, dv_m], axis=1).astype(BF16)
    g_w_kv = _mm_simple("mem_kv_dw", memn, dkv, TN, BF16, D_MODEL, 512, MEM_LEN)
    dmemn = _mm_simple("mem_kv_dx", dkv, w_kv, NT, F32, MEM_LEN, 512, D_MODEL)
    g_mem_norm = _memnorm_bwd(mem, dmemn)

    grads = dict(w_kv=g_w_kv, wt_a=g_wt[0], wt_b=g_wt[1], wt_c=g_wt[2], w_out=g_w_out, merge_bias=g_mb,
                 mem_norm=g_mem_norm, post_norm=g_post, na_rpb=g_rpb)
    dep = reduce_start(grads) if reduce_start is not None else None
    dparts = dqkv + [dq_b, dk_b, dv_b, dq_c] + list(dg) + list(dlog)
    grads["wt_in"] = _in_proj_dw(dparts, hst, dep)
    dep = reduce_start(grads) if reduce_start is not None else None
    dh = _in_proj_dh(dparts, wt_in, dep)
    grad_x, grads["pre_norm"] = _prenorm_bwd(x, pre_norm, dh, dout)
    return loss, grad_x, grads


ANY = pl.BlockSpec(memory_space=pl.ANY)


def _place():
    return lax.axis_index("x"), lax.axis_index("y"), lax.axis_index("c")


def _all_gather(shard):
    r = shard.shape[0]
    half = r // 2

    def body(src, out, send_sems, recv_sems, local_sem):
        x, y, c = _place()
        me, sib = (x, y, c), (x, y, 1 - c)
        xn, yn, dg = (1 - x, y, c), (x, 1 - y, c), (1 - x, 1 - y, c)

        def rows(dev, part=None):
            blk = out.at[4 * dev[0] + 2 * dev[1] + dev[2]]
            return blk if part is None else blk.at[pl.ds(part * half, half)]

        def copy(k, dev, part, to, own=False):
            return pltpu.make_async_remote_copy(
                src_ref=src if own else rows(dev, part), dst_ref=rows(dev, part),
                send_sem=send_sems.at[k], recv_sem=recv_sems.at[k], device_id=to, device_id_type=MESH_ID)

        def other(dev):
            return (dev[0], dev[1], 1 - dev[2])

        mine = pltpu.make_async_copy(src, rows(me), local_sem)
        mine.start()
        sent = [copy(0, me, None, sib, own=True), copy(1, me, None, xn, own=True), copy(2, me, None, yn, own=True)]
        for cp in sent:
            cp.start()
        copy(1, xn, None, me).wait_recv()
        sent += [copy(3, xn, 0, yn), copy(5, xn, None, sib)]
        sent[-2].start()
        sent[-1].start()
        copy(2, yn, None, me).wait_recv()
        sent += [copy(4, yn, 1, xn), copy(6, yn, None, sib)]
        sent[-2].start()
        sent[-1].start()
        copy(3, dg, 0, me).wait_recv()
        sent.append(copy(7, dg, 0, sib))
        sent[-1].start()
        copy(4, dg, 1, me).wait_recv()
        sent.append(copy(8, dg, 1, sib))
        sent[-1].start()
        copy(0, sib, None, me).wait_recv()
        copy(5, other(xn), None, me).wait_recv()
        copy(6, other(yn), None, me).wait_recv()
        copy(7, other(dg), 0, me).wait_recv()
        copy(8, other(dg), 1, me).wait_recv()
        for cp in sent:
            cp.wait_send()
        mine.wait()

    return pl.pallas_call(
        body, name="all_gather", in_specs=[ANY], out_specs=ANY,
        out_shape=jax.ShapeDtypeStruct((N_DEV,) + shard.shape, shard.dtype),
        scratch_shapes=[pltpu.SemaphoreType.DMA((9,)), pltpu.SemaphoreType.DMA((9,)), pltpu.SemaphoreType.DMA])(shard)


def _exchange_sibling(name, terms):
    nt = len(terms)

    def body(*refs):
        srcs, outs = refs[:nt], refs[nt:2 * nt]
        send_sems, recv_sems = refs[2 * nt:]
        x, y, c = _place()
        copies = []
        for q in range(4):
            for t in range(nt):
                copies.append(pltpu.make_async_remote_copy(
                    src_ref=srcs[t].at[2 * q + 1 - c], dst_ref=outs[t].at[q],
                    send_sem=send_sems.at[q * nt + t], recv_sem=recv_sems.at[q * nt + t],
                    device_id=(x, y, 1 - c), device_id_type=MESH_ID))
        for cp in copies:
            cp.start()
        for cp in copies:
            cp.wait()

    return pl.pallas_call(
        body, name=name, in_specs=[ANY] * nt, out_specs=[ANY] * nt,
        out_shape=[jax.ShapeDtypeStruct((4,) + s.shape[1:], s.dtype) for s in terms],
        scratch_shapes=[pltpu.SemaphoreType.DMA((4 * nt,)), pltpu.SemaphoreType.DMA((4 * nt,))])(*terms)


HBM = pl.BlockSpec(memory_space=pltpu.HBM)
SEM = pl.BlockSpec(memory_space=pltpu.SEMAPHORE)
DATAFLOW = pltpu.SideEffectType.DATAFLOW_SIDE_EFFECTING


def _split_copies(kind, srcs, lands, send_sems, recv_sems):
    nt = len(srcs)
    x, y, c = _place()
    copies = []
    if kind == "gather":
        me = 4 * x + 2 * y + c
        for mask in range(1, 8):
            fx, fy, fc = (mask >> 2) & 1, (mask >> 1) & 1, mask & 1
            to = (1 - x if fx else x, 1 - y if fy else y, 1 - c if fc else c)
            for t in range(nt):
                k = (mask - 1) * nt + t
                copies.append(pltpu.make_async_remote_copy(
                    src_ref=srcs[t], dst_ref=lands[t].at[me], send_sem=send_sems.at[k], recv_sem=recv_sems.at[k],
                    device_id=to, device_id_type=MESH_ID))
    else:
        for s, (tx, ty) in enumerate([(1 - x, y), (x, 1 - y), (1 - x, 1 - y)]):
            for t in range(nt):
                k = s * nt + t
                copies.append(pltpu.make_async_remote_copy(
                    src_ref=srcs[t].at[2 * tx + ty], dst_ref=lands[t].at[s], send_sem=send_sems.at[k],
                    recv_sem=recv_sems.at[k], device_id=(tx, ty, c), device_id_type=MESH_ID))
    return copies


def _split_count(kind, nt):
    return (7 if kind == "gather" else 3) * nt


def _exchange_start(name, kind, srcs, land_shapes, after=None):
    nt = len(srcs)
    n = _split_count(kind, nt)
    dep_specs, dep_args = _dep_operand(after)
    nd = len(dep_args)

    def body(*refs):
        src_refs, land_refs = refs[:nt], refs[nt:2 * nt]
        send_sems, recv_sems = refs[2 * nt + nd], refs[2 * nt + nd + 1]
        token = refs[-1]
        for cp in _split_copies(kind, src_refs, land_refs, send_sems, recv_sems):
            cp.start()
        token[...] = jnp.zeros_like(token)

    lands = [pltpu.with_memory_space_constraint(lax.empty(s.shape, s.dtype), pltpu.HBM) for s in land_shapes]
    res = pl.pallas_call(
        body, name=name,
        out_shape=(pltpu.SemaphoreType.DMA((n,)), pltpu.SemaphoreType.DMA((n,)),
                   *[pltpu.HBM(s.shape, s.dtype) for s in srcs], *[pltpu.HBM(s.shape, s.dtype) for s in land_shapes],
                   jax.ShapeDtypeStruct((8, 128), F32)),
        in_specs=[HBM] * (2 * nt) + dep_specs,
        out_specs=(SEM, SEM, *([HBM] * (2 * nt)), pl.BlockSpec(memory_space=pltpu.VMEM)),
        input_output_aliases={i: 2 + i for i in range(2 * nt)},
        compiler_params=pltpu.CompilerParams(has_side_effects=DATAFLOW))(
            *[pltpu.with_memory_space_constraint(s, pltpu.HBM) for s in srcs], *lands, *dep_args)
    return res[0], res[1], list(res[2:2 + nt]), list(res[2 + nt:2 + 2 * nt]), res[-1]


def _exchange_wait(name, kind, send_sems, recv_sems, srcs, lands, after):
    nt = len(srcs)

    def body(*refs):
        src_refs, land_refs = refs[:nt], refs[nt:2 * nt]
        s_sems, r_sems = refs[2 * nt], refs[2 * nt + 1]
        for cp in _split_copies(kind, src_refs, land_refs, s_sems, r_sems):
            cp.wait_send()
            cp.wait_recv()

    res = pl.pallas_call(
        body, name=name,
        out_shape=tuple(pltpu.HBM(s.shape, s.dtype) for s in list(srcs) + list(lands)),
        in_specs=[HBM] * (2 * nt) + [SEM, SEM, pl.BlockSpec(memory_space=pl.ANY)],
        out_specs=tuple([HBM] * (2 * nt)),
        input_output_aliases={i: i for i in range(2 * nt)},
        compiler_params=pltpu.CompilerParams(has_side_effects=DATAFLOW))(
            *srcs, *lands, send_sems, recv_sems, after)
    return list(res[:nt]), list(res[nt:])


def _add_sibling(name, term, recv, rows):
    _, r, w = term.shape
    cidx = lax.axis_index("c").astype(jnp.int32).reshape(1)

    def body(c_ref, a_ref, b_ref, o_ref):
        o_ref[...] = (a_ref[...].astype(F32) + b_ref[...].astype(F32)).astype(o_ref.dtype)

    grid_spec = pltpu.PrefetchScalarGridSpec(
        num_scalar_prefetch=1, grid=(4, r // rows),
        in_specs=[pl.BlockSpec((None, rows, w), lambda q, i, c_ref: (2 * q + c_ref[0], i, 0)),
                  pl.BlockSpec((None, rows, w), lambda q, i, c_ref: (q, i, 0))],
        out_specs=pl.BlockSpec((None, rows, w), lambda q, i, c_ref: (q, i, 0)))
    return pl.pallas_call(
        body, name=name, grid_spec=grid_spec, out_shape=jax.ShapeDtypeStruct((4, r, w), term.dtype),
        compiler_params=_params(("parallel", "parallel")))(cidx, term, recv)


def _add_sibling_small(name, terms, recvs):
    nt = len(terms)

    def body(*refs):
        c = lax.axis_index("c")
        for t_ref, r_ref, o_ref in zip(refs[:nt], refs[nt:2 * nt], refs[2 * nt:]):
            for q in range(4):
                o_ref[q] = (t_ref[2 * q + c].astype(F32) + r_ref[q].astype(F32)).astype(o_ref.dtype)

    return pl.pallas_call(
        body, name=name, out_shape=[jax.ShapeDtypeStruct((4,) + t.shape[1:], t.dtype) for t in terms],
        compiler_params=_params())(*terms, *recvs)


def _add_chips(name, sums, recv, rows):
    _, r, w = sums.shape
    qidx = (2 * lax.axis_index("x") + lax.axis_index("y")).astype(jnp.int32).reshape(1)

    def body(q_ref, a_ref, b_ref, o_ref):
        o_ref[...] = ((a_ref[...].astype(F32) + b_ref[0].astype(F32))
                      + (b_ref[1].astype(F32) + b_ref[2].astype(F32)))

    grid_spec = pltpu.PrefetchScalarGridSpec(
        num_scalar_prefetch=1, grid=(r // rows,),
        in_specs=[pl.BlockSpec((None, rows, w), lambda i, q_ref: (q_ref[0], i, 0)),
                  pl.BlockSpec((3, rows, w), lambda i, q_ref: (0, i, 0))],
        out_specs=pl.BlockSpec((rows, w), lambda i, q_ref: (i, 0)))
    return pl.pallas_call(
        body, name=name, grid_spec=grid_spec, out_shape=jax.ShapeDtypeStruct((r, w), F32),
        compiler_params=_params(("parallel",)))(qidx, sums, recv)


def _rs_rows(a):
    return SHARD_IN // 4 if a.shape[1] == SHARD_IN else a.shape[1]


def _reduce_scatter_start(tag, names, terms):
    recv1 = _exchange_sibling("exchange_sibling_" + tag, terms)
    if len(terms) == 1:
        sums = [_add_sibling("add_sibling_" + names[0], terms[0], recv1[0], _rs_rows(terms[0]))]
    else:
        sums = _add_sibling_small("add_sibling_" + tag, terms, recv1)
    lands =[jax.ShapeDtypeStruct((3,) + s.shape[1:], s.dtype) for s in sums]
    send_sems, recv_sems, sums, lands, token = _exchange_start("exchange_chips_start_" + tag, "chips", sums, lands)
    return (tag, names, send_sems, recv_sems, sums, lands), token


def _reduce_scatter_wait(state, after):
    tag, names, send_sems, recv_sems, sums, lands = state
    sums, recv2 = _exchange_wait("exchange_chips_wait_" + tag, "chips", send_sems, recv_sems, sums, lands, after)
    return names, sums, recv2


def _adamw(name, w, g, m, v):
    def body(w_ref, g_ref, m_ref, v_ref, d_ref, nm_ref, nv_ref):
        d_ref[...], nm_ref[...], nv_ref[...] = _adam_math(w_ref[...], g_ref[...], m_ref[...], v_ref[...])

    return pl.pallas_call(
        body, name=name, out_shape=[jax.ShapeDtypeStruct(w.shape, F32)] * 3, compiler_params=_params())(w, g, m, v)


def _adam_math(w, g, m, v):
    nm = ADAM_B1 * m + (1.0 - ADAM_B1) * g
    nv = ADAM_B2 * v + (1.0 - ADAM_B2) * (g * g)
    c1 = 1.0 - ADAM_B1 ** ADAM_STEP
    c2 = 1.0 - ADAM_B2 ** ADAM_STEP
    return -ADAM_LR * ((nm / c1) / (jnp.sqrt(nv / c2) + ADAM_EPS) + ADAM_WD * w), nm, nv


def _adamw_chips(name, sums, recv, w, m, v, transposed, rows=None, dep=None):
    r, c = w.shape
    rows = r if rows is None else rows
    qidx = (2 * lax.axis_index("x") + lax.axis_index("y")).astype(jnp.int32).reshape(1)
    dep_specs, dep_args = _dep_operand(dep)

    def body(q_ref, a_ref, b_ref, w_ref, m_ref, v_ref, *rest):
        g_ref, d_ref, nm_ref, nv_ref = rest[-4:]
        g = (a_ref[...].astype(F32) + b_ref[0].astype(F32)) + (b_ref[1].astype(F32) + b_ref[2].astype(F32))
        if transposed:
            g = g.T
        g_ref[...] = g
        d_ref[...], nm_ref[...], nv_ref[...] = _adam_math(w_ref[...], g, m_ref[...], v_ref[...])

    row = pl.BlockSpec((rows, c), lambda i, q_ref: (i, 0))
    if transposed:
        term_specs = [pl.BlockSpec((None, c, rows), lambda i, q_ref: (q_ref[0], 0, i)),
                      pl.BlockSpec((3, c, rows), lambda i, q_ref: (0, 0, i))]
    else:
        term_specs = [pl.BlockSpec((None, rows, c), lambda i, q_ref: (q_ref[0], i, 0)),
                      pl.BlockSpec((3, rows, c), lambda i, q_ref: (0, i, 0))]
    grid_spec = pltpu.PrefetchScalarGridSpec(
        num_scalar_prefetch=1, grid=(r // rows,), in_specs=term_specs + [row, row, row] + dep_specs,
        out_specs=[row] * 4)
    return pl.pallas_call(
        body, name=name, grid_spec=grid_spec, out_shape=[jax.ShapeDtypeStruct((r, c), F32)] * 4,
        compiler_params=_params(("parallel",)))(qidx, sums, recv, w, m, v, *dep_args)


def _sum_devices(gathered):
    def body(g_ref, o_ref):
        acc = g_ref[0]
        for j in range(1, N_DEV):
            acc = acc + g_ref[j]
        o_ref[...] = acc

    return pl.pallas_call(
        body, name="sum_devices", out_shape=jax.ShapeDtypeStruct(gathered.shape[1:], F32),
        compiler_params=_params())(gathered)


def _rows128(a, rows):
    flat = a.reshape(-1)
    return jnp.pad(flat, (0, rows * 128 - flat.shape[0])).reshape(rows, 128)


def kernel(x, mem, pre_norm, w_in, merge_bias, na_rpb, mem_norm, w_mem_kv, w_branch_a, w_branch_b, w_branch_c, w_out, post_norm, loss_target, m_pre_norm, m_w_in, m_merge_bias, m_na_rpb, m_mem_norm, m_w_mem_kv, m_w_branch_a, m_w_branch_b, m_w_branch_c, m_w_out, m_post_norm, v_pre_norm, v_w_in, v_merge_bias, v_na_rpb, v_mem_norm, v_w_mem_kv, v_w_branch_a, v_w_branch_b, v_w_branch_c, v_w_out, v_post_norm):
    wt_in_s = w_in[0].T.astype(BF16)
    rows_s = jnp.concatenate([w_mem_kv[0], w_out[0]], axis=0).astype(BF16)
    cols_s = jnp.concatenate([w_branch_a[0].T, w_branch_b[0].T, w_branch_c[0].T], axis=0).astype(BF16)
    mb_s = jnp.pad(merge_bias[0], ((0, 5), (0, 0)))
    wt_in = _all_gather(wt_in_s).reshape(N_IN, D_MODEL)

    late_own = [rows_s, cols_s, mb_s]
    late_lands = [jax.ShapeDtypeStruct((N_DEV,) + s.shape, s.dtype) for s in late_own]
    l_send, l_recv, late_own, late_lands, late_token = _exchange_start("gather_late_start", "gather", late_own,
                                                                       late_lands, after=wt_in)
    me = 4 * lax.axis_index("x") + 2 * lax.axis_index("y") + lax.axis_index("c")

    def late_weights(after):
        own, lands = _exchange_wait("gather_late_wait", "gather", l_send, l_recv, late_own, late_lands, after)
        g_rows, g_cols, g_mb = [lax.dynamic_update_slice(land, o[None], (me, 0, 0)) for land, o in zip(lands, own)]
        return (g_mb[:, :3].transpose(1, 0, 2).reshape(3, D_MODEL),
                g_rows[:, :128].reshape(D_MODEL, D_MODEL), g_cols[:, 0:128].reshape(D_MODEL, 512),
                g_cols[:, 128:256].reshape(D_MODEL, 512), g_cols[:, 256:384].reshape(D_MODEL, 512),
                g_rows[:, 128:].reshape(D_MODEL, D_MODEL))

    rs_state = []

    def reduce_start(grads):
        if "wt_in" in grads:
            state, token = _reduce_scatter_start("w_in", ["w_in"],
                                                 [grads["wt_in"].reshape(N_DEV, SHARD_IN, D_MODEL)])
        else:
            gmb_t = jnp.pad(grads["merge_bias"].reshape(3, N_DEV, 128).transpose(1, 0, 2), ((0, 0), (0, 5), (0, 0)))
            names = ["w_kv", "w_out", "a", "b", "c", "mb"]
            terms = [grads["w_kv"].reshape(N_DEV, 128, D_MODEL), grads["w_out"].reshape(N_DEV, 128, D_MODEL),
                     grads["wt_a"].reshape(N_DEV, 128, 512), grads["wt_b"].reshape(N_DEV, 128, 512),
                     grads["wt_c"].reshape(N_DEV, 128, 512), gmb_t]
            state, token = _reduce_scatter_start("rest", names, terms)
        rs_state.append(state)
        return token

    loss_term, grad_x, grads = _local_step(
        x[0], mem[0], loss_target[0], pre_norm, mem_norm, post_norm, na_rpb[0], wt_in, late_weights,
        dep_in=late_token, reduce_start=reduce_start)

    small = jnp.concatenate([_rows128(grads["pre_norm"], 8), _rows128(grads["mem_norm"], 8),
                             _rows128(grads["post_norm"], 8), _rows128(grads["na_rpb"], 32),
                             _rows128(loss_term, 8)], axis=0)
    s_send, s_recv, s_own, s_land, s_token = _exchange_start(
        "gather_small_start", "gather", [small], [jax.ShapeDtypeStruct((N_DEV,) + small.shape, F32)])
    grad = {}
    weights = {
        "pre_norm": (pre_norm, m_pre_norm, v_pre_norm), "w_in": (w_in, m_w_in, v_w_in),
        "merge_bias": (merge_bias, m_merge_bias, v_merge_bias), "na_rpb": (na_rpb, m_na_rpb, v_na_rpb),
        "mem_norm": (mem_norm, m_mem_norm, v_mem_norm), "w_mem_kv": (w_mem_kv, m_w_mem_kv, v_w_mem_kv),
        "w_branch_a": (w_branch_a, m_w_branch_a, v_w_branch_a), "w_branch_b": (w_branch_b, m_w_branch_b, v_w_branch_b),
        "w_branch_c": (w_branch_c, m_w_branch_c, v_w_branch_c), "w_out": (w_out, m_w_out, v_w_out),
        "post_norm": (post_norm, m_post_norm, v_post_norm)}
    order = ["pre_norm", "w_in", "merge_bias", "na_rpb", "mem_norm", "w_mem_kv", "w_branch_a", "w_branch_b",
             "w_branch_c", "w_out", "post_norm"]
    delta, new_m, new_v = {}, {}, {}

    def update(n):
        w, m, v = weights[n]
        shape = w.shape
        two_d = (-1, shape[-1])
        dl, nm, nv = _adamw("adamw_" + n, w.reshape(two_d), grad[n].reshape(two_d), m.reshape(two_d),
                            v.reshape(two_d))
        delta[n], new_m[n], new_v[n] = dl.reshape(shape), nm.reshape(shape), nv.reshape(shape)

    def update_sharded(n, sums, recv, transposed, rows=None, dep=None):
        w, m, v = weights[n]
        g, dl, nm, nv = _adamw_chips("adamw_" + n, sums, recv, w[0], m[0], v[0], transposed, rows, dep)
        grad[n], delta[n], new_m[n], new_v[n] = g[None], dl[None], nm[None], nv[None]
        return dl

    _, sums, recv2 = _reduce_scatter_wait(rs_state[0], s_token)
    dep = None
    for i, (n, transposed) in enumerate((("w_mem_kv", False), ("w_out", False), ("w_branch_a", True),
                                         ("w_branch_b", True), ("w_branch_c", True))):
        dep = update_sharded(n, sums[i], recv2[i], transposed, dep=dep)
    grad["merge_bias"] = _add_chips("add_chips_mb", sums[5], recv2[5], 8)[:3][None]
    update("merge_bias")
    s_own, s_land = _exchange_wait("gather_small_wait", "gather", s_send, s_recv, s_own, s_land, dep)
    total = _sum_devices(lax.dynamic_update_slice(s_land[0], s_own[0][None], (me, 0, 0)))
    loss = total[56, 0]
    grad.update({"pre_norm": total[0:8].reshape(1, D_MODEL), "mem_norm": total[8:16].reshape(1, D_MODEL),
                 "post_norm": total[16:24].reshape(1, D_MODEL),
                 "na_rpb": total[24:56].reshape(-1)[:8 * 15 * 31].reshape(1, 8, 15, 31)})
    for n in ("pre_norm", "na_rpb", "mem_norm", "post_norm"):
        update(n)
    _, sums_in, recv_in = _reduce_scatter_wait(rs_state[1], delta["post_norm"])
    update_sharded("w_in", sums_in[0], recv_in[0], True, 256)

    return (loss, grad_x[None], *[grad[n] for n in order], *[delta[n] for n in order],
            *[new_m[n] for n in order], *[new_v[n] for n in order])
```

```python
import functools

import numpy as np
import jax
import jax.numpy as jnp
from jax import lax
from jax.experimental import pallas as pl
from jax.experimental.pallas import tpu as pltpu

F32 = jnp.float32
BF16 = jnp.bfloat16

SEQ = 2048
D_MODEL = 1024
N_IN = 11264
N_DEV = 8
SHARD_IN = N_IN // N_DEV
HEAD_DIM = 64
GRID_W = 64
NA_ROWS = 8
MEM_LEN = 256
DILATIONS = (1, 4, 16)
REACH = 64
ROPE_THETA = 500000.0
ROPE_DIM = 16
EPS = 1e-6
NEG = -1e30
ADAM_LR = 0.001
ADAM_B1 = 0.9
ADAM_B2 = 0.999
ADAM_EPS = 1e-08
ADAM_WD = 0.01
ADAM_STEP = 10

VMEM_LIMIT_BYTES = 56 * 1024 * 1024
MESH_ID = pl.DeviceIdType.MESH

NN = (((1,), (0,)), ((), ()))
NT = (((1,), (1,)), ((), ()))
TN = (((0,), (0,)), ((), ()))


def _params(sem=None):
    return pltpu.CompilerParams(dimension_semantics=sem, vmem_limit_bytes=VMEM_LIMIT_BYTES)


def _iota(shape, dim):
    return lax.broadcasted_iota(jnp.int32, shape, dim)


def _sigmoid(x):
    return 1.0 / (1.0 + jnp.exp(-x))


def _rope_tables():
    half = ROPE_DIM // 2
    inv = (ROPE_THETA ** (-np.arange(half, dtype=np.float64) * 2.0 / ROPE_DIM)).astype(np.float32)
    pos = np.arange(SEQ, dtype=np.float32)
    ang = pos[:, None] * inv[None, :]
    cos, sin = np.cos(ang), np.sin(ang)
    zeros = np.zeros_like(cos)
    rest = HEAD_DIM - ROPE_DIM
    c64 = np.concatenate([cos, cos, np.ones((SEQ, rest), np.float32)], axis=1)
    s1 = np.concatenate([zeros, sin, np.zeros((SEQ, rest), np.float32)], axis=1)
    s2 = np.concatenate([-sin, zeros, np.zeros((SEQ, rest), np.float32)], axis=1)

    def fold(t, d):
        return t.reshape(SEQ // d, d, t.shape[1]).transpose(1, 0, 2).reshape(SEQ, t.shape[1])

    tabs = [np.stack([np.tile(fold(t, d), (1, 2)) for t in (c64, s1, s2)], axis=0) for d in DILATIONS]
    return jnp.asarray(np.stack(tabs, axis=0), dtype=F32)


def _rope(a, c, s1, s2):
    return a * c + pltpu.roll(a, 8, 1) * s1 + pltpu.roll(a, 120, 1) * s2


def _rope_t(a, c, s1, s2):
    return a * c + pltpu.roll(a * s1, 120, 1) + pltpu.roll(a * s2, 8, 1)


def _perm_of_block(j):
    return jnp.where(j < 3, 0, jnp.where(j < 6, 1, jnp.where(j < 9, 2, 0)))


def _mm(name, a, b, out_shape, out_dtype, grid, a_spec, b_spec, o_spec, acc_shape, dims, k_axis, nk):
    def body(a_ref, b_ref, o_ref, acc_ref):
        k = pl.program_id(k_axis)

        @pl.when(k == 0)
        def _():
            acc_ref[...] = jnp.zeros(acc_shape, F32)

        acc_ref[...] += lax.dot_general(a_ref[...], b_ref[...], dims, preferred_element_type=F32)

        @pl.when(k == nk - 1)
        def _():
            o_ref[...] = acc_ref[...].astype(out_dtype)

    sem = tuple("arbitrary" if ax == k_axis else "parallel" for ax in range(len(grid)))
    return pl.pallas_call(
        body, name=name, grid=grid, in_specs=[a_spec, b_spec], out_specs=o_spec,
        out_shape=jax.ShapeDtypeStruct(out_shape, out_dtype),
        scratch_shapes=[pltpu.VMEM(acc_shape, F32)], compiler_params=_params(sem))(a, b)


def _mm_simple(name, a, b, dims, out_dtype, tm, tn, tk):
    if dims is NN:
        m, kk = a.shape
        n = b.shape[1]
        a_spec = pl.BlockSpec((tm, tk), lambda i, j, k: (i, k))
        b_spec = pl.BlockSpec((tk, tn), lambda i, j, k: (k, j))
    elif dims is NT:
        m, kk = a.shape
        n = b.shape[0]
        a_spec = pl.BlockSpec((tm, tk), lambda i, j, k: (i, k))
        b_spec = pl.BlockSpec((tn, tk), lambda i, j, k: (j, k))
    else:
        kk, m = a.shape
        n = b.shape[1]
        a_spec = pl.BlockSpec((tk, tm), lambda i, j, k: (k, i))
        b_spec = pl.BlockSpec((tk, tn), lambda i, j, k: (k, j))
    grid = (m // tm, n // tn, kk // tk)
    o_spec = pl.BlockSpec((tm, tn), lambda i, j, k: (i, j))
    return _mm(name, a, b, (m, n), out_dtype, grid, a_spec, b_spec, o_spec, (tm, tn), dims, 2, kk // tk)


def _rmsnorm_fwd(name, x, gain, rows):
    n, d = x.shape

    def body(x_ref, g_ref, o_ref):
        xv = x_ref[...]
        rstd = lax.rsqrt(jnp.mean(xv * xv, axis=1, keepdims=True) + EPS)
        o_ref[...] = (xv * rstd * g_ref[...]).astype(BF16)

    return pl.pallas_call(
        body, name=name, grid=(n // rows,),
        in_specs=[pl.BlockSpec((rows, d), lambda i: (i, 0)), pl.BlockSpec((1, d), lambda i: (0, 0))],
        out_specs=pl.BlockSpec((rows, d), lambda i: (i, 0)),
        out_shape=jax.ShapeDtypeStruct((n, d), BF16), compiler_params=_params(("parallel",)))(x, gain)


def _folded_rows(first, rows, d):
    if d == 1:
        return pl.ds(pl.multiple_of(first, rows), rows)
    mlen = SEQ // d
    return pl.ds((first % mlen) * d + first // mlen, rows, stride=d)


def _prenorm_fold(x, gain):
    rows = 128

    nchunk = D_MODEL // 128

    def body(*refs):
        x_refs, g_ref, hs_ref, hst_ref = refs[:nchunk], refs[nchunk], refs[nchunk + 1], refs[nchunk + 2]
        first = pl.program_id(0) * rows
        for p, d in enumerate(DILATIONS):
            idx = _folded_rows(first, rows, d)
            xv = jnp.concatenate([r[idx, :] for r in x_refs], axis=1)
            rstd = lax.rsqrt(jnp.mean(xv * xv, axis=1, keepdims=True) + EPS)
            h = xv * rstd * g_ref[...]
            hs_ref[p] = h.astype(BF16)
            hst_ref[p] = h.T.astype(BF16)

    x_specs = [pl.BlockSpec((SEQ, 128), functools.partial(lambda c, i: (0, c), c)) for c in range(nchunk)]
    return pl.pallas_call(
        body, name="prenorm", grid=(SEQ // rows,),
        in_specs=x_specs + [pl.BlockSpec((1, D_MODEL), lambda i: (0, 0))],
        out_specs=[pl.BlockSpec((3, rows, D_MODEL), lambda i: (0, i, 0)),
                   pl.BlockSpec((3, D_MODEL, rows), lambda i: (0, 0, i))],
        out_shape=[jax.ShapeDtypeStruct((3, SEQ, D_MODEL), BF16), jax.ShapeDtypeStruct((3, D_MODEL, SEQ), BF16)],
        compiler_params=_params(("parallel",)))(*([x] * nchunk), gain)


def _prenorm_bwd(x, gain, dh, dout):
    rows = 256

    def body(x_ref, g_ref, a_ref, do_ref, dx_ref, gg_ref):
        xv = x_ref[...]
        rstd = lax.rsqrt(jnp.mean(xv * xv, axis=1, keepdims=True) + EPS)
        xn = xv * rstd
        dh = jnp.concatenate([a_ref[c] for c in range(D_MODEL // 128)], axis=1)
        gdh = dh * g_ref[...]
        dx_ref[...] = rstd * (gdh - xn * jnp.mean(gdh * xn, axis=1, keepdims=True)) + do_ref[...]

        @pl.when(pl.program_id(0) == 0)
        def _():
            gg_ref[...] = jnp.zeros((1, D_MODEL), F32)

        gg_ref[...] += jnp.sum(dh * xn, axis=0, keepdims=True)

    row = pl.BlockSpec((rows, D_MODEL), lambda i: (i, 0))
    vec = pl.BlockSpec((1, D_MODEL), lambda i: (0, 0))
    return pl.pallas_call(
        body, name="prenorm_bwd", grid=(SEQ // rows,),
        in_specs=[row, vec, pl.BlockSpec((D_MODEL // 128, rows, 128), lambda i: (0, i, 0)), row], out_specs=[row, vec],
        out_shape=[jax.ShapeDtypeStruct((SEQ, D_MODEL), F32), jax.ShapeDtypeStruct((1, D_MODEL), F32)],
        compiler_params=_params(("arbitrary",)))(x, gain, dh, dout)


def _memnorm_bwd(mem, dmemn):
    def body(m_ref, d_ref, gg_ref):
        mv = m_ref[...]
        rstd = lax.rsqrt(jnp.mean(mv * mv, axis=1, keepdims=True) + EPS)
        gg_ref[...] = jnp.sum(d_ref[...] * mv * rstd, axis=0, keepdims=True)

    return pl.pallas_call(
        body, name="memnorm_bwd", out_shape=jax.ShapeDtypeStruct((1, D_MODEL), F32),
        compiler_params=_params())(mem, dmemn)


def _dep_operand(dep):
    return ([], []) if dep is None else ([pl.BlockSpec(memory_space=pl.ANY)], [dep])


def _in_proj(hs, wt, tabs, dep=None):
    tm, tn = 512, 512
    dep_specs, dep_args = _dep_operand(dep)

    def body(h_ref, w_ref, t_ref, *rest):
        o_ref = rest[-1]
        j = pl.program_id(0)
        is_rope = jnp.logical_and(j < 9, j % 3 != 2)
        row_slices = [slice(r * tm, (r + 1) * tm) for r in range(SEQ // tm)]

        def product(rs):
            return lax.dot_general(h_ref[rs, :], w_ref[...], NT, preferred_element_type=F32)

        @pl.when(is_rope)
        def _():
            for rs in row_slices:
                acc = product(rs)
                c, s1, s2 = t_ref[0, rs, :], t_ref[1, rs, :], t_ref[2, rs, :]
                for q in range(tn // 128):
                    a = acc[:, q * 128:(q + 1) * 128]
                    o_ref[rs, q * 128:(q + 1) * 128] = _rope(a, c, s1, s2).astype(BF16)

        @pl.when(jnp.logical_not(is_rope))
        def _():
            for rs in row_slices:
                o_ref[rs, :] = product(rs).astype(BF16)

    return pl.pallas_call(
        body, name="in_proj", grid=(N_IN // tn,),
        in_specs=[pl.BlockSpec((None, SEQ, D_MODEL), lambda j: (_perm_of_block(j), 0, 0)),
                  pl.BlockSpec((tn, D_MODEL), lambda j: (j, 0)),
                  pl.BlockSpec((None, 3, SEQ, 128), lambda j: (_perm_of_block(j), 0, 0, 0))] + dep_specs,
        out_specs=pl.BlockSpec((SEQ, tn), lambda j: (0, j)),
        out_shape=jax.ShapeDtypeStruct((SEQ, N_IN), BF16),
        compiler_params=_params(("parallel",)))(hs, wt, tabs, *dep_args)


def _piece_blocks(pieces):
    return [(a, h * 512) for a, p in enumerate(pieces) for h in range(p.shape[1] // 512)]


def _block_fetch(piece_refs, blocks, buf, sem):
    def start(block, slot):
        for b, (a, col) in enumerate(blocks):
            @pl.when(block == b)
            def _():
                pltpu.make_async_copy(piece_refs[a].at[:, pl.ds(col, 512)], buf.at[slot], sem.at[slot]).start()

    def wait(slot):
        pltpu.make_async_copy(piece_refs[0].at[:, pl.ds(0, 512)], buf.at[slot], sem.at[slot]).wait()

    return start, wait


def _in_proj_dw(pieces, hst, dep=None):
    tn = 512
    blocks = _piece_blocks(pieces)
    nblk = len(blocks)
    npc = len(pieces)
    dep_specs, dep_args = _dep_operand(dep)

    def body(h_ref, *rest):
        piece_refs = rest[:npc]
        o_ref, buf, sem = rest[-3:]
        j = pl.program_id(0)
        slot = j % 2
        start, wait = _block_fetch(piece_refs, blocks, buf, sem)

        @pl.when(j == 0)
        def _():
            start(j, slot)

        wait(slot)

        @pl.when(j + 1 < nblk)
        def _():
            start(j + 1, 1 - slot)

        acc = jnp.dot(h_ref[...], buf[slot], preferred_element_type=F32)
        o_ref[...] = acc.T.astype(BF16)

    return pl.pallas_call(
        body, name="in_proj_dw", grid=(nblk,),
        in_specs=[pl.BlockSpec((None, D_MODEL, SEQ), lambda j: (_perm_of_block(j), 0, 0))] + [ANY] * npc + dep_specs,
        out_specs=pl.BlockSpec((tn, D_MODEL), lambda j: (j, 0)),
        out_shape=jax.ShapeDtypeStruct((N_IN, D_MODEL), BF16),
        scratch_shapes=[pltpu.VMEM((2, SEQ, tn), BF16), pltpu.SemaphoreType.DMA((2,))],
        compiler_params=_params(("arbitrary",)))(hst, *pieces, *dep_args)


def _in_proj_dh(pieces, wt, dep=None):
    tk = 512
    blocks = _piece_blocks(pieces)
    nblk = len(blocks)
    npc = len(pieces)
    nchunk = D_MODEL // 128

    def col(s):
        return jnp.where(s < 3, s, jnp.where(s < 16, s + 6, s - 13))

    dep_specs, dep_args = _dep_operand(dep)

    def body(w_ref, *rest):
        piece_refs = rest[:npc]
        o_ref, acc_ref, buf, sem = rest[-4:]
        s = pl.program_id(0)
        slot = s % 2
        start, wait = _block_fetch(piece_refs, blocks, buf, sem)

        @pl.when(s == 0)
        def _():
            start(col(s), slot)

        wait(slot)

        @pl.when(s + 1 < nblk)
        def _():
            start(col(s + 1), 1 - slot)

        row_slices = [slice(r * 512, (r + 1) * 512) for r in range(SEQ // 512)]

        def product(rs):
            return jnp.dot(buf[slot, rs, :], w_ref[...], preferred_element_type=F32)

        def accumulate(cond, to_out, init):
            @pl.when(cond)
            def _():
                for rs in row_slices:
                    prod = product(rs)
                    if not to_out:
                        if init:
                            acc_ref[rs, :] = prod
                        else:
                            acc_ref[rs, :] += prod
                        continue
                    for c in range(nchunk):
                        if init:
                            o_ref[c, rs, :] = prod[:, c * 128:(c + 1) * 128]
                        else:
                            o_ref[c, rs, :] += prod[:, c * 128:(c + 1) * 128]

        accumulate(s == 0, True, True)
        accumulate(jnp.logical_and(s > 0, s < 16), True, False)
        accumulate(jnp.logical_or(s == 16, s == 19), False, True)
        accumulate(jnp.logical_and(s > 16, s != 19), False, False)
        for last, d in ((18, 4), (21, 16)):
            @pl.when(s == last)
            def _():
                mlen = SEQ // d
                for r in range(d):
                    for c in range(nchunk):
                        o_ref[c, pl.ds(r, mlen, stride=d), :] += acc_ref[r * mlen:(r + 1) * mlen,
                                                                         c * 128:(c + 1) * 128]

    return pl.pallas_call(
        body, name="in_proj_dh", grid=(nblk,),
        in_specs=[pl.BlockSpec((tk, D_MODEL), lambda s: (col(s), 0))] + [ANY] * npc + dep_specs,
        out_specs=pl.BlockSpec((nchunk, SEQ, 128), lambda s: (0, 0, 0)),
        out_shape=jax.ShapeDtypeStruct((nchunk, SEQ, 128), F32),
        scratch_shapes=[pltpu.VMEM((SEQ, D_MODEL), F32), pltpu.VMEM((2, SEQ, tk), BF16),
                        pltpu.SemaphoreType.DMA((2,))],
        compiler_params=_params(("arbitrary",)))(wt, *pieces, *dep_args)


def _head_lanes(lanes, hh):
    return lanes >= 64 if hh == 1 else lanes < 64


def _head_rows(x, lanes, hh, pair):
    if not pair:
        return jnp.max(x, axis=1, keepdims=True)
    return jnp.max(jnp.where(_head_lanes(lanes, hh), x, -jnp.inf), axis=1, keepdims=True)


def _mask_head(x, lanes, hh, pair, scale=1.0):
    if not pair:
        return x
    xf = x.astype(F32) if scale == 1.0 else x.astype(F32) * scale
    return jnp.where(_head_lanes(lanes, hh), xf, 0.0).astype(BF16)


def _window(mode, qi, tq, mlen, tk):
    if mode == "dil":
        q0 = qi * tq
        seg = (q0 // mlen) * mlen
        ks = jnp.clip(q0 - REACH, seg, seg + mlen - tk)
        return pl.multiple_of(ks, 64)
    if mode == "na":
        r_start = jnp.clip(qi - NA_ROWS // 2, 0, SEQ // GRID_W - NA_ROWS)
        return pl.multiple_of(r_start * GRID_W, 64)
    return 0


def _band_mask(qi, tq, tk, ks):
    qpos = qi * tq + _iota((tq, tk), 0)
    kpos = ks + _iota((tq, tk), 1)
    return jnp.where(jnp.abs(qpos - kpos) <= REACH, 0.0, NEG).astype(F32)


def _stack_heads(x, lanes, pair, scale=1.0):
    if not pair:
        return x
    return jnp.concatenate([_mask_head(x, lanes, hh, pair, scale) for hh in range(2)], axis=0)


def _stack_rows(x, lanes, pair):
    if not pair:
        return _head_rows(x, lanes, 0, pair)
    return jnp.concatenate([_head_rows(x, lanes, hh, pair) for hh in range(2)], axis=0)


def _unstack_heads(x, lanes, pair, tq):
    if not pair:
        return x
    return jnp.where(lanes < 64, x[:tq], x[tq:])


def _scores(mode, qst, k, sscale, band, qi, bias_ref, pair):
    s = lax.dot_general(qst, k, NT, preferred_element_type=F32)
    if sscale != 1.0:
        s = s * sscale
    if mode == "dil":
        s = s + jnp.concatenate([band, band], axis=0)
    elif mode == "na":
        off = qi - jnp.clip(qi - NA_ROWS // 2, 0, SEQ // GRID_W - NA_ROWS)
        s = s + jnp.concatenate([bias_ref[0, off], bias_ref[1, off]], axis=0)
    return s


def _attn_cfg(mode, d):
    if mode == "dil":
        mlen = SEQ // d
        return dict(pair=True, tq=128, tk=min(256, mlen), mlen=mlen, lk=SEQ, scale=HEAD_DIM ** -0.5, units=4,
                    nsub=ATTN_SUBTILES)
    if mode == "na":
        return dict(pair=True, tq=GRID_W, tk=NA_ROWS * GRID_W, mlen=SEQ, lk=SEQ, scale=HEAD_DIM ** -0.5, units=4,
                    nsub=ATTN_SUBTILES)
    return dict(pair=False, tq=128, tk=MEM_LEN, mlen=SEQ, lk=MEM_LEN, scale=128 ** -0.5, units=4,
                nsub=ATTN_SUBTILES)


ATTN_SUBTILES = 16


def _attn_fwd(name, mode, q_arr, k_arr, v_arr, qcol, kcol, vcol, d=1, bias=None):
    cfg = _attn_cfg(mode, d)
    pair, tq, tk, mlen, lk, scale = cfg["pair"], cfg["tq"], cfg["tk"], cfg["mlen"], cfg["lk"], cfg["scale"]
    qscale, sscale = (scale, 1.0) if pair else (1.0, scale)
    nsub = cfg["nsub"]
    rows = nsub * tq

    def body(*refs):
        if mode == "na":
            q_ref, k_ref, v_ref, bias_ref, o_ref, l_ref = refs
        else:
            q_ref, k_ref, v_ref, o_ref, l_ref = refs
            bias_ref = None
        lanes = _iota((tq, 128), 1)
        qis = [pl.program_id(1) * nsub + sub for sub in range(nsub)]
        kss = [_window(mode, qi, tq, mlen, tk) for qi in qis]
        vs = [v_ref[pl.ds(ks, tk), :] for ks in kss]
        bands = [_band_mask(qi, tq, tk, ks) if mode == "dil" else None for qi, ks in zip(qis, kss)]
        ss = []
        for sub in range(nsub):
            qst = _stack_heads(q_ref[sub * tq:(sub + 1) * tq, :], lanes, pair, qscale)
            k = k_ref[pl.ds(kss[sub], tk), :]
            ss.append(_scores(mode, qst, k, sscale, bands[sub], qis[sub], bias_ref, pair))
        ms = [jnp.max(s_, axis=1, keepdims=True) for s_ in ss]
        ps = [jnp.exp(s_ - m) for s_, m in zip(ss, ms)]
        ls = [jnp.sum(p, axis=1, keepdims=True) for p in ps]
        os_ = [jnp.dot(p.astype(BF16), v, preferred_element_type=F32) for p, v in zip(ps, vs)]
        for sub in range(nsub):
            out = _unstack_heads(os_[sub] / ls[sub], lanes, pair, tq)
            lse = ms[sub] + jnp.log(ls[sub])
            lse = _unstack_heads(jnp.broadcast_to(lse, (lse.shape[0], 128)), lanes, pair, tq)
            dst = _folded_rows(qis[sub] * tq, tq, d) if mode == "dil" else slice(sub * tq, (sub + 1) * tq)
            o_ref[dst, :] = out
            l_ref[dst, :] = lse

    in_specs = [pl.BlockSpec((rows, 128), lambda u, i: (i, qcol + u)),
                pl.BlockSpec((lk, 128), lambda u, i: (0, kcol + u)),
                pl.BlockSpec((lk, 128), lambda u, i: (0, vcol + u))]
    args = [q_arr, k_arr, v_arr]
    if mode == "na":
        in_specs.append(pl.BlockSpec((2, NA_ROWS, GRID_W, NA_ROWS * GRID_W), lambda u, i: (u, 0, 0, 0)))
        args.append(bias)
    if mode == "dil":
        out_spec = pl.BlockSpec((SEQ, 128), lambda u, i: (0, u))
    else:
        out_spec = pl.BlockSpec((rows, 128), lambda u, i: (i, u))
    return pl.pallas_call(
        body, name=name, grid=(cfg["units"], SEQ // rows), in_specs=in_specs, out_specs=[out_spec, out_spec],
        out_shape=[jax.ShapeDtypeStruct((SEQ, 512), F32), jax.ShapeDtypeStruct((SEQ, 512), F32)],
        compiler_params=_params(("parallel", "arbitrary")))(*args)


def _attn_bwd(name, mode, q_arr, k_arr, v_arr, qcol, kcol, vcol, do, lse, dp=None, o=None, d=1, bias=None,
              tabs=None):
    cfg = _attn_cfg(mode, d)
    pair, tq, tk, mlen, lk, scale = cfg["pair"], cfg["tq"], cfg["tk"], cfg["mlen"], cfg["lk"], cfg["scale"]
    qscale, sscale = (scale, 1.0) if pair else (1.0, scale)
    nsub = cfg["nsub"]
    rows = nsub * tq
    nq = SEQ // rows
    kv_dtype = F32 if mode == "mem" else BF16

    def body(*refs):
        refs = list(refs)
        q_ref, k_ref, v_ref, do_ref, l_ref = refs[:5]
        rest = refs[5:]
        bias_ref = tq_ref = tk_ref = db_ref = None
        if mode == "dil":
            dp_ref, tq_ref, tk_ref, dq_ref, dk_ref, dv_ref, dk_acc, dv_acc = rest
        elif mode == "na":
            o_ref, bias_ref, dq_ref, dk_ref, dv_ref, db_ref, dk_acc, dv_acc = rest
        else:
            o_ref, dq_ref, dk_ref, dv_ref, dk_acc, dv_acc = rest
        step = pl.program_id(1)

        @pl.when(step == 0)
        def _():
            dk_acc[...] = jnp.zeros((lk, 128), F32)
            dv_acc[...] = jnp.zeros((lk, 128), F32)
            if mode == "na":
                db_ref[...] = jnp.zeros(db_ref.shape, F32)

        lanes = _iota((tq, 128), 1)
        qis = [step * nsub + sub for sub in range(nsub)]
        sls = [slice(sub * tq, (sub + 1) * tq) for sub in range(nsub)]
        kss = [_window(mode, qi, tq, mlen, tk) for qi in qis]
        ks_ = [k_ref[pl.ds(ks, tk), :] for ks in kss]
        vs = [v_ref[pl.ds(ks, tk), :] for ks in kss]
        qsts, dosts, lses, dphs = [], [], [], []
        for sub in range(nsub):
            if mode == "dil":
                src = _folded_rows(qis[sub] * tq, tq, d)
                dov = do_ref[src, :].astype(BF16)
                lsev = l_ref[src, :]
                dphs.append(_stack_rows(dp_ref[src, :], lanes, pair))
            else:
                dov = do_ref[sls[sub], :]
                lsev = l_ref[sls[sub], :]
                dpv = dov.astype(F32) * o_ref[sls[sub], :]
                if pair:
                    dphs.append(jnp.concatenate(
                        [jnp.sum(jnp.where(_head_lanes(lanes, hh), dpv, 0.0), axis=1, keepdims=True)
                         for hh in range(2)], axis=0))
                else:
                    dphs.append(jnp.sum(dpv, axis=1, keepdims=True))
            qsts.append(_stack_heads(q_ref[sls[sub], :], lanes, pair, qscale))
            dosts.append(_stack_heads(dov, lanes, pair))
            lses.append(_stack_rows(lsev, lanes, pair))
        bands = [_band_mask(qi, tq, tk, ks) if mode == "dil" else None for qi, ks in zip(qis, kss)]
        ss = [_scores(mode, qsts[sub], ks_[sub], sscale, bands[sub], qis[sub], bias_ref, pair) for sub in range(nsub)]
        dpms = [lax.dot_general(dosts[sub], vs[sub], NT, preferred_element_type=F32) for sub in range(nsub)]
        ps = [jnp.exp(s_ - lse) for s_, lse in zip(ss, lses)]
        dss = [p * (dpm - dph) for p, dpm, dph in zip(ps, dpms, dphs)]
        if mode == "na":
            for sub, ds in enumerate(dss):
                off = qis[sub] - jnp.clip(qis[sub] - NA_ROWS // 2, 0, SEQ // GRID_W - NA_ROWS)
                db_ref[0, off] += ds[:tq]
                db_ref[1, off] += ds[tq:]
        dsbs = [ds.astype(BF16) for ds in dss]
        dvs = [lax.dot_general(p.astype(BF16), dosts[sub], TN, preferred_element_type=F32)
               for sub, p in enumerate(ps)]
        dqs = [jnp.dot(dsb, ks_[sub], preferred_element_type=F32) * scale for sub, dsb in enumerate(dsbs)]
        dks = [lax.dot_general(dsb, qsts[sub], TN, preferred_element_type=F32) for sub, dsb in enumerate(dsbs)]
        for sub in range(nsub):
            sl = sls[sub]
            dq = _unstack_heads(dqs[sub], lanes, pair, tq)
            if mode == "dil":
                dq = _rope_t(dq, tq_ref[0, sl, :], tq_ref[1, sl, :], tq_ref[2, sl, :])
            dq_ref[sl, :] = dq.astype(BF16)
            dk_acc[pl.ds(kss[sub], tk), :] += dks[sub] if pair else dks[sub] * scale
            dv_acc[pl.ds(kss[sub], tk), :] += dvs[sub]

        @pl.when(step == nq - 1)
        def _():
            dkv = dk_acc[...]
            if mode == "dil":
                dkv = _rope_t(dkv, tk_ref[0], tk_ref[1], tk_ref[2])
            dk_ref[...] = dkv.astype(kv_dtype)
            dv_ref[...] = dv_acc[...].astype(kv_dtype)

    q_spec = pl.BlockSpec((rows, 128), lambda u, i: (i, qcol + u))
    row_spec = pl.BlockSpec((rows, 128), lambda u, i: (i, u))
    kv_out = pl.BlockSpec((lk, 128), lambda u, i: (0, u))
    whole = pl.BlockSpec((SEQ, 128), lambda u, i: (0, u))
    nat_spec = whole if mode == "dil" else row_spec
    in_specs = [q_spec,
                pl.BlockSpec((lk, 128), lambda u, i: (0, kcol + u)),
                pl.BlockSpec((lk, 128), lambda u, i: (0, vcol + u)),
                nat_spec, nat_spec]
    args = [q_arr, k_arr, v_arr, do, lse]
    out_specs = [row_spec, kv_out, kv_out]
    out_shape = [jax.ShapeDtypeStruct((SEQ, 512), BF16), jax.ShapeDtypeStruct((lk, 512), kv_dtype),
                 jax.ShapeDtypeStruct((lk, 512), kv_dtype)]
    if mode == "dil":
        in_specs += [whole, pl.BlockSpec((3, rows, 128), lambda u, i: (0, i, 0)),
                     pl.BlockSpec((3, SEQ, 128), lambda u, i: (0, 0, 0))]
        args += [dp, tabs, tabs]
    elif mode == "na":
        b_spec = pl.BlockSpec((2, NA_ROWS, GRID_W, NA_ROWS * GRID_W), lambda u, i: (u, 0, 0, 0))
        in_specs += [row_spec, b_spec]
        args += [o, bias]
        out_specs.append(b_spec)
        out_shape.append(jax.ShapeDtypeStruct((8, NA_ROWS, GRID_W, NA_ROWS * GRID_W), F32))
    else:
        in_specs.append(row_spec)
        args.append(o)
    return pl.pallas_call(
        body, name=name, grid=(cfg["units"], nq), in_specs=in_specs, out_specs=out_specs, out_shape=out_shape,
        scratch_shapes=[pltpu.VMEM((lk, 128), F32), pltpu.VMEM((lk, 128), F32)],
        compiler_params=_params(("parallel", "arbitrary")))(*args)


def _na_geometry():
    qc = _iota((GRID_W, 128), 0)
    lane = _iota((GRID_W, 128), 1)
    kc = lane & 63
    c_start = jnp.clip(qc - 8, 0, GRID_W - 16)
    valid = jnp.logical_and(kc >= c_start, kc < c_start + 16)
    return lane, valid


def _na_bias(rpb_rows):
    def body(r_ref, o_ref, t_ref):
        lane, valid = _na_geometry()
        for dd in range(14):
            row_a = jnp.broadcast_to(r_ref[dd:dd + 1, :], (GRID_W, 128))
            row_b = jnp.broadcast_to(r_ref[dd + 1:dd + 2, :], (GRID_W, 128))
            both = jnp.where(lane < 64, row_a, pltpu.roll(row_b, 64, 1))
            t = pltpu.roll(both, 128 - 15, 1, stride=1, stride_axis=0)
            t_ref[dd] = jnp.where(valid, t, NEG)
        for off in range(NA_ROWS):
            for p in range(4):
                o_ref[off, :, p * 128:(p + 1) * 128] = t_ref[2 * p - off + 7]

    return pl.pallas_call(
        body, name="na_bias", grid=(8,),
        in_specs=[pl.BlockSpec((None, 16, 128), lambda h: (h, 0, 0))],
        out_specs=pl.BlockSpec((None, NA_ROWS, GRID_W, NA_ROWS * GRID_W), lambda h: (h, 0, 0, 0)),
        out_shape=jax.ShapeDtypeStruct((8, NA_ROWS, GRID_W, NA_ROWS * GRID_W), F32),
        scratch_shapes=[pltpu.VMEM((14, GRID_W, 128), F32)],
        compiler_params=_params(("parallel",)))(rpb_rows)


def _na_bias_bwd(dbias):
    def body(d_ref, o_ref):
        lane, valid = _na_geometry()
        reverse = (_iota((GRID_W, GRID_W), 0) + _iota((GRID_W, GRID_W), 1) == GRID_W - 1).astype(F32)
        o_ref[...] = jnp.zeros((16, 128), F32)
        for dd in range(14):
            t = jnp.zeros((GRID_W, 128), F32)
            for off in range(NA_ROWS):
                for p in range(4):
                    if 2 * p - off + 7 == dd:
                        t = t + d_ref[off, :, p * 128:(p + 1) * 128]
            t = jnp.dot(reverse, jnp.where(valid, t, 0.0), precision=lax.Precision.HIGHEST,
                        preferred_element_type=F32)
            t = pltpu.roll(t, 128 - (GRID_W - 16), 1, stride=1, stride_axis=0)
            o_ref[dd:dd + 1, :] = jnp.sum(t, axis=0, keepdims=True)

    return pl.pallas_call(
        body, name="na_bias_bwd", grid=(8,),
        in_specs=[pl.BlockSpec((None, NA_ROWS, GRID_W, NA_ROWS * GRID_W), lambda h: (h, 0, 0, 0))],
        out_specs=pl.BlockSpec((None, 16, 128), lambda h: (h, 0, 0)),
        out_shape=jax.ShapeDtypeStruct((8, 16, 128), F32),
        compiler_params=_params(("parallel",)))(dbias)


GATE_ROWS = 128


def _group_weights(l0, l1, l2):
    m = jnp.maximum(jnp.maximum(l0, l1), l2)
    e0, e1, e2 = jnp.exp(l0 - m), jnp.exp(l1 - m), jnp.exp(l2 - m)
    inv = 1.0 / (e0 + e1 + e2)
    return e0 * inv, e1 * inv, e2 * inv


def _gate_specs():
    r512 = pl.BlockSpec((GATE_ROWS, 512), lambda i: (i, 0))
    r1024 = pl.BlockSpec((GATE_ROWS, D_MODEL), lambda i: (i, 0))
    silu_cols = [pl.BlockSpec((GATE_ROWS, 512), functools.partial(lambda b, i: (i, b), 13 + b)) for b in range(3)]
    logit_cols = [pl.BlockSpec((GATE_ROWS, D_MODEL), functools.partial(lambda b, i: (i, b), 8 + b)) for b in range(3)]
    return r512, r1024, silu_cols, logit_cols


def _gate_fwd(o_grp, l_grp, out_b, out_c, parts, merge_bias, wts):
    r512, r1024, silu_cols, logit_cols = _gate_specs()

    def body(o0, o1, o2, l0, l1, l2, ob, oc, ga, gb, gc, la, lb, lc, mb, wa, wb, wc,
             oa_ref, ua, ub, uc, za, zb, zc, y_ref):
        w0, w1, w2 = _group_weights(l0[...], l1[...], l2[...])
        out_a = w0 * o0[...] + w1 * o1[...] + w2 * o2[...]
        oa_ref[...] = out_a
        y = jnp.zeros((GATE_ROWS, D_MODEL), F32)
        for b, (ov, g_ref, l_ref, w_ref, u_ref, z_ref) in enumerate(
                ((out_a, ga, la, wa, ua, za), (ob[...], gb, lb, wb, ub, zb), (oc[...], gc, lc, wc, uc, zc))):
            g = g_ref[...].astype(F32)
            u = (ov * (g * _sigmoid(g))).astype(BF16)
            u_ref[...] = u
            z = lax.dot_general(u, w_ref[...], NT, preferred_element_type=F32)
            z_ref[...] = z.astype(BF16)
            gate = _sigmoid(l_ref[...].astype(F32) + mb[b:b + 1, :])
            y = y + gate * z
        y_ref[...] = y.astype(BF16)

    full = lambda shape: pl.BlockSpec(shape, lambda i: (0,) * len(shape))
    in_specs = ([r512] * 8 + silu_cols + logit_cols
                + [full((3, D_MODEL))] + [full((D_MODEL, 512))] * 3)
    out_specs = [r512] * 4 + [r1024] * 4
    out_shape = ([jax.ShapeDtypeStruct((SEQ, 512), F32)] + [jax.ShapeDtypeStruct((SEQ, 512), BF16)] * 3
                 + [jax.ShapeDtypeStruct((SEQ, D_MODEL), BF16)] * 4)
    res = pl.pallas_call(
        body, name="gate_fwd", grid=(SEQ // GATE_ROWS,), in_specs=in_specs, out_specs=out_specs,
        out_shape=out_shape, compiler_params=_params(("parallel",)))(
            *o_grp, *l_grp, out_b, out_c, parts, parts, parts, parts, parts, parts, merge_bias, *wts)
    return res[0], res[1:4], res[4:7], res[7]


def _gate_bwd(dy, z, parts, merge_bias, outs, o_grp, l_grp, wts, head_sum):
    r512, r1024, silu_cols, logit_cols = _gate_specs()

    def body(dy_ref, za, zb, zc, la, lb, lc, mb, oa, ob, oc, ga, gb, gc, o0, o1, o2, l0, l1, l2, wa, wb, wc, hs_ref,
             dla, dlb, dlc, gmb, dza, dzb, dzc, dga, dgb, dgc, do0, do1, do2, dp0, dp1, dp2, dob, doc):
        dyv = dy_ref[...].astype(F32)
        rows = []
        dos = []
        for b, (z_ref, l_ref, ov_ref, g_ref, w_ref, dl_ref, dz_ref, dg_ref) in enumerate(
                ((za, la, oa, ga, wa, dla, dza, dga), (zb, lb, ob, gb, wb, dlb, dzb, dgb),
                 (zc, lc, oc, gc, wc, dlc, dzc, dgc))):
            gate = _sigmoid(l_ref[...].astype(F32) + mb[b:b + 1, :])
            dl = dyv * z_ref[...].astype(F32) * gate * (1.0 - gate)
            dl_ref[...] = dl.astype(BF16)
            rows.append(jnp.sum(dl, axis=0, keepdims=True))
            dz = (dyv * gate).astype(BF16)
            dz_ref[...] = dz
            du = jnp.dot(dz, w_ref[...], preferred_element_type=F32)
            g = g_ref[...].astype(F32)
            sg = _sigmoid(g)
            dos.append(du * (g * sg))
            dg_ref[...] = (du * ov_ref[...] * (sg * (1.0 + g * (1.0 - sg)))).astype(BF16)

        @pl.when(pl.program_id(0) == 0)
        def _():
            gmb[...] = jnp.zeros((3, D_MODEL), F32)

        for b in range(3):
            gmb[b:b + 1, :] += rows[b]
        dob[...] = dos[1].astype(BF16)
        doc[...] = dos[2].astype(BF16)
        doa = dos[0]
        row_term = jnp.dot(doa * oa[...], hs_ref[...], precision=lax.Precision.HIGHEST, preferred_element_type=F32)
        ws = _group_weights(l0[...], l1[...], l2[...])
        for wg, do_ref, dp_ref in zip(ws, (do0, do1, do2), (dp0, dp1, dp2)):
            do_ref[...] = wg * doa
            dp_ref[...] = wg * row_term

    full = lambda shape: pl.BlockSpec(shape, lambda i: (0,) * len(shape))
    acc = pl.BlockSpec((3, D_MODEL), lambda i: (0, 0))
    in_specs = ([r1024] * 4 + logit_cols + [full((3, D_MODEL))] + [r512] * 3 + silu_cols + [r512] * 6
                + [full((D_MODEL, 512))] * 3 + [full((512, 512))])
    out_specs = [r1024] * 3 + [acc] + [r1024] * 3 + [r512] * 11
    out_shape = ([jax.ShapeDtypeStruct((SEQ, D_MODEL), BF16)] * 3 + [jax.ShapeDtypeStruct((3, D_MODEL), F32)]
                 + [jax.ShapeDtypeStruct((SEQ, D_MODEL), BF16)] * 3 + [jax.ShapeDtypeStruct((SEQ, 512), BF16)] * 3
                 + [jax.ShapeDtypeStruct((SEQ, 512), F32)] * 6 + [jax.ShapeDtypeStruct((SEQ, 512), BF16)] * 2)
    res = pl.pallas_call(
        body, name="gate_bwd", grid=(SEQ // GATE_ROWS,), in_specs=in_specs, out_specs=out_specs,
        out_shape=out_shape, compiler_params=_params(("arbitrary",)))(
            dy, *z, parts, parts, parts, merge_bias, *outs, parts, parts, parts, *o_grp, *l_grp, *wts, head_sum)
    return res[0:3], res[3], res[4:7], res[7:10], res[10:13], res[13:16], res[16], res[17]


def _post(y2, x, target, gain):
    rows = 256

    def body(y_ref, x_ref, t_ref, g_ref, do_ref, dy_ref, l_ref, gg_ref):
        yv = y_ref[...]
        rstd = lax.rsqrt(jnp.mean(yv * yv, axis=1, keepdims=True) + EPS)
        yn = yv * rstd
        gv = g_ref[...]
        err = x_ref[...] + yn * gv - t_ref[...]
        dout = err * (1.0 / D_MODEL)
        do_ref[...] = dout
        dn = dout * gv
        dy_ref[...] = (rstd * (dn - yn * jnp.mean(dn * yn, axis=1, keepdims=True))).astype(BF16)

        @pl.when(pl.program_id(0) == 0)
        def _():
            l_ref[...] = jnp.zeros((1, D_MODEL), F32)
            gg_ref[...] = jnp.zeros((1, D_MODEL), F32)

        l_ref[...] += jnp.sum(err * err, axis=0, keepdims=True)
        gg_ref[...] += jnp.sum(dout * yn, axis=0, keepdims=True)

    row = pl.BlockSpec((rows, D_MODEL), lambda i: (i, 0))
    vec = pl.BlockSpec((1, D_MODEL), lambda i: (0, 0))
    return pl.pallas_call(
        body, name="post", grid=(SEQ // rows,), in_specs=[row, row, row, vec], out_specs=[row, row, vec, vec],
        out_shape=[jax.ShapeDtypeStruct((SEQ, D_MODEL), F32), jax.ShapeDtypeStruct((SEQ, D_MODEL), BF16),
                   jax.ShapeDtypeStruct((1, D_MODEL), F32), jax.ShapeDtypeStruct((1, D_MODEL), F32)],
        compiler_params=_params(("arbitrary",)))(y2, x, target, gain)


def _local_step(x, mem, target, pre_norm, mem_norm, post_norm, na_rpb, wt_in, late_weights, dep_in=None,
                reduce_start=None):
    tabs = _rope_tables()
    hs, hst = _prenorm_fold(x, pre_norm)
    parts = _in_proj(hs, wt_in, tabs, dep_in)

    o_grp, l_grp = [], []
    for g, d in enumerate(DILATIONS):
        o, l = _attn_fwd("dil_fwd_%d" % g, "dil", parts, parts, parts, 12 * g, 12 * g + 4, 12 * g + 8, d=d)
        o_grp.append(o)
        l_grp.append(l)
    bias = _na_bias(jnp.pad(na_rpb, ((0, 0), (0, 1), (0, 128 - 31))))
    out_b, lse_b = _attn_fwd("na_fwd", "na", parts, parts, parts, 36, 40, 44, bias=bias)
    merge_bias, w_kv, wt_a, wt_b, wt_c, w_out = late_weights(out_b)
    memn = _rmsnorm_fwd("memnorm", mem, mem_norm, MEM_LEN)
    kv_m = _mm_simple("mem_kv", memn, w_kv, NN, BF16, MEM_LEN, 512, D_MODEL)
    out_c, lse_c = _attn_fwd("mem_fwd", "mem", parts, kv_m, kv_m, 48, 0, 4)

    wts = (wt_a, wt_b, wt_c)
    out_a, u, z, y = _gate_fwd(o_grp, l_grp, out_b, out_c, parts, merge_bias, wts)
    y2 = _mm_simple("out_proj", y, w_out, NN, F32, 512, D_MODEL, D_MODEL)
    dout, dy2, err_sq, g_post = _post(y2, x, target, post_norm)
    loss = 0.5 * jnp.sum(err_sq) / D_MODEL

    dy = _mm_simple("out_proj_dx", dy2, w_out, NT, BF16, 512, D_MODEL, D_MODEL)
    g_w_out = _mm_simple("out_proj_dw", y, dy2, TN, BF16, D_MODEL, 512, 512)

    rr = _iota((512, 512), 0) // HEAD_DIM
    cc = _iota((512, 512), 1) // HEAD_DIM
    head_sum = (rr == cc).astype(F32)
    dlog, g_mb, dz, dg, do_grp, dp_grp, do_b, do_c = _gate_bwd(
        dy, z, parts, merge_bias, (out_a, out_b, out_c), o_grp, l_grp, wts, head_sum)
    g_wt = [_mm_simple("branch_dw_%d" % b, dz[b], u[b], TN, BF16, D_MODEL, 512, 512) for b in range(3)]

    dqkv = []
    for g, d in enumerate(DILATIONS):
        dq, dk, dv = _attn_bwd("dil_bwd_%d" % g, "dil", parts, parts, parts, 12 * g, 12 * g + 4, 12 * g + 8,
                               do_grp[g], l_grp[g], dp=dp_grp[g], d=d, tabs=tabs[g])
        dqkv += [dq, dk, dv]
    dq_b, dk_b, dv_b, dbias = _attn_bwd("na_bwd", "na", parts, parts, parts, 36, 40, 44, do_b, lse_b, o=out_b,
                                        bias=bias)
    g_rpb_t = _na_bias_bwd(dbias)
    g_rpb = g_rpb_t[:, :15, :31] + jnp.pad(g_rpb_t[:, :14, 64:95], ((0, 0), (1, 0), (0, 0)))
    dq_c, dk_m, dv_m = _attn_bwd("mem_bwd", "mem", parts, kv_m, kv_m, 48, 0, 4, do_c, lse_c, o=out_c)

    dkv = jnp.concatenate([dk_m, dv_m], axis=1).astype(BF16)
    g_w_kv = _mm_simple("mem_kv_dw", memn, dkv, TN, BF16, D_MODEL, 512, MEM_LEN)
    dmemn = _mm_simple("mem_kv_dx", dkv, w_kv, NT, F32, MEM_LEN, 512, D_MODEL)
    g_mem_norm = _memnorm_bwd(mem, dmemn)

    grads = dict(w_kv=g_w_kv, wt_a=g_wt[0], wt_b=g_wt[1], wt_c=g_wt[2], w_out=g_w_out, merge_bias=g_mb,
                 mem_norm=g_mem_norm, post_norm=g_post, na_rpb=g_rpb)
    dep = reduce_start(grads) if reduce_start is not None else None
    dparts = dqkv + [dq_b, dk_b, dv_b, dq_c] + list(dg) + list(dlog)
    grads["wt_in"] = _in_proj_dw(dparts, hst, dep)
    dep = reduce_start(grads) if reduce_start is not None else None
    dh = _in_proj_dh(dparts, wt_in, dep)
    grad_x, grads["pre_norm"] = _prenorm_bwd(x, pre_norm, dh, dout)
    return loss, grad_x, grads


ANY = pl.BlockSpec(memory_space=pl.ANY)


def _place():
    return lax.axis_index("x"), lax.axis_index("y"), lax.axis_index("c")


def _all_gather(shard):
    r = shard.shape[0]
    half = r // 2

    def body(src, out, send_sems, recv_sems, local_sem):
        x, y, c = _place()
        me, sib = (x, y, c), (x, y, 1 - c)
        xn, yn, dg = (1 - x, y, c), (x, 1 - y, c), (1 - x, 1 - y, c)

        def rows(dev, part=None):
            blk = out.at[4 * dev[0] + 2 * dev[1] + dev[2]]
            return blk if part is None else blk.at[pl.ds(part * half, half)]

        def copy(k, dev, part, to, own=False):
            return pltpu.make_async_remote_copy(
                src_ref=src if own else rows(dev, part), dst_ref=rows(dev, part),
                send_sem=send_sems.at[k], recv_sem=recv_sems.at[k], device_id=to, device_id_type=MESH_ID)

        def other(dev):
            return (dev[0], dev[1], 1 - dev[2])

        mine = pltpu.make_async_copy(src, rows(me), local_sem)
        mine.start()
        sent = [copy(0, me, None, sib, own=True), copy(1, me, None, xn, own=True), copy(2, me, None, yn, own=True)]
        for cp in sent:
            cp.start()
        copy(1, xn, None, me).wait_recv()
        sent += [copy(3, xn, 0, yn), copy(5, xn, None, sib)]
        sent[-2].start()
        sent[-1].start()
        copy(2, yn, None, me).wait_recv()
        sent += [copy(4, yn, 1, xn), copy(6, yn, None, sib)]
        sent[-2].start()
        sent[-1].start()
        copy(3, dg, 0, me).wait_recv()
        sent.append(copy(7, dg, 0, sib))
        sent[-1].start()
        copy(4, dg, 1, me).wait_recv()
        sent.append(copy(8, dg, 1, sib))
        sent[-1].start()
        copy(0, sib, None, me).wait_recv()
        copy(5, other(xn), None, me).wait_recv()
        copy(6, other(yn), None, me).wait_recv()
        copy(7, other(dg), 0, me).wait_recv()
        copy(8, other(dg), 1, me).wait_recv()
        for cp in sent:
            cp.wait_send()
        mine.wait()

    return pl.pallas_call(
        body, name="all_gather", in_specs=[ANY], out_specs=ANY,
        out_shape=jax.ShapeDtypeStruct((N_DEV,) + shard.shape, shard.dtype),
        scratch_shapes=[pltpu.SemaphoreType.DMA((9,)), pltpu.SemaphoreType.DMA((9,)), pltpu.SemaphoreType.DMA])(shard)


def _exchange_sibling(name, terms):
    nt = len(terms)

    def body(*refs):
        srcs, outs = refs[:nt], refs[nt:2 * nt]
        send_sems, recv_sems = refs[2 * nt:]
        x, y, c = _place()
        copies = []
        for q in range(4):
            for t in range(nt):
                copies.append(pltpu.make_async_remote_copy(
                    src_ref=srcs[t].at[2 * q + 1 - c], dst_ref=outs[t].at[q],
                    send_sem=send_sems.at[q * nt + t], recv_sem=recv_sems.at[q * nt + t],
                    device_id=(x, y, 1 - c), device_id_type=MESH_ID))
        for cp in copies:
            cp.start()
        for cp in copies:
            cp.wait()

    return pl.pallas_call(
        body, name=name, in_specs=[ANY] * nt, out_specs=[ANY] * nt,
        out_shape=[jax.ShapeDtypeStruct((4,) + s.shape[1:], s.dtype) for s in terms],
        scratch_shapes=[pltpu.SemaphoreType.DMA((4 * nt,)), pltpu.SemaphoreType.DMA((4 * nt,))])(*terms)


HBM = pl.BlockSpec(memory_space=pltpu.HBM)
SEM = pl.BlockSpec(memory_space=pltpu.SEMAPHORE)
DATAFLOW = pltpu.SideEffectType.DATAFLOW_SIDE_EFFECTING


def _split_copies(kind, srcs, lands, send_sems, recv_sems):
    nt = len(srcs)
    x, y, c = _place()
    copies = []
    if kind == "gather":
        me = 4 * x + 2 * y + c
        for mask in range(1, 8):
            fx, fy, fc = (mask >> 2) & 1, (mask >> 1) & 1, mask & 1
            to = (1 - x if fx else x, 1 - y if fy else y, 1 - c if fc else c)
            for t in range(nt):
                k = (mask - 1) * nt + t
                copies.append(pltpu.make_async_remote_copy(
                    src_ref=srcs[t], dst_ref=lands[t].at[me], send_sem=send_sems.at[k], recv_sem=recv_sems.at[k],
                    device_id=to, device_id_type=MESH_ID))
    else:
        for s, (tx, ty) in enumerate([(1 - x, y), (x, 1 - y), (1 - x, 1 - y)]):
            for t in range(nt):
                k = s * nt + t
                copies.append(pltpu.make_async_remote_copy(
                    src_ref=srcs[t].at[2 * tx + ty], dst_ref=lands[t].at[s], send_sem=send_sems.at[k],
                    recv_sem=recv_sems.at[k], device_id=(tx, ty, c), device_id_type=MESH_ID))
    return copies


def _split_count(kind, nt):
    return (7 if kind == "gather" else 3) * nt


def _exchange_start(name, kind, srcs, land_shapes, after=None):
    nt = len(srcs)
    n = _split_count(kind, nt)
    dep_specs, dep_args = _dep_operand(after)
    nd = len(dep_args)

    def body(*refs):
        src_refs, land_refs = refs[:nt], refs[nt:2 * nt]
        send_sems, recv_sems = refs[2 * nt + nd], refs[2 * nt + nd + 1]
        token = refs[-1]
        for cp in _split_copies(kind, src_refs, land_refs, send_sems, recv_sems):
            cp.start()
        token[...] = jnp.zeros_like(token)

    lands = [pltpu.with_memory_space_constraint(lax.empty(s.shape, s.dtype), pltpu.HBM) for s in land_shapes]
    res = pl.pallas_call(
        body, name=name,
        out_shape=(pltpu.SemaphoreType.DMA((n,)), pltpu.SemaphoreType.DMA((n,)),
                   *[pltpu.HBM(s.shape, s.dtype) for s in srcs], *[pltpu.HBM(s.shape, s.dtype) for s in land_shapes],
                   jax.ShapeDtypeStruct((8, 128), F32)),
        in_specs=[HBM] * (2 * nt) + dep_specs,
        out_specs=(SEM, SEM, *([HBM] * (2 * nt)), pl.BlockSpec(memory_space=pltpu.VMEM)),
        input_output_aliases={i: 2 + i for i in range(2 * nt)},
        compiler_params=pltpu.CompilerParams(has_side_effects=DATAFLOW))(
            *[pltpu.with_memory_space_constraint(s, pltpu.HBM) for s in srcs], *lands, *dep_args)
    return res[0], res[1], list(res[2:2 + nt]), list(res[2 + nt:2 + 2 * nt]), res[-1]


def _exchange_wait(name, kind, send_sems, recv_sems, srcs, lands, after):
    nt = len(srcs)

    def body(*refs):
        src_refs, land_refs = refs[:nt], refs[nt:2 * nt]
        s_sems, r_sems = refs[2 * nt], refs[2 * nt + 1]
        for cp in _split_copies(kind, src_refs, land_refs, s_sems, r_sems):
            cp.wait_send()
            cp.wait_recv()

    res = pl.pallas_call(
        body, name=name,
        out_shape=tuple(pltpu.HBM(s.shape, s.dtype) for s in list(srcs) + list(lands)),
        in_specs=[HBM] * (2 * nt) + [SEM, SEM, pl.BlockSpec(memory_space=pl.ANY)],
        out_specs=tuple([HBM] * (2 * nt)),
        input_output_aliases={i: i for i in range(2 * nt)},
        compiler_params=pltpu.CompilerParams(has_side_effects=DATAFLOW))(
            *srcs, *lands, send_sems, recv_sems, after)
    return list(res[:nt]), list(res[nt:])


def _add_sibling(name, term, recv, rows):
    _, r, w = term.shape
    cidx = lax.axis_index("c").astype(jnp.int32).reshape(1)

    def body(c_ref, a_ref, b_ref, o_ref):
        o_ref[...] = (a_ref[...].astype(F32) + b_ref[...].astype(F32)).astype(o_ref.dtype)

    grid_spec = pltpu.PrefetchScalarGridSpec(
        num_scalar_prefetch=1, grid=(4, r // rows),
        in_specs=[pl.BlockSpec((None, rows, w), lambda q, i, c_ref: (2 * q + c_ref[0], i, 0)),
                  pl.BlockSpec((None, rows, w), lambda q, i, c_ref: (q, i, 0))],
        out_specs=pl.BlockSpec((None, rows, w), lambda q, i, c_ref: (q, i, 0)))
    return pl.pallas_call(
        body, name=name, grid_spec=grid_spec, out_shape=jax.ShapeDtypeStruct((4, r, w), term.dtype),
        compiler_params=_params(("parallel", "parallel")))(cidx, term, recv)


def _add_sibling_small(name, terms, recvs):
    nt = len(terms)

    def body(*refs):
        c = lax.axis_index("c")
        for t_ref, r_ref, o_ref in zip(refs[:nt], refs[nt:2 * nt], refs[2 * nt:]):
            for q in range(4):
                o_ref[q] = (t_ref[2 * q + c].astype(F32) + r_ref[q].astype(F32)).astype(o_ref.dtype)

    return pl.pallas_call(
        body, name=name, out_shape=[jax.ShapeDtypeStruct((4,) + t.shape[1:], t.dtype) for t in terms],
        compiler_params=_params())(*terms, *recvs)


def _add_chips(name, sums, recv, rows):
    _, r, w = sums.shape
    qidx = (2 * lax.axis_index("x") + lax.axis_index("y")).astype(jnp.int32).reshape(1)

    def body(q_ref, a_ref, b_ref, o_ref):
        o_ref[...] = ((a_ref[...].astype(F32) + b_ref[0].astype(F32))
                      + (b_ref[1].astype(F32) + b_ref[2].astype(F32)))

    grid_spec = pltpu.PrefetchScalarGridSpec(
        num_scalar_prefetch=1, grid=(r // rows,),
        in_specs=[pl.BlockSpec((None, rows, w), lambda i, q_ref: (q_ref[0], i, 0)),
                  pl.BlockSpec((3, rows, w), lambda i, q_ref: (0, i, 0))],
        out_specs=pl.BlockSpec((rows, w), lambda i, q_ref: (i, 0)))
    return pl.pallas_call(
        body, name=name, grid_spec=grid_spec, out_shape=jax.ShapeDtypeStruct((r, w), F32),
        compiler_params=_params(("parallel",)))(qidx, sums, recv)


def _rs_rows(a):
    return SHARD_IN // 4 if a.shape[1] == SHARD_IN else a.shape[1]


def _reduce_scatter_start(tag, names, terms):
    recv1 = _exchange_sibling("exchange_sibling_" + tag, terms)
    if len(terms) == 1:
        sums = [_add_sibling("add_sibling_" + names[0], terms[0], recv1[0], _rs_rows(terms[0]))]
    else:
        sums = _add_sibling_small("add_sibling_" + tag, terms, recv1)
    lands =[jax.ShapeDtypeStruct((3,) + s.shape[1:], s.dtype) for s in sums]
    send_sems, recv_sems, sums, lands, token = _exchange_start("exchange_chips_start_" + tag, "chips", sums, lands)
    return (tag, names, send_sems, recv_sems, sums, lands), token


def _reduce_scatter_wait(state, after):
    tag, names, send_sems, recv_sems, sums, lands = state
    sums, recv2 = _exchange_wait("exchange_chips_wait_" + tag, "chips", send_sems, recv_sems, sums, lands, after)
    return names, sums, recv2


def _adamw(name, w, g, m, v):
    def body(w_ref, g_ref, m_ref, v_ref, d_ref, nm_ref, nv_ref):
        d_ref[...], nm_ref[...], nv_ref[...] = _adam_math(w_ref[...], g_ref[...], m_ref[...], v_ref[...])

    return pl.pallas_call(
        body, name=name, out_shape=[jax.ShapeDtypeStruct(w.shape, F32)] * 3, compiler_params=_params())(w, g, m, v)


def _adam_math(w, g, m, v):
    nm = ADAM_B1 * m + (1.0 - ADAM_B1) * g
    nv = ADAM_B2 * v + (1.0 - ADAM_B2) * (g * g)
    c1 = 1.0 - ADAM_B1 ** ADAM_STEP
    c2 = 1.0 - ADAM_B2 ** ADAM_STEP
    return -ADAM_LR * ((nm / c1) / (jnp.sqrt(nv / c2) + ADAM_EPS) + ADAM_WD * w), nm, nv


def _adamw_chips(name, sums, recv, w, m, v, transposed, rows=None, dep=None):
    r, c = w.shape
    rows = r if rows is None else rows
    qidx = (2 * lax.axis_index("x") + lax.axis_index("y")).astype(jnp.int32).reshape(1)
    dep_specs, dep_args = _dep_operand(dep)

    def body(q_ref, a_ref, b_ref, w_ref, m_ref, v_ref, *rest):
        g_ref, d_ref, nm_ref, nv_ref = rest[-4:]
        g = (a_ref[...].astype(F32) + b_ref[0].astype(F32)) + (b_ref[1].astype(F32) + b_ref[2].astype(F32))
        if transposed:
            g = g.T
        g_ref[...] = g
        d_ref[...], nm_ref[...], nv_ref[...] = _adam_math(w_ref[...], g, m_ref[...], v_ref[...])

    row = pl.BlockSpec((rows, c), lambda i, q_ref: (i, 0))
    if transposed:
        term_specs = [pl.BlockSpec((None, c, rows), lambda i, q_ref: (q_ref[0], 0, i)),
                      pl.BlockSpec((3, c, rows), lambda i, q_ref: (0, 0, i))]
    else:
        term_specs = [pl.BlockSpec((None, rows, c), lambda i, q_ref: (q_ref[0], i, 0)),
                      pl.BlockSpec((3, rows, c), lambda i, q_ref: (0, i, 0))]
    grid_spec = pltpu.PrefetchScalarGridSpec(
        num_scalar_prefetch=1, grid=(r // rows,), in_specs=term_specs + [row, row, row] + dep_specs,
        out_specs=[row] * 4)
    return pl.pallas_call(
        body, name=name, grid_spec=grid_spec, out_shape=[jax.ShapeDtypeStruct((r, c), F32)] * 4,
        compiler_params=_params(("parallel",)))(qidx, sums, recv, w, m, v, *dep_args)


def _sum_devices(gathered):
    def body(g_ref, o_ref):
        acc = g_ref[0]
        for j in range(1, N_DEV):
            acc = acc + g_ref[j]
        o_ref[...] = acc

    return pl.pallas_call(
        body, name="sum_devices", out_shape=jax.ShapeDtypeStruct(gathered.shape[1:], F32),
        compiler_params=_params())(gathered)


def _rows128(a, rows):
    flat = a.reshape(-1)
    return jnp.pad(flat, (0, rows * 128 - flat.shape[0])).reshape(rows, 128)


def kernel(x, mem, pre_norm, w_in, merge_bias, na_rpb, mem_norm, w_mem_kv, w_branch_a, w_branch_b, w_branch_c, w_out, post_norm, loss_target, m_pre_norm, m_w_in, m_merge_bias, m_na_rpb, m_mem_norm, m_w_mem_kv, m_w_branch_a, m_w_branch_b, m_w_branch_c, m_w_out, m_post_norm, v_pre_norm, v_w_in, v_merge_bias, v_na_rpb, v_mem_norm, v_w_mem_kv, v_w_branch_a, v_w_branch_b, v_w_branch_c, v_w_out, v_post_norm):
    wt_in_s = w_in[0].T.astype(BF16)
    rows_s = jnp.concatenate([w_mem_kv[0], w_out[0]], axis=0).astype(BF16)
    cols_s = jnp.concatenate([w_branch_a[0].T, w_branch_b[0].T, w_branch_c[0].T], axis=0).astype(BF16)
    mb_s = jnp.pad(merge_bias[0], ((0, 5), (0, 0)))
    wt_in = _all_gather(wt_in_s).reshape(N_IN, D_MODEL)

    late_own = [rows_s, cols_s, mb_s]
    late_lands = [jax.ShapeDtypeStruct((N_DEV,) + s.shape, s.dtype) for s in late_own]
    l_send, l_recv, late_own, late_lands, late_token = _exchange_start("gather_late_start", "gather", late_own,
                                                                       late_lands, after=wt_in)
    me = 4 * lax.axis_index("x") + 2 * lax.axis_index("y") + lax.axis_index("c")

    def late_weights(after):
        own, lands = _exchange_wait("gather_late_wait", "gather", l_send, l_recv, late_own, late_lands, after)
        g_rows, g_cols, g_mb = [lax.dynamic_update_slice(land, o[None], (me, 0, 0)) for land, o in zip(lands, own)]
        return (g_mb[:, :3].transpose(1, 0, 2).reshape(3, D_MODEL),
                g_rows[:, :128].reshape(D_MODEL, D_MODEL), g_cols[:, 0:128].reshape(D_MODEL, 512),
                g_cols[:, 128:256].reshape(D_MODEL, 512), g_cols[:, 256:384].reshape(D_MODEL, 512),
                g_rows[:, 128:].reshape(D_MODEL, D_MODEL))

    rs_state = []

    def reduce_start(grads):
        if "wt_in" in grads:
            state, token = _reduce_scatter_start("w_in", ["w_in"],
                                                 [grads["wt_in"].reshape(N_DEV, SHARD_IN, D_MODEL)])
        else:
            gmb_t = jnp.pad(grads["merge_bias"].reshape(3, N_DEV, 128).transpose(1, 0, 2), ((0, 0), (0, 5), (0, 0)))
            names = ["w_kv", "w_out", "a", "b", "c", "mb"]
            terms = [grads["w_kv"].reshape(N_DEV, 128, D_MODEL), grads["w_out"].reshape(N_DEV, 128, D_MODEL),
                     grads["wt_a"].reshape(N_DEV, 128, 512), grads["wt_b"].reshape(N_DEV, 128, 512),
                     grads["wt_c"].reshape(N_DEV, 128, 512), gmb_t]
            state, token = _reduce_scatter_start("rest", names, terms)
        rs_state.append(state)
        return token

    loss_term, grad_x, grads = _local_step(
        x[0], mem[0], loss_target[0], pre_norm, mem_norm, post_norm, na_rpb[0], wt_in, late_weights,
        dep_in=late_token, reduce_start=reduce_start)

    small = jnp.concatenate([_rows128(grads["pre_norm"], 8), _rows128(grads["mem_norm"], 8),
                             _rows128(grads["post_norm"], 8), _rows128(grads["na_rpb"], 32),
                             _rows128(loss_term, 8)], axis=0)
    s_send, s_recv, s_own, s_land, s_token = _exchange_start(
        "gather_small_start", "gather", [small], [jax.ShapeDtypeStruct((N_DEV,) + small.shape, F32)])
    grad = {}
    weights = {
        "pre_norm": (pre_norm, m_pre_norm, v_pre_norm), "w_in": (w_in, m_w_in, v_w_in),
        "merge_bias": (merge_bias, m_merge_bias, v_merge_bias), "na_rpb": (na_rpb, m_na_rpb, v_na_rpb),
        "mem_norm": (mem_norm, m_mem_norm, v_mem_norm), "w_mem_kv": (w_mem_kv, m_w_mem_kv, v_w_mem_kv),
        "w_branch_a": (w_branch_a, m_w_branch_a, v_w_branch_a), "w_branch_b": (w_branch_b, m_w_branch_b, v_w_branch_b),
        "w_branch_c": (w_branch_c, m_w_branch_c, v_w_branch_c), "w_out": (w_out, m_w_out, v_w_out),
        "post_norm": (post_norm, m_post_norm, v_post_norm)}
    order = ["pre_norm", "w_in", "merge_bias", "na_rpb", "mem_norm", "w_mem_kv", "w_branch_a", "w_branch_b",
             "w_branch_c", "w_out", "post_norm"]
    delta, new_m, new_v = {}, {}, {}

    def update(n):
        w, m, v = weights[n]
        shape = w.shape
        two_d = (-1, shape[-1])
        dl, nm, nv = _adamw("adamw_" + n, w.reshape(two_d), grad[n].reshape(two_d), m.reshape(two_d),
                            v.reshape(two_d))
        delta[n], new_m[n], new_v[n] = dl.reshape(shape), nm.reshape(shape), nv.reshape(shape)

    def update_sharded(n, sums, recv, transposed, rows=None, dep=None):
        w, m, v = weights[n]
        g, dl, nm, nv = _adamw_chips("adamw_" + n, sums, recv, w[0], m[0], v[0], transposed, rows, dep)
        grad[n], delta[n], new_m[n], new_v[n] = g[None], dl[None], nm[None], nv[None]
        return dl

    _, sums, recv2 = _reduce_scatter_wait(rs_state[0], s_token)
    dep = None
    for i, (n, transposed) in enumerate((("w_mem_kv", False), ("w_out", False), ("w_branch_a", True),
                                         ("w_branch_b", True), ("w_branch_c", True))):
        dep = update_sharded(n, sums[i], recv2[i], transposed, dep=dep)
    grad["merge_bias"] = _add_chips("add_chips_mb", sums[5], recv2[5], 8)[:3][None]
    update("merge_bias")
    s_own, s_land = _exchange_wait("gather_small_wait", "gather", s_send, s_recv, s_own, s_land, dep)
    total = _sum_devices(lax.dynamic_update_slice(s_land[0], s_own[0][None], (me, 0, 0)))
    loss = total[56, 0]
    grad.update({"pre_norm": total[0:8].reshape(1, D_MODEL), "mem_norm": total[8:16].reshape(1, D_MODEL),
                 "post_norm": total[16:24].reshape(1, D_MODEL),
                 "na_rpb": total[24:56].reshape(-1)[:8 * 15 * 31].reshape(1, 8, 15, 31)})
    for n in ("pre_norm", "na_rpb", "mem_norm", "post_norm"):
        update(n)
    _, sums_in, recv_in = _reduce_scatter_wait(rs_state[1], delta["post_norm"])
    update_sharded("w_in", sums_in[0], recv_in[0], True, 256)

    return (loss, grad_x[None], *[grad[n] for n in order], *[delta[n] for n in order],
            *[new_m[n] for n in order], *[new_v[n] for n in order])
```

```python
import functools

import numpy as np
import jax
import jax.numpy as jnp
from jax import lax
from jax.experimental import pallas as pl
from jax.experimental.pallas import tpu as pltpu

F32 = jnp.float32
BF16 = jnp.bfloat16

SEQ = 2048
D_MODEL = 1024
N_IN = 11264
N_DEV = 8
SHARD_IN = N_IN // N_DEV
HEAD_DIM = 64
GRID_W = 64
NA_ROWS = 8
MEM_LEN = 256
DILATIONS = (1, 4, 16)
REACH = 64
ROPE_THETA = 500000.0
ROPE_DIM = 16
EPS = 1e-6
NEG = -1e30
ADAM_LR = 0.001
ADAM_B1 = 0.9
ADAM_B2 = 0.999
ADAM_EPS = 1e-08
ADAM_WD = 0.01
ADAM_STEP = 10

VMEM_LIMIT_BYTES = 56 * 1024 * 1024
MESH_ID = pl.DeviceIdType.MESH

NN = (((1,), (0,)), ((), ()))
NT = (((1,), (1,)), ((), ()))
TN = (((0,), (0,)), ((), ()))


def _params(sem=None):
    return pltpu.CompilerParams(dimension_semantics=sem, vmem_limit_bytes=VMEM_LIMIT_BYTES)


def _iota(shape, dim):
    return lax.broadcasted_iota(jnp.int32, shape, dim)


def _sigmoid(x):
    return 1.0 / (1.0 + jnp.exp(-x))


def _rope_tables():
    half = ROPE_DIM // 2
    inv = (ROPE_THETA ** (-np.arange(half, dtype=np.float64) * 2.0 / ROPE_DIM)).astype(np.float32)
    pos = np.arange(SEQ, dtype=np.float32)
    ang = pos[:, None] * inv[None, :]
    cos, sin = np.cos(ang), np.sin(ang)
    zeros = np.zeros_like(cos)
    rest = HEAD_DIM - ROPE_DIM
    c64 = np.concatenate([cos, cos, np.ones((SEQ, rest), np.float32)], axis=1)
    s1 = np.concatenate([zeros, sin, np.zeros((SEQ, rest), np.float32)], axis=1)
    s2 = np.concatenate([-sin, zeros, np.zeros((SEQ, rest), np.float32)], axis=1)

    def fold(t, d):
        return t.reshape(SEQ // d, d, t.shape[1]).transpose(1, 0, 2).reshape(SEQ, t.shape[1])

    tabs = [np.stack([np.tile(fold(t, d), (1, 2)) for t in (c64, s1, s2)], axis=0) for d in DILATIONS]
    return jnp.asarray(np.stack(tabs, axis=0), dtype=F32)


def _rope(a, c, s1, s2):
    return a * c + pltpu.roll(a, 8, 1) * s1 + pltpu.roll(a, 120, 1) * s2


def _rope_t(a, c, s1, s2):
    return a * c + pltpu.roll(a * s1, 120, 1) + pltpu.roll(a * s2, 8, 1)


def _perm_of_block(j):
    return jnp.where(j < 3, 0, jnp.where(j < 6, 1, jnp.where(j < 9, 2, 0)))


def _mm(name, a, b, out_shape, out_dtype, grid, a_spec, b_spec, o_spec, acc_shape, dims, k_axis, nk):
    def body(a_ref, b_ref, o_ref, acc_ref):
        k = pl.program_id(k_axis)

        @pl.when(k == 0)
        def _():
            acc_ref[...] = jnp.zeros(acc_shape, F32)

        acc_ref[...] += lax.dot_general(a_ref[...], b_ref[...], dims, preferred_element_type=F32)

        @pl.when(k == nk - 1)
        def _():
            o_ref[...] = acc_ref[...].astype(out_dtype)

    sem = tuple("arbitrary" if ax == k_axis else "parallel" for ax in range(len(grid)))
    return pl.pallas_call(
        body, name=name, grid=grid, in_specs=[a_spec, b_spec], out_specs=o_spec,
        out_shape=jax.ShapeDtypeStruct(out_shape, out_dtype),
        scratch_shapes=[pltpu.VMEM(acc_shape, F32)], compiler_params=_params(sem))(a, b)


def _mm_simple(name, a, b, dims, out_dtype, tm, tn, tk):
    if dims is NN:
        m, kk = a.shape
        n = b.shape[1]
        a_spec = pl.BlockSpec((tm, tk), lambda i, j, k: (i, k))
        b_spec = pl.BlockSpec((tk, tn), lambda i, j, k: (k, j))
    elif dims is NT:
        m, kk = a.shape
        n = b.shape[0]
        a_spec = pl.BlockSpec((tm, tk), lambda i, j, k: (i, k))
        b_spec = pl.BlockSpec((tn, tk), lambda i, j, k: (j, k))
    else:
        kk, m = a.shape
        n = b.shape[1]
        a_spec = pl.BlockSpec((tk, tm), lambda i, j, k: (k, i))
        b_spec = pl.BlockSpec((tk, tn), lambda i, j, k: (k, j))
    grid = (m // tm, n // tn, kk // tk)
    o_spec = pl.BlockSpec((tm, tn), lambda i, j, k: (i, j))
    return _mm(name, a, b, (m, n), out_dtype, grid, a_spec, b_spec, o_spec, (tm, tn), dims, 2, kk // tk)


def _rmsnorm_fwd(name, x, gain, rows):
    n, d = x.shape

    def body(x_ref, g_ref, o_ref):
        xv = x_ref[...]
        rstd = lax.rsqrt(jnp.mean(xv * xv, axis=1, keepdims=True) + EPS)
        o_ref[...] = (xv * rstd * g_ref[...]).astype(BF16)

    return pl.pallas_call(
        body, name=name, grid=(n // rows,),
        in_specs=[pl.BlockSpec((rows, d), lambda i: (i, 0)), pl.BlockSpec((1, d), lambda i: (0, 0))],
        out_specs=pl.BlockSpec((rows, d), lambda i: (i, 0)),
        out_shape=jax.ShapeDtypeStruct((n, d), BF16), compiler_params=_params(("parallel",)))(x, gain)


def _folded_rows(first, rows, d):
    if d == 1:
        return pl.ds(pl.multiple_of(first, rows), rows)
    mlen = SEQ // d
    return pl.ds((first % mlen) * d + first // mlen, rows, stride=d)


def _prenorm_fold(x, gain):
    rows = 128

    nchunk = D_MODEL // 128

    def body(*refs):
        x_refs, g_ref, hs_ref, hst_ref = refs[:nchunk], refs[nchunk], refs[nchunk + 1], refs[nchunk + 2]
        first = pl.program_id(0) * rows
        for p, d in enumerate(DILATIONS):
            idx = _folded_rows(first, rows, d)
            xv = jnp.concatenate([r[idx, :] for r in x_refs], axis=1)
            rstd = lax.rsqrt(jnp.mean(xv * xv, axis=1, keepdims=True) + EPS)
            h = xv * rstd * g_ref[...]
            hs_ref[p] = h.astype(BF16)
            hst_ref[p] = h.T.astype(BF16)

    x_specs = [pl.BlockSpec((SEQ, 128), functools.partial(lambda c, i: (0, c), c)) for c in range(nchunk)]
    return pl.pallas_call(
        body, name="prenorm", grid=(SEQ // rows,),
        in_specs=x_specs + [pl.BlockSpec((1, D_MODEL), lambda i: (0, 0))],
        out_specs=[pl.BlockSpec((3, rows, D_MODEL), lambda i: (0, i, 0)),
                   pl.BlockSpec((3, D_MODEL, rows), lambda i: (0, 0, i))],
        out_shape=[jax.ShapeDtypeStruct((3, SEQ, D_MODEL), BF16), jax.ShapeDtypeStruct((3, D_MODEL, SEQ), BF16)],
        compiler_params=_params(("parallel",)))(*([x] * nchunk), gain)


def _prenorm_bwd(x, gain, dh, dout):
    rows = 256

    def body(x_ref, g_ref, a_ref, do_ref, dx_ref, gg_ref):
        xv = x_ref[...]
        rstd = lax.rsqrt(jnp.mean(xv * xv, axis=1, keepdims=True) + EPS)
        xn = xv * rstd
        dh = jnp.concatenate([a_ref[c] for c in range(D_MODEL // 128)], axis=1)
        gdh = dh * g_ref[...]
        dx_ref[...] = rstd * (gdh - xn * jnp.mean(gdh * xn, axis=1, keepdims=True)) + do_ref[...]

        @pl.when(pl.program_id(0) == 0)
        def _():
            gg_ref[...] = jnp.zeros((1, D_MODEL), F32)

        gg_ref[...] += jnp.sum(dh * xn, axis=0, keepdims=True)

    row = pl.BlockSpec((rows, D_MODEL), lambda i: (i, 0))
    vec = pl.BlockSpec((1, D_MODEL), lambda i: (0, 0))
    return pl.pallas_call(
        body, name="prenorm_bwd", grid=(SEQ // rows,),
        in_specs=[row, vec, pl.BlockSpec((D_MODEL // 128, rows, 128), lambda i: (0, i, 0)), row], out_specs=[row, vec],
        out_shape=[jax.ShapeDtypeStruct((SEQ, D_MODEL), F32), jax.ShapeDtypeStruct((1, D_MODEL), F32)],
        compiler_params=_params(("arbitrary",)))(x, gain, dh, dout)


def _memnorm_bwd(mem, dmemn):
    def body(m_ref, d_ref, gg_ref):
        mv = m_ref[...]
        rstd = lax.rsqrt(jnp.mean(mv * mv, axis=1, keepdims=True) + EPS)
        gg_ref[...] = jnp.sum(d_ref[...] * mv * rstd, axis=0, keepdims=True)

    return pl.pallas_call(
        body, name="memnorm_bwd", out_shape=jax.ShapeDtypeStruct((1, D_MODEL), F32),
        compiler_params=_params())(mem, dmemn)


def _dep_operand(dep):
    return ([], []) if dep is None else ([pl.BlockSpec(memory_space=pl.ANY)], [dep])


def _in_proj(hs, wt, tabs, dep=None):
    tm, tn = 512, 512
    dep_specs, dep_args = _dep_operand(dep)

    def body(h_ref, w_ref, t_ref, *rest):
        o_ref = rest[-1]
        j = pl.program_id(0)
        is_rope = jnp.logical_and(j < 9, j % 3 != 2)
        row_slices = [slice(r * tm, (r + 1) * tm) for r in range(SEQ // tm)]

        def product(rs):
            return lax.dot_general(h_ref[rs, :], w_ref[...], NT, preferred_element_type=F32)

        @pl.when(is_rope)
        def _():
            for rs in row_slices:
                acc = product(rs)
                c, s1, s2 = t_ref[0, rs, :], t_ref[1, rs, :], t_ref[2, rs, :]
                for q in range(tn // 128):
                    a = acc[:, q * 128:(q + 1) * 128]
                    o_ref[rs, q * 128:(q + 1) * 128] = _rope(a, c, s1, s2).astype(BF16)

        @pl.when(jnp.logical_not(is_rope))
        def _():
            for rs in row_slices:
                o_ref[rs, :] = product(rs).astype(BF16)

    return pl.pallas_call(
        body, name="in_proj", grid=(N_IN // tn,),
        in_specs=[pl.BlockSpec((None, SEQ, D_MODEL), lambda j: (_perm_of_block(j), 0, 0)),
                  pl.BlockSpec((tn, D_MODEL), lambda j: (j, 0)),
                  pl.BlockSpec((None, 3, SEQ, 128), lambda j: (_perm_of_block(j), 0, 0, 0))] + dep_specs,
        out_specs=pl.BlockSpec((SEQ, tn), lambda j: (0, j)),
        out_shape=jax.ShapeDtypeStruct((SEQ, N_IN), BF16),
        compiler_params=_params(("parallel",)))(hs, wt, tabs, *dep_args)


def _piece_blocks(pieces):
    return [(a, h * 512) for a, p in enumerate(pieces) for h in range(p.shape[1] // 512)]


def _block_fetch(piece_refs, blocks, buf, sem):
    def start(block, slot):
        for b, (a, col) in enumerate(blocks):
            @pl.when(block == b)
            def _():
                pltpu.make_async_copy(piece_refs[a].at[:, pl.ds(col, 512)], buf.at[slot], sem.at[slot]).start()

    def wait(slot):
        pltpu.make_async_copy(piece_refs[0].at[:, pl.ds(0, 512)], buf.at[slot], sem.at[slot]).wait()

    return start, wait


def _in_proj_dw(pieces, hst, dep=None):
    tn = 512
    blocks = _piece_blocks(pieces)
    nblk = len(blocks)
    npc = len(pieces)
    dep_specs, dep_args = _dep_operand(dep)

    def body(h_ref, *rest):
        piece_refs = rest[:npc]
        o_ref, mirror, buf, sem, out_buf, send_sems, recv_sem = rest[-7:]
        j = pl.program_id(0)
        slot = j % 2
        start, wait = _block_fetch(piece_refs, blocks, buf, sem)
        x, y, c = _place()

        def to_sibling(step, slot_):
            return pltpu.make_async_remote_copy(
                src_ref=out_buf.at[slot_], dst_ref=mirror.at[pl.ds(pl.multiple_of(step * tn, tn), tn)],
                send_sem=send_sems.at[slot_], recv_sem=recv_sem, device_id=(x, y, 1 - c), device_id_type=MESH_ID)

        @pl.when(j == 0)
        def _():
            start(j, slot)

        wait(slot)

        @pl.when(j + 1 < nblk)
        def _():
            start(j + 1, 1 - slot)

        acc = jnp.dot(h_ref[...], buf[slot], preferred_element_type=F32)
        block = acc.T.astype(BF16)
        o_ref[...] = block

        @pl.when(j >= 2)
        def _():
            to_sibling(j - 2, slot).wait_send()

        out_buf[slot] = block
        to_sibling(j, slot).start()

        @pl.when(j == nblk - 1)
        def _():
            to_sibling(j - 1, 1 - slot).wait_send()
            to_sibling(j, slot).wait_send()
            pltpu.make_async_remote_copy(src_ref=mirror, dst_ref=mirror, send_sem=send_sems.at[0], recv_sem=recv_sem,
                                         device_id=(x, y, 1 - c), device_id_type=MESH_ID).wait_recv()

    return pl.pallas_call(
        body, name="in_proj_dw", grid=(nblk,),
        in_specs=[pl.BlockSpec((None, D_MODEL, SEQ), lambda j: (_perm_of_block(j), 0, 0))] + [ANY] * npc + dep_specs,
        out_specs=[pl.BlockSpec((tn, D_MODEL), lambda j: (j, 0)), ANY],
        out_shape=[jax.ShapeDtypeStruct((N_IN, D_MODEL), BF16), jax.ShapeDtypeStruct((N_IN, D_MODEL), BF16)],
        scratch_shapes=[pltpu.VMEM((2, SEQ, tn), BF16), pltpu.SemaphoreType.DMA((2,)),
                        pltpu.VMEM((2, tn, D_MODEL), BF16), pltpu.SemaphoreType.DMA((2,)), pltpu.SemaphoreType.DMA],
        compiler_params=_params(("arbitrary",)))(hst, *pieces, *dep_args)


def _in_proj_dh(pieces, wt, dep=None):
    tk = 512
    blocks = _piece_blocks(pieces)
    nblk = len(blocks)
    npc = len(pieces)
    nchunk = D_MODEL // 128

    def col(s):
        return jnp.where(s < 3, s, jnp.where(s < 16, s + 6, s - 13))

    dep_specs, dep_args = _dep_operand(dep)

    def body(w_ref, *rest):
        piece_refs = rest[:npc]
        o_ref, acc_ref, buf, sem = rest[-4:]
        s = pl.program_id(0)
        slot = s % 2
        start, wait = _block_fetch(piece_refs, blocks, buf, sem)

        @pl.when(s == 0)
        def _():
            start(col(s), slot)

        wait(slot)

        @pl.when(s + 1 < nblk)
        def _():
            start(col(s + 1), 1 - slot)

        row_slices = [slice(r * 512, (r + 1) * 512) for r in range(SEQ // 512)]

        def product(rs):
            return jnp.dot(buf[slot, rs, :], w_ref[...], preferred_element_type=F32)

        def accumulate(cond, to_out, init):
            @pl.when(cond)
            def _():
                for rs in row_slices:
                    prod = product(rs)
                    if not to_out:
                        if init:
                            acc_ref[rs, :] = prod
                        else:
                            acc_ref[rs, :] += prod
                        continue
                    for c in range(nchunk):
                        if init:
                            o_ref[c, rs, :] = prod[:, c * 128:(c + 1) * 128]
                        else:
                            o_ref[c, rs, :] += prod[:, c * 128:(c + 1) * 128]

        accumulate(s == 0, True, True)
        accumulate(jnp.logical_and(s > 0, s < 16), True, False)
        accumulate(jnp.logical_or(s == 16, s == 19), False, True)
        accumulate(jnp.logical_and(s > 16, s != 19), False, False)
        for last, d in ((18, 4), (21, 16)):
            @pl.when(s == last)
            def _():
                mlen = SEQ // d
                for r in range(d):
                    for c in range(nchunk):
                        o_ref[c, pl.ds(r, mlen, stride=d), :] += acc_ref[r * mlen:(r + 1) * mlen,
                                                                         c * 128:(c + 1) * 128]

    return pl.pallas_call(
        body, name="in_proj_dh", grid=(nblk,),
        in_specs=[pl.BlockSpec((tk, D_MODEL), lambda s: (col(s), 0))] + [ANY] * npc + dep_specs,
        out_specs=pl.BlockSpec((nchunk, SEQ, 128), lambda s: (0, 0, 0)),
        out_shape=jax.ShapeDtypeStruct((nchunk, SEQ, 128), F32),
        scratch_shapes=[pltpu.VMEM((SEQ, D_MODEL), F32), pltpu.VMEM((2, SEQ, tk), BF16),
                        pltpu.SemaphoreType.DMA((2,))],
        compiler_params=_params(("arbitrary",)))(wt, *pieces, *dep_args)


def _head_lanes(lanes, hh):
    return lanes >= 64 if hh == 1 else lanes < 64


def _head_rows(x, lanes, hh, pair):
    if not pair:
        return jnp.max(x, axis=1, keepdims=True)
    return jnp.max(jnp.where(_head_lanes(lanes, hh), x, -jnp.inf), axis=1, keepdims=True)


def _mask_head(x, lanes, hh, pair, scale=1.0):
    if not pair:
        return x
    xf = x.astype(F32) if scale == 1.0 else x.astype(F32) * scale
    return jnp.where(_head_lanes(lanes, hh), xf, 0.0).astype(BF16)


def _window(mode, qi, tq, mlen, tk):
    if mode == "dil":
        q0 = qi * tq
        seg = (q0 // mlen) * mlen
        ks = jnp.clip(q0 - REACH, seg, seg + mlen - tk)
        return pl.multiple_of(ks, 64)
    if mode == "na":
        r_start = jnp.clip(qi - NA_ROWS // 2, 0, SEQ // GRID_W - NA_ROWS)
        return pl.multiple_of(r_start * GRID_W, 64)
    return 0


def _band_mask(qi, tq, tk, ks):
    qpos = qi * tq + _iota((tq, tk), 0)
    kpos = ks + _iota((tq, tk), 1)
    return jnp.where(jnp.abs(qpos - kpos) <= REACH, 0.0, NEG).astype(F32)


def _stack_heads(x, lanes, pair, scale=1.0):
    if not pair:
        return x
    return jnp.concatenate([_mask_head(x, lanes, hh, pair, scale) for hh in range(2)], axis=0)


def _stack_rows(x, lanes, pair):
    if not pair:
        return _head_rows(x, lanes, 0, pair)
    return jnp.concatenate([_head_rows(x, lanes, hh, pair) for hh in range(2)], axis=0)


def _unstack_heads(x, lanes, pair, tq):
    if not pair:
        return x
    return jnp.where(lanes < 64, x[:tq], x[tq:])


def _scores(mode, qst, k, sscale, band, qi, bias_ref, pair):
    s = lax.dot_general(qst, k, NT, preferred_element_type=F32)
    if sscale != 1.0:
        s = s * sscale
    if mode == "dil":
        s = s + jnp.concatenate([band, band], axis=0)
    elif mode == "na":
        off = qi - jnp.clip(qi - NA_ROWS // 2, 0, SEQ // GRID_W - NA_ROWS)
        s = s + jnp.concatenate([bias_ref[0, off], bias_ref[1, off]], axis=0)
    return s


def _attn_cfg(mode, d):
    if mode == "dil":
        mlen = SEQ // d
        return dict(pair=True, tq=128, tk=min(256, mlen), mlen=mlen, lk=SEQ, scale=HEAD_DIM ** -0.5, units=4,
                    nsub=ATTN_SUBTILES)
    if mode == "na":
        return dict(pair=True, tq=GRID_W, tk=NA_ROWS * GRID_W, mlen=SEQ, lk=SEQ, scale=HEAD_DIM ** -0.5, units=4,
                    nsub=ATTN_SUBTILES)
    return dict(pair=False, tq=128, tk=MEM_LEN, mlen=SEQ, lk=MEM_LEN, scale=128 ** -0.5, units=4,
                nsub=ATTN_SUBTILES)


ATTN_SUBTILES = 16


def _attn_fwd(name, mode, q_arr, k_arr, v_arr, qcol, kcol, vcol, d=1, bias=None):
    cfg = _attn_cfg(mode, d)
    pair, tq, tk, mlen, lk, scale = cfg["pair"], cfg["tq"], cfg["tk"], cfg["mlen"], cfg["lk"], cfg["scale"]
    qscale, sscale = (scale, 1.0) if pair else (1.0, scale)
    nsub = cfg["nsub"]
    rows = nsub * tq

    def body(*refs):
        if mode == "na":
            q_ref, k_ref, v_ref, bias_ref, o_ref, l_ref = refs
        else:
            q_ref, k_ref, v_ref, o_ref, l_ref = refs
            bias_ref = None
        lanes = _iota((tq, 128), 1)
        qis = [pl.program_id(1) * nsub + sub for sub in range(nsub)]
        kss = [_window(mode, qi, tq, mlen, tk) for qi in qis]
        vs = [v_ref[pl.ds(ks, tk), :] for ks in kss]
        bands = [_band_mask(qi, tq, tk, ks) if mode == "dil" else None for qi, ks in zip(qis, kss)]
        ss = []
        for sub in range(nsub):
            qst = _stack_heads(q_ref[sub * tq:(sub + 1) * tq, :], lanes, pair, qscale)
            k = k_ref[pl.ds(kss[sub], tk), :]
            ss.append(_scores(mode, qst, k, sscale, bands[sub], qis[sub], bias_ref, pair))
        ms = [jnp.max(s_, axis=1, keepdims=True) for s_ in ss]
        ps = [jnp.exp(s_ - m) for s_, m in zip(ss, ms)]
        ls = [jnp.sum(p, axis=1, keepdims=True) for p in ps]
        os_ = [jnp.dot(p.astype(BF16), v, preferred_element_type=F32) for p, v in zip(ps, vs)]
        for sub in range(nsub):
            out = _unstack_heads(os_[sub] / ls[sub], lanes, pair, tq)
            lse = ms[sub] + jnp.log(ls[sub])
            lse = _unstack_heads(jnp.broadcast_to(lse, (lse.shape[0], 128)), lanes, pair, tq)
            dst = _folded_rows(qis[sub] * tq, tq, d) if mode == "dil" else slice(sub * tq, (sub + 1) * tq)
            o_ref[dst, :] = out
            l_ref[dst, :] = lse

    in_specs = [pl.BlockSpec((rows, 128), lambda u, i: (i, qcol + u)),
                pl.BlockSpec((lk, 128), lambda u, i: (0, kcol + u)),
                pl.BlockSpec((lk, 128), lambda u, i: (0, vcol + u))]
    args = [q_arr, k_arr, v_arr]
    if mode == "na":
        in_specs.append(pl.BlockSpec((2, NA_ROWS, GRID_W, NA_ROWS * GRID_W), lambda u, i: (u, 0, 0, 0)))
        args.append(bias)
    if mode == "dil":
        out_spec = pl.BlockSpec((SEQ, 128), lambda u, i: (0, u))
    else:
        out_spec = pl.BlockSpec((rows, 128), lambda u, i: (i, u))
    return pl.pallas_call(
        body, name=name, grid=(cfg["units"], SEQ // rows), in_specs=in_specs, out_specs=[out_spec, out_spec],
        out_shape=[jax.ShapeDtypeStruct((SEQ, 512), F32), jax.ShapeDtypeStruct((SEQ, 512), F32)],
        compiler_params=_params(("parallel", "arbitrary")))(*args)


def _attn_bwd(name, mode, q_arr, k_arr, v_arr, qcol, kcol, vcol, do, lse, dp=None, o=None, d=1, bias=None,
              tabs=None):
    cfg = _attn_cfg(mode, d)
    pair, tq, tk, mlen, lk, scale = cfg["pair"], cfg["tq"], cfg["tk"], cfg["mlen"], cfg["lk"], cfg["scale"]
    qscale, sscale = (scale, 1.0) if pair else (1.0, scale)
    nsub = cfg["nsub"]
    rows = nsub * tq
    nq = SEQ // rows
    kv_dtype = F32 if mode == "mem" else BF16

    def body(*refs):
        refs = list(refs)
        q_ref, k_ref, v_ref, do_ref, l_ref = refs[:5]
        rest = refs[5:]
        bias_ref = tq_ref = tk_ref = db_ref = None
        if mode == "dil":
            dp_ref, tq_ref, tk_ref, dq_ref, dk_ref, dv_ref, dk_acc, dv_acc = rest
        elif mode == "na":
            o_ref, bias_ref, dq_ref, dk_ref, dv_ref, db_ref, dk_acc, dv_acc = rest
        else:
            o_ref, dq_ref, dk_ref, dv_ref, dk_acc, dv_acc = rest
        step = pl.program_id(1)

        @pl.when(step == 0)
        def _():
            dk_acc[...] = jnp.zeros((lk, 128), F32)
            dv_acc[...] = jnp.zeros((lk, 128), F32)
            if mode == "na":
                db_ref[...] = jnp.zeros(db_ref.shape, F32)

        lanes = _iota((tq, 128), 1)
        qis = [step * nsub + sub for sub in range(nsub)]
        sls = [slice(sub * tq, (sub + 1) * tq) for sub in range(nsub)]
        kss = [_window(mode, qi, tq, mlen, tk) for qi in qis]
        ks_ = [k_ref[pl.ds(ks, tk), :] for ks in kss]
        vs = [v_ref[pl.ds(ks, tk), :] for ks in kss]
        qsts, dosts, lses, dphs = [], [], [], []
        for sub in range(nsub):
            if mode == "dil":
                src = _folded_rows(qis[sub] * tq, tq, d)
                dov = do_ref[src, :].astype(BF16)
                lsev = l_ref[src, :]
                dphs.append(_stack_rows(dp_ref[src, :], lanes, pair))
            else:
                dov = do_ref[sls[sub], :]
                lsev = l_ref[sls[sub], :]
                dpv = dov.astype(F32) * o_ref[sls[sub], :]
                if pair:
                    dphs.append(jnp.concatenate(
                        [jnp.sum(jnp.where(_head_lanes(lanes, hh), dpv, 0.0), axis=1, keepdims=True)
                         for hh in range(2)], axis=0))
                else:
                    dphs.append(jnp.sum(dpv, axis=1, keepdims=True))
            qsts.append(_stack_heads(q_ref[sls[sub], :], lanes, pair, qscale))
            dosts.append(_stack_heads(dov, lanes, pair))
            lses.append(_stack_rows(lsev, lanes, pair))
        bands = [_band_mask(qi, tq, tk, ks) if mode == "dil" else None for qi, ks in zip(qis, kss)]
        ss = [_scores(mode, qsts[sub], ks_[sub], sscale, bands[sub], qis[sub], bias_ref, pair) for sub in range(nsub)]
        dpms = [lax.dot_general(dosts[sub], vs[sub], NT, preferred_element_type=F32) for sub in range(nsub)]
        ps = [jnp.exp(s_ - lse) for s_, lse in zip(ss, lses)]
        dss = [p * (dpm - dph) for p, dpm, dph in zip(ps, dpms, dphs)]
        if mode == "na":
            for sub, ds in enumerate(dss):
                off = qis[sub] - jnp.clip(qis[sub] - NA_ROWS // 2, 0, SEQ // GRID_W - NA_ROWS)
                db_ref[0, off] += ds[:tq]
                db_ref[1, off] += ds[tq:]
        dsbs = [ds.astype(BF16) for ds in dss]
        dvs = [lax.dot_general(p.astype(BF16), dosts[sub], TN, preferred_element_type=F32)
               for sub, p in enumerate(ps)]
        dqs = [jnp.dot(dsb, ks_[sub], preferred_element_type=F32) * scale for sub, dsb in enumerate(dsbs)]
        dks = [lax.dot_general(dsb, qsts[sub], TN, preferred_element_type=F32) for sub, dsb in enumerate(dsbs)]
        for sub in range(nsub):
            sl = sls[sub]
            dq = _unstack_heads(dqs[sub], lanes, pair, tq)
            if mode == "dil":
                dq = _rope_t(dq, tq_ref[0, sl, :], tq_ref[1, sl, :], tq_ref[2, sl, :])
            dq_ref[sl, :] = dq.astype(BF16)
            dk_acc[pl.ds(kss[sub], tk), :] += dks[sub] if pair else dks[sub] * scale
            dv_acc[pl.ds(kss[sub], tk), :] += dvs[sub]

        @pl.when(step == nq - 1)
        def _():
            dkv = dk_acc[...]
            if mode == "dil":
                dkv = _rope_t(dkv, tk_ref[0], tk_ref[1], tk_ref[2])
            dk_ref[...] = dkv.astype(kv_dtype)
            dv_ref[...] = dv_acc[...].astype(kv_dtype)

    q_spec = pl.BlockSpec((rows, 128), lambda u, i: (i, qcol + u))
    row_spec = pl.BlockSpec((rows, 128), lambda u, i: (i, u))
    kv_out = pl.BlockSpec((lk, 128), lambda u, i: (0, u))
    whole = pl.BlockSpec((SEQ, 128), lambda u, i: (0, u))
    nat_spec = whole if mode == "dil" else row_spec
    in_specs = [q_spec,
                pl.BlockSpec((lk, 128), lambda u, i: (0, kcol + u)),
                pl.BlockSpec((lk, 128), lambda u, i: (0, vcol + u)),
                nat_spec, nat_spec]
    args = [q_arr, k_arr, v_arr, do, lse]
    out_specs = [row_spec, kv_out, kv_out]
    out_shape = [jax.ShapeDtypeStruct((SEQ, 512), BF16), jax.ShapeDtypeStruct((lk, 512), kv_dtype),
                 jax.ShapeDtypeStruct((lk, 512), kv_dtype)]
    if mode == "dil":
        in_specs += [whole, pl.BlockSpec((3, rows, 128), lambda u, i: (0, i, 0)),
                     pl.BlockSpec((3, SEQ, 128), lambda u, i: (0, 0, 0))]
        args += [dp, tabs, tabs]
    elif mode == "na":
        b_spec = pl.BlockSpec((2, NA_ROWS, GRID_W, NA_ROWS * GRID_W), lambda u, i: (u, 0, 0, 0))
        in_specs += [row_spec, b_spec]
        args += [o, bias]
        out_specs.append(b_spec)
        out_shape.append(jax.ShapeDtypeStruct((8, NA_ROWS, GRID_W, NA_ROWS * GRID_W), F32))
    else:
        in_specs.append(row_spec)
        args.append(o)
    return pl.pallas_call(
        body, name=name, grid=(cfg["units"], nq), in_specs=in_specs, out_specs=out_specs, out_shape=out_shape,
        scratch_shapes=[pltpu.VMEM((lk, 128), F32), pltpu.VMEM((lk, 128), F32)],
        compiler_params=_params(("parallel", "arbitrary")))(*args)


def _na_geometry():
    qc = _iota((GRID_W, 128), 0)
    lane = _iota((GRID_W, 128), 1)
    kc = lane & 63
    c_start = jnp.clip(qc - 8, 0, GRID_W - 16)
    valid = jnp.logical_and(kc >= c_start, kc < c_start + 16)
    return lane, valid


def _na_bias(rpb_rows):
    def body(r_ref, o_ref, t_ref):
        lane, valid = _na_geometry()
        for dd in range(14):
            row_a = jnp.broadcast_to(r_ref[dd:dd + 1, :], (GRID_W, 128))
            row_b = jnp.broadcast_to(r_ref[dd + 1:dd + 2, :], (GRID_W, 128))
            both = jnp.where(lane < 64, row_a, pltpu.roll(row_b, 64, 1))
            t = pltpu.roll(both, 128 - 15, 1, stride=1, stride_axis=0)
            t_ref[dd] = jnp.where(valid, t, NEG)
        for off in range(NA_ROWS):
            for p in range(4):
                o_ref[off, :, p * 128:(p + 1) * 128] = t_ref[2 * p - off + 7]

    return pl.pallas_call(
        body, name="na_bias", grid=(8,),
        in_specs=[pl.BlockSpec((None, 16, 128), lambda h: (h, 0, 0))],
        out_specs=pl.BlockSpec((None, NA_ROWS, GRID_W, NA_ROWS * GRID_W), lambda h: (h, 0, 0, 0)),
        out_shape=jax.ShapeDtypeStruct((8, NA_ROWS, GRID_W, NA_ROWS * GRID_W), F32),
        scratch_shapes=[pltpu.VMEM((14, GRID_W, 128), F32)],
        compiler_params=_params(("parallel",)))(rpb_rows)


def _na_bias_bwd(dbias):
    def body(d_ref, o_ref):
        lane, valid = _na_geometry()
        reverse = (_iota((GRID_W, GRID_W), 0) + _iota((GRID_W, GRID_W), 1) == GRID_W - 1).astype(F32)
        o_ref[...] = jnp.zeros((16, 128), F32)
        for dd in range(14):
            t = jnp.zeros((GRID_W, 128), F32)
            for off in range(NA_ROWS):
                for p in range(4):
                    if 2 * p - off + 7 == dd:
                        t = t + d_ref[off, :, p * 128:(p + 1) * 128]
            t = jnp.dot(reverse, jnp.where(valid, t, 0.0), precision=lax.Precision.HIGHEST,
                        preferred_element_type=F32)
            t = pltpu.roll(t, 128 - (GRID_W - 16), 1, stride=1, stride_axis=0)
            o_ref[dd:dd + 1, :] = jnp.sum(t, axis=0, keepdims=True)

    return pl.pallas_call(
        body, name="na_bias_bwd", grid=(8,),
        in_specs=[pl.BlockSpec((None, NA_ROWS, GRID_W, NA_ROWS * GRID_W), lambda h: (h, 0, 0, 0))],
        out_specs=pl.BlockSpec((None, 16, 128), lambda h: (h, 0, 0)),
        out_shape=jax.ShapeDtypeStruct((8, 16, 128), F32),
        compiler_params=_params(("parallel",)))(dbias)


GATE_ROWS = 128


def _group_weights(l0, l1, l2):
    m = jnp.maximum(jnp.maximum(l0, l1), l2)
    e0, e1, e2 = jnp.exp(l0 - m), jnp.exp(l1 - m), jnp.exp(l2 - m)
    inv = 1.0 / (e0 + e1 + e2)
    return e0 * inv, e1 * inv, e2 * inv


def _gate_specs():
    r512 = pl.BlockSpec((GATE_ROWS, 512), lambda i: (i, 0))
    r1024 = pl.BlockSpec((GATE_ROWS, D_MODEL), lambda i: (i, 0))
    silu_cols = [pl.BlockSpec((GATE_ROWS, 512), functools.partial(lambda b, i: (i, b), 13 + b)) for b in range(3)]
    logit_cols = [pl.BlockSpec((GATE_ROWS, D_MODEL), functools.partial(lambda b, i: (i, b), 8 + b)) for b in range(3)]
    return r512, r1024, silu_cols, logit_cols


def _gate_fwd(o_grp, l_grp, out_b, out_c, parts, merge_bias, wts):
    r512, r1024, silu_cols, logit_cols = _gate_specs()

    def body(o0, o1, o2, l0, l1, l2, ob, oc, ga, gb, gc, la, lb, lc, mb, wa, wb, wc,
             oa_ref, ua, ub, uc, za, zb, zc, y_ref):
        w0, w1, w2 = _group_weights(l0[...], l1[...], l2[...])
        out_a = w0 * o0[...] + w1 * o1[...] + w2 * o2[...]
        oa_ref[...] = out_a
        y = jnp.zeros((GATE_ROWS, D_MODEL), F32)
        for b, (ov, g_ref, l_ref, w_ref, u_ref, z_ref) in enumerate(
                ((out_a, ga, la, wa, ua, za), (ob[...], gb, lb, wb, ub, zb), (oc[...], gc, lc, wc, uc, zc))):
            g = g_ref[...].astype(F32)
            u = (ov * (g * _sigmoid(g))).astype(BF16)
            u_ref[...] = u
            z = lax.dot_general(u, w_ref[...], NT, preferred_element_type=F32)
            z_ref[...] = z.astype(BF16)
            gate = _sigmoid(l_ref[...].astype(F32) + mb[b:b + 1, :])
            y = y + gate * z
        y_ref[...] = y.astype(BF16)

    full = lambda shape: pl.BlockSpec(shape, lambda i: (0,) * len(shape))
    in_specs = ([r512] * 8 + silu_cols + logit_cols
                + [full((3, D_MODEL))] + [full((D_MODEL, 512))] * 3)
    out_specs = [r512] * 4 + [r1024] * 4
    out_shape = ([jax.ShapeDtypeStruct((SEQ, 512), F32)] + [jax.ShapeDtypeStruct((SEQ, 512), BF16)] * 3
                 + [jax.ShapeDtypeStruct((SEQ, D_MODEL), BF16)] * 4)
    res = pl.pallas_call(
        body, name="gate_fwd", grid=(SEQ // GATE_ROWS,), in_specs=in_specs, out_specs=out_specs,
        out_shape=out_shape, compiler_params=_params(("parallel",)))(
            *o_grp, *l_grp, out_b, out_c, parts, parts, parts, parts, parts, parts, merge_bias, *wts)
    return res[0], res[1:4], res[4:7], res[7]


def _gate_bwd(dy, z, parts, merge_bias, outs, o_grp, l_grp, wts, head_sum):
    r512, r1024, silu_cols, logit_cols = _gate_specs()

    def body(dy_ref, za, zb, zc, la, lb, lc, mb, oa, ob, oc, ga, gb, gc, o0, o1, o2, l0, l1, l2, wa, wb, wc, hs_ref,
             dla, dlb, dlc, gmb, dza, dzb, dzc, dga, dgb, dgc, do0, do1, do2, dp0, dp1, dp2, dob, doc):
        dyv = dy_ref[...].astype(F32)
        rows = []
        dos = []
        for b, (z_ref, l_ref, ov_ref, g_ref, w_ref, dl_ref, dz_ref, dg_ref) in enumerate(
                ((za, la, oa, ga, wa, dla, dza, dga), (zb, lb, ob, gb, wb, dlb, dzb, dgb),
                 (zc, lc, oc, gc, wc, dlc, dzc, dgc))):
            gate = _sigmoid(l_ref[...].astype(F32) + mb[b:b + 1, :])
            dl = dyv * z_ref[...].astype(F32) * gate * (1.0 - gate)
            dl_ref[...] = dl.astype(BF16)
            rows.append(jnp.sum(dl, axis=0, keepdims=True))
            dz = (dyv * gate).astype(BF16)
            dz_ref[...] = dz
            du = jnp.dot(dz, w_ref[...], preferred_element_type=F32)
            g = g_ref[...].astype(F32)
            sg = _sigmoid(g)
            dos.append(du * (g * sg))
            dg_ref[...] = (du * ov_ref[...] * (sg * (1.0 + g * (1.0 - sg)))).astype(BF16)

        @pl.when(pl.program_id(0) == 0)
        def _():
            gmb[...] = jnp.zeros((3, D_MODEL), F32)

        for b in range(3):
            gmb[b:b + 1, :] += rows[b]
        dob[...] = dos[1].astype(BF16)
        doc[...] = dos[2].astype(BF16)
        doa = dos[0]
        row_term = jnp.dot(doa * oa[...], hs_ref[...], precision=lax.Precision.HIGHEST, preferred_element_type=F32)
        ws = _group_weights(l0[...], l1[...], l2[...])
        for wg, do_ref, dp_ref in zip(ws, (do0, do1, do2), (dp0, dp1, dp2)):
            do_ref[...] = wg * doa
            dp_ref[...] = wg * row_term

    full = lambda shape: pl.BlockSpec(shape, lambda i: (0,) * len(shape))
    acc = pl.BlockSpec((3, D_MODEL), lambda i: (0, 0))
    in_specs = ([r1024] * 4 + logit_cols + [full((3, D_MODEL))] + [r512] * 3 + silu_cols + [r512] * 6
                + [full((D_MODEL, 512))] * 3 + [full((512, 512))])
    out_specs = [r1024] * 3 + [acc] + [r1024] * 3 + [r512] * 11
    out_shape = ([jax.ShapeDtypeStruct((SEQ, D_MODEL), BF16)] * 3 + [jax.ShapeDtypeStruct((3, D_MODEL), F32)]
                 + [jax.ShapeDtypeStruct((SEQ, D_MODEL), BF16)] * 3 + [jax.ShapeDtypeStruct((SEQ, 512), BF16)] * 3
                 + [jax.ShapeDtypeStruct((SEQ, 512), F32)] * 6 + [jax.ShapeDtypeStruct((SEQ, 512), BF16)] * 2)
    res = pl.pallas_call(
        body, name="gate_bwd", grid=(SEQ // GATE_ROWS,), in_specs=in_specs, out_specs=out_specs,
        out_shape=out_shape, compiler_params=_params(("arbitrary",)))(
            dy, *z, parts, parts, parts, merge_bias, *outs, parts, parts, parts, *o_grp, *l_grp, *wts, head_sum)
    return res[0:3], res[3], res[4:7], res[7:10], res[10:13], res[13:16], res[16], res[17]


def _post(y2, x, target, gain):
    rows = 256

    def body(y_ref, x_ref, t_ref, g_ref, do_ref, dy_ref, l_ref, gg_ref):
        yv = y_ref[...]
        rstd = lax.rsqrt(jnp.mean(yv * yv, axis=1, keepdims=True) + EPS)
        yn = yv * rstd
        gv = g_ref[...]
        err = x_ref[...] + yn * gv - t_ref[...]
        dout = err * (1.0 / D_MODEL)
        do_ref[...] = dout
        dn = dout * gv
        dy_ref[...] = (rstd * (dn - yn * jnp.mean(dn * yn, axis=1, keepdims=True))).astype(BF16)

        @pl.when(pl.program_id(0) == 0)
        def _():
            l_ref[...] = jnp.zeros((1, D_MODEL), F32)
            gg_ref[...] = jnp.zeros((1, D_MODEL), F32)

        l_ref[...] += jnp.sum(err * err, axis=0, keepdims=True)
        gg_ref[...] += jnp.sum(dout * yn, axis=0, keepdims=True)

    row = pl.BlockSpec((rows, D_MODEL), lambda i: (i, 0))
    vec = pl.BlockSpec((1, D_MODEL), lambda i: (0, 0))
    return pl.pallas_call(
        body, name="post", grid=(SEQ // rows,), in_specs=[row, row, row, vec], out_specs=[row, row, vec, vec],
        out_shape=[jax.ShapeDtypeStruct((SEQ, D_MODEL), F32), jax.ShapeDtypeStruct((SEQ, D_MODEL), BF16),
                   jax.ShapeDtypeStruct((1, D_MODEL), F32), jax.ShapeDtypeStruct((1, D_MODEL), F32)],
        compiler_params=_params(("arbitrary",)))(y2, x, target, gain)


def _local_step(x, mem, target, pre_norm, mem_norm, post_norm, na_rpb, wt_in, late_weights, dep_in=None,
                reduce_start=None):
    tabs = _rope_tables()
    hs, hst = _prenorm_fold(x, pre_norm)
    parts = _in_proj(hs, wt_in, tabs, dep_in)

    o_grp, l_grp = [], []
    for g, d in enumerate(DILATIONS):
        o, l = _attn_fwd("dil_fwd_%d" % g, "dil", parts, parts, parts, 12 * g, 12 * g + 4, 12 * g + 8, d=d)
        o_grp.append(o)
        l_grp.append(l)
    bias = _na_bias(jnp.pad(na_rpb, ((0, 0), (0, 1), (0, 128 - 31))))
    out_b, lse_b = _attn_fwd("na_fwd", "na", parts, parts, parts, 36, 40, 44, bias=bias)
    merge_bias, w_kv, wt_a, wt_b, wt_c, w_out = late_weights(out_b)
    memn = _rmsnorm_fwd("memnorm", mem, mem_norm, MEM_LEN)
    kv_m = _mm_simple("mem_kv", memn, w_kv, NN, BF16, MEM_LEN, 512, D_MODEL)
    out_c, lse_c = _attn_fwd("mem_fwd", "mem", parts, kv_m, kv_m, 48, 0, 4)

    wts = (wt_a, wt_b, wt_c)
    out_a, u, z, y = _gate_fwd(o_grp, l_grp, out_b, out_c, parts, merge_bias, wts)
    y2 = _mm_simple("out_proj", y, w_out, NN, F32, 512, D_MODEL, D_MODEL)
    dout, dy2, err_sq, g_post = _post(y2, x, target, post_norm)
    loss = 0.5 * jnp.sum(err_sq) / D_MODEL

    dy = _mm_simple("out_proj_dx", dy2, w_out, NT, BF16, 512, D_MODEL, D_MODEL)
    g_w_out = _mm_simple("out_proj_dw", y, dy2, TN, BF16, D_MODEL, 512, 512)

    rr = _iota((512, 512), 0) // HEAD_DIM
    cc = _iota((512, 512), 1) // HEAD_DIM
    head_sum = (rr == cc).astype(F32)
    dlog, g_mb, dz, dg, do_grp, dp_grp, do_b, do_c = _gate_bwd(
        dy, z, parts, merge_bias, (out_a, out_b, out_c), o_grp, l_grp, wts, head_sum)
    g_wt = [_mm_simple("branch_dw_%d" % b, dz[b], u[b], TN, BF16, D_MODEL, 512, 512) for b in range(3)]

    dqkv = []
    for g, d in enumerate(DILATIONS):
        dq, dk, dv = _attn_bwd("dil_bwd_%d" % g, "dil", parts, parts, parts, 12 * g, 12 * g + 4, 12 * g + 8,
                               do_grp[g], l_grp[g], dp=dp_grp[g], d=d, tabs=tabs[g])
        dqkv += [dq, dk, dv]
    dq_b, dk_b, dv_b, dbias = _attn_bwd("na_bwd", "na", parts, parts, parts, 36, 40, 44, do_b, lse_b, o=out_b,
                                        bias=bias)
    g_rpb_t = _na_bias_bwd(dbias)
    g_rpb = g_rpb_t[:, :15, :31] + jnp.pad(g_rpb_t[:, :14, 64:95], ((0, 0), (1, 0), (0, 0)))
    dq_c, dk_m, dv_m = _attn_bwd("mem_bwd", "mem", parts, kv_m, kv_m, 48, 0, 4, do_c, lse_c, o=out_c)

    dkv = jnp.concatenate([dk_m, dv_m], axis=1).astype(BF16)
    g_w_kv = _mm_simple("mem_kv_dw", memn, dkv, TN, BF16, D_MODEL, 512, MEM_LEN)
    dmemn = _mm_simple("mem_kv_dx", dkv, w_kv, NT, F32, MEM_LEN, 512, D_MODEL)
    g_mem_norm = _memnorm_bwd(mem, dmemn)

    grads = dict(w_kv=g_w_kv, wt_a=g_wt[0], wt_b=g_wt[1], wt_c=g_wt[2], w_out=g_w_out, merge_bias=g_mb,
                 mem_norm=g_mem_norm, post_norm=g_post, na_rpb=g_rpb)
    dep = reduce_start(grads) if reduce_start is not None else None
    dparts = dqkv + [dq_b, dk_b, dv_b, dq_c] + list(dg) + list(dlog)
    grads["wt_in"] = _in_proj_dw(dparts, hst, dep)
    dep = reduce_start(grads) if reduce_start is not None else None
    dh = _in_proj_dh(dparts, wt_in, dep)
    grad_x, grads["pre_norm"] = _prenorm_bwd(x, pre_norm, dh, dout)
    return loss, grad_x, grads


ANY = pl.BlockSpec(memory_space=pl.ANY)


def _place():
    return lax.axis_index("x"), lax.axis_index("y"), lax.axis_index("c")


def _all_gather(shard):
    r = shard.shape[0]
    half = r // 2

    def body(src, out, send_sems, recv_sems, local_sem):
        x, y, c = _place()
        me, sib = (x, y, c), (x, y, 1 - c)
        xn, yn, dg = (1 - x, y, c), (x, 1 - y, c), (1 - x, 1 - y, c)

        def rows(dev, part=None):
            blk = out.at[4 * dev[0] + 2 * dev[1] + dev[2]]
            return blk if part is None else blk.at[pl.ds(part * half, half)]

        def copy(k, dev, part, to, own=False):
            return pltpu.make_async_remote_copy(
                src_ref=src if own else rows(dev, part), dst_ref=rows(dev, part),
                send_sem=send_sems.at[k], recv_sem=recv_sems.at[k], device_id=to, device_id_type=MESH_ID)

        def other(dev):
            return (dev[0], dev[1], 1 - dev[2])

        mine = pltpu.make_async_copy(src, rows(me), local_sem)
        mine.start()
        sent = [copy(0, me, None, sib, own=True), copy(1, me, None, xn, own=True), copy(2, me, None, yn, own=True)]
        for cp in sent:
            cp.start()
        copy(1, xn, None, me).wait_recv()
        sent += [copy(3, xn, 0, yn), copy(5, xn, None, sib)]
        sent[-2].start()
        sent[-1].start()
        copy(2, yn, None, me).wait_recv()
        sent += [copy(4, yn, 1, xn), copy(6, yn, None, sib)]
        sent[-2].start()
        sent[-1].start()
        copy(3, dg, 0, me).wait_recv()
        sent.append(copy(7, dg, 0, sib))
        sent[-1].start()
        copy(4, dg, 1, me).wait_recv()
        sent.append(copy(8, dg, 1, sib))
        sent[-1].start()
        copy(0, sib, None, me).wait_recv()
        copy(5, other(xn), None, me).wait_recv()
        copy(6, other(yn), None, me).wait_recv()
        copy(7, other(dg), 0, me).wait_recv()
        copy(8, other(dg), 1, me).wait_recv()
        for cp in sent:
            cp.wait_send()
        mine.wait()

    return pl.pallas_call(
        body, name="all_gather", in_specs=[ANY], out_specs=ANY,
        out_shape=jax.ShapeDtypeStruct((N_DEV,) + shard.shape, shard.dtype),
        scratch_shapes=[pltpu.SemaphoreType.DMA((9,)), pltpu.SemaphoreType.DMA((9,)), pltpu.SemaphoreType.DMA])(shard)


def _exchange_sibling(name, terms):
    nt = len(terms)

    def body(*refs):
        srcs, outs = refs[:nt], refs[nt:2 * nt]
        send_sems, recv_sems = refs[2 * nt:]
        x, y, c = _place()
        copies = []
        for q in range(4):
            for t in range(nt):
                copies.append(pltpu.make_async_remote_copy(
                    src_ref=srcs[t].at[2 * q + 1 - c], dst_ref=outs[t].at[q],
                    send_sem=send_sems.at[q * nt + t], recv_sem=recv_sems.at[q * nt + t],
                    device_id=(x, y, 1 - c), device_id_type=MESH_ID))
        for cp in copies:
            cp.start()
        for cp in copies:
            cp.wait()

    return pl.pallas_call(
        body, name=name, in_specs=[ANY] * nt, out_specs=[ANY] * nt,
        out_shape=[jax.ShapeDtypeStruct((4,) + s.shape[1:], s.dtype) for s in terms],
        scratch_shapes=[pltpu.SemaphoreType.DMA((4 * nt,)), pltpu.SemaphoreType.DMA((4 * nt,))])(*terms)


HBM = pl.BlockSpec(memory_space=pltpu.HBM)
SEM = pl.BlockSpec(memory_space=pltpu.SEMAPHORE)
DATAFLOW = pltpu.SideEffectType.DATAFLOW_SIDE_EFFECTING


def _split_copies(kind, srcs, lands, send_sems, recv_sems):
    nt = len(srcs)
    x, y, c = _place()
    copies = []
    if kind == "gather":
        me = 4 * x + 2 * y + c
        for mask in range(1, 8):
            fx, fy, fc = (mask >> 2) & 1, (mask >> 1) & 1, mask & 1
            to = (1 - x if fx else x, 1 - y if fy else y, 1 - c if fc else c)
            for t in range(nt):
                k = (mask - 1) * nt + t
                copies.append(pltpu.make_async_remote_copy(
                    src_ref=srcs[t], dst_ref=lands[t].at[me], send_sem=send_sems.at[k], recv_sem=recv_sems.at[k],
                    device_id=to, device_id_type=MESH_ID))
    else:
        for s, (tx, ty) in enumerate([(1 - x, y), (x, 1 - y), (1 - x, 1 - y)]):
            for t in range(nt):
                k = s * nt + t
                copies.append(pltpu.make_async_remote_copy(
                    src_ref=srcs[t].at[2 * tx + ty], dst_ref=lands[t].at[s], send_sem=send_sems.at[k],
                    recv_sem=recv_sems.at[k], device_id=(tx, ty, c), device_id_type=MESH_ID))
    return copies


def _split_count(kind, nt):
    return (7 if kind == "gather" else 3) * nt


def _exchange_start(name, kind, srcs, land_shapes, after=None):
    nt = len(srcs)
    n = _split_count(kind, nt)
    dep_specs, dep_args = _dep_operand(after)
    nd = len(dep_args)

    def body(*refs):
        src_refs, land_refs = refs[:nt], refs[nt:2 * nt]
        send_sems, recv_sems = refs[2 * nt + nd], refs[2 * nt + nd + 1]
        token = refs[-1]
        for cp in _split_copies(kind, src_refs, land_refs, send_sems, recv_sems):
            cp.start()
        token[...] = jnp.zeros_like(token)

    lands = [pltpu.with_memory_space_constraint(lax.empty(s.shape, s.dtype), pltpu.HBM) for s in land_shapes]
    res = pl.pallas_call(
        body, name=name,
        out_shape=(pltpu.SemaphoreType.DMA((n,)), pltpu.SemaphoreType.DMA((n,)),
                   *[pltpu.HBM(s.shape, s.dtype) for s in srcs], *[pltpu.HBM(s.shape, s.dtype) for s in land_shapes],
                   jax.ShapeDtypeStruct((8, 128), F32)),
        in_specs=[HBM] * (2 * nt) + dep_specs,
        out_specs=(SEM, SEM, *([HBM] * (2 * nt)), pl.BlockSpec(memory_space=pltpu.VMEM)),
        input_output_aliases={i: 2 + i for i in range(2 * nt)},
        compiler_params=pltpu.CompilerParams(has_side_effects=DATAFLOW))(
            *[pltpu.with_memory_space_constraint(s, pltpu.HBM) for s in srcs], *lands, *dep_args)
    return res[0], res[1], list(res[2:2 + nt]), list(res[2 + nt:2 + 2 * nt]), res[-1]


def _exchange_wait(name, kind, send_sems, recv_sems, srcs, lands, after):
    nt = len(srcs)

    def body(*refs):
        src_refs, land_refs = refs[:nt], refs[nt:2 * nt]
        s_sems, r_sems = refs[2 * nt], refs[2 * nt + 1]
        for cp in _split_copies(kind, src_refs, land_refs, s_sems, r_sems):
            cp.wait_send()
            cp.wait_recv()

    res = pl.pallas_call(
        body, name=name,
        out_shape=tuple(pltpu.HBM(s.shape, s.dtype) for s in list(srcs) + list(lands)),
        in_specs=[HBM] * (2 * nt) + [SEM, SEM, pl.BlockSpec(memory_space=pl.ANY)],
        out_specs=tuple([HBM] * (2 * nt)),
        input_output_aliases={i: i for i in range(2 * nt)},
        compiler_params=pltpu.CompilerParams(has_side_effects=DATAFLOW))(
            *srcs, *lands, send_sems, recv_sems, after)
    return list(res[:nt]), list(res[nt:])


def _add_sibling(name, term, recv, rows):
    _, r, w = term.shape
    cidx = lax.axis_index("c").astype(jnp.int32).reshape(1)
    like_term = recv.shape[0] == N_DEV

    def body(c_ref, a_ref, b_ref, o_ref):
        o_ref[...] = (a_ref[...].astype(F32) + b_ref[...].astype(F32)).astype(o_ref.dtype)

    grid_spec = pltpu.PrefetchScalarGridSpec(
        num_scalar_prefetch=1, grid=(4, r // rows),
        in_specs=[pl.BlockSpec((None, rows, w), lambda q, i, c_ref: (2 * q + c_ref[0], i, 0)),
                  pl.BlockSpec((None, rows, w), lambda q, i, c_ref: (2 * q + c_ref[0] if like_term else q, i, 0))],
        out_specs=pl.BlockSpec((None, rows, w), lambda q, i, c_ref: (q, i, 0)))
    return pl.pallas_call(
        body, name=name, grid_spec=grid_spec, out_shape=jax.ShapeDtypeStruct((4, r, w), term.dtype),
        compiler_params=_params(("parallel", "parallel")))(cidx, term, recv)


def _add_sibling_small(name, terms, recvs):
    nt = len(terms)

    def body(*refs):
        c = lax.axis_index("c")
        for t_ref, r_ref, o_ref in zip(refs[:nt], refs[nt:2 * nt], refs[2 * nt:]):
            for q in range(4):
                o_ref[q] = (t_ref[2 * q + c].astype(F32) + r_ref[q].astype(F32)).astype(o_ref.dtype)

    return pl.pallas_call(
        body, name=name, out_shape=[jax.ShapeDtypeStruct((4,) + t.shape[1:], t.dtype) for t in terms],
        compiler_params=_params())(*terms, *recvs)


def _add_chips(name, sums, recv, rows):
    _, r, w = sums.shape
    qidx = (2 * lax.axis_index("x") + lax.axis_index("y")).astype(jnp.int32).reshape(1)

    def body(q_ref, a_ref, b_ref, o_ref):
        o_ref[...] = ((a_ref[...].astype(F32) + b_ref[0].astype(F32))
                      + (b_ref[1].astype(F32) + b_ref[2].astype(F32)))

    grid_spec = pltpu.PrefetchScalarGridSpec(
        num_scalar_prefetch=1, grid=(r // rows,),
        in_specs=[pl.BlockSpec((None, rows, w), lambda i, q_ref: (q_ref[0], i, 0)),
                  pl.BlockSpec((3, rows, w), lambda i, q_ref: (0, i, 0))],
        out_specs=pl.BlockSpec((rows, w), lambda i, q_ref: (i, 0)))
    return pl.pallas_call(
        body, name=name, grid_spec=grid_spec, out_shape=jax.ShapeDtypeStruct((r, w), F32),
        compiler_params=_params(("parallel",)))(qidx, sums, recv)


def _rs_rows(a):
    return SHARD_IN // 4 if a.shape[1] == SHARD_IN else a.shape[1]


def _reduce_scatter_start(tag, names, terms, recv1=None):
    if recv1 is None:
        recv1 = _exchange_sibling("exchange_sibling_" + tag, terms)
    if len(terms) == 1:
        sums = [_add_sibling("add_sibling_" + names[0], terms[0], recv1[0], _rs_rows(terms[0]))]
    else:
        sums = _add_sibling_small("add_sibling_" + tag, terms, recv1)
    lands =[jax.ShapeDtypeStruct((3,) + s.shape[1:], s.dtype) for s in sums]
    send_sems, recv_sems, sums, lands, token = _exchange_start("exchange_chips_start_" + tag, "chips", sums, lands)
    return (tag, names, send_sems, recv_sems, sums, lands), token


def _reduce_scatter_wait(state, after):
    tag, names, send_sems, recv_sems, sums, lands = state
    sums, recv2 = _exchange_wait("exchange_chips_wait_" + tag, "chips", send_sems, recv_sems, sums, lands, after)
    return names, sums, recv2


def _adamw(name, w, g, m, v):
    def body(w_ref, g_ref, m_ref, v_ref, d_ref, nm_ref, nv_ref):
        d_ref[...], nm_ref[...], nv_ref[...] = _adam_math(w_ref[...], g_ref[...], m_ref[...], v_ref[...])

    return pl.pallas_call(
        body, name=name, out_shape=[jax.ShapeDtypeStruct(w.shape, F32)] * 3, compiler_params=_params())(w, g, m, v)


def _adam_math(w, g, m, v):
    nm = ADAM_B1 * m + (1.0 - ADAM_B1) * g
    nv = ADAM_B2 * v + (1.0 - ADAM_B2) * (g * g)
    c1 = 1.0 - ADAM_B1 ** ADAM_STEP
    c2 = 1.0 - ADAM_B2 ** ADAM_STEP
    return -ADAM_LR * ((nm / c1) / (jnp.sqrt(nv / c2) + ADAM_EPS) + ADAM_WD * w), nm, nv


def _adamw_chips(name, sums, recv, w, m, v, transposed, rows=None, dep=None):
    r, c = w.shape
    rows = r if rows is None else rows
    qidx = (2 * lax.axis_index("x") + lax.axis_index("y")).astype(jnp.int32).reshape(1)
    dep_specs, dep_args = _dep_operand(dep)

    def body(q_ref, a_ref, b_ref, w_ref, m_ref, v_ref, *rest):
        g_ref, d_ref, nm_ref, nv_ref = rest[-4:]
        g = (a_ref[...].astype(F32) + b_ref[0].astype(F32)) + (b_ref[1].astype(F32) + b_ref[2].astype(F32))
        if transposed:
            g = g.T
        g_ref[...] = g
        d_ref[...], nm_ref[...], nv_ref[...] = _adam_math(w_ref[...], g, m_ref[...], v_ref[...])

    row = pl.BlockSpec((rows, c), lambda i, q_ref: (i, 0))
    if transposed:
        term_specs = [pl.BlockSpec((None, c, rows), lambda i, q_ref: (q_ref[0], 0, i)),
                      pl.BlockSpec((3, c, rows), lambda i, q_ref: (0, 0, i))]
    else:
        term_specs = [pl.BlockSpec((None, rows, c), lambda i, q_ref: (q_ref[0], i, 0)),
                      pl.BlockSpec((3, rows, c), lambda i, q_ref: (0, i, 0))]
    grid_spec = pltpu.PrefetchScalarGridSpec(
        num_scalar_prefetch=1, grid=(r // rows,), in_specs=term_specs + [row, row, row] + dep_specs,
        out_specs=[row] * 4)
    return pl.pallas_call(
        body, name=name, grid_spec=grid_spec, out_shape=[jax.ShapeDtypeStruct((r, c), F32)] * 4,
        compiler_params=_params(("parallel",)))(qidx, sums, recv, w, m, v, *dep_args)


def _sum_devices(gathered):
    def body(g_ref, o_ref):
        acc = g_ref[0]
        for j in range(1, N_DEV):
            acc = acc + g_ref[j]
        o_ref[...] = acc

    return pl.pallas_call(
        body, name="sum_devices", out_shape=jax.ShapeDtypeStruct(gathered.shape[1:], F32),
        compiler_params=_params())(gathered)


def _rows128(a, rows):
    flat = a.reshape(-1)
    return jnp.pad(flat, (0, rows * 128 - flat.shape[0])).reshape(rows, 128)


def kernel(x, mem, pre_norm, w_in, merge_bias, na_rpb, mem_norm, w_mem_kv, w_branch_a, w_branch_b, w_branch_c, w_out, post_norm, loss_target, m_pre_norm, m_w_in, m_merge_bias, m_na_rpb, m_mem_norm, m_w_mem_kv, m_w_branch_a, m_w_branch_b, m_w_branch_c, m_w_out, m_post_norm, v_pre_norm, v_w_in, v_merge_bias, v_na_rpb, v_mem_norm, v_w_mem_kv, v_w_branch_a, v_w_branch_b, v_w_branch_c, v_w_out, v_post_norm):
    wt_in_s = w_in[0].T.astype(BF16)
    rows_s = jnp.concatenate([w_mem_kv[0], w_out[0]], axis=0).astype(BF16)
    cols_s = jnp.concatenate([w_branch_a[0].T, w_branch_b[0].T, w_branch_c[0].T], axis=0).astype(BF16)
    mb_s = jnp.pad(merge_bias[0], ((0, 5), (0, 0)))
    wt_in = _all_gather(wt_in_s).reshape(N_IN, D_MODEL)

    late_own = [rows_s, cols_s, mb_s]
    late_lands = [jax.ShapeDtypeStruct((N_DEV,) + s.shape, s.dtype) for s in late_own]
    l_send, l_recv, late_own, late_lands, late_token = _exchange_start("gather_late_start", "gather", late_own,
                                                                       late_lands, after=wt_in)
    me = 4 * lax.axis_index("x") + 2 * lax.axis_index("y") + lax.axis_index("c")

    def late_weights(after):
        own, lands = _exchange_wait("gather_late_wait", "gather", l_send, l_recv, late_own, late_lands, after)
        g_rows, g_cols, g_mb = [lax.dynamic_update_slice(land, o[None], (me, 0, 0)) for land, o in zip(lands, own)]
        return (g_mb[:, :3].transpose(1, 0, 2).reshape(3, D_MODEL),
                g_rows[:, :128].reshape(D_MODEL, D_MODEL), g_cols[:, 0:128].reshape(D_MODEL, 512),
                g_cols[:, 128:256].reshape(D_MODEL, 512), g_cols[:, 256:384].reshape(D_MODEL, 512),
                g_rows[:, 128:].reshape(D_MODEL, D_MODEL))

    rs_state = []

    def reduce_start(grads):
        if "wt_in" in grads:
            own, sibling = [a.reshape(N_DEV, SHARD_IN, D_MODEL) for a in grads["wt_in"]]
            state, token = _reduce_scatter_start("w_in", ["w_in"], [own], [sibling])
        else:
            gmb_t = jnp.pad(grads["merge_bias"].reshape(3, N_DEV, 128).transpose(1, 0, 2), ((0, 0), (0, 5), (0, 0)))
            names = ["w_kv", "w_out", "a", "b", "c", "mb"]
            terms = [grads["w_kv"].reshape(N_DEV, 128, D_MODEL), grads["w_out"].reshape(N_DEV, 128, D_MODEL),
                     grads["wt_a"].reshape(N_DEV, 128, 512), grads["wt_b"].reshape(N_DEV, 128, 512),
                     grads["wt_c"].reshape(N_DEV, 128, 512), gmb_t]
            state, token = _reduce_scatter_start("rest", names, terms)
        rs_state.append(state)
        return token

    loss_term, grad_x, grads = _local_step(
        x[0], mem[0], loss_target[0], pre_norm, mem_norm, post_norm, na_rpb[0], wt_in, late_weights,
        dep_in=late_token, reduce_start=reduce_start)

    small = jnp.concatenate([_rows128(grads["pre_norm"], 8), _rows128(grads["mem_norm"], 8),
                             _rows128(grads["post_norm"], 8), _rows128(grads["na_rpb"], 32),
                             _rows128(loss_term, 8)], axis=0)
    s_send, s_recv, s_own, s_land, s_token = _exchange_start(
        "gather_small_start", "gather", [small], [jax.ShapeDtypeStruct((N_DEV,) + small.shape, F32)])
    grad = {}
    weights = {
        "pre_norm": (pre_norm, m_pre_norm, v_pre_norm), "w_in": (w_in, m_w_in, v_w_in),
        "merge_bias": (merge_bias, m_merge_bias, v_merge_bias), "na_rpb": (na_rpb, m_na_rpb, v_na_rpb),
        "mem_norm": (mem_norm, m_mem_norm, v_mem_norm), "w_mem_kv": (w_mem_kv, m_w_mem_kv, v_w_mem_kv),
        "w_branch_a": (w_branch_a, m_w_branch_a, v_w_branch_a), "w_branch_b": (w_branch_b, m_w_branch_b, v_w_branch_b),
        "w_branch_c": (w_branch_c, m_w_branch_c, v_w_branch_c), "w_out": (w_out, m_w_out, v_w_out),
        "post_norm": (post_norm, m_post_norm, v_post_norm)}
    order = ["pre_norm", "w_in", "merge_bias", "na_rpb", "mem_norm", "w_mem_kv", "w_branch_a", "w_branch_b",
             "w_branch_c", "w_out", "post_norm"]
    delta, new_m, new_v = {}, {}, {}

    def update(n):
        w, m, v = weights[n]
        shape = w.shape
        two_d = (-1, shape[-1])
        dl, nm, nv = _adamw("adamw_" + n, w.reshape(two_d), grad[n].reshape(two_d), m.reshape(two_d),
                            v.reshape(two_d))
        delta[n], new_m[n], new_v[n] = dl.reshape(shape), nm.reshape(shape), nv.reshape(shape)

    def update_sharded(n, sums, recv, transposed, rows=None, dep=None):
        w, m, v = weights[n]
        g, dl, nm, nv = _adamw_chips("adamw_" + n, sums, recv, w[0], m[0], v[0], transposed, rows, dep)
        grad[n], delta[n], new_m[n], new_v[n] = g[None], dl[None], nm[None], nv[None]
        return dl

    _, sums, recv2 = _reduce_scatter_wait(rs_state[0], s_token)
    dep = None
    for i, (n, transposed) in enumerate((("w_mem_kv", False), ("w_out", False), ("w_branch_a", True),
                                         ("w_branch_b", True), ("w_branch_c", True))):
        dep = update_sharded(n, sums[i], recv2[i], transposed, dep=dep)
    grad["merge_bias"] = _add_chips("add_chips_mb", sums[5], recv2[5], 8)[:3][None]
    update("merge_bias")
    s_own, s_land = _exchange_wait("gather_small_wait", "gather", s_send, s_recv, s_own, s_land, dep)
    total = _sum_devices(lax.dynamic_update_slice(s_land[0], s_own[0][None], (me, 0, 0)))
    loss = total[56, 0]
    grad.update({"pre_norm": total[0:8].reshape(1, D_MODEL), "mem_norm": total[8:16].reshape(1, D_MODEL),
                 "post_norm": total[16:24].reshape(1, D_MODEL),
                 "na_rpb": total[24:56].reshape(-1)[:8 * 15 * 31].reshape(1, 8, 15, 31)})
    for n in ("pre_norm", "na_rpb", "mem_norm", "post_norm"):
        update(n)
    _, sums_in, recv_in = _reduce_scatter_wait(rs_state[1], delta["post_norm"])
    update_sharded("w_in", sums_in[0], recv_in[0], True, 256)

    return (loss, grad_x[None], *[grad[n] for n in order], *[delta[n] for n in order],
            *[new_m[n] for n in order], *[new_v[n] for n in order])
```

```python
import functools

import numpy as np
import jax
import jax.numpy as jnp
from jax import lax
from jax.experimental import pallas as pl
from jax.experimental.pallas import tpu as pltpu

F32 = jnp.float32
BF16 = jnp.bfloat16

SEQ = 2048
D_MODEL = 1024
N_IN = 11264
N_DEV = 8
SHARD_IN = N_IN // N_DEV
HEAD_DIM = 64
GRID_W = 64
NA_ROWS = 8
MEM_LEN = 256
DILATIONS = (1, 4, 16)
REACH = 64
ROPE_THETA = 500000.0
ROPE_DIM = 16
EPS = 1e-6
NEG = -1e30
ADAM_LR = 0.001
ADAM_B1 = 0.9
ADAM_B2 = 0.999
ADAM_EPS = 1e-08
ADAM_WD = 0.01
ADAM_STEP = 10

VMEM_LIMIT_BYTES = 56 * 1024 * 1024
MESH_ID = pl.DeviceIdType.MESH

NN = (((1,), (0,)), ((), ()))
NT = (((1,), (1,)), ((), ()))
TN = (((0,), (0,)), ((), ()))


def _params(sem=None):
    return pltpu.CompilerParams(dimension_semantics=sem, vmem_limit_bytes=VMEM_LIMIT_BYTES)


def _iota(shape, dim):
    return lax.broadcasted_iota(jnp.int32, shape, dim)


def _sigmoid(x):
    return 1.0 / (1.0 + jnp.exp(-x))


def _rope_tables():
    half = ROPE_DIM // 2
    inv = (ROPE_THETA ** (-np.arange(half, dtype=np.float64) * 2.0 / ROPE_DIM)).astype(np.float32)
    pos = np.arange(SEQ, dtype=np.float32)
    ang = pos[:, None] * inv[None, :]
    cos, sin = np.cos(ang), np.sin(ang)
    zeros = np.zeros_like(cos)
    rest = HEAD_DIM - ROPE_DIM
    c64 = np.concatenate([cos, cos, np.ones((SEQ, rest), np.float32)], axis=1)
    s1 = np.concatenate([zeros, sin, np.zeros((SEQ, rest), np.float32)], axis=1)
    s2 = np.concatenate([-sin, zeros, np.zeros((SEQ, rest), np.float32)], axis=1)

    def fold(t, d):
        return t.reshape(SEQ // d, d, t.shape[1]).transpose(1, 0, 2).reshape(SEQ, t.shape[1])

    tabs = [np.stack([np.tile(fold(t, d), (1, 2)) for t in (c64, s1, s2)], axis=0) for d in DILATIONS]
    return jnp.asarray(np.stack(tabs, axis=0), dtype=F32)


def _rope(a, c, s1, s2):
    return a * c + pltpu.roll(a, 8, 1) * s1 + pltpu.roll(a, 120, 1) * s2


def _rope_t(a, c, s1, s2):
    return a * c + pltpu.roll(a * s1, 120, 1) + pltpu.roll(a * s2, 8, 1)


def _perm_of_block(j):
    return jnp.where(j < 3, 0, jnp.where(j < 6, 1, jnp.where(j < 9, 2, 0)))


def _mm(name, a, b, out_shape, out_dtype, grid, a_spec, b_spec, o_spec, acc_shape, dims, k_axis, nk):
    def body(a_ref, b_ref, o_ref, acc_ref):
        k = pl.program_id(k_axis)

        @pl.when(k == 0)
        def _():
            acc_ref[...] = jnp.zeros(acc_shape, F32)

        acc_ref[...] += lax.dot_general(a_ref[...], b_ref[...], dims, preferred_element_type=F32)

        @pl.when(k == nk - 1)
        def _():
            o_ref[...] = acc_ref[...].astype(out_dtype)

    sem = tuple("arbitrary" if ax == k_axis else "parallel" for ax in range(len(grid)))
    return pl.pallas_call(
        body, name=name, grid=grid, in_specs=[a_spec, b_spec], out_specs=o_spec,
        out_shape=jax.ShapeDtypeStruct(out_shape, out_dtype),
        scratch_shapes=[pltpu.VMEM(acc_shape, F32)], compiler_params=_params(sem))(a, b)


def _mm_simple(name, a, b, dims, out_dtype, tm, tn, tk):
    if dims is NN:
        m, kk = a.shape
        n = b.shape[1]
        a_spec = pl.BlockSpec((tm, tk), lambda i, j, k: (i, k))
        b_spec = pl.BlockSpec((tk, tn), lambda i, j, k: (k, j))
    elif dims is NT:
        m, kk = a.shape
        n = b.shape[0]
        a_spec = pl.BlockSpec((tm, tk), lambda i, j, k: (i, k))
        b_spec = pl.BlockSpec((tn, tk), lambda i, j, k: (j, k))
    else:
        kk, m = a.shape
        n = b.shape[1]
        a_spec = pl.BlockSpec((tk, tm), lambda i, j, k: (k, i))
        b_spec = pl.BlockSpec((tk, tn), lambda i, j, k: (k, j))
    grid = (m // tm, n // tn, kk // tk)
    o_spec = pl.BlockSpec((tm, tn), lambda i, j, k: (i, j))
    return _mm(name, a, b, (m, n), out_dtype, grid, a_spec, b_spec, o_spec, (tm, tn), dims, 2, kk // tk)


def _rmsnorm_fwd(name, x, gain, rows):
    n, d = x.shape

    def body(x_ref, g_ref, o_ref):
        xv = x_ref[...]
        rstd = lax.rsqrt(jnp.mean(xv * xv, axis=1, keepdims=True) + EPS)
        o_ref[...] = (xv * rstd * g_ref[...]).astype(BF16)

    return pl.pallas_call(
        body, name=name, grid=(n // rows,),
        in_specs=[pl.BlockSpec((rows, d), lambda i: (i, 0)), pl.BlockSpec((1, d), lambda i: (0, 0))],
        out_specs=pl.BlockSpec((rows, d), lambda i: (i, 0)),
        out_shape=jax.ShapeDtypeStruct((n, d), BF16), compiler_params=_params(("parallel",)))(x, gain)


def _folded_rows(first, rows, d):
    if d == 1:
        return pl.ds(pl.multiple_of(first, rows), rows)
    mlen = SEQ // d
    return pl.ds((first % mlen) * d + first // mlen, rows, stride=d)


def _prenorm_fold(x, gain):
    rows = 128

    nchunk = D_MODEL // 128

    def body(*refs):
        x_refs, g_ref, hs_ref, hst_ref = refs[:nchunk], refs[nchunk], refs[nchunk + 1], refs[nchunk + 2]
        first = pl.program_id(0) * rows
        for p, d in enumerate(DILATIONS):
            idx = _folded_rows(first, rows, d)
            xv = jnp.concatenate([r[idx, :] for r in x_refs], axis=1)
            rstd = lax.rsqrt(jnp.mean(xv * xv, axis=1, keepdims=True) + EPS)
            h = xv * rstd * g_ref[...]
            hs_ref[p] = h.astype(BF16)
            hst_ref[p] = h.T.astype(BF16)

    x_specs = [pl.BlockSpec((SEQ, 128), functools.partial(lambda c, i: (0, c), c)) for c in range(nchunk)]
    return pl.pallas_call(
        body, name="prenorm", grid=(SEQ // rows,),
        in_specs=x_specs + [pl.BlockSpec((1, D_MODEL), lambda i: (0, 0))],
        out_specs=[pl.BlockSpec((3, rows, D_MODEL), lambda i: (0, i, 0)),
                   pl.BlockSpec((3, D_MODEL, rows), lambda i: (0, 0, i))],
        out_shape=[jax.ShapeDtypeStruct((3, SEQ, D_MODEL), BF16), jax.ShapeDtypeStruct((3, D_MODEL, SEQ), BF16)],
        compiler_params=_params(("parallel",)))(*([x] * nchunk), gain)


def _prenorm_bwd(x, gain, dh, dout):
    rows = 256

    def body(x_ref, g_ref, a_ref, do_ref, dx_ref, gg_ref):
        xv = x_ref[...]
        rstd = lax.rsqrt(jnp.mean(xv * xv, axis=1, keepdims=True) + EPS)
        xn = xv * rstd
        dh = jnp.concatenate([a_ref[c] for c in range(D_MODEL // 128)], axis=1)
        gdh = dh * g_ref[...]
        dx_ref[...] = rstd * (gdh - xn * jnp.mean(gdh * xn, axis=1, keepdims=True)) + do_ref[...]

        @pl.when(pl.program_id(0) == 0)
        def _():
            gg_ref[...] = jnp.zeros((1, D_MODEL), F32)

        gg_ref[...] += jnp.sum(dh * xn, axis=0, keepdims=True)

    row = pl.BlockSpec((rows, D_MODEL), lambda i: (i, 0))
    vec = pl.BlockSpec((1, D_MODEL), lambda i: (0, 0))
    return pl.pallas_call(
        body, name="prenorm_bwd", grid=(SEQ // rows,),
        in_specs=[row, vec, pl.BlockSpec((D_MODEL // 128, rows, 128), lambda i: (0, i, 0)), row], out_specs=[row, vec],
        out_shape=[jax.ShapeDtypeStruct((SEQ, D_MODEL), F32), jax.ShapeDtypeStruct((1, D_MODEL), F32)],
        compiler_params=_params(("arbitrary",)))(x, gain, dh, dout)


def _memnorm_bwd(mem, dmemn, dep=None):
    dep_specs, dep_args = _dep_operand(dep)

    def body(m_ref, d_ref, *rest):
        mv = m_ref[...]
        rstd = lax.rsqrt(jnp.mean(mv * mv, axis=1, keepdims=True) + EPS)
        rest[-1][...] = jnp.sum(d_ref[...] * mv * rstd, axis=0, keepdims=True)

    whole = pl.BlockSpec(memory_space=pltpu.VMEM)
    return pl.pallas_call(
        body, name="memnorm_bwd", in_specs=[whole, whole] + dep_specs,
        out_shape=jax.ShapeDtypeStruct((1, D_MODEL), F32), compiler_params=_params())(mem, dmemn, *dep_args)


def _dep_operand(dep):
    return ([], []) if dep is None else ([pl.BlockSpec(memory_space=pl.ANY)], [dep])


def _in_proj(hs, wt, tabs, dep=None):
    tm, tn = 512, 512
    dep_specs, dep_args = _dep_operand(dep)

    def body(h_ref, w_ref, t_ref, *rest):
        o_ref = rest[-1]
        j = pl.program_id(0)
        is_rope = jnp.logical_and(j < 9, j % 3 != 2)
        row_slices = [slice(r * tm, (r + 1) * tm) for r in range(SEQ // tm)]

        def product(rs):
            return lax.dot_general(h_ref[rs, :], w_ref[...], NT, preferred_element_type=F32)

        @pl.when(is_rope)
        def _():
            for rs in row_slices:
                acc = product(rs)
                c, s1, s2 = t_ref[0, rs, :], t_ref[1, rs, :], t_ref[2, rs, :]
                for q in range(tn // 128):
                    a = acc[:, q * 128:(q + 1) * 128]
                    o_ref[rs, q * 128:(q + 1) * 128] = _rope(a, c, s1, s2).astype(BF16)

        @pl.when(jnp.logical_not(is_rope))
        def _():
            for rs in row_slices:
                o_ref[rs, :] = product(rs).astype(BF16)

    return pl.pallas_call(
        body, name="in_proj", grid=(N_IN // tn,),
        in_specs=[pl.BlockSpec((None, SEQ, D_MODEL), lambda j: (_perm_of_block(j), 0, 0)),
                  pl.BlockSpec((tn, D_MODEL), lambda j: (j, 0)),
                  pl.BlockSpec((None, 3, SEQ, 128), lambda j: (_perm_of_block(j), 0, 0, 0))] + dep_specs,
        out_specs=pl.BlockSpec((SEQ, tn), lambda j: (0, j)),
        out_shape=jax.ShapeDtypeStruct((SEQ, N_IN), BF16),
        compiler_params=_params(("parallel",)))(hs, wt, tabs, *dep_args)


def _piece_blocks(pieces):
    return [(a, h * 512) for a, p in enumerate(pieces) for h in range(p.shape[1] // 512)]


def _block_fetch(piece_refs, blocks, buf, sem):
    def start(block, slot):
        for b, (a, col) in enumerate(blocks):
            @pl.when(block == b)
            def _():
                pltpu.make_async_copy(piece_refs[a].at[:, pl.ds(col, 512)], buf.at[slot], sem.at[slot]).start()

    def wait(slot):
        pltpu.make_async_copy(piece_refs[0].at[:, pl.ds(0, 512)], buf.at[slot], sem.at[slot]).wait()

    return start, wait


def _in_proj_dw(pieces, hst, dep=None):
    tn = 512
    blocks = _piece_blocks(pieces)
    nblk = len(blocks)
    npc = len(pieces)
    dep_specs, dep_args = _dep_operand(dep)

    def body(h_ref, *rest):
        piece_refs = rest[:npc]
        o_ref, mirror, buf, sem, out_buf, send_sems, recv_sem = rest[-7:]
        j = pl.program_id(0)
        slot = j % 2
        start, wait = _block_fetch(piece_refs, blocks, buf, sem)
        x, y, c = _place()

        def to_sibling(step, slot_):
            return pltpu.make_async_remote_copy(
                src_ref=out_buf.at[slot_], dst_ref=mirror.at[pl.ds(pl.multiple_of(step * tn, tn), tn)],
                send_sem=send_sems.at[slot_], recv_sem=recv_sem, device_id=(x, y, 1 - c), device_id_type=MESH_ID)

        @pl.when(j == 0)
        def _():
            start(j, slot)

        wait(slot)

        @pl.when(j + 1 < nblk)
        def _():
            start(j + 1, 1 - slot)

        acc = jnp.dot(h_ref[...], buf[slot], preferred_element_type=F32)
        block = acc.T.astype(BF16)
        o_ref[...] = block

        @pl.when(j >= 2)
        def _():
            to_sibling(j - 2, slot).wait_send()

        out_buf[slot] = block
        to_sibling(j, slot).start()

        @pl.when(j == nblk - 1)
        def _():
            to_sibling(j - 1, 1 - slot).wait_send()
            to_sibling(j, slot).wait_send()
            pltpu.make_async_remote_copy(src_ref=mirror, dst_ref=mirror, send_sem=send_sems.at[0], recv_sem=recv_sem,
                                         device_id=(x, y, 1 - c), device_id_type=MESH_ID).wait_recv()

    return pl.pallas_call(
        body, name="in_proj_dw", grid=(nblk,),
        in_specs=[pl.BlockSpec((None, D_MODEL, SEQ), lambda j: (_perm_of_block(j), 0, 0))] + [ANY] * npc + dep_specs,
        out_specs=[pl.BlockSpec((tn, D_MODEL), lambda j: (j, 0)), ANY],
        out_shape=[jax.ShapeDtypeStruct((N_IN, D_MODEL), BF16), jax.ShapeDtypeStruct((N_IN, D_MODEL), BF16)],
        scratch_shapes=[pltpu.VMEM((2, SEQ, tn), BF16), pltpu.SemaphoreType.DMA((2,)),
                        pltpu.VMEM((2, tn, D_MODEL), BF16), pltpu.SemaphoreType.DMA((2,)), pltpu.SemaphoreType.DMA],
        compiler_params=_params(("arbitrary",)))(hst, *pieces, *dep_args)


def _in_proj_dh(pieces, wt, dep=None):
    tk = 512
    blocks = _piece_blocks(pieces)
    nblk = len(blocks)
    npc = len(pieces)
    nchunk = D_MODEL // 128

    def col(s):
        return jnp.where(s < 3, s, jnp.where(s < 16, s + 6, s - 13))

    dep_specs, dep_args = _dep_operand(dep)

    def body(w_ref, *rest):
        piece_refs = rest[:npc]
        o_ref, acc_ref, buf, sem = rest[-4:]
        s = pl.program_id(0)
        slot = s % 2
        start, wait = _block_fetch(piece_refs, blocks, buf, sem)

        @pl.when(s == 0)
        def _():
            start(col(s), slot)

        wait(slot)

        @pl.when(s + 1 < nblk)
        def _():
            start(col(s + 1), 1 - slot)

        row_slices = [slice(r * 512, (r + 1) * 512) for r in range(SEQ // 512)]

        def product(rs):
            return jnp.dot(buf[slot, rs, :], w_ref[...], preferred_element_type=F32)

        def accumulate(cond, to_out, init):
            @pl.when(cond)
            def _():
                for rs in row_slices:
                    prod = product(rs)
                    if not to_out:
                        if init:
                            acc_ref[rs, :] = prod
                        else:
                            acc_ref[rs, :] += prod
                        continue
                    for c in range(nchunk):
                        if init:
                            o_ref[c, rs, :] = prod[:, c * 128:(c + 1) * 128]
                        else:
                            o_ref[c, rs, :] += prod[:, c * 128:(c + 1) * 128]

        accumulate(s == 0, True, True)
        accumulate(jnp.logical_and(s > 0, s < 16), True, False)
        accumulate(jnp.logical_or(s == 16, s == 19), False, True)
        accumulate(jnp.logical_and(s > 16, s != 19), False, False)
        for last, d in ((18, 4), (21, 16)):
            @pl.when(s == last)
            def _():
                mlen = SEQ // d
                for r in range(d):
                    for c in range(nchunk):
                        o_ref[c, pl.ds(r, mlen, stride=d), :] += acc_ref[r * mlen:(r + 1) * mlen,
                                                                         c * 128:(c + 1) * 128]

    return pl.pallas_call(
        body, name="in_proj_dh", grid=(nblk,),
        in_specs=[pl.BlockSpec((tk, D_MODEL), lambda s: (col(s), 0))] + [ANY] * npc + dep_specs,
        out_specs=pl.BlockSpec((nchunk, SEQ, 128), lambda s: (0, 0, 0)),
        out_shape=jax.ShapeDtypeStruct((nchunk, SEQ, 128), F32),
        scratch_shapes=[pltpu.VMEM((SEQ, D_MODEL), F32), pltpu.VMEM((2, SEQ, tk), BF16),
                        pltpu.SemaphoreType.DMA((2,))],
        compiler_params=_params(("arbitrary",)))(wt, *pieces, *dep_args)


def _head_lanes(lanes, hh):
    return lanes >= 64 if hh == 1 else lanes < 64


def _head_rows(x, lanes, hh, pair):
    if not pair:
        return jnp.max(x, axis=1, keepdims=True)
    return jnp.max(jnp.where(_head_lanes(lanes, hh), x, -jnp.inf), axis=1, keepdims=True)


def _mask_head(x, lanes, hh, pair, scale=1.0):
    if not pair:
        return x
    xf = x.astype(F32) if scale == 1.0 else x.astype(F32) * scale
    return jnp.where(_head_lanes(lanes, hh), xf, 0.0).astype(BF16)


def _window(mode, qi, tq, mlen, tk):
    if mode == "dil":
        q0 = qi * tq
        seg = (q0 // mlen) * mlen
        ks = jnp.clip(q0 - REACH, seg, seg + mlen - tk)
        return pl.multiple_of(ks, 64)
    if mode == "na":
        r_start = jnp.clip(qi - NA_ROWS // 2, 0, SEQ // GRID_W - NA_ROWS)
        return pl.multiple_of(r_start * GRID_W, 64)
    return 0


def _band_mask(qi, tq, tk, ks):
    qpos = qi * tq + _iota((tq, tk), 0)
    kpos = ks + _iota((tq, tk), 1)
    return jnp.where(jnp.abs(qpos - kpos) <= REACH, 0.0, NEG).astype(F32)


def _stack_heads(x, lanes, pair, scale=1.0):
    if not pair:
        return x
    return jnp.concatenate([_mask_head(x, lanes, hh, pair, scale) for hh in range(2)], axis=0)


def _stack_rows(x, lanes, pair):
    if not pair:
        return _head_rows(x, lanes, 0, pair)
    return jnp.concatenate([_head_rows(x, lanes, hh, pair) for hh in range(2)], axis=0)


def _unstack_heads(x, lanes, pair, tq):
    if not pair:
        return x
    return jnp.where(lanes < 64, x[:tq], x[tq:])


def _scores(mode, qst, k, sscale, band, qi, bias_ref, pair):
    s = lax.dot_general(qst, k, NT, preferred_element_type=F32)
    if sscale != 1.0:
        s = s * sscale
    if mode == "dil":
        s = s + jnp.concatenate([band, band], axis=0)
    elif mode == "na":
        off = qi - jnp.clip(qi - NA_ROWS // 2, 0, SEQ // GRID_W - NA_ROWS)
        s = s + jnp.concatenate([bias_ref[0, off], bias_ref[1, off]], axis=0)
    return s


def _attn_cfg(mode, d):
    if mode == "dil":
        mlen = SEQ // d
        return dict(pair=True, tq=128, tk=min(256, mlen), mlen=mlen, lk=SEQ, scale=HEAD_DIM ** -0.5, units=4,
                    nsub=ATTN_SUBTILES)
    if mode == "na":
        return dict(pair=True, tq=GRID_W, tk=NA_ROWS * GRID_W, mlen=SEQ, lk=SEQ, scale=HEAD_DIM ** -0.5, units=4,
                    nsub=ATTN_SUBTILES)
    return dict(pair=False, tq=128, tk=MEM_LEN, mlen=SEQ, lk=MEM_LEN, scale=128 ** -0.5, units=4,
                nsub=ATTN_SUBTILES)


ATTN_SUBTILES = 16


def _attn_fwd(name, mode, q_arr, k_arr, v_arr, qcol, kcol, vcol, d=1, bias=None):
    cfg = _attn_cfg(mode, d)
    pair, tq, tk, mlen, lk, scale = cfg["pair"], cfg["tq"], cfg["tk"], cfg["mlen"], cfg["lk"], cfg["scale"]
    qscale, sscale = (scale, 1.0) if pair else (1.0, scale)
    nsub = cfg["nsub"]
    rows = nsub * tq

    def body(*refs):
        if mode == "na":
            q_ref, k_ref, v_ref, bias_ref, o_ref, l_ref = refs
        else:
            q_ref, k_ref, v_ref, o_ref, l_ref = refs
            bias_ref = None
        lanes = _iota((tq, 128), 1)
        qis = [pl.program_id(1) * nsub + sub for sub in range(nsub)]
        kss = [_window(mode, qi, tq, mlen, tk) for qi in qis]
        vs = [v_ref[pl.ds(ks, tk), :] for ks in kss]
        bands = [_band_mask(qi, tq, tk, ks) if mode == "dil" else None for qi, ks in zip(qis, kss)]
        ss = []
        for sub in range(nsub):
            qst = _stack_heads(q_ref[sub * tq:(sub + 1) * tq, :], lanes, pair, qscale)
            k = k_ref[pl.ds(kss[sub], tk), :]
            ss.append(_scores(mode, qst, k, sscale, bands[sub], qis[sub], bias_ref, pair))
        ms = [jnp.max(s_, axis=1, keepdims=True) for s_ in ss]
        ps = [jnp.exp(s_ - m) for s_, m in zip(ss, ms)]
        ls = [jnp.sum(p, axis=1, keepdims=True) for p in ps]
        os_ = [jnp.dot(p.astype(BF16), v, preferred_element_type=F32) for p, v in zip(ps, vs)]
        for sub in range(nsub):
            out = _unstack_heads(os_[sub] / ls[sub], lanes, pair, tq)
            lse = ms[sub] + jnp.log(ls[sub])
            lse = _unstack_heads(jnp.broadcast_to(lse, (lse.shape[0], 128)), lanes, pair, tq)
            dst = _folded_rows(qis[sub] * tq, tq, d) if mode == "dil" else slice(sub * tq, (sub + 1) * tq)
            o_ref[dst, :] = out
            l_ref[dst, :] = lse

    in_specs = [pl.BlockSpec((rows, 128), lambda u, i: (i, qcol + u)),
                pl.BlockSpec((lk, 128), lambda u, i: (0, kcol + u)),
                pl.BlockSpec((lk, 128), lambda u, i: (0, vcol + u))]
    args = [q_arr, k_arr, v_arr]
    if mode == "na":
        in_specs.append(pl.BlockSpec((2, NA_ROWS, GRID_W, NA_ROWS * GRID_W), lambda u, i: (u, 0, 0, 0)))
        args.append(bias)
    if mode == "dil":
        out_spec = pl.BlockSpec((SEQ, 128), lambda u, i: (0, u))
    else:
        out_spec = pl.BlockSpec((rows, 128), lambda u, i: (i, u))
    return pl.pallas_call(
        body, name=name, grid=(cfg["units"], SEQ // rows), in_specs=in_specs, out_specs=[out_spec, out_spec],
        out_shape=[jax.ShapeDtypeStruct((SEQ, 512), F32), jax.ShapeDtypeStruct((SEQ, 512), F32)],
        compiler_params=_params(("parallel", "arbitrary")))(*args)


def _attn_bwd(name, mode, q_arr, k_arr, v_arr, qcol, kcol, vcol, do, lse, dp=None, o=None, d=1, bias=None,
              tabs=None):
    cfg = _attn_cfg(mode, d)
    pair, tq, tk, mlen, lk, scale = cfg["pair"], cfg["tq"], cfg["tk"], cfg["mlen"], cfg["lk"], cfg["scale"]
    qscale, sscale = (scale, 1.0) if pair else (1.0, scale)
    nsub = cfg["nsub"]
    rows = nsub * tq
    nq = SEQ // rows
    kv_dtype = F32 if mode == "mem" else BF16

    def body(*refs):
        refs = list(refs)
        q_ref, k_ref, v_ref, do_ref, l_ref = refs[:5]
        rest = refs[5:]
        bias_ref = tq_ref = tk_ref = db_ref = None
        if mode == "dil":
            dp_ref, tq_ref, tk_ref, dq_ref, dk_ref, dv_ref, dk_acc, dv_acc = rest
        elif mode == "na":
            o_ref, bias_ref, dq_ref, dk_ref, dv_ref, db_ref, dk_acc, dv_acc = rest
        else:
            o_ref, dq_ref, dk_ref, dv_ref, dk_acc, dv_acc = rest
        step = pl.program_id(1)

        @pl.when(step == 0)
        def _():
            dk_acc[...] = jnp.zeros((lk, 128), F32)
            dv_acc[...] = jnp.zeros((lk, 128), F32)
            if mode == "na":
                db_ref[...] = jnp.zeros(db_ref.shape, F32)

        lanes = _iota((tq, 128), 1)
        qis = [step * nsub + sub for sub in range(nsub)]
        sls = [slice(sub * tq, (sub + 1) * tq) for sub in range(nsub)]
        kss = [_window(mode, qi, tq, mlen, tk) for qi in qis]
        ks_ = [k_ref[pl.ds(ks, tk), :] for ks in kss]
        vs = [v_ref[pl.ds(ks, tk), :] for ks in kss]
        qsts, dosts, lses, dphs = [], [], [], []
        for sub in range(nsub):
            if mode == "dil":
                src = _folded_rows(qis[sub] * tq, tq, d)
                dov = do_ref[src, :].astype(BF16)
                lsev = l_ref[src, :]
                dphs.append(_stack_rows(dp_ref[src, :], lanes, pair))
            else:
                dov = do_ref[sls[sub], :]
                lsev = l_ref[sls[sub], :]
                dpv = dov.astype(F32) * o_ref[sls[sub], :]
                if pair:
                    dphs.append(jnp.concatenate(
                        [jnp.sum(jnp.where(_head_lanes(lanes, hh), dpv, 0.0), axis=1, keepdims=True)
                         for hh in range(2)], axis=0))
                else:
                    dphs.append(jnp.sum(dpv, axis=1, keepdims=True))
            qsts.append(_stack_heads(q_ref[sls[sub], :], lanes, pair, qscale))
            dosts.append(_stack_heads(dov, lanes, pair))
            lses.append(_stack_rows(lsev, lanes, pair))
        bands = [_band_mask(qi, tq, tk, ks) if mode == "dil" else None for qi, ks in zip(qis, kss)]
        ss = [_scores(mode, qsts[sub], ks_[sub], sscale, bands[sub], qis[sub], bias_ref, pair) for sub in range(nsub)]
        dpms = [lax.dot_general(dosts[sub], vs[sub], NT, preferred_element_type=F32) for sub in range(nsub)]
        ps = [jnp.exp(s_ - lse) for s_, lse in zip(ss, lses)]
        dss = [p * (dpm - dph) for p, dpm, dph in zip(ps, dpms, dphs)]
        if mode == "na":
            for sub, ds in enumerate(dss):
                off = qis[sub] - jnp.clip(qis[sub] - NA_ROWS // 2, 0, SEQ // GRID_W - NA_ROWS)
                db_ref[0, off] += ds[:tq]
                db_ref[1, off] += ds[tq:]
        dsbs = [ds.astype(BF16) for ds in dss]
        dvs = [lax.dot_general(p.astype(BF16), dosts[sub], TN, preferred_element_type=F32)
               for sub, p in enumerate(ps)]
        dqs = [jnp.dot(dsb, ks_[sub], preferred_element_type=F32) * scale for sub, dsb in enumerate(dsbs)]
        dks = [lax.dot_general(dsb, qsts[sub], TN, preferred_element_type=F32) for sub, dsb in enumerate(dsbs)]
        for sub in range(nsub):
            sl = sls[sub]
            dq = _unstack_heads(dqs[sub], lanes, pair, tq)
            if mode == "dil":
                dq = _rope_t(dq, tq_ref[0, sl, :], tq_ref[1, sl, :], tq_ref[2, sl, :])
            dq_ref[sl, :] = dq.astype(BF16)
            dk_acc[pl.ds(kss[sub], tk), :] += dks[sub] if pair else dks[sub] * scale
            dv_acc[pl.ds(kss[sub], tk), :] += dvs[sub]

        @pl.when(step == nq - 1)
        def _():
            dkv = dk_acc[...]
            if mode == "dil":
                dkv = _rope_t(dkv, tk_ref[0], tk_ref[1], tk_ref[2])
            dk_ref[...] = dkv.astype(kv_dtype)
            dv_ref[...] = dv_acc[...].astype(kv_dtype)

    q_spec = pl.BlockSpec((rows, 128), lambda u, i: (i, qcol + u))
    row_spec = pl.BlockSpec((rows, 128), lambda u, i: (i, u))
    kv_out = pl.BlockSpec((lk, 128), lambda u, i: (0, u))
    whole = pl.BlockSpec((SEQ, 128), lambda u, i: (0, u))
    nat_spec = whole if mode == "dil" else row_spec
    in_specs = [q_spec,
                pl.BlockSpec((lk, 128), lambda u, i: (0, kcol + u)),
                pl.BlockSpec((lk, 128), lambda u, i: (0, vcol + u)),
                nat_spec, nat_spec]
    args = [q_arr, k_arr, v_arr, do, lse]
    out_specs = [row_spec, kv_out, kv_out]
    out_shape = [jax.ShapeDtypeStruct((SEQ, 512), BF16), jax.ShapeDtypeStruct((lk, 512), kv_dtype),
                 jax.ShapeDtypeStruct((lk, 512), kv_dtype)]
    if mode == "dil":
        in_specs += [whole, pl.BlockSpec((3, rows, 128), lambda u, i: (0, i, 0)),
                     pl.BlockSpec((3, SEQ, 128), lambda u, i: (0, 0, 0))]
        args += [dp, tabs, tabs]
    elif mode == "na":
        b_spec = pl.BlockSpec((2, NA_ROWS, GRID_W, NA_ROWS * GRID_W), lambda u, i: (u, 0, 0, 0))
        in_specs += [row_spec, b_spec]
        args += [o, bias]
        out_specs.append(b_spec)
        out_shape.append(jax.ShapeDtypeStruct((8, NA_ROWS, GRID_W, NA_ROWS * GRID_W), F32))
    else:
        in_specs.append(row_spec)
        args.append(o)
    return pl.pallas_call(
        body, name=name, grid=(cfg["units"], nq), in_specs=in_specs, out_specs=out_specs, out_shape=out_shape,
        scratch_shapes=[pltpu.VMEM((lk, 128), F32), pltpu.VMEM((lk, 128), F32)],
        compiler_params=_params(("parallel", "arbitrary")))(*args)


def _na_geometry():
    qc = _iota((GRID_W, 128), 0)
    lane = _iota((GRID_W, 128), 1)
    kc = lane & 63
    c_start = jnp.clip(qc - 8, 0, GRID_W - 16)
    valid = jnp.logical_and(kc >= c_start, kc < c_start + 16)
    return lane, valid


def _na_bias(rpb_rows):
    def body(r_ref, o_ref, t_ref):
        lane, valid = _na_geometry()
        for dd in range(14):
            row_a = jnp.broadcast_to(r_ref[dd:dd + 1, :], (GRID_W, 128))
            row_b = jnp.broadcast_to(r_ref[dd + 1:dd + 2, :], (GRID_W, 128))
            both = jnp.where(lane < 64, row_a, pltpu.roll(row_b, 64, 1))
            t = pltpu.roll(both, 128 - 15, 1, stride=1, stride_axis=0)
            t_ref[dd] = jnp.where(valid, t, NEG)
        for off in range(NA_ROWS):
            for p in range(4):
                o_ref[off, :, p * 128:(p + 1) * 128] = t_ref[2 * p - off + 7]

    return pl.pallas_call(
        body, name="na_bias", grid=(8,),
        in_specs=[pl.BlockSpec((None, 16, 128), lambda h: (h, 0, 0))],
        out_specs=pl.BlockSpec((None, NA_ROWS, GRID_W, NA_ROWS * GRID_W), lambda h: (h, 0, 0, 0)),
        out_shape=jax.ShapeDtypeStruct((8, NA_ROWS, GRID_W, NA_ROWS * GRID_W), F32),
        scratch_shapes=[pltpu.VMEM((14, GRID_W, 128), F32)],
        compiler_params=_params(("parallel",)))(rpb_rows)


def _na_bias_bwd(dbias, dep=None):
    dep_specs, dep_args = _dep_operand(dep)

    def body(d_ref, *rest):
        o_ref = rest[-1]
        lane, valid = _na_geometry()
        reverse = (_iota((GRID_W, GRID_W), 0) + _iota((GRID_W, GRID_W), 1) == GRID_W - 1).astype(F32)
        o_ref[...] = jnp.zeros((16, 128), F32)
        for dd in range(14):
            t = jnp.zeros((GRID_W, 128), F32)
            for off in range(NA_ROWS):
                for p in range(4):
                    if 2 * p - off + 7 == dd:
                        t = t + d_ref[off, :, p * 128:(p + 1) * 128]
            t = jnp.dot(reverse, jnp.where(valid, t, 0.0), precision=lax.Precision.HIGHEST,
                        preferred_element_type=F32)
            t = pltpu.roll(t, 128 - (GRID_W - 16), 1, stride=1, stride_axis=0)
            o_ref[dd:dd + 1, :] = jnp.sum(t, axis=0, keepdims=True)

    return pl.pallas_call(
        body, name="na_bias_bwd", grid=(8,),
        in_specs=[pl.BlockSpec((None, NA_ROWS, GRID_W, NA_ROWS * GRID_W), lambda h: (h, 0, 0, 0))] + dep_specs,
        out_specs=pl.BlockSpec((None, 16, 128), lambda h: (h, 0, 0)),
        out_shape=jax.ShapeDtypeStruct((8, 16, 128), F32),
        compiler_params=_params(("parallel",)))(dbias, *dep_args)


GATE_ROWS = 128


def _group_weights(l0, l1, l2):
    m = jnp.maximum(jnp.maximum(l0, l1), l2)
    e0, e1, e2 = jnp.exp(l0 - m), jnp.exp(l1 - m), jnp.exp(l2 - m)
    inv = 1.0 / (e0 + e1 + e2)
    return e0 * inv, e1 * inv, e2 * inv


def _gate_specs():
    r512 = pl.BlockSpec((GATE_ROWS, 512), lambda i: (i, 0))
    r1024 = pl.BlockSpec((GATE_ROWS, D_MODEL), lambda i: (i, 0))
    silu_cols = [pl.BlockSpec((GATE_ROWS, 512), functools.partial(lambda b, i: (i, b), 13 + b)) for b in range(3)]
    logit_cols = [pl.BlockSpec((GATE_ROWS, D_MODEL), functools.partial(lambda b, i: (i, b), 8 + b)) for b in range(3)]
    return r512, r1024, silu_cols, logit_cols


def _gate_fwd(o_grp, l_grp, out_b, out_c, parts, merge_bias, wts):
    r512, r1024, silu_cols, logit_cols = _gate_specs()

    def body(o0, o1, o2, l0, l1, l2, ob, oc, ga, gb, gc, la, lb, lc, mb, wa, wb, wc,
             oa_ref, ua, ub, uc, za, zb, zc, y_ref):
        w0, w1, w2 = _group_weights(l0[...], l1[...], l2[...])
        out_a = w0 * o0[...] + w1 * o1[...] + w2 * o2[...]
        oa_ref[...] = out_a
        y = jnp.zeros((GATE_ROWS, D_MODEL), F32)
        for b, (ov, g_ref, l_ref, w_ref, u_ref, z_ref) in enumerate(
                ((out_a, ga, la, wa, ua, za), (ob[...], gb, lb, wb, ub, zb), (oc[...], gc, lc, wc, uc, zc))):
            g = g_ref[...].astype(F32)
            u = (ov * (g * _sigmoid(g))).astype(BF16)
            u_ref[...] = u
            z = lax.dot_general(u, w_ref[...], NT, preferred_element_type=F32)
            z_ref[...] = z.astype(BF16)
            gate = _sigmoid(l_ref[...].astype(F32) + mb[b:b + 1, :])
            y = y + gate * z
        y_ref[...] = y.astype(BF16)

    full = lambda shape: pl.BlockSpec(shape, lambda i: (0,) * len(shape))
    in_specs = ([r512] * 8 + silu_cols + logit_cols
                + [full((3, D_MODEL))] + [full((D_MODEL, 512))] * 3)
    out_specs = [r512] * 4 + [r1024] * 4
    out_shape = ([jax.ShapeDtypeStruct((SEQ, 512), F32)] + [jax.ShapeDtypeStruct((SEQ, 512), BF16)] * 3
                 + [jax.ShapeDtypeStruct((SEQ, D_MODEL), BF16)] * 4)
    res = pl.pallas_call(
        body, name="gate_fwd", grid=(SEQ // GATE_ROWS,), in_specs=in_specs, out_specs=out_specs,
        out_shape=out_shape, compiler_params=_params(("parallel",)))(
            *o_grp, *l_grp, out_b, out_c, parts, parts, parts, parts, parts, parts, merge_bias, *wts)
    return res[0], res[1:4], res[4:7], res[7]


def _gate_bwd(dy, z, parts, merge_bias, outs, o_grp, l_grp, wts, head_sum):
    r512, r1024, silu_cols, logit_cols = _gate_specs()

    def body(dy_ref, za, zb, zc, la, lb, lc, mb, oa, ob, oc, ga, gb, gc, o0, o1, o2, l0, l1, l2, wa, wb, wc, hs_ref,
             dla, dlb, dlc, gmb, dza, dzb, dzc, dga, dgb, dgc, do0, do1, do2, dp0, dp1, dp2, dob, doc):
        dyv = dy_ref[...].astype(F32)
        rows = []
        dos = []
        for b, (z_ref, l_ref, ov_ref, g_ref, w_ref, dl_ref, dz_ref, dg_ref) in enumerate(
                ((za, la, oa, ga, wa, dla, dza, dga), (zb, lb, ob, gb, wb, dlb, dzb, dgb),
                 (zc, lc, oc, gc, wc, dlc, dzc, dgc))):
            gate = _sigmoid(l_ref[...].astype(F32) + mb[b:b + 1, :])
            dl = dyv * z_ref[...].astype(F32) * gate * (1.0 - gate)
            dl_ref[...] = dl.astype(BF16)
            rows.append(jnp.sum(dl, axis=0, keepdims=True))
            dz = (dyv * gate).astype(BF16)
            dz_ref[...] = dz
            du = jnp.dot(dz, w_ref[...], preferred_element_type=F32)
            g = g_ref[...].astype(F32)
            sg = _sigmoid(g)
            dos.append(du * (g * sg))
            dg_ref[...] = (du * ov_ref[...] * (sg * (1.0 + g * (1.0 - sg)))).astype(BF16)

        @pl.when(pl.program_id(0) == 0)
        def _():
            gmb[...] = jnp.zeros((3, D_MODEL), F32)

        for b in range(3):
            gmb[b:b + 1, :] += rows[b]
        dob[...] = dos[1].astype(BF16)
        doc[...] = dos[2].astype(BF16)
        doa = dos[0]
        row_term = jnp.dot(doa * oa[...], hs_ref[...], precision=lax.Precision.HIGHEST, preferred_element_type=F32)
        ws = _group_weights(l0[...], l1[...], l2[...])
        for wg, do_ref, dp_ref in zip(ws, (do0, do1, do2), (dp0, dp1, dp2)):
            do_ref[...] = wg * doa
            dp_ref[...] = wg * row_term

    full = lambda shape: pl.BlockSpec(shape, lambda i: (0,) * len(shape))
    acc = pl.BlockSpec((3, D_MODEL), lambda i: (0, 0))
    in_specs = ([r1024] * 4 + logit_cols + [full((3, D_MODEL))] + [r512] * 3 + silu_cols + [r512] * 6
                + [full((D_MODEL, 512))] * 3 + [full((512, 512))])
    out_specs = [r1024] * 3 + [acc] + [r1024] * 3 + [r512] * 11
    out_shape = ([jax.ShapeDtypeStruct((SEQ, D_MODEL), BF16)] * 3 + [jax.ShapeDtypeStruct((3, D_MODEL), F32)]
                 + [jax.ShapeDtypeStruct((SEQ, D_MODEL), BF16)] * 3 + [jax.ShapeDtypeStruct((SEQ, 512), BF16)] * 3
                 + [jax.ShapeDtypeStruct((SEQ, 512), F32)] * 6 + [jax.ShapeDtypeStruct((SEQ, 512), BF16)] * 2)
    res = pl.pallas_call(
        body, name="gate_bwd", grid=(SEQ // GATE_ROWS,), in_specs=in_specs, out_specs=out_specs,
        out_shape=out_shape, compiler_params=_params(("arbitrary",)))(
            dy, *z, parts, parts, parts, merge_bias, *outs, parts, parts, parts, *o_grp, *l_grp, *wts, head_sum)
    return res[0:3], res[3], res[4:7], res[7:10], res[10:13], res[13:16], res[16], res[17]


def _post(y2, x, target, gain):
    rows = 256

    def body(y_ref, x_ref, t_ref, g_ref, do_ref, dy_ref, l_ref, gg_ref):
        yv = y_ref[...]
        rstd = lax.rsqrt(jnp.mean(yv * yv, axis=1, keepdims=True) + EPS)
        yn = yv * rstd
        gv = g_ref[...]
        err = x_ref[...] + yn * gv - t_ref[...]
        dout = err * (1.0 / D_MODEL)
        do_ref[...] = dout
        dn = dout * gv
        dy_ref[...] = (rstd * (dn - yn * jnp.mean(dn * yn, axis=1, keepdims=True))).astype(BF16)

        @pl.when(pl.program_id(0) == 0)
        def _():
            l_ref[...] = jnp.zeros((1, D_MODEL), F32)
            gg_ref[...] = jnp.zeros((1, D_MODEL), F32)

        l_ref[...] += jnp.sum(err * err, axis=0, keepdims=True)
        gg_ref[...] += jnp.sum(dout * yn, axis=0, keepdims=True)

    row = pl.BlockSpec((rows, D_MODEL), lambda i: (i, 0))
    vec = pl.BlockSpec((1, D_MODEL), lambda i: (0, 0))
    return pl.pallas_call(
        body, name="post", grid=(SEQ // rows,), in_specs=[row, row, row, vec], out_specs=[row, row, vec, vec],
        out_shape=[jax.ShapeDtypeStruct((SEQ, D_MODEL), F32), jax.ShapeDtypeStruct((SEQ, D_MODEL), BF16),
                   jax.ShapeDtypeStruct((1, D_MODEL), F32), jax.ShapeDtypeStruct((1, D_MODEL), F32)],
        compiler_params=_params(("arbitrary",)))(y2, x, target, gain)


def _local_step(x, mem, target, pre_norm, mem_norm, post_norm, na_rpb, wt_in, late_weights, dep_in=None,
                reduce_start=None):
    tabs = _rope_tables()
    hs, hst = _prenorm_fold(x, pre_norm)
    parts = _in_proj(hs, wt_in, tabs, dep_in)

    o_grp, l_grp = [], []
    for g, d in enumerate(DILATIONS):
        o, l = _attn_fwd("dil_fwd_%d" % g, "dil", parts, parts, parts, 12 * g, 12 * g + 4, 12 * g + 8, d=d)
        o_grp.append(o)
        l_grp.append(l)
    bias = _na_bias(jnp.pad(na_rpb, ((0, 0), (0, 1), (0, 128 - 31))))
    out_b, lse_b = _attn_fwd("na_fwd", "na", parts, parts, parts, 36, 40, 44, bias=bias)
    merge_bias, w_kv, wt_a, wt_b, wt_c, w_out = late_weights(out_b)
    memn = _rmsnorm_fwd("memnorm", mem, mem_norm, MEM_LEN)
    kv_m = _mm_simple("mem_kv", memn, w_kv, NN, BF16, MEM_LEN, 512, D_MODEL)
    out_c, lse_c = _attn_fwd("mem_fwd", "mem", parts, kv_m, kv_m, 48, 0, 4)

    wts = (wt_a, wt_b, wt_c)
    out_a, u, z, y = _gate_fwd(o_grp, l_grp, out_b, out_c, parts, merge_bias, wts)
    y2 = _mm_simple("out_proj", y, w_out, NN, F32, 512, D_MODEL, D_MODEL)
    dout, dy2, err_sq, g_post = _post(y2, x, target, post_norm)
    loss = 0.5 * jnp.sum(err_sq) / D_MODEL

    dy = _mm_simple("out_proj_dx", dy2, w_out, NT, BF16, 512, D_MODEL, D_MODEL)
    g_w_out = _mm_simple("out_proj_dw", y, dy2, TN, BF16, D_MODEL, 512, 512)

    rr = _iota((512, 512), 0) // HEAD_DIM
    cc = _iota((512, 512), 1) // HEAD_DIM
    head_sum = (rr == cc).astype(F32)
    dlog, g_mb, dz, dg, do_grp, dp_grp, do_b, do_c = _gate_bwd(
        dy, z, parts, merge_bias, (out_a, out_b, out_c), o_grp, l_grp, wts, head_sum)
    g_wt = [_mm_simple("branch_dw_%d" % b, dz[b], u[b], TN, BF16, D_MODEL, 512, 512) for b in range(3)]

    dqkv = []
    for g, d in enumerate(DILATIONS):
        dq, dk, dv = _attn_bwd("dil_bwd_%d" % g, "dil", parts, parts, parts, 12 * g, 12 * g + 4, 12 * g + 8,
                               do_grp[g], l_grp[g], dp=dp_grp[g], d=d, tabs=tabs[g])
        dqkv += [dq, dk, dv]
    dq_b, dk_b, dv_b, dbias = _attn_bwd("na_bwd", "na", parts, parts, parts, 36, 40, 44, do_b, lse_b, o=out_b,
                                        bias=bias)
    dq_c, dk_m, dv_m = _attn_bwd("mem_bwd", "mem", parts, kv_m, kv_m, 48, 0, 4, do_c, lse_c, o=out_c)

    dkv = jnp.concatenate([dk_m, dv_m], axis=1).astype(BF16)
    g_w_kv = _mm_simple("mem_kv_dw", memn, dkv, TN, BF16, D_MODEL, 512, MEM_LEN)
    dmemn = _mm_simple("mem_kv_dx", dkv, w_kv, NT, F32, MEM_LEN, 512, D_MODEL)

    grads = dict(w_kv=g_w_kv, wt_a=g_wt[0], wt_b=g_wt[1], wt_c=g_wt[2], w_out=g_w_out, merge_bias=g_mb,
                 post_norm=g_post)
    dep = reduce_start(grads) if reduce_start is not None else None
    dparts = dqkv + [dq_b, dk_b, dv_b, dq_c] + list(dg) + list(dlog)
    grads["wt_in"] = _in_proj_dw(dparts, hst, dep)
    dep = reduce_start(grads) if reduce_start is not None else None
    dh = _in_proj_dh(dparts, wt_in, dep)
    grad_x, grads["pre_norm"] = _prenorm_bwd(x, pre_norm, dh, dout)
    g_rpb_t = _na_bias_bwd(dbias, dep)
    grads["na_rpb"] = g_rpb_t[:, :15, :31] + jnp.pad(g_rpb_t[:, :14, 64:95], ((0, 0), (1, 0), (0, 0)))
    grads["mem_norm"] = _memnorm_bwd(mem, dmemn, dep)
    return loss, grad_x, grads


ANY = pl.BlockSpec(memory_space=pl.ANY)


def _place():
    return lax.axis_index("x"), lax.axis_index("y"), lax.axis_index("c")


def _all_gather(shard):
    r = shard.shape[0]
    half = r // 2

    def body(src, out, send_sems, recv_sems, local_sem):
        x, y, c = _place()
        me, sib = (x, y, c), (x, y, 1 - c)
        xn, yn, dg = (1 - x, y, c), (x, 1 - y, c), (1 - x, 1 - y, c)

        def rows(dev, part=None):
            blk = out.at[4 * dev[0] + 2 * dev[1] + dev[2]]
            return blk if part is None else blk.at[pl.ds(part * half, half)]

        def copy(k, dev, part, to, own=False):
            return pltpu.make_async_remote_copy(
                src_ref=src if own else rows(dev, part), dst_ref=rows(dev, part),
                send_sem=send_sems.at[k], recv_sem=recv_sems.at[k], device_id=to, device_id_type=MESH_ID)

        def other(dev):
            return (dev[0], dev[1], 1 - dev[2])

        mine = pltpu.make_async_copy(src, rows(me), local_sem)
        mine.start()
        sent = [copy(0, me, None, sib, own=True), copy(1, me, None, xn, own=True), copy(2, me, None, yn, own=True)]
        for cp in sent:
            cp.start()
        copy(1, xn, None, me).wait_recv()
        sent += [copy(3, xn, 0, yn), copy(5, xn, None, sib)]
        sent[-2].start()
        sent[-1].start()
        copy(2, yn, None, me).wait_recv()
        sent += [copy(4, yn, 1, xn), copy(6, yn, None, sib)]
        sent[-2].start()
        sent[-1].start()
        copy(3, dg, 0, me).wait_recv()
        sent.append(copy(7, dg, 0, sib))
        sent[-1].start()
        copy(4, dg, 1, me).wait_recv()
        sent.append(copy(8, dg, 1, sib))
        sent[-1].start()
        copy(0, sib, None, me).wait_recv()
        copy(5, other(xn), None, me).wait_recv()
        copy(6, other(yn), None, me).wait_recv()
        copy(7, other(dg), 0, me).wait_recv()
        copy(8, other(dg), 1, me).wait_recv()
        for cp in sent:
            cp.wait_send()
        mine.wait()

    return pl.pallas_call(
        body, name="all_gather", in_specs=[ANY], out_specs=ANY,
        out_shape=jax.ShapeDtypeStruct((N_DEV,) + shard.shape, shard.dtype),
        scratch_shapes=[pltpu.SemaphoreType.DMA((9,)), pltpu.SemaphoreType.DMA((9,)), pltpu.SemaphoreType.DMA])(shard)


def _exchange_sibling(name, terms):
    nt = len(terms)

    def body(*refs):
        srcs, outs = refs[:nt], refs[nt:2 * nt]
        send_sems, recv_sems = refs[2 * nt:]
        x, y, c = _place()
        copies = []
        for q in range(4):
            for t in range(nt):
                copies.append(pltpu.make_async_remote_copy(
                    src_ref=srcs[t].at[2 * q + 1 - c], dst_ref=outs[t].at[q],
                    send_sem=send_sems.at[q * nt + t], recv_sem=recv_sems.at[q * nt + t],
                    device_id=(x, y, 1 - c), device_id_type=MESH_ID))
        for cp in copies:
            cp.start()
        for cp in copies:
            cp.wait()

    return pl.pallas_call(
        body, name=name, in_specs=[ANY] * nt, out_specs=[ANY] * nt,
        out_shape=[jax.ShapeDtypeStruct((4,) + s.shape[1:], s.dtype) for s in terms],
        scratch_shapes=[pltpu.SemaphoreType.DMA((4 * nt,)), pltpu.SemaphoreType.DMA((4 * nt,))])(*terms)


HBM = pl.BlockSpec(memory_space=pltpu.HBM)
SEM = pl.BlockSpec(memory_space=pltpu.SEMAPHORE)
DATAFLOW = pltpu.SideEffectType.DATAFLOW_SIDE_EFFECTING


def _split_copies(kind, srcs, lands, send_sems, recv_sems):
    nt = len(srcs)
    x, y, c = _place()
    copies = []
    if kind == "gather":
        me = 4 * x + 2 * y + c
        for mask in range(1, 8):
            fx, fy, fc = (mask >> 2) & 1, (mask >> 1) & 1, mask & 1
            to = (1 - x if fx else x, 1 - y if fy else y, 1 - c if fc else c)
            for t in range(nt):
                k = (mask - 1) * nt + t
                copies.append(pltpu.make_async_remote_copy(
                    src_ref=srcs[t], dst_ref=lands[t].at[me], send_sem=send_sems.at[k], recv_sem=recv_sems.at[k],
                    device_id=to, device_id_type=MESH_ID))
    else:
        for s, (tx, ty) in enumerate([(1 - x, y), (x, 1 - y), (1 - x, 1 - y)]):
            for t in range(nt):
                k = s * nt + t
                copies.append(pltpu.make_async_remote_copy(
                    src_ref=srcs[t].at[2 * tx + ty], dst_ref=lands[t].at[s], send_sem=send_sems.at[k],
                    recv_sem=recv_sems.at[k], device_id=(tx, ty, c), device_id_type=MESH_ID))
    return copies


def _split_count(kind, nt):
    return (7 if kind == "gather" else 3) * nt


def _exchange_start(name, kind, srcs, land_shapes, after=None):
    nt = len(srcs)
    n = _split_count(kind, nt)
    dep_specs, dep_args = _dep_operand(after)
    nd = len(dep_args)

    def body(*refs):
        src_refs, land_refs = refs[:nt], refs[nt:2 * nt]
        send_sems, recv_sems = refs[2 * nt + nd], refs[2 * nt + nd + 1]
        token = refs[-1]
        for cp in _split_copies(kind, src_refs, land_refs, send_sems, recv_sems):
            cp.start()
        token[...] = jnp.zeros_like(token)

    lands = [pltpu.with_memory_space_constraint(lax.empty(s.shape, s.dtype), pltpu.HBM) for s in land_shapes]
    res = pl.pallas_call(
        body, name=name,
        out_shape=(pltpu.SemaphoreType.DMA((n,)), pltpu.SemaphoreType.DMA((n,)),
                   *[pltpu.HBM(s.shape, s.dtype) for s in srcs], *[pltpu.HBM(s.shape, s.dtype) for s in land_shapes],
                   jax.ShapeDtypeStruct((8, 128), F32)),
        in_specs=[HBM] * (2 * nt) + dep_specs,
        out_specs=(SEM, SEM, *([HBM] * (2 * nt)), pl.BlockSpec(memory_space=pltpu.VMEM)),
        input_output_aliases={i: 2 + i for i in range(2 * nt)},
        compiler_params=pltpu.CompilerParams(has_side_effects=DATAFLOW))(
            *[pltpu.with_memory_space_constraint(s, pltpu.HBM) for s in srcs], *lands, *dep_args)
    return res[0], res[1], list(res[2:2 + nt]), list(res[2 + nt:2 + 2 * nt]), res[-1]


def _exchange_wait(name, kind, send_sems, recv_sems, srcs, lands, after):
    nt = len(srcs)

    def body(*refs):
        src_refs, land_refs = refs[:nt], refs[nt:2 * nt]
        s_sems, r_sems = refs[2 * nt], refs[2 * nt + 1]
        for cp in _split_copies(kind, src_refs, land_refs, s_sems, r_sems):
            cp.wait_send()
            cp.wait_recv()

    res = pl.pallas_call(
        body, name=name,
        out_shape=tuple(pltpu.HBM(s.shape, s.dtype) for s in list(srcs) + list(lands)),
        in_specs=[HBM] * (2 * nt) + [SEM, SEM, pl.BlockSpec(memory_space=pl.ANY)],
        out_specs=tuple([HBM] * (2 * nt)),
        input_output_aliases={i: i for i in range(2 * nt)},
        compiler_params=pltpu.CompilerParams(has_side_effects=DATAFLOW))(
            *srcs, *lands, send_sems, recv_sems, after)
    return list(res[:nt]), list(res[nt:])


def _add_sibling(name, term, recv, rows):
    _, r, w = term.shape
    cidx = lax.axis_index("c").astype(jnp.int32).reshape(1)
    like_term = recv.shape[0] == N_DEV

    def body(c_ref, a_ref, b_ref, o_ref):
        o_ref[...] = (a_ref[...].astype(F32) + b_ref[...].astype(F32)).astype(o_ref.dtype)

    grid_spec = pltpu.PrefetchScalarGridSpec(
        num_scalar_prefetch=1, grid=(4, r // rows),
        in_specs=[pl.BlockSpec((None, rows, w), lambda q, i, c_ref: (2 * q + c_ref[0], i, 0)),
                  pl.BlockSpec((None, rows, w), lambda q, i, c_ref: (2 * q + c_ref[0] if like_term else q, i, 0))],
        out_specs=pl.BlockSpec((None, rows, w), lambda q, i, c_ref: (q, i, 0)))
    return pl.pallas_call(
        body, name=name, grid_spec=grid_spec, out_shape=jax.ShapeDtypeStruct((4, r, w), term.dtype),
        compiler_params=_params(("parallel", "parallel")))(cidx, term, recv)


def _add_sibling_small(name, terms, recvs):
    nt = len(terms)

    def body(*refs):
        c = lax.axis_index("c")
        for t_ref, r_ref, o_ref in zip(refs[:nt], refs[nt:2 * nt], refs[2 * nt:]):
            for q in range(4):
                o_ref[q] = (t_ref[2 * q + c].astype(F32) + r_ref[q].astype(F32)).astype(o_ref.dtype)

    return pl.pallas_call(
        body, name=name, out_shape=[jax.ShapeDtypeStruct((4,) + t.shape[1:], t.dtype) for t in terms],
        compiler_params=_params())(*terms, *recvs)


def _add_chips(name, sums, recv, rows):
    _, r, w = sums.shape
    qidx = (2 * lax.axis_index("x") + lax.axis_index("y")).astype(jnp.int32).reshape(1)

    def body(q_ref, a_ref, b_ref, o_ref):
        o_ref[...] = ((a_ref[...].astype(F32) + b_ref[0].astype(F32))
                      + (b_ref[1].astype(F32) + b_ref[2].astype(F32)))

    grid_spec = pltpu.PrefetchScalarGridSpec(
        num_scalar_prefetch=1, grid=(r // rows,),
        in_specs=[pl.BlockSpec((None, rows, w), lambda i, q_ref: (q_ref[0], i, 0)),
                  pl.BlockSpec((3, rows, w), lambda i, q_ref: (0, i, 0))],
        out_specs=pl.BlockSpec((rows, w), lambda i, q_ref: (i, 0)))
    return pl.pallas_call(
        body, name=name, grid_spec=grid_spec, out_shape=jax.ShapeDtypeStruct((r, w), F32),
        compiler_params=_params(("parallel",)))(qidx, sums, recv)


def _rs_rows(a):
    return SHARD_IN // 4 if a.shape[1] == SHARD_IN else a.shape[1]


def _reduce_scatter_start(tag, names, terms, recv1=None):
    if recv1 is None:
        recv1 = _exchange_sibling("exchange_sibling_" + tag, terms)
    if len(terms) == 1:
        sums = [_add_sibling("add_sibling_" + names[0], terms[0], recv1[0], _rs_rows(terms[0]))]
    else:
        sums = _add_sibling_small("add_sibling_" + tag, terms, recv1)
    lands =[jax.ShapeDtypeStruct((3,) + s.shape[1:], s.dtype) for s in sums]
    send_sems, recv_sems, sums, lands, token = _exchange_start("exchange_chips_start_" + tag, "chips", sums, lands)
    return (tag, names, send_sems, recv_sems, sums, lands), token


def _reduce_scatter_wait(state, after):
    tag, names, send_sems, recv_sems, sums, lands = state
    sums, recv2 = _exchange_wait("exchange_chips_wait_" + tag, "chips", send_sems, recv_sems, sums, lands, after)
    return names, sums, recv2


def _adamw(name, w, g, m, v, dep=None):
    dep_specs, dep_args = _dep_operand(dep)

    def body(w_ref, g_ref, m_ref, v_ref, *rest):
        d_ref, nm_ref, nv_ref = rest[-3:]
        d_ref[...], nm_ref[...], nv_ref[...] = _adam_math(w_ref[...], g_ref[...], m_ref[...], v_ref[...])

    whole = pl.BlockSpec(memory_space=pltpu.VMEM)
    return pl.pallas_call(
        body, name=name, in_specs=[whole] * 4 + dep_specs, out_shape=[jax.ShapeDtypeStruct(w.shape, F32)] * 3,
        compiler_params=_params())(w, g, m, v, *dep_args)


def _adam_math(w, g, m, v):
    nm = ADAM_B1 * m + (1.0 - ADAM_B1) * g
    nv = ADAM_B2 * v + (1.0 - ADAM_B2) * (g * g)
    c1 = 1.0 - ADAM_B1 ** ADAM_STEP
    c2 = 1.0 - ADAM_B2 ** ADAM_STEP
    return -ADAM_LR * ((nm / c1) / (jnp.sqrt(nv / c2) + ADAM_EPS) + ADAM_WD * w), nm, nv


def _adamw_chips(name, sums, recv, w, m, v, transposed, rows=None, dep=None):
    r, c = w.shape
    rows = r if rows is None else rows
    qidx = (2 * lax.axis_index("x") + lax.axis_index("y")).astype(jnp.int32).reshape(1)
    dep_specs, dep_args = _dep_operand(dep)

    def body(q_ref, a_ref, b_ref, w_ref, m_ref, v_ref, *rest):
        g_ref, d_ref, nm_ref, nv_ref = rest[-4:]
        g = (a_ref[...].astype(F32) + b_ref[0].astype(F32)) + (b_ref[1].astype(F32) + b_ref[2].astype(F32))
        if transposed:
            g = g.T
        g_ref[...] = g
        d_ref[...], nm_ref[...], nv_ref[...] = _adam_math(w_ref[...], g, m_ref[...], v_ref[...])

    row = pl.BlockSpec((rows, c), lambda i, q_ref: (i, 0))
    if transposed:
        term_specs = [pl.BlockSpec((None, c, rows), lambda i, q_ref: (q_ref[0], 0, i)),
                      pl.BlockSpec((3, c, rows), lambda i, q_ref: (0, 0, i))]
    else:
        term_specs = [pl.BlockSpec((None, rows, c), lambda i, q_ref: (q_ref[0], i, 0)),
                      pl.BlockSpec((3, rows, c), lambda i, q_ref: (0, i, 0))]
    grid_spec = pltpu.PrefetchScalarGridSpec(
        num_scalar_prefetch=1, grid=(r // rows,), in_specs=term_specs + [row, row, row] + dep_specs,
        out_specs=[row] * 4)
    return pl.pallas_call(
        body, name=name, grid_spec=grid_spec, out_shape=[jax.ShapeDtypeStruct((r, c), F32)] * 4,
        compiler_params=_params(("parallel",)))(qidx, sums, recv, w, m, v, *dep_args)


def _sum_devices(gathered):
    def body(g_ref, o_ref):
        acc = g_ref[0]
        for j in range(1, N_DEV):
            acc = acc + g_ref[j]
        o_ref[...] = acc

    return pl.pallas_call(
        body, name="sum_devices", out_shape=jax.ShapeDtypeStruct(gathered.shape[1:], F32),
        compiler_params=_params())(gathered)


def _rows128(a, rows):
    flat = a.reshape(-1)
    return jnp.pad(flat, (0, rows * 128 - flat.shape[0])).reshape(rows, 128)


def kernel(x, mem, pre_norm, w_in, merge_bias, na_rpb, mem_norm, w_mem_kv, w_branch_a, w_branch_b, w_branch_c, w_out, post_norm, loss_target, m_pre_norm, m_w_in, m_merge_bias, m_na_rpb, m_mem_norm, m_w_mem_kv, m_w_branch_a, m_w_branch_b, m_w_branch_c, m_w_out, m_post_norm, v_pre_norm, v_w_in, v_merge_bias, v_na_rpb, v_mem_norm, v_w_mem_kv, v_w_branch_a, v_w_branch_b, v_w_branch_c, v_w_out, v_post_norm):
    wt_in_s = w_in[0].T.astype(BF16)
    rows_s = jnp.concatenate([w_mem_kv[0], w_out[0]], axis=0).astype(BF16)
    cols_s = jnp.concatenate([w_branch_a[0].T, w_branch_b[0].T, w_branch_c[0].T], axis=0).astype(BF16)
    mb_s = jnp.pad(merge_bias[0], ((0, 5), (0, 0)))
    wt_in = _all_gather(wt_in_s).reshape(N_IN, D_MODEL)

    late_own = [rows_s, cols_s, mb_s]
    late_lands = [jax.ShapeDtypeStruct((N_DEV,) + s.shape, s.dtype) for s in late_own]
    l_send, l_recv, late_own, late_lands, late_token = _exchange_start("gather_late_start", "gather", late_own,
                                                                       late_lands, after=wt_in)
    me = 4 * lax.axis_index("x") + 2 * lax.axis_index("y") + lax.axis_index("c")

    def late_weights(after):
        own, lands = _exchange_wait("gather_late_wait", "gather", l_send, l_recv, late_own, late_lands, after)
        g_rows, g_cols, g_mb = [lax.dynamic_update_slice(land, o[None], (me, 0, 0)) for land, o in zip(lands, own)]
        return (g_mb[:, :3].transpose(1, 0, 2).reshape(3, D_MODEL),
                g_rows[:, :128].reshape(D_MODEL, D_MODEL), g_cols[:, 0:128].reshape(D_MODEL, 512),
                g_cols[:, 128:256].reshape(D_MODEL, 512), g_cols[:, 256:384].reshape(D_MODEL, 512),
                g_rows[:, 128:].reshape(D_MODEL, D_MODEL))

    rs_state = []

    def reduce_start(grads):
        if "wt_in" in grads:
            own, sibling = [a.reshape(N_DEV, SHARD_IN, D_MODEL) for a in grads["wt_in"]]
            state, token = _reduce_scatter_start("w_in", ["w_in"], [own], [sibling])
        else:
            gmb_t = jnp.pad(grads["merge_bias"].reshape(3, N_DEV, 128).transpose(1, 0, 2), ((0, 0), (0, 5), (0, 0)))
            names = ["w_kv", "w_out", "a", "b", "c", "mb"]
            terms = [grads["w_kv"].reshape(N_DEV, 128, D_MODEL), grads["w_out"].reshape(N_DEV, 128, D_MODEL),
                     grads["wt_a"].reshape(N_DEV, 128, 512), grads["wt_b"].reshape(N_DEV, 128, 512),
                     grads["wt_c"].reshape(N_DEV, 128, 512), gmb_t]
            state, token = _reduce_scatter_start("rest", names, terms)
        rs_state.append(state)
        return token

    loss_term, grad_x, grads = _local_step(
        x[0], mem[0], loss_target[0], pre_norm, mem_norm, post_norm, na_rpb[0], wt_in, late_weights,
        dep_in=late_token, reduce_start=reduce_start)

    small = jnp.concatenate([_rows128(grads["pre_norm"], 8), _rows128(grads["mem_norm"], 8),
                             _rows128(grads["post_norm"], 8), _rows128(grads["na_rpb"], 32),
                             _rows128(loss_term, 8)], axis=0)
    s_send, s_recv, s_own, s_land, s_token = _exchange_start(
        "gather_small_start", "gather", [small], [jax.ShapeDtypeStruct((N_DEV,) + small.shape, F32)])
    grad = {}
    weights = {
        "pre_norm": (pre_norm, m_pre_norm, v_pre_norm), "w_in": (w_in, m_w_in, v_w_in),
        "merge_bias": (merge_bias, m_merge_bias, v_merge_bias), "na_rpb": (na_rpb, m_na_rpb, v_na_rpb),
        "mem_norm": (mem_norm, m_mem_norm, v_mem_norm), "w_mem_kv": (w_mem_kv, m_w_mem_kv, v_w_mem_kv),
        "w_branch_a": (w_branch_a, m_w_branch_a, v_w_branch_a), "w_branch_b": (w_branch_b, m_w_branch_b, v_w_branch_b),
        "w_branch_c": (w_branch_c, m_w_branch_c, v_w_branch_c), "w_out": (w_out, m_w_out, v_w_out),
        "post_norm": (post_norm, m_post_norm, v_post_norm)}
    order = ["pre_norm", "w_in", "merge_bias", "na_rpb", "mem_norm", "w_mem_kv", "w_branch_a", "w_branch_b",
             "w_branch_c", "w_out", "post_norm"]
    delta, new_m, new_v = {}, {}, {}

    def update(n, dep=None):
        w, m, v = weights[n]
        shape = w.shape
        two_d = (-1, shape[-1])
        dl, nm, nv = _adamw("adamw_" + n, w.reshape(two_d), grad[n].reshape(two_d), m.reshape(two_d),
                            v.reshape(two_d), dep)
        delta[n], new_m[n], new_v[n] = dl.reshape(shape), nm.reshape(shape), nv.reshape(shape)
        return dl

    def update_sharded(n, sums, recv, transposed, rows=None, dep=None):
        w, m, v = weights[n]
        g, dl, nm, nv = _adamw_chips("adamw_" + n, sums, recv, w[0], m[0], v[0], transposed, rows, dep)
        grad[n], delta[n], new_m[n], new_v[n] = g[None], dl[None], nm[None], nv[None]
        return dl

    _, sums, recv2 = _reduce_scatter_wait(rs_state[0], s_token)
    dep = None
    for i, (n, transposed) in enumerate((("w_mem_kv", False), ("w_out", False), ("w_branch_a", True),
                                         ("w_branch_b", True), ("w_branch_c", True))):
        dep = update_sharded(n, sums[i], recv2[i], transposed, dep=dep)
    grad["merge_bias"] = _add_chips("add_chips_mb", sums[5], recv2[5], 8)[:3][None]
    update("merge_bias")
    s_own, s_land = _exchange_wait("gather_small_wait", "gather", s_send, s_recv, s_own, s_land, dep)
    total = _sum_devices(lax.dynamic_update_slice(s_land[0], s_own[0][None], (me, 0, 0)))
    loss = total[56, 0]
    grad.update({"pre_norm": total[0:8].reshape(1, D_MODEL), "mem_norm": total[8:16].reshape(1, D_MODEL),
                 "post_norm": total[16:24].reshape(1, D_MODEL),
                 "na_rpb": total[24:56].reshape(-1)[:8 * 15 * 31].reshape(1, 8, 15, 31)})
    dep = None
    for n in ("pre_norm", "na_rpb", "mem_norm", "post_norm"):
        dep = update(n, dep)
    _, sums_in, recv_in = _reduce_scatter_wait(rs_state[1], dep)
    update_sharded("w_in", sums_in[0], recv_in[0], True, 256)

    return (loss, grad_x[None], *[grad[n] for n in order], *[delta[n] for n in order],
            *[new_m[n] for n in order], *[new_v[n] for n in order])
```

```python
import functools

import numpy as np
import jax
import jax.numpy as jnp
from jax import lax
from jax.experimental import pallas as pl
from jax.experimental.pallas import tpu as pltpu

F32 = jnp.float32
BF16 = jnp.bfloat16

SEQ = 2048
D_MODEL = 1024
N_IN = 11264
N_DEV = 8
SHARD_IN = N_IN // N_DEV
HEAD_DIM = 64
GRID_W = 64
NA_ROWS = 8
MEM_LEN = 256
DILATIONS = (1, 4, 16)
REACH = 64
ROPE_THETA = 500000.0
ROPE_DIM = 16
EPS = 1e-6
NEG = -1e30
ADAM_LR = 0.001
ADAM_B1 = 0.9
ADAM_B2 = 0.999
ADAM_EPS = 1e-08
ADAM_WD = 0.01
ADAM_STEP = 10

VMEM_LIMIT_BYTES = 56 * 1024 * 1024
MESH_ID = pl.DeviceIdType.MESH

NN = (((1,), (0,)), ((), ()))
NT = (((1,), (1,)), ((), ()))
TN = (((0,), (0,)), ((), ()))


def _params(sem=None):
    return pltpu.CompilerParams(dimension_semantics=sem, vmem_limit_bytes=VMEM_LIMIT_BYTES)


def _iota(shape, dim):
    return lax.broadcasted_iota(jnp.int32, shape, dim)


def _sigmoid(x):
    return 1.0 / (1.0 + jnp.exp(-x))


def _rope_tables():
    half = ROPE_DIM // 2
    inv = (ROPE_THETA ** (-np.arange(half, dtype=np.float64) * 2.0 / ROPE_DIM)).astype(np.float32)
    pos = np.arange(SEQ, dtype=np.float32)
    ang = pos[:, None] * inv[None, :]
    cos, sin = np.cos(ang), np.sin(ang)
    zeros = np.zeros_like(cos)
    rest = HEAD_DIM - ROPE_DIM
    c64 = np.concatenate([cos, cos, np.ones((SEQ, rest), np.float32)], axis=1)
    s1 = np.concatenate([zeros, sin, np.zeros((SEQ, rest), np.float32)], axis=1)
    s2 = np.concatenate([-sin, zeros, np.zeros((SEQ, rest), np.float32)], axis=1)

    def fold(t, d):
        return t.reshape(SEQ // d, d, t.shape[1]).transpose(1, 0, 2).reshape(SEQ, t.shape[1])

    tabs = [np.stack([np.tile(fold(t, d), (1, 2)) for t in (c64, s1, s2)], axis=0) for d in DILATIONS]
    return jnp.asarray(np.stack(tabs, axis=0), dtype=F32)


def _rope(a, c, s1, s2):
    return a * c + pltpu.roll(a, 8, 1) * s1 + pltpu.roll(a, 120, 1) * s2


def _rope_t(a, c, s1, s2):
    return a * c + pltpu.roll(a * s1, 120, 1) + pltpu.roll(a * s2, 8, 1)


def _perm_of_block(j):
    return jnp.where(j < 3, 0, jnp.where(j < 6, 1, jnp.where(j < 9, 2, 0)))


def _mm(name, a, b, out_shape, out_dtype, grid, a_spec, b_spec, o_spec, acc_shape, dims, k_axis, nk):
    def body(a_ref, b_ref, o_ref, acc_ref):
        k = pl.program_id(k_axis)

        @pl.when(k == 0)
        def _():
            acc_ref[...] = jnp.zeros(acc_shape, F32)

        acc_ref[...] += lax.dot_general(a_ref[...], b_ref[...], dims, preferred_element_type=F32)

        @pl.when(k == nk - 1)
        def _():
            o_ref[...] = acc_ref[...].astype(out_dtype)

    sem = tuple("arbitrary" if ax == k_axis else "parallel" for ax in range(len(grid)))
    return pl.pallas_call(
        body, name=name, grid=grid, in_specs=[a_spec, b_spec], out_specs=o_spec,
        out_shape=jax.ShapeDtypeStruct(out_shape, out_dtype),
        scratch_shapes=[pltpu.VMEM(acc_shape, F32)], compiler_params=_params(sem))(a, b)


def _mm_simple(name, a, b, dims, out_dtype, tm, tn, tk):
    if dims is NN:
        m, kk = a.shape
        n = b.shape[1]
        a_spec = pl.BlockSpec((tm, tk), lambda i, j, k: (i, k))
        b_spec = pl.BlockSpec((tk, tn), lambda i, j, k: (k, j))
    elif dims is NT:
        m, kk = a.shape
        n = b.shape[0]
        a_spec = pl.BlockSpec((tm, tk), lambda i, j, k: (i, k))
        b_spec = pl.BlockSpec((tn, tk), lambda i, j, k: (j, k))
    else:
        kk, m = a.shape
        n = b.shape[1]
        a_spec = pl.BlockSpec((tk, tm), lambda i, j, k: (k, i))
        b_spec = pl.BlockSpec((tk, tn), lambda i, j, k: (k, j))
    grid = (m // tm, n // tn, kk // tk)
    o_spec = pl.BlockSpec((tm, tn), lambda i, j, k: (i, j))
    return _mm(name, a, b, (m, n), out_dtype, grid, a_spec, b_spec, o_spec, (tm, tn), dims, 2, kk // tk)


def _rmsnorm_fwd(name, x, gain, rows):
    n, d = x.shape

    def body(x_ref, g_ref, o_ref):
        xv = x_ref[...]
        rstd = lax.rsqrt(jnp.mean(xv * xv, axis=1, keepdims=True) + EPS)
        o_ref[...] = (xv * rstd * g_ref[...]).astype(BF16)

    return pl.pallas_call(
        body, name=name, grid=(n // rows,),
        in_specs=[pl.BlockSpec((rows, d), lambda i: (i, 0)), pl.BlockSpec((1, d), lambda i: (0, 0))],
        out_specs=pl.BlockSpec((rows, d), lambda i: (i, 0)),
        out_shape=jax.ShapeDtypeStruct((n, d), BF16), compiler_params=_params(("parallel",)))(x, gain)


def _folded_rows(first, rows, d):
    if d == 1:
        return pl.ds(pl.multiple_of(first, rows), rows)
    mlen = SEQ // d
    return pl.ds((first % mlen) * d + first // mlen, rows, stride=d)


def _prenorm_fold(x, gain):
    rows = 128

    nchunk = D_MODEL // 128

    def body(*refs):
        x_refs, g_ref, hs_ref, hst_ref = refs[:nchunk], refs[nchunk], refs[nchunk + 1], refs[nchunk + 2]
        first = pl.program_id(0) * rows
        for p, d in enumerate(DILATIONS):
            idx = _folded_rows(first, rows, d)
            xv = jnp.concatenate([r[idx, :] for r in x_refs], axis=1)
            rstd = lax.rsqrt(jnp.mean(xv * xv, axis=1, keepdims=True) + EPS)
            h = xv * rstd * g_ref[...]
            hs_ref[p] = h.astype(BF16)
            hst_ref[p] = h.T.astype(BF16)

    x_specs = [pl.BlockSpec((SEQ, 128), functools.partial(lambda c, i: (0, c), c)) for c in range(nchunk)]
    return pl.pallas_call(
        body, name="prenorm", grid=(SEQ // rows,),
        in_specs=x_specs + [pl.BlockSpec((1, D_MODEL), lambda i: (0, 0))],
        out_specs=[pl.BlockSpec((3, rows, D_MODEL), lambda i: (0, i, 0)),
                   pl.BlockSpec((3, D_MODEL, rows), lambda i: (0, 0, i))],
        out_shape=[jax.ShapeDtypeStruct((3, SEQ, D_MODEL), BF16), jax.ShapeDtypeStruct((3, D_MODEL, SEQ), BF16)],
        compiler_params=_params(("parallel",)))(*([x] * nchunk), gain)


def _prenorm_bwd(x, gain, dh, dout):
    rows = 256

    def body(x_ref, g_ref, a_ref, do_ref, dx_ref, gg_ref):
        xv = x_ref[...]
        rstd = lax.rsqrt(jnp.mean(xv * xv, axis=1, keepdims=True) + EPS)
        xn = xv * rstd
        dh = jnp.concatenate([a_ref[c] for c in range(D_MODEL // 128)], axis=1)
        gdh = dh * g_ref[...]
        dx_ref[...] = rstd * (gdh - xn * jnp.mean(gdh * xn, axis=1, keepdims=True)) + do_ref[...]

        @pl.when(pl.program_id(0) == 0)
        def _():
            gg_ref[...] = jnp.zeros((1, D_MODEL), F32)

        gg_ref[...] += jnp.sum(dh * xn, axis=0, keepdims=True)

    row = pl.BlockSpec((rows, D_MODEL), lambda i: (i, 0))
    vec = pl.BlockSpec((1, D_MODEL), lambda i: (0, 0))
    return pl.pallas_call(
        body, name="prenorm_bwd", grid=(SEQ // rows,),
        in_specs=[row, vec, pl.BlockSpec((D_MODEL // 128, rows, 128), lambda i: (0, i, 0)), row], out_specs=[row, vec],
        out_shape=[jax.ShapeDtypeStruct((SEQ, D_MODEL), F32), jax.ShapeDtypeStruct((1, D_MODEL), F32)],
        compiler_params=_params(("arbitrary",)))(x, gain, dh, dout)


def _memnorm_bwd(mem, dmemn, dep=None):
    dep_specs, dep_args = _dep_operand(dep)

    def body(m_ref, d_ref, *rest):
        mv = m_ref[...]
        rstd = lax.rsqrt(jnp.mean(mv * mv, axis=1, keepdims=True) + EPS)
        rest[-1][...] = jnp.sum(d_ref[...] * mv * rstd, axis=0, keepdims=True)

    whole = pl.BlockSpec(memory_space=pltpu.VMEM)
    return pl.pallas_call(
        body, name="memnorm_bwd", in_specs=[whole, whole] + dep_specs,
        out_shape=jax.ShapeDtypeStruct((1, D_MODEL), F32), compiler_params=_params())(mem, dmemn, *dep_args)


def _dep_operand(dep):
    return ([], []) if dep is None else ([pl.BlockSpec(memory_space=pl.ANY)], [dep])


def _in_proj(hs, wt, tabs, dep=None):
    tm, tn = 512, 512
    dep_specs, dep_args = _dep_operand(dep)

    def body(h_ref, w_ref, t_ref, *rest):
        o_ref = rest[-1]
        j = pl.program_id(0)
        is_rope = jnp.logical_and(j < 9, j % 3 != 2)
        row_slices = [slice(r * tm, (r + 1) * tm) for r in range(SEQ // tm)]

        def product(rs):
            return lax.dot_general(h_ref[rs, :], w_ref[...], NT, preferred_element_type=F32)

        @pl.when(is_rope)
        def _():
            for rs in row_slices:
                acc = product(rs)
                c, s1, s2 = t_ref[0, rs, :], t_ref[1, rs, :], t_ref[2, rs, :]
                for q in range(tn // 128):
                    a = acc[:, q * 128:(q + 1) * 128]
                    o_ref[rs, q * 128:(q + 1) * 128] = _rope(a, c, s1, s2).astype(BF16)

        @pl.when(jnp.logical_not(is_rope))
        def _():
            for rs in row_slices:
                o_ref[rs, :] = product(rs).astype(BF16)

    return pl.pallas_call(
        body, name="in_proj", grid=(N_IN // tn,),
        in_specs=[pl.BlockSpec((None, SEQ, D_MODEL), lambda j: (_perm_of_block(j), 0, 0)),
                  pl.BlockSpec((tn, D_MODEL), lambda j: (j, 0)),
                  pl.BlockSpec((None, 3, SEQ, 128), lambda j: (_perm_of_block(j), 0, 0, 0))] + dep_specs,
        out_specs=pl.BlockSpec((SEQ, tn), lambda j: (0, j)),
        out_shape=jax.ShapeDtypeStruct((SEQ, N_IN), BF16),
        compiler_params=_params(("parallel",)))(hs, wt, tabs, *dep_args)


def _piece_blocks(pieces):
    return [(a, h * 512) for a, p in enumerate(pieces) for h in range(p.shape[1] // 512)]


def _block_fetch(piece_refs, blocks, buf, sem):
    def start(block, slot):
        for b, (a, col) in enumerate(blocks):
            @pl.when(block == b)
            def _():
                pltpu.make_async_copy(piece_refs[a].at[:, pl.ds(col, 512)], buf.at[slot], sem.at[slot]).start()

    def wait(slot):
        pltpu.make_async_copy(piece_refs[0].at[:, pl.ds(0, 512)], buf.at[slot], sem.at[slot]).wait()

    return start, wait


def _in_proj_dw(pieces, hst, dep=None):
    tn = 512
    blocks = _piece_blocks(pieces)
    nblk = len(blocks)
    npc = len(pieces)
    dep_specs, dep_args = _dep_operand(dep)

    def body(h_ref, *rest):
        piece_refs = rest[:npc]
        o_ref, mirror, buf, sem, out_buf, send_sems, recv_sem = rest[-7:]
        j = pl.program_id(0)
        slot = j % 2
        start, wait = _block_fetch(piece_refs, blocks, buf, sem)
        x, y, c = _place()

        def to_sibling(step, slot_):
            return pltpu.make_async_remote_copy(
                src_ref=out_buf.at[slot_], dst_ref=mirror.at[pl.ds(pl.multiple_of(step * tn, tn), tn)],
                send_sem=send_sems.at[slot_], recv_sem=recv_sem, device_id=(x, y, 1 - c), device_id_type=MESH_ID)

        @pl.when(j == 0)
        def _():
            start(j, slot)

        wait(slot)

        @pl.when(j + 1 < nblk)
        def _():
            start(j + 1, 1 - slot)

        acc = jnp.dot(h_ref[...], buf[slot], preferred_element_type=F32)
        block = acc.T.astype(BF16)
        o_ref[...] = block

        @pl.when(j >= 2)
        def _():
            to_sibling(j - 2, slot).wait_send()

        out_buf[slot] = block
        to_sibling(j, slot).start()

        @pl.when(j == nblk - 1)
        def _():
            to_sibling(j - 1, 1 - slot).wait_send()
            to_sibling(j, slot).wait_send()
            pltpu.make_async_remote_copy(src_ref=mirror, dst_ref=mirror, send_sem=send_sems.at[0], recv_sem=recv_sem,
                                         device_id=(x, y, 1 - c), device_id_type=MESH_ID).wait_recv()

    return pl.pallas_call(
        body, name="in_proj_dw", grid=(nblk,),
        in_specs=[pl.BlockSpec((None, D_MODEL, SEQ), lambda j: (_perm_of_block(j), 0, 0))] + [ANY] * npc + dep_specs,
        out_specs=[pl.BlockSpec((tn, D_MODEL), lambda j: (j, 0)), ANY],
        out_shape=[jax.ShapeDtypeStruct((N_IN, D_MODEL), BF16), jax.ShapeDtypeStruct((N_IN, D_MODEL), BF16)],
        scratch_shapes=[pltpu.VMEM((2, SEQ, tn), BF16), pltpu.SemaphoreType.DMA((2,)),
                        pltpu.VMEM((2, tn, D_MODEL), BF16), pltpu.SemaphoreType.DMA((2,)), pltpu.SemaphoreType.DMA],
        compiler_params=_params(("arbitrary",)))(hst, *pieces, *dep_args)


def _in_proj_dh(pieces, wt, dep=None):
    tk = 512
    blocks = _piece_blocks(pieces)
    nblk = len(blocks)
    npc = len(pieces)
    nchunk = D_MODEL // 128

    def col(s):
        return jnp.where(s < 3, s, jnp.where(s < 16, s + 6, s - 13))

    dep_specs, dep_args = _dep_operand(dep)

    def body(w_ref, *rest):
        piece_refs = rest[:npc]
        o_ref, acc_ref, buf, sem = rest[-4:]
        s = pl.program_id(0)
        slot = s % 2
        start, wait = _block_fetch(piece_refs, blocks, buf, sem)

        @pl.when(s == 0)
        def _():
            start(col(s), slot)

        wait(slot)

        @pl.when(s + 1 < nblk)
        def _():
            start(col(s + 1), 1 - slot)

        row_slices = [slice(r * 512, (r + 1) * 512) for r in range(SEQ // 512)]

        def product(rs):
            return jnp.dot(buf[slot, rs, :], w_ref[...], preferred_element_type=F32)

        def accumulate(cond, to_out, init):
            @pl.when(cond)
            def _():
                for rs in row_slices:
                    prod = product(rs)
                    if not to_out:
                        if init:
                            acc_ref[rs, :] = prod
                        else:
                            acc_ref[rs, :] += prod
                        continue
                    for c in range(nchunk):
                        if init:
                            o_ref[c, rs, :] = prod[:, c * 128:(c + 1) * 128]
                        else:
                            o_ref[c, rs, :] += prod[:, c * 128:(c + 1) * 128]

        accumulate(s == 0, True, True)
        accumulate(jnp.logical_and(s > 0, s < 16), True, False)
        accumulate(jnp.logical_or(s == 16, s == 19), False, True)
        accumulate(jnp.logical_and(s > 16, s != 19), False, False)
        for last, d in ((18, 4), (21, 16)):
            @pl.when(s == last)
            def _():
                mlen = SEQ // d
                for r in range(d):
                    for c in range(nchunk):
                        o_ref[c, pl.ds(r, mlen, stride=d), :] += acc_ref[r * mlen:(r + 1) * mlen,
                                                                         c * 128:(c + 1) * 128]

    return pl.pallas_call(
        body, name="in_proj_dh", grid=(nblk,),
        in_specs=[pl.BlockSpec((tk, D_MODEL), lambda s: (col(s), 0))] + [ANY] * npc + dep_specs,
        out_specs=pl.BlockSpec((nchunk, SEQ, 128), lambda s: (0, 0, 0)),
        out_shape=jax.ShapeDtypeStruct((nchunk, SEQ, 128), F32),
        scratch_shapes=[pltpu.VMEM((SEQ, D_MODEL), F32), pltpu.VMEM((2, SEQ, tk), BF16),
                        pltpu.SemaphoreType.DMA((2,))],
        compiler_params=_params(("arbitrary",)))(wt, *pieces, *dep_args)


def _head_lanes(lanes, hh):
    return lanes >= 64 if hh == 1 else lanes < 64


def _head_rows(x, lanes, hh, pair):
    if not pair:
        return jnp.max(x, axis=1, keepdims=True)
    return jnp.max(jnp.where(_head_lanes(lanes, hh), x, -jnp.inf), axis=1, keepdims=True)


def _mask_head(x, lanes, hh, pair, scale=1.0):
    if not pair:
        return x
    xf = x.astype(F32) if scale == 1.0 else x.astype(F32) * scale
    return jnp.where(_head_lanes(lanes, hh), xf, 0.0).astype(BF16)


def _window(mode, qi, tq, mlen, tk):
    if mode == "dil":
        q0 = qi * tq
        seg = (q0 // mlen) * mlen
        ks = jnp.clip(q0 - REACH, seg, seg + mlen - tk)
        return pl.multiple_of(ks, 64)
    if mode == "na":
        r_start = jnp.clip(qi - NA_ROWS // 2, 0, SEQ // GRID_W - NA_ROWS)
        return pl.multiple_of(r_start * GRID_W, 64)
    return 0


def _band_mask(qi, tq, tk, ks):
    qpos = qi * tq + _iota((tq, tk), 0)
    kpos = ks + _iota((tq, tk), 1)
    return jnp.where(jnp.abs(qpos - kpos) <= REACH, 0.0, NEG).astype(F32)


def _stack_heads(x, lanes, pair, scale=1.0):
    if not pair:
        return x
    return jnp.concatenate([_mask_head(x, lanes, hh, pair, scale) for hh in range(2)], axis=0)


def _stack_rows(x, lanes, pair):
    if not pair:
        return _head_rows(x, lanes, 0, pair)
    return jnp.concatenate([_head_rows(x, lanes, hh, pair) for hh in range(2)], axis=0)


def _unstack_heads(x, lanes, pair, tq):
    if not pair:
        return x
    return jnp.where(lanes < 64, x[:tq], x[tq:])


def _scores(mode, qst, k, sscale, band, qi, bias_ref, pair):
    s = lax.dot_general(qst, k, NT, preferred_element_type=F32)
    if sscale != 1.0:
        s = s * sscale
    if mode == "dil":
        s = s + jnp.concatenate([band, band], axis=0)
    elif mode == "na":
        off = qi - jnp.clip(qi - NA_ROWS // 2, 0, SEQ // GRID_W - NA_ROWS)
        s = s + jnp.concatenate([bias_ref[0, off], bias_ref[1, off]], axis=0)
    return s


def _attn_cfg(mode, d):
    if mode == "dil":
        mlen = SEQ // d
        return dict(pair=True, tq=128, tk=min(256, mlen), mlen=mlen, lk=SEQ, scale=HEAD_DIM ** -0.5, units=4,
                    nsub=ATTN_SUBTILES)
    if mode == "na":
        return dict(pair=True, tq=GRID_W, tk=NA_ROWS * GRID_W, mlen=SEQ, lk=SEQ, scale=HEAD_DIM ** -0.5, units=4,
                    nsub=ATTN_SUBTILES)
    return dict(pair=False, tq=128, tk=MEM_LEN, mlen=SEQ, lk=MEM_LEN, scale=128 ** -0.5, units=4,
                nsub=ATTN_SUBTILES)


ATTN_SUBTILES = 16


def _attn_fwd(name, mode, q_arr, k_arr, v_arr, qcol, kcol, vcol, d=1, bias=None):
    cfg = _attn_cfg(mode, d)
    pair, tq, tk, mlen, lk, scale = cfg["pair"], cfg["tq"], cfg["tk"], cfg["mlen"], cfg["lk"], cfg["scale"]
    qscale, sscale = (scale, 1.0) if pair else (1.0, scale)
    nsub = cfg["nsub"]
    rows = nsub * tq

    def body(*refs):
        if mode == "na":
            q_ref, k_ref, v_ref, bias_ref, o_ref, l_ref = refs
        else:
            q_ref, k_ref, v_ref, o_ref, l_ref = refs
            bias_ref = None
        lanes = _iota((tq, 128), 1)
        qis = [pl.program_id(1) * nsub + sub for sub in range(nsub)]
        kss = [_window(mode, qi, tq, mlen, tk) for qi in qis]
        vs = [v_ref[pl.ds(ks, tk), :] for ks in kss]
        bands = [_band_mask(qi, tq, tk, ks) if mode == "dil" else None for qi, ks in zip(qis, kss)]
        ss = []
        for sub in range(nsub):
            qst = _stack_heads(q_ref[sub * tq:(sub + 1) * tq, :], lanes, pair, qscale)
            k = k_ref[pl.ds(kss[sub], tk), :]
            ss.append(_scores(mode, qst, k, sscale, bands[sub], qis[sub], bias_ref, pair))
        ms = [jnp.max(s_, axis=1, keepdims=True) for s_ in ss]
        ps = [jnp.exp(s_ - m) for s_, m in zip(ss, ms)]
        ls = [jnp.sum(p, axis=1, keepdims=True) for p in ps]
        os_ = [jnp.dot(p.astype(BF16), v, preferred_element_type=F32) for p, v in zip(ps, vs)]
        for sub in range(nsub):
            out = _unstack_heads(os_[sub] / ls[sub], lanes, pair, tq)
            lse = ms[sub] + jnp.log(ls[sub])
            lse = _unstack_heads(jnp.broadcast_to(lse, (lse.shape[0], 128)), lanes, pair, tq)
            dst = _folded_rows(qis[sub] * tq, tq, d) if mode == "dil" else slice(sub * tq, (sub + 1) * tq)
            o_ref[dst, :] = out
            l_ref[dst, :] = lse

    in_specs = [pl.BlockSpec((rows, 128), lambda u, i: (i, qcol + u)),
                pl.BlockSpec((lk, 128), lambda u, i: (0, kcol + u)),
                pl.BlockSpec((lk, 128), lambda u, i: (0, vcol + u))]
    args = [q_arr, k_arr, v_arr]
    if mode == "na":
        in_specs.append(pl.BlockSpec((2, NA_ROWS, GRID_W, NA_ROWS * GRID_W), lambda u, i: (u, 0, 0, 0)))
        args.append(bias)
    if mode == "dil":
        out_spec = pl.BlockSpec((SEQ, 128), lambda u, i: (0, u))
    else:
        out_spec = pl.BlockSpec((rows, 128), lambda u, i: (i, u))
    return pl.pallas_call(
        body, name=name, grid=(cfg["units"], SEQ // rows), in_specs=in_specs, out_specs=[out_spec, out_spec],
        out_shape=[jax.ShapeDtypeStruct((SEQ, 512), F32), jax.ShapeDtypeStruct((SEQ, 512), F32)],
        compiler_params=_params(("parallel", "arbitrary")))(*args)


def _attn_bwd(name, mode, q_arr, k_arr, v_arr, qcol, kcol, vcol, do, lse, dp=None, o=None, d=1, bias=None,
              tabs=None):
    cfg = _attn_cfg(mode, d)
    pair, tq, tk, mlen, lk, scale = cfg["pair"], cfg["tq"], cfg["tk"], cfg["mlen"], cfg["lk"], cfg["scale"]
    qscale, sscale = (scale, 1.0) if pair else (1.0, scale)
    nsub = cfg["nsub"]
    rows = nsub * tq
    nq = SEQ // rows
    kv_dtype = F32 if mode == "mem" else BF16

    def body(*refs):
        refs = list(refs)
        q_ref, k_ref, v_ref, do_ref, l_ref = refs[:5]
        rest = refs[5:]
        bias_ref = tq_ref = tk_ref = db_ref = None
        if mode == "dil":
            dp_ref, tq_ref, tk_ref, dq_ref, dk_ref, dv_ref, dk_acc, dv_acc = rest
        elif mode == "na":
            o_ref, bias_ref, dq_ref, dk_ref, dv_ref, db_ref, dk_acc, dv_acc = rest
        else:
            o_ref, dq_ref, dk_ref, dv_ref, dk_acc, dv_acc = rest
        step = pl.program_id(1)

        @pl.when(step == 0)
        def _():
            dk_acc[...] = jnp.zeros((lk, 128), F32)
            dv_acc[...] = jnp.zeros((lk, 128), F32)
            if mode == "na":
                db_ref[...] = jnp.zeros(db_ref.shape, F32)

        lanes = _iota((tq, 128), 1)
        qis = [step * nsub + sub for sub in range(nsub)]
        sls = [slice(sub * tq, (sub + 1) * tq) for sub in range(nsub)]
        kss = [_window(mode, qi, tq, mlen, tk) for qi in qis]
        ks_ = [k_ref[pl.ds(ks, tk), :] for ks in kss]
        vs = [v_ref[pl.ds(ks, tk), :] for ks in kss]
        qsts, dosts, lses, dphs = [], [], [], []
        for sub in range(nsub):
            if mode == "dil":
                src = _folded_rows(qis[sub] * tq, tq, d)
                dov = do_ref[src, :].astype(BF16)
                lsev = l_ref[src, :]
                dphs.append(_stack_rows(dp_ref[src, :], lanes, pair))
            else:
                dov = do_ref[sls[sub], :]
                lsev = l_ref[sls[sub], :]
                dpv = dov.astype(F32) * o_ref[sls[sub], :]
                if pair:
                    dphs.append(jnp.concatenate(
                        [jnp.sum(jnp.where(_head_lanes(lanes, hh), dpv, 0.0), axis=1, keepdims=True)
                         for hh in range(2)], axis=0))
                else:
                    dphs.append(jnp.sum(dpv, axis=1, keepdims=True))
            qsts.append(_stack_heads(q_ref[sls[sub], :], lanes, pair, qscale))
            dosts.append(_stack_heads(dov, lanes, pair))
            lses.append(_stack_rows(lsev, lanes, pair))
        bands = [_band_mask(qi, tq, tk, ks) if mode == "dil" else None for qi, ks in zip(qis, kss)]
        ss = [_scores(mode, qsts[sub], ks_[sub], sscale, bands[sub], qis[sub], bias_ref, pair) for sub in range(nsub)]
        dpms = [lax.dot_general(dosts[sub], vs[sub], NT, preferred_element_type=F32) for sub in range(nsub)]
        ps = [jnp.exp(s_ - lse) for s_, lse in zip(ss, lses)]
        dss = [p * (dpm - dph) for p, dpm, dph in zip(ps, dpms, dphs)]
        if mode == "na":
            for sub, ds in enumerate(dss):
                off = qis[sub] - jnp.clip(qis[sub] - NA_ROWS // 2, 0, SEQ // GRID_W - NA_ROWS)
                db_ref[0, off] += ds[:tq]
                db_ref[1, off] += ds[tq:]
        dsbs = [ds.astype(BF16) for ds in dss]
        dvs = [lax.dot_general(p.astype(BF16), dosts[sub], TN, preferred_element_type=F32)
               for sub, p in enumerate(ps)]
        dqs = [jnp.dot(dsb, ks_[sub], preferred_element_type=F32) * scale for sub, dsb in enumerate(dsbs)]
        dks = [lax.dot_general(dsb, qsts[sub], TN, preferred_element_type=F32) for sub, dsb in enumerate(dsbs)]
        for sub in range(nsub):
            sl = sls[sub]
            dq = _unstack_heads(dqs[sub], lanes, pair, tq)
            if mode == "dil":
                dq = _rope_t(dq, tq_ref[0, sl, :], tq_ref[1, sl, :], tq_ref[2, sl, :])
            dq_ref[sl, :] = dq.astype(BF16)
            dk_acc[pl.ds(kss[sub], tk), :] += dks[sub] if pair else dks[sub] * scale
            dv_acc[pl.ds(kss[sub], tk), :] += dvs[sub]

        @pl.when(step == nq - 1)
        def _():
            dkv = dk_acc[...]
            if mode == "dil":
                dkv = _rope_t(dkv, tk_ref[0], tk_ref[1], tk_ref[2])
            dk_ref[...] = dkv.astype(kv_dtype)
            dv_ref[...] = dv_acc[...].astype(kv_dtype)

    q_spec = pl.BlockSpec((rows, 128), lambda u, i: (i, qcol + u))
    row_spec = pl.BlockSpec((rows, 128), lambda u, i: (i, u))
    kv_out = pl.BlockSpec((lk, 128), lambda u, i: (0, u))
    whole = pl.BlockSpec((SEQ, 128), lambda u, i: (0, u))
    nat_spec = whole if mode == "dil" else row_spec
    in_specs = [q_spec,
                pl.BlockSpec((lk, 128), lambda u, i: (0, kcol + u)),
                pl.BlockSpec((lk, 128), lambda u, i: (0, vcol + u)),
                nat_spec, nat_spec]
    args = [q_arr, k_arr, v_arr, do, lse]
    out_specs = [row_spec, kv_out, kv_out]
    out_shape = [jax.ShapeDtypeStruct((SEQ, 512), BF16), jax.ShapeDtypeStruct((lk, 512), kv_dtype),
                 jax.ShapeDtypeStruct((lk, 512), kv_dtype)]
    if mode == "dil":
        in_specs += [whole, pl.BlockSpec((3, rows, 128), lambda u, i: (0, i, 0)),
                     pl.BlockSpec((3, SEQ, 128), lambda u, i: (0, 0, 0))]
        args += [dp, tabs, tabs]
    elif mode == "na":
        b_spec = pl.BlockSpec((2, NA_ROWS, GRID_W, NA_ROWS * GRID_W), lambda u, i: (u, 0, 0, 0))
        in_specs += [row_spec, b_spec]
        args += [o, bias]
        out_specs.append(b_spec)
        out_shape.append(jax.ShapeDtypeStruct((8, NA_ROWS, GRID_W, NA_ROWS * GRID_W), F32))
    else:
        in_specs.append(row_spec)
        args.append(o)
    return pl.pallas_call(
        body, name=name, grid=(cfg["units"], nq), in_specs=in_specs, out_specs=out_specs, out_shape=out_shape,
        scratch_shapes=[pltpu.VMEM((lk, 128), F32), pltpu.VMEM((lk, 128), F32)],
        compiler_params=_params(("parallel", "arbitrary")))(*args)


def _na_geometry():
    qc = _iota((GRID_W, 128), 0)
    lane = _iota((GRID_W, 128), 1)
    kc = lane & 63
    c_start = jnp.clip(qc - 8, 0, GRID_W - 16)
    valid = jnp.logical_and(kc >= c_start, kc < c_start + 16)
    return lane, valid


def _na_bias(rpb_rows):
    def body(r_ref, o_ref, t_ref):
        lane, valid = _na_geometry()
        for dd in range(14):
            row_a = jnp.broadcast_to(r_ref[dd:dd + 1, :], (GRID_W, 128))
            row_b = jnp.broadcast_to(r_ref[dd + 1:dd + 2, :], (GRID_W, 128))
            both = jnp.where(lane < 64, row_a, pltpu.roll(row_b, 64, 1))
            t = pltpu.roll(both, 128 - 15, 1, stride=1, stride_axis=0)
            t_ref[dd] = jnp.where(valid, t, NEG)
        for off in range(NA_ROWS):
            for p in range(4):
                o_ref[off, :, p * 128:(p + 1) * 128] = t_ref[2 * p - off + 7]

    return pl.pallas_call(
        body, name="na_bias", grid=(8,),
        in_specs=[pl.BlockSpec((None, 16, 128), lambda h: (h, 0, 0))],
        out_specs=pl.BlockSpec((None, NA_ROWS, GRID_W, NA_ROWS * GRID_W), lambda h: (h, 0, 0, 0)),
        out_shape=jax.ShapeDtypeStruct((8, NA_ROWS, GRID_W, NA_ROWS * GRID_W), F32),
        scratch_shapes=[pltpu.VMEM((14, GRID_W, 128), F32)],
        compiler_params=_params(("parallel",)))(rpb_rows)


def _na_bias_bwd(dbias, dep=None):
    dep_specs, dep_args = _dep_operand(dep)

    def body(d_ref, *rest):
        o_ref = rest[-1]
        lane, valid = _na_geometry()
        reverse = (_iota((GRID_W, GRID_W), 0) + _iota((GRID_W, GRID_W), 1) == GRID_W - 1).astype(F32)
        o_ref[...] = jnp.zeros((16, 128), F32)
        for dd in range(14):
            t = jnp.zeros((GRID_W, 128), F32)
            for off in range(NA_ROWS):
                for p in range(4):
                    if 2 * p - off + 7 == dd:
                        t = t + d_ref[off, :, p * 128:(p + 1) * 128]
            t = jnp.dot(reverse, jnp.where(valid, t, 0.0), precision=lax.Precision.HIGHEST,
                        preferred_element_type=F32)
            t = pltpu.roll(t, 128 - (GRID_W - 16), 1, stride=1, stride_axis=0)
            o_ref[dd:dd + 1, :] = jnp.sum(t, axis=0, keepdims=True)

    return pl.pallas_call(
        body, name="na_bias_bwd", grid=(8,),
        in_specs=[pl.BlockSpec((None, NA_ROWS, GRID_W, NA_ROWS * GRID_W), lambda h: (h, 0, 0, 0))] + dep_specs,
        out_specs=pl.BlockSpec((None, 16, 128), lambda h: (h, 0, 0)),
        out_shape=jax.ShapeDtypeStruct((8, 16, 128), F32),
        compiler_params=_params(("parallel",)))(dbias, *dep_args)


GATE_ROWS = 128


def _group_weights(l0, l1, l2):
    m = jnp.maximum(jnp.maximum(l0, l1), l2)
    e0, e1, e2 = jnp.exp(l0 - m), jnp.exp(l1 - m), jnp.exp(l2 - m)
    inv = 1.0 / (e0 + e1 + e2)
    return e0 * inv, e1 * inv, e2 * inv


def _gate_block(o_grp, l_grp, out_b, out_c, parts, x, target, merge_bias, wts, w_out, gain, head_sum):
    rows = GATE_ROWS
    r512 = pl.BlockSpec((rows, 512), lambda i: (i, 0))
    r1024 = pl.BlockSpec((rows, D_MODEL), lambda i: (i, 0))
    silu_cols = [pl.BlockSpec((rows, 512), functools.partial(lambda b, i: (i, b), 13 + b)) for b in range(3)]
    logit_cols = [pl.BlockSpec((rows, D_MODEL), functools.partial(lambda b, i: (i, b), 8 + b)) for b in range(3)]

    def body(o0, o1, o2, l0, l1, l2, ob, oc, ga, gb, gc, la, lb, lc, x_ref, t_ref, mb, wa, wb, wc, wo_ref, gn_ref,
             hs_ref, ua, ub, uc, y_ref, dy2_ref, dout_ref, dla, dlb, dlc, dza, dzb, dzc, dga, dgb, dgc,
             do0, do1, do2, dp0, dp1, dp2, dob, doc, err_ref, gg_ref, gmb):
        ws = _group_weights(l0[...], l1[...], l2[...])
        out_a = ws[0] * o0[...] + ws[1] * o1[...] + ws[2] * o2[...]
        branches = ((out_a, ga, la, wa, ua, dla, dza, dga), (ob[...], gb, lb, wb, ub, dlb, dzb, dgb),
                    (oc[...], gc, lc, wc, uc, dlc, dzc, dgc))
        y = jnp.zeros((rows, D_MODEL), F32)
        zs, gates, silus, dsilus = [], [], [], []
        for b, (ov, g_ref, l_ref, w_ref, u_ref, _, _, _) in enumerate(branches):
            g = g_ref[...].astype(F32)
            sg = _sigmoid(g)
            silus.append(g * sg)
            dsilus.append(sg * (1.0 + g * (1.0 - sg)))
            u = (ov * silus[b]).astype(BF16)
            u_ref[...] = u
            zs.append(lax.dot_general(u, w_ref[...], NT, preferred_element_type=F32))
            gates.append(_sigmoid(l_ref[...].astype(F32) + mb[b:b + 1, :]))
            y = y + gates[b] * zs[b]
        yb = y.astype(BF16)
        y_ref[...] = yb
        y2 = jnp.dot(yb, wo_ref[...], preferred_element_type=F32)
        rstd = lax.rsqrt(jnp.mean(y2 * y2, axis=1, keepdims=True) + EPS)
        yn = y2 * rstd
        gv = gn_ref[...]
        err = x_ref[...] + yn * gv - t_ref[...]
        dout = err * (1.0 / D_MODEL)
        dout_ref[...] = dout
        dn = dout * gv
        dy2 = (rstd * (dn - yn * jnp.mean(dn * yn, axis=1, keepdims=True))).astype(BF16)
        dy2_ref[...] = dy2

        @pl.when(pl.program_id(0) == 0)
        def _():
            err_ref[...] = jnp.zeros((1, D_MODEL), F32)
            gg_ref[...] = jnp.zeros((1, D_MODEL), F32)
            gmb[...] = jnp.zeros((3, D_MODEL), F32)

        err_ref[...] += jnp.sum(err * err, axis=0, keepdims=True)
        gg_ref[...] += jnp.sum(dout * yn, axis=0, keepdims=True)
        dy = lax.dot_general(dy2, wo_ref[...], NT, preferred_element_type=F32)
        dos = []
        for b, (ov, _, _, w_ref, _, dl_ref, dz_ref, dg_ref) in enumerate(branches):
            dl = dy * zs[b] * gates[b] * (1.0 - gates[b])
            dl_ref[...] = dl.astype(BF16)
            gmb[b:b + 1, :] += jnp.sum(dl, axis=0, keepdims=True)
            dz = (dy * gates[b]).astype(BF16)
            dz_ref[...] = dz
            du = jnp.dot(dz, w_ref[...], preferred_element_type=F32)
            dos.append(du * silus[b])
            dg_ref[...] = (du * ov * dsilus[b]).astype(BF16)
        dob[...] = dos[1].astype(BF16)
        doc[...] = dos[2].astype(BF16)
        row_term = jnp.dot(dos[0] * out_a, hs_ref[...], precision=lax.Precision.HIGHEST, preferred_element_type=F32)
        for wg, do_ref, dp_ref in zip(ws, (do0, do1, do2), (dp0, dp1, dp2)):
            do_ref[...] = wg * dos[0]
            dp_ref[...] = wg * row_term

    full = lambda shape: pl.BlockSpec(shape, lambda i: (0,) * len(shape))
    vec = pl.BlockSpec((1, D_MODEL), lambda i: (0, 0))
    acc3 = pl.BlockSpec((3, D_MODEL), lambda i: (0, 0))
    in_specs = ([r512] * 8 + silu_cols + logit_cols + [r1024, r1024, full((3, D_MODEL))]
                + [full((D_MODEL, 512))] * 3 + [full((D_MODEL, D_MODEL)), vec, full((512, 512))])
    out_specs = [r512] * 3 + [r1024] * 3 + [r1024] * 6 + [r512] * 3 + [r512] * 6 + [r512] * 2 + [vec, vec, acc3]
    bf, f32 = BF16, F32
    sds = jax.ShapeDtypeStruct
    out_shape = ([sds((SEQ, 512), bf)] * 3 + [sds((SEQ, D_MODEL), bf)] * 2 + [sds((SEQ, D_MODEL), f32)]
                 + [sds((SEQ, D_MODEL), bf)] * 6 + [sds((SEQ, 512), bf)] * 3 + [sds((SEQ, 512), f32)] * 6
                 + [sds((SEQ, 512), bf)] * 2 + [sds((1, D_MODEL), f32)] * 2 + [sds((3, D_MODEL), f32)])
    res = pl.pallas_call(
        body, name="gate_block", grid=(SEQ // rows,), in_specs=in_specs, out_specs=out_specs, out_shape=out_shape,
        compiler_params=_params(("arbitrary",)))(
            *o_grp, *l_grp, out_b, out_c, parts, parts, parts, parts, parts, parts, x, target, merge_bias, *wts, w_out,
            gain, head_sum)
    return dict(u=res[0:3], y=res[3], dy2=res[4], dout=res[5], dlog=res[6:9], dz=res[9:12], dg=res[12:15],
                do_grp=res[15:18], dp_grp=res[18:21], do_b=res[21], do_c=res[22], err_sq=res[23], g_post=res[24],
                g_mb=res[25])


def _local_step(x, mem, target, pre_norm, mem_norm, post_norm, na_rpb, wt_in, late_weights, dep_in=None,
                reduce_start=None):
    tabs = _rope_tables()
    hs, hst = _prenorm_fold(x, pre_norm)
    parts = _in_proj(hs, wt_in, tabs, dep_in)

    o_grp, l_grp = [], []
    for g, d in enumerate(DILATIONS):
        o, l = _attn_fwd("dil_fwd_%d" % g, "dil", parts, parts, parts, 12 * g, 12 * g + 4, 12 * g + 8, d=d)
        o_grp.append(o)
        l_grp.append(l)
    bias = _na_bias(jnp.pad(na_rpb, ((0, 0), (0, 1), (0, 128 - 31))))
    out_b, lse_b = _attn_fwd("na_fwd", "na", parts, parts, parts, 36, 40, 44, bias=bias)
    merge_bias, w_kv, wt_a, wt_b, wt_c, w_out = late_weights(out_b)
    memn = _rmsnorm_fwd("memnorm", mem, mem_norm, MEM_LEN)
    kv_m = _mm_simple("mem_kv", memn, w_kv, NN, BF16, MEM_LEN, 512, D_MODEL)
    out_c, lse_c = _attn_fwd("mem_fwd", "mem", parts, kv_m, kv_m, 48, 0, 4)

    rr = _iota((512, 512), 0) // HEAD_DIM
    cc = _iota((512, 512), 1) // HEAD_DIM
    head_sum = (rr == cc).astype(F32)
    gb = _gate_block(o_grp, l_grp, out_b, out_c, parts, x, target, merge_bias, (wt_a, wt_b, wt_c), w_out, post_norm,
                     head_sum)
    u, y, dy2, dout, dlog, dz, dg = gb["u"], gb["y"], gb["dy2"], gb["dout"], gb["dlog"], gb["dz"], gb["dg"]
    do_grp, dp_grp, do_b, do_c, g_post, g_mb = (gb["do_grp"], gb["dp_grp"], gb["do_b"], gb["do_c"], gb["g_post"],
                                                gb["g_mb"])
    loss = 0.5 * jnp.sum(gb["err_sq"]) / D_MODEL
    g_w_out = _mm_simple("out_proj_dw", y, dy2, TN, BF16, D_MODEL, 512, 512)
    g_wt = [_mm_simple("branch_dw_%d" % b, dz[b], u[b], TN, BF16, D_MODEL, 512, 512) for b in range(3)]

    dqkv = []
    for g, d in enumerate(DILATIONS):
        dq, dk, dv = _attn_bwd("dil_bwd_%d" % g, "dil", parts, parts, parts, 12 * g, 12 * g + 4, 12 * g + 8,
                               do_grp[g], l_grp[g], dp=dp_grp[g], d=d, tabs=tabs[g])
        dqkv += [dq, dk, dv]
    dq_b, dk_b, dv_b, dbias = _attn_bwd("na_bwd", "na", parts, parts, parts, 36, 40, 44, do_b, lse_b, o=out_b,
                                        bias=bias)
    dq_c, dk_m, dv_m = _attn_bwd("mem_bwd", "mem", parts, kv_m, kv_m, 48, 0, 4, do_c, lse_c, o=out_c)

    dkv = jnp.concatenate([dk_m, dv_m], axis=1).astype(BF16)
    g_w_kv = _mm_simple("mem_kv_dw", memn, dkv, TN, BF16, D_MODEL, 512, MEM_LEN)
    dmemn = _mm_simple("mem_kv_dx", dkv, w_kv, NT, F32, MEM_LEN, 512, D_MODEL)

    grads = dict(w_kv=g_w_kv, wt_a=g_wt[0], wt_b=g_wt[1], wt_c=g_wt[2], w_out=g_w_out, merge_bias=g_mb,
                 post_norm=g_post)
    dep = reduce_start(grads) if reduce_start is not None else None
    dparts = dqkv + [dq_b, dk_b, dv_b, dq_c] + list(dg) + list(dlog)
    grads["wt_in"] = _in_proj_dw(dparts, hst, dep)
    dep = reduce_start(grads) if reduce_start is not None else None
    dh = _in_proj_dh(dparts, wt_in, dep)
    grad_x, grads["pre_norm"] = _prenorm_bwd(x, pre_norm, dh, dout)
    g_rpb_t = _na_bias_bwd(dbias, dep)
    grads["na_rpb"] = g_rpb_t[:, :15, :31] + jnp.pad(g_rpb_t[:, :14, 64:95], ((0, 0), (1, 0), (0, 0)))
    grads["mem_norm"] = _memnorm_bwd(mem, dmemn, dep)
    return loss, grad_x, grads


ANY = pl.BlockSpec(memory_space=pl.ANY)


def _place():
    return lax.axis_index("x"), lax.axis_index("y"), lax.axis_index("c")


def _all_gather(shard):
    r = shard.shape[0]
    half = r // 2

    def body(src, out, send_sems, recv_sems, local_sem):
        x, y, c = _place()
        me, sib = (x, y, c), (x, y, 1 - c)
        xn, yn, dg = (1 - x, y, c), (x, 1 - y, c), (1 - x, 1 - y, c)

        def rows(dev, part=None):
            blk = out.at[4 * dev[0] + 2 * dev[1] + dev[2]]
            return blk if part is None else blk.at[pl.ds(part * half, half)]

        def copy(k, dev, part, to, own=False):
            return pltpu.make_async_remote_copy(
                src_ref=src if own else rows(dev, part), dst_ref=rows(dev, part),
                send_sem=send_sems.at[k], recv_sem=recv_sems.at[k], device_id=to, device_id_type=MESH_ID)

        def other(dev):
            return (dev[0], dev[1], 1 - dev[2])

        mine = pltpu.make_async_copy(src, rows(me), local_sem)
        mine.start()
        sent = [copy(0, me, None, sib, own=True), copy(1, me, None, xn, own=True), copy(2, me, None, yn, own=True)]
        for cp in sent:
            cp.start()
        copy(1, xn, None, me).wait_recv()
        sent += [copy(3, xn, 0, yn), copy(5, xn, None, sib)]
        sent[-2].start()
        sent[-1].start()
        copy(2, yn, None, me).wait_recv()
        sent += [copy(4, yn, 1, xn), copy(6, yn, None, sib)]
        sent[-2].start()
        sent[-1].start()
        copy(3, dg, 0, me).wait_recv()
        sent.append(copy(7, dg, 0, sib))
        sent[-1].start()
        copy(4, dg, 1, me).wait_recv()
        sent.append(copy(8, dg, 1, sib))
        sent[-1].start()
        copy(0, sib, None, me).wait_recv()
        copy(5, other(xn), None, me).wait_recv()
        copy(6, other(yn), None, me).wait_recv()
        copy(7, other(dg), 0, me).wait_recv()
        copy(8, other(dg), 1, me).wait_recv()
        for cp in sent:
            cp.wait_send()
        mine.wait()

    return pl.pallas_call(
        body, name="all_gather", in_specs=[ANY], out_specs=ANY,
        out_shape=jax.ShapeDtypeStruct((N_DEV,) + shard.shape, shard.dtype),
        scratch_shapes=[pltpu.SemaphoreType.DMA((9,)), pltpu.SemaphoreType.DMA((9,)), pltpu.SemaphoreType.DMA])(shard)


def _exchange_sibling(name, terms):
    nt = len(terms)

    def body(*refs):
        srcs, outs = refs[:nt], refs[nt:2 * nt]
        send_sems, recv_sems = refs[2 * nt:]
        x, y, c = _place()
        copies = []
        for q in range(4):
            for t in range(nt):
                copies.append(pltpu.make_async_remote_copy(
                    src_ref=srcs[t].at[2 * q + 1 - c], dst_ref=outs[t].at[q],
                    send_sem=send_sems.at[q * nt + t], recv_sem=recv_sems.at[q * nt + t],
                    device_id=(x, y, 1 - c), device_id_type=MESH_ID))
        for cp in copies:
            cp.start()
        for cp in copies:
            cp.wait()

    return pl.pallas_call(
        body, name=name, in_specs=[ANY] * nt, out_specs=[ANY] * nt,
        out_shape=[jax.ShapeDtypeStruct((4,) + s.shape[1:], s.dtype) for s in terms],
        scratch_shapes=[pltpu.SemaphoreType.DMA((4 * nt,)), pltpu.SemaphoreType.DMA((4 * nt,))])(*terms)


HBM = pl.BlockSpec(memory_space=pltpu.HBM)
SEM = pl.BlockSpec(memory_space=pltpu.SEMAPHORE)
DATAFLOW = pltpu.SideEffectType.DATAFLOW_SIDE_EFFECTING


def _split_copies(kind, srcs, lands, send_sems, recv_sems):
    nt = len(srcs)
    x, y, c = _place()
    copies = []
    if kind == "gather":
        me = 4 * x + 2 * y + c
        for mask in range(1, 8):
            fx, fy, fc = (mask >> 2) & 1, (mask >> 1) & 1, mask & 1
            to = (1 - x if fx else x, 1 - y if fy else y, 1 - c if fc else c)
            for t in range(nt):
                k = (mask - 1) * nt + t
                copies.append(pltpu.make_async_remote_copy(
                    src_ref=srcs[t], dst_ref=lands[t].at[me], send_sem=send_sems.at[k], recv_sem=recv_sems.at[k],
                    device_id=to, device_id_type=MESH_ID))
    else:
        for s, (tx, ty) in enumerate([(1 - x, y), (x, 1 - y), (1 - x, 1 - y)]):
            for t in range(nt):
                k = s * nt + t
                copies.append(pltpu.make_async_remote_copy(
                    src_ref=srcs[t].at[2 * tx + ty], dst_ref=lands[t].at[s], send_sem=send_sems.at[k],
                    recv_sem=recv_sems.at[k], device_id=(tx, ty, c), device_id_type=MESH_ID))
    return copies


def _split_count(kind, nt):
    return (7 if kind == "gather" else 3) * nt


def _exchange_start(name, kind, srcs, land_shapes, after=None):
    nt = len(srcs)
    n = _split_count(kind, nt)
    dep_specs, dep_args = _dep_operand(after)
    nd = len(dep_args)

    def body(*refs):
        src_refs, land_refs = refs[:nt], refs[nt:2 * nt]
        send_sems, recv_sems = refs[2 * nt + nd], refs[2 * nt + nd + 1]
        token = refs[-1]
        for cp in _split_copies(kind, src_refs, land_refs, send_sems, recv_sems):
            cp.start()
        token[...] = jnp.zeros_like(token)

    lands = [pltpu.with_memory_space_constraint(lax.empty(s.shape, s.dtype), pltpu.HBM) for s in land_shapes]
    res = pl.pallas_call(
        body, name=name,
        out_shape=(pltpu.SemaphoreType.DMA((n,)), pltpu.SemaphoreType.DMA((n,)),
                   *[pltpu.HBM(s.shape, s.dtype) for s in srcs], *[pltpu.HBM(s.shape, s.dtype) for s in land_shapes],
                   jax.ShapeDtypeStruct((8, 128), F32)),
        in_specs=[HBM] * (2 * nt) + dep_specs,
        out_specs=(SEM, SEM, *([HBM] * (2 * nt)), pl.BlockSpec(memory_space=pltpu.VMEM)),
        input_output_aliases={i: 2 + i for i in range(2 * nt)},
        compiler_params=pltpu.CompilerParams(has_side_effects=DATAFLOW))(
            *[pltpu.with_memory_space_constraint(s, pltpu.HBM) for s in srcs], *lands, *dep_args)
    return res[0], res[1], list(res[2:2 + nt]), list(res[2 + nt:2 + 2 * nt]), res[-1]


def _exchange_wait(name, kind, send_sems, recv_sems, srcs, lands, after):
    nt = len(srcs)

    def body(*refs):
        src_refs, land_refs = refs[:nt], refs[nt:2 * nt]
        s_sems, r_sems = refs[2 * nt], refs[2 * nt + 1]
        for cp in _split_copies(kind, src_refs, land_refs, s_sems, r_sems):
            cp.wait_send()
            cp.wait_recv()

    res = pl.pallas_call(
        body, name=name,
        out_shape=tuple(pltpu.HBM(s.shape, s.dtype) for s in list(srcs) + list(lands)),
        in_specs=[HBM] * (2 * nt) + [SEM, SEM, pl.BlockSpec(memory_space=pl.ANY)],
        out_specs=tuple([HBM] * (2 * nt)),
        input_output_aliases={i: i for i in range(2 * nt)},
        compiler_params=pltpu.CompilerParams(has_side_effects=DATAFLOW))(
            *srcs, *lands, send_sems, recv_sems, after)
    return list(res[:nt]), list(res[nt:])


def _add_sibling(name, term, recv, rows):
    _, r, w = term.shape
    cidx = lax.axis_index("c").astype(jnp.int32).reshape(1)
    like_term = recv.shape[0] == N_DEV

    def body(c_ref, a_ref, b_ref, o_ref):
        o_ref[...] = (a_ref[...].astype(F32) + b_ref[...].astype(F32)).astype(o_ref.dtype)

    grid_spec = pltpu.PrefetchScalarGridSpec(
        num_scalar_prefetch=1, grid=(4, r // rows),
        in_specs=[pl.BlockSpec((None, rows, w), lambda q, i, c_ref: (2 * q + c_ref[0], i, 0)),
                  pl.BlockSpec((None, rows, w), lambda q, i, c_ref: (2 * q + c_ref[0] if like_term else q, i, 0))],
        out_specs=pl.BlockSpec((None, rows, w), lambda q, i, c_ref: (q, i, 0)))
    return pl.pallas_call(
        body, name=name, grid_spec=grid_spec, out_shape=jax.ShapeDtypeStruct((4, r, w), term.dtype),
        compiler_params=_params(("parallel", "parallel")))(cidx, term, recv)


def _add_sibling_small(name, terms, recvs):
    nt = len(terms)

    def body(*refs):
        c = lax.axis_index("c")
        for t_ref, r_ref, o_ref in zip(refs[:nt], refs[nt:2 * nt], refs[2 * nt:]):
            for q in range(4):
                o_ref[q] = (t_ref[2 * q + c].astype(F32) + r_ref[q].astype(F32)).astype(o_ref.dtype)

    return pl.pallas_call(
        body, name=name, out_shape=[jax.ShapeDtypeStruct((4,) + t.shape[1:], t.dtype) for t in terms],
        compiler_params=_params())(*terms, *recvs)


def _add_chips(name, sums, recv, rows):
    _, r, w = sums.shape
    qidx = (2 * lax.axis_index("x") + lax.axis_index("y")).astype(jnp.int32).reshape(1)

    def body(q_ref, a_ref, b_ref, o_ref):
        o_ref[...] = ((a_ref[...].astype(F32) + b_ref[0].astype(F32))
                      + (b_ref[1].astype(F32) + b_ref[2].astype(F32)))

    grid_spec = pltpu.PrefetchScalarGridSpec(
        num_scalar_prefetch=1, grid=(r // rows,),
        in_specs=[pl.BlockSpec((None, rows, w), lambda i, q_ref: (q_ref[0], i, 0)),
                  pl.BlockSpec((3, rows, w), lambda i, q_ref: (0, i, 0))],
        out_specs=pl.BlockSpec((rows, w), lambda i, q_ref: (i, 0)))
    return pl.pallas_call(
        body, name=name, grid_spec=grid_spec, out_shape=jax.ShapeDtypeStruct((r, w), F32),
        compiler_params=_params(("parallel",)))(qidx, sums, recv)


def _rs_rows(a):
    return SHARD_IN // 4 if a.shape[1] == SHARD_IN else a.shape[1]


def _reduce_scatter_start(tag, names, terms, recv1=None):
    if recv1 is None:
        recv1 = _exchange_sibling("exchange_sibling_" + tag, terms)
    if len(terms) == 1:
        sums = [_add_sibling("add_sibling_" + names[0], terms[0], recv1[0], _rs_rows(terms[0]))]
    else:
        sums = _add_sibling_small("add_sibling_" + tag, terms, recv1)
    lands =[jax.ShapeDtypeStruct((3,) + s.shape[1:], s.dtype) for s in sums]
    send_sems, recv_sems, sums, lands, token = _exchange_start("exchange_chips_start_" + tag, "chips", sums, lands)
    return (tag, names, send_sems, recv_sems, sums, lands), token


def _reduce_scatter_wait(state, after):
    tag, names, send_sems, recv_sems, sums, lands = state
    sums, recv2 = _exchange_wait("exchange_chips_wait_" + tag, "chips", send_sems, recv_sems, sums, lands, after)
    return names, sums, recv2


def _adamw(name, w, g, m, v, dep=None):
    dep_specs, dep_args = _dep_operand(dep)

    def body(w_ref, g_ref, m_ref, v_ref, *rest):
        d_ref, nm_ref, nv_ref = rest[-3:]
        d_ref[...], nm_ref[...], nv_ref[...] = _adam_math(w_ref[...], g_ref[...], m_ref[...], v_ref[...])

    whole = pl.BlockSpec(memory_space=pltpu.VMEM)
    return pl.pallas_call(
        body, name=name, in_specs=[whole] * 4 + dep_specs, out_shape=[jax.ShapeDtypeStruct(w.shape, F32)] * 3,
        compiler_params=_params())(w, g, m, v, *dep_args)


def _adam_math(w, g, m, v):
    nm = ADAM_B1 * m + (1.0 - ADAM_B1) * g
    nv = ADAM_B2 * v + (1.0 - ADAM_B2) * (g * g)
    c1 = 1.0 - ADAM_B1 ** ADAM_STEP
    c2 = 1.0 - ADAM_B2 ** ADAM_STEP
    return -ADAM_LR * ((nm / c1) / (jnp.sqrt(nv / c2) + ADAM_EPS) + ADAM_WD * w), nm, nv


def _adamw_chips(name, sums, recv, w, m, v, transposed, rows=None, dep=None):
    r, c = w.shape
    rows = r if rows is None else rows
    qidx = (2 * lax.axis_index("x") + lax.axis_index("y")).astype(jnp.int32).reshape(1)
    dep_specs, dep_args = _dep_operand(dep)

    def body(q_ref, a_ref, b_ref, w_ref, m_ref, v_ref, *rest):
        g_ref, d_ref, nm_ref, nv_ref = rest[-4:]
        g = (a_ref[...].astype(F32) + b_ref[0].astype(F32)) + (b_ref[1].astype(F32) + b_ref[2].astype(F32))
        if transposed:
            g = g.T
        g_ref[...] = g
        d_ref[...], nm_ref[...], nv_ref[...] = _adam_math(w_ref[...], g, m_ref[...], v_ref[...])

    row = pl.BlockSpec((rows, c), lambda i, q_ref: (i, 0))
    if transposed:
        term_specs = [pl.BlockSpec((None, c, rows), lambda i, q_ref: (q_ref[0], 0, i)),
                      pl.BlockSpec((3, c, rows), lambda i, q_ref: (0, 0, i))]
    else:
        term_specs = [pl.BlockSpec((None, rows, c), lambda i, q_ref: (q_ref[0], i, 0)),
                      pl.BlockSpec((3, rows, c), lambda i, q_ref: (0, i, 0))]
    grid_spec = pltpu.PrefetchScalarGridSpec(
        num_scalar_prefetch=1, grid=(r // rows,), in_specs=term_specs + [row, row, row] + dep_specs,
        out_specs=[row] * 4)
    return pl.pallas_call(
        body, name=name, grid_spec=grid_spec, out_shape=[jax.ShapeDtypeStruct((r, c), F32)] * 4,
        compiler_params=_params(("parallel",)))(qidx, sums, recv, w, m, v, *dep_args)


def _sum_devices(gathered):
    def body(g_ref, o_ref):
        acc = g_ref[0]
        for j in range(1, N_DEV):
            acc = acc + g_ref[j]
        o_ref[...] = acc

    return pl.pallas_call(
        body, name="sum_devices", out_shape=jax.ShapeDtypeStruct(gathered.shape[1:], F32),
        compiler_params=_params())(gathered)


def _rows128(a, rows):
    flat = a.reshape(-1)
    return jnp.pad(flat, (0, rows * 128 - flat.shape[0])).reshape(rows, 128)


def kernel(x, mem, pre_norm, w_in, merge_bias, na_rpb, mem_norm, w_mem_kv, w_branch_a, w_branch_b, w_branch_c, w_out, post_norm, loss_target, m_pre_norm, m_w_in, m_merge_bias, m_na_rpb, m_mem_norm, m_w_mem_kv, m_w_branch_a, m_w_branch_b, m_w_branch_c, m_w_out, m_post_norm, v_pre_norm, v_w_in, v_merge_bias, v_na_rpb, v_mem_norm, v_w_mem_kv, v_w_branch_a, v_w_branch_b, v_w_branch_c, v_w_out, v_post_norm):
    wt_in_s = w_in[0].T.astype(BF16)
    rows_s = jnp.concatenate([w_mem_kv[0], w_out[0]], axis=0).astype(BF16)
    cols_s = jnp.concatenate([w_branch_a[0].T, w_branch_b[0].T, w_branch_c[0].T], axis=0).astype(BF16)
    mb_s = jnp.pad(merge_bias[0], ((0, 5), (0, 0)))
    wt_in = _all_gather(wt_in_s).reshape(N_IN, D_MODEL)

    late_own = [rows_s, cols_s, mb_s]
    late_lands = [jax.ShapeDtypeStruct((N_DEV,) + s.shape, s.dtype) for s in late_own]
    l_send, l_recv, late_own, late_lands, late_token = _exchange_start("gather_late_start", "gather", late_own,
                                                                       late_lands, after=wt_in)
    me = 4 * lax.axis_index("x") + 2 * lax.axis_index("y") + lax.axis_index("c")

    def late_weights(after):
        own, lands = _exchange_wait("gather_late_wait", "gather", l_send, l_recv, late_own, late_lands, after)
        g_rows, g_cols, g_mb = [lax.dynamic_update_slice(land, o[None], (me, 0, 0)) for land, o in zip(lands, own)]
        return (g_mb[:, :3].transpose(1, 0, 2).reshape(3, D_MODEL),
                g_rows[:, :128].reshape(D_MODEL, D_MODEL), g_cols[:, 0:128].reshape(D_MODEL, 512),
                g_cols[:, 128:256].reshape(D_MODEL, 512), g_cols[:, 256:384].reshape(D_MODEL, 512),
                g_rows[:, 128:].reshape(D_MODEL, D_MODEL))

    rs_state = []

    def reduce_start(grads):
        if "wt_in" in grads:
            own, sibling = [a.reshape(N_DEV, SHARD_IN, D_MODEL) for a in grads["wt_in"]]
            state, token = _reduce_scatter_start("w_in", ["w_in"], [own], [sibling])
        else:
            gmb_t = jnp.pad(grads["merge_bias"].reshape(3, N_DEV, 128).transpose(1, 0, 2), ((0, 0), (0, 5), (0, 0)))
            names = ["w_kv", "w_out", "a", "b", "c", "mb"]
            terms = [grads["w_kv"].reshape(N_DEV, 128, D_MODEL), grads["w_out"].reshape(N_DEV, 128, D_MODEL),
                     grads["wt_a"].reshape(N_DEV, 128, 512), grads["wt_b"].reshape(N_DEV, 128, 512),
                     grads["wt_c"].reshape(N_DEV, 128, 512), gmb_t]
            state, token = _reduce_scatter_start("rest", names, terms)
        rs_state.append(state)
        return token

    loss_term, grad_x, grads = _local_step(
        x[0], mem[0], loss_target[0], pre_norm, mem_norm, post_norm, na_rpb[0], wt_in, late_weights,
        dep_in=late_token, reduce_start=reduce_start)

    small = jnp.concatenate([_rows128(grads["pre_norm"], 8), _rows128(grads["mem_norm"], 8),
                             _rows128(grads["post_norm"], 8), _rows128(grads["na_rpb"], 32),
                             _rows128(loss_term, 8)], axis=0)
    s_send, s_recv, s_own, s_land, s_token = _exchange_start(
        "gather_small_start", "gather", [small], [jax.ShapeDtypeStruct((N_DEV,) + small.shape, F32)])
    grad = {}
    weights = {
        "pre_norm": (pre_norm, m_pre_norm, v_pre_norm), "w_in": (w_in, m_w_in, v_w_in),
        "merge_bias": (merge_bias, m_merge_bias, v_merge_bias), "na_rpb": (na_rpb, m_na_rpb, v_na_rpb),
        "mem_norm": (mem_norm, m_mem_norm, v_mem_norm), "w_mem_kv": (w_mem_kv, m_w_mem_kv, v_w_mem_kv),
        "w_branch_a": (w_branch_a, m_w_branch_a, v_w_branch_a), "w_branch_b": (w_branch_b, m_w_branch_b, v_w_branch_b),
        "w_branch_c": (w_branch_c, m_w_branch_c, v_w_branch_c), "w_out": (w_out, m_w_out, v_w_out),
        "post_norm": (post_norm, m_post_norm, v_post_norm)}
    order = ["pre_norm", "w_in", "merge_bias", "na_rpb", "mem_norm", "w_mem_kv", "w_branch_a", "w_branch_b",
             "w_branch_c", "w_out", "post_norm"]
    delta, new_m, new_v = {}, {}, {}

    def update(n, dep=None):
        w, m, v = weights[n]
        shape = w.shape
        two_d = (-1, shape[-1])
        dl, nm, nv = _adamw("adamw_" + n, w.reshape(two_d), grad[n].reshape(two_d), m.reshape(two_d),
                            v.reshape(two_d), dep)
        delta[n], new_m[n], new_v[n] = dl.reshape(shape), nm.reshape(shape), nv.reshape(shape)
        return dl

    def update_sharded(n, sums, recv, transposed, rows=None, dep=None):
        w, m, v = weights[n]
        g, dl, nm, nv = _adamw_chips("adamw_" + n, sums, recv, w[0], m[0], v[0], transposed, rows, dep)
        grad[n], delta[n], new_m[n], new_v[n] = g[None], dl[None], nm[None], nv[None]
        return dl

    _, sums, recv2 = _reduce_scatter_wait(rs_state[0], s_token)
    dep = None
    for i, (n, transposed) in enumerate((("w_mem_kv", False), ("w_out", False), ("w_branch_a", True),
                                         ("w_branch_b", True), ("w_branch_c", True))):
        dep = update_sharded(n, sums[i], recv2[i], transposed, dep=dep)
    grad["merge_bias"] = _add_chips("add_chips_mb", sums[5], recv2[5], 8)[:3][None]
    update("merge_bias")
    s_own, s_land = _exchange_wait("gather_small_wait", "gather", s_send, s_recv, s_own, s_land, dep)
    total = _sum_devices(lax.dynamic_update_slice(s_land[0], s_own[0][None], (me, 0, 0)))
    loss = total[56, 0]
    grad.update({"pre_norm": total[0:8].reshape(1, D_MODEL), "mem_norm": total[8:16].reshape(1, D_MODEL),
                 "post_norm": total[16:24].reshape(1, D_MODEL),
                 "na_rpb": total[24:56].reshape(-1)[:8 * 15 * 31].reshape(1, 8, 15, 31)})
    dep = None
    for n in ("pre_norm", "na_rpb", "mem_norm", "post_norm"):
        dep = update(n, dep)
    _, sums_in, recv_in = _reduce_scatter_wait(rs_state[1], dep)
    update_sharded("w_in", sums_in[0], recv_in[0], True, 256)

    return (loss, grad_x[None], *[grad[n] for n in order], *[delta[n] for n in order],
            *[new_m[n] for n in order], *[new_v[n] for n in order])
```

```python
import functools

import numpy as np
import jax
import jax.numpy as jnp
from jax import lax
from jax.experimental import pallas as pl
from jax.experimental.pallas import tpu as pltpu

F32 = jnp.float32
BF16 = jnp.bfloat16

SEQ = 2048
D_MODEL = 1024
N_IN = 11264
N_DEV = 8
SHARD_IN = N_IN // N_DEV
HEAD_DIM = 64
GRID_W = 64
NA_ROWS = 8
MEM_LEN = 256
DILATIONS = (1, 4, 16)
REACH = 64
ROPE_THETA = 500000.0
ROPE_DIM = 16
EPS = 1e-6
NEG = -1e30
ADAM_LR = 0.001
ADAM_B1 = 0.9
ADAM_B2 = 0.999
ADAM_EPS = 1e-08
ADAM_WD = 0.01
ADAM_STEP = 10

VMEM_LIMIT_BYTES = 56 * 1024 * 1024
MESH_ID = pl.DeviceIdType.MESH

NN = (((1,), (0,)), ((), ()))
NT = (((1,), (1,)), ((), ()))
TN = (((0,), (0,)), ((), ()))


def _params(sem=None):
    return pltpu.CompilerParams(dimension_semantics=sem, vmem_limit_bytes=VMEM_LIMIT_BYTES)


def _iota(shape, dim):
    return lax.broadcasted_iota(jnp.int32, shape, dim)


def _sigmoid(x):
    return 1.0 / (1.0 + jnp.exp(-x))


def _rope_tables():
    half = ROPE_DIM // 2
    inv = (ROPE_THETA ** (-np.arange(half, dtype=np.float64) * 2.0 / ROPE_DIM)).astype(np.float32)
    pos = np.arange(SEQ, dtype=np.float32)
    ang = pos[:, None] * inv[None, :]
    cos, sin = np.cos(ang), np.sin(ang)
    zeros = np.zeros_like(cos)
    rest = HEAD_DIM - ROPE_DIM
    c64 = np.concatenate([cos, cos, np.ones((SEQ, rest), np.float32)], axis=1)
    s1 = np.concatenate([zeros, sin, np.zeros((SEQ, rest), np.float32)], axis=1)
    s2 = np.concatenate([-sin, zeros, np.zeros((SEQ, rest), np.float32)], axis=1)

    def fold(t, d):
        return t.reshape(SEQ // d, d, t.shape[1]).transpose(1, 0, 2).reshape(SEQ, t.shape[1])

    tabs = [np.stack([np.tile(fold(t, d), (1, 2)) for t in (c64, s1, s2)], axis=0) for d in DILATIONS]
    return jnp.asarray(np.stack(tabs, axis=0), dtype=F32)


def _rope(a, c, s1, s2):
    return a * c + pltpu.roll(a, 8, 1) * s1 + pltpu.roll(a, 120, 1) * s2


def _rope_t(a, c, s1, s2):
    return a * c + pltpu.roll(a * s1, 120, 1) + pltpu.roll(a * s2, 8, 1)


def _perm_of_block(j):
    return jnp.where(j < 3, 0, jnp.where(j < 6, 1, jnp.where(j < 9, 2, 0)))


def _mm(name, a, b, out_shape, out_dtype, grid, a_spec, b_spec, o_spec, acc_shape, dims, k_axis, nk):
    def body(a_ref, b_ref, o_ref, acc_ref):
        k = pl.program_id(k_axis)

        @pl.when(k == 0)
        def _():
            acc_ref[...] = jnp.zeros(acc_shape, F32)

        acc_ref[...] += lax.dot_general(a_ref[...], b_ref[...], dims, preferred_element_type=F32)

        @pl.when(k == nk - 1)
        def _():
            o_ref[...] = acc_ref[...].astype(out_dtype)

    sem = tuple("arbitrary" if ax == k_axis else "parallel" for ax in range(len(grid)))
    return pl.pallas_call(
        body, name=name, grid=grid, in_specs=[a_spec, b_spec], out_specs=o_spec,
        out_shape=jax.ShapeDtypeStruct(out_shape, out_dtype),
        scratch_shapes=[pltpu.VMEM(acc_shape, F32)], compiler_params=_params(sem))(a, b)


def _mm_simple(name, a, b, dims, out_dtype, tm, tn, tk):
    if dims is NN:
        m, kk = a.shape
        n = b.shape[1]
        a_spec = pl.BlockSpec((tm, tk), lambda i, j, k: (i, k))
        b_spec = pl.BlockSpec((tk, tn), lambda i, j, k: (k, j))
    elif dims is NT:
        m, kk = a.shape
        n = b.shape[0]
        a_spec = pl.BlockSpec((tm, tk), lambda i, j, k: (i, k))
        b_spec = pl.BlockSpec((tn, tk), lambda i, j, k: (j, k))
    else:
        kk, m = a.shape
        n = b.shape[1]
        a_spec = pl.BlockSpec((tk, tm), lambda i, j, k: (k, i))
        b_spec = pl.BlockSpec((tk, tn), lambda i, j, k: (k, j))
    grid = (m // tm, n // tn, kk // tk)
    o_spec = pl.BlockSpec((tm, tn), lambda i, j, k: (i, j))
    return _mm(name, a, b, (m, n), out_dtype, grid, a_spec, b_spec, o_spec, (tm, tn), dims, 2, kk // tk)


def _rmsnorm_fwd(name, x, gain, rows):
    n, d = x.shape

    def body(x_ref, g_ref, o_ref):
        xv = x_ref[...]
        rstd = lax.rsqrt(jnp.mean(xv * xv, axis=1, keepdims=True) + EPS)
        o_ref[...] = (xv * rstd * g_ref[...]).astype(BF16)

    return pl.pallas_call(
        body, name=name, grid=(n // rows,),
        in_specs=[pl.BlockSpec((rows, d), lambda i: (i, 0)), pl.BlockSpec((1, d), lambda i: (0, 0))],
        out_specs=pl.BlockSpec((rows, d), lambda i: (i, 0)),
        out_shape=jax.ShapeDtypeStruct((n, d), BF16), compiler_params=_params(("parallel",)))(x, gain)


def _folded_rows(first, rows, d):
    if d == 1:
        return pl.ds(pl.multiple_of(first, rows), rows)
    mlen = SEQ // d
    return pl.ds((first % mlen) * d + first // mlen, rows, stride=d)


def _prenorm_fold(x, gain):
    rows = 128

    nchunk = D_MODEL // 128

    def body(*refs):
        x_refs, g_ref, hs_ref, hst_ref = refs[:nchunk], refs[nchunk], refs[nchunk + 1], refs[nchunk + 2]
        first = pl.program_id(0) * rows
        for p, d in enumerate(DILATIONS):
            idx = _folded_rows(first, rows, d)
            xv = jnp.concatenate([r[idx, :] for r in x_refs], axis=1)
            rstd = lax.rsqrt(jnp.mean(xv * xv, axis=1, keepdims=True) + EPS)
            h = xv * rstd * g_ref[...]
            hs_ref[p] = h.astype(BF16)
            hst_ref[p] = h.T.astype(BF16)

    x_specs = [pl.BlockSpec((SEQ, 128), functools.partial(lambda c, i: (0, c), c)) for c in range(nchunk)]
    return pl.pallas_call(
        body, name="prenorm", grid=(SEQ // rows,),
        in_specs=x_specs + [pl.BlockSpec((1, D_MODEL), lambda i: (0, 0))],
        out_specs=[pl.BlockSpec((3, rows, D_MODEL), lambda i: (0, i, 0)),
                   pl.BlockSpec((3, D_MODEL, rows), lambda i: (0, 0, i))],
        out_shape=[jax.ShapeDtypeStruct((3, SEQ, D_MODEL), BF16), jax.ShapeDtypeStruct((3, D_MODEL, SEQ), BF16)],
        compiler_params=_params(("parallel",)))(*([x] * nchunk), gain)


def _prenorm_bwd(x, gain, dh, dout):
    rows = 256

    def body(x_ref, g_ref, a_ref, do_ref, dx_ref, gg_ref):
        xv = x_ref[...]
        rstd = lax.rsqrt(jnp.mean(xv * xv, axis=1, keepdims=True) + EPS)
        xn = xv * rstd
        dh = jnp.concatenate([a_ref[c] for c in range(D_MODEL // 128)], axis=1)
        gdh = dh * g_ref[...]
        dx_ref[...] = rstd * (gdh - xn * jnp.mean(gdh * xn, axis=1, keepdims=True)) + do_ref[...]

        @pl.when(pl.program_id(0) == 0)
        def _():
            gg_ref[...] = jnp.zeros((1, D_MODEL), F32)

        gg_ref[...] += jnp.sum(dh * xn, axis=0, keepdims=True)

    row = pl.BlockSpec((rows, D_MODEL), lambda i: (i, 0))
    vec = pl.BlockSpec((1, D_MODEL), lambda i: (0, 0))
    return pl.pallas_call(
        body, name="prenorm_bwd", grid=(SEQ // rows,),
        in_specs=[row, vec, pl.BlockSpec((D_MODEL // 128, rows, 128), lambda i: (0, i, 0)), row], out_specs=[row, vec],
        out_shape=[jax.ShapeDtypeStruct((SEQ, D_MODEL), F32), jax.ShapeDtypeStruct((1, D_MODEL), F32)],
        compiler_params=_params(("arbitrary",)))(x, gain, dh, dout)


def _memnorm_bwd(mem, dmemn, dep=None):
    dep_specs, dep_args = _dep_operand(dep)

    def body(m_ref, d_ref, *rest):
        mv = m_ref[...]
        rstd = lax.rsqrt(jnp.mean(mv * mv, axis=1, keepdims=True) + EPS)
        rest[-1][...] = jnp.sum(d_ref[...] * mv * rstd, axis=0, keepdims=True)

    whole = pl.BlockSpec(memory_space=pltpu.VMEM)
    return pl.pallas_call(
        body, name="memnorm_bwd", in_specs=[whole, whole] + dep_specs,
        out_shape=jax.ShapeDtypeStruct((1, D_MODEL), F32), compiler_params=_params())(mem, dmemn, *dep_args)


def _dep_operand(dep):
    return ([], []) if dep is None else ([pl.BlockSpec(memory_space=pl.ANY)], [dep])


def _in_proj(hs, wt, tabs, dep=None):
    tm, tn = 512, 512
    dep_specs, dep_args = _dep_operand(dep)

    def body(h_ref, w_ref, t_ref, *rest):
        o_ref = rest[-1]
        j = pl.program_id(0)
        is_rope = jnp.logical_and(j < 9, j % 3 != 2)
        row_slices = [slice(r * tm, (r + 1) * tm) for r in range(SEQ // tm)]

        def product(rs):
            return lax.dot_general(h_ref[rs, :], w_ref[...], NT, preferred_element_type=F32)

        @pl.when(is_rope)
        def _():
            for rs in row_slices:
                acc = product(rs)
                c, s1, s2 = t_ref[0, rs, :], t_ref[1, rs, :], t_ref[2, rs, :]
                for q in range(tn // 128):
                    a = acc[:, q * 128:(q + 1) * 128]
                    o_ref[rs, q * 128:(q + 1) * 128] = _rope(a, c, s1, s2).astype(BF16)

        @pl.when(jnp.logical_not(is_rope))
        def _():
            for rs in row_slices:
                o_ref[rs, :] = product(rs).astype(BF16)

    return pl.pallas_call(
        body, name="in_proj", grid=(N_IN // tn,),
        in_specs=[pl.BlockSpec((None, SEQ, D_MODEL), lambda j: (_perm_of_block(j), 0, 0)),
                  pl.BlockSpec((tn, D_MODEL), lambda j: (j, 0)),
                  pl.BlockSpec((None, 3, SEQ, 128), lambda j: (_perm_of_block(j), 0, 0, 0))] + dep_specs,
        out_specs=pl.BlockSpec((SEQ, tn), lambda j: (0, j)),
        out_shape=jax.ShapeDtypeStruct((SEQ, N_IN), BF16),
        compiler_params=_params(("parallel",)))(hs, wt, tabs, *dep_args)


def _piece_blocks(pieces):
    return [(a, h * 512) for a, p in enumerate(pieces) for h in range(p.shape[1] // 512)]


def _block_fetch(piece_refs, blocks, buf, sem):
    def start(block, slot):
        for b, (a, col) in enumerate(blocks):
            @pl.when(block == b)
            def _():
                pltpu.make_async_copy(piece_refs[a].at[:, pl.ds(col, 512)], buf.at[slot], sem.at[slot]).start()

    def wait(slot):
        pltpu.make_async_copy(piece_refs[0].at[:, pl.ds(0, 512)], buf.at[slot], sem.at[slot]).wait()

    return start, wait


def _in_proj_dw(pieces, hst, dep=None):
    tn = 512
    blocks = _piece_blocks(pieces)
    nblk = len(blocks)
    npc = len(pieces)
    dep_specs, dep_args = _dep_operand(dep)

    def body(h_ref, *rest):
        piece_refs = rest[:npc]
        o_ref, mirror, buf, sem, out_buf, send_sems, recv_sem = rest[-7:]
        j = pl.program_id(0)
        slot = j % 2
        start, wait = _block_fetch(piece_refs, blocks, buf, sem)
        x, y, c = _place()

        def to_sibling(step, slot_):
            return pltpu.make_async_remote_copy(
                src_ref=out_buf.at[slot_], dst_ref=mirror.at[pl.ds(pl.multiple_of(step * tn, tn), tn)],
                send_sem=send_sems.at[slot_], recv_sem=recv_sem, device_id=(x, y, 1 - c), device_id_type=MESH_ID)

        @pl.when(j == 0)
        def _():
            start(j, slot)

        wait(slot)

        @pl.when(j + 1 < nblk)
        def _():
            start(j + 1, 1 - slot)

        acc = jnp.dot(h_ref[...], buf[slot], preferred_element_type=F32)
        block = acc.T.astype(BF16)
        o_ref[...] = block

        @pl.when(j >= 2)
        def _():
            to_sibling(j - 2, slot).wait_send()

        out_buf[slot] = block
        to_sibling(j, slot).start()

        @pl.when(j == nblk - 1)
        def _():
            to_sibling(j - 1, 1 - slot).wait_send()
            to_sibling(j, slot).wait_send()
            pltpu.make_async_remote_copy(src_ref=mirror, dst_ref=mirror, send_sem=send_sems.at[0], recv_sem=recv_sem,
                                         device_id=(x, y, 1 - c), device_id_type=MESH_ID).wait_recv()

    return pl.pallas_call(
        body, name="in_proj_dw", grid=(nblk,),
        in_specs=[pl.BlockSpec((None, D_MODEL, SEQ), lambda j: (_perm_of_block(j), 0, 0))] + [ANY] * npc + dep_specs,
        out_specs=[pl.BlockSpec((tn, D_MODEL), lambda j: (j, 0)), ANY],
        out_shape=[jax.ShapeDtypeStruct((N_IN, D_MODEL), BF16), jax.ShapeDtypeStruct((N_IN, D_MODEL), BF16)],
        scratch_shapes=[pltpu.VMEM((2, SEQ, tn), BF16), pltpu.SemaphoreType.DMA((2,)),
                        pltpu.VMEM((2, tn, D_MODEL), BF16), pltpu.SemaphoreType.DMA((2,)), pltpu.SemaphoreType.DMA],
        compiler_params=_params(("arbitrary",)))(hst, *pieces, *dep_args)


def _in_proj_dh(pieces, wt, dep=None):
    tk = 512
    blocks = _piece_blocks(pieces)
    nblk = len(blocks)
    npc = len(pieces)
    nchunk = D_MODEL // 128

    def col(s):
        return jnp.where(s < 3, s, jnp.where(s < 16, s + 6, s - 13))

    dep_specs, dep_args = _dep_operand(dep)

    def body(w_ref, *rest):
        piece_refs = rest[:npc]
        o_ref, acc_ref, buf, sem = rest[-4:]
        s = pl.program_id(0)
        slot = s % 2
        start, wait = _block_fetch(piece_refs, blocks, buf, sem)

        @pl.when(s == 0)
        def _():
            start(col(s), slot)

        wait(slot)

        @pl.when(s + 1 < nblk)
        def _():
            start(col(s + 1), 1 - slot)

        row_slices = [slice(r * 512, (r + 1) * 512) for r in range(SEQ // 512)]

        def product(rs):
            return jnp.dot(buf[slot, rs, :], w_ref[...], preferred_element_type=F32)

        def accumulate(cond, to_out, init):
            @pl.when(cond)
            def _():
                for rs in row_slices:
                    prod = product(rs)
                    if not to_out:
                        if init:
                            acc_ref[rs, :] = prod
                        else:
                            acc_ref[rs, :] += prod
                        continue
                    for c in range(nchunk):
                        if init:
                            o_ref[c, rs, :] = prod[:, c * 128:(c + 1) * 128]
                        else:
                            o_ref[c, rs, :] += prod[:, c * 128:(c + 1) * 128]

        accumulate(s == 0, True, True)
        accumulate(jnp.logical_and(s > 0, s < 16), True, False)
        accumulate(jnp.logical_or(s == 16, s == 19), False, True)
        accumulate(jnp.logical_and(s > 16, s != 19), False, False)
        for last, d in ((18, 4), (21, 16)):
            @pl.when(s == last)
            def _():
                mlen = SEQ // d
                for r in range(d):
                    for c in range(nchunk):
                        o_ref[c, pl.ds(r, mlen, stride=d), :] += acc_ref[r * mlen:(r + 1) * mlen,
                                                                         c * 128:(c + 1) * 128]

    return pl.pallas_call(
        body, name="in_proj_dh", grid=(nblk,),
        in_specs=[pl.BlockSpec((tk, D_MODEL), lambda s: (col(s), 0))] + [ANY] * npc + dep_specs,
        out_specs=pl.BlockSpec((nchunk, SEQ, 128), lambda s: (0, 0, 0)),
        out_shape=jax.ShapeDtypeStruct((nchunk, SEQ, 128), F32),
        scratch_shapes=[pltpu.VMEM((SEQ, D_MODEL), F32), pltpu.VMEM((2, SEQ, tk), BF16),
                        pltpu.SemaphoreType.DMA((2,))],
        compiler_params=_params(("arbitrary",)))(wt, *pieces, *dep_args)


def _head_lanes(lanes, hh):
    return lanes >= 64 if hh == 1 else lanes < 64


def _head_rows(x, lanes, hh, pair):
    if not pair:
        return jnp.max(x, axis=1, keepdims=True)
    return jnp.max(jnp.where(_head_lanes(lanes, hh), x, -jnp.inf), axis=1, keepdims=True)


def _mask_head(x, lanes, hh, pair, scale=1.0):
    if not pair:
        return x
    xf = x.astype(F32) if scale == 1.0 else x.astype(F32) * scale
    return jnp.where(_head_lanes(lanes, hh), xf, 0.0).astype(BF16)


def _window(mode, qi, tq, mlen, tk):
    if mode == "dil":
        q0 = qi * tq
        seg = (q0 // mlen) * mlen
        ks = jnp.clip(q0 - REACH, seg, seg + mlen - tk)
        return pl.multiple_of(ks, 64)
    if mode == "na":
        r_start = jnp.clip(qi - NA_ROWS // 2, 0, SEQ // GRID_W - NA_ROWS)
        return pl.multiple_of(r_start * GRID_W, 64)
    return 0


def _band_mask(qi, tq, tk, ks):
    qpos = qi * tq + _iota((tq, tk), 0)
    kpos = ks + _iota((tq, tk), 1)
    return jnp.where(jnp.abs(qpos - kpos) <= REACH, 0.0, NEG).astype(F32)


def _stack_heads(x, lanes, pair, scale=1.0):
    if not pair:
        return x
    return jnp.concatenate([_mask_head(x, lanes, hh, pair, scale) for hh in range(2)], axis=0)


def _stack_rows(x, lanes, pair):
    if not pair:
        return _head_rows(x, lanes, 0, pair)
    return jnp.concatenate([_head_rows(x, lanes, hh, pair) for hh in range(2)], axis=0)


def _unstack_heads(x, lanes, pair, tq):
    if not pair:
        return x
    return jnp.where(lanes < 64, x[:tq], x[tq:])


def _scores(mode, qst, k, sscale, band, qi, bias_ref, pair):
    s = lax.dot_general(qst, k, NT, preferred_element_type=F32)
    if sscale != 1.0:
        s = s * sscale
    if mode == "dil":
        s = s + jnp.concatenate([band, band], axis=0)
    elif mode == "na":
        off = qi - jnp.clip(qi - NA_ROWS // 2, 0, SEQ // GRID_W - NA_ROWS)
        s = s + jnp.concatenate([bias_ref[0, off], bias_ref[1, off]], axis=0)
    return s


def _attn_cfg(mode, d):
    if mode == "dil":
        mlen = SEQ // d
        return dict(pair=True, tq=128, tk=min(256, mlen), mlen=mlen, lk=SEQ, scale=HEAD_DIM ** -0.5, units=4,
                    nsub=ATTN_SUBTILES)
    if mode == "na":
        return dict(pair=True, tq=GRID_W, tk=NA_ROWS * GRID_W, mlen=SEQ, lk=SEQ, scale=HEAD_DIM ** -0.5, units=4,
                    nsub=ATTN_SUBTILES)
    return dict(pair=False, tq=128, tk=MEM_LEN, mlen=SEQ, lk=MEM_LEN, scale=128 ** -0.5, units=4,
                nsub=ATTN_SUBTILES)


ATTN_SUBTILES = 16


def _attn_fwd(name, mode, q_arr, k_arr, v_arr, qcol, kcol, vcol, d=1, bias=None):
    cfg = _attn_cfg(mode, d)
    pair, tq, tk, mlen, lk, scale = cfg["pair"], cfg["tq"], cfg["tk"], cfg["mlen"], cfg["lk"], cfg["scale"]
    qscale, sscale = (scale, 1.0) if pair else (1.0, scale)
    nsub = cfg["nsub"]
    rows = nsub * tq

    def body(*refs):
        if mode == "na":
            q_ref, k_ref, v_ref, bias_ref, o_ref, l_ref = refs
        else:
            q_ref, k_ref, v_ref, o_ref, l_ref = refs
            bias_ref = None
        lanes = _iota((tq, 128), 1)
        qis = [pl.program_id(1) * nsub + sub for sub in range(nsub)]
        kss = [_window(mode, qi, tq, mlen, tk) for qi in qis]
        vs = [v_ref[pl.ds(ks, tk), :] for ks in kss]
        bands = [_band_mask(qi, tq, tk, ks) if mode == "dil" else None for qi, ks in zip(qis, kss)]
        ss = []
        for sub in range(nsub):
            qst = _stack_heads(q_ref[sub * tq:(sub + 1) * tq, :], lanes, pair, qscale)
            k = k_ref[pl.ds(kss[sub], tk), :]
            ss.append(_scores(mode, qst, k, sscale, bands[sub], qis[sub], bias_ref, pair))
        ms = [jnp.max(s_, axis=1, keepdims=True) for s_ in ss]
        ps = [jnp.exp(s_ - m) for s_, m in zip(ss, ms)]
        ls = [jnp.sum(p, axis=1, keepdims=True) for p in ps]
        os_ = [jnp.dot(p.astype(BF16), v, preferred_element_type=F32) for p, v in zip(ps, vs)]
        for sub in range(nsub):
            out = _unstack_heads(os_[sub] / ls[sub], lanes, pair, tq)
            lse = ms[sub] + jnp.log(ls[sub])
            lse = _unstack_heads(jnp.broadcast_to(lse, (lse.shape[0], 128)), lanes, pair, tq)
            dst = _folded_rows(qis[sub] * tq, tq, d) if mode == "dil" else slice(sub * tq, (sub + 1) * tq)
            o_ref[dst, :] = out
            l_ref[dst, :] = lse

    in_specs = [pl.BlockSpec((rows, 128), lambda u, i: (i, qcol + u)),
                pl.BlockSpec((lk, 128), lambda u, i: (0, kcol + u)),
                pl.BlockSpec((lk, 128), lambda u, i: (0, vcol + u))]
    args = [q_arr, k_arr, v_arr]
    if mode == "na":
        in_specs.append(pl.BlockSpec((2, NA_ROWS, GRID_W, NA_ROWS * GRID_W), lambda u, i: (u, 0, 0, 0)))
        args.append(bias)
    if mode == "dil":
        out_spec = pl.BlockSpec((SEQ, 128), lambda u, i: (0, u))
    else:
        out_spec = pl.BlockSpec((rows, 128), lambda u, i: (i, u))
    return pl.pallas_call(
        body, name=name, grid=(cfg["units"], SEQ // rows), in_specs=in_specs, out_specs=[out_spec, out_spec],
        out_shape=[jax.ShapeDtypeStruct((SEQ, 512), F32), jax.ShapeDtypeStruct((SEQ, 512), F32)],
        compiler_params=_params(("parallel", "arbitrary")))(*args)


def _attn_bwd(name, mode, q_arr, k_arr, v_arr, qcol, kcol, vcol, do, lse, dp=None, o=None, d=1, bias=None,
              tabs=None):
    cfg = _attn_cfg(mode, d)
    pair, tq, tk, mlen, lk, scale = cfg["pair"], cfg["tq"], cfg["tk"], cfg["mlen"], cfg["lk"], cfg["scale"]
    qscale, sscale = (scale, 1.0) if pair else (1.0, scale)
    nsub = cfg["nsub"]
    rows = nsub * tq
    nq = SEQ // rows
    kv_dtype = F32 if mode == "mem" else BF16

    def body(*refs):
        refs = list(refs)
        q_ref, k_ref, v_ref, do_ref, l_ref = refs[:5]
        rest = refs[5:]
        bias_ref = tq_ref = tk_ref = db_ref = None
        if mode == "dil":
            dp_ref, tq_ref, tk_ref, dq_ref, dk_ref, dv_ref, dk_acc, dv_acc = rest
        elif mode == "na":
            o_ref, bias_ref, dq_ref, dk_ref, dv_ref, db_ref, dk_acc, dv_acc = rest
        else:
            o_ref, dq_ref, dk_ref, dv_ref, dk_acc, dv_acc = rest
        step = pl.program_id(1)

        @pl.when(step == 0)
        def _():
            dk_acc[...] = jnp.zeros((lk, 128), F32)
            dv_acc[...] = jnp.zeros((lk, 128), F32)
            if mode == "na":
                db_ref[...] = jnp.zeros(db_ref.shape, F32)

        lanes = _iota((tq, 128), 1)
        qis = [step * nsub + sub for sub in range(nsub)]
        sls = [slice(sub * tq, (sub + 1) * tq) for sub in range(nsub)]
        kss = [_window(mode, qi, tq, mlen, tk) for qi in qis]
        ks_ = [k_ref[pl.ds(ks, tk), :] for ks in kss]
        vs = [v_ref[pl.ds(ks, tk), :] for ks in kss]
        qsts, dosts, lses, dphs = [], [], [], []
        for sub in range(nsub):
            if mode == "dil":
                src = _folded_rows(qis[sub] * tq, tq, d)
                dov = do_ref[src, :].astype(BF16)
                lsev = l_ref[src, :]
                dphs.append(_stack_rows(dp_ref[src, :], lanes, pair))
            else:
                dov = do_ref[sls[sub], :]
                lsev = l_ref[sls[sub], :]
                dpv = dov.astype(F32) * o_ref[sls[sub], :]
                if pair:
                    dphs.append(jnp.concatenate(
                        [jnp.sum(jnp.where(_head_lanes(lanes, hh), dpv, 0.0), axis=1, keepdims=True)
                         for hh in range(2)], axis=0))
                else:
                    dphs.append(jnp.sum(dpv, axis=1, keepdims=True))
            qsts.append(_stack_heads(q_ref[sls[sub], :], lanes, pair, qscale))
            dosts.append(_stack_heads(dov, lanes, pair))
            lses.append(_stack_rows(lsev, lanes, pair))
        bands = [_band_mask(qi, tq, tk, ks) if mode == "dil" else None for qi, ks in zip(qis, kss)]
        ss = [_scores(mode, qsts[sub], ks_[sub], sscale, bands[sub], qis[sub], bias_ref, pair) for sub in range(nsub)]
        dpms = [lax.dot_general(dosts[sub], vs[sub], NT, preferred_element_type=F32) for sub in range(nsub)]
        ps = [jnp.exp(s_ - lse) for s_, lse in zip(ss, lses)]
        dss = [p * (dpm - dph) for p, dpm, dph in zip(ps, dpms, dphs)]
        if mode == "na":
            for sub, ds in enumerate(dss):
                off = qis[sub] - jnp.clip(qis[sub] - NA_ROWS // 2, 0, SEQ // GRID_W - NA_ROWS)
                db_ref[0, off] += ds[:tq]
                db_ref[1, off] += ds[tq:]
        dsbs = [ds.astype(BF16) for ds in dss]
        dvs = [lax.dot_general(p.astype(BF16), dosts[sub], TN, preferred_element_type=F32)
               for sub, p in enumerate(ps)]
        dqs = [jnp.dot(dsb, ks_[sub], preferred_element_type=F32) * scale for sub, dsb in enumerate(dsbs)]
        dks = [lax.dot_general(dsb, qsts[sub], TN, preferred_element_type=F32) for sub, dsb in enumerate(dsbs)]
        for sub in range(nsub):
            sl = sls[sub]
            dq = _unstack_heads(dqs[sub], lanes, pair, tq)
            if mode == "dil":
                dq = _rope_t(dq, tq_ref[0, sl, :], tq_ref[1, sl, :], tq_ref[2, sl, :])
            dq_ref[sl, :] = dq.astype(BF16)
            dk_acc[pl.ds(kss[sub], tk), :] += dks[sub] if pair else dks[sub] * scale
            dv_acc[pl.ds(kss[sub], tk), :] += dvs[sub]

        @pl.when(step == nq - 1)
        def _():
            dkv = dk_acc[...]
            if mode == "dil":
                dkv = _rope_t(dkv, tk_ref[0], tk_ref[1], tk_ref[2])
            dk_ref[...] = dkv.astype(kv_dtype)
            dv_ref[...] = dv_acc[...].astype(kv_dtype)

    q_spec = pl.BlockSpec((rows, 128), lambda u, i: (i, qcol + u))
    row_spec = pl.BlockSpec((rows, 128), lambda u, i: (i, u))
    kv_out = pl.BlockSpec((lk, 128), lambda u, i: (0, u))
    whole = pl.BlockSpec((SEQ, 128), lambda u, i: (0, u))
    nat_spec = whole if mode == "dil" else row_spec
    in_specs = [q_spec,
                pl.BlockSpec((lk, 128), lambda u, i: (0, kcol + u)),
                pl.BlockSpec((lk, 128), lambda u, i: (0, vcol + u)),
                nat_spec, nat_spec]
    args = [q_arr, k_arr, v_arr, do, lse]
    out_specs = [row_spec, kv_out, kv_out]
    out_shape = [jax.ShapeDtypeStruct((SEQ, 512), BF16), jax.ShapeDtypeStruct((lk, 512), kv_dtype),
                 jax.ShapeDtypeStruct((lk, 512), kv_dtype)]
    if mode == "dil":
        in_specs += [whole, pl.BlockSpec((3, rows, 128), lambda u, i: (0, i, 0)),
                     pl.BlockSpec((3, SEQ, 128), lambda u, i: (0, 0, 0))]
        args += [dp, tabs, tabs]
    elif mode == "na":
        b_spec = pl.BlockSpec((2, NA_ROWS, GRID_W, NA_ROWS * GRID_W), lambda u, i: (u, 0, 0, 0))
        in_specs += [row_spec, b_spec]
        args += [o, bias]
        out_specs.append(b_spec)
        out_shape.append(jax.ShapeDtypeStruct((8, NA_ROWS, GRID_W, NA_ROWS * GRID_W), F32))
    else:
        in_specs.append(row_spec)
        args.append(o)
    return pl.pallas_call(
        body, name=name, grid=(cfg["units"], nq), in_specs=in_specs, out_specs=out_specs, out_shape=out_shape,
        scratch_shapes=[pltpu.VMEM((lk, 128), F32), pltpu.VMEM((lk, 128), F32)],
        compiler_params=_params(("parallel", "arbitrary")))(*args)


def _na_geometry():
    qc = _iota((GRID_W, 128), 0)
    lane = _iota((GRID_W, 128), 1)
    kc = lane & 63
    c_start = jnp.clip(qc - 8, 0, GRID_W - 16)
    valid = jnp.logical_and(kc >= c_start, kc < c_start + 16)
    return lane, valid


def _na_bias(rpb_rows):
    def body(r_ref, o_ref, t_ref):
        lane, valid = _na_geometry()
        for dd in range(14):
            row_a = jnp.broadcast_to(r_ref[dd:dd + 1, :], (GRID_W, 128))
            row_b = jnp.broadcast_to(r_ref[dd + 1:dd + 2, :], (GRID_W, 128))
            both = jnp.where(lane < 64, row_a, pltpu.roll(row_b, 64, 1))
            t = pltpu.roll(both, 128 - 15, 1, stride=1, stride_axis=0)
            t_ref[dd] = jnp.where(valid, t, NEG)
        for off in range(NA_ROWS):
            for p in range(4):
                o_ref[off, :, p * 128:(p + 1) * 128] = t_ref[2 * p - off + 7]

    return pl.pallas_call(
        body, name="na_bias", grid=(8,),
        in_specs=[pl.BlockSpec((None, 16, 128), lambda h: (h, 0, 0))],
        out_specs=pl.BlockSpec((None, NA_ROWS, GRID_W, NA_ROWS * GRID_W), lambda h: (h, 0, 0, 0)),
        out_shape=jax.ShapeDtypeStruct((8, NA_ROWS, GRID_W, NA_ROWS * GRID_W), F32),
        scratch_shapes=[pltpu.VMEM((14, GRID_W, 128), F32)],
        compiler_params=_params(("parallel",)))(rpb_rows)


def _na_bias_bwd(dbias, dep=None):
    dep_specs, dep_args = _dep_operand(dep)

    def body(d_ref, *rest):
        o_ref = rest[-1]
        lane, valid = _na_geometry()
        reverse = (_iota((GRID_W, GRID_W), 0) + _iota((GRID_W, GRID_W), 1) == GRID_W - 1).astype(F32)
        o_ref[...] = jnp.zeros((16, 128), F32)
        for dd in range(14):
            t = jnp.zeros((GRID_W, 128), F32)
            for off in range(NA_ROWS):
                for p in range(4):
                    if 2 * p - off + 7 == dd:
                        t = t + d_ref[off, :, p * 128:(p + 1) * 128]
            t = jnp.dot(reverse, jnp.where(valid, t, 0.0), precision=lax.Precision.HIGHEST,
                        preferred_element_type=F32)
            t = pltpu.roll(t, 128 - (GRID_W - 16), 1, stride=1, stride_axis=0)
            o_ref[dd:dd + 1, :] = jnp.sum(t, axis=0, keepdims=True)

    return pl.pallas_call(
        body, name="na_bias_bwd", grid=(8,),
        in_specs=[pl.BlockSpec((None, NA_ROWS, GRID_W, NA_ROWS * GRID_W), lambda h: (h, 0, 0, 0))] + dep_specs,
        out_specs=pl.BlockSpec((None, 16, 128), lambda h: (h, 0, 0)),
        out_shape=jax.ShapeDtypeStruct((8, 16, 128), F32),
        compiler_params=_params(("parallel",)))(dbias, *dep_args)


GATE_ROWS = 128


def _group_weights(l0, l1, l2):
    m = jnp.maximum(jnp.maximum(l0, l1), l2)
    e0, e1, e2 = jnp.exp(l0 - m), jnp.exp(l1 - m), jnp.exp(l2 - m)
    inv = 1.0 / (e0 + e1 + e2)
    return e0 * inv, e1 * inv, e2 * inv


def _gate_block(o_grp, l_grp, out_b, out_c, parts, x, target, merge_bias, wts, w_out, gain, head_sum):
    rows = GATE_ROWS
    r512 = pl.BlockSpec((rows, 512), lambda i: (i, 0))
    r1024 = pl.BlockSpec((rows, D_MODEL), lambda i: (i, 0))
    silu_cols = [pl.BlockSpec((rows, 512), functools.partial(lambda b, i: (i, b), 13 + b)) for b in range(3)]
    logit_cols = [pl.BlockSpec((rows, D_MODEL), functools.partial(lambda b, i: (i, b), 8 + b)) for b in range(3)]

    def body(o0, o1, o2, l0, l1, l2, ob, oc, ga, gb, gc, la, lb, lc, x_ref, t_ref, mb, wa, wb, wc, wo_ref, gn_ref,
             hs_ref, dout_ref, dla, dlb, dlc, dga, dgb, dgc, do0, do1, do2, dp0, dp1, dp2, dob, doc, err_ref, gg_ref,
             gmb, gwa, gwb, gwc, gwo, acc_a, acc_b, acc_c, acc_o):
        step = pl.program_id(0)
        ws = _group_weights(l0[...], l1[...], l2[...])
        out_a = ws[0] * o0[...] + ws[1] * o1[...] + ws[2] * o2[...]
        branches = ((out_a, ga, la, wa, acc_a, dla, dga), (ob[...], gb, lb, wb, acc_b, dlb, dgb),
                    (oc[...], gc, lc, wc, acc_c, dlc, dgc))

        @pl.when(step == 0)
        def _():
            for acc in (acc_a, acc_b, acc_c, acc_o):
                acc[...] = jnp.zeros(acc.shape, F32)
            err_ref[...] = jnp.zeros((1, D_MODEL), F32)
            gg_ref[...] = jnp.zeros((1, D_MODEL), F32)
            gmb[...] = jnp.zeros((3, D_MODEL), F32)

        y = jnp.zeros((rows, D_MODEL), F32)
        zs, gates, silus, dsilus, us = [], [], [], [], []
        for b, (ov, g_ref, l_ref, w_ref, _, _, _) in enumerate(branches):
            g = g_ref[...].astype(F32)
            sg = _sigmoid(g)
            silus.append(g * sg)
            dsilus.append(sg * (1.0 + g * (1.0 - sg)))
            us.append((ov * silus[b]).astype(BF16))
            zs.append(lax.dot_general(us[b], w_ref[...], NT, preferred_element_type=F32))
            gates.append(_sigmoid(l_ref[...].astype(F32) + mb[b:b + 1, :]))
            y = y + gates[b] * zs[b]
        yb = y.astype(BF16)
        y2 = jnp.dot(yb, wo_ref[...], preferred_element_type=F32)
        rstd = lax.rsqrt(jnp.mean(y2 * y2, axis=1, keepdims=True) + EPS)
        yn = y2 * rstd
        gv = gn_ref[...]
        err = x_ref[...] + yn * gv - t_ref[...]
        dout = err * (1.0 / D_MODEL)
        dout_ref[...] = dout
        dn = dout * gv
        dy2 = (rstd * (dn - yn * jnp.mean(dn * yn, axis=1, keepdims=True))).astype(BF16)
        acc_o[...] += lax.dot_general(yb, dy2, TN, preferred_element_type=F32)
        err_ref[...] += jnp.sum(err * err, axis=0, keepdims=True)
        gg_ref[...] += jnp.sum(dout * yn, axis=0, keepdims=True)
        dy = lax.dot_general(dy2, wo_ref[...], NT, preferred_element_type=F32)
        dos = []
        for b, (ov, _, _, w_ref, acc, dl_ref, dg_ref) in enumerate(branches):
            dl = dy * zs[b] * gates[b] * (1.0 - gates[b])
            dl_ref[...] = dl.astype(BF16)
            gmb[b:b + 1, :] += jnp.sum(dl, axis=0, keepdims=True)
            dz = (dy * gates[b]).astype(BF16)
            acc[...] += lax.dot_general(dz, us[b], TN, preferred_element_type=F32)
            du = jnp.dot(dz, w_ref[...], preferred_element_type=F32)
            dos.append(du * silus[b])
            dg_ref[...] = (du * ov * dsilus[b]).astype(BF16)
        dob[...] = dos[1].astype(BF16)
        doc[...] = dos[2].astype(BF16)
        row_term = jnp.dot(dos[0] * out_a, hs_ref[...], precision=lax.Precision.HIGHEST, preferred_element_type=F32)
        for wg, do_ref, dp_ref in zip(ws, (do0, do1, do2), (dp0, dp1, dp2)):
            do_ref[...] = wg * dos[0]
            dp_ref[...] = wg * row_term

        @pl.when(step == SEQ // rows - 1)
        def _():
            for acc, out in ((acc_a, gwa), (acc_b, gwb), (acc_c, gwc), (acc_o, gwo)):
                out[...] = acc[...].astype(BF16)

    full = lambda shape: pl.BlockSpec(shape, lambda i: (0,) * len(shape))
    vec = pl.BlockSpec((1, D_MODEL), lambda i: (0, 0))
    acc3 = pl.BlockSpec((3, D_MODEL), lambda i: (0, 0))
    in_specs = ([r512] * 8 + silu_cols + logit_cols + [r1024, r1024, full((3, D_MODEL))]
                + [full((D_MODEL, 512))] * 3 + [full((D_MODEL, D_MODEL)), vec, full((512, 512))])
    out_specs = ([r1024] + [r1024] * 3 + [r512] * 3 + [r512] * 6 + [r512] * 2 + [vec, vec, acc3]
                 + [full((D_MODEL, 512))] * 3 + [full((D_MODEL, D_MODEL))])
    bf, f32 = BF16, F32
    sds = jax.ShapeDtypeStruct
    out_shape = ([sds((SEQ, D_MODEL), f32)] + [sds((SEQ, D_MODEL), bf)] * 3 + [sds((SEQ, 512), bf)] * 3
                 + [sds((SEQ, 512), f32)] * 6 + [sds((SEQ, 512), bf)] * 2 + [sds((1, D_MODEL), f32)] * 2
                 + [sds((3, D_MODEL), f32)] + [sds((D_MODEL, 512), bf)] * 3 + [sds((D_MODEL, D_MODEL), bf)])
    res = pl.pallas_call(
        body, name="gate_block", grid=(SEQ // rows,), in_specs=in_specs, out_specs=out_specs, out_shape=out_shape,
        scratch_shapes=[pltpu.VMEM((D_MODEL, 512), F32)] * 3 + [pltpu.VMEM((D_MODEL, D_MODEL), F32)],
        compiler_params=_params(("arbitrary",)))(
            *o_grp, *l_grp, out_b, out_c, parts, parts, parts, parts, parts, parts, x, target, merge_bias, *wts, w_out,
            gain, head_sum)
    return dict(dout=res[0], dlog=res[1:4], dg=res[4:7], do_grp=res[7:10], dp_grp=res[10:13], do_b=res[13],
                do_c=res[14], err_sq=res[15], g_post=res[16], g_mb=res[17], g_wt=res[18:21], g_w_out=res[21])


def _local_step(x, mem, target, pre_norm, mem_norm, post_norm, na_rpb, wt_in, late_weights, dep_in=None,
                reduce_start=None):
    tabs = _rope_tables()
    hs, hst = _prenorm_fold(x, pre_norm)
    parts = _in_proj(hs, wt_in, tabs, dep_in)

    o_grp, l_grp = [], []
    for g, d in enumerate(DILATIONS):
        o, l = _attn_fwd("dil_fwd_%d" % g, "dil", parts, parts, parts, 12 * g, 12 * g + 4, 12 * g + 8, d=d)
        o_grp.append(o)
        l_grp.append(l)
    bias = _na_bias(jnp.pad(na_rpb, ((0, 0), (0, 1), (0, 128 - 31))))
    out_b, lse_b = _attn_fwd("na_fwd", "na", parts, parts, parts, 36, 40, 44, bias=bias)
    merge_bias, w_kv, wt_a, wt_b, wt_c, w_out = late_weights(out_b)
    memn = _rmsnorm_fwd("memnorm", mem, mem_norm, MEM_LEN)
    kv_m = _mm_simple("mem_kv", memn, w_kv, NN, BF16, MEM_LEN, 512, D_MODEL)
    out_c, lse_c = _attn_fwd("mem_fwd", "mem", parts, kv_m, kv_m, 48, 0, 4)

    rr = _iota((512, 512), 0) // HEAD_DIM
    cc = _iota((512, 512), 1) // HEAD_DIM
    head_sum = (rr == cc).astype(F32)
    gb = _gate_block(o_grp, l_grp, out_b, out_c, parts, x, target, merge_bias, (wt_a, wt_b, wt_c), w_out, post_norm,
                     head_sum)
    dout, dlog, dg, g_wt, g_w_out = gb["dout"], gb["dlog"], gb["dg"], gb["g_wt"], gb["g_w_out"]
    do_grp, dp_grp, do_b, do_c, g_post, g_mb = (gb["do_grp"], gb["dp_grp"], gb["do_b"], gb["do_c"], gb["g_post"],
                                                gb["g_mb"])
    loss = 0.5 * jnp.sum(gb["err_sq"]) / D_MODEL

    dqkv = []
    for g, d in enumerate(DILATIONS):
        dq, dk, dv = _attn_bwd("dil_bwd_%d" % g, "dil", parts, parts, parts, 12 * g, 12 * g + 4, 12 * g + 8,
                               do_grp[g], l_grp[g], dp=dp_grp[g], d=d, tabs=tabs[g])
        dqkv += [dq, dk, dv]
    dq_b, dk_b, dv_b, dbias = _attn_bwd("na_bwd", "na", parts, parts, parts, 36, 40, 44, do_b, lse_b, o=out_b,
                                        bias=bias)
    dq_c, dk_m, dv_m = _attn_bwd("mem_bwd", "mem", parts, kv_m, kv_m, 48, 0, 4, do_c, lse_c, o=out_c)

    dkv = jnp.concatenate([dk_m, dv_m], axis=1).astype(BF16)
    g_w_kv = _mm_simple("mem_kv_dw", memn, dkv, TN, BF16, D_MODEL, 512, MEM_LEN)
    dmemn = _mm_simple("mem_kv_dx", dkv, w_kv, NT, F32, MEM_LEN, 512, D_MODEL)

    grads = dict(w_kv=g_w_kv, wt_a=g_wt[0], wt_b=g_wt[1], wt_c=g_wt[2], w_out=g_w_out, merge_bias=g_mb,
                 post_norm=g_post)
    dep = reduce_start(grads) if reduce_start is not None else None
    dparts = dqkv + [dq_b, dk_b, dv_b, dq_c] + list(dg) + list(dlog)
    grads["wt_in"] = _in_proj_dw(dparts, hst, dep)
    dep = reduce_start(grads) if reduce_start is not None else None
    dh = _in_proj_dh(dparts, wt_in, dep)
    grad_x, grads["pre_norm"] = _prenorm_bwd(x, pre_norm, dh, dout)
    g_rpb_t = _na_bias_bwd(dbias, dep)
    grads["na_rpb"] = g_rpb_t[:, :15, :31] + jnp.pad(g_rpb_t[:, :14, 64:95], ((0, 0), (1, 0), (0, 0)))
    grads["mem_norm"] = _memnorm_bwd(mem, dmemn, dep)
    return loss, grad_x, grads


ANY = pl.BlockSpec(memory_space=pl.ANY)


def _place():
    return lax.axis_index("x"), lax.axis_index("y"), lax.axis_index("c")


def _all_gather(shard):
    r = shard.shape[0]
    half = r // 2

    def body(src, out, send_sems, recv_sems, local_sem):
        x, y, c = _place()
        me, sib = (x, y, c), (x, y, 1 - c)
        xn, yn, dg = (1 - x, y, c), (x, 1 - y, c), (1 - x, 1 - y, c)

        def rows(dev, part=None):
            blk = out.at[4 * dev[0] + 2 * dev[1] + dev[2]]
            return blk if part is None else blk.at[pl.ds(part * half, half)]

        def copy(k, dev, part, to, own=False):
            return pltpu.make_async_remote_copy(
                src_ref=src if own else rows(dev, part), dst_ref=rows(dev, part),
                send_sem=send_sems.at[k], recv_sem=recv_sems.at[k], device_id=to, device_id_type=MESH_ID)

        def other(dev):
            return (dev[0], dev[1], 1 - dev[2])

        mine = pltpu.make_async_copy(src, rows(me), local_sem)
        mine.start()
        sent = [copy(0, me, None, sib, own=True), copy(1, me, None, xn, own=True), copy(2, me, None, yn, own=True)]
        for cp in sent:
            cp.start()
        copy(1, xn, None, me).wait_recv()
        sent += [copy(3, xn, 0, yn), copy(5, xn, None, sib)]
        sent[-2].start()
        sent[-1].start()
        copy(2, yn, None, me).wait_recv()
        sent += [copy(4, yn, 1, xn), copy(6, yn, None, sib)]
        sent[-2].start()
        sent[-1].start()
        copy(3, dg, 0, me).wait_recv()
        sent.append(copy(7, dg, 0, sib))
        sent[-1].start()
        copy(4, dg, 1, me).wait_recv()
        sent.append(copy(8, dg, 1, sib))
        sent[-1].start()
        copy(0, sib, None, me).wait_recv()
        copy(5, other(xn), None, me).wait_recv()
        copy(6, other(yn), None, me).wait_recv()
        copy(7, other(dg), 0, me).wait_recv()
        copy(8, other(dg), 1, me).wait_recv()
        for cp in sent:
            cp.wait_send()
        mine.wait()

    return pl.pallas_call(
        body, name="all_gather", in_specs=[ANY], out_specs=ANY,
        out_shape=jax.ShapeDtypeStruct((N_DEV,) + shard.shape, shard.dtype),
        scratch_shapes=[pltpu.SemaphoreType.DMA((9,)), pltpu.SemaphoreType.DMA((9,)), pltpu.SemaphoreType.DMA])(shard)


def _exchange_sibling(name, terms):
    nt = len(terms)

    def body(*refs):
        srcs, outs = refs[:nt], refs[nt:2 * nt]
        send_sems, recv_sems = refs[2 * nt:]
        x, y, c = _place()
        copies = []
        for q in range(4):
            for t in range(nt):
                copies.append(pltpu.make_async_remote_copy(
                    src_ref=srcs[t].at[2 * q + 1 - c], dst_ref=outs[t].at[q],
                    send_sem=send_sems.at[q * nt + t], recv_sem=recv_sems.at[q * nt + t],
                    device_id=(x, y, 1 - c), device_id_type=MESH_ID))
        for cp in copies:
            cp.start()
        for cp in copies:
            cp.wait()

    return pl.pallas_call(
        body, name=name, in_specs=[ANY] * nt, out_specs=[ANY] * nt,
        out_shape=[jax.ShapeDtypeStruct((4,) + s.shape[1:], s.dtype) for s in terms],
        scratch_shapes=[pltpu.SemaphoreType.DMA((4 * nt,)), pltpu.SemaphoreType.DMA((4 * nt,))])(*terms)


HBM = pl.BlockSpec(memory_space=pltpu.HBM)
SEM = pl.BlockSpec(memory_space=pltpu.SEMAPHORE)
DATAFLOW = pltpu.SideEffectType.DATAFLOW_SIDE_EFFECTING


def _split_copies(kind, srcs, lands, send_sems, recv_sems):
    nt = len(srcs)
    x, y, c = _place()
    copies = []
    if kind == "gather":
        me = 4 * x + 2 * y + c
        for mask in range(1, 8):
            fx, fy, fc = (mask >> 2) & 1, (mask >> 1) & 1, mask & 1
            to = (1 - x if fx else x, 1 - y if fy else y, 1 - c if fc else c)
            for t in range(nt):
                k = (mask - 1) * nt + t
                copies.append(pltpu.make_async_remote_copy(
                    src_ref=srcs[t], dst_ref=lands[t].at[me], send_sem=send_sems.at[k], recv_sem=recv_sems.at[k],
                    device_id=to, device_id_type=MESH_ID))
    else:
        for s, (tx, ty) in enumerate([(1 - x, y), (x, 1 - y), (1 - x, 1 - y)]):
            for t in range(nt):
                k = s * nt + t
                copies.append(pltpu.make_async_remote_copy(
                    src_ref=srcs[t].at[2 * tx + ty], dst_ref=lands[t].at[s], send_sem=send_sems.at[k],
                    recv_sem=recv_sems.at[k], device_id=(tx, ty, c), device_id_type=MESH_ID))
    return copies


def _split_count(kind, nt):
    return (7 if kind == "gather" else 3) * nt


def _exchange_start(name, kind, srcs, land_shapes, after=None):
    nt = len(srcs)
    n = _split_count(kind, nt)
    dep_specs, dep_args = _dep_operand(after)
    nd = len(dep_args)

    def body(*refs):
        src_refs, land_refs = refs[:nt], refs[nt:2 * nt]
        send_sems, recv_sems = refs[2 * nt + nd], refs[2 * nt + nd + 1]
        token = refs[-1]
        for cp in _split_copies(kind, src_refs, land_refs, send_sems, recv_sems):
            cp.start()
        token[...] = jnp.zeros_like(token)

    lands = [pltpu.with_memory_space_constraint(lax.empty(s.shape, s.dtype), pltpu.HBM) for s in land_shapes]
    res = pl.pallas_call(
        body, name=name,
        out_shape=(pltpu.SemaphoreType.DMA((n,)), pltpu.SemaphoreType.DMA((n,)),
                   *[pltpu.HBM(s.shape, s.dtype) for s in srcs], *[pltpu.HBM(s.shape, s.dtype) for s in land_shapes],
                   jax.ShapeDtypeStruct((8, 128), F32)),
        in_specs=[HBM] * (2 * nt) + dep_specs,
        out_specs=(SEM, SEM, *([HBM] * (2 * nt)), pl.BlockSpec(memory_space=pltpu.VMEM)),
        input_output_aliases={i: 2 + i for i in range(2 * nt)},
        compiler_params=pltpu.CompilerParams(has_side_effects=DATAFLOW))(
            *[pltpu.with_memory_space_constraint(s, pltpu.HBM) for s in srcs], *lands, *dep_args)
    return res[0], res[1], list(res[2:2 + nt]), list(res[2 + nt:2 + 2 * nt]), res[-1]


def _exchange_wait(name, kind, send_sems, recv_sems, srcs, lands, after):
    nt = len(srcs)

    def body(*refs):
        src_refs, land_refs = refs[:nt], refs[nt:2 * nt]
        s_sems, r_sems = refs[2 * nt], refs[2 * nt + 1]
        for cp in _split_copies(kind, src_refs, land_refs, s_sems, r_sems):
            cp.wait_send()
            cp.wait_recv()

    res = pl.pallas_call(
        body, name=name,
        out_shape=tuple(pltpu.HBM(s.shape, s.dtype) for s in list(srcs) + list(lands)),
        in_specs=[HBM] * (2 * nt) + [SEM, SEM, pl.BlockSpec(memory_space=pl.ANY)],
        out_specs=tuple([HBM] * (2 * nt)),
        input_output_aliases={i: i for i in range(2 * nt)},
        compiler_params=pltpu.CompilerParams(has_side_effects=DATAFLOW))(
            *srcs, *lands, send_sems, recv_sems, after)
    return list(res[:nt]), list(res[nt:])


def _add_sibling(name, term, recv, rows):
    _, r, w = term.shape
    cidx = lax.axis_index("c").astype(jnp.int32).reshape(1)
    like_term = recv.shape[0] == N_DEV

    def body(c_ref, a_ref, b_ref, o_ref):
        o_ref[...] = (a_ref[...].astype(F32) + b_ref[...].astype(F32)).astype(o_ref.dtype)

    grid_spec = pltpu.PrefetchScalarGridSpec(
        num_scalar_prefetch=1, grid=(4, r // rows),
        in_specs=[pl.BlockSpec((None, rows, w), lambda q, i, c_ref: (2 * q + c_ref[0], i, 0)),
                  pl.BlockSpec((None, rows, w), lambda q, i, c_ref: (2 * q + c_ref[0] if like_term else q, i, 0))],
        out_specs=pl.BlockSpec((None, rows, w), lambda q, i, c_ref: (q, i, 0)))
    return pl.pallas_call(
        body, name=name, grid_spec=grid_spec, out_shape=jax.ShapeDtypeStruct((4, r, w), term.dtype),
        compiler_params=_params(("parallel", "parallel")))(cidx, term, recv)


def _add_sibling_small(name, terms, recvs):
    nt = len(terms)

    def body(*refs):
        c = lax.axis_index("c")
        for t_ref, r_ref, o_ref in zip(refs[:nt], refs[nt:2 * nt], refs[2 * nt:]):
            for q in range(4):
                o_ref[q] = (t_ref[2 * q + c].astype(F32) + r_ref[q].astype(F32)).astype(o_ref.dtype)

    return pl.pallas_call(
        body, name=name, out_shape=[jax.ShapeDtypeStruct((4,) + t.shape[1:], t.dtype) for t in terms],
        compiler_params=_params())(*terms, *recvs)


def _add_chips(name, sums, recv, rows):
    _, r, w = sums.shape
    qidx = (2 * lax.axis_index("x") + lax.axis_index("y")).astype(jnp.int32).reshape(1)

    def body(q_ref, a_ref, b_ref, o_ref):
        o_ref[...] = ((a_ref[...].astype(F32) + b_ref[0].astype(F32))
                      + (b_ref[1].astype(F32) + b_ref[2].astype(F32)))

    grid_spec = pltpu.PrefetchScalarGridSpec(
        num_scalar_prefetch=1, grid=(r // rows,),
        in_specs=[pl.BlockSpec((None, rows, w), lambda i, q_ref: (q_ref[0], i, 0)),
                  pl.BlockSpec((3, rows, w), lambda i, q_ref: (0, i, 0))],
        out_specs=pl.BlockSpec((rows, w), lambda i, q_ref: (i, 0)))
    return pl.pallas_call(
        body, name=name, grid_spec=grid_spec, out_shape=jax.ShapeDtypeStruct((r, w), F32),
        compiler_params=_params(("parallel",)))(qidx, sums, recv)


def _rs_rows(a):
    return SHARD_IN // 4 if a.shape[1] == SHARD_IN else a.shape[1]


def _reduce_scatter_start(tag, names, terms, recv1=None):
    if recv1 is None:
        recv1 = _exchange_sibling("exchange_sibling_" + tag, terms)
    if len(terms) == 1:
        sums = [_add_sibling("add_sibling_" + names[0], terms[0], recv1[0], _rs_rows(terms[0]))]
    else:
        sums = _add_sibling_small("add_sibling_" + tag, terms, recv1)
    lands =[jax.ShapeDtypeStruct((3,) + s.shape[1:], s.dtype) for s in sums]
    send_sems, recv_sems, sums, lands, token = _exchange_start("exchange_chips_start_" + tag, "chips", sums, lands)
    return (tag, names, send_sems, recv_sems, sums, lands), token


def _reduce_scatter_wait(state, after):
    tag, names, send_sems, recv_sems, sums, lands = state
    sums, recv2 = _exchange_wait("exchange_chips_wait_" + tag, "chips", send_sems, recv_sems, sums, lands, after)
    return names, sums, recv2


def _adamw(name, w, g, m, v, dep=None):
    dep_specs, dep_args = _dep_operand(dep)

    def body(w_ref, g_ref, m_ref, v_ref, *rest):
        d_ref, nm_ref, nv_ref = rest[-3:]
        d_ref[...], nm_ref[...], nv_ref[...] = _adam_math(w_ref[...], g_ref[...], m_ref[...], v_ref[...])

    whole = pl.BlockSpec(memory_space=pltpu.VMEM)
    return pl.pallas_call(
        body, name=name, in_specs=[whole] * 4 + dep_specs, out_shape=[jax.ShapeDtypeStruct(w.shape, F32)] * 3,
        compiler_params=_params())(w, g, m, v, *dep_args)


def _adam_math(w, g, m, v):
    nm = ADAM_B1 * m + (1.0 - ADAM_B1) * g
    nv = ADAM_B2 * v + (1.0 - ADAM_B2) * (g * g)
    c1 = 1.0 - ADAM_B1 ** ADAM_STEP
    c2 = 1.0 - ADAM_B2 ** ADAM_STEP
    return -ADAM_LR * ((nm / c1) / (jnp.sqrt(nv / c2) + ADAM_EPS) + ADAM_WD * w), nm, nv


def _adamw_chips(name, sums, recv, w, m, v, transposed, rows=None, dep=None):
    r, c = w.shape
    rows = r if rows is None else rows
    qidx = (2 * lax.axis_index("x") + lax.axis_index("y")).astype(jnp.int32).reshape(1)
    dep_specs, dep_args = _dep_operand(dep)

    def body(q_ref, a_ref, b_ref, w_ref, m_ref, v_ref, *rest):
        g_ref, d_ref, nm_ref, nv_ref = rest[-4:]
        g = (a_ref[...].astype(F32) + b_ref[0].astype(F32)) + (b_ref[1].astype(F32) + b_ref[2].astype(F32))
        if transposed:
            g = g.T
        g_ref[...] = g
        d_ref[...], nm_ref[...], nv_ref[...] = _adam_math(w_ref[...], g, m_ref[...], v_ref[...])

    row = pl.BlockSpec((rows, c), lambda i, q_ref: (i, 0))
    if transposed:
        term_specs = [pl.BlockSpec((None, c, rows), lambda i, q_ref: (q_ref[0], 0, i)),
                      pl.BlockSpec((3, c, rows), lambda i, q_ref: (0, 0, i))]
    else:
        term_specs = [pl.BlockSpec((None, rows, c), lambda i, q_ref: (q_ref[0], i, 0)),
                      pl.BlockSpec((3, rows, c), lambda i, q_ref: (0, i, 0))]
    grid_spec = pltpu.PrefetchScalarGridSpec(
        num_scalar_prefetch=1, grid=(r // rows,), in_specs=term_specs + [row, row, row] + dep_specs,
        out_specs=[row] * 4)
    return pl.pallas_call(
        body, name=name, grid_spec=grid_spec, out_shape=[jax.ShapeDtypeStruct((r, c), F32)] * 4,
        compiler_params=_params(("parallel",)))(qidx, sums, recv, w, m, v, *dep_args)


def _sum_devices(gathered):
    def body(g_ref, o_ref):
        acc = g_ref[0]
        for j in range(1, N_DEV):
            acc = acc + g_ref[j]
        o_ref[...] = acc

    return pl.pallas_call(
        body, name="sum_devices", out_shape=jax.ShapeDtypeStruct(gathered.shape[1:], F32),
        compiler_params=_params())(gathered)


def _rows128(a, rows):
    flat = a.reshape(-1)
    return jnp.pad(flat, (0, rows * 128 - flat.shape[0])).reshape(rows, 128)


def kernel(x, mem, pre_norm, w_in, merge_bias, na_rpb, mem_norm, w_mem_kv, w_branch_a, w_branch_b, w_branch_c, w_out, post_norm, loss_target, m_pre_norm, m_w_in, m_merge_bias, m_na_rpb, m_mem_norm, m_w_mem_kv, m_w_branch_a, m_w_branch_b, m_w_branch_c, m_w_out, m_post_norm, v_pre_norm, v_w_in, v_merge_bias, v_na_rpb, v_mem_norm, v_w_mem_kv, v_w_branch_a, v_w_branch_b, v_w_branch_c, v_w_out, v_post_norm):
    wt_in_s = w_in[0].T.astype(BF16)
    rows_s = jnp.concatenate([w_mem_kv[0], w_out[0]], axis=0).astype(BF16)
    cols_s = jnp.concatenate([w_branch_a[0].T, w_branch_b[0].T, w_branch_c[0].T], axis=0).astype(BF16)
    mb_s = jnp.pad(merge_bias[0], ((0, 5), (0, 0)))
    wt_in = _all_gather(wt_in_s).reshape(N_IN, D_MODEL)

    late_own = [rows_s, cols_s, mb_s]
    late_lands = [jax.ShapeDtypeStruct((N_DEV,) + s.shape, s.dtype) for s in late_own]
    l_send, l_recv, late_own, late_lands, late_token = _exchange_start("gather_late_start", "gather", late_own,
                                                                       late_lands, after=wt_in)
    me = 4 * lax.axis_index("x") + 2 * lax.axis_index("y") + lax.axis_index("c")

    def late_weights(after):
        own, lands = _exchange_wait("gather_late_wait", "gather", l_send, l_recv, late_own, late_lands, after)
        g_rows, g_cols, g_mb = [lax.dynamic_update_slice(land, o[None], (me, 0, 0)) for land, o in zip(lands, own)]
        return (g_mb[:, :3].transpose(1, 0, 2).reshape(3, D_MODEL),
                g_rows[:, :128].reshape(D_MODEL, D_MODEL), g_cols[:, 0:128].reshape(D_MODEL, 512),
                g_cols[:, 128:256].reshape(D_MODEL, 512), g_cols[:, 256:384].reshape(D_MODEL, 512),
                g_rows[:, 128:].reshape(D_MODEL, D_MODEL))

    rs_state = []

    def reduce_start(grads):
        if "wt_in" in grads:
            own, sibling = [a.reshape(N_DEV, SHARD_IN, D_MODEL) for a in grads["wt_in"]]
            state, token = _reduce_scatter_start("w_in", ["w_in"], [own], [sibling])
        else:
            gmb_t = jnp.pad(grads["merge_bias"].reshape(3, N_DEV, 128).transpose(1, 0, 2), ((0, 0), (0, 5), (0, 0)))
            names = ["w_kv", "w_out", "a", "b", "c", "mb"]
            terms = [grads["w_kv"].reshape(N_DEV, 128, D_MODEL), grads["w_out"].reshape(N_DEV, 128, D_MODEL),
                     grads["wt_a"].reshape(N_DEV, 128, 512), grads["wt_b"].reshape(N_DEV, 128, 512),
                     grads["wt_c"].reshape(N_DEV, 128, 512), gmb_t]
            state, token = _reduce_scatter_start("rest", names, terms)
        rs_state.append(state)
        return token

    loss_term, grad_x, grads = _local_step(
        x[0], mem[0], loss_target[0], pre_norm, mem_norm, post_norm, na_rpb[0], wt_in, late_weights,
        dep_in=late_token, reduce_start=reduce_start)

    small = jnp.concatenate([_rows128(grads["pre_norm"], 8), _rows128(grads["mem_norm"], 8),
                             _rows128(grads["post_norm"], 8), _rows128(grads["na_rpb"], 32),
                             _rows128(loss_term, 8)], axis=0)
    s_send, s_recv, s_own, s_land, s_token = _exchange_start(
        "gather_small_start", "gather", [small], [jax.ShapeDtypeStruct((N_DEV,) + small.shape, F32)])
    grad = {}
    weights = {
        "pre_norm": (pre_norm, m_pre_norm, v_pre_norm), "w_in": (w_in, m_w_in, v_w_in),
        "merge_bias": (merge_bias, m_merge_bias, v_merge_bias), "na_rpb": (na_rpb, m_na_rpb, v_na_rpb),
        "mem_norm": (mem_norm, m_mem_norm, v_mem_norm), "w_mem_kv": (w_mem_kv, m_w_mem_kv, v_w_mem_kv),
        "w_branch_a": (w_branch_a, m_w_branch_a, v_w_branch_a), "w_branch_b": (w_branch_b, m_w_branch_b, v_w_branch_b),
        "w_branch_c": (w_branch_c, m_w_branch_c, v_w_branch_c), "w_out": (w_out, m_w_out, v_w_out),
        "post_norm": (post_norm, m_post_norm, v_post_norm)}
    order = ["pre_norm", "w_in", "merge_bias", "na_rpb", "mem_norm", "w_mem_kv", "w_branch_a", "w_branch_b",
             "w_branch_c", "w_out", "post_norm"]
    delta, new_m, new_v = {}, {}, {}

    def update(n, dep=None):
        w, m, v = weights[n]
        shape = w.shape
        two_d = (-1, shape[-1])
        dl, nm, nv = _adamw("adamw_" + n, w.reshape(two_d), grad[n].reshape(two_d), m.reshape(two_d),
                            v.reshape(two_d), dep)
        delta[n], new_m[n], new_v[n] = dl.reshape(shape), nm.reshape(shape), nv.reshape(shape)
        return dl

    def update_sharded(n, sums, recv, transposed, rows=None, dep=None):
        w, m, v = weights[n]
        g, dl, nm, nv = _adamw_chips("adamw_" + n, sums, recv, w[0], m[0], v[0], transposed, rows, dep)
        grad[n], delta[n], new_m[n], new_v[n] = g[None], dl[None], nm[None], nv[None]
        return dl

    _, sums, recv2 = _reduce_scatter_wait(rs_state[0], s_token)
    dep = None
    for i, (n, transposed) in enumerate((("w_mem_kv", False), ("w_out", False), ("w_branch_a", True),
                                         ("w_branch_b", True), ("w_branch_c", True))):
        dep = update_sharded(n, sums[i], recv2[i], transposed, dep=dep)
    grad["merge_bias"] = _add_chips("add_chips_mb", sums[5], recv2[5], 8)[:3][None]
    update("merge_bias")
    s_own, s_land = _exchange_wait("gather_small_wait", "gather", s_send, s_recv, s_own, s_land, dep)
    total = _sum_devices(lax.dynamic_update_slice(s_land[0], s_own[0][None], (me, 0, 0)))
    loss = total[56, 0]
    grad.update({"pre_norm": total[0:8].reshape(1, D_MODEL), "mem_norm": total[8:16].reshape(1, D_MODEL),
                 "post_norm": total[16:24].reshape(1, D_MODEL),
                 "na_rpb": total[24:56].reshape(-1)[:8 * 15 * 31].reshape(1, 8, 15, 31)})
    dep = None
    for n in ("pre_norm", "na_rpb", "mem_norm", "post_norm"):
        dep = update(n, dep)
    _, sums_in, recv_in = _reduce_scatter_wait(rs_state[1], dep)
    update_sharded("w_in", sums_in[0], recv_in[0], True, 256)

    return (loss, grad_x[None], *[grad[n] for n in order], *[delta[n] for n in order],
            *[new_m[n] for n in order], *[new_v[n] for n in order])
```

```python
import functools

import numpy as np
import jax
import jax.numpy as jnp
from jax import lax
from jax.experimental import pallas as pl
from jax.experimental.pallas import tpu as pltpu

F32 = jnp.float32
BF16 = jnp.bfloat16

SEQ = 2048
D_MODEL = 1024
N_IN = 11264
N_DEV = 8
SHARD_IN = N_IN // N_DEV
HEAD_DIM = 64
GRID_W = 64
NA_ROWS = 8
MEM_LEN = 256
DILATIONS = (1, 4, 16)
REACH = 64
ROPE_THETA = 500000.0
ROPE_DIM = 16
EPS = 1e-6
NEG = -1e30
ADAM_LR = 0.001
ADAM_B1 = 0.9
ADAM_B2 = 0.999
ADAM_EPS = 1e-08
ADAM_WD = 0.01
ADAM_STEP = 10

VMEM_LIMIT_BYTES = 56 * 1024 * 1024
MESH_ID = pl.DeviceIdType.MESH

NN = (((1,), (0,)), ((), ()))
NT = (((1,), (1,)), ((), ()))
TN = (((0,), (0,)), ((), ()))


def _params(sem=None):
    return pltpu.CompilerParams(dimension_semantics=sem, vmem_limit_bytes=VMEM_LIMIT_BYTES)


def _iota(shape, dim):
    return lax.broadcasted_iota(jnp.int32, shape, dim)


def _sigmoid(x):
    return 1.0 / (1.0 + jnp.exp(-x))


def _rope_tables():
    half = ROPE_DIM // 2
    inv = (ROPE_THETA ** (-np.arange(half, dtype=np.float64) * 2.0 / ROPE_DIM)).astype(np.float32)
    pos = np.arange(SEQ, dtype=np.float32)
    ang = pos[:, None] * inv[None, :]
    cos, sin = np.cos(ang), np.sin(ang)
    zeros = np.zeros_like(cos)
    rest = HEAD_DIM - ROPE_DIM
    c64 = np.concatenate([cos, cos, np.ones((SEQ, rest), np.float32)], axis=1)
    s1 = np.concatenate([zeros, sin, np.zeros((SEQ, rest), np.float32)], axis=1)
    s2 = np.concatenate([-sin, zeros, np.zeros((SEQ, rest), np.float32)], axis=1)

    def fold(t, d):
        return t.reshape(SEQ // d, d, t.shape[1]).transpose(1, 0, 2).reshape(SEQ, t.shape[1])

    tabs = [np.stack([np.tile(fold(t, d), (1, 2)) for t in (c64, s1, s2)], axis=0) for d in DILATIONS]
    return jnp.asarray(np.stack(tabs, axis=0), dtype=F32)


def _rope(a, c, s1, s2):
    return a * c + pltpu.roll(a, 8, 1) * s1 + pltpu.roll(a, 120, 1) * s2


def _rope_t(a, c, s1, s2):
    return a * c + pltpu.roll(a * s1, 120, 1) + pltpu.roll(a * s2, 8, 1)


def _perm_of_block(j):
    return jnp.where(j < 3, 0, jnp.where(j < 6, 1, jnp.where(j < 9, 2, 0)))


def _mm(name, a, b, out_shape, out_dtype, grid, a_spec, b_spec, o_spec, acc_shape, dims, k_axis, nk):
    def body(a_ref, b_ref, o_ref, acc_ref):
        k = pl.program_id(k_axis)

        @pl.when(k == 0)
        def _():
            acc_ref[...] = jnp.zeros(acc_shape, F32)

        acc_ref[...] += lax.dot_general(a_ref[...], b_ref[...], dims, preferred_element_type=F32)

        @pl.when(k == nk - 1)
        def _():
            o_ref[...] = acc_ref[...].astype(out_dtype)

    sem = tuple("arbitrary" if ax == k_axis else "parallel" for ax in range(len(grid)))
    return pl.pallas_call(
        body, name=name, grid=grid, in_specs=[a_spec, b_spec], out_specs=o_spec,
        out_shape=jax.ShapeDtypeStruct(out_shape, out_dtype),
        scratch_shapes=[pltpu.VMEM(acc_shape, F32)], compiler_params=_params(sem))(a, b)


def _mm_simple(name, a, b, dims, out_dtype, tm, tn, tk):
    if dims is NN:
        m, kk = a.shape
        n = b.shape[1]
        a_spec = pl.BlockSpec((tm, tk), lambda i, j, k: (i, k))
        b_spec = pl.BlockSpec((tk, tn), lambda i, j, k: (k, j))
    elif dims is NT:
        m, kk = a.shape
        n = b.shape[0]
        a_spec = pl.BlockSpec((tm, tk), lambda i, j, k: (i, k))
        b_spec = pl.BlockSpec((tn, tk), lambda i, j, k: (j, k))
    else:
        kk, m = a.shape
        n = b.shape[1]
        a_spec = pl.BlockSpec((tk, tm), lambda i, j, k: (k, i))
        b_spec = pl.BlockSpec((tk, tn), lambda i, j, k: (k, j))
    grid = (m // tm, n // tn, kk // tk)
    o_spec = pl.BlockSpec((tm, tn), lambda i, j, k: (i, j))
    return _mm(name, a, b, (m, n), out_dtype, grid, a_spec, b_spec, o_spec, (tm, tn), dims, 2, kk // tk)


def _rmsnorm_fwd(name, x, gain, rows):
    n, d = x.shape

    def body(x_ref, g_ref, o_ref):
        xv = x_ref[...]
        rstd = lax.rsqrt(jnp.mean(xv * xv, axis=1, keepdims=True) + EPS)
        o_ref[...] = (xv * rstd * g_ref[...]).astype(BF16)

    return pl.pallas_call(
        body, name=name, grid=(n // rows,),
        in_specs=[pl.BlockSpec((rows, d), lambda i: (i, 0)), pl.BlockSpec((1, d), lambda i: (0, 0))],
        out_specs=pl.BlockSpec((rows, d), lambda i: (i, 0)),
        out_shape=jax.ShapeDtypeStruct((n, d), BF16), compiler_params=_params(("parallel",)))(x, gain)


def _folded_rows(first, rows, d):
    if d == 1:
        return pl.ds(pl.multiple_of(first, rows), rows)
    mlen = SEQ // d
    return pl.ds((first % mlen) * d + first // mlen, rows, stride=d)


def _prenorm_fold(x, gain, dep=None):
    rows = 128
    nchunk = D_MODEL // 128
    dep_specs, dep_args = _dep_operand(dep)

    def body(*refs):
        x_refs, g_ref, hs_ref, hst_ref = refs[:nchunk], refs[nchunk], refs[-2], refs[-1]
        first = pl.program_id(0) * rows
        for p, d in enumerate(DILATIONS):
            idx = _folded_rows(first, rows, d)
            xv = jnp.concatenate([r[idx, :] for r in x_refs], axis=1)
            rstd = lax.rsqrt(jnp.mean(xv * xv, axis=1, keepdims=True) + EPS)
            h = xv * rstd * g_ref[...]
            hs_ref[p] = h.astype(BF16)
            hst_ref[p] = h.T.astype(BF16)

    x_specs = [pl.BlockSpec((SEQ, 128), functools.partial(lambda c, i: (0, c), c)) for c in range(nchunk)]
    return pl.pallas_call(
        body, name="prenorm", grid=(SEQ // rows,),
        in_specs=x_specs + [pl.BlockSpec((1, D_MODEL), lambda i: (0, 0))] + dep_specs,
        out_specs=[pl.BlockSpec((3, rows, D_MODEL), lambda i: (0, i, 0)),
                   pl.BlockSpec((3, D_MODEL, rows), lambda i: (0, 0, i))],
        out_shape=[jax.ShapeDtypeStruct((3, SEQ, D_MODEL), BF16), jax.ShapeDtypeStruct((3, D_MODEL, SEQ), BF16)],
        compiler_params=_params(("parallel",)))(*([x] * nchunk), gain, *dep_args)


def _prenorm_bwd(x, gain, dh, dout):
    rows = 256

    def body(x_ref, g_ref, a_ref, do_ref, dx_ref, gg_ref):
        xv = x_ref[...]
        rstd = lax.rsqrt(jnp.mean(xv * xv, axis=1, keepdims=True) + EPS)
        xn = xv * rstd
        dh = jnp.concatenate([a_ref[c] for c in range(D_MODEL // 128)], axis=1)
        gdh = dh * g_ref[...]
        dx_ref[...] = rstd * (gdh - xn * jnp.mean(gdh * xn, axis=1, keepdims=True)) + do_ref[...]

        @pl.when(pl.program_id(0) == 0)
        def _():
            gg_ref[...] = jnp.zeros((1, D_MODEL), F32)

        gg_ref[...] += jnp.sum(dh * xn, axis=0, keepdims=True)

    row = pl.BlockSpec((rows, D_MODEL), lambda i: (i, 0))
    vec = pl.BlockSpec((1, D_MODEL), lambda i: (0, 0))
    return pl.pallas_call(
        body, name="prenorm_bwd", grid=(SEQ // rows,),
        in_specs=[row, vec, pl.BlockSpec((D_MODEL // 128, rows, 128), lambda i: (0, i, 0)), row], out_specs=[row, vec],
        out_shape=[jax.ShapeDtypeStruct((SEQ, D_MODEL), F32), jax.ShapeDtypeStruct((1, D_MODEL), F32)],
        compiler_params=_params(("arbitrary",)))(x, gain, dh, dout)


def _memnorm_bwd(mem, dmemn, dep=None):
    dep_specs, dep_args = _dep_operand(dep)

    def body(m_ref, d_ref, *rest):
        mv = m_ref[...]
        rstd = lax.rsqrt(jnp.mean(mv * mv, axis=1, keepdims=True) + EPS)
        rest[-1][...] = jnp.sum(d_ref[...] * mv * rstd, axis=0, keepdims=True)

    whole = pl.BlockSpec(memory_space=pltpu.VMEM)
    return pl.pallas_call(
        body, name="memnorm_bwd", in_specs=[whole, whole] + dep_specs,
        out_shape=jax.ShapeDtypeStruct((1, D_MODEL), F32), compiler_params=_params())(mem, dmemn, *dep_args)


def _dep_operand(dep):
    return ([], []) if dep is None else ([pl.BlockSpec(memory_space=pl.ANY)], [dep])


def _in_proj(hs, wt, tabs, dep=None):
    tm, tn = 512, 512
    dep_specs, dep_args = _dep_operand(dep)

    def body(h_ref, w_ref, t_ref, *rest):
        o_ref = rest[-1]
        j = pl.program_id(0)
        is_rope = jnp.logical_and(j < 9, j % 3 != 2)
        row_slices = [slice(r * tm, (r + 1) * tm) for r in range(SEQ // tm)]

        def product(rs):
            return lax.dot_general(h_ref[rs, :], w_ref[...], NT, preferred_element_type=F32)

        @pl.when(is_rope)
        def _():
            for rs in row_slices:
                acc = product(rs)
                c, s1, s2 = t_ref[0, rs, :], t_ref[1, rs, :], t_ref[2, rs, :]
                for q in range(tn // 128):
                    a = acc[:, q * 128:(q + 1) * 128]
                    o_ref[rs, q * 128:(q + 1) * 128] = _rope(a, c, s1, s2).astype(BF16)

        @pl.when(jnp.logical_not(is_rope))
        def _():
            for rs in row_slices:
                o_ref[rs, :] = product(rs).astype(BF16)

    return pl.pallas_call(
        body, name="in_proj", grid=(N_IN // tn,),
        in_specs=[pl.BlockSpec((None, SEQ, D_MODEL), lambda j: (_perm_of_block(j), 0, 0)),
                  pl.BlockSpec((tn, D_MODEL), lambda j: (j, 0)),
                  pl.BlockSpec((None, 3, SEQ, 128), lambda j: (_perm_of_block(j), 0, 0, 0))] + dep_specs,
        out_specs=pl.BlockSpec((SEQ, tn), lambda j: (0, j)),
        out_shape=jax.ShapeDtypeStruct((SEQ, N_IN), BF16),
        compiler_params=_params(("parallel",)))(hs, wt, tabs, *dep_args)


def _piece_blocks(pieces):
    return [(a, h * 512) for a, p in enumerate(pieces) for h in range(p.shape[1] // 512)]


def _block_fetch(piece_refs, blocks, buf, sem):
    def start(block, slot):
        for b, (a, col) in enumerate(blocks):
            @pl.when(block == b)
            def _():
                pltpu.make_async_copy(piece_refs[a].at[:, pl.ds(col, 512)], buf.at[slot], sem.at[slot]).start()

    def wait(slot):
        pltpu.make_async_copy(piece_refs[0].at[:, pl.ds(0, 512)], buf.at[slot], sem.at[slot]).wait()

    return start, wait


def _in_proj_dw(pieces, hst, dep=None):
    tn = 512
    blocks = _piece_blocks(pieces)
    nblk = len(blocks)
    npc = len(pieces)
    dep_specs, dep_args = _dep_operand(dep)

    def body(h_ref, *rest):
        piece_refs = rest[:npc]
        o_ref, mirror, buf, sem, out_buf, send_sems, recv_sem = rest[-7:]
        j = pl.program_id(0)
        slot = j % 2
        start, wait = _block_fetch(piece_refs, blocks, buf, sem)
        x, y, c = _place()

        def to_sibling(step, slot_):
            return pltpu.make_async_remote_copy(
                src_ref=out_buf.at[slot_], dst_ref=mirror.at[pl.ds(pl.multiple_of(step * tn, tn), tn)],
                send_sem=send_sems.at[slot_], recv_sem=recv_sem, device_id=(x, y, 1 - c), device_id_type=MESH_ID)

        @pl.when(j == 0)
        def _():
            start(j, slot)

        wait(slot)

        @pl.when(j + 1 < nblk)
        def _():
            start(j + 1, 1 - slot)

        acc = jnp.dot(h_ref[...], buf[slot], preferred_element_type=F32)
        block = acc.T.astype(BF16)
        o_ref[...] = block

        @pl.when(j >= 2)
        def _():
            to_sibling(j - 2, slot).wait_send()

        out_buf[slot] = block
        to_sibling(j, slot).start()

        @pl.when(j == nblk - 1)
        def _():
            to_sibling(j - 1, 1 - slot).wait_send()
            to_sibling(j, slot).wait_send()
            pltpu.make_async_remote_copy(src_ref=mirror, dst_ref=mirror, send_sem=send_sems.at[0], recv_sem=recv_sem,
                                         device_id=(x, y, 1 - c), device_id_type=MESH_ID).wait_recv()

    return pl.pallas_call(
        body, name="in_proj_dw", grid=(nblk,),
        in_specs=[pl.BlockSpec((None, D_MODEL, SEQ), lambda j: (_perm_of_block(j), 0, 0))] + [ANY] * npc + dep_specs,
        out_specs=[pl.BlockSpec((tn, D_MODEL), lambda j: (j, 0)), ANY],
        out_shape=[jax.ShapeDtypeStruct((N_IN, D_MODEL), BF16), jax.ShapeDtypeStruct((N_IN, D_MODEL), BF16)],
        scratch_shapes=[pltpu.VMEM((2, SEQ, tn), BF16), pltpu.SemaphoreType.DMA((2,)),
                        pltpu.VMEM((2, tn, D_MODEL), BF16), pltpu.SemaphoreType.DMA((2,)), pltpu.SemaphoreType.DMA],
        compiler_params=_params(("arbitrary",)))(hst, *pieces, *dep_args)


def _in_proj_dh(pieces, wt, dep=None):
    tk = 512
    blocks = _piece_blocks(pieces)
    nblk = len(blocks)
    npc = len(pieces)
    nchunk = D_MODEL // 128

    def col(s):
        return jnp.where(s < 3, s, jnp.where(s < 16, s + 6, s - 13))

    dep_specs, dep_args = _dep_operand(dep)

    def body(w_ref, *rest):
        piece_refs = rest[:npc]
        o_ref, acc_ref, buf, sem = rest[-4:]
        s = pl.program_id(0)
        slot = s % 2
        start, wait = _block_fetch(piece_refs, blocks, buf, sem)

        @pl.when(s == 0)
        def _():
            start(col(s), slot)

        wait(slot)

        @pl.when(s + 1 < nblk)
        def _():
            start(col(s + 1), 1 - slot)

        row_slices = [slice(r * 512, (r + 1) * 512) for r in range(SEQ // 512)]

        def product(rs):
            return jnp.dot(buf[slot, rs, :], w_ref[...], preferred_element_type=F32)

        def accumulate(cond, to_out, init):
            @pl.when(cond)
            def _():
                for rs in row_slices:
                    prod = product(rs)
                    if not to_out:
                        if init:
                            acc_ref[rs, :] = prod
                        else:
                            acc_ref[rs, :] += prod
                        continue
                    for c in range(nchunk):
                        if init:
                            o_ref[c, rs, :] = prod[:, c * 128:(c + 1) * 128]
                        else:
                            o_ref[c, rs, :] += prod[:, c * 128:(c + 1) * 128]

        accumulate(s == 0, True, True)
        accumulate(jnp.logical_and(s > 0, s < 16), True, False)
        accumulate(jnp.logical_or(s == 16, s == 19), False, True)
        accumulate(jnp.logical_and(s > 16, s != 19), False, False)
        for last, d in ((18, 4), (21, 16)):
            @pl.when(s == last)
            def _():
                mlen = SEQ // d
                for r in range(d):
                    for c in range(nchunk):
                        o_ref[c, pl.ds(r, mlen, stride=d), :] += acc_ref[r * mlen:(r + 1) * mlen,
                                                                         c * 128:(c + 1) * 128]

    return pl.pallas_call(
        body, name="in_proj_dh", grid=(nblk,),
        in_specs=[pl.BlockSpec((tk, D_MODEL), lambda s: (col(s), 0))] + [ANY] * npc + dep_specs,
        out_specs=pl.BlockSpec((nchunk, SEQ, 128), lambda s: (0, 0, 0)),
        out_shape=jax.ShapeDtypeStruct((nchunk, SEQ, 128), F32),
        scratch_shapes=[pltpu.VMEM((SEQ, D_MODEL), F32), pltpu.VMEM((2, SEQ, tk), BF16),
                        pltpu.SemaphoreType.DMA((2,))],
        compiler_params=_params(("arbitrary",)))(wt, *pieces, *dep_args)


def _head_lanes(lanes, hh):
    return lanes >= 64 if hh == 1 else lanes < 64


def _head_rows(x, lanes, hh, pair):
    if not pair:
        return jnp.max(x, axis=1, keepdims=True)
    return jnp.max(jnp.where(_head_lanes(lanes, hh), x, -jnp.inf), axis=1, keepdims=True)


def _mask_head(x, lanes, hh, pair, scale=1.0):
    if not pair:
        return x
    xf = x.astype(F32) if scale == 1.0 else x.astype(F32) * scale
    return jnp.where(_head_lanes(lanes, hh), xf, 0.0).astype(BF16)


def _window(mode, qi, tq, mlen, tk):
    if mode == "dil":
        q0 = qi * tq
        seg = (q0 // mlen) * mlen
        ks = jnp.clip(q0 - REACH, seg, seg + mlen - tk)
        return pl.multiple_of(ks, 64)
    if mode == "na":
        r_start = jnp.clip(qi - NA_ROWS // 2, 0, SEQ // GRID_W - NA_ROWS)
        return pl.multiple_of(r_start * GRID_W, 64)
    return 0


def _band_mask(qi, tq, tk, ks):
    qpos = qi * tq + _iota((tq, tk), 0)
    kpos = ks + _iota((tq, tk), 1)
    return jnp.where(jnp.abs(qpos - kpos) <= REACH, 0.0, NEG).astype(F32)


def _stack_heads(x, lanes, pair, scale=1.0):
    if not pair:
        return x
    return jnp.concatenate([_mask_head(x, lanes, hh, pair, scale) for hh in range(2)], axis=0)


def _stack_rows(x, lanes, pair):
    if not pair:
        return _head_rows(x, lanes, 0, pair)
    return jnp.concatenate([_head_rows(x, lanes, hh, pair) for hh in range(2)], axis=0)


def _unstack_heads(x, lanes, pair, tq):
    if not pair:
        return x
    return jnp.where(lanes < 64, x[:tq], x[tq:])


def _scores(mode, qst, k, sscale, band, qi, bias_ref, pair):
    s = lax.dot_general(qst, k, NT, preferred_element_type=F32)
    if sscale != 1.0:
        s = s * sscale
    if mode == "dil":
        s = s + jnp.concatenate([band, band], axis=0)
    elif mode == "na":
        off = qi - jnp.clip(qi - NA_ROWS // 2, 0, SEQ // GRID_W - NA_ROWS)
        s = s + jnp.concatenate([bias_ref[0, off], bias_ref[1, off]], axis=0)
    return s


def _attn_cfg(mode, d):
    if mode == "dil":
        mlen = SEQ // d
        return dict(pair=True, tq=128, tk=min(256, mlen), mlen=mlen, lk=SEQ, scale=HEAD_DIM ** -0.5, units=4,
                    nsub=ATTN_SUBTILES)
    if mode == "na":
        return dict(pair=True, tq=GRID_W, tk=NA_ROWS * GRID_W, mlen=SEQ, lk=SEQ, scale=HEAD_DIM ** -0.5, units=4,
                    nsub=ATTN_SUBTILES)
    return dict(pair=False, tq=128, tk=MEM_LEN, mlen=SEQ, lk=MEM_LEN, scale=128 ** -0.5, units=4,
                nsub=ATTN_SUBTILES)


ATTN_SUBTILES = 16


def _attn_fwd(name, mode, q_arr, k_arr, v_arr, qcol, kcol, vcol, d=1, bias=None):
    cfg = _attn_cfg(mode, d)
    pair, tq, tk, mlen, lk, scale = cfg["pair"], cfg["tq"], cfg["tk"], cfg["mlen"], cfg["lk"], cfg["scale"]
    qscale, sscale = (scale, 1.0) if pair else (1.0, scale)
    nsub = cfg["nsub"]
    rows = nsub * tq

    def body(*refs):
        if mode == "na":
            q_ref, k_ref, v_ref, bias_ref, o_ref, l_ref = refs
        else:
            q_ref, k_ref, v_ref, o_ref, l_ref = refs
            bias_ref = None
        lanes = _iota((tq, 128), 1)
        qis = [pl.program_id(1) * nsub + sub for sub in range(nsub)]
        kss = [_window(mode, qi, tq, mlen, tk) for qi in qis]
        vs = [v_ref[pl.ds(ks, tk), :] for ks in kss]
        bands = [_band_mask(qi, tq, tk, ks) if mode == "dil" else None for qi, ks in zip(qis, kss)]
        ss = []
        for sub in range(nsub):
            qst = _stack_heads(q_ref[sub * tq:(sub + 1) * tq, :], lanes, pair, qscale)
            k = k_ref[pl.ds(kss[sub], tk), :]
            ss.append(_scores(mode, qst, k, sscale, bands[sub], qis[sub], bias_ref, pair))
        ms = [jnp.max(s_, axis=1, keepdims=True) for s_ in ss]
        ps = [jnp.exp(s_ - m) for s_, m in zip(ss, ms)]
        ls = [jnp.sum(p, axis=1, keepdims=True) for p in ps]
        os_ = [jnp.dot(p.astype(BF16), v, preferred_element_type=F32) for p, v in zip(ps, vs)]
        for sub in range(nsub):
            out = _unstack_heads(os_[sub] / ls[sub], lanes, pair, tq)
            lse = ms[sub] + jnp.log(ls[sub])
            lse = _unstack_heads(jnp.broadcast_to(lse, (lse.shape[0], 128)), lanes, pair, tq)
            dst = _folded_rows(qis[sub] * tq, tq, d) if mode == "dil" else slice(sub * tq, (sub + 1) * tq)
            o_ref[dst, :] = out
            l_ref[dst, :] = lse

    in_specs = [pl.BlockSpec((rows, 128), lambda u, i: (i, qcol + u)),
                pl.BlockSpec((lk, 128), lambda u, i: (0, kcol + u)),
                pl.BlockSpec((lk, 128), lambda u, i: (0, vcol + u))]
    args = [q_arr, k_arr, v_arr]
    if mode == "na":
        in_specs.append(pl.BlockSpec((2, NA_ROWS, GRID_W, NA_ROWS * GRID_W), lambda u, i: (u, 0, 0, 0)))
        args.append(bias)
    if mode == "dil":
        out_spec = pl.BlockSpec((SEQ, 128), lambda u, i: (0, u))
    else:
        out_spec = pl.BlockSpec((rows, 128), lambda u, i: (i, u))
    return pl.pallas_call(
        body, name=name, grid=(cfg["units"], SEQ // rows), in_specs=in_specs, out_specs=[out_spec, out_spec],
        out_shape=[jax.ShapeDtypeStruct((SEQ, 512), F32), jax.ShapeDtypeStruct((SEQ, 512), F32)],
        compiler_params=_params(("parallel", "arbitrary")))(*args)


def _attn_bwd(name, mode, q_arr, k_arr, v_arr, qcol, kcol, vcol, do, lse, dp=None, o=None, d=1, bias=None,
              tabs=None):
    cfg = _attn_cfg(mode, d)
    pair, tq, tk, mlen, lk, scale = cfg["pair"], cfg["tq"], cfg["tk"], cfg["mlen"], cfg["lk"], cfg["scale"]
    qscale, sscale = (scale, 1.0) if pair else (1.0, scale)
    nsub = cfg["nsub"]
    rows = nsub * tq
    nq = SEQ // rows
    kv_dtype = F32 if mode == "mem" else BF16

    def body(*refs):
        refs = list(refs)
        q_ref, k_ref, v_ref, do_ref, l_ref = refs[:5]
        rest = refs[5:]
        bias_ref = tq_ref = tk_ref = db_ref = None
        if mode == "dil":
            dp_ref, tq_ref, tk_ref, dq_ref, dk_ref, dv_ref, dk_acc, dv_acc = rest
        elif mode == "na":
            o_ref, bias_ref, dq_ref, dk_ref, dv_ref, db_ref, dk_acc, dv_acc = rest
        else:
            o_ref, dq_ref, dk_ref, dv_ref, dk_acc, dv_acc = rest
        step = pl.program_id(1)

        @pl.when(step == 0)
        def _():
            dk_acc[...] = jnp.zeros((lk, 128), F32)
            dv_acc[...] = jnp.zeros((lk, 128), F32)
            if mode == "na":
                db_ref[...] = jnp.zeros(db_ref.shape, F32)

        lanes = _iota((tq, 128), 1)
        qis = [step * nsub + sub for sub in range(nsub)]
        sls = [slice(sub * tq, (sub + 1) * tq) for sub in range(nsub)]
        kss = [_window(mode, qi, tq, mlen, tk) for qi in qis]
        ks_ = [k_ref[pl.ds(ks, tk), :] for ks in kss]
        vs = [v_ref[pl.ds(ks, tk), :] for ks in kss]
        qsts, dosts, lses, dphs = [], [], [], []
        for sub in range(nsub):
            if mode == "dil":
                src = _folded_rows(qis[sub] * tq, tq, d)
                dov = do_ref[src, :].astype(BF16)
                lsev = l_ref[src, :]
                dphs.append(_stack_rows(dp_ref[src, :], lanes, pair))
            else:
                dov = do_ref[sls[sub], :]
                lsev = l_ref[sls[sub], :]
                dpv = dov.astype(F32) * o_ref[sls[sub], :]
                if pair:
                    dphs.append(jnp.concatenate(
                        [jnp.sum(jnp.where(_head_lanes(lanes, hh), dpv, 0.0), axis=1, keepdims=True)
                         for hh in range(2)], axis=0))
                else:
                    dphs.append(jnp.sum(dpv, axis=1, keepdims=True))
            qsts.append(_stack_heads(q_ref[sls[sub], :], lanes, pair, qscale))
            dosts.append(_stack_heads(dov, lanes, pair))
            lses.append(_stack_rows(lsev, lanes, pair))
        bands = [_band_mask(qi, tq, tk, ks) if mode == "dil" else None for qi, ks in zip(qis, kss)]
        ss = [_scores(mode, qsts[sub], ks_[sub], sscale, bands[sub], qis[sub], bias_ref, pair) for sub in range(nsub)]
        dpms = [lax.dot_general(dosts[sub], vs[sub], NT, preferred_element_type=F32) for sub in range(nsub)]
        ps = [jnp.exp(s_ - lse) for s_, lse in zip(ss, lses)]
        dss = [p * (dpm - dph) for p, dpm, dph in zip(ps, dpms, dphs)]
        if mode == "na":
            for sub, ds in enumerate(dss):
                off = qis[sub] - jnp.clip(qis[sub] - NA_ROWS // 2, 0, SEQ // GRID_W - NA_ROWS)
                db_ref[0, off] += ds[:tq]
                db_ref[1, off] += ds[tq:]
        dsbs = [ds.astype(BF16) for ds in dss]
        dvs = [lax.dot_general(p.astype(BF16), dosts[sub], TN, preferred_element_type=F32)
               for sub, p in enumerate(ps)]
        dqs = [jnp.dot(dsb, ks_[sub], preferred_element_type=F32) * scale for sub, dsb in enumerate(dsbs)]
        dks = [lax.dot_general(dsb, qsts[sub], TN, preferred_element_type=F32) for sub, dsb in enumerate(dsbs)]
        for sub in range(nsub):
            sl = sls[sub]
            dq = _unstack_heads(dqs[sub], lanes, pair, tq)
            if mode == "dil":
                dq = _rope_t(dq, tq_ref[0, sl, :], tq_ref[1, sl, :], tq_ref[2, sl, :])
            dq_ref[sl, :] = dq.astype(BF16)
            dk_acc[pl.ds(kss[sub], tk), :] += dks[sub] if pair else dks[sub] * scale
            dv_acc[pl.ds(kss[sub], tk), :] += dvs[sub]

        @pl.when(step == nq - 1)
        def _():
            dkv = dk_acc[...]
            if mode == "dil":
                dkv = _rope_t(dkv, tk_ref[0], tk_ref[1], tk_ref[2])
            dk_ref[...] = dkv.astype(kv_dtype)
            dv_ref[...] = dv_acc[...].astype(kv_dtype)

    q_spec = pl.BlockSpec((rows, 128), lambda u, i: (i, qcol + u))
    row_spec = pl.BlockSpec((rows, 128), lambda u, i: (i, u))
    kv_out = pl.BlockSpec((lk, 128), lambda u, i: (0, u))
    whole = pl.BlockSpec((SEQ, 128), lambda u, i: (0, u))
    nat_spec = whole if mode == "dil" else row_spec
    in_specs = [q_spec,
                pl.BlockSpec((lk, 128), lambda u, i: (0, kcol + u)),
                pl.BlockSpec((lk, 128), lambda u, i: (0, vcol + u)),
                nat_spec, nat_spec]
    args = [q_arr, k_arr, v_arr, do, lse]
    out_specs = [row_spec, kv_out, kv_out]
    out_shape = [jax.ShapeDtypeStruct((SEQ, 512), BF16), jax.ShapeDtypeStruct((lk, 512), kv_dtype),
                 jax.ShapeDtypeStruct((lk, 512), kv_dtype)]
    if mode == "dil":
        in_specs += [whole, pl.BlockSpec((3, rows, 128), lambda u, i: (0, i, 0)),
                     pl.BlockSpec((3, SEQ, 128), lambda u, i: (0, 0, 0))]
        args += [dp, tabs, tabs]
    elif mode == "na":
        b_spec = pl.BlockSpec((2, NA_ROWS, GRID_W, NA_ROWS * GRID_W), lambda u, i: (u, 0, 0, 0))
        in_specs += [row_spec, b_spec]
        args += [o, bias]
        out_specs.append(b_spec)
        out_shape.append(jax.ShapeDtypeStruct((8, NA_ROWS, GRID_W, NA_ROWS * GRID_W), F32))
    else:
        in_specs.append(row_spec)
        args.append(o)
    return pl.pallas_call(
        body, name=name, grid=(cfg["units"], nq), in_specs=in_specs, out_specs=out_specs, out_shape=out_shape,
        scratch_shapes=[pltpu.VMEM((lk, 128), F32), pltpu.VMEM((lk, 128), F32)],
        compiler_params=_params(("parallel", "arbitrary")))(*args)


def _na_geometry():
    qc = _iota((GRID_W, 128), 0)
    lane = _iota((GRID_W, 128), 1)
    kc = lane & 63
    c_start = jnp.clip(qc - 8, 0, GRID_W - 16)
    valid = jnp.logical_and(kc >= c_start, kc < c_start + 16)
    return lane, valid


def _na_bias(rpb_rows):
    def body(r_ref, o_ref, t_ref):
        lane, valid = _na_geometry()
        for dd in range(14):
            row_a = jnp.broadcast_to(r_ref[dd:dd + 1, :], (GRID_W, 128))
            row_b = jnp.broadcast_to(r_ref[dd + 1:dd + 2, :], (GRID_W, 128))
            both = jnp.where(lane < 64, row_a, pltpu.roll(row_b, 64, 1))
            t = pltpu.roll(both, 128 - 15, 1, stride=1, stride_axis=0)
            t_ref[dd] = jnp.where(valid, t, NEG)
        for off in range(NA_ROWS):
            for p in range(4):
                o_ref[off, :, p * 128:(p + 1) * 128] = t_ref[2 * p - off + 7]

    return pl.pallas_call(
        body, name="na_bias", grid=(8,),
        in_specs=[pl.BlockSpec((None, 16, 128), lambda h: (h, 0, 0))],
        out_specs=pl.BlockSpec((None, NA_ROWS, GRID_W, NA_ROWS * GRID_W), lambda h: (h, 0, 0, 0)),
        out_shape=jax.ShapeDtypeStruct((8, NA_ROWS, GRID_W, NA_ROWS * GRID_W), F32),
        scratch_shapes=[pltpu.VMEM((14, GRID_W, 128), F32)],
        compiler_params=_params(("parallel",)))(rpb_rows)


def _na_bias_bwd(dbias, dep=None):
    dep_specs, dep_args = _dep_operand(dep)

    def body(d_ref, *rest):
        o_ref = rest[-1]
        lane, valid = _na_geometry()
        reverse = (_iota((GRID_W, GRID_W), 0) + _iota((GRID_W, GRID_W), 1) == GRID_W - 1).astype(F32)
        o_ref[...] = jnp.zeros((16, 128), F32)
        for dd in range(14):
            t = jnp.zeros((GRID_W, 128), F32)
            for off in range(NA_ROWS):
                for p in range(4):
                    if 2 * p - off + 7 == dd:
                        t = t + d_ref[off, :, p * 128:(p + 1) * 128]
            t = jnp.dot(reverse, jnp.where(valid, t, 0.0), precision=lax.Precision.HIGHEST,
                        preferred_element_type=F32)
            t = pltpu.roll(t, 128 - (GRID_W - 16), 1, stride=1, stride_axis=0)
            o_ref[dd:dd + 1, :] = jnp.sum(t, axis=0, keepdims=True)

    return pl.pallas_call(
        body, name="na_bias_bwd", grid=(8,),
        in_specs=[pl.BlockSpec((None, NA_ROWS, GRID_W, NA_ROWS * GRID_W), lambda h: (h, 0, 0, 0))] + dep_specs,
        out_specs=pl.BlockSpec((None, 16, 128), lambda h: (h, 0, 0)),
        out_shape=jax.ShapeDtypeStruct((8, 16, 128), F32),
        compiler_params=_params(("parallel",)))(dbias, *dep_args)


GATE_ROWS = 128


def _group_weights(l0, l1, l2):
    m = jnp.maximum(jnp.maximum(l0, l1), l2)
    e0, e1, e2 = jnp.exp(l0 - m), jnp.exp(l1 - m), jnp.exp(l2 - m)
    inv = 1.0 / (e0 + e1 + e2)
    return e0 * inv, e1 * inv, e2 * inv


def _gate_block(o_grp, l_grp, out_b, out_c, parts, x, target, merge_bias, wts, w_out, gain, head_sum):
    rows = GATE_ROWS
    r512 = pl.BlockSpec((rows, 512), lambda i: (i, 0))
    r1024 = pl.BlockSpec((rows, D_MODEL), lambda i: (i, 0))
    silu_cols = [pl.BlockSpec((rows, 512), functools.partial(lambda b, i: (i, b), 13 + b)) for b in range(3)]
    logit_cols = [pl.BlockSpec((rows, D_MODEL), functools.partial(lambda b, i: (i, b), 8 + b)) for b in range(3)]

    def body(o0, o1, o2, l0, l1, l2, ob, oc, ga, gb, gc, la, lb, lc, x_ref, t_ref, mb, wa, wb, wc, wo_ref, gn_ref,
             hs_ref, dout_ref, dla, dlb, dlc, dga, dgb, dgc, do0, do1, do2, dp0, dp1, dp2, dob, doc, err_ref, gg_ref,
             gmb, gwa, gwb, gwc, gwo, acc_a, acc_b, acc_c, acc_o):
        step = pl.program_id(0)
        ws = _group_weights(l0[...], l1[...], l2[...])
        out_a = ws[0] * o0[...] + ws[1] * o1[...] + ws[2] * o2[...]
        branches = ((out_a, ga, la, wa, acc_a, dla, dga), (ob[...], gb, lb, wb, acc_b, dlb, dgb),
                    (oc[...], gc, lc, wc, acc_c, dlc, dgc))

        @pl.when(step == 0)
        def _():
            for acc in (acc_a, acc_b, acc_c, acc_o):
                acc[...] = jnp.zeros(acc.shape, F32)
            err_ref[...] = jnp.zeros((1, D_MODEL), F32)
            gg_ref[...] = jnp.zeros((1, D_MODEL), F32)
            gmb[...] = jnp.zeros((3, D_MODEL), F32)

        y = jnp.zeros((rows, D_MODEL), F32)
        zs, gates, silus, dsilus, us = [], [], [], [], []
        for b, (ov, g_ref, l_ref, w_ref, _, _, _) in enumerate(branches):
            g = g_ref[...].astype(F32)
            sg = _sigmoid(g)
            silus.append(g * sg)
            dsilus.append(sg * (1.0 + g * (1.0 - sg)))
            us.append((ov * silus[b]).astype(BF16))
            zs.append(lax.dot_general(us[b], w_ref[...], NT, preferred_element_type=F32))
            gates.append(_sigmoid(l_ref[...].astype(F32) + mb[b:b + 1, :]))
            y = y + gates[b] * zs[b]
        yb = y.astype(BF16)
        y2 = jnp.dot(yb, wo_ref[...], preferred_element_type=F32)
        rstd = lax.rsqrt(jnp.mean(y2 * y2, axis=1, keepdims=True) + EPS)
        yn = y2 * rstd
        gv = gn_ref[...]
        err = x_ref[...] + yn * gv - t_ref[...]
        dout = err * (1.0 / D_MODEL)
        dout_ref[...] = dout
        dn = dout * gv
        dy2 = (rstd * (dn - yn * jnp.mean(dn * yn, axis=1, keepdims=True))).astype(BF16)
        acc_o[...] += lax.dot_general(yb, dy2, TN, preferred_element_type=F32)
        err_ref[...] += jnp.sum(err * err, axis=0, keepdims=True)
        gg_ref[...] += jnp.sum(dout * yn, axis=0, keepdims=True)
        dy = lax.dot_general(dy2, wo_ref[...], NT, preferred_element_type=F32)
        dos = []
        for b, (ov, _, _, w_ref, acc, dl_ref, dg_ref) in enumerate(branches):
            dl = dy * zs[b] * gates[b] * (1.0 - gates[b])
            dl_ref[...] = dl.astype(BF16)
            gmb[b:b + 1, :] += jnp.sum(dl, axis=0, keepdims=True)
            dz = (dy * gates[b]).astype(BF16)
            acc[...] += lax.dot_general(dz, us[b], TN, preferred_element_type=F32)
            du = jnp.dot(dz, w_ref[...], preferred_element_type=F32)
            dos.append(du * silus[b])
            dg_ref[...] = (du * ov * dsilus[b]).astype(BF16)
        dob[...] = dos[1].astype(BF16)
        doc[...] = dos[2].astype(BF16)
        row_term = jnp.dot(dos[0] * out_a, hs_ref[...], precision=lax.Precision.HIGHEST, preferred_element_type=F32)
        for wg, do_ref, dp_ref in zip(ws, (do0, do1, do2), (dp0, dp1, dp2)):
            do_ref[...] = wg * dos[0]
            dp_ref[...] = wg * row_term

        @pl.when(step == SEQ // rows - 1)
        def _():
            for acc, out in ((acc_a, gwa), (acc_b, gwb), (acc_c, gwc), (acc_o, gwo)):
                out[...] = acc[...].astype(BF16)

    full = lambda shape: pl.BlockSpec(shape, lambda i: (0,) * len(shape))
    vec = pl.BlockSpec((1, D_MODEL), lambda i: (0, 0))
    acc3 = pl.BlockSpec((3, D_MODEL), lambda i: (0, 0))
    in_specs = ([r512] * 8 + silu_cols + logit_cols + [r1024, r1024, full((3, D_MODEL))]
                + [full((D_MODEL, 512))] * 3 + [full((D_MODEL, D_MODEL)), vec, full((512, 512))])
    out_specs = ([r1024] + [r1024] * 3 + [r512] * 3 + [r512] * 6 + [r512] * 2 + [vec, vec, acc3]
                 + [full((D_MODEL, 512))] * 3 + [full((D_MODEL, D_MODEL))])
    bf, f32 = BF16, F32
    sds = jax.ShapeDtypeStruct
    out_shape = ([sds((SEQ, D_MODEL), f32)] + [sds((SEQ, D_MODEL), bf)] * 3 + [sds((SEQ, 512), bf)] * 3
                 + [sds((SEQ, 512), f32)] * 6 + [sds((SEQ, 512), bf)] * 2 + [sds((1, D_MODEL), f32)] * 2
                 + [sds((3, D_MODEL), f32)] + [sds((D_MODEL, 512), bf)] * 3 + [sds((D_MODEL, D_MODEL), bf)])
    res = pl.pallas_call(
        body, name="gate_block", grid=(SEQ // rows,), in_specs=in_specs, out_specs=out_specs, out_shape=out_shape,
        scratch_shapes=[pltpu.VMEM((D_MODEL, 512), F32)] * 3 + [pltpu.VMEM((D_MODEL, D_MODEL), F32)],
        compiler_params=_params(("arbitrary",)))(
            *o_grp, *l_grp, out_b, out_c, parts, parts, parts, parts, parts, parts, x, target, merge_bias, *wts, w_out,
            gain, head_sum)
    return dict(dout=res[0], dlog=res[1:4], dg=res[4:7], do_grp=res[7:10], dp_grp=res[10:13], do_b=res[13],
                do_c=res[14], err_sq=res[15], g_post=res[16], g_mb=res[17], g_wt=res[18:21], g_w_out=res[21])


def _local_step(x, hs, hst, mem, target, pre_norm, mem_norm, post_norm, na_rpb, wt_in, late_weights, dep_in=None,
                reduce_start=None):
    tabs = _rope_tables()
    parts = _in_proj(hs, wt_in, tabs, dep_in)

    o_grp, l_grp = [], []
    for g, d in enumerate(DILATIONS):
        o, l = _attn_fwd("dil_fwd_%d" % g, "dil", parts, parts, parts, 12 * g, 12 * g + 4, 12 * g + 8, d=d)
        o_grp.append(o)
        l_grp.append(l)
    bias = _na_bias(jnp.pad(na_rpb, ((0, 0), (0, 1), (0, 128 - 31))))
    out_b, lse_b = _attn_fwd("na_fwd", "na", parts, parts, parts, 36, 40, 44, bias=bias)
    merge_bias, w_kv, wt_a, wt_b, wt_c, w_out = late_weights(out_b)
    memn = _rmsnorm_fwd("memnorm", mem, mem_norm, MEM_LEN)
    kv_m = _mm_simple("mem_kv", memn, w_kv, NN, BF16, MEM_LEN, 512, D_MODEL)
    out_c, lse_c = _attn_fwd("mem_fwd", "mem", parts, kv_m, kv_m, 48, 0, 4)

    rr = _iota((512, 512), 0) // HEAD_DIM
    cc = _iota((512, 512), 1) // HEAD_DIM
    head_sum = (rr == cc).astype(F32)
    gb = _gate_block(o_grp, l_grp, out_b, out_c, parts, x, target, merge_bias, (wt_a, wt_b, wt_c), w_out, post_norm,
                     head_sum)
    dout, dlog, dg, g_wt, g_w_out = gb["dout"], gb["dlog"], gb["dg"], gb["g_wt"], gb["g_w_out"]
    do_grp, dp_grp, do_b, do_c, g_post, g_mb = (gb["do_grp"], gb["dp_grp"], gb["do_b"], gb["do_c"], gb["g_post"],
                                                gb["g_mb"])
    loss = 0.5 * jnp.sum(gb["err_sq"]) / D_MODEL

    dqkv = []
    for g, d in enumerate(DILATIONS):
        dq, dk, dv = _attn_bwd("dil_bwd_%d" % g, "dil", parts, parts, parts, 12 * g, 12 * g + 4, 12 * g + 8,
                               do_grp[g], l_grp[g], dp=dp_grp[g], d=d, tabs=tabs[g])
        dqkv += [dq, dk, dv]
    dq_b, dk_b, dv_b, dbias = _attn_bwd("na_bwd", "na", parts, parts, parts, 36, 40, 44, do_b, lse_b, o=out_b,
                                        bias=bias)
    dq_c, dk_m, dv_m = _attn_bwd("mem_bwd", "mem", parts, kv_m, kv_m, 48, 0, 4, do_c, lse_c, o=out_c)

    dkv = jnp.concatenate([dk_m, dv_m], axis=1).astype(BF16)
    g_w_kv = _mm_simple("mem_kv_dw", memn, dkv, TN, BF16, D_MODEL, 512, MEM_LEN)
    dmemn = _mm_simple("mem_kv_dx", dkv, w_kv, NT, F32, MEM_LEN, 512, D_MODEL)

    grads = dict(w_kv=g_w_kv, wt_a=g_wt[0], wt_b=g_wt[1], wt_c=g_wt[2], w_out=g_w_out, merge_bias=g_mb,
                 post_norm=g_post)
    dep = reduce_start(grads) if reduce_start is not None else None
    dparts = dqkv + [dq_b, dk_b, dv_b, dq_c] + list(dg) + list(dlog)
    grads["wt_in"] = _in_proj_dw(dparts, hst, dep)
    dep = reduce_start(grads) if reduce_start is not None else None
    dh = _in_proj_dh(dparts, wt_in, dep)
    grad_x, grads["pre_norm"] = _prenorm_bwd(x, pre_norm, dh, dout)
    g_rpb_t = _na_bias_bwd(dbias, dep)
    grads["na_rpb"] = g_rpb_t[:, :15, :31] + jnp.pad(g_rpb_t[:, :14, 64:95], ((0, 0), (1, 0), (0, 0)))
    grads["mem_norm"] = _memnorm_bwd(mem, dmemn, dep)
    return loss, grad_x, grads


ANY = pl.BlockSpec(memory_space=pl.ANY)


def _place():
    return lax.axis_index("x"), lax.axis_index("y"), lax.axis_index("c")


def _all_gather(shard):
    r = shard.shape[0]
    half = r // 2

    def body(src, out, send_sems, recv_sems, local_sem):
        x, y, c = _place()
        me, sib = (x, y, c), (x, y, 1 - c)
        xn, yn, dg = (1 - x, y, c), (x, 1 - y, c), (1 - x, 1 - y, c)

        def rows(dev, part=None):
            blk = out.at[4 * dev[0] + 2 * dev[1] + dev[2]]
            return blk if part is None else blk.at[pl.ds(part * half, half)]

        def copy(k, dev, part, to, own=False):
            return pltpu.make_async_remote_copy(
                src_ref=src if own else rows(dev, part), dst_ref=rows(dev, part),
                send_sem=send_sems.at[k], recv_sem=recv_sems.at[k], device_id=to, device_id_type=MESH_ID)

        def other(dev):
            return (dev[0], dev[1], 1 - dev[2])

        mine = pltpu.make_async_copy(src, rows(me), local_sem)
        mine.start()
        sent = [copy(0, me, None, sib, own=True), copy(1, me, None, xn, own=True), copy(2, me, None, yn, own=True)]
        for cp in sent:
            cp.start()
        copy(1, xn, None, me).wait_recv()
        sent += [copy(3, xn, 0, yn), copy(5, xn, None, sib)]
        sent[-2].start()
        sent[-1].start()
        copy(2, yn, None, me).wait_recv()
        sent += [copy(4, yn, 1, xn), copy(6, yn, None, sib)]
        sent[-2].start()
        sent[-1].start()
        copy(3, dg, 0, me).wait_recv()
        sent.append(copy(7, dg, 0, sib))
        sent[-1].start()
        copy(4, dg, 1, me).wait_recv()
        sent.append(copy(8, dg, 1, sib))
        sent[-1].start()
        copy(0, sib, None, me).wait_recv()
        copy(5, other(xn), None, me).wait_recv()
        copy(6, other(yn), None, me).wait_recv()
        copy(7, other(dg), 0, me).wait_recv()
        copy(8, other(dg), 1, me).wait_recv()
        for cp in sent:
            cp.wait_send()
        mine.wait()

    return pl.pallas_call(
        body, name="all_gather", in_specs=[ANY], out_specs=ANY,
        out_shape=jax.ShapeDtypeStruct((N_DEV,) + shard.shape, shard.dtype),
        scratch_shapes=[pltpu.SemaphoreType.DMA((9,)), pltpu.SemaphoreType.DMA((9,)), pltpu.SemaphoreType.DMA])(shard)


def _exchange_sibling(name, terms):
    nt = len(terms)

    def body(*refs):
        srcs, outs = refs[:nt], refs[nt:2 * nt]
        send_sems, recv_sems = refs[2 * nt:]
        x, y, c = _place()
        copies = []
        for q in range(4):
            for t in range(nt):
                copies.append(pltpu.make_async_remote_copy(
                    src_ref=srcs[t].at[2 * q + 1 - c], dst_ref=outs[t].at[q],
                    send_sem=send_sems.at[q * nt + t], recv_sem=recv_sems.at[q * nt + t],
                    device_id=(x, y, 1 - c), device_id_type=MESH_ID))
        for cp in copies:
            cp.start()
        for cp in copies:
            cp.wait()

    return pl.pallas_call(
        body, name=name, in_specs=[ANY] * nt, out_specs=[ANY] * nt,
        out_shape=[jax.ShapeDtypeStruct((4,) + s.shape[1:], s.dtype) for s in terms],
        scratch_shapes=[pltpu.SemaphoreType.DMA((4 * nt,)), pltpu.SemaphoreType.DMA((4 * nt,))])(*terms)


HBM = pl.BlockSpec(memory_space=pltpu.HBM)
SEM = pl.BlockSpec(memory_space=pltpu.SEMAPHORE)
DATAFLOW = pltpu.SideEffectType.DATAFLOW_SIDE_EFFECTING


def _split_copies(kind, srcs, lands, send_sems, recv_sems):
    nt = len(srcs)
    x, y, c = _place()
    copies = []
    if kind == "gather":
        me = 4 * x + 2 * y + c
        for mask in range(1, 8):
            fx, fy, fc = (mask >> 2) & 1, (mask >> 1) & 1, mask & 1
            to = (1 - x if fx else x, 1 - y if fy else y, 1 - c if fc else c)
            for t in range(nt):
                k = (mask - 1) * nt + t
                copies.append(pltpu.make_async_remote_copy(
                    src_ref=srcs[t], dst_ref=lands[t].at[me], send_sem=send_sems.at[k], recv_sem=recv_sems.at[k],
                    device_id=to, device_id_type=MESH_ID))
    else:
        for s, (tx, ty) in enumerate([(1 - x, y), (x, 1 - y), (1 - x, 1 - y)]):
            for t in range(nt):
                k = s * nt + t
                copies.append(pltpu.make_async_remote_copy(
                    src_ref=srcs[t].at[2 * tx + ty], dst_ref=lands[t].at[s], send_sem=send_sems.at[k],
                    recv_sem=recv_sems.at[k], device_id=(tx, ty, c), device_id_type=MESH_ID))
    return copies


def _split_count(kind, nt):
    return (7 if kind == "gather" else 3) * nt


def _exchange_start(name, kind, srcs, land_shapes, after=None):
    nt = len(srcs)
    n = _split_count(kind, nt)
    dep_specs, dep_args = _dep_operand(after)
    nd = len(dep_args)

    def body(*refs):
        src_refs, land_refs = refs[:nt], refs[nt:2 * nt]
        send_sems, recv_sems = refs[2 * nt + nd], refs[2 * nt + nd + 1]
        token = refs[-1]
        for cp in _split_copies(kind, src_refs, land_refs, send_sems, recv_sems):
            cp.start()
        token[...] = jnp.zeros_like(token)

    lands = [pltpu.with_memory_space_constraint(lax.empty(s.shape, s.dtype), pltpu.HBM) for s in land_shapes]
    res = pl.pallas_call(
        body, name=name,
        out_shape=(pltpu.SemaphoreType.DMA((n,)), pltpu.SemaphoreType.DMA((n,)),
                   *[pltpu.HBM(s.shape, s.dtype) for s in srcs], *[pltpu.HBM(s.shape, s.dtype) for s in land_shapes],
                   jax.ShapeDtypeStruct((8, 128), F32)),
        in_specs=[HBM] * (2 * nt) + dep_specs,
        out_specs=(SEM, SEM, *([HBM] * (2 * nt)), pl.BlockSpec(memory_space=pltpu.VMEM)),
        input_output_aliases={i: 2 + i for i in range(2 * nt)},
        compiler_params=pltpu.CompilerParams(has_side_effects=DATAFLOW))(
            *[pltpu.with_memory_space_constraint(s, pltpu.HBM) for s in srcs], *lands, *dep_args)
    return res[0], res[1], list(res[2:2 + nt]), list(res[2 + nt:2 + 2 * nt]), res[-1]


def _exchange_wait(name, kind, send_sems, recv_sems, srcs, lands, after):
    nt = len(srcs)

    def body(*refs):
        src_refs, land_refs = refs[:nt], refs[nt:2 * nt]
        s_sems, r_sems = refs[2 * nt], refs[2 * nt + 1]
        for cp in _split_copies(kind, src_refs, land_refs, s_sems, r_sems):
            cp.wait_send()
            cp.wait_recv()

    res = pl.pallas_call(
        body, name=name,
        out_shape=tuple(pltpu.HBM(s.shape, s.dtype) for s in list(srcs) + list(lands)),
        in_specs=[HBM] * (2 * nt) + [SEM, SEM, pl.BlockSpec(memory_space=pl.ANY)],
        out_specs=tuple([HBM] * (2 * nt)),
        input_output_aliases={i: i for i in range(2 * nt)},
        compiler_params=pltpu.CompilerParams(has_side_effects=DATAFLOW))(
            *srcs, *lands, send_sems, recv_sems, after)
    return list(res[:nt]), list(res[nt:])


AG_GROUPS = ((0, 3), (3, 4), (7, 2))


def _ag_phase(name, own, land, sems, waits, starts, after=None):
    r = own.shape[0]
    half = r // 2
    ns = len(sems)
    dep_specs, dep_args = _dep_operand(after)
    nd = len(dep_args)
    new_group = None
    if starts:
        (new_group,) = [g for g, (first, n) in enumerate(AG_GROUPS) if first == starts[0]]
        assert list(starts) == list(range(AG_GROUPS[new_group][0], sum(AG_GROUPS[new_group])))

    def body(*refs):
        own_ref, land_ref = refs[0], refs[1]
        sem_refs = list(refs[2:2 + 2 * ns])
        outs = refs[2 + 2 * ns + nd:]
        if starts:
            sem_refs += [outs[0], outs[1]]
        x, y, c = _place()
        me, sib = (x, y, c), (x, y, 1 - c)
        xn, yn, dg = (1 - x, y, c), (x, 1 - y, c), (1 - x, 1 - y, c)

        def other(dev):
            return (dev[0], dev[1], 1 - dev[2])

        def rows(dev, part):
            blk = land_ref.at[4 * dev[0] + 2 * dev[1] + dev[2]]
            return blk if part is None else blk.at[pl.ds(part * half, half)]

        def sem_of(k):
            (g,) = [g for g, (first, n) in enumerate(AG_GROUPS) if first <= k < first + n]
            return sem_refs[2 * g].at[k - AG_GROUPS[g][0]], sem_refs[2 * g + 1].at[k - AG_GROUPS[g][0]]

        sent = {0: (me, None, sib), 1: (me, None, xn), 2: (me, None, yn), 3: (xn, 0, yn), 4: (yn, 1, xn),
                5: (xn, None, sib), 6: (yn, None, sib), 7: (dg, 0, sib), 8: (dg, 1, sib)}
        landed = {0: (sib, None), 1: (xn, None), 2: (yn, None), 3: (dg, 0), 4: (dg, 1), 5: (other(xn), None),
                  6: (other(yn), None), 7: (other(dg), 0), 8: (other(dg), 1)}

        def copy(k, receiving):
            send_sem, recv_sem = sem_of(k)
            dev, part, to = (*landed[k], me) if receiving else sent[k]
            src = own_ref if (dev is me and not receiving) else rows(dev, part)
            return pltpu.make_async_remote_copy(src_ref=src, dst_ref=rows(dev, part), send_sem=send_sem,
                                                recv_sem=recv_sem, device_id=to, device_id_type=MESH_ID)

        for kind, k in waits:
            if kind == "recv":
                copy(k, True).wait_recv()
            else:
                copy(k, False).wait_send()
        for k in starts:
            copy(k, False).start()
        if starts:
            outs[-1][...] = jnp.zeros_like(outs[-1])

    n_new = AG_GROUPS[new_group][1] if starts else 0
    sem_out = (pltpu.SemaphoreType.DMA((n_new,)), pltpu.SemaphoreType.DMA((n_new,))) if starts else ()
    token_out = (jax.ShapeDtypeStruct((8, 128), F32),) if starts else ()
    res = pl.pallas_call(
        body, name=name,
        out_shape=(*sem_out, pltpu.HBM(own.shape, own.dtype), pltpu.HBM(land.shape, land.dtype), *token_out),
        in_specs=[HBM, HBM] + [SEM] * (2 * ns) + dep_specs,
        out_specs=(*([SEM] * len(sem_out)), HBM, HBM, *([pl.BlockSpec(memory_space=pltpu.VMEM)] * len(token_out))),
        input_output_aliases={0: len(sem_out), 1: len(sem_out) + 1},
        compiler_params=pltpu.CompilerParams(has_side_effects=DATAFLOW))(
            own, land, *[a for pair in sems for a in pair], *dep_args)
    if starts:
        return (res[0], res[1]), res[2], res[3], res[4]
    return None, res[0], res[1], None


def _add_sibling(name, term, recv, rows):
    _, r, w = term.shape
    cidx = lax.axis_index("c").astype(jnp.int32).reshape(1)
    like_term = recv.shape[0] == N_DEV

    def body(c_ref, a_ref, b_ref, o_ref):
        o_ref[...] = (a_ref[...].astype(F32) + b_ref[...].astype(F32)).astype(o_ref.dtype)

    grid_spec = pltpu.PrefetchScalarGridSpec(
        num_scalar_prefetch=1, grid=(4, r // rows),
        in_specs=[pl.BlockSpec((None, rows, w), lambda q, i, c_ref: (2 * q + c_ref[0], i, 0)),
                  pl.BlockSpec((None, rows, w), lambda q, i, c_ref: (2 * q + c_ref[0] if like_term else q, i, 0))],
        out_specs=pl.BlockSpec((None, rows, w), lambda q, i, c_ref: (q, i, 0)))
    return pl.pallas_call(
        body, name=name, grid_spec=grid_spec, out_shape=jax.ShapeDtypeStruct((4, r, w), term.dtype),
        compiler_params=_params(("parallel", "parallel")))(cidx, term, recv)


def _add_sibling_small(name, terms, recvs):
    nt = len(terms)

    def body(*refs):
        c = lax.axis_index("c")
        for t_ref, r_ref, o_ref in zip(refs[:nt], refs[nt:2 * nt], refs[2 * nt:]):
            for q in range(4):
                o_ref[q] = (t_ref[2 * q + c].astype(F32) + r_ref[q].astype(F32)).astype(o_ref.dtype)

    return pl.pallas_call(
        body, name=name, out_shape=[jax.ShapeDtypeStruct((4,) + t.shape[1:], t.dtype) for t in terms],
        compiler_params=_params())(*terms, *recvs)


def _add_chips(name, sums, recv, rows):
    _, r, w = sums.shape
    qidx = (2 * lax.axis_index("x") + lax.axis_index("y")).astype(jnp.int32).reshape(1)

    def body(q_ref, a_ref, b_ref, o_ref):
        o_ref[...] = ((a_ref[...].astype(F32) + b_ref[0].astype(F32))
                      + (b_ref[1].astype(F32) + b_ref[2].astype(F32)))

    grid_spec = pltpu.PrefetchScalarGridSpec(
        num_scalar_prefetch=1, grid=(r // rows,),
        in_specs=[pl.BlockSpec((None, rows, w), lambda i, q_ref: (q_ref[0], i, 0)),
                  pl.BlockSpec((3, rows, w), lambda i, q_ref: (0, i, 0))],
        out_specs=pl.BlockSpec((rows, w), lambda i, q_ref: (i, 0)))
    return pl.pallas_call(
        body, name=name, grid_spec=grid_spec, out_shape=jax.ShapeDtypeStruct((r, w), F32),
        compiler_params=_params(("parallel",)))(qidx, sums, recv)


def _rs_rows(a):
    return SHARD_IN // 4 if a.shape[1] == SHARD_IN else a.shape[1]


def _reduce_scatter_start(tag, names, terms, recv1=None):
    if recv1 is None:
        recv1 = _exchange_sibling("exchange_sibling_" + tag, terms)
    if len(terms) == 1:
        sums = [_add_sibling("add_sibling_" + names[0], terms[0], recv1[0], _rs_rows(terms[0]))]
    else:
        sums = _add_sibling_small("add_sibling_" + tag, terms, recv1)
    lands =[jax.ShapeDtypeStruct((3,) + s.shape[1:], s.dtype) for s in sums]
    send_sems, recv_sems, sums, lands, token = _exchange_start("exchange_chips_start_" + tag, "chips", sums, lands)
    return (tag, names, send_sems, recv_sems, sums, lands), token


def _reduce_scatter_wait(state, after):
    tag, names, send_sems, recv_sems, sums, lands = state
    sums, recv2 = _exchange_wait("exchange_chips_wait_" + tag, "chips", send_sems, recv_sems, sums, lands, after)
    return names, sums, recv2


def _adamw(name, w, g, m, v, dep=None):
    dep_specs, dep_args = _dep_operand(dep)

    def body(w_ref, g_ref, m_ref, v_ref, *rest):
        d_ref, nm_ref, nv_ref = rest[-3:]
        d_ref[...], nm_ref[...], nv_ref[...] = _adam_math(w_ref[...], g_ref[...], m_ref[...], v_ref[...])

    whole = pl.BlockSpec(memory_space=pltpu.VMEM)
    return pl.pallas_call(
        body, name=name, in_specs=[whole] * 4 + dep_specs, out_shape=[jax.ShapeDtypeStruct(w.shape, F32)] * 3,
        compiler_params=_params())(w, g, m, v, *dep_args)


def _adam_math(w, g, m, v):
    nm = ADAM_B1 * m + (1.0 - ADAM_B1) * g
    nv = ADAM_B2 * v + (1.0 - ADAM_B2) * (g * g)
    c1 = 1.0 - ADAM_B1 ** ADAM_STEP
    c2 = 1.0 - ADAM_B2 ** ADAM_STEP
    return -ADAM_LR * ((nm / c1) / (jnp.sqrt(nv / c2) + ADAM_EPS) + ADAM_WD * w), nm, nv


def _adamw_chips(name, sums, recv, w, m, v, transposed, rows=None, dep=None):
    r, c = w.shape
    rows = r if rows is None else rows
    qidx = (2 * lax.axis_index("x") + lax.axis_index("y")).astype(jnp.int32).reshape(1)
    dep_specs, dep_args = _dep_operand(dep)

    def body(q_ref, a_ref, b_ref, w_ref, m_ref, v_ref, *rest):
        g_ref, d_ref, nm_ref, nv_ref = rest[-4:]
        g = (a_ref[...].astype(F32) + b_ref[0].astype(F32)) + (b_ref[1].astype(F32) + b_ref[2].astype(F32))
        if transposed:
            g = g.T
        g_ref[...] = g
        d_ref[...], nm_ref[...], nv_ref[...] = _adam_math(w_ref[...], g, m_ref[...], v_ref[...])

    row = pl.BlockSpec((rows, c), lambda i, q_ref: (i, 0))
    if transposed:
        term_specs = [pl.BlockSpec((None, c, rows), lambda i, q_ref: (q_ref[0], 0, i)),
                      pl.BlockSpec((3, c, rows), lambda i, q_ref: (0, 0, i))]
    else:
        term_specs = [pl.BlockSpec((None, rows, c), lambda i, q_ref: (q_ref[0], i, 0)),
                      pl.BlockSpec((3, rows, c), lambda i, q_ref: (0, i, 0))]
    grid_spec = pltpu.PrefetchScalarGridSpec(
        num_scalar_prefetch=1, grid=(r // rows,), in_specs=term_specs + [row, row, row] + dep_specs,
        out_specs=[row] * 4)
    return pl.pallas_call(
        body, name=name, grid_spec=grid_spec, out_shape=[jax.ShapeDtypeStruct((r, c), F32)] * 4,
        compiler_params=_params(("parallel",)))(qidx, sums, recv, w, m, v, *dep_args)


def _sum_devices(gathered):
    def body(g_ref, o_ref):
        acc = g_ref[0]
        for j in range(1, N_DEV):
            acc = acc + g_ref[j]
        o_ref[...] = acc

    return pl.pallas_call(
        body, name="sum_devices", out_shape=jax.ShapeDtypeStruct(gathered.shape[1:], F32),
        compiler_params=_params())(gathered)


def _rows128(a, rows):
    flat = a.reshape(-1)
    return jnp.pad(flat, (0, rows * 128 - flat.shape[0])).reshape(rows, 128)


def kernel(x, mem, pre_norm, w_in, merge_bias, na_rpb, mem_norm, w_mem_kv, w_branch_a, w_branch_b, w_branch_c, w_out, post_norm, loss_target, m_pre_norm, m_w_in, m_merge_bias, m_na_rpb, m_mem_norm, m_w_mem_kv, m_w_branch_a, m_w_branch_b, m_w_branch_c, m_w_out, m_post_norm, v_pre_norm, v_w_in, v_merge_bias, v_na_rpb, v_mem_norm, v_w_mem_kv, v_w_branch_a, v_w_branch_b, v_w_branch_c, v_w_out, v_post_norm):
    wt_in_s = w_in[0].T.astype(BF16)
    rows_s = jnp.concatenate([w_mem_kv[0], w_out[0]], axis=0).astype(BF16)
    cols_s = jnp.concatenate([w_branch_a[0].T, w_branch_b[0].T, w_branch_c[0].T], axis=0).astype(BF16)
    mb_s = jnp.pad(merge_bias[0], ((0, 5), (0, 0)))
    me = 4 * lax.axis_index("x") + 2 * lax.axis_index("y") + lax.axis_index("c")

    land = pltpu.with_memory_space_constraint(lax.empty((N_DEV,) + wt_in_s.shape, BF16), pltpu.HBM)
    own = pltpu.with_memory_space_constraint(wt_in_s, pltpu.HBM)
    sem_a, own, land, token = _ag_phase("ag_start", own, land, [], [], [0, 1, 2])
    hs, hst = _prenorm_fold(x[0], pre_norm, token)
    sem_b, own, land, _ = _ag_phase("ag_mid1", own, land, [sem_a], [("recv", 1), ("recv", 2)], [3, 4, 5, 6], hs)
    _, own, land, _ = _ag_phase("ag_wait1", own, land, [sem_a, sem_b], [("recv", 0), ("recv", 5), ("recv", 6)], [])
    sem_c, own, land, _ = _ag_phase("ag_mid2", own, land, [sem_a, sem_b], [("recv", 3), ("recv", 4)], [7, 8])
    _, own, land, _ = _ag_phase("ag_end", own, land, [sem_a, sem_b, sem_c],
                                [("recv", 7), ("recv", 8)] + [("send", k) for k in range(9)], [])
    wt_in = lax.dynamic_update_slice(land, own[None], (me, 0, 0)).reshape(N_IN, D_MODEL)

    late_own = [rows_s, cols_s, mb_s]
    late_lands = [jax.ShapeDtypeStruct((N_DEV,) + s.shape, s.dtype) for s in late_own]
    l_send, l_recv, late_own, late_lands, late_token = _exchange_start("gather_late_start", "gather", late_own,
                                                                       late_lands, after=wt_in)
    def late_weights(after):
        own, lands = _exchange_wait("gather_late_wait", "gather", l_send, l_recv, late_own, late_lands, after)
        g_rows, g_cols, g_mb = [lax.dynamic_update_slice(land, o[None], (me, 0, 0)) for land, o in zip(lands, own)]
        return (g_mb[:, :3].transpose(1, 0, 2).reshape(3, D_MODEL),
                g_rows[:, :128].reshape(D_MODEL, D_MODEL), g_cols[:, 0:128].reshape(D_MODEL, 512),
                g_cols[:, 128:256].reshape(D_MODEL, 512), g_cols[:, 256:384].reshape(D_MODEL, 512),
                g_rows[:, 128:].reshape(D_MODEL, D_MODEL))

    rs_state = []

    def reduce_start(grads):
        if "wt_in" in grads:
            own, sibling = [a.reshape(N_DEV, SHARD_IN, D_MODEL) for a in grads["wt_in"]]
            state, token = _reduce_scatter_start("w_in", ["w_in"], [own], [sibling])
        else:
            gmb_t = jnp.pad(grads["merge_bias"].reshape(3, N_DEV, 128).transpose(1, 0, 2), ((0, 0), (0, 5), (0, 0)))
            names = ["w_kv", "w_out", "a", "b", "c", "mb"]
            terms = [grads["w_kv"].reshape(N_DEV, 128, D_MODEL), grads["w_out"].reshape(N_DEV, 128, D_MODEL),
                     grads["wt_a"].reshape(N_DEV, 128, 512), grads["wt_b"].reshape(N_DEV, 128, 512),
                     grads["wt_c"].reshape(N_DEV, 128, 512), gmb_t]
            state, token = _reduce_scatter_start("rest", names, terms)
        rs_state.append(state)
        return token

    loss_term, grad_x, grads = _local_step(
        x[0], hs, hst, mem[0], loss_target[0], pre_norm, mem_norm, post_norm, na_rpb[0], wt_in, late_weights,
        dep_in=late_token, reduce_start=reduce_start)

    small = jnp.concatenate([_rows128(grads["pre_norm"], 8), _rows128(grads["mem_norm"], 8),
                             _rows128(grads["post_norm"], 8), _rows128(grads["na_rpb"], 32),
                             _rows128(loss_term, 8)], axis=0)
    s_send, s_recv, s_own, s_land, s_token = _exchange_start(
        "gather_small_start", "gather", [small], [jax.ShapeDtypeStruct((N_DEV,) + small.shape, F32)])
    grad = {}
    weights = {
        "pre_norm": (pre_norm, m_pre_norm, v_pre_norm), "w_in": (w_in, m_w_in, v_w_in),
        "merge_bias": (merge_bias, m_merge_bias, v_merge_bias), "na_rpb": (na_rpb, m_na_rpb, v_na_rpb),
        "mem_norm": (mem_norm, m_mem_norm, v_mem_norm), "w_mem_kv": (w_mem_kv, m_w_mem_kv, v_w_mem_kv),
        "w_branch_a": (w_branch_a, m_w_branch_a, v_w_branch_a), "w_branch_b": (w_branch_b, m_w_branch_b, v_w_branch_b),
        "w_branch_c": (w_branch_c, m_w_branch_c, v_w_branch_c), "w_out": (w_out, m_w_out, v_w_out),
        "post_norm": (post_norm, m_post_norm, v_post_norm)}
    order = ["pre_norm", "w_in", "merge_bias", "na_rpb", "mem_norm", "w_mem_kv", "w_branch_a", "w_branch_b",
             "w_branch_c", "w_out", "post_norm"]
    delta, new_m, new_v = {}, {}, {}

    def update(n, dep=None):
        w, m, v = weights[n]
        shape = w.shape
        two_d = (-1, shape[-1])
        dl, nm, nv = _adamw("adamw_" + n, w.reshape(two_d), grad[n].reshape(two_d), m.reshape(two_d),
                            v.reshape(two_d), dep)
        delta[n], new_m[n], new_v[n] = dl.reshape(shape), nm.reshape(shape), nv.reshape(shape)
        return dl

    def update_sharded(n, sums, recv, transposed, rows=None, dep=None):
        w, m, v = weights[n]
        g, dl, nm, nv = _adamw_chips("adamw_" + n, sums, recv, w[0], m[0], v[0], transposed, rows, dep)
        grad[n], delta[n], new_m[n], new_v[n] = g[None], dl[None], nm[None], nv[None]
        return dl

    _, sums, recv2 = _reduce_scatter_wait(rs_state[0], s_token)
    dep = None
    for i, (n, transposed) in enumerate((("w_mem_kv", False), ("w_out", False), ("w_branch_a", True),
                                         ("w_branch_b", True), ("w_branch_c", True))):
        dep = update_sharded(n, sums[i], recv2[i], transposed, dep=dep)
    grad["merge_bias"] = _add_chips("add_chips_mb", sums[5], recv2[5], 8)[:3][None]
    update("merge_bias")
    s_own, s_land = _exchange_wait("gather_small_wait", "gather", s_send, s_recv, s_own, s_land, dep)
    total = _sum_devices(lax.dynamic_update_slice(s_land[0], s_own[0][None], (me, 0, 0)))
    loss = total[56, 0]
    grad.update({"pre_norm": total[0:8].reshape(1, D_MODEL), "mem_norm": total[8:16].reshape(1, D_MODEL),
                 "post_norm": total[16:24].reshape(1, D_MODEL),
                 "na_rpb": total[24:56].reshape(-1)[:8 * 15 * 31].reshape(1, 8, 15, 31)})
    dep = None
    for n in ("pre_norm", "na_rpb", "mem_norm", "post_norm"):
        dep = update(n, dep)
    _, sums_in, recv_in = _reduce_scatter_wait(rs_state[1], dep)
    update_sharded("w_in", sums_in[0], recv_in[0], True, 256)

    return (loss, grad_x[None], *[grad[n] for n in order], *[delta[n] for n in order],
            *[new_m[n] for n in order], *[new_v[n] for n in order])
```

```python
import functools

import numpy as np
import jax
import jax.numpy as jnp
from jax import lax
from jax.experimental import pallas as pl
from jax.experimental.pallas import tpu as pltpu

F32 = jnp.float32
BF16 = jnp.bfloat16

SEQ = 2048
D_MODEL = 1024
N_IN = 11264
N_DEV = 8
SHARD_IN = N_IN // N_DEV
HEAD_DIM = 64
GRID_W = 64
NA_ROWS = 8
MEM_LEN = 256
DILATIONS = (1, 4, 16)
REACH = 64
ROPE_THETA = 500000.0
ROPE_DIM = 16
EPS = 1e-6
NEG = -1e30
ADAM_LR = 0.001
ADAM_B1 = 0.9
ADAM_B2 = 0.999
ADAM_EPS = 1e-08
ADAM_WD = 0.01
ADAM_STEP = 10

VMEM_LIMIT_BYTES = 56 * 1024 * 1024
MESH_ID = pl.DeviceIdType.MESH

NN = (((1,), (0,)), ((), ()))
NT = (((1,), (1,)), ((), ()))
TN = (((0,), (0,)), ((), ()))


def _params(sem=None):
    return pltpu.CompilerParams(dimension_semantics=sem, vmem_limit_bytes=VMEM_LIMIT_BYTES)


def _iota(shape, dim):
    return lax.broadcasted_iota(jnp.int32, shape, dim)


def _sigmoid(x):
    return 1.0 / (1.0 + jnp.exp(-x))


def _rope_tables():
    half = ROPE_DIM // 2
    inv = (ROPE_THETA ** (-np.arange(half, dtype=np.float64) * 2.0 / ROPE_DIM)).astype(np.float32)
    pos = np.arange(SEQ, dtype=np.float32)
    ang = pos[:, None] * inv[None, :]
    cos, sin = np.cos(ang), np.sin(ang)
    zeros = np.zeros_like(cos)
    rest = HEAD_DIM - ROPE_DIM
    c64 = np.concatenate([cos, cos, np.ones((SEQ, rest), np.float32)], axis=1)
    s1 = np.concatenate([zeros, sin, np.zeros((SEQ, rest), np.float32)], axis=1)
    s2 = np.concatenate([-sin, zeros, np.zeros((SEQ, rest), np.float32)], axis=1)

    def fold(t, d):
        return t.reshape(SEQ // d, d, t.shape[1]).transpose(1, 0, 2).reshape(SEQ, t.shape[1])

    tabs = [np.stack([np.tile(fold(t, d), (1, 2)) for t in (c64, s1, s2)], axis=0) for d in DILATIONS]
    return jnp.asarray(np.stack(tabs, axis=0), dtype=F32)


def _rope(a, c, s1, s2):
    return a * c + pltpu.roll(a, 8, 1) * s1 + pltpu.roll(a, 120, 1) * s2


def _rope_t(a, c, s1, s2):
    return a * c + pltpu.roll(a * s1, 120, 1) + pltpu.roll(a * s2, 8, 1)


def _perm_of_block(j):
    return jnp.where(j < 3, 0, jnp.where(j < 6, 1, jnp.where(j < 9, 2, 0)))


def _mm(name, a, b, out_shape, out_dtype, grid, a_spec, b_spec, o_spec, acc_shape, dims, k_axis, nk):
    def body(a_ref, b_ref, o_ref, acc_ref):
        k = pl.program_id(k_axis)

        @pl.when(k == 0)
        def _():
            acc_ref[...] = jnp.zeros(acc_shape, F32)

        acc_ref[...] += lax.dot_general(a_ref[...], b_ref[...], dims, preferred_element_type=F32)

        @pl.when(k == nk - 1)
        def _():
            o_ref[...] = acc_ref[...].astype(out_dtype)

    sem = tuple("arbitrary" if ax == k_axis else "parallel" for ax in range(len(grid)))
    return pl.pallas_call(
        body, name=name, grid=grid, in_specs=[a_spec, b_spec], out_specs=o_spec,
        out_shape=jax.ShapeDtypeStruct(out_shape, out_dtype),
        scratch_shapes=[pltpu.VMEM(acc_shape, F32)], compiler_params=_params(sem))(a, b)


def _mm_simple(name, a, b, dims, out_dtype, tm, tn, tk):
    if dims is NN:
        m, kk = a.shape
        n = b.shape[1]
        a_spec = pl.BlockSpec((tm, tk), lambda i, j, k: (i, k))
        b_spec = pl.BlockSpec((tk, tn), lambda i, j, k: (k, j))
    elif dims is NT:
        m, kk = a.shape
        n = b.shape[0]
        a_spec = pl.BlockSpec((tm, tk), lambda i, j, k: (i, k))
        b_spec = pl.BlockSpec((tn, tk), lambda i, j, k: (j, k))
    else:
        kk, m = a.shape
        n = b.shape[1]
        a_spec = pl.BlockSpec((tk, tm), lambda i, j, k: (k, i))
        b_spec = pl.BlockSpec((tk, tn), lambda i, j, k: (k, j))
    grid = (m // tm, n // tn, kk // tk)
    o_spec = pl.BlockSpec((tm, tn), lambda i, j, k: (i, j))
    return _mm(name, a, b, (m, n), out_dtype, grid, a_spec, b_spec, o_spec, (tm, tn), dims, 2, kk // tk)


def _rmsnorm_fwd(name, x, gain, rows):
    n, d = x.shape

    def body(x_ref, g_ref, o_ref):
        xv = x_ref[...]
        rstd = lax.rsqrt(jnp.mean(xv * xv, axis=1, keepdims=True) + EPS)
        o_ref[...] = (xv * rstd * g_ref[...]).astype(BF16)

    return pl.pallas_call(
        body, name=name, grid=(n // rows,),
        in_specs=[pl.BlockSpec((rows, d), lambda i: (i, 0)), pl.BlockSpec((1, d), lambda i: (0, 0))],
        out_specs=pl.BlockSpec((rows, d), lambda i: (i, 0)),
        out_shape=jax.ShapeDtypeStruct((n, d), BF16), compiler_params=_params(("parallel",)))(x, gain)


def _folded_rows(first, rows, d):
    if d == 1:
        return pl.ds(pl.multiple_of(first, rows), rows)
    mlen = SEQ // d
    return pl.ds((first % mlen) * d + first // mlen, rows, stride=d)


def _prenorm_fold(x, gain, dep=None):
    rows = 128
    nchunk = D_MODEL // 128
    dep_specs, dep_args = _dep_operand(dep)

    def body(*refs):
        x_refs, g_ref, hs_ref, hst_ref = refs[:nchunk], refs[nchunk], refs[-2], refs[-1]
        first = pl.program_id(0) * rows
        for p, d in enumerate(DILATIONS):
            idx = _folded_rows(first, rows, d)
            xv = jnp.concatenate([r[idx, :] for r in x_refs], axis=1)
            rstd = lax.rsqrt(jnp.mean(xv * xv, axis=1, keepdims=True) + EPS)
            h = xv * rstd * g_ref[...]
            hs_ref[p] = h.astype(BF16)
            hst_ref[p] = h.T.astype(BF16)

    x_specs = [pl.BlockSpec((SEQ, 128), functools.partial(lambda c, i: (0, c), c)) for c in range(nchunk)]
    return pl.pallas_call(
        body, name="prenorm", grid=(SEQ // rows,),
        in_specs=x_specs + [pl.BlockSpec((1, D_MODEL), lambda i: (0, 0))] + dep_specs,
        out_specs=[pl.BlockSpec((3, rows, D_MODEL), lambda i: (0, i, 0)),
                   pl.BlockSpec((3, D_MODEL, rows), lambda i: (0, 0, i))],
        out_shape=[jax.ShapeDtypeStruct((3, SEQ, D_MODEL), BF16), jax.ShapeDtypeStruct((3, D_MODEL, SEQ), BF16)],
        compiler_params=_params(("parallel",)))(*([x] * nchunk), gain, *dep_args)


def _prenorm_bwd(x, gain, dh, dout):
    rows = 256

    def body(x_ref, g_ref, a_ref, do_ref, dx_ref, gg_ref):
        xv = x_ref[...]
        rstd = lax.rsqrt(jnp.mean(xv * xv, axis=1, keepdims=True) + EPS)
        xn = xv * rstd
        dh = jnp.concatenate([a_ref[c] for c in range(D_MODEL // 128)], axis=1)
        gdh = dh * g_ref[...]
        dx_ref[...] = rstd * (gdh - xn * jnp.mean(gdh * xn, axis=1, keepdims=True)) + do_ref[...]

        @pl.when(pl.program_id(0) == 0)
        def _():
            gg_ref[...] = jnp.zeros((1, D_MODEL), F32)

        gg_ref[...] += jnp.sum(dh * xn, axis=0, keepdims=True)

    row = pl.BlockSpec((rows, D_MODEL), lambda i: (i, 0))
    vec = pl.BlockSpec((1, D_MODEL), lambda i: (0, 0))
    return pl.pallas_call(
        body, name="prenorm_bwd", grid=(SEQ // rows,),
        in_specs=[row, vec, pl.BlockSpec((D_MODEL // 128, rows, 128), lambda i: (0, i, 0)), row], out_specs=[row, vec],
        out_shape=[jax.ShapeDtypeStruct((SEQ, D_MODEL), F32), jax.ShapeDtypeStruct((1, D_MODEL), F32)],
        compiler_params=_params(("arbitrary",)))(x, gain, dh, dout)


def _memnorm_bwd(mem, dmemn, dep=None):
    dep_specs, dep_args = _dep_operand(dep)

    def body(m_ref, d_ref, *rest):
        mv = m_ref[...]
        rstd = lax.rsqrt(jnp.mean(mv * mv, axis=1, keepdims=True) + EPS)
        rest[-1][...] = jnp.sum(d_ref[...] * mv * rstd, axis=0, keepdims=True)

    whole = pl.BlockSpec(memory_space=pltpu.VMEM)
    return pl.pallas_call(
        body, name="memnorm_bwd", in_specs=[whole, whole] + dep_specs,
        out_shape=jax.ShapeDtypeStruct((1, D_MODEL), F32), compiler_params=_params())(mem, dmemn, *dep_args)


def _dep_operand(dep):
    return ([], []) if dep is None else ([pl.BlockSpec(memory_space=pl.ANY)], [dep])


def _in_proj(name, hs, wt, tabs, order, prev=None, dep=None):
    tm, tn = 512, 512
    prev_specs, prev_args = ([], []) if prev is None else ([ANY], [prev])
    dep_specs, dep_args = _dep_operand(dep)

    def body(order_ref, h_ref, w_ref, t_ref, *rest):
        o_ref = rest[-1]
        j = order_ref[pl.program_id(0)]
        is_rope = jnp.logical_and(j < 9, j % 3 != 2)
        row_slices = [slice(r * tm, (r + 1) * tm) for r in range(SEQ // tm)]

        def product(rs):
            return lax.dot_general(h_ref[rs, :], w_ref[...], NT, preferred_element_type=F32)

        @pl.when(is_rope)
        def _():
            for rs in row_slices:
                acc = product(rs)
                c, s1, s2 = t_ref[0, rs, :], t_ref[1, rs, :], t_ref[2, rs, :]
                for q in range(tn // 128):
                    a = acc[:, q * 128:(q + 1) * 128]
                    o_ref[rs, q * 128:(q + 1) * 128] = _rope(a, c, s1, s2).astype(BF16)

        @pl.when(jnp.logical_not(is_rope))
        def _():
            for rs in row_slices:
                o_ref[rs, :] = product(rs).astype(BF16)

    grid_spec = pltpu.PrefetchScalarGridSpec(
        num_scalar_prefetch=1, grid=(order.shape[0],),
        in_specs=[pl.BlockSpec((None, SEQ, D_MODEL), lambda t, o: (_perm_of_block(o[t]), 0, 0)),
                  pl.BlockSpec((tn, D_MODEL), lambda t, o: (o[t], 0)),
                  pl.BlockSpec((None, 3, SEQ, 128), lambda t, o: (_perm_of_block(o[t]), 0, 0, 0))] + prev_specs
        + dep_specs,
        out_specs=pl.BlockSpec((SEQ, tn), lambda t, o: (0, o[t])))
    return pl.pallas_call(
        body, name=name, grid_spec=grid_spec, out_shape=jax.ShapeDtypeStruct((SEQ, N_IN), BF16),
        input_output_aliases={} if prev is None else {4: 0},
        compiler_params=_params(("arbitrary",)))(order, hs, wt, tabs, *prev_args, *dep_args)


def _piece_blocks(pieces):
    return [(a, h * 512) for a, p in enumerate(pieces) for h in range(p.shape[1] // 512)]


def _block_fetch(piece_refs, blocks, buf, sem):
    def start(block, slot):
        for b, (a, col) in enumerate(blocks):
            @pl.when(block == b)
            def _():
                pltpu.make_async_copy(piece_refs[a].at[:, pl.ds(col, 512)], buf.at[slot], sem.at[slot]).start()

    def wait(slot):
        pltpu.make_async_copy(piece_refs[0].at[:, pl.ds(0, 512)], buf.at[slot], sem.at[slot]).wait()

    return start, wait


def _in_proj_dw(pieces, hst, dep=None):
    tn = 512
    blocks = _piece_blocks(pieces)
    nblk = len(blocks)
    npc = len(pieces)
    dep_specs, dep_args = _dep_operand(dep)

    def body(h_ref, *rest):
        piece_refs = rest[:npc]
        o_ref, mirror, buf, sem, out_buf, send_sems, recv_sem = rest[-7:]
        j = pl.program_id(0)
        slot = j % 2
        start, wait = _block_fetch(piece_refs, blocks, buf, sem)
        x, y, c = _place()

        def to_sibling(step, slot_):
            return pltpu.make_async_remote_copy(
                src_ref=out_buf.at[slot_], dst_ref=mirror.at[pl.ds(pl.multiple_of(step * tn, tn), tn)],
                send_sem=send_sems.at[slot_], recv_sem=recv_sem, device_id=(x, y, 1 - c), device_id_type=MESH_ID)

        @pl.when(j == 0)
        def _():
            start(j, slot)

        wait(slot)

        @pl.when(j + 1 < nblk)
        def _():
            start(j + 1, 1 - slot)

        acc = jnp.dot(h_ref[...], buf[slot], preferred_element_type=F32)
        block = acc.T.astype(BF16)
        o_ref[...] = block

        @pl.when(j >= 2)
        def _():
            to_sibling(j - 2, slot).wait_send()

        out_buf[slot] = block
        to_sibling(j, slot).start()

        @pl.when(j == nblk - 1)
        def _():
            to_sibling(j - 1, 1 - slot).wait_send()
            to_sibling(j, slot).wait_send()
            pltpu.make_async_remote_copy(src_ref=mirror, dst_ref=mirror, send_sem=send_sems.at[0], recv_sem=recv_sem,
                                         device_id=(x, y, 1 - c), device_id_type=MESH_ID).wait_recv()

    return pl.pallas_call(
        body, name="in_proj_dw", grid=(nblk,),
        in_specs=[pl.BlockSpec((None, D_MODEL, SEQ), lambda j: (_perm_of_block(j), 0, 0))] + [ANY] * npc + dep_specs,
        out_specs=[pl.BlockSpec((tn, D_MODEL), lambda j: (j, 0)), ANY],
        out_shape=[jax.ShapeDtypeStruct((N_IN, D_MODEL), BF16), jax.ShapeDtypeStruct((N_IN, D_MODEL), BF16)],
        scratch_shapes=[pltpu.VMEM((2, SEQ, tn), BF16), pltpu.SemaphoreType.DMA((2,)),
                        pltpu.VMEM((2, tn, D_MODEL), BF16), pltpu.SemaphoreType.DMA((2,)), pltpu.SemaphoreType.DMA],
        compiler_params=_params(("arbitrary",)))(hst, *pieces, *dep_args)


def _in_proj_dh(pieces, wt, dep=None):
    tk = 512
    blocks = _piece_blocks(pieces)
    nblk = len(blocks)
    npc = len(pieces)
    nchunk = D_MODEL // 128

    def col(s):
        return jnp.where(s < 3, s, jnp.where(s < 16, s + 6, s - 13))

    dep_specs, dep_args = _dep_operand(dep)

    def body(w_ref, *rest):
        piece_refs = rest[:npc]
        o_ref, acc_ref, buf, sem = rest[-4:]
        s = pl.program_id(0)
        slot = s % 2
        start, wait = _block_fetch(piece_refs, blocks, buf, sem)

        @pl.when(s == 0)
        def _():
            start(col(s), slot)

        wait(slot)

        @pl.when(s + 1 < nblk)
        def _():
            start(col(s + 1), 1 - slot)

        row_slices = [slice(r * 512, (r + 1) * 512) for r in range(SEQ // 512)]

        def product(rs):
            return jnp.dot(buf[slot, rs, :], w_ref[...], preferred_element_type=F32)

        def accumulate(cond, to_out, init):
            @pl.when(cond)
            def _():
                for rs in row_slices:
                    prod = product(rs)
                    if not to_out:
                        if init:
                            acc_ref[rs, :] = prod
                        else:
                            acc_ref[rs, :] += prod
                        continue
                    for c in range(nchunk):
                        if init:
                            o_ref[c, rs, :] = prod[:, c * 128:(c + 1) * 128]
                        else:
                            o_ref[c, rs, :] += prod[:, c * 128:(c + 1) * 128]

        accumulate(s == 0, True, True)
        accumulate(jnp.logical_and(s > 0, s < 16), True, False)
        accumulate(jnp.logical_or(s == 16, s == 19), False, True)
        accumulate(jnp.logical_and(s > 16, s != 19), False, False)
        for last, d in ((18, 4), (21, 16)):
            @pl.when(s == last)
            def _():
                mlen = SEQ // d
                for r in range(d):
                    for c in range(nchunk):
                        o_ref[c, pl.ds(r, mlen, stride=d), :] += acc_ref[r * mlen:(r + 1) * mlen,
                                                                         c * 128:(c + 1) * 128]

    return pl.pallas_call(
        body, name="in_proj_dh", grid=(nblk,),
        in_specs=[pl.BlockSpec((tk, D_MODEL), lambda s: (col(s), 0))] + [ANY] * npc + dep_specs,
        out_specs=pl.BlockSpec((nchunk, SEQ, 128), lambda s: (0, 0, 0)),
        out_shape=jax.ShapeDtypeStruct((nchunk, SEQ, 128), F32),
        scratch_shapes=[pltpu.VMEM((SEQ, D_MODEL), F32), pltpu.VMEM((2, SEQ, tk), BF16),
                        pltpu.SemaphoreType.DMA((2,))],
        compiler_params=_params(("arbitrary",)))(wt, *pieces, *dep_args)


def _head_lanes(lanes, hh):
    return lanes >= 64 if hh == 1 else lanes < 64


def _head_rows(x, lanes, hh, pair):
    if not pair:
        return jnp.max(x, axis=1, keepdims=True)
    return jnp.max(jnp.where(_head_lanes(lanes, hh), x, -jnp.inf), axis=1, keepdims=True)


def _mask_head(x, lanes, hh, pair, scale=1.0):
    if not pair:
        return x
    xf = x.astype(F32) if scale == 1.0 else x.astype(F32) * scale
    return jnp.where(_head_lanes(lanes, hh), xf, 0.0).astype(BF16)


def _window(mode, qi, tq, mlen, tk):
    if mode == "dil":
        q0 = qi * tq
        seg = (q0 // mlen) * mlen
        ks = jnp.clip(q0 - REACH, seg, seg + mlen - tk)
        return pl.multiple_of(ks, 64)
    if mode == "na":
        r_start = jnp.clip(qi - NA_ROWS // 2, 0, SEQ // GRID_W - NA_ROWS)
        return pl.multiple_of(r_start * GRID_W, 64)
    return 0


def _band_mask(qi, tq, tk, ks):
    qpos = qi * tq + _iota((tq, tk), 0)
    kpos = ks + _iota((tq, tk), 1)
    return jnp.where(jnp.abs(qpos - kpos) <= REACH, 0.0, NEG).astype(F32)


def _stack_heads(x, lanes, pair, scale=1.0):
    if not pair:
        return x
    return jnp.concatenate([_mask_head(x, lanes, hh, pair, scale) for hh in range(2)], axis=0)


def _stack_rows(x, lanes, pair):
    if not pair:
        return _head_rows(x, lanes, 0, pair)
    return jnp.concatenate([_head_rows(x, lanes, hh, pair) for hh in range(2)], axis=0)


def _unstack_heads(x, lanes, pair, tq):
    if not pair:
        return x
    return jnp.where(lanes < 64, x[:tq], x[tq:])


def _scores(mode, qst, k, sscale, band, qi, bias_ref, pair):
    s = lax.dot_general(qst, k, NT, preferred_element_type=F32)
    if sscale != 1.0:
        s = s * sscale
    if mode == "dil":
        s = s + jnp.concatenate([band, band], axis=0)
    elif mode == "na":
        off = qi - jnp.clip(qi - NA_ROWS // 2, 0, SEQ // GRID_W - NA_ROWS)
        s = s + jnp.concatenate([bias_ref[0, off], bias_ref[1, off]], axis=0)
    return s


def _attn_cfg(mode, d):
    if mode == "dil":
        mlen = SEQ // d
        return dict(pair=True, tq=128, tk=min(256, mlen), mlen=mlen, lk=SEQ, scale=HEAD_DIM ** -0.5, units=4,
                    nsub=ATTN_SUBTILES)
    if mode == "na":
        return dict(pair=True, tq=GRID_W, tk=NA_ROWS * GRID_W, mlen=SEQ, lk=SEQ, scale=HEAD_DIM ** -0.5, units=4,
                    nsub=ATTN_SUBTILES)
    return dict(pair=False, tq=128, tk=MEM_LEN, mlen=SEQ, lk=MEM_LEN, scale=128 ** -0.5, units=4,
                nsub=ATTN_SUBTILES)


ATTN_SUBTILES = 16


def _attn_fwd(name, mode, q_arr, k_arr, v_arr, qcol, kcol, vcol, d=1, bias=None):
    cfg = _attn_cfg(mode, d)
    pair, tq, tk, mlen, lk, scale = cfg["pair"], cfg["tq"], cfg["tk"], cfg["mlen"], cfg["lk"], cfg["scale"]
    qscale, sscale = (scale, 1.0) if pair else (1.0, scale)
    nsub = cfg["nsub"]
    rows = nsub * tq

    def body(*refs):
        if mode == "na":
            q_ref, k_ref, v_ref, bias_ref, o_ref, l_ref = refs
        else:
            q_ref, k_ref, v_ref, o_ref, l_ref = refs
            bias_ref = None
        lanes = _iota((tq, 128), 1)
        qis = [pl.program_id(1) * nsub + sub for sub in range(nsub)]
        kss = [_window(mode, qi, tq, mlen, tk) for qi in qis]
        vs = [v_ref[pl.ds(ks, tk), :] for ks in kss]
        bands = [_band_mask(qi, tq, tk, ks) if mode == "dil" else None for qi, ks in zip(qis, kss)]
        ss = []
        for sub in range(nsub):
            qst = _stack_heads(q_ref[sub * tq:(sub + 1) * tq, :], lanes, pair, qscale)
            k = k_ref[pl.ds(kss[sub], tk), :]
            ss.append(_scores(mode, qst, k, sscale, bands[sub], qis[sub], bias_ref, pair))
        ms = [jnp.max(s_, axis=1, keepdims=True) for s_ in ss]
        ps = [jnp.exp(s_ - m) for s_, m in zip(ss, ms)]
        ls = [jnp.sum(p, axis=1, keepdims=True) for p in ps]
        os_ = [jnp.dot(p.astype(BF16), v, preferred_element_type=F32) for p, v in zip(ps, vs)]
        for sub in range(nsub):
            out = _unstack_heads(os_[sub] / ls[sub], lanes, pair, tq)
            lse = ms[sub] + jnp.log(ls[sub])
            lse = _unstack_heads(jnp.broadcast_to(lse, (lse.shape[0], 128)), lanes, pair, tq)
            dst = _folded_rows(qis[sub] * tq, tq, d) if mode == "dil" else slice(sub * tq, (sub + 1) * tq)
            o_ref[dst, :] = out
            l_ref[dst, :] = lse

    in_specs = [pl.BlockSpec((rows, 128), lambda u, i: (i, qcol + u)),
                pl.BlockSpec((lk, 128), lambda u, i: (0, kcol + u)),
                pl.BlockSpec((lk, 128), lambda u, i: (0, vcol + u))]
    args = [q_arr, k_arr, v_arr]
    if mode == "na":
        in_specs.append(pl.BlockSpec((2, NA_ROWS, GRID_W, NA_ROWS * GRID_W), lambda u, i: (u, 0, 0, 0)))
        args.append(bias)
    if mode == "dil":
        out_spec = pl.BlockSpec((SEQ, 128), lambda u, i: (0, u))
    else:
        out_spec = pl.BlockSpec((rows, 128), lambda u, i: (i, u))
    return pl.pallas_call(
        body, name=name, grid=(cfg["units"], SEQ // rows), in_specs=in_specs, out_specs=[out_spec, out_spec],
        out_shape=[jax.ShapeDtypeStruct((SEQ, 512), F32), jax.ShapeDtypeStruct((SEQ, 512), F32)],
        compiler_params=_params(("parallel", "arbitrary")))(*args)


def _attn_bwd(name, mode, q_arr, k_arr, v_arr, qcol, kcol, vcol, do, lse, dp=None, o=None, d=1, bias=None,
              tabs=None):
    cfg = _attn_cfg(mode, d)
    pair, tq, tk, mlen, lk, scale = cfg["pair"], cfg["tq"], cfg["tk"], cfg["mlen"], cfg["lk"], cfg["scale"]
    qscale, sscale = (scale, 1.0) if pair else (1.0, scale)
    nsub = cfg["nsub"]
    rows = nsub * tq
    nq = SEQ // rows
    kv_dtype = F32 if mode == "mem" else BF16

    def body(*refs):
        refs = list(refs)
        q_ref, k_ref, v_ref, do_ref, l_ref = refs[:5]
        rest = refs[5:]
        bias_ref = tq_ref = tk_ref = db_ref = None
        if mode == "dil":
            dp_ref, tq_ref, tk_ref, dq_ref, dk_ref, dv_ref, dk_acc, dv_acc = rest
        elif mode == "na":
            o_ref, bias_ref, dq_ref, dk_ref, dv_ref, db_ref, dk_acc, dv_acc = rest
        else:
            o_ref, dq_ref, dk_ref, dv_ref, dk_acc, dv_acc = rest
        step = pl.program_id(1)

        @pl.when(step == 0)
        def _():
            dk_acc[...] = jnp.zeros((lk, 128), F32)
            dv_acc[...] = jnp.zeros((lk, 128), F32)
            if mode == "na":
                db_ref[...] = jnp.zeros(db_ref.shape, F32)

        lanes = _iota((tq, 128), 1)
        qis = [step * nsub + sub for sub in range(nsub)]
        sls = [slice(sub * tq, (sub + 1) * tq) for sub in range(nsub)]
        kss = [_window(mode, qi, tq, mlen, tk) for qi in qis]
        ks_ = [k_ref[pl.ds(ks, tk), :] for ks in kss]
        vs = [v_ref[pl.ds(ks, tk), :] for ks in kss]
        qsts, dosts, lses, dphs = [], [], [], []
        for sub in range(nsub):
            if mode == "dil":
                src = _folded_rows(qis[sub] * tq, tq, d)
                dov = do_ref[src, :].astype(BF16)
                lsev = l_ref[src, :]
                dphs.append(_stack_rows(dp_ref[src, :], lanes, pair))
            else:
                dov = do_ref[sls[sub], :]
                lsev = l_ref[sls[sub], :]
                dpv = dov.astype(F32) * o_ref[sls[sub], :]
                if pair:
                    dphs.append(jnp.concatenate(
                        [jnp.sum(jnp.where(_head_lanes(lanes, hh), dpv, 0.0), axis=1, keepdims=True)
                         for hh in range(2)], axis=0))
                else:
                    dphs.append(jnp.sum(dpv, axis=1, keepdims=True))
            qsts.append(_stack_heads(q_ref[sls[sub], :], lanes, pair, qscale))
            dosts.append(_stack_heads(dov, lanes, pair))
            lses.append(_stack_rows(lsev, lanes, pair))
        bands = [_band_mask(qi, tq, tk, ks) if mode == "dil" else None for qi, ks in zip(qis, kss)]
        ss = [_scores(mode, qsts[sub], ks_[sub], sscale, bands[sub], qis[sub], bias_ref, pair) for sub in range(nsub)]
        dpms = [lax.dot_general(dosts[sub], vs[sub], NT, preferred_element_type=F32) for sub in range(nsub)]
        ps = [jnp.exp(s_ - lse) for s_, lse in zip(ss, lses)]
        dss = [p * (dpm - dph) for p, dpm, dph in zip(ps, dpms, dphs)]
        if mode == "na":
            for sub, ds in enumerate(dss):
                off = qis[sub] - jnp.clip(qis[sub] - NA_ROWS // 2, 0, SEQ // GRID_W - NA_ROWS)
                db_ref[0, off] += ds[:tq]
                db_ref[1, off] += ds[tq:]
        dsbs = [ds.astype(BF16) for ds in dss]
        dvs = [lax.dot_general(p.astype(BF16), dosts[sub], TN, preferred_element_type=F32)
               for sub, p in enumerate(ps)]
        dqs = [jnp.dot(dsb, ks_[sub], preferred_element_type=F32) * scale for sub, dsb in enumerate(dsbs)]
        dks = [lax.dot_general(dsb, qsts[sub], TN, preferred_element_type=F32) for sub, dsb in enumerate(dsbs)]
        for sub in range(nsub):
            sl = sls[sub]
            dq = _unstack_heads(dqs[sub], lanes, pair, tq)
            if mode == "dil":
                dq = _rope_t(dq, tq_ref[0, sl, :], tq_ref[1, sl, :], tq_ref[2, sl, :])
            dq_ref[sl, :] = dq.astype(BF16)
            dk_acc[pl.ds(kss[sub], tk), :] += dks[sub] if pair else dks[sub] * scale
            dv_acc[pl.ds(kss[sub], tk), :] += dvs[sub]

        @pl.when(step == nq - 1)
        def _():
            dkv = dk_acc[...]
            if mode == "dil":
                dkv = _rope_t(dkv, tk_ref[0], tk_ref[1], tk_ref[2])
            dk_ref[...] = dkv.astype(kv_dtype)
            dv_ref[...] = dv_acc[...].astype(kv_dtype)

    q_spec = pl.BlockSpec((rows, 128), lambda u, i: (i, qcol + u))
    row_spec = pl.BlockSpec((rows, 128), lambda u, i: (i, u))
    kv_out = pl.BlockSpec((lk, 128), lambda u, i: (0, u))
    whole = pl.BlockSpec((SEQ, 128), lambda u, i: (0, u))
    nat_spec = whole if mode == "dil" else row_spec
    in_specs = [q_spec,
                pl.BlockSpec((lk, 128), lambda u, i: (0, kcol + u)),
                pl.BlockSpec((lk, 128), lambda u, i: (0, vcol + u)),
                nat_spec, nat_spec]
    args = [q_arr, k_arr, v_arr, do, lse]
    out_specs = [row_spec, kv_out, kv_out]
    out_shape = [jax.ShapeDtypeStruct((SEQ, 512), BF16), jax.ShapeDtypeStruct((lk, 512), kv_dtype),
                 jax.ShapeDtypeStruct((lk, 512), kv_dtype)]
    if mode == "dil":
        in_specs += [whole, pl.BlockSpec((3, rows, 128), lambda u, i: (0, i, 0)),
                     pl.BlockSpec((3, SEQ, 128), lambda u, i: (0, 0, 0))]
        args += [dp, tabs, tabs]
    elif mode == "na":
        b_spec = pl.BlockSpec((2, NA_ROWS, GRID_W, NA_ROWS * GRID_W), lambda u, i: (u, 0, 0, 0))
        in_specs += [row_spec, b_spec]
        args += [o, bias]
        out_specs.append(b_spec)
        out_shape.append(jax.ShapeDtypeStruct((8, NA_ROWS, GRID_W, NA_ROWS * GRID_W), F32))
    else:
        in_specs.append(row_spec)
        args.append(o)
    return pl.pallas_call(
        body, name=name, grid=(cfg["units"], nq), in_specs=in_specs, out_specs=out_specs, out_shape=out_shape,
        scratch_shapes=[pltpu.VMEM((lk, 128), F32), pltpu.VMEM((lk, 128), F32)],
        compiler_params=_params(("parallel", "arbitrary")))(*args)


def _na_geometry():
    qc = _iota((GRID_W, 128), 0)
    lane = _iota((GRID_W, 128), 1)
    kc = lane & 63
    c_start = jnp.clip(qc - 8, 0, GRID_W - 16)
    valid = jnp.logical_and(kc >= c_start, kc < c_start + 16)
    return lane, valid


def _na_bias(rpb_rows):
    def body(r_ref, o_ref, t_ref):
        lane, valid = _na_geometry()
        for dd in range(14):
            row_a = jnp.broadcast_to(r_ref[dd:dd + 1, :], (GRID_W, 128))
            row_b = jnp.broadcast_to(r_ref[dd + 1:dd + 2, :], (GRID_W, 128))
            both = jnp.where(lane < 64, row_a, pltpu.roll(row_b, 64, 1))
            t = pltpu.roll(both, 128 - 15, 1, stride=1, stride_axis=0)
            t_ref[dd] = jnp.where(valid, t, NEG)
        for off in range(NA_ROWS):
            for p in range(4):
                o_ref[off, :, p * 128:(p + 1) * 128] = t_ref[2 * p - off + 7]

    return pl.pallas_call(
        body, name="na_bias", grid=(8,),
        in_specs=[pl.BlockSpec((None, 16, 128), lambda h: (h, 0, 0))],
        out_specs=pl.BlockSpec((None, NA_ROWS, GRID_W, NA_ROWS * GRID_W), lambda h: (h, 0, 0, 0)),
        out_shape=jax.ShapeDtypeStruct((8, NA_ROWS, GRID_W, NA_ROWS * GRID_W), F32),
        scratch_shapes=[pltpu.VMEM((14, GRID_W, 128), F32)],
        compiler_params=_params(("parallel",)))(rpb_rows)


def _na_bias_bwd(dbias, dep=None):
    dep_specs, dep_args = _dep_operand(dep)

    def body(d_ref, *rest):
        o_ref = rest[-1]
        lane, valid = _na_geometry()
        reverse = (_iota((GRID_W, GRID_W), 0) + _iota((GRID_W, GRID_W), 1) == GRID_W - 1).astype(F32)
        o_ref[...] = jnp.zeros((16, 128), F32)
        for dd in range(14):
            t = jnp.zeros((GRID_W, 128), F32)
            for off in range(NA_ROWS):
                for p in range(4):
                    if 2 * p - off + 7 == dd:
                        t = t + d_ref[off, :, p * 128:(p + 1) * 128]
            t = jnp.dot(reverse, jnp.where(valid, t, 0.0), precision=lax.Precision.HIGHEST,
                        preferred_element_type=F32)
            t = pltpu.roll(t, 128 - (GRID_W - 16), 1, stride=1, stride_axis=0)
            o_ref[dd:dd + 1, :] = jnp.sum(t, axis=0, keepdims=True)

    return pl.pallas_call(
        body, name="na_bias_bwd", grid=(8,),
        in_specs=[pl.BlockSpec((None, NA_ROWS, GRID_W, NA_ROWS * GRID_W), lambda h: (h, 0, 0, 0))] + dep_specs,
        out_specs=pl.BlockSpec((None, 16, 128), lambda h: (h, 0, 0)),
        out_shape=jax.ShapeDtypeStruct((8, 16, 128), F32),
        compiler_params=_params(("parallel",)))(dbias, *dep_args)


GATE_ROWS = 128


def _group_weights(l0, l1, l2):
    m = jnp.maximum(jnp.maximum(l0, l1), l2)
    e0, e1, e2 = jnp.exp(l0 - m), jnp.exp(l1 - m), jnp.exp(l2 - m)
    inv = 1.0 / (e0 + e1 + e2)
    return e0 * inv, e1 * inv, e2 * inv


def _gate_block(o_grp, l_grp, out_b, out_c, parts, x, target, merge_bias, wts, w_out, gain, head_sum):
    rows = GATE_ROWS
    r512 = pl.BlockSpec((rows, 512), lambda i: (i, 0))
    r1024 = pl.BlockSpec((rows, D_MODEL), lambda i: (i, 0))
    silu_cols = [pl.BlockSpec((rows, 512), functools.partial(lambda b, i: (i, b), 13 + b)) for b in range(3)]
    logit_cols = [pl.BlockSpec((rows, D_MODEL), functools.partial(lambda b, i: (i, b), 8 + b)) for b in range(3)]

    def body(o0, o1, o2, l0, l1, l2, ob, oc, ga, gb, gc, la, lb, lc, x_ref, t_ref, mb, wa, wb, wc, wo_ref, gn_ref,
             hs_ref, dout_ref, dla, dlb, dlc, dga, dgb, dgc, do0, do1, do2, dp0, dp1, dp2, dob, doc, err_ref, gg_ref,
             gmb, gwa, gwb, gwc, gwo, acc_a, acc_b, acc_c, acc_o):
        step = pl.program_id(0)
        ws = _group_weights(l0[...], l1[...], l2[...])
        out_a = ws[0] * o0[...] + ws[1] * o1[...] + ws[2] * o2[...]
        branches = ((out_a, ga, la, wa, acc_a, dla, dga), (ob[...], gb, lb, wb, acc_b, dlb, dgb),
                    (oc[...], gc, lc, wc, acc_c, dlc, dgc))

        @pl.when(step == 0)
        def _():
            for acc in (acc_a, acc_b, acc_c, acc_o):
                acc[...] = jnp.zeros(acc.shape, F32)
            err_ref[...] = jnp.zeros((1, D_MODEL), F32)
            gg_ref[...] = jnp.zeros((1, D_MODEL), F32)
            gmb[...] = jnp.zeros((3, D_MODEL), F32)

        y = jnp.zeros((rows, D_MODEL), F32)
        zs, gates, silus, dsilus, us = [], [], [], [], []
        for b, (ov, g_ref, l_ref, w_ref, _, _, _) in enumerate(branches):
            g = g_ref[...].astype(F32)
            sg = _sigmoid(g)
            silus.append(g * sg)
            dsilus.append(sg * (1.0 + g * (1.0 - sg)))
            us.append((ov * silus[b]).astype(BF16))
            zs.append(lax.dot_general(us[b], w_ref[...], NT, preferred_element_type=F32))
            gates.append(_sigmoid(l_ref[...].astype(F32) + mb[b:b + 1, :]))
            y = y + gates[b] * zs[b]
        yb = y.astype(BF16)
        y2 = jnp.dot(yb, wo_ref[...], preferred_element_type=F32)
        rstd = lax.rsqrt(jnp.mean(y2 * y2, axis=1, keepdims=True) + EPS)
        yn = y2 * rstd
        gv = gn_ref[...]
        err = x_ref[...] + yn * gv - t_ref[...]
        dout = err * (1.0 / D_MODEL)
        dout_ref[...] = dout
        dn = dout * gv
        dy2 = (rstd * (dn - yn * jnp.mean(dn * yn, axis=1, keepdims=True))).astype(BF16)
        acc_o[...] += lax.dot_general(yb, dy2, TN, preferred_element_type=F32)
        err_ref[...] += jnp.sum(err * err, axis=0, keepdims=True)
        gg_ref[...] += jnp.sum(dout * yn, axis=0, keepdims=True)
        dy = lax.dot_general(dy2, wo_ref[...], NT, preferred_element_type=F32)
        dos = []
        for b, (ov, _, _, w_ref, acc, dl_ref, dg_ref) in enumerate(branches):
            dl = dy * zs[b] * gates[b] * (1.0 - gates[b])
            dl_ref[...] = dl.astype(BF16)
            gmb[b:b + 1, :] += jnp.sum(dl, axis=0, keepdims=True)
            dz = (dy * gates[b]).astype(BF16)
            acc[...] += lax.dot_general(dz, us[b], TN, preferred_element_type=F32)
            du = jnp.dot(dz, w_ref[...], preferred_element_type=F32)
            dos.append(du * silus[b])
            dg_ref[...] = (du * ov * dsilus[b]).astype(BF16)
        dob[...] = dos[1].astype(BF16)
        doc[...] = dos[2].astype(BF16)
        row_term = jnp.dot(dos[0] * out_a, hs_ref[...], precision=lax.Precision.HIGHEST, preferred_element_type=F32)
        for wg, do_ref, dp_ref in zip(ws, (do0, do1, do2), (dp0, dp1, dp2)):
            do_ref[...] = wg * dos[0]
            dp_ref[...] = wg * row_term

        @pl.when(step == SEQ // rows - 1)
        def _():
            for acc, out in ((acc_a, gwa), (acc_b, gwb), (acc_c, gwc), (acc_o, gwo)):
                out[...] = acc[...].astype(BF16)

    full = lambda shape: pl.BlockSpec(shape, lambda i: (0,) * len(shape))
    vec = pl.BlockSpec((1, D_MODEL), lambda i: (0, 0))
    acc3 = pl.BlockSpec((3, D_MODEL), lambda i: (0, 0))
    in_specs = ([r512] * 8 + silu_cols + logit_cols + [r1024, r1024, full((3, D_MODEL))]
                + [full((D_MODEL, 512))] * 3 + [full((D_MODEL, D_MODEL)), vec, full((512, 512))])
    out_specs = ([r1024] + [r1024] * 3 + [r512] * 3 + [r512] * 6 + [r512] * 2 + [vec, vec, acc3]
                 + [full((D_MODEL, 512))] * 3 + [full((D_MODEL, D_MODEL))])
    bf, f32 = BF16, F32
    sds = jax.ShapeDtypeStruct
    out_shape = ([sds((SEQ, D_MODEL), f32)] + [sds((SEQ, D_MODEL), bf)] * 3 + [sds((SEQ, 512), bf)] * 3
                 + [sds((SEQ, 512), f32)] * 6 + [sds((SEQ, 512), bf)] * 2 + [sds((1, D_MODEL), f32)] * 2
                 + [sds((3, D_MODEL), f32)] + [sds((D_MODEL, 512), bf)] * 3 + [sds((D_MODEL, D_MODEL), bf)])
    res = pl.pallas_call(
        body, name="gate_block", grid=(SEQ // rows,), in_specs=in_specs, out_specs=out_specs, out_shape=out_shape,
        scratch_shapes=[pltpu.VMEM((D_MODEL, 512), F32)] * 3 + [pltpu.VMEM((D_MODEL, D_MODEL), F32)],
        compiler_params=_params(("arbitrary",)))(
            *o_grp, *l_grp, out_b, out_c, parts, parts, parts, parts, parts, parts, x, target, merge_bias, *wts, w_out,
            gain, head_sum)
    return dict(dout=res[0], dlog=res[1:4], dg=res[4:7], do_grp=res[7:10], dp_grp=res[10:13], do_b=res[13],
                do_c=res[14], err_sq=res[15], g_post=res[16], g_mb=res[17], g_wt=res[18:21], g_w_out=res[21])


def _local_step(x, hst, parts, tabs, mem, target, pre_norm, mem_norm, post_norm, na_rpb, wt_in, late_weights,
                reduce_start=None):
    o_grp, l_grp = [], []
    for g, d in enumerate(DILATIONS):
        o, l = _attn_fwd("dil_fwd_%d" % g, "dil", parts, parts, parts, 12 * g, 12 * g + 4, 12 * g + 8, d=d)
        o_grp.append(o)
        l_grp.append(l)
    bias = _na_bias(jnp.pad(na_rpb, ((0, 0), (0, 1), (0, 128 - 31))))
    out_b, lse_b = _attn_fwd("na_fwd", "na", parts, parts, parts, 36, 40, 44, bias=bias)
    merge_bias, w_kv, wt_a, wt_b, wt_c, w_out = late_weights(out_b[:8, :128] + o_grp[-1][:8, :128])
    memn = _rmsnorm_fwd("memnorm", mem, mem_norm, MEM_LEN)
    kv_m = _mm_simple("mem_kv", memn, w_kv, NN, BF16, MEM_LEN, 512, D_MODEL)
    out_c, lse_c = _attn_fwd("mem_fwd", "mem", parts, kv_m, kv_m, 48, 0, 4)

    rr = _iota((512, 512), 0) // HEAD_DIM
    cc = _iota((512, 512), 1) // HEAD_DIM
    head_sum = (rr == cc).astype(F32)
    gb = _gate_block(o_grp, l_grp, out_b, out_c, parts, x, target, merge_bias, (wt_a, wt_b, wt_c), w_out, post_norm,
                     head_sum)
    dout, dlog, dg, g_wt, g_w_out = gb["dout"], gb["dlog"], gb["dg"], gb["g_wt"], gb["g_w_out"]
    do_grp, dp_grp, do_b, do_c, g_post, g_mb = (gb["do_grp"], gb["dp_grp"], gb["do_b"], gb["do_c"], gb["g_post"],
                                                gb["g_mb"])
    loss = 0.5 * jnp.sum(gb["err_sq"]) / D_MODEL

    dqkv = []
    for g, d in enumerate(DILATIONS):
        dq, dk, dv = _attn_bwd("dil_bwd_%d" % g, "dil", parts, parts, parts, 12 * g, 12 * g + 4, 12 * g + 8,
                               do_grp[g], l_grp[g], dp=dp_grp[g], d=d, tabs=tabs[g])
        dqkv += [dq, dk, dv]
    dq_b, dk_b, dv_b, dbias = _attn_bwd("na_bwd", "na", parts, parts, parts, 36, 40, 44, do_b, lse_b, o=out_b,
                                        bias=bias)
    dq_c, dk_m, dv_m = _attn_bwd("mem_bwd", "mem", parts, kv_m, kv_m, 48, 0, 4, do_c, lse_c, o=out_c)

    dkv = jnp.concatenate([dk_m, dv_m], axis=1).astype(BF16)
    g_w_kv = _mm_simple("mem_kv_dw", memn, dkv, TN, BF16, D_MODEL, 512, MEM_LEN)
    dmemn = _mm_simple("mem_kv_dx", dkv, w_kv, NT, F32, MEM_LEN, 512, D_MODEL)

    grads = dict(w_kv=g_w_kv, wt_a=g_wt[0], wt_b=g_wt[1], wt_c=g_wt[2], w_out=g_w_out, merge_bias=g_mb,
                 post_norm=g_post)
    dep = reduce_start(grads) if reduce_start is not None else None
    dparts = dqkv + [dq_b, dk_b, dv_b, dq_c] + list(dg) + list(dlog)
    grads["wt_in"] = _in_proj_dw(dparts, hst, dep)
    dep = reduce_start(grads) if reduce_start is not None else None
    dh = _in_proj_dh(dparts, wt_in, dep)
    grad_x, grads["pre_norm"] = _prenorm_bwd(x, pre_norm, dh, dout)
    g_rpb_t = _na_bias_bwd(dbias, dep)
    grads["na_rpb"] = g_rpb_t[:, :15, :31] + jnp.pad(g_rpb_t[:, :14, 64:95], ((0, 0), (1, 0), (0, 0)))
    grads["mem_norm"] = _memnorm_bwd(mem, dmemn, dep)
    return loss, grad_x, grads


ANY = pl.BlockSpec(memory_space=pl.ANY)


def _place():
    return lax.axis_index("x"), lax.axis_index("y"), lax.axis_index("c")


def _all_gather(shard):
    r = shard.shape[0]
    half = r // 2

    def body(src, out, send_sems, recv_sems, local_sem):
        x, y, c = _place()
        me, sib = (x, y, c), (x, y, 1 - c)
        xn, yn, dg = (1 - x, y, c), (x, 1 - y, c), (1 - x, 1 - y, c)

        def rows(dev, part=None):
            blk = out.at[4 * dev[0] + 2 * dev[1] + dev[2]]
            return blk if part is None else blk.at[pl.ds(part * half, half)]

        def copy(k, dev, part, to, own=False):
            return pltpu.make_async_remote_copy(
                src_ref=src if own else rows(dev, part), dst_ref=rows(dev, part),
                send_sem=send_sems.at[k], recv_sem=recv_sems.at[k], device_id=to, device_id_type=MESH_ID)

        def other(dev):
            return (dev[0], dev[1], 1 - dev[2])

        mine = pltpu.make_async_copy(src, rows(me), local_sem)
        mine.start()
        sent = [copy(0, me, None, sib, own=True), copy(1, me, None, xn, own=True), copy(2, me, None, yn, own=True)]
        for cp in sent:
            cp.start()
        copy(1, xn, None, me).wait_recv()
        sent += [copy(3, xn, 0, yn), copy(5, xn, None, sib)]
        sent[-2].start()
        sent[-1].start()
        copy(2, yn, None, me).wait_recv()
        sent += [copy(4, yn, 1, xn), copy(6, yn, None, sib)]
        sent[-2].start()
        sent[-1].start()
        copy(3, dg, 0, me).wait_recv()
        sent.append(copy(7, dg, 0, sib))
        sent[-1].start()
        copy(4, dg, 1, me).wait_recv()
        sent.append(copy(8, dg, 1, sib))
        sent[-1].start()
        copy(0, sib, None, me).wait_recv()
        copy(5, other(xn), None, me).wait_recv()
        copy(6, other(yn), None, me).wait_recv()
        copy(7, other(dg), 0, me).wait_recv()
        copy(8, other(dg), 1, me).wait_recv()
        for cp in sent:
            cp.wait_send()
        mine.wait()

    return pl.pallas_call(
        body, name="all_gather", in_specs=[ANY], out_specs=ANY,
        out_shape=jax.ShapeDtypeStruct((N_DEV,) + shard.shape, shard.dtype),
        scratch_shapes=[pltpu.SemaphoreType.DMA((9,)), pltpu.SemaphoreType.DMA((9,)), pltpu.SemaphoreType.DMA])(shard)


def _exchange_sibling(name, terms):
    nt = len(terms)

    def body(*refs):
        srcs, outs = refs[:nt], refs[nt:2 * nt]
        send_sems, recv_sems = refs[2 * nt:]
        x, y, c = _place()
        copies = []
        for q in range(4):
            for t in range(nt):
                copies.append(pltpu.make_async_remote_copy(
                    src_ref=srcs[t].at[2 * q + 1 - c], dst_ref=outs[t].at[q],
                    send_sem=send_sems.at[q * nt + t], recv_sem=recv_sems.at[q * nt + t],
                    device_id=(x, y, 1 - c), device_id_type=MESH_ID))
        for cp in copies:
            cp.start()
        for cp in copies:
            cp.wait()

    return pl.pallas_call(
        body, name=name, in_specs=[ANY] * nt, out_specs=[ANY] * nt,
        out_shape=[jax.ShapeDtypeStruct((4,) + s.shape[1:], s.dtype) for s in terms],
        scratch_shapes=[pltpu.SemaphoreType.DMA((4 * nt,)), pltpu.SemaphoreType.DMA((4 * nt,))])(*terms)


HBM = pl.BlockSpec(memory_space=pltpu.HBM)
SEM = pl.BlockSpec(memory_space=pltpu.SEMAPHORE)
DATAFLOW = pltpu.SideEffectType.DATAFLOW_SIDE_EFFECTING


def _split_copies(kind, srcs, lands, send_sems, recv_sems):
    nt = len(srcs)
    x, y, c = _place()
    copies = []
    if kind == "gather":
        me = 4 * x + 2 * y + c
        for mask in range(1, 8):
            fx, fy, fc = (mask >> 2) & 1, (mask >> 1) & 1, mask & 1
            to = (1 - x if fx else x, 1 - y if fy else y, 1 - c if fc else c)
            for t in range(nt):
                k = (mask - 1) * nt + t
                copies.append(pltpu.make_async_remote_copy(
                    src_ref=srcs[t], dst_ref=lands[t].at[me], send_sem=send_sems.at[k], recv_sem=recv_sems.at[k],
                    device_id=to, device_id_type=MESH_ID))
    else:
        for s, (tx, ty) in enumerate([(1 - x, y), (x, 1 - y), (1 - x, 1 - y)]):
            for t in range(nt):
                k = s * nt + t
                copies.append(pltpu.make_async_remote_copy(
                    src_ref=srcs[t].at[2 * tx + ty], dst_ref=lands[t].at[s], send_sem=send_sems.at[k],
                    recv_sem=recv_sems.at[k], device_id=(tx, ty, c), device_id_type=MESH_ID))
    return copies


def _split_count(kind, nt):
    return (7 if kind == "gather" else 3) * nt


def _exchange_start(name, kind, srcs, land_shapes, after=None):
    nt = len(srcs)
    n = _split_count(kind, nt)
    dep_specs, dep_args = _dep_operand(after)
    nd = len(dep_args)

    def body(*refs):
        src_refs, land_refs = refs[:nt], refs[nt:2 * nt]
        send_sems, recv_sems = refs[2 * nt + nd], refs[2 * nt + nd + 1]
        token = refs[-1]
        for cp in _split_copies(kind, src_refs, land_refs, send_sems, recv_sems):
            cp.start()
        token[...] = jnp.zeros_like(token)

    lands = [pltpu.with_memory_space_constraint(lax.empty(s.shape, s.dtype), pltpu.HBM) for s in land_shapes]
    res = pl.pallas_call(
        body, name=name,
        out_shape=(pltpu.SemaphoreType.DMA((n,)), pltpu.SemaphoreType.DMA((n,)),
                   *[pltpu.HBM(s.shape, s.dtype) for s in srcs], *[pltpu.HBM(s.shape, s.dtype) for s in land_shapes],
                   jax.ShapeDtypeStruct((8, 128), F32)),
        in_specs=[HBM] * (2 * nt) + dep_specs,
        out_specs=(SEM, SEM, *([HBM] * (2 * nt)), pl.BlockSpec(memory_space=pltpu.VMEM)),
        input_output_aliases={i: 2 + i for i in range(2 * nt)},
        compiler_params=pltpu.CompilerParams(has_side_effects=DATAFLOW))(
            *[pltpu.with_memory_space_constraint(s, pltpu.HBM) for s in srcs], *lands, *dep_args)
    return res[0], res[1], list(res[2:2 + nt]), list(res[2 + nt:2 + 2 * nt]), res[-1]


def _exchange_wait(name, kind, send_sems, recv_sems, srcs, lands, after):
    nt = len(srcs)

    def body(*refs):
        src_refs, land_refs = refs[:nt], refs[nt:2 * nt]
        s_sems, r_sems = refs[2 * nt], refs[2 * nt + 1]
        for cp in _split_copies(kind, src_refs, land_refs, s_sems, r_sems):
            cp.wait_send()
            cp.wait_recv()

    res = pl.pallas_call(
        body, name=name,
        out_shape=tuple(pltpu.HBM(s.shape, s.dtype) for s in list(srcs) + list(lands)),
        in_specs=[HBM] * (2 * nt) + [SEM, SEM, pl.BlockSpec(memory_space=pl.ANY)],
        out_specs=tuple([HBM] * (2 * nt)),
        input_output_aliases={i: i for i in range(2 * nt)},
        compiler_params=pltpu.CompilerParams(has_side_effects=DATAFLOW))(
            *srcs, *lands, send_sems, recv_sems, after)
    return list(res[:nt]), list(res[nt:])


AG_GROUPS = ((0, 3), (3, 4), (7, 2))


def _ag_phase(name, own, land, sems, waits, starts, after=None):
    r = own.shape[0]
    half = r // 2
    ns = len(sems)
    dep_specs, dep_args = _dep_operand(after)
    nd = len(dep_args)
    new_group = None
    if starts:
        (new_group,) = [g for g, (first, n) in enumerate(AG_GROUPS) if first == starts[0]]
        assert list(starts) == list(range(AG_GROUPS[new_group][0], sum(AG_GROUPS[new_group])))

    def body(*refs):
        own_ref, land_ref = refs[0], refs[1]
        sem_refs = list(refs[2:2 + 2 * ns])
        outs = refs[2 + 2 * ns + nd:]
        if starts:
            sem_refs += [outs[0], outs[1]]
        x, y, c = _place()
        me, sib = (x, y, c), (x, y, 1 - c)
        xn, yn, dg = (1 - x, y, c), (x, 1 - y, c), (1 - x, 1 - y, c)

        def other(dev):
            return (dev[0], dev[1], 1 - dev[2])

        def rows(dev, part):
            blk = land_ref.at[4 * dev[0] + 2 * dev[1] + dev[2]]
            return blk if part is None else blk.at[pl.ds(part * half, half)]

        def sem_of(k):
            (g,) = [g for g, (first, n) in enumerate(AG_GROUPS) if first <= k < first + n]
            return sem_refs[2 * g].at[k - AG_GROUPS[g][0]], sem_refs[2 * g + 1].at[k - AG_GROUPS[g][0]]

        sent = {0: (me, None, sib), 1: (me, None, xn), 2: (me, None, yn), 3: (xn, 0, yn), 4: (yn, 1, xn),
                5: (xn, None, sib), 6: (yn, None, sib), 7: (dg, 0, sib), 8: (dg, 1, sib)}
        landed = {0: (sib, None), 1: (xn, None), 2: (yn, None), 3: (dg, 0), 4: (dg, 1), 5: (other(xn), None),
                  6: (other(yn), None), 7: (other(dg), 0), 8: (other(dg), 1)}

        def copy(k, receiving):
            send_sem, recv_sem = sem_of(k)
            dev, part, to = (*landed[k], me) if receiving else sent[k]
            src = own_ref if (dev is me and not receiving) else rows(dev, part)
            return pltpu.make_async_remote_copy(src_ref=src, dst_ref=rows(dev, part), send_sem=send_sem,
                                                recv_sem=recv_sem, device_id=to, device_id_type=MESH_ID)

        for kind, k in waits:
            if kind == "recv":
                copy(k, True).wait_recv()
            else:
                copy(k, False).wait_send()
        for k in starts:
            copy(k, False).start()
        if starts:
            outs[-1][...] = jnp.zeros_like(outs[-1])

    n_new = AG_GROUPS[new_group][1] if starts else 0
    sem_out = (pltpu.SemaphoreType.DMA((n_new,)), pltpu.SemaphoreType.DMA((n_new,))) if starts else ()
    token_out = (jax.ShapeDtypeStruct((8, 128), F32),) if starts else ()
    res = pl.pallas_call(
        body, name=name,
        out_shape=(*sem_out, pltpu.HBM(own.shape, own.dtype), pltpu.HBM(land.shape, land.dtype), *token_out),
        in_specs=[HBM, HBM] + [SEM] * (2 * ns) + dep_specs,
        out_specs=(*([SEM] * len(sem_out)), HBM, HBM, *([pl.BlockSpec(memory_space=pltpu.VMEM)] * len(token_out))),
        input_output_aliases={0: len(sem_out), 1: len(sem_out) + 1},
        compiler_params=pltpu.CompilerParams(has_side_effects=DATAFLOW))(
            own, land, *[a for pair in sems for a in pair], *dep_args)
    if starts:
        return (res[0], res[1]), res[2], res[3], res[4]
    return None, res[0], res[1], None


def _add_sibling(name, term, recv, rows):
    _, r, w = term.shape
    cidx = lax.axis_index("c").astype(jnp.int32).reshape(1)
    like_term = recv.shape[0] == N_DEV

    def body(c_ref, a_ref, b_ref, o_ref):
        o_ref[...] = (a_ref[...].astype(F32) + b_ref[...].astype(F32)).astype(o_ref.dtype)

    grid_spec = pltpu.PrefetchScalarGridSpec(
        num_scalar_prefetch=1, grid=(4, r // rows),
        in_specs=[pl.BlockSpec((None, rows, w), lambda q, i, c_ref: (2 * q + c_ref[0], i, 0)),
                  pl.BlockSpec((None, rows, w), lambda q, i, c_ref: (2 * q + c_ref[0] if like_term else q, i, 0))],
        out_specs=pl.BlockSpec((None, rows, w), lambda q, i, c_ref: (q, i, 0)))
    return pl.pallas_call(
        body, name=name, grid_spec=grid_spec, out_shape=jax.ShapeDtypeStruct((4, r, w), term.dtype),
        compiler_params=_params(("parallel", "parallel")))(cidx, term, recv)


def _add_sibling_small(name, terms, recvs):
    nt = len(terms)

    def body(*refs):
        c = lax.axis_index("c")
        for t_ref, r_ref, o_ref in zip(refs[:nt], refs[nt:2 * nt], refs[2 * nt:]):
            for q in range(4):
                o_ref[q] = (t_ref[2 * q + c].astype(F32) + r_ref[q].astype(F32)).astype(o_ref.dtype)

    return pl.pallas_call(
        body, name=name, out_shape=[jax.ShapeDtypeStruct((4,) + t.shape[1:], t.dtype) for t in terms],
        compiler_params=_params())(*terms, *recvs)


def _add_chips(name, sums, recv, rows):
    _, r, w = sums.shape
    qidx = (2 * lax.axis_index("x") + lax.axis_index("y")).astype(jnp.int32).reshape(1)

    def body(q_ref, a_ref, b_ref, o_ref):
        o_ref[...] = ((a_ref[...].astype(F32) + b_ref[0].astype(F32))
                      + (b_ref[1].astype(F32) + b_ref[2].astype(F32)))

    grid_spec = pltpu.PrefetchScalarGridSpec(
        num_scalar_prefetch=1, grid=(r // rows,),
        in_specs=[pl.BlockSpec((None, rows, w), lambda i, q_ref: (q_ref[0], i, 0)),
                  pl.BlockSpec((3, rows, w), lambda i, q_ref: (0, i, 0))],
        out_specs=pl.BlockSpec((rows, w), lambda i, q_ref: (i, 0)))
    return pl.pallas_call(
        body, name=name, grid_spec=grid_spec, out_shape=jax.ShapeDtypeStruct((r, w), F32),
        compiler_params=_params(("parallel",)))(qidx, sums, recv)


def _rs_rows(a):
    return SHARD_IN // 4 if a.shape[1] == SHARD_IN else a.shape[1]


def _reduce_scatter_start(tag, names, terms, recv1=None):
    if recv1 is None:
        recv1 = _exchange_sibling("exchange_sibling_" + tag, terms)
    if len(terms) == 1:
        sums = [_add_sibling("add_sibling_" + names[0], terms[0], recv1[0], _rs_rows(terms[0]))]
    else:
        sums = _add_sibling_small("add_sibling_" + tag, terms, recv1)
    lands =[jax.ShapeDtypeStruct((3,) + s.shape[1:], s.dtype) for s in sums]
    send_sems, recv_sems, sums, lands, token = _exchange_start("exchange_chips_start_" + tag, "chips", sums, lands)
    return (tag, names, send_sems, recv_sems, sums, lands), token


def _reduce_scatter_wait(state, after):
    tag, names, send_sems, recv_sems, sums, lands = state
    sums, recv2 = _exchange_wait("exchange_chips_wait_" + tag, "chips", send_sems, recv_sems, sums, lands, after)
    return names, sums, recv2


def _adamw(name, w, g, m, v, dep=None):
    dep_specs, dep_args = _dep_operand(dep)

    def body(w_ref, g_ref, m_ref, v_ref, *rest):
        d_ref, nm_ref, nv_ref = rest[-3:]
        d_ref[...], nm_ref[...], nv_ref[...] = _adam_math(w_ref[...], g_ref[...], m_ref[...], v_ref[...])

    whole = pl.BlockSpec(memory_space=pltpu.VMEM)
    return pl.pallas_call(
        body, name=name, in_specs=[whole] * 4 + dep_specs, out_shape=[jax.ShapeDtypeStruct(w.shape, F32)] * 3,
        compiler_params=_params())(w, g, m, v, *dep_args)


def _adam_math(w, g, m, v):
    nm = ADAM_B1 * m + (1.0 - ADAM_B1) * g
    nv = ADAM_B2 * v + (1.0 - ADAM_B2) * (g * g)
    c1 = 1.0 - ADAM_B1 ** ADAM_STEP
    c2 = 1.0 - ADAM_B2 ** ADAM_STEP
    return -ADAM_LR * ((nm / c1) / (jnp.sqrt(nv / c2) + ADAM_EPS) + ADAM_WD * w), nm, nv


def _adamw_chips(name, sums, recv, w, m, v, transposed, rows=None, dep=None):
    r, c = w.shape
    rows = r if rows is None else rows
    qidx = (2 * lax.axis_index("x") + lax.axis_index("y")).astype(jnp.int32).reshape(1)
    dep_specs, dep_args = _dep_operand(dep)

    def body(q_ref, a_ref, b_ref, w_ref, m_ref, v_ref, *rest):
        g_ref, d_ref, nm_ref, nv_ref = rest[-4:]
        g = (a_ref[...].astype(F32) + b_ref[0].astype(F32)) + (b_ref[1].astype(F32) + b_ref[2].astype(F32))
        if transposed:
            g = g.T
        g_ref[...] = g
        d_ref[...], nm_ref[...], nv_ref[...] = _adam_math(w_ref[...], g, m_ref[...], v_ref[...])

    row = pl.BlockSpec((rows, c), lambda i, q_ref: (i, 0))
    if transposed:
        term_specs = [pl.BlockSpec((None, c, rows), lambda i, q_ref: (q_ref[0], 0, i)),
                      pl.BlockSpec((3, c, rows), lambda i, q_ref: (0, 0, i))]
    else:
        term_specs = [pl.BlockSpec((None, rows, c), lambda i, q_ref: (q_ref[0], i, 0)),
                      pl.BlockSpec((3, rows, c), lambda i, q_ref: (0, i, 0))]
    grid_spec = pltpu.PrefetchScalarGridSpec(
        num_scalar_prefetch=1, grid=(r // rows,), in_specs=term_specs + [row, row, row] + dep_specs,
        out_specs=[row] * 4)
    return pl.pallas_call(
        body, name=name, grid_spec=grid_spec, out_shape=[jax.ShapeDtypeStruct((r, c), F32)] * 4,
        compiler_params=_params(("parallel",)))(qidx, sums, recv, w, m, v, *dep_args)


def _sum_devices(gathered):
    def body(g_ref, o_ref):
        acc = g_ref[0]
        for j in range(1, N_DEV):
            acc = acc + g_ref[j]
        o_ref[...] = acc

    return pl.pallas_call(
        body, name="sum_devices", out_shape=jax.ShapeDtypeStruct(gathered.shape[1:], F32),
        compiler_params=_params())(gathered)


def _rows128(a, rows):
    flat = a.reshape(-1)
    return jnp.pad(flat, (0, rows * 128 - flat.shape[0])).reshape(rows, 128)


def kernel(x, mem, pre_norm, w_in, merge_bias, na_rpb, mem_norm, w_mem_kv, w_branch_a, w_branch_b, w_branch_c, w_out, post_norm, loss_target, m_pre_norm, m_w_in, m_merge_bias, m_na_rpb, m_mem_norm, m_w_mem_kv, m_w_branch_a, m_w_branch_b, m_w_branch_c, m_w_out, m_post_norm, v_pre_norm, v_w_in, v_merge_bias, v_na_rpb, v_mem_norm, v_w_mem_kv, v_w_branch_a, v_w_branch_b, v_w_branch_c, v_w_out, v_post_norm):
    wt_in_s = w_in[0].T.astype(BF16)
    rows_s = jnp.concatenate([w_mem_kv[0], w_out[0]], axis=0).astype(BF16)
    cols_s = jnp.concatenate([w_branch_a[0].T, w_branch_b[0].T, w_branch_c[0].T], axis=0).astype(BF16)
    mb_s = jnp.pad(merge_bias[0], ((0, 5), (0, 0)))
    me = 4 * lax.axis_index("x") + 2 * lax.axis_index("y") + lax.axis_index("c")

    chip = 2 * lax.axis_index("x") + lax.axis_index("y")

    def first_block(q):
        return jnp.where(q == 0, 0, jnp.where(q == 1, 6, jnp.where(q == 2, 11, 17)))

    five = jnp.arange(5, dtype=jnp.int32)
    near, far = jnp.where(chip < 2, 5, 16), jnp.where(chip < 2, 16, 5)
    order1 = (first_block(chip) + five).astype(jnp.int32)
    order2 = jnp.concatenate([first_block(chip ^ 1) + five, near[None], first_block(chip ^ 2) + five]).astype(jnp.int32)
    order3 = jnp.concatenate([first_block(chip ^ 3) + five, far[None]]).astype(jnp.int32)
    tabs = _rope_tables()

    def weights_of(land):
        return land.reshape(N_IN, D_MODEL)

    land = pltpu.with_memory_space_constraint(lax.empty((N_DEV,) + wt_in_s.shape, BF16), pltpu.HBM)
    own = pltpu.with_memory_space_constraint(wt_in_s, pltpu.HBM)
    sem_a, own, land, token = _ag_phase("ag_start", own, land, [], [], [0, 1, 2])
    hs, hst = _prenorm_fold(x[0], pre_norm, token)
    _, own, land, _ = _ag_phase("ag_wait0", own, land, [sem_a], [("recv", 0)], [], hs)
    land = lax.dynamic_update_slice(land, own[None], (me, 0, 0))
    parts = _in_proj("in_proj_1", hs, weights_of(land), tabs, order1)
    sem_b, own, land, _ = _ag_phase("ag_mid1", own, land, [sem_a], [("recv", 1), ("recv", 2)], [3, 4, 5, 6], parts)
    _, own, land, _ = _ag_phase("ag_wait1", own, land, [sem_a, sem_b], [("recv", 5), ("recv", 6)], [])
    parts = _in_proj("in_proj_2", hs, weights_of(land), tabs, order2, parts)
    sem_c, own, land, _ = _ag_phase("ag_mid2", own, land, [sem_a, sem_b], [("recv", 3), ("recv", 4)], [7, 8], parts)
    _, own, land, _ = _ag_phase("ag_end", own, land, [sem_a, sem_b, sem_c],
                                [("recv", 7), ("recv", 8)] + [("send", k) for k in range(9)], [])
    wt_in = weights_of(land)

    late_own = [rows_s, cols_s, mb_s]
    late_lands = [jax.ShapeDtypeStruct((N_DEV,) + s.shape, s.dtype) for s in late_own]
    l_send, l_recv, late_own, late_lands, late_token = _exchange_start("gather_late_start", "gather", late_own,
                                                                       late_lands, after=wt_in)
    parts = _in_proj("in_proj_3", hs, wt_in, tabs, order3, parts, late_token)

    def late_weights(after):
        own, lands = _exchange_wait("gather_late_wait", "gather", l_send, l_recv, late_own, late_lands, after)
        g_rows, g_cols, g_mb = [lax.dynamic_update_slice(land, o[None], (me, 0, 0)) for land, o in zip(lands, own)]
        return (g_mb[:, :3].transpose(1, 0, 2).reshape(3, D_MODEL),
                g_rows[:, :128].reshape(D_MODEL, D_MODEL), g_cols[:, 0:128].reshape(D_MODEL, 512),
                g_cols[:, 128:256].reshape(D_MODEL, 512), g_cols[:, 256:384].reshape(D_MODEL, 512),
                g_rows[:, 128:].reshape(D_MODEL, D_MODEL))

    rs_state = []

    def reduce_start(grads):
        if "wt_in" in grads:
            own, sibling = [a.reshape(N_DEV, SHARD_IN, D_MODEL) for a in grads["wt_in"]]
            state, token = _reduce_scatter_start("w_in", ["w_in"], [own], [sibling])
        else:
            gmb_t = jnp.pad(grads["merge_bias"].reshape(3, N_DEV, 128).transpose(1, 0, 2), ((0, 0), (0, 5), (0, 0)))
            names = ["w_kv", "w_out", "a", "b", "c", "mb"]
            terms = [grads["w_kv"].reshape(N_DEV, 128, D_MODEL), grads["w_out"].reshape(N_DEV, 128, D_MODEL),
                     grads["wt_a"].reshape(N_DEV, 128, 512), grads["wt_b"].reshape(N_DEV, 128, 512),
                     grads["wt_c"].reshape(N_DEV, 128, 512), gmb_t]
            state, token = _reduce_scatter_start("rest", names, terms)
        rs_state.append(state)
        return token

    loss_term, grad_x, grads = _local_step(
        x[0], hst, parts, tabs, mem[0], loss_target[0], pre_norm, mem_norm, post_norm, na_rpb[0], wt_in,
        late_weights, reduce_start=reduce_start)

    small = jnp.concatenate([_rows128(grads["pre_norm"], 8), _rows128(grads["mem_norm"], 8),
                             _rows128(grads["post_norm"], 8), _rows128(grads["na_rpb"], 32),
                             _rows128(loss_term, 8)], axis=0)
    s_send, s_recv, s_own, s_land, s_token = _exchange_start(
        "gather_small_start", "gather", [small], [jax.ShapeDtypeStruct((N_DEV,) + small.shape, F32)])
    grad = {}
    weights = {
        "pre_norm": (pre_norm, m_pre_norm, v_pre_norm), "w_in": (w_in, m_w_in, v_w_in),
        "merge_bias": (merge_bias, m_merge_bias, v_merge_bias), "na_rpb": (na_rpb, m_na_rpb, v_na_rpb),
        "mem_norm": (mem_norm, m_mem_norm, v_mem_norm), "w_mem_kv": (w_mem_kv, m_w_mem_kv, v_w_mem_kv),
        "w_branch_a": (w_branch_a, m_w_branch_a, v_w_branch_a), "w_branch_b": (w_branch_b, m_w_branch_b, v_w_branch_b),
        "w_branch_c": (w_branch_c, m_w_branch_c, v_w_branch_c), "w_out": (w_out, m_w_out, v_w_out),
        "post_norm": (post_norm, m_post_norm, v_post_norm)}
    order = ["pre_norm", "w_in", "merge_bias", "na_rpb", "mem_norm", "w_mem_kv", "w_branch_a", "w_branch_b",
             "w_branch_c", "w_out", "post_norm"]
    delta, new_m, new_v = {}, {}, {}

    def update(n, dep=None):
        w, m, v = weights[n]
        shape = w.shape
        two_d = (-1, shape[-1])
        dl, nm, nv = _adamw("adamw_" + n, w.reshape(two_d), grad[n].reshape(two_d), m.reshape(two_d),
                            v.reshape(two_d), dep)
        delta[n], new_m[n], new_v[n] = dl.reshape(shape), nm.reshape(shape), nv.reshape(shape)
        return dl

    def update_sharded(n, sums, recv, transposed, rows=None, dep=None):
        w, m, v = weights[n]
        g, dl, nm, nv = _adamw_chips("adamw_" + n, sums, recv, w[0], m[0], v[0], transposed, rows, dep)
        grad[n], delta[n], new_m[n], new_v[n] = g[None], dl[None], nm[None], nv[None]
        return dl

    _, sums, recv2 = _reduce_scatter_wait(rs_state[0], s_token)
    dep = None
    for i, (n, transposed) in enumerate((("w_mem_kv", False), ("w_out", False), ("w_branch_a", True),
                                         ("w_branch_b", True), ("w_branch_c", True))):
        dep = update_sharded(n, sums[i], recv2[i], transposed, dep=dep)
    grad["merge_bias"] = _add_chips("add_chips_mb", sums[5], recv2[5], 8)[:3][None]
    update("merge_bias")
    s_own, s_land = _exchange_wait("gather_small_wait", "gather", s_send, s_recv, s_own, s_land, dep)
    total = _sum_devices(lax.dynamic_update_slice(s_land[0], s_own[0][None], (me, 0, 0)))
    loss = total[56, 0]
    grad.update({"pre_norm": total[0:8].reshape(1, D_MODEL), "mem_norm": total[8:16].reshape(1, D_MODEL),
                 "post_norm": total[16:24].reshape(1, D_MODEL),
                 "na_rpb": total[24:56].reshape(-1)[:8 * 15 * 31].reshape(1, 8, 15, 31)})
    dep = None
    for n in ("pre_norm", "na_rpb", "mem_norm", "post_norm"):
        dep = update(n, dep)
    _, sums_in, recv_in = _reduce_scatter_wait(rs_state[1], dep)
    update_sharded("w_in", sums_in[0], recv_in[0], True, 256)

    return (loss, grad_x[None], *[grad[n] for n in order], *[delta[n] for n in order],
            *[new_m[n] for n in order], *[new_v[n] for n in order])
```

```python
import functools

import numpy as np
import jax
import jax.numpy as jnp
from jax import lax
from jax.experimental import pallas as pl
from jax.experimental.pallas import tpu as pltpu

F32 = jnp.float32
BF16 = jnp.bfloat16

SEQ = 2048
D_MODEL = 1024
N_IN = 11264
N_DEV = 8
SHARD_IN = N_IN // N_DEV
HEAD_DIM = 64
GRID_W = 64
NA_ROWS = 8
MEM_LEN = 256
DILATIONS = (1, 4, 16)
REACH = 64
ROPE_THETA = 500000.0
ROPE_DIM = 16
EPS = 1e-6
NEG = -1e30
ADAM_LR = 0.001
ADAM_B1 = 0.9
ADAM_B2 = 0.999
ADAM_EPS = 1e-08
ADAM_WD = 0.01
ADAM_STEP = 10

VMEM_LIMIT_BYTES = 56 * 1024 * 1024
MESH_ID = pl.DeviceIdType.MESH

NN = (((1,), (0,)), ((), ()))
NT = (((1,), (1,)), ((), ()))
TN = (((0,), (0,)), ((), ()))


def _params(sem=None):
    return pltpu.CompilerParams(dimension_semantics=sem, vmem_limit_bytes=VMEM_LIMIT_BYTES)


def _iota(shape, dim):
    return lax.broadcasted_iota(jnp.int32, shape, dim)


def _sigmoid(x):
    return 1.0 / (1.0 + jnp.exp(-x))


def _rope_tables():
    half = ROPE_DIM // 2
    inv = (ROPE_THETA ** (-np.arange(half, dtype=np.float64) * 2.0 / ROPE_DIM)).astype(np.float32)
    pos = np.arange(SEQ, dtype=np.float32)
    ang = pos[:, None] * inv[None, :]
    cos, sin = np.cos(ang), np.sin(ang)
    zeros = np.zeros_like(cos)
    rest = HEAD_DIM - ROPE_DIM
    c64 = np.concatenate([cos, cos, np.ones((SEQ, rest), np.float32)], axis=1)
    s1 = np.concatenate([zeros, sin, np.zeros((SEQ, rest), np.float32)], axis=1)
    s2 = np.concatenate([-sin, zeros, np.zeros((SEQ, rest), np.float32)], axis=1)

    def fold(t, d):
        return t.reshape(SEQ // d, d, t.shape[1]).transpose(1, 0, 2).reshape(SEQ, t.shape[1])

    tabs = [np.stack([np.tile(fold(t, d), (1, 2)) for t in (c64, s1, s2)], axis=0) for d in DILATIONS]
    return jnp.asarray(np.stack(tabs, axis=0), dtype=F32)


def _rope(a, c, s1, s2):
    return a * c + pltpu.roll(a, 8, 1) * s1 + pltpu.roll(a, 120, 1) * s2


def _rope_t(a, c, s1, s2):
    return a * c + pltpu.roll(a * s1, 120, 1) + pltpu.roll(a * s2, 8, 1)


def _perm_of_block(j):
    return jnp.where(j < 3, 0, jnp.where(j < 6, 1, jnp.where(j < 9, 2, 0)))


def _mm(name, a, b, out_shape, out_dtype, grid, a_spec, b_spec, o_spec, acc_shape, dims, k_axis, nk):
    def body(a_ref, b_ref, o_ref, acc_ref):
        k = pl.program_id(k_axis)

        @pl.when(k == 0)
        def _():
            acc_ref[...] = jnp.zeros(acc_shape, F32)

        acc_ref[...] += lax.dot_general(a_ref[...], b_ref[...], dims, preferred_element_type=F32)

        @pl.when(k == nk - 1)
        def _():
            o_ref[...] = acc_ref[...].astype(out_dtype)

    sem = tuple("arbitrary" if ax == k_axis else "parallel" for ax in range(len(grid)))
    return pl.pallas_call(
        body, name=name, grid=grid, in_specs=[a_spec, b_spec], out_specs=o_spec,
        out_shape=jax.ShapeDtypeStruct(out_shape, out_dtype),
        scratch_shapes=[pltpu.VMEM(acc_shape, F32)], compiler_params=_params(sem))(a, b)


def _mm_simple(name, a, b, dims, out_dtype, tm, tn, tk):
    if dims is NN:
        m, kk = a.shape
        n = b.shape[1]
        a_spec = pl.BlockSpec((tm, tk), lambda i, j, k: (i, k))
        b_spec = pl.BlockSpec((tk, tn), lambda i, j, k: (k, j))
    elif dims is NT:
        m, kk = a.shape
        n = b.shape[0]
        a_spec = pl.BlockSpec((tm, tk), lambda i, j, k: (i, k))
        b_spec = pl.BlockSpec((tn, tk), lambda i, j, k: (j, k))
    else:
        kk, m = a.shape
        n = b.shape[1]
        a_spec = pl.BlockSpec((tk, tm), lambda i, j, k: (k, i))
        b_spec = pl.BlockSpec((tk, tn), lambda i, j, k: (k, j))
    grid = (m // tm, n // tn, kk // tk)
    o_spec = pl.BlockSpec((tm, tn), lambda i, j, k: (i, j))
    return _mm(name, a, b, (m, n), out_dtype, grid, a_spec, b_spec, o_spec, (tm, tn), dims, 2, kk // tk)


def _rmsnorm_fwd(name, x, gain, rows):
    n, d = x.shape

    def body(x_ref, g_ref, o_ref):
        xv = x_ref[...]
        rstd = lax.rsqrt(jnp.mean(xv * xv, axis=1, keepdims=True) + EPS)
        o_ref[...] = (xv * rstd * g_ref[...]).astype(BF16)

    return pl.pallas_call(
        body, name=name, grid=(n // rows,),
        in_specs=[pl.BlockSpec((rows, d), lambda i: (i, 0)), pl.BlockSpec((1, d), lambda i: (0, 0))],
        out_specs=pl.BlockSpec((rows, d), lambda i: (i, 0)),
        out_shape=jax.ShapeDtypeStruct((n, d), BF16), compiler_params=_params(("parallel",)))(x, gain)


def _folded_rows(first, rows, d):
    if d == 1:
        return pl.ds(pl.multiple_of(first, rows), rows)
    mlen = SEQ // d
    return pl.ds((first % mlen) * d + first // mlen, rows, stride=d)


def _prenorm_fold(x, gain, dep=None):
    rows = 128
    nchunk = D_MODEL // 128
    dep_specs, dep_args = _dep_operand(dep)

    def body(*refs):
        x_refs, g_ref, hs_ref, hst_ref = refs[:nchunk], refs[nchunk], refs[-2], refs[-1]
        first = pl.program_id(0) * rows
        for p, d in enumerate(DILATIONS):
            idx = _folded_rows(first, rows, d)
            xv = jnp.concatenate([r[idx, :] for r in x_refs], axis=1)
            rstd = lax.rsqrt(jnp.mean(xv * xv, axis=1, keepdims=True) + EPS)
            h = xv * rstd * g_ref[...]
            hs_ref[p] = h.astype(BF16)
            hst_ref[p] = h.T.astype(BF16)

    x_specs = [pl.BlockSpec((SEQ, 128), functools.partial(lambda c, i: (0, c), c)) for c in range(nchunk)]
    return pl.pallas_call(
        body, name="prenorm", grid=(SEQ // rows,),
        in_specs=x_specs + [pl.BlockSpec((1, D_MODEL), lambda i: (0, 0))] + dep_specs,
        out_specs=[pl.BlockSpec((3, rows, D_MODEL), lambda i: (0, i, 0)),
                   pl.BlockSpec((3, D_MODEL, rows), lambda i: (0, 0, i))],
        out_shape=[jax.ShapeDtypeStruct((3, SEQ, D_MODEL), BF16), jax.ShapeDtypeStruct((3, D_MODEL, SEQ), BF16)],
        compiler_params=_params(("parallel",)))(*([x] * nchunk), gain, *dep_args)


def _prenorm_bwd(x, gain, dh, dout):
    rows = 256

    def body(x_ref, g_ref, a_ref, do_ref, dx_ref, gg_ref):
        xv = x_ref[...]
        rstd = lax.rsqrt(jnp.mean(xv * xv, axis=1, keepdims=True) + EPS)
        xn = xv * rstd
        dh = jnp.concatenate([a_ref[c] for c in range(D_MODEL // 128)], axis=1)
        gdh = dh * g_ref[...]
        dx_ref[...] = rstd * (gdh - xn * jnp.mean(gdh * xn, axis=1, keepdims=True)) + do_ref[...]

        @pl.when(pl.program_id(0) == 0)
        def _():
            gg_ref[...] = jnp.zeros((1, D_MODEL), F32)

        gg_ref[...] += jnp.sum(dh * xn, axis=0, keepdims=True)

    row = pl.BlockSpec((rows, D_MODEL), lambda i: (i, 0))
    vec = pl.BlockSpec((1, D_MODEL), lambda i: (0, 0))
    return pl.pallas_call(
        body, name="prenorm_bwd", grid=(SEQ // rows,),
        in_specs=[row, vec, pl.BlockSpec((D_MODEL // 128, rows, 128), lambda i: (0, i, 0)), row], out_specs=[row, vec],
        out_shape=[jax.ShapeDtypeStruct((SEQ, D_MODEL), F32), jax.ShapeDtypeStruct((1, D_MODEL), F32)],
        compiler_params=_params(("arbitrary",)))(x, gain, dh, dout)


def _memnorm_bwd(mem, dmemn, dep=None):
    dep_specs, dep_args = _dep_operand(dep)

    def body(m_ref, d_ref, *rest):
        mv = m_ref[...]
        rstd = lax.rsqrt(jnp.mean(mv * mv, axis=1, keepdims=True) + EPS)
        rest[-1][...] = jnp.sum(d_ref[...] * mv * rstd, axis=0, keepdims=True)

    whole = pl.BlockSpec(memory_space=pltpu.VMEM)
    return pl.pallas_call(
        body, name="memnorm_bwd", in_specs=[whole, whole] + dep_specs,
        out_shape=jax.ShapeDtypeStruct((1, D_MODEL), F32), compiler_params=_params())(mem, dmemn, *dep_args)


def _dep_operand(dep):
    return ([], []) if dep is None else ([pl.BlockSpec(memory_space=pl.ANY)], [dep])


def _in_proj(name, hs, wt, tabs, order, prev=None, dep=None):
    tm, tn = 512, 512
    prev_specs, prev_args = ([], []) if prev is None else ([ANY], [prev])
    dep_specs, dep_args = _dep_operand(dep)

    def body(order_ref, h_ref, w_ref, t_ref, *rest):
        o_ref = rest[-1]
        j = order_ref[pl.program_id(0)]
        is_rope = jnp.logical_and(j < 9, j % 3 != 2)
        row_slices = [slice(r * tm, (r + 1) * tm) for r in range(SEQ // tm)]

        def product(rs):
            return lax.dot_general(h_ref[rs, :], w_ref[...], NT, preferred_element_type=F32)

        @pl.when(is_rope)
        def _():
            for rs in row_slices:
                acc = product(rs)
                c, s1, s2 = t_ref[0, rs, :], t_ref[1, rs, :], t_ref[2, rs, :]
                for q in range(tn // 128):
                    a = acc[:, q * 128:(q + 1) * 128]
                    o_ref[rs, q * 128:(q + 1) * 128] = _rope(a, c, s1, s2).astype(BF16)

        @pl.when(jnp.logical_not(is_rope))
        def _():
            for rs in row_slices:
                o_ref[rs, :] = product(rs).astype(BF16)

    grid_spec = pltpu.PrefetchScalarGridSpec(
        num_scalar_prefetch=1, grid=(order.shape[0],),
        in_specs=[pl.BlockSpec((None, SEQ, D_MODEL), lambda t, o: (_perm_of_block(o[t]), 0, 0)),
                  pl.BlockSpec((tn, D_MODEL), lambda t, o: (o[t], 0)),
                  pl.BlockSpec((None, 3, SEQ, 128), lambda t, o: (_perm_of_block(o[t]), 0, 0, 0))] + prev_specs
        + dep_specs,
        out_specs=pl.BlockSpec((SEQ, tn), lambda t, o: (0, o[t])))
    return pl.pallas_call(
        body, name=name, grid_spec=grid_spec, out_shape=jax.ShapeDtypeStruct((SEQ, N_IN), BF16),
        input_output_aliases={} if prev is None else {4: 0},
        compiler_params=_params(("arbitrary",)))(order, hs, wt, tabs, *prev_args, *dep_args)


def _piece_blocks(pieces):
    return [(a, h * 512) for a, p in enumerate(pieces) for h in range(p.shape[1] // 512)]


def _block_fetch(piece_refs, blocks, buf, sem):
    def start(block, slot):
        for b, (a, col) in enumerate(blocks):
            @pl.when(block == b)
            def _():
                pltpu.make_async_copy(piece_refs[a].at[:, pl.ds(col, 512)], buf.at[slot], sem.at[slot]).start()

    def wait(slot):
        pltpu.make_async_copy(piece_refs[0].at[:, pl.ds(0, 512)], buf.at[slot], sem.at[slot]).wait()

    return start, wait


def _in_proj_dw(pieces, hst, dep=None):
    tn = 512
    blocks = _piece_blocks(pieces)
    nblk = len(blocks)
    npc = len(pieces)
    dep_specs, dep_args = _dep_operand(dep)

    def body(h_ref, *rest):
        piece_refs = rest[:npc]
        o_ref, mirror, buf, sem, out_buf, send_sems, recv_sem = rest[-7:]
        j = pl.program_id(0)
        slot = j % 2
        start, wait = _block_fetch(piece_refs, blocks, buf, sem)
        x, y, c = _place()

        def to_sibling(step, slot_):
            return pltpu.make_async_remote_copy(
                src_ref=out_buf.at[slot_], dst_ref=mirror.at[pl.ds(pl.multiple_of(step * tn, tn), tn)],
                send_sem=send_sems.at[slot_], recv_sem=recv_sem, device_id=(x, y, 1 - c), device_id_type=MESH_ID)

        @pl.when(j == 0)
        def _():
            start(j, slot)

        wait(slot)

        @pl.when(j + 1 < nblk)
        def _():
            start(j + 1, 1 - slot)

        acc = jnp.dot(h_ref[...], buf[slot], preferred_element_type=F32)
        block = acc.T.astype(BF16)
        o_ref[...] = block

        @pl.when(j >= 2)
        def _():
            to_sibling(j - 2, slot).wait_send()

        out_buf[slot] = block
        to_sibling(j, slot).start()

        @pl.when(j == nblk - 1)
        def _():
            to_sibling(j - 1, 1 - slot).wait_send()
            to_sibling(j, slot).wait_send()
            pltpu.make_async_remote_copy(src_ref=mirror, dst_ref=mirror, send_sem=send_sems.at[0], recv_sem=recv_sem,
                                         device_id=(x, y, 1 - c), device_id_type=MESH_ID).wait_recv()

    return pl.pallas_call(
        body, name="in_proj_dw", grid=(nblk,),
        in_specs=[pl.BlockSpec((None, D_MODEL, SEQ), lambda j: (_perm_of_block(j), 0, 0))] + [ANY] * npc + dep_specs,
        out_specs=[pl.BlockSpec((tn, D_MODEL), lambda j: (j, 0)), ANY],
        out_shape=[jax.ShapeDtypeStruct((N_IN, D_MODEL), BF16), jax.ShapeDtypeStruct((N_IN, D_MODEL), BF16)],
        scratch_shapes=[pltpu.VMEM((2, SEQ, tn), BF16), pltpu.SemaphoreType.DMA((2,)),
                        pltpu.VMEM((2, tn, D_MODEL), BF16), pltpu.SemaphoreType.DMA((2,)), pltpu.SemaphoreType.DMA],
        compiler_params=_params(("arbitrary",)))(hst, *pieces, *dep_args)


def _in_proj_dh(pieces, wt, dep=None):
    tk = 512
    blocks = _piece_blocks(pieces)
    nblk = len(blocks)
    npc = len(pieces)
    nchunk = D_MODEL // 128

    def col(s):
        return jnp.where(s < 3, s, jnp.where(s < 16, s + 6, s - 13))

    dep_specs, dep_args = _dep_operand(dep)

    def body(w_ref, *rest):
        piece_refs = rest[:npc]
        o_ref, acc_ref, buf, sem = rest[-4:]
        s = pl.program_id(0)
        slot = s % 2
        start, wait = _block_fetch(piece_refs, blocks, buf, sem)

        @pl.when(s == 0)
        def _():
            start(col(s), slot)

        wait(slot)

        @pl.when(s + 1 < nblk)
        def _():
            start(col(s + 1), 1 - slot)

        row_slices = [slice(r * 512, (r + 1) * 512) for r in range(SEQ // 512)]

        def product(rs):
            return jnp.dot(buf[slot, rs, :], w_ref[...], preferred_element_type=F32)

        def accumulate(cond, to_out, init):
            @pl.when(cond)
            def _():
                for rs in row_slices:
                    prod = product(rs)
                    if not to_out:
                        if init:
                            acc_ref[rs, :] = prod
                        else:
                            acc_ref[rs, :] += prod
                        continue
                    for c in range(nchunk):
                        if init:
                            o_ref[c, rs, :] = prod[:, c * 128:(c + 1) * 128]
                        else:
                            o_ref[c, rs, :] += prod[:, c * 128:(c + 1) * 128]

        accumulate(s == 0, True, True)
        accumulate(jnp.logical_and(s > 0, s < 16), True, False)
        accumulate(jnp.logical_or(s == 16, s == 19), False, True)
        accumulate(jnp.logical_and(s > 16, s != 19), False, False)
        for last, d in ((18, 4), (21, 16)):
            @pl.when(s == last)
            def _():
                mlen = SEQ // d
                for r in range(d):
                    for c in range(nchunk):
                        o_ref[c, pl.ds(r, mlen, stride=d), :] += acc_ref[r * mlen:(r + 1) * mlen,
                                                                         c * 128:(c + 1) * 128]

    return pl.pallas_call(
        body, name="in_proj_dh", grid=(nblk,),
        in_specs=[pl.BlockSpec((tk, D_MODEL), lambda s: (col(s), 0))] + [ANY] * npc + dep_specs,
        out_specs=pl.BlockSpec((nchunk, SEQ, 128), lambda s: (0, 0, 0)),
        out_shape=jax.ShapeDtypeStruct((nchunk, SEQ, 128), F32),
        scratch_shapes=[pltpu.VMEM((SEQ, D_MODEL), F32), pltpu.VMEM((2, SEQ, tk), BF16),
                        pltpu.SemaphoreType.DMA((2,))],
        compiler_params=_params(("arbitrary",)))(wt, *pieces, *dep_args)


def _head_lanes(lanes, hh):
    return lanes >= 64 if hh == 1 else lanes < 64


def _head_rows(x, lanes, hh, pair):
    if not pair:
        return jnp.max(x, axis=1, keepdims=True)
    return jnp.max(jnp.where(_head_lanes(lanes, hh), x, -jnp.inf), axis=1, keepdims=True)


def _mask_head(x, lanes, hh, pair, scale=1.0):
    if not pair:
        return x
    xf = x.astype(F32) if scale == 1.0 else x.astype(F32) * scale
    return jnp.where(_head_lanes(lanes, hh), xf, 0.0).astype(BF16)


def _window(mode, qi, tq, mlen, tk):
    if mode == "dil":
        q0 = qi * tq
        seg = (q0 // mlen) * mlen
        ks = jnp.clip(q0 - REACH, seg, seg + mlen - tk)
        return pl.multiple_of(ks, 64)
    if mode == "na":
        r_start = jnp.clip(qi - NA_ROWS // 2, 0, SEQ // GRID_W - NA_ROWS)
        return pl.multiple_of(r_start * GRID_W, 64)
    return 0


def _band_mask(qi, tq, tk, ks):
    qpos = qi * tq + _iota((tq, tk), 0)
    kpos = ks + _iota((tq, tk), 1)
    return jnp.where(jnp.abs(qpos - kpos) <= REACH, 0.0, NEG).astype(F32)


def _stack_heads(x, lanes, pair, scale=1.0):
    if not pair:
        return x
    return jnp.concatenate([_mask_head(x, lanes, hh, pair, scale) for hh in range(2)], axis=0)


def _stack_rows(x, lanes, pair):
    if not pair:
        return _head_rows(x, lanes, 0, pair)
    return jnp.concatenate([_head_rows(x, lanes, hh, pair) for hh in range(2)], axis=0)


def _unstack_heads(x, lanes, pair, tq):
    if not pair:
        return x
    return jnp.where(lanes < 64, x[:tq], x[tq:])


def _scores(mode, qst, k, sscale, band, qi, bias_ref, pair):
    s = lax.dot_general(qst, k, NT, preferred_element_type=F32)
    if sscale != 1.0:
        s = s * sscale
    if mode == "dil":
        s = s + jnp.concatenate([band, band], axis=0)
    elif mode == "na":
        off = qi - jnp.clip(qi - NA_ROWS // 2, 0, SEQ // GRID_W - NA_ROWS)
        s = s + jnp.concatenate([bias_ref[0, off], bias_ref[1, off]], axis=0)
    return s


def _attn_cfg(mode, d):
    if mode == "dil":
        mlen = SEQ // d
        return dict(pair=True, tq=128, tk=min(256, mlen), mlen=mlen, lk=SEQ, scale=HEAD_DIM ** -0.5, units=4,
                    nsub=ATTN_SUBTILES)
    if mode == "na":
        return dict(pair=True, tq=GRID_W, tk=NA_ROWS * GRID_W, mlen=SEQ, lk=SEQ, scale=HEAD_DIM ** -0.5, units=4,
                    nsub=ATTN_SUBTILES)
    return dict(pair=False, tq=128, tk=MEM_LEN, mlen=SEQ, lk=MEM_LEN, scale=128 ** -0.5, units=4,
                nsub=ATTN_SUBTILES)


ATTN_SUBTILES = 16


def _attn_fwd(name, mode, q_arr, k_arr, v_arr, qcol, kcol, vcol, d=1, bias=None):
    cfg = _attn_cfg(mode, d)
    pair, tq, tk, mlen, lk, scale = cfg["pair"], cfg["tq"], cfg["tk"], cfg["mlen"], cfg["lk"], cfg["scale"]
    qscale, sscale = (scale, 1.0) if pair else (1.0, scale)
    nsub = cfg["nsub"]
    rows = nsub * tq

    def body(*refs):
        if mode == "na":
            q_ref, k_ref, v_ref, bias_ref, o_ref, l_ref = refs
        else:
            q_ref, k_ref, v_ref, o_ref, l_ref = refs
            bias_ref = None
        lanes = _iota((tq, 128), 1)
        qis = [pl.program_id(1) * nsub + sub for sub in range(nsub)]
        kss = [_window(mode, qi, tq, mlen, tk) for qi in qis]
        vs = [v_ref[pl.ds(ks, tk), :] for ks in kss]
        bands = [_band_mask(qi, tq, tk, ks) if mode == "dil" else None for qi, ks in zip(qis, kss)]
        ss = []
        for sub in range(nsub):
            qst = _stack_heads(q_ref[sub * tq:(sub + 1) * tq, :], lanes, pair, qscale)
            k = k_ref[pl.ds(kss[sub], tk), :]
            ss.append(_scores(mode, qst, k, sscale, bands[sub], qis[sub], bias_ref, pair))
        ms = [jnp.max(s_, axis=1, keepdims=True) for s_ in ss]
        ps = [jnp.exp(s_ - m) for s_, m in zip(ss, ms)]
        ls = [jnp.sum(p, axis=1, keepdims=True) for p in ps]
        os_ = [jnp.dot(p.astype(BF16), v, preferred_element_type=F32) for p, v in zip(ps, vs)]
        for sub in range(nsub):
            out = _unstack_heads(os_[sub] / ls[sub], lanes, pair, tq)
            lse = ms[sub] + jnp.log(ls[sub])
            lse = _unstack_heads(jnp.broadcast_to(lse, (lse.shape[0], 128)), lanes, pair, tq)
            dst = _folded_rows(qis[sub] * tq, tq, d) if mode == "dil" else slice(sub * tq, (sub + 1) * tq)
            o_ref[dst, :] = out
            l_ref[dst, :] = lse

    in_specs = [pl.BlockSpec((rows, 128), lambda u, i: (i, qcol + u)),
                pl.BlockSpec((lk, 128), lambda u, i: (0, kcol + u)),
                pl.BlockSpec((lk, 128), lambda u, i: (0, vcol + u))]
    args = [q_arr, k_arr, v_arr]
    if mode == "na":
        in_specs.append(pl.BlockSpec((2, NA_ROWS, GRID_W, NA_ROWS * GRID_W), lambda u, i: (u, 0, 0, 0)))
        args.append(bias)
    if mode == "dil":
        out_spec = pl.BlockSpec((SEQ, 128), lambda u, i: (0, u))
    else:
        out_spec = pl.BlockSpec((rows, 128), lambda u, i: (i, u))
    return pl.pallas_call(
        body, name=name, grid=(cfg["units"], SEQ // rows), in_specs=in_specs, out_specs=[out_spec, out_spec],
        out_shape=[jax.ShapeDtypeStruct((SEQ, 512), F32), jax.ShapeDtypeStruct((SEQ, 512), F32)],
        compiler_params=_params(("parallel", "arbitrary")))(*args)


def _attn_bwd(name, mode, q_arr, k_arr, v_arr, qcol, kcol, vcol, do, lse, dp=None, o=None, d=1, bias=None,
              tabs=None):
    cfg = _attn_cfg(mode, d)
    pair, tq, tk, mlen, lk, scale = cfg["pair"], cfg["tq"], cfg["tk"], cfg["mlen"], cfg["lk"], cfg["scale"]
    qscale, sscale = (scale, 1.0) if pair else (1.0, scale)
    nsub = cfg["nsub"]
    rows = nsub * tq
    nq = SEQ // rows
    kv_dtype = F32 if mode == "mem" else BF16

    def body(*refs):
        refs = list(refs)
        q_ref, k_ref, v_ref, do_ref, l_ref = refs[:5]
        rest = refs[5:]
        bias_ref = tq_ref = tk_ref = db_ref = None
        if mode == "dil":
            dp_ref, tq_ref, tk_ref, dq_ref, dk_ref, dv_ref, dk_acc, dv_acc = rest
        elif mode == "na":
            o_ref, bias_ref, dq_ref, dk_ref, dv_ref, db_ref, dk_acc, dv_acc = rest
        else:
            o_ref, dq_ref, dk_ref, dv_ref, dk_acc, dv_acc = rest
        step = pl.program_id(1)

        @pl.when(step == 0)
        def _():
            dk_acc[...] = jnp.zeros((lk, 128), F32)
            dv_acc[...] = jnp.zeros((lk, 128), F32)
            if mode == "na":
                db_ref[...] = jnp.zeros(db_ref.shape, F32)

        lanes = _iota((tq, 128), 1)
        qis = [step * nsub + sub for sub in range(nsub)]
        sls = [slice(sub * tq, (sub + 1) * tq) for sub in range(nsub)]
        kss = [_window(mode, qi, tq, mlen, tk) for qi in qis]
        ks_ = [k_ref[pl.ds(ks, tk), :] for ks in kss]
        vs = [v_ref[pl.ds(ks, tk), :] for ks in kss]
        qsts, dosts, lses, dphs = [], [], [], []
        for sub in range(nsub):
            if mode == "dil":
                src = _folded_rows(qis[sub] * tq, tq, d)
                dov = do_ref[src, :].astype(BF16)
                lsev = l_ref[src, :]
                dphs.append(_stack_rows(dp_ref[src, :], lanes, pair))
            else:
                dov = do_ref[sls[sub], :]
                lsev = l_ref[sls[sub], :]
                dpv = dov.astype(F32) * o_ref[sls[sub], :]
                if pair:
                    dphs.append(jnp.concatenate(
                        [jnp.sum(jnp.where(_head_lanes(lanes, hh), dpv, 0.0), axis=1, keepdims=True)
                         for hh in range(2)], axis=0))
                else:
                    dphs.append(jnp.sum(dpv, axis=1, keepdims=True))
            qsts.append(_stack_heads(q_ref[sls[sub], :], lanes, pair, qscale))
            dosts.append(_stack_heads(dov, lanes, pair))
            lses.append(_stack_rows(lsev, lanes, pair))
        bands = [_band_mask(qi, tq, tk, ks) if mode == "dil" else None for qi, ks in zip(qis, kss)]
        ss = [_scores(mode, qsts[sub], ks_[sub], sscale, bands[sub], qis[sub], bias_ref, pair) for sub in range(nsub)]
        dpms = [lax.dot_general(dosts[sub], vs[sub], NT, preferred_element_type=F32) for sub in range(nsub)]
        ps = [jnp.exp(s_ - lse) for s_, lse in zip(ss, lses)]
        dss = [p * (dpm - dph) for p, dpm, dph in zip(ps, dpms, dphs)]
        if mode == "na":
            for sub, ds in enumerate(dss):
                off = qis[sub] - jnp.clip(qis[sub] - NA_ROWS // 2, 0, SEQ // GRID_W - NA_ROWS)
                db_ref[0, off] += ds[:tq]
                db_ref[1, off] += ds[tq:]
        dsbs = [ds.astype(BF16) for ds in dss]
        dvs = [lax.dot_general(p.astype(BF16), dosts[sub], TN, preferred_element_type=F32)
               for sub, p in enumerate(ps)]
        dqs = [jnp.dot(dsb, ks_[sub], preferred_element_type=F32) * scale for sub, dsb in enumerate(dsbs)]
        dks = [lax.dot_general(dsb, qsts[sub], TN, preferred_element_type=F32) for sub, dsb in enumerate(dsbs)]
        for sub in range(nsub):
            sl = sls[sub]
            dq = _unstack_heads(dqs[sub], lanes, pair, tq)
            if mode == "dil":
                dq = _rope_t(dq, tq_ref[0, sl, :], tq_ref[1, sl, :], tq_ref[2, sl, :])
            dq_ref[sl, :] = dq.astype(BF16)
            dk_acc[pl.ds(kss[sub], tk), :] += dks[sub] if pair else dks[sub] * scale
            dv_acc[pl.ds(kss[sub], tk), :] += dvs[sub]

        @pl.when(step == nq - 1)
        def _():
            dkv = dk_acc[...]
            if mode == "dil":
                dkv = _rope_t(dkv, tk_ref[0], tk_ref[1], tk_ref[2])
            dk_ref[...] = dkv.astype(kv_dtype)
            dv_ref[...] = dv_acc[...].astype(kv_dtype)

    q_spec = pl.BlockSpec((rows, 128), lambda u, i: (i, qcol + u))
    row_spec = pl.BlockSpec((rows, 128), lambda u, i: (i, u))
    kv_out = pl.BlockSpec((lk, 128), lambda u, i: (0, u))
    whole = pl.BlockSpec((SEQ, 128), lambda u, i: (0, u))
    nat_spec = whole if mode == "dil" else row_spec
    in_specs = [q_spec,
                pl.BlockSpec((lk, 128), lambda u, i: (0, kcol + u)),
                pl.BlockSpec((lk, 128), lambda u, i: (0, vcol + u)),
                nat_spec, nat_spec]
    args = [q_arr, k_arr, v_arr, do, lse]
    out_specs = [row_spec, kv_out, kv_out]
    out_shape = [jax.ShapeDtypeStruct((SEQ, 512), BF16), jax.ShapeDtypeStruct((lk, 512), kv_dtype),
                 jax.ShapeDtypeStruct((lk, 512), kv_dtype)]
    if mode == "dil":
        in_specs += [whole, pl.BlockSpec((3, rows, 128), lambda u, i: (0, i, 0)),
                     pl.BlockSpec((3, SEQ, 128), lambda u, i: (0, 0, 0))]
        args += [dp, tabs, tabs]
    elif mode == "na":
        b_spec = pl.BlockSpec((2, NA_ROWS, GRID_W, NA_ROWS * GRID_W), lambda u, i: (u, 0, 0, 0))
        in_specs += [row_spec, b_spec]
        args += [o, bias]
        out_specs.append(b_spec)
        out_shape.append(jax.ShapeDtypeStruct((8, NA_ROWS, GRID_W, NA_ROWS * GRID_W), F32))
    else:
        in_specs.append(row_spec)
        args.append(o)
    return pl.pallas_call(
        body, name=name, grid=(cfg["units"], nq), in_specs=in_specs, out_specs=out_specs, out_shape=out_shape,
        scratch_shapes=[pltpu.VMEM((lk, 128), F32), pltpu.VMEM((lk, 128), F32)],
        compiler_params=_params(("parallel", "arbitrary")))(*args)


def _na_geometry():
    qc = _iota((GRID_W, 128), 0)
    lane = _iota((GRID_W, 128), 1)
    kc = lane & 63
    c_start = jnp.clip(qc - 8, 0, GRID_W - 16)
    valid = jnp.logical_and(kc >= c_start, kc < c_start + 16)
    return lane, valid


def _na_bias(rpb_rows, dep=None):
    dep_specs, dep_args = _dep_operand(dep)

    def body(r_ref, *rest):
        o_ref, t_ref = rest[-2:]
        lane, valid = _na_geometry()
        for dd in range(14):
            row_a = jnp.broadcast_to(r_ref[dd:dd + 1, :], (GRID_W, 128))
            row_b = jnp.broadcast_to(r_ref[dd + 1:dd + 2, :], (GRID_W, 128))
            both = jnp.where(lane < 64, row_a, pltpu.roll(row_b, 64, 1))
            t = pltpu.roll(both, 128 - 15, 1, stride=1, stride_axis=0)
            t_ref[dd] = jnp.where(valid, t, NEG)
        for off in range(NA_ROWS):
            for p in range(4):
                o_ref[off, :, p * 128:(p + 1) * 128] = t_ref[2 * p - off + 7]

    return pl.pallas_call(
        body, name="na_bias", grid=(8,),
        in_specs=[pl.BlockSpec((None, 16, 128), lambda h: (h, 0, 0))] + dep_specs,
        out_specs=pl.BlockSpec((None, NA_ROWS, GRID_W, NA_ROWS * GRID_W), lambda h: (h, 0, 0, 0)),
        out_shape=jax.ShapeDtypeStruct((8, NA_ROWS, GRID_W, NA_ROWS * GRID_W), F32),
        scratch_shapes=[pltpu.VMEM((14, GRID_W, 128), F32)],
        compiler_params=_params(("parallel",)))(rpb_rows, *dep_args)


def _na_bias_bwd(dbias, dep=None):
    dep_specs, dep_args = _dep_operand(dep)

    def body(d_ref, *rest):
        o_ref = rest[-1]
        lane, valid = _na_geometry()
        reverse = (_iota((GRID_W, GRID_W), 0) + _iota((GRID_W, GRID_W), 1) == GRID_W - 1).astype(F32)
        o_ref[...] = jnp.zeros((16, 128), F32)
        for dd in range(14):
            t = jnp.zeros((GRID_W, 128), F32)
            for off in range(NA_ROWS):
                for p in range(4):
                    if 2 * p - off + 7 == dd:
                        t = t + d_ref[off, :, p * 128:(p + 1) * 128]
            t = jnp.dot(reverse, jnp.where(valid, t, 0.0), precision=lax.Precision.HIGHEST,
                        preferred_element_type=F32)
            t = pltpu.roll(t, 128 - (GRID_W - 16), 1, stride=1, stride_axis=0)
            o_ref[dd:dd + 1, :] = jnp.sum(t, axis=0, keepdims=True)

    return pl.pallas_call(
        body, name="na_bias_bwd", grid=(8,),
        in_specs=[pl.BlockSpec((None, NA_ROWS, GRID_W, NA_ROWS * GRID_W), lambda h: (h, 0, 0, 0))] + dep_specs,
        out_specs=pl.BlockSpec((None, 16, 128), lambda h: (h, 0, 0)),
        out_shape=jax.ShapeDtypeStruct((8, 16, 128), F32),
        compiler_params=_params(("parallel",)))(dbias, *dep_args)


GATE_ROWS = 128


def _group_weights(l0, l1, l2):
    m = jnp.maximum(jnp.maximum(l0, l1), l2)
    e0, e1, e2 = jnp.exp(l0 - m), jnp.exp(l1 - m), jnp.exp(l2 - m)
    inv = 1.0 / (e0 + e1 + e2)
    return e0 * inv, e1 * inv, e2 * inv


def _gate_block(o_grp, l_grp, out_b, out_c, parts, x, target, merge_bias, wts, w_out, gain, head_sum):
    rows = GATE_ROWS
    r512 = pl.BlockSpec((rows, 512), lambda i: (i, 0))
    r1024 = pl.BlockSpec((rows, D_MODEL), lambda i: (i, 0))
    silu_cols = [pl.BlockSpec((rows, 512), functools.partial(lambda b, i: (i, b), 13 + b)) for b in range(3)]
    logit_cols = [pl.BlockSpec((rows, D_MODEL), functools.partial(lambda b, i: (i, b), 8 + b)) for b in range(3)]

    def body(o0, o1, o2, l0, l1, l2, ob, oc, ga, gb, gc, la, lb, lc, x_ref, t_ref, mb, wa, wb, wc, wo_ref, gn_ref,
             hs_ref, dout_ref, dla, dlb, dlc, dga, dgb, dgc, do0, do1, do2, dp0, dp1, dp2, dob, doc, err_ref, gg_ref,
             gmb, gwa, gwb, gwc, gwo, acc_a, acc_b, acc_c, acc_o):
        step = pl.program_id(0)
        ws = _group_weights(l0[...], l1[...], l2[...])
        out_a = ws[0] * o0[...] + ws[1] * o1[...] + ws[2] * o2[...]
        branches = ((out_a, ga, la, wa, acc_a, dla, dga), (ob[...], gb, lb, wb, acc_b, dlb, dgb),
                    (oc[...], gc, lc, wc, acc_c, dlc, dgc))

        @pl.when(step == 0)
        def _():
            for acc in (acc_a, acc_b, acc_c, acc_o):
                acc[...] = jnp.zeros(acc.shape, F32)
            err_ref[...] = jnp.zeros((1, D_MODEL), F32)
            gg_ref[...] = jnp.zeros((1, D_MODEL), F32)
            gmb[...] = jnp.zeros((3, D_MODEL), F32)

        y = jnp.zeros((rows, D_MODEL), F32)
        zs, gates, silus, dsilus, us = [], [], [], [], []
        for b, (ov, g_ref, l_ref, w_ref, _, _, _) in enumerate(branches):
            g = g_ref[...].astype(F32)
            sg = _sigmoid(g)
            silus.append(g * sg)
            dsilus.append(sg * (1.0 + g * (1.0 - sg)))
            us.append((ov * silus[b]).astype(BF16))
            zs.append(lax.dot_general(us[b], w_ref[...], NT, preferred_element_type=F32))
            gates.append(_sigmoid(l_ref[...].astype(F32) + mb[b:b + 1, :]))
            y = y + gates[b] * zs[b]
        yb = y.astype(BF16)
        y2 = jnp.dot(yb, wo_ref[...], preferred_element_type=F32)
        rstd = lax.rsqrt(jnp.mean(y2 * y2, axis=1, keepdims=True) + EPS)
        yn = y2 * rstd
        gv = gn_ref[...]
        err = x_ref[...] + yn * gv - t_ref[...]
        dout = err * (1.0 / D_MODEL)
        dout_ref[...] = dout
        dn = dout * gv
        dy2 = (rstd * (dn - yn * jnp.mean(dn * yn, axis=1, keepdims=True))).astype(BF16)
        acc_o[...] += lax.dot_general(yb, dy2, TN, preferred_element_type=F32)
        err_ref[...] += jnp.sum(err * err, axis=0, keepdims=True)
        gg_ref[...] += jnp.sum(dout * yn, axis=0, keepdims=True)
        dy = lax.dot_general(dy2, wo_ref[...], NT, preferred_element_type=F32)
        dos = []
        for b, (ov, _, _, w_ref, acc, dl_ref, dg_ref) in enumerate(branches):
            dl = dy * zs[b] * gates[b] * (1.0 - gates[b])
            dl_ref[...] = dl.astype(BF16)
            gmb[b:b + 1, :] += jnp.sum(dl, axis=0, keepdims=True)
            dz = (dy * gates[b]).astype(BF16)
            acc[...] += lax.dot_general(dz, us[b], TN, preferred_element_type=F32)
            du = jnp.dot(dz, w_ref[...], preferred_element_type=F32)
            dos.append(du * silus[b])
            dg_ref[...] = (du * ov * dsilus[b]).astype(BF16)
        dob[...] = dos[1].astype(BF16)
        doc[...] = dos[2].astype(BF16)
        row_term = jnp.dot(dos[0] * out_a, hs_ref[...], precision=lax.Precision.HIGHEST, preferred_element_type=F32)
        for wg, do_ref, dp_ref in zip(ws, (do0, do1, do2), (dp0, dp1, dp2)):
            do_ref[...] = wg * dos[0]
            dp_ref[...] = wg * row_term

        @pl.when(step == SEQ // rows - 1)
        def _():
            for acc, out in ((acc_a, gwa), (acc_b, gwb), (acc_c, gwc), (acc_o, gwo)):
                out[...] = acc[...].astype(BF16)

    full = lambda shape: pl.BlockSpec(shape, lambda i: (0,) * len(shape))
    vec = pl.BlockSpec((1, D_MODEL), lambda i: (0, 0))
    acc3 = pl.BlockSpec((3, D_MODEL), lambda i: (0, 0))
    in_specs = ([r512] * 8 + silu_cols + logit_cols + [r1024, r1024, full((3, D_MODEL))]
                + [full((D_MODEL, 512))] * 3 + [full((D_MODEL, D_MODEL)), vec, full((512, 512))])
    out_specs = ([r1024] + [r1024] * 3 + [r512] * 3 + [r512] * 6 + [r512] * 2 + [vec, vec, acc3]
                 + [full((D_MODEL, 512))] * 3 + [full((D_MODEL, D_MODEL))])
    bf, f32 = BF16, F32
    sds = jax.ShapeDtypeStruct
    out_shape = ([sds((SEQ, D_MODEL), f32)] + [sds((SEQ, D_MODEL), bf)] * 3 + [sds((SEQ, 512), bf)] * 3
                 + [sds((SEQ, 512), f32)] * 6 + [sds((SEQ, 512), bf)] * 2 + [sds((1, D_MODEL), f32)] * 2
                 + [sds((3, D_MODEL), f32)] + [sds((D_MODEL, 512), bf)] * 3 + [sds((D_MODEL, D_MODEL), bf)])
    res = pl.pallas_call(
        body, name="gate_block", grid=(SEQ // rows,), in_specs=in_specs, out_specs=out_specs, out_shape=out_shape,
        scratch_shapes=[pltpu.VMEM((D_MODEL, 512), F32)] * 3 + [pltpu.VMEM((D_MODEL, D_MODEL), F32)],
        compiler_params=_params(("arbitrary",)))(
            *o_grp, *l_grp, out_b, out_c, parts, parts, parts, parts, parts, parts, x, target, merge_bias, *wts, w_out,
            gain, head_sum)
    return dict(dout=res[0], dlog=res[1:4], dg=res[4:7], do_grp=res[7:10], dp_grp=res[10:13], do_b=res[13],
                do_c=res[14], err_sq=res[15], g_post=res[16], g_mb=res[17], g_wt=res[18:21], g_w_out=res[21])


def _local_step(x, hst, parts, tabs, bias, mem, target, pre_norm, mem_norm, post_norm, wt_in, late_weights,
                reduce_start=None):
    o_grp, l_grp = [], []
    for g, d in enumerate(DILATIONS):
        o, l = _attn_fwd("dil_fwd_%d" % g, "dil", parts, parts, parts, 12 * g, 12 * g + 4, 12 * g + 8, d=d)
        o_grp.append(o)
        l_grp.append(l)
    out_b, lse_b = _attn_fwd("na_fwd", "na", parts, parts, parts, 36, 40, 44, bias=bias)
    merge_bias, w_kv, wt_a, wt_b, wt_c, w_out = late_weights(sum(a[:8, :128] for a in [out_b] + o_grp))
    memn = _rmsnorm_fwd("memnorm", mem, mem_norm, MEM_LEN)
    kv_m = _mm_simple("mem_kv", memn, w_kv, NN, BF16, MEM_LEN, 512, D_MODEL)
    out_c, lse_c = _attn_fwd("mem_fwd", "mem", parts, kv_m, kv_m, 48, 0, 4)

    rr = _iota((512, 512), 0) // HEAD_DIM
    cc = _iota((512, 512), 1) // HEAD_DIM
    head_sum = (rr == cc).astype(F32)
    gb = _gate_block(o_grp, l_grp, out_b, out_c, parts, x, target, merge_bias, (wt_a, wt_b, wt_c), w_out, post_norm,
                     head_sum)
    dout, dlog, dg, g_wt, g_w_out = gb["dout"], gb["dlog"], gb["dg"], gb["g_wt"], gb["g_w_out"]
    do_grp, dp_grp, do_b, do_c, g_post, g_mb = (gb["do_grp"], gb["dp_grp"], gb["do_b"], gb["do_c"], gb["g_post"],
                                                gb["g_mb"])
    loss = 0.5 * jnp.sum(gb["err_sq"]) / D_MODEL

    dqkv = []
    for g, d in enumerate(DILATIONS):
        dq, dk, dv = _attn_bwd("dil_bwd_%d" % g, "dil", parts, parts, parts, 12 * g, 12 * g + 4, 12 * g + 8,
                               do_grp[g], l_grp[g], dp=dp_grp[g], d=d, tabs=tabs[g])
        dqkv += [dq, dk, dv]
    dq_b, dk_b, dv_b, dbias = _attn_bwd("na_bwd", "na", parts, parts, parts, 36, 40, 44, do_b, lse_b, o=out_b,
                                        bias=bias)
    dq_c, dk_m, dv_m = _attn_bwd("mem_bwd", "mem", parts, kv_m, kv_m, 48, 0, 4, do_c, lse_c, o=out_c)

    dkv = jnp.concatenate([dk_m, dv_m], axis=1).astype(BF16)
    g_w_kv = _mm_simple("mem_kv_dw", memn, dkv, TN, BF16, D_MODEL, 512, MEM_LEN)
    dmemn = _mm_simple("mem_kv_dx", dkv, w_kv, NT, F32, MEM_LEN, 512, D_MODEL)

    grads = dict(w_kv=g_w_kv, wt_a=g_wt[0], wt_b=g_wt[1], wt_c=g_wt[2], w_out=g_w_out, merge_bias=g_mb,
                 post_norm=g_post)
    dep = reduce_start(grads) if reduce_start is not None else None
    dparts = dqkv + [dq_b, dk_b, dv_b, dq_c] + list(dg) + list(dlog)
    grads["wt_in"] = _in_proj_dw(dparts, hst, dep)
    dep = reduce_start(grads) if reduce_start is not None else None
    dh = _in_proj_dh(dparts, wt_in, dep)
    grad_x, grads["pre_norm"] = _prenorm_bwd(x, pre_norm, dh, dout)
    g_rpb_t = _na_bias_bwd(dbias, dep)
    grads["na_rpb"] = g_rpb_t[:, :15, :31] + jnp.pad(g_rpb_t[:, :14, 64:95], ((0, 0), (1, 0), (0, 0)))
    grads["mem_norm"] = _memnorm_bwd(mem, dmemn, dep)
    return loss, grad_x, grads


ANY = pl.BlockSpec(memory_space=pl.ANY)


def _place():
    return lax.axis_index("x"), lax.axis_index("y"), lax.axis_index("c")


def _exchange_sibling(name, terms):
    nt = len(terms)

    def body(*refs):
        srcs, outs = refs[:nt], refs[nt:2 * nt]
        send_sems, recv_sems = refs[2 * nt:]
        x, y, c = _place()
        copies = []
        for q in range(4):
            for t in range(nt):
                copies.append(pltpu.make_async_remote_copy(
                    src_ref=srcs[t].at[2 * q + 1 - c], dst_ref=outs[t].at[q],
                    send_sem=send_sems.at[q * nt + t], recv_sem=recv_sems.at[q * nt + t],
                    device_id=(x, y, 1 - c), device_id_type=MESH_ID))
        for cp in copies:
            cp.start()
        for cp in copies:
            cp.wait()

    return pl.pallas_call(
        body, name=name, in_specs=[ANY] * nt, out_specs=[ANY] * nt,
        out_shape=[jax.ShapeDtypeStruct((4,) + s.shape[1:], s.dtype) for s in terms],
        scratch_shapes=[pltpu.SemaphoreType.DMA((4 * nt,)), pltpu.SemaphoreType.DMA((4 * nt,))])(*terms)


HBM = pl.BlockSpec(memory_space=pltpu.HBM)
SEM = pl.BlockSpec(memory_space=pltpu.SEMAPHORE)
DATAFLOW = pltpu.SideEffectType.DATAFLOW_SIDE_EFFECTING


def _split_copies(kind, srcs, lands, send_sems, recv_sems):
    nt = len(srcs)
    x, y, c = _place()
    copies = []
    if kind == "gather":
        me = 4 * x + 2 * y + c
        for mask in range(1, 8):
            fx, fy, fc = (mask >> 2) & 1, (mask >> 1) & 1, mask & 1
            to = (1 - x if fx else x, 1 - y if fy else y, 1 - c if fc else c)
            for t in range(nt):
                k = (mask - 1) * nt + t
                copies.append(pltpu.make_async_remote_copy(
                    src_ref=srcs[t], dst_ref=lands[t].at[me], send_sem=send_sems.at[k], recv_sem=recv_sems.at[k],
                    device_id=to, device_id_type=MESH_ID))
    else:
        for s, (tx, ty) in enumerate([(1 - x, y), (x, 1 - y), (1 - x, 1 - y)]):
            for t in range(nt):
                k = s * nt + t
                copies.append(pltpu.make_async_remote_copy(
                    src_ref=srcs[t].at[2 * tx + ty], dst_ref=lands[t].at[s], send_sem=send_sems.at[k],
                    recv_sem=recv_sems.at[k], device_id=(tx, ty, c), device_id_type=MESH_ID))
    return copies


def _split_count(kind, nt):
    return (7 if kind == "gather" else 3) * nt


def _exchange_start(name, kind, srcs, land_shapes, after=None):
    nt = len(srcs)
    n = _split_count(kind, nt)
    dep_specs, dep_args = _dep_operand(after)
    nd = len(dep_args)

    def body(*refs):
        src_refs, land_refs = refs[:nt], refs[nt:2 * nt]
        send_sems, recv_sems = refs[2 * nt + nd], refs[2 * nt + nd + 1]
        token = refs[-1]
        for cp in _split_copies(kind, src_refs, land_refs, send_sems, recv_sems):
            cp.start()
        token[...] = jnp.zeros_like(token)

    lands = [pltpu.with_memory_space_constraint(lax.empty(s.shape, s.dtype), pltpu.HBM) for s in land_shapes]
    res = pl.pallas_call(
        body, name=name,
        out_shape=(pltpu.SemaphoreType.DMA((n,)), pltpu.SemaphoreType.DMA((n,)),
                   *[pltpu.HBM(s.shape, s.dtype) for s in srcs], *[pltpu.HBM(s.shape, s.dtype) for s in land_shapes],
                   jax.ShapeDtypeStruct((8, 128), F32)),
        in_specs=[HBM] * (2 * nt) + dep_specs,
        out_specs=(SEM, SEM, *([HBM] * (2 * nt)), pl.BlockSpec(memory_space=pltpu.VMEM)),
        input_output_aliases={i: 2 + i for i in range(2 * nt)},
        compiler_params=pltpu.CompilerParams(has_side_effects=DATAFLOW))(
            *[pltpu.with_memory_space_constraint(s, pltpu.HBM) for s in srcs], *lands, *dep_args)
    return res[0], res[1], list(res[2:2 + nt]), list(res[2 + nt:2 + 2 * nt]), res[-1]


def _exchange_wait(name, kind, send_sems, recv_sems, srcs, lands, after):
    nt = len(srcs)

    def body(*refs):
        src_refs, land_refs = refs[:nt], refs[nt:2 * nt]
        s_sems, r_sems = refs[2 * nt], refs[2 * nt + 1]
        for cp in _split_copies(kind, src_refs, land_refs, s_sems, r_sems):
            cp.wait_send()
            cp.wait_recv()

    res = pl.pallas_call(
        body, name=name,
        out_shape=tuple(pltpu.HBM(s.shape, s.dtype) for s in list(srcs) + list(lands)),
        in_specs=[HBM] * (2 * nt) + [SEM, SEM, pl.BlockSpec(memory_space=pl.ANY)],
        out_specs=tuple([HBM] * (2 * nt)),
        input_output_aliases={i: i for i in range(2 * nt)},
        compiler_params=pltpu.CompilerParams(has_side_effects=DATAFLOW))(
            *srcs, *lands, send_sems, recv_sems, after)
    return list(res[:nt]), list(res[nt:])


AG_GROUPS = ((0, 3), (3, 4), (7, 2))


def _ag_phase(name, own, land, sems, waits, starts, after=None):
    r = own.shape[0]
    half = r // 2
    ns = len(sems)
    dep_specs, dep_args = _dep_operand(after)
    nd = len(dep_args)
    new_group = None
    if starts:
        (new_group,) = [g for g, (first, n) in enumerate(AG_GROUPS) if first == starts[0]]
        assert list(starts) == list(range(AG_GROUPS[new_group][0], sum(AG_GROUPS[new_group])))

    def body(*refs):
        own_ref, land_ref = refs[0], refs[1]
        sem_refs = list(refs[2:2 + 2 * ns])
        outs = refs[2 + 2 * ns + nd:]
        if starts:
            sem_refs += [outs[0], outs[1]]
        x, y, c = _place()
        me, sib = (x, y, c), (x, y, 1 - c)
        xn, yn, dg = (1 - x, y, c), (x, 1 - y, c), (1 - x, 1 - y, c)

        def other(dev):
            return (dev[0], dev[1], 1 - dev[2])

        def rows(dev, part):
            blk = land_ref.at[4 * dev[0] + 2 * dev[1] + dev[2]]
            return blk if part is None else blk.at[pl.ds(part * half, half)]

        def sem_of(k):
            (g,) = [g for g, (first, n) in enumerate(AG_GROUPS) if first <= k < first + n]
            return sem_refs[2 * g].at[k - AG_GROUPS[g][0]], sem_refs[2 * g + 1].at[k - AG_GROUPS[g][0]]

        sent = {0: (me, None, sib), 1: (me, None, xn), 2: (me, None, yn), 3: (xn, 0, yn), 4: (yn, 1, xn),
                5: (xn, None, sib), 6: (yn, None, sib), 7: (dg, 0, sib), 8: (dg, 1, sib)}
        landed = {0: (sib, None), 1: (xn, None), 2: (yn, None), 3: (dg, 0), 4: (dg, 1), 5: (other(xn), None),
                  6: (other(yn), None), 7: (other(dg), 0), 8: (other(dg), 1)}

        def copy(k, receiving):
            send_sem, recv_sem = sem_of(k)
            dev, part, to = (*landed[k], me) if receiving else sent[k]
            src = own_ref if (dev is me and not receiving) else rows(dev, part)
            return pltpu.make_async_remote_copy(src_ref=src, dst_ref=rows(dev, part), send_sem=send_sem,
                                                recv_sem=recv_sem, device_id=to, device_id_type=MESH_ID)

        for kind, k in waits:
            if kind == "recv":
                copy(k, True).wait_recv()
            else:
                copy(k, False).wait_send()
        for k in starts:
            copy(k, False).start()
        if starts:
            outs[-1][...] = jnp.zeros_like(outs[-1])

    n_new = AG_GROUPS[new_group][1] if starts else 0
    sem_out = (pltpu.SemaphoreType.DMA((n_new,)), pltpu.SemaphoreType.DMA((n_new,))) if starts else ()
    token_out = (jax.ShapeDtypeStruct((8, 128), F32),) if starts else ()
    res = pl.pallas_call(
        body, name=name,
        out_shape=(*sem_out, pltpu.HBM(own.shape, own.dtype), pltpu.HBM(land.shape, land.dtype), *token_out),
        in_specs=[HBM, HBM] + [SEM] * (2 * ns) + dep_specs,
        out_specs=(*([SEM] * len(sem_out)), HBM, HBM, *([pl.BlockSpec(memory_space=pltpu.VMEM)] * len(token_out))),
        input_output_aliases={0: len(sem_out), 1: len(sem_out) + 1},
        compiler_params=pltpu.CompilerParams(has_side_effects=DATAFLOW))(
            own, land, *[a for pair in sems for a in pair], *dep_args)
    if starts:
        return (res[0], res[1]), res[2], res[3], res[4]
    return None, res[0], res[1], None


def _add_sibling(name, term, recv, rows):
    _, r, w = term.shape
    cidx = lax.axis_index("c").astype(jnp.int32).reshape(1)
    like_term = recv.shape[0] == N_DEV

    def body(c_ref, a_ref, b_ref, o_ref):
        o_ref[...] = (a_ref[...].astype(F32) + b_ref[...].astype(F32)).astype(o_ref.dtype)

    grid_spec = pltpu.PrefetchScalarGridSpec(
        num_scalar_prefetch=1, grid=(4, r // rows),
        in_specs=[pl.BlockSpec((None, rows, w), lambda q, i, c_ref: (2 * q + c_ref[0], i, 0)),
                  pl.BlockSpec((None, rows, w), lambda q, i, c_ref: (2 * q + c_ref[0] if like_term else q, i, 0))],
        out_specs=pl.BlockSpec((None, rows, w), lambda q, i, c_ref: (q, i, 0)))
    return pl.pallas_call(
        body, name=name, grid_spec=grid_spec, out_shape=jax.ShapeDtypeStruct((4, r, w), term.dtype),
        compiler_params=_params(("parallel", "parallel")))(cidx, term, recv)


def _add_sibling_small(name, terms, recvs):
    nt = len(terms)

    def body(*refs):
        c = lax.axis_index("c")
        for t_ref, r_ref, o_ref in zip(refs[:nt], refs[nt:2 * nt], refs[2 * nt:]):
            for q in range(4):
                o_ref[q] = (t_ref[2 * q + c].astype(F32) + r_ref[q].astype(F32)).astype(o_ref.dtype)

    return pl.pallas_call(
        body, name=name, out_shape=[jax.ShapeDtypeStruct((4,) + t.shape[1:], t.dtype) for t in terms],
        compiler_params=_params())(*terms, *recvs)


def _add_chips(name, sums, recv, rows):
    _, r, w = sums.shape
    qidx = (2 * lax.axis_index("x") + lax.axis_index("y")).astype(jnp.int32).reshape(1)

    def body(q_ref, a_ref, b_ref, o_ref):
        o_ref[...] = ((a_ref[...].astype(F32) + b_ref[0].astype(F32))
                      + (b_ref[1].astype(F32) + b_ref[2].astype(F32)))

    grid_spec = pltpu.PrefetchScalarGridSpec(
        num_scalar_prefetch=1, grid=(r // rows,),
        in_specs=[pl.BlockSpec((None, rows, w), lambda i, q_ref: (q_ref[0], i, 0)),
                  pl.BlockSpec((3, rows, w), lambda i, q_ref: (0, i, 0))],
        out_specs=pl.BlockSpec((rows, w), lambda i, q_ref: (i, 0)))
    return pl.pallas_call(
        body, name=name, grid_spec=grid_spec, out_shape=jax.ShapeDtypeStruct((r, w), F32),
        compiler_params=_params(("parallel",)))(qidx, sums, recv)


def _rs_rows(a):
    return SHARD_IN // 4 if a.shape[1] == SHARD_IN else a.shape[1]


def _reduce_scatter_start(tag, names, terms, recv1=None):
    if recv1 is None:
        recv1 = _exchange_sibling("exchange_sibling_" + tag, terms)
    if len(terms) == 1:
        sums = [_add_sibling("add_sibling_" + names[0], terms[0], recv1[0], _rs_rows(terms[0]))]
    else:
        sums = _add_sibling_small("add_sibling_" + tag, terms, recv1)
    lands =[jax.ShapeDtypeStruct((3,) + s.shape[1:], s.dtype) for s in sums]
    send_sems, recv_sems, sums, lands, token = _exchange_start("exchange_chips_start_" + tag, "chips", sums, lands)
    return (tag, names, send_sems, recv_sems, sums, lands), token


def _reduce_scatter_wait(state, after):
    tag, names, send_sems, recv_sems, sums, lands = state
    sums, recv2 = _exchange_wait("exchange_chips_wait_" + tag, "chips", send_sems, recv_sems, sums, lands, after)
    return names, sums, recv2


def _adamw(name, w, g, m, v, dep=None):
    dep_specs, dep_args = _dep_operand(dep)

    def body(w_ref, g_ref, m_ref, v_ref, *rest):
        d_ref, nm_ref, nv_ref = rest[-3:]
        d_ref[...], nm_ref[...], nv_ref[...] = _adam_math(w_ref[...], g_ref[...], m_ref[...], v_ref[...])

    whole = pl.BlockSpec(memory_space=pltpu.VMEM)
    return pl.pallas_call(
        body, name=name, in_specs=[whole] * 4 + dep_specs, out_shape=[jax.ShapeDtypeStruct(w.shape, F32)] * 3,
        compiler_params=_params())(w, g, m, v, *dep_args)


def _adam_math(w, g, m, v):
    nm = ADAM_B1 * m + (1.0 - ADAM_B1) * g
    nv = ADAM_B2 * v + (1.0 - ADAM_B2) * (g * g)
    c1 = 1.0 - ADAM_B1 ** ADAM_STEP
    c2 = 1.0 - ADAM_B2 ** ADAM_STEP
    return -ADAM_LR * ((nm / c1) / (jnp.sqrt(nv / c2) + ADAM_EPS) + ADAM_WD * w), nm, nv


def _adamw_chips(name, sums, recv, w, m, v, transposed, rows=None, dep=None):
    r, c = w.shape
    rows = r if rows is None else rows
    qidx = (2 * lax.axis_index("x") + lax.axis_index("y")).astype(jnp.int32).reshape(1)
    dep_specs, dep_args = _dep_operand(dep)

    def body(q_ref, a_ref, b_ref, w_ref, m_ref, v_ref, *rest):
        g_ref, d_ref, nm_ref, nv_ref = rest[-4:]
        g = (a_ref[...].astype(F32) + b_ref[0].astype(F32)) + (b_ref[1].astype(F32) + b_ref[2].astype(F32))
        if transposed:
            g = g.T
        g_ref[...] = g
        d_ref[...], nm_ref[...], nv_ref[...] = _adam_math(w_ref[...], g, m_ref[...], v_ref[...])

    row = pl.BlockSpec((rows, c), lambda i, q_ref: (i, 0))
    if transposed:
        term_specs = [pl.BlockSpec((None, c, rows), lambda i, q_ref: (q_ref[0], 0, i)),
                      pl.BlockSpec((3, c, rows), lambda i, q_ref: (0, 0, i))]
    else:
        term_specs = [pl.BlockSpec((None, rows, c), lambda i, q_ref: (q_ref[0], i, 0)),
                      pl.BlockSpec((3, rows, c), lambda i, q_ref: (0, i, 0))]
    grid_spec = pltpu.PrefetchScalarGridSpec(
        num_scalar_prefetch=1, grid=(r // rows,), in_specs=term_specs + [row, row, row] + dep_specs,
        out_specs=[row] * 4)
    return pl.pallas_call(
        body, name=name, grid_spec=grid_spec, out_shape=[jax.ShapeDtypeStruct((r, c), F32)] * 4,
        compiler_params=_params(("parallel",)))(qidx, sums, recv, w, m, v, *dep_args)


def _sum_devices(gathered):
    def body(g_ref, o_ref):
        acc = g_ref[0]
        for j in range(1, N_DEV):
            acc = acc + g_ref[j]
        o_ref[...] = acc

    return pl.pallas_call(
        body, name="sum_devices", out_shape=jax.ShapeDtypeStruct(gathered.shape[1:], F32),
        compiler_params=_params())(gathered)


def _rows128(a, rows):
    flat = a.reshape(-1)
    return jnp.pad(flat, (0, rows * 128 - flat.shape[0])).reshape(rows, 128)


def kernel(x, mem, pre_norm, w_in, merge_bias, na_rpb, mem_norm, w_mem_kv, w_branch_a, w_branch_b, w_branch_c, w_out, post_norm, loss_target, m_pre_norm, m_w_in, m_merge_bias, m_na_rpb, m_mem_norm, m_w_mem_kv, m_w_branch_a, m_w_branch_b, m_w_branch_c, m_w_out, m_post_norm, v_pre_norm, v_w_in, v_merge_bias, v_na_rpb, v_mem_norm, v_w_mem_kv, v_w_branch_a, v_w_branch_b, v_w_branch_c, v_w_out, v_post_norm):
    wt_in_s = w_in[0].T.astype(BF16)
    rows_s = jnp.concatenate([w_mem_kv[0], w_out[0]], axis=0).astype(BF16)
    cols_s = jnp.concatenate([w_branch_a[0].T, w_branch_b[0].T, w_branch_c[0].T], axis=0).astype(BF16)
    mb_s = jnp.pad(merge_bias[0], ((0, 5), (0, 0)))
    me = 4 * lax.axis_index("x") + 2 * lax.axis_index("y") + lax.axis_index("c")

    chip = 2 * lax.axis_index("x") + lax.axis_index("y")

    def first_block(q):
        return jnp.where(q == 0, 0, jnp.where(q == 1, 6, jnp.where(q == 2, 11, 17)))

    five = jnp.arange(5, dtype=jnp.int32)
    near, far = jnp.where(chip < 2, 5, 16), jnp.where(chip < 2, 16, 5)
    order1 = (first_block(chip) + five).astype(jnp.int32)
    order2 = jnp.concatenate([first_block(chip ^ 1) + five, near[None], first_block(chip ^ 2) + five]).astype(jnp.int32)
    order3 = jnp.concatenate([first_block(chip ^ 3) + five, far[None]]).astype(jnp.int32)
    tabs = _rope_tables()

    def weights_of(land):
        return land.reshape(N_IN, D_MODEL)

    land = pltpu.with_memory_space_constraint(lax.empty((N_DEV,) + wt_in_s.shape, BF16), pltpu.HBM)
    own = pltpu.with_memory_space_constraint(wt_in_s, pltpu.HBM)
    sem_a, own, land, token = _ag_phase("ag_start", own, land, [], [], [0, 1, 2])
    hs, hst = _prenorm_fold(x[0], pre_norm, token)
    _, own, land, _ = _ag_phase("ag_wait0", own, land, [sem_a], [("recv", 0)], [], hs)
    land = lax.dynamic_update_slice(land, own[None], (me, 0, 0))
    parts = _in_proj("in_proj_1", hs, weights_of(land), tabs, order1)
    bias = _na_bias(jnp.pad(na_rpb[0], ((0, 0), (0, 1), (0, 128 - 31))), parts)
    sem_b, own, land, _ = _ag_phase("ag_mid1", own, land, [sem_a], [("recv", 1), ("recv", 2)], [3, 4, 5, 6], bias)
    _, own, land, _ = _ag_phase("ag_wait1", own, land, [sem_a, sem_b], [("recv", 5), ("recv", 6)], [])
    parts = _in_proj("in_proj_2", hs, weights_of(land), tabs, order2, parts)
    sem_c, own, land, _ = _ag_phase("ag_mid2", own, land, [sem_a, sem_b], [("recv", 3), ("recv", 4)], [7, 8], parts)
    _, own, land, _ = _ag_phase("ag_end", own, land, [sem_a, sem_b, sem_c],
                                [("recv", 7), ("recv", 8)] + [("send", k) for k in range(9)], [])
    wt_in = weights_of(land)

    late_own = [rows_s, cols_s, mb_s]
    late_lands = [jax.ShapeDtypeStruct((N_DEV,) + s.shape, s.dtype) for s in late_own]
    l_send, l_recv, late_own, late_lands, late_token = _exchange_start("gather_late_start", "gather", late_own,
                                                                       late_lands, after=wt_in)
    parts = _in_proj("in_proj_3", hs, wt_in, tabs, order3, parts, late_token)

    def late_weights(after):
        own, lands = _exchange_wait("gather_late_wait", "gather", l_send, l_recv, late_own, late_lands, after)
        g_rows, g_cols, g_mb = [lax.dynamic_update_slice(land, o[None], (me, 0, 0)) for land, o in zip(lands, own)]
        return (g_mb[:, :3].transpose(1, 0, 2).reshape(3, D_MODEL),
                g_rows[:, :128].reshape(D_MODEL, D_MODEL), g_cols[:, 0:128].reshape(D_MODEL, 512),
                g_cols[:, 128:256].reshape(D_MODEL, 512), g_cols[:, 256:384].reshape(D_MODEL, 512),
                g_rows[:, 128:].reshape(D_MODEL, D_MODEL))

    rs_state = []

    def reduce_start(grads):
        if "wt_in" in grads:
            own, sibling = [a.reshape(N_DEV, SHARD_IN, D_MODEL) for a in grads["wt_in"]]
            state, token = _reduce_scatter_start("w_in", ["w_in"], [own], [sibling])
        else:
            gmb_t = jnp.pad(grads["merge_bias"].reshape(3, N_DEV, 128).transpose(1, 0, 2), ((0, 0), (0, 5), (0, 0)))
            names = ["w_kv", "w_out", "a", "b", "c", "mb"]
            terms = [grads["w_kv"].reshape(N_DEV, 128, D_MODEL), grads["w_out"].reshape(N_DEV, 128, D_MODEL),
                     grads["wt_a"].reshape(N_DEV, 128, 512), grads["wt_b"].reshape(N_DEV, 128, 512),
                     grads["wt_c"].reshape(N_DEV, 128, 512), gmb_t]
            state, token = _reduce_scatter_start("rest", names, terms)
        rs_state.append(state)
        return token

    loss_term, grad_x, grads = _local_step(
        x[0], hst, parts, tabs, bias, mem[0], loss_target[0], pre_norm, mem_norm, post_norm, wt_in, late_weights,
        reduce_start=reduce_start)

    small = jnp.concatenate([_rows128(grads["pre_norm"], 8), _rows128(grads["mem_norm"], 8),
                             _rows128(grads["post_norm"], 8), _rows128(grads["na_rpb"], 32),
                             _rows128(loss_term, 8)], axis=0)
    s_send, s_recv, s_own, s_land, s_token = _exchange_start(
        "gather_small_start", "gather", [small], [jax.ShapeDtypeStruct((N_DEV,) + small.shape, F32)])
    grad = {}
    weights = {
        "pre_norm": (pre_norm, m_pre_norm, v_pre_norm), "w_in": (w_in, m_w_in, v_w_in),
        "merge_bias": (merge_bias, m_merge_bias, v_merge_bias), "na_rpb": (na_rpb, m_na_rpb, v_na_rpb),
        "mem_norm": (mem_norm, m_mem_norm, v_mem_norm), "w_mem_kv": (w_mem_kv, m_w_mem_kv, v_w_mem_kv),
        "w_branch_a": (w_branch_a, m_w_branch_a, v_w_branch_a), "w_branch_b": (w_branch_b, m_w_branch_b, v_w_branch_b),
        "w_branch_c": (w_branch_c, m_w_branch_c, v_w_branch_c), "w_out": (w_out, m_w_out, v_w_out),
        "post_norm": (post_norm, m_post_norm, v_post_norm)}
    order = ["pre_norm", "w_in", "merge_bias", "na_rpb", "mem_norm", "w_mem_kv", "w_branch_a", "w_branch_b",
             "w_branch_c", "w_out", "post_norm"]
    delta, new_m, new_v = {}, {}, {}

    def update(n, dep=None):
        w, m, v = weights[n]
        shape = w.shape
        two_d = (-1, shape[-1])
        dl, nm, nv = _adamw("adamw_" + n, w.reshape(two_d), grad[n].reshape(two_d), m.reshape(two_d),
                            v.reshape(two_d), dep)
        delta[n], new_m[n], new_v[n] = dl.reshape(shape), nm.reshape(shape), nv.reshape(shape)
        return dl

    def update_sharded(n, sums, recv, transposed, rows=None, dep=None):
        w, m, v = weights[n]
        g, dl, nm, nv = _adamw_chips("adamw_" + n, sums, recv, w[0], m[0], v[0], transposed, rows, dep)
        grad[n], delta[n], new_m[n], new_v[n] = g[None], dl[None], nm[None], nv[None]
        return dl

    _, sums, recv2 = _reduce_scatter_wait(rs_state[0], s_token)
    dep = None
    for i, (n, transposed) in enumerate((("w_mem_kv", False), ("w_out", False), ("w_branch_a", True),
                                         ("w_branch_b", True), ("w_branch_c", True))):
        dep = update_sharded(n, sums[i], recv2[i], transposed, dep=dep)
    grad["merge_bias"] = _add_chips("add_chips_mb", sums[5], recv2[5], 8)[:3][None]
    update("merge_bias")
    s_own, s_land = _exchange_wait("gather_small_wait", "gather", s_send, s_recv, s_own, s_land, dep)
    total = _sum_devices(lax.dynamic_update_slice(s_land[0], s_own[0][None], (me, 0, 0)))
    loss = total[56, 0]
    grad.update({"pre_norm": total[0:8].reshape(1, D_MODEL), "mem_norm": total[8:16].reshape(1, D_MODEL),
                 "post_norm": total[16:24].reshape(1, D_MODEL),
                 "na_rpb": total[24:56].reshape(-1)[:8 * 15 * 31].reshape(1, 8, 15, 31)})
    dep = None
    for n in ("pre_norm", "na_rpb", "mem_norm", "post_norm"):
        dep = update(n, dep)
    _, sums_in, recv_in = _reduce_scatter_wait(rs_state[1], dep)
    update_sharded("w_in", sums_in[0], recv_in[0], True, 256)

    return (loss, grad_x[None], *[grad[n] for n in order], *[delta[n] for n in order],
            *[new_m[n] for n in order], *[new_v[n] for n in order])
```

```python
import functools

import numpy as np
import jax
import jax.numpy as jnp
from jax import lax
from jax.experimental import pallas as pl
from jax.experimental.pallas import tpu as pltpu

F32 = jnp.float32
BF16 = jnp.bfloat16

SEQ = 2048
D_MODEL = 1024
N_IN = 11264
N_DEV = 8
SHARD_IN = N_IN // N_DEV
HEAD_DIM = 64
GRID_W = 64
NA_ROWS = 8
MEM_LEN = 256
DILATIONS = (1, 4, 16)
REACH = 64
ROPE_THETA = 500000.0
ROPE_DIM = 16
EPS = 1e-6
NEG = -1e30
ADAM_LR = 0.001
ADAM_B1 = 0.9
ADAM_B2 = 0.999
ADAM_EPS = 1e-08
ADAM_WD = 0.01
ADAM_STEP = 10

VMEM_LIMIT_BYTES = 56 * 1024 * 1024
MESH_ID = pl.DeviceIdType.MESH

NN = (((1,), (0,)), ((), ()))
NT = (((1,), (1,)), ((), ()))
TN = (((0,), (0,)), ((), ()))


def _params(sem=None):
    return pltpu.CompilerParams(dimension_semantics=sem, vmem_limit_bytes=VMEM_LIMIT_BYTES)


def _iota(shape, dim):
    return lax.broadcasted_iota(jnp.int32, shape, dim)


def _sigmoid(x):
    return 1.0 / (1.0 + jnp.exp(-x))


def _rope_tables():
    half = ROPE_DIM // 2
    inv = (ROPE_THETA ** (-np.arange(half, dtype=np.float64) * 2.0 / ROPE_DIM)).astype(np.float32)
    pos = np.arange(SEQ, dtype=np.float32)
    ang = pos[:, None] * inv[None, :]
    cos, sin = np.cos(ang), np.sin(ang)
    zeros = np.zeros_like(cos)
    rest = HEAD_DIM - ROPE_DIM
    c64 = np.concatenate([cos, cos, np.ones((SEQ, rest), np.float32)], axis=1)
    s1 = np.concatenate([zeros, sin, np.zeros((SEQ, rest), np.float32)], axis=1)
    s2 = np.concatenate([-sin, zeros, np.zeros((SEQ, rest), np.float32)], axis=1)

    def fold(t, d):
        return t.reshape(SEQ // d, d, t.shape[1]).transpose(1, 0, 2).reshape(SEQ, t.shape[1])

    tabs = [np.stack([np.tile(fold(t, d), (1, 2)) for t in (c64, s1, s2)], axis=0) for d in DILATIONS]
    return jnp.asarray(np.stack(tabs, axis=0), dtype=F32)


def _rope(a, c, s1, s2):
    return a * c + pltpu.roll(a, 8, 1) * s1 + pltpu.roll(a, 120, 1) * s2


def _rope_t(a, c, s1, s2):
    return a * c + pltpu.roll(a * s1, 120, 1) + pltpu.roll(a * s2, 8, 1)


def _perm_of_block(j):
    return jnp.where(j < 3, 0, jnp.where(j < 6, 1, jnp.where(j < 9, 2, 0)))


def _mm(name, a, b, out_shape, out_dtype, grid, a_spec, b_spec, o_spec, acc_shape, dims, k_axis, nk):
    def body(a_ref, b_ref, o_ref, acc_ref):
        k = pl.program_id(k_axis)

        @pl.when(k == 0)
        def _():
            acc_ref[...] = jnp.zeros(acc_shape, F32)

        acc_ref[...] += lax.dot_general(a_ref[...], b_ref[...], dims, preferred_element_type=F32)

        @pl.when(k == nk - 1)
        def _():
            o_ref[...] = acc_ref[...].astype(out_dtype)

    sem = tuple("arbitrary" if ax == k_axis else "parallel" for ax in range(len(grid)))
    return pl.pallas_call(
        body, name=name, grid=grid, in_specs=[a_spec, b_spec], out_specs=o_spec,
        out_shape=jax.ShapeDtypeStruct(out_shape, out_dtype),
        scratch_shapes=[pltpu.VMEM(acc_shape, F32)], compiler_params=_params(sem))(a, b)


def _mm_simple(name, a, b, dims, out_dtype, tm, tn, tk):
    if dims is NN:
        m, kk = a.shape
        n = b.shape[1]
        a_spec = pl.BlockSpec((tm, tk), lambda i, j, k: (i, k))
        b_spec = pl.BlockSpec((tk, tn), lambda i, j, k: (k, j))
    elif dims is NT:
        m, kk = a.shape
        n = b.shape[0]
        a_spec = pl.BlockSpec((tm, tk), lambda i, j, k: (i, k))
        b_spec = pl.BlockSpec((tn, tk), lambda i, j, k: (j, k))
    else:
        kk, m = a.shape
        n = b.shape[1]
        a_spec = pl.BlockSpec((tk, tm), lambda i, j, k: (k, i))
        b_spec = pl.BlockSpec((tk, tn), lambda i, j, k: (k, j))
    grid = (m // tm, n // tn, kk // tk)
    o_spec = pl.BlockSpec((tm, tn), lambda i, j, k: (i, j))
    return _mm(name, a, b, (m, n), out_dtype, grid, a_spec, b_spec, o_spec, (tm, tn), dims, 2, kk // tk)


def _rmsnorm_fwd(name, x, gain, rows):
    n, d = x.shape

    def body(x_ref, g_ref, o_ref):
        xv = x_ref[...]
        rstd = lax.rsqrt(jnp.mean(xv * xv, axis=1, keepdims=True) + EPS)
        o_ref[...] = (xv * rstd * g_ref[...]).astype(BF16)

    return pl.pallas_call(
        body, name=name, grid=(n // rows,),
        in_specs=[pl.BlockSpec((rows, d), lambda i: (i, 0)), pl.BlockSpec((1, d), lambda i: (0, 0))],
        out_specs=pl.BlockSpec((rows, d), lambda i: (i, 0)),
        out_shape=jax.ShapeDtypeStruct((n, d), BF16), compiler_params=_params(("parallel",)))(x, gain)


def _folded_rows(first, rows, d):
    if d == 1:
        return pl.ds(pl.multiple_of(first, rows), rows)
    mlen = SEQ // d
    return pl.ds((first % mlen) * d + first // mlen, rows, stride=d)


def _prenorm_fold(x, gain, dep=None):
    rows = 128
    nchunk = D_MODEL // 128
    dep_specs, dep_args = _dep_operand(dep)

    def body(*refs):
        x_refs, g_ref, hs_ref, hst_ref = refs[:nchunk], refs[nchunk], refs[-2], refs[-1]
        first = pl.program_id(0) * rows
        for p, d in enumerate(DILATIONS):
            idx = _folded_rows(first, rows, d)
            xv = jnp.concatenate([r[idx, :] for r in x_refs], axis=1)
            rstd = lax.rsqrt(jnp.mean(xv * xv, axis=1, keepdims=True) + EPS)
            h = xv * rstd * g_ref[...]
            hs_ref[p] = h.astype(BF16)
            hst_ref[p] = h.T.astype(BF16)

    x_specs = [pl.BlockSpec((SEQ, 128), functools.partial(lambda c, i: (0, c), c)) for c in range(nchunk)]
    return pl.pallas_call(
        body, name="prenorm", grid=(SEQ // rows,),
        in_specs=x_specs + [pl.BlockSpec((1, D_MODEL), lambda i: (0, 0))] + dep_specs,
        out_specs=[pl.BlockSpec((3, rows, D_MODEL), lambda i: (0, i, 0)),
                   pl.BlockSpec((3, D_MODEL, rows), lambda i: (0, 0, i))],
        out_shape=[jax.ShapeDtypeStruct((3, SEQ, D_MODEL), BF16), jax.ShapeDtypeStruct((3, D_MODEL, SEQ), BF16)],
        compiler_params=_params(("parallel",)))(*([x] * nchunk), gain, *dep_args)


def _prenorm_bwd(x, gain, dh, dout):
    rows = 256

    def body(x_ref, g_ref, a_ref, do_ref, dx_ref, gg_ref):
        xv = x_ref[...]
        rstd = lax.rsqrt(jnp.mean(xv * xv, axis=1, keepdims=True) + EPS)
        xn = xv * rstd
        dh = jnp.concatenate([a_ref[c] for c in range(D_MODEL // 128)], axis=1)
        gdh = dh * g_ref[...]
        dx_ref[...] = rstd * (gdh - xn * jnp.mean(gdh * xn, axis=1, keepdims=True)) + do_ref[...]

        @pl.when(pl.program_id(0) == 0)
        def _():
            gg_ref[...] = jnp.zeros((1, D_MODEL), F32)

        gg_ref[...] += jnp.sum(dh * xn, axis=0, keepdims=True)

    row = pl.BlockSpec((rows, D_MODEL), lambda i: (i, 0))
    vec = pl.BlockSpec((1, D_MODEL), lambda i: (0, 0))
    return pl.pallas_call(
        body, name="prenorm_bwd", grid=(SEQ // rows,),
        in_specs=[row, vec, pl.BlockSpec((D_MODEL // 128, rows, 128), lambda i: (0, i, 0)), row], out_specs=[row, vec],
        out_shape=[jax.ShapeDtypeStruct((SEQ, D_MODEL), F32), jax.ShapeDtypeStruct((1, D_MODEL), F32)],
        compiler_params=_params(("arbitrary",)))(x, gain, dh, dout)


def _memnorm_bwd(mem, dmemn, dep=None):
    dep_specs, dep_args = _dep_operand(dep)

    def body(m_ref, d_ref, *rest):
        mv = m_ref[...]
        rstd = lax.rsqrt(jnp.mean(mv * mv, axis=1, keepdims=True) + EPS)
        rest[-1][...] = jnp.sum(d_ref[...] * mv * rstd, axis=0, keepdims=True)

    whole = pl.BlockSpec(memory_space=pltpu.VMEM)
    return pl.pallas_call(
        body, name="memnorm_bwd", in_specs=[whole, whole] + dep_specs,
        out_shape=jax.ShapeDtypeStruct((1, D_MODEL), F32), compiler_params=_params())(mem, dmemn, *dep_args)


def _dep_operand(dep):
    return ([], []) if dep is None else ([pl.BlockSpec(memory_space=pl.ANY)], [dep])


def _in_proj(name, hs, wt, tabs, order, prev=None, dep=None):
    tm, tn = 512, 512
    prev_specs, prev_args = ([], []) if prev is None else ([ANY], [prev])
    dep_specs, dep_args = _dep_operand(dep)

    def body(order_ref, h_ref, w_ref, t_ref, *rest):
        o_ref = rest[-1]
        j = order_ref[pl.program_id(0)]
        is_rope = jnp.logical_and(j < 9, j % 3 != 2)
        row_slices = [slice(r * tm, (r + 1) * tm) for r in range(SEQ // tm)]

        def product(rs):
            return lax.dot_general(h_ref[rs, :], w_ref[...], NT, preferred_element_type=F32)

        @pl.when(is_rope)
        def _():
            for rs in row_slices:
                acc = product(rs)
                c, s1, s2 = t_ref[0, rs, :], t_ref[1, rs, :], t_ref[2, rs, :]
                for q in range(tn // 128):
                    a = acc[:, q * 128:(q + 1) * 128]
                    o_ref[rs, q * 128:(q + 1) * 128] = _rope(a, c, s1, s2).astype(BF16)

        @pl.when(jnp.logical_not(is_rope))
        def _():
            for rs in row_slices:
                o_ref[rs, :] = product(rs).astype(BF16)

    grid_spec = pltpu.PrefetchScalarGridSpec(
        num_scalar_prefetch=1, grid=(order.shape[0],),
        in_specs=[pl.BlockSpec((None, SEQ, D_MODEL), lambda t, o: (_perm_of_block(o[t]), 0, 0)),
                  pl.BlockSpec((tn, D_MODEL), lambda t, o: (o[t], 0)),
                  pl.BlockSpec((None, 3, SEQ, 128), lambda t, o: (_perm_of_block(o[t]), 0, 0, 0))] + prev_specs
        + dep_specs,
        out_specs=pl.BlockSpec((SEQ, tn), lambda t, o: (0, o[t])))
    return pl.pallas_call(
        body, name=name, grid_spec=grid_spec, out_shape=jax.ShapeDtypeStruct((SEQ, N_IN), BF16),
        input_output_aliases={} if prev is None else {4: 0},
        compiler_params=_params(("arbitrary",)))(order, hs, wt, tabs, *prev_args, *dep_args)


def _piece_blocks(pieces):
    return [(a, h * 512) for a, p in enumerate(pieces) for h in range(p.shape[1] // 512)]


def _block_fetch(piece_refs, blocks, buf, sem):
    def start(block, slot):
        for b, (a, col) in enumerate(blocks):
            @pl.when(block == b)
            def _():
                pltpu.make_async_copy(piece_refs[a].at[:, pl.ds(col, 512)], buf.at[slot], sem.at[slot]).start()

    def wait(slot):
        pltpu.make_async_copy(piece_refs[0].at[:, pl.ds(0, 512)], buf.at[slot], sem.at[slot]).wait()

    return start, wait


def _in_proj_dw(pieces, hst, dep=None):
    tn = 512
    blocks = _piece_blocks(pieces)
    nblk = len(blocks)
    npc = len(pieces)
    dep_specs, dep_args = _dep_operand(dep)

    def body(h_ref, *rest):
        piece_refs = rest[:npc]
        own_out, mirror, buf, sem, out_buf, send_sems, recv_sem, local_sems = rest[-8:]
        j = pl.program_id(0)
        slot = j % 2
        start, wait = _block_fetch(piece_refs, blocks, buf, sem)
        x, y, c = _place()

        def rows_of(step):
            return pl.ds(pl.multiple_of(step * tn, tn), tn)

        def to_sibling(step, slot_):
            return pltpu.make_async_remote_copy(
                src_ref=out_buf.at[slot_], dst_ref=mirror.at[rows_of(step)],
                send_sem=send_sems.at[slot_], recv_sem=recv_sem, device_id=(x, y, 1 - c), device_id_type=MESH_ID)

        def to_own(step, slot_):
            return pltpu.make_async_copy(out_buf.at[slot_], own_out.at[rows_of(step)], local_sems.at[slot_])

        @pl.when(j == 0)
        def _():
            start(j, slot)

        wait(slot)

        @pl.when(j + 1 < nblk)
        def _():
            start(j + 1, 1 - slot)

        acc = jnp.dot(h_ref[...], buf[slot], preferred_element_type=F32)

        @pl.when(j >= 2)
        def _():
            to_sibling(j - 2, slot).wait_send()
            to_own(j - 2, slot).wait()

        out_buf[slot] = acc.T.astype(BF16)
        to_sibling(j, slot).start()
        to_own(j, slot).start()

        @pl.when(j == nblk - 1)
        def _():
            to_sibling(j - 1, 1 - slot).wait_send()
            to_own(j - 1, 1 - slot).wait()
            to_sibling(j, slot).wait_send()
            to_own(j, slot).wait()
            pltpu.make_async_remote_copy(src_ref=mirror, dst_ref=mirror, send_sem=send_sems.at[0], recv_sem=recv_sem,
                                         device_id=(x, y, 1 - c), device_id_type=MESH_ID).wait_recv()

    return pl.pallas_call(
        body, name="in_proj_dw", grid=(nblk,),
        in_specs=[pl.BlockSpec((None, D_MODEL, SEQ), lambda j: (_perm_of_block(j), 0, 0))] + [ANY] * npc + dep_specs,
        out_specs=[ANY, ANY],
        out_shape=[jax.ShapeDtypeStruct((N_IN, D_MODEL), BF16), jax.ShapeDtypeStruct((N_IN, D_MODEL), BF16)],
        scratch_shapes=[pltpu.VMEM((2, SEQ, tn), BF16), pltpu.SemaphoreType.DMA((2,)),
                        pltpu.VMEM((2, tn, D_MODEL), BF16), pltpu.SemaphoreType.DMA((2,)), pltpu.SemaphoreType.DMA,
                        pltpu.SemaphoreType.DMA((2,))],
        compiler_params=_params(("arbitrary",)))(hst, *pieces, *dep_args)


def _in_proj_dh(pieces, wt, dep=None):
    tk = 512
    blocks = _piece_blocks(pieces)
    nblk = len(blocks)
    npc = len(pieces)
    nchunk = D_MODEL // 128

    def col(s):
        return jnp.where(s < 3, s, jnp.where(s < 16, s + 6, s - 13))

    dep_specs, dep_args = _dep_operand(dep)

    def body(w_ref, *rest):
        piece_refs = rest[:npc]
        o_ref, acc_ref, buf, sem = rest[-4:]
        s = pl.program_id(0)
        slot = s % 2
        start, wait = _block_fetch(piece_refs, blocks, buf, sem)

        @pl.when(s == 0)
        def _():
            start(col(s), slot)

        wait(slot)

        @pl.when(s + 1 < nblk)
        def _():
            start(col(s + 1), 1 - slot)

        row_slices = [slice(r * 512, (r + 1) * 512) for r in range(SEQ // 512)]

        def product(rs):
            return jnp.dot(buf[slot, rs, :], w_ref[...], preferred_element_type=F32)

        def accumulate(cond, to_out, init):
            @pl.when(cond)
            def _():
                for rs in row_slices:
                    prod = product(rs)
                    if not to_out:
                        if init:
                            acc_ref[rs, :] = prod
                        else:
                            acc_ref[rs, :] += prod
                        continue
                    for c in range(nchunk):
                        if init:
                            o_ref[c, rs, :] = prod[:, c * 128:(c + 1) * 128]
                        else:
                            o_ref[c, rs, :] += prod[:, c * 128:(c + 1) * 128]

        accumulate(s == 0, True, True)
        accumulate(jnp.logical_and(s > 0, s < 16), True, False)
        accumulate(jnp.logical_or(s == 16, s == 19), False, True)
        accumulate(jnp.logical_and(s > 16, s != 19), False, False)
        for last, d in ((18, 4), (21, 16)):
            @pl.when(s == last)
            def _():
                mlen = SEQ // d
                for r in range(d):
                    for c in range(nchunk):
                        o_ref[c, pl.ds(r, mlen, stride=d), :] += acc_ref[r * mlen:(r + 1) * mlen,
                                                                         c * 128:(c + 1) * 128]

    return pl.pallas_call(
        body, name="in_proj_dh", grid=(nblk,),
        in_specs=[pl.BlockSpec((tk, D_MODEL), lambda s: (col(s), 0))] + [ANY] * npc + dep_specs,
        out_specs=pl.BlockSpec((nchunk, SEQ, 128), lambda s: (0, 0, 0)),
        out_shape=jax.ShapeDtypeStruct((nchunk, SEQ, 128), F32),
        scratch_shapes=[pltpu.VMEM((SEQ, D_MODEL), F32), pltpu.VMEM((2, SEQ, tk), BF16),
                        pltpu.SemaphoreType.DMA((2,))],
        compiler_params=_params(("arbitrary",)))(wt, *pieces, *dep_args)


def _head_lanes(lanes, hh):
    return lanes >= 64 if hh == 1 else lanes < 64


def _head_rows(x, lanes, hh, pair):
    if not pair:
        return jnp.max(x, axis=1, keepdims=True)
    return jnp.max(jnp.where(_head_lanes(lanes, hh), x, -jnp.inf), axis=1, keepdims=True)


def _mask_head(x, lanes, hh, pair, scale=1.0):
    if not pair:
        return x
    xf = x.astype(F32) if scale == 1.0 else x.astype(F32) * scale
    return jnp.where(_head_lanes(lanes, hh), xf, 0.0).astype(BF16)


def _window(mode, qi, tq, mlen, tk):
    if mode == "dil":
        q0 = qi * tq
        seg = (q0 // mlen) * mlen
        ks = jnp.clip(q0 - REACH, seg, seg + mlen - tk)
        return pl.multiple_of(ks, 64)
    if mode == "na":
        r_start = jnp.clip(qi - NA_ROWS // 2, 0, SEQ // GRID_W - NA_ROWS)
        return pl.multiple_of(r_start * GRID_W, 64)
    return 0


def _band_mask(qi, tq, tk, ks):
    qpos = qi * tq + _iota((tq, tk), 0)
    kpos = ks + _iota((tq, tk), 1)
    return jnp.where(jnp.abs(qpos - kpos) <= REACH, 0.0, NEG).astype(F32)


def _stack_heads(x, lanes, pair, scale=1.0):
    if not pair:
        return x
    return jnp.concatenate([_mask_head(x, lanes, hh, pair, scale) for hh in range(2)], axis=0)


def _stack_rows(x, lanes, pair):
    if not pair:
        return _head_rows(x, lanes, 0, pair)
    return jnp.concatenate([_head_rows(x, lanes, hh, pair) for hh in range(2)], axis=0)


def _unstack_heads(x, lanes, pair, tq):
    if not pair:
        return x
    return jnp.where(lanes < 64, x[:tq], x[tq:])


def _scores(mode, qst, k, sscale, band, qi, bias_ref, pair):
    s = lax.dot_general(qst, k, NT, preferred_element_type=F32)
    if sscale != 1.0:
        s = s * sscale
    if mode == "dil":
        s = s + jnp.concatenate([band, band], axis=0)
    elif mode == "na":
        off = qi - jnp.clip(qi - NA_ROWS // 2, 0, SEQ // GRID_W - NA_ROWS)
        s = s + jnp.concatenate([bias_ref[0, off], bias_ref[1, off]], axis=0)
    return s


def _attn_cfg(mode, d):
    if mode == "dil":
        mlen = SEQ // d
        return dict(pair=True, tq=128, tk=min(256, mlen), mlen=mlen, lk=SEQ, scale=HEAD_DIM ** -0.5, units=4,
                    nsub=ATTN_SUBTILES)
    if mode == "na":
        return dict(pair=True, tq=GRID_W, tk=NA_ROWS * GRID_W, mlen=SEQ, lk=SEQ, scale=HEAD_DIM ** -0.5, units=4,
                    nsub=ATTN_SUBTILES)
    return dict(pair=False, tq=128, tk=MEM_LEN, mlen=SEQ, lk=MEM_LEN, scale=128 ** -0.5, units=4,
                nsub=ATTN_SUBTILES)


ATTN_SUBTILES = 16


def _attn_fwd(name, mode, q_arr, k_arr, v_arr, qcol, kcol, vcol, d=1, bias=None):
    cfg = _attn_cfg(mode, d)
    pair, tq, tk, mlen, lk, scale = cfg["pair"], cfg["tq"], cfg["tk"], cfg["mlen"], cfg["lk"], cfg["scale"]
    qscale, sscale = (scale, 1.0) if pair else (1.0, scale)
    nsub = cfg["nsub"]
    rows = nsub * tq

    def body(*refs):
        if mode == "na":
            q_ref, k_ref, v_ref, bias_ref, o_ref, l_ref = refs
        else:
            q_ref, k_ref, v_ref, o_ref, l_ref = refs
            bias_ref = None
        lanes = _iota((tq, 128), 1)
        qis = [pl.program_id(1) * nsub + sub for sub in range(nsub)]
        kss = [_window(mode, qi, tq, mlen, tk) for qi in qis]
        vs = [v_ref[pl.ds(ks, tk), :] for ks in kss]
        bands = [_band_mask(qi, tq, tk, ks) if mode == "dil" else None for qi, ks in zip(qis, kss)]
        ss = []
        for sub in range(nsub):
            qst = _stack_heads(q_ref[sub * tq:(sub + 1) * tq, :], lanes, pair, qscale)
            k = k_ref[pl.ds(kss[sub], tk), :]
            ss.append(_scores(mode, qst, k, sscale, bands[sub], qis[sub], bias_ref, pair))
        ms = [jnp.max(s_, axis=1, keepdims=True) for s_ in ss]
        ps = [jnp.exp(s_ - m) for s_, m in zip(ss, ms)]
        ls = [jnp.sum(p, axis=1, keepdims=True) for p in ps]
        os_ = [jnp.dot(p.astype(BF16), v, preferred_element_type=F32) for p, v in zip(ps, vs)]
        for sub in range(nsub):
            out = _unstack_heads(os_[sub] / ls[sub], lanes, pair, tq)
            lse = ms[sub] + jnp.log(ls[sub])
            lse = _unstack_heads(jnp.broadcast_to(lse, (lse.shape[0], 128)), lanes, pair, tq)
            dst = _folded_rows(qis[sub] * tq, tq, d) if mode == "dil" else slice(sub * tq, (sub + 1) * tq)
            o_ref[dst, :] = out
            l_ref[dst, :] = lse

    in_specs = [pl.BlockSpec((rows, 128), lambda u, i: (i, qcol + u)),
                pl.BlockSpec((lk, 128), lambda u, i: (0, kcol + u)),
                pl.BlockSpec((lk, 128), lambda u, i: (0, vcol + u))]
    args = [q_arr, k_arr, v_arr]
    if mode == "na":
        in_specs.append(pl.BlockSpec((2, NA_ROWS, GRID_W, NA_ROWS * GRID_W), lambda u, i: (u, 0, 0, 0)))
        args.append(bias)
    if mode == "dil":
        out_spec = pl.BlockSpec((SEQ, 128), lambda u, i: (0, u))
    else:
        out_spec = pl.BlockSpec((rows, 128), lambda u, i: (i, u))
    return pl.pallas_call(
        body, name=name, grid=(cfg["units"], SEQ // rows), in_specs=in_specs, out_specs=[out_spec, out_spec],
        out_shape=[jax.ShapeDtypeStruct((SEQ, 512), F32), jax.ShapeDtypeStruct((SEQ, 512), F32)],
        compiler_params=_params(("parallel", "arbitrary")))(*args)


def _attn_bwd(name, mode, q_arr, k_arr, v_arr, qcol, kcol, vcol, do, lse, dp=None, o=None, d=1, bias=None,
              tabs=None):
    cfg = _attn_cfg(mode, d)
    pair, tq, tk, mlen, lk, scale = cfg["pair"], cfg["tq"], cfg["tk"], cfg["mlen"], cfg["lk"], cfg["scale"]
    qscale, sscale = (scale, 1.0) if pair else (1.0, scale)
    nsub = cfg["nsub"]
    rows = nsub * tq
    nq = SEQ // rows
    kv_dtype = F32 if mode == "mem" else BF16

    def body(*refs):
        refs = list(refs)
        q_ref, k_ref, v_ref, do_ref, l_ref = refs[:5]
        rest = refs[5:]
        bias_ref = tq_ref = tk_ref = db_ref = None
        if mode == "dil":
            dp_ref, tq_ref, tk_ref, dq_ref, dk_ref, dv_ref, dk_acc, dv_acc = rest
        elif mode == "na":
            o_ref, bias_ref, dq_ref, dk_ref, dv_ref, db_ref, dk_acc, dv_acc = rest
        else:
            o_ref, dq_ref, dk_ref, dv_ref, dk_acc, dv_acc = rest
        step = pl.program_id(1)

        @pl.when(step == 0)
        def _():
            dk_acc[...] = jnp.zeros((lk, 128), F32)
            dv_acc[...] = jnp.zeros((lk, 128), F32)
            if mode == "na":
                db_ref[...] = jnp.zeros(db_ref.shape, F32)

        lanes = _iota((tq, 128), 1)
        qis = [step * nsub + sub for sub in range(nsub)]
        sls = [slice(sub * tq, (sub + 1) * tq) for sub in range(nsub)]
        kss = [_window(mode, qi, tq, mlen, tk) for qi in qis]
        ks_ = [k_ref[pl.ds(ks, tk), :] for ks in kss]
        vs = [v_ref[pl.ds(ks, tk), :] for ks in kss]
        qsts, dosts, lses, dphs = [], [], [], []
        for sub in range(nsub):
            if mode == "dil":
                src = _folded_rows(qis[sub] * tq, tq, d)
                dov = do_ref[src, :].astype(BF16)
                lsev = l_ref[src, :]
                dphs.append(_stack_rows(dp_ref[src, :], lanes, pair))
            else:
                dov = do_ref[sls[sub], :]
                lsev = l_ref[sls[sub], :]
                dpv = dov.astype(F32) * o_ref[sls[sub], :]
                if pair:
                    dphs.append(jnp.concatenate(
                        [jnp.sum(jnp.where(_head_lanes(lanes, hh), dpv, 0.0), axis=1, keepdims=True)
                         for hh in range(2)], axis=0))
                else:
                    dphs.append(jnp.sum(dpv, axis=1, keepdims=True))
            qsts.append(_stack_heads(q_ref[sls[sub], :], lanes, pair, qscale))
            dosts.append(_stack_heads(dov, lanes, pair))
            lses.append(_stack_rows(lsev, lanes, pair))
        bands = [_band_mask(qi, tq, tk, ks) if mode == "dil" else None for qi, ks in zip(qis, kss)]
        ss = [_scores(mode, qsts[sub], ks_[sub], sscale, bands[sub], qis[sub], bias_ref, pair) for sub in range(nsub)]
        dpms = [lax.dot_general(dosts[sub], vs[sub], NT, preferred_element_type=F32) for sub in range(nsub)]
        ps = [jnp.exp(s_ - lse) for s_, lse in zip(ss, lses)]
        dss = [p * (dpm - dph) for p, dpm, dph in zip(ps, dpms, dphs)]
        if mode == "na":
            for sub, ds in enumerate(dss):
                off = qis[sub] - jnp.clip(qis[sub] - NA_ROWS // 2, 0, SEQ // GRID_W - NA_ROWS)
                db_ref[0, off] += ds[:tq]
                db_ref[1, off] += ds[tq:]
        dsbs = [ds.astype(BF16) for ds in dss]
        dvs = [lax.dot_general(p.astype(BF16), dosts[sub], TN, preferred_element_type=F32)
               for sub, p in enumerate(ps)]
        dqs = [jnp.dot(dsb, ks_[sub], preferred_element_type=F32) * scale for sub, dsb in enumerate(dsbs)]
        dks = [lax.dot_general(dsb, qsts[sub], TN, preferred_element_type=F32) for sub, dsb in enumerate(dsbs)]
        for sub in range(nsub):
            sl = sls[sub]
            dq = _unstack_heads(dqs[sub], lanes, pair, tq)
            if mode == "dil":
                dq = _rope_t(dq, tq_ref[0, sl, :], tq_ref[1, sl, :], tq_ref[2, sl, :])
            dq_ref[sl, :] = dq.astype(BF16)
            dk_acc[pl.ds(kss[sub], tk), :] += dks[sub] if pair else dks[sub] * scale
            dv_acc[pl.ds(kss[sub], tk), :] += dvs[sub]

        @pl.when(step == nq - 1)
        def _():
            dkv = dk_acc[...]
            if mode == "dil":
                dkv = _rope_t(dkv, tk_ref[0], tk_ref[1], tk_ref[2])
            dk_ref[...] = dkv.astype(kv_dtype)
            dv_ref[...] = dv_acc[...].astype(kv_dtype)

    q_spec = pl.BlockSpec((rows, 128), lambda u, i: (i, qcol + u))
    row_spec = pl.BlockSpec((rows, 128), lambda u, i: (i, u))
    kv_out = pl.BlockSpec((lk, 128), lambda u, i: (0, u))
    whole = pl.BlockSpec((SEQ, 128), lambda u, i: (0, u))
    nat_spec = whole if mode == "dil" else row_spec
    in_specs = [q_spec,
                pl.BlockSpec((lk, 128), lambda u, i: (0, kcol + u)),
                pl.BlockSpec((lk, 128), lambda u, i: (0, vcol + u)),
                nat_spec, nat_spec]
    args = [q_arr, k_arr, v_arr, do, lse]
    out_specs = [row_spec, kv_out, kv_out]
    out_shape = [jax.ShapeDtypeStruct((SEQ, 512), BF16), jax.ShapeDtypeStruct((lk, 512), kv_dtype),
                 jax.ShapeDtypeStruct((lk, 512), kv_dtype)]
    if mode == "dil":
        in_specs += [whole, pl.BlockSpec((3, rows, 128), lambda u, i: (0, i, 0)),
                     pl.BlockSpec((3, SEQ, 128), lambda u, i: (0, 0, 0))]
        args += [dp, tabs, tabs]
    elif mode == "na":
        b_spec = pl.BlockSpec((2, NA_ROWS, GRID_W, NA_ROWS * GRID_W), lambda u, i: (u, 0, 0, 0))
        in_specs += [row_spec, b_spec]
        args += [o, bias]
        out_specs.append(b_spec)
        out_shape.append(jax.ShapeDtypeStruct((8, NA_ROWS, GRID_W, NA_ROWS * GRID_W), F32))
    else:
        in_specs.append(row_spec)
        args.append(o)
    return pl.pallas_call(
        body, name=name, grid=(cfg["units"], nq), in_specs=in_specs, out_specs=out_specs, out_shape=out_shape,
        scratch_shapes=[pltpu.VMEM((lk, 128), F32), pltpu.VMEM((lk, 128), F32)],
        compiler_params=_params(("parallel", "arbitrary")))(*args)


def _na_geometry():
    qc = _iota((GRID_W, 128), 0)
    lane = _iota((GRID_W, 128), 1)
    kc = lane & 63
    c_start = jnp.clip(qc - 8, 0, GRID_W - 16)
    valid = jnp.logical_and(kc >= c_start, kc < c_start + 16)
    return lane, valid


def _na_bias(rpb_rows, dep=None):
    dep_specs, dep_args = _dep_operand(dep)

    def body(r_ref, *rest):
        o_ref, t_ref = rest[-2:]
        lane, valid = _na_geometry()
        for dd in range(14):
            row_a = jnp.broadcast_to(r_ref[dd:dd + 1, :], (GRID_W, 128))
            row_b = jnp.broadcast_to(r_ref[dd + 1:dd + 2, :], (GRID_W, 128))
            both = jnp.where(lane < 64, row_a, pltpu.roll(row_b, 64, 1))
            t = pltpu.roll(both, 128 - 15, 1, stride=1, stride_axis=0)
            t_ref[dd] = jnp.where(valid, t, NEG)
        for off in range(NA_ROWS):
            for p in range(4):
                o_ref[off, :, p * 128:(p + 1) * 128] = t_ref[2 * p - off + 7]

    return pl.pallas_call(
        body, name="na_bias", grid=(8,),
        in_specs=[pl.BlockSpec((None, 16, 128), lambda h: (h, 0, 0))] + dep_specs,
        out_specs=pl.BlockSpec((None, NA_ROWS, GRID_W, NA_ROWS * GRID_W), lambda h: (h, 0, 0, 0)),
        out_shape=jax.ShapeDtypeStruct((8, NA_ROWS, GRID_W, NA_ROWS * GRID_W), F32),
        scratch_shapes=[pltpu.VMEM((14, GRID_W, 128), F32)],
        compiler_params=_params(("parallel",)))(rpb_rows, *dep_args)


def _na_bias_bwd(dbias, dep=None):
    dep_specs, dep_args = _dep_operand(dep)

    def body(d_ref, *rest):
        o_ref = rest[-1]
        lane, valid = _na_geometry()
        reverse = (_iota((GRID_W, GRID_W), 0) + _iota((GRID_W, GRID_W), 1) == GRID_W - 1).astype(F32)
        o_ref[...] = jnp.zeros((16, 128), F32)
        for dd in range(14):
            t = jnp.zeros((GRID_W, 128), F32)
            for off in range(NA_ROWS):
                for p in range(4):
                    if 2 * p - off + 7 == dd:
                        t = t + d_ref[off, :, p * 128:(p + 1) * 128]
            t = jnp.dot(reverse, jnp.where(valid, t, 0.0), precision=lax.Precision.HIGHEST,
                        preferred_element_type=F32)
            t = pltpu.roll(t, 128 - (GRID_W - 16), 1, stride=1, stride_axis=0)
            o_ref[dd:dd + 1, :] = jnp.sum(t, axis=0, keepdims=True)

    return pl.pallas_call(
        body, name="na_bias_bwd", grid=(8,),
        in_specs=[pl.BlockSpec((None, NA_ROWS, GRID_W, NA_ROWS * GRID_W), lambda h: (h, 0, 0, 0))] + dep_specs,
        out_specs=pl.BlockSpec((None, 16, 128), lambda h: (h, 0, 0)),
        out_shape=jax.ShapeDtypeStruct((8, 16, 128), F32),
        compiler_params=_params(("parallel",)))(dbias, *dep_args)


GATE_ROWS = 128


def _group_weights(l0, l1, l2):
    m = jnp.maximum(jnp.maximum(l0, l1), l2)
    e0, e1, e2 = jnp.exp(l0 - m), jnp.exp(l1 - m), jnp.exp(l2 - m)
    inv = 1.0 / (e0 + e1 + e2)
    return e0 * inv, e1 * inv, e2 * inv


def _gate_block(o_grp, l_grp, out_b, out_c, parts, x, target, merge_bias, wts, w_out, gain, head_sum):
    rows = GATE_ROWS
    r512 = pl.BlockSpec((rows, 512), lambda i: (i, 0))
    r1024 = pl.BlockSpec((rows, D_MODEL), lambda i: (i, 0))
    silu_cols = [pl.BlockSpec((rows, 512), functools.partial(lambda b, i: (i, b), 13 + b)) for b in range(3)]
    logit_cols = [pl.BlockSpec((rows, D_MODEL), functools.partial(lambda b, i: (i, b), 8 + b)) for b in range(3)]

    def body(o0, o1, o2, l0, l1, l2, ob, oc, ga, gb, gc, la, lb, lc, x_ref, t_ref, mb, wa, wb, wc, wo_ref, gn_ref,
             hs_ref, dout_ref, dla, dlb, dlc, dga, dgb, dgc, do0, do1, do2, dp0, dp1, dp2, dob, doc, err_ref, gg_ref,
             gmb, gwa, gwb, gwc, gwo, acc_a, acc_b, acc_c, acc_o):
        step = pl.program_id(0)
        ws = _group_weights(l0[...], l1[...], l2[...])
        out_a = ws[0] * o0[...] + ws[1] * o1[...] + ws[2] * o2[...]
        branches = ((out_a, ga, la, wa, acc_a, dla, dga), (ob[...], gb, lb, wb, acc_b, dlb, dgb),
                    (oc[...], gc, lc, wc, acc_c, dlc, dgc))

        @pl.when(step == 0)
        def _():
            for acc in (acc_a, acc_b, acc_c, acc_o):
                acc[...] = jnp.zeros(acc.shape, F32)
            err_ref[...] = jnp.zeros((1, D_MODEL), F32)
            gg_ref[...] = jnp.zeros((1, D_MODEL), F32)
            gmb[...] = jnp.zeros((3, D_MODEL), F32)

        y = jnp.zeros((rows, D_MODEL), F32)
        zs, gates, silus, dsilus, us = [], [], [], [], []
        for b, (ov, g_ref, l_ref, w_ref, _, _, _) in enumerate(branches):
            g = g_ref[...].astype(F32)
            sg = _sigmoid(g)
            silus.append(g * sg)
            dsilus.append(sg * (1.0 + g * (1.0 - sg)))
            us.append((ov * silus[b]).astype(BF16))
            zs.append(lax.dot_general(us[b], w_ref[...], NT, preferred_element_type=F32))
            gates.append(_sigmoid(l_ref[...].astype(F32) + mb[b:b + 1, :]))
            y = y + gates[b] * zs[b]
        yb = y.astype(BF16)
        y2 = jnp.dot(yb, wo_ref[...], preferred_element_type=F32)
        rstd = lax.rsqrt(jnp.mean(y2 * y2, axis=1, keepdims=True) + EPS)
        yn = y2 * rstd
        gv = gn_ref[...]
        err = x_ref[...] + yn * gv - t_ref[...]
        dout = err * (1.0 / D_MODEL)
        dout_ref[...] = dout
        dn = dout * gv
        dy2 = (rstd * (dn - yn * jnp.mean(dn * yn, axis=1, keepdims=True))).astype(BF16)
        acc_o[...] += lax.dot_general(yb, dy2, TN, preferred_element_type=F32)
        err_ref[...] += jnp.sum(err * err, axis=0, keepdims=True)
        gg_ref[...] += jnp.sum(dout * yn, axis=0, keepdims=True)
        dy = lax.dot_general(dy2, wo_ref[...], NT, preferred_element_type=F32)
        dos = []
        for b, (ov, _, _, w_ref, acc, dl_ref, dg_ref) in enumerate(branches):
            dl = dy * zs[b] * gates[b] * (1.0 - gates[b])
            dl_ref[...] = dl.astype(BF16)
            gmb[b:b + 1, :] += jnp.sum(dl, axis=0, keepdims=True)
            dz = (dy * gates[b]).astype(BF16)
            acc[...] += lax.dot_general(dz, us[b], TN, preferred_element_type=F32)
            du = jnp.dot(dz, w_ref[...], preferred_element_type=F32)
            dos.append(du * silus[b])
            dg_ref[...] = (du * ov * dsilus[b]).astype(BF16)
        dob[...] = dos[1].astype(BF16)
        doc[...] = dos[2].astype(BF16)
        row_term = jnp.dot(dos[0] * out_a, hs_ref[...], precision=lax.Precision.HIGHEST, preferred_element_type=F32)
        for wg, do_ref, dp_ref in zip(ws, (do0, do1, do2), (dp0, dp1, dp2)):
            do_ref[...] = wg * dos[0]
            dp_ref[...] = wg * row_term

        @pl.when(step == SEQ // rows - 1)
        def _():
            for acc, out in ((acc_a, gwa), (acc_b, gwb), (acc_c, gwc), (acc_o, gwo)):
                out[...] = acc[...].astype(BF16)

    full = lambda shape: pl.BlockSpec(shape, lambda i: (0,) * len(shape))
    vec = pl.BlockSpec((1, D_MODEL), lambda i: (0, 0))
    acc3 = pl.BlockSpec((3, D_MODEL), lambda i: (0, 0))
    in_specs = ([r512] * 8 + silu_cols + logit_cols + [r1024, r1024, full((3, D_MODEL))]
                + [full((D_MODEL, 512))] * 3 + [full((D_MODEL, D_MODEL)), vec, full((512, 512))])
    out_specs = ([r1024] + [r1024] * 3 + [r512] * 3 + [r512] * 6 + [r512] * 2 + [vec, vec, acc3]
                 + [full((D_MODEL, 512))] * 3 + [full((D_MODEL, D_MODEL))])
    bf, f32 = BF16, F32
    sds = jax.ShapeDtypeStruct
    out_shape = ([sds((SEQ, D_MODEL), f32)] + [sds((SEQ, D_MODEL), bf)] * 3 + [sds((SEQ, 512), bf)] * 3
                 + [sds((SEQ, 512), f32)] * 6 + [sds((SEQ, 512), bf)] * 2 + [sds((1, D_MODEL), f32)] * 2
                 + [sds((3, D_MODEL), f32)] + [sds((D_MODEL, 512), bf)] * 3 + [sds((D_MODEL, D_MODEL), bf)])
    res = pl.pallas_call(
        body, name="gate_block", grid=(SEQ // rows,), in_specs=in_specs, out_specs=out_specs, out_shape=out_shape,
        scratch_shapes=[pltpu.VMEM((D_MODEL, 512), F32)] * 3 + [pltpu.VMEM((D_MODEL, D_MODEL), F32)],
        compiler_params=_params(("arbitrary",)))(
            *o_grp, *l_grp, out_b, out_c, parts, parts, parts, parts, parts, parts, x, target, merge_bias, *wts, w_out,
            gain, head_sum)
    return dict(dout=res[0], dlog=res[1:4], dg=res[4:7], do_grp=res[7:10], dp_grp=res[10:13], do_b=res[13],
                do_c=res[14], err_sq=res[15], g_post=res[16], g_mb=res[17], g_wt=res[18:21], g_w_out=res[21])


def _local_step(x, hst, parts, tabs, bias, mem, target, pre_norm, mem_norm, post_norm, wt_in, late_weights,
                reduce_start=None):
    o_grp, l_grp = [], []
    for g, d in enumerate(DILATIONS):
        o, l = _attn_fwd("dil_fwd_%d" % g, "dil", parts, parts, parts, 12 * g, 12 * g + 4, 12 * g + 8, d=d)
        o_grp.append(o)
        l_grp.append(l)
    out_b, lse_b = _attn_fwd("na_fwd", "na", parts, parts, parts, 36, 40, 44, bias=bias)
    merge_bias, w_kv, wt_a, wt_b, wt_c, w_out = late_weights(sum(a[:8, :128] for a in [out_b] + o_grp))
    memn = _rmsnorm_fwd("memnorm", mem, mem_norm, MEM_LEN)
    kv_m = _mm_simple("mem_kv", memn, w_kv, NN, BF16, MEM_LEN, 512, D_MODEL)
    out_c, lse_c = _attn_fwd("mem_fwd", "mem", parts, kv_m, kv_m, 48, 0, 4)

    rr = _iota((512, 512), 0) // HEAD_DIM
    cc = _iota((512, 512), 1) // HEAD_DIM
    head_sum = (rr == cc).astype(F32)
    gb = _gate_block(o_grp, l_grp, out_b, out_c, parts, x, target, merge_bias, (wt_a, wt_b, wt_c), w_out, post_norm,
                     head_sum)
    dout, dlog, dg, g_wt, g_w_out = gb["dout"], gb["dlog"], gb["dg"], gb["g_wt"], gb["g_w_out"]
    do_grp, dp_grp, do_b, do_c, g_post, g_mb = (gb["do_grp"], gb["dp_grp"], gb["do_b"], gb["do_c"], gb["g_post"],
                                                gb["g_mb"])
    loss = 0.5 * jnp.sum(gb["err_sq"]) / D_MODEL

    dqkv = []
    for g, d in enumerate(DILATIONS):
        dq, dk, dv = _attn_bwd("dil_bwd_%d" % g, "dil", parts, parts, parts, 12 * g, 12 * g + 4, 12 * g + 8,
                               do_grp[g], l_grp[g], dp=dp_grp[g], d=d, tabs=tabs[g])
        dqkv += [dq, dk, dv]
    dq_b, dk_b, dv_b, dbias = _attn_bwd("na_bwd", "na", parts, parts, parts, 36, 40, 44, do_b, lse_b, o=out_b,
                                        bias=bias)
    dq_c, dk_m, dv_m = _attn_bwd("mem_bwd", "mem", parts, kv_m, kv_m, 48, 0, 4, do_c, lse_c, o=out_c)

    dkv = jnp.concatenate([dk_m, dv_m], axis=1).astype(BF16)
    g_w_kv = _mm_simple("mem_kv_dw", memn, dkv, TN, BF16, D_MODEL, 512, MEM_LEN)
    dmemn = _mm_simple("mem_kv_dx", dkv, w_kv, NT, F32, MEM_LEN, 512, D_MODEL)

    grads = dict(w_kv=g_w_kv, wt_a=g_wt[0], wt_b=g_wt[1], wt_c=g_wt[2], w_out=g_w_out, merge_bias=g_mb,
                 post_norm=g_post)
    dep = None
    if reduce_start is not None:
        reduce_start("rest_sibling", grads)
        dep = reduce_start("rest_chips", grads, sum(a[:8, :128] for a in (dqkv[0], dqkv[3], dqkv[6], dq_b, dq_c)))
    dparts = dqkv + [dq_b, dk_b, dv_b, dq_c] + list(dg) + list(dlog)
    grads["wt_in"] = _in_proj_dw(dparts, hst, dep)
    dep = reduce_start("w_in", grads) if reduce_start is not None else None
    dh = _in_proj_dh(dparts, wt_in, dep)
    grad_x, grads["pre_norm"] = _prenorm_bwd(x, pre_norm, dh, dout)
    g_rpb_t = _na_bias_bwd(dbias, dep)
    grads["na_rpb"] = g_rpb_t[:, :15, :31] + jnp.pad(g_rpb_t[:, :14, 64:95], ((0, 0), (1, 0), (0, 0)))
    grads["mem_norm"] = _memnorm_bwd(mem, dmemn, dep)
    return loss, grad_x, grads


ANY = pl.BlockSpec(memory_space=pl.ANY)


def _place():
    return lax.axis_index("x"), lax.axis_index("y"), lax.axis_index("c")


HBM = pl.BlockSpec(memory_space=pltpu.HBM)
SEM = pl.BlockSpec(memory_space=pltpu.SEMAPHORE)
DATAFLOW = pltpu.SideEffectType.DATAFLOW_SIDE_EFFECTING


def _split_copies(kind, srcs, lands, send_sems, recv_sems):
    nt = len(srcs)
    x, y, c = _place()
    copies = []
    if kind == "sibling":
        for q in range(4):
            for t in range(nt):
                k = q * nt + t
                copies.append(pltpu.make_async_remote_copy(
                    src_ref=srcs[t].at[2 * q + 1 - c], dst_ref=lands[t].at[q], send_sem=send_sems.at[k],
                    recv_sem=recv_sems.at[k], device_id=(x, y, 1 - c), device_id_type=MESH_ID))
    elif kind == "gather":
        me = 4 * x + 2 * y + c
        for mask in range(1, 8):
            fx, fy, fc = (mask >> 2) & 1, (mask >> 1) & 1, mask & 1
            to = (1 - x if fx else x, 1 - y if fy else y, 1 - c if fc else c)
            for t in range(nt):
                k = (mask - 1) * nt + t
                copies.append(pltpu.make_async_remote_copy(
                    src_ref=srcs[t], dst_ref=lands[t].at[me], send_sem=send_sems.at[k], recv_sem=recv_sems.at[k],
                    device_id=to, device_id_type=MESH_ID))
    else:
        for s, (tx, ty) in enumerate([(1 - x, y), (x, 1 - y), (1 - x, 1 - y)]):
            for t in range(nt):
                k = s * nt + t
                copies.append(pltpu.make_async_remote_copy(
                    src_ref=srcs[t].at[2 * tx + ty], dst_ref=lands[t].at[s], send_sem=send_sems.at[k],
                    recv_sem=recv_sems.at[k], device_id=(tx, ty, c), device_id_type=MESH_ID))
    return copies


def _split_count(kind, nt):
    return {"gather": 7, "chips": 3, "sibling": 4}[kind] * nt


def _exchange_start(name, kind, srcs, land_shapes, after=None):
    nt = len(srcs)
    n = _split_count(kind, nt)
    dep_specs, dep_args = _dep_operand(after)
    nd = len(dep_args)

    def body(*refs):
        src_refs, land_refs = refs[:nt], refs[nt:2 * nt]
        send_sems, recv_sems = refs[2 * nt + nd], refs[2 * nt + nd + 1]
        token = refs[-1]
        for cp in _split_copies(kind, src_refs, land_refs, send_sems, recv_sems):
            cp.start()
        token[...] = jnp.zeros_like(token)

    lands = [pltpu.with_memory_space_constraint(lax.empty(s.shape, s.dtype), pltpu.HBM) for s in land_shapes]
    res = pl.pallas_call(
        body, name=name,
        out_shape=(pltpu.SemaphoreType.DMA((n,)), pltpu.SemaphoreType.DMA((n,)),
                   *[pltpu.HBM(s.shape, s.dtype) for s in srcs], *[pltpu.HBM(s.shape, s.dtype) for s in land_shapes],
                   jax.ShapeDtypeStruct((8, 128), F32)),
        in_specs=[HBM] * (2 * nt) + dep_specs,
        out_specs=(SEM, SEM, *([HBM] * (2 * nt)), pl.BlockSpec(memory_space=pltpu.VMEM)),
        input_output_aliases={i: 2 + i for i in range(2 * nt)},
        compiler_params=pltpu.CompilerParams(has_side_effects=DATAFLOW))(
            *[pltpu.with_memory_space_constraint(s, pltpu.HBM) for s in srcs], *lands, *dep_args)
    return res[0], res[1], list(res[2:2 + nt]), list(res[2 + nt:2 + 2 * nt]), res[-1]


def _exchange_wait(name, kind, send_sems, recv_sems, srcs, lands, after):
    nt = len(srcs)

    def body(*refs):
        src_refs, land_refs = refs[:nt], refs[nt:2 * nt]
        s_sems, r_sems = refs[2 * nt], refs[2 * nt + 1]
        for cp in _split_copies(kind, src_refs, land_refs, s_sems, r_sems):
            cp.wait_send()
            cp.wait_recv()

    res = pl.pallas_call(
        body, name=name,
        out_shape=tuple(pltpu.HBM(s.shape, s.dtype) for s in list(srcs) + list(lands)),
        in_specs=[HBM] * (2 * nt) + [SEM, SEM, pl.BlockSpec(memory_space=pl.ANY)],
        out_specs=tuple([HBM] * (2 * nt)),
        input_output_aliases={i: i for i in range(2 * nt)},
        compiler_params=pltpu.CompilerParams(has_side_effects=DATAFLOW))(
            *srcs, *lands, send_sems, recv_sems, after)
    return list(res[:nt]), list(res[nt:])


AG_GROUPS = ((0, 3), (3, 4), (7, 2))


def _ag_phase(name, own, land, sems, waits, starts, after=None):
    r = own.shape[0]
    half = r // 2
    ns = len(sems)
    dep_specs, dep_args = _dep_operand(after)
    nd = len(dep_args)
    new_group = None
    if starts:
        (new_group,) = [g for g, (first, n) in enumerate(AG_GROUPS) if first == starts[0]]
        assert list(starts) == list(range(AG_GROUPS[new_group][0], sum(AG_GROUPS[new_group])))

    def body(*refs):
        own_ref, land_ref = refs[0], refs[1]
        sem_refs = list(refs[2:2 + 2 * ns])
        outs = refs[2 + 2 * ns + nd:]
        if starts:
            sem_refs += [outs[0], outs[1]]
        x, y, c = _place()
        me, sib = (x, y, c), (x, y, 1 - c)
        xn, yn, dg = (1 - x, y, c), (x, 1 - y, c), (1 - x, 1 - y, c)

        def other(dev):
            return (dev[0], dev[1], 1 - dev[2])

        def rows(dev, part):
            blk = land_ref.at[4 * dev[0] + 2 * dev[1] + dev[2]]
            return blk if part is None else blk.at[pl.ds(part * half, half)]

        def sem_of(k):
            (g,) = [g for g, (first, n) in enumerate(AG_GROUPS) if first <= k < first + n]
            return sem_refs[2 * g].at[k - AG_GROUPS[g][0]], sem_refs[2 * g + 1].at[k - AG_GROUPS[g][0]]

        sent = {0: (me, None, sib), 1: (me, None, xn), 2: (me, None, yn), 3: (xn, 0, yn), 4: (yn, 1, xn),
                5: (xn, None, sib), 6: (yn, None, sib), 7: (dg, 0, sib), 8: (dg, 1, sib)}
        landed = {0: (sib, None), 1: (xn, None), 2: (yn, None), 3: (dg, 0), 4: (dg, 1), 5: (other(xn), None),
                  6: (other(yn), None), 7: (other(dg), 0), 8: (other(dg), 1)}

        def copy(k, receiving):
            send_sem, recv_sem = sem_of(k)
            dev, part, to = (*landed[k], me) if receiving else sent[k]
            src = own_ref if (dev is me and not receiving) else rows(dev, part)
            return pltpu.make_async_remote_copy(src_ref=src, dst_ref=rows(dev, part), send_sem=send_sem,
                                                recv_sem=recv_sem, device_id=to, device_id_type=MESH_ID)

        for kind, k in waits:
            if kind == "recv":
                copy(k, True).wait_recv()
            else:
                copy(k, False).wait_send()
        for k in starts:
            copy(k, False).start()
        if starts:
            outs[-1][...] = jnp.zeros_like(outs[-1])

    n_new = AG_GROUPS[new_group][1] if starts else 0
    sem_out = (pltpu.SemaphoreType.DMA((n_new,)), pltpu.SemaphoreType.DMA((n_new,))) if starts else ()
    token_out = (jax.ShapeDtypeStruct((8, 128), F32),) if starts else ()
    res = pl.pallas_call(
        body, name=name,
        out_shape=(*sem_out, pltpu.HBM(own.shape, own.dtype), pltpu.HBM(land.shape, land.dtype), *token_out),
        in_specs=[HBM, HBM] + [SEM] * (2 * ns) + dep_specs,
        out_specs=(*([SEM] * len(sem_out)), HBM, HBM, *([pl.BlockSpec(memory_space=pltpu.VMEM)] * len(token_out))),
        input_output_aliases={0: len(sem_out), 1: len(sem_out) + 1},
        compiler_params=pltpu.CompilerParams(has_side_effects=DATAFLOW))(
            own, land, *[a for pair in sems for a in pair], *dep_args)
    if starts:
        return (res[0], res[1]), res[2], res[3], res[4]
    return None, res[0], res[1], None


def _add_sibling(name, term, recv, rows):
    _, r, w = term.shape
    cidx = lax.axis_index("c").astype(jnp.int32).reshape(1)
    like_term = recv.shape[0] == N_DEV

    def body(c_ref, a_ref, b_ref, o_ref):
        o_ref[...] = (a_ref[...].astype(F32) + b_ref[...].astype(F32)).astype(o_ref.dtype)

    grid_spec = pltpu.PrefetchScalarGridSpec(
        num_scalar_prefetch=1, grid=(4, r // rows),
        in_specs=[pl.BlockSpec((None, rows, w), lambda q, i, c_ref: (2 * q + c_ref[0], i, 0)),
                  pl.BlockSpec((None, rows, w), lambda q, i, c_ref: (2 * q + c_ref[0] if like_term else q, i, 0))],
        out_specs=pl.BlockSpec((None, rows, w), lambda q, i, c_ref: (q, i, 0)))
    return pl.pallas_call(
        body, name=name, grid_spec=grid_spec, out_shape=jax.ShapeDtypeStruct((4, r, w), term.dtype),
        compiler_params=_params(("parallel", "parallel")))(cidx, term, recv)


def _add_sibling_small(name, terms, recvs):
    nt = len(terms)

    def body(*refs):
        c = lax.axis_index("c")
        for t_ref, r_ref, o_ref in zip(refs[:nt], refs[nt:2 * nt], refs[2 * nt:]):
            for q in range(4):
                o_ref[q] = (t_ref[2 * q + c].astype(F32) + r_ref[q].astype(F32)).astype(o_ref.dtype)

    return pl.pallas_call(
        body, name=name, out_shape=[jax.ShapeDtypeStruct((4,) + t.shape[1:], t.dtype) for t in terms],
        compiler_params=_params())(*terms, *recvs)


def _add_chips(name, sums, recv, rows):
    _, r, w = sums.shape
    qidx = (2 * lax.axis_index("x") + lax.axis_index("y")).astype(jnp.int32).reshape(1)

    def body(q_ref, a_ref, b_ref, o_ref):
        o_ref[...] = ((a_ref[...].astype(F32) + b_ref[0].astype(F32))
                      + (b_ref[1].astype(F32) + b_ref[2].astype(F32)))

    grid_spec = pltpu.PrefetchScalarGridSpec(
        num_scalar_prefetch=1, grid=(r // rows,),
        in_specs=[pl.BlockSpec((None, rows, w), lambda i, q_ref: (q_ref[0], i, 0)),
                  pl.BlockSpec((3, rows, w), lambda i, q_ref: (0, i, 0))],
        out_specs=pl.BlockSpec((rows, w), lambda i, q_ref: (i, 0)))
    return pl.pallas_call(
        body, name=name, grid_spec=grid_spec, out_shape=jax.ShapeDtypeStruct((r, w), F32),
        compiler_params=_params(("parallel",)))(qidx, sums, recv)


def _rs_rows(a):
    return SHARD_IN // 4 if a.shape[1] == SHARD_IN else a.shape[1]


def _reduce_scatter_start(tag, names, terms, recv1):
    if len(terms) == 1:
        sums = [_add_sibling("add_sibling_" + names[0], terms[0], recv1[0], _rs_rows(terms[0]))]
    else:
        sums = _add_sibling_small("add_sibling_" + tag, terms, recv1)
    lands =[jax.ShapeDtypeStruct((3,) + s.shape[1:], s.dtype) for s in sums]
    send_sems, recv_sems, sums, lands, token = _exchange_start("exchange_chips_start_" + tag, "chips", sums, lands)
    return (tag, names, send_sems, recv_sems, sums, lands), token


def _reduce_scatter_wait(state, after):
    tag, names, send_sems, recv_sems, sums, lands = state
    sums, recv2 = _exchange_wait("exchange_chips_wait_" + tag, "chips", send_sems, recv_sems, sums, lands, after)
    return names, sums, recv2


def _adamw(name, w, g, m, v, dep=None):
    dep_specs, dep_args = _dep_operand(dep)

    def body(w_ref, g_ref, m_ref, v_ref, *rest):
        d_ref, nm_ref, nv_ref = rest[-3:]
        d_ref[...], nm_ref[...], nv_ref[...] = _adam_math(w_ref[...], g_ref[...], m_ref[...], v_ref[...])

    whole = pl.BlockSpec(memory_space=pltpu.VMEM)
    return pl.pallas_call(
        body, name=name, in_specs=[whole] * 4 + dep_specs, out_shape=[jax.ShapeDtypeStruct(w.shape, F32)] * 3,
        compiler_params=_params())(w, g, m, v, *dep_args)


def _adam_math(w, g, m, v):
    nm = ADAM_B1 * m + (1.0 - ADAM_B1) * g
    nv = ADAM_B2 * v + (1.0 - ADAM_B2) * (g * g)
    c1 = 1.0 - ADAM_B1 ** ADAM_STEP
    c2 = 1.0 - ADAM_B2 ** ADAM_STEP
    return -ADAM_LR * ((nm / c1) / (jnp.sqrt(nv / c2) + ADAM_EPS) + ADAM_WD * w), nm, nv


def _adamw_chips(name, sums, recv, w, m, v, transposed, rows=None, dep=None):
    r, c = w.shape
    rows = r if rows is None else rows
    qidx = (2 * lax.axis_index("x") + lax.axis_index("y")).astype(jnp.int32).reshape(1)
    dep_specs, dep_args = _dep_operand(dep)

    def body(q_ref, a_ref, b_ref, w_ref, m_ref, v_ref, *rest):
        g_ref, d_ref, nm_ref, nv_ref = rest[-4:]
        g = (a_ref[...].astype(F32) + b_ref[0].astype(F32)) + (b_ref[1].astype(F32) + b_ref[2].astype(F32))
        if transposed:
            g = g.T
        g_ref[...] = g
        d_ref[...], nm_ref[...], nv_ref[...] = _adam_math(w_ref[...], g, m_ref[...], v_ref[...])

    row = pl.BlockSpec((rows, c), lambda i, q_ref: (i, 0))
    if transposed:
        term_specs = [pl.BlockSpec((None, c, rows), lambda i, q_ref: (q_ref[0], 0, i)),
                      pl.BlockSpec((3, c, rows), lambda i, q_ref: (0, 0, i))]
    else:
        term_specs = [pl.BlockSpec((None, rows, c), lambda i, q_ref: (q_ref[0], i, 0)),
                      pl.BlockSpec((3, rows, c), lambda i, q_ref: (0, i, 0))]
    grid_spec = pltpu.PrefetchScalarGridSpec(
        num_scalar_prefetch=1, grid=(r // rows,), in_specs=term_specs + [row, row, row] + dep_specs,
        out_specs=[row] * 4)
    return pl.pallas_call(
        body, name=name, grid_spec=grid_spec, out_shape=[jax.ShapeDtypeStruct((r, c), F32)] * 4,
        compiler_params=_params(("parallel",)))(qidx, sums, recv, w, m, v, *dep_args)


def _sum_devices(gathered):
    def body(g_ref, o_ref):
        acc = g_ref[0]
        for j in range(1, N_DEV):
            acc = acc + g_ref[j]
        o_ref[...] = acc

    return pl.pallas_call(
        body, name="sum_devices", out_shape=jax.ShapeDtypeStruct(gathered.shape[1:], F32),
        compiler_params=_params())(gathered)


def _rows128(a, rows):
    flat = a.reshape(-1)
    return jnp.pad(flat, (0, rows * 128 - flat.shape[0])).reshape(rows, 128)


def kernel(x, mem, pre_norm, w_in, merge_bias, na_rpb, mem_norm, w_mem_kv, w_branch_a, w_branch_b, w_branch_c, w_out, post_norm, loss_target, m_pre_norm, m_w_in, m_merge_bias, m_na_rpb, m_mem_norm, m_w_mem_kv, m_w_branch_a, m_w_branch_b, m_w_branch_c, m_w_out, m_post_norm, v_pre_norm, v_w_in, v_merge_bias, v_na_rpb, v_mem_norm, v_w_mem_kv, v_w_branch_a, v_w_branch_b, v_w_branch_c, v_w_out, v_post_norm):
    wt_in_s = w_in[0].T.astype(BF16)
    rows_s = jnp.concatenate([w_mem_kv[0], w_out[0]], axis=0).astype(BF16)
    cols_s = jnp.concatenate([w_branch_a[0].T, w_branch_b[0].T, w_branch_c[0].T], axis=0).astype(BF16)
    mb_s = jnp.pad(merge_bias[0], ((0, 5), (0, 0)))
    me = 4 * lax.axis_index("x") + 2 * lax.axis_index("y") + lax.axis_index("c")

    chip = 2 * lax.axis_index("x") + lax.axis_index("y")

    def first_block(q):
        return jnp.where(q == 0, 0, jnp.where(q == 1, 6, jnp.where(q == 2, 11, 17)))

    five = jnp.arange(5, dtype=jnp.int32)
    near, far = jnp.where(chip < 2, 5, 16), jnp.where(chip < 2, 16, 5)
    order1 = (first_block(chip) + five).astype(jnp.int32)
    order2 = jnp.concatenate([first_block(chip ^ 1) + five, near[None], first_block(chip ^ 2) + five]).astype(jnp.int32)
    order3 = jnp.concatenate([first_block(chip ^ 3) + five, far[None]]).astype(jnp.int32)
    tabs = _rope_tables()

    def weights_of(land):
        return land.reshape(N_IN, D_MODEL)

    land = pltpu.with_memory_space_constraint(lax.empty((N_DEV,) + wt_in_s.shape, BF16), pltpu.HBM)
    own = pltpu.with_memory_space_constraint(wt_in_s, pltpu.HBM)
    sem_a, own, land, token = _ag_phase("ag_start", own, land, [], [], [0, 1, 2])
    hs, hst = _prenorm_fold(x[0], pre_norm, token)
    _, own, land, _ = _ag_phase("ag_wait0", own, land, [sem_a], [("recv", 0)], [], hs)
    land = lax.dynamic_update_slice(land, own[None], (me, 0, 0))
    parts = _in_proj("in_proj_1", hs, weights_of(land), tabs, order1)
    bias = _na_bias(jnp.pad(na_rpb[0], ((0, 0), (0, 1), (0, 128 - 31))), parts)
    sem_b, own, land, _ = _ag_phase("ag_mid1", own, land, [sem_a], [("recv", 1), ("recv", 2)], [3, 4, 5, 6], bias)
    _, own, land, _ = _ag_phase("ag_wait1", own, land, [sem_a, sem_b], [("recv", 5), ("recv", 6)], [])
    parts = _in_proj("in_proj_2", hs, weights_of(land), tabs, order2, parts)
    sem_c, own, land, _ = _ag_phase("ag_mid2", own, land, [sem_a, sem_b], [("recv", 3), ("recv", 4)], [7, 8], parts)
    _, own, land, _ = _ag_phase("ag_end", own, land, [sem_a, sem_b, sem_c],
                                [("recv", 7), ("recv", 8)] + [("send", k) for k in range(9)], [])
    wt_in = weights_of(land)

    late_own = [rows_s, cols_s, mb_s]
    late_lands = [jax.ShapeDtypeStruct((N_DEV,) + s.shape, s.dtype) for s in late_own]
    l_send, l_recv, late_own, late_lands, late_token = _exchange_start("gather_late_start", "gather", late_own,
                                                                       late_lands, after=wt_in)
    parts = _in_proj("in_proj_3", hs, wt_in, tabs, order3, parts, late_token)

    def late_weights(after):
        own, lands = _exchange_wait("gather_late_wait", "gather", l_send, l_recv, late_own, late_lands, after)
        g_rows, g_cols, g_mb = [lax.dynamic_update_slice(land, o[None], (me, 0, 0)) for land, o in zip(lands, own)]
        return (g_mb[:, :3].transpose(1, 0, 2).reshape(3, D_MODEL),
                g_rows[:, :128].reshape(D_MODEL, D_MODEL), g_cols[:, 0:128].reshape(D_MODEL, 512),
                g_cols[:, 128:256].reshape(D_MODEL, 512), g_cols[:, 256:384].reshape(D_MODEL, 512),
                g_rows[:, 128:].reshape(D_MODEL, D_MODEL))

    rs_state = []
    rest_names = ["w_kv", "w_out", "a", "b", "c", "mb"]
    rest_sibling = []

    def reduce_start(phase, grads, after=None):
        if phase == "rest_sibling":
            gmb_t = jnp.pad(grads["merge_bias"].reshape(3, N_DEV, 128).transpose(1, 0, 2), ((0, 0), (0, 5), (0, 0)))
            terms = [grads["w_kv"].reshape(N_DEV, 128, D_MODEL), grads["w_out"].reshape(N_DEV, 128, D_MODEL),
                     grads["wt_a"].reshape(N_DEV, 128, 512), grads["wt_b"].reshape(N_DEV, 128, 512),
                     grads["wt_c"].reshape(N_DEV, 128, 512), gmb_t]
            lands = [jax.ShapeDtypeStruct((4,) + t.shape[1:], t.dtype) for t in terms]
            rest_sibling.extend(_exchange_start("exchange_sibling_start_rest", "sibling", terms, lands)[:4])
            return None
        if phase == "rest_chips":
            s_send, s_recv, terms, lands = rest_sibling
            terms, recv1 = _exchange_wait("exchange_sibling_wait_rest", "sibling", s_send, s_recv, terms, lands, after)
            state, token = _reduce_scatter_start("rest", rest_names, terms, recv1)
        else:
            own, sibling = [a.reshape(N_DEV, SHARD_IN, D_MODEL) for a in grads["wt_in"]]
            state, token = _reduce_scatter_start("w_in", ["w_in"], [own], [sibling])
        rs_state.append(state)
        return token

    loss_term, grad_x, grads = _local_step(
        x[0], hst, parts, tabs, bias, mem[0], loss_target[0], pre_norm, mem_norm, post_norm, wt_in, late_weights,
        reduce_start=reduce_start)

    small = jnp.concatenate([_rows128(grads["pre_norm"], 8), _rows128(grads["mem_norm"], 8),
                             _rows128(grads["post_norm"], 8), _rows128(grads["na_rpb"], 32),
                             _rows128(loss_term, 8)], axis=0)
    s_send, s_recv, s_own, s_land, s_token = _exchange_start(
        "gather_small_start", "gather", [small], [jax.ShapeDtypeStruct((N_DEV,) + small.shape, F32)])
    grad = {}
    weights = {
        "pre_norm": (pre_norm, m_pre_norm, v_pre_norm), "w_in": (w_in, m_w_in, v_w_in),
        "merge_bias": (merge_bias, m_merge_bias, v_merge_bias), "na_rpb": (na_rpb, m_na_rpb, v_na_rpb),
        "mem_norm": (mem_norm, m_mem_norm, v_mem_norm), "w_mem_kv": (w_mem_kv, m_w_mem_kv, v_w_mem_kv),
        "w_branch_a": (w_branch_a, m_w_branch_a, v_w_branch_a), "w_branch_b": (w_branch_b, m_w_branch_b, v_w_branch_b),
        "w_branch_c": (w_branch_c, m_w_branch_c, v_w_branch_c), "w_out": (w_out, m_w_out, v_w_out),
        "post_norm": (post_norm, m_post_norm, v_post_norm)}
    order = ["pre_norm", "w_in", "merge_bias", "na_rpb", "mem_norm", "w_mem_kv", "w_branch_a", "w_branch_b",
             "w_branch_c", "w_out", "post_norm"]
    delta, new_m, new_v = {}, {}, {}

    def update(n, dep=None):
        w, m, v = weights[n]
        shape = w.shape
        two_d = (-1, shape[-1])
        dl, nm, nv = _adamw("adamw_" + n, w.reshape(two_d), grad[n].reshape(two_d), m.reshape(two_d),
                            v.reshape(two_d), dep)
        delta[n], new_m[n], new_v[n] = dl.reshape(shape), nm.reshape(shape), nv.reshape(shape)
        return dl

    def update_sharded(n, sums, recv, transposed, rows=None, dep=None):
        w, m, v = weights[n]
        g, dl, nm, nv = _adamw_chips("adamw_" + n, sums, recv, w[0], m[0], v[0], transposed, rows, dep)
        grad[n], delta[n], new_m[n], new_v[n] = g[None], dl[None], nm[None], nv[None]
        return dl

    _, sums, recv2 = _reduce_scatter_wait(rs_state[0], s_token)
    dep = None
    for i, (n, transposed) in enumerate((("w_mem_kv", False), ("w_out", False), ("w_branch_a", True),
                                         ("w_branch_b", True), ("w_branch_c", True))):
        dep = update_sharded(n, sums[i], recv2[i], transposed, dep=dep)
    grad["merge_bias"] = _add_chips("add_chips_mb", sums[5], recv2[5], 8)[:3][None]
    update("merge_bias")
    s_own, s_land = _exchange_wait("gather_small_wait", "gather", s_send, s_recv, s_own, s_land, dep)
    total = _sum_devices(lax.dynamic_update_slice(s_land[0], s_own[0][None], (me, 0, 0)))
    loss = total[56, 0]
    grad.update({"pre_norm": total[0:8].reshape(1, D_MODEL), "mem_norm": total[8:16].reshape(1, D_MODEL),
                 "post_norm": total[16:24].reshape(1, D_MODEL),
                 "na_rpb": total[24:56].reshape(-1)[:8 * 15 * 31].reshape(1, 8, 15, 31)})
    dep = None
    for n in ("pre_norm", "na_rpb", "mem_norm", "post_norm"):
        dep = update(n, dep)
    _, sums_in, recv_in = _reduce_scatter_wait(rs_state[1], dep)
    update_sharded("w_in", sums_in[0], recv_in[0], True, 256)

    return (loss, grad_x[None], *[grad[n] for n in order], *[delta[n] for n in order],
            *[new_m[n] for n in order], *[new_v[n] for n in order])
```

```python
import functools

import numpy as np
import jax
import jax.numpy as jnp
from jax import lax
from jax.experimental import pallas as pl
from jax.experimental.pallas import tpu as pltpu

F32 = jnp.float32
BF16 = jnp.bfloat16

SEQ = 2048
D_MODEL = 1024
N_IN = 11264
N_DEV = 8
SHARD_IN = N_IN // N_DEV
HEAD_DIM = 64
GRID_W = 64
NA_ROWS = 8
MEM_LEN = 256
DILATIONS = (1, 4, 16)
REACH = 64
ROPE_THETA = 500000.0
ROPE_DIM = 16
EPS = 1e-6
NEG = -1e30
ADAM_LR = 0.001
ADAM_B1 = 0.9
ADAM_B2 = 0.999
ADAM_EPS = 1e-08
ADAM_WD = 0.01
ADAM_STEP = 10

VMEM_LIMIT_BYTES = 56 * 1024 * 1024
MESH_ID = pl.DeviceIdType.MESH

NN = (((1,), (0,)), ((), ()))
NT = (((1,), (1,)), ((), ()))
TN = (((0,), (0,)), ((), ()))


def _params(sem=None):
    return pltpu.CompilerParams(dimension_semantics=sem, vmem_limit_bytes=VMEM_LIMIT_BYTES)


def _iota(shape, dim):
    return lax.broadcasted_iota(jnp.int32, shape, dim)


def _sigmoid(x):
    return 1.0 / (1.0 + jnp.exp(-x))


def _rope_tables():
    half = ROPE_DIM // 2
    inv = (ROPE_THETA ** (-np.arange(half, dtype=np.float64) * 2.0 / ROPE_DIM)).astype(np.float32)
    pos = np.arange(SEQ, dtype=np.float32)
    ang = pos[:, None] * inv[None, :]
    cos, sin = np.cos(ang), np.sin(ang)
    zeros = np.zeros_like(cos)
    rest = HEAD_DIM - ROPE_DIM
    c64 = np.concatenate([cos, cos, np.ones((SEQ, rest), np.float32)], axis=1)
    s1 = np.concatenate([zeros, sin, np.zeros((SEQ, rest), np.float32)], axis=1)
    s2 = np.concatenate([-sin, zeros, np.zeros((SEQ, rest), np.float32)], axis=1)

    def fold(t, d):
        return t.reshape(SEQ // d, d, t.shape[1]).transpose(1, 0, 2).reshape(SEQ, t.shape[1])

    tabs = [np.stack([np.tile(fold(t, d), (1, 2)) for t in (c64, s1, s2)], axis=0) for d in DILATIONS]
    return jnp.asarray(np.stack(tabs, axis=0), dtype=F32)


def _rope(a, c, s1, s2):
    return a * c + pltpu.roll(a, 8, 1) * s1 + pltpu.roll(a, 120, 1) * s2


def _rope_t(a, c, s1, s2):
    return a * c + pltpu.roll(a * s1, 120, 1) + pltpu.roll(a * s2, 8, 1)


def _perm_of_block(j):
    return jnp.where(j < 3, 0, jnp.where(j < 6, 1, jnp.where(j < 9, 2, 0)))


def _mm(name, a, b, out_shape, out_dtype, grid, a_spec, b_spec, o_spec, acc_shape, dims, k_axis, nk):
    def body(a_ref, b_ref, o_ref, acc_ref):
        k = pl.program_id(k_axis)

        @pl.when(k == 0)
        def _():
            acc_ref[...] = jnp.zeros(acc_shape, F32)

        acc_ref[...] += lax.dot_general(a_ref[...], b_ref[...], dims, preferred_element_type=F32)

        @pl.when(k == nk - 1)
        def _():
            o_ref[...] = acc_ref[...].astype(out_dtype)

    sem = tuple("arbitrary" if ax == k_axis else "parallel" for ax in range(len(grid)))
    return pl.pallas_call(
        body, name=name, grid=grid, in_specs=[a_spec, b_spec], out_specs=o_spec,
        out_shape=jax.ShapeDtypeStruct(out_shape, out_dtype),
        scratch_shapes=[pltpu.VMEM(acc_shape, F32)], compiler_params=_params(sem))(a, b)


def _mm_simple(name, a, b, dims, out_dtype, tm, tn, tk):
    if dims is NN:
        m, kk = a.shape
        n = b.shape[1]
        a_spec = pl.BlockSpec((tm, tk), lambda i, j, k: (i, k))
        b_spec = pl.BlockSpec((tk, tn), lambda i, j, k: (k, j))
    elif dims is NT:
        m, kk = a.shape
        n = b.shape[0]
        a_spec = pl.BlockSpec((tm, tk), lambda i, j, k: (i, k))
        b_spec = pl.BlockSpec((tn, tk), lambda i, j, k: (j, k))
    else:
        kk, m = a.shape
        n = b.shape[1]
        a_spec = pl.BlockSpec((tk, tm), lambda i, j, k: (k, i))
        b_spec = pl.BlockSpec((tk, tn), lambda i, j, k: (k, j))
    grid = (m // tm, n // tn, kk // tk)
    o_spec = pl.BlockSpec((tm, tn), lambda i, j, k: (i, j))
    return _mm(name, a, b, (m, n), out_dtype, grid, a_spec, b_spec, o_spec, (tm, tn), dims, 2, kk // tk)


def _rmsnorm_fwd(name, x, gain, rows):
    n, d = x.shape

    def body(x_ref, g_ref, o_ref):
        xv = x_ref[...]
        rstd = lax.rsqrt(jnp.mean(xv * xv, axis=1, keepdims=True) + EPS)
        o_ref[...] = (xv * rstd * g_ref[...]).astype(BF16)

    return pl.pallas_call(
        body, name=name, grid=(n // rows,),
        in_specs=[pl.BlockSpec((rows, d), lambda i: (i, 0)), pl.BlockSpec((1, d), lambda i: (0, 0))],
        out_specs=pl.BlockSpec((rows, d), lambda i: (i, 0)),
        out_shape=jax.ShapeDtypeStruct((n, d), BF16), compiler_params=_params(("parallel",)))(x, gain)


def _folded_rows(first, rows, d):
    if d == 1:
        return pl.ds(pl.multiple_of(first, rows), rows)
    mlen = SEQ // d
    return pl.ds((first % mlen) * d + first // mlen, rows, stride=d)


def _prenorm_fold(x, gain, dep=None):
    rows = 128
    nchunk = D_MODEL // 128
    dep_specs, dep_args = _dep_operand(dep)

    def body(*refs):
        x_refs, g_ref, hs_ref, hst_ref = refs[:nchunk], refs[nchunk], refs[-2], refs[-1]
        first = pl.program_id(0) * rows
        for p, d in enumerate(DILATIONS):
            idx = _folded_rows(first, rows, d)
            xv = jnp.concatenate([r[idx, :] for r in x_refs], axis=1)
            rstd = lax.rsqrt(jnp.mean(xv * xv, axis=1, keepdims=True) + EPS)
            h = xv * rstd * g_ref[...]
            hs_ref[p] = h.astype(BF16)
            hst_ref[p] = h.T.astype(BF16)

    x_specs = [pl.BlockSpec((SEQ, 128), functools.partial(lambda c, i: (0, c), c)) for c in range(nchunk)]
    return pl.pallas_call(
        body, name="prenorm", grid=(SEQ // rows,),
        in_specs=x_specs + [pl.BlockSpec((1, D_MODEL), lambda i: (0, 0))] + dep_specs,
        out_specs=[pl.BlockSpec((3, rows, D_MODEL), lambda i: (0, i, 0)),
                   pl.BlockSpec((3, D_MODEL, rows), lambda i: (0, 0, i))],
        out_shape=[jax.ShapeDtypeStruct((3, SEQ, D_MODEL), BF16), jax.ShapeDtypeStruct((3, D_MODEL, SEQ), BF16)],
        compiler_params=_params(("parallel",)))(*([x] * nchunk), gain, *dep_args)


def _prenorm_bwd(x, gain, dh, dout):
    rows = 256

    def body(x_ref, g_ref, a_ref, do_ref, dx_ref, gg_ref):
        xv = x_ref[...]
        rstd = lax.rsqrt(jnp.mean(xv * xv, axis=1, keepdims=True) + EPS)
        xn = xv * rstd
        dh = jnp.concatenate([a_ref[c] for c in range(D_MODEL // 128)], axis=1)
        gdh = dh * g_ref[...]
        dx_ref[...] = rstd * (gdh - xn * jnp.mean(gdh * xn, axis=1, keepdims=True)) + do_ref[...]

        @pl.when(pl.program_id(0) == 0)
        def _():
            gg_ref[...] = jnp.zeros((1, D_MODEL), F32)

        gg_ref[...] += jnp.sum(dh * xn, axis=0, keepdims=True)

    row = pl.BlockSpec((rows, D_MODEL), lambda i: (i, 0))
    vec = pl.BlockSpec((1, D_MODEL), lambda i: (0, 0))
    return pl.pallas_call(
        body, name="prenorm_bwd", grid=(SEQ // rows,),
        in_specs=[row, vec, pl.BlockSpec((D_MODEL // 128, rows, 128), lambda i: (0, i, 0)), row], out_specs=[row, vec],
        out_shape=[jax.ShapeDtypeStruct((SEQ, D_MODEL), F32), jax.ShapeDtypeStruct((1, D_MODEL), F32)],
        compiler_params=_params(("arbitrary",)))(x, gain, dh, dout)


def _memnorm_bwd(mem, dmemn, dep=None):
    dep_specs, dep_args = _dep_operand(dep)

    def body(m_ref, d_ref, *rest):
        mv = m_ref[...]
        rstd = lax.rsqrt(jnp.mean(mv * mv, axis=1, keepdims=True) + EPS)
        rest[-1][...] = jnp.sum(d_ref[...] * mv * rstd, axis=0, keepdims=True)

    whole = pl.BlockSpec(memory_space=pltpu.VMEM)
    return pl.pallas_call(
        body, name="memnorm_bwd", in_specs=[whole, whole] + dep_specs,
        out_shape=jax.ShapeDtypeStruct((1, D_MODEL), F32), compiler_params=_params())(mem, dmemn, *dep_args)


def _dep_operand(dep):
    return ([], []) if dep is None else ([pl.BlockSpec(memory_space=pl.ANY)], [dep])


def _in_proj(name, hs, wt, tabs, order, prev=None, dep=None):
    tm, tn = 512, 512
    prev_specs, prev_args = ([], []) if prev is None else ([ANY], [prev])
    dep_specs, dep_args = _dep_operand(dep)

    def body(order_ref, h_ref, w_ref, t_ref, *rest):
        o_ref = rest[-1]
        j = order_ref[pl.program_id(0)]
        is_rope = jnp.logical_and(j < 9, j % 3 != 2)
        row_slices = [slice(r * tm, (r + 1) * tm) for r in range(SEQ // tm)]

        def product(rs):
            return lax.dot_general(h_ref[rs, :], w_ref[...], NT, preferred_element_type=F32)

        @pl.when(is_rope)
        def _():
            for rs in row_slices:
                acc = product(rs)
                c, s1, s2 = t_ref[0, rs, :], t_ref[1, rs, :], t_ref[2, rs, :]
                for q in range(tn // 128):
                    a = acc[:, q * 128:(q + 1) * 128]
                    o_ref[rs, q * 128:(q + 1) * 128] = _rope(a, c, s1, s2).astype(BF16)

        @pl.when(jnp.logical_not(is_rope))
        def _():
            for rs in row_slices:
                o_ref[rs, :] = product(rs).astype(BF16)

    grid_spec = pltpu.PrefetchScalarGridSpec(
        num_scalar_prefetch=1, grid=(order.shape[0],),
        in_specs=[pl.BlockSpec((None, SEQ, D_MODEL), lambda t, o: (_perm_of_block(o[t]), 0, 0)),
                  pl.BlockSpec((tn, D_MODEL), lambda t, o: (o[t], 0)),
                  pl.BlockSpec((None, 3, SEQ, 128), lambda t, o: (_perm_of_block(o[t]), 0, 0, 0))] + prev_specs
        + dep_specs,
        out_specs=pl.BlockSpec((SEQ, tn), lambda t, o: (0, o[t])))
    return pl.pallas_call(
        body, name=name, grid_spec=grid_spec, out_shape=jax.ShapeDtypeStruct((SEQ, N_IN), BF16),
        input_output_aliases={} if prev is None else {4: 0},
        compiler_params=_params(("arbitrary",)))(order, hs, wt, tabs, *prev_args, *dep_args)


def _piece_blocks(pieces):
    return [(a, h * 512) for a, p in enumerate(pieces) for h in range(p.shape[1] // 512)]


def _block_fetch(piece_refs, blocks, buf, sem):
    def start(block, slot):
        for b, (a, col) in enumerate(blocks):
            @pl.when(block == b)
            def _():
                pltpu.make_async_copy(piece_refs[a].at[:, pl.ds(col, 512)], buf.at[slot], sem.at[slot]).start()

    def wait(slot):
        pltpu.make_async_copy(piece_refs[0].at[:, pl.ds(0, 512)], buf.at[slot], sem.at[slot]).wait()

    return start, wait


def _in_proj_dw(pieces, hst, dep=None):
    tn = 512
    blocks = _piece_blocks(pieces)
    nblk = len(blocks)
    npc = len(pieces)
    dep_specs, dep_args = _dep_operand(dep)

    def body(h_ref, *rest):
        piece_refs = rest[:npc]
        own_out, mirror, buf, sem, out_buf, send_sems, recv_sem, local_sems = rest[-8:]
        j = pl.program_id(0)
        slot = j % 2
        start, wait = _block_fetch(piece_refs, blocks, buf, sem)
        x, y, c = _place()

        def rows_of(step):
            return pl.ds(pl.multiple_of(step * tn, tn), tn)

        def to_sibling(step, slot_):
            return pltpu.make_async_remote_copy(
                src_ref=out_buf.at[slot_], dst_ref=mirror.at[rows_of(step)],
                send_sem=send_sems.at[slot_], recv_sem=recv_sem, device_id=(x, y, 1 - c), device_id_type=MESH_ID)

        def to_own(step, slot_):
            return pltpu.make_async_copy(out_buf.at[slot_], own_out.at[rows_of(step)], local_sems.at[slot_])

        @pl.when(j == 0)
        def _():
            start(j, slot)

        wait(slot)

        @pl.when(j + 1 < nblk)
        def _():
            start(j + 1, 1 - slot)

        acc = jnp.dot(h_ref[...], buf[slot], preferred_element_type=F32)

        @pl.when(j >= 2)
        def _():
            to_sibling(j - 2, slot).wait_send()
            to_own(j - 2, slot).wait()

        out_buf[slot] = acc.T.astype(BF16)
        to_sibling(j, slot).start()
        to_own(j, slot).start()

        @pl.when(j == nblk - 1)
        def _():
            to_sibling(j - 1, 1 - slot).wait_send()
            to_own(j - 1, 1 - slot).wait()
            to_sibling(j, slot).wait_send()
            to_own(j, slot).wait()
            pltpu.make_async_remote_copy(src_ref=mirror, dst_ref=mirror, send_sem=send_sems.at[0], recv_sem=recv_sem,
                                         device_id=(x, y, 1 - c), device_id_type=MESH_ID).wait_recv()

    return pl.pallas_call(
        body, name="in_proj_dw", grid=(nblk,),
        in_specs=[pl.BlockSpec((None, D_MODEL, SEQ), lambda j: (_perm_of_block(j), 0, 0))] + [ANY] * npc + dep_specs,
        out_specs=[ANY, ANY],
        out_shape=[jax.ShapeDtypeStruct((N_IN, D_MODEL), BF16), jax.ShapeDtypeStruct((N_IN, D_MODEL), BF16)],
        scratch_shapes=[pltpu.VMEM((2, SEQ, tn), BF16), pltpu.SemaphoreType.DMA((2,)),
                        pltpu.VMEM((2, tn, D_MODEL), BF16), pltpu.SemaphoreType.DMA((2,)), pltpu.SemaphoreType.DMA,
                        pltpu.SemaphoreType.DMA((2,))],
        compiler_params=_params(("arbitrary",)))(hst, *pieces, *dep_args)


def _in_proj_dh(pieces, wt, dep=None):
    tk = 512
    blocks = _piece_blocks(pieces)
    nblk = len(blocks)
    npc = len(pieces)
    nchunk = D_MODEL // 128

    def col(s):
        return jnp.where(s < 3, s, jnp.where(s < 16, s + 6, s - 13))

    dep_specs, dep_args = _dep_operand(dep)

    def body(w_ref, *rest):
        piece_refs = rest[:npc]
        o_ref, acc_ref, buf, sem = rest[-4:]
        s = pl.program_id(0)
        slot = s % 2
        start, wait = _block_fetch(piece_refs, blocks, buf, sem)

        @pl.when(s == 0)
        def _():
            start(col(s), slot)

        wait(slot)

        @pl.when(s + 1 < nblk)
        def _():
            start(col(s + 1), 1 - slot)

        row_slices = [slice(r * 512, (r + 1) * 512) for r in range(SEQ // 512)]

        def product(rs):
            return jnp.dot(buf[slot, rs, :], w_ref[...], preferred_element_type=F32)

        def accumulate(cond, to_out, init):
            @pl.when(cond)
            def _():
                for rs in row_slices:
                    prod = product(rs)
                    if not to_out:
                        if init:
                            acc_ref[rs, :] = prod
                        else:
                            acc_ref[rs, :] += prod
                        continue
                    for c in range(nchunk):
                        if init:
                            o_ref[c, rs, :] = prod[:, c * 128:(c + 1) * 128]
                        else:
                            o_ref[c, rs, :] += prod[:, c * 128:(c + 1) * 128]

        accumulate(s == 0, True, True)
        accumulate(jnp.logical_and(s > 0, s < 16), True, False)
        accumulate(jnp.logical_or(s == 16, s == 19), False, True)
        accumulate(jnp.logical_and(s > 16, s != 19), False, False)
        for last, d in ((18, 4), (21, 16)):
            @pl.when(s == last)
            def _():
                mlen = SEQ // d
                for r in range(d):
                    for c in range(nchunk):
                        o_ref[c, pl.ds(r, mlen, stride=d), :] += acc_ref[r * mlen:(r + 1) * mlen,
                                                                         c * 128:(c + 1) * 128]

    return pl.pallas_call(
        body, name="in_proj_dh", grid=(nblk,),
        in_specs=[pl.BlockSpec((tk, D_MODEL), lambda s: (col(s), 0))] + [ANY] * npc + dep_specs,
        out_specs=pl.BlockSpec((nchunk, SEQ, 128), lambda s: (0, 0, 0)),
        out_shape=jax.ShapeDtypeStruct((nchunk, SEQ, 128), F32),
        scratch_shapes=[pltpu.VMEM((SEQ, D_MODEL), F32), pltpu.VMEM((2, SEQ, tk), BF16),
                        pltpu.SemaphoreType.DMA((2,))],
        compiler_params=_params(("arbitrary",)))(wt, *pieces, *dep_args)


def _head_lanes(lanes, hh):
    return lanes >= 64 if hh == 1 else lanes < 64


def _head_rows(x, lanes, hh, pair):
    if not pair:
        return jnp.max(x, axis=1, keepdims=True)
    return jnp.max(jnp.where(_head_lanes(lanes, hh), x, -jnp.inf), axis=1, keepdims=True)


def _mask_head(x, lanes, hh, pair, scale=1.0):
    if not pair:
        return x
    xf = x.astype(F32) if scale == 1.0 else x.astype(F32) * scale
    return jnp.where(_head_lanes(lanes, hh), xf, 0.0).astype(BF16)


def _window(mode, qi, tq, mlen, tk):
    if mode == "dil":
        q0 = qi * tq
        seg = (q0 // mlen) * mlen
        ks = jnp.clip(q0 - REACH, seg, seg + mlen - tk)
        return pl.multiple_of(ks, 64)
    if mode == "na":
        r_start = jnp.clip(qi - NA_ROWS // 2, 0, SEQ // GRID_W - NA_ROWS)
        return pl.multiple_of(r_start * GRID_W, 64)
    return 0


def _band_mask(qi, tq, tk, ks):
    qpos = qi * tq + _iota((tq, tk), 0)
    kpos = ks + _iota((tq, tk), 1)
    return jnp.where(jnp.abs(qpos - kpos) <= REACH, 0.0, NEG).astype(F32)


def _stack_heads(x, lanes, pair, scale=1.0):
    if not pair:
        return x
    return jnp.concatenate([_mask_head(x, lanes, hh, pair, scale) for hh in range(2)], axis=0)


def _stack_rows(x, lanes, pair):
    if not pair:
        return _head_rows(x, lanes, 0, pair)
    return jnp.concatenate([_head_rows(x, lanes, hh, pair) for hh in range(2)], axis=0)


def _unstack_heads(x, lanes, pair, tq):
    if not pair:
        return x
    return jnp.where(lanes < 64, x[:tq], x[tq:])


def _scores(mode, qst, k, sscale, band, qi, bias_ref, pair):
    s = lax.dot_general(qst, k, NT, preferred_element_type=F32)
    if sscale != 1.0:
        s = s * sscale
    if mode == "dil":
        s = s + jnp.concatenate([band, band], axis=0)
    elif mode == "na":
        off = qi - jnp.clip(qi - NA_ROWS // 2, 0, SEQ // GRID_W - NA_ROWS)
        s = s + jnp.concatenate([bias_ref[0, off], bias_ref[1, off]], axis=0)
    return s


def _attn_cfg(mode, d):
    if mode == "dil":
        mlen = SEQ // d
        return dict(pair=True, tq=128, tk=min(256, mlen), mlen=mlen, lk=SEQ, scale=HEAD_DIM ** -0.5, units=4,
                    nsub=ATTN_SUBTILES)
    if mode == "na":
        return dict(pair=True, tq=GRID_W, tk=NA_ROWS * GRID_W, mlen=SEQ, lk=SEQ, scale=HEAD_DIM ** -0.5, units=4,
                    nsub=ATTN_SUBTILES)
    return dict(pair=False, tq=128, tk=MEM_LEN, mlen=SEQ, lk=MEM_LEN, scale=128 ** -0.5, units=4,
                nsub=ATTN_SUBTILES)


ATTN_SUBTILES = 16


def _attn_fwd(name, mode, q_arr, k_arr, v_arr, qcol, kcol, vcol, d=1, bias=None):
    cfg = _attn_cfg(mode, d)
    pair, tq, tk, mlen, lk, scale = cfg["pair"], cfg["tq"], cfg["tk"], cfg["mlen"], cfg["lk"], cfg["scale"]
    qscale, sscale = (scale, 1.0) if pair else (1.0, scale)
    nsub = cfg["nsub"]
    rows = nsub * tq

    def body(*refs):
        if mode == "na":
            q_ref, k_ref, v_ref, bias_ref, o_ref, l_ref = refs
        else:
            q_ref, k_ref, v_ref, o_ref, l_ref = refs
            bias_ref = None
        lanes = _iota((tq, 128), 1)
        qis = [pl.program_id(1) * nsub + sub for sub in range(nsub)]
        kss = [_window(mode, qi, tq, mlen, tk) for qi in qis]
        vs = [v_ref[pl.ds(ks, tk), :] for ks in kss]
        bands = [_band_mask(qi, tq, tk, ks) if mode == "dil" else None for qi, ks in zip(qis, kss)]
        ss = []
        for sub in range(nsub):
            qst = _stack_heads(q_ref[sub * tq:(sub + 1) * tq, :], lanes, pair, qscale)
            k = k_ref[pl.ds(kss[sub], tk), :]
            ss.append(_scores(mode, qst, k, sscale, bands[sub], qis[sub], bias_ref, pair))
        ms = [jnp.max(s_, axis=1, keepdims=True) for s_ in ss]
        ps = [jnp.exp(s_ - m) for s_, m in zip(ss, ms)]
        ls = [jnp.sum(p, axis=1, keepdims=True) for p in ps]
        os_ = [jnp.dot(p.astype(BF16), v, preferred_element_type=F32) for p, v in zip(ps, vs)]
        for sub in range(nsub):
            out = _unstack_heads(os_[sub] / ls[sub], lanes, pair, tq)
            lse = ms[sub] + jnp.log(ls[sub])
            lse = _unstack_heads(jnp.broadcast_to(lse, (lse.shape[0], 128)), lanes, pair, tq)
            dst = _folded_rows(qis[sub] * tq, tq, d) if mode == "dil" else slice(sub * tq, (sub + 1) * tq)
            o_ref[dst, :] = out
            l_ref[dst, :] = lse

    in_specs = [pl.BlockSpec((rows, 128), lambda u, i: (i, qcol + u)),
                pl.BlockSpec((lk, 128), lambda u, i: (0, kcol + u)),
                pl.BlockSpec((lk, 128), lambda u, i: (0, vcol + u))]
    args = [q_arr, k_arr, v_arr]
    if mode == "na":
        in_specs.append(pl.BlockSpec((2, NA_ROWS, GRID_W, NA_ROWS * GRID_W), lambda u, i: (u, 0, 0, 0)))
        args.append(bias)
    if mode == "dil":
        out_spec = pl.BlockSpec((SEQ, 128), lambda u, i: (0, u))
    else:
        out_spec = pl.BlockSpec((rows, 128), lambda u, i: (i, u))
    return pl.pallas_call(
        body, name=name, grid=(cfg["units"], SEQ // rows), in_specs=in_specs, out_specs=[out_spec, out_spec],
        out_shape=[jax.ShapeDtypeStruct((SEQ, 512), F32), jax.ShapeDtypeStruct((SEQ, 512), F32)],
        compiler_params=_params(("parallel", "arbitrary")))(*args)


def _attn_bwd(name, mode, q_arr, k_arr, v_arr, qcol, kcol, vcol, do, lse, dp=None, o=None, d=1, bias=None,
              tabs=None):
    cfg = _attn_cfg(mode, d)
    pair, tq, tk, mlen, lk, scale = cfg["pair"], cfg["tq"], cfg["tk"], cfg["mlen"], cfg["lk"], cfg["scale"]
    qscale, sscale = (scale, 1.0) if pair else (1.0, scale)
    nsub = cfg["nsub"]
    rows = nsub * tq
    nq = SEQ // rows
    kv_dtype = F32 if mode == "mem" else BF16

    def body(*refs):
        refs = list(refs)
        q_ref, k_ref, v_ref, do_ref, l_ref = refs[:5]
        rest = refs[5:]
        bias_ref = tq_ref = tk_ref = db_ref = None
        if mode == "dil":
            dp_ref, tq_ref, tk_ref, dq_ref, dk_ref, dv_ref, dk_acc, dv_acc = rest
        elif mode == "na":
            o_ref, bias_ref, dq_ref, dk_ref, dv_ref, db_ref, dk_acc, dv_acc = rest
        else:
            o_ref, dq_ref, dk_ref, dv_ref, dk_acc, dv_acc = rest
        step = pl.program_id(1)

        @pl.when(step == 0)
        def _():
            dk_acc[...] = jnp.zeros((lk, 128), F32)
            dv_acc[...] = jnp.zeros((lk, 128), F32)
            if mode == "na":
                db_ref[...] = jnp.zeros(db_ref.shape, F32)

        lanes = _iota((tq, 128), 1)
        qis = [step * nsub + sub for sub in range(nsub)]
        sls = [slice(sub * tq, (sub + 1) * tq) for sub in range(nsub)]
        kss = [_window(mode, qi, tq, mlen, tk) for qi in qis]
        ks_ = [k_ref[pl.ds(ks, tk), :] for ks in kss]
        vs = [v_ref[pl.ds(ks, tk), :] for ks in kss]
        qsts, dosts, lses, dphs = [], [], [], []
        for sub in range(nsub):
            if mode == "dil":
                src = _folded_rows(qis[sub] * tq, tq, d)
                dov = do_ref[src, :].astype(BF16)
                lsev = l_ref[src, :]
                dphs.append(_stack_rows(dp_ref[src, :], lanes, pair))
            else:
                dov = do_ref[sls[sub], :]
                lsev = l_ref[sls[sub], :]
                dpv = dov.astype(F32) * o_ref[sls[sub], :]
                if pair:
                    dphs.append(jnp.concatenate(
                        [jnp.sum(jnp.where(_head_lanes(lanes, hh), dpv, 0.0), axis=1, keepdims=True)
                         for hh in range(2)], axis=0))
                else:
                    dphs.append(jnp.sum(dpv, axis=1, keepdims=True))
            qsts.append(_stack_heads(q_ref[sls[sub], :], lanes, pair, qscale))
            dosts.append(_stack_heads(dov, lanes, pair))
            lses.append(_stack_rows(lsev, lanes, pair))
        bands = [_band_mask(qi, tq, tk, ks) if mode == "dil" else None for qi, ks in zip(qis, kss)]
        ss = [_scores(mode, qsts[sub], ks_[sub], sscale, bands[sub], qis[sub], bias_ref, pair) for sub in range(nsub)]
        dpms = [lax.dot_general(dosts[sub], vs[sub], NT, preferred_element_type=F32) for sub in range(nsub)]
        ps = [jnp.exp(s_ - lse) for s_, lse in zip(ss, lses)]
        dss = [p * (dpm - dph) for p, dpm, dph in zip(ps, dpms, dphs)]
        if mode == "na":
            for sub, ds in enumerate(dss):
                off = qis[sub] - jnp.clip(qis[sub] - NA_ROWS // 2, 0, SEQ // GRID_W - NA_ROWS)
                db_ref[0, off] += ds[:tq]
                db_ref[1, off] += ds[tq:]
        dsbs = [ds.astype(BF16) for ds in dss]
        dvs = [lax.dot_general(p.astype(BF16), dosts[sub], TN, preferred_element_type=F32)
               for sub, p in enumerate(ps)]
        dqs = [jnp.dot(dsb, ks_[sub], preferred_element_type=F32) * scale for sub, dsb in enumerate(dsbs)]
        dks = [lax.dot_general(dsb, qsts[sub], TN, preferred_element_type=F32) for sub, dsb in enumerate(dsbs)]
        for sub in range(nsub):
            sl = sls[sub]
            dq = _unstack_heads(dqs[sub], lanes, pair, tq)
            if mode == "dil":
                dq = _rope_t(dq, tq_ref[0, sl, :], tq_ref[1, sl, :], tq_ref[2, sl, :])
            dq_ref[sl, :] = dq.astype(BF16)
            dk_acc[pl.ds(kss[sub], tk), :] += dks[sub] if pair else dks[sub] * scale
            dv_acc[pl.ds(kss[sub], tk), :] += dvs[sub]

        @pl.when(step == nq - 1)
        def _():
            dkv = dk_acc[...]
            if mode == "dil":
                dkv = _rope_t(dkv, tk_ref[0], tk_ref[1], tk_ref[2])
            dk_ref[...] = dkv.astype(kv_dtype)
            dv_ref[...] = dv_acc[...].astype(kv_dtype)

    q_spec = pl.BlockSpec((rows, 128), lambda u, i: (i, qcol + u))
    row_spec = pl.BlockSpec((rows, 128), lambda u, i: (i, u))
    kv_out = pl.BlockSpec((lk, 128), lambda u, i: (0, u))
    whole = pl.BlockSpec((SEQ, 128), lambda u, i: (0, u))
    nat_spec = whole if mode == "dil" else row_spec
    in_specs = [q_spec,
                pl.BlockSpec((lk, 128), lambda u, i: (0, kcol + u)),
                pl.BlockSpec((lk, 128), lambda u, i: (0, vcol + u)),
                nat_spec, nat_spec]
    args = [q_arr, k_arr, v_arr, do, lse]
    out_specs = [row_spec, kv_out, kv_out]
    out_shape = [jax.ShapeDtypeStruct((SEQ, 512), BF16), jax.ShapeDtypeStruct((lk, 512), kv_dtype),
                 jax.ShapeDtypeStruct((lk, 512), kv_dtype)]
    if mode == "dil":
        in_specs += [whole, pl.BlockSpec((3, rows, 128), lambda u, i: (0, i, 0)),
                     pl.BlockSpec((3, SEQ, 128), lambda u, i: (0, 0, 0))]
        args += [dp, tabs, tabs]
    elif mode == "na":
        b_spec = pl.BlockSpec((2, NA_ROWS, GRID_W, NA_ROWS * GRID_W), lambda u, i: (u, 0, 0, 0))
        in_specs += [row_spec, b_spec]
        args += [o, bias]
        out_specs.append(b_spec)
        out_shape.append(jax.ShapeDtypeStruct((8, NA_ROWS, GRID_W, NA_ROWS * GRID_W), F32))
    else:
        in_specs.append(row_spec)
        args.append(o)
    return pl.pallas_call(
        body, name=name, grid=(cfg["units"], nq), in_specs=in_specs, out_specs=out_specs, out_shape=out_shape,
        scratch_shapes=[pltpu.VMEM((lk, 128), F32), pltpu.VMEM((lk, 128), F32)],
        compiler_params=_params(("parallel", "arbitrary")))(*args)


def _na_geometry():
    qc = _iota((GRID_W, 128), 0)
    lane = _iota((GRID_W, 128), 1)
    kc = lane & 63
    c_start = jnp.clip(qc - 8, 0, GRID_W - 16)
    valid = jnp.logical_and(kc >= c_start, kc < c_start + 16)
    return lane, valid


def _na_bias(rpb_rows, dep=None):
    dep_specs, dep_args = _dep_operand(dep)

    def body(r_ref, *rest):
        o_ref, t_ref = rest[-2:]
        lane, valid = _na_geometry()
        for dd in range(14):
            row_a = jnp.broadcast_to(r_ref[dd:dd + 1, :], (GRID_W, 128))
            row_b = jnp.broadcast_to(r_ref[dd + 1:dd + 2, :], (GRID_W, 128))
            both = jnp.where(lane < 64, row_a, pltpu.roll(row_b, 64, 1))
            t = pltpu.roll(both, 128 - 15, 1, stride=1, stride_axis=0)
            t_ref[dd] = jnp.where(valid, t, NEG)
        for off in range(NA_ROWS):
            for p in range(4):
                o_ref[off, :, p * 128:(p + 1) * 128] = t_ref[2 * p - off + 7]

    return pl.pallas_call(
        body, name="na_bias", grid=(8,),
        in_specs=[pl.BlockSpec((None, 16, 128), lambda h: (h, 0, 0))] + dep_specs,
        out_specs=pl.BlockSpec((None, NA_ROWS, GRID_W, NA_ROWS * GRID_W), lambda h: (h, 0, 0, 0)),
        out_shape=jax.ShapeDtypeStruct((8, NA_ROWS, GRID_W, NA_ROWS * GRID_W), F32),
        scratch_shapes=[pltpu.VMEM((14, GRID_W, 128), F32)],
        compiler_params=_params(("parallel",)))(rpb_rows, *dep_args)


def _na_bias_bwd(dbias, dep=None):
    dep_specs, dep_args = _dep_operand(dep)

    def body(d_ref, *rest):
        o_ref = rest[-1]
        lane, valid = _na_geometry()
        reverse = (_iota((GRID_W, GRID_W), 0) + _iota((GRID_W, GRID_W), 1) == GRID_W - 1).astype(F32)
        o_ref[...] = jnp.zeros((16, 128), F32)
        for dd in range(14):
            t = jnp.zeros((GRID_W, 128), F32)
            for off in range(NA_ROWS):
                for p in range(4):
                    if 2 * p - off + 7 == dd:
                        t = t + d_ref[off, :, p * 128:(p + 1) * 128]
            t = jnp.dot(reverse, jnp.where(valid, t, 0.0), precision=lax.Precision.HIGHEST,
                        preferred_element_type=F32)
            t = pltpu.roll(t, 128 - (GRID_W - 16), 1, stride=1, stride_axis=0)
            o_ref[dd:dd + 1, :] = jnp.sum(t, axis=0, keepdims=True)

    return pl.pallas_call(
        body, name="na_bias_bwd", grid=(8,),
        in_specs=[pl.BlockSpec((None, NA_ROWS, GRID_W, NA_ROWS * GRID_W), lambda h: (h, 0, 0, 0))] + dep_specs,
        out_specs=pl.BlockSpec((None, 16, 128), lambda h: (h, 0, 0)),
        out_shape=jax.ShapeDtypeStruct((8, 16, 128), F32),
        compiler_params=_params(("parallel",)))(dbias, *dep_args)


GATE_ROWS = 128


def _group_weights(l0, l1, l2):
    m = jnp.maximum(jnp.maximum(l0, l1), l2)
    e0, e1, e2 = jnp.exp(l0 - m), jnp.exp(l1 - m), jnp.exp(l2 - m)
    inv = 1.0 / (e0 + e1 + e2)
    return e0 * inv, e1 * inv, e2 * inv


def _gate_block(o_grp, l_grp, out_b, out_c, parts, x, target, merge_bias, wts, w_out, gain, head_sum):
    rows = GATE_ROWS
    r512 = pl.BlockSpec((rows, 512), lambda i: (i, 0))
    r1024 = pl.BlockSpec((rows, D_MODEL), lambda i: (i, 0))
    silu_cols = [pl.BlockSpec((rows, 512), functools.partial(lambda b, i: (i, b), 13 + b)) for b in range(3)]
    logit_cols = [pl.BlockSpec((rows, D_MODEL), functools.partial(lambda b, i: (i, b), 8 + b)) for b in range(3)]

    def body(o0, o1, o2, l0, l1, l2, ob, oc, ga, gb, gc, la, lb, lc, x_ref, t_ref, mb, wa, wb, wc, wo_ref, gn_ref,
             hs_ref, dout_ref, dla, dlb, dlc, dga, dgb, dgc, do0, do1, do2, dp0, dp1, dp2, dob, doc, err_ref, gg_ref,
             gmb, gwa, gwb, gwc, gwo, acc_a, acc_b, acc_c, acc_o):
        step = pl.program_id(0)
        ws = _group_weights(l0[...], l1[...], l2[...])
        out_a = ws[0] * o0[...] + ws[1] * o1[...] + ws[2] * o2[...]
        branches = ((out_a, ga, la, wa, acc_a, dla, dga), (ob[...], gb, lb, wb, acc_b, dlb, dgb),
                    (oc[...], gc, lc, wc, acc_c, dlc, dgc))

        @pl.when(step == 0)
        def _():
            for acc in (acc_a, acc_b, acc_c, acc_o):
                acc[...] = jnp.zeros(acc.shape, F32)
            err_ref[...] = jnp.zeros((1, D_MODEL), F32)
            gg_ref[...] = jnp.zeros((1, D_MODEL), F32)
            gmb[...] = jnp.zeros((3, D_MODEL), F32)

        y = jnp.zeros((rows, D_MODEL), F32)
        zs, gates, silus, dsilus, us = [], [], [], [], []
        for b, (ov, g_ref, l_ref, w_ref, _, _, _) in enumerate(branches):
            g = g_ref[...].astype(F32)
            sg = _sigmoid(g)
            silus.append(g * sg)
            dsilus.append(sg * (1.0 + g * (1.0 - sg)))
            us.append((ov * silus[b]).astype(BF16))
            zs.append(lax.dot_general(us[b], w_ref[...], NT, preferred_element_type=F32))
            gates.append(_sigmoid(l_ref[...].astype(F32) + mb[b:b + 1, :]))
            y = y + gates[b] * zs[b]
        yb = y.astype(BF16)
        y2 = jnp.dot(yb, wo_ref[...], preferred_element_type=F32)
        rstd = lax.rsqrt(jnp.mean(y2 * y2, axis=1, keepdims=True) + EPS)
        yn = y2 * rstd
        gv = gn_ref[...]
        err = x_ref[...] + yn * gv - t_ref[...]
        dout = err * (1.0 / D_MODEL)
        dout_ref[...] = dout
        dn = dout * gv
        dy2 = (rstd * (dn - yn * jnp.mean(dn * yn, axis=1, keepdims=True))).astype(BF16)
        acc_o[...] += lax.dot_general(yb, dy2, TN, preferred_element_type=F32)
        err_ref[...] += jnp.sum(err * err, axis=0, keepdims=True)
        gg_ref[...] += jnp.sum(dout * yn, axis=0, keepdims=True)
        dy = lax.dot_general(dy2, wo_ref[...], NT, preferred_element_type=F32)
        dos = []
        for b, (ov, _, _, w_ref, acc, dl_ref, dg_ref) in enumerate(branches):
            dl = dy * zs[b] * gates[b] * (1.0 - gates[b])
            dl_ref[...] = dl.astype(BF16)
            gmb[b:b + 1, :] += jnp.sum(dl, axis=0, keepdims=True)
            dz = (dy * gates[b]).astype(BF16)
            acc[...] += lax.dot_general(dz, us[b], TN, preferred_element_type=F32)
            du = jnp.dot(dz, w_ref[...], preferred_element_type=F32)
            dos.append(du * silus[b])
            dg_ref[...] = (du * ov * dsilus[b]).astype(BF16)
        dob[...] = dos[1].astype(BF16)
        doc[...] = dos[2].astype(BF16)
        row_term = jnp.dot(dos[0] * out_a, hs_ref[...], precision=lax.Precision.HIGHEST, preferred_element_type=F32)
        for wg, do_ref, dp_ref in zip(ws, (do0, do1, do2), (dp0, dp1, dp2)):
            do_ref[...] = wg * dos[0]
            dp_ref[...] = wg * row_term

        @pl.when(step == SEQ // rows - 1)
        def _():
            for acc, out in ((acc_a, gwa), (acc_b, gwb), (acc_c, gwc), (acc_o, gwo)):
                out[...] = acc[...].astype(BF16)

    full = lambda shape: pl.BlockSpec(shape, lambda i: (0,) * len(shape))
    vec = pl.BlockSpec((1, D_MODEL), lambda i: (0, 0))
    acc3 = pl.BlockSpec((3, D_MODEL), lambda i: (0, 0))
    in_specs = ([r512] * 8 + silu_cols + logit_cols + [r1024, r1024, full((3, D_MODEL))]
                + [full((D_MODEL, 512))] * 3 + [full((D_MODEL, D_MODEL)), vec, full((512, 512))])
    out_specs = ([r1024] + [r1024] * 3 + [r512] * 3 + [r512] * 6 + [r512] * 2 + [vec, vec, acc3]
                 + [full((D_MODEL, 512))] * 3 + [full((D_MODEL, D_MODEL))])
    bf, f32 = BF16, F32
    sds = jax.ShapeDtypeStruct
    out_shape = ([sds((SEQ, D_MODEL), f32)] + [sds((SEQ, D_MODEL), bf)] * 3 + [sds((SEQ, 512), bf)] * 3
                 + [sds((SEQ, 512), f32)] * 6 + [sds((SEQ, 512), bf)] * 2 + [sds((1, D_MODEL), f32)] * 2
                 + [sds((3, D_MODEL), f32)] + [sds((D_MODEL, 512), bf)] * 3 + [sds((D_MODEL, D_MODEL), bf)])
    res = pl.pallas_call(
        body, name="gate_block", grid=(SEQ // rows,), in_specs=in_specs, out_specs=out_specs, out_shape=out_shape,
        scratch_shapes=[pltpu.VMEM((D_MODEL, 512), F32)] * 3 + [pltpu.VMEM((D_MODEL, D_MODEL), F32)],
        compiler_params=_params(("arbitrary",)))(
            *o_grp, *l_grp, out_b, out_c, parts, parts, parts, parts, parts, parts, x, target, merge_bias, *wts, w_out,
            gain, head_sum)
    return dict(dout=res[0], dlog=res[1:4], dg=res[4:7], do_grp=res[7:10], dp_grp=res[10:13], do_b=res[13],
                do_c=res[14], err_sq=res[15], g_post=res[16], g_mb=res[17], g_wt=res[18:21], g_w_out=res[21])


def _local_step(x, hst, parts, tabs, bias, mem, target, pre_norm, mem_norm, post_norm, wt_in, late_weights,
                reduce_start=None):
    o_grp, l_grp = [], []
    for g, d in enumerate(DILATIONS):
        o, l = _attn_fwd("dil_fwd_%d" % g, "dil", parts, parts, parts, 12 * g, 12 * g + 4, 12 * g + 8, d=d)
        o_grp.append(o)
        l_grp.append(l)
    out_b, lse_b = _attn_fwd("na_fwd", "na", parts, parts, parts, 36, 40, 44, bias=bias)
    merge_bias, w_kv, wt_a, wt_b, wt_c, w_out = late_weights(sum(a[:8, :128] for a in [out_b] + o_grp))
    memn = _rmsnorm_fwd("memnorm", mem, mem_norm, MEM_LEN)
    kv_m = _mm_simple("mem_kv", memn, w_kv, NN, BF16, MEM_LEN, 512, D_MODEL)
    out_c, lse_c = _attn_fwd("mem_fwd", "mem", parts, kv_m, kv_m, 48, 0, 4)

    rr = _iota((512, 512), 0) // HEAD_DIM
    cc = _iota((512, 512), 1) // HEAD_DIM
    head_sum = (rr == cc).astype(F32)
    gb = _gate_block(o_grp, l_grp, out_b, out_c, parts, x, target, merge_bias, (wt_a, wt_b, wt_c), w_out, post_norm,
                     head_sum)
    dout, dlog, dg, g_wt, g_w_out = gb["dout"], gb["dlog"], gb["dg"], gb["g_wt"], gb["g_w_out"]
    do_grp, dp_grp, do_b, do_c, g_post, g_mb = (gb["do_grp"], gb["dp_grp"], gb["do_b"], gb["do_c"], gb["g_post"],
                                                gb["g_mb"])
    loss = 0.5 * jnp.sum(gb["err_sq"]) / D_MODEL

    dqkv = []
    for g, d in enumerate(DILATIONS):
        dq, dk, dv = _attn_bwd("dil_bwd_%d" % g, "dil", parts, parts, parts, 12 * g, 12 * g + 4, 12 * g + 8,
                               do_grp[g], l_grp[g], dp=dp_grp[g], d=d, tabs=tabs[g])
        dqkv += [dq, dk, dv]
    dq_b, dk_b, dv_b, dbias = _attn_bwd("na_bwd", "na", parts, parts, parts, 36, 40, 44, do_b, lse_b, o=out_b,
                                        bias=bias)
    dq_c, dk_m, dv_m = _attn_bwd("mem_bwd", "mem", parts, kv_m, kv_m, 48, 0, 4, do_c, lse_c, o=out_c)

    dkv = jnp.concatenate([dk_m, dv_m], axis=1).astype(BF16)
    g_w_kv = _mm_simple("mem_kv_dw", memn, dkv, TN, BF16, D_MODEL, 512, MEM_LEN)
    dmemn = _mm_simple("mem_kv_dx", dkv, w_kv, NT, F32, MEM_LEN, 512, D_MODEL)

    grads = dict(w_kv=g_w_kv, wt_a=g_wt[0], wt_b=g_wt[1], wt_c=g_wt[2], w_out=g_w_out, merge_bias=g_mb,
                 post_norm=g_post)
    dep = None
    if reduce_start is not None:
        reduce_start("rest_sibling", grads)
        dep = reduce_start("rest_chips", grads, sum(a[:8, :128] for a in (dqkv[0], dqkv[3], dqkv[6], dq_b, dq_c)))
    dparts = dqkv + [dq_b, dk_b, dv_b, dq_c] + list(dg) + list(dlog)
    grads["wt_in"] = _in_proj_dw(dparts, hst, dep)
    dep = reduce_start("w_in", grads) if reduce_start is not None else None
    dh = _in_proj_dh(dparts, wt_in, dep)
    if reduce_start is not None:
        dep = reduce_start("w_in_second", grads, dh)
    grad_x, grads["pre_norm"] = _prenorm_bwd(x, pre_norm, dh, dout)
    g_rpb_t = _na_bias_bwd(dbias, dep)
    grads["na_rpb"] = g_rpb_t[:, :15, :31] + jnp.pad(g_rpb_t[:, :14, 64:95], ((0, 0), (1, 0), (0, 0)))
    grads["mem_norm"] = _memnorm_bwd(mem, dmemn, dep)
    return loss, grad_x, grads


ANY = pl.BlockSpec(memory_space=pl.ANY)


def _place():
    return lax.axis_index("x"), lax.axis_index("y"), lax.axis_index("c")


HBM = pl.BlockSpec(memory_space=pltpu.HBM)
SEM = pl.BlockSpec(memory_space=pltpu.SEMAPHORE)
DATAFLOW = pltpu.SideEffectType.DATAFLOW_SIDE_EFFECTING


def _split_copies(kind, srcs, lands, send_sems, recv_sems):
    nt = len(srcs)
    x, y, c = _place()
    copies = []
    if kind == "sibling":
        for q in range(4):
            for t in range(nt):
                k = q * nt + t
                copies.append(pltpu.make_async_remote_copy(
                    src_ref=srcs[t].at[2 * q + 1 - c], dst_ref=lands[t].at[q], send_sem=send_sems.at[k],
                    recv_sem=recv_sems.at[k], device_id=(x, y, 1 - c), device_id_type=MESH_ID))
    elif kind in ("rs_a", "rs_b"):
        half = lands[0].shape[1]
        xn, yn = (1 - x, y, c), (x, 1 - y, c)
        q_xn, q_yn, q_dg = 2 * (1 - x) + y, 2 * x + 1 - y, 2 * (1 - x) + 1 - y
        if kind == "rs_a":
            plan = [(srcs[0].at[q_yn].at[pl.ds(0, half)], 0, yn), (srcs[0].at[q_dg].at[pl.ds(0, half)], 1, yn),
                    (srcs[0].at[q_xn].at[pl.ds(half, half)], 2, xn), (srcs[0].at[q_dg].at[pl.ds(half, half)], 3, xn)]
        else:
            plan = [(srcs[0].at[0], 0, xn), (srcs[0].at[1], 1, yn)]
        for k, (src, slot, to) in enumerate(plan):
            copies.append(pltpu.make_async_remote_copy(
                src_ref=src, dst_ref=lands[0].at[slot], send_sem=send_sems.at[k], recv_sem=recv_sems.at[k],
                device_id=to, device_id_type=MESH_ID))
    elif kind == "gather":
        me = 4 * x + 2 * y + c
        for mask in range(1, 8):
            fx, fy, fc = (mask >> 2) & 1, (mask >> 1) & 1, mask & 1
            to = (1 - x if fx else x, 1 - y if fy else y, 1 - c if fc else c)
            for t in range(nt):
                k = (mask - 1) * nt + t
                copies.append(pltpu.make_async_remote_copy(
                    src_ref=srcs[t], dst_ref=lands[t].at[me], send_sem=send_sems.at[k], recv_sem=recv_sems.at[k],
                    device_id=to, device_id_type=MESH_ID))
    else:
        for s, (tx, ty) in enumerate([(1 - x, y), (x, 1 - y), (1 - x, 1 - y)]):
            for t in range(nt):
                k = s * nt + t
                copies.append(pltpu.make_async_remote_copy(
                    src_ref=srcs[t].at[2 * tx + ty], dst_ref=lands[t].at[s], send_sem=send_sems.at[k],
                    recv_sem=recv_sems.at[k], device_id=(tx, ty, c), device_id_type=MESH_ID))
    return copies


def _split_count(kind, nt):
    return {"gather": 7, "chips": 3, "sibling": 4, "rs_a": 4, "rs_b": 2}[kind] * nt


def _exchange_start(name, kind, srcs, land_shapes, after=None):
    nt = len(srcs)
    n = _split_count(kind, nt)
    dep_specs, dep_args = _dep_operand(after)
    nd = len(dep_args)

    def body(*refs):
        src_refs, land_refs = refs[:nt], refs[nt:2 * nt]
        send_sems, recv_sems = refs[2 * nt + nd], refs[2 * nt + nd + 1]
        token = refs[-1]
        for cp in _split_copies(kind, src_refs, land_refs, send_sems, recv_sems):
            cp.start()
        token[...] = jnp.zeros_like(token)

    lands = [pltpu.with_memory_space_constraint(lax.empty(s.shape, s.dtype), pltpu.HBM) for s in land_shapes]
    res = pl.pallas_call(
        body, name=name,
        out_shape=(pltpu.SemaphoreType.DMA((n,)), pltpu.SemaphoreType.DMA((n,)),
                   *[pltpu.HBM(s.shape, s.dtype) for s in srcs], *[pltpu.HBM(s.shape, s.dtype) for s in land_shapes],
                   jax.ShapeDtypeStruct((8, 128), F32)),
        in_specs=[HBM] * (2 * nt) + dep_specs,
        out_specs=(SEM, SEM, *([HBM] * (2 * nt)), pl.BlockSpec(memory_space=pltpu.VMEM)),
        input_output_aliases={i: 2 + i for i in range(2 * nt)},
        compiler_params=pltpu.CompilerParams(has_side_effects=DATAFLOW))(
            *[pltpu.with_memory_space_constraint(s, pltpu.HBM) for s in srcs], *lands, *dep_args)
    return res[0], res[1], list(res[2:2 + nt]), list(res[2 + nt:2 + 2 * nt]), res[-1]


def _exchange_wait(name, kind, send_sems, recv_sems, srcs, lands, after):
    nt = len(srcs)

    def body(*refs):
        src_refs, land_refs = refs[:nt], refs[nt:2 * nt]
        s_sems, r_sems = refs[2 * nt], refs[2 * nt + 1]
        for cp in _split_copies(kind, src_refs, land_refs, s_sems, r_sems):
            cp.wait_send()
            cp.wait_recv()

    res = pl.pallas_call(
        body, name=name,
        out_shape=tuple(pltpu.HBM(s.shape, s.dtype) for s in list(srcs) + list(lands)),
        in_specs=[HBM] * (2 * nt) + [SEM, SEM, pl.BlockSpec(memory_space=pl.ANY)],
        out_specs=tuple([HBM] * (2 * nt)),
        input_output_aliases={i: i for i in range(2 * nt)},
        compiler_params=pltpu.CompilerParams(has_side_effects=DATAFLOW))(
            *srcs, *lands, send_sems, recv_sems, after)
    return list(res[:nt]), list(res[nt:])


AG_GROUPS = ((0, 3), (3, 4), (7, 2))


def _ag_phase(name, own, land, sems, waits, starts, after=None):
    r = own.shape[0]
    half = r // 2
    ns = len(sems)
    dep_specs, dep_args = _dep_operand(after)
    nd = len(dep_args)
    new_group = None
    if starts:
        (new_group,) = [g for g, (first, n) in enumerate(AG_GROUPS) if first == starts[0]]
        assert list(starts) == list(range(AG_GROUPS[new_group][0], sum(AG_GROUPS[new_group])))

    def body(*refs):
        own_ref, land_ref = refs[0], refs[1]
        sem_refs = list(refs[2:2 + 2 * ns])
        outs = refs[2 + 2 * ns + nd:]
        if starts:
            sem_refs += [outs[0], outs[1]]
        x, y, c = _place()
        me, sib = (x, y, c), (x, y, 1 - c)
        xn, yn, dg = (1 - x, y, c), (x, 1 - y, c), (1 - x, 1 - y, c)

        def other(dev):
            return (dev[0], dev[1], 1 - dev[2])

        def rows(dev, part):
            blk = land_ref.at[4 * dev[0] + 2 * dev[1] + dev[2]]
            return blk if part is None else blk.at[pl.ds(part * half, half)]

        def sem_of(k):
            (g,) = [g for g, (first, n) in enumerate(AG_GROUPS) if first <= k < first + n]
            return sem_refs[2 * g].at[k - AG_GROUPS[g][0]], sem_refs[2 * g + 1].at[k - AG_GROUPS[g][0]]

        sent = {0: (me, None, sib), 1: (me, None, xn), 2: (me, None, yn), 3: (xn, 0, yn), 4: (yn, 1, xn),
                5: (xn, None, sib), 6: (yn, None, sib), 7: (dg, 0, sib), 8: (dg, 1, sib)}
        landed = {0: (sib, None), 1: (xn, None), 2: (yn, None), 3: (dg, 0), 4: (dg, 1), 5: (other(xn), None),
                  6: (other(yn), None), 7: (other(dg), 0), 8: (other(dg), 1)}

        def copy(k, receiving):
            send_sem, recv_sem = sem_of(k)
            dev, part, to = (*landed[k], me) if receiving else sent[k]
            src = own_ref if (dev is me and not receiving) else rows(dev, part)
            return pltpu.make_async_remote_copy(src_ref=src, dst_ref=rows(dev, part), send_sem=send_sem,
                                                recv_sem=recv_sem, device_id=to, device_id_type=MESH_ID)

        for kind, k in waits:
            if kind == "recv":
                copy(k, True).wait_recv()
            else:
                copy(k, False).wait_send()
        for k in starts:
            copy(k, False).start()
        if starts:
            outs[-1][...] = jnp.zeros_like(outs[-1])

    n_new = AG_GROUPS[new_group][1] if starts else 0
    sem_out = (pltpu.SemaphoreType.DMA((n_new,)), pltpu.SemaphoreType.DMA((n_new,))) if starts else ()
    token_out = (jax.ShapeDtypeStruct((8, 128), F32),) if starts else ()
    res = pl.pallas_call(
        body, name=name,
        out_shape=(*sem_out, pltpu.HBM(own.shape, own.dtype), pltpu.HBM(land.shape, land.dtype), *token_out),
        in_specs=[HBM, HBM] + [SEM] * (2 * ns) + dep_specs,
        out_specs=(*([SEM] * len(sem_out)), HBM, HBM, *([pl.BlockSpec(memory_space=pltpu.VMEM)] * len(token_out))),
        input_output_aliases={0: len(sem_out), 1: len(sem_out) + 1},
        compiler_params=pltpu.CompilerParams(has_side_effects=DATAFLOW))(
            own, land, *[a for pair in sems for a in pair], *dep_args)
    if starts:
        return (res[0], res[1]), res[2], res[3], res[4]
    return None, res[0], res[1], None


def _add_sibling(name, term, recv, rows):
    _, r, w = term.shape
    cidx = lax.axis_index("c").astype(jnp.int32).reshape(1)
    like_term = recv.shape[0] == N_DEV

    def body(c_ref, a_ref, b_ref, o_ref):
        o_ref[...] = (a_ref[...].astype(F32) + b_ref[...].astype(F32)).astype(o_ref.dtype)

    grid_spec = pltpu.PrefetchScalarGridSpec(
        num_scalar_prefetch=1, grid=(4, r // rows),
        in_specs=[pl.BlockSpec((None, rows, w), lambda q, i, c_ref: (2 * q + c_ref[0], i, 0)),
                  pl.BlockSpec((None, rows, w), lambda q, i, c_ref: (2 * q + c_ref[0] if like_term else q, i, 0))],
        out_specs=pl.BlockSpec((None, rows, w), lambda q, i, c_ref: (q, i, 0)))
    return pl.pallas_call(
        body, name=name, grid_spec=grid_spec, out_shape=jax.ShapeDtypeStruct((4, r, w), term.dtype),
        compiler_params=_params(("parallel", "parallel")))(cidx, term, recv)


def _add_sibling_small(name, terms, recvs):
    nt = len(terms)

    def body(*refs):
        c = lax.axis_index("c")
        for t_ref, r_ref, o_ref in zip(refs[:nt], refs[nt:2 * nt], refs[2 * nt:]):
            for q in range(4):
                o_ref[q] = (t_ref[2 * q + c].astype(F32) + r_ref[q].astype(F32)).astype(o_ref.dtype)

    return pl.pallas_call(
        body, name=name, out_shape=[jax.ShapeDtypeStruct((4,) + t.shape[1:], t.dtype) for t in terms],
        compiler_params=_params())(*terms, *recvs)


def _add_chips(name, sums, recv, rows):
    _, r, w = sums.shape
    qidx = (2 * lax.axis_index("x") + lax.axis_index("y")).astype(jnp.int32).reshape(1)

    def body(q_ref, a_ref, b_ref, o_ref):
        o_ref[...] = ((a_ref[...].astype(F32) + b_ref[0].astype(F32))
                      + (b_ref[1].astype(F32) + b_ref[2].astype(F32)))

    grid_spec = pltpu.PrefetchScalarGridSpec(
        num_scalar_prefetch=1, grid=(r // rows,),
        in_specs=[pl.BlockSpec((None, rows, w), lambda i, q_ref: (q_ref[0], i, 0)),
                  pl.BlockSpec((3, rows, w), lambda i, q_ref: (0, i, 0))],
        out_specs=pl.BlockSpec((rows, w), lambda i, q_ref: (i, 0)))
    return pl.pallas_call(
        body, name=name, grid_spec=grid_spec, out_shape=jax.ShapeDtypeStruct((r, w), F32),
        compiler_params=_params(("parallel",)))(qidx, sums, recv)


def _rs_rows(a):
    return SHARD_IN // 4 if a.shape[1] == SHARD_IN else a.shape[1]


def _reduce_scatter_start(tag, names, terms, recv1):
    if len(terms) == 1:
        sums = [_add_sibling("add_sibling_" + names[0], terms[0], recv1[0], _rs_rows(terms[0]))]
    else:
        sums = _add_sibling_small("add_sibling_" + tag, terms, recv1)
    lands =[jax.ShapeDtypeStruct((3,) + s.shape[1:], s.dtype) for s in sums]
    send_sems, recv_sems, sums, lands, token = _exchange_start("exchange_chips_start_" + tag, "chips", sums, lands)
    return (tag, names, send_sems, recv_sems, sums, lands), token


def _reduce_scatter_wait(state, after):
    tag, names, send_sems, recv_sems, sums, lands = state
    sums, recv2 = _exchange_wait("exchange_chips_wait_" + tag, "chips", send_sems, recv_sems, sums, lands, after)
    return names, sums, recv2


def _adamw(name, w, g, m, v, dep=None):
    dep_specs, dep_args = _dep_operand(dep)

    def body(w_ref, g_ref, m_ref, v_ref, *rest):
        d_ref, nm_ref, nv_ref = rest[-3:]
        d_ref[...], nm_ref[...], nv_ref[...] = _adam_math(w_ref[...], g_ref[...], m_ref[...], v_ref[...])

    whole = pl.BlockSpec(memory_space=pltpu.VMEM)
    return pl.pallas_call(
        body, name=name, in_specs=[whole] * 4 + dep_specs, out_shape=[jax.ShapeDtypeStruct(w.shape, F32)] * 3,
        compiler_params=_params())(w, g, m, v, *dep_args)


def _adam_math(w, g, m, v):
    nm = ADAM_B1 * m + (1.0 - ADAM_B1) * g
    nv = ADAM_B2 * v + (1.0 - ADAM_B2) * (g * g)
    c1 = 1.0 - ADAM_B1 ** ADAM_STEP
    c2 = 1.0 - ADAM_B2 ** ADAM_STEP
    return -ADAM_LR * ((nm / c1) / (jnp.sqrt(nv / c2) + ADAM_EPS) + ADAM_WD * w), nm, nv


def _adamw_chips(name, sums, recv, w, m, v, transposed, rows=None, dep=None):
    r, c = w.shape
    rows = r if rows is None else rows
    qidx = (2 * lax.axis_index("x") + lax.axis_index("y")).astype(jnp.int32).reshape(1)
    dep_specs, dep_args = _dep_operand(dep)

    def body(q_ref, a_ref, b_ref, w_ref, m_ref, v_ref, *rest):
        g_ref, d_ref, nm_ref, nv_ref = rest[-4:]
        g = (a_ref[...].astype(F32) + b_ref[0].astype(F32)) + (b_ref[1].astype(F32) + b_ref[2].astype(F32))
        if transposed:
            g = g.T
        g_ref[...] = g
        d_ref[...], nm_ref[...], nv_ref[...] = _adam_math(w_ref[...], g, m_ref[...], v_ref[...])

    row = pl.BlockSpec((rows, c), lambda i, q_ref: (i, 0))
    if transposed:
        term_specs = [pl.BlockSpec((None, c, rows), lambda i, q_ref: (q_ref[0], 0, i)),
                      pl.BlockSpec((3, c, rows), lambda i, q_ref: (0, 0, i))]
    else:
        term_specs = [pl.BlockSpec((None, rows, c), lambda i, q_ref: (q_ref[0], i, 0)),
                      pl.BlockSpec((3, rows, c), lambda i, q_ref: (0, i, 0))]
    grid_spec = pltpu.PrefetchScalarGridSpec(
        num_scalar_prefetch=1, grid=(r // rows,), in_specs=term_specs + [row, row, row] + dep_specs,
        out_specs=[row] * 4)
    return pl.pallas_call(
        body, name=name, grid_spec=grid_spec, out_shape=[jax.ShapeDtypeStruct((r, c), F32)] * 4,
        compiler_params=_params(("parallel",)))(qidx, sums, recv, w, m, v, *dep_args)


def _presum_halves(sums, landed):
    _, r, w = sums.shape
    rows = r // 4
    x, y = lax.axis_index("x"), lax.axis_index("y")
    dest = jnp.stack([2 * (1 - x) + y, 2 * x + 1 - y]).astype(jnp.int32)

    def body(q_ref, a_ref, b_ref, o_ref):
        o_ref[...] = (a_ref[...].astype(F32) + b_ref[...].astype(F32)).astype(o_ref.dtype)

    grid_spec = pltpu.PrefetchScalarGridSpec(
        num_scalar_prefetch=1, grid=(2, 2),
        in_specs=[pl.BlockSpec((None, rows, w), lambda h, i, q_ref: (q_ref[h], 2 * h + i, 0)),
                  pl.BlockSpec((None, rows, w), lambda h, i, q_ref: (1 + 2 * h, i, 0))],
        out_specs=pl.BlockSpec((None, rows, w), lambda h, i, q_ref: (h, i, 0)))
    return pl.pallas_call(
        body, name="presum_halves", grid_spec=grid_spec, out_shape=jax.ShapeDtypeStruct((2, r // 2, w), sums.dtype),
        compiler_params=_params(("parallel", "parallel")))(dest, sums, landed)


def _adamw_halves(name, sums, landed_a, landed_b, w, m, v, rows):
    r, c = w.shape
    half = c // 2
    qidx = (2 * lax.axis_index("x") + lax.axis_index("y")).astype(jnp.int32).reshape(1)

    def body(q_ref, s_ref, a_ref, b_ref, w_ref, m_ref, v_ref, g_ref, d_ref, nm_ref, nv_ref):
        first = (s_ref[:half, :].astype(F32) + a_ref[0].astype(F32)) + b_ref[0].astype(F32)
        second = (s_ref[half:, :].astype(F32) + a_ref[2].astype(F32)) + b_ref[1].astype(F32)
        g = jnp.concatenate([first, second], axis=0).T
        g_ref[...] = g
        d_ref[...], nm_ref[...], nv_ref[...] = _adam_math(w_ref[...], g, m_ref[...], v_ref[...])

    row = pl.BlockSpec((rows, c), lambda i, q_ref: (i, 0))
    grid_spec = pltpu.PrefetchScalarGridSpec(
        num_scalar_prefetch=1, grid=(r // rows,),
        in_specs=[pl.BlockSpec((None, c, rows), lambda i, q_ref: (q_ref[0], 0, i)),
                  pl.BlockSpec((4, half, rows), lambda i, q_ref: (0, 0, i)),
                  pl.BlockSpec((2, half, rows), lambda i, q_ref: (0, 0, i)), row, row, row],
        out_specs=[row] * 4)
    return pl.pallas_call(
        body, name=name, grid_spec=grid_spec, out_shape=[jax.ShapeDtypeStruct((r, c), F32)] * 4,
        compiler_params=_params(("parallel",)))(qidx, sums, landed_a, landed_b, w, m, v)


def _sum_devices(gathered):
    def body(g_ref, o_ref):
        acc = g_ref[0]
        for j in range(1, N_DEV):
            acc = acc + g_ref[j]
        o_ref[...] = acc

    return pl.pallas_call(
        body, name="sum_devices", out_shape=jax.ShapeDtypeStruct(gathered.shape[1:], F32),
        compiler_params=_params())(gathered)


def _rows128(a, rows):
    flat = a.reshape(-1)
    return jnp.pad(flat, (0, rows * 128 - flat.shape[0])).reshape(rows, 128)


def kernel(x, mem, pre_norm, w_in, merge_bias, na_rpb, mem_norm, w_mem_kv, w_branch_a, w_branch_b, w_branch_c, w_out, post_norm, loss_target, m_pre_norm, m_w_in, m_merge_bias, m_na_rpb, m_mem_norm, m_w_mem_kv, m_w_branch_a, m_w_branch_b, m_w_branch_c, m_w_out, m_post_norm, v_pre_norm, v_w_in, v_merge_bias, v_na_rpb, v_mem_norm, v_w_mem_kv, v_w_branch_a, v_w_branch_b, v_w_branch_c, v_w_out, v_post_norm):
    wt_in_s = w_in[0].T.astype(BF16)
    rows_s = jnp.concatenate([w_mem_kv[0], w_out[0]], axis=0).astype(BF16)
    cols_s = jnp.concatenate([w_branch_a[0].T, w_branch_b[0].T, w_branch_c[0].T], axis=0).astype(BF16)
    mb_s = jnp.pad(merge_bias[0], ((0, 5), (0, 0)))
    me = 4 * lax.axis_index("x") + 2 * lax.axis_index("y") + lax.axis_index("c")

    chip = 2 * lax.axis_index("x") + lax.axis_index("y")

    def first_block(q):
        return jnp.where(q == 0, 0, jnp.where(q == 1, 6, jnp.where(q == 2, 11, 17)))

    five = jnp.arange(5, dtype=jnp.int32)
    near, far = jnp.where(chip < 2, 5, 16), jnp.where(chip < 2, 16, 5)
    order1 = (first_block(chip) + five).astype(jnp.int32)
    order2 = jnp.concatenate([first_block(chip ^ 1) + five, near[None], first_block(chip ^ 2) + five]).astype(jnp.int32)
    order3 = jnp.concatenate([first_block(chip ^ 3) + five, far[None]]).astype(jnp.int32)
    tabs = _rope_tables()

    def weights_of(land):
        return land.reshape(N_IN, D_MODEL)

    land = pltpu.with_memory_space_constraint(lax.empty((N_DEV,) + wt_in_s.shape, BF16), pltpu.HBM)
    own = pltpu.with_memory_space_constraint(wt_in_s, pltpu.HBM)
    sem_a, own, land, token = _ag_phase("ag_start", own, land, [], [], [0, 1, 2])
    hs, hst = _prenorm_fold(x[0], pre_norm, token)
    _, own, land, _ = _ag_phase("ag_wait0", own, land, [sem_a], [("recv", 0)], [], hs)
    land = lax.dynamic_update_slice(land, own[None], (me, 0, 0))
    parts = _in_proj("in_proj_1", hs, weights_of(land), tabs, order1)
    bias = _na_bias(jnp.pad(na_rpb[0], ((0, 0), (0, 1), (0, 128 - 31))), parts)
    sem_b, own, land, _ = _ag_phase("ag_mid1", own, land, [sem_a], [("recv", 1), ("recv", 2)], [3, 4, 5, 6], bias)
    _, own, land, _ = _ag_phase("ag_wait1", own, land, [sem_a, sem_b], [("recv", 5), ("recv", 6)], [])
    parts = _in_proj("in_proj_2", hs, weights_of(land), tabs, order2, parts)
    sem_c, own, land, _ = _ag_phase("ag_mid2", own, land, [sem_a, sem_b], [("recv", 3), ("recv", 4)], [7, 8], parts)
    _, own, land, _ = _ag_phase("ag_end", own, land, [sem_a, sem_b, sem_c],
                                [("recv", 7), ("recv", 8)] + [("send", k) for k in range(9)], [])
    wt_in = weights_of(land)

    late_own = [rows_s, cols_s, mb_s]
    late_lands = [jax.ShapeDtypeStruct((N_DEV,) + s.shape, s.dtype) for s in late_own]
    l_send, l_recv, late_own, late_lands, late_token = _exchange_start("gather_late_start", "gather", late_own,
                                                                       late_lands, after=wt_in)
    parts = _in_proj("in_proj_3", hs, wt_in, tabs, order3, parts, late_token)

    def late_weights(after):
        own, lands = _exchange_wait("gather_late_wait", "gather", l_send, l_recv, late_own, late_lands, after)
        g_rows, g_cols, g_mb = [lax.dynamic_update_slice(land, o[None], (me, 0, 0)) for land, o in zip(lands, own)]
        return (g_mb[:, :3].transpose(1, 0, 2).reshape(3, D_MODEL),
                g_rows[:, :128].reshape(D_MODEL, D_MODEL), g_cols[:, 0:128].reshape(D_MODEL, 512),
                g_cols[:, 128:256].reshape(D_MODEL, 512), g_cols[:, 256:384].reshape(D_MODEL, 512),
                g_rows[:, 128:].reshape(D_MODEL, D_MODEL))

    rs_state = []
    rest_names = ["w_kv", "w_out", "a", "b", "c", "mb"]
    rest_sibling, w_in_a, w_in_b = [], [], []

    def reduce_start(phase, grads, after=None):
        if phase == "rest_sibling":
            gmb_t = jnp.pad(grads["merge_bias"].reshape(3, N_DEV, 128).transpose(1, 0, 2), ((0, 0), (0, 5), (0, 0)))
            terms = [grads["w_kv"].reshape(N_DEV, 128, D_MODEL), grads["w_out"].reshape(N_DEV, 128, D_MODEL),
                     grads["wt_a"].reshape(N_DEV, 128, 512), grads["wt_b"].reshape(N_DEV, 128, 512),
                     grads["wt_c"].reshape(N_DEV, 128, 512), gmb_t]
            lands = [jax.ShapeDtypeStruct((4,) + t.shape[1:], t.dtype) for t in terms]
            rest_sibling.extend(_exchange_start("exchange_sibling_start_rest", "sibling", terms, lands)[:4])
            return None
        if phase == "rest_chips":
            s_send, s_recv, terms, lands = rest_sibling
            terms, recv1 = _exchange_wait("exchange_sibling_wait_rest", "sibling", s_send, s_recv, terms, lands, after)
            state, token = _reduce_scatter_start("rest", rest_names, terms, recv1)
        elif phase == "w_in":
            own, sibling = [a.reshape(N_DEV, SHARD_IN, D_MODEL) for a in grads["wt_in"]]
            sums = _add_sibling("add_sibling_w_in", own, sibling, SHARD_IN // 4)
            lands = [jax.ShapeDtypeStruct((4, SHARD_IN // 2, D_MODEL), BF16)]
            w_in_a.extend(_exchange_start("rs_a_start", "rs_a", [sums], lands))
            return w_in_a[4]
        else:
            (sums,), (landed_a,) = _exchange_wait("rs_a_wait", "rs_a", w_in_a[0], w_in_a[1], w_in_a[2], w_in_a[3], after)
            lands = [jax.ShapeDtypeStruct((2, SHARD_IN // 2, D_MODEL), BF16)]
            w_in_b.extend(_exchange_start("rs_b_start", "rs_b", [_presum_halves(sums, landed_a)], lands))
            w_in_b.extend([sums, landed_a])
            return w_in_b[4]
        rs_state.append(state)
        return token

    loss_term, grad_x, grads = _local_step(
        x[0], hst, parts, tabs, bias, mem[0], loss_target[0], pre_norm, mem_norm, post_norm, wt_in, late_weights,
        reduce_start=reduce_start)

    small = jnp.concatenate([_rows128(grads["pre_norm"], 8), _rows128(grads["mem_norm"], 8),
                             _rows128(grads["post_norm"], 8), _rows128(grads["na_rpb"], 32),
                             _rows128(loss_term, 8)], axis=0)
    s_send, s_recv, s_own, s_land, s_token = _exchange_start(
        "gather_small_start", "gather", [small], [jax.ShapeDtypeStruct((N_DEV,) + small.shape, F32)])
    grad = {}
    weights = {
        "pre_norm": (pre_norm, m_pre_norm, v_pre_norm), "w_in": (w_in, m_w_in, v_w_in),
        "merge_bias": (merge_bias, m_merge_bias, v_merge_bias), "na_rpb": (na_rpb, m_na_rpb, v_na_rpb),
        "mem_norm": (mem_norm, m_mem_norm, v_mem_norm), "w_mem_kv": (w_mem_kv, m_w_mem_kv, v_w_mem_kv),
        "w_branch_a": (w_branch_a, m_w_branch_a, v_w_branch_a), "w_branch_b": (w_branch_b, m_w_branch_b, v_w_branch_b),
        "w_branch_c": (w_branch_c, m_w_branch_c, v_w_branch_c), "w_out": (w_out, m_w_out, v_w_out),
        "post_norm": (post_norm, m_post_norm, v_post_norm)}
    order = ["pre_norm", "w_in", "merge_bias", "na_rpb", "mem_norm", "w_mem_kv", "w_branch_a", "w_branch_b",
             "w_branch_c", "w_out", "post_norm"]
    delta, new_m, new_v = {}, {}, {}

    def update(n, dep=None):
        w, m, v = weights[n]
        shape = w.shape
        two_d = (-1, shape[-1])
        dl, nm, nv = _adamw("adamw_" + n, w.reshape(two_d), grad[n].reshape(two_d), m.reshape(two_d),
                            v.reshape(two_d), dep)
        delta[n], new_m[n], new_v[n] = dl.reshape(shape), nm.reshape(shape), nv.reshape(shape)
        return dl

    def update_sharded(n, sums, recv, transposed, rows=None, dep=None):
        w, m, v = weights[n]
        g, dl, nm, nv = _adamw_chips("adamw_" + n, sums, recv, w[0], m[0], v[0], transposed, rows, dep)
        grad[n], delta[n], new_m[n], new_v[n] = g[None], dl[None], nm[None], nv[None]
        return dl

    _, sums, recv2 = _reduce_scatter_wait(rs_state[0], s_token)
    dep = None
    for i, (n, transposed) in enumerate((("w_mem_kv", False), ("w_out", False), ("w_branch_a", True),
                                         ("w_branch_b", True), ("w_branch_c", True))):
        dep = update_sharded(n, sums[i], recv2[i], transposed, dep=dep)
    grad["merge_bias"] = _add_chips("add_chips_mb", sums[5], recv2[5], 8)[:3][None]
    update("merge_bias")
    s_own, s_land = _exchange_wait("gather_small_wait", "gather", s_send, s_recv, s_own, s_land, dep)
    total = _sum_devices(lax.dynamic_update_slice(s_land[0], s_own[0][None], (me, 0, 0)))
    loss = total[56, 0]
    grad.update({"pre_norm": total[0:8].reshape(1, D_MODEL), "mem_norm": total[8:16].reshape(1, D_MODEL),
                 "post_norm": total[16:24].reshape(1, D_MODEL),
                 "na_rpb": total[24:56].reshape(-1)[:8 * 15 * 31].reshape(1, 8, 15, 31)})
    dep = None
    for n in ("pre_norm", "na_rpb", "mem_norm", "post_norm"):
        dep = update(n, dep)
    _, (landed_b,) = _exchange_wait("rs_b_wait", "rs_b", w_in_b[0], w_in_b[1], w_in_b[2], w_in_b[3], dep)
    g, dl, nm, nv = _adamw_halves("adamw_w_in", w_in_b[5], w_in_b[6], landed_b, w_in[0], m_w_in[0], v_w_in[0], 256)
    grad["w_in"], delta["w_in"], new_m["w_in"], new_v["w_in"] = g[None], dl[None], nm[None], nv[None]

    return (loss, grad_x[None], *[grad[n] for n in order], *[delta[n] for n in order],
            *[new_m[n] for n in order], *[new_v[n] for n in order])
```

```python
import functools

import numpy as np
import jax
import jax.numpy as jnp
from jax import lax
from jax.experimental import pallas as pl
from jax.experimental.pallas import tpu as pltpu

F32 = jnp.float32
BF16 = jnp.bfloat16

SEQ = 2048
D_MODEL = 1024
N_IN = 11264
N_DEV = 8
SHARD_IN = N_IN // N_DEV
HEAD_DIM = 64
GRID_W = 64
NA_ROWS = 8
MEM_LEN = 256
DILATIONS = (1, 4, 16)
REACH = 64
ROPE_THETA = 500000.0
ROPE_DIM = 16
EPS = 1e-6
NEG = -1e30
ADAM_LR = 0.001
ADAM_B1 = 0.9
ADAM_B2 = 0.999
ADAM_EPS = 1e-08
ADAM_WD = 0.01
ADAM_STEP = 10

VMEM_LIMIT_BYTES = 56 * 1024 * 1024
MESH_ID = pl.DeviceIdType.MESH

NN = (((1,), (0,)), ((), ()))
NT = (((1,), (1,)), ((), ()))
TN = (((0,), (0,)), ((), ()))


def _params(sem=None):
    return pltpu.CompilerParams(dimension_semantics=sem, vmem_limit_bytes=VMEM_LIMIT_BYTES)


def _iota(shape, dim):
    return lax.broadcasted_iota(jnp.int32, shape, dim)


def _sigmoid(x):
    return 1.0 / (1.0 + jnp.exp(-x))


def _rope_tables():
    half = ROPE_DIM // 2
    inv = (ROPE_THETA ** (-np.arange(half, dtype=np.float64) * 2.0 / ROPE_DIM)).astype(np.float32)
    pos = np.arange(SEQ, dtype=np.float32)
    ang = pos[:, None] * inv[None, :]
    cos, sin = np.cos(ang), np.sin(ang)
    zeros = np.zeros_like(cos)
    rest = HEAD_DIM - ROPE_DIM
    c64 = np.concatenate([cos, cos, np.ones((SEQ, rest), np.float32)], axis=1)
    s1 = np.concatenate([zeros, sin, np.zeros((SEQ, rest), np.float32)], axis=1)
    s2 = np.concatenate([-sin, zeros, np.zeros((SEQ, rest), np.float32)], axis=1)

    def fold(t, d):
        return t.reshape(SEQ // d, d, t.shape[1]).transpose(1, 0, 2).reshape(SEQ, t.shape[1])

    tabs = [np.stack([np.tile(fold(t, d), (1, 2)) for t in (c64, s1, s2)], axis=0) for d in DILATIONS]
    return jnp.asarray(np.stack(tabs, axis=0), dtype=F32)


def _rope(a, c, s1, s2):
    return a * c + pltpu.roll(a, 8, 1) * s1 + pltpu.roll(a, 120, 1) * s2


def _rope_t(a, c, s1, s2):
    return a * c + pltpu.roll(a * s1, 120, 1) + pltpu.roll(a * s2, 8, 1)


def _perm_of_block(j):
    return jnp.where(j < 3, 0, jnp.where(j < 6, 1, jnp.where(j < 9, 2, 0)))


def _mm(name, a, b, out_shape, out_dtype, grid, a_spec, b_spec, o_spec, acc_shape, dims, k_axis, nk):
    def body(a_ref, b_ref, o_ref, acc_ref):
        k = pl.program_id(k_axis)

        @pl.when(k == 0)
        def _():
            acc_ref[...] = jnp.zeros(acc_shape, F32)

        acc_ref[...] += lax.dot_general(a_ref[...], b_ref[...], dims, preferred_element_type=F32)

        @pl.when(k == nk - 1)
        def _():
            o_ref[...] = acc_ref[...].astype(out_dtype)

    sem = tuple("arbitrary" if ax == k_axis else "parallel" for ax in range(len(grid)))
    return pl.pallas_call(
        body, name=name, grid=grid, in_specs=[a_spec, b_spec], out_specs=o_spec,
        out_shape=jax.ShapeDtypeStruct(out_shape, out_dtype),
        scratch_shapes=[pltpu.VMEM(acc_shape, F32)], compiler_params=_params(sem))(a, b)


def _mm_simple(name, a, b, dims, out_dtype, tm, tn, tk):
    if dims is NN:
        m, kk = a.shape
        n = b.shape[1]
        a_spec = pl.BlockSpec((tm, tk), lambda i, j, k: (i, k))
        b_spec = pl.BlockSpec((tk, tn), lambda i, j, k: (k, j))
    elif dims is NT:
        m, kk = a.shape
        n = b.shape[0]
        a_spec = pl.BlockSpec((tm, tk), lambda i, j, k: (i, k))
        b_spec = pl.BlockSpec((tn, tk), lambda i, j, k: (j, k))
    else:
        kk, m = a.shape
        n = b.shape[1]
        a_spec = pl.BlockSpec((tk, tm), lambda i, j, k: (k, i))
        b_spec = pl.BlockSpec((tk, tn), lambda i, j, k: (k, j))
    grid = (m // tm, n // tn, kk // tk)
    o_spec = pl.BlockSpec((tm, tn), lambda i, j, k: (i, j))
    return _mm(name, a, b, (m, n), out_dtype, grid, a_spec, b_spec, o_spec, (tm, tn), dims, 2, kk // tk)


def _rmsnorm_fwd(name, x, gain, rows):
    n, d = x.shape

    def body(x_ref, g_ref, o_ref):
        xv = x_ref[...]
        rstd = lax.rsqrt(jnp.mean(xv * xv, axis=1, keepdims=True) + EPS)
        o_ref[...] = (xv * rstd * g_ref[...]).astype(BF16)

    return pl.pallas_call(
        body, name=name, grid=(n // rows,),
        in_specs=[pl.BlockSpec((rows, d), lambda i: (i, 0)), pl.BlockSpec((1, d), lambda i: (0, 0))],
        out_specs=pl.BlockSpec((rows, d), lambda i: (i, 0)),
        out_shape=jax.ShapeDtypeStruct((n, d), BF16), compiler_params=_params(("parallel",)))(x, gain)


def _folded_rows(first, rows, d):
    if d == 1:
        return pl.ds(pl.multiple_of(first, rows), rows)
    mlen = SEQ // d
    return pl.ds((first % mlen) * d + first // mlen, rows, stride=d)


def _prenorm_fold(x, gain, dep=None):
    rows = 128
    nchunk = D_MODEL // 128
    dep_specs, dep_args = _dep_operand(dep)

    def body(*refs):
        x_refs, g_ref, hs_ref, hst_ref = refs[:nchunk], refs[nchunk], refs[-2], refs[-1]
        first = pl.program_id(0) * rows
        for p, d in enumerate(DILATIONS):
            idx = _folded_rows(first, rows, d)
            xv = jnp.concatenate([r[idx, :] for r in x_refs], axis=1)
            rstd = lax.rsqrt(jnp.mean(xv * xv, axis=1, keepdims=True) + EPS)
            h = xv * rstd * g_ref[...]
            hs_ref[p] = h.astype(BF16)
            hst_ref[p] = h.T.astype(BF16)

    x_specs = [pl.BlockSpec((SEQ, 128), functools.partial(lambda c, i: (0, c), c)) for c in range(nchunk)]
    return pl.pallas_call(
        body, name="prenorm", grid=(SEQ // rows,),
        in_specs=x_specs + [pl.BlockSpec((1, D_MODEL), lambda i: (0, 0))] + dep_specs,
        out_specs=[pl.BlockSpec((3, rows, D_MODEL), lambda i: (0, i, 0)),
                   pl.BlockSpec((3, D_MODEL, rows), lambda i: (0, 0, i))],
        out_shape=[jax.ShapeDtypeStruct((3, SEQ, D_MODEL), BF16), jax.ShapeDtypeStruct((3, D_MODEL, SEQ), BF16)],
        compiler_params=_params(("parallel",)))(*([x] * nchunk), gain, *dep_args)


def _prenorm_bwd(x, gain, dh, dout):
    rows = 256

    def body(x_ref, g_ref, a_ref, do_ref, dx_ref, gg_ref):
        xv = x_ref[...]
        rstd = lax.rsqrt(jnp.mean(xv * xv, axis=1, keepdims=True) + EPS)
        xn = xv * rstd
        dh = jnp.concatenate([a_ref[c] for c in range(D_MODEL // 128)], axis=1)
        gdh = dh * g_ref[...]
        dx_ref[...] = rstd * (gdh - xn * jnp.mean(gdh * xn, axis=1, keepdims=True)) + do_ref[...]

        @pl.when(pl.program_id(0) == 0)
        def _():
            gg_ref[...] = jnp.zeros((1, D_MODEL), F32)

        gg_ref[...] += jnp.sum(dh * xn, axis=0, keepdims=True)

    row = pl.BlockSpec((rows, D_MODEL), lambda i: (i, 0))
    vec = pl.BlockSpec((1, D_MODEL), lambda i: (0, 0))
    return pl.pallas_call(
        body, name="prenorm_bwd", grid=(SEQ // rows,),
        in_specs=[row, vec, pl.BlockSpec((D_MODEL // 128, rows, 128), lambda i: (0, i, 0)), row], out_specs=[row, vec],
        out_shape=[jax.ShapeDtypeStruct((SEQ, D_MODEL), F32), jax.ShapeDtypeStruct((1, D_MODEL), F32)],
        compiler_params=_params(("arbitrary",)))(x, gain, dh, dout)


def _memnorm_bwd(mem, dmemn, dep=None):
    dep_specs, dep_args = _dep_operand(dep)

    def body(m_ref, d_ref, *rest):
        mv = m_ref[...]
        rstd = lax.rsqrt(jnp.mean(mv * mv, axis=1, keepdims=True) + EPS)
        rest[-1][...] = jnp.sum(d_ref[...] * mv * rstd, axis=0, keepdims=True)

    whole = pl.BlockSpec(memory_space=pltpu.VMEM)
    return pl.pallas_call(
        body, name="memnorm_bwd", in_specs=[whole, whole] + dep_specs,
        out_shape=jax.ShapeDtypeStruct((1, D_MODEL), F32), compiler_params=_params())(mem, dmemn, *dep_args)


def _dep_operand(dep):
    return ([], []) if dep is None else ([pl.BlockSpec(memory_space=pl.ANY)], [dep])


def _in_proj(name, hs, wt, tabs, order, prev=None, dep=None):
    tm, tn = 512, 512
    prev_specs, prev_args = ([], []) if prev is None else ([ANY], [prev])
    dep_specs, dep_args = _dep_operand(dep)

    def body(order_ref, h_ref, w_ref, t_ref, *rest):
        o_ref = rest[-1]
        j = order_ref[pl.program_id(0)]
        is_rope = jnp.logical_and(j < 9, j % 3 != 2)
        row_slices = [slice(r * tm, (r + 1) * tm) for r in range(SEQ // tm)]

        def product(rs):
            return lax.dot_general(h_ref[rs, :], w_ref[...], NT, preferred_element_type=F32)

        @pl.when(is_rope)
        def _():
            for rs in row_slices:
                acc = product(rs)
                c, s1, s2 = t_ref[0, rs, :], t_ref[1, rs, :], t_ref[2, rs, :]
                for q in range(tn // 128):
                    a = acc[:, q * 128:(q + 1) * 128]
                    o_ref[rs, q * 128:(q + 1) * 128] = _rope(a, c, s1, s2).astype(BF16)

        @pl.when(jnp.logical_not(is_rope))
        def _():
            for rs in row_slices:
                o_ref[rs, :] = product(rs).astype(BF16)

    grid_spec = pltpu.PrefetchScalarGridSpec(
        num_scalar_prefetch=1, grid=(order.shape[0],),
        in_specs=[pl.BlockSpec((None, SEQ, D_MODEL), lambda t, o: (_perm_of_block(o[t]), 0, 0)),
                  pl.BlockSpec((tn, D_MODEL), lambda t, o: (o[t], 0)),
                  pl.BlockSpec((None, 3, SEQ, 128), lambda t, o: (_perm_of_block(o[t]), 0, 0, 0))] + prev_specs
        + dep_specs,
        out_specs=pl.BlockSpec((SEQ, tn), lambda t, o: (0, o[t])))
    return pl.pallas_call(
        body, name=name, grid_spec=grid_spec, out_shape=jax.ShapeDtypeStruct((SEQ, N_IN), BF16),
        input_output_aliases={} if prev is None else {4: 0},
        compiler_params=_params(("arbitrary",)))(order, hs, wt, tabs, *prev_args, *dep_args)


def _piece_blocks(pieces):
    return [(a, h * 512) for a, p in enumerate(pieces) for h in range(p.shape[1] // 512)]


def _block_fetch(piece_refs, blocks, buf, sem):
    def start(block, slot):
        for b, (a, col) in enumerate(blocks):
            @pl.when(block == b)
            def _():
                pltpu.make_async_copy(piece_refs[a].at[:, pl.ds(col, 512)], buf.at[slot], sem.at[slot]).start()

    def wait(slot):
        pltpu.make_async_copy(piece_refs[0].at[:, pl.ds(0, 512)], buf.at[slot], sem.at[slot]).wait()

    return start, wait


def _in_proj_dw(pieces, hst, dep=None):
    tn = 512
    blocks = _piece_blocks(pieces)
    nblk = len(blocks)
    npc = len(pieces)
    dep_specs, dep_args = _dep_operand(dep)

    def body(h_ref, *rest):
        piece_refs = rest[:npc]
        own_out, mirror, buf, sem, out_buf, send_sems, recv_sem, local_sems = rest[-8:]
        j = pl.program_id(0)
        slot = j % 2
        start, wait = _block_fetch(piece_refs, blocks, buf, sem)
        x, y, c = _place()

        def rows_of(step):
            return pl.ds(pl.multiple_of(step * tn, tn), tn)

        def to_sibling(step, slot_):
            return pltpu.make_async_remote_copy(
                src_ref=out_buf.at[slot_], dst_ref=mirror.at[rows_of(step)],
                send_sem=send_sems.at[slot_], recv_sem=recv_sem, device_id=(x, y, 1 - c), device_id_type=MESH_ID)

        def to_own(step, slot_):
            return pltpu.make_async_copy(out_buf.at[slot_], own_out.at[rows_of(step)], local_sems.at[slot_])

        @pl.when(j == 0)
        def _():
            start(j, slot)

        wait(slot)

        @pl.when(j + 1 < nblk)
        def _():
            start(j + 1, 1 - slot)

        acc = jnp.dot(h_ref[...], buf[slot], preferred_element_type=F32)

        @pl.when(j >= 2)
        def _():
            to_sibling(j - 2, slot).wait_send()
            to_own(j - 2, slot).wait()

        out_buf[slot] = acc.T.astype(BF16)
        to_sibling(j, slot).start()
        to_own(j, slot).start()

        @pl.when(j == nblk - 1)
        def _():
            to_sibling(j - 1, 1 - slot).wait_send()
            to_own(j - 1, 1 - slot).wait()
            to_sibling(j, slot).wait_send()
            to_own(j, slot).wait()
            pltpu.make_async_remote_copy(src_ref=mirror, dst_ref=mirror, send_sem=send_sems.at[0], recv_sem=recv_sem,
                                         device_id=(x, y, 1 - c), device_id_type=MESH_ID).wait_recv()

    return pl.pallas_call(
        body, name="in_proj_dw", grid=(nblk,),
        in_specs=[pl.BlockSpec((None, D_MODEL, SEQ), lambda j: (_perm_of_block(j), 0, 0))] + [ANY] * npc + dep_specs,
        out_specs=[ANY, ANY],
        out_shape=[jax.ShapeDtypeStruct((N_IN, D_MODEL), BF16), jax.ShapeDtypeStruct((N_IN, D_MODEL), BF16)],
        scratch_shapes=[pltpu.VMEM((2, SEQ, tn), BF16), pltpu.SemaphoreType.DMA((2,)),
                        pltpu.VMEM((2, tn, D_MODEL), BF16), pltpu.SemaphoreType.DMA((2,)), pltpu.SemaphoreType.DMA,
                        pltpu.SemaphoreType.DMA((2,))],
        compiler_params=_params(("arbitrary",)))(hst, *pieces, *dep_args)


def _in_proj_dh(pieces, wt, dep=None):
    tk = 512
    blocks = _piece_blocks(pieces)
    nblk = len(blocks)
    npc = len(pieces)
    nchunk = D_MODEL // 128

    def col(s):
        return jnp.where(s < 3, s, jnp.where(s < 16, s + 6, s - 13))

    dep_specs, dep_args = _dep_operand(dep)

    def body(w_ref, *rest):
        piece_refs = rest[:npc]
        o_ref, acc_ref, buf, sem = rest[-4:]
        s = pl.program_id(0)
        slot = s % 2
        start, wait = _block_fetch(piece_refs, blocks, buf, sem)

        @pl.when(s == 0)
        def _():
            start(col(s), slot)

        wait(slot)

        @pl.when(s + 1 < nblk)
        def _():
            start(col(s + 1), 1 - slot)

        row_slices = [slice(r * 512, (r + 1) * 512) for r in range(SEQ // 512)]

        def product(rs):
            return jnp.dot(buf[slot, rs, :], w_ref[...], preferred_element_type=F32)

        def accumulate(cond, to_out, init):
            @pl.when(cond)
            def _():
                for rs in row_slices:
                    prod = product(rs)
                    if not to_out:
                        if init:
                            acc_ref[rs, :] = prod
                        else:
                            acc_ref[rs, :] += prod
                        continue
                    for c in range(nchunk):
                        if init:
                            o_ref[c, rs, :] = prod[:, c * 128:(c + 1) * 128]
                        else:
                            o_ref[c, rs, :] += prod[:, c * 128:(c + 1) * 128]

        accumulate(s == 0, True, True)
        accumulate(jnp.logical_and(s > 0, s < 16), True, False)
        accumulate(jnp.logical_or(s == 16, s == 19), False, True)
        accumulate(jnp.logical_and(s > 16, s != 19), False, False)
        for last, d in ((18, 4), (21, 16)):
            @pl.when(s == last)
            def _():
                mlen = SEQ // d
                for r in range(d):
                    for c in range(nchunk):
                        o_ref[c, pl.ds(r, mlen, stride=d), :] += acc_ref[r * mlen:(r + 1) * mlen,
                                                                         c * 128:(c + 1) * 128]

    return pl.pallas_call(
        body, name="in_proj_dh", grid=(nblk,),
        in_specs=[pl.BlockSpec((tk, D_MODEL), lambda s: (col(s), 0))] + [ANY] * npc + dep_specs,
        out_specs=pl.BlockSpec((nchunk, SEQ, 128), lambda s: (0, 0, 0)),
        out_shape=jax.ShapeDtypeStruct((nchunk, SEQ, 128), F32),
        scratch_shapes=[pltpu.VMEM((SEQ, D_MODEL), F32), pltpu.VMEM((2, SEQ, tk), BF16),
                        pltpu.SemaphoreType.DMA((2,))],
        compiler_params=_params(("arbitrary",)))(wt, *pieces, *dep_args)


def _head_lanes(lanes, hh):
    return lanes >= 64 if hh == 1 else lanes < 64


def _head_rows(x, lanes, hh, pair):
    if not pair:
        return jnp.max(x, axis=1, keepdims=True)
    return jnp.max(jnp.where(_head_lanes(lanes, hh), x, -jnp.inf), axis=1, keepdims=True)


def _mask_head(x, lanes, hh, pair, scale=1.0):
    if not pair:
        return x
    xf = x.astype(F32) if scale == 1.0 else x.astype(F32) * scale
    return jnp.where(_head_lanes(lanes, hh), xf, 0.0).astype(BF16)


def _window(mode, qi, tq, mlen, tk):
    if mode == "dil":
        q0 = qi * tq
        seg = (q0 // mlen) * mlen
        ks = jnp.clip(q0 - REACH, seg, seg + mlen - tk)
        return pl.multiple_of(ks, 64)
    if mode == "na":
        r_start = jnp.clip(qi - NA_ROWS // 2, 0, SEQ // GRID_W - NA_ROWS)
        return pl.multiple_of(r_start * GRID_W, 64)
    return 0


def _band_mask(qi, tq, tk, ks):
    qpos = qi * tq + _iota((tq, tk), 0)
    kpos = ks + _iota((tq, tk), 1)
    return jnp.where(jnp.abs(qpos - kpos) <= REACH, 0.0, NEG).astype(F32)


def _stack_heads(x, lanes, pair, scale=1.0):
    if not pair:
        return x
    return jnp.concatenate([_mask_head(x, lanes, hh, pair, scale) for hh in range(2)], axis=0)


def _stack_rows(x, lanes, pair):
    if not pair:
        return _head_rows(x, lanes, 0, pair)
    return jnp.concatenate([_head_rows(x, lanes, hh, pair) for hh in range(2)], axis=0)


def _unstack_heads(x, lanes, pair, tq):
    if not pair:
        return x
    return jnp.where(lanes < 64, x[:tq], x[tq:])


def _scores(mode, qst, k, sscale, band, qi, bias_ref, pair):
    s = lax.dot_general(qst, k, NT, preferred_element_type=F32)
    if sscale != 1.0:
        s = s * sscale
    if mode == "dil":
        s = s + jnp.concatenate([band, band], axis=0)
    elif mode == "na":
        off = qi - jnp.clip(qi - NA_ROWS // 2, 0, SEQ // GRID_W - NA_ROWS)
        s = s + jnp.concatenate([bias_ref[0, off], bias_ref[1, off]], axis=0)
    return s


def _attn_cfg(mode, d):
    if mode == "dil":
        mlen = SEQ // d
        return dict(pair=True, tq=128, tk=min(256, mlen), mlen=mlen, lk=SEQ, scale=HEAD_DIM ** -0.5, units=4,
                    nsub=ATTN_SUBTILES)
    if mode == "na":
        return dict(pair=True, tq=GRID_W, tk=NA_ROWS * GRID_W, mlen=SEQ, lk=SEQ, scale=HEAD_DIM ** -0.5, units=4,
                    nsub=ATTN_SUBTILES)
    return dict(pair=False, tq=128, tk=MEM_LEN, mlen=SEQ, lk=MEM_LEN, scale=128 ** -0.5, units=4,
                nsub=ATTN_SUBTILES)


ATTN_SUBTILES = 16


def _attn_fwd(name, mode, q_arr, k_arr, v_arr, qcol, kcol, vcol, d=1, bias=None):
    cfg = _attn_cfg(mode, d)
    pair, tq, tk, mlen, lk, scale = cfg["pair"], cfg["tq"], cfg["tk"], cfg["mlen"], cfg["lk"], cfg["scale"]
    qscale, sscale = (scale, 1.0) if pair else (1.0, scale)
    nsub = cfg["nsub"]
    rows = nsub * tq

    def body(*refs):
        if mode == "na":
            q_ref, k_ref, v_ref, bias_ref, o_ref, l_ref = refs
        else:
            q_ref, k_ref, v_ref, o_ref, l_ref = refs
            bias_ref = None
        lanes = _iota((tq, 128), 1)
        qis = [pl.program_id(1) * nsub + sub for sub in range(nsub)]
        kss = [_window(mode, qi, tq, mlen, tk) for qi in qis]
        vs = [v_ref[pl.ds(ks, tk), :] for ks in kss]
        bands = [_band_mask(qi, tq, tk, ks) if mode == "dil" else None for qi, ks in zip(qis, kss)]
        ss = []
        for sub in range(nsub):
            qst = _stack_heads(q_ref[sub * tq:(sub + 1) * tq, :], lanes, pair, qscale)
            k = k_ref[pl.ds(kss[sub], tk), :]
            ss.append(_scores(mode, qst, k, sscale, bands[sub], qis[sub], bias_ref, pair))
        ms = [jnp.max(s_, axis=1, keepdims=True) for s_ in ss]
        ps = [jnp.exp(s_ - m) for s_, m in zip(ss, ms)]
        ls = [jnp.sum(p, axis=1, keepdims=True) for p in ps]
        os_ = [jnp.dot(p.astype(BF16), v, preferred_element_type=F32) for p, v in zip(ps, vs)]
        for sub in range(nsub):
            out = _unstack_heads(os_[sub] / ls[sub], lanes, pair, tq)
            lse = ms[sub] + jnp.log(ls[sub])
            lse = _unstack_heads(jnp.broadcast_to(lse, (lse.shape[0], 128)), lanes, pair, tq)
            dst = _folded_rows(qis[sub] * tq, tq, d) if mode == "dil" else slice(sub * tq, (sub + 1) * tq)
            o_ref[dst, :] = out
            l_ref[dst, :] = lse

    in_specs = [pl.BlockSpec((rows, 128), lambda u, i: (i, qcol + u)),
                pl.BlockSpec((lk, 128), lambda u, i: (0, kcol + u)),
                pl.BlockSpec((lk, 128), lambda u, i: (0, vcol + u))]
    args = [q_arr, k_arr, v_arr]
    if mode == "na":
        in_specs.append(pl.BlockSpec((2, NA_ROWS, GRID_W, NA_ROWS * GRID_W), lambda u, i: (u, 0, 0, 0)))
        args.append(bias)
    if mode == "dil":
        out_spec = pl.BlockSpec((SEQ, 128), lambda u, i: (0, u))
    else:
        out_spec = pl.BlockSpec((rows, 128), lambda u, i: (i, u))
    return pl.pallas_call(
        body, name=name, grid=(cfg["units"], SEQ // rows), in_specs=in_specs, out_specs=[out_spec, out_spec],
        out_shape=[jax.ShapeDtypeStruct((SEQ, 512), F32), jax.ShapeDtypeStruct((SEQ, 512), F32)],
        compiler_params=_params(("parallel", "arbitrary")))(*args)


def _attn_bwd(name, mode, q_arr, k_arr, v_arr, qcol, kcol, vcol, do, lse, dp=None, o=None, d=1, bias=None,
              tabs=None):
    cfg = _attn_cfg(mode, d)
    pair, tq, tk, mlen, lk, scale = cfg["pair"], cfg["tq"], cfg["tk"], cfg["mlen"], cfg["lk"], cfg["scale"]
    qscale, sscale = (scale, 1.0) if pair else (1.0, scale)
    nsub = cfg["nsub"]
    rows = nsub * tq
    nq = SEQ // rows
    kv_dtype = F32 if mode == "mem" else BF16

    def body(*refs):
        refs = list(refs)
        q_ref, k_ref, v_ref, do_ref, l_ref = refs[:5]
        rest = refs[5:]
        bias_ref = tq_ref = tk_ref = db_ref = None
        if mode == "dil":
            dp_ref, tq_ref, tk_ref, dq_ref, dk_ref, dv_ref, dk_acc, dv_acc = rest
        elif mode == "na":
            o_ref, bias_ref, dq_ref, dk_ref, dv_ref, db_ref, dk_acc, dv_acc = rest
        else:
            o_ref, dq_ref, dk_ref, dv_ref, dk_acc, dv_acc = rest
        step = pl.program_id(1)

        @pl.when(step == 0)
        def _():
            dk_acc[...] = jnp.zeros((lk, 128), F32)
            dv_acc[...] = jnp.zeros((lk, 128), F32)
            if mode == "na":
                db_ref[...] = jnp.zeros(db_ref.shape, F32)

        lanes = _iota((tq, 128), 1)
        qis = [step * nsub + sub for sub in range(nsub)]
        sls = [slice(sub * tq, (sub + 1) * tq) for sub in range(nsub)]
        kss = [_window(mode, qi, tq, mlen, tk) for qi in qis]
        ks_ = [k_ref[pl.ds(ks, tk), :] for ks in kss]
        vs = [v_ref[pl.ds(ks, tk), :] for ks in kss]
        qsts, dosts, lses, dphs = [], [], [], []
        for sub in range(nsub):
            if mode == "dil":
                src = _folded_rows(qis[sub] * tq, tq, d)
                dov = do_ref[src, :].astype(BF16)
                lsev = l_ref[src, :]
                dphs.append(_stack_rows(dp_ref[src, :], lanes, pair))
            else:
                dov = do_ref[sls[sub], :]
                lsev = l_ref[sls[sub], :]
                dpv = dov.astype(F32) * o_ref[sls[sub], :]
                if pair:
                    dphs.append(jnp.concatenate(
                        [jnp.sum(jnp.where(_head_lanes(lanes, hh), dpv, 0.0), axis=1, keepdims=True)
                         for hh in range(2)], axis=0))
                else:
                    dphs.append(jnp.sum(dpv, axis=1, keepdims=True))
            qsts.append(_stack_heads(q_ref[sls[sub], :], lanes, pair, qscale))
            dosts.append(_stack_heads(dov, lanes, pair))
            lses.append(_stack_rows(lsev, lanes, pair))
        bands = [_band_mask(qi, tq, tk, ks) if mode == "dil" else None for qi, ks in zip(qis, kss)]
        ss = [_scores(mode, qsts[sub], ks_[sub], sscale, bands[sub], qis[sub], bias_ref, pair) for sub in range(nsub)]
        dpms = [lax.dot_general(dosts[sub], vs[sub], NT, preferred_element_type=F32) for sub in range(nsub)]
        ps = [jnp.exp(s_ - lse) for s_, lse in zip(ss, lses)]
        dss = [p * (dpm - dph) for p, dpm, dph in zip(ps, dpms, dphs)]
        if mode == "na":
            for sub, ds in enumerate(dss):
                off = qis[sub] - jnp.clip(qis[sub] - NA_ROWS // 2, 0, SEQ // GRID_W - NA_ROWS)
                db_ref[0, off] += ds[:tq]
                db_ref[1, off] += ds[tq:]
        dsbs = [ds.astype(BF16) for ds in dss]
        dvs = [lax.dot_general(p.astype(BF16), dosts[sub], TN, preferred_element_type=F32)
               for sub, p in enumerate(ps)]
        dqs = [jnp.dot(dsb, ks_[sub], preferred_element_type=F32) * scale for sub, dsb in enumerate(dsbs)]
        dks = [lax.dot_general(dsb, qsts[sub], TN, preferred_element_type=F32) for sub, dsb in enumerate(dsbs)]
        for sub in range(nsub):
            sl = sls[sub]
            dq = _unstack_heads(dqs[sub], lanes, pair, tq)
            if mode == "dil":
                dq = _rope_t(dq, tq_ref[0, sl, :], tq_ref[1, sl, :], tq_ref[2, sl, :])
            dq_ref[sl, :] = dq.astype(BF16)
            dk_acc[pl.ds(kss[sub], tk), :] += dks[sub] if pair else dks[sub] * scale
            dv_acc[pl.ds(kss[sub], tk), :] += dvs[sub]

        @pl.when(step == nq - 1)
        def _():
            dkv = dk_acc[...]
            if mode == "dil":
                dkv = _rope_t(dkv, tk_ref[0], tk_ref[1], tk_ref[2])
            dk_ref[...] = dkv.astype(kv_dtype)
            dv_ref[...] = dv_acc[...].astype(kv_dtype)

    q_spec = pl.BlockSpec((rows, 128), lambda u, i: (i, qcol + u))
    row_spec = pl.BlockSpec((rows, 128), lambda u, i: (i, u))
    kv_out = pl.BlockSpec((lk, 128), lambda u, i: (0, u))
    whole = pl.BlockSpec((SEQ, 128), lambda u, i: (0, u))
    nat_spec = whole if mode == "dil" else row_spec
    in_specs = [q_spec,
                pl.BlockSpec((lk, 128), lambda u, i: (0, kcol + u)),
                pl.BlockSpec((lk, 128), lambda u, i: (0, vcol + u)),
                nat_spec, nat_spec]
    args = [q_arr, k_arr, v_arr, do, lse]
    out_specs = [row_spec, kv_out, kv_out]
    out_shape = [jax.ShapeDtypeStruct((SEQ, 512), BF16), jax.ShapeDtypeStruct((lk, 512), kv_dtype),
                 jax.ShapeDtypeStruct((lk, 512), kv_dtype)]
    if mode == "dil":
        in_specs += [whole, pl.BlockSpec((3, rows, 128), lambda u, i: (0, i, 0)),
                     pl.BlockSpec((3, SEQ, 128), lambda u, i: (0, 0, 0))]
        args += [dp, tabs, tabs]
    elif mode == "na":
        b_spec = pl.BlockSpec((2, NA_ROWS, GRID_W, NA_ROWS * GRID_W), lambda u, i: (u, 0, 0, 0))
        in_specs += [row_spec, b_spec]
        args += [o, bias]
        out_specs.append(b_spec)
        out_shape.append(jax.ShapeDtypeStruct((8, NA_ROWS, GRID_W, NA_ROWS * GRID_W), F32))
    else:
        in_specs.append(row_spec)
        args.append(o)
    return pl.pallas_call(
        body, name=name, grid=(cfg["units"], nq), in_specs=in_specs, out_specs=out_specs, out_shape=out_shape,
        scratch_shapes=[pltpu.VMEM((lk, 128), F32), pltpu.VMEM((lk, 128), F32)],
        compiler_params=_params(("parallel", "arbitrary")))(*args)


def _na_geometry():
    qc = _iota((GRID_W, 128), 0)
    lane = _iota((GRID_W, 128), 1)
    kc = lane & 63
    c_start = jnp.clip(qc - 8, 0, GRID_W - 16)
    valid = jnp.logical_and(kc >= c_start, kc < c_start + 16)
    return lane, valid


def _na_bias(rpb_rows, dep=None):
    dep_specs, dep_args = _dep_operand(dep)

    def body(r_ref, *rest):
        o_ref, t_ref = rest[-2:]
        lane, valid = _na_geometry()
        for dd in range(14):
            row_a = jnp.broadcast_to(r_ref[dd:dd + 1, :], (GRID_W, 128))
            row_b = jnp.broadcast_to(r_ref[dd + 1:dd + 2, :], (GRID_W, 128))
            both = jnp.where(lane < 64, row_a, pltpu.roll(row_b, 64, 1))
            t = pltpu.roll(both, 128 - 15, 1, stride=1, stride_axis=0)
            t_ref[dd] = jnp.where(valid, t, NEG)
        for off in range(NA_ROWS):
            for p in range(4):
                o_ref[off, :, p * 128:(p + 1) * 128] = t_ref[2 * p - off + 7]

    return pl.pallas_call(
        body, name="na_bias", grid=(8,),
        in_specs=[pl.BlockSpec((None, 16, 128), lambda h: (h, 0, 0))] + dep_specs,
        out_specs=pl.BlockSpec((None, NA_ROWS, GRID_W, NA_ROWS * GRID_W), lambda h: (h, 0, 0, 0)),
        out_shape=jax.ShapeDtypeStruct((8, NA_ROWS, GRID_W, NA_ROWS * GRID_W), F32),
        scratch_shapes=[pltpu.VMEM((14, GRID_W, 128), F32)],
        compiler_params=_params(("parallel",)))(rpb_rows, *dep_args)


def _na_bias_bwd(dbias, dep=None):
    dep_specs, dep_args = _dep_operand(dep)

    def body(d_ref, *rest):
        o_ref = rest[-1]
        lane, valid = _na_geometry()
        reverse = (_iota((GRID_W, GRID_W), 0) + _iota((GRID_W, GRID_W), 1) == GRID_W - 1).astype(F32)
        o_ref[...] = jnp.zeros((16, 128), F32)
        for dd in range(14):
            t = jnp.zeros((GRID_W, 128), F32)
            for off in range(NA_ROWS):
                for p in range(4):
                    if 2 * p - off + 7 == dd:
                        t = t + d_ref[off, :, p * 128:(p + 1) * 128]
            t = jnp.dot(reverse, jnp.where(valid, t, 0.0), precision=lax.Precision.HIGHEST,
                        preferred_element_type=F32)
            t = pltpu.roll(t, 128 - (GRID_W - 16), 1, stride=1, stride_axis=0)
            o_ref[dd:dd + 1, :] = jnp.sum(t, axis=0, keepdims=True)

    return pl.pallas_call(
        body, name="na_bias_bwd", grid=(8,),
        in_specs=[pl.BlockSpec((None, NA_ROWS, GRID_W, NA_ROWS * GRID_W), lambda h: (h, 0, 0, 0))] + dep_specs,
        out_specs=pl.BlockSpec((None, 16, 128), lambda h: (h, 0, 0)),
        out_shape=jax.ShapeDtypeStruct((8, 16, 128), F32),
        compiler_params=_params(("parallel",)))(dbias, *dep_args)


GATE_ROWS = 128


def _group_weights(l0, l1, l2):
    m = jnp.maximum(jnp.maximum(l0, l1), l2)
    e0, e1, e2 = jnp.exp(l0 - m), jnp.exp(l1 - m), jnp.exp(l2 - m)
    inv = 1.0 / (e0 + e1 + e2)
    return e0 * inv, e1 * inv, e2 * inv


def _gate_block(o_grp, l_grp, out_b, out_c, parts, x, target, merge_bias, wts, w_out, gain, head_sum):
    rows = GATE_ROWS
    r512 = pl.BlockSpec((rows, 512), lambda i: (i, 0))
    r1024 = pl.BlockSpec((rows, D_MODEL), lambda i: (i, 0))
    silu_cols = [pl.BlockSpec((rows, 512), functools.partial(lambda b, i: (i, b), 13 + b)) for b in range(3)]
    logit_cols = [pl.BlockSpec((rows, D_MODEL), functools.partial(lambda b, i: (i, b), 8 + b)) for b in range(3)]

    def body(o0, o1, o2, l0, l1, l2, ob, oc, ga, gb, gc, la, lb, lc, x_ref, t_ref, mb, wa, wb, wc, wo_ref, gn_ref,
             hs_ref, dout_ref, dla, dlb, dlc, dga, dgb, dgc, do0, do1, do2, dp0, dp1, dp2, dob, doc, err_ref, gg_ref,
             gmb, gwa, gwb, gwc, gwo, acc_a, acc_b, acc_c, acc_o):
        step = pl.program_id(0)
        ws = _group_weights(l0[...], l1[...], l2[...])
        out_a = ws[0] * o0[...] + ws[1] * o1[...] + ws[2] * o2[...]
        branches = ((out_a, ga, la, wa, acc_a, dla, dga), (ob[...], gb, lb, wb, acc_b, dlb, dgb),
                    (oc[...], gc, lc, wc, acc_c, dlc, dgc))

        @pl.when(step == 0)
        def _():
            for acc in (acc_a, acc_b, acc_c, acc_o):
                acc[...] = jnp.zeros(acc.shape, F32)
            err_ref[...] = jnp.zeros((1, D_MODEL), F32)
            gg_ref[...] = jnp.zeros((1, D_MODEL), F32)
            gmb[...] = jnp.zeros((3, D_MODEL), F32)

        y = jnp.zeros((rows, D_MODEL), F32)
        zs, gates, silus, dsilus, us = [], [], [], [], []
        for b, (ov, g_ref, l_ref, w_ref, _, _, _) in enumerate(branches):
            g = g_ref[...].astype(F32)
            sg = _sigmoid(g)
            silus.append(g * sg)
            dsilus.append(sg * (1.0 + g * (1.0 - sg)))
            us.append((ov * silus[b]).astype(BF16))
            zs.append(lax.dot_general(us[b], w_ref[...], NT, preferred_element_type=F32))
            gates.append(_sigmoid(l_ref[...].astype(F32) + mb[b:b + 1, :]))
            y = y + gates[b] * zs[b]
        yb = y.astype(BF16)
        y2 = jnp.dot(yb, wo_ref[...], preferred_element_type=F32)
        rstd = lax.rsqrt(jnp.mean(y2 * y2, axis=1, keepdims=True) + EPS)
        yn = y2 * rstd
        gv = gn_ref[...]
        err = x_ref[...] + yn * gv - t_ref[...]
        dout = err * (1.0 / D_MODEL)
        dout_ref[...] = dout
        dn = dout * gv
        dy2 = (rstd * (dn - yn * jnp.mean(dn * yn, axis=1, keepdims=True))).astype(BF16)
        acc_o[...] += lax.dot_general(yb, dy2, TN, preferred_element_type=F32)
        err_ref[...] += jnp.sum(err * err, axis=0, keepdims=True)
        gg_ref[...] += jnp.sum(dout * yn, axis=0, keepdims=True)
        dy = lax.dot_general(dy2, wo_ref[...], NT, preferred_element_type=F32)
        dos = []
        for b, (ov, _, _, w_ref, acc, dl_ref, dg_ref) in enumerate(branches):
            dl = dy * zs[b] * gates[b] * (1.0 - gates[b])
            dl_ref[...] = dl.astype(BF16)
            gmb[b:b + 1, :] += jnp.sum(dl, axis=0, keepdims=True)
            dz = (dy * gates[b]).astype(BF16)
            acc[...] += lax.dot_general(dz, us[b], TN, preferred_element_type=F32)
            du = jnp.dot(dz, w_ref[...], preferred_element_type=F32)
            dos.append(du * silus[b])
            dg_ref[...] = (du * ov * dsilus[b]).astype(BF16)
        dob[...] = dos[1].astype(BF16)
        doc[...] = dos[2].astype(BF16)
        row_term = jnp.dot(dos[0] * out_a, hs_ref[...], precision=lax.Precision.HIGHEST, preferred_element_type=F32)
        for wg, do_ref, dp_ref in zip(ws, (do0, do1, do2), (dp0, dp1, dp2)):
            do_ref[...] = wg * dos[0]
            dp_ref[...] = wg * row_term

        @pl.when(step == SEQ // rows - 1)
        def _():
            for acc, out in ((acc_a, gwa), (acc_b, gwb), (acc_c, gwc), (acc_o, gwo)):
                out[...] = acc[...].astype(BF16)

    full = lambda shape: pl.BlockSpec(shape, lambda i: (0,) * len(shape))
    vec = pl.BlockSpec((1, D_MODEL), lambda i: (0, 0))
    acc3 = pl.BlockSpec((3, D_MODEL), lambda i: (0, 0))
    in_specs = ([r512] * 8 + silu_cols + logit_cols + [r1024, r1024, full((3, D_MODEL))]
                + [full((D_MODEL, 512))] * 3 + [full((D_MODEL, D_MODEL)), vec, full((512, 512))])
    out_specs = ([r1024] + [r1024] * 3 + [r512] * 3 + [r512] * 6 + [r512] * 2 + [vec, vec, acc3]
                 + [full((D_MODEL, 512))] * 3 + [full((D_MODEL, D_MODEL))])
    bf, f32 = BF16, F32
    sds = jax.ShapeDtypeStruct
    out_shape = ([sds((SEQ, D_MODEL), f32)] + [sds((SEQ, D_MODEL), bf)] * 3 + [sds((SEQ, 512), bf)] * 3
                 + [sds((SEQ, 512), f32)] * 6 + [sds((SEQ, 512), bf)] * 2 + [sds((1, D_MODEL), f32)] * 2
                 + [sds((3, D_MODEL), f32)] + [sds((D_MODEL, 512), bf)] * 3 + [sds((D_MODEL, D_MODEL), bf)])
    res = pl.pallas_call(
        body, name="gate_block", grid=(SEQ // rows,), in_specs=in_specs, out_specs=out_specs, out_shape=out_shape,
        scratch_shapes=[pltpu.VMEM((D_MODEL, 512), F32)] * 3 + [pltpu.VMEM((D_MODEL, D_MODEL), F32)],
        compiler_params=_params(("arbitrary",)))(
            *o_grp, *l_grp, out_b, out_c, parts, parts, parts, parts, parts, parts, x, target, merge_bias, *wts, w_out,
            gain, head_sum)
    return dict(dout=res[0], dlog=res[1:4], dg=res[4:7], do_grp=res[7:10], dp_grp=res[10:13], do_b=res[13],
                do_c=res[14], err_sq=res[15], g_post=res[16], g_mb=res[17], g_wt=res[18:21], g_w_out=res[21])


def _local_step(x, hst, parts, tabs, bias, mem, target, pre_norm, mem_norm, post_norm, wt_in, late_weights,
                reduce_start=None):
    o_grp, l_grp = [], []
    for g, d in enumerate(DILATIONS):
        o, l = _attn_fwd("dil_fwd_%d" % g, "dil", parts, parts, parts, 12 * g, 12 * g + 4, 12 * g + 8, d=d)
        o_grp.append(o)
        l_grp.append(l)
    out_b, lse_b = _attn_fwd("na_fwd", "na", parts, parts, parts, 36, 40, 44, bias=bias)
    merge_bias, w_kv, wt_a, wt_b, wt_c, w_out = late_weights(sum(a[:8, :128] for a in [out_b] + o_grp))
    memn = _rmsnorm_fwd("memnorm", mem, mem_norm, MEM_LEN)
    kv_m = _mm_simple("mem_kv", memn, w_kv, NN, BF16, MEM_LEN, 512, D_MODEL)
    out_c, lse_c = _attn_fwd("mem_fwd", "mem", parts, kv_m, kv_m, 48, 0, 4)

    rr = _iota((512, 512), 0) // HEAD_DIM
    cc = _iota((512, 512), 1) // HEAD_DIM
    head_sum = (rr == cc).astype(F32)
    gb = _gate_block(o_grp, l_grp, out_b, out_c, parts, x, target, merge_bias, (wt_a, wt_b, wt_c), w_out, post_norm,
                     head_sum)
    dout, dlog, dg, g_wt, g_w_out = gb["dout"], gb["dlog"], gb["dg"], gb["g_wt"], gb["g_w_out"]
    do_grp, dp_grp, do_b, do_c, g_post, g_mb = (gb["do_grp"], gb["dp_grp"], gb["do_b"], gb["do_c"], gb["g_post"],
                                                gb["g_mb"])
    loss = 0.5 * jnp.sum(gb["err_sq"]) / D_MODEL

    dqkv = []
    for g, d in enumerate(DILATIONS):
        dq, dk, dv = _attn_bwd("dil_bwd_%d" % g, "dil", parts, parts, parts, 12 * g, 12 * g + 4, 12 * g + 8,
                               do_grp[g], l_grp[g], dp=dp_grp[g], d=d, tabs=tabs[g])
        dqkv += [dq, dk, dv]
    dq_b, dk_b, dv_b, dbias = _attn_bwd("na_bwd", "na", parts, parts, parts, 36, 40, 44, do_b, lse_b, o=out_b,
                                        bias=bias)
    dq_c, dk_m, dv_m = _attn_bwd("mem_bwd", "mem", parts, kv_m, kv_m, 48, 0, 4, do_c, lse_c, o=out_c)

    dkv = jnp.concatenate([dk_m, dv_m], axis=1).astype(BF16)
    g_w_kv = _mm_simple("mem_kv_dw", memn, dkv, TN, BF16, D_MODEL, 512, MEM_LEN)
    dmemn = _mm_simple("mem_kv_dx", dkv, w_kv, NT, F32, MEM_LEN, 512, D_MODEL)

    grads = dict(w_kv=g_w_kv, wt_a=g_wt[0], wt_b=g_wt[1], wt_c=g_wt[2], w_out=g_w_out, merge_bias=g_mb,
                 post_norm=g_post)
    dep = None
    if reduce_start is not None:
        reduce_start("rest_sibling", grads)
        dep = reduce_start("rest_chips", grads, sum(a[:8, :128] for a in (dqkv[0], dqkv[3], dqkv[6], dq_b, dq_c)))
    dparts = dqkv + [dq_b, dk_b, dv_b, dq_c] + list(dg) + list(dlog)
    grads["wt_in"] = _in_proj_dw(dparts, hst, dep)
    dep = reduce_start("w_in", grads) if reduce_start is not None else None
    dh = _in_proj_dh(dparts, wt_in, dep)
    grad_x, grads["pre_norm"] = _prenorm_bwd(x, pre_norm, dh, dout)
    g_rpb_t = _na_bias_bwd(dbias, dep)
    grads["na_rpb"] = g_rpb_t[:, :15, :31] + jnp.pad(g_rpb_t[:, :14, 64:95], ((0, 0), (1, 0), (0, 0)))
    grads["mem_norm"] = _memnorm_bwd(mem, dmemn, dep)
    return loss, grad_x, grads


ANY = pl.BlockSpec(memory_space=pl.ANY)


def _place():
    return lax.axis_index("x"), lax.axis_index("y"), lax.axis_index("c")


HBM = pl.BlockSpec(memory_space=pltpu.HBM)
SEM = pl.BlockSpec(memory_space=pltpu.SEMAPHORE)
DATAFLOW = pltpu.SideEffectType.DATAFLOW_SIDE_EFFECTING


def _split_copies(kind, srcs, lands, send_sems, recv_sems):
    nt = len(srcs)
    x, y, c = _place()
    copies = []
    if kind == "sibling":
        for q in range(4):
            for t in range(nt):
                k = q * nt + t
                copies.append(pltpu.make_async_remote_copy(
                    src_ref=srcs[t].at[2 * q + 1 - c], dst_ref=lands[t].at[q], send_sem=send_sems.at[k],
                    recv_sem=recv_sems.at[k], device_id=(x, y, 1 - c), device_id_type=MESH_ID))
    elif kind == "gather":
        me = 4 * x + 2 * y + c
        for mask in range(1, 8):
            fx, fy, fc = (mask >> 2) & 1, (mask >> 1) & 1, mask & 1
            to = (1 - x if fx else x, 1 - y if fy else y, 1 - c if fc else c)
            for t in range(nt):
                k = (mask - 1) * nt + t
                copies.append(pltpu.make_async_remote_copy(
                    src_ref=srcs[t], dst_ref=lands[t].at[me], send_sem=send_sems.at[k], recv_sem=recv_sems.at[k],
                    device_id=to, device_id_type=MESH_ID))
    else:
        for s, (tx, ty) in enumerate([(1 - x, y), (x, 1 - y), (1 - x, 1 - y)]):
            for t in range(nt):
                k = s * nt + t
                copies.append(pltpu.make_async_remote_copy(
                    src_ref=srcs[t].at[2 * tx + ty], dst_ref=lands[t].at[s], send_sem=send_sems.at[k],
                    recv_sem=recv_sems.at[k], device_id=(tx, ty, c), device_id_type=MESH_ID))
    return copies


def _split_count(kind, nt):
    return {"gather": 7, "chips": 3, "sibling": 4}[kind] * nt


def _exchange_start(name, kind, srcs, land_shapes, after=None):
    nt = len(srcs)
    n = _split_count(kind, nt)
    dep_specs, dep_args = _dep_operand(after)
    nd = len(dep_args)

    def body(*refs):
        src_refs, land_refs = refs[:nt], refs[nt:2 * nt]
        send_sems, recv_sems = refs[2 * nt + nd], refs[2 * nt + nd + 1]
        token = refs[-1]
        for cp in _split_copies(kind, src_refs, land_refs, send_sems, recv_sems):
            cp.start()
        token[...] = jnp.zeros_like(token)

    lands = [pltpu.with_memory_space_constraint(lax.empty(s.shape, s.dtype), pltpu.HBM) for s in land_shapes]
    res = pl.pallas_call(
        body, name=name,
        out_shape=(pltpu.SemaphoreType.DMA((n,)), pltpu.SemaphoreType.DMA((n,)),
                   *[pltpu.HBM(s.shape, s.dtype) for s in srcs], *[pltpu.HBM(s.shape, s.dtype) for s in land_shapes],
                   jax.ShapeDtypeStruct((8, 128), F32)),
        in_specs=[HBM] * (2 * nt) + dep_specs,
        out_specs=(SEM, SEM, *([HBM] * (2 * nt)), pl.BlockSpec(memory_space=pltpu.VMEM)),
        input_output_aliases={i: 2 + i for i in range(2 * nt)},
        compiler_params=pltpu.CompilerParams(has_side_effects=DATAFLOW))(
            *[pltpu.with_memory_space_constraint(s, pltpu.HBM) for s in srcs], *lands, *dep_args)
    return res[0], res[1], list(res[2:2 + nt]), list(res[2 + nt:2 + 2 * nt]), res[-1]


def _exchange_wait(name, kind, send_sems, recv_sems, srcs, lands, after):
    nt = len(srcs)

    def body(*refs):
        src_refs, land_refs = refs[:nt], refs[nt:2 * nt]
        s_sems, r_sems = refs[2 * nt], refs[2 * nt + 1]
        for cp in _split_copies(kind, src_refs, land_refs, s_sems, r_sems):
            cp.wait_send()
            cp.wait_recv()

    res = pl.pallas_call(
        body, name=name,
        out_shape=tuple(pltpu.HBM(s.shape, s.dtype) for s in list(srcs) + list(lands)),
        in_specs=[HBM] * (2 * nt) + [SEM, SEM, pl.BlockSpec(memory_space=pl.ANY)],
        out_specs=tuple([HBM] * (2 * nt)),
        input_output_aliases={i: i for i in range(2 * nt)},
        compiler_params=pltpu.CompilerParams(has_side_effects=DATAFLOW))(
            *srcs, *lands, send_sems, recv_sems, after)
    return list(res[:nt]), list(res[nt:])


AG_GROUPS = ((0, 3), (3, 4), (7, 2))


def _ag_phase(name, own, land, sems, waits, starts, after=None):
    r = own.shape[0]
    half = r // 2
    ns = len(sems)
    dep_specs, dep_args = _dep_operand(after)
    nd = len(dep_args)
    new_group = None
    if starts:
        (new_group,) = [g for g, (first, n) in enumerate(AG_GROUPS) if first == starts[0]]
        assert list(starts) == list(range(AG_GROUPS[new_group][0], sum(AG_GROUPS[new_group])))

    def body(*refs):
        own_ref, land_ref = refs[0], refs[1]
        sem_refs = list(refs[2:2 + 2 * ns])
        outs = refs[2 + 2 * ns + nd:]
        if starts:
            sem_refs += [outs[0], outs[1]]
        x, y, c = _place()
        me, sib = (x, y, c), (x, y, 1 - c)
        xn, yn, dg = (1 - x, y, c), (x, 1 - y, c), (1 - x, 1 - y, c)

        def other(dev):
            return (dev[0], dev[1], 1 - dev[2])

        def rows(dev, part):
            blk = land_ref.at[4 * dev[0] + 2 * dev[1] + dev[2]]
            return blk if part is None else blk.at[pl.ds(part * half, half)]

        def sem_of(k):
            (g,) = [g for g, (first, n) in enumerate(AG_GROUPS) if first <= k < first + n]
            return sem_refs[2 * g].at[k - AG_GROUPS[g][0]], sem_refs[2 * g + 1].at[k - AG_GROUPS[g][0]]

        sent = {0: (me, None, sib), 1: (me, None, xn), 2: (me, None, yn), 3: (xn, 0, yn), 4: (yn, 1, xn),
                5: (xn, None, sib), 6: (yn, None, sib), 7: (dg, 0, sib), 8: (dg, 1, sib)}
        landed = {0: (sib, None), 1: (xn, None), 2: (yn, None), 3: (dg, 0), 4: (dg, 1), 5: (other(xn), None),
                  6: (other(yn), None), 7: (other(dg), 0), 8: (other(dg), 1)}

        def copy(k, receiving):
            send_sem, recv_sem = sem_of(k)
            dev, part, to = (*landed[k], me) if receiving else sent[k]
            src = own_ref if (dev is me and not receiving) else rows(dev, part)
            return pltpu.make_async_remote_copy(src_ref=src, dst_ref=rows(dev, part), send_sem=send_sem,
                                                recv_sem=recv_sem, device_id=to, device_id_type=MESH_ID)

        for kind, k in waits:
            if kind == "recv":
                copy(k, True).wait_recv()
            else:
                copy(k, False).wait_send()
        for k in starts:
            copy(k, False).start()
        if starts:
            outs[-1][...] = jnp.zeros_like(outs[-1])

    n_new = AG_GROUPS[new_group][1] if starts else 0
    sem_out = (pltpu.SemaphoreType.DMA((n_new,)), pltpu.SemaphoreType.DMA((n_new,))) if starts else ()
    token_out = (jax.ShapeDtypeStruct((8, 128), F32),) if starts else ()
    res = pl.pallas_call(
        body, name=name,
        out_shape=(*sem_out, pltpu.HBM(own.shape, own.dtype), pltpu.HBM(land.shape, land.dtype), *token_out),
        in_specs=[HBM, HBM] + [SEM] * (2 * ns) + dep_specs,
        out_specs=(*([SEM] * len(sem_out)), HBM, HBM, *([pl.BlockSpec(memory_space=pltpu.VMEM)] * len(token_out))),
        input_output_aliases={0: len(sem_out), 1: len(sem_out) + 1},
        compiler_params=pltpu.CompilerParams(has_side_effects=DATAFLOW))(
            own, land, *[a for pair in sems for a in pair], *dep_args)
    if starts:
        return (res[0], res[1]), res[2], res[3], res[4]
    return None, res[0], res[1], None


def _add_sibling(name, term, recv, rows):
    _, r, w = term.shape
    cidx = lax.axis_index("c").astype(jnp.int32).reshape(1)
    like_term = recv.shape[0] == N_DEV

    def body(c_ref, a_ref, b_ref, o_ref):
        o_ref[...] = (a_ref[...].astype(F32) + b_ref[...].astype(F32)).astype(o_ref.dtype)

    grid_spec = pltpu.PrefetchScalarGridSpec(
        num_scalar_prefetch=1, grid=(4, r // rows),
        in_specs=[pl.BlockSpec((None, rows, w), lambda q, i, c_ref: (2 * q + c_ref[0], i, 0)),
                  pl.BlockSpec((None, rows, w), lambda q, i, c_ref: (2 * q + c_ref[0] if like_term else q, i, 0))],
        out_specs=pl.BlockSpec((None, rows, w), lambda q, i, c_ref: (q, i, 0)))
    return pl.pallas_call(
        body, name=name, grid_spec=grid_spec, out_shape=jax.ShapeDtypeStruct((4, r, w), term.dtype),
        compiler_params=_params(("parallel", "parallel")))(cidx, term, recv)


def _add_sibling_small(name, terms, recvs):
    nt = len(terms)

    def body(*refs):
        c = lax.axis_index("c")
        for t_ref, r_ref, o_ref in zip(refs[:nt], refs[nt:2 * nt], refs[2 * nt:]):
            for q in range(4):
                o_ref[q] = (t_ref[2 * q + c].astype(F32) + r_ref[q].astype(F32)).astype(o_ref.dtype)

    return pl.pallas_call(
        body, name=name, out_shape=[jax.ShapeDtypeStruct((4,) + t.shape[1:], t.dtype) for t in terms],
        compiler_params=_params())(*terms, *recvs)


def _rs_rows(a):
    return SHARD_IN // 4 if a.shape[1] == SHARD_IN else a.shape[1]


def _reduce_scatter_start(tag, names, terms, recv1):
    if len(terms) == 1:
        sums = [_add_sibling("add_sibling_" + names[0], terms[0], recv1[0], _rs_rows(terms[0]))]
    else:
        sums = _add_sibling_small("add_sibling_" + tag, terms, recv1)
    lands =[jax.ShapeDtypeStruct((3,) + s.shape[1:], s.dtype) for s in sums]
    send_sems, recv_sems, sums, lands, token = _exchange_start("exchange_chips_start_" + tag, "chips", sums, lands)
    return (tag, names, send_sems, recv_sems, sums, lands), token


def _reduce_scatter_wait(state, after):
    tag, names, send_sems, recv_sems, sums, lands = state
    sums, recv2 = _exchange_wait("exchange_chips_wait_" + tag, "chips", send_sems, recv_sems, sums, lands, after)
    return names, sums, recv2


def _adamw(name, w, g, m, v, dep=None):
    dep_specs, dep_args = _dep_operand(dep)

    def body(w_ref, g_ref, m_ref, v_ref, *rest):
        d_ref, nm_ref, nv_ref = rest[-3:]
        d_ref[...], nm_ref[...], nv_ref[...] = _adam_math(w_ref[...], g_ref[...], m_ref[...], v_ref[...])

    whole = pl.BlockSpec(memory_space=pltpu.VMEM)
    return pl.pallas_call(
        body, name=name, in_specs=[whole] * 4 + dep_specs, out_shape=[jax.ShapeDtypeStruct(w.shape, F32)] * 3,
        compiler_params=_params())(w, g, m, v, *dep_args)


def _adam_math(w, g, m, v):
    nm = ADAM_B1 * m + (1.0 - ADAM_B1) * g
    nv = ADAM_B2 * v + (1.0 - ADAM_B2) * (g * g)
    c1 = 1.0 - ADAM_B1 ** ADAM_STEP
    c2 = 1.0 - ADAM_B2 ** ADAM_STEP
    return -ADAM_LR * ((nm / c1) / (jnp.sqrt(nv / c2) + ADAM_EPS) + ADAM_WD * w), nm, nv


def _adamw_chips(name, sums, recv, w, m, v, transposed, rows):
    r, c = w.shape
    qidx = (2 * lax.axis_index("x") + lax.axis_index("y")).astype(jnp.int32).reshape(1)

    def body(q_ref, a_ref, b_ref, w_ref, m_ref, v_ref, g_ref, d_ref, nm_ref, nv_ref):
        g = (a_ref[...].astype(F32) + b_ref[0].astype(F32)) + (b_ref[1].astype(F32) + b_ref[2].astype(F32))
        if transposed:
            g = g.T
        g_ref[...] = g
        d_ref[...], nm_ref[...], nv_ref[...] = _adam_math(w_ref[...], g, m_ref[...], v_ref[...])

    row = pl.BlockSpec((rows, c), lambda i, q_ref: (i, 0))
    if transposed:
        term_specs = [pl.BlockSpec((None, c, rows), lambda i, q_ref: (q_ref[0], 0, i)),
                      pl.BlockSpec((3, c, rows), lambda i, q_ref: (0, 0, i))]
    else:
        term_specs = [pl.BlockSpec((None, rows, c), lambda i, q_ref: (q_ref[0], i, 0)),
                      pl.BlockSpec((3, rows, c), lambda i, q_ref: (0, i, 0))]
    grid_spec = pltpu.PrefetchScalarGridSpec(
        num_scalar_prefetch=1, grid=(r // rows,), in_specs=term_specs + [row, row, row], out_specs=[row] * 4)
    return pl.pallas_call(
        body, name=name, grid_spec=grid_spec, out_shape=[jax.ShapeDtypeStruct((r, c), F32)] * 4,
        compiler_params=_params(("parallel",)))(qidx, sums, recv, w, m, v)


def _adamw_chips_small(name, items):
    n = len(items)

    def body(*refs):
        q = 2 * lax.axis_index("x") + lax.axis_index("y")
        ins, outs = refs[:5 * n], refs[5 * n:]
        for i, (_, _, w, _, _, transposed) in enumerate(items):
            s_ref, r_ref, w_ref, m_ref, v_ref = ins[5 * i:5 * i + 5]
            g_ref, d_ref, nm_ref, nv_ref = outs[4 * i:4 * i + 4]
            g = (s_ref[q].astype(F32) + r_ref[0].astype(F32)) + (r_ref[1].astype(F32) + r_ref[2].astype(F32))
            g = g.T if transposed else g[:w.shape[0]]
            g_ref[...] = g
            d_ref[...], nm_ref[...], nv_ref[...] = _adam_math(w_ref[...], g, m_ref[...], v_ref[...])

    res = pl.pallas_call(
        body, name=name, out_shape=[jax.ShapeDtypeStruct(it[2].shape, F32) for it in items for _ in range(4)],
        compiler_params=_params())(*[a for it in items for a in it[:5]])
    return [res[4 * i:4 * i + 4] for i in range(n)]


def _sum_devices(gathered):
    def body(g_ref, o_ref):
        acc = g_ref[0]
        for j in range(1, N_DEV):
            acc = acc + g_ref[j]
        o_ref[...] = acc

    return pl.pallas_call(
        body, name="sum_devices", out_shape=jax.ShapeDtypeStruct(gathered.shape[1:], F32),
        compiler_params=_params())(gathered)


def _rows128(a, rows):
    flat = a.reshape(-1)
    return jnp.pad(flat, (0, rows * 128 - flat.shape[0])).reshape(rows, 128)


def kernel(x, mem, pre_norm, w_in, merge_bias, na_rpb, mem_norm, w_mem_kv, w_branch_a, w_branch_b, w_branch_c, w_out, post_norm, loss_target, m_pre_norm, m_w_in, m_merge_bias, m_na_rpb, m_mem_norm, m_w_mem_kv, m_w_branch_a, m_w_branch_b, m_w_branch_c, m_w_out, m_post_norm, v_pre_norm, v_w_in, v_merge_bias, v_na_rpb, v_mem_norm, v_w_mem_kv, v_w_branch_a, v_w_branch_b, v_w_branch_c, v_w_out, v_post_norm):
    wt_in_s = w_in[0].T.astype(BF16)
    rows_s = jnp.concatenate([w_mem_kv[0], w_out[0]], axis=0).astype(BF16)
    cols_s = jnp.concatenate([w_branch_a[0].T, w_branch_b[0].T, w_branch_c[0].T], axis=0).astype(BF16)
    mb_s = jnp.pad(merge_bias[0], ((0, 5), (0, 0)))
    me = 4 * lax.axis_index("x") + 2 * lax.axis_index("y") + lax.axis_index("c")

    chip = 2 * lax.axis_index("x") + lax.axis_index("y")

    def first_block(q):
        return jnp.where(q == 0, 0, jnp.where(q == 1, 6, jnp.where(q == 2, 11, 17)))

    five = jnp.arange(5, dtype=jnp.int32)
    near, far = jnp.where(chip < 2, 5, 16), jnp.where(chip < 2, 16, 5)
    order1 = (first_block(chip) + five).astype(jnp.int32)
    order2 = jnp.concatenate([first_block(chip ^ 1) + five, near[None], first_block(chip ^ 2) + five]).astype(jnp.int32)
    order3 = jnp.concatenate([first_block(chip ^ 3) + five, far[None]]).astype(jnp.int32)
    tabs = _rope_tables()

    def weights_of(land):
        return land.reshape(N_IN, D_MODEL)

    land = pltpu.with_memory_space_constraint(lax.empty((N_DEV,) + wt_in_s.shape, BF16), pltpu.HBM)
    own = pltpu.with_memory_space_constraint(wt_in_s, pltpu.HBM)
    sem_a, own, land, token = _ag_phase("ag_start", own, land, [], [], [0, 1, 2])
    hs, hst = _prenorm_fold(x[0], pre_norm, token)
    _, own, land, _ = _ag_phase("ag_wait0", own, land, [sem_a], [("recv", 0)], [], hs)
    land = lax.dynamic_update_slice(land, own[None], (me, 0, 0))
    parts = _in_proj("in_proj_1", hs, weights_of(land), tabs, order1)
    bias = _na_bias(jnp.pad(na_rpb[0], ((0, 0), (0, 1), (0, 128 - 31))), parts)
    sem_b, own, land, _ = _ag_phase("ag_mid1", own, land, [sem_a], [("recv", 1), ("recv", 2)], [3, 4, 5, 6], bias)
    _, own, land, _ = _ag_phase("ag_wait1", own, land, [sem_a, sem_b], [("recv", 5), ("recv", 6)], [])
    parts = _in_proj("in_proj_2", hs, weights_of(land), tabs, order2, parts)
    sem_c, own, land, _ = _ag_phase("ag_mid2", own, land, [sem_a, sem_b], [("recv", 3), ("recv", 4)], [7, 8], parts)
    _, own, land, _ = _ag_phase("ag_end", own, land, [sem_a, sem_b, sem_c],
                                [("recv", 7), ("recv", 8)] + [("send", k) for k in range(9)], [])
    wt_in = weights_of(land)

    late_own = [rows_s, cols_s, mb_s]
    late_lands = [jax.ShapeDtypeStruct((N_DEV,) + s.shape, s.dtype) for s in late_own]
    l_send, l_recv, late_own, late_lands, late_token = _exchange_start("gather_late_start", "gather", late_own,
                                                                       late_lands, after=wt_in)
    parts = _in_proj("in_proj_3", hs, wt_in, tabs, order3, parts, late_token)

    def late_weights(after):
        own, lands = _exchange_wait("gather_late_wait", "gather", l_send, l_recv, late_own, late_lands, after)
        g_rows, g_cols, g_mb = [lax.dynamic_update_slice(land, o[None], (me, 0, 0)) for land, o in zip(lands, own)]
        return (g_mb[:, :3].transpose(1, 0, 2).reshape(3, D_MODEL),
                g_rows[:, :128].reshape(D_MODEL, D_MODEL), g_cols[:, 0:128].reshape(D_MODEL, 512),
                g_cols[:, 128:256].reshape(D_MODEL, 512), g_cols[:, 256:384].reshape(D_MODEL, 512),
                g_rows[:, 128:].reshape(D_MODEL, D_MODEL))

    rs_state = []
    rest_names = ["w_kv", "w_out", "a", "b", "c", "mb"]
    rest_sibling = []

    def reduce_start(phase, grads, after=None):
        if phase == "rest_sibling":
            gmb_t = jnp.pad(grads["merge_bias"].reshape(3, N_DEV, 128).transpose(1, 0, 2), ((0, 0), (0, 5), (0, 0)))
            terms = [grads["w_kv"].reshape(N_DEV, 128, D_MODEL), grads["w_out"].reshape(N_DEV, 128, D_MODEL),
                     grads["wt_a"].reshape(N_DEV, 128, 512), grads["wt_b"].reshape(N_DEV, 128, 512),
                     grads["wt_c"].reshape(N_DEV, 128, 512), gmb_t]
            lands = [jax.ShapeDtypeStruct((4,) + t.shape[1:], t.dtype) for t in terms]
            rest_sibling.extend(_exchange_start("exchange_sibling_start_rest", "sibling", terms, lands)[:4])
            return None
        if phase == "rest_chips":
            s_send, s_recv, terms, lands = rest_sibling
            terms, recv1 = _exchange_wait("exchange_sibling_wait_rest", "sibling", s_send, s_recv, terms, lands, after)
            state, token = _reduce_scatter_start("rest", rest_names, terms, recv1)
        else:
            own, sibling = [a.reshape(N_DEV, SHARD_IN, D_MODEL) for a in grads["wt_in"]]
            state, token = _reduce_scatter_start("w_in", ["w_in"], [own], [sibling])
        rs_state.append(state)
        return token

    loss_term, grad_x, grads = _local_step(
        x[0], hst, parts, tabs, bias, mem[0], loss_target[0], pre_norm, mem_norm, post_norm, wt_in, late_weights,
        reduce_start=reduce_start)

    small = jnp.concatenate([_rows128(grads["pre_norm"], 8), _rows128(grads["mem_norm"], 8),
                             _rows128(grads["post_norm"], 8), _rows128(grads["na_rpb"], 32),
                             _rows128(loss_term, 8)], axis=0)
    s_send, s_recv, s_own, s_land, s_token = _exchange_start(
        "gather_small_start", "gather", [small], [jax.ShapeDtypeStruct((N_DEV,) + small.shape, F32)])
    grad = {}
    weights = {
        "pre_norm": (pre_norm, m_pre_norm, v_pre_norm), "w_in": (w_in, m_w_in, v_w_in),
        "merge_bias": (merge_bias, m_merge_bias, v_merge_bias), "na_rpb": (na_rpb, m_na_rpb, v_na_rpb),
        "mem_norm": (mem_norm, m_mem_norm, v_mem_norm), "w_mem_kv": (w_mem_kv, m_w_mem_kv, v_w_mem_kv),
        "w_branch_a": (w_branch_a, m_w_branch_a, v_w_branch_a), "w_branch_b": (w_branch_b, m_w_branch_b, v_w_branch_b),
        "w_branch_c": (w_branch_c, m_w_branch_c, v_w_branch_c), "w_out": (w_out, m_w_out, v_w_out),
        "post_norm": (post_norm, m_post_norm, v_post_norm)}
    order = ["pre_norm", "w_in", "merge_bias", "na_rpb", "mem_norm", "w_mem_kv", "w_branch_a", "w_branch_b",
             "w_branch_c", "w_out", "post_norm"]
    delta, new_m, new_v = {}, {}, {}

    def update(n, dep=None):
        w, m, v = weights[n]
        shape = w.shape
        two_d = (-1, shape[-1])
        dl, nm, nv = _adamw("adamw_" + n, w.reshape(two_d), grad[n].reshape(two_d), m.reshape(two_d),
                            v.reshape(two_d), dep)
        delta[n], new_m[n], new_v[n] = dl.reshape(shape), nm.reshape(shape), nv.reshape(shape)
        return dl

    _, sums, recv2 = _reduce_scatter_wait(rs_state[0], s_token)
    rest = (("w_mem_kv", False), ("w_out", False), ("w_branch_a", True), ("w_branch_b", True), ("w_branch_c", True),
            ("merge_bias", False))
    items = [(sums[i], recv2[i], *[a[0] for a in weights[n]], transposed) for i, (n, transposed) in enumerate(rest)]
    for (n, _), (g, dl, nm, nv) in zip(rest, _adamw_chips_small("adamw_rest", items)):
        grad[n], delta[n], new_m[n], new_v[n] = g[None], dl[None], nm[None], nv[None]
    s_own, s_land = _exchange_wait("gather_small_wait", "gather", s_send, s_recv, s_own, s_land, delta["w_out"])
    total = _sum_devices(lax.dynamic_update_slice(s_land[0], s_own[0][None], (me, 0, 0)))
    loss = total[56, 0]
    grad.update({"pre_norm": total[0:8].reshape(1, D_MODEL), "mem_norm": total[8:16].reshape(1, D_MODEL),
                 "post_norm": total[16:24].reshape(1, D_MODEL),
                 "na_rpb": total[24:56].reshape(-1)[:8 * 15 * 31].reshape(1, 8, 15, 31)})
    dep = None
    for n in ("pre_norm", "na_rpb", "mem_norm", "post_norm"):
        dep = update(n, dep)
    _, sums_in, recv_in = _reduce_scatter_wait(rs_state[1], dep)
    g, dl, nm, nv = _adamw_chips("adamw_w_in", sums_in[0], recv_in[0], w_in[0], m_w_in[0], v_w_in[0], True, 256)
    grad["w_in"], delta["w_in"], new_m["w_in"], new_v["w_in"] = g[None], dl[None], nm[None], nv[None]

    return (loss, grad_x[None], *[grad[n] for n in order], *[delta[n] for n in order],
            *[new_m[n] for n in order], *[new_v[n] for n in order])
```

```python
import functools

import numpy as np
import jax
import jax.numpy as jnp
from jax import lax
from jax.experimental import pallas as pl
from jax.experimental.pallas import tpu as pltpu

F32 = jnp.float32
BF16 = jnp.bfloat16

SEQ = 2048
D_MODEL = 1024
N_IN = 11264
N_DEV = 8
SHARD_IN = N_IN // N_DEV
HEAD_DIM = 64
GRID_W = 64
NA_ROWS = 8
MEM_LEN = 256
DILATIONS = (1, 4, 16)
REACH = 64
ROPE_THETA = 500000.0
ROPE_DIM = 16
EPS = 1e-6
NEG = -1e30
ADAM_LR = 0.001
ADAM_B1 = 0.9
ADAM_B2 = 0.999
ADAM_EPS = 1e-08
ADAM_WD = 0.01
ADAM_STEP = 10

VMEM_LIMIT_BYTES = 56 * 1024 * 1024
MESH_ID = pl.DeviceIdType.MESH

NN = (((1,), (0,)), ((), ()))
NT = (((1,), (1,)), ((), ()))
TN = (((0,), (0,)), ((), ()))


def _params(sem=None):
    return pltpu.CompilerParams(dimension_semantics=sem, vmem_limit_bytes=VMEM_LIMIT_BYTES)


def _iota(shape, dim):
    return lax.broadcasted_iota(jnp.int32, shape, dim)


def _sigmoid(x):
    return 1.0 / (1.0 + jnp.exp(-x))


def _rope_tables():
    half = ROPE_DIM // 2
    inv = (ROPE_THETA ** (-np.arange(half, dtype=np.float64) * 2.0 / ROPE_DIM)).astype(np.float32)
    pos = np.arange(SEQ, dtype=np.float32)
    ang = pos[:, None] * inv[None, :]
    cos, sin = np.cos(ang), np.sin(ang)
    zeros = np.zeros_like(cos)
    rest = HEAD_DIM - ROPE_DIM
    c64 = np.concatenate([cos, cos, np.ones((SEQ, rest), np.float32)], axis=1)
    s1 = np.concatenate([zeros, sin, np.zeros((SEQ, rest), np.float32)], axis=1)
    s2 = np.concatenate([-sin, zeros, np.zeros((SEQ, rest), np.float32)], axis=1)

    def fold(t, d):
        return t.reshape(SEQ // d, d, t.shape[1]).transpose(1, 0, 2).reshape(SEQ, t.shape[1])

    tabs = [np.stack([np.tile(fold(t, d), (1, 2)) for t in (c64, s1, s2)], axis=0) for d in DILATIONS]
    return jnp.asarray(np.stack(tabs, axis=0), dtype=F32)


def _rope(a, c, s1, s2):
    return a * c + pltpu.roll(a, 8, 1) * s1 + pltpu.roll(a, 120, 1) * s2


def _rope_t(a, c, s1, s2):
    return a * c + pltpu.roll(a * s1, 120, 1) + pltpu.roll(a * s2, 8, 1)


def _perm_of_block(j):
    return jnp.where(j < 3, 0, jnp.where(j < 6, 1, jnp.where(j < 9, 2, 0)))


def _mm(name, a, b, out_shape, out_dtype, grid, a_spec, b_spec, o_spec, acc_shape, dims, k_axis, nk):
    def body(a_ref, b_ref, o_ref, acc_ref):
        k = pl.program_id(k_axis)

        @pl.when(k == 0)
        def _():
            acc_ref[...] = jnp.zeros(acc_shape, F32)

        acc_ref[...] += lax.dot_general(a_ref[...], b_ref[...], dims, preferred_element_type=F32)

        @pl.when(k == nk - 1)
        def _():
            o_ref[...] = acc_ref[...].astype(out_dtype)

    sem = tuple("arbitrary" if ax == k_axis else "parallel" for ax in range(len(grid)))
    return pl.pallas_call(
        body, name=name, grid=grid, in_specs=[a_spec, b_spec], out_specs=o_spec,
        out_shape=jax.ShapeDtypeStruct(out_shape, out_dtype),
        scratch_shapes=[pltpu.VMEM(acc_shape, F32)], compiler_params=_params(sem))(a, b)


def _mm_simple(name, a, b, dims, out_dtype, tm, tn, tk):
    if dims is NN:
        m, kk = a.shape
        n = b.shape[1]
        a_spec = pl.BlockSpec((tm, tk), lambda i, j, k: (i, k))
        b_spec = pl.BlockSpec((tk, tn), lambda i, j, k: (k, j))
    elif dims is NT:
        m, kk = a.shape
        n = b.shape[0]
        a_spec = pl.BlockSpec((tm, tk), lambda i, j, k: (i, k))
        b_spec = pl.BlockSpec((tn, tk), lambda i, j, k: (j, k))
    else:
        kk, m = a.shape
        n = b.shape[1]
        a_spec = pl.BlockSpec((tk, tm), lambda i, j, k: (k, i))
        b_spec = pl.BlockSpec((tk, tn), lambda i, j, k: (k, j))
    grid = (m // tm, n // tn, kk // tk)
    o_spec = pl.BlockSpec((tm, tn), lambda i, j, k: (i, j))
    return _mm(name, a, b, (m, n), out_dtype, grid, a_spec, b_spec, o_spec, (tm, tn), dims, 2, kk // tk)


def _rmsnorm_fwd(name, x, gain, rows):
    n, d = x.shape

    def body(x_ref, g_ref, o_ref):
        xv = x_ref[...]
        rstd = lax.rsqrt(jnp.mean(xv * xv, axis=1, keepdims=True) + EPS)
        o_ref[...] = (xv * rstd * g_ref[...]).astype(BF16)

    return pl.pallas_call(
        body, name=name, grid=(n // rows,),
        in_specs=[pl.BlockSpec((rows, d), lambda i: (i, 0)), pl.BlockSpec((1, d), lambda i: (0, 0))],
        out_specs=pl.BlockSpec((rows, d), lambda i: (i, 0)),
        out_shape=jax.ShapeDtypeStruct((n, d), BF16), compiler_params=_params(("parallel",)))(x, gain)


def _folded_rows(first, rows, d):
    if d == 1:
        return pl.ds(pl.multiple_of(first, rows), rows)
    mlen = SEQ // d
    return pl.ds((first % mlen) * d + first // mlen, rows, stride=d)


def _prenorm_fold(x, gain, dep=None):
    rows = 128
    nchunk = D_MODEL // 128
    dep_specs, dep_args = _dep_operand(dep)

    def body(*refs):
        x_refs, g_ref, hs_ref, hst_ref = refs[:nchunk], refs[nchunk], refs[-2], refs[-1]
        first = pl.program_id(0) * rows
        for p, d in enumerate(DILATIONS):
            idx = _folded_rows(first, rows, d)
            xv = jnp.concatenate([r[idx, :] for r in x_refs], axis=1)
            rstd = lax.rsqrt(jnp.mean(xv * xv, axis=1, keepdims=True) + EPS)
            h = xv * rstd * g_ref[...]
            hs_ref[p] = h.astype(BF16)
            hst_ref[p] = h.T.astype(BF16)

    x_specs = [pl.BlockSpec((SEQ, 128), functools.partial(lambda c, i: (0, c), c)) for c in range(nchunk)]
    return pl.pallas_call(
        body, name="prenorm", grid=(SEQ // rows,),
        in_specs=x_specs + [pl.BlockSpec((1, D_MODEL), lambda i: (0, 0))] + dep_specs,
        out_specs=[pl.BlockSpec((3, rows, D_MODEL), lambda i: (0, i, 0)),
                   pl.BlockSpec((3, D_MODEL, rows), lambda i: (0, 0, i))],
        out_shape=[jax.ShapeDtypeStruct((3, SEQ, D_MODEL), BF16), jax.ShapeDtypeStruct((3, D_MODEL, SEQ), BF16)],
        compiler_params=_params(("parallel",)))(*([x] * nchunk), gain, *dep_args)


def _prenorm_bwd(x, gain, dh, dout):
    rows = 256

    def body(x_ref, g_ref, a_ref, do_ref, dx_ref, gg_ref):
        xv = x_ref[...]
        rstd = lax.rsqrt(jnp.mean(xv * xv, axis=1, keepdims=True) + EPS)
        xn = xv * rstd
        dh = jnp.concatenate([a_ref[c] for c in range(D_MODEL // 128)], axis=1)
        gdh = dh * g_ref[...]
        dx_ref[...] = rstd * (gdh - xn * jnp.mean(gdh * xn, axis=1, keepdims=True)) + do_ref[...]

        @pl.when(pl.program_id(0) == 0)
        def _():
            gg_ref[...] = jnp.zeros((1, D_MODEL), F32)

        gg_ref[...] += jnp.sum(dh * xn, axis=0, keepdims=True)

    row = pl.BlockSpec((rows, D_MODEL), lambda i: (i, 0))
    vec = pl.BlockSpec((1, D_MODEL), lambda i: (0, 0))
    return pl.pallas_call(
        body, name="prenorm_bwd", grid=(SEQ // rows,),
        in_specs=[row, vec, pl.BlockSpec((D_MODEL // 128, rows, 128), lambda i: (0, i, 0)), row], out_specs=[row, vec],
        out_shape=[jax.ShapeDtypeStruct((SEQ, D_MODEL), F32), jax.ShapeDtypeStruct((1, D_MODEL), F32)],
        compiler_params=_params(("arbitrary",)))(x, gain, dh, dout)


def _memnorm_bwd(mem, dmemn, dep=None):
    dep_specs, dep_args = _dep_operand(dep)

    def body(m_ref, d_ref, *rest):
        mv = m_ref[...]
        rstd = lax.rsqrt(jnp.mean(mv * mv, axis=1, keepdims=True) + EPS)
        rest[-1][...] = jnp.sum(d_ref[...] * mv * rstd, axis=0, keepdims=True)

    whole = pl.BlockSpec(memory_space=pltpu.VMEM)
    return pl.pallas_call(
        body, name="memnorm_bwd", in_specs=[whole, whole] + dep_specs,
        out_shape=jax.ShapeDtypeStruct((1, D_MODEL), F32), compiler_params=_params())(mem, dmemn, *dep_args)


def _dep_operand(dep):
    return ([], []) if dep is None else ([pl.BlockSpec(memory_space=pl.ANY)], [dep])


def _in_proj(name, hs, wt, tabs, order, prev=None, dep=None):
    tm, tn = 512, 512
    prev_specs, prev_args = ([], []) if prev is None else ([ANY], [prev])
    dep_specs, dep_args = _dep_operand(dep)

    def body(order_ref, h_ref, w_ref, t_ref, *rest):
        o_ref = rest[-1]
        j = order_ref[pl.program_id(0)]
        is_rope = jnp.logical_and(j < 9, j % 3 != 2)
        row_slices = [slice(r * tm, (r + 1) * tm) for r in range(SEQ // tm)]

        def product(rs):
            return lax.dot_general(h_ref[rs, :], w_ref[...], NT, preferred_element_type=F32)

        @pl.when(is_rope)
        def _():
            for rs in row_slices:
                acc = product(rs)
                c, s1, s2 = t_ref[0, rs, :], t_ref[1, rs, :], t_ref[2, rs, :]
                for q in range(tn // 128):
                    a = acc[:, q * 128:(q + 1) * 128]
                    o_ref[rs, q * 128:(q + 1) * 128] = _rope(a, c, s1, s2).astype(BF16)

        @pl.when(jnp.logical_not(is_rope))
        def _():
            for rs in row_slices:
                o_ref[rs, :] = product(rs).astype(BF16)

    grid_spec = pltpu.PrefetchScalarGridSpec(
        num_scalar_prefetch=1, grid=(order.shape[0],),
        in_specs=[pl.BlockSpec((None, SEQ, D_MODEL), lambda t, o: (_perm_of_block(o[t]), 0, 0)),
                  pl.BlockSpec((tn, D_MODEL), lambda t, o: (o[t], 0)),
                  pl.BlockSpec((None, 3, SEQ, 128), lambda t, o: (_perm_of_block(o[t]), 0, 0, 0))] + prev_specs
        + dep_specs,
        out_specs=pl.BlockSpec((SEQ, tn), lambda t, o: (0, o[t])))
    return pl.pallas_call(
        body, name=name, grid_spec=grid_spec, out_shape=jax.ShapeDtypeStruct((SEQ, N_IN), BF16),
        input_output_aliases={} if prev is None else {4: 0},
        compiler_params=_params(("arbitrary",)))(order, hs, wt, tabs, *prev_args, *dep_args)


def _piece_blocks(pieces):
    return [(a, h * 512) for a, p in enumerate(pieces) for h in range(p.shape[1] // 512)]


def _block_fetch(piece_refs, blocks, buf, sem):
    def start(block, slot):
        for b, (a, col) in enumerate(blocks):
            @pl.when(block == b)
            def _():
                pltpu.make_async_copy(piece_refs[a].at[:, pl.ds(col, 512)], buf.at[slot], sem.at[slot]).start()

    def wait(slot):
        pltpu.make_async_copy(piece_refs[0].at[:, pl.ds(0, 512)], buf.at[slot], sem.at[slot]).wait()

    return start, wait


def _in_proj_dw(pieces, hst, dep=None):
    tn = 512
    blocks = _piece_blocks(pieces)
    nblk = len(blocks)
    npc = len(pieces)
    dep_specs, dep_args = _dep_operand(dep)

    def body(h_ref, *rest):
        piece_refs = rest[:npc]
        own_out, mirror, buf, sem, out_buf, send_sems, recv_sem, local_sems = rest[-8:]
        j = pl.program_id(0)
        slot = j % 2
        start, wait = _block_fetch(piece_refs, blocks, buf, sem)
        x, y, c = _place()

        def rows_of(step):
            return pl.ds(pl.multiple_of(step * tn, tn), tn)

        def to_sibling(step, slot_):
            return pltpu.make_async_remote_copy(
                src_ref=out_buf.at[slot_], dst_ref=mirror.at[rows_of(step)],
                send_sem=send_sems.at[slot_], recv_sem=recv_sem, device_id=(x, y, 1 - c), device_id_type=MESH_ID)

        def to_own(step, slot_):
            return pltpu.make_async_copy(out_buf.at[slot_], own_out.at[rows_of(step)], local_sems.at[slot_])

        @pl.when(j == 0)
        def _():
            start(j, slot)

        wait(slot)

        @pl.when(j + 1 < nblk)
        def _():
            start(j + 1, 1 - slot)

        acc = jnp.dot(h_ref[...], buf[slot], preferred_element_type=F32)

        @pl.when(j >= 2)
        def _():
            to_sibling(j - 2, slot).wait_send()
            to_own(j - 2, slot).wait()

        out_buf[slot] = acc.T.astype(BF16)
        to_sibling(j, slot).start()
        to_own(j, slot).start()

        @pl.when(j == nblk - 1)
        def _():
            to_sibling(j - 1, 1 - slot).wait_send()
            to_own(j - 1, 1 - slot).wait()
            to_sibling(j, slot).wait_send()
            to_own(j, slot).wait()
            pltpu.make_async_remote_copy(src_ref=mirror, dst_ref=mirror, send_sem=send_sems.at[0], recv_sem=recv_sem,
                                         device_id=(x, y, 1 - c), device_id_type=MESH_ID).wait_recv()

    return pl.pallas_call(
        body, name="in_proj_dw", grid=(nblk,),
        in_specs=[pl.BlockSpec((None, D_MODEL, SEQ), lambda j: (_perm_of_block(j), 0, 0))] + [ANY] * npc + dep_specs,
        out_specs=[ANY, ANY],
        out_shape=[jax.ShapeDtypeStruct((N_IN, D_MODEL), BF16), jax.ShapeDtypeStruct((N_IN, D_MODEL), BF16)],
        scratch_shapes=[pltpu.VMEM((2, SEQ, tn), BF16), pltpu.SemaphoreType.DMA((2,)),
                        pltpu.VMEM((2, tn, D_MODEL), BF16), pltpu.SemaphoreType.DMA((2,)), pltpu.SemaphoreType.DMA,
                        pltpu.SemaphoreType.DMA((2,))],
        compiler_params=_params(("arbitrary",)))(hst, *pieces, *dep_args)


def _in_proj_dh(pieces, wt, dep=None):
    tk = 512
    blocks = _piece_blocks(pieces)
    nblk = len(blocks)
    npc = len(pieces)
    nchunk = D_MODEL // 128

    def col(s):
        return jnp.where(s < 3, s, jnp.where(s < 16, s + 6, s - 13))

    dep_specs, dep_args = _dep_operand(dep)

    def body(w_ref, *rest):
        piece_refs = rest[:npc]
        o_ref, acc_ref, buf, sem = rest[-4:]
        s = pl.program_id(0)
        slot = s % 2
        start, wait = _block_fetch(piece_refs, blocks, buf, sem)

        @pl.when(s == 0)
        def _():
            start(col(s), slot)

        wait(slot)

        @pl.when(s + 1 < nblk)
        def _():
            start(col(s + 1), 1 - slot)

        row_slices = [slice(r * 512, (r + 1) * 512) for r in range(SEQ // 512)]

        def product(rs):
            return jnp.dot(buf[slot, rs, :], w_ref[...], preferred_element_type=F32)

        def accumulate(cond, to_out, init):
            @pl.when(cond)
            def _():
                for rs in row_slices:
                    prod = product(rs)
                    if not to_out:
                        if init:
                            acc_ref[rs, :] = prod
                        else:
                            acc_ref[rs, :] += prod
                        continue
                    for c in range(nchunk):
                        if init:
                            o_ref[c, rs, :] = prod[:, c * 128:(c + 1) * 128]
                        else:
                            o_ref[c, rs, :] += prod[:, c * 128:(c + 1) * 128]

        accumulate(s == 0, True, True)
        accumulate(jnp.logical_and(s > 0, s < 16), True, False)
        accumulate(jnp.logical_or(s == 16, s == 19), False, True)
        accumulate(jnp.logical_and(s > 16, s != 19), False, False)
        for last, d in ((18, 4), (21, 16)):
            @pl.when(s == last)
            def _():
                mlen = SEQ // d
                for r in range(d):
                    for c in range(nchunk):
                        o_ref[c, pl.ds(r, mlen, stride=d), :] += acc_ref[r * mlen:(r + 1) * mlen,
                                                                         c * 128:(c + 1) * 128]

    return pl.pallas_call(
        body, name="in_proj_dh", grid=(nblk,),
        in_specs=[pl.BlockSpec((tk, D_MODEL), lambda s: (col(s), 0))] + [ANY] * npc + dep_specs,
        out_specs=pl.BlockSpec((nchunk, SEQ, 128), lambda s: (0, 0, 0)),
        out_shape=jax.ShapeDtypeStruct((nchunk, SEQ, 128), F32),
        scratch_shapes=[pltpu.VMEM((SEQ, D_MODEL), F32), pltpu.VMEM((2, SEQ, tk), BF16),
                        pltpu.SemaphoreType.DMA((2,))],
        compiler_params=_params(("arbitrary",)))(wt, *pieces, *dep_args)


def _head_lanes(lanes, hh):
    return lanes >= 64 if hh == 1 else lanes < 64


def _head_rows(x, lanes, hh, pair):
    if not pair:
        return jnp.max(x, axis=1, keepdims=True)
    return jnp.max(jnp.where(_head_lanes(lanes, hh), x, -jnp.inf), axis=1, keepdims=True)


def _mask_head(x, lanes, hh, pair, scale=1.0):
    if not pair:
        return x
    xf = x.astype(F32) if scale == 1.0 else x.astype(F32) * scale
    return jnp.where(_head_lanes(lanes, hh), xf, 0.0).astype(BF16)


def _window(mode, qi, tq, mlen, tk):
    if mode == "dil":
        q0 = qi * tq
        seg = (q0 // mlen) * mlen
        ks = jnp.clip(q0 - REACH, seg, seg + mlen - tk)
        return pl.multiple_of(ks, 64)
    if mode == "na":
        r_start = jnp.clip(qi - NA_ROWS // 2, 0, SEQ // GRID_W - NA_ROWS)
        return pl.multiple_of(r_start * GRID_W, 64)
    return 0


def _band_mask(qi, tq, tk, ks):
    qpos = qi * tq + _iota((tq, tk), 0)
    kpos = ks + _iota((tq, tk), 1)
    return jnp.where(jnp.abs(qpos - kpos) <= REACH, 0.0, NEG).astype(F32)


def _stack_heads(x, lanes, pair, scale=1.0):
    if not pair:
        return x
    return jnp.concatenate([_mask_head(x, lanes, hh, pair, scale) for hh in range(2)], axis=0)


def _stack_rows(x, lanes, pair):
    if not pair:
        return _head_rows(x, lanes, 0, pair)
    return jnp.concatenate([_head_rows(x, lanes, hh, pair) for hh in range(2)], axis=0)


def _unstack_heads(x, lanes, pair, tq):
    if not pair:
        return x
    return jnp.where(lanes < 64, x[:tq], x[tq:])


def _scores(mode, qst, k, sscale, band, qi, bias_ref, pair):
    s = lax.dot_general(qst, k, NT, preferred_element_type=F32)
    if sscale != 1.0:
        s = s * sscale
    if mode == "dil":
        s = s + jnp.concatenate([band, band], axis=0)
    elif mode == "na":
        off = qi - jnp.clip(qi - NA_ROWS // 2, 0, SEQ // GRID_W - NA_ROWS)
        s = s + jnp.concatenate([bias_ref[0, off], bias_ref[1, off]], axis=0)
    return s


def _attn_cfg(mode, d):
    if mode == "dil":
        mlen = SEQ // d
        return dict(pair=True, tq=128, tk=min(256, mlen), mlen=mlen, lk=SEQ, scale=HEAD_DIM ** -0.5, units=4,
                    nsub=ATTN_SUBTILES)
    if mode == "na":
        return dict(pair=True, tq=GRID_W, tk=NA_ROWS * GRID_W, mlen=SEQ, lk=SEQ, scale=HEAD_DIM ** -0.5, units=4,
                    nsub=ATTN_SUBTILES)
    return dict(pair=False, tq=128, tk=MEM_LEN, mlen=SEQ, lk=MEM_LEN, scale=128 ** -0.5, units=4,
                nsub=ATTN_SUBTILES)


ATTN_SUBTILES = 16


def _attn_fwd(name, mode, q_arr, k_arr, v_arr, qcol, kcol, vcol, d=1, bias=None):
    cfg = _attn_cfg(mode, d)
    pair, tq, tk, mlen, lk, scale = cfg["pair"], cfg["tq"], cfg["tk"], cfg["mlen"], cfg["lk"], cfg["scale"]
    qscale, sscale = (scale, 1.0) if pair else (1.0, scale)
    nsub = cfg["nsub"]
    rows = nsub * tq

    def body(*refs):
        if mode == "na":
            q_ref, k_ref, v_ref, bias_ref, o_ref, l_ref = refs
        else:
            q_ref, k_ref, v_ref, o_ref, l_ref = refs
            bias_ref = None
        lanes = _iota((tq, 128), 1)
        qis = [pl.program_id(1) * nsub + sub for sub in range(nsub)]
        kss = [_window(mode, qi, tq, mlen, tk) for qi in qis]
        vs = [v_ref[pl.ds(ks, tk), :] for ks in kss]
        bands = [_band_mask(qi, tq, tk, ks) if mode == "dil" else None for qi, ks in zip(qis, kss)]
        ss = []
        for sub in range(nsub):
            qst = _stack_heads(q_ref[sub * tq:(sub + 1) * tq, :], lanes, pair, qscale)
            k = k_ref[pl.ds(kss[sub], tk), :]
            ss.append(_scores(mode, qst, k, sscale, bands[sub], qis[sub], bias_ref, pair))
        ms = [jnp.max(s_, axis=1, keepdims=True) for s_ in ss]
        ps = [jnp.exp(s_ - m) for s_, m in zip(ss, ms)]
        ls = [jnp.sum(p, axis=1, keepdims=True) for p in ps]
        os_ = [jnp.dot(p.astype(BF16), v, preferred_element_type=F32) for p, v in zip(ps, vs)]
        for sub in range(nsub):
            out = _unstack_heads(os_[sub] / ls[sub], lanes, pair, tq)
            lse = ms[sub] + jnp.log(ls[sub])
            lse = _unstack_heads(jnp.broadcast_to(lse, (lse.shape[0], 128)), lanes, pair, tq)
            dst = _folded_rows(qis[sub] * tq, tq, d) if mode == "dil" else slice(sub * tq, (sub + 1) * tq)
            o_ref[dst, :] = out
            l_ref[dst, :] = lse

    in_specs = [pl.BlockSpec((rows, 128), lambda u, i: (i, qcol + u)),
                pl.BlockSpec((lk, 128), lambda u, i: (0, kcol + u)),
                pl.BlockSpec((lk, 128), lambda u, i: (0, vcol + u))]
    args = [q_arr, k_arr, v_arr]
    if mode == "na":
        in_specs.append(pl.BlockSpec((2, NA_ROWS, GRID_W, NA_ROWS * GRID_W), lambda u, i: (u, 0, 0, 0)))
        args.append(bias)
    if mode == "dil":
        out_spec = pl.BlockSpec((SEQ, 128), lambda u, i: (0, u))
    else:
        out_spec = pl.BlockSpec((rows, 128), lambda u, i: (i, u))
    return pl.pallas_call(
        body, name=name, grid=(cfg["units"], SEQ // rows), in_specs=in_specs, out_specs=[out_spec, out_spec],
        out_shape=[jax.ShapeDtypeStruct((SEQ, 512), F32), jax.ShapeDtypeStruct((SEQ, 512), F32)],
        compiler_params=_params(("parallel", "arbitrary")))(*args)


def _attn_bwd(name, mode, q_arr, k_arr, v_arr, qcol, kcol, vcol, do, lse, dp=None, o=None, d=1, bias=None,
              tabs=None):
    cfg = _attn_cfg(mode, d)
    pair, tq, tk, mlen, lk, scale = cfg["pair"], cfg["tq"], cfg["tk"], cfg["mlen"], cfg["lk"], cfg["scale"]
    qscale, sscale = (scale, 1.0) if pair else (1.0, scale)
    nsub = cfg["nsub"]
    rows = nsub * tq
    nq = SEQ // rows
    kv_dtype = F32 if mode == "mem" else BF16

    def body(*refs):
        refs = list(refs)
        q_ref, k_ref, v_ref, do_ref, l_ref = refs[:5]
        rest = refs[5:]
        bias_ref = tq_ref = tk_ref = db_ref = None
        if mode == "dil":
            dp_ref, tq_ref, tk_ref, dq_ref, dk_ref, dv_ref, dk_acc, dv_acc = rest
        elif mode == "na":
            o_ref, bias_ref, dq_ref, dk_ref, dv_ref, db_ref, dk_acc, dv_acc = rest
        else:
            o_ref, dq_ref, dk_ref, dv_ref, dk_acc, dv_acc = rest
        step = pl.program_id(1)

        @pl.when(step == 0)
        def _():
            dk_acc[...] = jnp.zeros((lk, 128), F32)
            dv_acc[...] = jnp.zeros((lk, 128), F32)
            if mode == "na":
                db_ref[...] = jnp.zeros(db_ref.shape, F32)

        lanes = _iota((tq, 128), 1)
        qis = [step * nsub + sub for sub in range(nsub)]
        sls = [slice(sub * tq, (sub + 1) * tq) for sub in range(nsub)]
        kss = [_window(mode, qi, tq, mlen, tk) for qi in qis]
        ks_ = [k_ref[pl.ds(ks, tk), :] for ks in kss]
        vs = [v_ref[pl.ds(ks, tk), :] for ks in kss]
        qsts, dosts, lses, dphs = [], [], [], []
        for sub in range(nsub):
            if mode == "dil":
                src = _folded_rows(qis[sub] * tq, tq, d)
                dov = do_ref[src, :].astype(BF16)
                lsev = l_ref[src, :]
                dphs.append(_stack_rows(dp_ref[src, :], lanes, pair))
            else:
                dov = do_ref[sls[sub], :]
                lsev = l_ref[sls[sub], :]
                dpv = dov.astype(F32) * o_ref[sls[sub], :]
                if pair:
                    dphs.append(jnp.concatenate(
                        [jnp.sum(jnp.where(_head_lanes(lanes, hh), dpv, 0.0), axis=1, keepdims=True)
                         for hh in range(2)], axis=0))
                else:
                    dphs.append(jnp.sum(dpv, axis=1, keepdims=True))
            qsts.append(_stack_heads(q_ref[sls[sub], :], lanes, pair, qscale))
            dosts.append(_stack_heads(dov, lanes, pair))
            lses.append(_stack_rows(lsev, lanes, pair))
        bands = [_band_mask(qi, tq, tk, ks) if mode == "dil" else None for qi, ks in zip(qis, kss)]
        ss = [_scores(mode, qsts[sub], ks_[sub], sscale, bands[sub], qis[sub], bias_ref, pair) for sub in range(nsub)]
        dpms = [lax.dot_general(dosts[sub], vs[sub], NT, preferred_element_type=F32) for sub in range(nsub)]
        ps = [jnp.exp(s_ - lse) for s_, lse in zip(ss, lses)]
        dss = [p * (dpm - dph) for p, dpm, dph in zip(ps, dpms, dphs)]
        if mode == "na":
            for sub, ds in enumerate(dss):
                off = qis[sub] - jnp.clip(qis[sub] - NA_ROWS // 2, 0, SEQ // GRID_W - NA_ROWS)
                db_ref[0, off] += ds[:tq]
                db_ref[1, off] += ds[tq:]
        dsbs = [ds.astype(BF16) for ds in dss]
        dvs = [lax.dot_general(p.astype(BF16), dosts[sub], TN, preferred_element_type=F32)
               for sub, p in enumerate(ps)]
        dqs = [jnp.dot(dsb, ks_[sub], preferred_element_type=F32) * scale for sub, dsb in enumerate(dsbs)]
        dks = [lax.dot_general(dsb, qsts[sub], TN, preferred_element_type=F32) for sub, dsb in enumerate(dsbs)]
        for sub in range(nsub):
            sl = sls[sub]
            dq = _unstack_heads(dqs[sub], lanes, pair, tq)
            if mode == "dil":
                dq = _rope_t(dq, tq_ref[0, sl, :], tq_ref[1, sl, :], tq_ref[2, sl, :])
            dq_ref[sl, :] = dq.astype(BF16)
            dk_acc[pl.ds(kss[sub], tk), :] += dks[sub] if pair else dks[sub] * scale
            dv_acc[pl.ds(kss[sub], tk), :] += dvs[sub]

        @pl.when(step == nq - 1)
        def _():
            dkv = dk_acc[...]
            if mode == "dil":
                dkv = _rope_t(dkv, tk_ref[0], tk_ref[1], tk_ref[2])
            dk_ref[...] = dkv.astype(kv_dtype)
            dv_ref[...] = dv_acc[...].astype(kv_dtype)

    q_spec = pl.BlockSpec((rows, 128), lambda u, i: (i, qcol + u))
    row_spec = pl.BlockSpec((rows, 128), lambda u, i: (i, u))
    kv_out = pl.BlockSpec((lk, 128), lambda u, i: (0, u))
    whole = pl.BlockSpec((SEQ, 128), lambda u, i: (0, u))
    nat_spec = whole if mode == "dil" else row_spec
    in_specs = [q_spec,
                pl.BlockSpec((lk, 128), lambda u, i: (0, kcol + u)),
                pl.BlockSpec((lk, 128), lambda u, i: (0, vcol + u)),
                nat_spec, nat_spec]
    args = [q_arr, k_arr, v_arr, do, lse]
    out_specs = [row_spec, kv_out, kv_out]
    out_shape = [jax.ShapeDtypeStruct((SEQ, 512), BF16), jax.ShapeDtypeStruct((lk, 512), kv_dtype),
                 jax.ShapeDtypeStruct((lk, 512), kv_dtype)]
    if mode == "dil":
        in_specs += [whole, pl.BlockSpec((3, rows, 128), lambda u, i: (0, i, 0)),
                     pl.BlockSpec((3, SEQ, 128), lambda u, i: (0, 0, 0))]
        args += [dp, tabs, tabs]
    elif mode == "na":
        b_spec = pl.BlockSpec((2, NA_ROWS, GRID_W, NA_ROWS * GRID_W), lambda u, i: (u, 0, 0, 0))
        in_specs += [row_spec, b_spec]
        args += [o, bias]
        out_specs.append(b_spec)
        out_shape.append(jax.ShapeDtypeStruct((8, NA_ROWS, GRID_W, NA_ROWS * GRID_W), F32))
    else:
        in_specs.append(row_spec)
        args.append(o)
    return pl.pallas_call(
        body, name=name, grid=(cfg["units"], nq), in_specs=in_specs, out_specs=out_specs, out_shape=out_shape,
        scratch_shapes=[pltpu.VMEM((lk, 128), F32), pltpu.VMEM((lk, 128), F32)],
        compiler_params=_params(("parallel", "arbitrary")))(*args)


def _na_geometry():
    qc = _iota((GRID_W, 128), 0)
    lane = _iota((GRID_W, 128), 1)
    kc = lane & 63
    c_start = jnp.clip(qc - 8, 0, GRID_W - 16)
    valid = jnp.logical_and(kc >= c_start, kc < c_start + 16)
    return lane, valid


def _na_bias(rpb_rows, dep=None):
    dep_specs, dep_args = _dep_operand(dep)

    def body(r_ref, *rest):
        o_ref, t_ref = rest[-2:]
        lane, valid = _na_geometry()
        for dd in range(14):
            row_a = jnp.broadcast_to(r_ref[dd:dd + 1, :], (GRID_W, 128))
            row_b = jnp.broadcast_to(r_ref[dd + 1:dd + 2, :], (GRID_W, 128))
            both = jnp.where(lane < 64, row_a, pltpu.roll(row_b, 64, 1))
            t = pltpu.roll(both, 128 - 15, 1, stride=1, stride_axis=0)
            t_ref[dd] = jnp.where(valid, t, NEG)
        for off in range(NA_ROWS):
            for p in range(4):
                o_ref[off, :, p * 128:(p + 1) * 128] = t_ref[2 * p - off + 7]

    return pl.pallas_call(
        body, name="na_bias", grid=(8,),
        in_specs=[pl.BlockSpec((None, 16, 128), lambda h: (h, 0, 0))] + dep_specs,
        out_specs=pl.BlockSpec((None, NA_ROWS, GRID_W, NA_ROWS * GRID_W), lambda h: (h, 0, 0, 0)),
        out_shape=jax.ShapeDtypeStruct((8, NA_ROWS, GRID_W, NA_ROWS * GRID_W), F32),
        scratch_shapes=[pltpu.VMEM((14, GRID_W, 128), F32)],
        compiler_params=_params(("parallel",)))(rpb_rows, *dep_args)


def _na_bias_bwd(dbias, dep=None):
    dep_specs, dep_args = _dep_operand(dep)

    def body(d_ref, *rest):
        o_ref = rest[-1]
        lane, valid = _na_geometry()
        reverse = (_iota((GRID_W, GRID_W), 0) + _iota((GRID_W, GRID_W), 1) == GRID_W - 1).astype(F32)
        o_ref[...] = jnp.zeros((16, 128), F32)
        for dd in range(14):
            t = jnp.zeros((GRID_W, 128), F32)
            for off in range(NA_ROWS):
                for p in range(4):
                    if 2 * p - off + 7 == dd:
                        t = t + d_ref[off, :, p * 128:(p + 1) * 128]
            t = jnp.dot(reverse, jnp.where(valid, t, 0.0), precision=lax.Precision.HIGHEST,
                        preferred_element_type=F32)
            t = pltpu.roll(t, 128 - (GRID_W - 16), 1, stride=1, stride_axis=0)
            o_ref[dd:dd + 1, :] = jnp.sum(t, axis=0, keepdims=True)

    return pl.pallas_call(
        body, name="na_bias_bwd", grid=(8,),
        in_specs=[pl.BlockSpec((None, NA_ROWS, GRID_W, NA_ROWS * GRID_W), lambda h: (h, 0, 0, 0))] + dep_specs,
        out_specs=pl.BlockSpec((None, 16, 128), lambda h: (h, 0, 0)),
        out_shape=jax.ShapeDtypeStruct((8, 16, 128), F32),
        compiler_params=_params(("parallel",)))(dbias, *dep_args)


GATE_ROWS = 128


def _group_weights(l0, l1, l2):
    m = jnp.maximum(jnp.maximum(l0, l1), l2)
    e0, e1, e2 = jnp.exp(l0 - m), jnp.exp(l1 - m), jnp.exp(l2 - m)
    inv = 1.0 / (e0 + e1 + e2)
    return e0 * inv, e1 * inv, e2 * inv


def _gate_block(o_grp, l_grp, out_b, out_c, parts, x, target, merge_bias, wts, w_out, gain, head_sum):
    rows = GATE_ROWS
    r512 = pl.BlockSpec((rows, 512), lambda i: (i, 0))
    r1024 = pl.BlockSpec((rows, D_MODEL), lambda i: (i, 0))
    silu_cols = [pl.BlockSpec((rows, 512), functools.partial(lambda b, i: (i, b), 13 + b)) for b in range(3)]
    logit_cols = [pl.BlockSpec((rows, D_MODEL), functools.partial(lambda b, i: (i, b), 8 + b)) for b in range(3)]

    def body(o0, o1, o2, l0, l1, l2, ob, oc, ga, gb, gc, la, lb, lc, x_ref, t_ref, mb, wa, wb, wc, wo_ref, gn_ref,
             hs_ref, dout_ref, dla, dlb, dlc, dga, dgb, dgc, do0, do1, do2, dp0, dp1, dp2, dob, doc, err_ref, gg_ref,
             gmb, gwa, gwb, gwc, gwo, acc_a, acc_b, acc_c, acc_o):
        step = pl.program_id(0)
        ws = _group_weights(l0[...], l1[...], l2[...])
        out_a = ws[0] * o0[...] + ws[1] * o1[...] + ws[2] * o2[...]
        branches = ((out_a, ga, la, wa, acc_a, dla, dga), (ob[...], gb, lb, wb, acc_b, dlb, dgb),
                    (oc[...], gc, lc, wc, acc_c, dlc, dgc))

        @pl.when(step == 0)
        def _():
            for acc in (acc_a, acc_b, acc_c, acc_o):
                acc[...] = jnp.zeros(acc.shape, F32)
            err_ref[...] = jnp.zeros((1, D_MODEL), F32)
            gg_ref[...] = jnp.zeros((1, D_MODEL), F32)
            gmb[...] = jnp.zeros((3, D_MODEL), F32)

        y = jnp.zeros((rows, D_MODEL), F32)
        zs, gates, silus, dsilus, us = [], [], [], [], []
        for b, (ov, g_ref, l_ref, w_ref, _, _, _) in enumerate(branches):
            g = g_ref[...].astype(F32)
            sg = _sigmoid(g)
            silus.append(g * sg)
            dsilus.append(sg * (1.0 + g * (1.0 - sg)))
            us.append((ov * silus[b]).astype(BF16))
            zs.append(lax.dot_general(us[b], w_ref[...], NT, preferred_element_type=F32))
            gates.append(_sigmoid(l_ref[...].astype(F32) + mb[b:b + 1, :]))
            y = y + gates[b] * zs[b]
        yb = y.astype(BF16)
        y2 = jnp.dot(yb, wo_ref[...], preferred_element_type=F32)
        rstd = lax.rsqrt(jnp.mean(y2 * y2, axis=1, keepdims=True) + EPS)
        yn = y2 * rstd
        gv = gn_ref[...]
        err = x_ref[...] + yn * gv - t_ref[...]
        dout = err * (1.0 / D_MODEL)
        dout_ref[...] = dout
        dn = dout * gv
        dy2 = (rstd * (dn - yn * jnp.mean(dn * yn, axis=1, keepdims=True))).astype(BF16)
        acc_o[...] += lax.dot_general(yb, dy2, TN, preferred_element_type=F32)
        err_ref[...] += jnp.sum(err * err, axis=0, keepdims=True)
        gg_ref[...] += jnp.sum(dout * yn, axis=0, keepdims=True)
        dy = lax.dot_general(dy2, wo_ref[...], NT, preferred_element_type=F32)
        dos = []
        for b, (ov, _, _, w_ref, acc, dl_ref, dg_ref) in enumerate(branches):
            dl = dy * zs[b] * gates[b] * (1.0 - gates[b])
            dl_ref[...] = dl.astype(BF16)
            gmb[b:b + 1, :] += jnp.sum(dl, axis=0, keepdims=True)
            dz = (dy * gates[b]).astype(BF16)
            acc[...] += lax.dot_general(dz, us[b], TN, preferred_element_type=F32)
            du = jnp.dot(dz, w_ref[...], preferred_element_type=F32)
            dos.append(du * silus[b])
            dg_ref[...] = (du * ov * dsilus[b]).astype(BF16)
        dob[...] = dos[1].astype(BF16)
        doc[...] = dos[2].astype(BF16)
        row_term = jnp.dot(dos[0] * out_a, hs_ref[...], precision=lax.Precision.HIGHEST, preferred_element_type=F32)
        for wg, do_ref, dp_ref in zip(ws, (do0, do1, do2), (dp0, dp1, dp2)):
            do_ref[...] = wg * dos[0]
            dp_ref[...] = wg * row_term

        @pl.when(step == SEQ // rows - 1)
        def _():
            for acc, out in ((acc_a, gwa), (acc_b, gwb), (acc_c, gwc), (acc_o, gwo)):
                out[...] = acc[...].astype(BF16)

    full = lambda shape: pl.BlockSpec(shape, lambda i: (0,) * len(shape))
    vec = pl.BlockSpec((1, D_MODEL), lambda i: (0, 0))
    acc3 = pl.BlockSpec((3, D_MODEL), lambda i: (0, 0))
    in_specs = ([r512] * 8 + silu_cols + logit_cols + [r1024, r1024, full((3, D_MODEL))]
                + [full((D_MODEL, 512))] * 3 + [full((D_MODEL, D_MODEL)), vec, full((512, 512))])
    out_specs = ([r1024] + [r1024] * 3 + [r512] * 3 + [r512] * 6 + [r512] * 2 + [vec, vec, acc3]
                 + [full((D_MODEL, 512))] * 3 + [full((D_MODEL, D_MODEL))])
    bf, f32 = BF16, F32
    sds = jax.ShapeDtypeStruct
    out_shape = ([sds((SEQ, D_MODEL), f32)] + [sds((SEQ, D_MODEL), bf)] * 3 + [sds((SEQ, 512), bf)] * 3
                 + [sds((SEQ, 512), f32)] * 6 + [sds((SEQ, 512), bf)] * 2 + [sds((1, D_MODEL), f32)] * 2
                 + [sds((3, D_MODEL), f32)] + [sds((D_MODEL, 512), bf)] * 3 + [sds((D_MODEL, D_MODEL), bf)])
    res = pl.pallas_call(
        body, name="gate_block", grid=(SEQ // rows,), in_specs=in_specs, out_specs=out_specs, out_shape=out_shape,
        scratch_shapes=[pltpu.VMEM((D_MODEL, 512), F32)] * 3 + [pltpu.VMEM((D_MODEL, D_MODEL), F32)],
        compiler_params=_params(("arbitrary",)))(
            *o_grp, *l_grp, out_b, out_c, parts, parts, parts, parts, parts, parts, x, target, merge_bias, *wts, w_out,
            gain, head_sum)
    return dict(dout=res[0], dlog=res[1:4], dg=res[4:7], do_grp=res[7:10], dp_grp=res[10:13], do_b=res[13],
                do_c=res[14], err_sq=res[15], g_post=res[16], g_mb=res[17], g_wt=res[18:21], g_w_out=res[21])


def _local_step(x, hst, parts, tabs, bias, mem, target, pre_norm, mem_norm, post_norm, wt_in, late_weights,
                reduce_start=None):
    o_grp, l_grp = [], []
    for g, d in enumerate(DILATIONS):
        o, l = _attn_fwd("dil_fwd_%d" % g, "dil", parts, parts, parts, 12 * g, 12 * g + 4, 12 * g + 8, d=d)
        o_grp.append(o)
        l_grp.append(l)
    out_b, lse_b = _attn_fwd("na_fwd", "na", parts, parts, parts, 36, 40, 44, bias=bias)
    merge_bias, w_kv, wt_a, wt_b, wt_c, w_out = late_weights(sum(a[:8, :128] for a in [out_b] + o_grp))
    memn = _rmsnorm_fwd("memnorm", mem, mem_norm, MEM_LEN)
    kv_m = _mm_simple("mem_kv", memn, w_kv, NN, BF16, MEM_LEN, 512, D_MODEL)
    out_c, lse_c = _attn_fwd("mem_fwd", "mem", parts, kv_m, kv_m, 48, 0, 4)

    rr = _iota((512, 512), 0) // HEAD_DIM
    cc = _iota((512, 512), 1) // HEAD_DIM
    head_sum = (rr == cc).astype(F32)
    gb = _gate_block(o_grp, l_grp, out_b, out_c, parts, x, target, merge_bias, (wt_a, wt_b, wt_c), w_out, post_norm,
                     head_sum)
    dout, dlog, dg, g_wt, g_w_out = gb["dout"], gb["dlog"], gb["dg"], gb["g_wt"], gb["g_w_out"]
    do_grp, dp_grp, do_b, do_c, g_post, g_mb = (gb["do_grp"], gb["dp_grp"], gb["do_b"], gb["do_c"], gb["g_post"],
                                                gb["g_mb"])
    loss = 0.5 * jnp.sum(gb["err_sq"]) / D_MODEL

    dqkv = []
    for g, d in enumerate(DILATIONS):
        dq, dk, dv = _attn_bwd("dil_bwd_%d" % g, "dil", parts, parts, parts, 12 * g, 12 * g + 4, 12 * g + 8,
                               do_grp[g], l_grp[g], dp=dp_grp[g], d=d, tabs=tabs[g])
        dqkv += [dq, dk, dv]
    dq_b, dk_b, dv_b, dbias = _attn_bwd("na_bwd", "na", parts, parts, parts, 36, 40, 44, do_b, lse_b, o=out_b,
                                        bias=bias)
    dq_c, dk_m, dv_m = _attn_bwd("mem_bwd", "mem", parts, kv_m, kv_m, 48, 0, 4, do_c, lse_c, o=out_c)

    dkv = jnp.concatenate([dk_m, dv_m], axis=1).astype(BF16)
    g_w_kv = _mm_simple("mem_kv_dw", memn, dkv, TN, BF16, D_MODEL, 512, MEM_LEN)
    dmemn = _mm_simple("mem_kv_dx", dkv, w_kv, NT, F32, MEM_LEN, 512, D_MODEL)

    grads = dict(w_kv=g_w_kv, wt_a=g_wt[0], wt_b=g_wt[1], wt_c=g_wt[2], w_out=g_w_out, merge_bias=g_mb,
                 post_norm=g_post)
    dep = None
    if reduce_start is not None:
        reduce_start("rest_sibling", grads)
        dep = reduce_start("rest_chips", grads, sum(a[:8, :128] for a in (dqkv[0], dqkv[3], dqkv[6], dq_b, dq_c)))
    dparts = dqkv + [dq_b, dk_b, dv_b, dq_c] + list(dg) + list(dlog)
    grads["wt_in"] = _in_proj_dw(dparts, hst, dep)
    dep = reduce_start("w_in", grads) if reduce_start is not None else None
    dh = _in_proj_dh(dparts, wt_in, dep)
    if reduce_start is not None:
        dep = reduce_start("w_in_second", grads, dh)
    grad_x, grads["pre_norm"] = _prenorm_bwd(x, pre_norm, dh, dout)
    g_rpb_t = _na_bias_bwd(dbias, dep)
    grads["na_rpb"] = g_rpb_t[:, :15, :31] + jnp.pad(g_rpb_t[:, :14, 64:95], ((0, 0), (1, 0), (0, 0)))
    grads["mem_norm"] = _memnorm_bwd(mem, dmemn, dep)
    return loss, grad_x, grads


ANY = pl.BlockSpec(memory_space=pl.ANY)


def _place():
    return lax.axis_index("x"), lax.axis_index("y"), lax.axis_index("c")


HBM = pl.BlockSpec(memory_space=pltpu.HBM)
SEM = pl.BlockSpec(memory_space=pltpu.SEMAPHORE)
DATAFLOW = pltpu.SideEffectType.DATAFLOW_SIDE_EFFECTING


def _split_copies(kind, srcs, lands, send_sems, recv_sems):
    nt = len(srcs)
    x, y, c = _place()
    copies = []
    if kind == "sibling":
        for q in range(4):
            for t in range(nt):
                k = q * nt + t
                copies.append(pltpu.make_async_remote_copy(
                    src_ref=srcs[t].at[2 * q + 1 - c], dst_ref=lands[t].at[q], send_sem=send_sems.at[k],
                    recv_sem=recv_sems.at[k], device_id=(x, y, 1 - c), device_id_type=MESH_ID))
    elif kind in ("rs_a", "rs_b"):
        half = lands[0].shape[1]
        xn, yn = (1 - x, y, c), (x, 1 - y, c)
        q_xn, q_yn, q_dg = 2 * (1 - x) + y, 2 * x + 1 - y, 2 * (1 - x) + 1 - y
        if kind == "rs_a":
            plan = [(srcs[0].at[q_yn].at[pl.ds(0, half)], 0, yn), (srcs[0].at[q_dg].at[pl.ds(0, half)], 1, yn),
                    (srcs[0].at[q_xn].at[pl.ds(half, half)], 2, xn), (srcs[0].at[q_dg].at[pl.ds(half, half)], 3, xn)]
        else:
            plan = [(srcs[0].at[0], 0, xn), (srcs[0].at[1], 1, yn)]
        for k, (src, slot, to) in enumerate(plan):
            copies.append(pltpu.make_async_remote_copy(
                src_ref=src, dst_ref=lands[0].at[slot], send_sem=send_sems.at[k], recv_sem=recv_sems.at[k],
                device_id=to, device_id_type=MESH_ID))
    elif kind == "gather":
        me = 4 * x + 2 * y + c
        for mask in range(1, 8):
            fx, fy, fc = (mask >> 2) & 1, (mask >> 1) & 1, mask & 1
            to = (1 - x if fx else x, 1 - y if fy else y, 1 - c if fc else c)
            for t in range(nt):
                k = (mask - 1) * nt + t
                copies.append(pltpu.make_async_remote_copy(
                    src_ref=srcs[t], dst_ref=lands[t].at[me], send_sem=send_sems.at[k], recv_sem=recv_sems.at[k],
                    device_id=to, device_id_type=MESH_ID))
    else:
        for s, (tx, ty) in enumerate([(1 - x, y), (x, 1 - y), (1 - x, 1 - y)]):
            for t in range(nt):
                k = s * nt + t
                copies.append(pltpu.make_async_remote_copy(
                    src_ref=srcs[t].at[2 * tx + ty], dst_ref=lands[t].at[s], send_sem=send_sems.at[k],
                    recv_sem=recv_sems.at[k], device_id=(tx, ty, c), device_id_type=MESH_ID))
    return copies


def _split_count(kind, nt):
    return {"gather": 7, "chips": 3, "sibling": 4, "rs_a": 4, "rs_b": 2}[kind] * nt


def _exchange_start(name, kind, srcs, land_shapes, after=None):
    nt = len(srcs)
    n = _split_count(kind, nt)
    dep_specs, dep_args = _dep_operand(after)
    nd = len(dep_args)

    def body(*refs):
        src_refs, land_refs = refs[:nt], refs[nt:2 * nt]
        send_sems, recv_sems = refs[2 * nt + nd], refs[2 * nt + nd + 1]
        token = refs[-1]
        for cp in _split_copies(kind, src_refs, land_refs, send_sems, recv_sems):
            cp.start()
        token[...] = jnp.zeros_like(token)

    lands = [pltpu.with_memory_space_constraint(lax.empty(s.shape, s.dtype), pltpu.HBM) for s in land_shapes]
    res = pl.pallas_call(
        body, name=name,
        out_shape=(pltpu.SemaphoreType.DMA((n,)), pltpu.SemaphoreType.DMA((n,)),
                   *[pltpu.HBM(s.shape, s.dtype) for s in srcs], *[pltpu.HBM(s.shape, s.dtype) for s in land_shapes],
                   jax.ShapeDtypeStruct((8, 128), F32)),
        in_specs=[HBM] * (2 * nt) + dep_specs,
        out_specs=(SEM, SEM, *([HBM] * (2 * nt)), pl.BlockSpec(memory_space=pltpu.VMEM)),
        input_output_aliases={i: 2 + i for i in range(2 * nt)},
        compiler_params=pltpu.CompilerParams(has_side_effects=DATAFLOW))(
            *[pltpu.with_memory_space_constraint(s, pltpu.HBM) for s in srcs], *lands, *dep_args)
    return res[0], res[1], list(res[2:2 + nt]), list(res[2 + nt:2 + 2 * nt]), res[-1]


def _exchange_wait(name, kind, send_sems, recv_sems, srcs, lands, after):
    nt = len(srcs)

    def body(*refs):
        src_refs, land_refs = refs[:nt], refs[nt:2 * nt]
        s_sems, r_sems = refs[2 * nt], refs[2 * nt + 1]
        for cp in _split_copies(kind, src_refs, land_refs, s_sems, r_sems):
            cp.wait_send()
            cp.wait_recv()

    res = pl.pallas_call(
        body, name=name,
        out_shape=tuple(pltpu.HBM(s.shape, s.dtype) for s in list(srcs) + list(lands)),
        in_specs=[HBM] * (2 * nt) + [SEM, SEM, pl.BlockSpec(memory_space=pl.ANY)],
        out_specs=tuple([HBM] * (2 * nt)),
        input_output_aliases={i: i for i in range(2 * nt)},
        compiler_params=pltpu.CompilerParams(has_side_effects=DATAFLOW))(
            *srcs, *lands, send_sems, recv_sems, after)
    return list(res[:nt]), list(res[nt:])


AG_GROUPS = ((0, 3), (3, 4), (7, 2))


def _ag_phase(name, own, land, sems, waits, starts, after=None):
    r = own.shape[0]
    half = r // 2
    ns = len(sems)
    dep_specs, dep_args = _dep_operand(after)
    nd = len(dep_args)
    new_group = None
    if starts:
        (new_group,) = [g for g, (first, n) in enumerate(AG_GROUPS) if first == starts[0]]
        assert list(starts) == list(range(AG_GROUPS[new_group][0], sum(AG_GROUPS[new_group])))

    def body(*refs):
        own_ref, land_ref = refs[0], refs[1]
        sem_refs = list(refs[2:2 + 2 * ns])
        outs = refs[2 + 2 * ns + nd:]
        if starts:
            sem_refs += [outs[0], outs[1]]
        x, y, c = _place()
        me, sib = (x, y, c), (x, y, 1 - c)
        xn, yn, dg = (1 - x, y, c), (x, 1 - y, c), (1 - x, 1 - y, c)

        def other(dev):
            return (dev[0], dev[1], 1 - dev[2])

        def rows(dev, part):
            blk = land_ref.at[4 * dev[0] + 2 * dev[1] + dev[2]]
            return blk if part is None else blk.at[pl.ds(part * half, half)]

        def sem_of(k):
            (g,) = [g for g, (first, n) in enumerate(AG_GROUPS) if first <= k < first + n]
            return sem_refs[2 * g].at[k - AG_GROUPS[g][0]], sem_refs[2 * g + 1].at[k - AG_GROUPS[g][0]]

        sent = {0: (me, None, sib), 1: (me, None, xn), 2: (me, None, yn), 3: (xn, 0, yn), 4: (yn, 1, xn),
                5: (xn, None, sib), 6: (yn, None, sib), 7: (dg, 0, sib), 8: (dg, 1, sib)}
        landed = {0: (sib, None), 1: (xn, None), 2: (yn, None), 3: (dg, 0), 4: (dg, 1), 5: (other(xn), None),
                  6: (other(yn), None), 7: (other(dg), 0), 8: (other(dg), 1)}

        def copy(k, receiving):
            send_sem, recv_sem = sem_of(k)
            dev, part, to = (*landed[k], me) if receiving else sent[k]
            src = own_ref if (dev is me and not receiving) else rows(dev, part)
            return pltpu.make_async_remote_copy(src_ref=src, dst_ref=rows(dev, part), send_sem=send_sem,
                                                recv_sem=recv_sem, device_id=to, device_id_type=MESH_ID)

        for kind, k in waits:
            if kind == "recv":
                copy(k, True).wait_recv()
            else:
                copy(k, False).wait_send()
        for k in starts:
            copy(k, False).start()
        if starts:
            outs[-1][...] = jnp.zeros_like(outs[-1])

    n_new = AG_GROUPS[new_group][1] if starts else 0
    sem_out = (pltpu.SemaphoreType.DMA((n_new,)), pltpu.SemaphoreType.DMA((n_new,))) if starts else ()
    token_out = (jax.ShapeDtypeStruct((8, 128), F32),) if starts else ()
    res = pl.pallas_call(
        body, name=name,
        out_shape=(*sem_out, pltpu.HBM(own.shape, own.dtype), pltpu.HBM(land.shape, land.dtype), *token_out),
        in_specs=[HBM, HBM] + [SEM] * (2 * ns) + dep_specs,
        out_specs=(*([SEM] * len(sem_out)), HBM, HBM, *([pl.BlockSpec(memory_space=pltpu.VMEM)] * len(token_out))),
        input_output_aliases={0: len(sem_out), 1: len(sem_out) + 1},
        compiler_params=pltpu.CompilerParams(has_side_effects=DATAFLOW))(
            own, land, *[a for pair in sems for a in pair], *dep_args)
    if starts:
        return (res[0], res[1]), res[2], res[3], res[4]
    return None, res[0], res[1], None


def _add_sibling(name, term, recv, rows):
    _, r, w = term.shape
    cidx = lax.axis_index("c").astype(jnp.int32).reshape(1)
    like_term = recv.shape[0] == N_DEV

    def body(c_ref, a_ref, b_ref, o_ref):
        o_ref[...] = (a_ref[...].astype(F32) + b_ref[...].astype(F32)).astype(o_ref.dtype)

    grid_spec = pltpu.PrefetchScalarGridSpec(
        num_scalar_prefetch=1, grid=(4, r // rows),
        in_specs=[pl.BlockSpec((None, rows, w), lambda q, i, c_ref: (2 * q + c_ref[0], i, 0)),
                  pl.BlockSpec((None, rows, w), lambda q, i, c_ref: (2 * q + c_ref[0] if like_term else q, i, 0))],
        out_specs=pl.BlockSpec((None, rows, w), lambda q, i, c_ref: (q, i, 0)))
    return pl.pallas_call(
        body, name=name, grid_spec=grid_spec, out_shape=jax.ShapeDtypeStruct((4, r, w), term.dtype),
        compiler_params=_params(("parallel", "parallel")))(cidx, term, recv)


def _add_sibling_small(name, terms, recvs):
    nt = len(terms)

    def body(*refs):
        c = lax.axis_index("c")
        for t_ref, r_ref, o_ref in zip(refs[:nt], refs[nt:2 * nt], refs[2 * nt:]):
            for q in range(4):
                o_ref[q] = (t_ref[2 * q + c].astype(F32) + r_ref[q].astype(F32)).astype(o_ref.dtype)

    return pl.pallas_call(
        body, name=name, out_shape=[jax.ShapeDtypeStruct((4,) + t.shape[1:], t.dtype) for t in terms],
        compiler_params=_params())(*terms, *recvs)


def _rs_rows(a):
    return SHARD_IN // 4 if a.shape[1] == SHARD_IN else a.shape[1]


def _reduce_scatter_start(tag, names, terms, recv1):
    if len(terms) == 1:
        sums = [_add_sibling("add_sibling_" + names[0], terms[0], recv1[0], _rs_rows(terms[0]))]
    else:
        sums = _add_sibling_small("add_sibling_" + tag, terms, recv1)
    lands =[jax.ShapeDtypeStruct((3,) + s.shape[1:], s.dtype) for s in sums]
    send_sems, recv_sems, sums, lands, token = _exchange_start("exchange_chips_start_" + tag, "chips", sums, lands)
    return (tag, names, send_sems, recv_sems, sums, lands), token


def _reduce_scatter_wait(state, after):
    tag, names, send_sems, recv_sems, sums, lands = state
    sums, recv2 = _exchange_wait("exchange_chips_wait_" + tag, "chips", send_sems, recv_sems, sums, lands, after)
    return names, sums, recv2


def _adamw(name, w, g, m, v, dep=None):
    dep_specs, dep_args = _dep_operand(dep)

    def body(w_ref, g_ref, m_ref, v_ref, *rest):
        d_ref, nm_ref, nv_ref = rest[-3:]
        d_ref[...], nm_ref[...], nv_ref[...] = _adam_math(w_ref[...], g_ref[...], m_ref[...], v_ref[...])

    whole = pl.BlockSpec(memory_space=pltpu.VMEM)
    return pl.pallas_call(
        body, name=name, in_specs=[whole] * 4 + dep_specs, out_shape=[jax.ShapeDtypeStruct(w.shape, F32)] * 3,
        compiler_params=_params())(w, g, m, v, *dep_args)


def _adam_math(w, g, m, v):
    nm = ADAM_B1 * m + (1.0 - ADAM_B1) * g
    nv = ADAM_B2 * v + (1.0 - ADAM_B2) * (g * g)
    c1 = 1.0 - ADAM_B1 ** ADAM_STEP
    c2 = 1.0 - ADAM_B2 ** ADAM_STEP
    return -ADAM_LR * ((nm / c1) / (jnp.sqrt(nv / c2) + ADAM_EPS) + ADAM_WD * w), nm, nv


def _presum_halves(sums, landed):
    _, r, w = sums.shape
    rows = r // 4
    x, y = lax.axis_index("x"), lax.axis_index("y")
    dest = jnp.stack([2 * (1 - x) + y, 2 * x + 1 - y]).astype(jnp.int32)

    def body(q_ref, a_ref, b_ref, o_ref):
        o_ref[...] = (a_ref[...].astype(F32) + b_ref[...].astype(F32)).astype(o_ref.dtype)

    grid_spec = pltpu.PrefetchScalarGridSpec(
        num_scalar_prefetch=1, grid=(2, 2),
        in_specs=[pl.BlockSpec((None, rows, w), lambda h, i, q_ref: (q_ref[h], 2 * h + i, 0)),
                  pl.BlockSpec((None, rows, w), lambda h, i, q_ref: (1 + 2 * h, i, 0))],
        out_specs=pl.BlockSpec((None, rows, w), lambda h, i, q_ref: (h, i, 0)))
    return pl.pallas_call(
        body, name="presum_halves", grid_spec=grid_spec, out_shape=jax.ShapeDtypeStruct((2, r // 2, w), sums.dtype),
        compiler_params=_params(("parallel", "parallel")))(dest, sums, landed)


def _adamw_halves(name, sums, landed_a, landed_b, w, m, v, rows):
    r, c = w.shape
    half = c // 2
    qidx = (2 * lax.axis_index("x") + lax.axis_index("y")).astype(jnp.int32).reshape(1)

    def body(q_ref, s_ref, a_ref, b_ref, w_ref, m_ref, v_ref, g_ref, d_ref, nm_ref, nv_ref):
        first = (s_ref[:half, :].astype(F32) + a_ref[0].astype(F32)) + b_ref[0].astype(F32)
        second = (s_ref[half:, :].astype(F32) + a_ref[2].astype(F32)) + b_ref[1].astype(F32)
        g = jnp.concatenate([first, second], axis=0).T
        g_ref[...] = g
        d_ref[...], nm_ref[...], nv_ref[...] = _adam_math(w_ref[...], g, m_ref[...], v_ref[...])

    row = pl.BlockSpec((rows, c), lambda i, q_ref: (i, 0))
    grid_spec = pltpu.PrefetchScalarGridSpec(
        num_scalar_prefetch=1, grid=(r // rows,),
        in_specs=[pl.BlockSpec((None, c, rows), lambda i, q_ref: (q_ref[0], 0, i)),
                  pl.BlockSpec((4, half, rows), lambda i, q_ref: (0, 0, i)),
                  pl.BlockSpec((2, half, rows), lambda i, q_ref: (0, 0, i)), row, row, row],
        out_specs=[row] * 4)
    return pl.pallas_call(
        body, name=name, grid_spec=grid_spec, out_shape=[jax.ShapeDtypeStruct((r, c), F32)] * 4,
        compiler_params=_params(("parallel",)))(qidx, sums, landed_a, landed_b, w, m, v)


def _adamw_chips_small(name, items):
    n = len(items)

    def body(*refs):
        q = 2 * lax.axis_index("x") + lax.axis_index("y")
        ins, outs = refs[:5 * n], refs[5 * n:]
        for i, (_, _, w, _, _, transposed) in enumerate(items):
            s_ref, r_ref, w_ref, m_ref, v_ref = ins[5 * i:5 * i + 5]
            g_ref, d_ref, nm_ref, nv_ref = outs[4 * i:4 * i + 4]
            g = (s_ref[q].astype(F32) + r_ref[0].astype(F32)) + (r_ref[1].astype(F32) + r_ref[2].astype(F32))
            g = g.T if transposed else g[:w.shape[0]]
            g_ref[...] = g
            d_ref[...], nm_ref[...], nv_ref[...] = _adam_math(w_ref[...], g, m_ref[...], v_ref[...])

    res = pl.pallas_call(
        body, name=name, out_shape=[jax.ShapeDtypeStruct(it[2].shape, F32) for it in items for _ in range(4)],
        compiler_params=_params())(*[a for it in items for a in it[:5]])
    return [res[4 * i:4 * i + 4] for i in range(n)]


def _sum_devices(gathered):
    def body(g_ref, o_ref):
        acc = g_ref[0]
        for j in range(1, N_DEV):
            acc = acc + g_ref[j]
        o_ref[...] = acc

    return pl.pallas_call(
        body, name="sum_devices", out_shape=jax.ShapeDtypeStruct(gathered.shape[1:], F32),
        compiler_params=_params())(gathered)


def _rows128(a, rows):
    flat = a.reshape(-1)
    return jnp.pad(flat, (0, rows * 128 - flat.shape[0])).reshape(rows, 128)


def kernel(x, mem, pre_norm, w_in, merge_bias, na_rpb, mem_norm, w_mem_kv, w_branch_a, w_branch_b, w_branch_c, w_out, post_norm, loss_target, m_pre_norm, m_w_in, m_merge_bias, m_na_rpb, m_mem_norm, m_w_mem_kv, m_w_branch_a, m_w_branch_b, m_w_branch_c, m_w_out, m_post_norm, v_pre_norm, v_w_in, v_merge_bias, v_na_rpb, v_mem_norm, v_w_mem_kv, v_w_branch_a, v_w_branch_b, v_w_branch_c, v_w_out, v_post_norm):
    wt_in_s = w_in[0].T.astype(BF16)
    rows_s = jnp.concatenate([w_mem_kv[0], w_out[0]], axis=0).astype(BF16)
    cols_s = jnp.concatenate([w_branch_a[0].T, w_branch_b[0].T, w_branch_c[0].T], axis=0).astype(BF16)
    mb_s = jnp.pad(merge_bias[0], ((0, 5), (0, 0)))
    me = 4 * lax.axis_index("x") + 2 * lax.axis_index("y") + lax.axis_index("c")

    chip = 2 * lax.axis_index("x") + lax.axis_index("y")

    def first_block(q):
        return jnp.where(q == 0, 0, jnp.where(q == 1, 6, jnp.where(q == 2, 11, 17)))

    five = jnp.arange(5, dtype=jnp.int32)
    near, far = jnp.where(chip < 2, 5, 16), jnp.where(chip < 2, 16, 5)
    order1 = (first_block(chip) + five).astype(jnp.int32)
    order2 = jnp.concatenate([first_block(chip ^ 1) + five, near[None], first_block(chip ^ 2) + five]).astype(jnp.int32)
    order3 = jnp.concatenate([first_block(chip ^ 3) + five, far[None]]).astype(jnp.int32)
    tabs = _rope_tables()

    def weights_of(land):
        return land.reshape(N_IN, D_MODEL)

    land = pltpu.with_memory_space_constraint(lax.empty((N_DEV,) + wt_in_s.shape, BF16), pltpu.HBM)
    own = pltpu.with_memory_space_constraint(wt_in_s, pltpu.HBM)
    sem_a, own, land, token = _ag_phase("ag_start", own, land, [], [], [0, 1, 2])
    hs, hst = _prenorm_fold(x[0], pre_norm, token)
    _, own, land, _ = _ag_phase("ag_wait0", own, land, [sem_a], [("recv", 0)], [], hs)
    land = lax.dynamic_update_slice(land, own[None], (me, 0, 0))
    parts = _in_proj("in_proj_1", hs, weights_of(land), tabs, order1)
    bias = _na_bias(jnp.pad(na_rpb[0], ((0, 0), (0, 1), (0, 128 - 31))), parts)
    sem_b, own, land, _ = _ag_phase("ag_mid1", own, land, [sem_a], [("recv", 1), ("recv", 2)], [3, 4, 5, 6], bias)
    _, own, land, _ = _ag_phase("ag_wait1", own, land, [sem_a, sem_b], [("recv", 5), ("recv", 6)], [])
    parts = _in_proj("in_proj_2", hs, weights_of(land), tabs, order2, parts)
    sem_c, own, land, _ = _ag_phase("ag_mid2", own, land, [sem_a, sem_b], [("recv", 3), ("recv", 4)], [7, 8], parts)
    _, own, land, _ = _ag_phase("ag_end", own, land, [sem_a, sem_b, sem_c],
                                [("recv", 7), ("recv", 8)] + [("send", k) for k in range(9)], [])
    wt_in = weights_of(land)

    late_own = [rows_s, cols_s, mb_s]
    late_lands = [jax.ShapeDtypeStruct((N_DEV,) + s.shape, s.dtype) for s in late_own]
    l_send, l_recv, late_own, late_lands, late_token = _exchange_start("gather_late_start", "gather", late_own,
                                                                       late_lands, after=wt_in)
    parts = _in_proj("in_proj_3", hs, wt_in, tabs, order3, parts, late_token)

    def late_weights(after):
        own, lands = _exchange_wait("gather_late_wait", "gather", l_send, l_recv, late_own, late_lands, after)
        g_rows, g_cols, g_mb = [lax.dynamic_update_slice(land, o[None], (me, 0, 0)) for land, o in zip(lands, own)]
        return (g_mb[:, :3].transpose(1, 0, 2).reshape(3, D_MODEL),
                g_rows[:, :128].reshape(D_MODEL, D_MODEL), g_cols[:, 0:128].reshape(D_MODEL, 512),
                g_cols[:, 128:256].reshape(D_MODEL, 512), g_cols[:, 256:384].reshape(D_MODEL, 512),
                g_rows[:, 128:].reshape(D_MODEL, D_MODEL))

    rs_state = []
    rest_names = ["w_kv", "w_out", "a", "b", "c", "mb"]
    rest_sibling, w_in_a, w_in_b = [], [], []

    def reduce_start(phase, grads, after=None):
        if phase == "rest_sibling":
            gmb_t = jnp.pad(grads["merge_bias"].reshape(3, N_DEV, 128).transpose(1, 0, 2), ((0, 0), (0, 5), (0, 0)))
            terms = [grads["w_kv"].reshape(N_DEV, 128, D_MODEL), grads["w_out"].reshape(N_DEV, 128, D_MODEL),
                     grads["wt_a"].reshape(N_DEV, 128, 512), grads["wt_b"].reshape(N_DEV, 128, 512),
                     grads["wt_c"].reshape(N_DEV, 128, 512), gmb_t]
            lands = [jax.ShapeDtypeStruct((4,) + t.shape[1:], t.dtype) for t in terms]
            rest_sibling.extend(_exchange_start("exchange_sibling_start_rest", "sibling", terms, lands)[:4])
            return None
        if phase == "rest_chips":
            s_send, s_recv, terms, lands = rest_sibling
            terms, recv1 = _exchange_wait("exchange_sibling_wait_rest", "sibling", s_send, s_recv, terms, lands, after)
            state, token = _reduce_scatter_start("rest", rest_names, terms, recv1)
        elif phase == "w_in":
            own, sibling = [a.reshape(N_DEV, SHARD_IN, D_MODEL) for a in grads["wt_in"]]
            sums = _add_sibling("add_sibling_w_in", own, sibling, SHARD_IN // 4)
            lands = [jax.ShapeDtypeStruct((4, SHARD_IN // 2, D_MODEL), BF16)]
            w_in_a.extend(_exchange_start("rs_a_start", "rs_a", [sums], lands))
            return w_in_a[4]
        else:
            (sums,), (landed_a,) = _exchange_wait("rs_a_wait", "rs_a", w_in_a[0], w_in_a[1], w_in_a[2], w_in_a[3], after)
            lands = [jax.ShapeDtypeStruct((2, SHARD_IN // 2, D_MODEL), BF16)]
            w_in_b.extend(_exchange_start("rs_b_start", "rs_b", [_presum_halves(sums, landed_a)], lands))
            w_in_b.extend([sums, landed_a])
            return w_in_b[4]
        rs_state.append(state)
        return token

    loss_term, grad_x, grads = _local_step(
        x[0], hst, parts, tabs, bias, mem[0], loss_target[0], pre_norm, mem_norm, post_norm, wt_in, late_weights,
        reduce_start=reduce_start)

    small = jnp.concatenate([_rows128(grads["pre_norm"], 8), _rows128(grads["mem_norm"], 8),
                             _rows128(grads["post_norm"], 8), _rows128(grads["na_rpb"], 32),
                             _rows128(loss_term, 8)], axis=0)
    s_send, s_recv, s_own, s_land, s_token = _exchange_start(
        "gather_small_start", "gather", [small], [jax.ShapeDtypeStruct((N_DEV,) + small.shape, F32)])
    grad = {}
    weights = {
        "pre_norm": (pre_norm, m_pre_norm, v_pre_norm), "w_in": (w_in, m_w_in, v_w_in),
        "merge_bias": (merge_bias, m_merge_bias, v_merge_bias), "na_rpb": (na_rpb, m_na_rpb, v_na_rpb),
        "mem_norm": (mem_norm, m_mem_norm, v_mem_norm), "w_mem_kv": (w_mem_kv, m_w_mem_kv, v_w_mem_kv),
        "w_branch_a": (w_branch_a, m_w_branch_a, v_w_branch_a), "w_branch_b": (w_branch_b, m_w_branch_b, v_w_branch_b),
        "w_branch_c": (w_branch_c, m_w_branch_c, v_w_branch_c), "w_out": (w_out, m_w_out, v_w_out),
        "post_norm": (post_norm, m_post_norm, v_post_norm)}
    order = ["pre_norm", "w_in", "merge_bias", "na_rpb", "mem_norm", "w_mem_kv", "w_branch_a", "w_branch_b",
             "w_branch_c", "w_out", "post_norm"]
    delta, new_m, new_v = {}, {}, {}

    def update(n, dep=None):
        w, m, v = weights[n]
        shape = w.shape
        two_d = (-1, shape[-1])
        dl, nm, nv = _adamw("adamw_" + n, w.reshape(two_d), grad[n].reshape(two_d), m.reshape(two_d),
                            v.reshape(two_d), dep)
        delta[n], new_m[n], new_v[n] = dl.reshape(shape), nm.reshape(shape), nv.reshape(shape)
        return dl

    _, sums, recv2 = _reduce_scatter_wait(rs_state[0], s_token)
    rest = (("w_mem_kv", False), ("w_out", False), ("w_branch_a", True), ("w_branch_b", True), ("w_branch_c", True),
            ("merge_bias", False))
    items = [(sums[i], recv2[i], *[a[0] for a in weights[n]], transposed) for i, (n, transposed) in enumerate(rest)]
    for (n, _), (g, dl, nm, nv) in zip(rest, _adamw_chips_small("adamw_rest", items)):
        grad[n], delta[n], new_m[n], new_v[n] = g[None], dl[None], nm[None], nv[None]
    s_own, s_land = _exchange_wait("gather_small_wait", "gather", s_send, s_recv, s_own, s_land, delta["w_out"])
    total = _sum_devices(lax.dynamic_update_slice(s_land[0], s_own[0][None], (me, 0, 0)))
    loss = total[56, 0]
    grad.update({"pre_norm": total[0:8].reshape(1, D_MODEL), "mem_norm": total[8:16].reshape(1, D_MODEL),
                 "post_norm": total[16:24].reshape(1, D_MODEL),
                 "na_rpb": total[24:56].reshape(-1)[:8 * 15 * 31].reshape(1, 8, 15, 31)})
    dep = None
    for n in ("pre_norm", "na_rpb", "mem_norm", "post_norm"):
        dep = update(n, dep)
    _, (landed_b,) = _exchange_wait("rs_b_wait", "rs_b", w_in_b[0], w_in_b[1], w_in_b[2], w_in_b[3], dep)
    g, dl, nm, nv = _adamw_halves("adamw_w_in", w_in_b[5], w_in_b[6], landed_b, w_in[0], m_w_in[0], v_w_in[0], 256)
    grad["w_in"], delta["w_in"], new_m["w_in"], new_v["w_in"] = g[None], dl[None], nm[None], nv[None]

    return (loss, grad_x[None], *[grad[n] for n in order], *[delta[n] for n in order],
            *[new_m[n] for n in order], *[new_v[n] for n in order])
```

```python
import functools

import numpy as np
import jax
import jax.numpy as jnp
from jax import lax
from jax.experimental import pallas as pl
from jax.experimental.pallas import tpu as pltpu

F32 = jnp.float32
BF16 = jnp.bfloat16

SEQ = 2048
D_MODEL = 1024
N_IN = 11264
N_DEV = 8
SHARD_IN = N_IN // N_DEV
HEAD_DIM = 64
GRID_W = 64
NA_ROWS = 8
MEM_LEN = 256
DILATIONS = (1, 4, 16)
REACH = 64
ROPE_THETA = 500000.0
ROPE_DIM = 16
EPS = 1e-6
NEG = -1e30
ADAM_LR = 0.001
ADAM_B1 = 0.9
ADAM_B2 = 0.999
ADAM_EPS = 1e-08
ADAM_WD = 0.01
ADAM_STEP = 10

VMEM_LIMIT_BYTES = 56 * 1024 * 1024
MESH_ID = pl.DeviceIdType.MESH

NN = (((1,), (0,)), ((), ()))
NT = (((1,), (1,)), ((), ()))
TN = (((0,), (0,)), ((), ()))


def _params(sem=None):
    return pltpu.CompilerParams(dimension_semantics=sem, vmem_limit_bytes=VMEM_LIMIT_BYTES)


def _iota(shape, dim):
    return lax.broadcasted_iota(jnp.int32, shape, dim)


def _sigmoid(x):
    return 1.0 / (1.0 + jnp.exp(-x))


def _rope_tables():
    half = ROPE_DIM // 2
    inv = (ROPE_THETA ** (-np.arange(half, dtype=np.float64) * 2.0 / ROPE_DIM)).astype(np.float32)
    pos = np.arange(SEQ, dtype=np.float32)
    ang = pos[:, None] * inv[None, :]
    cos, sin = np.cos(ang), np.sin(ang)
    zeros = np.zeros_like(cos)
    rest = HEAD_DIM - ROPE_DIM
    c64 = np.concatenate([cos, cos, np.ones((SEQ, rest), np.float32)], axis=1)
    s1 = np.concatenate([zeros, sin, np.zeros((SEQ, rest), np.float32)], axis=1)
    s2 = np.concatenate([-sin, zeros, np.zeros((SEQ, rest), np.float32)], axis=1)

    def fold(t, d):
        return t.reshape(SEQ // d, d, t.shape[1]).transpose(1, 0, 2).reshape(SEQ, t.shape[1])

    tabs = [np.stack([np.tile(fold(t, d), (1, 2)) for t in (c64, s1, s2)], axis=0) for d in DILATIONS]
    return jnp.asarray(np.stack(tabs, axis=0), dtype=F32)


def _rope(a, c, s1, s2):
    return a * c + pltpu.roll(a, 8, 1) * s1 + pltpu.roll(a, 120, 1) * s2


def _rope_t(a, c, s1, s2):
    return a * c + pltpu.roll(a * s1, 120, 1) + pltpu.roll(a * s2, 8, 1)


def _perm_of_block(j):
    return jnp.where(j < 3, 0, jnp.where(j < 6, 1, jnp.where(j < 9, 2, 0)))


def _mm(name, a, b, out_shape, out_dtype, grid, a_spec, b_spec, o_spec, acc_shape, dims, k_axis, nk):
    def body(a_ref, b_ref, o_ref, acc_ref):
        k = pl.program_id(k_axis)

        @pl.when(k == 0)
        def _():
            acc_ref[...] = jnp.zeros(acc_shape, F32)

        acc_ref[...] += lax.dot_general(a_ref[...], b_ref[...], dims, preferred_element_type=F32)

        @pl.when(k == nk - 1)
        def _():
            o_ref[...] = acc_ref[...].astype(out_dtype)

    sem = tuple("arbitrary" if ax == k_axis else "parallel" for ax in range(len(grid)))
    return pl.pallas_call(
        body, name=name, grid=grid, in_specs=[a_spec, b_spec], out_specs=o_spec,
        out_shape=jax.ShapeDtypeStruct(out_shape, out_dtype),
        scratch_shapes=[pltpu.VMEM(acc_shape, F32)], compiler_params=_params(sem))(a, b)


def _mm_simple(name, a, b, dims, out_dtype, tm, tn, tk):
    if dims is NN:
        m, kk = a.shape
        n = b.shape[1]
        a_spec = pl.BlockSpec((tm, tk), lambda i, j, k: (i, k))
        b_spec = pl.BlockSpec((tk, tn), lambda i, j, k: (k, j))
    elif dims is NT:
        m, kk = a.shape
        n = b.shape[0]
        a_spec = pl.BlockSpec((tm, tk), lambda i, j, k: (i, k))
        b_spec = pl.BlockSpec((tn, tk), lambda i, j, k: (j, k))
    else:
        kk, m = a.shape
        n = b.shape[1]
        a_spec = pl.BlockSpec((tk, tm), lambda i, j, k: (k, i))
        b_spec = pl.BlockSpec((tk, tn), lambda i, j, k: (k, j))
    grid = (m // tm, n // tn, kk // tk)
    o_spec = pl.BlockSpec((tm, tn), lambda i, j, k: (i, j))
    return _mm(name, a, b, (m, n), out_dtype, grid, a_spec, b_spec, o_spec, (tm, tn), dims, 2, kk // tk)


def _rmsnorm_fwd(name, x, gain, rows):
    n, d = x.shape

    def body(x_ref, g_ref, o_ref):
        xv = x_ref[...]
        rstd = lax.rsqrt(jnp.mean(xv * xv, axis=1, keepdims=True) + EPS)
        o_ref[...] = (xv * rstd * g_ref[...]).astype(BF16)

    return pl.pallas_call(
        body, name=name, grid=(n // rows,),
        in_specs=[pl.BlockSpec((rows, d), lambda i: (i, 0)), pl.BlockSpec((1, d), lambda i: (0, 0))],
        out_specs=pl.BlockSpec((rows, d), lambda i: (i, 0)),
        out_shape=jax.ShapeDtypeStruct((n, d), BF16), compiler_params=_params(("parallel",)))(x, gain)


def _folded_rows(first, rows, d):
    if d == 1:
        return pl.ds(pl.multiple_of(first, rows), rows)
    mlen = SEQ // d
    return pl.ds((first % mlen) * d + first // mlen, rows, stride=d)


def _prenorm_fold(x, gain, dep=None):
    rows = 128
    nchunk = D_MODEL // 128
    dep_specs, dep_args = _dep_operand(dep)

    def body(*refs):
        x_refs, g_ref, hs_ref, hst_ref = refs[:nchunk], refs[nchunk], refs[-2], refs[-1]
        first = pl.program_id(0) * rows
        for p, d in enumerate(DILATIONS):
            idx = _folded_rows(first, rows, d)
            xv = jnp.concatenate([r[idx, :] for r in x_refs], axis=1)
            rstd = lax.rsqrt(jnp.mean(xv * xv, axis=1, keepdims=True) + EPS)
            h = xv * rstd * g_ref[...]
            hs_ref[p] = h.astype(BF16)
            hst_ref[p] = h.T.astype(BF16)

    x_specs = [pl.BlockSpec((SEQ, 128), functools.partial(lambda c, i: (0, c), c)) for c in range(nchunk)]
    return pl.pallas_call(
        body, name="prenorm", grid=(SEQ // rows,),
        in_specs=x_specs + [pl.BlockSpec((1, D_MODEL), lambda i: (0, 0))] + dep_specs,
        out_specs=[pl.BlockSpec((3, rows, D_MODEL), lambda i: (0, i, 0)),
                   pl.BlockSpec((3, D_MODEL, rows), lambda i: (0, 0, i))],
        out_shape=[jax.ShapeDtypeStruct((3, SEQ, D_MODEL), BF16), jax.ShapeDtypeStruct((3, D_MODEL, SEQ), BF16)],
        compiler_params=_params(("parallel",)))(*([x] * nchunk), gain, *dep_args)


def _prenorm_bwd(x, gain, dh, dout):
    rows = 256

    def body(x_ref, g_ref, a_ref, do_ref, dx_ref, gg_ref):
        xv = x_ref[...]
        rstd = lax.rsqrt(jnp.mean(xv * xv, axis=1, keepdims=True) + EPS)
        xn = xv * rstd
        dh = jnp.concatenate([a_ref[c] for c in range(D_MODEL // 128)], axis=1)
        gdh = dh * g_ref[...]
        dx_ref[...] = rstd * (gdh - xn * jnp.mean(gdh * xn, axis=1, keepdims=True)) + do_ref[...]

        @pl.when(pl.program_id(0) == 0)
        def _():
            gg_ref[...] = jnp.zeros((1, D_MODEL), F32)

        gg_ref[...] += jnp.sum(dh * xn, axis=0, keepdims=True)

    row = pl.BlockSpec((rows, D_MODEL), lambda i: (i, 0))
    vec = pl.BlockSpec((1, D_MODEL), lambda i: (0, 0))
    return pl.pallas_call(
        body, name="prenorm_bwd", grid=(SEQ // rows,),
        in_specs=[row, vec, pl.BlockSpec((D_MODEL // 128, rows, 128), lambda i: (0, i, 0)), row], out_specs=[row, vec],
        out_shape=[jax.ShapeDtypeStruct((SEQ, D_MODEL), F32), jax.ShapeDtypeStruct((1, D_MODEL), F32)],
        compiler_params=_params(("arbitrary",)))(x, gain, dh, dout)


def _memnorm_bwd(mem, dmemn, dep=None):
    dep_specs, dep_args = _dep_operand(dep)

    def body(m_ref, d_ref, *rest):
        mv = m_ref[...]
        rstd = lax.rsqrt(jnp.mean(mv * mv, axis=1, keepdims=True) + EPS)
        rest[-1][...] = jnp.sum(d_ref[...] * mv * rstd, axis=0, keepdims=True)

    whole = pl.BlockSpec(memory_space=pltpu.VMEM)
    return pl.pallas_call(
        body, name="memnorm_bwd", in_specs=[whole, whole] + dep_specs,
        out_shape=jax.ShapeDtypeStruct((1, D_MODEL), F32), compiler_params=_params())(mem, dmemn, *dep_args)


def _dep_operand(dep):
    return ([], []) if dep is None else ([pl.BlockSpec(memory_space=pl.ANY)], [dep])


def _in_proj(name, hs, wt, tabs, order, prev=None, dep=None):
    tm, tn = 512, 512
    prev_specs, prev_args = ([], []) if prev is None else ([ANY], [prev])
    dep_specs, dep_args = _dep_operand(dep)

    def body(order_ref, h_ref, w_ref, t_ref, *rest):
        o_ref = rest[-1]
        j = order_ref[pl.program_id(0)]
        is_rope = jnp.logical_and(j < 9, j % 3 != 2)
        row_slices = [slice(r * tm, (r + 1) * tm) for r in range(SEQ // tm)]

        def product(rs):
            return lax.dot_general(h_ref[rs, :], w_ref[...], NT, preferred_element_type=F32)

        @pl.when(is_rope)
        def _():
            for rs in row_slices:
                acc = product(rs)
                c, s1, s2 = t_ref[0, rs, :], t_ref[1, rs, :], t_ref[2, rs, :]
                for q in range(tn // 128):
                    a = acc[:, q * 128:(q + 1) * 128]
                    o_ref[rs, q * 128:(q + 1) * 128] = _rope(a, c, s1, s2).astype(BF16)

        @pl.when(jnp.logical_not(is_rope))
        def _():
            for rs in row_slices:
                o_ref[rs, :] = product(rs).astype(BF16)

    grid_spec = pltpu.PrefetchScalarGridSpec(
        num_scalar_prefetch=1, grid=(order.shape[0],),
        in_specs=[pl.BlockSpec((None, SEQ, D_MODEL), lambda t, o: (_perm_of_block(o[t]), 0, 0)),
                  pl.BlockSpec((tn, D_MODEL), lambda t, o: (o[t], 0)),
                  pl.BlockSpec((None, 3, SEQ, 128), lambda t, o: (_perm_of_block(o[t]), 0, 0, 0))] + prev_specs
        + dep_specs,
        out_specs=pl.BlockSpec((SEQ, tn), lambda t, o: (0, o[t])))
    return pl.pallas_call(
        body, name=name, grid_spec=grid_spec, out_shape=jax.ShapeDtypeStruct((SEQ, N_IN), BF16),
        input_output_aliases={} if prev is None else {4: 0},
        compiler_params=_params(("arbitrary",)))(order, hs, wt, tabs, *prev_args, *dep_args)


def _piece_blocks(pieces):
    return [(a, h * 512) for a, p in enumerate(pieces) for h in range(p.shape[1] // 512)]


def _block_fetch(piece_refs, blocks, buf, sem):
    def start(block, slot):
        for b, (a, col) in enumerate(blocks):
            @pl.when(block == b)
            def _():
                pltpu.make_async_copy(piece_refs[a].at[:, pl.ds(col, 512)], buf.at[slot], sem.at[slot]).start()

    def wait(slot):
        pltpu.make_async_copy(piece_refs[0].at[:, pl.ds(0, 512)], buf.at[slot], sem.at[slot]).wait()

    return start, wait


def _in_proj_dw(pieces, hst, dep=None):
    tn = 512
    blocks = _piece_blocks(pieces)
    nblk = len(blocks)
    npc = len(pieces)
    dep_specs, dep_args = _dep_operand(dep)

    def body(h_ref, *rest):
        piece_refs = rest[:npc]
        own_out, mirror, buf, sem, out_buf, send_sems, recv_sem, local_sems = rest[-8:]
        j = pl.program_id(0)
        slot = j % 2
        start, wait = _block_fetch(piece_refs, blocks, buf, sem)
        x, y, c = _place()

        def rows_of(step):
            return pl.ds(pl.multiple_of(step * tn, tn), tn)

        def to_sibling(step, slot_):
            return pltpu.make_async_remote_copy(
                src_ref=out_buf.at[slot_], dst_ref=mirror.at[rows_of(step)],
                send_sem=send_sems.at[slot_], recv_sem=recv_sem, device_id=(x, y, 1 - c), device_id_type=MESH_ID)

        def to_own(step, slot_):
            return pltpu.make_async_copy(out_buf.at[slot_], own_out.at[rows_of(step)], local_sems.at[slot_])

        @pl.when(j == 0)
        def _():
            start(j, slot)

        wait(slot)

        @pl.when(j + 1 < nblk)
        def _():
            start(j + 1, 1 - slot)

        acc = jnp.dot(h_ref[...], buf[slot], preferred_element_type=F32)

        @pl.when(j >= 2)
        def _():
            to_sibling(j - 2, slot).wait_send()
            to_own(j - 2, slot).wait()

        out_buf[slot] = acc.T.astype(BF16)
        to_sibling(j, slot).start()
        to_own(j, slot).start()

        @pl.when(j == nblk - 1)
        def _():
            to_sibling(j - 1, 1 - slot).wait_send()
            to_own(j - 1, 1 - slot).wait()
            to_sibling(j, slot).wait_send()
            to_own(j, slot).wait()
            pltpu.make_async_remote_copy(src_ref=mirror, dst_ref=mirror, send_sem=send_sems.at[0], recv_sem=recv_sem,
                                         device_id=(x, y, 1 - c), device_id_type=MESH_ID).wait_recv()

    return pl.pallas_call(
        body, name="in_proj_dw", grid=(nblk,),
        in_specs=[pl.BlockSpec((None, D_MODEL, SEQ), lambda j: (_perm_of_block(j), 0, 0))] + [ANY] * npc + dep_specs,
        out_specs=[ANY, ANY],
        out_shape=[jax.ShapeDtypeStruct((N_IN, D_MODEL), BF16), jax.ShapeDtypeStruct((N_IN, D_MODEL), BF16)],
        scratch_shapes=[pltpu.VMEM((2, SEQ, tn), BF16), pltpu.SemaphoreType.DMA((2,)),
                        pltpu.VMEM((2, tn, D_MODEL), BF16), pltpu.SemaphoreType.DMA((2,)), pltpu.SemaphoreType.DMA,
                        pltpu.SemaphoreType.DMA((2,))],
        compiler_params=_params(("arbitrary",)))(hst, *pieces, *dep_args)


def _in_proj_dh(pieces, wt, dep=None):
    tk = 512
    blocks = _piece_blocks(pieces)
    nblk = len(blocks)
    npc = len(pieces)
    nchunk = D_MODEL // 128

    def col(s):
        return jnp.where(s < 3, s, jnp.where(s < 16, s + 6, s - 13))

    dep_specs, dep_args = _dep_operand(dep)

    def body(w_ref, *rest):
        piece_refs = rest[:npc]
        o_ref, acc_ref, buf, sem = rest[-4:]
        s = pl.program_id(0)
        slot = s % 2
        start, wait = _block_fetch(piece_refs, blocks, buf, sem)

        @pl.when(s == 0)
        def _():
            start(col(s), slot)

        wait(slot)

        @pl.when(s + 1 < nblk)
        def _():
            start(col(s + 1), 1 - slot)

        row_slices = [slice(r * 512, (r + 1) * 512) for r in range(SEQ // 512)]

        def product(rs):
            return jnp.dot(buf[slot, rs, :], w_ref[...], preferred_element_type=F32)

        def accumulate(cond, to_out, init):
            @pl.when(cond)
            def _():
                for rs in row_slices:
                    prod = product(rs)
                    if not to_out:
                        if init:
                            acc_ref[rs, :] = prod
                        else:
                            acc_ref[rs, :] += prod
                        continue
                    for c in range(nchunk):
                        if init:
                            o_ref[c, rs, :] = prod[:, c * 128:(c + 1) * 128]
                        else:
                            o_ref[c, rs, :] += prod[:, c * 128:(c + 1) * 128]

        accumulate(s == 0, True, True)
        accumulate(jnp.logical_and(s > 0, s < 16), True, False)
        accumulate(jnp.logical_or(s == 16, s == 19), False, True)
        accumulate(jnp.logical_and(s > 16, s != 19), False, False)
        for last, d in ((18, 4), (21, 16)):
            @pl.when(s == last)
            def _():
                mlen = SEQ // d
                for r in range(d):
                    for c in range(nchunk):
                        o_ref[c, pl.ds(r, mlen, stride=d), :] += acc_ref[r * mlen:(r + 1) * mlen,
                                                                         c * 128:(c + 1) * 128]

    return pl.pallas_call(
        body, name="in_proj_dh", grid=(nblk,),
        in_specs=[pl.BlockSpec((tk, D_MODEL), lambda s: (col(s), 0))] + [ANY] * npc + dep_specs,
        out_specs=pl.BlockSpec((nchunk, SEQ, 128), lambda s: (0, 0, 0)),
        out_shape=jax.ShapeDtypeStruct((nchunk, SEQ, 128), F32),
        scratch_shapes=[pltpu.VMEM((SEQ, D_MODEL), F32), pltpu.VMEM((2, SEQ, tk), BF16),
                        pltpu.SemaphoreType.DMA((2,))],
        compiler_params=_params(("arbitrary",)))(wt, *pieces, *dep_args)


def _head_lanes(lanes, hh):
    return lanes >= 64 if hh == 1 else lanes < 64


def _head_rows(x, lanes, hh, pair):
    if not pair:
        return jnp.max(x, axis=1, keepdims=True)
    return jnp.max(jnp.where(_head_lanes(lanes, hh), x, -jnp.inf), axis=1, keepdims=True)


def _mask_head(x, lanes, hh, pair, scale=1.0):
    if not pair:
        return x
    xf = x.astype(F32) if scale == 1.0 else x.astype(F32) * scale
    return jnp.where(_head_lanes(lanes, hh), xf, 0.0).astype(BF16)


def _window(mode, qi, tq, mlen, tk):
    if mode == "dil":
        q0 = qi * tq
        seg = (q0 // mlen) * mlen
        ks = jnp.clip(q0 - REACH, seg, seg + mlen - tk)
        return pl.multiple_of(ks, 64)
    if mode == "na":
        r_start = jnp.clip(qi - NA_ROWS // 2, 0, SEQ // GRID_W - NA_ROWS)
        return pl.multiple_of(r_start * GRID_W, 64)
    return 0


def _band_mask(qi, tq, tk, ks):
    qpos = qi * tq + _iota((tq, tk), 0)
    kpos = ks + _iota((tq, tk), 1)
    return jnp.where(jnp.abs(qpos - kpos) <= REACH, 0.0, NEG).astype(F32)


def _stack_heads(x, lanes, pair, scale=1.0):
    if not pair:
        return x
    return jnp.concatenate([_mask_head(x, lanes, hh, pair, scale) for hh in range(2)], axis=0)


def _stack_rows(x, lanes, pair):
    if not pair:
        return _head_rows(x, lanes, 0, pair)
    return jnp.concatenate([_head_rows(x, lanes, hh, pair) for hh in range(2)], axis=0)


def _unstack_heads(x, lanes, pair, tq):
    if not pair:
        return x
    return jnp.where(lanes < 64, x[:tq], x[tq:])


def _scores(mode, qst, k, sscale, band, qi, bias_ref, pair):
    s = lax.dot_general(qst, k, NT, preferred_element_type=F32)
    if sscale != 1.0:
        s = s * sscale
    if mode == "dil":
        s = s + jnp.concatenate([band, band], axis=0)
    elif mode == "na":
        off = qi - jnp.clip(qi - NA_ROWS // 2, 0, SEQ // GRID_W - NA_ROWS)
        s = s + jnp.concatenate([bias_ref[0, off], bias_ref[1, off]], axis=0)
    return s


def _attn_cfg(mode, d):
    if mode == "dil":
        mlen = SEQ // d
        return dict(pair=True, tq=128, tk=min(256, mlen), mlen=mlen, lk=SEQ, scale=HEAD_DIM ** -0.5, units=4,
                    nsub=ATTN_SUBTILES)
    if mode == "na":
        return dict(pair=True, tq=GRID_W, tk=NA_ROWS * GRID_W, mlen=SEQ, lk=SEQ, scale=HEAD_DIM ** -0.5, units=4,
                    nsub=2 * ATTN_SUBTILES)
    return dict(pair=False, tq=128, tk=MEM_LEN, mlen=SEQ, lk=MEM_LEN, scale=128 ** -0.5, units=4,
                nsub=ATTN_SUBTILES)


ATTN_SUBTILES = 16


def _attn_fwd(name, mode, q_arr, k_arr, v_arr, qcol, kcol, vcol, d=1, bias=None):
    cfg = _attn_cfg(mode, d)
    pair, tq, tk, mlen, lk, scale = cfg["pair"], cfg["tq"], cfg["tk"], cfg["mlen"], cfg["lk"], cfg["scale"]
    qscale, sscale = (scale, 1.0) if pair else (1.0, scale)
    nsub = cfg["nsub"]
    rows = nsub * tq

    def body(*refs):
        if mode == "na":
            q_ref, k_ref, v_ref, bias_ref, o_ref, l_ref = refs
        else:
            q_ref, k_ref, v_ref, o_ref, l_ref = refs
            bias_ref = None
        lanes = _iota((tq, 128), 1)
        qis = [pl.program_id(1) * nsub + sub for sub in range(nsub)]
        kss = [_window(mode, qi, tq, mlen, tk) for qi in qis]
        vs = [v_ref[pl.ds(ks, tk), :] for ks in kss]
        bands = [_band_mask(qi, tq, tk, ks) if mode == "dil" else None for qi, ks in zip(qis, kss)]
        ss = []
        for sub in range(nsub):
            qst = _stack_heads(q_ref[sub * tq:(sub + 1) * tq, :], lanes, pair, qscale)
            k = k_ref[pl.ds(kss[sub], tk), :]
            ss.append(_scores(mode, qst, k, sscale, bands[sub], qis[sub], bias_ref, pair))
        ms = [jnp.max(s_, axis=1, keepdims=True) for s_ in ss]
        ps = [jnp.exp(s_ - m) for s_, m in zip(ss, ms)]
        ls = [jnp.sum(p, axis=1, keepdims=True) for p in ps]
        os_ = [jnp.dot(p.astype(BF16), v, preferred_element_type=F32) for p, v in zip(ps, vs)]
        for sub in range(nsub):
            out = _unstack_heads(os_[sub] / ls[sub], lanes, pair, tq)
            lse = ms[sub] + jnp.log(ls[sub])
            lse = _unstack_heads(jnp.broadcast_to(lse, (lse.shape[0], 128)), lanes, pair, tq)
            dst = _folded_rows(qis[sub] * tq, tq, d) if mode == "dil" else slice(sub * tq, (sub + 1) * tq)
            o_ref[dst, :] = out
            l_ref[dst, :] = lse

    in_specs = [pl.BlockSpec((rows, 128), lambda u, i: (i, qcol + u)),
                pl.BlockSpec((lk, 128), lambda u, i: (0, kcol + u)),
                pl.BlockSpec((lk, 128), lambda u, i: (0, vcol + u))]
    args = [q_arr, k_arr, v_arr]
    if mode == "na":
        in_specs.append(pl.BlockSpec((2, NA_ROWS, GRID_W, NA_ROWS * GRID_W), lambda u, i: (u, 0, 0, 0)))
        args.append(bias)
    if mode == "dil":
        out_spec = pl.BlockSpec((SEQ, 128), lambda u, i: (0, u))
    else:
        out_spec = pl.BlockSpec((rows, 128), lambda u, i: (i, u))
    return pl.pallas_call(
        body, name=name, grid=(cfg["units"], SEQ // rows), in_specs=in_specs, out_specs=[out_spec, out_spec],
        out_shape=[jax.ShapeDtypeStruct((SEQ, 512), F32), jax.ShapeDtypeStruct((SEQ, 512), F32)],
        compiler_params=_params(("parallel", "arbitrary")))(*args)


def _attn_bwd(name, mode, q_arr, k_arr, v_arr, qcol, kcol, vcol, do, lse, dp=None, o=None, d=1, bias=None,
              tabs=None):
    cfg = _attn_cfg(mode, d)
    pair, tq, tk, mlen, lk, scale = cfg["pair"], cfg["tq"], cfg["tk"], cfg["mlen"], cfg["lk"], cfg["scale"]
    qscale, sscale = (scale, 1.0) if pair else (1.0, scale)
    nsub = cfg["nsub"]
    rows = nsub * tq
    nq = SEQ // rows
    kv_dtype = F32 if mode == "mem" else BF16

    def body(*refs):
        refs = list(refs)
        q_ref, k_ref, v_ref, do_ref, l_ref = refs[:5]
        rest = refs[5:]
        bias_ref = tq_ref = tk_ref = db_ref = None
        if mode == "dil":
            dp_ref, tq_ref, tk_ref, dq_ref, dk_ref, dv_ref, dk_acc, dv_acc = rest
        elif mode == "na":
            o_ref, bias_ref, dq_ref, dk_ref, dv_ref, db_ref, dk_acc, dv_acc = rest
        else:
            o_ref, dq_ref, dk_ref, dv_ref, dk_acc, dv_acc = rest
        step = pl.program_id(1)

        @pl.when(step == 0)
        def _():
            dk_acc[...] = jnp.zeros((lk, 128), F32)
            dv_acc[...] = jnp.zeros((lk, 128), F32)
            if mode == "na":
                db_ref[...] = jnp.zeros(db_ref.shape, F32)

        lanes = _iota((tq, 128), 1)
        qis = [step * nsub + sub for sub in range(nsub)]
        sls = [slice(sub * tq, (sub + 1) * tq) for sub in range(nsub)]
        kss = [_window(mode, qi, tq, mlen, tk) for qi in qis]
        ks_ = [k_ref[pl.ds(ks, tk), :] for ks in kss]
        vs = [v_ref[pl.ds(ks, tk), :] for ks in kss]
        qsts, dosts, lses, dphs = [], [], [], []
        for sub in range(nsub):
            if mode == "dil":
                src = _folded_rows(qis[sub] * tq, tq, d)
                dov = do_ref[src, :].astype(BF16)
                lsev = l_ref[src, :]
                dphs.append(_stack_rows(dp_ref[src, :], lanes, pair))
            else:
                dov = do_ref[sls[sub], :]
                lsev = l_ref[sls[sub], :]
                dpv = dov.astype(F32) * o_ref[sls[sub], :]
                if pair:
                    dphs.append(jnp.concatenate(
                        [jnp.sum(jnp.where(_head_lanes(lanes, hh), dpv, 0.0), axis=1, keepdims=True)
                         for hh in range(2)], axis=0))
                else:
                    dphs.append(jnp.sum(dpv, axis=1, keepdims=True))
            qsts.append(_stack_heads(q_ref[sls[sub], :], lanes, pair, qscale))
            dosts.append(_stack_heads(dov, lanes, pair))
            lses.append(_stack_rows(lsev, lanes, pair))
        bands = [_band_mask(qi, tq, tk, ks) if mode == "dil" else None for qi, ks in zip(qis, kss)]
        ss = [_scores(mode, qsts[sub], ks_[sub], sscale, bands[sub], qis[sub], bias_ref, pair) for sub in range(nsub)]
        dpms = [lax.dot_general(dosts[sub], vs[sub], NT, preferred_element_type=F32) for sub in range(nsub)]
        ps = [jnp.exp(s_ - lse) for s_, lse in zip(ss, lses)]
        dss = [p * (dpm - dph) for p, dpm, dph in zip(ps, dpms, dphs)]
        if mode == "na":
            for sub, ds in enumerate(dss):
                off = qis[sub] - jnp.clip(qis[sub] - NA_ROWS // 2, 0, SEQ // GRID_W - NA_ROWS)
                db_ref[0, off] += ds[:tq]
                db_ref[1, off] += ds[tq:]
        dsbs = [ds.astype(BF16) for ds in dss]
        dvs = [lax.dot_general(p.astype(BF16), dosts[sub], TN, preferred_element_type=F32)
               for sub, p in enumerate(ps)]
        dqs = [jnp.dot(dsb, ks_[sub], preferred_element_type=F32) * scale for sub, dsb in enumerate(dsbs)]
        dks = [lax.dot_general(dsb, qsts[sub], TN, preferred_element_type=F32) for sub, dsb in enumerate(dsbs)]
        for sub in range(nsub):
            sl = sls[sub]
            dq = _unstack_heads(dqs[sub], lanes, pair, tq)
            if mode == "dil":
                dq = _rope_t(dq, tq_ref[0, sl, :], tq_ref[1, sl, :], tq_ref[2, sl, :])
            dq_ref[sl, :] = dq.astype(BF16)
            dk_acc[pl.ds(kss[sub], tk), :] += dks[sub] if pair else dks[sub] * scale
            dv_acc[pl.ds(kss[sub], tk), :] += dvs[sub]

        @pl.when(step == nq - 1)
        def _():
            dkv = dk_acc[...]
            if mode == "dil":
                dkv = _rope_t(dkv, tk_ref[0], tk_ref[1], tk_ref[2])
            dk_ref[...] = dkv.astype(kv_dtype)
            dv_ref[...] = dv_acc[...].astype(kv_dtype)

    q_spec = pl.BlockSpec((rows, 128), lambda u, i: (i, qcol + u))
    row_spec = pl.BlockSpec((rows, 128), lambda u, i: (i, u))
    kv_out = pl.BlockSpec((lk, 128), lambda u, i: (0, u))
    whole = pl.BlockSpec((SEQ, 128), lambda u, i: (0, u))
    nat_spec = whole if mode == "dil" else row_spec
    in_specs = [q_spec,
                pl.BlockSpec((lk, 128), lambda u, i: (0, kcol + u)),
                pl.BlockSpec((lk, 128), lambda u, i: (0, vcol + u)),
                nat_spec, nat_spec]
    args = [q_arr, k_arr, v_arr, do, lse]
    out_specs = [row_spec, kv_out, kv_out]
    out_shape = [jax.ShapeDtypeStruct((SEQ, 512), BF16), jax.ShapeDtypeStruct((lk, 512), kv_dtype),
                 jax.ShapeDtypeStruct((lk, 512), kv_dtype)]
    if mode == "dil":
        in_specs += [whole, pl.BlockSpec((3, rows, 128), lambda u, i: (0, i, 0)),
                     pl.BlockSpec((3, SEQ, 128), lambda u, i: (0, 0, 0))]
        args += [dp, tabs, tabs]
    elif mode == "na":
        b_spec = pl.BlockSpec((2, NA_ROWS, GRID_W, NA_ROWS * GRID_W), lambda u, i: (u, 0, 0, 0))
        in_specs += [row_spec, b_spec]
        args += [o, bias]
        out_specs.append(b_spec)
        out_shape.append(jax.ShapeDtypeStruct((8, NA_ROWS, GRID_W, NA_ROWS * GRID_W), F32))
    else:
        in_specs.append(row_spec)
        args.append(o)
    return pl.pallas_call(
        body, name=name, grid=(cfg["units"], nq), in_specs=in_specs, out_specs=out_specs, out_shape=out_shape,
        scratch_shapes=[pltpu.VMEM((lk, 128), F32), pltpu.VMEM((lk, 128), F32)],
        compiler_params=_params(("parallel", "arbitrary")))(*args)


def _na_geometry():
    qc = _iota((GRID_W, 128), 0)
    lane = _iota((GRID_W, 128), 1)
    kc = lane & 63
    c_start = jnp.clip(qc - 8, 0, GRID_W - 16)
    valid = jnp.logical_and(kc >= c_start, kc < c_start + 16)
    return lane, valid


def _na_bias(rpb_rows, dep=None):
    dep_specs, dep_args = _dep_operand(dep)

    def body(r_ref, *rest):
        o_ref, t_ref = rest[-2:]
        lane, valid = _na_geometry()
        for dd in range(14):
            row_a = jnp.broadcast_to(r_ref[dd:dd + 1, :], (GRID_W, 128))
            row_b = jnp.broadcast_to(r_ref[dd + 1:dd + 2, :], (GRID_W, 128))
            both = jnp.where(lane < 64, row_a, pltpu.roll(row_b, 64, 1))
            t = pltpu.roll(both, 128 - 15, 1, stride=1, stride_axis=0)
            t_ref[dd] = jnp.where(valid, t, NEG)
        for off in range(NA_ROWS):
            for p in range(4):
                o_ref[off, :, p * 128:(p + 1) * 128] = t_ref[2 * p - off + 7]

    return pl.pallas_call(
        body, name="na_bias", grid=(8,),
        in_specs=[pl.BlockSpec((None, 16, 128), lambda h: (h, 0, 0))] + dep_specs,
        out_specs=pl.BlockSpec((None, NA_ROWS, GRID_W, NA_ROWS * GRID_W), lambda h: (h, 0, 0, 0)),
        out_shape=jax.ShapeDtypeStruct((8, NA_ROWS, GRID_W, NA_ROWS * GRID_W), F32),
        scratch_shapes=[pltpu.VMEM((14, GRID_W, 128), F32)],
        compiler_params=_params(("parallel",)))(rpb_rows, *dep_args)


def _na_bias_bwd(dbias, dep=None):
    dep_specs, dep_args = _dep_operand(dep)

    def body(d_ref, *rest):
        o_ref = rest[-1]
        lane, valid = _na_geometry()
        reverse = (_iota((GRID_W, GRID_W), 0) + _iota((GRID_W, GRID_W), 1) == GRID_W - 1).astype(F32)
        o_ref[...] = jnp.zeros((16, 128), F32)
        for dd in range(14):
            t = jnp.zeros((GRID_W, 128), F32)
            for off in range(NA_ROWS):
                for p in range(4):
                    if 2 * p - off + 7 == dd:
                        t = t + d_ref[off, :, p * 128:(p + 1) * 128]
            t = jnp.dot(reverse, jnp.where(valid, t, 0.0), precision=lax.Precision.HIGHEST,
                        preferred_element_type=F32)
            t = pltpu.roll(t, 128 - (GRID_W - 16), 1, stride=1, stride_axis=0)
            o_ref[dd:dd + 1, :] = jnp.sum(t, axis=0, keepdims=True)

    return pl.pallas_call(
        body, name="na_bias_bwd", grid=(8,),
        in_specs=[pl.BlockSpec((None, NA_ROWS, GRID_W, NA_ROWS * GRID_W), lambda h: (h, 0, 0, 0))] + dep_specs,
        out_specs=pl.BlockSpec((None, 16, 128), lambda h: (h, 0, 0)),
        out_shape=jax.ShapeDtypeStruct((8, 16, 128), F32),
        compiler_params=_params(("parallel",)))(dbias, *dep_args)


GATE_ROWS = 128


def _group_weights(l0, l1, l2):
    m = jnp.maximum(jnp.maximum(l0, l1), l2)
    e0, e1, e2 = jnp.exp(l0 - m), jnp.exp(l1 - m), jnp.exp(l2 - m)
    inv = 1.0 / (e0 + e1 + e2)
    return e0 * inv, e1 * inv, e2 * inv


def _gate_block(o_grp, l_grp, out_b, out_c, parts, x, target, merge_bias, wts, w_out, gain, head_sum):
    rows = GATE_ROWS
    r512 = pl.BlockSpec((rows, 512), lambda i: (i, 0))
    r1024 = pl.BlockSpec((rows, D_MODEL), lambda i: (i, 0))
    silu_cols = [pl.BlockSpec((rows, 512), functools.partial(lambda b, i: (i, b), 13 + b)) for b in range(3)]
    logit_cols = [pl.BlockSpec((rows, D_MODEL), functools.partial(lambda b, i: (i, b), 8 + b)) for b in range(3)]

    def body(o0, o1, o2, l0, l1, l2, ob, oc, ga, gb, gc, la, lb, lc, x_ref, t_ref, mb, wa, wb, wc, wo_ref, gn_ref,
             hs_ref, dout_ref, dla, dlb, dlc, dga, dgb, dgc, do0, do1, do2, dp0, dp1, dp2, dob, doc, err_ref, gg_ref,
             gmb, gwa, gwb, gwc, gwo, acc_a, acc_b, acc_c, acc_o):
        step = pl.program_id(0)
        ws = _group_weights(l0[...], l1[...], l2[...])
        out_a = ws[0] * o0[...] + ws[1] * o1[...] + ws[2] * o2[...]
        branches = ((out_a, ga, la, wa, acc_a, dla, dga), (ob[...], gb, lb, wb, acc_b, dlb, dgb),
                    (oc[...], gc, lc, wc, acc_c, dlc, dgc))

        @pl.when(step == 0)
        def _():
            for acc in (acc_a, acc_b, acc_c, acc_o):
                acc[...] = jnp.zeros(acc.shape, F32)
            err_ref[...] = jnp.zeros((1, D_MODEL), F32)
            gg_ref[...] = jnp.zeros((1, D_MODEL), F32)
            gmb[...] = jnp.zeros((3, D_MODEL), F32)

        y = jnp.zeros((rows, D_MODEL), F32)
        zs, gates, silus, dsilus, us = [], [], [], [], []
        for b, (ov, g_ref, l_ref, w_ref, _, _, _) in enumerate(branches):
            g = g_ref[...].astype(F32)
            sg = _sigmoid(g)
            silus.append(g * sg)
            dsilus.append(sg * (1.0 + g * (1.0 - sg)))
            us.append((ov * silus[b]).astype(BF16))
            zs.append(lax.dot_general(us[b], w_ref[...], NT, preferred_element_type=F32))
            gates.append(_sigmoid(l_ref[...].astype(F32) + mb[b:b + 1, :]))
            y = y + gates[b] * zs[b]
        yb = y.astype(BF16)
        y2 = jnp.dot(yb, wo_ref[...], preferred_element_type=F32)
        rstd = lax.rsqrt(jnp.mean(y2 * y2, axis=1, keepdims=True) + EPS)
        yn = y2 * rstd
        gv = gn_ref[...]
        err = x_ref[...] + yn * gv - t_ref[...]
        dout = err * (1.0 / D_MODEL)
        dout_ref[...] = dout
        dn = dout * gv
        dy2 = (rstd * (dn - yn * jnp.mean(dn * yn, axis=1, keepdims=True))).astype(BF16)
        acc_o[...] += lax.dot_general(yb, dy2, TN, preferred_element_type=F32)
        err_ref[...] += jnp.sum(err * err, axis=0, keepdims=True)
        gg_ref[...] += jnp.sum(dout * yn, axis=0, keepdims=True)
        dy = lax.dot_general(dy2, wo_ref[...], NT, preferred_element_type=F32)
        dos = []
        for b, (ov, _, _, w_ref, acc, dl_ref, dg_ref) in enumerate(branches):
            dl = dy * zs[b] * gates[b] * (1.0 - gates[b])
            dl_ref[...] = dl.astype(BF16)
            gmb[b:b + 1, :] += jnp.sum(dl, axis=0, keepdims=True)
            dz = (dy * gates[b]).astype(BF16)
            acc[...] += lax.dot_general(dz, us[b], TN, preferred_element_type=F32)
            du = jnp.dot(dz, w_ref[...], preferred_element_type=F32)
            dos.append(du * silus[b])
            dg_ref[...] = (du * ov * dsilus[b]).astype(BF16)
        dob[...] = dos[1].astype(BF16)
        doc[...] = dos[2].astype(BF16)
        row_term = jnp.dot(dos[0] * out_a, hs_ref[...], precision=lax.Precision.HIGHEST, preferred_element_type=F32)
        for wg, do_ref, dp_ref in zip(ws, (do0, do1, do2), (dp0, dp1, dp2)):
            do_ref[...] = wg * dos[0]
            dp_ref[...] = wg * row_term

        @pl.when(step == SEQ // rows - 1)
        def _():
            for acc, out in ((acc_a, gwa), (acc_b, gwb), (acc_c, gwc), (acc_o, gwo)):
                out[...] = acc[...].astype(BF16)

    full = lambda shape: pl.BlockSpec(shape, lambda i: (0,) * len(shape))
    vec = pl.BlockSpec((1, D_MODEL), lambda i: (0, 0))
    acc3 = pl.BlockSpec((3, D_MODEL), lambda i: (0, 0))
    in_specs = ([r512] * 8 + silu_cols + logit_cols + [r1024, r1024, full((3, D_MODEL))]
                + [full((D_MODEL, 512))] * 3 + [full((D_MODEL, D_MODEL)), vec, full((512, 512))])
    out_specs = ([r1024] + [r1024] * 3 + [r512] * 3 + [r512] * 6 + [r512] * 2 + [vec, vec, acc3]
                 + [full((D_MODEL, 512))] * 3 + [full((D_MODEL, D_MODEL))])
    bf, f32 = BF16, F32
    sds = jax.ShapeDtypeStruct
    out_shape = ([sds((SEQ, D_MODEL), f32)] + [sds((SEQ, D_MODEL), bf)] * 3 + [sds((SEQ, 512), bf)] * 3
                 + [sds((SEQ, 512), f32)] * 6 + [sds((SEQ, 512), bf)] * 2 + [sds((1, D_MODEL), f32)] * 2
                 + [sds((3, D_MODEL), f32)] + [sds((D_MODEL, 512), bf)] * 3 + [sds((D_MODEL, D_MODEL), bf)])
    res = pl.pallas_call(
        body, name="gate_block", grid=(SEQ // rows,), in_specs=in_specs, out_specs=out_specs, out_shape=out_shape,
        scratch_shapes=[pltpu.VMEM((D_MODEL, 512), F32)] * 3 + [pltpu.VMEM((D_MODEL, D_MODEL), F32)],
        compiler_params=_params(("arbitrary",)))(
            *o_grp, *l_grp, out_b, out_c, parts, parts, parts, parts, parts, parts, x, target, merge_bias, *wts, w_out,
            gain, head_sum)
    return dict(dout=res[0], dlog=res[1:4], dg=res[4:7], do_grp=res[7:10], dp_grp=res[10:13], do_b=res[13],
                do_c=res[14], err_sq=res[15], g_post=res[16], g_mb=res[17], g_wt=res[18:21], g_w_out=res[21])


def _local_step(x, hst, parts, tabs, bias, mem, target, pre_norm, mem_norm, post_norm, wt_in, late_weights,
                reduce_start=None):
    o_grp, l_grp = [], []
    for g, d in enumerate(DILATIONS):
        o, l = _attn_fwd("dil_fwd_%d" % g, "dil", parts, parts, parts, 12 * g, 12 * g + 4, 12 * g + 8, d=d)
        o_grp.append(o)
        l_grp.append(l)
    out_b, lse_b = _attn_fwd("na_fwd", "na", parts, parts, parts, 36, 40, 44, bias=bias)
    merge_bias, w_kv, wt_a, wt_b, wt_c, w_out = late_weights(sum(a[:8, :128] for a in [out_b] + o_grp))
    memn = _rmsnorm_fwd("memnorm", mem, mem_norm, MEM_LEN)
    kv_m = _mm_simple("mem_kv", memn, w_kv, NN, BF16, MEM_LEN, 512, D_MODEL)
    out_c, lse_c = _attn_fwd("mem_fwd", "mem", parts, kv_m, kv_m, 48, 0, 4)

    rr = _iota((512, 512), 0) // HEAD_DIM
    cc = _iota((512, 512), 1) // HEAD_DIM
    head_sum = (rr == cc).astype(F32)
    gb = _gate_block(o_grp, l_grp, out_b, out_c, parts, x, target, merge_bias, (wt_a, wt_b, wt_c), w_out, post_norm,
                     head_sum)
    dout, dlog, dg, g_wt, g_w_out = gb["dout"], gb["dlog"], gb["dg"], gb["g_wt"], gb["g_w_out"]
    do_grp, dp_grp, do_b, do_c, g_post, g_mb = (gb["do_grp"], gb["dp_grp"], gb["do_b"], gb["do_c"], gb["g_post"],
                                                gb["g_mb"])
    loss = 0.5 * jnp.sum(gb["err_sq"]) / D_MODEL

    dqkv = []
    for g, d in enumerate(DILATIONS):
        dq, dk, dv = _attn_bwd("dil_bwd_%d" % g, "dil", parts, parts, parts, 12 * g, 12 * g + 4, 12 * g + 8,
                               do_grp[g], l_grp[g], dp=dp_grp[g], d=d, tabs=tabs[g])
        dqkv += [dq, dk, dv]
    dq_b, dk_b, dv_b, dbias = _attn_bwd("na_bwd", "na", parts, parts, parts, 36, 40, 44, do_b, lse_b, o=out_b,
                                        bias=bias)
    dq_c, dk_m, dv_m = _attn_bwd("mem_bwd", "mem", parts, kv_m, kv_m, 48, 0, 4, do_c, lse_c, o=out_c)

    dkv = jnp.concatenate([dk_m, dv_m], axis=1).astype(BF16)
    g_w_kv = _mm_simple("mem_kv_dw", memn, dkv, TN, BF16, D_MODEL, 512, MEM_LEN)
    dmemn = _mm_simple("mem_kv_dx", dkv, w_kv, NT, F32, MEM_LEN, 512, D_MODEL)

    grads = dict(w_kv=g_w_kv, wt_a=g_wt[0], wt_b=g_wt[1], wt_c=g_wt[2], w_out=g_w_out, merge_bias=g_mb,
                 post_norm=g_post)
    dep = None
    if reduce_start is not None:
        reduce_start("rest_sibling", grads)
        dep = reduce_start("rest_chips", grads, sum(a[:8, :128] for a in (dqkv[0], dqkv[3], dqkv[6], dq_b, dq_c)))
    dparts = dqkv + [dq_b, dk_b, dv_b, dq_c] + list(dg) + list(dlog)
    grads["wt_in"] = _in_proj_dw(dparts, hst, dep)
    dep = reduce_start("w_in", grads) if reduce_start is not None else None
    dh = _in_proj_dh(dparts, wt_in, dep)
    if reduce_start is not None:
        dep = reduce_start("w_in_second", grads, dh)
    grad_x, grads["pre_norm"] = _prenorm_bwd(x, pre_norm, dh, dout)
    g_rpb_t = _na_bias_bwd(dbias, dep)
    grads["na_rpb"] = g_rpb_t[:, :15, :31] + jnp.pad(g_rpb_t[:, :14, 64:95], ((0, 0), (1, 0), (0, 0)))
    grads["mem_norm"] = _memnorm_bwd(mem, dmemn, dep)
    return loss, grad_x, grads


ANY = pl.BlockSpec(memory_space=pl.ANY)


def _place():
    return lax.axis_index("x"), lax.axis_index("y"), lax.axis_index("c")


HBM = pl.BlockSpec(memory_space=pltpu.HBM)
SEM = pl.BlockSpec(memory_space=pltpu.SEMAPHORE)
DATAFLOW = pltpu.SideEffectType.DATAFLOW_SIDE_EFFECTING


def _split_copies(kind, srcs, lands, send_sems, recv_sems):
    nt = len(srcs)
    x, y, c = _place()
    copies = []
    if kind == "sibling":
        for q in range(4):
            for t in range(nt):
                k = q * nt + t
                copies.append(pltpu.make_async_remote_copy(
                    src_ref=srcs[t].at[2 * q + 1 - c], dst_ref=lands[t].at[q], send_sem=send_sems.at[k],
                    recv_sem=recv_sems.at[k], device_id=(x, y, 1 - c), device_id_type=MESH_ID))
    elif kind in ("rs_a", "rs_b"):
        half = lands[0].shape[1]
        xn, yn = (1 - x, y, c), (x, 1 - y, c)
        q_xn, q_yn, q_dg = 2 * (1 - x) + y, 2 * x + 1 - y, 2 * (1 - x) + 1 - y
        if kind == "rs_a":
            plan = [(srcs[0].at[q_yn].at[pl.ds(0, half)], 0, yn), (srcs[0].at[q_dg].at[pl.ds(0, half)], 1, yn),
                    (srcs[0].at[q_xn].at[pl.ds(half, half)], 2, xn), (srcs[0].at[q_dg].at[pl.ds(half, half)], 3, xn)]
        else:
            plan = [(srcs[0].at[0], 0, xn), (srcs[0].at[1], 1, yn)]
        for k, (src, slot, to) in enumerate(plan):
            copies.append(pltpu.make_async_remote_copy(
                src_ref=src, dst_ref=lands[0].at[slot], send_sem=send_sems.at[k], recv_sem=recv_sems.at[k],
                device_id=to, device_id_type=MESH_ID))
    elif kind == "gather":
        me = 4 * x + 2 * y + c
        for mask in range(1, 8):
            fx, fy, fc = (mask >> 2) & 1, (mask >> 1) & 1, mask & 1
            to = (1 - x if fx else x, 1 - y if fy else y, 1 - c if fc else c)
            for t in range(nt):
                k = (mask - 1) * nt + t
                copies.append(pltpu.make_async_remote_copy(
                    src_ref=srcs[t], dst_ref=lands[t].at[me], send_sem=send_sems.at[k], recv_sem=recv_sems.at[k],
                    device_id=to, device_id_type=MESH_ID))
    else:
        for s, (tx, ty) in enumerate([(1 - x, y), (x, 1 - y), (1 - x, 1 - y)]):
            for t in range(nt):
                k = s * nt + t
                copies.append(pltpu.make_async_remote_copy(
                    src_ref=srcs[t].at[2 * tx + ty], dst_ref=lands[t].at[s], send_sem=send_sems.at[k],
                    recv_sem=recv_sems.at[k], device_id=(tx, ty, c), device_id_type=MESH_ID))
    return copies


def _split_count(kind, nt):
    return {"gather": 7, "chips": 3, "sibling": 4, "rs_a": 4, "rs_b": 2}[kind] * nt


def _exchange_start(name, kind, srcs, land_shapes, after=None):
    nt = len(srcs)
    n = _split_count(kind, nt)
    dep_specs, dep_args = _dep_operand(after)
    nd = len(dep_args)

    def body(*refs):
        src_refs, land_refs = refs[:nt], refs[nt:2 * nt]
        send_sems, recv_sems = refs[2 * nt + nd], refs[2 * nt + nd + 1]
        token = refs[-1]
        for cp in _split_copies(kind, src_refs, land_refs, send_sems, recv_sems):
            cp.start()
        token[...] = jnp.zeros_like(token)

    lands = [pltpu.with_memory_space_constraint(lax.empty(s.shape, s.dtype), pltpu.HBM) for s in land_shapes]
    res = pl.pallas_call(
        body, name=name,
        out_shape=(pltpu.SemaphoreType.DMA((n,)), pltpu.SemaphoreType.DMA((n,)),
                   *[pltpu.HBM(s.shape, s.dtype) for s in srcs], *[pltpu.HBM(s.shape, s.dtype) for s in land_shapes],
                   jax.ShapeDtypeStruct((8, 128), F32)),
        in_specs=[HBM] * (2 * nt) + dep_specs,
        out_specs=(SEM, SEM, *([HBM] * (2 * nt)), pl.BlockSpec(memory_space=pltpu.VMEM)),
        input_output_aliases={i: 2 + i for i in range(2 * nt)},
        compiler_params=pltpu.CompilerParams(has_side_effects=DATAFLOW))(
            *[pltpu.with_memory_space_constraint(s, pltpu.HBM) for s in srcs], *lands, *dep_args)
    return res[0], res[1], list(res[2:2 + nt]), list(res[2 + nt:2 + 2 * nt]), res[-1]


def _exchange_wait(name, kind, send_sems, recv_sems, srcs, lands, after):
    nt = len(srcs)

    def body(*refs):
        src_refs, land_refs = refs[:nt], refs[nt:2 * nt]
        s_sems, r_sems = refs[2 * nt], refs[2 * nt + 1]
        for cp in _split_copies(kind, src_refs, land_refs, s_sems, r_sems):
            cp.wait_send()
            cp.wait_recv()

    res = pl.pallas_call(
        body, name=name,
        out_shape=tuple(pltpu.HBM(s.shape, s.dtype) for s in list(srcs) + list(lands)),
        in_specs=[HBM] * (2 * nt) + [SEM, SEM, pl.BlockSpec(memory_space=pl.ANY)],
        out_specs=tuple([HBM] * (2 * nt)),
        input_output_aliases={i: i for i in range(2 * nt)},
        compiler_params=pltpu.CompilerParams(has_side_effects=DATAFLOW))(
            *srcs, *lands, send_sems, recv_sems, after)
    return list(res[:nt]), list(res[nt:])


AG_GROUPS = ((0, 3), (3, 4), (7, 2))


def _ag_phase(name, own, land, sems, waits, starts, after=None):
    r = own.shape[0]
    half = r // 2
    ns = len(sems)
    dep_specs, dep_args = _dep_operand(after)
    nd = len(dep_args)
    new_group = None
    if starts:
        (new_group,) = [g for g, (first, n) in enumerate(AG_GROUPS) if first == starts[0]]
        assert list(starts) == list(range(AG_GROUPS[new_group][0], sum(AG_GROUPS[new_group])))

    def body(*refs):
        own_ref, land_ref = refs[0], refs[1]
        sem_refs = list(refs[2:2 + 2 * ns])
        outs = refs[2 + 2 * ns + nd:]
        if starts:
            sem_refs += [outs[0], outs[1]]
        x, y, c = _place()
        me, sib = (x, y, c), (x, y, 1 - c)
        xn, yn, dg = (1 - x, y, c), (x, 1 - y, c), (1 - x, 1 - y, c)

        def other(dev):
            return (dev[0], dev[1], 1 - dev[2])

        def rows(dev, part):
            blk = land_ref.at[4 * dev[0] + 2 * dev[1] + dev[2]]
            return blk if part is None else blk.at[pl.ds(part * half, half)]

        def sem_of(k):
            (g,) = [g for g, (first, n) in enumerate(AG_GROUPS) if first <= k < first + n]
            return sem_refs[2 * g].at[k - AG_GROUPS[g][0]], sem_refs[2 * g + 1].at[k - AG_GROUPS[g][0]]

        sent = {0: (me, None, sib), 1: (me, None, xn), 2: (me, None, yn), 3: (xn, 0, yn), 4: (yn, 1, xn),
                5: (xn, None, sib), 6: (yn, None, sib), 7: (dg, 0, sib), 8: (dg, 1, sib)}
        landed = {0: (sib, None), 1: (xn, None), 2: (yn, None), 3: (dg, 0), 4: (dg, 1), 5: (other(xn), None),
                  6: (other(yn), None), 7: (other(dg), 0), 8: (other(dg), 1)}

        def copy(k, receiving):
            send_sem, recv_sem = sem_of(k)
            dev, part, to = (*landed[k], me) if receiving else sent[k]
            src = own_ref if (dev is me and not receiving) else rows(dev, part)
            return pltpu.make_async_remote_copy(src_ref=src, dst_ref=rows(dev, part), send_sem=send_sem,
                                                recv_sem=recv_sem, device_id=to, device_id_type=MESH_ID)

        for kind, k in waits:
            if kind == "recv":
                copy(k, True).wait_recv()
            else:
                copy(k, False).wait_send()
        for k in starts:
            copy(k, False).start()
        if starts:
            outs[-1][...] = jnp.zeros_like(outs[-1])

    n_new = AG_GROUPS[new_group][1] if starts else 0
    sem_out = (pltpu.SemaphoreType.DMA((n_new,)), pltpu.SemaphoreType.DMA((n_new,))) if starts else ()
    token_out = (jax.ShapeDtypeStruct((8, 128), F32),) if starts else ()
    res = pl.pallas_call(
        body, name=name,
        out_shape=(*sem_out, pltpu.HBM(own.shape, own.dtype), pltpu.HBM(land.shape, land.dtype), *token_out),
        in_specs=[HBM, HBM] + [SEM] * (2 * ns) + dep_specs,
        out_specs=(*([SEM] * len(sem_out)), HBM, HBM, *([pl.BlockSpec(memory_space=pltpu.VMEM)] * len(token_out))),
        input_output_aliases={0: len(sem_out), 1: len(sem_out) + 1},
        compiler_params=pltpu.CompilerParams(has_side_effects=DATAFLOW))(
            own, land, *[a for pair in sems for a in pair], *dep_args)
    if starts:
        return (res[0], res[1]), res[2], res[3], res[4]
    return None, res[0], res[1], None


def _add_sibling(name, term, recv, rows):
    _, r, w = term.shape
    cidx = lax.axis_index("c").astype(jnp.int32).reshape(1)
    like_term = recv.shape[0] == N_DEV

    def body(c_ref, a_ref, b_ref, o_ref):
        o_ref[...] = (a_ref[...].astype(F32) + b_ref[...].astype(F32)).astype(o_ref.dtype)

    grid_spec = pltpu.PrefetchScalarGridSpec(
        num_scalar_prefetch=1, grid=(4, r // rows),
        in_specs=[pl.BlockSpec((None, rows, w), lambda q, i, c_ref: (2 * q + c_ref[0], i, 0)),
                  pl.BlockSpec((None, rows, w), lambda q, i, c_ref: (2 * q + c_ref[0] if like_term else q, i, 0))],
        out_specs=pl.BlockSpec((None, rows, w), lambda q, i, c_ref: (q, i, 0)))
    return pl.pallas_call(
        body, name=name, grid_spec=grid_spec, out_shape=jax.ShapeDtypeStruct((4, r, w), term.dtype),
        compiler_params=_params(("parallel", "parallel")))(cidx, term, recv)


def _add_sibling_small(name, terms, recvs):
    nt = len(terms)

    def body(*refs):
        c = lax.axis_index("c")
        for t_ref, r_ref, o_ref in zip(refs[:nt], refs[nt:2 * nt], refs[2 * nt:]):
            for q in range(4):
                o_ref[q] = (t_ref[2 * q + c].astype(F32) + r_ref[q].astype(F32)).astype(o_ref.dtype)

    return pl.pallas_call(
        body, name=name, out_shape=[jax.ShapeDtypeStruct((4,) + t.shape[1:], t.dtype) for t in terms],
        compiler_params=_params())(*terms, *recvs)


def _reduce_scatter_start(tag, terms, recv1):
    sums = _add_sibling_small("add_sibling_" + tag, terms, recv1)
    lands = [jax.ShapeDtypeStruct((3,) + s.shape[1:], s.dtype) for s in sums]
    send_sems, recv_sems, sums, lands, token = _exchange_start("exchange_chips_start_" + tag, "chips", sums, lands)
    return (tag, send_sems, recv_sems, sums, lands), token


def _reduce_scatter_wait(state, after):
    tag, send_sems, recv_sems, sums, lands = state
    return _exchange_wait("exchange_chips_wait_" + tag, "chips", send_sems, recv_sems, sums, lands, after)


def _adamw(name, w, g, m, v, dep=None):
    dep_specs, dep_args = _dep_operand(dep)

    def body(w_ref, g_ref, m_ref, v_ref, *rest):
        d_ref, nm_ref, nv_ref = rest[-3:]
        d_ref[...], nm_ref[...], nv_ref[...] = _adam_math(w_ref[...], g_ref[...], m_ref[...], v_ref[...])

    whole = pl.BlockSpec(memory_space=pltpu.VMEM)
    return pl.pallas_call(
        body, name=name, in_specs=[whole] * 4 + dep_specs, out_shape=[jax.ShapeDtypeStruct(w.shape, F32)] * 3,
        compiler_params=_params())(w, g, m, v, *dep_args)


def _adam_math(w, g, m, v):
    nm = ADAM_B1 * m + (1.0 - ADAM_B1) * g
    nv = ADAM_B2 * v + (1.0 - ADAM_B2) * (g * g)
    c1 = 1.0 - ADAM_B1 ** ADAM_STEP
    c2 = 1.0 - ADAM_B2 ** ADAM_STEP
    return -ADAM_LR * ((nm / c1) / (jnp.sqrt(nv / c2) + ADAM_EPS) + ADAM_WD * w), nm, nv


def _presum_halves(sums, landed):
    _, r, w = sums.shape
    rows = r // 4
    x, y = lax.axis_index("x"), lax.axis_index("y")
    dest = jnp.stack([2 * (1 - x) + y, 2 * x + 1 - y]).astype(jnp.int32)

    def body(q_ref, a_ref, b_ref, o_ref):
        o_ref[...] = (a_ref[...].astype(F32) + b_ref[...].astype(F32)).astype(o_ref.dtype)

    grid_spec = pltpu.PrefetchScalarGridSpec(
        num_scalar_prefetch=1, grid=(2, 2),
        in_specs=[pl.BlockSpec((None, rows, w), lambda h, i, q_ref: (q_ref[h], 2 * h + i, 0)),
                  pl.BlockSpec((None, rows, w), lambda h, i, q_ref: (1 + 2 * h, i, 0))],
        out_specs=pl.BlockSpec((None, rows, w), lambda h, i, q_ref: (h, i, 0)))
    return pl.pallas_call(
        body, name="presum_halves", grid_spec=grid_spec, out_shape=jax.ShapeDtypeStruct((2, r // 2, w), sums.dtype),
        compiler_params=_params(("parallel", "parallel")))(dest, sums, landed)


def _adamw_halves(name, sums, landed_a, landed_b, w, m, v, rows):
    r, c = w.shape
    half = c // 2
    qidx = (2 * lax.axis_index("x") + lax.axis_index("y")).astype(jnp.int32).reshape(1)

    def body(q_ref, s_ref, a_ref, b_ref, w_ref, m_ref, v_ref, g_ref, d_ref, nm_ref, nv_ref):
        first = (s_ref[:half, :].astype(F32) + a_ref[0].astype(F32)) + b_ref[0].astype(F32)
        second = (s_ref[half:, :].astype(F32) + a_ref[2].astype(F32)) + b_ref[1].astype(F32)
        g = jnp.concatenate([first, second], axis=0).T
        g_ref[...] = g
        d_ref[...], nm_ref[...], nv_ref[...] = _adam_math(w_ref[...], g, m_ref[...], v_ref[...])

    row = pl.BlockSpec((rows, c), lambda i, q_ref: (i, 0))
    grid_spec = pltpu.PrefetchScalarGridSpec(
        num_scalar_prefetch=1, grid=(r // rows,),
        in_specs=[pl.BlockSpec((None, c, rows), lambda i, q_ref: (q_ref[0], 0, i)),
                  pl.BlockSpec((4, half, rows), lambda i, q_ref: (0, 0, i)),
                  pl.BlockSpec((2, half, rows), lambda i, q_ref: (0, 0, i)), row, row, row],
        out_specs=[row] * 4)
    return pl.pallas_call(
        body, name=name, grid_spec=grid_spec, out_shape=[jax.ShapeDtypeStruct((r, c), F32)] * 4,
        compiler_params=_params(("parallel",)))(qidx, sums, landed_a, landed_b, w, m, v)


def _adamw_chips_small(name, items):
    n = len(items)

    def body(*refs):
        q = 2 * lax.axis_index("x") + lax.axis_index("y")
        ins, outs = refs[:5 * n], refs[5 * n:]
        for i, (_, _, w, _, _, transposed) in enumerate(items):
            s_ref, r_ref, w_ref, m_ref, v_ref = ins[5 * i:5 * i + 5]
            g_ref, d_ref, nm_ref, nv_ref = outs[4 * i:4 * i + 4]
            g = (s_ref[q].astype(F32) + r_ref[0].astype(F32)) + (r_ref[1].astype(F32) + r_ref[2].astype(F32))
            g = g.T if transposed else g[:w.shape[0]]
            g_ref[...] = g
            d_ref[...], nm_ref[...], nv_ref[...] = _adam_math(w_ref[...], g, m_ref[...], v_ref[...])

    res = pl.pallas_call(
        body, name=name, out_shape=[jax.ShapeDtypeStruct(it[2].shape, F32) for it in items for _ in range(4)],
        compiler_params=_params())(*[a for it in items for a in it[:5]])
    return [res[4 * i:4 * i + 4] for i in range(n)]


def _sum_devices(gathered):
    def body(g_ref, o_ref):
        acc = g_ref[0]
        for j in range(1, N_DEV):
            acc = acc + g_ref[j]
        o_ref[...] = acc

    return pl.pallas_call(
        body, name="sum_devices", out_shape=jax.ShapeDtypeStruct(gathered.shape[1:], F32),
        compiler_params=_params())(gathered)


def _rows128(a, rows):
    flat = a.reshape(-1)
    return jnp.pad(flat, (0, rows * 128 - flat.shape[0])).reshape(rows, 128)


def kernel(x, mem, pre_norm, w_in, merge_bias, na_rpb, mem_norm, w_mem_kv, w_branch_a, w_branch_b, w_branch_c, w_out, post_norm, loss_target, m_pre_norm, m_w_in, m_merge_bias, m_na_rpb, m_mem_norm, m_w_mem_kv, m_w_branch_a, m_w_branch_b, m_w_branch_c, m_w_out, m_post_norm, v_pre_norm, v_w_in, v_merge_bias, v_na_rpb, v_mem_norm, v_w_mem_kv, v_w_branch_a, v_w_branch_b, v_w_branch_c, v_w_out, v_post_norm):
    wt_in_s = w_in[0].T.astype(BF16)
    rows_s = jnp.concatenate([w_mem_kv[0], w_out[0]], axis=0).astype(BF16)
    cols_s = jnp.concatenate([w_branch_a[0].T, w_branch_b[0].T, w_branch_c[0].T], axis=0).astype(BF16)
    mb_s = jnp.pad(merge_bias[0], ((0, 5), (0, 0)))
    me = 4 * lax.axis_index("x") + 2 * lax.axis_index("y") + lax.axis_index("c")

    chip = 2 * lax.axis_index("x") + lax.axis_index("y")

    def first_block(q):
        return jnp.where(q == 0, 0, jnp.where(q == 1, 6, jnp.where(q == 2, 11, 17)))

    five = jnp.arange(5, dtype=jnp.int32)
    near, far = jnp.where(chip < 2, 5, 16), jnp.where(chip < 2, 16, 5)
    order1 = (first_block(chip) + five).astype(jnp.int32)
    order2 = jnp.concatenate([first_block(chip ^ 1) + five, near[None], first_block(chip ^ 2) + five]).astype(jnp.int32)
    order3 = jnp.concatenate([first_block(chip ^ 3) + five, far[None]]).astype(jnp.int32)
    tabs = _rope_tables()

    def weights_of(land):
        return land.reshape(N_IN, D_MODEL)

    land = pltpu.with_memory_space_constraint(lax.empty((N_DEV,) + wt_in_s.shape, BF16), pltpu.HBM)
    own = pltpu.with_memory_space_constraint(wt_in_s, pltpu.HBM)
    sem_a, own, land, token = _ag_phase("ag_start", own, land, [], [], [0, 1, 2])
    hs, hst = _prenorm_fold(x[0], pre_norm, token)
    _, own, land, _ = _ag_phase("ag_wait0", own, land, [sem_a], [("recv", 0)], [], hs)
    land = lax.dynamic_update_slice(land, own[None], (me, 0, 0))
    parts = _in_proj("in_proj_1", hs, weights_of(land), tabs, order1)
    bias = _na_bias(jnp.pad(na_rpb[0], ((0, 0), (0, 1), (0, 128 - 31))), parts)
    sem_b, own, land, _ = _ag_phase("ag_mid1", own, land, [sem_a], [("recv", 1), ("recv", 2)], [3, 4, 5, 6], bias)
    _, own, land, _ = _ag_phase("ag_wait1", own, land, [sem_a, sem_b], [("recv", 5), ("recv", 6)], [])
    parts = _in_proj("in_proj_2", hs, weights_of(land), tabs, order2, parts)
    sem_c, own, land, _ = _ag_phase("ag_mid2", own, land, [sem_a, sem_b], [("recv", 3), ("recv", 4)], [7, 8], parts)
    _, own, land, _ = _ag_phase("ag_end", own, land, [sem_a, sem_b, sem_c],
                                [("recv", 7), ("recv", 8)] + [("send", k) for k in range(9)], [])
    wt_in = weights_of(land)

    late_own = [rows_s, cols_s, mb_s]
    late_lands = [jax.ShapeDtypeStruct((N_DEV,) + s.shape, s.dtype) for s in late_own]
    l_send, l_recv, late_own, late_lands, late_token = _exchange_start("gather_late_start", "gather", late_own,
                                                                       late_lands, after=wt_in)
    parts = _in_proj("in_proj_3", hs, wt_in, tabs, order3, parts, late_token)

    def late_weights(after):
        own, lands = _exchange_wait("gather_late_wait", "gather", l_send, l_recv, late_own, late_lands, after)
        g_rows, g_cols, g_mb = [lax.dynamic_update_slice(land, o[None], (me, 0, 0)) for land, o in zip(lands, own)]
        return (g_mb[:, :3].transpose(1, 0, 2).reshape(3, D_MODEL),
                g_rows[:, :128].reshape(D_MODEL, D_MODEL), g_cols[:, 0:128].reshape(D_MODEL, 512),
                g_cols[:, 128:256].reshape(D_MODEL, 512), g_cols[:, 256:384].reshape(D_MODEL, 512),
                g_rows[:, 128:].reshape(D_MODEL, D_MODEL))

    rest_state, rest_sibling, w_in_a, w_in_b = [], [], [], []

    def reduce_start(phase, grads, after=None):
        if phase == "rest_sibling":
            gmb_t = jnp.pad(grads["merge_bias"].reshape(3, N_DEV, 128).transpose(1, 0, 2), ((0, 0), (0, 5), (0, 0)))
            terms = [grads["w_kv"].reshape(N_DEV, 128, D_MODEL), grads["w_out"].reshape(N_DEV, 128, D_MODEL),
                     grads["wt_a"].reshape(N_DEV, 128, 512), grads["wt_b"].reshape(N_DEV, 128, 512),
                     grads["wt_c"].reshape(N_DEV, 128, 512), gmb_t]
            lands = [jax.ShapeDtypeStruct((4,) + t.shape[1:], t.dtype) for t in terms]
            rest_sibling.extend(_exchange_start("exchange_sibling_start_rest", "sibling", terms, lands)[:4])
            return None
        if phase == "rest_chips":
            s_send, s_recv, terms, lands = rest_sibling
            terms, recv1 = _exchange_wait("exchange_sibling_wait_rest", "sibling", s_send, s_recv, terms, lands, after)
            state, token = _reduce_scatter_start("rest", terms, recv1)
            rest_state.append(state)
            return token
        if phase == "w_in":
            own, sibling = [a.reshape(N_DEV, SHARD_IN, D_MODEL) for a in grads["wt_in"]]
            sums = _add_sibling("add_sibling_w_in", own, sibling, SHARD_IN)
            lands = [jax.ShapeDtypeStruct((4, SHARD_IN // 2, D_MODEL), BF16)]
            w_in_a.extend(_exchange_start("rs_a_start", "rs_a", [sums], lands))
            return w_in_a[4]
        (sums,), (landed_a,) = _exchange_wait("rs_a_wait", "rs_a", w_in_a[0], w_in_a[1], w_in_a[2], w_in_a[3], after)
        lands = [jax.ShapeDtypeStruct((2, SHARD_IN // 2, D_MODEL), BF16)]
        w_in_b.extend(_exchange_start("rs_b_start", "rs_b", [_presum_halves(sums, landed_a)], lands))
        w_in_b.extend([sums, landed_a])
        return w_in_b[4]

    loss_term, grad_x, grads = _local_step(
        x[0], hst, parts, tabs, bias, mem[0], loss_target[0], pre_norm, mem_norm, post_norm, wt_in, late_weights,
        reduce_start=reduce_start)

    small = jnp.concatenate([_rows128(grads["pre_norm"], 8), _rows128(grads["mem_norm"], 8),
                             _rows128(grads["post_norm"], 8), _rows128(grads["na_rpb"], 32),
                             _rows128(loss_term, 8)], axis=0)
    s_send, s_recv, s_own, s_land, s_token = _exchange_start(
        "gather_small_start", "gather", [small], [jax.ShapeDtypeStruct((N_DEV,) + small.shape, F32)])
    grad = {}
    weights = {
        "pre_norm": (pre_norm, m_pre_norm, v_pre_norm), "w_in": (w_in, m_w_in, v_w_in),
        "merge_bias": (merge_bias, m_merge_bias, v_merge_bias), "na_rpb": (na_rpb, m_na_rpb, v_na_rpb),
        "mem_norm": (mem_norm, m_mem_norm, v_mem_norm), "w_mem_kv": (w_mem_kv, m_w_mem_kv, v_w_mem_kv),
        "w_branch_a": (w_branch_a, m_w_branch_a, v_w_branch_a), "w_branch_b": (w_branch_b, m_w_branch_b, v_w_branch_b),
        "w_branch_c": (w_branch_c, m_w_branch_c, v_w_branch_c), "w_out": (w_out, m_w_out, v_w_out),
        "post_norm": (post_norm, m_post_norm, v_post_norm)}
    order = ["pre_norm", "w_in", "merge_bias", "na_rpb", "mem_norm", "w_mem_kv", "w_branch_a", "w_branch_b",
             "w_branch_c", "w_out", "post_norm"]
    delta, new_m, new_v = {}, {}, {}

    def update(n, dep=None):
        w, m, v = weights[n]
        shape = w.shape
        two_d = (-1, shape[-1])
        dl, nm, nv = _adamw("adamw_" + n, w.reshape(two_d), grad[n].reshape(two_d), m.reshape(two_d),
                            v.reshape(two_d), dep)
        delta[n], new_m[n], new_v[n] = dl.reshape(shape), nm.reshape(shape), nv.reshape(shape)
        return dl

    sums, recv2 = _reduce_scatter_wait(rest_state[0], s_token)
    rest = (("w_mem_kv", False), ("w_out", False), ("w_branch_a", True), ("w_branch_b", True), ("w_branch_c", True),
            ("merge_bias", False))
    items = [(sums[i], recv2[i], *[a[0] for a in weights[n]], transposed) for i, (n, transposed) in enumerate(rest)]
    for (n, _), (g, dl, nm, nv) in zip(rest, _adamw_chips_small("adamw_rest", items)):
        grad[n], delta[n], new_m[n], new_v[n] = g[None], dl[None], nm[None], nv[None]
    s_own, s_land = _exchange_wait("gather_small_wait", "gather", s_send, s_recv, s_own, s_land, delta["w_out"])
    total = _sum_devices(lax.dynamic_update_slice(s_land[0], s_own[0][None], (me, 0, 0)))
    loss = total[56, 0]
    grad.update({"pre_norm": total[0:8].reshape(1, D_MODEL), "mem_norm": total[8:16].reshape(1, D_MODEL),
                 "post_norm": total[16:24].reshape(1, D_MODEL),
                 "na_rpb": total[24:56].reshape(-1)[:8 * 15 * 31].reshape(1, 8, 15, 31)})
    dep = None
    for n in ("pre_norm", "na_rpb", "mem_norm", "post_norm"):
        dep = update(n, dep)
    _, (landed_b,) = _exchange_wait("rs_b_wait", "rs_b", w_in_b[0], w_in_b[1], w_in_b[2], w_in_b[3], dep)
    g, dl, nm, nv = _adamw_halves("adamw_w_in", w_in_b[5], w_in_b[6], landed_b, w_in[0], m_w_in[0], v_w_in[0], 256)
    grad["w_in"], delta["w_in"], new_m["w_in"], new_v["w_in"] = g[None], dl[None], nm[None], nv[None]

    return (loss, grad_x[None], *[grad[n] for n in order], *[delta[n] for n in order],
            *[new_m[n] for n in order], *[new_v[n] for n in order])
```

```python
import functools

import numpy as np
import jax
import jax.numpy as jnp
from jax import lax
from jax.experimental import pallas as pl
from jax.experimental.pallas import tpu as pltpu

F32 = jnp.float32
BF16 = jnp.bfloat16

SEQ = 2048
D_MODEL = 1024
N_IN = 11264
N_DEV = 8
SHARD_IN = N_IN // N_DEV
HEAD_DIM = 64
GRID_W = 64
NA_ROWS = 8
MEM_LEN = 256
DILATIONS = (1, 4, 16)
REACH = 64
ROPE_THETA = 500000.0
ROPE_DIM = 16
EPS = 1e-6
NEG = -1e30
ADAM_LR = 0.001
ADAM_B1 = 0.9
ADAM_B2 = 0.999
ADAM_EPS = 1e-08
ADAM_WD = 0.01
ADAM_STEP = 10

VMEM_LIMIT_BYTES = 56 * 1024 * 1024
MESH_ID = pl.DeviceIdType.MESH

NN = (((1,), (0,)), ((), ()))
NT = (((1,), (1,)), ((), ()))
TN = (((0,), (0,)), ((), ()))


def _params(sem=None):
    return pltpu.CompilerParams(dimension_semantics=sem, vmem_limit_bytes=VMEM_LIMIT_BYTES)


def _iota(shape, dim):
    return lax.broadcasted_iota(jnp.int32, shape, dim)


def _sigmoid(x):
    return 1.0 / (1.0 + jnp.exp(-x))


def _rope_tables():
    half = ROPE_DIM // 2
    inv = (ROPE_THETA ** (-np.arange(half, dtype=np.float64) * 2.0 / ROPE_DIM)).astype(np.float32)
    pos = np.arange(SEQ, dtype=np.float32)
    ang = pos[:, None] * inv[None, :]
    cos, sin = np.cos(ang), np.sin(ang)
    zeros = np.zeros_like(cos)
    rest = HEAD_DIM - ROPE_DIM
    c64 = np.concatenate([cos, cos, np.ones((SEQ, rest), np.float32)], axis=1)
    s1 = np.concatenate([zeros, sin, np.zeros((SEQ, rest), np.float32)], axis=1)
    s2 = np.concatenate([-sin, zeros, np.zeros((SEQ, rest), np.float32)], axis=1)

    def fold(t, d):
        return t.reshape(SEQ // d, d, t.shape[1]).transpose(1, 0, 2).reshape(SEQ, t.shape[1])

    tabs = [np.stack([np.tile(fold(t, d), (1, 2)) for t in (c64, s1, s2)], axis=0) for d in DILATIONS]
    return jnp.asarray(np.stack(tabs, axis=0), dtype=F32)


def _rope(a, c, s1, s2):
    return a * c + pltpu.roll(a, 8, 1) * s1 + pltpu.roll(a, 120, 1) * s2


def _rope_t(a, c, s1, s2):
    return a * c + pltpu.roll(a * s1, 120, 1) + pltpu.roll(a * s2, 8, 1)


def _perm_of_block(j):
    return jnp.where(j < 3, 0, jnp.where(j < 6, 1, jnp.where(j < 9, 2, 0)))


def _mm(name, a, b, out_shape, out_dtype, grid, a_spec, b_spec, o_spec, acc_shape, dims, k_axis, nk):
    def body(a_ref, b_ref, o_ref, acc_ref):
        k = pl.program_id(k_axis)

        @pl.when(k == 0)
        def _():
            acc_ref[...] = jnp.zeros(acc_shape, F32)

        acc_ref[...] += lax.dot_general(a_ref[...], b_ref[...], dims, preferred_element_type=F32)

        @pl.when(k == nk - 1)
        def _():
            o_ref[...] = acc_ref[...].astype(out_dtype)

    sem = tuple("arbitrary" if ax == k_axis else "parallel" for ax in range(len(grid)))
    return pl.pallas_call(
        body, name=name, grid=grid, in_specs=[a_spec, b_spec], out_specs=o_spec,
        out_shape=jax.ShapeDtypeStruct(out_shape, out_dtype),
        scratch_shapes=[pltpu.VMEM(acc_shape, F32)], compiler_params=_params(sem))(a, b)


def _mm_simple(name, a, b, dims, out_dtype, tm, tn, tk):
    if dims is NN:
        m, kk = a.shape
        n = b.shape[1]
        a_spec = pl.BlockSpec((tm, tk), lambda i, j, k: (i, k))
        b_spec = pl.BlockSpec((tk, tn), lambda i, j, k: (k, j))
    elif dims is NT:
        m, kk = a.shape
        n = b.shape[0]
        a_spec = pl.BlockSpec((tm, tk), lambda i, j, k: (i, k))
        b_spec = pl.BlockSpec((tn, tk), lambda i, j, k: (j, k))
    else:
        kk, m = a.shape
        n = b.shape[1]
        a_spec = pl.BlockSpec((tk, tm), lambda i, j, k: (k, i))
        b_spec = pl.BlockSpec((tk, tn), lambda i, j, k: (k, j))
    grid = (m // tm, n // tn, kk // tk)
    o_spec = pl.BlockSpec((tm, tn), lambda i, j, k: (i, j))
    return _mm(name, a, b, (m, n), out_dtype, grid, a_spec, b_spec, o_spec, (tm, tn), dims, 2, kk // tk)


def _rmsnorm_fwd(name, x, gain, rows):
    n, d = x.shape

    def body(x_ref, g_ref, o_ref):
        xv = x_ref[...]
        rstd = lax.rsqrt(jnp.mean(xv * xv, axis=1, keepdims=True) + EPS)
        o_ref[...] = (xv * rstd * g_ref[...]).astype(BF16)

    return pl.pallas_call(
        body, name=name, grid=(n // rows,),
        in_specs=[pl.BlockSpec((rows, d), lambda i: (i, 0)), pl.BlockSpec((1, d), lambda i: (0, 0))],
        out_specs=pl.BlockSpec((rows, d), lambda i: (i, 0)),
        out_shape=jax.ShapeDtypeStruct((n, d), BF16), compiler_params=_params(("parallel",)))(x, gain)


def _folded_rows(first, rows, d):
    if d == 1:
        return pl.ds(pl.multiple_of(first, rows), rows)
    mlen = SEQ // d
    return pl.ds((first % mlen) * d + first // mlen, rows, stride=d)


def _prenorm_fold(x, gain, dep=None):
    rows = 128
    nchunk = D_MODEL // 128
    dep_specs, dep_args = _dep_operand(dep)

    def body(*refs):
        x_refs, g_ref, hs_ref, hst_ref = refs[:nchunk], refs[nchunk], refs[-2], refs[-1]
        first = pl.program_id(0) * rows
        for p, d in enumerate(DILATIONS):
            idx = _folded_rows(first, rows, d)
            xv = jnp.concatenate([r[idx, :] for r in x_refs], axis=1)
            rstd = lax.rsqrt(jnp.mean(xv * xv, axis=1, keepdims=True) + EPS)
            h = xv * rstd * g_ref[...]
            hs_ref[p] = h.astype(BF16)
            hst_ref[p] = h.T.astype(BF16)

    x_specs = [pl.BlockSpec((SEQ, 128), functools.partial(lambda c, i: (0, c), c)) for c in range(nchunk)]
    return pl.pallas_call(
        body, name="prenorm", grid=(SEQ // rows,),
        in_specs=x_specs + [pl.BlockSpec((1, D_MODEL), lambda i: (0, 0))] + dep_specs,
        out_specs=[pl.BlockSpec((3, rows, D_MODEL), lambda i: (0, i, 0)),
                   pl.BlockSpec((3, D_MODEL, rows), lambda i: (0, 0, i))],
        out_shape=[jax.ShapeDtypeStruct((3, SEQ, D_MODEL), BF16), jax.ShapeDtypeStruct((3, D_MODEL, SEQ), BF16)],
        compiler_params=_params(("parallel",)))(*([x] * nchunk), gain, *dep_args)


def _prenorm_bwd(x, gain, dh, dout):
    rows = 512

    def body(x_ref, g_ref, a_ref, do_ref, dx_ref, gg_ref):
        xv = x_ref[...]
        rstd = lax.rsqrt(jnp.mean(xv * xv, axis=1, keepdims=True) + EPS)
        xn = xv * rstd
        dh = jnp.concatenate([a_ref[c] for c in range(D_MODEL // 128)], axis=1)
        gdh = dh * g_ref[...]
        dx_ref[...] = rstd * (gdh - xn * jnp.mean(gdh * xn, axis=1, keepdims=True)) + do_ref[...]

        @pl.when(pl.program_id(0) == 0)
        def _():
            gg_ref[...] = jnp.zeros((1, D_MODEL), F32)

        gg_ref[...] += jnp.sum(dh * xn, axis=0, keepdims=True)

    row = pl.BlockSpec((rows, D_MODEL), lambda i: (i, 0))
    vec = pl.BlockSpec((1, D_MODEL), lambda i: (0, 0))
    return pl.pallas_call(
        body, name="prenorm_bwd", grid=(SEQ // rows,),
        in_specs=[row, vec, pl.BlockSpec((D_MODEL // 128, rows, 128), lambda i: (0, i, 0)), row], out_specs=[row, vec],
        out_shape=[jax.ShapeDtypeStruct((SEQ, D_MODEL), F32), jax.ShapeDtypeStruct((1, D_MODEL), F32)],
        compiler_params=_params(("arbitrary",)))(x, gain, dh, dout)


def _memnorm_bwd(mem, dmemn, dep=None):
    dep_specs, dep_args = _dep_operand(dep)

    def body(m_ref, d_ref, *rest):
        mv = m_ref[...]
        rstd = lax.rsqrt(jnp.mean(mv * mv, axis=1, keepdims=True) + EPS)
        rest[-1][...] = jnp.sum(d_ref[...] * mv * rstd, axis=0, keepdims=True)

    whole = pl.BlockSpec(memory_space=pltpu.VMEM)
    return pl.pallas_call(
        body, name="memnorm_bwd", in_specs=[whole, whole] + dep_specs,
        out_shape=jax.ShapeDtypeStruct((1, D_MODEL), F32), compiler_params=_params())(mem, dmemn, *dep_args)


def _dep_operand(dep):
    return ([], []) if dep is None else ([pl.BlockSpec(memory_space=pl.ANY)], [dep])


def _in_proj(name, hs, wt, tabs, order, prev=None, dep=None):
    tm, tn = 512, 512
    prev_specs, prev_args = ([], []) if prev is None else ([ANY], [prev])
    dep_specs, dep_args = _dep_operand(dep)

    def body(order_ref, h_ref, w_ref, t_ref, *rest):
        o_ref = rest[-1]
        j = order_ref[pl.program_id(0)]
        is_rope = jnp.logical_and(j < 9, j % 3 != 2)
        row_slices = [slice(r * tm, (r + 1) * tm) for r in range(SEQ // tm)]

        def product(rs):
            return lax.dot_general(h_ref[rs, :], w_ref[...], NT, preferred_element_type=F32)

        @pl.when(is_rope)
        def _():
            for rs in row_slices:
                acc = product(rs)
                c, s1, s2 = t_ref[0, rs, :], t_ref[1, rs, :], t_ref[2, rs, :]
                for q in range(tn // 128):
                    a = acc[:, q * 128:(q + 1) * 128]
                    o_ref[rs, q * 128:(q + 1) * 128] = _rope(a, c, s1, s2).astype(BF16)

        @pl.when(jnp.logical_not(is_rope))
        def _():
            for rs in row_slices:
                o_ref[rs, :] = product(rs).astype(BF16)

    grid_spec = pltpu.PrefetchScalarGridSpec(
        num_scalar_prefetch=1, grid=(order.shape[0],),
        in_specs=[pl.BlockSpec((None, SEQ, D_MODEL), lambda t, o: (_perm_of_block(o[t]), 0, 0)),
                  pl.BlockSpec((tn, D_MODEL), lambda t, o: (o[t], 0)),
                  pl.BlockSpec((None, 3, SEQ, 128), lambda t, o: (_perm_of_block(o[t]), 0, 0, 0))] + prev_specs
        + dep_specs,
        out_specs=pl.BlockSpec((SEQ, tn), lambda t, o: (0, o[t])))
    return pl.pallas_call(
        body, name=name, grid_spec=grid_spec, out_shape=jax.ShapeDtypeStruct((SEQ, N_IN), BF16),
        input_output_aliases={} if prev is None else {4: 0},
        compiler_params=_params(("arbitrary",)))(order, hs, wt, tabs, *prev_args, *dep_args)


def _piece_blocks(pieces):
    return [(a, h * 512) for a, p in enumerate(pieces) for h in range(p.shape[1] // 512)]


def _block_fetch(piece_refs, blocks, buf, sem):
    def start(block, slot):
        for b, (a, col) in enumerate(blocks):
            @pl.when(block == b)
            def _():
                pltpu.make_async_copy(piece_refs[a].at[:, pl.ds(col, 512)], buf.at[slot], sem.at[slot]).start()

    def wait(slot):
        pltpu.make_async_copy(piece_refs[0].at[:, pl.ds(0, 512)], buf.at[slot], sem.at[slot]).wait()

    return start, wait


def _in_proj_dw(pieces, hst, dep=None):
    tn = 512
    blocks = _piece_blocks(pieces)
    nblk = len(blocks)
    npc = len(pieces)
    dep_specs, dep_args = _dep_operand(dep)

    def body(h_ref, *rest):
        piece_refs = rest[:npc]
        own_out, mirror, buf, sem, out_buf, send_sems, recv_sem, local_sems = rest[-8:]
        j = pl.program_id(0)
        slot = j % 2
        start, wait = _block_fetch(piece_refs, blocks, buf, sem)
        x, y, c = _place()

        def rows_of(step):
            return pl.ds(pl.multiple_of(step * tn, tn), tn)

        def to_sibling(step, slot_):
            return pltpu.make_async_remote_copy(
                src_ref=out_buf.at[slot_], dst_ref=mirror.at[rows_of(step)],
                send_sem=send_sems.at[slot_], recv_sem=recv_sem, device_id=(x, y, 1 - c), device_id_type=MESH_ID)

        def to_own(step, slot_):
            return pltpu.make_async_copy(out_buf.at[slot_], own_out.at[rows_of(step)], local_sems.at[slot_])

        @pl.when(j == 0)
        def _():
            start(j, slot)

        wait(slot)

        @pl.when(j + 1 < nblk)
        def _():
            start(j + 1, 1 - slot)

        acc = jnp.dot(h_ref[...], buf[slot], preferred_element_type=F32)

        @pl.when(j >= 2)
        def _():
            to_sibling(j - 2, slot).wait_send()
            to_own(j - 2, slot).wait()

        out_buf[slot] = acc.T.astype(BF16)
        to_sibling(j, slot).start()
        to_own(j, slot).start()

        @pl.when(j == nblk - 1)
        def _():
            to_sibling(j - 1, 1 - slot).wait_send()
            to_own(j - 1, 1 - slot).wait()
            to_sibling(j, slot).wait_send()
            to_own(j, slot).wait()
            pltpu.make_async_remote_copy(src_ref=mirror, dst_ref=mirror, send_sem=send_sems.at[0], recv_sem=recv_sem,
                                         device_id=(x, y, 1 - c), device_id_type=MESH_ID).wait_recv()

    return pl.pallas_call(
        body, name="in_proj_dw", grid=(nblk,),
        in_specs=[pl.BlockSpec((None, D_MODEL, SEQ), lambda j: (_perm_of_block(j), 0, 0))] + [ANY] * npc + dep_specs,
        out_specs=[ANY, ANY],
        out_shape=[jax.ShapeDtypeStruct((N_IN, D_MODEL), BF16), jax.ShapeDtypeStruct((N_IN, D_MODEL), BF16)],
        scratch_shapes=[pltpu.VMEM((2, SEQ, tn), BF16), pltpu.SemaphoreType.DMA((2,)),
                        pltpu.VMEM((2, tn, D_MODEL), BF16), pltpu.SemaphoreType.DMA((2,)), pltpu.SemaphoreType.DMA,
                        pltpu.SemaphoreType.DMA((2,))],
        compiler_params=_params(("arbitrary",)))(hst, *pieces, *dep_args)


def _in_proj_dh(pieces, wt, dep=None):
    tk = 512
    blocks = _piece_blocks(pieces)
    nblk = len(blocks)
    npc = len(pieces)
    nchunk = D_MODEL // 128

    def col(s):
        return jnp.where(s < 3, s, jnp.where(s < 16, s + 6, s - 13))

    dep_specs, dep_args = _dep_operand(dep)

    def body(w_ref, *rest):
        piece_refs = rest[:npc]
        o_ref, acc_ref, buf, sem = rest[-4:]
        s = pl.program_id(0)
        slot = s % 2
        start, wait = _block_fetch(piece_refs, blocks, buf, sem)

        @pl.when(s == 0)
        def _():
            start(col(s), slot)

        wait(slot)

        @pl.when(s + 1 < nblk)
        def _():
            start(col(s + 1), 1 - slot)

        row_slices = [slice(r * 512, (r + 1) * 512) for r in range(SEQ // 512)]

        def product(rs):
            return jnp.dot(buf[slot, rs, :], w_ref[...], preferred_element_type=F32)

        def accumulate(cond, to_out, init):
            @pl.when(cond)
            def _():
                for rs in row_slices:
                    prod = product(rs)
                    if not to_out:
                        if init:
                            acc_ref[rs, :] = prod
                        else:
                            acc_ref[rs, :] += prod
                        continue
                    for c in range(nchunk):
                        if init:
                            o_ref[c, rs, :] = prod[:, c * 128:(c + 1) * 128]
                        else:
                            o_ref[c, rs, :] += prod[:, c * 128:(c + 1) * 128]

        accumulate(s == 0, True, True)
        accumulate(jnp.logical_and(s > 0, s < 16), True, False)
        accumulate(jnp.logical_or(s == 16, s == 19), False, True)
        accumulate(jnp.logical_and(s > 16, s != 19), False, False)
        for last, d in ((18, 4), (21, 16)):
            @pl.when(s == last)
            def _():
                mlen = SEQ // d
                for r in range(d):
                    for c in range(nchunk):
                        o_ref[c, pl.ds(r, mlen, stride=d), :] += acc_ref[r * mlen:(r + 1) * mlen,
                                                                         c * 128:(c + 1) * 128]

    return pl.pallas_call(
        body, name="in_proj_dh", grid=(nblk,),
        in_specs=[pl.BlockSpec((tk, D_MODEL), lambda s: (col(s), 0))] + [ANY] * npc + dep_specs,
        out_specs=pl.BlockSpec((nchunk, SEQ, 128), lambda s: (0, 0, 0)),
        out_shape=jax.ShapeDtypeStruct((nchunk, SEQ, 128), F32),
        scratch_shapes=[pltpu.VMEM((SEQ, D_MODEL), F32), pltpu.VMEM((2, SEQ, tk), BF16),
                        pltpu.SemaphoreType.DMA((2,))],
        compiler_params=_params(("arbitrary",)))(wt, *pieces, *dep_args)


def _head_lanes(lanes, hh):
    return lanes >= 64 if hh == 1 else lanes < 64


def _head_rows(x, lanes, hh, pair):
    if not pair:
        return jnp.max(x, axis=1, keepdims=True)
    return jnp.max(jnp.where(_head_lanes(lanes, hh), x, -jnp.inf), axis=1, keepdims=True)


def _mask_head(x, lanes, hh, pair, scale=1.0):
    if not pair:
        return x
    xf = x.astype(F32) if scale == 1.0 else x.astype(F32) * scale
    return jnp.where(_head_lanes(lanes, hh), xf, 0.0).astype(BF16)


def _window(mode, qi, tq, mlen, tk):
    if mode == "dil":
        q0 = qi * tq
        seg = (q0 // mlen) * mlen
        ks = jnp.clip(q0 - REACH, seg, seg + mlen - tk)
        return pl.multiple_of(ks, 64)
    if mode == "na":
        r_start = jnp.clip(qi - NA_ROWS // 2, 0, SEQ // GRID_W - NA_ROWS)
        return pl.multiple_of(r_start * GRID_W, 64)
    return 0


def _band_mask(qi, tq, tk, ks):
    qpos = qi * tq + _iota((tq, tk), 0)
    kpos = ks + _iota((tq, tk), 1)
    return jnp.where(jnp.abs(qpos - kpos) <= REACH, 0.0, NEG).astype(F32)


def _stack_heads(x, lanes, pair, scale=1.0):
    if not pair:
        return x
    return jnp.concatenate([_mask_head(x, lanes, hh, pair, scale) for hh in range(2)], axis=0)


def _stack_rows(x, lanes, pair):
    if not pair:
        return _head_rows(x, lanes, 0, pair)
    return jnp.concatenate([_head_rows(x, lanes, hh, pair) for hh in range(2)], axis=0)


def _unstack_heads(x, lanes, pair, tq):
    if not pair:
        return x
    return jnp.where(lanes < 64, x[:tq], x[tq:])


def _scores(mode, qst, k, sscale, band, qi, bias_ref, pair):
    s = lax.dot_general(qst, k, NT, preferred_element_type=F32)
    if sscale != 1.0:
        s = s * sscale
    if mode == "dil":
        s = s + jnp.concatenate([band, band], axis=0)
    elif mode == "na":
        off = qi - jnp.clip(qi - NA_ROWS // 2, 0, SEQ // GRID_W - NA_ROWS)
        s = s + jnp.concatenate([bias_ref[0, off], bias_ref[1, off]], axis=0)
    return s


def _attn_cfg(mode, d):
    if mode == "dil":
        mlen = SEQ // d
        return dict(pair=True, tq=128, tk=min(256, mlen), mlen=mlen, lk=SEQ, scale=HEAD_DIM ** -0.5, units=4,
                    nsub=ATTN_SUBTILES)
    if mode == "na":
        return dict(pair=True, tq=GRID_W, tk=NA_ROWS * GRID_W, mlen=SEQ, lk=SEQ, scale=HEAD_DIM ** -0.5, units=4,
                    nsub=2 * ATTN_SUBTILES)
    return dict(pair=False, tq=128, tk=MEM_LEN, mlen=SEQ, lk=MEM_LEN, scale=128 ** -0.5, units=4,
                nsub=ATTN_SUBTILES)


ATTN_SUBTILES = 16


def _attn_fwd(name, mode, q_arr, k_arr, v_arr, qcol, kcol, vcol, d=1, bias=None):
    cfg = _attn_cfg(mode, d)
    pair, tq, tk, mlen, lk, scale = cfg["pair"], cfg["tq"], cfg["tk"], cfg["mlen"], cfg["lk"], cfg["scale"]
    qscale, sscale = (scale, 1.0) if pair else (1.0, scale)
    nsub = cfg["nsub"]
    rows = nsub * tq

    def body(*refs):
        if mode == "na":
            q_ref, k_ref, v_ref, bias_ref, o_ref, l_ref = refs
        else:
            q_ref, k_ref, v_ref, o_ref, l_ref = refs
            bias_ref = None
        lanes = _iota((tq, 128), 1)
        qis = [pl.program_id(1) * nsub + sub for sub in range(nsub)]
        kss = [_window(mode, qi, tq, mlen, tk) for qi in qis]
        vs = [v_ref[pl.ds(ks, tk), :] for ks in kss]
        bands = [_band_mask(qi, tq, tk, ks) if mode == "dil" else None for qi, ks in zip(qis, kss)]
        ss = []
        for sub in range(nsub):
            qst = _stack_heads(q_ref[sub * tq:(sub + 1) * tq, :], lanes, pair, qscale)
            k = k_ref[pl.ds(kss[sub], tk), :]
            ss.append(_scores(mode, qst, k, sscale, bands[sub], qis[sub], bias_ref, pair))
        ms = [jnp.max(s_, axis=1, keepdims=True) for s_ in ss]
        ps = [jnp.exp(s_ - m) for s_, m in zip(ss, ms)]
        ls = [jnp.sum(p, axis=1, keepdims=True) for p in ps]
        os_ = [jnp.dot(p.astype(BF16), v, preferred_element_type=F32) for p, v in zip(ps, vs)]
        for sub in range(nsub):
            out = _unstack_heads(os_[sub] / ls[sub], lanes, pair, tq)
            lse = ms[sub] + jnp.log(ls[sub])
            lse = _unstack_heads(jnp.broadcast_to(lse, (lse.shape[0], 128)), lanes, pair, tq)
            dst = _folded_rows(qis[sub] * tq, tq, d) if mode == "dil" else slice(sub * tq, (sub + 1) * tq)
            o_ref[dst, :] = out
            l_ref[dst, :] = lse

    in_specs = [pl.BlockSpec((rows, 128), lambda u, i: (i, qcol + u)),
                pl.BlockSpec((lk, 128), lambda u, i: (0, kcol + u)),
                pl.BlockSpec((lk, 128), lambda u, i: (0, vcol + u))]
    args = [q_arr, k_arr, v_arr]
    if mode == "na":
        in_specs.append(pl.BlockSpec((2, NA_ROWS, GRID_W, NA_ROWS * GRID_W), lambda u, i: (u, 0, 0, 0)))
        args.append(bias)
    if mode == "dil":
        out_spec = pl.BlockSpec((SEQ, 128), lambda u, i: (0, u))
    else:
        out_spec = pl.BlockSpec((rows, 128), lambda u, i: (i, u))
    return pl.pallas_call(
        body, name=name, grid=(cfg["units"], SEQ // rows), in_specs=in_specs, out_specs=[out_spec, out_spec],
        out_shape=[jax.ShapeDtypeStruct((SEQ, 512), F32), jax.ShapeDtypeStruct((SEQ, 512), F32)],
        compiler_params=_params(("parallel", "arbitrary")))(*args)


def _attn_bwd(name, mode, q_arr, k_arr, v_arr, qcol, kcol, vcol, do, lse, dp=None, o=None, d=1, bias=None,
              tabs=None):
    cfg = _attn_cfg(mode, d)
    pair, tq, tk, mlen, lk, scale = cfg["pair"], cfg["tq"], cfg["tk"], cfg["mlen"], cfg["lk"], cfg["scale"]
    qscale, sscale = (scale, 1.0) if pair else (1.0, scale)
    nsub = cfg["nsub"]
    rows = nsub * tq
    nq = SEQ // rows
    kv_dtype = F32 if mode == "mem" else BF16

    def body(*refs):
        refs = list(refs)
        q_ref, k_ref, v_ref, do_ref, l_ref = refs[:5]
        rest = refs[5:]
        bias_ref = tq_ref = tk_ref = db_ref = None
        if mode == "dil":
            dp_ref, tq_ref, tk_ref, dq_ref, dk_ref, dv_ref, dk_acc, dv_acc = rest
        elif mode == "na":
            o_ref, bias_ref, dq_ref, dk_ref, dv_ref, db_ref, dk_acc, dv_acc = rest
        else:
            o_ref, dq_ref, dk_ref, dv_ref, dk_acc, dv_acc = rest
        step = pl.program_id(1)

        @pl.when(step == 0)
        def _():
            dk_acc[...] = jnp.zeros((lk, 128), F32)
            dv_acc[...] = jnp.zeros((lk, 128), F32)
            if mode == "na":
                db_ref[...] = jnp.zeros(db_ref.shape, F32)

        lanes = _iota((tq, 128), 1)
        qis = [step * nsub + sub for sub in range(nsub)]
        sls = [slice(sub * tq, (sub + 1) * tq) for sub in range(nsub)]
        kss = [_window(mode, qi, tq, mlen, tk) for qi in qis]
        ks_ = [k_ref[pl.ds(ks, tk), :] for ks in kss]
        vs = [v_ref[pl.ds(ks, tk), :] for ks in kss]
        qsts, dosts, lses, dphs = [], [], [], []
        for sub in range(nsub):
            if mode == "dil":
                src = _folded_rows(qis[sub] * tq, tq, d)
                dov = do_ref[src, :].astype(BF16)
                lsev = l_ref[src, :]
                dphs.append(_stack_rows(dp_ref[src, :], lanes, pair))
            else:
                dov = do_ref[sls[sub], :]
                lsev = l_ref[sls[sub], :]
                dpv = dov.astype(F32) * o_ref[sls[sub], :]
                if pair:
                    dphs.append(jnp.concatenate(
                        [jnp.sum(jnp.where(_head_lanes(lanes, hh), dpv, 0.0), axis=1, keepdims=True)
                         for hh in range(2)], axis=0))
                else:
                    dphs.append(jnp.sum(dpv, axis=1, keepdims=True))
            qsts.append(_stack_heads(q_ref[sls[sub], :], lanes, pair, qscale))
            dosts.append(_stack_heads(dov, lanes, pair))
            lses.append(_stack_rows(lsev, lanes, pair))
        bands = [_band_mask(qi, tq, tk, ks) if mode == "dil" else None for qi, ks in zip(qis, kss)]
        ss = [_scores(mode, qsts[sub], ks_[sub], sscale, bands[sub], qis[sub], bias_ref, pair) for sub in range(nsub)]
        dpms = [lax.dot_general(dosts[sub], vs[sub], NT, preferred_element_type=F32) for sub in range(nsub)]
        ps = [jnp.exp(s_ - lse) for s_, lse in zip(ss, lses)]
        dss = [p * (dpm - dph) for p, dpm, dph in zip(ps, dpms, dphs)]
        if mode == "na":
            for sub, ds in enumerate(dss):
                off = qis[sub] - jnp.clip(qis[sub] - NA_ROWS // 2, 0, SEQ // GRID_W - NA_ROWS)
                db_ref[0, off] += ds[:tq]
                db_ref[1, off] += ds[tq:]
        dsbs = [ds.astype(BF16) for ds in dss]
        dvs = [lax.dot_general(p.astype(BF16), dosts[sub], TN, preferred_element_type=F32)
               for sub, p in enumerate(ps)]
        dqs = [jnp.dot(dsb, ks_[sub], preferred_element_type=F32) * scale for sub, dsb in enumerate(dsbs)]
        dks = [lax.dot_general(dsb, qsts[sub], TN, preferred_element_type=F32) for sub, dsb in enumerate(dsbs)]
        for sub in range(nsub):
            sl = sls[sub]
            dq = _unstack_heads(dqs[sub], lanes, pair, tq)
            if mode == "dil":
                dq = _rope_t(dq, tq_ref[0, sl, :], tq_ref[1, sl, :], tq_ref[2, sl, :])
            dq_ref[sl, :] = dq.astype(BF16)
            dk_acc[pl.ds(kss[sub], tk), :] += dks[sub] if pair else dks[sub] * scale
            dv_acc[pl.ds(kss[sub], tk), :] += dvs[sub]

        @pl.when(step == nq - 1)
        def _():
            dkv = dk_acc[...]
            if mode == "dil":
                dkv = _rope_t(dkv, tk_ref[0], tk_ref[1], tk_ref[2])
            dk_ref[...] = dkv.astype(kv_dtype)
            dv_ref[...] = dv_acc[...].astype(kv_dtype)

    q_spec = pl.BlockSpec((rows, 128), lambda u, i: (i, qcol + u))
    row_spec = pl.BlockSpec((rows, 128), lambda u, i: (i, u))
    kv_out = pl.BlockSpec((lk, 128), lambda u, i: (0, u))
    whole = pl.BlockSpec((SEQ, 128), lambda u, i: (0, u))
    nat_spec = whole if mode == "dil" else row_spec
    in_specs = [q_spec,
                pl.BlockSpec((lk, 128), lambda u, i: (0, kcol + u)),
                pl.BlockSpec((lk, 128), lambda u, i: (0, vcol + u)),
                nat_spec, nat_spec]
    args = [q_arr, k_arr, v_arr, do, lse]
    out_specs = [row_spec, kv_out, kv_out]
    out_shape = [jax.ShapeDtypeStruct((SEQ, 512), BF16), jax.ShapeDtypeStruct((lk, 512), kv_dtype),
                 jax.ShapeDtypeStruct((lk, 512), kv_dtype)]
    if mode == "dil":
        in_specs += [whole, pl.BlockSpec((3, rows, 128), lambda u, i: (0, i, 0)),
                     pl.BlockSpec((3, SEQ, 128), lambda u, i: (0, 0, 0))]
        args += [dp, tabs, tabs]
    elif mode == "na":
        b_spec = pl.BlockSpec((2, NA_ROWS, GRID_W, NA_ROWS * GRID_W), lambda u, i: (u, 0, 0, 0))
        in_specs += [row_spec, b_spec]
        args += [o, bias]
        out_specs.append(b_spec)
        out_shape.append(jax.ShapeDtypeStruct((8, NA_ROWS, GRID_W, NA_ROWS * GRID_W), F32))
    else:
        in_specs.append(row_spec)
        args.append(o)
    return pl.pallas_call(
        body, name=name, grid=(cfg["units"], nq), in_specs=in_specs, out_specs=out_specs, out_shape=out_shape,
        scratch_shapes=[pltpu.VMEM((lk, 128), F32), pltpu.VMEM((lk, 128), F32)],
        compiler_params=_params(("parallel", "arbitrary")))(*args)


def _na_geometry():
    qc = _iota((GRID_W, 128), 0)
    lane = _iota((GRID_W, 128), 1)
    kc = lane & 63
    c_start = jnp.clip(qc - 8, 0, GRID_W - 16)
    valid = jnp.logical_and(kc >= c_start, kc < c_start + 16)
    return lane, valid


def _na_bias(rpb_rows, dep=None):
    dep_specs, dep_args = _dep_operand(dep)

    def body(r_ref, *rest):
        o_ref, t_ref = rest[-2:]
        lane, valid = _na_geometry()
        for dd in range(14):
            row_a = jnp.broadcast_to(r_ref[dd:dd + 1, :], (GRID_W, 128))
            row_b = jnp.broadcast_to(r_ref[dd + 1:dd + 2, :], (GRID_W, 128))
            both = jnp.where(lane < 64, row_a, pltpu.roll(row_b, 64, 1))
            t = pltpu.roll(both, 128 - 15, 1, stride=1, stride_axis=0)
            t_ref[dd] = jnp.where(valid, t, NEG)
        for off in range(NA_ROWS):
            for p in range(4):
                o_ref[off, :, p * 128:(p + 1) * 128] = t_ref[2 * p - off + 7]

    return pl.pallas_call(
        body, name="na_bias", grid=(8,),
        in_specs=[pl.BlockSpec((None, 16, 128), lambda h: (h, 0, 0))] + dep_specs,
        out_specs=pl.BlockSpec((None, NA_ROWS, GRID_W, NA_ROWS * GRID_W), lambda h: (h, 0, 0, 0)),
        out_shape=jax.ShapeDtypeStruct((8, NA_ROWS, GRID_W, NA_ROWS * GRID_W), F32),
        scratch_shapes=[pltpu.VMEM((14, GRID_W, 128), F32)],
        compiler_params=_params(("parallel",)))(rpb_rows, *dep_args)


def _na_bias_bwd(dbias, dep=None):
    dep_specs, dep_args = _dep_operand(dep)

    def body(d_ref, *rest):
        o_ref = rest[-1]
        lane, valid = _na_geometry()
        reverse = (_iota((GRID_W, GRID_W), 0) + _iota((GRID_W, GRID_W), 1) == GRID_W - 1).astype(F32)
        o_ref[...] = jnp.zeros((16, 128), F32)
        for dd in range(14):
            t = jnp.zeros((GRID_W, 128), F32)
            for off in range(NA_ROWS):
                for p in range(4):
                    if 2 * p - off + 7 == dd:
                        t = t + d_ref[off, :, p * 128:(p + 1) * 128]
            t = jnp.dot(reverse, jnp.where(valid, t, 0.0), precision=lax.Precision.HIGHEST,
                        preferred_element_type=F32)
            t = pltpu.roll(t, 128 - (GRID_W - 16), 1, stride=1, stride_axis=0)
            o_ref[dd:dd + 1, :] = jnp.sum(t, axis=0, keepdims=True)

    return pl.pallas_call(
        body, name="na_bias_bwd", grid=(8,),
        in_specs=[pl.BlockSpec((None, NA_ROWS, GRID_W, NA_ROWS * GRID_W), lambda h: (h, 0, 0, 0))] + dep_specs,
        out_specs=pl.BlockSpec((None, 16, 128), lambda h: (h, 0, 0)),
        out_shape=jax.ShapeDtypeStruct((8, 16, 128), F32),
        compiler_params=_params(("parallel",)))(dbias, *dep_args)


GATE_ROWS = 128


def _group_weights(l0, l1, l2):
    m = jnp.maximum(jnp.maximum(l0, l1), l2)
    e0, e1, e2 = jnp.exp(l0 - m), jnp.exp(l1 - m), jnp.exp(l2 - m)
    inv = 1.0 / (e0 + e1 + e2)
    return e0 * inv, e1 * inv, e2 * inv


def _gate_block(o_grp, l_grp, out_b, out_c, parts, x, target, merge_bias, wts, w_out, gain, head_sum):
    rows = GATE_ROWS
    r512 = pl.BlockSpec((rows, 512), lambda i: (i, 0))
    r1024 = pl.BlockSpec((rows, D_MODEL), lambda i: (i, 0))
    silu_cols = [pl.BlockSpec((rows, 512), functools.partial(lambda b, i: (i, b), 13 + b)) for b in range(3)]
    logit_cols = [pl.BlockSpec((rows, D_MODEL), functools.partial(lambda b, i: (i, b), 8 + b)) for b in range(3)]

    def body(o0, o1, o2, l0, l1, l2, ob, oc, ga, gb, gc, la, lb, lc, x_ref, t_ref, mb, wa, wb, wc, wo_ref, gn_ref,
             hs_ref, dout_ref, dla, dlb, dlc, dga, dgb, dgc, do0, do1, do2, dp0, dp1, dp2, dob, doc, err_ref, gg_ref,
             gmb, gwa, gwb, gwc, gwo, acc_a, acc_b, acc_c, acc_o):
        step = pl.program_id(0)
        ws = _group_weights(l0[...], l1[...], l2[...])
        out_a = ws[0] * o0[...] + ws[1] * o1[...] + ws[2] * o2[...]
        branches = ((out_a, ga, la, wa, acc_a, dla, dga), (ob[...], gb, lb, wb, acc_b, dlb, dgb),
                    (oc[...], gc, lc, wc, acc_c, dlc, dgc))

        @pl.when(step == 0)
        def _():
            for acc in (acc_a, acc_b, acc_c, acc_o):
                acc[...] = jnp.zeros(acc.shape, F32)
            err_ref[...] = jnp.zeros((1, D_MODEL), F32)
            gg_ref[...] = jnp.zeros((1, D_MODEL), F32)
            gmb[...] = jnp.zeros((3, D_MODEL), F32)

        y = jnp.zeros((rows, D_MODEL), F32)
        zs, gates, silus, dsilus, us = [], [], [], [], []
        for b, (ov, g_ref, l_ref, w_ref, _, _, _) in enumerate(branches):
            g = g_ref[...].astype(F32)
            sg = _sigmoid(g)
            silus.append(g * sg)
            dsilus.append(sg * (1.0 + g * (1.0 - sg)))
            us.append((ov * silus[b]).astype(BF16))
            zs.append(lax.dot_general(us[b], w_ref[...], NT, preferred_element_type=F32))
            gates.append(_sigmoid(l_ref[...].astype(F32) + mb[b:b + 1, :]))
            y = y + gates[b] * zs[b]
        yb = y.astype(BF16)
        y2 = jnp.dot(yb, wo_ref[...], preferred_element_type=F32)
        rstd = lax.rsqrt(jnp.mean(y2 * y2, axis=1, keepdims=True) + EPS)
        yn = y2 * rstd
        gv = gn_ref[...]
        err = x_ref[...] + yn * gv - t_ref[...]
        dout = err * (1.0 / D_MODEL)
        dout_ref[...] = dout
        dn = dout * gv
        dy2 = (rstd * (dn - yn * jnp.mean(dn * yn, axis=1, keepdims=True))).astype(BF16)
        acc_o[...] += lax.dot_general(yb, dy2, TN, preferred_element_type=F32)
        err_ref[...] += jnp.sum(err * err, axis=0, keepdims=True)
        gg_ref[...] += jnp.sum(dout * yn, axis=0, keepdims=True)
        dy = lax.dot_general(dy2, wo_ref[...], NT, preferred_element_type=F32)
        dos = []
        for b, (ov, _, _, w_ref, acc, dl_ref, dg_ref) in enumerate(branches):
            dl = dy * zs[b] * gates[b] * (1.0 - gates[b])
            dl_ref[...] = dl.astype(BF16)
            gmb[b:b + 1, :] += jnp.sum(dl, axis=0, keepdims=True)
            dz = (dy * gates[b]).astype(BF16)
            acc[...] += lax.dot_general(dz, us[b], TN, preferred_element_type=F32)
            du = jnp.dot(dz, w_ref[...], preferred_element_type=F32)
            dos.append(du * silus[b])
            dg_ref[...] = (du * ov * dsilus[b]).astype(BF16)
        dob[...] = dos[1].astype(BF16)
        doc[...] = dos[2].astype(BF16)
        row_term = jnp.dot(dos[0] * out_a, hs_ref[...], precision=lax.Precision.HIGHEST, preferred_element_type=F32)
        for wg, do_ref, dp_ref in zip(ws, (do0, do1, do2), (dp0, dp1, dp2)):
            do_ref[...] = wg * dos[0]
            dp_ref[...] = wg * row_term

        @pl.when(step == SEQ // rows - 1)
        def _():
            for acc, out in ((acc_a, gwa), (acc_b, gwb), (acc_c, gwc), (acc_o, gwo)):
                out[...] = acc[...].astype(BF16)

    full = lambda shape: pl.BlockSpec(shape, lambda i: (0,) * len(shape))
    vec = pl.BlockSpec((1, D_MODEL), lambda i: (0, 0))
    acc3 = pl.BlockSpec((3, D_MODEL), lambda i: (0, 0))
    in_specs = ([r512] * 8 + silu_cols + logit_cols + [r1024, r1024, full((3, D_MODEL))]
                + [full((D_MODEL, 512))] * 3 + [full((D_MODEL, D_MODEL)), vec, full((512, 512))])
    out_specs = ([r1024] + [r1024] * 3 + [r512] * 3 + [r512] * 6 + [r512] * 2 + [vec, vec, acc3]
                 + [full((D_MODEL, 512))] * 3 + [full((D_MODEL, D_MODEL))])
    bf, f32 = BF16, F32
    sds = jax.ShapeDtypeStruct
    out_shape = ([sds((SEQ, D_MODEL), f32)] + [sds((SEQ, D_MODEL), bf)] * 3 + [sds((SEQ, 512), bf)] * 3
                 + [sds((SEQ, 512), f32)] * 6 + [sds((SEQ, 512), bf)] * 2 + [sds((1, D_MODEL), f32)] * 2
                 + [sds((3, D_MODEL), f32)] + [sds((D_MODEL, 512), bf)] * 3 + [sds((D_MODEL, D_MODEL), bf)])
    res = pl.pallas_call(
        body, name="gate_block", grid=(SEQ // rows,), in_specs=in_specs, out_specs=out_specs, out_shape=out_shape,
        scratch_shapes=[pltpu.VMEM((D_MODEL, 512), F32)] * 3 + [pltpu.VMEM((D_MODEL, D_MODEL), F32)],
        compiler_params=_params(("arbitrary",)))(
            *o_grp, *l_grp, out_b, out_c, parts, parts, parts, parts, parts, parts, x, target, merge_bias, *wts, w_out,
            gain, head_sum)
    return dict(dout=res[0], dlog=res[1:4], dg=res[4:7], do_grp=res[7:10], dp_grp=res[10:13], do_b=res[13],
                do_c=res[14], err_sq=res[15], g_post=res[16], g_mb=res[17], g_wt=res[18:21], g_w_out=res[21])


def _local_step(x, hst, parts, tabs, bias, mem, target, pre_norm, mem_norm, post_norm, wt_in, late_weights,
                reduce_start=None):
    o_grp, l_grp = [], []
    for g, d in enumerate(DILATIONS):
        o, l = _attn_fwd("dil_fwd_%d" % g, "dil", parts, parts, parts, 12 * g, 12 * g + 4, 12 * g + 8, d=d)
        o_grp.append(o)
        l_grp.append(l)
    out_b, lse_b = _attn_fwd("na_fwd", "na", parts, parts, parts, 36, 40, 44, bias=bias)
    merge_bias, w_kv, wt_a, wt_b, wt_c, w_out = late_weights(sum(a[:8, :128] for a in [out_b] + o_grp))
    memn = _rmsnorm_fwd("memnorm", mem, mem_norm, MEM_LEN)
    kv_m = _mm_simple("mem_kv", memn, w_kv, NN, BF16, MEM_LEN, 512, D_MODEL)
    out_c, lse_c = _attn_fwd("mem_fwd", "mem", parts, kv_m, kv_m, 48, 0, 4)

    rr = _iota((512, 512), 0) // HEAD_DIM
    cc = _iota((512, 512), 1) // HEAD_DIM
    head_sum = (rr == cc).astype(F32)
    gb = _gate_block(o_grp, l_grp, out_b, out_c, parts, x, target, merge_bias, (wt_a, wt_b, wt_c), w_out, post_norm,
                     head_sum)
    dout, dlog, dg, g_wt, g_w_out = gb["dout"], gb["dlog"], gb["dg"], gb["g_wt"], gb["g_w_out"]
    do_grp, dp_grp, do_b, do_c, g_post, g_mb = (gb["do_grp"], gb["dp_grp"], gb["do_b"], gb["do_c"], gb["g_post"],
                                                gb["g_mb"])
    loss = 0.5 * jnp.sum(gb["err_sq"]) / D_MODEL

    dqkv = []
    for g, d in enumerate(DILATIONS):
        dq, dk, dv = _attn_bwd("dil_bwd_%d" % g, "dil", parts, parts, parts, 12 * g, 12 * g + 4, 12 * g + 8,
                               do_grp[g], l_grp[g], dp=dp_grp[g], d=d, tabs=tabs[g])
        dqkv += [dq, dk, dv]
    dq_b, dk_b, dv_b, dbias = _attn_bwd("na_bwd", "na", parts, parts, parts, 36, 40, 44, do_b, lse_b, o=out_b,
                                        bias=bias)
    dq_c, dk_m, dv_m = _attn_bwd("mem_bwd", "mem", parts, kv_m, kv_m, 48, 0, 4, do_c, lse_c, o=out_c)

    dkv = jnp.concatenate([dk_m, dv_m], axis=1).astype(BF16)
    g_w_kv = _mm_simple("mem_kv_dw", memn, dkv, TN, BF16, D_MODEL, 512, MEM_LEN)
    dmemn = _mm_simple("mem_kv_dx", dkv, w_kv, NT, F32, MEM_LEN, 512, D_MODEL)

    grads = dict(w_kv=g_w_kv, wt_a=g_wt[0], wt_b=g_wt[1], wt_c=g_wt[2], w_out=g_w_out, merge_bias=g_mb,
                 post_norm=g_post)
    dep = None
    if reduce_start is not None:
        reduce_start("rest_sibling", grads)
        dep = reduce_start("rest_chips", grads, sum(a[:8, :128] for a in (dqkv[0], dqkv[3], dqkv[6], dq_b, dq_c)))
    dparts = dqkv + [dq_b, dk_b, dv_b, dq_c] + list(dg) + list(dlog)
    grads["wt_in"] = _in_proj_dw(dparts, hst, dep)
    dep = reduce_start("w_in", grads) if reduce_start is not None else None
    dh = _in_proj_dh(dparts, wt_in, dep)
    if reduce_start is not None:
        dep = reduce_start("w_in_second", grads, dh)
    grad_x, grads["pre_norm"] = _prenorm_bwd(x, pre_norm, dh, dout)
    g_rpb_t = _na_bias_bwd(dbias, dep)
    grads["na_rpb"] = g_rpb_t[:, :15, :31] + jnp.pad(g_rpb_t[:, :14, 64:95], ((0, 0), (1, 0), (0, 0)))
    grads["mem_norm"] = _memnorm_bwd(mem, dmemn, dep)
    return loss, grad_x, grads


ANY = pl.BlockSpec(memory_space=pl.ANY)


def _place():
    return lax.axis_index("x"), lax.axis_index("y"), lax.axis_index("c")


HBM = pl.BlockSpec(memory_space=pltpu.HBM)
SEM = pl.BlockSpec(memory_space=pltpu.SEMAPHORE)
DATAFLOW = pltpu.SideEffectType.DATAFLOW_SIDE_EFFECTING


def _split_copies(kind, srcs, lands, send_sems, recv_sems):
    nt = len(srcs)
    x, y, c = _place()
    copies = []
    if kind == "sibling":
        for q in range(4):
            for t in range(nt):
                k = q * nt + t
                copies.append(pltpu.make_async_remote_copy(
                    src_ref=srcs[t].at[2 * q + 1 - c], dst_ref=lands[t].at[q], send_sem=send_sems.at[k],
                    recv_sem=recv_sems.at[k], device_id=(x, y, 1 - c), device_id_type=MESH_ID))
    elif kind in ("rs_a", "rs_b"):
        half = lands[0].shape[1]
        xn, yn = (1 - x, y, c), (x, 1 - y, c)
        q_xn, q_yn, q_dg = 2 * (1 - x) + y, 2 * x + 1 - y, 2 * (1 - x) + 1 - y
        if kind == "rs_a":
            plan = [(srcs[0].at[q_yn].at[pl.ds(0, half)], 0, yn), (srcs[0].at[q_dg].at[pl.ds(0, half)], 1, yn),
                    (srcs[0].at[q_xn].at[pl.ds(half, half)], 2, xn), (srcs[0].at[q_dg].at[pl.ds(half, half)], 3, xn)]
        else:
            plan = [(srcs[0].at[0], 0, xn), (srcs[0].at[1], 1, yn)]
        for k, (src, slot, to) in enumerate(plan):
            copies.append(pltpu.make_async_remote_copy(
                src_ref=src, dst_ref=lands[0].at[slot], send_sem=send_sems.at[k], recv_sem=recv_sems.at[k],
                device_id=to, device_id_type=MESH_ID))
    elif kind == "gather":
        me = 4 * x + 2 * y + c
        for mask in range(1, 8):
            fx, fy, fc = (mask >> 2) & 1, (mask >> 1) & 1, mask & 1
            to = (1 - x if fx else x, 1 - y if fy else y, 1 - c if fc else c)
            for t in range(nt):
                k = (mask - 1) * nt + t
                copies.append(pltpu.make_async_remote_copy(
                    src_ref=srcs[t], dst_ref=lands[t].at[me], send_sem=send_sems.at[k], recv_sem=recv_sems.at[k],
                    device_id=to, device_id_type=MESH_ID))
    else:
        for s, (tx, ty) in enumerate([(1 - x, y), (x, 1 - y), (1 - x, 1 - y)]):
            for t in range(nt):
                k = s * nt + t
                copies.append(pltpu.make_async_remote_copy(
                    src_ref=srcs[t].at[2 * tx + ty], dst_ref=lands[t].at[s], send_sem=send_sems.at[k],
                    recv_sem=recv_sems.at[k], device_id=(tx, ty, c), device_id_type=MESH_ID))
    return copies


def _split_count(kind, nt):
    return {"gather": 7, "chips": 3, "sibling": 4, "rs_a": 4, "rs_b": 2}[kind] * nt


def _exchange_start(name, kind, srcs, land_shapes, after=None):
    nt = len(srcs)
    n = _split_count(kind, nt)
    dep_specs, dep_args = _dep_operand(after)
    nd = len(dep_args)

    def body(*refs):
        src_refs, land_refs = refs[:nt], refs[nt:2 * nt]
        send_sems, recv_sems = refs[2 * nt + nd], refs[2 * nt + nd + 1]
        token = refs[-1]
        for cp in _split_copies(kind, src_refs, land_refs, send_sems, recv_sems):
            cp.start()
        token[...] = jnp.zeros_like(token)

    lands = [pltpu.with_memory_space_constraint(lax.empty(s.shape, s.dtype), pltpu.HBM) for s in land_shapes]
    res = pl.pallas_call(
        body, name=name,
        out_shape=(pltpu.SemaphoreType.DMA((n,)), pltpu.SemaphoreType.DMA((n,)),
                   *[pltpu.HBM(s.shape, s.dtype) for s in srcs], *[pltpu.HBM(s.shape, s.dtype) for s in land_shapes],
                   jax.ShapeDtypeStruct((8, 128), F32)),
        in_specs=[HBM] * (2 * nt) + dep_specs,
        out_specs=(SEM, SEM, *([HBM] * (2 * nt)), pl.BlockSpec(memory_space=pltpu.VMEM)),
        input_output_aliases={i: 2 + i for i in range(2 * nt)},
        compiler_params=pltpu.CompilerParams(has_side_effects=DATAFLOW))(
            *[pltpu.with_memory_space_constraint(s, pltpu.HBM) for s in srcs], *lands, *dep_args)
    return res[0], res[1], list(res[2:2 + nt]), list(res[2 + nt:2 + 2 * nt]), res[-1]


def _exchange_wait(name, kind, send_sems, recv_sems, srcs, lands, after):
    nt = len(srcs)

    def body(*refs):
        src_refs, land_refs = refs[:nt], refs[nt:2 * nt]
        s_sems, r_sems = refs[2 * nt], refs[2 * nt + 1]
        for cp in _split_copies(kind, src_refs, land_refs, s_sems, r_sems):
            cp.wait_send()
            cp.wait_recv()

    res = pl.pallas_call(
        body, name=name,
        out_shape=tuple(pltpu.HBM(s.shape, s.dtype) for s in list(srcs) + list(lands)),
        in_specs=[HBM] * (2 * nt) + [SEM, SEM, pl.BlockSpec(memory_space=pl.ANY)],
        out_specs=tuple([HBM] * (2 * nt)),
        input_output_aliases={i: i for i in range(2 * nt)},
        compiler_params=pltpu.CompilerParams(has_side_effects=DATAFLOW))(
            *srcs, *lands, send_sems, recv_sems, after)
    return list(res[:nt]), list(res[nt:])


AG_GROUPS = ((0, 3), (3, 4), (7, 2))


def _ag_phase(name, own, land, sems, waits, starts, after=None):
    r = own.shape[0]
    half = r // 2
    ns = len(sems)
    dep_specs, dep_args = _dep_operand(after)
    nd = len(dep_args)
    new_group = None
    if starts:
        (new_group,) = [g for g, (first, n) in enumerate(AG_GROUPS) if first == starts[0]]
        assert list(starts) == list(range(AG_GROUPS[new_group][0], sum(AG_GROUPS[new_group])))

    def body(*refs):
        own_ref, land_ref = refs[0], refs[1]
        sem_refs = list(refs[2:2 + 2 * ns])
        outs = refs[2 + 2 * ns + nd:]
        if starts:
            sem_refs += [outs[0], outs[1]]
        x, y, c = _place()
        me, sib = (x, y, c), (x, y, 1 - c)
        xn, yn, dg = (1 - x, y, c), (x, 1 - y, c), (1 - x, 1 - y, c)

        def other(dev):
            return (dev[0], dev[1], 1 - dev[2])

        def rows(dev, part):
            blk = land_ref.at[4 * dev[0] + 2 * dev[1] + dev[2]]
            return blk if part is None else blk.at[pl.ds(part * half, half)]

        def sem_of(k):
            (g,) = [g for g, (first, n) in enumerate(AG_GROUPS) if first <= k < first + n]
            return sem_refs[2 * g].at[k - AG_GROUPS[g][0]], sem_refs[2 * g + 1].at[k - AG_GROUPS[g][0]]

        sent = {0: (me, None, sib), 1: (me, None, xn), 2: (me, None, yn), 3: (xn, 0, yn), 4: (yn, 1, xn),
                5: (xn, None, sib), 6: (yn, None, sib), 7: (dg, 0, sib), 8: (dg, 1, sib)}
        landed = {0: (sib, None), 1: (xn, None), 2: (yn, None), 3: (dg, 0), 4: (dg, 1), 5: (other(xn), None),
                  6: (other(yn), None), 7: (other(dg), 0), 8: (other(dg), 1)}

        def copy(k, receiving):
            send_sem, recv_sem = sem_of(k)
            dev, part, to = (*landed[k], me) if receiving else sent[k]
            src = own_ref if (dev is me and not receiving) else rows(dev, part)
            return pltpu.make_async_remote_copy(src_ref=src, dst_ref=rows(dev, part), send_sem=send_sem,
                                                recv_sem=recv_sem, device_id=to, device_id_type=MESH_ID)

        for kind, k in waits:
            if kind == "recv":
                copy(k, True).wait_recv()
            else:
                copy(k, False).wait_send()
        for k in starts:
            copy(k, False).start()
        if starts:
            outs[-1][...] = jnp.zeros_like(outs[-1])

    n_new = AG_GROUPS[new_group][1] if starts else 0
    sem_out = (pltpu.SemaphoreType.DMA((n_new,)), pltpu.SemaphoreType.DMA((n_new,))) if starts else ()
    token_out = (jax.ShapeDtypeStruct((8, 128), F32),) if starts else ()
    res = pl.pallas_call(
        body, name=name,
        out_shape=(*sem_out, pltpu.HBM(own.shape, own.dtype), pltpu.HBM(land.shape, land.dtype), *token_out),
        in_specs=[HBM, HBM] + [SEM] * (2 * ns) + dep_specs,
        out_specs=(*([SEM] * len(sem_out)), HBM, HBM, *([pl.BlockSpec(memory_space=pltpu.VMEM)] * len(token_out))),
        input_output_aliases={0: len(sem_out), 1: len(sem_out) + 1},
        compiler_params=pltpu.CompilerParams(has_side_effects=DATAFLOW))(
            own, land, *[a for pair in sems for a in pair], *dep_args)
    if starts:
        return (res[0], res[1]), res[2], res[3], res[4]
    return None, res[0], res[1], None


def _add_sibling(name, term, recv, rows):
    _, r, w = term.shape
    cidx = lax.axis_index("c").astype(jnp.int32).reshape(1)
    like_term = recv.shape[0] == N_DEV

    def body(c_ref, a_ref, b_ref, o_ref):
        o_ref[...] = (a_ref[...].astype(F32) + b_ref[...].astype(F32)).astype(o_ref.dtype)

    grid_spec = pltpu.PrefetchScalarGridSpec(
        num_scalar_prefetch=1, grid=(4, r // rows),
        in_specs=[pl.BlockSpec((None, rows, w), lambda q, i, c_ref: (2 * q + c_ref[0], i, 0)),
                  pl.BlockSpec((None, rows, w), lambda q, i, c_ref: (2 * q + c_ref[0] if like_term else q, i, 0))],
        out_specs=pl.BlockSpec((None, rows, w), lambda q, i, c_ref: (q, i, 0)))
    return pl.pallas_call(
        body, name=name, grid_spec=grid_spec, out_shape=jax.ShapeDtypeStruct((4, r, w), term.dtype),
        compiler_params=_params(("parallel", "parallel")))(cidx, term, recv)


def _add_sibling_small(name, terms, recvs):
    nt = len(terms)

    def body(*refs):
        c = lax.axis_index("c")
        for t_ref, r_ref, o_ref in zip(refs[:nt], refs[nt:2 * nt], refs[2 * nt:]):
            for q in range(4):
                o_ref[q] = (t_ref[2 * q + c].astype(F32) + r_ref[q].astype(F32)).astype(o_ref.dtype)

    return pl.pallas_call(
        body, name=name, out_shape=[jax.ShapeDtypeStruct((4,) + t.shape[1:], t.dtype) for t in terms],
        compiler_params=_params())(*terms, *recvs)


def _reduce_scatter_start(tag, terms, recv1):
    sums = _add_sibling_small("add_sibling_" + tag, terms, recv1)
    lands = [jax.ShapeDtypeStruct((3,) + s.shape[1:], s.dtype) for s in sums]
    send_sems, recv_sems, sums, lands, token = _exchange_start("exchange_chips_start_" + tag, "chips", sums, lands)
    return (tag, send_sems, recv_sems, sums, lands), token


def _reduce_scatter_wait(state, after):
    tag, send_sems, recv_sems, sums, lands = state
    return _exchange_wait("exchange_chips_wait_" + tag, "chips", send_sems, recv_sems, sums, lands, after)


def _adamw(name, w, g, m, v, dep=None):
    dep_specs, dep_args = _dep_operand(dep)

    def body(w_ref, g_ref, m_ref, v_ref, *rest):
        d_ref, nm_ref, nv_ref = rest[-3:]
        d_ref[...], nm_ref[...], nv_ref[...] = _adam_math(w_ref[...], g_ref[...], m_ref[...], v_ref[...])

    whole = pl.BlockSpec(memory_space=pltpu.VMEM)
    return pl.pallas_call(
        body, name=name, in_specs=[whole] * 4 + dep_specs, out_shape=[jax.ShapeDtypeStruct(w.shape, F32)] * 3,
        compiler_params=_params())(w, g, m, v, *dep_args)


def _adam_math(w, g, m, v):
    nm = ADAM_B1 * m + (1.0 - ADAM_B1) * g
    nv = ADAM_B2 * v + (1.0 - ADAM_B2) * (g * g)
    c1 = 1.0 - ADAM_B1 ** ADAM_STEP
    c2 = 1.0 - ADAM_B2 ** ADAM_STEP
    return -ADAM_LR * ((nm / c1) / (jnp.sqrt(nv / c2) + ADAM_EPS) + ADAM_WD * w), nm, nv


def _presum_halves(sums, landed):
    _, r, w = sums.shape
    rows = r // 2
    x, y = lax.axis_index("x"), lax.axis_index("y")
    dest = jnp.stack([2 * (1 - x) + y, 2 * x + 1 - y]).astype(jnp.int32)

    def body(q_ref, a_ref, b_ref, o_ref):
        o_ref[...] = (a_ref[...].astype(F32) + b_ref[...].astype(F32)).astype(o_ref.dtype)

    grid_spec = pltpu.PrefetchScalarGridSpec(
        num_scalar_prefetch=1, grid=(2,),
        in_specs=[pl.BlockSpec((None, rows, w), lambda h, q_ref: (q_ref[h], h, 0)),
                  pl.BlockSpec((None, rows, w), lambda h, q_ref: (1 + 2 * h, 0, 0))],
        out_specs=pl.BlockSpec((None, rows, w), lambda h, q_ref: (h, 0, 0)))
    return pl.pallas_call(
        body, name="presum_halves", grid_spec=grid_spec, out_shape=jax.ShapeDtypeStruct((2, r // 2, w), sums.dtype),
        compiler_params=_params(("parallel",)))(dest, sums, landed)


def _adamw_halves(name, sums, landed_a, landed_b, w, m, v, rows):
    r, c = w.shape
    half = c // 2
    qidx = (2 * lax.axis_index("x") + lax.axis_index("y")).astype(jnp.int32).reshape(1)

    def body(q_ref, s_ref, a_ref, b_ref, w_ref, m_ref, v_ref, g_ref, d_ref, nm_ref, nv_ref):
        first = (s_ref[:half, :].astype(F32) + a_ref[0].astype(F32)) + b_ref[0].astype(F32)
        second = (s_ref[half:, :].astype(F32) + a_ref[2].astype(F32)) + b_ref[1].astype(F32)
        g = jnp.concatenate([first, second], axis=0).T
        g_ref[...] = g
        d_ref[...], nm_ref[...], nv_ref[...] = _adam_math(w_ref[...], g, m_ref[...], v_ref[...])

    row = pl.BlockSpec((rows, c), lambda i, q_ref: (i, 0))
    grid_spec = pltpu.PrefetchScalarGridSpec(
        num_scalar_prefetch=1, grid=(r // rows,),
        in_specs=[pl.BlockSpec((None, c, rows), lambda i, q_ref: (q_ref[0], 0, i)),
                  pl.BlockSpec((4, half, rows), lambda i, q_ref: (0, 0, i)),
                  pl.BlockSpec((2, half, rows), lambda i, q_ref: (0, 0, i)), row, row, row],
        out_specs=[row] * 4)
    return pl.pallas_call(
        body, name=name, grid_spec=grid_spec, out_shape=[jax.ShapeDtypeStruct((r, c), F32)] * 4,
        compiler_params=_params(("parallel",)))(qidx, sums, landed_a, landed_b, w, m, v)


def _adamw_chips_small(name, items):
    n = len(items)

    def body(*refs):
        q = 2 * lax.axis_index("x") + lax.axis_index("y")
        ins, outs = refs[:5 * n], refs[5 * n:]
        for i, (_, _, w, _, _, transposed) in enumerate(items):
            s_ref, r_ref, w_ref, m_ref, v_ref = ins[5 * i:5 * i + 5]
            g_ref, d_ref, nm_ref, nv_ref = outs[4 * i:4 * i + 4]
            g = (s_ref[q].astype(F32) + r_ref[0].astype(F32)) + (r_ref[1].astype(F32) + r_ref[2].astype(F32))
            g = g.T if transposed else g[:w.shape[0]]
            g_ref[...] = g
            d_ref[...], nm_ref[...], nv_ref[...] = _adam_math(w_ref[...], g, m_ref[...], v_ref[...])

    res = pl.pallas_call(
        body, name=name, out_shape=[jax.ShapeDtypeStruct(it[2].shape, F32) for it in items for _ in range(4)],
        compiler_params=_params())(*[a for it in items for a in it[:5]])
    return [res[4 * i:4 * i + 4] for i in range(n)]


def _sum_devices(gathered):
    def body(g_ref, o_ref):
        acc = g_ref[0]
        for j in range(1, N_DEV):
            acc = acc + g_ref[j]
        o_ref[...] = acc

    return pl.pallas_call(
        body, name="sum_devices", out_shape=jax.ShapeDtypeStruct(gathered.shape[1:], F32),
        compiler_params=_params())(gathered)


def _rows128(a, rows):
    flat = a.reshape(-1)
    return jnp.pad(flat, (0, rows * 128 - flat.shape[0])).reshape(rows, 128)


def kernel(x, mem, pre_norm, w_in, merge_bias, na_rpb, mem_norm, w_mem_kv, w_branch_a, w_branch_b, w_branch_c, w_out, post_norm, loss_target, m_pre_norm, m_w_in, m_merge_bias, m_na_rpb, m_mem_norm, m_w_mem_kv, m_w_branch_a, m_w_branch_b, m_w_branch_c, m_w_out, m_post_norm, v_pre_norm, v_w_in, v_merge_bias, v_na_rpb, v_mem_norm, v_w_mem_kv, v_w_branch_a, v_w_branch_b, v_w_branch_c, v_w_out, v_post_norm):
    wt_in_s = w_in[0].T.astype(BF16)
    rows_s = jnp.concatenate([w_mem_kv[0], w_out[0]], axis=0).astype(BF16)
    cols_s = jnp.concatenate([w_branch_a[0].T, w_branch_b[0].T, w_branch_c[0].T], axis=0).astype(BF16)
    mb_s = jnp.pad(merge_bias[0], ((0, 5), (0, 0)))
    me = 4 * lax.axis_index("x") + 2 * lax.axis_index("y") + lax.axis_index("c")

    chip = 2 * lax.axis_index("x") + lax.axis_index("y")

    def first_block(q):
        return jnp.where(q == 0, 0, jnp.where(q == 1, 6, jnp.where(q == 2, 11, 17)))

    five = jnp.arange(5, dtype=jnp.int32)
    near, far = jnp.where(chip < 2, 5, 16), jnp.where(chip < 2, 16, 5)
    order1 = (first_block(chip) + five).astype(jnp.int32)
    order2 = jnp.concatenate([first_block(chip ^ 1) + five, near[None], first_block(chip ^ 2) + five]).astype(jnp.int32)
    order3 = jnp.concatenate([first_block(chip ^ 3) + five, far[None]]).astype(jnp.int32)
    tabs = _rope_tables()

    def weights_of(land):
        return land.reshape(N_IN, D_MODEL)

    land = pltpu.with_memory_space_constraint(lax.empty((N_DEV,) + wt_in_s.shape, BF16), pltpu.HBM)
    own = pltpu.with_memory_space_constraint(wt_in_s, pltpu.HBM)
    sem_a, own, land, token = _ag_phase("ag_start", own, land, [], [], [0, 1, 2])
    hs, hst = _prenorm_fold(x[0], pre_norm, token)
    _, own, land, _ = _ag_phase("ag_wait0", own, land, [sem_a], [("recv", 0)], [], hs)
    land = lax.dynamic_update_slice(land, own[None], (me, 0, 0))
    parts = _in_proj("in_proj_1", hs, weights_of(land), tabs, order1)
    bias = _na_bias(jnp.pad(na_rpb[0], ((0, 0), (0, 1), (0, 128 - 31))), parts)
    sem_b, own, land, _ = _ag_phase("ag_mid1", own, land, [sem_a], [("recv", 1), ("recv", 2)], [3, 4, 5, 6], bias)
    _, own, land, _ = _ag_phase("ag_wait1", own, land, [sem_a, sem_b], [("recv", 5), ("recv", 6)], [])
    parts = _in_proj("in_proj_2", hs, weights_of(land), tabs, order2, parts)
    sem_c, own, land, _ = _ag_phase("ag_mid2", own, land, [sem_a, sem_b], [("recv", 3), ("recv", 4)], [7, 8], parts)
    _, own, land, _ = _ag_phase("ag_end", own, land, [sem_a, sem_b, sem_c],
                                [("recv", 7), ("recv", 8)] + [("send", k) for k in range(9)], [])
    wt_in = weights_of(land)

    late_own = [rows_s, cols_s, mb_s]
    late_lands = [jax.ShapeDtypeStruct((N_DEV,) + s.shape, s.dtype) for s in late_own]
    l_send, l_recv, late_own, late_lands, late_token = _exchange_start("gather_late_start", "gather", late_own,
                                                                       late_lands, after=wt_in)
    parts = _in_proj("in_proj_3", hs, wt_in, tabs, order3, parts, late_token)

    def late_weights(after):
        own, lands = _exchange_wait("gather_late_wait", "gather", l_send, l_recv, late_own, late_lands, after)
        g_rows, g_cols, g_mb = [lax.dynamic_update_slice(land, o[None], (me, 0, 0)) for land, o in zip(lands, own)]
        return (g_mb[:, :3].transpose(1, 0, 2).reshape(3, D_MODEL),
                g_rows[:, :128].reshape(D_MODEL, D_MODEL), g_cols[:, 0:128].reshape(D_MODEL, 512),
                g_cols[:, 128:256].reshape(D_MODEL, 512), g_cols[:, 256:384].reshape(D_MODEL, 512),
                g_rows[:, 128:].reshape(D_MODEL, D_MODEL))

    rest_state, rest_sibling, w_in_a, w_in_b = [], [], [], []

    def reduce_start(phase, grads, after=None):
        if phase == "rest_sibling":
            gmb_t = jnp.pad(grads["merge_bias"].reshape(3, N_DEV, 128).transpose(1, 0, 2), ((0, 0), (0, 5), (0, 0)))
            terms = [grads["w_kv"].reshape(N_DEV, 128, D_MODEL), grads["w_out"].reshape(N_DEV, 128, D_MODEL),
                     grads["wt_a"].reshape(N_DEV, 128, 512), grads["wt_b"].reshape(N_DEV, 128, 512),
                     grads["wt_c"].reshape(N_DEV, 128, 512), gmb_t]
            lands = [jax.ShapeDtypeStruct((4,) + t.shape[1:], t.dtype) for t in terms]
            rest_sibling.extend(_exchange_start("exchange_sibling_start_rest", "sibling", terms, lands)[:4])
            return None
        if phase == "rest_chips":
            s_send, s_recv, terms, lands = rest_sibling
            terms, recv1 = _exchange_wait("exchange_sibling_wait_rest", "sibling", s_send, s_recv, terms, lands, after)
            state, token = _reduce_scatter_start("rest", terms, recv1)
            rest_state.append(state)
            return token
        if phase == "w_in":
            own, sibling = [a.reshape(N_DEV, SHARD_IN, D_MODEL) for a in grads["wt_in"]]
            sums = _add_sibling("add_sibling_w_in", own, sibling, SHARD_IN)
            lands = [jax.ShapeDtypeStruct((4, SHARD_IN // 2, D_MODEL), BF16)]
            w_in_a.extend(_exchange_start("rs_a_start", "rs_a", [sums], lands))
            return w_in_a[4]
        (sums,), (landed_a,) = _exchange_wait("rs_a_wait", "rs_a", w_in_a[0], w_in_a[1], w_in_a[2], w_in_a[3], after)
        lands = [jax.ShapeDtypeStruct((2, SHARD_IN // 2, D_MODEL), BF16)]
        w_in_b.extend(_exchange_start("rs_b_start", "rs_b", [_presum_halves(sums, landed_a)], lands))
        w_in_b.extend([sums, landed_a])
        return w_in_b[4]

    loss_term, grad_x, grads = _local_step(
        x[0], hst, parts, tabs, bias, mem[0], loss_target[0], pre_norm, mem_norm, post_norm, wt_in, late_weights,
        reduce_start=reduce_start)

    small = jnp.concatenate([_rows128(grads["pre_norm"], 8), _rows128(grads["mem_norm"], 8),
                             _rows128(grads["post_norm"], 8), _rows128(grads["na_rpb"], 32),
                             _rows128(loss_term, 8)], axis=0)
    s_send, s_recv, s_own, s_land, s_token = _exchange_start(
        "gather_small_start", "gather", [small], [jax.ShapeDtypeStruct((N_DEV,) + small.shape, F32)])
    grad = {}
    weights = {
        "pre_norm": (pre_norm, m_pre_norm, v_pre_norm), "w_in": (w_in, m_w_in, v_w_in),
        "merge_bias": (merge_bias, m_merge_bias, v_merge_bias), "na_rpb": (na_rpb, m_na_rpb, v_na_rpb),
        "mem_norm": (mem_norm, m_mem_norm, v_mem_norm), "w_mem_kv": (w_mem_kv, m_w_mem_kv, v_w_mem_kv),
        "w_branch_a": (w_branch_a, m_w_branch_a, v_w_branch_a), "w_branch_b": (w_branch_b, m_w_branch_b, v_w_branch_b),
        "w_branch_c": (w_branch_c, m_w_branch_c, v_w_branch_c), "w_out": (w_out, m_w_out, v_w_out),
        "post_norm": (post_norm, m_post_norm, v_post_norm)}
    order = ["pre_norm", "w_in", "merge_bias", "na_rpb", "mem_norm", "w_mem_kv", "w_branch_a", "w_branch_b",
             "w_branch_c", "w_out", "post_norm"]
    delta, new_m, new_v = {}, {}, {}

    def update(n, dep=None):
        w, m, v = weights[n]
        shape = w.shape
        two_d = (-1, shape[-1])
        dl, nm, nv = _adamw("adamw_" + n, w.reshape(two_d), grad[n].reshape(two_d), m.reshape(two_d),
                            v.reshape(two_d), dep)
        delta[n], new_m[n], new_v[n] = dl.reshape(shape), nm.reshape(shape), nv.reshape(shape)
        return dl

    sums, recv2 = _reduce_scatter_wait(rest_state[0], s_token)
    rest = (("w_mem_kv", False), ("w_out", False), ("w_branch_a", True), ("w_branch_b", True), ("w_branch_c", True),
            ("merge_bias", False))
    items = [(sums[i], recv2[i], *[a[0] for a in weights[n]], transposed) for i, (n, transposed) in enumerate(rest)]
    for (n, _), (g, dl, nm, nv) in zip(rest, _adamw_chips_small("adamw_rest", items)):
        grad[n], delta[n], new_m[n], new_v[n] = g[None], dl[None], nm[None], nv[None]
    s_own, s_land = _exchange_wait("gather_small_wait", "gather", s_send, s_recv, s_own, s_land, delta["w_out"])
    total = _sum_devices(lax.dynamic_update_slice(s_land[0], s_own[0][None], (me, 0, 0)))
    loss = total[56, 0]
    grad.update({"pre_norm": total[0:8].reshape(1, D_MODEL), "mem_norm": total[8:16].reshape(1, D_MODEL),
                 "post_norm": total[16:24].reshape(1, D_MODEL),
                 "na_rpb": total[24:56].reshape(-1)[:8 * 15 * 31].reshape(1, 8, 15, 31)})
    dep = None
    for n in ("pre_norm", "na_rpb", "mem_norm", "post_norm"):
        dep = update(n, dep)
    _, (landed_b,) = _exchange_wait("rs_b_wait", "rs_b", w_in_b[0], w_in_b[1], w_in_b[2], w_in_b[3], dep)
    g, dl, nm, nv = _adamw_halves("adamw_w_in", w_in_b[5], w_in_b[6], landed_b, w_in[0], m_w_in[0], v_w_in[0], 256)
    grad["w_in"], delta["w_in"], new_m["w_in"], new_v["w_in"] = g[None], dl[None], nm[None], nv[None]

    return (loss, grad_x[None], *[grad[n] for n in order], *[delta[n] for n in order],
            *[new_m[n] for n in order], *[new_v[n] for n in order])
```

```python
import functools

import numpy as np
import jax
import jax.numpy as jnp
from jax import lax
from jax.experimental import pallas as pl
from jax.experimental.pallas import tpu as pltpu

F32 = jnp.float32
BF16 = jnp.bfloat16

SEQ = 2048
D_MODEL = 1024
N_IN = 11264
N_DEV = 8
SHARD_IN = N_IN // N_DEV
HEAD_DIM = 64
GRID_W = 64
NA_ROWS = 8
MEM_LEN = 256
DILATIONS = (1, 4, 16)
REACH = 64
ROPE_THETA = 500000.0
ROPE_DIM = 16
EPS = 1e-6
NEG = -1e30
ADAM_LR = 0.001
ADAM_B1 = 0.9
ADAM_B2 = 0.999
ADAM_EPS = 1e-08
ADAM_WD = 0.01
ADAM_STEP = 10

VMEM_LIMIT_BYTES = 56 * 1024 * 1024
MESH_ID = pl.DeviceIdType.MESH

NN = (((1,), (0,)), ((), ()))
NT = (((1,), (1,)), ((), ()))
TN = (((0,), (0,)), ((), ()))


def _params(sem=None):
    return pltpu.CompilerParams(dimension_semantics=sem, vmem_limit_bytes=VMEM_LIMIT_BYTES)


def _iota(shape, dim):
    return lax.broadcasted_iota(jnp.int32, shape, dim)


def _sigmoid(x):
    return 1.0 / (1.0 + jnp.exp(-x))


def _rope_tables():
    half = ROPE_DIM // 2
    inv = (ROPE_THETA ** (-np.arange(half, dtype=np.float64) * 2.0 / ROPE_DIM)).astype(np.float32)
    pos = np.arange(SEQ, dtype=np.float32)
    ang = pos[:, None] * inv[None, :]
    cos, sin = np.cos(ang), np.sin(ang)
    zeros = np.zeros_like(cos)
    rest = HEAD_DIM - ROPE_DIM
    c64 = np.concatenate([cos, cos, np.ones((SEQ, rest), np.float32)], axis=1)
    s1 = np.concatenate([zeros, sin, np.zeros((SEQ, rest), np.float32)], axis=1)
    s2 = np.concatenate([-sin, zeros, np.zeros((SEQ, rest), np.float32)], axis=1)

    def fold(t, d):
        return t.reshape(SEQ // d, d, t.shape[1]).transpose(1, 0, 2).reshape(SEQ, t.shape[1])

    tabs = [np.stack([np.tile(fold(t, d), (1, 2)) for t in (c64, s1, s2)], axis=0) for d in DILATIONS]
    return jnp.asarray(np.stack(tabs, axis=0), dtype=F32)


def _rope(a, c, s1, s2):
    return a * c + pltpu.roll(a, 8, 1) * s1 + pltpu.roll(a, 120, 1) * s2


def _rope_t(a, c, s1, s2):
    return a * c + pltpu.roll(a * s1, 120, 1) + pltpu.roll(a * s2, 8, 1)


def _perm_of_block(j):
    return jnp.where(j < 3, 0, jnp.where(j < 6, 1, jnp.where(j < 9, 2, 0)))


def _mm(name, a, b, out_shape, out_dtype, grid, a_spec, b_spec, o_spec, acc_shape, dims, k_axis, nk):
    def body(a_ref, b_ref, o_ref, acc_ref):
        k = pl.program_id(k_axis)

        @pl.when(k == 0)
        def _():
            acc_ref[...] = jnp.zeros(acc_shape, F32)

        acc_ref[...] += lax.dot_general(a_ref[...], b_ref[...], dims, preferred_element_type=F32)

        @pl.when(k == nk - 1)
        def _():
            o_ref[...] = acc_ref[...].astype(out_dtype)

    sem = tuple("arbitrary" if ax == k_axis else "parallel" for ax in range(len(grid)))
    return pl.pallas_call(
        body, name=name, grid=grid, in_specs=[a_spec, b_spec], out_specs=o_spec,
        out_shape=jax.ShapeDtypeStruct(out_shape, out_dtype),
        scratch_shapes=[pltpu.VMEM(acc_shape, F32)], compiler_params=_params(sem))(a, b)


def _mm_simple(name, a, b, dims, out_dtype, tm, tn, tk):
    if dims is NN:
        m, kk = a.shape
        n = b.shape[1]
        a_spec = pl.BlockSpec((tm, tk), lambda i, j, k: (i, k))
        b_spec = pl.BlockSpec((tk, tn), lambda i, j, k: (k, j))
    elif dims is NT:
        m, kk = a.shape
        n = b.shape[0]
        a_spec = pl.BlockSpec((tm, tk), lambda i, j, k: (i, k))
        b_spec = pl.BlockSpec((tn, tk), lambda i, j, k: (j, k))
    else:
        kk, m = a.shape
        n = b.shape[1]
        a_spec = pl.BlockSpec((tk, tm), lambda i, j, k: (k, i))
        b_spec = pl.BlockSpec((tk, tn), lambda i, j, k: (k, j))
    grid = (m // tm, n // tn, kk // tk)
    o_spec = pl.BlockSpec((tm, tn), lambda i, j, k: (i, j))
    return _mm(name, a, b, (m, n), out_dtype, grid, a_spec, b_spec, o_spec, (tm, tn), dims, 2, kk // tk)


def _rmsnorm_fwd(name, x, gain, rows):
    n, d = x.shape

    def body(x_ref, g_ref, o_ref):
        xv = x_ref[...]
        rstd = lax.rsqrt(jnp.mean(xv * xv, axis=1, keepdims=True) + EPS)
        o_ref[...] = (xv * rstd * g_ref[...]).astype(BF16)

    return pl.pallas_call(
        body, name=name, grid=(n // rows,),
        in_specs=[pl.BlockSpec((rows, d), lambda i: (i, 0)), pl.BlockSpec((1, d), lambda i: (0, 0))],
        out_specs=pl.BlockSpec((rows, d), lambda i: (i, 0)),
        out_shape=jax.ShapeDtypeStruct((n, d), BF16), compiler_params=_params(("parallel",)))(x, gain)


def _folded_rows(first, rows, d):
    if d == 1:
        return pl.ds(pl.multiple_of(first, rows), rows)
    mlen = SEQ // d
    return pl.ds((first % mlen) * d + first // mlen, rows, stride=d)


def _prenorm_fold(x, gain, dep=None):
    rows = 128
    nchunk = D_MODEL // 128
    dep_specs, dep_args = _dep_operand(dep)

    def body(*refs):
        x_refs, g_ref, hs_ref, hst_ref = refs[:nchunk], refs[nchunk], refs[-2], refs[-1]
        first = pl.program_id(0) * rows
        for p, d in enumerate(DILATIONS):
            idx = _folded_rows(first, rows, d)
            xv = jnp.concatenate([r[idx, :] for r in x_refs], axis=1)
            rstd = lax.rsqrt(jnp.mean(xv * xv, axis=1, keepdims=True) + EPS)
            h = xv * rstd * g_ref[...]
            hs_ref[p] = h.astype(BF16)
            hst_ref[p] = h.T.astype(BF16)

    x_specs = [pl.BlockSpec((SEQ, 128), functools.partial(lambda c, i: (0, c), c)) for c in range(nchunk)]
    return pl.pallas_call(
        body, name="prenorm", grid=(SEQ // rows,),
        in_specs=x_specs + [pl.BlockSpec((1, D_MODEL), lambda i: (0, 0))] + dep_specs,
        out_specs=[pl.BlockSpec((3, rows, D_MODEL), lambda i: (0, i, 0)),
                   pl.BlockSpec((3, D_MODEL, rows), lambda i: (0, 0, i))],
        out_shape=[jax.ShapeDtypeStruct((3, SEQ, D_MODEL), BF16), jax.ShapeDtypeStruct((3, D_MODEL, SEQ), BF16)],
        compiler_params=_params(("parallel",)))(*([x] * nchunk), gain, *dep_args)


def _prenorm_bwd(x, gain, dh, dout):
    rows = 512

    def body(x_ref, g_ref, a_ref, do_ref, dx_ref, gg_ref):
        xv = x_ref[...]
        rstd = lax.rsqrt(jnp.mean(xv * xv, axis=1, keepdims=True) + EPS)
        xn = xv * rstd
        dh = jnp.concatenate([a_ref[c] for c in range(D_MODEL // 128)], axis=1)
        gdh = dh * g_ref[...]
        dx_ref[...] = rstd * (gdh - xn * jnp.mean(gdh * xn, axis=1, keepdims=True)) + do_ref[...]

        @pl.when(pl.program_id(0) == 0)
        def _():
            gg_ref[...] = jnp.zeros((1, D_MODEL), F32)

        gg_ref[...] += jnp.sum(dh * xn, axis=0, keepdims=True)

    row = pl.BlockSpec((rows, D_MODEL), lambda i: (i, 0))
    vec = pl.BlockSpec((1, D_MODEL), lambda i: (0, 0))
    return pl.pallas_call(
        body, name="prenorm_bwd", grid=(SEQ // rows,),
        in_specs=[row, vec, pl.BlockSpec((D_MODEL // 128, rows, 128), lambda i: (0, i, 0)), row], out_specs=[row, vec],
        out_shape=[jax.ShapeDtypeStruct((SEQ, D_MODEL), F32), jax.ShapeDtypeStruct((1, D_MODEL), F32)],
        compiler_params=_params(("arbitrary",)))(x, gain, dh, dout)


def _memnorm_bwd(mem, dmemn, dep=None):
    dep_specs, dep_args = _dep_operand(dep)

    def body(m_ref, d_ref, *rest):
        mv = m_ref[...]
        rstd = lax.rsqrt(jnp.mean(mv * mv, axis=1, keepdims=True) + EPS)
        rest[-1][...] = jnp.sum(d_ref[...] * mv * rstd, axis=0, keepdims=True)

    whole = pl.BlockSpec(memory_space=pltpu.VMEM)
    return pl.pallas_call(
        body, name="memnorm_bwd", in_specs=[whole, whole] + dep_specs,
        out_shape=jax.ShapeDtypeStruct((1, D_MODEL), F32), compiler_params=_params())(mem, dmemn, *dep_args)


def _dep_operand(dep):
    return ([], []) if dep is None else ([pl.BlockSpec(memory_space=pl.ANY)], [dep])


def _in_proj(name, hs, wt, tabs, order, prev=None, dep=None):
    tm, tn = 512, 512
    prev_specs, prev_args = ([], []) if prev is None else ([ANY], [prev])
    dep_specs, dep_args = _dep_operand(dep)

    def body(order_ref, h_ref, w_ref, t_ref, *rest):
        o_ref = rest[-1]
        j = order_ref[pl.program_id(0)]
        is_rope = jnp.logical_and(j < 9, j % 3 != 2)
        row_slices = [slice(r * tm, (r + 1) * tm) for r in range(SEQ // tm)]

        def product(rs):
            return lax.dot_general(h_ref[rs, :], w_ref[...], NT, preferred_element_type=F32)

        @pl.when(is_rope)
        def _():
            for rs in row_slices:
                acc = product(rs)
                c, s1, s2 = t_ref[0, rs, :], t_ref[1, rs, :], t_ref[2, rs, :]
                for q in range(tn // 128):
                    a = acc[:, q * 128:(q + 1) * 128]
                    o_ref[rs, q * 128:(q + 1) * 128] = _rope(a, c, s1, s2).astype(BF16)

        @pl.when(jnp.logical_not(is_rope))
        def _():
            for rs in row_slices:
                o_ref[rs, :] = product(rs).astype(BF16)

    grid_spec = pltpu.PrefetchScalarGridSpec(
        num_scalar_prefetch=1, grid=(order.shape[0],),
        in_specs=[pl.BlockSpec((None, SEQ, D_MODEL), lambda t, o: (_perm_of_block(o[t]), 0, 0)),
                  pl.BlockSpec((tn, D_MODEL), lambda t, o: (o[t], 0)),
                  pl.BlockSpec((None, 3, SEQ, 128), lambda t, o: (_perm_of_block(o[t]), 0, 0, 0))] + prev_specs
        + dep_specs,
        out_specs=pl.BlockSpec((SEQ, tn), lambda t, o: (0, o[t])))
    return pl.pallas_call(
        body, name=name, grid_spec=grid_spec, out_shape=jax.ShapeDtypeStruct((SEQ, N_IN), BF16),
        input_output_aliases={} if prev is None else {4: 0},
        compiler_params=_params(("arbitrary",)))(order, hs, wt, tabs, *prev_args, *dep_args)


def _piece_blocks(pieces):
    return [(a, h * 512) for a, p in enumerate(pieces) for h in range(p.shape[1] // 512)]


def _block_fetch(piece_refs, blocks, buf, sem):
    def start(block, slot):
        for b, (a, col) in enumerate(blocks):
            @pl.when(block == b)
            def _():
                pltpu.make_async_copy(piece_refs[a].at[:, pl.ds(col, 512)], buf.at[slot], sem.at[slot]).start()

    def wait(slot):
        pltpu.make_async_copy(piece_refs[0].at[:, pl.ds(0, 512)], buf.at[slot], sem.at[slot]).wait()

    return start, wait


def _in_proj_dw(pieces, hst, dep=None):
    tn = 512
    blocks = _piece_blocks(pieces)
    nblk = len(blocks)
    npc = len(pieces)
    dep_specs, dep_args = _dep_operand(dep)

    def body(h_ref, *rest):
        piece_refs = rest[:npc]
        own_out, mirror, buf, sem, out_buf, send_sems, recv_sem, local_sems = rest[-8:]
        j = pl.program_id(0)
        slot = j % 2
        start, wait = _block_fetch(piece_refs, blocks, buf, sem)
        x, y, c = _place()

        def rows_of(step):
            return pl.ds(pl.multiple_of(step * tn, tn), tn)

        def to_sibling(step, slot_):
            return pltpu.make_async_remote_copy(
                src_ref=out_buf.at[slot_], dst_ref=mirror.at[rows_of(step)],
                send_sem=send_sems.at[slot_], recv_sem=recv_sem, device_id=(x, y, 1 - c), device_id_type=MESH_ID)

        def to_own(step, slot_):
            return pltpu.make_async_copy(out_buf.at[slot_], own_out.at[rows_of(step)], local_sems.at[slot_])

        @pl.when(j == 0)
        def _():
            start(j, slot)

        wait(slot)

        @pl.when(j + 1 < nblk)
        def _():
            start(j + 1, 1 - slot)

        acc = jnp.dot(h_ref[...], buf[slot], preferred_element_type=F32)

        @pl.when(j >= 2)
        def _():
            to_sibling(j - 2, slot).wait_send()
            to_own(j - 2, slot).wait()

        out_buf[slot] = acc.T.astype(BF16)
        to_sibling(j, slot).start()
        to_own(j, slot).start()

        @pl.when(j == nblk - 1)
        def _():
            to_sibling(j - 1, 1 - slot).wait_send()
            to_own(j - 1, 1 - slot).wait()
            to_sibling(j, slot).wait_send()
            to_own(j, slot).wait()
            pltpu.make_async_remote_copy(src_ref=mirror, dst_ref=mirror, send_sem=send_sems.at[0], recv_sem=recv_sem,
                                         device_id=(x, y, 1 - c), device_id_type=MESH_ID).wait_recv()

    return pl.pallas_call(
        body, name="in_proj_dw", grid=(nblk,),
        in_specs=[pl.BlockSpec((None, D_MODEL, SEQ), lambda j: (_perm_of_block(j), 0, 0))] + [ANY] * npc + dep_specs,
        out_specs=[ANY, ANY],
        out_shape=[jax.ShapeDtypeStruct((N_IN, D_MODEL), BF16), jax.ShapeDtypeStruct((N_IN, D_MODEL), BF16)],
        scratch_shapes=[pltpu.VMEM((2, SEQ, tn), BF16), pltpu.SemaphoreType.DMA((2,)),
                        pltpu.VMEM((2, tn, D_MODEL), BF16), pltpu.SemaphoreType.DMA((2,)), pltpu.SemaphoreType.DMA,
                        pltpu.SemaphoreType.DMA((2,))],
        compiler_params=_params(("arbitrary",)))(hst, *pieces, *dep_args)


def _in_proj_dh(pieces, wt, dep=None):
    tk = 512
    blocks = _piece_blocks(pieces)
    nblk = len(blocks)
    npc = len(pieces)
    nchunk = D_MODEL // 128

    def col(s):
        return jnp.where(s < 3, s, jnp.where(s < 16, s + 6, s - 13))

    dep_specs, dep_args = _dep_operand(dep)

    def body(w_ref, *rest):
        piece_refs = rest[:npc]
        o_ref, acc_ref, buf, sem = rest[-4:]
        s = pl.program_id(0)
        slot = s % 2
        start, wait = _block_fetch(piece_refs, blocks, buf, sem)

        @pl.when(s == 0)
        def _():
            start(col(s), slot)

        wait(slot)

        @pl.when(s + 1 < nblk)
        def _():
            start(col(s + 1), 1 - slot)

        row_slices = [slice(r * 512, (r + 1) * 512) for r in range(SEQ // 512)]

        def product(rs):
            return jnp.dot(buf[slot, rs, :], w_ref[...], preferred_element_type=F32)

        def accumulate(cond, to_out, init):
            @pl.when(cond)
            def _():
                for rs in row_slices:
                    prod = product(rs)
                    if not to_out:
                        if init:
                            acc_ref[rs, :] = prod
                        else:
                            acc_ref[rs, :] += prod
                        continue
                    for c in range(nchunk):
                        if init:
                            o_ref[c, rs, :] = prod[:, c * 128:(c + 1) * 128]
                        else:
                            o_ref[c, rs, :] += prod[:, c * 128:(c + 1) * 128]

        accumulate(s == 0, True, True)
        accumulate(jnp.logical_and(s > 0, s < 16), True, False)
        accumulate(jnp.logical_or(s == 16, s == 19), False, True)
        accumulate(jnp.logical_and(s > 16, s != 19), False, False)
        for last, d in ((18, 4), (21, 16)):
            @pl.when(s == last)
            def _():
                mlen = SEQ // d
                for r in range(d):
                    for c in range(nchunk):
                        o_ref[c, pl.ds(r, mlen, stride=d), :] += acc_ref[r * mlen:(r + 1) * mlen,
                                                                         c * 128:(c + 1) * 128]

    return pl.pallas_call(
        body, name="in_proj_dh", grid=(nblk,),
        in_specs=[pl.BlockSpec((tk, D_MODEL), lambda s: (col(s), 0))] + [ANY] * npc + dep_specs,
        out_specs=pl.BlockSpec((nchunk, SEQ, 128), lambda s: (0, 0, 0)),
        out_shape=jax.ShapeDtypeStruct((nchunk, SEQ, 128), F32),
        scratch_shapes=[pltpu.VMEM((SEQ, D_MODEL), F32), pltpu.VMEM((2, SEQ, tk), BF16),
                        pltpu.SemaphoreType.DMA((2,))],
        compiler_params=_params(("arbitrary",)))(wt, *pieces, *dep_args)


def _head_lanes(lanes, hh):
    return lanes >= 64 if hh == 1 else lanes < 64


def _head_rows(x, lanes, hh, pair):
    if not pair:
        return jnp.max(x, axis=1, keepdims=True)
    return jnp.max(jnp.where(_head_lanes(lanes, hh), x, -jnp.inf), axis=1, keepdims=True)


def _mask_head(x, lanes, hh, pair, scale=1.0):
    if not pair:
        return x
    xf = x.astype(F32) if scale == 1.0 else x.astype(F32) * scale
    return jnp.where(_head_lanes(lanes, hh), xf, 0.0).astype(BF16)


def _window(mode, qi, tq, mlen, tk):
    if mode == "dil":
        q0 = qi * tq
        seg = (q0 // mlen) * mlen
        ks = jnp.clip(q0 - REACH, seg, seg + mlen - tk)
        return pl.multiple_of(ks, 64)
    if mode == "na":
        r_start = jnp.clip(qi - NA_ROWS // 2, 0, SEQ // GRID_W - NA_ROWS)
        return pl.multiple_of(r_start * GRID_W, 64)
    return 0


def _band_mask(qi, tq, tk, ks):
    qpos = qi * tq + _iota((tq, tk), 0)
    kpos = ks + _iota((tq, tk), 1)
    return jnp.where(jnp.abs(qpos - kpos) <= REACH, 0.0, NEG).astype(F32)


def _stack_heads(x, lanes, pair, scale=1.0):
    if not pair:
        return x
    return jnp.concatenate([_mask_head(x, lanes, hh, pair, scale) for hh in range(2)], axis=0)


def _stack_rows(x, lanes, pair):
    if not pair:
        return _head_rows(x, lanes, 0, pair)
    return jnp.concatenate([_head_rows(x, lanes, hh, pair) for hh in range(2)], axis=0)


def _unstack_heads(x, lanes, pair, tq):
    if not pair:
        return x
    return jnp.where(lanes < 64, x[:tq], x[tq:])


def _scores(mode, qst, k, sscale, band, qi, bias_ref, pair):
    s = lax.dot_general(qst, k, NT, preferred_element_type=F32)
    if sscale != 1.0:
        s = s * sscale
    if mode == "dil":
        s = s + jnp.concatenate([band, band], axis=0)
    elif mode == "na":
        off = qi - jnp.clip(qi - NA_ROWS // 2, 0, SEQ // GRID_W - NA_ROWS)
        s = s + jnp.concatenate([bias_ref[0, off], bias_ref[1, off]], axis=0)
    return s


def _attn_cfg(mode, d):
    if mode == "dil":
        mlen = SEQ // d
        return dict(pair=True, tq=128, tk=min(256, mlen), mlen=mlen, lk=SEQ, scale=HEAD_DIM ** -0.5, units=4,
                    nsub=ATTN_SUBTILES)
    if mode == "na":
        return dict(pair=True, tq=GRID_W, tk=NA_ROWS * GRID_W, mlen=SEQ, lk=SEQ, scale=HEAD_DIM ** -0.5, units=4,
                    nsub=2 * ATTN_SUBTILES)
    return dict(pair=False, tq=128, tk=MEM_LEN, mlen=SEQ, lk=MEM_LEN, scale=128 ** -0.5, units=4,
                nsub=ATTN_SUBTILES)


ATTN_SUBTILES = 16


def _attn_fwd(name, mode, q_arr, k_arr, v_arr, qcol, kcol, vcol, d=1, bias=None):
    cfg = _attn_cfg(mode, d)
    pair, tq, tk, mlen, lk, scale = cfg["pair"], cfg["tq"], cfg["tk"], cfg["mlen"], cfg["lk"], cfg["scale"]
    qscale, sscale = (scale, 1.0) if pair else (1.0, scale)
    nsub = cfg["nsub"]
    rows = nsub * tq

    def body(*refs):
        if mode == "na":
            q_ref, k_ref, v_ref, bias_ref, o_ref, l_ref = refs
        else:
            q_ref, k_ref, v_ref, o_ref, l_ref = refs
            bias_ref = None
        lanes = _iota((tq, 128), 1)
        qis = [pl.program_id(1) * nsub + sub for sub in range(nsub)]
        kss = [_window(mode, qi, tq, mlen, tk) for qi in qis]
        vs = [v_ref[pl.ds(ks, tk), :] for ks in kss]
        bands = [_band_mask(qi, tq, tk, ks) if mode == "dil" else None for qi, ks in zip(qis, kss)]
        ss = []
        for sub in range(nsub):
            qst = _stack_heads(q_ref[sub * tq:(sub + 1) * tq, :], lanes, pair, qscale)
            k = k_ref[pl.ds(kss[sub], tk), :]
            ss.append(_scores(mode, qst, k, sscale, bands[sub], qis[sub], bias_ref, pair))
        ms = [jnp.max(s_, axis=1, keepdims=True) for s_ in ss]
        ps = [jnp.exp(s_ - m) for s_, m in zip(ss, ms)]
        ls = [jnp.sum(p, axis=1, keepdims=True) for p in ps]
        os_ = [jnp.dot(p.astype(BF16), v, preferred_element_type=F32) for p, v in zip(ps, vs)]
        for sub in range(nsub):
            out = _unstack_heads(os_[sub] / ls[sub], lanes, pair, tq)
            lse = ms[sub] + jnp.log(ls[sub])
            lse = _unstack_heads(jnp.broadcast_to(lse, (lse.shape[0], 128)), lanes, pair, tq)
            dst = _folded_rows(qis[sub] * tq, tq, d) if mode == "dil" else slice(sub * tq, (sub + 1) * tq)
            o_ref[dst, :] = out
            l_ref[dst, :] = lse

    in_specs = [pl.BlockSpec((rows, 128), lambda u, i: (i, qcol + u)),
                pl.BlockSpec((lk, 128), lambda u, i: (0, kcol + u)),
                pl.BlockSpec((lk, 128), lambda u, i: (0, vcol + u))]
    args = [q_arr, k_arr, v_arr]
    if mode == "na":
        in_specs.append(pl.BlockSpec((2, NA_ROWS, GRID_W, NA_ROWS * GRID_W), lambda u, i: (u, 0, 0, 0)))
        args.append(bias)
    if mode == "dil":
        out_spec = pl.BlockSpec((SEQ, 128), lambda u, i: (0, u))
    else:
        out_spec = pl.BlockSpec((rows, 128), lambda u, i: (i, u))
    return pl.pallas_call(
        body, name=name, grid=(cfg["units"], SEQ // rows), in_specs=in_specs, out_specs=[out_spec, out_spec],
        out_shape=[jax.ShapeDtypeStruct((SEQ, 512), F32), jax.ShapeDtypeStruct((SEQ, 512), F32)],
        compiler_params=_params(("parallel", "arbitrary")))(*args)


def _attn_bwd(name, mode, q_arr, k_arr, v_arr, qcol, kcol, vcol, do, lse, dp=None, o=None, d=1, bias=None,
              tabs=None, dep=None):
    cfg = _attn_cfg(mode, d)
    pair, tq, tk, mlen, lk, scale = cfg["pair"], cfg["tq"], cfg["tk"], cfg["mlen"], cfg["lk"], cfg["scale"]
    qscale, sscale = (scale, 1.0) if pair else (1.0, scale)
    nsub = cfg["nsub"]
    rows = nsub * tq
    nq = SEQ // rows
    kv_dtype = F32 if mode == "mem" else BF16
    dep_specs, dep_args = _dep_operand(dep)
    mode_inputs = {"dil": 3, "na": 2, "mem": 1}[mode]

    def body(*refs):
        refs = list(refs)
        q_ref, k_ref, v_ref, do_ref, l_ref = refs[:5]
        rest = refs[5:5 + mode_inputs] + refs[5 + mode_inputs + len(dep_args):]
        bias_ref = tq_ref = tk_ref = db_ref = None
        if mode == "dil":
            dp_ref, tq_ref, tk_ref, dq_ref, dk_ref, dv_ref, dk_acc, dv_acc = rest
        elif mode == "na":
            o_ref, bias_ref, dq_ref, dk_ref, dv_ref, db_ref, dk_acc, dv_acc = rest
        else:
            o_ref, dq_ref, dk_ref, dv_ref, dk_acc, dv_acc = rest
        step = pl.program_id(1)

        @pl.when(step == 0)
        def _():
            dk_acc[...] = jnp.zeros((lk, 128), F32)
            dv_acc[...] = jnp.zeros((lk, 128), F32)
            if mode == "na":
                db_ref[...] = jnp.zeros(db_ref.shape, F32)

        lanes = _iota((tq, 128), 1)
        qis = [step * nsub + sub for sub in range(nsub)]
        sls = [slice(sub * tq, (sub + 1) * tq) for sub in range(nsub)]
        kss = [_window(mode, qi, tq, mlen, tk) for qi in qis]
        ks_ = [k_ref[pl.ds(ks, tk), :] for ks in kss]
        vs = [v_ref[pl.ds(ks, tk), :] for ks in kss]
        qsts, dosts, lses, dphs = [], [], [], []
        for sub in range(nsub):
            if mode == "dil":
                src = _folded_rows(qis[sub] * tq, tq, d)
                dov = do_ref[src, :].astype(BF16)
                lsev = l_ref[src, :]
                dphs.append(_stack_rows(dp_ref[src, :], lanes, pair))
            else:
                dov = do_ref[sls[sub], :]
                lsev = l_ref[sls[sub], :]
                dpv = dov.astype(F32) * o_ref[sls[sub], :]
                if pair:
                    dphs.append(jnp.concatenate(
                        [jnp.sum(jnp.where(_head_lanes(lanes, hh), dpv, 0.0), axis=1, keepdims=True)
                         for hh in range(2)], axis=0))
                else:
                    dphs.append(jnp.sum(dpv, axis=1, keepdims=True))
            qsts.append(_stack_heads(q_ref[sls[sub], :], lanes, pair, qscale))
            dosts.append(_stack_heads(dov, lanes, pair))
            lses.append(_stack_rows(lsev, lanes, pair))
        bands = [_band_mask(qi, tq, tk, ks) if mode == "dil" else None for qi, ks in zip(qis, kss)]
        ss = [_scores(mode, qsts[sub], ks_[sub], sscale, bands[sub], qis[sub], bias_ref, pair) for sub in range(nsub)]
        dpms = [lax.dot_general(dosts[sub], vs[sub], NT, preferred_element_type=F32) for sub in range(nsub)]
        ps = [jnp.exp(s_ - lse) for s_, lse in zip(ss, lses)]
        dss = [p * (dpm - dph) for p, dpm, dph in zip(ps, dpms, dphs)]
        if mode == "na":
            for sub, ds in enumerate(dss):
                off = qis[sub] - jnp.clip(qis[sub] - NA_ROWS // 2, 0, SEQ // GRID_W - NA_ROWS)
                db_ref[0, off] += ds[:tq]
                db_ref[1, off] += ds[tq:]
        dsbs = [ds.astype(BF16) for ds in dss]
        dvs = [lax.dot_general(p.astype(BF16), dosts[sub], TN, preferred_element_type=F32)
               for sub, p in enumerate(ps)]
        dqs = [jnp.dot(dsb, ks_[sub], preferred_element_type=F32) * scale for sub, dsb in enumerate(dsbs)]
        dks = [lax.dot_general(dsb, qsts[sub], TN, preferred_element_type=F32) for sub, dsb in enumerate(dsbs)]
        for sub in range(nsub):
            sl = sls[sub]
            dq = _unstack_heads(dqs[sub], lanes, pair, tq)
            if mode == "dil":
                dq = _rope_t(dq, tq_ref[0, sl, :], tq_ref[1, sl, :], tq_ref[2, sl, :])
            dq_ref[sl, :] = dq.astype(BF16)
            dk_acc[pl.ds(kss[sub], tk), :] += dks[sub] if pair else dks[sub] * scale
            dv_acc[pl.ds(kss[sub], tk), :] += dvs[sub]

        @pl.when(step == nq - 1)
        def _():
            dkv = dk_acc[...]
            if mode == "dil":
                dkv = _rope_t(dkv, tk_ref[0], tk_ref[1], tk_ref[2])
            dk_ref[...] = dkv.astype(kv_dtype)
            dv_ref[...] = dv_acc[...].astype(kv_dtype)

    q_spec = pl.BlockSpec((rows, 128), lambda u, i: (i, qcol + u))
    row_spec = pl.BlockSpec((rows, 128), lambda u, i: (i, u))
    kv_out = pl.BlockSpec((lk, 128), lambda u, i: (0, u))
    whole = pl.BlockSpec((SEQ, 128), lambda u, i: (0, u))
    nat_spec = whole if mode == "dil" else row_spec
    in_specs = [q_spec,
                pl.BlockSpec((lk, 128), lambda u, i: (0, kcol + u)),
                pl.BlockSpec((lk, 128), lambda u, i: (0, vcol + u)),
                nat_spec, nat_spec]
    args = [q_arr, k_arr, v_arr, do, lse]
    out_specs = [row_spec, kv_out, kv_out]
    out_shape = [jax.ShapeDtypeStruct((SEQ, 512), BF16), jax.ShapeDtypeStruct((lk, 512), kv_dtype),
                 jax.ShapeDtypeStruct((lk, 512), kv_dtype)]
    if mode == "dil":
        in_specs += [whole, pl.BlockSpec((3, rows, 128), lambda u, i: (0, i, 0)),
                     pl.BlockSpec((3, SEQ, 128), lambda u, i: (0, 0, 0))]
        args += [dp, tabs, tabs]
    elif mode == "na":
        b_spec = pl.BlockSpec((2, NA_ROWS, GRID_W, NA_ROWS * GRID_W), lambda u, i: (u, 0, 0, 0))
        in_specs += [row_spec, b_spec]
        args += [o, bias]
        out_specs.append(b_spec)
        out_shape.append(jax.ShapeDtypeStruct((8, NA_ROWS, GRID_W, NA_ROWS * GRID_W), F32))
    else:
        in_specs.append(row_spec)
        args.append(o)
    return pl.pallas_call(
        body, name=name, grid=(cfg["units"], nq), in_specs=in_specs + dep_specs, out_specs=out_specs,
        out_shape=out_shape, scratch_shapes=[pltpu.VMEM((lk, 128), F32), pltpu.VMEM((lk, 128), F32)],
        compiler_params=_params(("parallel", "arbitrary")))(*args, *dep_args)


def _na_geometry():
    qc = _iota((GRID_W, 128), 0)
    lane = _iota((GRID_W, 128), 1)
    kc = lane & 63
    c_start = jnp.clip(qc - 8, 0, GRID_W - 16)
    valid = jnp.logical_and(kc >= c_start, kc < c_start + 16)
    return lane, valid


def _na_bias(rpb_rows, dep=None):
    dep_specs, dep_args = _dep_operand(dep)

    def body(r_ref, *rest):
        o_ref, t_ref = rest[-2:]
        lane, valid = _na_geometry()
        for dd in range(14):
            row_a = jnp.broadcast_to(r_ref[dd:dd + 1, :], (GRID_W, 128))
            row_b = jnp.broadcast_to(r_ref[dd + 1:dd + 2, :], (GRID_W, 128))
            both = jnp.where(lane < 64, row_a, pltpu.roll(row_b, 64, 1))
            t = pltpu.roll(both, 128 - 15, 1, stride=1, stride_axis=0)
            t_ref[dd] = jnp.where(valid, t, NEG)
        for off in range(NA_ROWS):
            for p in range(4):
                o_ref[off, :, p * 128:(p + 1) * 128] = t_ref[2 * p - off + 7]

    return pl.pallas_call(
        body, name="na_bias", grid=(8,),
        in_specs=[pl.BlockSpec((None, 16, 128), lambda h: (h, 0, 0))] + dep_specs,
        out_specs=pl.BlockSpec((None, NA_ROWS, GRID_W, NA_ROWS * GRID_W), lambda h: (h, 0, 0, 0)),
        out_shape=jax.ShapeDtypeStruct((8, NA_ROWS, GRID_W, NA_ROWS * GRID_W), F32),
        scratch_shapes=[pltpu.VMEM((14, GRID_W, 128), F32)],
        compiler_params=_params(("parallel",)))(rpb_rows, *dep_args)


def _na_bias_bwd(dbias, dep=None):
    dep_specs, dep_args = _dep_operand(dep)

    def body(d_ref, *rest):
        o_ref = rest[-1]
        lane, valid = _na_geometry()
        reverse = (_iota((GRID_W, GRID_W), 0) + _iota((GRID_W, GRID_W), 1) == GRID_W - 1).astype(F32)
        o_ref[...] = jnp.zeros((16, 128), F32)
        for dd in range(14):
            t = jnp.zeros((GRID_W, 128), F32)
            for off in range(NA_ROWS):
                for p in range(4):
                    if 2 * p - off + 7 == dd:
                        t = t + d_ref[off, :, p * 128:(p + 1) * 128]
            t = jnp.dot(reverse, jnp.where(valid, t, 0.0), precision=lax.Precision.HIGHEST,
                        preferred_element_type=F32)
            t = pltpu.roll(t, 128 - (GRID_W - 16), 1, stride=1, stride_axis=0)
            o_ref[dd:dd + 1, :] = jnp.sum(t, axis=0, keepdims=True)

    return pl.pallas_call(
        body, name="na_bias_bwd", grid=(8,),
        in_specs=[pl.BlockSpec((None, NA_ROWS, GRID_W, NA_ROWS * GRID_W), lambda h: (h, 0, 0, 0))] + dep_specs,
        out_specs=pl.BlockSpec((None, 16, 128), lambda h: (h, 0, 0)),
        out_shape=jax.ShapeDtypeStruct((8, 16, 128), F32),
        compiler_params=_params(("parallel",)))(dbias, *dep_args)


GATE_ROWS = 128


def _group_weights(l0, l1, l2):
    m = jnp.maximum(jnp.maximum(l0, l1), l2)
    e0, e1, e2 = jnp.exp(l0 - m), jnp.exp(l1 - m), jnp.exp(l2 - m)
    inv = 1.0 / (e0 + e1 + e2)
    return e0 * inv, e1 * inv, e2 * inv


def _gate_block(o_grp, l_grp, out_b, out_c, parts, x, target, merge_bias, wts, w_out, gain, head_sum):
    rows = GATE_ROWS
    r512 = pl.BlockSpec((rows, 512), lambda i: (i, 0))
    r1024 = pl.BlockSpec((rows, D_MODEL), lambda i: (i, 0))
    silu_cols = [pl.BlockSpec((rows, 512), functools.partial(lambda b, i: (i, b), 13 + b)) for b in range(3)]
    logit_cols = [pl.BlockSpec((rows, D_MODEL), functools.partial(lambda b, i: (i, b), 8 + b)) for b in range(3)]

    def body(o0, o1, o2, l0, l1, l2, ob, oc, ga, gb, gc, la, lb, lc, x_ref, t_ref, mb, wa, wb, wc, wo_ref, gn_ref,
             hs_ref, dout_ref, dla, dlb, dlc, dga, dgb, dgc, do0, do1, do2, dp0, dp1, dp2, dob, doc, err_ref, gg_ref,
             gmb, gwa, gwb, gwc, gwo, acc_a, acc_b, acc_c, acc_o):
        step = pl.program_id(0)
        ws = _group_weights(l0[...], l1[...], l2[...])
        out_a = ws[0] * o0[...] + ws[1] * o1[...] + ws[2] * o2[...]
        branches = ((out_a, ga, la, wa, acc_a, dla, dga), (ob[...], gb, lb, wb, acc_b, dlb, dgb),
                    (oc[...], gc, lc, wc, acc_c, dlc, dgc))

        @pl.when(step == 0)
        def _():
            for acc in (acc_a, acc_b, acc_c, acc_o):
                acc[...] = jnp.zeros(acc.shape, F32)
            err_ref[...] = jnp.zeros((1, D_MODEL), F32)
            gg_ref[...] = jnp.zeros((1, D_MODEL), F32)
            gmb[...] = jnp.zeros((3, D_MODEL), F32)

        y = jnp.zeros((rows, D_MODEL), F32)
        zs, gates, silus, dsilus, us = [], [], [], [], []
        for b, (ov, g_ref, l_ref, w_ref, _, _, _) in enumerate(branches):
            g = g_ref[...].astype(F32)
            sg = _sigmoid(g)
            silus.append(g * sg)
            dsilus.append(sg * (1.0 + g * (1.0 - sg)))
            us.append((ov * silus[b]).astype(BF16))
            zs.append(lax.dot_general(us[b], w_ref[...], NT, preferred_element_type=F32))
            gates.append(_sigmoid(l_ref[...].astype(F32) + mb[b:b + 1, :]))
            y = y + gates[b] * zs[b]
        yb = y.astype(BF16)
        y2 = jnp.dot(yb, wo_ref[...], preferred_element_type=F32)
        rstd = lax.rsqrt(jnp.mean(y2 * y2, axis=1, keepdims=True) + EPS)
        yn = y2 * rstd
        gv = gn_ref[...]
        err = x_ref[...] + yn * gv - t_ref[...]
        dout = err * (1.0 / D_MODEL)
        dout_ref[...] = dout
        dn = dout * gv
        dy2 = (rstd * (dn - yn * jnp.mean(dn * yn, axis=1, keepdims=True))).astype(BF16)
        acc_o[...] += lax.dot_general(yb, dy2, TN, preferred_element_type=F32)
        err_ref[...] += jnp.sum(err * err, axis=0, keepdims=True)
        gg_ref[...] += jnp.sum(dout * yn, axis=0, keepdims=True)
        dy = lax.dot_general(dy2, wo_ref[...], NT, preferred_element_type=F32)
        dos = []
        for b, (ov, _, _, w_ref, acc, dl_ref, dg_ref) in enumerate(branches):
            dl = dy * zs[b] * gates[b] * (1.0 - gates[b])
            dl_ref[...] = dl.astype(BF16)
            gmb[b:b + 1, :] += jnp.sum(dl, axis=0, keepdims=True)
            dz = (dy * gates[b]).astype(BF16)
            acc[...] += lax.dot_general(dz, us[b], TN, preferred_element_type=F32)
            du = jnp.dot(dz, w_ref[...], preferred_element_type=F32)
            dos.append(du * silus[b])
            dg_ref[...] = (du * ov * dsilus[b]).astype(BF16)
        dob[...] = dos[1].astype(BF16)
        doc[...] = dos[2].astype(BF16)
        row_term = jnp.dot(dos[0] * out_a, hs_ref[...], precision=lax.Precision.HIGHEST, preferred_element_type=F32)
        for wg, do_ref, dp_ref in zip(ws, (do0, do1, do2), (dp0, dp1, dp2)):
            do_ref[...] = wg * dos[0]
            dp_ref[...] = wg * row_term

        @pl.when(step == SEQ // rows - 1)
        def _():
            for acc, out in ((acc_a, gwa), (acc_b, gwb), (acc_c, gwc), (acc_o, gwo)):
                out[...] = acc[...].astype(BF16)

    full = lambda shape: pl.BlockSpec(shape, lambda i: (0,) * len(shape))
    vec = pl.BlockSpec((1, D_MODEL), lambda i: (0, 0))
    acc3 = pl.BlockSpec((3, D_MODEL), lambda i: (0, 0))
    in_specs = ([r512] * 8 + silu_cols + logit_cols + [r1024, r1024, full((3, D_MODEL))]
                + [full((D_MODEL, 512))] * 3 + [full((D_MODEL, D_MODEL)), vec, full((512, 512))])
    out_specs = ([r1024] + [r1024] * 3 + [r512] * 3 + [r512] * 6 + [r512] * 2 + [vec, vec, acc3]
                 + [full((D_MODEL, 512))] * 3 + [full((D_MODEL, D_MODEL))])
    bf, f32 = BF16, F32
    sds = jax.ShapeDtypeStruct
    out_shape = ([sds((SEQ, D_MODEL), f32)] + [sds((SEQ, D_MODEL), bf)] * 3 + [sds((SEQ, 512), bf)] * 3
                 + [sds((SEQ, 512), f32)] * 6 + [sds((SEQ, 512), bf)] * 2 + [sds((1, D_MODEL), f32)] * 2
                 + [sds((3, D_MODEL), f32)] + [sds((D_MODEL, 512), bf)] * 3 + [sds((D_MODEL, D_MODEL), bf)])
    res = pl.pallas_call(
        body, name="gate_block", grid=(SEQ // rows,), in_specs=in_specs, out_specs=out_specs, out_shape=out_shape,
        scratch_shapes=[pltpu.VMEM((D_MODEL, 512), F32)] * 3 + [pltpu.VMEM((D_MODEL, D_MODEL), F32)],
        compiler_params=_params(("arbitrary",)))(
            *o_grp, *l_grp, out_b, out_c, parts, parts, parts, parts, parts, parts, x, target, merge_bias, *wts, w_out,
            gain, head_sum)
    return dict(dout=res[0], dlog=res[1:4], dg=res[4:7], do_grp=res[7:10], dp_grp=res[10:13], do_b=res[13],
                do_c=res[14], err_sq=res[15], g_post=res[16], g_mb=res[17], g_wt=res[18:21], g_w_out=res[21])


def _local_step(x, hst, parts, tabs, bias, mem, target, pre_norm, mem_norm, post_norm, wt_in, late_weights,
                reduce_start=None):
    o_grp, l_grp = [], []
    for g, d in enumerate(DILATIONS):
        o, l = _attn_fwd("dil_fwd_%d" % g, "dil", parts, parts, parts, 12 * g, 12 * g + 4, 12 * g + 8, d=d)
        o_grp.append(o)
        l_grp.append(l)
    out_b, lse_b = _attn_fwd("na_fwd", "na", parts, parts, parts, 36, 40, 44, bias=bias)
    merge_bias, w_kv, wt_a, wt_b, wt_c, w_out = late_weights(sum(a[:8, :128] for a in [out_b] + o_grp))
    memn = _rmsnorm_fwd("memnorm", mem, mem_norm, MEM_LEN)
    kv_m = _mm_simple("mem_kv", memn, w_kv, NN, BF16, MEM_LEN, 512, D_MODEL)
    out_c, lse_c = _attn_fwd("mem_fwd", "mem", parts, kv_m, kv_m, 48, 0, 4)

    rr = _iota((512, 512), 0) // HEAD_DIM
    cc = _iota((512, 512), 1) // HEAD_DIM
    head_sum = (rr == cc).astype(F32)
    gb = _gate_block(o_grp, l_grp, out_b, out_c, parts, x, target, merge_bias, (wt_a, wt_b, wt_c), w_out, post_norm,
                     head_sum)
    dout, dlog, dg, g_wt, g_w_out = gb["dout"], gb["dlog"], gb["dg"], gb["g_wt"], gb["g_w_out"]
    do_grp, dp_grp, do_b, do_c, g_post, g_mb = (gb["do_grp"], gb["dp_grp"], gb["do_b"], gb["do_c"], gb["g_post"],
                                                gb["g_mb"])
    loss = 0.5 * jnp.sum(gb["err_sq"]) / D_MODEL

    dq_c, dk_m, dv_m = _attn_bwd("mem_bwd", "mem", parts, kv_m, kv_m, 48, 0, 4, do_c, lse_c, o=out_c)
    dkv = jnp.concatenate([dk_m, dv_m], axis=1).astype(BF16)
    g_w_kv = _mm_simple("mem_kv_dw", memn, dkv, TN, BF16, D_MODEL, 512, MEM_LEN)
    dmemn = _mm_simple("mem_kv_dx", dkv, w_kv, NT, F32, MEM_LEN, 512, D_MODEL)
    grads = dict(w_kv=g_w_kv, wt_a=g_wt[0], wt_b=g_wt[1], wt_c=g_wt[2], w_out=g_w_out, merge_bias=g_mb,
                 post_norm=g_post)
    dep = reduce_start("rest_sibling", grads) if reduce_start is not None else None

    dq_b, dk_b, dv_b, dbias = _attn_bwd("na_bwd", "na", parts, parts, parts, 36, 40, 44, do_b, lse_b, o=out_b,
                                        bias=bias, dep=dep)
    dqkv = []
    for g, d in enumerate(DILATIONS):
        dq, dk, dv = _attn_bwd("dil_bwd_%d" % g, "dil", parts, parts, parts, 12 * g, 12 * g + 4, 12 * g + 8,
                               do_grp[g], l_grp[g], dp=dp_grp[g], d=d, tabs=tabs[g])
        dqkv += [dq, dk, dv]
    if reduce_start is not None:
        dep = reduce_start("rest_chips", grads, sum(a[:8, :128] for a in (dqkv[0], dqkv[3], dqkv[6], dq_b)))
    dparts = dqkv + [dq_b, dk_b, dv_b, dq_c] + list(dg) + list(dlog)
    grads["wt_in"] = _in_proj_dw(dparts, hst, dep)
    dep = reduce_start("w_in", grads) if reduce_start is not None else None
    dh = _in_proj_dh(dparts, wt_in, dep)
    if reduce_start is not None:
        dep = reduce_start("w_in_second", grads, dh)
    grad_x, grads["pre_norm"] = _prenorm_bwd(x, pre_norm, dh, dout)
    g_rpb_t = _na_bias_bwd(dbias, dep)
    grads["na_rpb"] = g_rpb_t[:, :15, :31] + jnp.pad(g_rpb_t[:, :14, 64:95], ((0, 0), (1, 0), (0, 0)))
    grads["mem_norm"] = _memnorm_bwd(mem, dmemn, dep)
    return loss, grad_x, grads


ANY = pl.BlockSpec(memory_space=pl.ANY)


def _place():
    return lax.axis_index("x"), lax.axis_index("y"), lax.axis_index("c")


HBM = pl.BlockSpec(memory_space=pltpu.HBM)
SEM = pl.BlockSpec(memory_space=pltpu.SEMAPHORE)
DATAFLOW = pltpu.SideEffectType.DATAFLOW_SIDE_EFFECTING


def _split_copies(kind, srcs, lands, send_sems, recv_sems):
    nt = len(srcs)
    x, y, c = _place()
    copies = []
    if kind == "sibling":
        for q in range(4):
            for t in range(nt):
                k = q * nt + t
                copies.append(pltpu.make_async_remote_copy(
                    src_ref=srcs[t].at[2 * q + 1 - c], dst_ref=lands[t].at[q], send_sem=send_sems.at[k],
                    recv_sem=recv_sems.at[k], device_id=(x, y, 1 - c), device_id_type=MESH_ID))
    elif kind in ("rs_a", "rs_b"):
        half = lands[0].shape[1]
        xn, yn = (1 - x, y, c), (x, 1 - y, c)
        q_xn, q_yn, q_dg = 2 * (1 - x) + y, 2 * x + 1 - y, 2 * (1 - x) + 1 - y
        if kind == "rs_a":
            plan = [(srcs[0].at[q_yn].at[pl.ds(0, half)], 0, yn), (srcs[0].at[q_dg].at[pl.ds(0, half)], 1, yn),
                    (srcs[0].at[q_xn].at[pl.ds(half, half)], 2, xn), (srcs[0].at[q_dg].at[pl.ds(half, half)], 3, xn)]
        else:
            plan = [(srcs[0].at[0], 0, xn), (srcs[0].at[1], 1, yn)]
        for k, (src, slot, to) in enumerate(plan):
            copies.append(pltpu.make_async_remote_copy(
                src_ref=src, dst_ref=lands[0].at[slot], send_sem=send_sems.at[k], recv_sem=recv_sems.at[k],
                device_id=to, device_id_type=MESH_ID))
    elif kind == "gather":
        me = 4 * x + 2 * y + c
        for mask in range(1, 8):
            fx, fy, fc = (mask >> 2) & 1, (mask >> 1) & 1, mask & 1
            to = (1 - x if fx else x, 1 - y if fy else y, 1 - c if fc else c)
            for t in range(nt):
                k = (mask - 1) * nt + t
                copies.append(pltpu.make_async_remote_copy(
                    src_ref=srcs[t], dst_ref=lands[t].at[me], send_sem=send_sems.at[k], recv_sem=recv_sems.at[k],
                    device_id=to, device_id_type=MESH_ID))
    else:
        for s, (tx, ty) in enumerate([(1 - x, y), (x, 1 - y), (1 - x, 1 - y)]):
            for t in range(nt):
                k = s * nt + t
                copies.append(pltpu.make_async_remote_copy(
                    src_ref=srcs[t].at[2 * tx + ty], dst_ref=lands[t].at[s], send_sem=send_sems.at[k],
                    recv_sem=recv_sems.at[k], device_id=(tx, ty, c), device_id_type=MESH_ID))
    return copies


def _split_count(kind, nt):
    return {"gather": 7, "chips": 3, "sibling": 4, "rs_a": 4, "rs_b": 2}[kind] * nt


def _exchange_start(name, kind, srcs, land_shapes, after=None):
    nt = len(srcs)
    n = _split_count(kind, nt)
    dep_specs, dep_args = _dep_operand(after)
    nd = len(dep_args)

    def body(*refs):
        src_refs, land_refs = refs[:nt], refs[nt:2 * nt]
        send_sems, recv_sems = refs[2 * nt + nd], refs[2 * nt + nd + 1]
        token = refs[-1]
        for cp in _split_copies(kind, src_refs, land_refs, send_sems, recv_sems):
            cp.start()
        token[...] = jnp.zeros_like(token)

    lands = [pltpu.with_memory_space_constraint(lax.empty(s.shape, s.dtype), pltpu.HBM) for s in land_shapes]
    res = pl.pallas_call(
        body, name=name,
        out_shape=(pltpu.SemaphoreType.DMA((n,)), pltpu.SemaphoreType.DMA((n,)),
                   *[pltpu.HBM(s.shape, s.dtype) for s in srcs], *[pltpu.HBM(s.shape, s.dtype) for s in land_shapes],
                   jax.ShapeDtypeStruct((8, 128), F32)),
        in_specs=[HBM] * (2 * nt) + dep_specs,
        out_specs=(SEM, SEM, *([HBM] * (2 * nt)), pl.BlockSpec(memory_space=pltpu.VMEM)),
        input_output_aliases={i: 2 + i for i in range(2 * nt)},
        compiler_params=pltpu.CompilerParams(has_side_effects=DATAFLOW))(
            *[pltpu.with_memory_space_constraint(s, pltpu.HBM) for s in srcs], *lands, *dep_args)
    return res[0], res[1], list(res[2:2 + nt]), list(res[2 + nt:2 + 2 * nt]), res[-1]


def _exchange_wait(name, kind, send_sems, recv_sems, srcs, lands, after):
    nt = len(srcs)

    def body(*refs):
        src_refs, land_refs = refs[:nt], refs[nt:2 * nt]
        s_sems, r_sems = refs[2 * nt], refs[2 * nt + 1]
        for cp in _split_copies(kind, src_refs, land_refs, s_sems, r_sems):
            cp.wait_send()
            cp.wait_recv()

    res = pl.pallas_call(
        body, name=name,
        out_shape=tuple(pltpu.HBM(s.shape, s.dtype) for s in list(srcs) + list(lands)),
        in_specs=[HBM] * (2 * nt) + [SEM, SEM, pl.BlockSpec(memory_space=pl.ANY)],
        out_specs=tuple([HBM] * (2 * nt)),
        input_output_aliases={i: i for i in range(2 * nt)},
        compiler_params=pltpu.CompilerParams(has_side_effects=DATAFLOW))(
            *srcs, *lands, send_sems, recv_sems, after)
    return list(res[:nt]), list(res[nt:])


AG_GROUPS = ((0, 3), (3, 4), (7, 2))


def _ag_phase(name, own, land, sems, waits, starts, after=None):
    r = own.shape[0]
    half = r // 2
    ns = len(sems)
    dep_specs, dep_args = _dep_operand(after)
    nd = len(dep_args)
    new_group = None
    if starts:
        (new_group,) = [g for g, (first, n) in enumerate(AG_GROUPS) if first == starts[0]]
        assert list(starts) == list(range(AG_GROUPS[new_group][0], sum(AG_GROUPS[new_group])))

    def body(*refs):
        own_ref, land_ref = refs[0], refs[1]
        sem_refs = list(refs[2:2 + 2 * ns])
        outs = refs[2 + 2 * ns + nd:]
        if starts:
            sem_refs += [outs[0], outs[1]]
        x, y, c = _place()
        me, sib = (x, y, c), (x, y, 1 - c)
        xn, yn, dg = (1 - x, y, c), (x, 1 - y, c), (1 - x, 1 - y, c)

        def other(dev):
            return (dev[0], dev[1], 1 - dev[2])

        def rows(dev, part):
            blk = land_ref.at[4 * dev[0] + 2 * dev[1] + dev[2]]
            return blk if part is None else blk.at[pl.ds(part * half, half)]

        def sem_of(k):
            (g,) = [g for g, (first, n) in enumerate(AG_GROUPS) if first <= k < first + n]
            return sem_refs[2 * g].at[k - AG_GROUPS[g][0]], sem_refs[2 * g + 1].at[k - AG_GROUPS[g][0]]

        sent = {0: (me, None, sib), 1: (me, None, xn), 2: (me, None, yn), 3: (xn, 0, yn), 4: (yn, 1, xn),
                5: (xn, None, sib), 6: (yn, None, sib), 7: (dg, 0, sib), 8: (dg, 1, sib)}
        landed = {0: (sib, None), 1: (xn, None), 2: (yn, None), 3: (dg, 0), 4: (dg, 1), 5: (other(xn), None),
                  6: (other(yn), None), 7: (other(dg), 0), 8: (other(dg), 1)}

        def copy(k, receiving):
            send_sem, recv_sem = sem_of(k)
            dev, part, to = (*landed[k], me) if receiving else sent[k]
            src = own_ref if (dev is me and not receiving) else rows(dev, part)
            return pltpu.make_async_remote_copy(src_ref=src, dst_ref=rows(dev, part), send_sem=send_sem,
                                                recv_sem=recv_sem, device_id=to, device_id_type=MESH_ID)

        for kind, k in waits:
            if kind == "recv":
                copy(k, True).wait_recv()
            else:
                copy(k, False).wait_send()
        for k in starts:
            copy(k, False).start()
        if starts:
            outs[-1][...] = jnp.zeros_like(outs[-1])

    n_new = AG_GROUPS[new_group][1] if starts else 0
    sem_out = (pltpu.SemaphoreType.DMA((n_new,)), pltpu.SemaphoreType.DMA((n_new,))) if starts else ()
    token_out = (jax.ShapeDtypeStruct((8, 128), F32),) if starts else ()
    res = pl.pallas_call(
        body, name=name,
        out_shape=(*sem_out, pltpu.HBM(own.shape, own.dtype), pltpu.HBM(land.shape, land.dtype), *token_out),
        in_specs=[HBM, HBM] + [SEM] * (2 * ns) + dep_specs,
        out_specs=(*([SEM] * len(sem_out)), HBM, HBM, *([pl.BlockSpec(memory_space=pltpu.VMEM)] * len(token_out))),
        input_output_aliases={0: len(sem_out), 1: len(sem_out) + 1},
        compiler_params=pltpu.CompilerParams(has_side_effects=DATAFLOW))(
            own, land, *[a for pair in sems for a in pair], *dep_args)
    if starts:
        return (res[0], res[1]), res[2], res[3], res[4]
    return None, res[0], res[1], None


def _add_sibling(name, term, recv, rows):
    _, r, w = term.shape
    cidx = lax.axis_index("c").astype(jnp.int32).reshape(1)
    like_term = recv.shape[0] == N_DEV

    def body(c_ref, a_ref, b_ref, o_ref):
        o_ref[...] = (a_ref[...].astype(F32) + b_ref[...].astype(F32)).astype(o_ref.dtype)

    grid_spec = pltpu.PrefetchScalarGridSpec(
        num_scalar_prefetch=1, grid=(4, r // rows),
        in_specs=[pl.BlockSpec((None, rows, w), lambda q, i, c_ref: (2 * q + c_ref[0], i, 0)),
                  pl.BlockSpec((None, rows, w), lambda q, i, c_ref: (2 * q + c_ref[0] if like_term else q, i, 0))],
        out_specs=pl.BlockSpec((None, rows, w), lambda q, i, c_ref: (q, i, 0)))
    return pl.pallas_call(
        body, name=name, grid_spec=grid_spec, out_shape=jax.ShapeDtypeStruct((4, r, w), term.dtype),
        compiler_params=_params(("parallel", "parallel")))(cidx, term, recv)


def _add_sibling_small(name, terms, recvs):
    nt = len(terms)

    def body(*refs):
        c = lax.axis_index("c")
        for t_ref, r_ref, o_ref in zip(refs[:nt], refs[nt:2 * nt], refs[2 * nt:]):
            for q in range(4):
                o_ref[q] = (t_ref[2 * q + c].astype(F32) + r_ref[q].astype(F32)).astype(o_ref.dtype)

    return pl.pallas_call(
        body, name=name, out_shape=[jax.ShapeDtypeStruct((4,) + t.shape[1:], t.dtype) for t in terms],
        compiler_params=_params())(*terms, *recvs)


def _reduce_scatter_start(tag, terms, recv1):
    sums = _add_sibling_small("add_sibling_" + tag, terms, recv1)
    lands = [jax.ShapeDtypeStruct((3,) + s.shape[1:], s.dtype) for s in sums]
    send_sems, recv_sems, sums, lands, token = _exchange_start("exchange_chips_start_" + tag, "chips", sums, lands)
    return (tag, send_sems, recv_sems, sums, lands), token


def _reduce_scatter_wait(state, after):
    tag, send_sems, recv_sems, sums, lands = state
    return _exchange_wait("exchange_chips_wait_" + tag, "chips", send_sems, recv_sems, sums, lands, after)


def _adamw(name, w, g, m, v, dep=None):
    dep_specs, dep_args = _dep_operand(dep)

    def body(w_ref, g_ref, m_ref, v_ref, *rest):
        d_ref, nm_ref, nv_ref = rest[-3:]
        d_ref[...], nm_ref[...], nv_ref[...] = _adam_math(w_ref[...], g_ref[...], m_ref[...], v_ref[...])

    whole = pl.BlockSpec(memory_space=pltpu.VMEM)
    return pl.pallas_call(
        body, name=name, in_specs=[whole] * 4 + dep_specs, out_shape=[jax.ShapeDtypeStruct(w.shape, F32)] * 3,
        compiler_params=_params())(w, g, m, v, *dep_args)


def _adam_math(w, g, m, v):
    nm = ADAM_B1 * m + (1.0 - ADAM_B1) * g
    nv = ADAM_B2 * v + (1.0 - ADAM_B2) * (g * g)
    c1 = 1.0 - ADAM_B1 ** ADAM_STEP
    c2 = 1.0 - ADAM_B2 ** ADAM_STEP
    return -ADAM_LR * ((nm / c1) / (jnp.sqrt(nv / c2) + ADAM_EPS) + ADAM_WD * w), nm, nv


def _presum_halves(sums, landed):
    _, r, w = sums.shape
    rows = r // 2
    x, y = lax.axis_index("x"), lax.axis_index("y")
    dest = jnp.stack([2 * (1 - x) + y, 2 * x + 1 - y]).astype(jnp.int32)

    def body(q_ref, a_ref, b_ref, o_ref):
        o_ref[...] = (a_ref[...].astype(F32) + b_ref[...].astype(F32)).astype(o_ref.dtype)

    grid_spec = pltpu.PrefetchScalarGridSpec(
        num_scalar_prefetch=1, grid=(2,),
        in_specs=[pl.BlockSpec((None, rows, w), lambda h, q_ref: (q_ref[h], h, 0)),
                  pl.BlockSpec((None, rows, w), lambda h, q_ref: (1 + 2 * h, 0, 0))],
        out_specs=pl.BlockSpec((None, rows, w), lambda h, q_ref: (h, 0, 0)))
    return pl.pallas_call(
        body, name="presum_halves", grid_spec=grid_spec, out_shape=jax.ShapeDtypeStruct((2, r // 2, w), sums.dtype),
        compiler_params=_params(("parallel",)))(dest, sums, landed)


def _adamw_halves(name, sums, landed_a, landed_b, w, m, v, rows):
    r, c = w.shape
    half = c // 2
    qidx = (2 * lax.axis_index("x") + lax.axis_index("y")).astype(jnp.int32).reshape(1)

    def body(q_ref, s_ref, a_ref, b_ref, w_ref, m_ref, v_ref, g_ref, d_ref, nm_ref, nv_ref):
        first = (s_ref[:half, :].astype(F32) + a_ref[0].astype(F32)) + b_ref[0].astype(F32)
        second = (s_ref[half:, :].astype(F32) + a_ref[2].astype(F32)) + b_ref[1].astype(F32)
        g = jnp.concatenate([first, second], axis=0).T
        g_ref[...] = g
        d_ref[...], nm_ref[...], nv_ref[...] = _adam_math(w_ref[...], g, m_ref[...], v_ref[...])

    row = pl.BlockSpec((rows, c), lambda i, q_ref: (i, 0))
    grid_spec = pltpu.PrefetchScalarGridSpec(
        num_scalar_prefetch=1, grid=(r // rows,),
        in_specs=[pl.BlockSpec((None, c, rows), lambda i, q_ref: (q_ref[0], 0, i)),
                  pl.BlockSpec((4, half, rows), lambda i, q_ref: (0, 0, i)),
                  pl.BlockSpec((2, half, rows), lambda i, q_ref: (0, 0, i)), row, row, row],
        out_specs=[row] * 4)
    return pl.pallas_call(
        body, name=name, grid_spec=grid_spec, out_shape=[jax.ShapeDtypeStruct((r, c), F32)] * 4,
        compiler_params=_params(("parallel",)))(qidx, sums, landed_a, landed_b, w, m, v)


def _adamw_chips_small(name, items):
    n = len(items)

    def body(*refs):
        q = 2 * lax.axis_index("x") + lax.axis_index("y")
        ins, outs = refs[:5 * n], refs[5 * n:]
        for i, (_, _, w, _, _, transposed) in enumerate(items):
            s_ref, r_ref, w_ref, m_ref, v_ref = ins[5 * i:5 * i + 5]
            g_ref, d_ref, nm_ref, nv_ref = outs[4 * i:4 * i + 4]
            g = (s_ref[q].astype(F32) + r_ref[0].astype(F32)) + (r_ref[1].astype(F32) + r_ref[2].astype(F32))
            g = g.T if transposed else g[:w.shape[0]]
            g_ref[...] = g
            d_ref[...], nm_ref[...], nv_ref[...] = _adam_math(w_ref[...], g, m_ref[...], v_ref[...])

    res = pl.pallas_call(
        body, name=name, out_shape=[jax.ShapeDtypeStruct(it[2].shape, F32) for it in items for _ in range(4)],
        compiler_params=_params())(*[a for it in items for a in it[:5]])
    return [res[4 * i:4 * i + 4] for i in range(n)]


def _sum_devices(gathered):
    def body(g_ref, o_ref):
        acc = g_ref[0]
        for j in range(1, N_DEV):
            acc = acc + g_ref[j]
        o_ref[...] = acc

    return pl.pallas_call(
        body, name="sum_devices", out_shape=jax.ShapeDtypeStruct(gathered.shape[1:], F32),
        compiler_params=_params())(gathered)


def _rows128(a, rows):
    flat = a.reshape(-1)
    return jnp.pad(flat, (0, rows * 128 - flat.shape[0])).reshape(rows, 128)


def kernel(x, mem, pre_norm, w_in, merge_bias, na_rpb, mem_norm, w_mem_kv, w_branch_a, w_branch_b, w_branch_c, w_out, post_norm, loss_target, m_pre_norm, m_w_in, m_merge_bias, m_na_rpb, m_mem_norm, m_w_mem_kv, m_w_branch_a, m_w_branch_b, m_w_branch_c, m_w_out, m_post_norm, v_pre_norm, v_w_in, v_merge_bias, v_na_rpb, v_mem_norm, v_w_mem_kv, v_w_branch_a, v_w_branch_b, v_w_branch_c, v_w_out, v_post_norm):
    wt_in_s = w_in[0].T.astype(BF16)
    rows_s = jnp.concatenate([w_mem_kv[0], w_out[0]], axis=0).astype(BF16)
    cols_s = jnp.concatenate([w_branch_a[0].T, w_branch_b[0].T, w_branch_c[0].T], axis=0).astype(BF16)
    mb_s = jnp.pad(merge_bias[0], ((0, 5), (0, 0)))
    me = 4 * lax.axis_index("x") + 2 * lax.axis_index("y") + lax.axis_index("c")

    chip = 2 * lax.axis_index("x") + lax.axis_index("y")

    def first_block(q):
        return jnp.where(q == 0, 0, jnp.where(q == 1, 6, jnp.where(q == 2, 11, 17)))

    five = jnp.arange(5, dtype=jnp.int32)
    near, far = jnp.where(chip < 2, 5, 16), jnp.where(chip < 2, 16, 5)
    order1 = (first_block(chip) + five).astype(jnp.int32)
    order2 = jnp.concatenate([first_block(chip ^ 1) + five, near[None], first_block(chip ^ 2) + five]).astype(jnp.int32)
    order3 = jnp.concatenate([first_block(chip ^ 3) + five, far[None]]).astype(jnp.int32)
    tabs = _rope_tables()

    def weights_of(land):
        return land.reshape(N_IN, D_MODEL)

    land = pltpu.with_memory_space_constraint(lax.empty((N_DEV,) + wt_in_s.shape, BF16), pltpu.HBM)
    own = pltpu.with_memory_space_constraint(wt_in_s, pltpu.HBM)
    sem_a, own, land, token = _ag_phase("ag_start", own, land, [], [], [0, 1, 2])
    hs, hst = _prenorm_fold(x[0], pre_norm, token)
    _, own, land, _ = _ag_phase("ag_wait0", own, land, [sem_a], [("recv", 0)], [], hs)
    land = lax.dynamic_update_slice(land, own[None], (me, 0, 0))
    parts = _in_proj("in_proj_1", hs, weights_of(land), tabs, order1)
    bias = _na_bias(jnp.pad(na_rpb[0], ((0, 0), (0, 1), (0, 128 - 31))), parts)
    sem_b, own, land, _ = _ag_phase("ag_mid1", own, land, [sem_a], [("recv", 1), ("recv", 2)], [3, 4, 5, 6], bias)
    _, own, land, _ = _ag_phase("ag_wait1", own, land, [sem_a, sem_b], [("recv", 5), ("recv", 6)], [])
    parts = _in_proj("in_proj_2", hs, weights_of(land), tabs, order2, parts)
    sem_c, own, land, _ = _ag_phase("ag_mid2", own, land, [sem_a, sem_b], [("recv", 3), ("recv", 4)], [7, 8], parts)
    _, own, land, _ = _ag_phase("ag_end", own, land, [sem_a, sem_b, sem_c],
                                [("recv", 7), ("recv", 8)] + [("send", k) for k in range(9)], [])
    wt_in = weights_of(land)

    late_own = [rows_s, cols_s, mb_s]
    late_lands = [jax.ShapeDtypeStruct((N_DEV,) + s.shape, s.dtype) for s in late_own]
    l_send, l_recv, late_own, late_lands, late_token = _exchange_start("gather_late_start", "gather", late_own,
                                                                       late_lands, after=wt_in)
    parts = _in_proj("in_proj_3", hs, wt_in, tabs, order3, parts, late_token)

    def late_weights(after):
        own, lands = _exchange_wait("gather_late_wait", "gather", l_send, l_recv, late_own, late_lands, after)
        g_rows, g_cols, g_mb = [lax.dynamic_update_slice(land, o[None], (me, 0, 0)) for land, o in zip(lands, own)]
        return (g_mb[:, :3].transpose(1, 0, 2).reshape(3, D_MODEL),
                g_rows[:, :128].reshape(D_MODEL, D_MODEL), g_cols[:, 0:128].reshape(D_MODEL, 512),
                g_cols[:, 128:256].reshape(D_MODEL, 512), g_cols[:, 256:384].reshape(D_MODEL, 512),
                g_rows[:, 128:].reshape(D_MODEL, D_MODEL))

    rest_state, rest_sibling, w_in_a, w_in_b = [], [], [], []

    def reduce_start(phase, grads, after=None):
        if phase == "rest_sibling":
            gmb_t = jnp.pad(grads["merge_bias"].reshape(3, N_DEV, 128).transpose(1, 0, 2), ((0, 0), (0, 5), (0, 0)))
            terms = [grads["w_kv"].reshape(N_DEV, 128, D_MODEL), grads["w_out"].reshape(N_DEV, 128, D_MODEL),
                     grads["wt_a"].reshape(N_DEV, 128, 512), grads["wt_b"].reshape(N_DEV, 128, 512),
                     grads["wt_c"].reshape(N_DEV, 128, 512), gmb_t]
            lands = [jax.ShapeDtypeStruct((4,) + t.shape[1:], t.dtype) for t in terms]
            started = _exchange_start("exchange_sibling_start_rest", "sibling", terms, lands)
            rest_sibling.extend(started[:4])
            return started[4]
        if phase == "rest_chips":
            s_send, s_recv, terms, lands = rest_sibling
            terms, recv1 = _exchange_wait("exchange_sibling_wait_rest", "sibling", s_send, s_recv, terms, lands, after)
            state, token = _reduce_scatter_start("rest", terms, recv1)
            rest_state.append(state)
            return token
        if phase == "w_in":
            own, sibling = [a.reshape(N_DEV, SHARD_IN, D_MODEL) for a in grads["wt_in"]]
            sums = _add_sibling("add_sibling_w_in", own, sibling, SHARD_IN)
            lands = [jax.ShapeDtypeStruct((4, SHARD_IN // 2, D_MODEL), BF16)]
            w_in_a.extend(_exchange_start("rs_a_start", "rs_a", [sums], lands))
            return w_in_a[4]
        (sums,), (landed_a,) = _exchange_wait("rs_a_wait", "rs_a", w_in_a[0], w_in_a[1], w_in_a[2], w_in_a[3], after)
        lands = [jax.ShapeDtypeStruct((2, SHARD_IN // 2, D_MODEL), BF16)]
        w_in_b.extend(_exchange_start("rs_b_start", "rs_b", [_presum_halves(sums, landed_a)], lands))
        w_in_b.extend([sums, landed_a])
        return w_in_b[4]

    loss_term, grad_x, grads = _local_step(
        x[0], hst, parts, tabs, bias, mem[0], loss_target[0], pre_norm, mem_norm, post_norm, wt_in, late_weights,
        reduce_start=reduce_start)

    small = jnp.concatenate([_rows128(grads["pre_norm"], 8), _rows128(grads["mem_norm"], 8),
                             _rows128(grads["post_norm"], 8), _rows128(grads["na_rpb"], 32),
                             _rows128(loss_term, 8)], axis=0)
    s_send, s_recv, s_own, s_land, s_token = _exchange_start(
        "gather_small_start", "gather", [small], [jax.ShapeDtypeStruct((N_DEV,) + small.shape, F32)])
    grad = {}
    weights = {
        "pre_norm": (pre_norm, m_pre_norm, v_pre_norm), "w_in": (w_in, m_w_in, v_w_in),
        "merge_bias": (merge_bias, m_merge_bias, v_merge_bias), "na_rpb": (na_rpb, m_na_rpb, v_na_rpb),
        "mem_norm": (mem_norm, m_mem_norm, v_mem_norm), "w_mem_kv": (w_mem_kv, m_w_mem_kv, v_w_mem_kv),
        "w_branch_a": (w_branch_a, m_w_branch_a, v_w_branch_a), "w_branch_b": (w_branch_b, m_w_branch_b, v_w_branch_b),
        "w_branch_c": (w_branch_c, m_w_branch_c, v_w_branch_c), "w_out": (w_out, m_w_out, v_w_out),
        "post_norm": (post_norm, m_post_norm, v_post_norm)}
    order = ["pre_norm", "w_in", "merge_bias", "na_rpb", "mem_norm", "w_mem_kv", "w_branch_a", "w_branch_b",
             "w_branch_c", "w_out", "post_norm"]
    delta, new_m, new_v = {}, {}, {}

    def update(n, dep=None):
        w, m, v = weights[n]
        shape = w.shape
        two_d = (-1, shape[-1])
        dl, nm, nv = _adamw("adamw_" + n, w.reshape(two_d), grad[n].reshape(two_d), m.reshape(two_d),
                            v.reshape(two_d), dep)
        delta[n], new_m[n], new_v[n] = dl.reshape(shape), nm.reshape(shape), nv.reshape(shape)
        return dl

    sums, recv2 = _reduce_scatter_wait(rest_state[0], s_token)
    rest = (("w_mem_kv", False), ("w_out", False), ("w_branch_a", True), ("w_branch_b", True), ("w_branch_c", True),
            ("merge_bias", False))
    items = [(sums[i], recv2[i], *[a[0] for a in weights[n]], transposed) for i, (n, transposed) in enumerate(rest)]
    for (n, _), (g, dl, nm, nv) in zip(rest, _adamw_chips_small("adamw_rest", items)):
        grad[n], delta[n], new_m[n], new_v[n] = g[None], dl[None], nm[None], nv[None]
    s_own, s_land = _exchange_wait("gather_small_wait", "gather", s_send, s_recv, s_own, s_land, delta["w_out"])
    total = _sum_devices(lax.dynamic_update_slice(s_land[0], s_own[0][None], (me, 0, 0)))
    loss = total[56, 0]
    grad.update({"pre_norm": total[0:8].reshape(1, D_MODEL), "mem_norm": total[8:16].reshape(1, D_MODEL),
                 "post_norm": total[16:24].reshape(1, D_MODEL),
                 "na_rpb": total[24:56].reshape(-1)[:8 * 15 * 31].reshape(1, 8, 15, 31)})
    dep = None
    for n in ("pre_norm", "na_rpb", "mem_norm", "post_norm"):
        dep = update(n, dep)
    _, (landed_b,) = _exchange_wait("rs_b_wait", "rs_b", w_in_b[0], w_in_b[1], w_in_b[2], w_in_b[3], dep)
    g, dl, nm, nv = _adamw_halves("adamw_w_in", w_in_b[5], w_in_b[6], landed_b, w_in[0], m_w_in[0], v_w_in[0], 256)
    grad["w_in"], delta["w_in"], new_m["w_in"], new_v["w_in"] = g[None], dl[None], nm[None], nv[None]

    return (loss, grad_x[None], *[grad[n] for n in order], *[delta[n] for n in order],
            *[new_m[n] for n in order], *[new_v[n] for n in order])
```

```python
import functools

import numpy as np
import jax
import jax.numpy as jnp
from jax import lax
from jax.experimental import pallas as pl
from jax.experimental.pallas import tpu as pltpu

F32 = jnp.float32
BF16 = jnp.bfloat16

SEQ = 2048
D_MODEL = 1024
N_IN = 11264
N_DEV = 8
SHARD_IN = N_IN // N_DEV
HEAD_DIM = 64
GRID_W = 64
NA_ROWS = 8
MEM_LEN = 256
DILATIONS = (1, 4, 16)
REACH = 64
ROPE_THETA = 500000.0
ROPE_DIM = 16
EPS = 1e-6
NEG = -1e30
ADAM_LR = 0.001
ADAM_B1 = 0.9
ADAM_B2 = 0.999
ADAM_EPS = 1e-08
ADAM_WD = 0.01
ADAM_STEP = 10

VMEM_LIMIT_BYTES = 56 * 1024 * 1024
MESH_ID = pl.DeviceIdType.MESH

NN = (((1,), (0,)), ((), ()))
NT = (((1,), (1,)), ((), ()))
TN = (((0,), (0,)), ((), ()))


def _params(sem=None):
    return pltpu.CompilerParams(dimension_semantics=sem, vmem_limit_bytes=VMEM_LIMIT_BYTES)


def _iota(shape, dim):
    return lax.broadcasted_iota(jnp.int32, shape, dim)


def _sigmoid(x):
    return 1.0 / (1.0 + jnp.exp(-x))


def _rope_tables():
    half = ROPE_DIM // 2
    inv = (ROPE_THETA ** (-np.arange(half, dtype=np.float64) * 2.0 / ROPE_DIM)).astype(np.float32)
    pos = np.arange(SEQ, dtype=np.float32)
    ang = pos[:, None] * inv[None, :]
    cos, sin = np.cos(ang), np.sin(ang)
    zeros = np.zeros_like(cos)
    rest = HEAD_DIM - ROPE_DIM
    c64 = np.concatenate([cos, cos, np.ones((SEQ, rest), np.float32)], axis=1)
    s1 = np.concatenate([zeros, sin, np.zeros((SEQ, rest), np.float32)], axis=1)
    s2 = np.concatenate([-sin, zeros, np.zeros((SEQ, rest), np.float32)], axis=1)

    def fold(t, d):
        return t.reshape(SEQ // d, d, t.shape[1]).transpose(1, 0, 2).reshape(SEQ, t.shape[1])

    tabs = [np.stack([np.tile(fold(t, d), (1, 2)) for t in (c64, s1, s2)], axis=0) for d in DILATIONS]
    return jnp.asarray(np.stack(tabs, axis=0), dtype=F32)


def _rope(a, c, s1, s2):
    return a * c + pltpu.roll(a, 8, 1) * s1 + pltpu.roll(a, 120, 1) * s2


def _rope_t(a, c, s1, s2):
    return a * c + pltpu.roll(a * s1, 120, 1) + pltpu.roll(a * s2, 8, 1)


def _perm_of_block(j):
    return jnp.where(j < 3, 0, jnp.where(j < 6, 1, jnp.where(j < 9, 2, 0)))


def _mm(name, a, b, out_shape, out_dtype, grid, a_spec, b_spec, o_spec, acc_shape, dims, k_axis, nk):
    def body(a_ref, b_ref, o_ref, acc_ref):
        k = pl.program_id(k_axis)

        @pl.when(k == 0)
        def _():
            acc_ref[...] = jnp.zeros(acc_shape, F32)

        acc_ref[...] += lax.dot_general(a_ref[...], b_ref[...], dims, preferred_element_type=F32)

        @pl.when(k == nk - 1)
        def _():
            o_ref[...] = acc_ref[...].astype(out_dtype)

    sem = tuple("arbitrary" if ax == k_axis else "parallel" for ax in range(len(grid)))
    return pl.pallas_call(
        body, name=name, grid=grid, in_specs=[a_spec, b_spec], out_specs=o_spec,
        out_shape=jax.ShapeDtypeStruct(out_shape, out_dtype),
        scratch_shapes=[pltpu.VMEM(acc_shape, F32)], compiler_params=_params(sem))(a, b)


def _mm_simple(name, a, b, dims, out_dtype, tm, tn, tk):
    if dims is NN:
        m, kk = a.shape
        n = b.shape[1]
        a_spec = pl.BlockSpec((tm, tk), lambda i, j, k: (i, k))
        b_spec = pl.BlockSpec((tk, tn), lambda i, j, k: (k, j))
    elif dims is NT:
        m, kk = a.shape
        n = b.shape[0]
        a_spec = pl.BlockSpec((tm, tk), lambda i, j, k: (i, k))
        b_spec = pl.BlockSpec((tn, tk), lambda i, j, k: (j, k))
    else:
        kk, m = a.shape
        n = b.shape[1]
        a_spec = pl.BlockSpec((tk, tm), lambda i, j, k: (k, i))
        b_spec = pl.BlockSpec((tk, tn), lambda i, j, k: (k, j))
    grid = (m // tm, n // tn, kk // tk)
    o_spec = pl.BlockSpec((tm, tn), lambda i, j, k: (i, j))
    return _mm(name, a, b, (m, n), out_dtype, grid, a_spec, b_spec, o_spec, (tm, tn), dims, 2, kk // tk)


def _rmsnorm_fwd(name, x, gain, rows):
    n, d = x.shape

    def body(x_ref, g_ref, o_ref):
        xv = x_ref[...]
        rstd = lax.rsqrt(jnp.mean(xv * xv, axis=1, keepdims=True) + EPS)
        o_ref[...] = (xv * rstd * g_ref[...]).astype(BF16)

    return pl.pallas_call(
        body, name=name, grid=(n // rows,),
        in_specs=[pl.BlockSpec((rows, d), lambda i: (i, 0)), pl.BlockSpec((1, d), lambda i: (0, 0))],
        out_specs=pl.BlockSpec((rows, d), lambda i: (i, 0)),
        out_shape=jax.ShapeDtypeStruct((n, d), BF16), compiler_params=_params(("parallel",)))(x, gain)


def _folded_rows(first, rows, d):
    if d == 1:
        return pl.ds(pl.multiple_of(first, rows), rows)
    mlen = SEQ // d
    return pl.ds((first % mlen) * d + first // mlen, rows, stride=d)


def _prenorm_fold(x, gain, dep=None):
    rows = 128
    nchunk = D_MODEL // 128
    dep_specs, dep_args = _dep_operand(dep)

    def body(*refs):
        x_refs, g_ref, hs_ref, hst_ref = refs[:nchunk], refs[nchunk], refs[-2], refs[-1]
        first = pl.program_id(0) * rows
        for p, d in enumerate(DILATIONS):
            idx = _folded_rows(first, rows, d)
            xv = jnp.concatenate([r[idx, :] for r in x_refs], axis=1)
            rstd = lax.rsqrt(jnp.mean(xv * xv, axis=1, keepdims=True) + EPS)
            h = xv * rstd * g_ref[...]
            hs_ref[p] = h.astype(BF16)
            hst_ref[p] = h.T.astype(BF16)

    x_specs = [pl.BlockSpec((SEQ, 128), functools.partial(lambda c, i: (0, c), c)) for c in range(nchunk)]
    return pl.pallas_call(
        body, name="prenorm", grid=(SEQ // rows,),
        in_specs=x_specs + [pl.BlockSpec((1, D_MODEL), lambda i: (0, 0))] + dep_specs,
        out_specs=[pl.BlockSpec((3, rows, D_MODEL), lambda i: (0, i, 0)),
                   pl.BlockSpec((3, D_MODEL, rows), lambda i: (0, 0, i))],
        out_shape=[jax.ShapeDtypeStruct((3, SEQ, D_MODEL), BF16), jax.ShapeDtypeStruct((3, D_MODEL, SEQ), BF16)],
        compiler_params=_params(("parallel",)))(*([x] * nchunk), gain, *dep_args)


def _prenorm_bwd(x, gain, dh, dout):
    rows = 512

    def body(x_ref, g_ref, a_ref, do_ref, dx_ref, gg_ref):
        xv = x_ref[...]
        rstd = lax.rsqrt(jnp.mean(xv * xv, axis=1, keepdims=True) + EPS)
        xn = xv * rstd
        dh = jnp.concatenate([a_ref[c] for c in range(D_MODEL // 128)], axis=1)
        gdh = dh * g_ref[...]
        dx_ref[...] = rstd * (gdh - xn * jnp.mean(gdh * xn, axis=1, keepdims=True)) + do_ref[...]

        @pl.when(pl.program_id(0) == 0)
        def _():
            gg_ref[...] = jnp.zeros((1, D_MODEL), F32)

        gg_ref[...] += jnp.sum(dh * xn, axis=0, keepdims=True)

    row = pl.BlockSpec((rows, D_MODEL), lambda i: (i, 0))
    vec = pl.BlockSpec((1, D_MODEL), lambda i: (0, 0))
    return pl.pallas_call(
        body, name="prenorm_bwd", grid=(SEQ // rows,),
        in_specs=[row, vec, pl.BlockSpec((D_MODEL // 128, rows, 128), lambda i: (0, i, 0)), row], out_specs=[row, vec],
        out_shape=[jax.ShapeDtypeStruct((SEQ, D_MODEL), F32), jax.ShapeDtypeStruct((1, D_MODEL), F32)],
        compiler_params=_params(("arbitrary",)))(x, gain, dh, dout)


def _memnorm_bwd(mem, dmemn, dep=None):
    dep_specs, dep_args = _dep_operand(dep)

    def body(m_ref, d_ref, *rest):
        mv = m_ref[...]
        rstd = lax.rsqrt(jnp.mean(mv * mv, axis=1, keepdims=True) + EPS)
        rest[-1][...] = jnp.sum(d_ref[...] * mv * rstd, axis=0, keepdims=True)

    whole = pl.BlockSpec(memory_space=pltpu.VMEM)
    return pl.pallas_call(
        body, name="memnorm_bwd", in_specs=[whole, whole] + dep_specs,
        out_shape=jax.ShapeDtypeStruct((1, D_MODEL), F32), compiler_params=_params())(mem, dmemn, *dep_args)


def _dep_operand(dep):
    return ([], []) if dep is None else ([pl.BlockSpec(memory_space=pl.ANY)], [dep])


def _in_proj(name, hs, wt, tabs, order, prev=None, dep=None):
    tm, tn = 512, 512
    prev_specs, prev_args = ([], []) if prev is None else ([ANY], [prev])
    dep_specs, dep_args = _dep_operand(dep)

    def body(order_ref, h_ref, w_ref, t_ref, *rest):
        o_ref = rest[-1]
        j = order_ref[pl.program_id(0)]
        is_rope = jnp.logical_and(j < 9, j % 3 != 2)
        row_slices = [slice(r * tm, (r + 1) * tm) for r in range(SEQ // tm)]

        def product(rs):
            return lax.dot_general(h_ref[rs, :], w_ref[...], NT, preferred_element_type=F32)

        @pl.when(is_rope)
        def _():
            for rs in row_slices:
                acc = product(rs)
                c, s1, s2 = t_ref[0, rs, :], t_ref[1, rs, :], t_ref[2, rs, :]
                for q in range(tn // 128):
                    a = acc[:, q * 128:(q + 1) * 128]
                    o_ref[rs, q * 128:(q + 1) * 128] = _rope(a, c, s1, s2).astype(BF16)

        @pl.when(jnp.logical_not(is_rope))
        def _():
            for rs in row_slices:
                o_ref[rs, :] = product(rs).astype(BF16)

    grid_spec = pltpu.PrefetchScalarGridSpec(
        num_scalar_prefetch=1, grid=(order.shape[0],),
        in_specs=[pl.BlockSpec((None, SEQ, D_MODEL), lambda t, o: (_perm_of_block(o[t]), 0, 0)),
                  pl.BlockSpec((tn, D_MODEL), lambda t, o: (o[t], 0)),
                  pl.BlockSpec((None, 3, SEQ, 128), lambda t, o: (_perm_of_block(o[t]), 0, 0, 0))] + prev_specs
        + dep_specs,
        out_specs=pl.BlockSpec((SEQ, tn), lambda t, o: (0, o[t])))
    return pl.pallas_call(
        body, name=name, grid_spec=grid_spec, out_shape=jax.ShapeDtypeStruct((SEQ, N_IN), BF16),
        input_output_aliases={} if prev is None else {4: 0},
        compiler_params=_params(("arbitrary",)))(order, hs, wt, tabs, *prev_args, *dep_args)


def _piece_blocks(pieces):
    return [(a, h * 512) for a, p in enumerate(pieces) for h in range(p.shape[1] // 512)]


def _block_fetch(piece_refs, blocks, buf, sem):
    def start(block, slot):
        for b, (a, col) in enumerate(blocks):
            @pl.when(block == b)
            def _():
                pltpu.make_async_copy(piece_refs[a].at[:, pl.ds(col, 512)], buf.at[slot], sem.at[slot]).start()

    def wait(slot):
        pltpu.make_async_copy(piece_refs[0].at[:, pl.ds(0, 512)], buf.at[slot], sem.at[slot]).wait()

    return start, wait


def _in_proj_dw(pieces, hst, dep=None):
    tn = 512
    blocks = _piece_blocks(pieces)
    nblk = len(blocks)
    npc = len(pieces)
    dep_specs, dep_args = _dep_operand(dep)

    def body(h_ref, *rest):
        piece_refs = rest[:npc]
        own_out, mirror, buf, sem, out_buf, send_sems, recv_sem, local_sems = rest[-8:]
        j = pl.program_id(0)
        slot = j % 2
        start, wait = _block_fetch(piece_refs, blocks, buf, sem)
        x, y, c = _place()

        def rows_of(step):
            return pl.ds(pl.multiple_of(step * tn, tn), tn)

        def to_sibling(step, slot_):
            return pltpu.make_async_remote_copy(
                src_ref=out_buf.at[slot_], dst_ref=mirror.at[rows_of(step)],
                send_sem=send_sems.at[slot_], recv_sem=recv_sem, device_id=(x, y, 1 - c), device_id_type=MESH_ID)

        def to_own(step, slot_):
            return pltpu.make_async_copy(out_buf.at[slot_], own_out.at[rows_of(step)], local_sems.at[slot_])

        @pl.when(j == 0)
        def _():
            start(j, slot)

        wait(slot)

        @pl.when(j + 1 < nblk)
        def _():
            start(j + 1, 1 - slot)

        acc = jnp.dot(h_ref[...], buf[slot], preferred_element_type=F32)

        @pl.when(j >= 2)
        def _():
            to_sibling(j - 2, slot).wait_send()
            to_own(j - 2, slot).wait()

        out_buf[slot] = acc.T.astype(BF16)
        to_sibling(j, slot).start()
        to_own(j, slot).start()

        @pl.when(j == nblk - 1)
        def _():
            to_sibling(j - 1, 1 - slot).wait_send()
            to_own(j - 1, 1 - slot).wait()
            to_sibling(j, slot).wait_send()
            to_own(j, slot).wait()
            pltpu.make_async_remote_copy(src_ref=mirror, dst_ref=mirror, send_sem=send_sems.at[0], recv_sem=recv_sem,
                                         device_id=(x, y, 1 - c), device_id_type=MESH_ID).wait_recv()

    return pl.pallas_call(
        body, name="in_proj_dw", grid=(nblk,),
        in_specs=[pl.BlockSpec((None, D_MODEL, SEQ), lambda j: (_perm_of_block(j), 0, 0))] + [ANY] * npc + dep_specs,
        out_specs=[ANY, ANY],
        out_shape=[jax.ShapeDtypeStruct((N_IN, D_MODEL), BF16), jax.ShapeDtypeStruct((N_IN, D_MODEL), BF16)],
        scratch_shapes=[pltpu.VMEM((2, SEQ, tn), BF16), pltpu.SemaphoreType.DMA((2,)),
                        pltpu.VMEM((2, tn, D_MODEL), BF16), pltpu.SemaphoreType.DMA((2,)), pltpu.SemaphoreType.DMA,
                        pltpu.SemaphoreType.DMA((2,))],
        compiler_params=_params(("arbitrary",)))(hst, *pieces, *dep_args)


def _in_proj_dh(pieces, wt, dep=None):
    tk = 512
    blocks = _piece_blocks(pieces)
    nblk = len(blocks)
    npc = len(pieces)
    nchunk = D_MODEL // 128

    def col(s):
        return jnp.where(s < 3, s, jnp.where(s < 16, s + 6, s - 13))

    dep_specs, dep_args = _dep_operand(dep)

    def body(w_ref, *rest):
        piece_refs = rest[:npc]
        o_ref, acc_ref, buf, sem = rest[-4:]
        s = pl.program_id(0)
        slot = s % 2
        start, wait = _block_fetch(piece_refs, blocks, buf, sem)

        @pl.when(s == 0)
        def _():
            start(col(s), slot)

        wait(slot)

        @pl.when(s + 1 < nblk)
        def _():
            start(col(s + 1), 1 - slot)

        row_slices = [slice(r * 512, (r + 1) * 512) for r in range(SEQ // 512)]

        def product(rs):
            return jnp.dot(buf[slot, rs, :], w_ref[...], preferred_element_type=F32)

        def accumulate(cond, to_out, init):
            @pl.when(cond)
            def _():
                for rs in row_slices:
                    prod = product(rs)
                    if not to_out:
                        if init:
                            acc_ref[rs, :] = prod
                        else:
                            acc_ref[rs, :] += prod
                        continue
                    for c in range(nchunk):
                        if init:
                            o_ref[c, rs, :] = prod[:, c * 128:(c + 1) * 128]
                        else:
                            o_ref[c, rs, :] += prod[:, c * 128:(c + 1) * 128]

        accumulate(s == 0, True, True)
        accumulate(jnp.logical_and(s > 0, s < 16), True, False)
        accumulate(jnp.logical_or(s == 16, s == 19), False, True)
        accumulate(jnp.logical_and(s > 16, s != 19), False, False)
        for last, d in ((18, 4), (21, 16)):
            @pl.when(s == last)
            def _():
                mlen = SEQ // d
                for r in range(d):
                    for c in range(nchunk):
                        o_ref[c, pl.ds(r, mlen, stride=d), :] += acc_ref[r * mlen:(r + 1) * mlen,
                                                                         c * 128:(c + 1) * 128]

    return pl.pallas_call(
        body, name="in_proj_dh", grid=(nblk,),
        in_specs=[pl.BlockSpec((tk, D_MODEL), lambda s: (col(s), 0))] + [ANY] * npc + dep_specs,
        out_specs=pl.BlockSpec((nchunk, SEQ, 128), lambda s: (0, 0, 0)),
        out_shape=jax.ShapeDtypeStruct((nchunk, SEQ, 128), F32),
        scratch_shapes=[pltpu.VMEM((SEQ, D_MODEL), F32), pltpu.VMEM((2, SEQ, tk), BF16),
                        pltpu.SemaphoreType.DMA((2,))],
        compiler_params=_params(("arbitrary",)))(wt, *pieces, *dep_args)


def _head_lanes(lanes, hh):
    return lanes >= 64 if hh == 1 else lanes < 64


def _head_rows(x, lanes, hh, pair):
    if not pair:
        return jnp.max(x, axis=1, keepdims=True)
    return jnp.max(jnp.where(_head_lanes(lanes, hh), x, -jnp.inf), axis=1, keepdims=True)


def _mask_head(x, lanes, hh, pair, scale=1.0):
    if not pair:
        return x
    xf = x.astype(F32) if scale == 1.0 else x.astype(F32) * scale
    return jnp.where(_head_lanes(lanes, hh), xf, 0.0).astype(BF16)


def _window(mode, qi, tq, mlen, tk):
    if mode == "dil":
        q0 = qi * tq
        seg = (q0 // mlen) * mlen
        ks = jnp.clip(q0 - REACH, seg, seg + mlen - tk)
        return pl.multiple_of(ks, 64)
    if mode == "na":
        r_start = jnp.clip(qi - NA_ROWS // 2, 0, SEQ // GRID_W - NA_ROWS)
        return pl.multiple_of(r_start * GRID_W, 64)
    return 0


def _band_mask(qi, tq, tk, ks):
    qpos = qi * tq + _iota((tq, tk), 0)
    kpos = ks + _iota((tq, tk), 1)
    return jnp.where(jnp.abs(qpos - kpos) <= REACH, 0.0, NEG).astype(F32)


def _stack_heads(x, lanes, pair, scale=1.0):
    if not pair:
        return x
    return jnp.concatenate([_mask_head(x, lanes, hh, pair, scale) for hh in range(2)], axis=0)


def _stack_rows(x, lanes, pair):
    if not pair:
        return _head_rows(x, lanes, 0, pair)
    return jnp.concatenate([_head_rows(x, lanes, hh, pair) for hh in range(2)], axis=0)


def _unstack_heads(x, lanes, pair, tq):
    if not pair:
        return x
    return jnp.where(lanes < 64, x[:tq], x[tq:])


def _scores(mode, qst, k, sscale, band, qi, bias_ref, pair):
    s = lax.dot_general(qst, k, NT, preferred_element_type=F32)
    if sscale != 1.0:
        s = s * sscale
    if mode == "dil":
        s = s + jnp.concatenate([band, band], axis=0)
    elif mode == "na":
        off = qi - jnp.clip(qi - NA_ROWS // 2, 0, SEQ // GRID_W - NA_ROWS)
        s = s + jnp.concatenate([bias_ref[0, off], bias_ref[1, off]], axis=0)
    return s


def _attn_cfg(mode, d):
    if mode == "dil":
        mlen = SEQ // d
        return dict(pair=True, tq=128, tk=min(256, mlen), mlen=mlen, lk=SEQ, scale=HEAD_DIM ** -0.5, units=4,
                    nsub=ATTN_SUBTILES)
    if mode == "na":
        return dict(pair=True, tq=GRID_W, tk=NA_ROWS * GRID_W, mlen=SEQ, lk=SEQ, scale=HEAD_DIM ** -0.5, units=4,
                    nsub=2 * ATTN_SUBTILES)
    return dict(pair=False, tq=128, tk=MEM_LEN, mlen=SEQ, lk=MEM_LEN, scale=128 ** -0.5, units=4,
                nsub=ATTN_SUBTILES)


ATTN_SUBTILES = 16


def _attn_fwd(name, mode, q_arr, k_arr, v_arr, qcol, kcol, vcol, d=1, bias=None):
    cfg = _attn_cfg(mode, d)
    pair, tq, tk, mlen, lk, scale = cfg["pair"], cfg["tq"], cfg["tk"], cfg["mlen"], cfg["lk"], cfg["scale"]
    qscale, sscale = (scale, 1.0) if pair else (1.0, scale)
    nsub = cfg["nsub"]
    rows = nsub * tq

    def body(*refs):
        if mode == "na":
            q_ref, k_ref, v_ref, bias_ref, o_ref, l_ref = refs
        else:
            q_ref, k_ref, v_ref, o_ref, l_ref = refs
            bias_ref = None
        lanes = _iota((tq, 128), 1)
        qis = [pl.program_id(1) * nsub + sub for sub in range(nsub)]
        kss = [_window(mode, qi, tq, mlen, tk) for qi in qis]
        vs = [v_ref[pl.ds(ks, tk), :] for ks in kss]
        bands = [_band_mask(qi, tq, tk, ks) if mode == "dil" else None for qi, ks in zip(qis, kss)]
        ss = []
        for sub in range(nsub):
            qst = _stack_heads(q_ref[sub * tq:(sub + 1) * tq, :], lanes, pair, qscale)
            k = k_ref[pl.ds(kss[sub], tk), :]
            ss.append(_scores(mode, qst, k, sscale, bands[sub], qis[sub], bias_ref, pair))
        ms = [jnp.max(s_, axis=1, keepdims=True) for s_ in ss]
        ps = [jnp.exp(s_ - m) for s_, m in zip(ss, ms)]
        ls = [jnp.sum(p, axis=1, keepdims=True) for p in ps]
        os_ = [jnp.dot(p.astype(BF16), v, preferred_element_type=F32) for p, v in zip(ps, vs)]
        for sub in range(nsub):
            out = _unstack_heads(os_[sub] / ls[sub], lanes, pair, tq)
            lse = ms[sub] + jnp.log(ls[sub])
            lse = _unstack_heads(jnp.broadcast_to(lse, (lse.shape[0], 128)), lanes, pair, tq)
            dst = _folded_rows(qis[sub] * tq, tq, d) if mode == "dil" else slice(sub * tq, (sub + 1) * tq)
            o_ref[dst, :] = out
            l_ref[dst, :] = lse

    in_specs = [pl.BlockSpec((rows, 128), lambda u, i: (i, qcol + u)),
                pl.BlockSpec((lk, 128), lambda u, i: (0, kcol + u)),
                pl.BlockSpec((lk, 128), lambda u, i: (0, vcol + u))]
    args = [q_arr, k_arr, v_arr]
    if mode == "na":
        in_specs.append(pl.BlockSpec((2, NA_ROWS, GRID_W, NA_ROWS * GRID_W), lambda u, i: (u, 0, 0, 0)))
        args.append(bias)
    if mode == "dil":
        out_spec = pl.BlockSpec((SEQ, 128), lambda u, i: (0, u))
    else:
        out_spec = pl.BlockSpec((rows, 128), lambda u, i: (i, u))
    return pl.pallas_call(
        body, name=name, grid=(cfg["units"], SEQ // rows), in_specs=in_specs, out_specs=[out_spec, out_spec],
        out_shape=[jax.ShapeDtypeStruct((SEQ, 512), F32), jax.ShapeDtypeStruct((SEQ, 512), F32)],
        compiler_params=_params(("parallel", "arbitrary")))(*args)


def _attn_bwd(name, mode, q_arr, k_arr, v_arr, qcol, kcol, vcol, do, lse, dp=None, o=None, d=1, bias=None,
              tabs=None, dep=None):
    cfg = _attn_cfg(mode, d)
    pair, tq, tk, mlen, lk, scale = cfg["pair"], cfg["tq"], cfg["tk"], cfg["mlen"], cfg["lk"], cfg["scale"]
    qscale, sscale = (scale, 1.0) if pair else (1.0, scale)
    nsub = cfg["nsub"]
    rows = nsub * tq
    nq = SEQ // rows
    kv_dtype = F32 if mode == "mem" else BF16
    dep_specs, dep_args = _dep_operand(dep)
    mode_inputs = {"dil": 3, "na": 2, "mem": 1}[mode]

    def body(*refs):
        refs = list(refs)
        q_ref, k_ref, v_ref, do_ref, l_ref = refs[:5]
        rest = refs[5:5 + mode_inputs] + refs[5 + mode_inputs + len(dep_args):]
        bias_ref = tq_ref = tk_ref = db_ref = None
        if mode == "dil":
            dp_ref, tq_ref, tk_ref, dq_ref, dk_ref, dv_ref, dk_acc, dv_acc = rest
        elif mode == "na":
            o_ref, bias_ref, dq_ref, dk_ref, dv_ref, db_ref, dk_acc, dv_acc = rest
        else:
            o_ref, dq_ref, dk_ref, dv_ref, dk_acc, dv_acc = rest
        step = pl.program_id(1)

        @pl.when(step == 0)
        def _():
            dk_acc[...] = jnp.zeros((lk, 128), F32)
            dv_acc[...] = jnp.zeros((lk, 128), F32)
            if mode == "na":
                db_ref[...] = jnp.zeros(db_ref.shape, F32)

        lanes = _iota((tq, 128), 1)
        qis = [step * nsub + sub for sub in range(nsub)]
        sls = [slice(sub * tq, (sub + 1) * tq) for sub in range(nsub)]
        kss = [_window(mode, qi, tq, mlen, tk) for qi in qis]
        ks_ = [k_ref[pl.ds(ks, tk), :] for ks in kss]
        vs = [v_ref[pl.ds(ks, tk), :] for ks in kss]
        qsts, dosts, lses, dphs = [], [], [], []
        for sub in range(nsub):
            if mode == "dil":
                src = _folded_rows(qis[sub] * tq, tq, d)
                dov = do_ref[src, :].astype(BF16)
                lsev = l_ref[src, :]
                dphs.append(_stack_rows(dp_ref[src, :], lanes, pair))
            else:
                dov = do_ref[sls[sub], :]
                lsev = l_ref[sls[sub], :]
                dpv = dov.astype(F32) * o_ref[sls[sub], :]
                if pair:
                    dphs.append(jnp.concatenate(
                        [jnp.sum(jnp.where(_head_lanes(lanes, hh), dpv, 0.0), axis=1, keepdims=True)
                         for hh in range(2)], axis=0))
                else:
                    dphs.append(jnp.sum(dpv, axis=1, keepdims=True))
            qsts.append(_stack_heads(q_ref[sls[sub], :], lanes, pair, qscale))
            dosts.append(_stack_heads(dov, lanes, pair))
            lses.append(_stack_rows(lsev, lanes, pair))
        bands = [_band_mask(qi, tq, tk, ks) if mode == "dil" else None for qi, ks in zip(qis, kss)]
        ss = [_scores(mode, qsts[sub], ks_[sub], sscale, bands[sub], qis[sub], bias_ref, pair) for sub in range(nsub)]
        dpms = [lax.dot_general(dosts[sub], vs[sub], NT, preferred_element_type=F32) for sub in range(nsub)]
        ps = [jnp.exp(s_ - lse) for s_, lse in zip(ss, lses)]
        dss = [p * (dpm - dph) for p, dpm, dph in zip(ps, dpms, dphs)]
        if mode == "na":
            for sub, ds in enumerate(dss):
                off = qis[sub] - jnp.clip(qis[sub] - NA_ROWS // 2, 0, SEQ // GRID_W - NA_ROWS)
                db_ref[0, off] += ds[:tq]
                db_ref[1, off] += ds[tq:]
        dsbs = [ds.astype(BF16) for ds in dss]
        dvs = [lax.dot_general(p.astype(BF16), dosts[sub], TN, preferred_element_type=F32)
               for sub, p in enumerate(ps)]
        dqs = [jnp.dot(dsb, ks_[sub], preferred_element_type=F32) * scale for sub, dsb in enumerate(dsbs)]
        dks = [lax.dot_general(dsb, qsts[sub], TN, preferred_element_type=F32) for sub, dsb in enumerate(dsbs)]
        for sub in range(nsub):
            sl = sls[sub]
            dq = _unstack_heads(dqs[sub], lanes, pair, tq)
            if mode == "dil":
                dq = _rope_t(dq, tq_ref[0, sl, :], tq_ref[1, sl, :], tq_ref[2, sl, :])
            dq_ref[sl, :] = dq.astype(BF16)
            dk_acc[pl.ds(kss[sub], tk), :] += dks[sub] if pair else dks[sub] * scale
            dv_acc[pl.ds(kss[sub], tk), :] += dvs[sub]

        @pl.when(step == nq - 1)
        def _():
            dkv = dk_acc[...]
            if mode == "dil":
                dkv = _rope_t(dkv, tk_ref[0], tk_ref[1], tk_ref[2])
            dk_ref[...] = dkv.astype(kv_dtype)
            dv_ref[...] = dv_acc[...].astype(kv_dtype)

    q_spec = pl.BlockSpec((rows, 128), lambda u, i: (i, qcol + u))
    row_spec = pl.BlockSpec((rows, 128), lambda u, i: (i, u))
    kv_out = pl.BlockSpec((lk, 128), lambda u, i: (0, u))
    whole = pl.BlockSpec((SEQ, 128), lambda u, i: (0, u))
    nat_spec = whole if mode == "dil" else row_spec
    in_specs = [q_spec,
                pl.BlockSpec((lk, 128), lambda u, i: (0, kcol + u)),
                pl.BlockSpec((lk, 128), lambda u, i: (0, vcol + u)),
                nat_spec, nat_spec]
    args = [q_arr, k_arr, v_arr, do, lse]
    out_specs = [row_spec, kv_out, kv_out]
    out_shape = [jax.ShapeDtypeStruct((SEQ, 512), BF16), jax.ShapeDtypeStruct((lk, 512), kv_dtype),
                 jax.ShapeDtypeStruct((lk, 512), kv_dtype)]
    if mode == "dil":
        in_specs += [whole, pl.BlockSpec((3, rows, 128), lambda u, i: (0, i, 0)),
                     pl.BlockSpec((3, SEQ, 128), lambda u, i: (0, 0, 0))]
        args += [dp, tabs, tabs]
    elif mode == "na":
        b_spec = pl.BlockSpec((2, NA_ROWS, GRID_W, NA_ROWS * GRID_W), lambda u, i: (u, 0, 0, 0))
        in_specs += [row_spec, b_spec]
        args += [o, bias]
        out_specs.append(b_spec)
        out_shape.append(jax.ShapeDtypeStruct((8, NA_ROWS, GRID_W, NA_ROWS * GRID_W), F32))
    else:
        in_specs.append(row_spec)
        args.append(o)
    return pl.pallas_call(
        body, name=name, grid=(cfg["units"], nq), in_specs=in_specs + dep_specs, out_specs=out_specs,
        out_shape=out_shape, scratch_shapes=[pltpu.VMEM((lk, 128), F32), pltpu.VMEM((lk, 128), F32)],
        compiler_params=_params(("parallel", "arbitrary")))(*args, *dep_args)


def _na_geometry():
    qc = _iota((GRID_W, 128), 0)
    lane = _iota((GRID_W, 128), 1)
    kc = lane & 63
    c_start = jnp.clip(qc - 8, 0, GRID_W - 16)
    valid = jnp.logical_and(kc >= c_start, kc < c_start + 16)
    return lane, valid


def _na_bias(rpb_rows, dep=None):
    dep_specs, dep_args = _dep_operand(dep)

    def body(r_ref, *rest):
        o_ref, t_ref = rest[-2:]
        lane, valid = _na_geometry()
        for dd in range(14):
            row_a = jnp.broadcast_to(r_ref[dd:dd + 1, :], (GRID_W, 128))
            row_b = jnp.broadcast_to(r_ref[dd + 1:dd + 2, :], (GRID_W, 128))
            both = jnp.where(lane < 64, row_a, pltpu.roll(row_b, 64, 1))
            t = pltpu.roll(both, 128 - 15, 1, stride=1, stride_axis=0)
            t_ref[dd] = jnp.where(valid, t, NEG)
        for off in range(NA_ROWS):
            for p in range(4):
                o_ref[off, :, p * 128:(p + 1) * 128] = t_ref[2 * p - off + 7]

    return pl.pallas_call(
        body, name="na_bias", grid=(8,),
        in_specs=[pl.BlockSpec((None, 16, 128), lambda h: (h, 0, 0))] + dep_specs,
        out_specs=pl.BlockSpec((None, NA_ROWS, GRID_W, NA_ROWS * GRID_W), lambda h: (h, 0, 0, 0)),
        out_shape=jax.ShapeDtypeStruct((8, NA_ROWS, GRID_W, NA_ROWS * GRID_W), F32),
        scratch_shapes=[pltpu.VMEM((14, GRID_W, 128), F32)],
        compiler_params=_params(("parallel",)))(rpb_rows, *dep_args)


def _na_bias_bwd(dbias, dep=None):
    dep_specs, dep_args = _dep_operand(dep)

    def body(d_ref, *rest):
        o_ref = rest[-1]
        lane, valid = _na_geometry()
        reverse = (_iota((GRID_W, GRID_W), 0) + _iota((GRID_W, GRID_W), 1) == GRID_W - 1).astype(F32)
        o_ref[...] = jnp.zeros((16, 128), F32)
        for dd in range(14):
            t = jnp.zeros((GRID_W, 128), F32)
            for off in range(NA_ROWS):
                for p in range(4):
                    if 2 * p - off + 7 == dd:
                        t = t + d_ref[off, :, p * 128:(p + 1) * 128]
            t = jnp.dot(reverse, jnp.where(valid, t, 0.0), precision=lax.Precision.HIGHEST,
                        preferred_element_type=F32)
            t = pltpu.roll(t, 128 - (GRID_W - 16), 1, stride=1, stride_axis=0)
            o_ref[dd:dd + 1, :] = jnp.sum(t, axis=0, keepdims=True)

    return pl.pallas_call(
        body, name="na_bias_bwd", grid=(8,),
        in_specs=[pl.BlockSpec((None, NA_ROWS, GRID_W, NA_ROWS * GRID_W), lambda h: (h, 0, 0, 0))] + dep_specs,
        out_specs=pl.BlockSpec((None, 16, 128), lambda h: (h, 0, 0)),
        out_shape=jax.ShapeDtypeStruct((8, 16, 128), F32),
        compiler_params=_params(("parallel",)))(dbias, *dep_args)


GATE_ROWS = 128


def _group_weights(l0, l1, l2):
    m = jnp.maximum(jnp.maximum(l0, l1), l2)
    e0, e1, e2 = jnp.exp(l0 - m), jnp.exp(l1 - m), jnp.exp(l2 - m)
    inv = 1.0 / (e0 + e1 + e2)
    return e0 * inv, e1 * inv, e2 * inv


def _gate_block(o_grp, l_grp, out_b, out_c, parts, x, target, merge_bias, wts, w_out, gain, head_sum):
    rows = GATE_ROWS
    r512 = pl.BlockSpec((rows, 512), lambda i: (i, 0))
    r1024 = pl.BlockSpec((rows, D_MODEL), lambda i: (i, 0))
    silu_cols = [pl.BlockSpec((rows, 512), functools.partial(lambda b, i: (i, b), 13 + b)) for b in range(3)]
    logit_cols = [pl.BlockSpec((rows, D_MODEL), functools.partial(lambda b, i: (i, b), 8 + b)) for b in range(3)]

    def body(o0, o1, o2, l0, l1, l2, ob, oc, ga, gb, gc, la, lb, lc, x_ref, t_ref, mb, wa, wb, wc, wo_ref, gn_ref,
             hs_ref, dout_ref, dla, dlb, dlc, dga, dgb, dgc, do0, do1, do2, dp0, dp1, dp2, dob, doc, err_ref, gg_ref,
             gmb, gwa, gwb, gwc, gwo, acc_a, acc_b, acc_c, acc_o):
        step = pl.program_id(0)
        ws = _group_weights(l0[...], l1[...], l2[...])
        out_a = ws[0] * o0[...] + ws[1] * o1[...] + ws[2] * o2[...]
        branches = ((out_a, ga, la, wa, acc_a, dla, dga), (ob[...], gb, lb, wb, acc_b, dlb, dgb),
                    (oc[...], gc, lc, wc, acc_c, dlc, dgc))

        @pl.when(step == 0)
        def _():
            for acc in (acc_a, acc_b, acc_c, acc_o):
                acc[...] = jnp.zeros(acc.shape, F32)
            err_ref[...] = jnp.zeros((1, D_MODEL), F32)
            gg_ref[...] = jnp.zeros((1, D_MODEL), F32)
            gmb[...] = jnp.zeros((3, D_MODEL), F32)

        y = jnp.zeros((rows, D_MODEL), F32)
        zs, gates, silus, dsilus, us = [], [], [], [], []
        for b, (ov, g_ref, l_ref, w_ref, _, _, _) in enumerate(branches):
            g = g_ref[...].astype(F32)
            sg = _sigmoid(g)
            silus.append(g * sg)
            dsilus.append(sg * (1.0 + g * (1.0 - sg)))
            us.append((ov * silus[b]).astype(BF16))
            zs.append(lax.dot_general(us[b], w_ref[...], NT, preferred_element_type=F32))
            gates.append(_sigmoid(l_ref[...].astype(F32) + mb[b:b + 1, :]))
            y = y + gates[b] * zs[b]
        yb = y.astype(BF16)
        y2 = jnp.dot(yb, wo_ref[...], preferred_element_type=F32)
        rstd = lax.rsqrt(jnp.mean(y2 * y2, axis=1, keepdims=True) + EPS)
        yn = y2 * rstd
        gv = gn_ref[...]
        err = x_ref[...] + yn * gv - t_ref[...]
        dout = err * (1.0 / D_MODEL)
        dout_ref[...] = dout
        dn = dout * gv
        dy2 = (rstd * (dn - yn * jnp.mean(dn * yn, axis=1, keepdims=True))).astype(BF16)
        acc_o[...] += lax.dot_general(yb, dy2, TN, preferred_element_type=F32)
        err_ref[...] += jnp.sum(err * err, axis=0, keepdims=True)
        gg_ref[...] += jnp.sum(dout * yn, axis=0, keepdims=True)
        dy = lax.dot_general(dy2, wo_ref[...], NT, preferred_element_type=F32)
        dos = []
        for b, (ov, _, _, w_ref, acc, dl_ref, dg_ref) in enumerate(branches):
            dl = dy * zs[b] * gates[b] * (1.0 - gates[b])
            dl_ref[...] = dl.astype(BF16)
            gmb[b:b + 1, :] += jnp.sum(dl, axis=0, keepdims=True)
            dz = (dy * gates[b]).astype(BF16)
            acc[...] += lax.dot_general(dz, us[b], TN, preferred_element_type=F32)
            du = jnp.dot(dz, w_ref[...], preferred_element_type=F32)
            dos.append(du * silus[b])
            dg_ref[...] = (du * ov * dsilus[b]).astype(BF16)
        dob[...] = dos[1].astype(BF16)
        doc[...] = dos[2].astype(BF16)
        row_term = jnp.dot(dos[0] * out_a, hs_ref[...], precision=lax.Precision.HIGHEST, preferred_element_type=F32)
        for wg, do_ref, dp_ref in zip(ws, (do0, do1, do2), (dp0, dp1, dp2)):
            do_ref[...] = wg * dos[0]
            dp_ref[...] = wg * row_term

        @pl.when(step == SEQ // rows - 1)
        def _():
            for acc, out in ((acc_a, gwa), (acc_b, gwb), (acc_c, gwc), (acc_o, gwo)):
                out[...] = acc[...].astype(BF16)

    full = lambda shape: pl.BlockSpec(shape, lambda i: (0,) * len(shape))
    vec = pl.BlockSpec((1, D_MODEL), lambda i: (0, 0))
    acc3 = pl.BlockSpec((3, D_MODEL), lambda i: (0, 0))
    in_specs = ([r512] * 8 + silu_cols + logit_cols + [r1024, r1024, full((3, D_MODEL))]
                + [full((D_MODEL, 512))] * 3 + [full((D_MODEL, D_MODEL)), vec, full((512, 512))])
    out_specs = ([r1024] + [r1024] * 3 + [r512] * 3 + [r512] * 6 + [r512] * 2 + [vec, vec, acc3]
                 + [full((D_MODEL, 512))] * 3 + [full((D_MODEL, D_MODEL))])
    bf, f32 = BF16, F32
    sds = jax.ShapeDtypeStruct
    out_shape = ([sds((SEQ, D_MODEL), f32)] + [sds((SEQ, D_MODEL), bf)] * 3 + [sds((SEQ, 512), bf)] * 3
                 + [sds((SEQ, 512), f32)] * 6 + [sds((SEQ, 512), bf)] * 2 + [sds((1, D_MODEL), f32)] * 2
                 + [sds((3, D_MODEL), f32)] + [sds((D_MODEL, 512), bf)] * 3 + [sds((D_MODEL, D_MODEL), bf)])
    res = pl.pallas_call(
        body, name="gate_block", grid=(SEQ // rows,), in_specs=in_specs, out_specs=out_specs, out_shape=out_shape,
        scratch_shapes=[pltpu.VMEM((D_MODEL, 512), F32)] * 3 + [pltpu.VMEM((D_MODEL, D_MODEL), F32)],
        compiler_params=_params(("arbitrary",)))(
            *o_grp, *l_grp, out_b, out_c, parts, parts, parts, parts, parts, parts, x, target, merge_bias, *wts, w_out,
            gain, head_sum)
    return dict(dout=res[0], dlog=res[1:4], dg=res[4:7], do_grp=res[7:10], dp_grp=res[10:13], do_b=res[13],
                do_c=res[14], err_sq=res[15], g_post=res[16], g_mb=res[17], g_wt=res[18:21], g_w_out=res[21])


def _local_step(x, hst, parts, tabs, bias, mem, target, pre_norm, mem_norm, post_norm, wt_in, late_weights,
                reduce_start=None):
    o_grp, l_grp = [], []
    for g, d in enumerate(DILATIONS):
        o, l = _attn_fwd("dil_fwd_%d" % g, "dil", parts, parts, parts, 12 * g, 12 * g + 4, 12 * g + 8, d=d)
        o_grp.append(o)
        l_grp.append(l)
    out_b, lse_b = _attn_fwd("na_fwd", "na", parts, parts, parts, 36, 40, 44, bias=bias)
    merge_bias, w_kv, wt_a, wt_b, wt_c, w_out = late_weights(sum(a[:8, :128] for a in [out_b] + o_grp))
    memn = _rmsnorm_fwd("memnorm", mem, mem_norm, MEM_LEN)
    kv_m = _mm_simple("mem_kv", memn, w_kv, NN, BF16, MEM_LEN, 512, D_MODEL)
    out_c, lse_c = _attn_fwd("mem_fwd", "mem", parts, kv_m, kv_m, 48, 0, 4)

    rr = _iota((512, 512), 0) // HEAD_DIM
    cc = _iota((512, 512), 1) // HEAD_DIM
    head_sum = (rr == cc).astype(F32)
    gb = _gate_block(o_grp, l_grp, out_b, out_c, parts, x, target, merge_bias, (wt_a, wt_b, wt_c), w_out, post_norm,
                     head_sum)
    dout, dlog, dg, g_wt, g_w_out = gb["dout"], gb["dlog"], gb["dg"], gb["g_wt"], gb["g_w_out"]
    do_grp, dp_grp, do_b, do_c, g_post, g_mb = (gb["do_grp"], gb["dp_grp"], gb["do_b"], gb["do_c"], gb["g_post"],
                                                gb["g_mb"])
    loss = 0.5 * jnp.sum(gb["err_sq"]) / D_MODEL

    dq_c, dk_m, dv_m = _attn_bwd("mem_bwd", "mem", parts, kv_m, kv_m, 48, 0, 4, do_c, lse_c, o=out_c)
    dkv = jnp.concatenate([dk_m, dv_m], axis=1).astype(BF16)
    g_w_kv = _mm_simple("mem_kv_dw", memn, dkv, TN, BF16, D_MODEL, 512, MEM_LEN)
    dmemn = _mm_simple("mem_kv_dx", dkv, w_kv, NT, F32, MEM_LEN, 512, D_MODEL)
    grads = dict(w_kv=g_w_kv, wt_a=g_wt[0], wt_b=g_wt[1], wt_c=g_wt[2], w_out=g_w_out, merge_bias=g_mb,
                 post_norm=g_post)
    dep = reduce_start("rest_sibling", grads) if reduce_start is not None else None

    dq_b, dk_b, dv_b, dbias = _attn_bwd("na_bwd", "na", parts, parts, parts, 36, 40, 44, do_b, lse_b, o=out_b,
                                        bias=bias, dep=dep)
    dqkv = []
    for g, d in enumerate(DILATIONS):
        dq, dk, dv = _attn_bwd("dil_bwd_%d" % g, "dil", parts, parts, parts, 12 * g, 12 * g + 4, 12 * g + 8,
                               do_grp[g], l_grp[g], dp=dp_grp[g], d=d, tabs=tabs[g])
        dqkv += [dq, dk, dv]
    if reduce_start is not None:
        dep = reduce_start("rest_chips", grads, sum(a[:8, :128] for a in (dqkv[0], dqkv[3], dqkv[6], dq_b)))
    dparts = dqkv + [dq_b, dk_b, dv_b, dq_c] + list(dg) + list(dlog)
    grads["wt_in"] = _in_proj_dw(dparts, hst, dep)
    dep = reduce_start("w_in", grads) if reduce_start is not None else None
    dh = _in_proj_dh(dparts, wt_in, dep)
    if reduce_start is not None:
        dep = reduce_start("w_in_second", grads, dh)
    grad_x, grads["pre_norm"] = _prenorm_bwd(x, pre_norm, dh, dout)
    g_rpb_t = _na_bias_bwd(dbias, dep)
    grads["na_rpb"] = g_rpb_t[:, :15, :31] + jnp.pad(g_rpb_t[:, :14, 64:95], ((0, 0), (1, 0), (0, 0)))
    grads["mem_norm"] = _memnorm_bwd(mem, dmemn, dep)
    return loss, grad_x, grads


ANY = pl.BlockSpec(memory_space=pl.ANY)


def _place():
    return lax.axis_index("x"), lax.axis_index("y"), lax.axis_index("c")


HBM = pl.BlockSpec(memory_space=pltpu.HBM)
SEM = pl.BlockSpec(memory_space=pltpu.SEMAPHORE)
DATAFLOW = pltpu.SideEffectType.DATAFLOW_SIDE_EFFECTING


def _split_copies(kind, srcs, lands, send_sems, recv_sems):
    nt = len(srcs)
    x, y, c = _place()
    copies = []
    if kind == "sibling":
        for q in range(4):
            for t in range(nt):
                k = q * nt + t
                copies.append(pltpu.make_async_remote_copy(
                    src_ref=srcs[t].at[2 * q + 1 - c], dst_ref=lands[t].at[q], send_sem=send_sems.at[k],
                    recv_sem=recv_sems.at[k], device_id=(x, y, 1 - c), device_id_type=MESH_ID))
    elif kind in ("rs_a", "rs_b"):
        half = lands[0].shape[1]
        xn, yn = (1 - x, y, c), (x, 1 - y, c)
        q_xn, q_yn, q_dg = 2 * (1 - x) + y, 2 * x + 1 - y, 2 * (1 - x) + 1 - y
        if kind == "rs_a":
            plan = [(srcs[0].at[q_yn].at[pl.ds(0, half)], 0, yn), (srcs[0].at[q_dg].at[pl.ds(0, half)], 1, yn),
                    (srcs[0].at[q_xn].at[pl.ds(half, half)], 2, xn), (srcs[0].at[q_dg].at[pl.ds(half, half)], 3, xn)]
        else:
            plan = [(srcs[0].at[0], 0, xn), (srcs[0].at[1], 1, yn)]
        for k, (src, slot, to) in enumerate(plan):
            copies.append(pltpu.make_async_remote_copy(
                src_ref=src, dst_ref=lands[0].at[slot], send_sem=send_sems.at[k], recv_sem=recv_sems.at[k],
                device_id=to, device_id_type=MESH_ID))
    elif kind == "gather":
        me = 4 * x + 2 * y + c
        for mask in range(1, 8):
            fx, fy, fc = (mask >> 2) & 1, (mask >> 1) & 1, mask & 1
            to = (1 - x if fx else x, 1 - y if fy else y, 1 - c if fc else c)
            for t in range(nt):
                k = (mask - 1) * nt + t
                copies.append(pltpu.make_async_remote_copy(
                    src_ref=srcs[t], dst_ref=lands[t].at[me], send_sem=send_sems.at[k], recv_sem=recv_sems.at[k],
                    device_id=to, device_id_type=MESH_ID))
    else:
        for s, (tx, ty) in enumerate([(1 - x, y), (x, 1 - y), (1 - x, 1 - y)]):
            for t in range(nt):
                k = s * nt + t
                copies.append(pltpu.make_async_remote_copy(
                    src_ref=srcs[t].at[2 * tx + ty], dst_ref=lands[t].at[s], send_sem=send_sems.at[k],
                    recv_sem=recv_sems.at[k], device_id=(tx, ty, c), device_id_type=MESH_ID))
    return copies


def _split_count(kind, nt):
    return {"gather": 7, "chips": 3, "sibling": 4, "rs_a": 4, "rs_b": 2}[kind] * nt


def _exchange_start(name, kind, srcs, land_shapes, after=None):
    nt = len(srcs)
    n = _split_count(kind, nt)
    dep_specs, dep_args = _dep_operand(after)
    nd = len(dep_args)

    def body(*refs):
        src_refs, land_refs = refs[:nt], refs[nt:2 * nt]
        send_sems, recv_sems = refs[2 * nt + nd], refs[2 * nt + nd + 1]
        token = refs[-1]
        for cp in _split_copies(kind, src_refs, land_refs, send_sems, recv_sems):
            cp.start()
        token[...] = jnp.zeros_like(token)

    lands = [pltpu.with_memory_space_constraint(lax.empty(s.shape, s.dtype), pltpu.HBM) for s in land_shapes]
    res = pl.pallas_call(
        body, name=name,
        out_shape=(pltpu.SemaphoreType.DMA((n,)), pltpu.SemaphoreType.DMA((n,)),
                   *[pltpu.HBM(s.shape, s.dtype) for s in srcs], *[pltpu.HBM(s.shape, s.dtype) for s in land_shapes],
                   jax.ShapeDtypeStruct((8, 128), F32)),
        in_specs=[HBM] * (2 * nt) + dep_specs,
        out_specs=(SEM, SEM, *([HBM] * (2 * nt)), pl.BlockSpec(memory_space=pltpu.VMEM)),
        input_output_aliases={i: 2 + i for i in range(2 * nt)},
        compiler_params=pltpu.CompilerParams(has_side_effects=DATAFLOW))(
            *[pltpu.with_memory_space_constraint(s, pltpu.HBM) for s in srcs], *lands, *dep_args)
    return res[0], res[1], list(res[2:2 + nt]), list(res[2 + nt:2 + 2 * nt]), res[-1]


def _exchange_wait(name, kind, send_sems, recv_sems, srcs, lands, after):
    nt = len(srcs)

    def body(*refs):
        src_refs, land_refs = refs[:nt], refs[nt:2 * nt]
        s_sems, r_sems = refs[2 * nt], refs[2 * nt + 1]
        for cp in _split_copies(kind, src_refs, land_refs, s_sems, r_sems):
            cp.wait_send()
            cp.wait_recv()

    res = pl.pallas_call(
        body, name=name,
        out_shape=tuple(pltpu.HBM(s.shape, s.dtype) for s in list(srcs) + list(lands)),
        in_specs=[HBM] * (2 * nt) + [SEM, SEM, pl.BlockSpec(memory_space=pl.ANY)],
        out_specs=tuple([HBM] * (2 * nt)),
        input_output_aliases={i: i for i in range(2 * nt)},
        compiler_params=pltpu.CompilerParams(has_side_effects=DATAFLOW))(
            *srcs, *lands, send_sems, recv_sems, after)
    return list(res[:nt]), list(res[nt:])


AG_GROUPS = ((0, 3), (3, 4), (7, 2))


def _ag_phase(name, own, land, sems, waits, starts, after=None):
    r = own.shape[0]
    half = r // 2
    ns = len(sems)
    dep_specs, dep_args = _dep_operand(after)
    nd = len(dep_args)
    new_group = None
    if starts:
        (new_group,) = [g for g, (first, n) in enumerate(AG_GROUPS) if first == starts[0]]
        assert list(starts) == list(range(AG_GROUPS[new_group][0], sum(AG_GROUPS[new_group])))

    def body(*refs):
        own_ref, land_ref = refs[0], refs[1]
        sem_refs = list(refs[2:2 + 2 * ns])
        outs = refs[2 + 2 * ns + nd:]
        if starts:
            sem_refs += [outs[0], outs[1]]
        x, y, c = _place()
        me, sib = (x, y, c), (x, y, 1 - c)
        xn, yn, dg = (1 - x, y, c), (x, 1 - y, c), (1 - x, 1 - y, c)

        def other(dev):
            return (dev[0], dev[1], 1 - dev[2])

        def rows(dev, part):
            blk = land_ref.at[4 * dev[0] + 2 * dev[1] + dev[2]]
            return blk if part is None else blk.at[pl.ds(part * half, half)]

        def sem_of(k):
            (g,) = [g for g, (first, n) in enumerate(AG_GROUPS) if first <= k < first + n]
            return sem_refs[2 * g].at[k - AG_GROUPS[g][0]], sem_refs[2 * g + 1].at[k - AG_GROUPS[g][0]]

        sent = {0: (me, None, sib), 1: (me, None, xn), 2: (me, None, yn), 3: (xn, 0, yn), 4: (yn, 1, xn),
                5: (xn, None, sib), 6: (yn, None, sib), 7: (dg, 0, sib), 8: (dg, 1, sib)}
        landed = {0: (sib, None), 1: (xn, None), 2: (yn, None), 3: (dg, 0), 4: (dg, 1), 5: (other(xn), None),
                  6: (other(yn), None), 7: (other(dg), 0), 8: (other(dg), 1)}

        def copy(k, receiving):
            send_sem, recv_sem = sem_of(k)
            dev, part, to = (*landed[k], me) if receiving else sent[k]
            src = own_ref if (dev is me and not receiving) else rows(dev, part)
            return pltpu.make_async_remote_copy(src_ref=src, dst_ref=rows(dev, part), send_sem=send_sem,
                                                recv_sem=recv_sem, device_id=to, device_id_type=MESH_ID)

        for kind, k in waits:
            if kind == "recv":
                copy(k, True).wait_recv()
            else:
                copy(k, False).wait_send()
        for k in starts:
            copy(k, False).start()
        if starts:
            outs[-1][...] = jnp.zeros_like(outs[-1])

    n_new = AG_GROUPS[new_group][1] if starts else 0
    sem_out = (pltpu.SemaphoreType.DMA((n_new,)), pltpu.SemaphoreType.DMA((n_new,))) if starts else ()
    token_out = (jax.ShapeDtypeStruct((8, 128), F32),) if starts else ()
    res = pl.pallas_call(
        body, name=name,
        out_shape=(*sem_out, pltpu.HBM(own.shape, own.dtype), pltpu.HBM(land.shape, land.dtype), *token_out),
        in_specs=[HBM, HBM] + [SEM] * (2 * ns) + dep_specs,
        out_specs=(*([SEM] * len(sem_out)), HBM, HBM, *([pl.BlockSpec(memory_space=pltpu.VMEM)] * len(token_out))),
        input_output_aliases={0: len(sem_out), 1: len(sem_out) + 1},
        compiler_params=pltpu.CompilerParams(has_side_effects=DATAFLOW))(
            own, land, *[a for pair in sems for a in pair], *dep_args)
    if starts:
        return (res[0], res[1]), res[2], res[3], res[4]
    return None, res[0], res[1], None


def _add_sibling(name, term, recv, rows):
    _, r, w = term.shape
    cidx = lax.axis_index("c").astype(jnp.int32).reshape(1)
    like_term = recv.shape[0] == N_DEV

    def body(c_ref, a_ref, b_ref, o_ref):
        o_ref[...] = (a_ref[...].astype(F32) + b_ref[...].astype(F32)).astype(o_ref.dtype)

    grid_spec = pltpu.PrefetchScalarGridSpec(
        num_scalar_prefetch=1, grid=(4, r // rows),
        in_specs=[pl.BlockSpec((None, rows, w), lambda q, i, c_ref: (2 * q + c_ref[0], i, 0)),
                  pl.BlockSpec((None, rows, w), lambda q, i, c_ref: (2 * q + c_ref[0] if like_term else q, i, 0))],
        out_specs=pl.BlockSpec((None, rows, w), lambda q, i, c_ref: (q, i, 0)))
    return pl.pallas_call(
        body, name=name, grid_spec=grid_spec, out_shape=jax.ShapeDtypeStruct((4, r, w), term.dtype),
        compiler_params=_params(("parallel", "parallel")))(cidx, term, recv)


def _add_sibling_small(name, terms, recvs):
    nt = len(terms)

    def body(*refs):
        c = lax.axis_index("c")
        for t_ref, r_ref, o_ref in zip(refs[:nt], refs[nt:2 * nt], refs[2 * nt:]):
            for q in range(4):
                o_ref[q] = (t_ref[2 * q + c].astype(F32) + r_ref[q].astype(F32)).astype(o_ref.dtype)

    return pl.pallas_call(
        body, name=name, out_shape=[jax.ShapeDtypeStruct((4,) + t.shape[1:], t.dtype) for t in terms],
        compiler_params=_params())(*terms, *recvs)


def _reduce_scatter_start(tag, terms, recv1):
    sums = _add_sibling_small("add_sibling_" + tag, terms, recv1)
    lands = [jax.ShapeDtypeStruct((3,) + s.shape[1:], s.dtype) for s in sums]
    send_sems, recv_sems, sums, lands, token = _exchange_start("exchange_chips_start_" + tag, "chips", sums, lands)
    return (tag, send_sems, recv_sems, sums, lands), token


def _reduce_scatter_wait(state, after):
    tag, send_sems, recv_sems, sums, lands = state
    return _exchange_wait("exchange_chips_wait_" + tag, "chips", send_sems, recv_sems, sums, lands, after)


def _adam_math(w, g, m, v):
    nm = ADAM_B1 * m + (1.0 - ADAM_B1) * g
    nv = ADAM_B2 * v + (1.0 - ADAM_B2) * (g * g)
    c1 = 1.0 - ADAM_B1 ** ADAM_STEP
    c2 = 1.0 - ADAM_B2 ** ADAM_STEP
    return -ADAM_LR * ((nm / c1) / (jnp.sqrt(nv / c2) + ADAM_EPS) + ADAM_WD * w), nm, nv


def _presum_halves(sums, landed):
    _, r, w = sums.shape
    rows = r // 2
    x, y = lax.axis_index("x"), lax.axis_index("y")
    dest = jnp.stack([2 * (1 - x) + y, 2 * x + 1 - y]).astype(jnp.int32)

    def body(q_ref, a_ref, b_ref, o_ref):
        o_ref[...] = (a_ref[...].astype(F32) + b_ref[...].astype(F32)).astype(o_ref.dtype)

    grid_spec = pltpu.PrefetchScalarGridSpec(
        num_scalar_prefetch=1, grid=(2,),
        in_specs=[pl.BlockSpec((None, rows, w), lambda h, q_ref: (q_ref[h], h, 0)),
                  pl.BlockSpec((None, rows, w), lambda h, q_ref: (1 + 2 * h, 0, 0))],
        out_specs=pl.BlockSpec((None, rows, w), lambda h, q_ref: (h, 0, 0)))
    return pl.pallas_call(
        body, name="presum_halves", grid_spec=grid_spec, out_shape=jax.ShapeDtypeStruct((2, r // 2, w), sums.dtype),
        compiler_params=_params(("parallel",)))(dest, sums, landed)


def _adamw_halves(name, sums, landed_a, landed_b, w, m, v, rows):
    r, c = w.shape
    half = c // 2
    qidx = (2 * lax.axis_index("x") + lax.axis_index("y")).astype(jnp.int32).reshape(1)

    def body(q_ref, s_ref, a_ref, b_ref, w_ref, m_ref, v_ref, g_ref, d_ref, nm_ref, nv_ref):
        first = (s_ref[:half, :].astype(F32) + a_ref[0].astype(F32)) + b_ref[0].astype(F32)
        second = (s_ref[half:, :].astype(F32) + a_ref[2].astype(F32)) + b_ref[1].astype(F32)
        g = jnp.concatenate([first, second], axis=0).T
        g_ref[...] = g
        d_ref[...], nm_ref[...], nv_ref[...] = _adam_math(w_ref[...], g, m_ref[...], v_ref[...])

    row = pl.BlockSpec((rows, c), lambda i, q_ref: (i, 0))
    grid_spec = pltpu.PrefetchScalarGridSpec(
        num_scalar_prefetch=1, grid=(r // rows,),
        in_specs=[pl.BlockSpec((None, c, rows), lambda i, q_ref: (q_ref[0], 0, i)),
                  pl.BlockSpec((4, half, rows), lambda i, q_ref: (0, 0, i)),
                  pl.BlockSpec((2, half, rows), lambda i, q_ref: (0, 0, i)), row, row, row],
        out_specs=[row] * 4)
    return pl.pallas_call(
        body, name=name, grid_spec=grid_spec, out_shape=[jax.ShapeDtypeStruct((r, c), F32)] * 4,
        compiler_params=_params(("parallel",)))(qidx, sums, landed_a, landed_b, w, m, v)


def _adamw_chips_small(name, items):
    n = len(items)

    def body(*refs):
        q = 2 * lax.axis_index("x") + lax.axis_index("y")
        ins, outs = refs[:5 * n], refs[5 * n:]
        for i, (_, _, w, _, _, transposed) in enumerate(items):
            s_ref, r_ref, w_ref, m_ref, v_ref = ins[5 * i:5 * i + 5]
            g_ref, d_ref, nm_ref, nv_ref = outs[4 * i:4 * i + 4]
            g = (s_ref[q].astype(F32) + r_ref[0].astype(F32)) + (r_ref[1].astype(F32) + r_ref[2].astype(F32))
            g = g.T if transposed else g[:w.shape[0]]
            g_ref[...] = g
            d_ref[...], nm_ref[...], nv_ref[...] = _adam_math(w_ref[...], g, m_ref[...], v_ref[...])

    res = pl.pallas_call(
        body, name=name, out_shape=[jax.ShapeDtypeStruct(it[2].shape, F32) for it in items for _ in range(4)],
        compiler_params=_params())(*[a for it in items for a in it[:5]])
    return [res[4 * i:4 * i + 4] for i in range(n)]


def _adamw_replicated(gathered, items):
    n = len(items)

    def body(g_ref, *refs):
        ins, t_ref, outs = refs[:3 * n], refs[3 * n], refs[3 * n + 1:]
        acc = g_ref[0]
        for j in range(1, N_DEV):
            acc = acc + g_ref[j]
        t_ref[...] = acc
        for i, (first, w, _, _) in enumerate(items):
            w_ref, m_ref, v_ref = ins[3 * i:3 * i + 3]
            g = t_ref[first:first + w.shape[0], :]
            outs[3 * i][...], outs[3 * i + 1][...], outs[3 * i + 2][...] = _adam_math(w_ref[...], g, m_ref[...], v_ref[...])

    res = pl.pallas_call(
        body, name="adamw_replicated",
        out_shape=[jax.ShapeDtypeStruct(gathered.shape[1:], F32)]
        + [jax.ShapeDtypeStruct(it[1].shape, F32) for it in items for _ in range(3)],
        compiler_params=_params())(gathered, *[a for it in items for a in it[1:]])
    return res[0], [res[1 + 3 * i:4 + 3 * i] for i in range(n)]


def _wide_rows(a):
    rows = -(-a.size // D_MODEL)
    return jnp.pad(a.reshape(-1), (0, rows * D_MODEL - a.size)).reshape(rows, D_MODEL)


def kernel(x, mem, pre_norm, w_in, merge_bias, na_rpb, mem_norm, w_mem_kv, w_branch_a, w_branch_b, w_branch_c, w_out, post_norm, loss_target, m_pre_norm, m_w_in, m_merge_bias, m_na_rpb, m_mem_norm, m_w_mem_kv, m_w_branch_a, m_w_branch_b, m_w_branch_c, m_w_out, m_post_norm, v_pre_norm, v_w_in, v_merge_bias, v_na_rpb, v_mem_norm, v_w_mem_kv, v_w_branch_a, v_w_branch_b, v_w_branch_c, v_w_out, v_post_norm):
    wt_in_s = w_in[0].T.astype(BF16)
    rows_s = jnp.concatenate([w_mem_kv[0], w_out[0]], axis=0).astype(BF16)
    cols_s = jnp.concatenate([w_branch_a[0].T, w_branch_b[0].T, w_branch_c[0].T], axis=0).astype(BF16)
    mb_s = jnp.pad(merge_bias[0], ((0, 5), (0, 0)))
    me = 4 * lax.axis_index("x") + 2 * lax.axis_index("y") + lax.axis_index("c")

    chip = 2 * lax.axis_index("x") + lax.axis_index("y")

    def first_block(q):
        return jnp.where(q == 0, 0, jnp.where(q == 1, 6, jnp.where(q == 2, 11, 17)))

    five = jnp.arange(5, dtype=jnp.int32)
    near, far = jnp.where(chip < 2, 5, 16), jnp.where(chip < 2, 16, 5)
    order1 = (first_block(chip) + five).astype(jnp.int32)
    order2 = jnp.concatenate([first_block(chip ^ 1) + five, near[None], first_block(chip ^ 2) + five]).astype(jnp.int32)
    order3 = jnp.concatenate([first_block(chip ^ 3) + five, far[None]]).astype(jnp.int32)
    tabs = _rope_tables()

    def weights_of(land):
        return land.reshape(N_IN, D_MODEL)

    land = pltpu.with_memory_space_constraint(lax.empty((N_DEV,) + wt_in_s.shape, BF16), pltpu.HBM)
    own = pltpu.with_memory_space_constraint(wt_in_s, pltpu.HBM)
    sem_a, own, land, token = _ag_phase("ag_start", own, land, [], [], [0, 1, 2])
    hs, hst = _prenorm_fold(x[0], pre_norm, token)
    _, own, land, _ = _ag_phase("ag_wait0", own, land, [sem_a], [("recv", 0)], [], hs)
    land = lax.dynamic_update_slice(land, own[None], (me, 0, 0))
    parts = _in_proj("in_proj_1", hs, weights_of(land), tabs, order1)
    bias = _na_bias(jnp.pad(na_rpb[0], ((0, 0), (0, 1), (0, 128 - 31))), parts)
    sem_b, own, land, _ = _ag_phase("ag_mid1", own, land, [sem_a], [("recv", 1), ("recv", 2)], [3, 4, 5, 6], bias)
    _, own, land, _ = _ag_phase("ag_wait1", own, land, [sem_a, sem_b], [("recv", 5), ("recv", 6)], [])
    parts = _in_proj("in_proj_2", hs, weights_of(land), tabs, order2, parts)
    sem_c, own, land, _ = _ag_phase("ag_mid2", own, land, [sem_a, sem_b], [("recv", 3), ("recv", 4)], [7, 8], parts)
    _, own, land, _ = _ag_phase("ag_end", own, land, [sem_a, sem_b, sem_c],
                                [("recv", 7), ("recv", 8)] + [("send", k) for k in range(9)], [])
    wt_in = weights_of(land)

    late_own = [rows_s, cols_s, mb_s]
    late_lands = [jax.ShapeDtypeStruct((N_DEV,) + s.shape, s.dtype) for s in late_own]
    l_send, l_recv, late_own, late_lands, late_token = _exchange_start("gather_late_start", "gather", late_own,
                                                                       late_lands, after=wt_in)
    parts = _in_proj("in_proj_3", hs, wt_in, tabs, order3, parts, late_token)

    def late_weights(after):
        own, lands = _exchange_wait("gather_late_wait", "gather", l_send, l_recv, late_own, late_lands, after)
        g_rows, g_cols, g_mb = [lax.dynamic_update_slice(land, o[None], (me, 0, 0)) for land, o in zip(lands, own)]
        return (g_mb[:, :3].transpose(1, 0, 2).reshape(3, D_MODEL),
                g_rows[:, :128].reshape(D_MODEL, D_MODEL), g_cols[:, 0:128].reshape(D_MODEL, 512),
                g_cols[:, 128:256].reshape(D_MODEL, 512), g_cols[:, 256:384].reshape(D_MODEL, 512),
                g_rows[:, 128:].reshape(D_MODEL, D_MODEL))

    rest_state, rest_sibling, w_in_a, w_in_b = [], [], [], []

    def reduce_start(phase, grads, after=None):
        if phase == "rest_sibling":
            gmb_t = jnp.pad(grads["merge_bias"].reshape(3, N_DEV, 128).transpose(1, 0, 2), ((0, 0), (0, 5), (0, 0)))
            terms = [grads["w_kv"].reshape(N_DEV, 128, D_MODEL), grads["w_out"].reshape(N_DEV, 128, D_MODEL),
                     grads["wt_a"].reshape(N_DEV, 128, 512), grads["wt_b"].reshape(N_DEV, 128, 512),
                     grads["wt_c"].reshape(N_DEV, 128, 512), gmb_t]
            lands = [jax.ShapeDtypeStruct((4,) + t.shape[1:], t.dtype) for t in terms]
            started = _exchange_start("exchange_sibling_start_rest", "sibling", terms, lands)
            rest_sibling.extend(started[:4])
            return started[4]
        if phase == "rest_chips":
            s_send, s_recv, terms, lands = rest_sibling
            terms, recv1 = _exchange_wait("exchange_sibling_wait_rest", "sibling", s_send, s_recv, terms, lands, after)
            state, token = _reduce_scatter_start("rest", terms, recv1)
            rest_state.append(state)
            return token
        if phase == "w_in":
            own, sibling = [a.reshape(N_DEV, SHARD_IN, D_MODEL) for a in grads["wt_in"]]
            sums = _add_sibling("add_sibling_w_in", own, sibling, SHARD_IN)
            lands = [jax.ShapeDtypeStruct((4, SHARD_IN // 2, D_MODEL), BF16)]
            w_in_a.extend(_exchange_start("rs_a_start", "rs_a", [sums], lands))
            return w_in_a[4]
        (sums,), (landed_a,) = _exchange_wait("rs_a_wait", "rs_a", w_in_a[0], w_in_a[1], w_in_a[2], w_in_a[3], after)
        lands = [jax.ShapeDtypeStruct((2, SHARD_IN // 2, D_MODEL), BF16)]
        w_in_b.extend(_exchange_start("rs_b_start", "rs_b", [_presum_halves(sums, landed_a)], lands))
        w_in_b.extend([sums, landed_a])
        return w_in_b[4]

    loss_term, grad_x, grads = _local_step(
        x[0], hst, parts, tabs, bias, mem[0], loss_target[0], pre_norm, mem_norm, post_norm, wt_in, late_weights,
        reduce_start=reduce_start)

    replicated = ("pre_norm", "mem_norm", "post_norm", "na_rpb")
    pieces = [_wide_rows(grads[n]) for n in replicated] + [_wide_rows(loss_term)]
    first_rows = [sum(p.shape[0] for p in pieces[:i]) for i in range(len(pieces))]
    small = jnp.concatenate(pieces, axis=0)
    s_send, s_recv, s_own, s_land, s_token = _exchange_start(
        "gather_small_start", "gather", [small], [jax.ShapeDtypeStruct((N_DEV,) + small.shape, F32)])
    grad = {}
    weights = {
        "pre_norm": (pre_norm, m_pre_norm, v_pre_norm), "w_in": (w_in, m_w_in, v_w_in),
        "merge_bias": (merge_bias, m_merge_bias, v_merge_bias), "na_rpb": (na_rpb, m_na_rpb, v_na_rpb),
        "mem_norm": (mem_norm, m_mem_norm, v_mem_norm), "w_mem_kv": (w_mem_kv, m_w_mem_kv, v_w_mem_kv),
        "w_branch_a": (w_branch_a, m_w_branch_a, v_w_branch_a), "w_branch_b": (w_branch_b, m_w_branch_b, v_w_branch_b),
        "w_branch_c": (w_branch_c, m_w_branch_c, v_w_branch_c), "w_out": (w_out, m_w_out, v_w_out),
        "post_norm": (post_norm, m_post_norm, v_post_norm)}
    order = ["pre_norm", "w_in", "merge_bias", "na_rpb", "mem_norm", "w_mem_kv", "w_branch_a", "w_branch_b",
             "w_branch_c", "w_out", "post_norm"]
    delta, new_m, new_v = {}, {}, {}

    sums, recv2 = _reduce_scatter_wait(rest_state[0], s_token)
    rest = (("w_mem_kv", False), ("w_out", False), ("w_branch_a", True), ("w_branch_b", True), ("w_branch_c", True),
            ("merge_bias", False))
    items = [(sums[i], recv2[i], *[a[0] for a in weights[n]], transposed) for i, (n, transposed) in enumerate(rest)]
    for (n, _), (g, dl, nm, nv) in zip(rest, _adamw_chips_small("adamw_rest", items)):
        grad[n], delta[n], new_m[n], new_v[n] = g[None], dl[None], nm[None], nv[None]
    s_own, s_land = _exchange_wait("gather_small_wait", "gather", s_send, s_recv, s_own, s_land, delta["w_out"])
    items = [(first, *[_wide_rows(a) for a in weights[n]]) for n, first in zip(replicated, first_rows)]
    total, updates = _adamw_replicated(lax.dynamic_update_slice(s_land[0], s_own[0][None], (me, 0, 0)), items)
    loss = total[first_rows[-1], 0]
    for n, first, it, (dl, nm, nv) in zip(replicated, first_rows, items, updates):
        w = weights[n][0]
        grad[n], delta[n], new_m[n], new_v[n] = [
            a.reshape(-1)[:w.size].reshape(w.shape) for a in (total[first:first + it[1].shape[0]], dl, nm, nv)]
    _, (landed_b,) = _exchange_wait("rs_b_wait", "rs_b", w_in_b[0], w_in_b[1], w_in_b[2], w_in_b[3], updates[-1][0])
    g, dl, nm, nv = _adamw_halves("adamw_w_in", w_in_b[5], w_in_b[6], landed_b, w_in[0], m_w_in[0], v_w_in[0], 256)
    grad["w_in"], delta["w_in"], new_m["w_in"], new_v["w_in"] = g[None], dl[None], nm[None], nv[None]

    return (loss, grad_x[None], *[grad[n] for n in order], *[delta[n] for n in order],
            *[new_m[n] for n in order], *[new_v[n] for n in order])
```

```python
import functools

import numpy as np
import jax
import jax.numpy as jnp
from jax import lax
from jax.experimental import pallas as pl
from jax.experimental.pallas import tpu as pltpu

F32 = jnp.float32
BF16 = jnp.bfloat16

SEQ = 2048
D_MODEL = 1024
N_IN = 11264
N_DEV = 8
SHARD_IN = N_IN // N_DEV
HEAD_DIM = 64
GRID_W = 64
NA_ROWS = 8
MEM_LEN = 256
DILATIONS = (1, 4, 16)
REACH = 64
ROPE_THETA = 500000.0
ROPE_DIM = 16
EPS = 1e-6
NEG = -1e30
ADAM_LR = 0.001
ADAM_B1 = 0.9
ADAM_B2 = 0.999
ADAM_EPS = 1e-08
ADAM_WD = 0.01
ADAM_STEP = 10

VMEM_LIMIT_BYTES = 56 * 1024 * 1024
MESH_ID = pl.DeviceIdType.MESH

NN = (((1,), (0,)), ((), ()))
NT = (((1,), (1,)), ((), ()))
TN = (((0,), (0,)), ((), ()))


def _params(sem=None):
    return pltpu.CompilerParams(dimension_semantics=sem, vmem_limit_bytes=VMEM_LIMIT_BYTES)


def _iota(shape, dim):
    return lax.broadcasted_iota(jnp.int32, shape, dim)


def _sigmoid(x):
    return 1.0 / (1.0 + jnp.exp(-x))


def _rope_tables():
    half = ROPE_DIM // 2
    inv = (ROPE_THETA ** (-np.arange(half, dtype=np.float64) * 2.0 / ROPE_DIM)).astype(np.float32)
    pos = np.arange(SEQ, dtype=np.float32)
    ang = pos[:, None] * inv[None, :]
    cos, sin = np.cos(ang), np.sin(ang)
    zeros = np.zeros_like(cos)
    rest = HEAD_DIM - ROPE_DIM
    c64 = np.concatenate([cos, cos, np.ones((SEQ, rest), np.float32)], axis=1)
    s1 = np.concatenate([zeros, sin, np.zeros((SEQ, rest), np.float32)], axis=1)
    s2 = np.concatenate([-sin, zeros, np.zeros((SEQ, rest), np.float32)], axis=1)

    def fold(t, d):
        return t.reshape(SEQ // d, d, t.shape[1]).transpose(1, 0, 2).reshape(SEQ, t.shape[1])

    tabs = [np.stack([np.tile(fold(t, d), (1, 2)) for t in (c64, s1, s2)], axis=0) for d in DILATIONS]
    return jnp.asarray(np.stack(tabs, axis=0), dtype=F32)


def _rope(a, c, s1, s2):
    return a * c + pltpu.roll(a, 8, 1) * s1 + pltpu.roll(a, 120, 1) * s2


def _rope_t(a, c, s1, s2):
    return a * c + pltpu.roll(a * s1, 120, 1) + pltpu.roll(a * s2, 8, 1)


def _perm_of_block(j):
    return jnp.where(j < 3, 0, jnp.where(j < 6, 1, jnp.where(j < 9, 2, 0)))


def _mm(name, a, b, out_shape, out_dtype, grid, a_spec, b_spec, o_spec, acc_shape, dims, k_axis, nk):
    def body(a_ref, b_ref, o_ref, acc_ref):
        k = pl.program_id(k_axis)

        @pl.when(k == 0)
        def _():
            acc_ref[...] = jnp.zeros(acc_shape, F32)

        acc_ref[...] += lax.dot_general(a_ref[...], b_ref[...], dims, preferred_element_type=F32)

        @pl.when(k == nk - 1)
        def _():
            o_ref[...] = acc_ref[...].astype(out_dtype)

    sem = tuple("arbitrary" if ax == k_axis else "parallel" for ax in range(len(grid)))
    return pl.pallas_call(
        body, name=name, grid=grid, in_specs=[a_spec, b_spec], out_specs=o_spec,
        out_shape=jax.ShapeDtypeStruct(out_shape, out_dtype),
        scratch_shapes=[pltpu.VMEM(acc_shape, F32)], compiler_params=_params(sem))(a, b)


def _mm_simple(name, a, b, dims, out_dtype, tm, tn, tk):
    if dims is NN:
        m, kk = a.shape
        n = b.shape[1]
        a_spec = pl.BlockSpec((tm, tk), lambda i, j, k: (i, k))
        b_spec = pl.BlockSpec((tk, tn), lambda i, j, k: (k, j))
    elif dims is NT:
        m, kk = a.shape
        n = b.shape[0]
        a_spec = pl.BlockSpec((tm, tk), lambda i, j, k: (i, k))
        b_spec = pl.BlockSpec((tn, tk), lambda i, j, k: (j, k))
    else:
        kk, m = a.shape
        n = b.shape[1]
        a_spec = pl.BlockSpec((tk, tm), lambda i, j, k: (k, i))
        b_spec = pl.BlockSpec((tk, tn), lambda i, j, k: (k, j))
    grid = (m // tm, n // tn, kk // tk)
    o_spec = pl.BlockSpec((tm, tn), lambda i, j, k: (i, j))
    return _mm(name, a, b, (m, n), out_dtype, grid, a_spec, b_spec, o_spec, (tm, tn), dims, 2, kk // tk)


def _rmsnorm_fwd(name, x, gain, rows):
    n, d = x.shape

    def body(x_ref, g_ref, o_ref):
        xv = x_ref[...]
        rstd = lax.rsqrt(jnp.mean(xv * xv, axis=1, keepdims=True) + EPS)
        o_ref[...] = (xv * rstd * g_ref[...]).astype(BF16)

    return pl.pallas_call(
        body, name=name, grid=(n // rows,),
        in_specs=[pl.BlockSpec((rows, d), lambda i: (i, 0)), pl.BlockSpec((1, d), lambda i: (0, 0))],
        out_specs=pl.BlockSpec((rows, d), lambda i: (i, 0)),
        out_shape=jax.ShapeDtypeStruct((n, d), BF16), compiler_params=_params(("parallel",)))(x, gain)


def _folded_rows(first, rows, d):
    if d == 1:
        return pl.ds(pl.multiple_of(first, rows), rows)
    mlen = SEQ // d
    return pl.ds((first % mlen) * d + first // mlen, rows, stride=d)


def _prenorm_fold(x, gain, dep=None):
    rows = 128
    nchunk = D_MODEL // 128
    dep_specs, dep_args = _dep_operand(dep)

    def body(*refs):
        x_refs, g_ref, hs_ref, hst_ref = refs[:nchunk], refs[nchunk], refs[-2], refs[-1]
        first = pl.program_id(0) * rows
        for p, d in enumerate(DILATIONS):
            idx = _folded_rows(first, rows, d)
            xv = jnp.concatenate([r[idx, :] for r in x_refs], axis=1)
            rstd = lax.rsqrt(jnp.mean(xv * xv, axis=1, keepdims=True) + EPS)
            h = xv * rstd * g_ref[...]
            hs_ref[p] = h.astype(BF16)
            hst_ref[p] = h.T.astype(BF16)

    x_specs = [pl.BlockSpec((SEQ, 128), functools.partial(lambda c, i: (0, c), c)) for c in range(nchunk)]
    return pl.pallas_call(
        body, name="prenorm", grid=(SEQ // rows,),
        in_specs=x_specs + [pl.BlockSpec((1, D_MODEL), lambda i: (0, 0))] + dep_specs,
        out_specs=[pl.BlockSpec((3, rows, D_MODEL), lambda i: (0, i, 0)),
                   pl.BlockSpec((3, D_MODEL, rows), lambda i: (0, 0, i))],
        out_shape=[jax.ShapeDtypeStruct((3, SEQ, D_MODEL), BF16), jax.ShapeDtypeStruct((3, D_MODEL, SEQ), BF16)],
        compiler_params=_params(("parallel",)))(*([x] * nchunk), gain, *dep_args)


def _prenorm_bwd(x, gain, dh, dout):
    rows = 512

    def body(x_ref, g_ref, a_ref, do_ref, dx_ref, gg_ref):
        xv = x_ref[...]
        rstd = lax.rsqrt(jnp.mean(xv * xv, axis=1, keepdims=True) + EPS)
        xn = xv * rstd
        dh = jnp.concatenate([a_ref[c] for c in range(D_MODEL // 128)], axis=1)
        gdh = dh * g_ref[...]
        dx_ref[...] = rstd * (gdh - xn * jnp.mean(gdh * xn, axis=1, keepdims=True)) + do_ref[...]

        @pl.when(pl.program_id(0) == 0)
        def _():
            gg_ref[...] = jnp.zeros((1, D_MODEL), F32)

        gg_ref[...] += jnp.sum(dh * xn, axis=0, keepdims=True)

    row = pl.BlockSpec((rows, D_MODEL), lambda i: (i, 0))
    vec = pl.BlockSpec((1, D_MODEL), lambda i: (0, 0))
    return pl.pallas_call(
        body, name="prenorm_bwd", grid=(SEQ // rows,),
        in_specs=[row, vec, pl.BlockSpec((D_MODEL // 128, rows, 128), lambda i: (0, i, 0)), row], out_specs=[row, vec],
        out_shape=[jax.ShapeDtypeStruct((SEQ, D_MODEL), F32), jax.ShapeDtypeStruct((1, D_MODEL), F32)],
        compiler_params=_params(("arbitrary",)))(x, gain, dh, dout)


def _memnorm_bwd(mem, dmemn, dep=None):
    dep_specs, dep_args = _dep_operand(dep)

    def body(m_ref, d_ref, *rest):
        mv = m_ref[...]
        rstd = lax.rsqrt(jnp.mean(mv * mv, axis=1, keepdims=True) + EPS)
        rest[-1][...] = jnp.sum(d_ref[...] * mv * rstd, axis=0, keepdims=True)

    whole = pl.BlockSpec(memory_space=pltpu.VMEM)
    return pl.pallas_call(
        body, name="memnorm_bwd", in_specs=[whole, whole] + dep_specs,
        out_shape=jax.ShapeDtypeStruct((1, D_MODEL), F32), compiler_params=_params())(mem, dmemn, *dep_args)


def _dep_operand(dep):
    return ([], []) if dep is None else ([pl.BlockSpec(memory_space=pl.ANY)], [dep])


def _in_proj(name, hs, wt, tabs, order, prev=None, dep=None):
    tm, tn = 512, 512
    prev_specs, prev_args = ([], []) if prev is None else ([ANY], [prev])
    dep_specs, dep_args = _dep_operand(dep)

    def body(order_ref, h_ref, w_ref, t_ref, *rest):
        o_ref = rest[-1]
        j = order_ref[pl.program_id(0)]
        is_rope = jnp.logical_and(j < 9, j % 3 != 2)
        row_slices = [slice(r * tm, (r + 1) * tm) for r in range(SEQ // tm)]

        def product(rs):
            return lax.dot_general(h_ref[rs, :], w_ref[...], NT, preferred_element_type=F32)

        @pl.when(is_rope)
        def _():
            for rs in row_slices:
                acc = product(rs)
                c, s1, s2 = t_ref[0, rs, :], t_ref[1, rs, :], t_ref[2, rs, :]
                for q in range(tn // 128):
                    a = acc[:, q * 128:(q + 1) * 128]
                    o_ref[rs, q * 128:(q + 1) * 128] = _rope(a, c, s1, s2).astype(BF16)

        @pl.when(jnp.logical_not(is_rope))
        def _():
            for rs in row_slices:
                o_ref[rs, :] = product(rs).astype(BF16)

    grid_spec = pltpu.PrefetchScalarGridSpec(
        num_scalar_prefetch=1, grid=(order.shape[0],),
        in_specs=[pl.BlockSpec((None, SEQ, D_MODEL), lambda t, o: (_perm_of_block(o[t]), 0, 0)),
                  pl.BlockSpec((tn, D_MODEL), lambda t, o: (o[t], 0)),
                  pl.BlockSpec((None, 3, SEQ, 128), lambda t, o: (_perm_of_block(o[t]), 0, 0, 0))] + prev_specs
        + dep_specs,
        out_specs=pl.BlockSpec((SEQ, tn), lambda t, o: (0, o[t])))
    return pl.pallas_call(
        body, name=name, grid_spec=grid_spec, out_shape=jax.ShapeDtypeStruct((SEQ, N_IN), BF16),
        input_output_aliases={} if prev is None else {4: 0},
        compiler_params=_params(("arbitrary",)))(order, hs, wt, tabs, *prev_args, *dep_args)


def _piece_blocks(pieces):
    return [(a, h * 512) for a, p in enumerate(pieces) for h in range(p.shape[1] // 512)]


def _block_fetch(piece_refs, blocks, buf, sem):
    def start(block, slot):
        for b, (a, col) in enumerate(blocks):
            @pl.when(block == b)
            def _():
                pltpu.make_async_copy(piece_refs[a].at[:, pl.ds(col, 512)], buf.at[slot], sem.at[slot]).start()

    def wait(slot):
        pltpu.make_async_copy(piece_refs[0].at[:, pl.ds(0, 512)], buf.at[slot], sem.at[slot]).wait()

    return start, wait


def _in_proj_dw(pieces, hst, dep=None):
    tn = 512
    blocks = _piece_blocks(pieces)
    nblk = len(blocks)
    npc = len(pieces)
    dep_specs, dep_args = _dep_operand(dep)

    def body(h_ref, *rest):
        piece_refs = rest[:npc]
        own_out, mirror, buf, sem, out_buf, send_sems, recv_sem, local_sems = rest[-8:]
        j = pl.program_id(0)
        slot = j % 2
        start, wait = _block_fetch(piece_refs, blocks, buf, sem)
        x, y, c = _place()

        def rows_of(step):
            return pl.ds(pl.multiple_of(step * tn, tn), tn)

        def to_sibling(step, slot_):
            return pltpu.make_async_remote_copy(
                src_ref=out_buf.at[slot_], dst_ref=mirror.at[rows_of(step)],
                send_sem=send_sems.at[slot_], recv_sem=recv_sem, device_id=(x, y, 1 - c), device_id_type=MESH_ID)

        def to_own(step, slot_):
            return pltpu.make_async_copy(out_buf.at[slot_], own_out.at[rows_of(step)], local_sems.at[slot_])

        @pl.when(j == 0)
        def _():
            start(j, slot)

        wait(slot)

        @pl.when(j + 1 < nblk)
        def _():
            start(j + 1, 1 - slot)

        acc = jnp.dot(h_ref[...], buf[slot], preferred_element_type=F32)

        @pl.when(j >= 2)
        def _():
            to_sibling(j - 2, slot).wait_send()
            to_own(j - 2, slot).wait()

        out_buf[slot] = acc.T.astype(BF16)
        to_sibling(j, slot).start()
        to_own(j, slot).start()

        @pl.when(j == nblk - 1)
        def _():
            to_sibling(j - 1, 1 - slot).wait_send()
            to_own(j - 1, 1 - slot).wait()
            to_sibling(j, slot).wait_send()
            to_own(j, slot).wait()
            pltpu.make_async_remote_copy(src_ref=mirror, dst_ref=mirror, send_sem=send_sems.at[0], recv_sem=recv_sem,
                                         device_id=(x, y, 1 - c), device_id_type=MESH_ID).wait_recv()

    return pl.pallas_call(
        body, name="in_proj_dw", grid=(nblk,),
        in_specs=[pl.BlockSpec((None, D_MODEL, SEQ), lambda j: (_perm_of_block(j), 0, 0))] + [ANY] * npc + dep_specs,
        out_specs=[ANY, ANY],
        out_shape=[jax.ShapeDtypeStruct((N_IN, D_MODEL), BF16), jax.ShapeDtypeStruct((N_IN, D_MODEL), BF16)],
        scratch_shapes=[pltpu.VMEM((2, SEQ, tn), BF16), pltpu.SemaphoreType.DMA((2,)),
                        pltpu.VMEM((2, tn, D_MODEL), BF16), pltpu.SemaphoreType.DMA((2,)), pltpu.SemaphoreType.DMA,
                        pltpu.SemaphoreType.DMA((2,))],
        compiler_params=_params(("arbitrary",)))(hst, *pieces, *dep_args)


def _in_proj_dh(pieces, wt, dep=None):
    tk = 512
    blocks = _piece_blocks(pieces)
    nblk = len(blocks)
    npc = len(pieces)
    nchunk = D_MODEL // 128

    def col(s):
        return jnp.where(s < 3, s, jnp.where(s < 16, s + 6, s - 13))

    dep_specs, dep_args = _dep_operand(dep)

    def body(w_ref, *rest):
        piece_refs = rest[:npc]
        o_ref, acc_ref, buf, sem = rest[-4:]
        s = pl.program_id(0)
        slot = s % 2
        start, wait = _block_fetch(piece_refs, blocks, buf, sem)

        @pl.when(s == 0)
        def _():
            start(col(s), slot)

        wait(slot)

        @pl.when(s + 1 < nblk)
        def _():
            start(col(s + 1), 1 - slot)

        row_slices = [slice(r * 512, (r + 1) * 512) for r in range(SEQ // 512)]

        def product(rs):
            return jnp.dot(buf[slot, rs, :], w_ref[...], preferred_element_type=F32)

        def accumulate(cond, to_out, init):
            @pl.when(cond)
            def _():
                for rs in row_slices:
                    prod = product(rs)
                    if not to_out:
                        if init:
                            acc_ref[rs, :] = prod
                        else:
                            acc_ref[rs, :] += prod
                        continue
                    for c in range(nchunk):
                        if init:
                            o_ref[c, rs, :] = prod[:, c * 128:(c + 1) * 128]
                        else:
                            o_ref[c, rs, :] += prod[:, c * 128:(c + 1) * 128]

        accumulate(s == 0, True, True)
        accumulate(jnp.logical_and(s > 0, s < 16), True, False)
        accumulate(jnp.logical_or(s == 16, s == 19), False, True)
        accumulate(jnp.logical_and(s > 16, s != 19), False, False)
        for last, d in ((18, 4), (21, 16)):
            @pl.when(s == last)
            def _():
                mlen = SEQ // d
                for r in range(d):
                    for c in range(nchunk):
                        o_ref[c, pl.ds(r, mlen, stride=d), :] += acc_ref[r * mlen:(r + 1) * mlen,
                                                                         c * 128:(c + 1) * 128]

    return pl.pallas_call(
        body, name="in_proj_dh", grid=(nblk,),
        in_specs=[pl.BlockSpec((tk, D_MODEL), lambda s: (col(s), 0))] + [ANY] * npc + dep_specs,
        out_specs=pl.BlockSpec((nchunk, SEQ, 128), lambda s: (0, 0, 0)),
        out_shape=jax.ShapeDtypeStruct((nchunk, SEQ, 128), F32),
        scratch_shapes=[pltpu.VMEM((SEQ, D_MODEL), F32), pltpu.VMEM((2, SEQ, tk), BF16),
                        pltpu.SemaphoreType.DMA((2,))],
        compiler_params=_params(("arbitrary",)))(wt, *pieces, *dep_args)


def _head_lanes(lanes, hh):
    return lanes >= 64 if hh == 1 else lanes < 64


def _head_rows(x, lanes, hh, pair):
    if not pair:
        return jnp.max(x, axis=1, keepdims=True)
    return jnp.max(jnp.where(_head_lanes(lanes, hh), x, -jnp.inf), axis=1, keepdims=True)


def _mask_head(x, lanes, hh, pair, scale=1.0):
    if not pair:
        return x
    xf = x.astype(F32) if scale == 1.0 else x.astype(F32) * scale
    return jnp.where(_head_lanes(lanes, hh), xf, 0.0).astype(BF16)


def _window(mode, qi, tq, mlen, tk):
    if mode == "dil":
        q0 = qi * tq
        seg = (q0 // mlen) * mlen
        ks = jnp.clip(q0 - REACH, seg, seg + mlen - tk)
        return pl.multiple_of(ks, 64)
    if mode == "na":
        r_start = jnp.clip(qi - NA_ROWS // 2, 0, SEQ // GRID_W - NA_ROWS)
        return pl.multiple_of(r_start * GRID_W, 64)
    return 0


def _band_mask(qi, tq, tk, ks):
    qpos = qi * tq + _iota((tq, tk), 0)
    kpos = ks + _iota((tq, tk), 1)
    return jnp.where(jnp.abs(qpos - kpos) <= REACH, 0.0, NEG).astype(F32)


def _stack_heads(x, lanes, pair, scale=1.0):
    if not pair:
        return x
    return jnp.concatenate([_mask_head(x, lanes, hh, pair, scale) for hh in range(2)], axis=0)


def _stack_rows(x, lanes, pair):
    if not pair:
        return _head_rows(x, lanes, 0, pair)
    return jnp.concatenate([_head_rows(x, lanes, hh, pair) for hh in range(2)], axis=0)


def _unstack_heads(x, lanes, pair, tq):
    if not pair:
        return x
    return jnp.where(lanes < 64, x[:tq], x[tq:])


def _scores(mode, qst, k, sscale, band, qi, bias_ref, pair):
    s = lax.dot_general(qst, k, NT, preferred_element_type=F32)
    if sscale != 1.0:
        s = s * sscale
    if mode == "dil":
        s = s + jnp.concatenate([band, band], axis=0)
    elif mode == "na":
        off = qi - jnp.clip(qi - NA_ROWS // 2, 0, SEQ // GRID_W - NA_ROWS)
        s = s + jnp.concatenate([bias_ref[0, off], bias_ref[1, off]], axis=0)
    return s


def _attn_cfg(mode, d):
    if mode == "dil":
        mlen = SEQ // d
        return dict(pair=True, tq=128, tk=min(256, mlen), mlen=mlen, lk=SEQ, scale=HEAD_DIM ** -0.5, units=4,
                    nsub=ATTN_SUBTILES)
    if mode == "na":
        return dict(pair=True, tq=GRID_W, tk=NA_ROWS * GRID_W, mlen=SEQ, lk=SEQ, scale=HEAD_DIM ** -0.5, units=4,
                    nsub=2 * ATTN_SUBTILES)
    return dict(pair=False, tq=128, tk=MEM_LEN, mlen=SEQ, lk=MEM_LEN, scale=128 ** -0.5, units=4,
                nsub=ATTN_SUBTILES)


ATTN_SUBTILES = 16


def _attn_fwd(name, mode, q_arr, k_arr, v_arr, qcol, kcol, vcol, d=1, bias=None):
    cfg = _attn_cfg(mode, d)
    pair, tq, tk, mlen, lk, scale = cfg["pair"], cfg["tq"], cfg["tk"], cfg["mlen"], cfg["lk"], cfg["scale"]
    qscale, sscale = (scale, 1.0) if pair else (1.0, scale)
    nsub = cfg["nsub"]
    rows = nsub * tq

    def body(*refs):
        if mode == "na":
            q_ref, k_ref, v_ref, bias_ref, o_ref, l_ref = refs
        else:
            q_ref, k_ref, v_ref, o_ref, l_ref = refs
            bias_ref = None
        lanes = _iota((tq, 128), 1)
        qis = [pl.program_id(1) * nsub + sub for sub in range(nsub)]
        kss = [_window(mode, qi, tq, mlen, tk) for qi in qis]
        vs = [v_ref[pl.ds(ks, tk), :] for ks in kss]
        bands = [_band_mask(qi, tq, tk, ks) if mode == "dil" else None for qi, ks in zip(qis, kss)]
        ss = []
        for sub in range(nsub):
            qst = _stack_heads(q_ref[sub * tq:(sub + 1) * tq, :], lanes, pair, qscale)
            k = k_ref[pl.ds(kss[sub], tk), :]
            ss.append(_scores(mode, qst, k, sscale, bands[sub], qis[sub], bias_ref, pair))
        ms = [jnp.max(s_, axis=1, keepdims=True) for s_ in ss]
        ps = [jnp.exp(s_ - m) for s_, m in zip(ss, ms)]
        ls = [jnp.sum(p, axis=1, keepdims=True) for p in ps]
        os_ = [jnp.dot(p.astype(BF16), v, preferred_element_type=F32) for p, v in zip(ps, vs)]
        for sub in range(nsub):
            out = _unstack_heads(os_[sub] / ls[sub], lanes, pair, tq)
            lse = ms[sub] + jnp.log(ls[sub])
            lse = _unstack_heads(jnp.broadcast_to(lse, (lse.shape[0], 128)), lanes, pair, tq)
            dst = _folded_rows(qis[sub] * tq, tq, d) if mode == "dil" else slice(sub * tq, (sub + 1) * tq)
            o_ref[dst, :] = out
            l_ref[dst, :] = lse

    in_specs = [pl.BlockSpec((rows, 128), lambda u, i: (i, qcol + u)),
                pl.BlockSpec((lk, 128), lambda u, i: (0, kcol + u)),
                pl.BlockSpec((lk, 128), lambda u, i: (0, vcol + u))]
    args = [q_arr, k_arr, v_arr]
    if mode == "na":
        in_specs.append(pl.BlockSpec((2, NA_ROWS, GRID_W, NA_ROWS * GRID_W), lambda u, i: (u, 0, 0, 0)))
        args.append(bias)
    if mode == "dil":
        out_spec = pl.BlockSpec((SEQ, 128), lambda u, i: (0, u))
    else:
        out_spec = pl.BlockSpec((rows, 128), lambda u, i: (i, u))
    return pl.pallas_call(
        body, name=name, grid=(cfg["units"], SEQ // rows), in_specs=in_specs, out_specs=[out_spec, out_spec],
        out_shape=[jax.ShapeDtypeStruct((SEQ, 512), F32), jax.ShapeDtypeStruct((SEQ, 512), F32)],
        compiler_params=_params(("parallel", "arbitrary")))(*args)


def _attn_bwd(name, mode, q_arr, k_arr, v_arr, qcol, kcol, vcol, do, lse, dp=None, o=None, d=1, bias=None,
              tabs=None, dep=None):
    cfg = _attn_cfg(mode, d)
    pair, tq, tk, mlen, lk, scale = cfg["pair"], cfg["tq"], cfg["tk"], cfg["mlen"], cfg["lk"], cfg["scale"]
    qscale, sscale = (scale, 1.0) if pair else (1.0, scale)
    nsub = cfg["nsub"]
    rows = nsub * tq
    nq = SEQ // rows
    kv_dtype = F32 if mode == "mem" else BF16
    dep_specs, dep_args = _dep_operand(dep)
    mode_inputs = {"dil": 3, "na": 2, "mem": 1}[mode]

    def body(*refs):
        refs = list(refs)
        q_ref, k_ref, v_ref, do_ref, l_ref = refs[:5]
        rest = refs[5:5 + mode_inputs] + refs[5 + mode_inputs + len(dep_args):]
        bias_ref = tq_ref = tk_ref = db_ref = None
        if mode == "dil":
            dp_ref, tq_ref, tk_ref, dq_ref, dk_ref, dv_ref, dk_acc, dv_acc = rest
        elif mode == "na":
            o_ref, bias_ref, dq_ref, dk_ref, dv_ref, db_ref, dk_acc, dv_acc = rest
        else:
            o_ref, dq_ref, dk_ref, dv_ref, dk_acc, dv_acc = rest
        step = pl.program_id(1)

        @pl.when(step == 0)
        def _():
            dk_acc[...] = jnp.zeros((lk, 128), F32)
            dv_acc[...] = jnp.zeros((lk, 128), F32)
            if mode == "na":
                db_ref[...] = jnp.zeros(db_ref.shape, F32)

        lanes = _iota((tq, 128), 1)
        qis = [step * nsub + sub for sub in range(nsub)]
        sls = [slice(sub * tq, (sub + 1) * tq) for sub in range(nsub)]
        kss = [_window(mode, qi, tq, mlen, tk) for qi in qis]
        ks_ = [k_ref[pl.ds(ks, tk), :] for ks in kss]
        vs = [v_ref[pl.ds(ks, tk), :] for ks in kss]
        qsts, dosts, lses, dphs = [], [], [], []
        for sub in range(nsub):
            if mode == "dil":
                src = _folded_rows(qis[sub] * tq, tq, d)
                dov = do_ref[src, :].astype(BF16)
                lsev = l_ref[src, :]
                dphs.append(_stack_rows(dp_ref[src, :], lanes, pair))
            else:
                dov = do_ref[sls[sub], :]
                lsev = l_ref[sls[sub], :]
                dpv = dov.astype(F32) * o_ref[sls[sub], :]
                if pair:
                    dphs.append(jnp.concatenate(
                        [jnp.sum(jnp.where(_head_lanes(lanes, hh), dpv, 0.0), axis=1, keepdims=True)
                         for hh in range(2)], axis=0))
                else:
                    dphs.append(jnp.sum(dpv, axis=1, keepdims=True))
            qsts.append(_stack_heads(q_ref[sls[sub], :], lanes, pair, qscale))
            dosts.append(_stack_heads(dov, lanes, pair))
            lses.append(_stack_rows(lsev, lanes, pair))
        bands = [_band_mask(qi, tq, tk, ks) if mode == "dil" else None for qi, ks in zip(qis, kss)]
        ss = [_scores(mode, qsts[sub], ks_[sub], sscale, bands[sub], qis[sub], bias_ref, pair) for sub in range(nsub)]
        dpms = [lax.dot_general(dosts[sub], vs[sub], NT, preferred_element_type=F32) for sub in range(nsub)]
        ps = [jnp.exp(s_ - lse) for s_, lse in zip(ss, lses)]
        dss = [p * (dpm - dph) for p, dpm, dph in zip(ps, dpms, dphs)]
        if mode == "na":
            for sub, ds in enumerate(dss):
                off = qis[sub] - jnp.clip(qis[sub] - NA_ROWS // 2, 0, SEQ // GRID_W - NA_ROWS)
                db_ref[0, off] += ds[:tq]
                db_ref[1, off] += ds[tq:]
        dsbs = [ds.astype(BF16) for ds in dss]
        dvs = [lax.dot_general(p.astype(BF16), dosts[sub], TN, preferred_element_type=F32)
               for sub, p in enumerate(ps)]
        dqs = [jnp.dot(dsb, ks_[sub], preferred_element_type=F32) * scale for sub, dsb in enumerate(dsbs)]
        dks = [lax.dot_general(dsb, qsts[sub], TN, preferred_element_type=F32) for sub, dsb in enumerate(dsbs)]
        for sub in range(nsub):
            sl = sls[sub]
            dq = _unstack_heads(dqs[sub], lanes, pair, tq)
            if mode == "dil":
                dq = _rope_t(dq, tq_ref[0, sl, :], tq_ref[1, sl, :], tq_ref[2, sl, :])
            dq_ref[sl, :] = dq.astype(BF16)
            dk_acc[pl.ds(kss[sub], tk), :] += dks[sub] if pair else dks[sub] * scale
            dv_acc[pl.ds(kss[sub], tk), :] += dvs[sub]

        @pl.when(step == nq - 1)
        def _():
            dkv = dk_acc[...]
            if mode == "dil":
                dkv = _rope_t(dkv, tk_ref[0], tk_ref[1], tk_ref[2])
            dk_ref[...] = dkv.astype(kv_dtype)
            dv_ref[...] = dv_acc[...].astype(kv_dtype)

    q_spec = pl.BlockSpec((rows, 128), lambda u, i: (i, qcol + u))
    row_spec = pl.BlockSpec((rows, 128), lambda u, i: (i, u))
    kv_out = pl.BlockSpec((lk, 128), lambda u, i: (0, u))
    whole = pl.BlockSpec((SEQ, 128), lambda u, i: (0, u))
    nat_spec = whole if mode == "dil" else row_spec
    in_specs = [q_spec,
                pl.BlockSpec((lk, 128), lambda u, i: (0, kcol + u)),
                pl.BlockSpec((lk, 128), lambda u, i: (0, vcol + u)),
                nat_spec, nat_spec]
    args = [q_arr, k_arr, v_arr, do, lse]
    out_specs = [row_spec, kv_out, kv_out]
    out_shape = [jax.ShapeDtypeStruct((SEQ, 512), BF16), jax.ShapeDtypeStruct((lk, 512), kv_dtype),
                 jax.ShapeDtypeStruct((lk, 512), kv_dtype)]
    if mode == "dil":
        in_specs += [whole, pl.BlockSpec((3, rows, 128), lambda u, i: (0, i, 0)),
                     pl.BlockSpec((3, SEQ, 128), lambda u, i: (0, 0, 0))]
        args += [dp, tabs, tabs]
    elif mode == "na":
        b_spec = pl.BlockSpec((2, NA_ROWS, GRID_W, NA_ROWS * GRID_W), lambda u, i: (u, 0, 0, 0))
        in_specs += [row_spec, b_spec]
        args += [o, bias]
        out_specs.append(b_spec)
        out_shape.append(jax.ShapeDtypeStruct((8, NA_ROWS, GRID_W, NA_ROWS * GRID_W), F32))
    else:
        in_specs.append(row_spec)
        args.append(o)
    return pl.pallas_call(
        body, name=name, grid=(cfg["units"], nq), in_specs=in_specs + dep_specs, out_specs=out_specs,
        out_shape=out_shape, scratch_shapes=[pltpu.VMEM((lk, 128), F32), pltpu.VMEM((lk, 128), F32)],
        compiler_params=_params(("parallel", "arbitrary")))(*args, *dep_args)


def _na_geometry():
    qc = _iota((GRID_W, 128), 0)
    lane = _iota((GRID_W, 128), 1)
    kc = lane & 63
    c_start = jnp.clip(qc - 8, 0, GRID_W - 16)
    valid = jnp.logical_and(kc >= c_start, kc < c_start + 16)
    return lane, valid


def _na_bias(rpb_rows, dep=None):
    dep_specs, dep_args = _dep_operand(dep)

    def body(r_ref, *rest):
        o_ref, t_ref = rest[-2:]
        lane, valid = _na_geometry()
        for dd in range(14):
            row_a = jnp.broadcast_to(r_ref[dd:dd + 1, :], (GRID_W, 128))
            row_b = jnp.broadcast_to(r_ref[dd + 1:dd + 2, :], (GRID_W, 128))
            both = jnp.where(lane < 64, row_a, pltpu.roll(row_b, 64, 1))
            t = pltpu.roll(both, 128 - 15, 1, stride=1, stride_axis=0)
            t_ref[dd] = jnp.where(valid, t, NEG)
        for off in range(NA_ROWS):
            for p in range(4):
                o_ref[off, :, p * 128:(p + 1) * 128] = t_ref[2 * p - off + 7]

    return pl.pallas_call(
        body, name="na_bias", grid=(8,),
        in_specs=[pl.BlockSpec((None, 16, 128), lambda h: (h, 0, 0))] + dep_specs,
        out_specs=pl.BlockSpec((None, NA_ROWS, GRID_W, NA_ROWS * GRID_W), lambda h: (h, 0, 0, 0)),
        out_shape=jax.ShapeDtypeStruct((8, NA_ROWS, GRID_W, NA_ROWS * GRID_W), F32),
        scratch_shapes=[pltpu.VMEM((14, GRID_W, 128), F32)],
        compiler_params=_params(("parallel",)))(rpb_rows, *dep_args)


def _na_bias_bwd(dbias, dep=None):
    dep_specs, dep_args = _dep_operand(dep)

    def body(d_ref, *rest):
        o_ref = rest[-1]
        lane, valid = _na_geometry()
        reverse = (_iota((GRID_W, GRID_W), 0) + _iota((GRID_W, GRID_W), 1) == GRID_W - 1).astype(F32)
        o_ref[...] = jnp.zeros((16, 128), F32)
        for dd in range(14):
            t = jnp.zeros((GRID_W, 128), F32)
            for off in range(NA_ROWS):
                for p in range(4):
                    if 2 * p - off + 7 == dd:
                        t = t + d_ref[off, :, p * 128:(p + 1) * 128]
            t = jnp.dot(reverse, jnp.where(valid, t, 0.0), precision=lax.Precision.HIGHEST,
                        preferred_element_type=F32)
            t = pltpu.roll(t, 128 - (GRID_W - 16), 1, stride=1, stride_axis=0)
            o_ref[dd:dd + 1, :] = jnp.sum(t, axis=0, keepdims=True)

    return pl.pallas_call(
        body, name="na_bias_bwd", grid=(8,),
        in_specs=[pl.BlockSpec((None, NA_ROWS, GRID_W, NA_ROWS * GRID_W), lambda h: (h, 0, 0, 0))] + dep_specs,
        out_specs=pl.BlockSpec((None, 16, 128), lambda h: (h, 0, 0)),
        out_shape=jax.ShapeDtypeStruct((8, 16, 128), F32),
        compiler_params=_params(("parallel",)))(dbias, *dep_args)


GATE_ROWS = 128


def _group_weights(l0, l1, l2):
    m = jnp.maximum(jnp.maximum(l0, l1), l2)
    e0, e1, e2 = jnp.exp(l0 - m), jnp.exp(l1 - m), jnp.exp(l2 - m)
    inv = 1.0 / (e0 + e1 + e2)
    return e0 * inv, e1 * inv, e2 * inv


def _gate_block(o_grp, l_grp, out_b, out_c, parts, x, target, merge_bias, wts, w_out, gain, head_sum):
    rows = GATE_ROWS
    r512 = pl.BlockSpec((rows, 512), lambda i: (i, 0))
    r1024 = pl.BlockSpec((rows, D_MODEL), lambda i: (i, 0))
    silu_cols = [pl.BlockSpec((rows, 512), functools.partial(lambda b, i: (i, b), 13 + b)) for b in range(3)]
    logit_cols = [pl.BlockSpec((rows, D_MODEL), functools.partial(lambda b, i: (i, b), 8 + b)) for b in range(3)]

    def body(o0, o1, o2, l0, l1, l2, ob, oc, ga, gb, gc, la, lb, lc, x_ref, t_ref, mb, wa, wb, wc, wo_ref, gn_ref,
             hs_ref, dout_ref, dla, dlb, dlc, dga, dgb, dgc, do0, do1, do2, dp0, dp1, dp2, dob, doc, err_ref, gg_ref,
             gmb, gwa, gwb, gwc, gwo, acc_a, acc_b, acc_c, acc_o):
        step = pl.program_id(0)
        ws = _group_weights(l0[...], l1[...], l2[...])
        out_a = ws[0] * o0[...] + ws[1] * o1[...] + ws[2] * o2[...]
        branches = ((out_a, ga, la, wa, acc_a, dla, dga), (ob[...], gb, lb, wb, acc_b, dlb, dgb),
                    (oc[...], gc, lc, wc, acc_c, dlc, dgc))

        @pl.when(step == 0)
        def _():
            for acc in (acc_a, acc_b, acc_c, acc_o):
                acc[...] = jnp.zeros(acc.shape, F32)
            err_ref[...] = jnp.zeros((1, D_MODEL), F32)
            gg_ref[...] = jnp.zeros((1, D_MODEL), F32)
            gmb[...] = jnp.zeros((3, D_MODEL), F32)

        y = jnp.zeros((rows, D_MODEL), F32)
        zs, gates, silus, dsilus, us = [], [], [], [], []
        for b, (ov, g_ref, l_ref, w_ref, _, _, _) in enumerate(branches):
            g = g_ref[...].astype(F32)
            sg = _sigmoid(g)
            silus.append(g * sg)
            dsilus.append(sg * (1.0 + g * (1.0 - sg)))
            us.append((ov * silus[b]).astype(BF16))
            zs.append(lax.dot_general(us[b], w_ref[...], NT, preferred_element_type=F32))
            gates.append(_sigmoid(l_ref[...].astype(F32) + mb[b:b + 1, :]))
            y = y + gates[b] * zs[b]
        yb = y.astype(BF16)
        y2 = jnp.dot(yb, wo_ref[...], preferred_element_type=F32)
        rstd = lax.rsqrt(jnp.mean(y2 * y2, axis=1, keepdims=True) + EPS)
        yn = y2 * rstd
        gv = gn_ref[...]
        err = x_ref[...] + yn * gv - t_ref[...]
        dout = err * (1.0 / D_MODEL)
        dout_ref[...] = dout
        dn = dout * gv
        dy2 = (rstd * (dn - yn * jnp.mean(dn * yn, axis=1, keepdims=True))).astype(BF16)
        acc_o[...] += lax.dot_general(yb, dy2, TN, preferred_element_type=F32)
        err_ref[...] += jnp.sum(err * err, axis=0, keepdims=True)
        gg_ref[...] += jnp.sum(dout * yn, axis=0, keepdims=True)
        dy = lax.dot_general(dy2, wo_ref[...], NT, preferred_element_type=F32)
        dos = []
        for b, (ov, _, _, w_ref, acc, dl_ref, dg_ref) in enumerate(branches):
            dl = dy * zs[b] * gates[b] * (1.0 - gates[b])
            dl_ref[...] = dl.astype(BF16)
            gmb[b:b + 1, :] += jnp.sum(dl, axis=0, keepdims=True)
            dz = (dy * gates[b]).astype(BF16)
            acc[...] += lax.dot_general(dz, us[b], TN, preferred_element_type=F32)
            du = jnp.dot(dz, w_ref[...], preferred_element_type=F32)
            dos.append(du * silus[b])
            dg_ref[...] = (du * ov * dsilus[b]).astype(BF16)
        dob[...] = dos[1].astype(BF16)
        doc[...] = dos[2].astype(BF16)
        row_term = jnp.dot(dos[0] * out_a, hs_ref[...], precision=lax.Precision.HIGHEST, preferred_element_type=F32)
        for wg, do_ref, dp_ref in zip(ws, (do0, do1, do2), (dp0, dp1, dp2)):
            do_ref[...] = wg * dos[0]
            dp_ref[...] = wg * row_term

        @pl.when(step == SEQ // rows - 1)
        def _():
            for acc, out in ((acc_a, gwa), (acc_b, gwb), (acc_c, gwc), (acc_o, gwo)):
                out[...] = acc[...].astype(BF16)

    full = lambda shape: pl.BlockSpec(shape, lambda i: (0,) * len(shape))
    vec = pl.BlockSpec((1, D_MODEL), lambda i: (0, 0))
    acc3 = pl.BlockSpec((3, D_MODEL), lambda i: (0, 0))
    in_specs = ([r512] * 8 + silu_cols + logit_cols + [r1024, r1024, full((3, D_MODEL))]
                + [full((D_MODEL, 512))] * 3 + [full((D_MODEL, D_MODEL)), vec, full((512, 512))])
    out_specs = ([r1024] + [r1024] * 3 + [r512] * 3 + [r512] * 6 + [r512] * 2 + [vec, vec, acc3]
                 + [full((D_MODEL, 512))] * 3 + [full((D_MODEL, D_MODEL))])
    bf, f32 = BF16, F32
    sds = jax.ShapeDtypeStruct
    out_shape = ([sds((SEQ, D_MODEL), f32)] + [sds((SEQ, D_MODEL), bf)] * 3 + [sds((SEQ, 512), bf)] * 3
                 + [sds((SEQ, 512), f32)] * 6 + [sds((SEQ, 512), bf)] * 2 + [sds((1, D_MODEL), f32)] * 2
                 + [sds((3, D_MODEL), f32)] + [sds((D_MODEL, 512), bf)] * 3 + [sds((D_MODEL, D_MODEL), bf)])
    res = pl.pallas_call(
        body, name="gate_block", grid=(SEQ // rows,), in_specs=in_specs, out_specs=out_specs, out_shape=out_shape,
        scratch_shapes=[pltpu.VMEM((D_MODEL, 512), F32)] * 3 + [pltpu.VMEM((D_MODEL, D_MODEL), F32)],
        compiler_params=_params(("arbitrary",)))(
            *o_grp, *l_grp, out_b, out_c, parts, parts, parts, parts, parts, parts, x, target, merge_bias, *wts, w_out,
            gain, head_sum)
    return dict(dout=res[0], dlog=res[1:4], dg=res[4:7], do_grp=res[7:10], dp_grp=res[10:13], do_b=res[13],
                do_c=res[14], err_sq=res[15], g_post=res[16], g_mb=res[17], g_wt=res[18:21], g_w_out=res[21])


def _local_step(x, hst, parts, tabs, bias, mem, target, pre_norm, mem_norm, post_norm, wt_in, late_weights,
                reduce_start=None):
    o_grp, l_grp = [], []
    for g, d in enumerate(DILATIONS):
        o, l = _attn_fwd("dil_fwd_%d" % g, "dil", parts, parts, parts, 12 * g, 12 * g + 4, 12 * g + 8, d=d)
        o_grp.append(o)
        l_grp.append(l)
    out_b, lse_b = _attn_fwd("na_fwd", "na", parts, parts, parts, 36, 40, 44, bias=bias)
    merge_bias, w_kv, wt_a, wt_b, wt_c, w_out = late_weights(sum(a[:8, :128] for a in [out_b] + o_grp))
    memn = _rmsnorm_fwd("memnorm", mem, mem_norm, MEM_LEN)
    kv_m = _mm_simple("mem_kv", memn, w_kv, NN, BF16, MEM_LEN, 512, D_MODEL)
    out_c, lse_c = _attn_fwd("mem_fwd", "mem", parts, kv_m, kv_m, 48, 0, 4)

    rr = _iota((512, 512), 0) // HEAD_DIM
    cc = _iota((512, 512), 1) // HEAD_DIM
    head_sum = (rr == cc).astype(F32)
    gb = _gate_block(o_grp, l_grp, out_b, out_c, parts, x, target, merge_bias, (wt_a, wt_b, wt_c), w_out, post_norm,
                     head_sum)
    dout, dlog, dg, g_wt, g_w_out = gb["dout"], gb["dlog"], gb["dg"], gb["g_wt"], gb["g_w_out"]
    do_grp, dp_grp, do_b, do_c, g_post, g_mb = (gb["do_grp"], gb["dp_grp"], gb["do_b"], gb["do_c"], gb["g_post"],
                                                gb["g_mb"])
    loss = 0.5 * jnp.sum(gb["err_sq"]) / D_MODEL

    dq_c, dk_m, dv_m = _attn_bwd("mem_bwd", "mem", parts, kv_m, kv_m, 48, 0, 4, do_c, lse_c, o=out_c)
    dkv = jnp.concatenate([dk_m, dv_m], axis=1).astype(BF16)
    g_w_kv = _mm_simple("mem_kv_dw", memn, dkv, TN, BF16, D_MODEL, 512, MEM_LEN)
    dmemn = _mm_simple("mem_kv_dx", dkv, w_kv, NT, F32, MEM_LEN, 512, D_MODEL)
    grads = dict(w_kv=g_w_kv, wt_a=g_wt[0], wt_b=g_wt[1], wt_c=g_wt[2], w_out=g_w_out, merge_bias=g_mb,
                 post_norm=g_post)
    dep = reduce_start("rest_sibling", grads) if reduce_start is not None else None

    dq_b, dk_b, dv_b, dbias = _attn_bwd("na_bwd", "na", parts, parts, parts, 36, 40, 44, do_b, lse_b, o=out_b,
                                        bias=bias, dep=dep)
    dqkv = []
    for g, d in enumerate(DILATIONS):
        dq, dk, dv = _attn_bwd("dil_bwd_%d" % g, "dil", parts, parts, parts, 12 * g, 12 * g + 4, 12 * g + 8,
                               do_grp[g], l_grp[g], dp=dp_grp[g], d=d, tabs=tabs[g])
        dqkv += [dq, dk, dv]
    if reduce_start is not None:
        dep = reduce_start("rest_chips", grads, sum(a[:8, :128] for a in (dqkv[0], dqkv[3], dqkv[6], dq_b)))
    dparts = dqkv + [dq_b, dk_b, dv_b, dq_c] + list(dg) + list(dlog)
    grads["wt_in"] = _in_proj_dw(dparts, hst, dep)
    dep = reduce_start("w_in", grads) if reduce_start is not None else None
    dh = _in_proj_dh(dparts, wt_in, dep)
    if reduce_start is not None:
        dep = reduce_start("w_in_second", grads, dh)
    grad_x, grads["pre_norm"] = _prenorm_bwd(x, pre_norm, dh, dout)
    g_rpb_t = _na_bias_bwd(dbias, dep)
    grads["na_rpb"] = g_rpb_t[:, :15, :31] + jnp.pad(g_rpb_t[:, :14, 64:95], ((0, 0), (1, 0), (0, 0)))
    grads["mem_norm"] = _memnorm_bwd(mem, dmemn, dep)
    return loss, grad_x, grads


ANY = pl.BlockSpec(memory_space=pl.ANY)


def _place():
    return lax.axis_index("x"), lax.axis_index("y"), lax.axis_index("c")


HBM = pl.BlockSpec(memory_space=pltpu.HBM)
SEM = pl.BlockSpec(memory_space=pltpu.SEMAPHORE)
DATAFLOW = pltpu.SideEffectType.DATAFLOW_SIDE_EFFECTING


def _split_copies(kind, srcs, lands, send_sems, recv_sems):
    nt = len(srcs)
    x, y, c = _place()
    copies = []
    if kind == "sibling":
        for t in range(nt):
            copies.append(pltpu.make_async_remote_copy(
                src_ref=srcs[t].at[:, 1 - c], dst_ref=lands[t], send_sem=send_sems.at[t],
                recv_sem=recv_sems.at[t], device_id=(x, y, 1 - c), device_id_type=MESH_ID))
    elif kind in ("rs_a", "rs_b"):
        half = lands[0].shape[1]
        xn, yn = (1 - x, y, c), (x, 1 - y, c)
        q_xn, q_yn, q_dg = 2 * (1 - x) + y, 2 * x + 1 - y, 2 * (1 - x) + 1 - y
        if kind == "rs_a":
            plan = [(srcs[0].at[q_yn].at[pl.ds(0, half)], 0, yn), (srcs[0].at[q_dg].at[pl.ds(0, half)], 1, yn),
                    (srcs[0].at[q_xn].at[pl.ds(half, half)], 2, xn), (srcs[0].at[q_dg].at[pl.ds(half, half)], 3, xn)]
        else:
            plan = [(srcs[0].at[0], 0, xn), (srcs[0].at[1], 1, yn)]
        for k, (src, slot, to) in enumerate(plan):
            copies.append(pltpu.make_async_remote_copy(
                src_ref=src, dst_ref=lands[0].at[slot], send_sem=send_sems.at[k], recv_sem=recv_sems.at[k],
                device_id=to, device_id_type=MESH_ID))
    elif kind == "gather":
        me = 4 * x + 2 * y + c
        for mask in range(1, 8):
            fx, fy, fc = (mask >> 2) & 1, (mask >> 1) & 1, mask & 1
            to = (1 - x if fx else x, 1 - y if fy else y, 1 - c if fc else c)
            for t in range(nt):
                k = (mask - 1) * nt + t
                copies.append(pltpu.make_async_remote_copy(
                    src_ref=srcs[t], dst_ref=lands[t].at[me], send_sem=send_sems.at[k], recv_sem=recv_sems.at[k],
                    device_id=to, device_id_type=MESH_ID))
    else:
        for s, (tx, ty) in enumerate([(1 - x, y), (x, 1 - y), (1 - x, 1 - y)]):
            for t in range(nt):
                k = s * nt + t
                copies.append(pltpu.make_async_remote_copy(
                    src_ref=srcs[t].at[2 * tx + ty], dst_ref=lands[t].at[s], send_sem=send_sems.at[k],
                    recv_sem=recv_sems.at[k], device_id=(tx, ty, c), device_id_type=MESH_ID))
    return copies


def _split_count(kind, nt):
    return {"gather": 7, "chips": 3, "sibling": 1, "rs_a": 4, "rs_b": 2}[kind] * nt


def _exchange_start(name, kind, srcs, land_shapes, after=None):
    nt = len(srcs)
    n = _split_count(kind, nt)
    dep_specs, dep_args = _dep_operand(after)
    nd = len(dep_args)

    def body(*refs):
        src_refs, land_refs = refs[:nt], refs[nt:2 * nt]
        send_sems, recv_sems = refs[2 * nt + nd], refs[2 * nt + nd + 1]
        token = refs[-1]
        for cp in _split_copies(kind, src_refs, land_refs, send_sems, recv_sems):
            cp.start()
        token[...] = jnp.zeros_like(token)

    lands = [pltpu.with_memory_space_constraint(lax.empty(s.shape, s.dtype), pltpu.HBM) for s in land_shapes]
    res = pl.pallas_call(
        body, name=name,
        out_shape=(pltpu.SemaphoreType.DMA((n,)), pltpu.SemaphoreType.DMA((n,)),
                   *[pltpu.HBM(s.shape, s.dtype) for s in srcs], *[pltpu.HBM(s.shape, s.dtype) for s in land_shapes],
                   jax.ShapeDtypeStruct((8, 128), F32)),
        in_specs=[HBM] * (2 * nt) + dep_specs,
        out_specs=(SEM, SEM, *([HBM] * (2 * nt)), pl.BlockSpec(memory_space=pltpu.VMEM)),
        input_output_aliases={i: 2 + i for i in range(2 * nt)},
        compiler_params=pltpu.CompilerParams(has_side_effects=DATAFLOW))(
            *[pltpu.with_memory_space_constraint(s, pltpu.HBM) for s in srcs], *lands, *dep_args)
    return res[0], res[1], list(res[2:2 + nt]), list(res[2 + nt:2 + 2 * nt]), res[-1]


def _exchange_wait(name, kind, send_sems, recv_sems, srcs, lands, after):
    nt = len(srcs)

    def body(*refs):
        src_refs, land_refs = refs[:nt], refs[nt:2 * nt]
        s_sems, r_sems = refs[2 * nt], refs[2 * nt + 1]
        for cp in _split_copies(kind, src_refs, land_refs, s_sems, r_sems):
            cp.wait_send()
            cp.wait_recv()

    res = pl.pallas_call(
        body, name=name,
        out_shape=tuple(pltpu.HBM(s.shape, s.dtype) for s in list(srcs) + list(lands)),
        in_specs=[HBM] * (2 * nt) + [SEM, SEM, pl.BlockSpec(memory_space=pl.ANY)],
        out_specs=tuple([HBM] * (2 * nt)),
        input_output_aliases={i: i for i in range(2 * nt)},
        compiler_params=pltpu.CompilerParams(has_side_effects=DATAFLOW))(
            *srcs, *lands, send_sems, recv_sems, after)
    return list(res[:nt]), list(res[nt:])


AG_GROUPS = ((0, 3), (3, 4), (7, 2))


def _ag_phase(name, own, land, sems, waits, starts, after=None):
    r = own.shape[0]
    half = r // 2
    ns = len(sems)
    dep_specs, dep_args = _dep_operand(after)
    nd = len(dep_args)
    new_group = None
    if starts:
        (new_group,) = [g for g, (first, n) in enumerate(AG_GROUPS) if first == starts[0]]
        assert list(starts) == list(range(AG_GROUPS[new_group][0], sum(AG_GROUPS[new_group])))

    def body(*refs):
        own_ref, land_ref = refs[0], refs[1]
        sem_refs = list(refs[2:2 + 2 * ns])
        outs = refs[2 + 2 * ns + nd:]
        if starts:
            sem_refs += [outs[0], outs[1]]
        x, y, c = _place()
        me, sib = (x, y, c), (x, y, 1 - c)
        xn, yn, dg = (1 - x, y, c), (x, 1 - y, c), (1 - x, 1 - y, c)

        def other(dev):
            return (dev[0], dev[1], 1 - dev[2])

        def rows(dev, part):
            blk = land_ref.at[4 * dev[0] + 2 * dev[1] + dev[2]]
            return blk if part is None else blk.at[pl.ds(part * half, half)]

        def sem_of(k):
            (g,) = [g for g, (first, n) in enumerate(AG_GROUPS) if first <= k < first + n]
            return sem_refs[2 * g].at[k - AG_GROUPS[g][0]], sem_refs[2 * g + 1].at[k - AG_GROUPS[g][0]]

        sent = {0: (me, None, sib), 1: (me, None, xn), 2: (me, None, yn), 3: (xn, 0, yn), 4: (yn, 1, xn),
                5: (xn, None, sib), 6: (yn, None, sib), 7: (dg, 0, sib), 8: (dg, 1, sib)}
        landed = {0: (sib, None), 1: (xn, None), 2: (yn, None), 3: (dg, 0), 4: (dg, 1), 5: (other(xn), None),
                  6: (other(yn), None), 7: (other(dg), 0), 8: (other(dg), 1)}

        def copy(k, receiving):
            send_sem, recv_sem = sem_of(k)
            dev, part, to = (*landed[k], me) if receiving else sent[k]
            src = own_ref if (dev is me and not receiving) else rows(dev, part)
            return pltpu.make_async_remote_copy(src_ref=src, dst_ref=rows(dev, part), send_sem=send_sem,
                                                recv_sem=recv_sem, device_id=to, device_id_type=MESH_ID)

        for kind, k in waits:
            if kind == "recv":
                copy(k, True).wait_recv()
            else:
                copy(k, False).wait_send()
        for k in starts:
            copy(k, False).start()
        if starts:
            outs[-1][...] = jnp.zeros_like(outs[-1])

    n_new = AG_GROUPS[new_group][1] if starts else 0
    sem_out = (pltpu.SemaphoreType.DMA((n_new,)), pltpu.SemaphoreType.DMA((n_new,))) if starts else ()
    token_out = (jax.ShapeDtypeStruct((8, 128), F32),) if starts else ()
    res = pl.pallas_call(
        body, name=name,
        out_shape=(*sem_out, pltpu.HBM(own.shape, own.dtype), pltpu.HBM(land.shape, land.dtype), *token_out),
        in_specs=[HBM, HBM] + [SEM] * (2 * ns) + dep_specs,
        out_specs=(*([SEM] * len(sem_out)), HBM, HBM, *([pl.BlockSpec(memory_space=pltpu.VMEM)] * len(token_out))),
        input_output_aliases={0: len(sem_out), 1: len(sem_out) + 1},
        compiler_params=pltpu.CompilerParams(has_side_effects=DATAFLOW))(
            own, land, *[a for pair in sems for a in pair], *dep_args)
    if starts:
        return (res[0], res[1]), res[2], res[3], res[4]
    return None, res[0], res[1], None


def _add_sibling(name, term, recv, rows):
    _, r, w = term.shape
    cidx = lax.axis_index("c").astype(jnp.int32).reshape(1)
    like_term = recv.shape[0] == N_DEV

    def body(c_ref, a_ref, b_ref, o_ref):
        o_ref[...] = (a_ref[...].astype(F32) + b_ref[...].astype(F32)).astype(o_ref.dtype)

    grid_spec = pltpu.PrefetchScalarGridSpec(
        num_scalar_prefetch=1, grid=(4, r // rows),
        in_specs=[pl.BlockSpec((None, rows, w), lambda q, i, c_ref: (2 * q + c_ref[0], i, 0)),
                  pl.BlockSpec((None, rows, w), lambda q, i, c_ref: (2 * q + c_ref[0] if like_term else q, i, 0))],
        out_specs=pl.BlockSpec((None, rows, w), lambda q, i, c_ref: (q, i, 0)))
    return pl.pallas_call(
        body, name=name, grid_spec=grid_spec, out_shape=jax.ShapeDtypeStruct((4, r, w), term.dtype),
        compiler_params=_params(("parallel", "parallel")))(cidx, term, recv)


def _add_sibling_small(name, terms, recvs):
    nt = len(terms)

    def body(*refs):
        c = lax.axis_index("c")
        for t_ref, r_ref, o_ref in zip(refs[:nt], refs[nt:2 * nt], refs[2 * nt:]):
            for q in range(4):
                o_ref[q] = (t_ref[2 * q + c].astype(F32) + r_ref[q].astype(F32)).astype(o_ref.dtype)

    return pl.pallas_call(
        body, name=name, out_shape=[jax.ShapeDtypeStruct((4,) + t.shape[1:], t.dtype) for t in terms],
        compiler_params=_params())(*terms, *recvs)


def _reduce_scatter_start(tag, terms, recv1):
    sums = _add_sibling_small("add_sibling_" + tag, terms, recv1)
    lands = [jax.ShapeDtypeStruct((3,) + s.shape[1:], s.dtype) for s in sums]
    send_sems, recv_sems, sums, lands, token = _exchange_start("exchange_chips_start_" + tag, "chips", sums, lands)
    return (tag, send_sems, recv_sems, sums, lands), token


def _reduce_scatter_wait(state, after):
    tag, send_sems, recv_sems, sums, lands = state
    return _exchange_wait("exchange_chips_wait_" + tag, "chips", send_sems, recv_sems, sums, lands, after)


def _adam_math(w, g, m, v):
    nm = ADAM_B1 * m + (1.0 - ADAM_B1) * g
    nv = ADAM_B2 * v + (1.0 - ADAM_B2) * (g * g)
    c1 = 1.0 - ADAM_B1 ** ADAM_STEP
    c2 = 1.0 - ADAM_B2 ** ADAM_STEP
    return -ADAM_LR * ((nm / c1) / (jnp.sqrt(nv / c2) + ADAM_EPS) + ADAM_WD * w), nm, nv


def _presum_halves(sums, landed):
    _, r, w = sums.shape
    rows = r // 2
    x, y = lax.axis_index("x"), lax.axis_index("y")
    dest = jnp.stack([2 * (1 - x) + y, 2 * x + 1 - y]).astype(jnp.int32)

    def body(q_ref, a_ref, b_ref, o_ref):
        o_ref[...] = (a_ref[...].astype(F32) + b_ref[...].astype(F32)).astype(o_ref.dtype)

    grid_spec = pltpu.PrefetchScalarGridSpec(
        num_scalar_prefetch=1, grid=(2,),
        in_specs=[pl.BlockSpec((None, rows, w), lambda h, q_ref: (q_ref[h], h, 0)),
                  pl.BlockSpec((None, rows, w), lambda h, q_ref: (1 + 2 * h, 0, 0))],
        out_specs=pl.BlockSpec((None, rows, w), lambda h, q_ref: (h, 0, 0)))
    return pl.pallas_call(
        body, name="presum_halves", grid_spec=grid_spec, out_shape=jax.ShapeDtypeStruct((2, r // 2, w), sums.dtype),
        compiler_params=_params(("parallel",)))(dest, sums, landed)


def _adamw_halves(name, sums, landed_a, landed_b, w, m, v, rows):
    r, c = w.shape
    half = c // 2
    qidx = (2 * lax.axis_index("x") + lax.axis_index("y")).astype(jnp.int32).reshape(1)

    def body(q_ref, s_ref, a_ref, b_ref, w_ref, m_ref, v_ref, g_ref, d_ref, nm_ref, nv_ref):
        first = (s_ref[:half, :].astype(F32) + a_ref[0].astype(F32)) + b_ref[0].astype(F32)
        second = (s_ref[half:, :].astype(F32) + a_ref[2].astype(F32)) + b_ref[1].astype(F32)
        g = jnp.concatenate([first, second], axis=0).T
        g_ref[...] = g
        d_ref[...], nm_ref[...], nv_ref[...] = _adam_math(w_ref[...], g, m_ref[...], v_ref[...])

    row = pl.BlockSpec((rows, c), lambda i, q_ref: (i, 0))
    grid_spec = pltpu.PrefetchScalarGridSpec(
        num_scalar_prefetch=1, grid=(r // rows,),
        in_specs=[pl.BlockSpec((None, c, rows), lambda i, q_ref: (q_ref[0], 0, i)),
                  pl.BlockSpec((4, half, rows), lambda i, q_ref: (0, 0, i)),
                  pl.BlockSpec((2, half, rows), lambda i, q_ref: (0, 0, i)), row, row, row],
        out_specs=[row] * 4)
    return pl.pallas_call(
        body, name=name, grid_spec=grid_spec, out_shape=[jax.ShapeDtypeStruct((r, c), F32)] * 4,
        compiler_params=_params(("parallel",)))(qidx, sums, landed_a, landed_b, w, m, v)


def _adamw_chips_small(name, items):
    n = len(items)

    def body(*refs):
        q = 2 * lax.axis_index("x") + lax.axis_index("y")
        ins, outs = refs[:5 * n], refs[5 * n:]
        for i, (_, _, w, _, _, transposed) in enumerate(items):
            s_ref, r_ref, w_ref, m_ref, v_ref = ins[5 * i:5 * i + 5]
            g_ref, d_ref, nm_ref, nv_ref = outs[4 * i:4 * i + 4]
            g = (s_ref[q].astype(F32) + r_ref[0].astype(F32)) + (r_ref[1].astype(F32) + r_ref[2].astype(F32))
            g = g.T if transposed else g[:w.shape[0]]
            g_ref[...] = g
            d_ref[...], nm_ref[...], nv_ref[...] = _adam_math(w_ref[...], g, m_ref[...], v_ref[...])

    res = pl.pallas_call(
        body, name=name, out_shape=[jax.ShapeDtypeStruct(it[2].shape, F32) for it in items for _ in range(4)],
        compiler_params=_params())(*[a for it in items for a in it[:5]])
    return [res[4 * i:4 * i + 4] for i in range(n)]


def _adamw_replicated(gathered, items):
    n = len(items)

    def body(g_ref, *refs):
        ins, t_ref, outs = refs[:3 * n], refs[3 * n], refs[3 * n + 1:]
        acc = g_ref[0]
        for j in range(1, N_DEV):
            acc = acc + g_ref[j]
        t_ref[...] = acc
        for i, (first, w, _, _) in enumerate(items):
            w_ref, m_ref, v_ref = ins[3 * i:3 * i + 3]
            g = t_ref[first:first + w.shape[0], :]
            outs[3 * i][...], outs[3 * i + 1][...], outs[3 * i + 2][...] = _adam_math(w_ref[...], g, m_ref[...], v_ref[...])

    res = pl.pallas_call(
        body, name="adamw_replicated",
        out_shape=[jax.ShapeDtypeStruct(gathered.shape[1:], F32)]
        + [jax.ShapeDtypeStruct(it[1].shape, F32) for it in items for _ in range(3)],
        compiler_params=_params())(gathered, *[a for it in items for a in it[1:]])
    return res[0], [res[1 + 3 * i:4 + 3 * i] for i in range(n)]


def _wide_rows(a):
    rows = -(-a.size // D_MODEL)
    return jnp.pad(a.reshape(-1), (0, rows * D_MODEL - a.size)).reshape(rows, D_MODEL)


def kernel(x, mem, pre_norm, w_in, merge_bias, na_rpb, mem_norm, w_mem_kv, w_branch_a, w_branch_b, w_branch_c, w_out, post_norm, loss_target, m_pre_norm, m_w_in, m_merge_bias, m_na_rpb, m_mem_norm, m_w_mem_kv, m_w_branch_a, m_w_branch_b, m_w_branch_c, m_w_out, m_post_norm, v_pre_norm, v_w_in, v_merge_bias, v_na_rpb, v_mem_norm, v_w_mem_kv, v_w_branch_a, v_w_branch_b, v_w_branch_c, v_w_out, v_post_norm):
    wt_in_s = w_in[0].T.astype(BF16)
    rows_s = jnp.concatenate([w_mem_kv[0], w_out[0]], axis=0).astype(BF16)
    cols_s = jnp.concatenate([w_branch_a[0].T, w_branch_b[0].T, w_branch_c[0].T], axis=0).astype(BF16)
    mb_s = jnp.pad(merge_bias[0], ((0, 5), (0, 0)))
    me = 4 * lax.axis_index("x") + 2 * lax.axis_index("y") + lax.axis_index("c")

    chip = 2 * lax.axis_index("x") + lax.axis_index("y")

    def first_block(q):
        return jnp.where(q == 0, 0, jnp.where(q == 1, 6, jnp.where(q == 2, 11, 17)))

    five = jnp.arange(5, dtype=jnp.int32)
    near, far = jnp.where(chip < 2, 5, 16), jnp.where(chip < 2, 16, 5)
    order1 = (first_block(chip) + five).astype(jnp.int32)
    order2 = jnp.concatenate([first_block(chip ^ 1) + five, near[None], first_block(chip ^ 2) + five]).astype(jnp.int32)
    order3 = jnp.concatenate([first_block(chip ^ 3) + five, far[None]]).astype(jnp.int32)
    tabs = _rope_tables()

    def weights_of(land):
        return land.reshape(N_IN, D_MODEL)

    land = pltpu.with_memory_space_constraint(lax.empty((N_DEV,) + wt_in_s.shape, BF16), pltpu.HBM)
    own = pltpu.with_memory_space_constraint(wt_in_s, pltpu.HBM)
    sem_a, own, land, token = _ag_phase("ag_start", own, land, [], [], [0, 1, 2])
    hs, hst = _prenorm_fold(x[0], pre_norm, token)
    _, own, land, _ = _ag_phase("ag_wait0", own, land, [sem_a], [("recv", 0)], [], hs)
    land = lax.dynamic_update_slice(land, own[None], (me, 0, 0))
    parts = _in_proj("in_proj_1", hs, weights_of(land), tabs, order1)
    bias = _na_bias(jnp.pad(na_rpb[0], ((0, 0), (0, 1), (0, 128 - 31))), parts)
    sem_b, own, land, _ = _ag_phase("ag_mid1", own, land, [sem_a], [("recv", 1), ("recv", 2)], [3, 4, 5, 6], bias)
    _, own, land, _ = _ag_phase("ag_wait1", own, land, [sem_a, sem_b], [("recv", 5), ("recv", 6)], [])
    parts = _in_proj("in_proj_2", hs, weights_of(land), tabs, order2, parts)
    sem_c, own, land, _ = _ag_phase("ag_mid2", own, land, [sem_a, sem_b], [("recv", 3), ("recv", 4)], [7, 8], parts)
    _, own, land, _ = _ag_phase("ag_end", own, land, [sem_a, sem_b, sem_c],
                                [("recv", 7), ("recv", 8)] + [("send", k) for k in range(9)], [])
    wt_in = weights_of(land)

    late_own = [rows_s, cols_s, mb_s]
    late_lands = [jax.ShapeDtypeStruct((N_DEV,) + s.shape, s.dtype) for s in late_own]
    l_send, l_recv, late_own, late_lands, late_token = _exchange_start("gather_late_start", "gather", late_own,
                                                                       late_lands, after=wt_in)
    parts = _in_proj("in_proj_3", hs, wt_in, tabs, order3, parts, late_token)

    def late_weights(after):
        own, lands = _exchange_wait("gather_late_wait", "gather", l_send, l_recv, late_own, late_lands, after)
        g_rows, g_cols, g_mb = [lax.dynamic_update_slice(land, o[None], (me, 0, 0)) for land, o in zip(lands, own)]
        return (g_mb[:, :3].transpose(1, 0, 2).reshape(3, D_MODEL),
                g_rows[:, :128].reshape(D_MODEL, D_MODEL), g_cols[:, 0:128].reshape(D_MODEL, 512),
                g_cols[:, 128:256].reshape(D_MODEL, 512), g_cols[:, 256:384].reshape(D_MODEL, 512),
                g_rows[:, 128:].reshape(D_MODEL, D_MODEL))

    rest_state, rest_sibling, w_in_a, w_in_b = [], [], [], []

    def reduce_start(phase, grads, after=None):
        if phase == "rest_sibling":
            gmb_t = jnp.pad(grads["merge_bias"].reshape(3, N_DEV, 128).transpose(1, 0, 2), ((0, 0), (0, 5), (0, 0)))
            terms = [grads["w_kv"].reshape(N_DEV, 128, D_MODEL), grads["w_out"].reshape(N_DEV, 128, D_MODEL),
                     grads["wt_a"].reshape(N_DEV, 128, 512), grads["wt_b"].reshape(N_DEV, 128, 512),
                     grads["wt_c"].reshape(N_DEV, 128, 512), gmb_t]
            lands = [jax.ShapeDtypeStruct((4,) + t.shape[1:], t.dtype) for t in terms]
            terms = [t.reshape(4, 2, *t.shape[1:]) for t in terms]
            started = _exchange_start("exchange_sibling_start_rest", "sibling", terms, lands)
            rest_sibling.extend(started[:4])
            return started[4]
        if phase == "rest_chips":
            s_send, s_recv, terms, lands = rest_sibling
            terms, recv1 = _exchange_wait("exchange_sibling_wait_rest", "sibling", s_send, s_recv, terms, lands, after)
            terms = [t.reshape(N_DEV, *t.shape[2:]) for t in terms]
            state, token = _reduce_scatter_start("rest", terms, recv1)
            rest_state.append(state)
            return token
        if phase == "w_in":
            own, sibling = [a.reshape(N_DEV, SHARD_IN, D_MODEL) for a in grads["wt_in"]]
            sums = _add_sibling("add_sibling_w_in", own, sibling, SHARD_IN)
            lands = [jax.ShapeDtypeStruct((4, SHARD_IN // 2, D_MODEL), BF16)]
            w_in_a.extend(_exchange_start("rs_a_start", "rs_a", [sums], lands))
            return w_in_a[4]
        (sums,), (landed_a,) = _exchange_wait("rs_a_wait", "rs_a", w_in_a[0], w_in_a[1], w_in_a[2], w_in_a[3], after)
        lands = [jax.ShapeDtypeStruct((2, SHARD_IN // 2, D_MODEL), BF16)]
        w_in_b.extend(_exchange_start("rs_b_start", "rs_b", [_presum_halves(sums, landed_a)], lands))
        w_in_b.extend([sums, landed_a])
        return w_in_b[4]

    loss_term, grad_x, grads = _local_step(
        x[0], hst, parts, tabs, bias, mem[0], loss_target[0], pre_norm, mem_norm, post_norm, wt_in, late_weights,
        reduce_start=reduce_start)

    replicated = ("pre_norm", "mem_norm", "post_norm", "na_rpb")
    pieces = [_wide_rows(grads[n]) for n in replicated] + [_wide_rows(loss_term)]
    first_rows = [sum(p.shape[0] for p in pieces[:i]) for i in range(len(pieces))]
    small = jnp.concatenate(pieces, axis=0)
    s_send, s_recv, s_own, s_land, s_token = _exchange_start(
        "gather_small_start", "gather", [small], [jax.ShapeDtypeStruct((N_DEV,) + small.shape, F32)])
    grad = {}
    weights = {
        "pre_norm": (pre_norm, m_pre_norm, v_pre_norm), "w_in": (w_in, m_w_in, v_w_in),
        "merge_bias": (merge_bias, m_merge_bias, v_merge_bias), "na_rpb": (na_rpb, m_na_rpb, v_na_rpb),
        "mem_norm": (mem_norm, m_mem_norm, v_mem_norm), "w_mem_kv": (w_mem_kv, m_w_mem_kv, v_w_mem_kv),
        "w_branch_a": (w_branch_a, m_w_branch_a, v_w_branch_a), "w_branch_b": (w_branch_b, m_w_branch_b, v_w_branch_b),
        "w_branch_c": (w_branch_c, m_w_branch_c, v_w_branch_c), "w_out": (w_out, m_w_out, v_w_out),
        "post_norm": (post_norm, m_post_norm, v_post_norm)}
    order = ["pre_norm", "w_in", "merge_bias", "na_rpb", "mem_norm", "w_mem_kv", "w_branch_a", "w_branch_b",
             "w_branch_c", "w_out", "post_norm"]
    delta, new_m, new_v = {}, {}, {}

    sums, recv2 = _reduce_scatter_wait(rest_state[0], s_token)
    rest = (("w_mem_kv", False), ("w_out", False), ("w_branch_a", True), ("w_branch_b", True), ("w_branch_c", True),
            ("merge_bias", False))
    items = [(sums[i], recv2[i], *[a[0] for a in weights[n]], transposed) for i, (n, transposed) in enumerate(rest)]
    for (n, _), (g, dl, nm, nv) in zip(rest, _adamw_chips_small("adamw_rest", items)):
        grad[n], delta[n], new_m[n], new_v[n] = g[None], dl[None], nm[None], nv[None]
    s_own, s_land = _exchange_wait("gather_small_wait", "gather", s_send, s_recv, s_own, s_land, delta["w_out"])
    items = [(first, *[_wide_rows(a) for a in weights[n]]) for n, first in zip(replicated, first_rows)]
    total, updates = _adamw_replicated(lax.dynamic_update_slice(s_land[0], s_own[0][None], (me, 0, 0)), items)
    loss = total[first_rows[-1], 0]
    for n, first, it, (dl, nm, nv) in zip(replicated, first_rows, items, updates):
        w = weights[n][0]
        grad[n], delta[n], new_m[n], new_v[n] = [
            a.reshape(-1)[:w.size].reshape(w.shape) for a in (total[first:first + it[1].shape[0]], dl, nm, nv)]
    _, (landed_b,) = _exchange_wait("rs_b_wait", "rs_b", w_in_b[0], w_in_b[1], w_in_b[2], w_in_b[3], updates[-1][0])
    g, dl, nm, nv = _adamw_halves("adamw_w_in", w_in_b[5], w_in_b[6], landed_b, w_in[0], m_w_in[0], v_w_in[0], 256)
    grad["w_in"], delta["w_in"], new_m["w_in"], new_v["w_in"] = g[None], dl[None], nm[None], nv[None]

    return (loss, grad_x[None], *[grad[n] for n in order], *[delta[n] for n in order],
            *[new_m[n] for n in order], *[new_v[n] for n in order])
```

```python
import functools

import numpy as np
import jax
import jax.numpy as jnp
from jax import lax
from jax.experimental import pallas as pl
from jax.experimental.pallas import tpu as pltpu

F32 = jnp.float32
BF16 = jnp.bfloat16

SEQ = 2048
D_MODEL = 1024
N_IN = 11264
N_DEV = 8
SHARD_IN = N_IN // N_DEV
HEAD_DIM = 64
GRID_W = 64
NA_ROWS = 8
MEM_LEN = 256
DILATIONS = (1, 4, 16)
REACH = 64
ROPE_THETA = 500000.0
ROPE_DIM = 16
EPS = 1e-6
NEG = -1e30
ADAM_LR = 0.001
ADAM_B1 = 0.9
ADAM_B2 = 0.999
ADAM_EPS = 1e-08
ADAM_WD = 0.01
ADAM_STEP = 10

VMEM_LIMIT_BYTES = 56 * 1024 * 1024
MESH_ID = pl.DeviceIdType.MESH

NN = (((1,), (0,)), ((), ()))
NT = (((1,), (1,)), ((), ()))
TN = (((0,), (0,)), ((), ()))


def _params(sem=None):
    return pltpu.CompilerParams(dimension_semantics=sem, vmem_limit_bytes=VMEM_LIMIT_BYTES)


def _iota(shape, dim):
    return lax.broadcasted_iota(jnp.int32, shape, dim)


def _sigmoid(x):
    return 1.0 / (1.0 + jnp.exp(-x))


def _rope_tables():
    half = ROPE_DIM // 2
    inv = (ROPE_THETA ** (-np.arange(half, dtype=np.float64) * 2.0 / ROPE_DIM)).astype(np.float32)
    pos = np.arange(SEQ, dtype=np.float32)
    ang = pos[:, None] * inv[None, :]
    cos, sin = np.cos(ang), np.sin(ang)
    zeros = np.zeros_like(cos)
    rest = HEAD_DIM - ROPE_DIM
    c64 = np.concatenate([cos, cos, np.ones((SEQ, rest), np.float32)], axis=1)
    s1 = np.concatenate([zeros, sin, np.zeros((SEQ, rest), np.float32)], axis=1)
    s2 = np.concatenate([-sin, zeros, np.zeros((SEQ, rest), np.float32)], axis=1)

    def fold(t, d):
        return t.reshape(SEQ // d, d, t.shape[1]).transpose(1, 0, 2).reshape(SEQ, t.shape[1])

    tabs = [np.stack([np.tile(fold(t, d), (1, 2)) for t in (c64, s1, s2)], axis=0) for d in DILATIONS]
    return jnp.asarray(np.stack(tabs, axis=0), dtype=F32)


def _rope(a, c, s1, s2):
    return a * c + pltpu.roll(a, 8, 1) * s1 + pltpu.roll(a, 120, 1) * s2


def _rope_t(a, c, s1, s2):
    return a * c + pltpu.roll(a * s1, 120, 1) + pltpu.roll(a * s2, 8, 1)


def _perm_of_block(j):
    return jnp.where(j < 3, 0, jnp.where(j < 6, 1, jnp.where(j < 9, 2, 0)))


def _mm(name, a, b, out_shape, out_dtype, grid, a_spec, b_spec, o_spec, acc_shape, dims, k_axis, nk):
    def body(a_ref, b_ref, o_ref, acc_ref):
        k = pl.program_id(k_axis)

        @pl.when(k == 0)
        def _():
            acc_ref[...] = jnp.zeros(acc_shape, F32)

        acc_ref[...] += lax.dot_general(a_ref[...], b_ref[...], dims, preferred_element_type=F32)

        @pl.when(k == nk - 1)
        def _():
            o_ref[...] = acc_ref[...].astype(out_dtype)

    sem = tuple("arbitrary" if ax == k_axis else "parallel" for ax in range(len(grid)))
    return pl.pallas_call(
        body, name=name, grid=grid, in_specs=[a_spec, b_spec], out_specs=o_spec,
        out_shape=jax.ShapeDtypeStruct(out_shape, out_dtype),
        scratch_shapes=[pltpu.VMEM(acc_shape, F32)], compiler_params=_params(sem))(a, b)


def _mm_simple(name, a, b, dims, out_dtype, tm, tn, tk):
    if dims is NN:
        m, kk = a.shape
        n = b.shape[1]
        a_spec = pl.BlockSpec((tm, tk), lambda i, j, k: (i, k))
        b_spec = pl.BlockSpec((tk, tn), lambda i, j, k: (k, j))
    elif dims is NT:
        m, kk = a.shape
        n = b.shape[0]
        a_spec = pl.BlockSpec((tm, tk), lambda i, j, k: (i, k))
        b_spec = pl.BlockSpec((tn, tk), lambda i, j, k: (j, k))
    else:
        kk, m = a.shape
        n = b.shape[1]
        a_spec = pl.BlockSpec((tk, tm), lambda i, j, k: (k, i))
        b_spec = pl.BlockSpec((tk, tn), lambda i, j, k: (k, j))
    grid = (m // tm, n // tn, kk // tk)
    o_spec = pl.BlockSpec((tm, tn), lambda i, j, k: (i, j))
    return _mm(name, a, b, (m, n), out_dtype, grid, a_spec, b_spec, o_spec, (tm, tn), dims, 2, kk // tk)


def _rmsnorm_fwd(name, x, gain, rows):
    n, d = x.shape

    def body(x_ref, g_ref, o_ref):
        xv = x_ref[...]
        rstd = lax.rsqrt(jnp.mean(xv * xv, axis=1, keepdims=True) + EPS)
        o_ref[...] = (xv * rstd * g_ref[...]).astype(BF16)

    return pl.pallas_call(
        body, name=name, grid=(n // rows,),
        in_specs=[pl.BlockSpec((rows, d), lambda i: (i, 0)), pl.BlockSpec((1, d), lambda i: (0, 0))],
        out_specs=pl.BlockSpec((rows, d), lambda i: (i, 0)),
        out_shape=jax.ShapeDtypeStruct((n, d), BF16), compiler_params=_params(("parallel",)))(x, gain)


def _folded_rows(first, rows, d):
    if d == 1:
        return pl.ds(pl.multiple_of(first, rows), rows)
    mlen = SEQ // d
    return pl.ds((first % mlen) * d + first // mlen, rows, stride=d)


def _prenorm_fold(x, gain, dep=None):
    rows = 128
    nchunk = D_MODEL // 128
    dep_specs, dep_args = _dep_operand(dep)

    def body(*refs):
        x_refs, g_ref, hs_ref, hst_ref = refs[:nchunk], refs[nchunk], refs[-2], refs[-1]
        first = pl.program_id(0) * rows
        for p, d in enumerate(DILATIONS):
            idx = _folded_rows(first, rows, d)
            xv = jnp.concatenate([r[idx, :] for r in x_refs], axis=1)
            rstd = lax.rsqrt(jnp.mean(xv * xv, axis=1, keepdims=True) + EPS)
            h = xv * rstd * g_ref[...]
            hs_ref[p] = h.astype(BF16)
            hst_ref[p] = h.T.astype(BF16)

    x_specs = [pl.BlockSpec((SEQ, 128), functools.partial(lambda c, i: (0, c), c)) for c in range(nchunk)]
    return pl.pallas_call(
        body, name="prenorm", grid=(SEQ // rows,),
        in_specs=x_specs + [pl.BlockSpec((1, D_MODEL), lambda i: (0, 0))] + dep_specs,
        out_specs=[pl.BlockSpec((3, rows, D_MODEL), lambda i: (0, i, 0)),
                   pl.BlockSpec((3, D_MODEL, rows), lambda i: (0, 0, i))],
        out_shape=[jax.ShapeDtypeStruct((3, SEQ, D_MODEL), BF16), jax.ShapeDtypeStruct((3, D_MODEL, SEQ), BF16)],
        compiler_params=_params(("parallel",)))(*([x] * nchunk), gain, *dep_args)


def _prenorm_bwd(x, gain, dh, dout):
    rows = 512

    def body(x_ref, g_ref, a_ref, do_ref, dx_ref, gg_ref):
        xv = x_ref[...]
        rstd = lax.rsqrt(jnp.mean(xv * xv, axis=1, keepdims=True) + EPS)
        xn = xv * rstd
        dh = jnp.concatenate([a_ref[c] for c in range(D_MODEL // 128)], axis=1)
        gdh = dh * g_ref[...]
        dx_ref[...] = rstd * (gdh - xn * jnp.mean(gdh * xn, axis=1, keepdims=True)) + do_ref[...]

        @pl.when(pl.program_id(0) == 0)
        def _():
            gg_ref[...] = jnp.zeros((1, D_MODEL), F32)

        gg_ref[...] += jnp.sum(dh * xn, axis=0, keepdims=True)

    row = pl.BlockSpec((rows, D_MODEL), lambda i: (i, 0))
    vec = pl.BlockSpec((1, D_MODEL), lambda i: (0, 0))
    return pl.pallas_call(
        body, name="prenorm_bwd", grid=(SEQ // rows,),
        in_specs=[row, vec, pl.BlockSpec((D_MODEL // 128, rows, 128), lambda i: (0, i, 0)), row], out_specs=[row, vec],
        out_shape=[jax.ShapeDtypeStruct((SEQ, D_MODEL), F32), jax.ShapeDtypeStruct((1, D_MODEL), F32)],
        compiler_params=_params(("arbitrary",)))(x, gain, dh, dout)


def _memnorm_bwd(mem, dmemn, dep=None):
    dep_specs, dep_args = _dep_operand(dep)

    def body(m_ref, d_ref, *rest):
        mv = m_ref[...]
        rstd = lax.rsqrt(jnp.mean(mv * mv, axis=1, keepdims=True) + EPS)
        rest[-1][...] = jnp.sum(d_ref[...] * mv * rstd, axis=0, keepdims=True)

    whole = pl.BlockSpec(memory_space=pltpu.VMEM)
    return pl.pallas_call(
        body, name="memnorm_bwd", in_specs=[whole, whole] + dep_specs,
        out_shape=jax.ShapeDtypeStruct((1, D_MODEL), F32), compiler_params=_params())(mem, dmemn, *dep_args)


def _dep_operand(dep):
    return ([], []) if dep is None else ([pl.BlockSpec(memory_space=pl.ANY)], [dep])


def _in_proj(name, hs, wt, tabs, order, prev=None, dep=None):
    tm, tn = 512, 512
    prev_specs, prev_args = ([], []) if prev is None else ([ANY], [prev])
    dep_specs, dep_args = _dep_operand(dep)

    def body(order_ref, h_ref, w_ref, t_ref, *rest):
        o_ref = rest[-1]
        j = order_ref[pl.program_id(0)]
        is_rope = jnp.logical_and(j < 9, j % 3 != 2)
        row_slices = [slice(r * tm, (r + 1) * tm) for r in range(SEQ // tm)]

        def product(rs):
            return lax.dot_general(h_ref[rs, :], w_ref[...], NT, preferred_element_type=F32)

        @pl.when(is_rope)
        def _():
            for rs in row_slices:
                acc = product(rs)
                c, s1, s2 = t_ref[0, rs, :], t_ref[1, rs, :], t_ref[2, rs, :]
                for q in range(tn // 128):
                    a = acc[:, q * 128:(q + 1) * 128]
                    o_ref[rs, q * 128:(q + 1) * 128] = _rope(a, c, s1, s2).astype(BF16)

        @pl.when(jnp.logical_not(is_rope))
        def _():
            for rs in row_slices:
                o_ref[rs, :] = product(rs).astype(BF16)

    grid_spec = pltpu.PrefetchScalarGridSpec(
        num_scalar_prefetch=1, grid=(order.shape[0],),
        in_specs=[pl.BlockSpec((None, SEQ, D_MODEL), lambda t, o: (_perm_of_block(o[t]), 0, 0)),
                  pl.BlockSpec((tn, D_MODEL), lambda t, o: (o[t], 0)),
                  pl.BlockSpec((None, 3, SEQ, 128), lambda t, o: (_perm_of_block(o[t]), 0, 0, 0))] + prev_specs
        + dep_specs,
        out_specs=pl.BlockSpec((SEQ, tn), lambda t, o: (0, o[t])))
    return pl.pallas_call(
        body, name=name, grid_spec=grid_spec, out_shape=jax.ShapeDtypeStruct((SEQ, N_IN), BF16),
        input_output_aliases={} if prev is None else {4: 0},
        compiler_params=_params(("arbitrary",)))(order, hs, wt, tabs, *prev_args, *dep_args)


def _piece_blocks(pieces):
    return [(a, h * 512) for a, p in enumerate(pieces) for h in range(p.shape[1] // 512)]


def _block_fetch(piece_refs, blocks, buf, sem):
    def start(block, slot):
        for b, (a, col) in enumerate(blocks):
            @pl.when(block == b)
            def _():
                pltpu.make_async_copy(piece_refs[a].at[:, pl.ds(col, 512)], buf.at[slot], sem.at[slot]).start()

    def wait(slot):
        pltpu.make_async_copy(piece_refs[0].at[:, pl.ds(0, 512)], buf.at[slot], sem.at[slot]).wait()

    return start, wait


def _in_proj_dw(pieces, hst, dep=None):
    tn = 512
    blocks = _piece_blocks(pieces)
    nblk = len(blocks)
    npc = len(pieces)
    dep_specs, dep_args = _dep_operand(dep)

    def body(h_ref, *rest):
        piece_refs = rest[:npc]
        own_out, mirror, buf, sem, out_buf, send_sems, recv_sem, local_sems = rest[-8:]
        j = pl.program_id(0)
        slot = j % 2
        start, wait = _block_fetch(piece_refs, blocks, buf, sem)
        x, y, c = _place()

        def rows_of(step):
            return pl.ds(pl.multiple_of(step * tn, tn), tn)

        def to_sibling(step, slot_):
            return pltpu.make_async_remote_copy(
                src_ref=out_buf.at[slot_], dst_ref=mirror.at[rows_of(step)],
                send_sem=send_sems.at[slot_], recv_sem=recv_sem, device_id=(x, y, 1 - c), device_id_type=MESH_ID)

        def to_own(step, slot_):
            return pltpu.make_async_copy(out_buf.at[slot_], own_out.at[rows_of(step)], local_sems.at[slot_])

        @pl.when(j == 0)
        def _():
            start(j, slot)

        wait(slot)

        @pl.when(j + 1 < nblk)
        def _():
            start(j + 1, 1 - slot)

        acc = jnp.dot(h_ref[...], buf[slot], preferred_element_type=F32)

        @pl.when(j >= 2)
        def _():
            to_sibling(j - 2, slot).wait_send()
            to_own(j - 2, slot).wait()

        out_buf[slot] = acc.T.astype(BF16)
        to_sibling(j, slot).start()
        to_own(j, slot).start()

        @pl.when(j == nblk - 1)
        def _():
            to_sibling(j - 1, 1 - slot).wait_send()
            to_own(j - 1, 1 - slot).wait()
            to_sibling(j, slot).wait_send()
            to_own(j, slot).wait()
            pltpu.make_async_remote_copy(src_ref=mirror, dst_ref=mirror, send_sem=send_sems.at[0], recv_sem=recv_sem,
                                         device_id=(x, y, 1 - c), device_id_type=MESH_ID).wait_recv()

    return pl.pallas_call(
        body, name="in_proj_dw", grid=(nblk,),
        in_specs=[pl.BlockSpec((None, D_MODEL, SEQ), lambda j: (_perm_of_block(j), 0, 0))] + [ANY] * npc + dep_specs,
        out_specs=[ANY, ANY],
        out_shape=[jax.ShapeDtypeStruct((N_IN, D_MODEL), BF16), jax.ShapeDtypeStruct((N_IN, D_MODEL), BF16)],
        scratch_shapes=[pltpu.VMEM((2, SEQ, tn), BF16), pltpu.SemaphoreType.DMA((2,)),
                        pltpu.VMEM((2, tn, D_MODEL), BF16), pltpu.SemaphoreType.DMA((2,)), pltpu.SemaphoreType.DMA,
                        pltpu.SemaphoreType.DMA((2,))],
        compiler_params=_params(("arbitrary",)))(hst, *pieces, *dep_args)


def _in_proj_dh(pieces, wt, dep=None):
    tk = 512
    blocks = _piece_blocks(pieces)
    nblk = len(blocks)
    npc = len(pieces)
    nchunk = D_MODEL // 128

    def col(s):
        return jnp.where(s < 3, s, jnp.where(s < 16, s + 6, s - 13))

    dep_specs, dep_args = _dep_operand(dep)

    def body(w_ref, *rest):
        piece_refs = rest[:npc]
        o_ref, acc_ref, buf, sem = rest[-4:]
        s = pl.program_id(0)
        slot = s % 2
        start, wait = _block_fetch(piece_refs, blocks, buf, sem)

        @pl.when(s == 0)
        def _():
            start(col(s), slot)

        wait(slot)

        @pl.when(s + 1 < nblk)
        def _():
            start(col(s + 1), 1 - slot)

        row_slices = [slice(r * 512, (r + 1) * 512) for r in range(SEQ // 512)]

        def product(rs):
            return jnp.dot(buf[slot, rs, :], w_ref[...], preferred_element_type=F32)

        def accumulate(cond, to_out, init):
            @pl.when(cond)
            def _():
                for rs in row_slices:
                    prod = product(rs)
                    if not to_out:
                        if init:
                            acc_ref[rs, :] = prod
                        else:
                            acc_ref[rs, :] += prod
                        continue
                    for c in range(nchunk):
                        if init:
                            o_ref[c, rs, :] = prod[:, c * 128:(c + 1) * 128]
                        else:
                            o_ref[c, rs, :] += prod[:, c * 128:(c + 1) * 128]

        accumulate(s == 0, True, True)
        accumulate(jnp.logical_and(s > 0, s < 16), True, False)
        accumulate(jnp.logical_or(s == 16, s == 19), False, True)
        accumulate(jnp.logical_and(s > 16, s != 19), False, False)
        for last, d in ((18, 4), (21, 16)):
            @pl.when(s == last)
            def _():
                mlen = SEQ // d
                for r in range(d):
                    for c in range(nchunk):
                        o_ref[c, pl.ds(r, mlen, stride=d), :] += acc_ref[r * mlen:(r + 1) * mlen,
                                                                         c * 128:(c + 1) * 128]

    return pl.pallas_call(
        body, name="in_proj_dh", grid=(nblk,),
        in_specs=[pl.BlockSpec((tk, D_MODEL), lambda s: (col(s), 0))] + [ANY] * npc + dep_specs,
        out_specs=pl.BlockSpec((nchunk, SEQ, 128), lambda s: (0, 0, 0)),
        out_shape=jax.ShapeDtypeStruct((nchunk, SEQ, 128), F32),
        scratch_shapes=[pltpu.VMEM((SEQ, D_MODEL), F32), pltpu.VMEM((2, SEQ, tk), BF16),
                        pltpu.SemaphoreType.DMA((2,))],
        compiler_params=_params(("arbitrary",)))(wt, *pieces, *dep_args)


def _head_lanes(lanes, hh):
    return lanes >= 64 if hh == 1 else lanes < 64


def _head_rows(x, lanes, hh, pair):
    if not pair:
        return jnp.max(x, axis=1, keepdims=True)
    return jnp.max(jnp.where(_head_lanes(lanes, hh), x, -jnp.inf), axis=1, keepdims=True)


def _mask_head(x, lanes, hh, pair, scale=1.0):
    if not pair:
        return x
    xf = x.astype(F32) if scale == 1.0 else x.astype(F32) * scale
    return jnp.where(_head_lanes(lanes, hh), xf, 0.0).astype(BF16)


def _window(mode, qi, tq, mlen, tk):
    if mode == "dil":
        q0 = qi * tq
        seg = (q0 // mlen) * mlen
        ks = jnp.clip(q0 - REACH, seg, seg + mlen - tk)
        return pl.multiple_of(ks, 64)
    if mode == "na":
        r_start = jnp.clip(qi - NA_ROWS // 2, 0, SEQ // GRID_W - NA_ROWS)
        return pl.multiple_of(r_start * GRID_W, 64)
    return 0


def _band_mask(qi, tq, tk, ks):
    qpos = qi * tq + _iota((tq, tk), 0)
    kpos = ks + _iota((tq, tk), 1)
    return jnp.where(jnp.abs(qpos - kpos) <= REACH, 0.0, NEG).astype(F32)


def _stack_heads(x, lanes, pair, scale=1.0):
    if not pair:
        return x
    return jnp.concatenate([_mask_head(x, lanes, hh, pair, scale) for hh in range(2)], axis=0)


def _stack_rows(x, lanes, pair):
    if not pair:
        return _head_rows(x, lanes, 0, pair)
    return jnp.concatenate([_head_rows(x, lanes, hh, pair) for hh in range(2)], axis=0)


def _unstack_heads(x, lanes, pair, tq):
    if not pair:
        return x
    return jnp.where(lanes < 64, x[:tq], x[tq:])


def _scores(mode, qst, k, sscale, band, qi, bias_ref, pair):
    s = lax.dot_general(qst, k, NT, preferred_element_type=F32)
    if sscale != 1.0:
        s = s * sscale
    if mode == "dil":
        s = s + jnp.concatenate([band, band], axis=0)
    elif mode == "na":
        off = qi - jnp.clip(qi - NA_ROWS // 2, 0, SEQ // GRID_W - NA_ROWS)
        s = s + jnp.concatenate([bias_ref[0, off], bias_ref[1, off]], axis=0)
    return s


def _attn_cfg(mode, d):
    if mode == "dil":
        mlen = SEQ // d
        return dict(pair=True, tq=128, tk=min(256, mlen), mlen=mlen, lk=SEQ, scale=HEAD_DIM ** -0.5, units=4,
                    nsub=ATTN_SUBTILES)
    if mode == "na":
        return dict(pair=True, tq=GRID_W, tk=NA_ROWS * GRID_W, mlen=SEQ, lk=SEQ, scale=HEAD_DIM ** -0.5, units=4,
                    nsub=2 * ATTN_SUBTILES)
    return dict(pair=False, tq=128, tk=MEM_LEN, mlen=SEQ, lk=MEM_LEN, scale=128 ** -0.5, units=4,
                nsub=ATTN_SUBTILES)


ATTN_SUBTILES = 16


def _attn_fwd(name, mode, q_arr, k_arr, v_arr, qcol, kcol, vcol, d=1, bias=None):
    cfg = _attn_cfg(mode, d)
    pair, tq, tk, mlen, lk, scale = cfg["pair"], cfg["tq"], cfg["tk"], cfg["mlen"], cfg["lk"], cfg["scale"]
    qscale, sscale = (scale, 1.0) if pair else (1.0, scale)
    nsub = cfg["nsub"]
    rows = nsub * tq

    def body(*refs):
        if mode == "na":
            q_ref, k_ref, v_ref, bias_ref, o_ref, l_ref = refs
        else:
            q_ref, k_ref, v_ref, o_ref, l_ref = refs
            bias_ref = None
        lanes = _iota((tq, 128), 1)
        qis = [pl.program_id(1) * nsub + sub for sub in range(nsub)]
        kss = [_window(mode, qi, tq, mlen, tk) for qi in qis]
        vs = [v_ref[pl.ds(ks, tk), :] for ks in kss]
        bands = [_band_mask(qi, tq, tk, ks) if mode == "dil" else None for qi, ks in zip(qis, kss)]
        ss = []
        for sub in range(nsub):
            qst = _stack_heads(q_ref[sub * tq:(sub + 1) * tq, :], lanes, pair, qscale)
            k = k_ref[pl.ds(kss[sub], tk), :]
            ss.append(_scores(mode, qst, k, sscale, bands[sub], qis[sub], bias_ref, pair))
        ms = [jnp.max(s_, axis=1, keepdims=True) for s_ in ss]
        ps = [jnp.exp(s_ - m) for s_, m in zip(ss, ms)]
        ls = [jnp.sum(p, axis=1, keepdims=True) for p in ps]
        os_ = [jnp.dot(p.astype(BF16), v, preferred_element_type=F32) for p, v in zip(ps, vs)]
        for sub in range(nsub):
            out = _unstack_heads(os_[sub] / ls[sub], lanes, pair, tq)
            lse = ms[sub] + jnp.log(ls[sub])
            lse = _unstack_heads(jnp.broadcast_to(lse, (lse.shape[0], 128)), lanes, pair, tq)
            dst = _folded_rows(qis[sub] * tq, tq, d) if mode == "dil" else slice(sub * tq, (sub + 1) * tq)
            o_ref[dst, :] = out
            l_ref[dst, :] = lse

    in_specs = [pl.BlockSpec((rows, 128), lambda u, i: (i, qcol + u)),
                pl.BlockSpec((lk, 128), lambda u, i: (0, kcol + u)),
                pl.BlockSpec((lk, 128), lambda u, i: (0, vcol + u))]
    args = [q_arr, k_arr, v_arr]
    if mode == "na":
        in_specs.append(pl.BlockSpec((2, NA_ROWS, GRID_W, NA_ROWS * GRID_W), lambda u, i: (u, 0, 0, 0)))
        args.append(bias)
    if mode == "dil":
        out_spec = pl.BlockSpec((SEQ, 128), lambda u, i: (0, u))
    else:
        out_spec = pl.BlockSpec((rows, 128), lambda u, i: (i, u))
    return pl.pallas_call(
        body, name=name, grid=(cfg["units"], SEQ // rows), in_specs=in_specs, out_specs=[out_spec, out_spec],
        out_shape=[jax.ShapeDtypeStruct((SEQ, 512), F32), jax.ShapeDtypeStruct((SEQ, 512), F32)],
        compiler_params=_params(("parallel", "arbitrary")))(*args)


def _attn_bwd(name, mode, q_arr, k_arr, v_arr, qcol, kcol, vcol, do, lse, dp=None, o=None, d=1, bias=None,
              tabs=None, dep=None):
    cfg = _attn_cfg(mode, d)
    pair, tq, tk, mlen, lk, scale = cfg["pair"], cfg["tq"], cfg["tk"], cfg["mlen"], cfg["lk"], cfg["scale"]
    qscale, sscale = (scale, 1.0) if pair else (1.0, scale)
    nsub = cfg["nsub"]
    rows = nsub * tq
    nq = SEQ // rows
    kv_dtype = F32 if mode == "mem" else BF16
    dep_specs, dep_args = _dep_operand(dep)
    mode_inputs = {"dil": 3, "na": 2, "mem": 1}[mode]

    def body(*refs):
        refs = list(refs)
        q_ref, k_ref, v_ref, do_ref, l_ref = refs[:5]
        rest = refs[5:5 + mode_inputs] + refs[5 + mode_inputs + len(dep_args):]
        bias_ref = tq_ref = tk_ref = db_ref = None
        if mode == "dil":
            dp_ref, tq_ref, tk_ref, dq_ref, dk_ref, dv_ref, dk_acc, dv_acc = rest
        elif mode == "na":
            o_ref, bias_ref, dq_ref, dk_ref, dv_ref, db_ref, dk_acc, dv_acc = rest
        else:
            o_ref, dq_ref, dk_ref, dv_ref, dk_acc, dv_acc = rest
        step = pl.program_id(1)

        @pl.when(step == 0)
        def _():
            dk_acc[...] = jnp.zeros((lk, 128), F32)
            dv_acc[...] = jnp.zeros((lk, 128), F32)
            if mode == "na":
                db_ref[...] = jnp.zeros(db_ref.shape, F32)

        lanes = _iota((tq, 128), 1)
        qis = [step * nsub + sub for sub in range(nsub)]
        sls = [slice(sub * tq, (sub + 1) * tq) for sub in range(nsub)]
        kss = [_window(mode, qi, tq, mlen, tk) for qi in qis]
        ks_ = [k_ref[pl.ds(ks, tk), :] for ks in kss]
        vs = [v_ref[pl.ds(ks, tk), :] for ks in kss]
        qsts, dosts, lses, dphs = [], [], [], []
        for sub in range(nsub):
            if mode == "dil":
                src = _folded_rows(qis[sub] * tq, tq, d)
                dov = do_ref[src, :].astype(BF16)
                lsev = l_ref[src, :]
                dphs.append(_stack_rows(dp_ref[src, :], lanes, pair))
            else:
                dov = do_ref[sls[sub], :]
                lsev = l_ref[sls[sub], :]
                dpv = dov.astype(F32) * o_ref[sls[sub], :]
                if pair:
                    dphs.append(jnp.concatenate(
                        [jnp.sum(jnp.where(_head_lanes(lanes, hh), dpv, 0.0), axis=1, keepdims=True)
                         for hh in range(2)], axis=0))
                else:
                    dphs.append(jnp.sum(dpv, axis=1, keepdims=True))
            qsts.append(_stack_heads(q_ref[sls[sub], :], lanes, pair, qscale))
            dosts.append(_stack_heads(dov, lanes, pair))
            lses.append(_stack_rows(lsev, lanes, pair))
        bands = [_band_mask(qi, tq, tk, ks) if mode == "dil" else None for qi, ks in zip(qis, kss)]
        ss = [_scores(mode, qsts[sub], ks_[sub], sscale, bands[sub], qis[sub], bias_ref, pair) for sub in range(nsub)]
        dpms = [lax.dot_general(dosts[sub], vs[sub], NT, preferred_element_type=F32) for sub in range(nsub)]
        ps = [jnp.exp(s_ - lse) for s_, lse in zip(ss, lses)]
        dss = [p * (dpm - dph) for p, dpm, dph in zip(ps, dpms, dphs)]
        if mode == "na":
            for sub, ds in enumerate(dss):
                off = qis[sub] - jnp.clip(qis[sub] - NA_ROWS // 2, 0, SEQ // GRID_W - NA_ROWS)
                db_ref[0, off] += ds[:tq]
                db_ref[1, off] += ds[tq:]
        dsbs = [ds.astype(BF16) for ds in dss]
        dvs = [lax.dot_general(p.astype(BF16), dosts[sub], TN, preferred_element_type=F32)
               for sub, p in enumerate(ps)]
        dqs = [jnp.dot(dsb, ks_[sub], preferred_element_type=F32) * scale for sub, dsb in enumerate(dsbs)]
        dks = [lax.dot_general(dsb, qsts[sub], TN, preferred_element_type=F32) for sub, dsb in enumerate(dsbs)]
        for sub in range(nsub):
            sl = sls[sub]
            dq = _unstack_heads(dqs[sub], lanes, pair, tq)
            if mode == "dil":
                dq = _rope_t(dq, tq_ref[0, sl, :], tq_ref[1, sl, :], tq_ref[2, sl, :])
            dq_ref[sl, :] = dq.astype(BF16)
            dk_acc[pl.ds(kss[sub], tk), :] += dks[sub] if pair else dks[sub] * scale
            dv_acc[pl.ds(kss[sub], tk), :] += dvs[sub]

        @pl.when(step == nq - 1)
        def _():
            dkv = dk_acc[...]
            if mode == "dil":
                dkv = _rope_t(dkv, tk_ref[0], tk_ref[1], tk_ref[2])
            dk_ref[...] = dkv.astype(kv_dtype)
            dv_ref[...] = dv_acc[...].astype(kv_dtype)

    q_spec = pl.BlockSpec((rows, 128), lambda u, i: (i, qcol + u))
    row_spec = pl.BlockSpec((rows, 128), lambda u, i: (i, u))
    kv_out = pl.BlockSpec((lk, 128), lambda u, i: (0, u))
    whole = pl.BlockSpec((SEQ, 128), lambda u, i: (0, u))
    nat_spec = whole if mode == "dil" else row_spec
    in_specs = [q_spec,
                pl.BlockSpec((lk, 128), lambda u, i: (0, kcol + u)),
                pl.BlockSpec((lk, 128), lambda u, i: (0, vcol + u)),
                nat_spec, nat_spec]
    args = [q_arr, k_arr, v_arr, do, lse]
    out_specs = [row_spec, kv_out, kv_out]
    out_shape = [jax.ShapeDtypeStruct((SEQ, 512), BF16), jax.ShapeDtypeStruct((lk, 512), kv_dtype),
                 jax.ShapeDtypeStruct((lk, 512), kv_dtype)]
    if mode == "dil":
        in_specs += [whole, pl.BlockSpec((3, rows, 128), lambda u, i: (0, i, 0)),
                     pl.BlockSpec((3, SEQ, 128), lambda u, i: (0, 0, 0))]
        args += [dp, tabs, tabs]
    elif mode == "na":
        b_spec = pl.BlockSpec((2, NA_ROWS, GRID_W, NA_ROWS * GRID_W), lambda u, i: (u, 0, 0, 0))
        in_specs += [row_spec, b_spec]
        args += [o, bias]
        out_specs.append(b_spec)
        out_shape.append(jax.ShapeDtypeStruct((8, NA_ROWS, GRID_W, NA_ROWS * GRID_W), F32))
    else:
        in_specs.append(row_spec)
        args.append(o)
    return pl.pallas_call(
        body, name=name, grid=(cfg["units"], nq), in_specs=in_specs + dep_specs, out_specs=out_specs,
        out_shape=out_shape, scratch_shapes=[pltpu.VMEM((lk, 128), F32), pltpu.VMEM((lk, 128), F32)],
        compiler_params=_params(("parallel", "arbitrary")))(*args, *dep_args)


def _na_geometry():
    qc = _iota((GRID_W, 128), 0)
    lane = _iota((GRID_W, 128), 1)
    kc = lane & 63
    c_start = jnp.clip(qc - 8, 0, GRID_W - 16)
    valid = jnp.logical_and(kc >= c_start, kc < c_start + 16)
    return lane, valid


def _na_bias(rpb_rows, dep=None):
    dep_specs, dep_args = _dep_operand(dep)

    def body(r_ref, *rest):
        o_ref, t_ref = rest[-2:]
        lane, valid = _na_geometry()
        for dd in range(14):
            row_a = jnp.broadcast_to(r_ref[dd:dd + 1, :], (GRID_W, 128))
            row_b = jnp.broadcast_to(r_ref[dd + 1:dd + 2, :], (GRID_W, 128))
            both = jnp.where(lane < 64, row_a, pltpu.roll(row_b, 64, 1))
            t = pltpu.roll(both, 128 - 15, 1, stride=1, stride_axis=0)
            t_ref[dd] = jnp.where(valid, t, NEG)
        for off in range(NA_ROWS):
            for p in range(4):
                o_ref[off, :, p * 128:(p + 1) * 128] = t_ref[2 * p - off + 7]

    return pl.pallas_call(
        body, name="na_bias", grid=(8,),
        in_specs=[pl.BlockSpec((None, 16, 128), lambda h: (h, 0, 0))] + dep_specs,
        out_specs=pl.BlockSpec((None, NA_ROWS, GRID_W, NA_ROWS * GRID_W), lambda h: (h, 0, 0, 0)),
        out_shape=jax.ShapeDtypeStruct((8, NA_ROWS, GRID_W, NA_ROWS * GRID_W), F32),
        scratch_shapes=[pltpu.VMEM((14, GRID_W, 128), F32)],
        compiler_params=_params(("parallel",)))(rpb_rows, *dep_args)


def _na_bias_bwd(dbias, dep=None):
    dep_specs, dep_args = _dep_operand(dep)

    def body(d_ref, *rest):
        o_ref = rest[-1]
        lane, valid = _na_geometry()
        reverse = (_iota((GRID_W, GRID_W), 0) + _iota((GRID_W, GRID_W), 1) == GRID_W - 1).astype(F32)
        o_ref[...] = jnp.zeros((16, 128), F32)
        for dd in range(14):
            t = jnp.zeros((GRID_W, 128), F32)
            for off in range(NA_ROWS):
                for p in range(4):
                    if 2 * p - off + 7 == dd:
                        t = t + d_ref[off, :, p * 128:(p + 1) * 128]
            t = jnp.dot(reverse, jnp.where(valid, t, 0.0), precision=lax.Precision.HIGHEST,
                        preferred_element_type=F32)
            t = pltpu.roll(t, 128 - (GRID_W - 16), 1, stride=1, stride_axis=0)
            o_ref[dd:dd + 1, :] = jnp.sum(t, axis=0, keepdims=True)

    return pl.pallas_call(
        body, name="na_bias_bwd", grid=(8,),
        in_specs=[pl.BlockSpec((None, NA_ROWS, GRID_W, NA_ROWS * GRID_W), lambda h: (h, 0, 0, 0))] + dep_specs,
        out_specs=pl.BlockSpec((None, 16, 128), lambda h: (h, 0, 0)),
        out_shape=jax.ShapeDtypeStruct((8, 16, 128), F32),
        compiler_params=_params(("parallel",)))(dbias, *dep_args)


GATE_ROWS = 128


def _group_weights(l0, l1, l2):
    m = jnp.maximum(jnp.maximum(l0, l1), l2)
    e0, e1, e2 = jnp.exp(l0 - m), jnp.exp(l1 - m), jnp.exp(l2 - m)
    inv = 1.0 / (e0 + e1 + e2)
    return e0 * inv, e1 * inv, e2 * inv


def _gate_block(o_grp, l_grp, out_b, out_c, parts, x, target, merge_bias, branch_rows, out_rows, gain, head_sum):
    rows = GATE_ROWS
    r512 = pl.BlockSpec((rows, 512), lambda i: (i, 0))
    r1024 = pl.BlockSpec((rows, D_MODEL), lambda i: (i, 0))
    silu_cols = [pl.BlockSpec((rows, 512), functools.partial(lambda b, i: (i, b), 13 + b)) for b in range(3)]
    logit_cols = [pl.BlockSpec((rows, D_MODEL), functools.partial(lambda b, i: (i, b), 8 + b)) for b in range(3)]

    def body(o0, o1, o2, l0, l1, l2, ob, oc, ga, gb, gc, la, lb, lc, x_ref, t_ref, mb, wa, wb, wc, wo_ref, gn_ref,
             hs_ref, dout_ref, dla, dlb, dlc, dga, dgb, dgc, do0, do1, do2, dp0, dp1, dp2, dob, doc, err_ref, gg_ref,
             gmb, gwa, gwb, gwc, gwo, acc_a, acc_b, acc_c, acc_o):
        step = pl.program_id(0)
        whole = lambda w_ref: w_ref[...].reshape(D_MODEL, w_ref.shape[-1])
        ws = _group_weights(l0[...], l1[...], l2[...])
        out_a = ws[0] * o0[...] + ws[1] * o1[...] + ws[2] * o2[...]
        branches = ((out_a, ga, la, wa, acc_a, dla, dga), (ob[...], gb, lb, wb, acc_b, dlb, dgb),
                    (oc[...], gc, lc, wc, acc_c, dlc, dgc))

        @pl.when(step == 0)
        def _():
            for acc in (acc_a, acc_b, acc_c, acc_o):
                acc[...] = jnp.zeros(acc.shape, F32)
            err_ref[...] = jnp.zeros((1, D_MODEL), F32)
            gg_ref[...] = jnp.zeros((1, D_MODEL), F32)
            gmb[...] = jnp.zeros((3, D_MODEL), F32)

        y = jnp.zeros((rows, D_MODEL), F32)
        zs, gates, silus, dsilus, us = [], [], [], [], []
        for b, (ov, g_ref, l_ref, w_ref, _, _, _) in enumerate(branches):
            g = g_ref[...].astype(F32)
            sg = _sigmoid(g)
            silus.append(g * sg)
            dsilus.append(sg * (1.0 + g * (1.0 - sg)))
            us.append((ov * silus[b]).astype(BF16))
            zs.append(lax.dot_general(us[b], whole(w_ref), NT, preferred_element_type=F32))
            gates.append(_sigmoid(l_ref[...].astype(F32) + mb[b:b + 1, :]))
            y = y + gates[b] * zs[b]
        yb = y.astype(BF16)
        y2 = jnp.dot(yb, whole(wo_ref), preferred_element_type=F32)
        rstd = lax.rsqrt(jnp.mean(y2 * y2, axis=1, keepdims=True) + EPS)
        yn = y2 * rstd
        gv = gn_ref[...]
        err = x_ref[...] + yn * gv - t_ref[...]
        dout = err * (1.0 / D_MODEL)
        dout_ref[...] = dout
        dn = dout * gv
        dy2 = (rstd * (dn - yn * jnp.mean(dn * yn, axis=1, keepdims=True))).astype(BF16)
        acc_o[...] += lax.dot_general(yb, dy2, TN, preferred_element_type=F32)
        err_ref[...] += jnp.sum(err * err, axis=0, keepdims=True)
        gg_ref[...] += jnp.sum(dout * yn, axis=0, keepdims=True)
        dy = lax.dot_general(dy2, whole(wo_ref), NT, preferred_element_type=F32)
        dos = []
        for b, (ov, _, _, w_ref, acc, dl_ref, dg_ref) in enumerate(branches):
            dl = dy * zs[b] * gates[b] * (1.0 - gates[b])
            dl_ref[...] = dl.astype(BF16)
            gmb[b:b + 1, :] += jnp.sum(dl, axis=0, keepdims=True)
            dz = (dy * gates[b]).astype(BF16)
            acc[...] += lax.dot_general(dz, us[b], TN, preferred_element_type=F32)
            du = jnp.dot(dz, whole(w_ref), preferred_element_type=F32)
            dos.append(du * silus[b])
            dg_ref[...] = (du * ov * dsilus[b]).astype(BF16)
        dob[...] = dos[1].astype(BF16)
        doc[...] = dos[2].astype(BF16)
        row_term = jnp.dot(dos[0] * out_a, hs_ref[...], precision=lax.Precision.HIGHEST, preferred_element_type=F32)
        for wg, do_ref, dp_ref in zip(ws, (do0, do1, do2), (dp0, dp1, dp2)):
            do_ref[...] = wg * dos[0]
            dp_ref[...] = wg * row_term

        @pl.when(step == SEQ // rows - 1)
        def _():
            for acc, out in ((acc_a, gwa), (acc_b, gwb), (acc_c, gwc), (acc_o, gwo)):
                out[...] = acc[...].astype(BF16)

    full = lambda shape: pl.BlockSpec(shape, lambda i: (0,) * len(shape))
    vec = pl.BlockSpec((1, D_MODEL), lambda i: (0, 0))
    acc3 = pl.BlockSpec((3, D_MODEL), lambda i: (0, 0))
    shard = D_MODEL // N_DEV
    dev_rows = lambda width, k: pl.BlockSpec((N_DEV, shard, width), lambda i: (0, k, 0))
    in_specs = ([r512] * 8 + silu_cols + logit_cols + [r1024, r1024, full((3, D_MODEL))]
                + [dev_rows(512, k) for k in range(3)] + [dev_rows(D_MODEL, 1), vec, full((512, 512))])
    out_specs = ([r1024] + [r1024] * 3 + [r512] * 3 + [r512] * 6 + [r512] * 2 + [vec, vec, acc3]
                 + [full((D_MODEL, 512))] * 3 + [full((D_MODEL, D_MODEL))])
    bf, f32 = BF16, F32
    sds = jax.ShapeDtypeStruct
    out_shape = ([sds((SEQ, D_MODEL), f32)] + [sds((SEQ, D_MODEL), bf)] * 3 + [sds((SEQ, 512), bf)] * 3
                 + [sds((SEQ, 512), f32)] * 6 + [sds((SEQ, 512), bf)] * 2 + [sds((1, D_MODEL), f32)] * 2
                 + [sds((3, D_MODEL), f32)] + [sds((D_MODEL, 512), bf)] * 3 + [sds((D_MODEL, D_MODEL), bf)])
    res = pl.pallas_call(
        body, name="gate_block", grid=(SEQ // rows,), in_specs=in_specs, out_specs=out_specs, out_shape=out_shape,
        scratch_shapes=[pltpu.VMEM((D_MODEL, 512), F32)] * 3 + [pltpu.VMEM((D_MODEL, D_MODEL), F32)],
        compiler_params=_params(("arbitrary",)))(
            *o_grp, *l_grp, out_b, out_c, parts, parts, parts, parts, parts, parts, x, target, merge_bias,
            branch_rows, branch_rows, branch_rows, out_rows, gain, head_sum)
    return dict(dout=res[0], dlog=res[1:4], dg=res[4:7], do_grp=res[7:10], dp_grp=res[10:13], do_b=res[13],
                do_c=res[14], err_sq=res[15], g_post=res[16], g_mb=res[17], g_wt=res[18:21], g_w_out=res[21])


def _local_step(x, hst, parts, tabs, bias, mem, target, pre_norm, mem_norm, post_norm, wt_in, late_weights,
                reduce_start=None):
    o_grp, l_grp = [], []
    for g, d in enumerate(DILATIONS):
        o, l = _attn_fwd("dil_fwd_%d" % g, "dil", parts, parts, parts, 12 * g, 12 * g + 4, 12 * g + 8, d=d)
        o_grp.append(o)
        l_grp.append(l)
    out_b, lse_b = _attn_fwd("na_fwd", "na", parts, parts, parts, 36, 40, 44, bias=bias)
    merge_bias, w_kv, branch_rows, out_rows = late_weights(sum(a[:8, :128] for a in [out_b] + o_grp))
    memn = _rmsnorm_fwd("memnorm", mem, mem_norm, MEM_LEN)
    kv_m = _mm_simple("mem_kv", memn, w_kv, NN, BF16, MEM_LEN, 512, D_MODEL)
    out_c, lse_c = _attn_fwd("mem_fwd", "mem", parts, kv_m, kv_m, 48, 0, 4)

    rr = _iota((512, 512), 0) // HEAD_DIM
    cc = _iota((512, 512), 1) // HEAD_DIM
    head_sum = (rr == cc).astype(F32)
    gb = _gate_block(o_grp, l_grp, out_b, out_c, parts, x, target, merge_bias, branch_rows, out_rows, post_norm,
                     head_sum)
    dout, dlog, dg, g_wt, g_w_out = gb["dout"], gb["dlog"], gb["dg"], gb["g_wt"], gb["g_w_out"]
    do_grp, dp_grp, do_b, do_c, g_post, g_mb = (gb["do_grp"], gb["dp_grp"], gb["do_b"], gb["do_c"], gb["g_post"],
                                                gb["g_mb"])
    loss = 0.5 * jnp.sum(gb["err_sq"]) / D_MODEL

    dq_c, dk_m, dv_m = _attn_bwd("mem_bwd", "mem", parts, kv_m, kv_m, 48, 0, 4, do_c, lse_c, o=out_c)
    dkv = jnp.concatenate([dk_m, dv_m], axis=1).astype(BF16)
    g_w_kv = _mm_simple("mem_kv_dw", memn, dkv, TN, BF16, D_MODEL, 512, MEM_LEN)
    dmemn = _mm_simple("mem_kv_dx", dkv, w_kv, NT, F32, MEM_LEN, 512, D_MODEL)
    grads = dict(w_kv=g_w_kv, wt_a=g_wt[0], wt_b=g_wt[1], wt_c=g_wt[2], w_out=g_w_out, merge_bias=g_mb,
                 post_norm=g_post)
    dep = reduce_start("rest_sibling", grads) if reduce_start is not None else None

    dq_b, dk_b, dv_b, dbias = _attn_bwd("na_bwd", "na", parts, parts, parts, 36, 40, 44, do_b, lse_b, o=out_b,
                                        bias=bias, dep=dep)
    dqkv = []
    for g, d in enumerate(DILATIONS):
        dq, dk, dv = _attn_bwd("dil_bwd_%d" % g, "dil", parts, parts, parts, 12 * g, 12 * g + 4, 12 * g + 8,
                               do_grp[g], l_grp[g], dp=dp_grp[g], d=d, tabs=tabs[g])
        dqkv += [dq, dk, dv]
    if reduce_start is not None:
        dep = reduce_start("rest_chips", grads, sum(a[:8, :128] for a in (dqkv[0], dqkv[3], dqkv[6], dq_b)))
    dparts = dqkv + [dq_b, dk_b, dv_b, dq_c] + list(dg) + list(dlog)
    grads["wt_in"] = _in_proj_dw(dparts, hst, dep)
    dep = reduce_start("w_in", grads) if reduce_start is not None else None
    dh = _in_proj_dh(dparts, wt_in, dep)
    if reduce_start is not None:
        dep = reduce_start("w_in_second", grads, dh)
    grad_x, grads["pre_norm"] = _prenorm_bwd(x, pre_norm, dh, dout)
    g_rpb_t = _na_bias_bwd(dbias, dep)
    grads["na_rpb"] = g_rpb_t[:, :15, :31] + jnp.pad(g_rpb_t[:, :14, 64:95], ((0, 0), (1, 0), (0, 0)))
    grads["mem_norm"] = _memnorm_bwd(mem, dmemn, dep)
    return loss, grad_x, grads


ANY = pl.BlockSpec(memory_space=pl.ANY)


def _place():
    return lax.axis_index("x"), lax.axis_index("y"), lax.axis_index("c")


HBM = pl.BlockSpec(memory_space=pltpu.HBM)
SEM = pl.BlockSpec(memory_space=pltpu.SEMAPHORE)
DATAFLOW = pltpu.SideEffectType.DATAFLOW_SIDE_EFFECTING


def _split_copies(kind, srcs, lands, send_sems, recv_sems):
    nt = len(srcs)
    x, y, c = _place()
    copies = []
    if kind == "sibling":
        for q in range(4):
            for t in range(nt):
                k = q * nt + t
                copies.append(pltpu.make_async_remote_copy(
                    src_ref=srcs[t].at[2 * q + 1 - c], dst_ref=lands[t].at[q], send_sem=send_sems.at[k],
                    recv_sem=recv_sems.at[k], device_id=(x, y, 1 - c), device_id_type=MESH_ID))
    elif kind in ("rs_a", "rs_b"):
        half = lands[0].shape[1]
        xn, yn = (1 - x, y, c), (x, 1 - y, c)
        q_xn, q_yn, q_dg = 2 * (1 - x) + y, 2 * x + 1 - y, 2 * (1 - x) + 1 - y
        if kind == "rs_a":
            plan = [(srcs[0].at[q_yn].at[pl.ds(0, half)], 0, yn), (srcs[0].at[q_dg].at[pl.ds(0, half)], 1, yn),
                    (srcs[0].at[q_xn].at[pl.ds(half, half)], 2, xn), (srcs[0].at[q_dg].at[pl.ds(half, half)], 3, xn)]
        else:
            plan = [(srcs[0].at[0], 0, xn), (srcs[0].at[1], 1, yn)]
        for k, (src, slot, to) in enumerate(plan):
            copies.append(pltpu.make_async_remote_copy(
                src_ref=src, dst_ref=lands[0].at[slot], send_sem=send_sems.at[k], recv_sem=recv_sems.at[k],
                device_id=to, device_id_type=MESH_ID))
    elif kind == "gather":
        me = 4 * x + 2 * y + c
        for mask in range(1, 8):
            fx, fy, fc = (mask >> 2) & 1, (mask >> 1) & 1, mask & 1
            to = (1 - x if fx else x, 1 - y if fy else y, 1 - c if fc else c)
            for t in range(nt):
                k = (mask - 1) * nt + t
                copies.append(pltpu.make_async_remote_copy(
                    src_ref=srcs[t], dst_ref=lands[t].at[me], send_sem=send_sems.at[k], recv_sem=recv_sems.at[k],
                    device_id=to, device_id_type=MESH_ID))
    else:
        for s, (tx, ty) in enumerate([(1 - x, y), (x, 1 - y), (1 - x, 1 - y)]):
            for t in range(nt):
                k = s * nt + t
                copies.append(pltpu.make_async_remote_copy(
                    src_ref=srcs[t].at[2 * tx + ty], dst_ref=lands[t].at[s], send_sem=send_sems.at[k],
                    recv_sem=recv_sems.at[k], device_id=(tx, ty, c), device_id_type=MESH_ID))
    return copies


def _split_count(kind, nt):
    return {"gather": 7, "chips": 3, "sibling": 4, "rs_a": 4, "rs_b": 2}[kind] * nt


def _exchange_start(name, kind, srcs, land_shapes, after=None):
    nt = len(srcs)
    n = _split_count(kind, nt)
    dep_specs, dep_args = _dep_operand(after)
    nd = len(dep_args)

    def body(*refs):
        src_refs, land_refs = refs[:nt], refs[nt:2 * nt]
        send_sems, recv_sems = refs[2 * nt + nd], refs[2 * nt + nd + 1]
        token = refs[-1]
        for cp in _split_copies(kind, src_refs, land_refs, send_sems, recv_sems):
            cp.start()
        token[...] = jnp.zeros_like(token)

    lands = [pltpu.with_memory_space_constraint(lax.empty(s.shape, s.dtype), pltpu.HBM) for s in land_shapes]
    res = pl.pallas_call(
        body, name=name,
        out_shape=(pltpu.SemaphoreType.DMA((n,)), pltpu.SemaphoreType.DMA((n,)),
                   *[pltpu.HBM(s.shape, s.dtype) for s in srcs], *[pltpu.HBM(s.shape, s.dtype) for s in land_shapes],
                   jax.ShapeDtypeStruct((8, 128), F32)),
        in_specs=[HBM] * (2 * nt) + dep_specs,
        out_specs=(SEM, SEM, *([HBM] * (2 * nt)), pl.BlockSpec(memory_space=pltpu.VMEM)),
        input_output_aliases={i: 2 + i for i in range(2 * nt)},
        compiler_params=pltpu.CompilerParams(has_side_effects=DATAFLOW))(
            *[pltpu.with_memory_space_constraint(s, pltpu.HBM) for s in srcs], *lands, *dep_args)
    return res[0], res[1], list(res[2:2 + nt]), list(res[2 + nt:2 + 2 * nt]), res[-1]


def _exchange_wait(name, kind, send_sems, recv_sems, srcs, lands, after):
    nt = len(srcs)

    def body(*refs):
        src_refs, land_refs = refs[:nt], refs[nt:2 * nt]
        s_sems, r_sems = refs[2 * nt], refs[2 * nt + 1]
        for cp in _split_copies(kind, src_refs, land_refs, s_sems, r_sems):
            cp.wait_send()
            cp.wait_recv()

    res = pl.pallas_call(
        body, name=name,
        out_shape=tuple(pltpu.HBM(s.shape, s.dtype) for s in list(srcs) + list(lands)),
        in_specs=[HBM] * (2 * nt) + [SEM, SEM, pl.BlockSpec(memory_space=pl.ANY)],
        out_specs=tuple([HBM] * (2 * nt)),
        input_output_aliases={i: i for i in range(2 * nt)},
        compiler_params=pltpu.CompilerParams(has_side_effects=DATAFLOW))(
            *srcs, *lands, send_sems, recv_sems, after)
    return list(res[:nt]), list(res[nt:])


AG_GROUPS = ((0, 3), (3, 4), (7, 2))


def _ag_phase(name, own, land, sems, waits, starts, after=None):
    r = own.shape[0]
    half = r // 2
    ns = len(sems)
    dep_specs, dep_args = _dep_operand(after)
    nd = len(dep_args)
    new_group = None
    if starts:
        (new_group,) = [g for g, (first, n) in enumerate(AG_GROUPS) if first == starts[0]]
        assert list(starts) == list(range(AG_GROUPS[new_group][0], sum(AG_GROUPS[new_group])))

    def body(*refs):
        own_ref, land_ref = refs[0], refs[1]
        sem_refs = list(refs[2:2 + 2 * ns])
        outs = refs[2 + 2 * ns + nd:]
        if starts:
            sem_refs += [outs[0], outs[1]]
        x, y, c = _place()
        me, sib = (x, y, c), (x, y, 1 - c)
        xn, yn, dg = (1 - x, y, c), (x, 1 - y, c), (1 - x, 1 - y, c)

        def other(dev):
            return (dev[0], dev[1], 1 - dev[2])

        def rows(dev, part):
            blk = land_ref.at[4 * dev[0] + 2 * dev[1] + dev[2]]
            return blk if part is None else blk.at[pl.ds(part * half, half)]

        def sem_of(k):
            (g,) = [g for g, (first, n) in enumerate(AG_GROUPS) if first <= k < first + n]
            return sem_refs[2 * g].at[k - AG_GROUPS[g][0]], sem_refs[2 * g + 1].at[k - AG_GROUPS[g][0]]

        sent = {0: (me, None, sib), 1: (me, None, xn), 2: (me, None, yn), 3: (xn, 0, yn), 4: (yn, 1, xn),
                5: (xn, None, sib), 6: (yn, None, sib), 7: (dg, 0, sib), 8: (dg, 1, sib)}
        landed = {0: (sib, None), 1: (xn, None), 2: (yn, None), 3: (dg, 0), 4: (dg, 1), 5: (other(xn), None),
                  6: (other(yn), None), 7: (other(dg), 0), 8: (other(dg), 1)}

        def copy(k, receiving):
            send_sem, recv_sem = sem_of(k)
            dev, part, to = (*landed[k], me) if receiving else sent[k]
            src = own_ref if (dev is me and not receiving) else rows(dev, part)
            return pltpu.make_async_remote_copy(src_ref=src, dst_ref=rows(dev, part), send_sem=send_sem,
                                                recv_sem=recv_sem, device_id=to, device_id_type=MESH_ID)

        for kind, k in waits:
            if kind == "recv":
                copy(k, True).wait_recv()
            else:
                copy(k, False).wait_send()
        for k in starts:
            copy(k, False).start()
        if starts:
            outs[-1][...] = jnp.zeros_like(outs[-1])

    n_new = AG_GROUPS[new_group][1] if starts else 0
    sem_out = (pltpu.SemaphoreType.DMA((n_new,)), pltpu.SemaphoreType.DMA((n_new,))) if starts else ()
    token_out = (jax.ShapeDtypeStruct((8, 128), F32),) if starts else ()
    res = pl.pallas_call(
        body, name=name,
        out_shape=(*sem_out, pltpu.HBM(own.shape, own.dtype), pltpu.HBM(land.shape, land.dtype), *token_out),
        in_specs=[HBM, HBM] + [SEM] * (2 * ns) + dep_specs,
        out_specs=(*([SEM] * len(sem_out)), HBM, HBM, *([pl.BlockSpec(memory_space=pltpu.VMEM)] * len(token_out))),
        input_output_aliases={0: len(sem_out), 1: len(sem_out) + 1},
        compiler_params=pltpu.CompilerParams(has_side_effects=DATAFLOW))(
            own, land, *[a for pair in sems for a in pair], *dep_args)
    if starts:
        return (res[0], res[1]), res[2], res[3], res[4]
    return None, res[0], res[1], None


def _add_sibling(name, term, recv, rows):
    _, r, w = term.shape
    cidx = lax.axis_index("c").astype(jnp.int32).reshape(1)
    like_term = recv.shape[0] == N_DEV

    def body(c_ref, a_ref, b_ref, o_ref):
        o_ref[...] = (a_ref[...].astype(F32) + b_ref[...].astype(F32)).astype(o_ref.dtype)

    grid_spec = pltpu.PrefetchScalarGridSpec(
        num_scalar_prefetch=1, grid=(4, r // rows),
        in_specs=[pl.BlockSpec((None, rows, w), lambda q, i, c_ref: (2 * q + c_ref[0], i, 0)),
                  pl.BlockSpec((None, rows, w), lambda q, i, c_ref: (2 * q + c_ref[0] if like_term else q, i, 0))],
        out_specs=pl.BlockSpec((None, rows, w), lambda q, i, c_ref: (q, i, 0)))
    return pl.pallas_call(
        body, name=name, grid_spec=grid_spec, out_shape=jax.ShapeDtypeStruct((4, r, w), term.dtype),
        compiler_params=_params(("parallel", "parallel")))(cidx, term, recv)


def _add_sibling_small(name, terms, recvs):
    nt = len(terms)

    def body(*refs):
        c = lax.axis_index("c")
        for t_ref, r_ref, o_ref in zip(refs[:nt], refs[nt:2 * nt], refs[2 * nt:]):
            for q in range(4):
                o_ref[q] = (t_ref[2 * q + c].astype(F32) + r_ref[q].astype(F32)).astype(o_ref.dtype)

    return pl.pallas_call(
        body, name=name, out_shape=[jax.ShapeDtypeStruct((4,) + t.shape[1:], t.dtype) for t in terms],
        compiler_params=_params())(*terms, *recvs)


def _reduce_scatter_start(tag, terms, recv1):
    sums = _add_sibling_small("add_sibling_" + tag, terms, recv1)
    lands = [jax.ShapeDtypeStruct((3,) + s.shape[1:], s.dtype) for s in sums]
    send_sems, recv_sems, sums, lands, token = _exchange_start("exchange_chips_start_" + tag, "chips", sums, lands)
    return (tag, send_sems, recv_sems, sums, lands), token


def _reduce_scatter_wait(state, after):
    tag, send_sems, recv_sems, sums, lands = state
    return _exchange_wait("exchange_chips_wait_" + tag, "chips", send_sems, recv_sems, sums, lands, after)


def _adam_math(w, g, m, v):
    nm = ADAM_B1 * m + (1.0 - ADAM_B1) * g
    nv = ADAM_B2 * v + (1.0 - ADAM_B2) * (g * g)
    c1 = 1.0 - ADAM_B1 ** ADAM_STEP
    c2 = 1.0 - ADAM_B2 ** ADAM_STEP
    return -ADAM_LR * ((nm / c1) / (jnp.sqrt(nv / c2) + ADAM_EPS) + ADAM_WD * w), nm, nv


def _presum_halves(sums, landed):
    _, r, w = sums.shape
    rows = r // 2
    x, y = lax.axis_index("x"), lax.axis_index("y")
    dest = jnp.stack([2 * (1 - x) + y, 2 * x + 1 - y]).astype(jnp.int32)

    def body(q_ref, a_ref, b_ref, o_ref):
        o_ref[...] = (a_ref[...].astype(F32) + b_ref[...].astype(F32)).astype(o_ref.dtype)

    grid_spec = pltpu.PrefetchScalarGridSpec(
        num_scalar_prefetch=1, grid=(2,),
        in_specs=[pl.BlockSpec((None, rows, w), lambda h, q_ref: (q_ref[h], h, 0)),
                  pl.BlockSpec((None, rows, w), lambda h, q_ref: (1 + 2 * h, 0, 0))],
        out_specs=pl.BlockSpec((None, rows, w), lambda h, q_ref: (h, 0, 0)))
    return pl.pallas_call(
        body, name="presum_halves", grid_spec=grid_spec, out_shape=jax.ShapeDtypeStruct((2, r // 2, w), sums.dtype),
        compiler_params=_params(("parallel",)))(dest, sums, landed)


def _adamw_halves(name, sums, landed_a, landed_b, w, m, v, rows):
    r, c = w.shape
    half = c // 2
    qidx = (2 * lax.axis_index("x") + lax.axis_index("y")).astype(jnp.int32).reshape(1)

    def body(q_ref, s_ref, a_ref, b_ref, w_ref, m_ref, v_ref, g_ref, d_ref, nm_ref, nv_ref):
        first = (s_ref[:half, :].astype(F32) + a_ref[0].astype(F32)) + b_ref[0].astype(F32)
        second = (s_ref[half:, :].astype(F32) + a_ref[2].astype(F32)) + b_ref[1].astype(F32)
        g = jnp.concatenate([first, second], axis=0).T
        g_ref[...] = g
        d_ref[...], nm_ref[...], nv_ref[...] = _adam_math(w_ref[...], g, m_ref[...], v_ref[...])

    row = pl.BlockSpec((rows, c), lambda i, q_ref: (i, 0))
    grid_spec = pltpu.PrefetchScalarGridSpec(
        num_scalar_prefetch=1, grid=(r // rows,),
        in_specs=[pl.BlockSpec((None, c, rows), lambda i, q_ref: (q_ref[0], 0, i)),
                  pl.BlockSpec((4, half, rows), lambda i, q_ref: (0, 0, i)),
                  pl.BlockSpec((2, half, rows), lambda i, q_ref: (0, 0, i)), row, row, row],
        out_specs=[row] * 4)
    return pl.pallas_call(
        body, name=name, grid_spec=grid_spec, out_shape=[jax.ShapeDtypeStruct((r, c), F32)] * 4,
        compiler_params=_params(("parallel",)))(qidx, sums, landed_a, landed_b, w, m, v)


def _adamw_chips_small(name, items):
    n = len(items)

    def body(*refs):
        q = 2 * lax.axis_index("x") + lax.axis_index("y")
        ins, outs = refs[:5 * n], refs[5 * n:]
        for i, (_, _, w, _, _, transposed) in enumerate(items):
            s_ref, r_ref, w_ref, m_ref, v_ref = ins[5 * i:5 * i + 5]
            g_ref, d_ref, nm_ref, nv_ref = outs[4 * i:4 * i + 4]
            g = (s_ref[q].astype(F32) + r_ref[0].astype(F32)) + (r_ref[1].astype(F32) + r_ref[2].astype(F32))
            g = g.T if transposed else g[:w.shape[0]]
            g_ref[...] = g
            d_ref[...], nm_ref[...], nv_ref[...] = _adam_math(w_ref[...], g, m_ref[...], v_ref[...])

    res = pl.pallas_call(
        body, name=name, out_shape=[jax.ShapeDtypeStruct(it[2].shape, F32) for it in items for _ in range(4)],
        compiler_params=_params())(*[a for it in items for a in it[:5]])
    return [res[4 * i:4 * i + 4] for i in range(n)]


def _adamw_replicated(gathered, items):
    n = len(items)

    def body(g_ref, *refs):
        ins, t_ref, outs = refs[:3 * n], refs[3 * n], refs[3 * n + 1:]
        acc = g_ref[0]
        for j in range(1, N_DEV):
            acc = acc + g_ref[j]
        t_ref[...] = acc
        for i, (first, w, _, _) in enumerate(items):
            w_ref, m_ref, v_ref = ins[3 * i:3 * i + 3]
            g = t_ref[first:first + w.shape[0], :]
            outs[3 * i][...], outs[3 * i + 1][...], outs[3 * i + 2][...] = _adam_math(w_ref[...], g, m_ref[...], v_ref[...])

    res = pl.pallas_call(
        body, name="adamw_replicated",
        out_shape=[jax.ShapeDtypeStruct(gathered.shape[1:], F32)]
        + [jax.ShapeDtypeStruct(it[1].shape, F32) for it in items for _ in range(3)],
        compiler_params=_params())(gathered, *[a for it in items for a in it[1:]])
    return res[0], [res[1 + 3 * i:4 + 3 * i] for i in range(n)]


def _wide_rows(a):
    rows = -(-a.size // D_MODEL)
    return jnp.pad(a.reshape(-1), (0, rows * D_MODEL - a.size)).reshape(rows, D_MODEL)


def kernel(x, mem, pre_norm, w_in, merge_bias, na_rpb, mem_norm, w_mem_kv, w_branch_a, w_branch_b, w_branch_c, w_out, post_norm, loss_target, m_pre_norm, m_w_in, m_merge_bias, m_na_rpb, m_mem_norm, m_w_mem_kv, m_w_branch_a, m_w_branch_b, m_w_branch_c, m_w_out, m_post_norm, v_pre_norm, v_w_in, v_merge_bias, v_na_rpb, v_mem_norm, v_w_mem_kv, v_w_branch_a, v_w_branch_b, v_w_branch_c, v_w_out, v_post_norm):
    wt_in_s = w_in[0].T.astype(BF16)
    rows_s = jnp.concatenate([w_mem_kv[0], w_out[0]], axis=0).astype(BF16)
    cols_s = jnp.concatenate([w_branch_a[0].T, w_branch_b[0].T, w_branch_c[0].T], axis=0).astype(BF16)
    mb_s = jnp.pad(merge_bias[0], ((0, 5), (0, 0)))
    me = 4 * lax.axis_index("x") + 2 * lax.axis_index("y") + lax.axis_index("c")

    chip = 2 * lax.axis_index("x") + lax.axis_index("y")

    def first_block(q):
        return jnp.where(q == 0, 0, jnp.where(q == 1, 6, jnp.where(q == 2, 11, 17)))

    five = jnp.arange(5, dtype=jnp.int32)
    near, far = jnp.where(chip < 2, 5, 16), jnp.where(chip < 2, 16, 5)
    order1 = (first_block(chip) + five).astype(jnp.int32)
    order2 = jnp.concatenate([first_block(chip ^ 1) + five, near[None], first_block(chip ^ 2) + five]).astype(jnp.int32)
    order3 = jnp.concatenate([first_block(chip ^ 3) + five, far[None]]).astype(jnp.int32)
    tabs = _rope_tables()

    def weights_of(land):
        return land.reshape(N_IN, D_MODEL)

    land = pltpu.with_memory_space_constraint(lax.empty((N_DEV,) + wt_in_s.shape, BF16), pltpu.HBM)
    own = pltpu.with_memory_space_constraint(wt_in_s, pltpu.HBM)
    sem_a, own, land, token = _ag_phase("ag_start", own, land, [], [], [0, 1, 2])
    hs, hst = _prenorm_fold(x[0], pre_norm, token)
    _, own, land, _ = _ag_phase("ag_wait0", own, land, [sem_a], [("recv", 0)], [], hs)
    land = lax.dynamic_update_slice(land, own[None], (me, 0, 0))
    parts = _in_proj("in_proj_1", hs, weights_of(land), tabs, order1)
    bias = _na_bias(jnp.pad(na_rpb[0], ((0, 0), (0, 1), (0, 128 - 31))), parts)
    sem_b, own, land, _ = _ag_phase("ag_mid1", own, land, [sem_a], [("recv", 1), ("recv", 2)], [3, 4, 5, 6], bias)
    _, own, land, _ = _ag_phase("ag_wait1", own, land, [sem_a, sem_b], [("recv", 5), ("recv", 6)], [])
    parts = _in_proj("in_proj_2", hs, weights_of(land), tabs, order2, parts)
    sem_c, own, land, _ = _ag_phase("ag_mid2", own, land, [sem_a, sem_b], [("recv", 3), ("recv", 4)], [7, 8], parts)
    _, own, land, _ = _ag_phase("ag_end", own, land, [sem_a, sem_b, sem_c],
                                [("recv", 7), ("recv", 8)] + [("send", k) for k in range(9)], [])
    wt_in = weights_of(land)

    late_own = [rows_s, cols_s, mb_s]
    late_lands = [jax.ShapeDtypeStruct((N_DEV,) + s.shape, s.dtype) for s in late_own]
    l_send, l_recv, late_own, late_lands, late_token = _exchange_start("gather_late_start", "gather", late_own,
                                                                       late_lands, after=wt_in)
    parts = _in_proj("in_proj_3", hs, wt_in, tabs, order3, parts, late_token)

    def late_weights(after):
        own, lands = _exchange_wait("gather_late_wait", "gather", l_send, l_recv, late_own, late_lands, after)
        g_rows, g_cols, g_mb = [lax.dynamic_update_slice(land, o[None], (me, 0, 0)) for land, o in zip(lands, own)]
        return g_mb[:, :3].transpose(1, 0, 2).reshape(3, D_MODEL), g_rows[:, :128].reshape(D_MODEL, D_MODEL), g_cols, g_rows

    rest_state, rest_sibling, w_in_a, w_in_b = [], [], [], []

    def reduce_start(phase, grads, after=None):
        if phase == "rest_sibling":
            gmb_t = jnp.pad(grads["merge_bias"].reshape(3, N_DEV, 128).transpose(1, 0, 2), ((0, 0), (0, 5), (0, 0)))
            terms = [grads["w_kv"].reshape(N_DEV, 128, D_MODEL), grads["w_out"].reshape(N_DEV, 128, D_MODEL),
                     grads["wt_a"].reshape(N_DEV, 128, 512), grads["wt_b"].reshape(N_DEV, 128, 512),
                     grads["wt_c"].reshape(N_DEV, 128, 512), gmb_t]
            lands = [jax.ShapeDtypeStruct((4,) + t.shape[1:], t.dtype) for t in terms]
            started = _exchange_start("exchange_sibling_start_rest", "sibling", terms, lands)
            rest_sibling.extend(started[:4])
            return started[4]
        if phase == "rest_chips":
            s_send, s_recv, terms, lands = rest_sibling
            terms, recv1 = _exchange_wait("exchange_sibling_wait_rest", "sibling", s_send, s_recv, terms, lands, after)
            state, token = _reduce_scatter_start("rest", terms, recv1)
            rest_state.append(state)
            return token
        if phase == "w_in":
            own, sibling = [a.reshape(N_DEV, SHARD_IN, D_MODEL) for a in grads["wt_in"]]
            sums = _add_sibling("add_sibling_w_in", own, sibling, SHARD_IN)
            lands = [jax.ShapeDtypeStruct((4, SHARD_IN // 2, D_MODEL), BF16)]
            w_in_a.extend(_exchange_start("rs_a_start", "rs_a", [sums], lands))
            return w_in_a[4]
        (sums,), (landed_a,) = _exchange_wait("rs_a_wait", "rs_a", w_in_a[0], w_in_a[1], w_in_a[2], w_in_a[3], after)
        lands = [jax.ShapeDtypeStruct((2, SHARD_IN // 2, D_MODEL), BF16)]
        w_in_b.extend(_exchange_start("rs_b_start", "rs_b", [_presum_halves(sums, landed_a)], lands))
        w_in_b.extend([sums, landed_a])
        return w_in_b[4]

    loss_term, grad_x, grads = _local_step(
        x[0], hst, parts, tabs, bias, mem[0], loss_target[0], pre_norm, mem_norm, post_norm, wt_in, late_weights,
        reduce_start=reduce_start)

    replicated = ("pre_norm", "mem_norm", "post_norm", "na_rpb")
    pieces = [_wide_rows(grads[n]) for n in replicated] + [_wide_rows(loss_term)]
    first_rows = [sum(p.shape[0] for p in pieces[:i]) for i in range(len(pieces))]
    small = jnp.concatenate(pieces, axis=0)
    s_send, s_recv, s_own, s_land, s_token = _exchange_start(
        "gather_small_start", "gather", [small], [jax.ShapeDtypeStruct((N_DEV,) + small.shape, F32)])
    grad = {}
    weights = {
        "pre_norm": (pre_norm, m_pre_norm, v_pre_norm), "w_in": (w_in, m_w_in, v_w_in),
        "merge_bias": (merge_bias, m_merge_bias, v_merge_bias), "na_rpb": (na_rpb, m_na_rpb, v_na_rpb),
        "mem_norm": (mem_norm, m_mem_norm, v_mem_norm), "w_mem_kv": (w_mem_kv, m_w_mem_kv, v_w_mem_kv),
        "w_branch_a": (w_branch_a, m_w_branch_a, v_w_branch_a), "w_branch_b": (w_branch_b, m_w_branch_b, v_w_branch_b),
        "w_branch_c": (w_branch_c, m_w_branch_c, v_w_branch_c), "w_out": (w_out, m_w_out, v_w_out),
        "post_norm": (post_norm, m_post_norm, v_post_norm)}
    order = ["pre_norm", "w_in", "merge_bias", "na_rpb", "mem_norm", "w_mem_kv", "w_branch_a", "w_branch_b",
             "w_branch_c", "w_out", "post_norm"]
    delta, new_m, new_v = {}, {}, {}

    sums, recv2 = _reduce_scatter_wait(rest_state[0], s_token)
    rest = (("w_mem_kv", False), ("w_out", False), ("w_branch_a", True), ("w_branch_b", True), ("w_branch_c", True),
            ("merge_bias", False))
    items = [(sums[i], recv2[i], *[a[0] for a in weights[n]], transposed) for i, (n, transposed) in enumerate(rest)]
    for (n, _), (g, dl, nm, nv) in zip(rest, _adamw_chips_small("adamw_rest", items)):
        grad[n], delta[n], new_m[n], new_v[n] = g[None], dl[None], nm[None], nv[None]
    s_own, s_land = _exchange_wait("gather_small_wait", "gather", s_send, s_recv, s_own, s_land, delta["w_out"])
    items = [(first, *[_wide_rows(a) for a in weights[n]]) for n, first in zip(replicated, first_rows)]
    total, updates = _adamw_replicated(lax.dynamic_update_slice(s_land[0], s_own[0][None], (me, 0, 0)), items)
    loss = total[first_rows[-1], 0]
    for n, first, it, (dl, nm, nv) in zip(replicated, first_rows, items, updates):
        w = weights[n][0]
        grad[n], delta[n], new_m[n], new_v[n] = [
            a.reshape(-1)[:w.size].reshape(w.shape) for a in (total[first:first + it[1].shape[0]], dl, nm, nv)]
    _, (landed_b,) = _exchange_wait("rs_b_wait", "rs_b", w_in_b[0], w_in_b[1], w_in_b[2], w_in_b[3], updates[-1][0])
    g, dl, nm, nv = _adamw_halves("adamw_w_in", w_in_b[5], w_in_b[6], landed_b, w_in[0], m_w_in[0], v_w_in[0], 256)
    grad["w_in"], delta["w_in"], new_m["w_in"], new_v["w_in"] = g[None], dl[None], nm[None], nv[None]

    return (loss, grad_x[None], *[grad[n] for n in order], *[delta[n] for n in order],
            *[new_m[n] for n in order], *[new_v[n] for n in order])
```

```python
import functools

import numpy as np
import jax
import jax.numpy as jnp
from jax import lax
from jax.experimental import pallas as pl
from jax.experimental.pallas import tpu as pltpu

F32 = jnp.float32
BF16 = jnp.bfloat16

SEQ = 2048
D_MODEL = 1024
N_IN = 11264
N_DEV = 8
SHARD_IN = N_IN // N_DEV
HEAD_DIM = 64
GRID_W = 64
NA_ROWS = 8
MEM_LEN = 256
DILATIONS = (1, 4, 16)
REACH = 64
ROPE_THETA = 500000.0
ROPE_DIM = 16
EPS = 1e-6
NEG = -1e30
ADAM_LR = 0.001
ADAM_B1 = 0.9
ADAM_B2 = 0.999
ADAM_EPS = 1e-08
ADAM_WD = 0.01
ADAM_STEP = 10

VMEM_LIMIT_BYTES = 56 * 1024 * 1024
MESH_ID = pl.DeviceIdType.MESH

NN = (((1,), (0,)), ((), ()))
NT = (((1,), (1,)), ((), ()))
TN = (((0,), (0,)), ((), ()))


def _params(sem=None):
    return pltpu.CompilerParams(dimension_semantics=sem, vmem_limit_bytes=VMEM_LIMIT_BYTES)


def _iota(shape, dim):
    return lax.broadcasted_iota(jnp.int32, shape, dim)


def _sigmoid(x):
    return 1.0 / (1.0 + jnp.exp(-x))


def _rope_tables():
    half = ROPE_DIM // 2
    inv = (ROPE_THETA ** (-np.arange(half, dtype=np.float64) * 2.0 / ROPE_DIM)).astype(np.float32)
    pos = np.arange(SEQ, dtype=np.float32)
    ang = pos[:, None] * inv[None, :]
    cos, sin = np.cos(ang), np.sin(ang)
    zeros = np.zeros_like(cos)
    rest = HEAD_DIM - ROPE_DIM
    c64 = np.concatenate([cos, cos, np.ones((SEQ, rest), np.float32)], axis=1)
    s1 = np.concatenate([zeros, sin, np.zeros((SEQ, rest), np.float32)], axis=1)
    s2 = np.concatenate([-sin, zeros, np.zeros((SEQ, rest), np.float32)], axis=1)

    def fold(t, d):
        return t.reshape(SEQ // d, d, t.shape[1]).transpose(1, 0, 2).reshape(SEQ, t.shape[1])

    tabs = [np.stack([np.tile(fold(t, d), (1, 2)) for t in (c64, s1, s2)], axis=0) for d in DILATIONS]
    return jnp.asarray(np.stack(tabs, axis=0), dtype=F32)


def _rope(a, c, s1, s2):
    return a * c + pltpu.roll(a, 8, 1) * s1 + pltpu.roll(a, 120, 1) * s2


def _rope_t(a, c, s1, s2):
    return a * c + pltpu.roll(a * s1, 120, 1) + pltpu.roll(a * s2, 8, 1)


def _perm_of_block(j):
    return jnp.where(j < 3, 0, jnp.where(j < 6, 1, jnp.where(j < 9, 2, 0)))


def _mm(name, a, b, out_shape, out_dtype, grid, a_spec, b_spec, o_spec, acc_shape, dims, k_axis, nk):
    def body(a_ref, b_ref, o_ref, acc_ref):
        k = pl.program_id(k_axis)

        @pl.when(k == 0)
        def _():
            acc_ref[...] = jnp.zeros(acc_shape, F32)

        acc_ref[...] += lax.dot_general(a_ref[...], b_ref[...], dims, preferred_element_type=F32)

        @pl.when(k == nk - 1)
        def _():
            o_ref[...] = acc_ref[...].astype(out_dtype)

    sem = tuple("arbitrary" if ax == k_axis else "parallel" for ax in range(len(grid)))
    return pl.pallas_call(
        body, name=name, grid=grid, in_specs=[a_spec, b_spec], out_specs=o_spec,
        out_shape=jax.ShapeDtypeStruct(out_shape, out_dtype),
        scratch_shapes=[pltpu.VMEM(acc_shape, F32)], compiler_params=_params(sem))(a, b)


def _mm_simple(name, a, b, dims, out_dtype, tm, tn, tk):
    if dims is NN:
        m, kk = a.shape
        n = b.shape[1]
        a_spec = pl.BlockSpec((tm, tk), lambda i, j, k: (i, k))
        b_spec = pl.BlockSpec((tk, tn), lambda i, j, k: (k, j))
    elif dims is NT:
        m, kk = a.shape
        n = b.shape[0]
        a_spec = pl.BlockSpec((tm, tk), lambda i, j, k: (i, k))
        b_spec = pl.BlockSpec((tn, tk), lambda i, j, k: (j, k))
    else:
        kk, m = a.shape
        n = b.shape[1]
        a_spec = pl.BlockSpec((tk, tm), lambda i, j, k: (k, i))
        b_spec = pl.BlockSpec((tk, tn), lambda i, j, k: (k, j))
    grid = (m // tm, n // tn, kk // tk)
    o_spec = pl.BlockSpec((tm, tn), lambda i, j, k: (i, j))
    return _mm(name, a, b, (m, n), out_dtype, grid, a_spec, b_spec, o_spec, (tm, tn), dims, 2, kk // tk)


def _rmsnorm_fwd(name, x, gain, rows):
    n, d = x.shape

    def body(x_ref, g_ref, o_ref):
        xv = x_ref[...]
        rstd = lax.rsqrt(jnp.mean(xv * xv, axis=1, keepdims=True) + EPS)
        o_ref[...] = (xv * rstd * g_ref[...]).astype(BF16)

    return pl.pallas_call(
        body, name=name, grid=(n // rows,),
        in_specs=[pl.BlockSpec((rows, d), lambda i: (i, 0)), pl.BlockSpec((1, d), lambda i: (0, 0))],
        out_specs=pl.BlockSpec((rows, d), lambda i: (i, 0)),
        out_shape=jax.ShapeDtypeStruct((n, d), BF16), compiler_params=_params(("parallel",)))(x, gain)


def _folded_rows(first, rows, d):
    if d == 1:
        return pl.ds(pl.multiple_of(first, rows), rows)
    mlen = SEQ // d
    return pl.ds((first % mlen) * d + first // mlen, rows, stride=d)


def _prenorm_fold(x, gain, dep=None):
    rows = 128
    nchunk = D_MODEL // 128
    dep_specs, dep_args = _dep_operand(dep)

    def body(*refs):
        x_refs, g_ref, hs_ref, hst_ref = refs[:nchunk], refs[nchunk], refs[-2], refs[-1]
        first = pl.program_id(0) * rows
        for p, d in enumerate(DILATIONS):
            idx = _folded_rows(first, rows, d)
            xv = jnp.concatenate([r[idx, :] for r in x_refs], axis=1)
            rstd = lax.rsqrt(jnp.mean(xv * xv, axis=1, keepdims=True) + EPS)
            h = xv * rstd * g_ref[...]
            hs_ref[p] = h.astype(BF16)
            hst_ref[p] = h.T.astype(BF16)

    x_specs = [pl.BlockSpec((SEQ, 128), functools.partial(lambda c, i: (0, c), c)) for c in range(nchunk)]
    return pl.pallas_call(
        body, name="prenorm", grid=(SEQ // rows,),
        in_specs=x_specs + [pl.BlockSpec((1, D_MODEL), lambda i: (0, 0))] + dep_specs,
        out_specs=[pl.BlockSpec((3, rows, D_MODEL), lambda i: (0, i, 0)),
                   pl.BlockSpec((3, D_MODEL, rows), lambda i: (0, 0, i))],
        out_shape=[jax.ShapeDtypeStruct((3, SEQ, D_MODEL), BF16), jax.ShapeDtypeStruct((3, D_MODEL, SEQ), BF16)],
        compiler_params=_params(("parallel",)))(*([x] * nchunk), gain, *dep_args)


def _prenorm_bwd(x, gain, dh, dout):
    rows = 512

    def body(x_ref, g_ref, a_ref, do_ref, dx_ref, gg_ref):
        xv = x_ref[...]
        rstd = lax.rsqrt(jnp.mean(xv * xv, axis=1, keepdims=True) + EPS)
        xn = xv * rstd
        dh = jnp.concatenate([a_ref[c] for c in range(D_MODEL // 128)], axis=1)
        gdh = dh * g_ref[...]
        dx_ref[...] = rstd * (gdh - xn * jnp.mean(gdh * xn, axis=1, keepdims=True)) + do_ref[...]

        @pl.when(pl.program_id(0) == 0)
        def _():
            gg_ref[...] = jnp.zeros((1, D_MODEL), F32)

        gg_ref[...] += jnp.sum(dh * xn, axis=0, keepdims=True)

    row = pl.BlockSpec((rows, D_MODEL), lambda i: (i, 0))
    vec = pl.BlockSpec((1, D_MODEL), lambda i: (0, 0))
    return pl.pallas_call(
        body, name="prenorm_bwd", grid=(SEQ // rows,),
        in_specs=[row, vec, pl.BlockSpec((D_MODEL // 128, rows, 128), lambda i: (0, i, 0)), row], out_specs=[row, vec],
        out_shape=[jax.ShapeDtypeStruct((SEQ, D_MODEL), F32), jax.ShapeDtypeStruct((1, D_MODEL), F32)],
        compiler_params=_params(("arbitrary",)))(x, gain, dh, dout)


def _memnorm_bwd(mem, dmemn, dep=None):
    dep_specs, dep_args = _dep_operand(dep)

    def body(m_ref, d_ref, *rest):
        mv = m_ref[...]
        rstd = lax.rsqrt(jnp.mean(mv * mv, axis=1, keepdims=True) + EPS)
        rest[-1][...] = jnp.sum(d_ref[...] * mv * rstd, axis=0, keepdims=True)

    whole = pl.BlockSpec(memory_space=pltpu.VMEM)
    return pl.pallas_call(
        body, name="memnorm_bwd", in_specs=[whole, whole] + dep_specs,
        out_shape=jax.ShapeDtypeStruct((1, D_MODEL), F32), compiler_params=_params())(mem, dmemn, *dep_args)


def _dep_operand(dep):
    return ([], []) if dep is None else ([pl.BlockSpec(memory_space=pl.ANY)], [dep])


def _in_proj(name, hs, wt, tabs, order, prev=None, dep=None):
    tm, tn = 512, 512
    prev_specs, prev_args = ([], []) if prev is None else ([ANY], [prev])
    dep_specs, dep_args = _dep_operand(dep)

    def body(order_ref, h_ref, w_ref, t_ref, *rest):
        o_ref = rest[-1]
        j = order_ref[pl.program_id(0)]
        is_rope = jnp.logical_and(j < 9, j % 3 != 2)
        row_slices = [slice(r * tm, (r + 1) * tm) for r in range(SEQ // tm)]

        def product(rs):
            return lax.dot_general(h_ref[rs, :], w_ref[...], NT, preferred_element_type=F32)

        @pl.when(is_rope)
        def _():
            for rs in row_slices:
                acc = product(rs)
                c, s1, s2 = t_ref[0, rs, :], t_ref[1, rs, :], t_ref[2, rs, :]
                for q in range(tn // 128):
                    a = acc[:, q * 128:(q + 1) * 128]
                    o_ref[rs, q * 128:(q + 1) * 128] = _rope(a, c, s1, s2).astype(BF16)

        @pl.when(jnp.logical_not(is_rope))
        def _():
            for rs in row_slices:
                o_ref[rs, :] = product(rs).astype(BF16)

    grid_spec = pltpu.PrefetchScalarGridSpec(
        num_scalar_prefetch=1, grid=(order.shape[0],),
        in_specs=[pl.BlockSpec((None, SEQ, D_MODEL), lambda t, o: (_perm_of_block(o[t]), 0, 0)),
                  pl.BlockSpec((tn, D_MODEL), lambda t, o: (o[t], 0)),
                  pl.BlockSpec((None, 3, SEQ, 128), lambda t, o: (_perm_of_block(o[t]), 0, 0, 0))] + prev_specs
        + dep_specs,
        out_specs=pl.BlockSpec((SEQ, tn), lambda t, o: (0, o[t])))
    return pl.pallas_call(
        body, name=name, grid_spec=grid_spec, out_shape=jax.ShapeDtypeStruct((SEQ, N_IN), BF16),
        input_output_aliases={} if prev is None else {4: 0},
        compiler_params=_params(("arbitrary",)))(order, hs, wt, tabs, *prev_args, *dep_args)


def _piece_blocks(pieces):
    return [(a, h * 512) for a, p in enumerate(pieces) for h in range(p.shape[1] // 512)]


def _block_fetch(piece_refs, blocks, buf, sem):
    def start(block, slot):
        for b, (a, col) in enumerate(blocks):
            @pl.when(block == b)
            def _():
                pltpu.make_async_copy(piece_refs[a].at[:, pl.ds(col, 512)], buf.at[slot], sem.at[slot]).start()

    def wait(slot):
        pltpu.make_async_copy(piece_refs[0].at[:, pl.ds(0, 512)], buf.at[slot], sem.at[slot]).wait()

    return start, wait


def _in_proj_dw(pieces, hst, dep=None):
    tn = 512
    blocks = _piece_blocks(pieces)
    nblk = len(blocks)
    npc = len(pieces)
    dep_specs, dep_args = _dep_operand(dep)

    def body(h_ref, *rest):
        piece_refs = rest[:npc]
        own_out, mirror, buf, sem, out_buf, send_sems, recv_sem, local_sems = rest[-8:]
        j = pl.program_id(0)
        slot = j % 2
        start, wait = _block_fetch(piece_refs, blocks, buf, sem)
        x, y, c = _place()

        def rows_of(step):
            return pl.ds(pl.multiple_of(step * tn, tn), tn)

        def to_sibling(step, slot_):
            return pltpu.make_async_remote_copy(
                src_ref=out_buf.at[slot_], dst_ref=mirror.at[rows_of(step)],
                send_sem=send_sems.at[slot_], recv_sem=recv_sem, device_id=(x, y, 1 - c), device_id_type=MESH_ID)

        def to_own(step, slot_):
            return pltpu.make_async_copy(out_buf.at[slot_], own_out.at[rows_of(step)], local_sems.at[slot_])

        @pl.when(j == 0)
        def _():
            start(j, slot)

        wait(slot)

        @pl.when(j + 1 < nblk)
        def _():
            start(j + 1, 1 - slot)

        acc = jnp.dot(h_ref[...], buf[slot], preferred_element_type=F32)

        @pl.when(j >= 2)
        def _():
            to_sibling(j - 2, slot).wait_send()
            to_own(j - 2, slot).wait()

        out_buf[slot] = acc.T.astype(BF16)
        to_sibling(j, slot).start()
        to_own(j, slot).start()

        @pl.when(j == nblk - 1)
        def _():
            to_sibling(j - 1, 1 - slot).wait_send()
            to_own(j - 1, 1 - slot).wait()
            to_sibling(j, slot).wait_send()
            to_own(j, slot).wait()
            pltpu.make_async_remote_copy(src_ref=mirror, dst_ref=mirror, send_sem=send_sems.at[0], recv_sem=recv_sem,
                                         device_id=(x, y, 1 - c), device_id_type=MESH_ID).wait_recv()

    return pl.pallas_call(
        body, name="in_proj_dw", grid=(nblk,),
        in_specs=[pl.BlockSpec((None, D_MODEL, SEQ), lambda j: (_perm_of_block(j), 0, 0))] + [ANY] * npc + dep_specs,
        out_specs=[ANY, ANY],
        out_shape=[jax.ShapeDtypeStruct((N_IN, D_MODEL), BF16), jax.ShapeDtypeStruct((N_IN, D_MODEL), BF16)],
        scratch_shapes=[pltpu.VMEM((2, SEQ, tn), BF16), pltpu.SemaphoreType.DMA((2,)),
                        pltpu.VMEM((2, tn, D_MODEL), BF16), pltpu.SemaphoreType.DMA((2,)), pltpu.SemaphoreType.DMA,
                        pltpu.SemaphoreType.DMA((2,))],
        compiler_params=_params(("arbitrary",)))(hst, *pieces, *dep_args)


def _in_proj_dh(pieces, wt, dep=None):
    tk = 512
    blocks = _piece_blocks(pieces)
    nblk = len(blocks)
    npc = len(pieces)
    nchunk = D_MODEL // 128

    def col(s):
        return jnp.where(s < 3, s, jnp.where(s < 16, s + 6, s - 13))

    dep_specs, dep_args = _dep_operand(dep)

    def body(w_ref, *rest):
        piece_refs = rest[:npc]
        o_ref, acc_ref, buf, sem = rest[-4:]
        s = pl.program_id(0)
        slot = s % 2
        start, wait = _block_fetch(piece_refs, blocks, buf, sem)

        @pl.when(s == 0)
        def _():
            start(col(s), slot)

        wait(slot)

        @pl.when(s + 1 < nblk)
        def _():
            start(col(s + 1), 1 - slot)

        row_slices = [slice(r * 512, (r + 1) * 512) for r in range(SEQ // 512)]

        def product(rs):
            return jnp.dot(buf[slot, rs, :], w_ref[...], preferred_element_type=F32)

        def accumulate(cond, to_out, init):
            @pl.when(cond)
            def _():
                for rs in row_slices:
                    prod = product(rs)
                    if not to_out:
                        if init:
                            acc_ref[rs, :] = prod
                        else:
                            acc_ref[rs, :] += prod
                        continue
                    for c in range(nchunk):
                        if init:
                            o_ref[c, rs, :] = prod[:, c * 128:(c + 1) * 128]
                        else:
                            o_ref[c, rs, :] += prod[:, c * 128:(c + 1) * 128]

        accumulate(s == 0, True, True)
        accumulate(jnp.logical_and(s > 0, s < 16), True, False)
        accumulate(jnp.logical_or(s == 16, s == 19), False, True)
        accumulate(jnp.logical_and(s > 16, s != 19), False, False)
        for last, d in ((18, 4), (21, 16)):
            @pl.when(s == last)
            def _():
                mlen = SEQ // d
                for r in range(d):
                    for c in range(nchunk):
                        o_ref[c, pl.ds(r, mlen, stride=d), :] += acc_ref[r * mlen:(r + 1) * mlen,
                                                                         c * 128:(c + 1) * 128]

    return pl.pallas_call(
        body, name="in_proj_dh", grid=(nblk,),
        in_specs=[pl.BlockSpec((tk, D_MODEL), lambda s: (col(s), 0))] + [ANY] * npc + dep_specs,
        out_specs=pl.BlockSpec((nchunk, SEQ, 128), lambda s: (0, 0, 0)),
        out_shape=jax.ShapeDtypeStruct((nchunk, SEQ, 128), F32),
        scratch_shapes=[pltpu.VMEM((SEQ, D_MODEL), F32), pltpu.VMEM((2, SEQ, tk), BF16),
                        pltpu.SemaphoreType.DMA((2,))],
        compiler_params=_params(("arbitrary",)))(wt, *pieces, *dep_args)


def _head_lanes(lanes, hh):
    return lanes >= 64 if hh == 1 else lanes < 64


def _head_rows(x, lanes, hh, pair):
    if not pair:
        return jnp.max(x, axis=1, keepdims=True)
    return jnp.max(jnp.where(_head_lanes(lanes, hh), x, -jnp.inf), axis=1, keepdims=True)


def _mask_head(x, lanes, hh, pair, scale=1.0):
    if not pair:
        return x
    xf = x.astype(F32) if scale == 1.0 else x.astype(F32) * scale
    return jnp.where(_head_lanes(lanes, hh), xf, 0.0).astype(BF16)


def _window(mode, qi, tq, mlen, tk):
    if mode == "dil":
        q0 = qi * tq
        seg = (q0 // mlen) * mlen
        ks = jnp.clip(q0 - REACH, seg, seg + mlen - tk)
        return pl.multiple_of(ks, 64)
    if mode == "na":
        r_start = jnp.clip(qi - NA_ROWS // 2, 0, SEQ // GRID_W - NA_ROWS)
        return pl.multiple_of(r_start * GRID_W, 64)
    return 0


def _band_mask(qi, tq, tk, ks):
    qpos = qi * tq + _iota((tq, tk), 0)
    kpos = ks + _iota((tq, tk), 1)
    return jnp.where(jnp.abs(qpos - kpos) <= REACH, 0.0, NEG).astype(F32)


def _stack_heads(x, lanes, pair, scale=1.0):
    if not pair:
        return x
    return jnp.concatenate([_mask_head(x, lanes, hh, pair, scale) for hh in range(2)], axis=0)


def _stack_rows(x, lanes, pair):
    if not pair:
        return _head_rows(x, lanes, 0, pair)
    return jnp.concatenate([_head_rows(x, lanes, hh, pair) for hh in range(2)], axis=0)


def _unstack_heads(x, lanes, pair, tq):
    if not pair:
        return x
    return jnp.where(lanes < 64, x[:tq], x[tq:])


def _scores(mode, qst, k, sscale, band, qi, bias_ref, pair):
    s = lax.dot_general(qst, k, NT, preferred_element_type=F32)
    if sscale != 1.0:
        s = s * sscale
    if mode == "dil":
        s = s + jnp.concatenate([band, band], axis=0)
    elif mode == "na":
        off = qi - jnp.clip(qi - NA_ROWS // 2, 0, SEQ // GRID_W - NA_ROWS)
        s = s + jnp.concatenate([bias_ref[0, off], bias_ref[1, off]], axis=0)
    return s


def _attn_cfg(mode, d):
    if mode == "dil":
        mlen = SEQ // d
        return dict(pair=True, tq=128, tk=min(256, mlen), mlen=mlen, lk=SEQ, scale=HEAD_DIM ** -0.5, units=4,
                    nsub=ATTN_SUBTILES)
    if mode == "na":
        return dict(pair=True, tq=GRID_W, tk=NA_ROWS * GRID_W, mlen=SEQ, lk=SEQ, scale=HEAD_DIM ** -0.5, units=4,
                    nsub=2 * ATTN_SUBTILES)
    return dict(pair=False, tq=128, tk=MEM_LEN, mlen=SEQ, lk=MEM_LEN, scale=128 ** -0.5, units=4,
                nsub=ATTN_SUBTILES)


ATTN_SUBTILES = 16


def _attn_fwd(name, mode, q_arr, k_arr, v_arr, qcol, kcol, vcol, d=1, bias=None):
    cfg = _attn_cfg(mode, d)
    pair, tq, tk, mlen, lk, scale = cfg["pair"], cfg["tq"], cfg["tk"], cfg["mlen"], cfg["lk"], cfg["scale"]
    qscale, sscale = (scale, 1.0) if pair else (1.0, scale)
    nsub = cfg["nsub"]
    rows = nsub * tq

    def body(*refs):
        if mode == "na":
            q_ref, k_ref, v_ref, bias_ref, o_ref, l_ref = refs
        else:
            q_ref, k_ref, v_ref, o_ref, l_ref = refs
            bias_ref = None
        lanes = _iota((tq, 128), 1)
        qis = [pl.program_id(1) * nsub + sub for sub in range(nsub)]
        kss = [_window(mode, qi, tq, mlen, tk) for qi in qis]
        vs = [v_ref[pl.ds(ks, tk), :] for ks in kss]
        bands = [_band_mask(qi, tq, tk, ks) if mode == "dil" else None for qi, ks in zip(qis, kss)]
        ss = []
        for sub in range(nsub):
            qst = _stack_heads(q_ref[sub * tq:(sub + 1) * tq, :], lanes, pair, qscale)
            k = k_ref[pl.ds(kss[sub], tk), :]
            ss.append(_scores(mode, qst, k, sscale, bands[sub], qis[sub], bias_ref, pair))
        ms = [jnp.max(s_, axis=1, keepdims=True) for s_ in ss]
        ps = [jnp.exp(s_ - m) for s_, m in zip(ss, ms)]
        ls = [jnp.sum(p, axis=1, keepdims=True) for p in ps]
        os_ = [jnp.dot(p.astype(BF16), v, preferred_element_type=F32) for p, v in zip(ps, vs)]
        for sub in range(nsub):
            out = _unstack_heads(os_[sub] / ls[sub], lanes, pair, tq)
            lse = ms[sub] + jnp.log(ls[sub])
            lse = _unstack_heads(jnp.broadcast_to(lse, (lse.shape[0], 128)), lanes, pair, tq)
            dst = _folded_rows(qis[sub] * tq, tq, d) if mode == "dil" else slice(sub * tq, (sub + 1) * tq)
            o_ref[dst, :] = out
            l_ref[dst, :] = lse

    in_specs = [pl.BlockSpec((rows, 128), lambda u, i: (i, qcol + u)),
                pl.BlockSpec((lk, 128), lambda u, i: (0, kcol + u)),
                pl.BlockSpec((lk, 128), lambda u, i: (0, vcol + u))]
    args = [q_arr, k_arr, v_arr]
    if mode == "na":
        in_specs.append(pl.BlockSpec((2, NA_ROWS, GRID_W, NA_ROWS * GRID_W), lambda u, i: (u, 0, 0, 0)))
        args.append(bias)
    if mode == "dil":
        out_spec = pl.BlockSpec((SEQ, 128), lambda u, i: (0, u))
    else:
        out_spec = pl.BlockSpec((rows, 128), lambda u, i: (i, u))
    return pl.pallas_call(
        body, name=name, grid=(cfg["units"], SEQ // rows), in_specs=in_specs, out_specs=[out_spec, out_spec],
        out_shape=[jax.ShapeDtypeStruct((SEQ, 512), F32), jax.ShapeDtypeStruct((SEQ, 512), F32)],
        compiler_params=_params(("parallel", "arbitrary")))(*args)


def _attn_bwd(name, mode, q_arr, k_arr, v_arr, qcol, kcol, vcol, do, lse, dp=None, o=None, d=1, bias=None,
              tabs=None, dep=None):
    cfg = _attn_cfg(mode, d)
    pair, tq, tk, mlen, lk, scale = cfg["pair"], cfg["tq"], cfg["tk"], cfg["mlen"], cfg["lk"], cfg["scale"]
    qscale, sscale = (scale, 1.0) if pair else (1.0, scale)
    nsub = cfg["nsub"]
    rows = nsub * tq
    nq = SEQ // rows
    kv_dtype = F32 if mode == "mem" else BF16
    dep_specs, dep_args = _dep_operand(dep)
    mode_inputs = {"dil": 3, "na": 2, "mem": 1}[mode]

    def body(*refs):
        refs = list(refs)
        q_ref, k_ref, v_ref, do_ref, l_ref = refs[:5]
        rest = refs[5:5 + mode_inputs] + refs[5 + mode_inputs + len(dep_args):]
        bias_ref = tq_ref = tk_ref = db_ref = None
        if mode == "dil":
            dp_ref, tq_ref, tk_ref, dq_ref, dk_ref, dv_ref, dk_acc, dv_acc = rest
        elif mode == "na":
            o_ref, bias_ref, dq_ref, dk_ref, dv_ref, db_ref, dk_acc, dv_acc = rest
        else:
            o_ref, dq_ref, dk_ref, dv_ref, dk_acc, dv_acc = rest
        step = pl.program_id(1)

        @pl.when(step == 0)
        def _():
            dk_acc[...] = jnp.zeros((lk, 128), F32)
            dv_acc[...] = jnp.zeros((lk, 128), F32)
            if mode == "na":
                db_ref[...] = jnp.zeros(db_ref.shape, F32)

        lanes = _iota((tq, 128), 1)
        qis = [step * nsub + sub for sub in range(nsub)]
        sls = [slice(sub * tq, (sub + 1) * tq) for sub in range(nsub)]
        kss = [_window(mode, qi, tq, mlen, tk) for qi in qis]
        ks_ = [k_ref[pl.ds(ks, tk), :] for ks in kss]
        vs = [v_ref[pl.ds(ks, tk), :] for ks in kss]
        qsts, dosts, lses, dphs = [], [], [], []
        for sub in range(nsub):
            if mode == "dil":
                src = _folded_rows(qis[sub] * tq, tq, d)
                dov = do_ref[src, :].astype(BF16)
                lsev = l_ref[src, :]
                dphs.append(_stack_rows(dp_ref[src, :], lanes, pair))
            else:
                dov = do_ref[sls[sub], :]
                lsev = l_ref[sls[sub], :]
                dpv = dov.astype(F32) * o_ref[sls[sub], :]
                if pair:
                    dphs.append(jnp.concatenate(
                        [jnp.sum(jnp.where(_head_lanes(lanes, hh), dpv, 0.0), axis=1, keepdims=True)
                         for hh in range(2)], axis=0))
                else:
                    dphs.append(jnp.sum(dpv, axis=1, keepdims=True))
            qsts.append(_stack_heads(q_ref[sls[sub], :], lanes, pair, qscale))
            dosts.append(_stack_heads(dov, lanes, pair))
            lses.append(_stack_rows(lsev, lanes, pair))
        bands = [_band_mask(qi, tq, tk, ks) if mode == "dil" else None for qi, ks in zip(qis, kss)]
        ss = [_scores(mode, qsts[sub], ks_[sub], sscale, bands[sub], qis[sub], bias_ref, pair) for sub in range(nsub)]
        dpms = [lax.dot_general(dosts[sub], vs[sub], NT, preferred_element_type=F32) for sub in range(nsub)]
        ps = [jnp.exp(s_ - lse) for s_, lse in zip(ss, lses)]
        dss = [p * (dpm - dph) for p, dpm, dph in zip(ps, dpms, dphs)]
        if mode == "na":
            for sub, ds in enumerate(dss):
                off = qis[sub] - jnp.clip(qis[sub] - NA_ROWS // 2, 0, SEQ // GRID_W - NA_ROWS)
                db_ref[0, off] += ds[:tq]
                db_ref[1, off] += ds[tq:]
        dsbs = [ds.astype(BF16) for ds in dss]
        dvs = [lax.dot_general(p.astype(BF16), dosts[sub], TN, preferred_element_type=F32)
               for sub, p in enumerate(ps)]
        dqs = [jnp.dot(dsb, ks_[sub], preferred_element_type=F32) * scale for sub, dsb in enumerate(dsbs)]
        dks = [lax.dot_general(dsb, qsts[sub], TN, preferred_element_type=F32) for sub, dsb in enumerate(dsbs)]
        for sub in range(nsub):
            sl = sls[sub]
            dq = _unstack_heads(dqs[sub], lanes, pair, tq)
            if mode == "dil":
                dq = _rope_t(dq, tq_ref[0, sl, :], tq_ref[1, sl, :], tq_ref[2, sl, :])
            dq_ref[sl, :] = dq.astype(BF16)
            dk_acc[pl.ds(kss[sub], tk), :] += dks[sub] if pair else dks[sub] * scale
            dv_acc[pl.ds(kss[sub], tk), :] += dvs[sub]

        @pl.when(step == nq - 1)
        def _():
            dkv = dk_acc[...]
            if mode == "dil":
                dkv = _rope_t(dkv, tk_ref[0], tk_ref[1], tk_ref[2])
            dk_ref[...] = dkv.astype(kv_dtype)
            dv_ref[...] = dv_acc[...].astype(kv_dtype)

    q_spec = pl.BlockSpec((rows, 128), lambda u, i: (i, qcol + u))
    row_spec = pl.BlockSpec((rows, 128), lambda u, i: (i, u))
    kv_out = pl.BlockSpec((lk, 128), lambda u, i: (0, u))
    whole = pl.BlockSpec((SEQ, 128), lambda u, i: (0, u))
    nat_spec = whole if mode == "dil" else row_spec
    in_specs = [q_spec,
                pl.BlockSpec((lk, 128), lambda u, i: (0, kcol + u)),
                pl.BlockSpec((lk, 128), lambda u, i: (0, vcol + u)),
                nat_spec, nat_spec]
    args = [q_arr, k_arr, v_arr, do, lse]
    out_specs = [row_spec, kv_out, kv_out]
    out_shape = [jax.ShapeDtypeStruct((SEQ, 512), BF16), jax.ShapeDtypeStruct((lk, 512), kv_dtype),
                 jax.ShapeDtypeStruct((lk, 512), kv_dtype)]
    if mode == "dil":
        in_specs += [whole, pl.BlockSpec((3, rows, 128), lambda u, i: (0, i, 0)),
                     pl.BlockSpec((3, SEQ, 128), lambda u, i: (0, 0, 0))]
        args += [dp, tabs, tabs]
    elif mode == "na":
        b_spec = pl.BlockSpec((2, NA_ROWS, GRID_W, NA_ROWS * GRID_W), lambda u, i: (u, 0, 0, 0))
        in_specs += [row_spec, b_spec]
        args += [o, bias]
        out_specs.append(b_spec)
        out_shape.append(jax.ShapeDtypeStruct((8, NA_ROWS, GRID_W, NA_ROWS * GRID_W), F32))
    else:
        in_specs.append(row_spec)
        args.append(o)
    return pl.pallas_call(
        body, name=name, grid=(cfg["units"], nq), in_specs=in_specs + dep_specs, out_specs=out_specs,
        out_shape=out_shape, scratch_shapes=[pltpu.VMEM((lk, 128), F32), pltpu.VMEM((lk, 128), F32)],
        compiler_params=_params(("parallel", "arbitrary")))(*args, *dep_args)


def _na_geometry():
    qc = _iota((GRID_W, 128), 0)
    lane = _iota((GRID_W, 128), 1)
    kc = lane & 63
    c_start = jnp.clip(qc - 8, 0, GRID_W - 16)
    valid = jnp.logical_and(kc >= c_start, kc < c_start + 16)
    return lane, valid


def _na_bias(rpb_rows, dep=None):
    dep_specs, dep_args = _dep_operand(dep)

    def body(r_ref, *rest):
        o_ref, t_ref = rest[-2:]
        lane, valid = _na_geometry()
        for dd in range(14):
            row_a = jnp.broadcast_to(r_ref[dd:dd + 1, :], (GRID_W, 128))
            row_b = jnp.broadcast_to(r_ref[dd + 1:dd + 2, :], (GRID_W, 128))
            both = jnp.where(lane < 64, row_a, pltpu.roll(row_b, 64, 1))
            t = pltpu.roll(both, 128 - 15, 1, stride=1, stride_axis=0)
            t_ref[dd] = jnp.where(valid, t, NEG)
        for off in range(NA_ROWS):
            for p in range(4):
                o_ref[off, :, p * 128:(p + 1) * 128] = t_ref[2 * p - off + 7]

    return pl.pallas_call(
        body, name="na_bias", grid=(8,),
        in_specs=[pl.BlockSpec((None, 16, 128), lambda h: (h, 0, 0))] + dep_specs,
        out_specs=pl.BlockSpec((None, NA_ROWS, GRID_W, NA_ROWS * GRID_W), lambda h: (h, 0, 0, 0)),
        out_shape=jax.ShapeDtypeStruct((8, NA_ROWS, GRID_W, NA_ROWS * GRID_W), F32),
        scratch_shapes=[pltpu.VMEM((14, GRID_W, 128), F32)],
        compiler_params=_params(("parallel",)))(rpb_rows, *dep_args)


def _na_bias_bwd(dbias, dep=None):
    dep_specs, dep_args = _dep_operand(dep)

    def body(d_ref, *rest):
        o_ref = rest[-1]
        lane, valid = _na_geometry()
        reverse = (_iota((GRID_W, GRID_W), 0) + _iota((GRID_W, GRID_W), 1) == GRID_W - 1).astype(F32)
        o_ref[...] = jnp.zeros((16, 128), F32)
        for dd in range(14):
            t = jnp.zeros((GRID_W, 128), F32)
            for off in range(NA_ROWS):
                for p in range(4):
                    if 2 * p - off + 7 == dd:
                        t = t + d_ref[off, :, p * 128:(p + 1) * 128]
            t = jnp.dot(reverse, jnp.where(valid, t, 0.0), precision=lax.Precision.HIGHEST,
                        preferred_element_type=F32)
            t = pltpu.roll(t, 128 - (GRID_W - 16), 1, stride=1, stride_axis=0)
            o_ref[dd:dd + 1, :] = jnp.sum(t, axis=0, keepdims=True)

    return pl.pallas_call(
        body, name="na_bias_bwd", grid=(8,),
        in_specs=[pl.BlockSpec((None, NA_ROWS, GRID_W, NA_ROWS * GRID_W), lambda h: (h, 0, 0, 0))] + dep_specs,
        out_specs=pl.BlockSpec((None, 16, 128), lambda h: (h, 0, 0)),
        out_shape=jax.ShapeDtypeStruct((8, 16, 128), F32),
        compiler_params=_params(("parallel",)))(dbias, *dep_args)


GATE_ROWS = 128


def _group_weights(l0, l1, l2):
    m = jnp.maximum(jnp.maximum(l0, l1), l2)
    e0, e1, e2 = jnp.exp(l0 - m), jnp.exp(l1 - m), jnp.exp(l2 - m)
    inv = 1.0 / (e0 + e1 + e2)
    return e0 * inv, e1 * inv, e2 * inv


def _gate_block(o_grp, l_grp, out_b, out_c, parts, x, target, merge_bias, branch_rows, out_rows, gain, head_sum):
    rows = GATE_ROWS
    r512 = pl.BlockSpec((rows, 512), lambda i: (i, 0))
    r1024 = pl.BlockSpec((rows, D_MODEL), lambda i: (i, 0))
    silu_cols = [pl.BlockSpec((rows, 512), functools.partial(lambda b, i: (i, b), 13 + b)) for b in range(3)]
    logit_cols = [pl.BlockSpec((rows, D_MODEL), functools.partial(lambda b, i: (i, b), 8 + b)) for b in range(3)]

    def body(o0, o1, o2, l0, l1, l2, ob, oc, ga, gb, gc, la, lb, lc, x_ref, t_ref, mb, wa, wb, wc, wo_ref, gn_ref,
             hs_ref, dout_ref, dla, dlb, dlc, dga, dgb, dgc, do0, do1, do2, dp0, dp1, dp2, dob, doc, err_ref, gg_ref,
             gmb, gwa, gwb, gwc, gwo, acc_a, acc_b, acc_c, acc_o):
        step = pl.program_id(0)
        whole = lambda w_ref: w_ref[...].reshape(D_MODEL, w_ref.shape[-1])
        ws = _group_weights(l0[...], l1[...], l2[...])
        out_a = ws[0] * o0[...] + ws[1] * o1[...] + ws[2] * o2[...]
        branches = ((out_a, ga, la, wa, acc_a, dla, dga), (ob[...], gb, lb, wb, acc_b, dlb, dgb),
                    (oc[...], gc, lc, wc, acc_c, dlc, dgc))

        @pl.when(step == 0)
        def _():
            for acc in (acc_a, acc_b, acc_c, acc_o):
                acc[...] = jnp.zeros(acc.shape, F32)
            err_ref[...] = jnp.zeros((1, D_MODEL), F32)
            gg_ref[...] = jnp.zeros((1, D_MODEL), F32)
            gmb[...] = jnp.zeros((3, D_MODEL), F32)

        y = jnp.zeros((rows, D_MODEL), F32)
        zs, gates, silus, dsilus, us = [], [], [], [], []
        for b, (ov, g_ref, l_ref, w_ref, _, _, _) in enumerate(branches):
            g = g_ref[...].astype(F32)
            sg = _sigmoid(g)
            silus.append(g * sg)
            dsilus.append(sg * (1.0 + g * (1.0 - sg)))
            us.append((ov * silus[b]).astype(BF16))
            zs.append(lax.dot_general(us[b], whole(w_ref), NT, preferred_element_type=F32))
            gates.append(_sigmoid(l_ref[...].astype(F32) + mb[b:b + 1, :]))
            y = y + gates[b] * zs[b]
        yb = y.astype(BF16)
        y2 = jnp.dot(yb, whole(wo_ref), preferred_element_type=F32)
        rstd = lax.rsqrt(jnp.mean(y2 * y2, axis=1, keepdims=True) + EPS)
        yn = y2 * rstd
        gv = gn_ref[...]
        err = x_ref[...] + yn * gv - t_ref[...]
        dout = err * (1.0 / D_MODEL)
        dout_ref[...] = dout
        dn = dout * gv
        dy2 = (rstd * (dn - yn * jnp.mean(dn * yn, axis=1, keepdims=True))).astype(BF16)
        acc_o[...] += lax.dot_general(yb, dy2, TN, preferred_element_type=F32)
        err_ref[...] += jnp.sum(err * err, axis=0, keepdims=True)
        gg_ref[...] += jnp.sum(dout * yn, axis=0, keepdims=True)
        dy = lax.dot_general(dy2, whole(wo_ref), NT, preferred_element_type=F32)
        dos = []
        for b, (ov, _, _, w_ref, acc, dl_ref, dg_ref) in enumerate(branches):
            dl = dy * zs[b] * gates[b] * (1.0 - gates[b])
            dl_ref[...] = dl.astype(BF16)
            gmb[b:b + 1, :] += jnp.sum(dl, axis=0, keepdims=True)
            dz = (dy * gates[b]).astype(BF16)
            acc[...] += lax.dot_general(dz, us[b], TN, preferred_element_type=F32)
            du = jnp.dot(dz, whole(w_ref), preferred_element_type=F32)
            dos.append(du * silus[b])
            dg_ref[...] = (du * ov * dsilus[b]).astype(BF16)
        dob[...] = dos[1].astype(BF16)
        doc[...] = dos[2].astype(BF16)
        row_term = jnp.dot(dos[0] * out_a, hs_ref[...], precision=lax.Precision.HIGHEST, preferred_element_type=F32)
        for wg, do_ref, dp_ref in zip(ws, (do0, do1, do2), (dp0, dp1, dp2)):
            do_ref[...] = wg * dos[0]
            dp_ref[...] = wg * row_term

        @pl.when(step == SEQ // rows - 1)
        def _():
            for acc, out in ((acc_a, gwa), (acc_b, gwb), (acc_c, gwc), (acc_o, gwo)):
                out[...] = acc[...].astype(BF16)

    full = lambda shape: pl.BlockSpec(shape, lambda i: (0,) * len(shape))
    vec = pl.BlockSpec((1, D_MODEL), lambda i: (0, 0))
    acc3 = pl.BlockSpec((3, D_MODEL), lambda i: (0, 0))
    shard = D_MODEL // N_DEV
    dev_rows = lambda width, k: pl.BlockSpec((N_DEV, shard, width), lambda i: (0, k, 0))
    in_specs = ([r512] * 8 + silu_cols + logit_cols + [r1024, r1024, full((3, D_MODEL))]
                + [dev_rows(512, k) for k in range(3)] + [dev_rows(D_MODEL, 1), vec, full((512, 512))])
    out_specs = ([r1024] + [r1024] * 3 + [r512] * 3 + [r512] * 6 + [r512] * 2 + [vec, vec, acc3]
                 + [full((D_MODEL, 512))] * 3 + [full((D_MODEL, D_MODEL))])
    bf, f32 = BF16, F32
    sds = jax.ShapeDtypeStruct
    out_shape = ([sds((SEQ, D_MODEL), f32)] + [sds((SEQ, D_MODEL), bf)] * 3 + [sds((SEQ, 512), bf)] * 3
                 + [sds((SEQ, 512), f32)] * 6 + [sds((SEQ, 512), bf)] * 2 + [sds((1, D_MODEL), f32)] * 2
                 + [sds((3, D_MODEL), f32)] + [sds((D_MODEL, 512), bf)] * 3 + [sds((D_MODEL, D_MODEL), bf)])
    res = pl.pallas_call(
        body, name="gate_block", grid=(SEQ // rows,), in_specs=in_specs, out_specs=out_specs, out_shape=out_shape,
        scratch_shapes=[pltpu.VMEM((D_MODEL, 512), F32)] * 3 + [pltpu.VMEM((D_MODEL, D_MODEL), F32)],
        compiler_params=_params(("arbitrary",)))(
            *o_grp, *l_grp, out_b, out_c, parts, parts, parts, parts, parts, parts, x, target, merge_bias,
            branch_rows, branch_rows, branch_rows, out_rows, gain, head_sum)
    return dict(dout=res[0], dlog=res[1:4], dg=res[4:7], do_grp=res[7:10], dp_grp=res[10:13], do_b=res[13],
                do_c=res[14], err_sq=res[15], g_post=res[16], g_mb=res[17], g_wt=res[18:21], g_w_out=res[21])


def _local_step(x, hst, parts, tabs, bias, mem, target, pre_norm, mem_norm, post_norm, wt_in, late_weights,
                reduce_start=None):
    o_grp, l_grp = [], []
    for g, d in enumerate(DILATIONS):
        o, l = _attn_fwd("dil_fwd_%d" % g, "dil", parts, parts, parts, 12 * g, 12 * g + 4, 12 * g + 8, d=d)
        o_grp.append(o)
        l_grp.append(l)
    out_b, lse_b = _attn_fwd("na_fwd", "na", parts, parts, parts, 36, 40, 44, bias=bias)
    merge_bias, w_kv, branch_rows, out_rows = late_weights(sum(a[:8, :128] for a in [out_b] + o_grp))
    memn = _rmsnorm_fwd("memnorm", mem, mem_norm, MEM_LEN)
    kv_m = _mm_simple("mem_kv", memn, w_kv, NN, BF16, MEM_LEN, 512, D_MODEL)
    out_c, lse_c = _attn_fwd("mem_fwd", "mem", parts, kv_m, kv_m, 48, 0, 4)

    rr = _iota((512, 512), 0) // HEAD_DIM
    cc = _iota((512, 512), 1) // HEAD_DIM
    head_sum = (rr == cc).astype(F32)
    gb = _gate_block(o_grp, l_grp, out_b, out_c, parts, x, target, merge_bias, branch_rows, out_rows, post_norm,
                     head_sum)
    dout, dlog, dg, g_wt, g_w_out = gb["dout"], gb["dlog"], gb["dg"], gb["g_wt"], gb["g_w_out"]
    do_grp, dp_grp, do_b, do_c, g_post, g_mb = (gb["do_grp"], gb["dp_grp"], gb["do_b"], gb["do_c"], gb["g_post"],
                                                gb["g_mb"])
    loss = 0.5 * jnp.sum(gb["err_sq"]) / D_MODEL

    dq_c, dk_m, dv_m = _attn_bwd("mem_bwd", "mem", parts, kv_m, kv_m, 48, 0, 4, do_c, lse_c, o=out_c)
    dkv = jnp.concatenate([dk_m, dv_m], axis=1).astype(BF16)
    g_w_kv = _mm_simple("mem_kv_dw", memn, dkv, TN, BF16, D_MODEL, 512, MEM_LEN)
    dmemn = _mm_simple("mem_kv_dx", dkv, w_kv, NT, F32, MEM_LEN, 512, D_MODEL)
    grads = dict(w_kv=g_w_kv, wt_a=g_wt[0], wt_b=g_wt[1], wt_c=g_wt[2], w_out=g_w_out, merge_bias=g_mb,
                 post_norm=g_post)
    dep = reduce_start("rest_sibling", grads) if reduce_start is not None else None

    dq_b, dk_b, dv_b, dbias = _attn_bwd("na_bwd", "na", parts, parts, parts, 36, 40, 44, do_b, lse_b, o=out_b,
                                        bias=bias, dep=dep)
    dqkv = []
    for g, d in enumerate(DILATIONS):
        dq, dk, dv = _attn_bwd("dil_bwd_%d" % g, "dil", parts, parts, parts, 12 * g, 12 * g + 4, 12 * g + 8,
                               do_grp[g], l_grp[g], dp=dp_grp[g], d=d, tabs=tabs[g])
        dqkv += [dq, dk, dv]
    if reduce_start is not None:
        dep = reduce_start("rest_chips", grads, sum(a[:8, :128] for a in (dqkv[0], dqkv[3], dqkv[6], dq_b)))
    dparts = dqkv + [dq_b, dk_b, dv_b, dq_c] + list(dg) + list(dlog)
    grads["wt_in"] = _in_proj_dw(dparts, hst, dep)
    dep = reduce_start("w_in", grads) if reduce_start is not None else None
    dh = _in_proj_dh(dparts, wt_in, dep)
    if reduce_start is not None:
        dep = reduce_start("w_in_second", grads, dh)
    grad_x, grads["pre_norm"] = _prenorm_bwd(x, pre_norm, dh, dout)
    g_rpb_t = _na_bias_bwd(dbias, dep)
    grads["na_rpb"] = g_rpb_t[:, :15, :31] + jnp.pad(g_rpb_t[:, :14, 64:95], ((0, 0), (1, 0), (0, 0)))
    grads["mem_norm"] = _memnorm_bwd(mem, dmemn, dep)
    return loss, grad_x, grads


ANY = pl.BlockSpec(memory_space=pl.ANY)


def _place():
    return lax.axis_index("x"), lax.axis_index("y"), lax.axis_index("c")


HBM = pl.BlockSpec(memory_space=pltpu.HBM)
SEM = pl.BlockSpec(memory_space=pltpu.SEMAPHORE)
DATAFLOW = pltpu.SideEffectType.DATAFLOW_SIDE_EFFECTING


def _split_copies(kind, srcs, lands, send_sems, recv_sems):
    nt = len(srcs)
    x, y, c = _place()
    copies = []
    if kind == "sibling":
        for q in range(4):
            for t in range(nt):
                k = q * nt + t
                copies.append(pltpu.make_async_remote_copy(
                    src_ref=srcs[t].at[2 * q + 1 - c], dst_ref=lands[t].at[q], send_sem=send_sems.at[k],
                    recv_sem=recv_sems.at[k], device_id=(x, y, 1 - c), device_id_type=MESH_ID))
    elif kind in ("rs_a", "rs_b"):
        half = lands[0].shape[1]
        xn, yn = (1 - x, y, c), (x, 1 - y, c)
        q_xn, q_yn, q_dg = 2 * (1 - x) + y, 2 * x + 1 - y, 2 * (1 - x) + 1 - y
        if kind == "rs_a":
            plan = [(srcs[0].at[q_yn].at[pl.ds(0, half)], 0, yn), (srcs[0].at[q_dg].at[pl.ds(0, half)], 1, yn),
                    (srcs[0].at[q_xn].at[pl.ds(half, half)], 2, xn), (srcs[0].at[q_dg].at[pl.ds(half, half)], 3, xn)]
        else:
            plan = [(srcs[0].at[0], 0, xn), (srcs[0].at[1], 1, yn)]
        for k, (src, slot, to) in enumerate(plan):
            copies.append(pltpu.make_async_remote_copy(
                src_ref=src, dst_ref=lands[0].at[slot], send_sem=send_sems.at[k], recv_sem=recv_sems.at[k],
                device_id=to, device_id_type=MESH_ID))
    elif kind == "gather":
        me = 4 * x + 2 * y + c
        for mask in range(1, 8):
            fx, fy, fc = (mask >> 2) & 1, (mask >> 1) & 1, mask & 1
            to = (1 - x if fx else x, 1 - y if fy else y, 1 - c if fc else c)
            for t in range(nt):
                k = (mask - 1) * nt + t
                copies.append(pltpu.make_async_remote_copy(
                    src_ref=srcs[t], dst_ref=lands[t].at[me], send_sem=send_sems.at[k], recv_sem=recv_sems.at[k],
                    device_id=to, device_id_type=MESH_ID))
        for t in range(nt):
            copies.append(pltpu.make_async_copy(srcs[t], lands[t].at[me], recv_sems.at[7 * nt + t]))
    else:
        for s, (tx, ty) in enumerate([(1 - x, y), (x, 1 - y), (1 - x, 1 - y)]):
            for t in range(nt):
                k = s * nt + t
                copies.append(pltpu.make_async_remote_copy(
                    src_ref=srcs[t].at[2 * tx + ty], dst_ref=lands[t].at[s], send_sem=send_sems.at[k],
                    recv_sem=recv_sems.at[k], device_id=(tx, ty, c), device_id_type=MESH_ID))
    return copies


def _split_count(kind, nt):
    return {"gather": 8, "chips": 3, "sibling": 4, "rs_a": 4, "rs_b": 2}[kind] * nt


def _exchange_start(name, kind, srcs, land_shapes, after=None):
    nt = len(srcs)
    n = _split_count(kind, nt)
    dep_specs, dep_args = _dep_operand(after)
    nd = len(dep_args)

    def body(*refs):
        src_refs, land_refs = refs[:nt], refs[nt:2 * nt]
        send_sems, recv_sems = refs[2 * nt + nd], refs[2 * nt + nd + 1]
        token = refs[-1]
        for cp in _split_copies(kind, src_refs, land_refs, send_sems, recv_sems):
            cp.start()
        token[...] = jnp.zeros_like(token)

    lands = [pltpu.with_memory_space_constraint(lax.empty(s.shape, s.dtype), pltpu.HBM) for s in land_shapes]
    res = pl.pallas_call(
        body, name=name,
        out_shape=(pltpu.SemaphoreType.DMA((n,)), pltpu.SemaphoreType.DMA((n,)),
                   *[pltpu.HBM(s.shape, s.dtype) for s in srcs], *[pltpu.HBM(s.shape, s.dtype) for s in land_shapes],
                   jax.ShapeDtypeStruct((8, 128), F32)),
        in_specs=[HBM] * (2 * nt) + dep_specs,
        out_specs=(SEM, SEM, *([HBM] * (2 * nt)), pl.BlockSpec(memory_space=pltpu.VMEM)),
        input_output_aliases={i: 2 + i for i in range(2 * nt)},
        compiler_params=pltpu.CompilerParams(has_side_effects=DATAFLOW))(
            *[pltpu.with_memory_space_constraint(s, pltpu.HBM) for s in srcs], *lands, *dep_args)
    return res[0], res[1], list(res[2:2 + nt]), list(res[2 + nt:2 + 2 * nt]), res[-1]


def _exchange_wait(name, kind, send_sems, recv_sems, srcs, lands, after):
    nt = len(srcs)

    def body(*refs):
        src_refs, land_refs = refs[:nt], refs[nt:2 * nt]
        s_sems, r_sems = refs[2 * nt], refs[2 * nt + 1]
        for cp in _split_copies(kind, src_refs, land_refs, s_sems, r_sems):
            if cp.is_remote:
                cp.wait_send()
                cp.wait_recv()
            else:
                cp.wait()

    res = pl.pallas_call(
        body, name=name,
        out_shape=tuple(pltpu.HBM(s.shape, s.dtype) for s in list(srcs) + list(lands)),
        in_specs=[HBM] * (2 * nt) + [SEM, SEM, pl.BlockSpec(memory_space=pl.ANY)],
        out_specs=tuple([HBM] * (2 * nt)),
        input_output_aliases={i: i for i in range(2 * nt)},
        compiler_params=pltpu.CompilerParams(has_side_effects=DATAFLOW))(
            *srcs, *lands, send_sems, recv_sems, after)
    return list(res[:nt]), list(res[nt:])


AG_GROUPS = ((0, 3), (3, 4), (7, 2))


def _ag_phase(name, own, land, sems, waits, starts, after=None):
    r = own.shape[0]
    half = r // 2
    ns = len(sems)
    dep_specs, dep_args = _dep_operand(after)
    nd = len(dep_args)
    new_group = None
    if starts:
        (new_group,) = [g for g, (first, n) in enumerate(AG_GROUPS) if first == starts[0]]
        assert list(starts) == list(range(AG_GROUPS[new_group][0], sum(AG_GROUPS[new_group])))

    def body(*refs):
        own_ref, land_ref = refs[0], refs[1]
        sem_refs = list(refs[2:2 + 2 * ns])
        outs = refs[2 + 2 * ns + nd:]
        if starts:
            sem_refs += [outs[0], outs[1]]
        x, y, c = _place()
        me, sib = (x, y, c), (x, y, 1 - c)
        xn, yn, dg = (1 - x, y, c), (x, 1 - y, c), (1 - x, 1 - y, c)

        def other(dev):
            return (dev[0], dev[1], 1 - dev[2])

        def rows(dev, part):
            blk = land_ref.at[4 * dev[0] + 2 * dev[1] + dev[2]]
            return blk if part is None else blk.at[pl.ds(part * half, half)]

        def sem_of(k):
            (g,) = [g for g, (first, n) in enumerate(AG_GROUPS) if first <= k < first + n]
            return sem_refs[2 * g].at[k - AG_GROUPS[g][0]], sem_refs[2 * g + 1].at[k - AG_GROUPS[g][0]]

        sent = {0: (me, None, sib), 1: (me, None, xn), 2: (me, None, yn), 3: (xn, 0, yn), 4: (yn, 1, xn),
                5: (xn, None, sib), 6: (yn, None, sib), 7: (dg, 0, sib), 8: (dg, 1, sib)}
        landed = {0: (sib, None), 1: (xn, None), 2: (yn, None), 3: (dg, 0), 4: (dg, 1), 5: (other(xn), None),
                  6: (other(yn), None), 7: (other(dg), 0), 8: (other(dg), 1)}

        def copy(k, receiving):
            send_sem, recv_sem = sem_of(k)
            dev, part, to = (*landed[k], me) if receiving else sent[k]
            src = own_ref if (dev is me and not receiving) else rows(dev, part)
            return pltpu.make_async_remote_copy(src_ref=src, dst_ref=rows(dev, part), send_sem=send_sem,
                                                recv_sem=recv_sem, device_id=to, device_id_type=MESH_ID)

        for kind, k in waits:
            if kind == "recv":
                copy(k, True).wait_recv()
            else:
                copy(k, False).wait_send()
        for k in starts:
            copy(k, False).start()
        if starts:
            outs[-1][...] = jnp.zeros_like(outs[-1])

    n_new = AG_GROUPS[new_group][1] if starts else 0
    sem_out = (pltpu.SemaphoreType.DMA((n_new,)), pltpu.SemaphoreType.DMA((n_new,))) if starts else ()
    token_out = (jax.ShapeDtypeStruct((8, 128), F32),) if starts else ()
    res = pl.pallas_call(
        body, name=name,
        out_shape=(*sem_out, pltpu.HBM(own.shape, own.dtype), pltpu.HBM(land.shape, land.dtype), *token_out),
        in_specs=[HBM, HBM] + [SEM] * (2 * ns) + dep_specs,
        out_specs=(*([SEM] * len(sem_out)), HBM, HBM, *([pl.BlockSpec(memory_space=pltpu.VMEM)] * len(token_out))),
        input_output_aliases={0: len(sem_out), 1: len(sem_out) + 1},
        compiler_params=pltpu.CompilerParams(has_side_effects=DATAFLOW))(
            own, land, *[a for pair in sems for a in pair], *dep_args)
    if starts:
        return (res[0], res[1]), res[2], res[3], res[4]
    return None, res[0], res[1], None


def _add_sibling(name, term, recv, rows):
    _, r, w = term.shape
    cidx = lax.axis_index("c").astype(jnp.int32).reshape(1)
    like_term = recv.shape[0] == N_DEV

    def body(c_ref, a_ref, b_ref, o_ref):
        o_ref[...] = (a_ref[...].astype(F32) + b_ref[...].astype(F32)).astype(o_ref.dtype)

    grid_spec = pltpu.PrefetchScalarGridSpec(
        num_scalar_prefetch=1, grid=(4, r // rows),
        in_specs=[pl.BlockSpec((None, rows, w), lambda q, i, c_ref: (2 * q + c_ref[0], i, 0)),
                  pl.BlockSpec((None, rows, w), lambda q, i, c_ref: (2 * q + c_ref[0] if like_term else q, i, 0))],
        out_specs=pl.BlockSpec((None, rows, w), lambda q, i, c_ref: (q, i, 0)))
    return pl.pallas_call(
        body, name=name, grid_spec=grid_spec, out_shape=jax.ShapeDtypeStruct((4, r, w), term.dtype),
        compiler_params=_params(("parallel", "parallel")))(cidx, term, recv)


def _add_sibling_small(name, terms, recvs):
    nt = len(terms)

    def body(*refs):
        c = lax.axis_index("c")
        for t_ref, r_ref, o_ref in zip(refs[:nt], refs[nt:2 * nt], refs[2 * nt:]):
            for q in range(4):
                o_ref[q] = (t_ref[2 * q + c].astype(F32) + r_ref[q].astype(F32)).astype(o_ref.dtype)

    return pl.pallas_call(
        body, name=name, out_shape=[jax.ShapeDtypeStruct((4,) + t.shape[1:], t.dtype) for t in terms],
        compiler_params=_params())(*terms, *recvs)


def _reduce_scatter_start(tag, terms, recv1):
    sums = _add_sibling_small("add_sibling_" + tag, terms, recv1)
    lands = [jax.ShapeDtypeStruct((3,) + s.shape[1:], s.dtype) for s in sums]
    send_sems, recv_sems, sums, lands, token = _exchange_start("exchange_chips_start_" + tag, "chips", sums, lands)
    return (tag, send_sems, recv_sems, sums, lands), token


def _reduce_scatter_wait(state, after):
    tag, send_sems, recv_sems, sums, lands = state
    return _exchange_wait("exchange_chips_wait_" + tag, "chips", send_sems, recv_sems, sums, lands, after)


def _adam_math(w, g, m, v):
    nm = ADAM_B1 * m + (1.0 - ADAM_B1) * g
    nv = ADAM_B2 * v + (1.0 - ADAM_B2) * (g * g)
    c1 = 1.0 - ADAM_B1 ** ADAM_STEP
    c2 = 1.0 - ADAM_B2 ** ADAM_STEP
    return -ADAM_LR * ((nm / c1) / (jnp.sqrt(nv / c2) + ADAM_EPS) + ADAM_WD * w), nm, nv


def _presum_halves(sums, landed):
    _, r, w = sums.shape
    rows = r // 2
    x, y = lax.axis_index("x"), lax.axis_index("y")
    dest = jnp.stack([2 * (1 - x) + y, 2 * x + 1 - y]).astype(jnp.int32)

    def body(q_ref, a_ref, b_ref, o_ref):
        o_ref[...] = (a_ref[...].astype(F32) + b_ref[...].astype(F32)).astype(o_ref.dtype)

    grid_spec = pltpu.PrefetchScalarGridSpec(
        num_scalar_prefetch=1, grid=(2,),
        in_specs=[pl.BlockSpec((None, rows, w), lambda h, q_ref: (q_ref[h], h, 0)),
                  pl.BlockSpec((None, rows, w), lambda h, q_ref: (1 + 2 * h, 0, 0))],
        out_specs=pl.BlockSpec((None, rows, w), lambda h, q_ref: (h, 0, 0)))
    return pl.pallas_call(
        body, name="presum_halves", grid_spec=grid_spec, out_shape=jax.ShapeDtypeStruct((2, r // 2, w), sums.dtype),
        compiler_params=_params(("parallel",)))(dest, sums, landed)


def _adamw_halves(name, sums, landed_a, landed_b, w, m, v, rows):
    r, c = w.shape
    half = c // 2
    qidx = (2 * lax.axis_index("x") + lax.axis_index("y")).astype(jnp.int32).reshape(1)

    def body(q_ref, s_ref, a_ref, b_ref, w_ref, m_ref, v_ref, g_ref, d_ref, nm_ref, nv_ref):
        first = (s_ref[:half, :].astype(F32) + a_ref[0].astype(F32)) + b_ref[0].astype(F32)
        second = (s_ref[half:, :].astype(F32) + a_ref[2].astype(F32)) + b_ref[1].astype(F32)
        g = jnp.concatenate([first, second], axis=0).T
        g_ref[...] = g
        d_ref[...], nm_ref[...], nv_ref[...] = _adam_math(w_ref[...], g, m_ref[...], v_ref[...])

    row = pl.BlockSpec((rows, c), lambda i, q_ref: (i, 0))
    grid_spec = pltpu.PrefetchScalarGridSpec(
        num_scalar_prefetch=1, grid=(r // rows,),
        in_specs=[pl.BlockSpec((None, c, rows), lambda i, q_ref: (q_ref[0], 0, i)),
                  pl.BlockSpec((4, half, rows), lambda i, q_ref: (0, 0, i)),
                  pl.BlockSpec((2, half, rows), lambda i, q_ref: (0, 0, i)), row, row, row],
        out_specs=[row] * 4)
    return pl.pallas_call(
        body, name=name, grid_spec=grid_spec, out_shape=[jax.ShapeDtypeStruct((r, c), F32)] * 4,
        compiler_params=_params(("parallel",)))(qidx, sums, landed_a, landed_b, w, m, v)


def _adamw_chips_small(name, items):
    n = len(items)

    def body(*refs):
        q = 2 * lax.axis_index("x") + lax.axis_index("y")
        ins, outs = refs[:5 * n], refs[5 * n:]
        for i, (_, _, w, _, _, transposed) in enumerate(items):
            s_ref, r_ref, w_ref, m_ref, v_ref = ins[5 * i:5 * i + 5]
            g_ref, d_ref, nm_ref, nv_ref = outs[4 * i:4 * i + 4]
            g = (s_ref[q].astype(F32) + r_ref[0].astype(F32)) + (r_ref[1].astype(F32) + r_ref[2].astype(F32))
            g = g.T if transposed else g[:w.shape[0]]
            g_ref[...] = g
            d_ref[...], nm_ref[...], nv_ref[...] = _adam_math(w_ref[...], g, m_ref[...], v_ref[...])

    res = pl.pallas_call(
        body, name=name, out_shape=[jax.ShapeDtypeStruct(it[2].shape, F32) for it in items for _ in range(4)],
        compiler_params=_params())(*[a for it in items for a in it[:5]])
    return [res[4 * i:4 * i + 4] for i in range(n)]


def _adamw_replicated(gathered, items):
    n = len(items)

    def body(g_ref, *refs):
        ins, t_ref, outs = refs[:3 * n], refs[3 * n], refs[3 * n + 1:]
        acc = g_ref[0]
        for j in range(1, N_DEV):
            acc = acc + g_ref[j]
        t_ref[...] = acc
        for i, (first, w, _, _) in enumerate(items):
            w_ref, m_ref, v_ref = ins[3 * i:3 * i + 3]
            g = t_ref[first:first + w.shape[0], :]
            outs[3 * i][...], outs[3 * i + 1][...], outs[3 * i + 2][...] = _adam_math(w_ref[...], g, m_ref[...], v_ref[...])

    res = pl.pallas_call(
        body, name="adamw_replicated",
        out_shape=[jax.ShapeDtypeStruct(gathered.shape[1:], F32)]
        + [jax.ShapeDtypeStruct(it[1].shape, F32) for it in items for _ in range(3)],
        compiler_params=_params())(gathered, *[a for it in items for a in it[1:]])
    return res[0], [res[1 + 3 * i:4 + 3 * i] for i in range(n)]


def _wide_rows(a):
    rows = -(-a.size // D_MODEL)
    return jnp.pad(a.reshape(-1), (0, rows * D_MODEL - a.size)).reshape(rows, D_MODEL)


def kernel(x, mem, pre_norm, w_in, merge_bias, na_rpb, mem_norm, w_mem_kv, w_branch_a, w_branch_b, w_branch_c, w_out, post_norm, loss_target, m_pre_norm, m_w_in, m_merge_bias, m_na_rpb, m_mem_norm, m_w_mem_kv, m_w_branch_a, m_w_branch_b, m_w_branch_c, m_w_out, m_post_norm, v_pre_norm, v_w_in, v_merge_bias, v_na_rpb, v_mem_norm, v_w_mem_kv, v_w_branch_a, v_w_branch_b, v_w_branch_c, v_w_out, v_post_norm):
    wt_in_s = w_in[0].T.astype(BF16)
    rows_s = jnp.concatenate([w_mem_kv[0], w_out[0]], axis=0).astype(BF16)
    cols_s = jnp.concatenate([w_branch_a[0].T, w_branch_b[0].T, w_branch_c[0].T], axis=0).astype(BF16)
    mb_s = jnp.pad(merge_bias[0], ((0, 5), (0, 0)))
    me = 4 * lax.axis_index("x") + 2 * lax.axis_index("y") + lax.axis_index("c")

    chip = 2 * lax.axis_index("x") + lax.axis_index("y")

    def first_block(q):
        return jnp.where(q == 0, 0, jnp.where(q == 1, 6, jnp.where(q == 2, 11, 17)))

    five = jnp.arange(5, dtype=jnp.int32)
    near, far = jnp.where(chip < 2, 5, 16), jnp.where(chip < 2, 16, 5)
    order1 = (first_block(chip) + five).astype(jnp.int32)
    order2 = jnp.concatenate([first_block(chip ^ 1) + five, near[None], first_block(chip ^ 2) + five]).astype(jnp.int32)
    order3 = jnp.concatenate([first_block(chip ^ 3) + five, far[None]]).astype(jnp.int32)
    tabs = _rope_tables()

    def weights_of(land):
        return land.reshape(N_IN, D_MODEL)

    land = pltpu.with_memory_space_constraint(lax.empty((N_DEV,) + wt_in_s.shape, BF16), pltpu.HBM)
    own = pltpu.with_memory_space_constraint(wt_in_s, pltpu.HBM)
    sem_a, own, land, token = _ag_phase("ag_start", own, land, [], [], [0, 1, 2])
    hs, hst = _prenorm_fold(x[0], pre_norm, token)
    _, own, land, _ = _ag_phase("ag_wait0", own, land, [sem_a], [("recv", 0)], [], hs)
    land = lax.dynamic_update_slice(land, own[None], (me, 0, 0))
    parts = _in_proj("in_proj_1", hs, weights_of(land), tabs, order1)
    bias = _na_bias(jnp.pad(na_rpb[0], ((0, 0), (0, 1), (0, 128 - 31))), parts)
    sem_b, own, land, _ = _ag_phase("ag_mid1", own, land, [sem_a], [("recv", 1), ("recv", 2)], [3, 4, 5, 6], bias)
    _, own, land, _ = _ag_phase("ag_wait1", own, land, [sem_a, sem_b], [("recv", 5), ("recv", 6)], [])
    parts = _in_proj("in_proj_2", hs, weights_of(land), tabs, order2, parts)
    sem_c, own, land, _ = _ag_phase("ag_mid2", own, land, [sem_a, sem_b], [("recv", 3), ("recv", 4)], [7, 8], parts)
    _, own, land, _ = _ag_phase("ag_end", own, land, [sem_a, sem_b, sem_c],
                                [("recv", 7), ("recv", 8)] + [("send", k) for k in range(9)], [])
    wt_in = weights_of(land)

    late_own = [rows_s, cols_s, mb_s]
    late_lands = [jax.ShapeDtypeStruct((N_DEV,) + s.shape, s.dtype) for s in late_own]
    l_send, l_recv, late_own, late_lands, late_token = _exchange_start("gather_late_start", "gather", late_own,
                                                                       late_lands, after=wt_in)
    parts = _in_proj("in_proj_3", hs, wt_in, tabs, order3, parts, late_token)

    def late_weights(after):
        _, lands = _exchange_wait("gather_late_wait", "gather", l_send, l_recv, late_own, late_lands, after)
        g_rows, g_cols, g_mb = lands
        return (g_mb[:, :3].transpose(1, 0, 2).reshape(3, D_MODEL), g_rows[:, :128].reshape(D_MODEL, D_MODEL),
                g_cols, g_rows)

    rest_state, rest_sibling, w_in_a, w_in_b = [], [], [], []

    def reduce_start(phase, grads, after=None):
        if phase == "rest_sibling":
            gmb_t = jnp.pad(grads["merge_bias"].reshape(3, N_DEV, 128).transpose(1, 0, 2), ((0, 0), (0, 5), (0, 0)))
            terms = [grads["w_kv"].reshape(N_DEV, 128, D_MODEL), grads["w_out"].reshape(N_DEV, 128, D_MODEL),
                     grads["wt_a"].reshape(N_DEV, 128, 512), grads["wt_b"].reshape(N_DEV, 128, 512),
                     grads["wt_c"].reshape(N_DEV, 128, 512), gmb_t]
            lands = [jax.ShapeDtypeStruct((4,) + t.shape[1:], t.dtype) for t in terms]
            started = _exchange_start("exchange_sibling_start_rest", "sibling", terms, lands)
            rest_sibling.extend(started[:4])
            return started[4]
        if phase == "rest_chips":
            s_send, s_recv, terms, lands = rest_sibling
            terms, recv1 = _exchange_wait("exchange_sibling_wait_rest", "sibling", s_send, s_recv, terms, lands, after)
            state, token = _reduce_scatter_start("rest", terms, recv1)
            rest_state.append(state)
            return token
        if phase == "w_in":
            own, sibling = [a.reshape(N_DEV, SHARD_IN, D_MODEL) for a in grads["wt_in"]]
            sums = _add_sibling("add_sibling_w_in", own, sibling, SHARD_IN)
            lands = [jax.ShapeDtypeStruct((4, SHARD_IN // 2, D_MODEL), BF16)]
            w_in_a.extend(_exchange_start("rs_a_start", "rs_a", [sums], lands))
            return w_in_a[4]
        (sums,), (landed_a,) = _exchange_wait("rs_a_wait", "rs_a", w_in_a[0], w_in_a[1], w_in_a[2], w_in_a[3], after)
        lands = [jax.ShapeDtypeStruct((2, SHARD_IN // 2, D_MODEL), BF16)]
        w_in_b.extend(_exchange_start("rs_b_start", "rs_b", [_presum_halves(sums, landed_a)], lands))
        w_in_b.extend([sums, landed_a])
        return w_in_b[4]

    loss_term, grad_x, grads = _local_step(
        x[0], hst, parts, tabs, bias, mem[0], loss_target[0], pre_norm, mem_norm, post_norm, wt_in, late_weights,
        reduce_start=reduce_start)

    replicated = ("pre_norm", "mem_norm", "post_norm", "na_rpb")
    pieces = [_wide_rows(grads[n]) for n in replicated] + [_wide_rows(loss_term)]
    first_rows = [sum(p.shape[0] for p in pieces[:i]) for i in range(len(pieces))]
    small = jnp.concatenate(pieces, axis=0)
    s_send, s_recv, s_own, s_land, s_token = _exchange_start(
        "gather_small_start", "gather", [small], [jax.ShapeDtypeStruct((N_DEV,) + small.shape, F32)])
    grad = {}
    weights = {
        "pre_norm": (pre_norm, m_pre_norm, v_pre_norm), "w_in": (w_in, m_w_in, v_w_in),
        "merge_bias": (merge_bias, m_merge_bias, v_merge_bias), "na_rpb": (na_rpb, m_na_rpb, v_na_rpb),
        "mem_norm": (mem_norm, m_mem_norm, v_mem_norm), "w_mem_kv": (w_mem_kv, m_w_mem_kv, v_w_mem_kv),
        "w_branch_a": (w_branch_a, m_w_branch_a, v_w_branch_a), "w_branch_b": (w_branch_b, m_w_branch_b, v_w_branch_b),
        "w_branch_c": (w_branch_c, m_w_branch_c, v_w_branch_c), "w_out": (w_out, m_w_out, v_w_out),
        "post_norm": (post_norm, m_post_norm, v_post_norm)}
    order = ["pre_norm", "w_in", "merge_bias", "na_rpb", "mem_norm", "w_mem_kv", "w_branch_a", "w_branch_b",
             "w_branch_c", "w_out", "post_norm"]
    delta, new_m, new_v = {}, {}, {}

    sums, recv2 = _reduce_scatter_wait(rest_state[0], s_token)
    rest = (("w_mem_kv", False), ("w_out", False), ("w_branch_a", True), ("w_branch_b", True), ("w_branch_c", True),
            ("merge_bias", False))
    items = [(sums[i], recv2[i], *[a[0] for a in weights[n]], transposed) for i, (n, transposed) in enumerate(rest)]
    for (n, _), (g, dl, nm, nv) in zip(rest, _adamw_chips_small("adamw_rest", items)):
        grad[n], delta[n], new_m[n], new_v[n] = g[None], dl[None], nm[None], nv[None]
    s_own, s_land = _exchange_wait("gather_small_wait", "gather", s_send, s_recv, s_own, s_land, delta["w_out"])
    items = [(first, *[_wide_rows(a) for a in weights[n]]) for n, first in zip(replicated, first_rows)]
    total, updates = _adamw_replicated(s_land[0], items)
    loss = total[first_rows[-1], 0]
    for n, first, it, (dl, nm, nv) in zip(replicated, first_rows, items, updates):
        w = weights[n][0]
        grad[n], delta[n], new_m[n], new_v[n] = [
            a.reshape(-1)[:w.size].reshape(w.shape) for a in (total[first:first + it[1].shape[0]], dl, nm, nv)]
    _, (landed_b,) = _exchange_wait("rs_b_wait", "rs_b", w_in_b[0], w_in_b[1], w_in_b[2], w_in_b[3], updates[-1][0])
    g, dl, nm, nv = _adamw_halves("adamw_w_in", w_in_b[5], w_in_b[6], landed_b, w_in[0], m_w_in[0], v_w_in[0], 256)
    grad["w_in"], delta["w_in"], new_m["w_in"], new_v["w_in"] = g[None], dl[None], nm[None], nv[None]

    return (loss, grad_x[None], *[grad[n] for n in order], *[delta[n] for n in order],
            *[new_m[n] for n in order], *[new_v[n] for n in order])
```

```python
import functools

import numpy as np
import jax
import jax.numpy as jnp
from jax import lax
from jax.experimental import pallas as pl
from jax.experimental.pallas import tpu as pltpu

F32 = jnp.float32
BF16 = jnp.bfloat16

SEQ = 2048
D_MODEL = 1024
N_IN = 11264
N_DEV = 8
SHARD_IN = N_IN // N_DEV
HEAD_DIM = 64
GRID_W = 64
NA_ROWS = 8
MEM_LEN = 256
DILATIONS = (1, 4, 16)
REACH = 64
ROPE_THETA = 500000.0
ROPE_DIM = 16
EPS = 1e-6
NEG = -1e30
ADAM_LR = 0.001
ADAM_B1 = 0.9
ADAM_B2 = 0.999
ADAM_EPS = 1e-08
ADAM_WD = 0.01
ADAM_STEP = 10

VMEM_LIMIT_BYTES = 56 * 1024 * 1024
MESH_ID = pl.DeviceIdType.MESH

NN = (((1,), (0,)), ((), ()))
NT = (((1,), (1,)), ((), ()))
TN = (((0,), (0,)), ((), ()))


def _params(sem=None):
    return pltpu.CompilerParams(dimension_semantics=sem, vmem_limit_bytes=VMEM_LIMIT_BYTES)


def _iota(shape, dim):
    return lax.broadcasted_iota(jnp.int32, shape, dim)


def _sigmoid(x):
    return 1.0 / (1.0 + jnp.exp(-x))


def _rope_tables():
    half = ROPE_DIM // 2
    inv = (ROPE_THETA ** (-np.arange(half, dtype=np.float64) * 2.0 / ROPE_DIM)).astype(np.float32)
    pos = np.arange(SEQ, dtype=np.float32)
    ang = pos[:, None] * inv[None, :]
    cos, sin = np.cos(ang), np.sin(ang)
    zeros = np.zeros_like(cos)
    rest = HEAD_DIM - ROPE_DIM
    c64 = np.concatenate([cos, cos, np.ones((SEQ, rest), np.float32)], axis=1)
    s1 = np.concatenate([zeros, sin, np.zeros((SEQ, rest), np.float32)], axis=1)
    s2 = np.concatenate([-sin, zeros, np.zeros((SEQ, rest), np.float32)], axis=1)

    def fold(t, d):
        return t.reshape(SEQ // d, d, t.shape[1]).transpose(1, 0, 2).reshape(SEQ, t.shape[1])

    tabs = [np.stack([np.tile(fold(t, d), (1, 2)) for t in (c64, s1, s2)], axis=0) for d in DILATIONS]
    return jnp.asarray(np.stack(tabs, axis=0), dtype=F32)


def _rope(a, c, s1, s2):
    return a * c + pltpu.roll(a, 8, 1) * s1 + pltpu.roll(a, 120, 1) * s2


def _rope_t(a, c, s1, s2):
    return a * c + pltpu.roll(a * s1, 120, 1) + pltpu.roll(a * s2, 8, 1)


def _perm_of_block(j):
    return jnp.where(j < 3, 0, jnp.where(j < 6, 1, jnp.where(j < 9, 2, 0)))


def _mm(name, a, b, out_shape, out_dtype, grid, a_spec, b_spec, o_spec, acc_shape, dims, k_axis, nk):
    def body(a_ref, b_ref, o_ref, acc_ref):
        k = pl.program_id(k_axis)

        @pl.when(k == 0)
        def _():
            acc_ref[...] = jnp.zeros(acc_shape, F32)

        acc_ref[...] += lax.dot_general(a_ref[...], b_ref[...], dims, preferred_element_type=F32)

        @pl.when(k == nk - 1)
        def _():
            o_ref[...] = acc_ref[...].astype(out_dtype)

    sem = tuple("arbitrary" if ax == k_axis else "parallel" for ax in range(len(grid)))
    return pl.pallas_call(
        body, name=name, grid=grid, in_specs=[a_spec, b_spec], out_specs=o_spec,
        out_shape=jax.ShapeDtypeStruct(out_shape, out_dtype),
        scratch_shapes=[pltpu.VMEM(acc_shape, F32)], compiler_params=_params(sem))(a, b)


def _mm_simple(name, a, b, dims, out_dtype, tm, tn, tk):
    if dims is NN:
        m, kk = a.shape
        n = b.shape[1]
        a_spec = pl.BlockSpec((tm, tk), lambda i, j, k: (i, k))
        b_spec = pl.BlockSpec((tk, tn), lambda i, j, k: (k, j))
    elif dims is NT:
        m, kk = a.shape
        n = b.shape[0]
        a_spec = pl.BlockSpec((tm, tk), lambda i, j, k: (i, k))
        b_spec = pl.BlockSpec((tn, tk), lambda i, j, k: (j, k))
    else:
        kk, m = a.shape
        n = b.shape[1]
        a_spec = pl.BlockSpec((tk, tm), lambda i, j, k: (k, i))
        b_spec = pl.BlockSpec((tk, tn), lambda i, j, k: (k, j))
    grid = (m // tm, n // tn, kk // tk)
    o_spec = pl.BlockSpec((tm, tn), lambda i, j, k: (i, j))
    return _mm(name, a, b, (m, n), out_dtype, grid, a_spec, b_spec, o_spec, (tm, tn), dims, 2, kk // tk)


def _rmsnorm_fwd(name, x, gain, rows):
    n, d = x.shape

    def body(x_ref, g_ref, o_ref):
        xv = x_ref[...]
        rstd = lax.rsqrt(jnp.mean(xv * xv, axis=1, keepdims=True) + EPS)
        o_ref[...] = (xv * rstd * g_ref[...]).astype(BF16)

    return pl.pallas_call(
        body, name=name, grid=(n // rows,),
        in_specs=[pl.BlockSpec((rows, d), lambda i: (i, 0)), pl.BlockSpec((1, d), lambda i: (0, 0))],
        out_specs=pl.BlockSpec((rows, d), lambda i: (i, 0)),
        out_shape=jax.ShapeDtypeStruct((n, d), BF16), compiler_params=_params(("parallel",)))(x, gain)


def _folded_rows(first, rows, d):
    if d == 1:
        return pl.ds(pl.multiple_of(first, rows), rows)
    mlen = SEQ // d
    return pl.ds((first % mlen) * d + first // mlen, rows, stride=d)


def _prenorm_fold(x, gain, dep=None):
    rows = 128
    nchunk = D_MODEL // 128
    dep_specs, dep_args = _dep_operand(dep)

    def body(*refs):
        x_refs, g_ref, hs_ref, hst_ref = refs[:nchunk], refs[nchunk], refs[-2], refs[-1]
        first = pl.program_id(0) * rows
        for p, d in enumerate(DILATIONS):
            idx = _folded_rows(first, rows, d)
            xv = jnp.concatenate([r[idx, :] for r in x_refs], axis=1)
            rstd = lax.rsqrt(jnp.mean(xv * xv, axis=1, keepdims=True) + EPS)
            h = xv * rstd * g_ref[...]
            hs_ref[p] = h.astype(BF16)
            hst_ref[p] = h.T.astype(BF16)

    x_specs = [pl.BlockSpec((SEQ, 128), functools.partial(lambda c, i: (0, c), c)) for c in range(nchunk)]
    return pl.pallas_call(
        body, name="prenorm", grid=(SEQ // rows,),
        in_specs=x_specs + [pl.BlockSpec((1, D_MODEL), lambda i: (0, 0))] + dep_specs,
        out_specs=[pl.BlockSpec((3, rows, D_MODEL), lambda i: (0, i, 0)),
                   pl.BlockSpec((3, D_MODEL, rows), lambda i: (0, 0, i))],
        out_shape=[jax.ShapeDtypeStruct((3, SEQ, D_MODEL), BF16), jax.ShapeDtypeStruct((3, D_MODEL, SEQ), BF16)],
        compiler_params=_params(("parallel",)))(*([x] * nchunk), gain, *dep_args)


def _prenorm_bwd(x, gain, dh, dout):
    rows = 512

    def body(x_ref, g_ref, a_ref, do_ref, dx_ref, gg_ref):
        xv = x_ref[...]
        rstd = lax.rsqrt(jnp.mean(xv * xv, axis=1, keepdims=True) + EPS)
        xn = xv * rstd
        dh = jnp.concatenate([a_ref[c] for c in range(D_MODEL // 128)], axis=1)
        gdh = dh * g_ref[...]
        dx_ref[...] = rstd * (gdh - xn * jnp.mean(gdh * xn, axis=1, keepdims=True)) + do_ref[...]

        @pl.when(pl.program_id(0) == 0)
        def _():
            gg_ref[...] = jnp.zeros((1, D_MODEL), F32)

        gg_ref[...] += jnp.sum(dh * xn, axis=0, keepdims=True)

    row = pl.BlockSpec((rows, D_MODEL), lambda i: (i, 0))
    vec = pl.BlockSpec((1, D_MODEL), lambda i: (0, 0))
    return pl.pallas_call(
        body, name="prenorm_bwd", grid=(SEQ // rows,),
        in_specs=[row, vec, pl.BlockSpec((D_MODEL // 128, rows, 128), lambda i: (0, i, 0)), row], out_specs=[row, vec],
        out_shape=[jax.ShapeDtypeStruct((SEQ, D_MODEL), F32), jax.ShapeDtypeStruct((1, D_MODEL), F32)],
        compiler_params=_params(("arbitrary",)))(x, gain, dh, dout)


def _memnorm_bwd(mem, dmemn, dep=None):
    dep_specs, dep_args = _dep_operand(dep)

    def body(m_ref, d_ref, *rest):
        mv = m_ref[...]
        rstd = lax.rsqrt(jnp.mean(mv * mv, axis=1, keepdims=True) + EPS)
        rest[-1][...] = jnp.sum(d_ref[...] * mv * rstd, axis=0, keepdims=True)

    whole = pl.BlockSpec(memory_space=pltpu.VMEM)
    return pl.pallas_call(
        body, name="memnorm_bwd", in_specs=[whole, whole] + dep_specs,
        out_shape=jax.ShapeDtypeStruct((1, D_MODEL), F32), compiler_params=_params())(mem, dmemn, *dep_args)


def _dep_operand(dep):
    return ([], []) if dep is None else ([pl.BlockSpec(memory_space=pl.ANY)], [dep])


def _in_proj(name, hs, wt, tabs, order, prev=None, dep=None):
    tm, tn = 512, 512
    prev_specs, prev_args = ([], []) if prev is None else ([ANY], [prev])
    dep_specs, dep_args = _dep_operand(dep)

    def body(order_ref, h_ref, w_ref, t_ref, *rest):
        o_ref = rest[-1]
        j = order_ref[pl.program_id(0)]
        is_rope = jnp.logical_and(j < 9, j % 3 != 2)
        row_slices = [slice(r * tm, (r + 1) * tm) for r in range(SEQ // tm)]

        def product(rs):
            return lax.dot_general(h_ref[rs, :], w_ref[...], NT, preferred_element_type=F32)

        @pl.when(is_rope)
        def _():
            for rs in row_slices:
                acc = product(rs)
                c, s1, s2 = t_ref[0, rs, :], t_ref[1, rs, :], t_ref[2, rs, :]
                for q in range(tn // 128):
                    a = acc[:, q * 128:(q + 1) * 128]
                    o_ref[rs, q * 128:(q + 1) * 128] = _rope(a, c, s1, s2).astype(BF16)

        @pl.when(jnp.logical_not(is_rope))
        def _():
            for rs in row_slices:
                o_ref[rs, :] = product(rs).astype(BF16)

    grid_spec = pltpu.PrefetchScalarGridSpec(
        num_scalar_prefetch=1, grid=(order.shape[0],),
        in_specs=[pl.BlockSpec((None, SEQ, D_MODEL), lambda t, o: (_perm_of_block(o[t]), 0, 0)),
                  pl.BlockSpec((tn, D_MODEL), lambda t, o: (o[t], 0)),
                  pl.BlockSpec((None, 3, SEQ, 128), lambda t, o: (_perm_of_block(o[t]), 0, 0, 0))] + prev_specs
        + dep_specs,
        out_specs=pl.BlockSpec((SEQ, tn), lambda t, o: (0, o[t])))
    return pl.pallas_call(
        body, name=name, grid_spec=grid_spec, out_shape=jax.ShapeDtypeStruct((SEQ, N_IN), BF16),
        input_output_aliases={} if prev is None else {4: 0},
        compiler_params=_params(("arbitrary",)))(order, hs, wt, tabs, *prev_args, *dep_args)


def _piece_blocks(pieces):
    return [(a, h * 512) for a, p in enumerate(pieces) for h in range(p.shape[1] // 512)]


def _block_fetch(piece_refs, blocks, buf, sem):
    def start(block, slot):
        for b, (a, col) in enumerate(blocks):
            @pl.when(block == b)
            def _():
                pltpu.make_async_copy(piece_refs[a].at[:, pl.ds(col, 512)], buf.at[slot], sem.at[slot]).start()

    def wait(slot):
        pltpu.make_async_copy(piece_refs[0].at[:, pl.ds(0, 512)], buf.at[slot], sem.at[slot]).wait()

    return start, wait


def _in_proj_dw(pieces, hst, dep=None):
    tn = 512
    blocks = _piece_blocks(pieces)
    nblk = len(blocks)
    npc = len(pieces)
    dep_specs, dep_args = _dep_operand(dep)

    def body(h_ref, *rest):
        piece_refs = rest[:npc]
        own_out, mirror, buf, sem, out_buf, send_sems, recv_sem, local_sems = rest[-8:]
        j = pl.program_id(0)
        slot = j % 2
        start, wait = _block_fetch(piece_refs, blocks, buf, sem)
        x, y, c = _place()

        def rows_of(step):
            return pl.ds(pl.multiple_of(step * tn, tn), tn)

        def to_sibling(step, slot_):
            return pltpu.make_async_remote_copy(
                src_ref=out_buf.at[slot_], dst_ref=mirror.at[rows_of(step)],
                send_sem=send_sems.at[slot_], recv_sem=recv_sem, device_id=(x, y, 1 - c), device_id_type=MESH_ID)

        def to_own(step, slot_):
            return pltpu.make_async_copy(out_buf.at[slot_], own_out.at[rows_of(step)], local_sems.at[slot_])

        @pl.when(j == 0)
        def _():
            start(j, slot)

        wait(slot)

        @pl.when(j + 1 < nblk)
        def _():
            start(j + 1, 1 - slot)

        acc = jnp.dot(h_ref[...], buf[slot], preferred_element_type=F32)

        @pl.when(j >= 2)
        def _():
            to_sibling(j - 2, slot).wait_send()
            to_own(j - 2, slot).wait()

        out_buf[slot] = acc.T.astype(BF16)
        to_sibling(j, slot).start()
        to_own(j, slot).start()

        @pl.when(j == nblk - 1)
        def _():
            to_sibling(j - 1, 1 - slot).wait_send()
            to_own(j - 1, 1 - slot).wait()
            to_sibling(j, slot).wait_send()
            to_own(j, slot).wait()
            pltpu.make_async_remote_copy(src_ref=mirror, dst_ref=mirror, send_sem=send_sems.at[0], recv_sem=recv_sem,
                                         device_id=(x, y, 1 - c), device_id_type=MESH_ID).wait_recv()

    return pl.pallas_call(
        body, name="in_proj_dw", grid=(nblk,),
        in_specs=[pl.BlockSpec((None, D_MODEL, SEQ), lambda j: (_perm_of_block(j), 0, 0))] + [ANY] * npc + dep_specs,
        out_specs=[ANY, ANY],
        out_shape=[jax.ShapeDtypeStruct((N_IN, D_MODEL), BF16), jax.ShapeDtypeStruct((N_IN, D_MODEL), BF16)],
        scratch_shapes=[pltpu.VMEM((2, SEQ, tn), BF16), pltpu.SemaphoreType.DMA((2,)),
                        pltpu.VMEM((2, tn, D_MODEL), BF16), pltpu.SemaphoreType.DMA((2,)), pltpu.SemaphoreType.DMA,
                        pltpu.SemaphoreType.DMA((2,))],
        compiler_params=_params(("arbitrary",)))(hst, *pieces, *dep_args)


def _in_proj_dh(pieces, wt, dep=None):
    tk = 512
    blocks = _piece_blocks(pieces)
    nblk = len(blocks)
    npc = len(pieces)
    nchunk = D_MODEL // 128

    def col(s):
        return jnp.where(s < 3, s, jnp.where(s < 16, s + 6, s - 13))

    dep_specs, dep_args = _dep_operand(dep)

    def body(w_ref, *rest):
        piece_refs = rest[:npc]
        o_ref, acc_ref, buf, sem = rest[-4:]
        s = pl.program_id(0)
        slot = s % 2
        start, wait = _block_fetch(piece_refs, blocks, buf, sem)

        @pl.when(s == 0)
        def _():
            start(col(s), slot)

        wait(slot)

        @pl.when(s + 1 < nblk)
        def _():
            start(col(s + 1), 1 - slot)

        row_slices = [slice(r * 512, (r + 1) * 512) for r in range(SEQ // 512)]

        def product(rs):
            return jnp.dot(buf[slot, rs, :], w_ref[...], preferred_element_type=F32)

        def accumulate(cond, to_out, init):
            @pl.when(cond)
            def _():
                for rs in row_slices:
                    prod = product(rs)
                    if not to_out:
                        if init:
                            acc_ref[rs, :] = prod
                        else:
                            acc_ref[rs, :] += prod
                        continue
                    for c in range(nchunk):
                        if init:
                            o_ref[c, rs, :] = prod[:, c * 128:(c + 1) * 128]
                        else:
                            o_ref[c, rs, :] += prod[:, c * 128:(c + 1) * 128]

        accumulate(s == 0, True, True)
        accumulate(jnp.logical_and(s > 0, s < 16), True, False)
        accumulate(jnp.logical_or(s == 16, s == 19), False, True)
        accumulate(jnp.logical_and(s > 16, s != 19), False, False)
        for last, d in ((18, 4), (21, 16)):
            @pl.when(s == last)
            def _():
                mlen = SEQ // d
                for r in range(d):
                    for c in range(nchunk):
                        o_ref[c, pl.ds(r, mlen, stride=d), :] += acc_ref[r * mlen:(r + 1) * mlen,
                                                                         c * 128:(c + 1) * 128]

    return pl.pallas_call(
        body, name="in_proj_dh", grid=(nblk,),
        in_specs=[pl.BlockSpec((tk, D_MODEL), lambda s: (col(s), 0))] + [ANY] * npc + dep_specs,
        out_specs=pl.BlockSpec((nchunk, SEQ, 128), lambda s: (0, 0, 0)),
        out_shape=jax.ShapeDtypeStruct((nchunk, SEQ, 128), F32),
        scratch_shapes=[pltpu.VMEM((SEQ, D_MODEL), F32), pltpu.VMEM((2, SEQ, tk), BF16),
                        pltpu.SemaphoreType.DMA((2,))],
        compiler_params=_params(("arbitrary",)))(wt, *pieces, *dep_args)


def _head_lanes(lanes, hh):
    return lanes >= 64 if hh == 1 else lanes < 64


def _head_rows(x, lanes, hh, pair):
    if not pair:
        return jnp.max(x, axis=1, keepdims=True)
    return jnp.max(jnp.where(_head_lanes(lanes, hh), x, -jnp.inf), axis=1, keepdims=True)


def _mask_head(x, lanes, hh, pair, scale=1.0):
    if not pair:
        return x
    xf = x.astype(F32) if scale == 1.0 else x.astype(F32) * scale
    return jnp.where(_head_lanes(lanes, hh), xf, 0.0).astype(BF16)


def _window(mode, qi, tq, mlen, tk):
    if mode == "dil":
        q0 = qi * tq
        seg = (q0 // mlen) * mlen
        ks = jnp.clip(q0 - REACH, seg, seg + mlen - tk)
        return pl.multiple_of(ks, 64)
    if mode == "na":
        r_start = jnp.clip(qi - NA_ROWS // 2, 0, SEQ // GRID_W - NA_ROWS)
        return pl.multiple_of(r_start * GRID_W, 64)
    return 0


def _band_mask(qi, tq, tk, ks):
    qpos = qi * tq + _iota((tq, tk), 0)
    kpos = ks + _iota((tq, tk), 1)
    return jnp.where(jnp.abs(qpos - kpos) <= REACH, 0.0, NEG).astype(F32)


def _stack_heads(x, lanes, pair, scale=1.0):
    if not pair:
        return x
    return jnp.concatenate([_mask_head(x, lanes, hh, pair, scale) for hh in range(2)], axis=0)


def _stack_rows(x, lanes, pair):
    if not pair:
        return _head_rows(x, lanes, 0, pair)
    return jnp.concatenate([_head_rows(x, lanes, hh, pair) for hh in range(2)], axis=0)


def _unstack_heads(x, lanes, pair, tq):
    if not pair:
        return x
    return jnp.where(lanes < 64, x[:tq], x[tq:])


def _scores(mode, qst, k, sscale, band, qi, bias_ref, pair):
    s = lax.dot_general(qst, k, NT, preferred_element_type=F32)
    if sscale != 1.0:
        s = s * sscale
    if mode == "dil":
        s = s + jnp.concatenate([band, band], axis=0)
    elif mode == "na":
        off = qi - jnp.clip(qi - NA_ROWS // 2, 0, SEQ // GRID_W - NA_ROWS)
        s = s + jnp.concatenate([bias_ref[0, off], bias_ref[1, off]], axis=0)
    return s


def _attn_cfg(mode, d):
    if mode == "dil":
        mlen = SEQ // d
        return dict(pair=True, tq=128, tk=min(256, mlen), mlen=mlen, lk=SEQ, scale=HEAD_DIM ** -0.5, units=4,
                    nsub=ATTN_SUBTILES)
    if mode == "na":
        return dict(pair=True, tq=GRID_W, tk=NA_ROWS * GRID_W, mlen=SEQ, lk=SEQ, scale=HEAD_DIM ** -0.5, units=4,
                    nsub=2 * ATTN_SUBTILES)
    return dict(pair=False, tq=128, tk=MEM_LEN, mlen=SEQ, lk=MEM_LEN, scale=128 ** -0.5, units=4,
                nsub=ATTN_SUBTILES)


ATTN_SUBTILES = 16


def _attn_fwd(name, mode, q_arr, k_arr, v_arr, qcol, kcol, vcol, d=1, bias=None):
    cfg = _attn_cfg(mode, d)
    pair, tq, tk, mlen, lk, scale = cfg["pair"], cfg["tq"], cfg["tk"], cfg["mlen"], cfg["lk"], cfg["scale"]
    qscale, sscale = (scale, 1.0) if pair else (1.0, scale)
    nsub = cfg["nsub"]
    rows = nsub * tq

    def body(*refs):
        if mode == "na":
            q_ref, k_ref, v_ref, bias_ref, o_ref, l_ref = refs
        else:
            q_ref, k_ref, v_ref, o_ref, l_ref = refs
            bias_ref = None
        lanes = _iota((tq, 128), 1)
        qis = [pl.program_id(1) * nsub + sub for sub in range(nsub)]
        kss = [_window(mode, qi, tq, mlen, tk) for qi in qis]
        vs = [v_ref[pl.ds(ks, tk), :] for ks in kss]
        bands = [_band_mask(qi, tq, tk, ks) if mode == "dil" else None for qi, ks in zip(qis, kss)]
        ss = []
        for sub in range(nsub):
            qst = _stack_heads(q_ref[sub * tq:(sub + 1) * tq, :], lanes, pair, qscale)
            k = k_ref[pl.ds(kss[sub], tk), :]
            ss.append(_scores(mode, qst, k, sscale, bands[sub], qis[sub], bias_ref, pair))
        ms = [jnp.max(s_, axis=1, keepdims=True) for s_ in ss]
        ps = [jnp.exp(s_ - m) for s_, m in zip(ss, ms)]
        ls = [jnp.sum(p, axis=1, keepdims=True) for p in ps]
        os_ = [jnp.dot(p.astype(BF16), v, preferred_element_type=F32) for p, v in zip(ps, vs)]
        for sub in range(nsub):
            out = _unstack_heads(os_[sub] / ls[sub], lanes, pair, tq)
            lse = ms[sub] + jnp.log(ls[sub])
            lse = _unstack_heads(jnp.broadcast_to(lse, (lse.shape[0], 128)), lanes, pair, tq)
            dst = _folded_rows(qis[sub] * tq, tq, d) if mode == "dil" else slice(sub * tq, (sub + 1) * tq)
            o_ref[dst, :] = out
            l_ref[dst, :] = lse

    in_specs = [pl.BlockSpec((rows, 128), lambda u, i: (i, qcol + u)),
                pl.BlockSpec((lk, 128), lambda u, i: (0, kcol + u)),
                pl.BlockSpec((lk, 128), lambda u, i: (0, vcol + u))]
    args = [q_arr, k_arr, v_arr]
    if mode == "na":
        in_specs.append(pl.BlockSpec((2, NA_ROWS, GRID_W, NA_ROWS * GRID_W), lambda u, i: (u, 0, 0, 0)))
        args.append(bias)
    if mode == "dil":
        out_spec = pl.BlockSpec((SEQ, 128), lambda u, i: (0, u))
    else:
        out_spec = pl.BlockSpec((rows, 128), lambda u, i: (i, u))
    return pl.pallas_call(
        body, name=name, grid=(cfg["units"], SEQ // rows), in_specs=in_specs, out_specs=[out_spec, out_spec],
        out_shape=[jax.ShapeDtypeStruct((SEQ, 512), F32), jax.ShapeDtypeStruct((SEQ, 512), F32)],
        compiler_params=_params(("parallel", "arbitrary")))(*args)


def _attn_bwd(name, mode, q_arr, k_arr, v_arr, qcol, kcol, vcol, do, lse, dp=None, o=None, d=1, bias=None,
              tabs=None, dep=None):
    cfg = _attn_cfg(mode, d)
    pair, tq, tk, mlen, lk, scale = cfg["pair"], cfg["tq"], cfg["tk"], cfg["mlen"], cfg["lk"], cfg["scale"]
    qscale, sscale = (scale, 1.0) if pair else (1.0, scale)
    nsub = cfg["nsub"]
    rows = nsub * tq
    nq = SEQ // rows
    kv_dtype = F32 if mode == "mem" else BF16
    dep_specs, dep_args = _dep_operand(dep)
    mode_inputs = {"dil": 3, "na": 2, "mem": 1}[mode]

    def body(*refs):
        refs = list(refs)
        q_ref, k_ref, v_ref, do_ref, l_ref = refs[:5]
        rest = refs[5:5 + mode_inputs] + refs[5 + mode_inputs + len(dep_args):]
        bias_ref = tq_ref = tk_ref = db_ref = None
        if mode == "dil":
            dp_ref, tq_ref, tk_ref, dq_ref, dk_ref, dv_ref, dk_acc, dv_acc = rest
        elif mode == "na":
            o_ref, bias_ref, dq_ref, dk_ref, dv_ref, db_ref, dk_acc, dv_acc = rest
        else:
            o_ref, dq_ref, dk_ref, dv_ref, dk_acc, dv_acc = rest
        step = pl.program_id(1)

        @pl.when(step == 0)
        def _():
            dk_acc[...] = jnp.zeros((lk, 128), F32)
            dv_acc[...] = jnp.zeros((lk, 128), F32)
            if mode == "na":
                db_ref[...] = jnp.zeros(db_ref.shape, F32)

        lanes = _iota((tq, 128), 1)
        qis = [step * nsub + sub for sub in range(nsub)]
        sls = [slice(sub * tq, (sub + 1) * tq) for sub in range(nsub)]
        kss = [_window(mode, qi, tq, mlen, tk) for qi in qis]
        ks_ = [k_ref[pl.ds(ks, tk), :] for ks in kss]
        vs = [v_ref[pl.ds(ks, tk), :] for ks in kss]
        qsts, dosts, lses, dphs = [], [], [], []
        for sub in range(nsub):
            if mode == "dil":
                src = _folded_rows(qis[sub] * tq, tq, d)
                dov = do_ref[src, :].astype(BF16)
                lsev = l_ref[src, :]
                dphs.append(_stack_rows(dp_ref[src, :], lanes, pair))
            else:
                dov = do_ref[sls[sub], :]
                lsev = l_ref[sls[sub], :]
                dpv = dov.astype(F32) * o_ref[sls[sub], :]
                if pair:
                    dphs.append(jnp.concatenate(
                        [jnp.sum(jnp.where(_head_lanes(lanes, hh), dpv, 0.0), axis=1, keepdims=True)
                         for hh in range(2)], axis=0))
                else:
                    dphs.append(jnp.sum(dpv, axis=1, keepdims=True))
            qsts.append(_stack_heads(q_ref[sls[sub], :], lanes, pair, qscale))
            dosts.append(_stack_heads(dov, lanes, pair))
            lses.append(_stack_rows(lsev, lanes, pair))
        bands = [_band_mask(qi, tq, tk, ks) if mode == "dil" else None for qi, ks in zip(qis, kss)]
        ss = [_scores(mode, qsts[sub], ks_[sub], sscale, bands[sub], qis[sub], bias_ref, pair) for sub in range(nsub)]
        dpms = [lax.dot_general(dosts[sub], vs[sub], NT, preferred_element_type=F32) for sub in range(nsub)]
        ps = [jnp.exp(s_ - lse) for s_, lse in zip(ss, lses)]
        dss = [p * (dpm - dph) for p, dpm, dph in zip(ps, dpms, dphs)]
        if mode == "na":
            for sub, ds in enumerate(dss):
                off = qis[sub] - jnp.clip(qis[sub] - NA_ROWS // 2, 0, SEQ // GRID_W - NA_ROWS)
                db_ref[0, off] += ds[:tq]
                db_ref[1, off] += ds[tq:]
        dsbs = [ds.astype(BF16) for ds in dss]
        dvs = [lax.dot_general(p.astype(BF16), dosts[sub], TN, preferred_element_type=F32)
               for sub, p in enumerate(ps)]
        dqs = [jnp.dot(dsb, ks_[sub], preferred_element_type=F32) * scale for sub, dsb in enumerate(dsbs)]
        dks = [lax.dot_general(dsb, qsts[sub], TN, preferred_element_type=F32) for sub, dsb in enumerate(dsbs)]
        for sub in range(nsub):
            sl = sls[sub]
            dq = _unstack_heads(dqs[sub], lanes, pair, tq)
            if mode == "dil":
                dq = _rope_t(dq, tq_ref[0, sl, :], tq_ref[1, sl, :], tq_ref[2, sl, :])
            dq_ref[sl, :] = dq.astype(BF16)
            dk_acc[pl.ds(kss[sub], tk), :] += dks[sub] if pair else dks[sub] * scale
            dv_acc[pl.ds(kss[sub], tk), :] += dvs[sub]

        @pl.when(step == nq - 1)
        def _():
            dkv = dk_acc[...]
            if mode == "dil":
                dkv = _rope_t(dkv, tk_ref[0], tk_ref[1], tk_ref[2])
            dk_ref[...] = dkv.astype(kv_dtype)
            dv_ref[...] = dv_acc[...].astype(kv_dtype)

    q_spec = pl.BlockSpec((rows, 128), lambda u, i: (i, qcol + u))
    row_spec = pl.BlockSpec((rows, 128), lambda u, i: (i, u))
    kv_out = pl.BlockSpec((lk, 128), lambda u, i: (0, u))
    whole = pl.BlockSpec((SEQ, 128), lambda u, i: (0, u))
    nat_spec = whole if mode == "dil" else row_spec
    in_specs = [q_spec,
                pl.BlockSpec((lk, 128), lambda u, i: (0, kcol + u)),
                pl.BlockSpec((lk, 128), lambda u, i: (0, vcol + u)),
                nat_spec, nat_spec]
    args = [q_arr, k_arr, v_arr, do, lse]
    out_specs = [row_spec, kv_out, kv_out]
    out_shape = [jax.ShapeDtypeStruct((SEQ, 512), BF16), jax.ShapeDtypeStruct((lk, 512), kv_dtype),
                 jax.ShapeDtypeStruct((lk, 512), kv_dtype)]
    if mode == "dil":
        in_specs += [whole, pl.BlockSpec((3, rows, 128), lambda u, i: (0, i, 0)),
                     pl.BlockSpec((3, SEQ, 128), lambda u, i: (0, 0, 0))]
        args += [dp, tabs, tabs]
    elif mode == "na":
        b_spec = pl.BlockSpec((2, NA_ROWS, GRID_W, NA_ROWS * GRID_W), lambda u, i: (u, 0, 0, 0))
        in_specs += [row_spec, b_spec]
        args += [o, bias]
        out_specs.append(b_spec)
        out_shape.append(jax.ShapeDtypeStruct((8, NA_ROWS, GRID_W, NA_ROWS * GRID_W), F32))
    else:
        in_specs.append(row_spec)
        args.append(o)
    return pl.pallas_call(
        body, name=name, grid=(cfg["units"], nq), in_specs=in_specs + dep_specs, out_specs=out_specs,
        out_shape=out_shape, scratch_shapes=[pltpu.VMEM((lk, 128), F32), pltpu.VMEM((lk, 128), F32)],
        compiler_params=_params(("parallel", "arbitrary")))(*args, *dep_args)


def _na_geometry():
    qc = _iota((GRID_W, 128), 0)
    lane = _iota((GRID_W, 128), 1)
    kc = lane & 63
    c_start = jnp.clip(qc - 8, 0, GRID_W - 16)
    valid = jnp.logical_and(kc >= c_start, kc < c_start + 16)
    return lane, valid


def _na_bias(rpb_rows, dep=None):
    dep_specs, dep_args = _dep_operand(dep)

    def body(r_ref, *rest):
        o_ref, t_ref = rest[-2:]
        lane, valid = _na_geometry()
        for dd in range(14):
            row_a = jnp.broadcast_to(r_ref[dd:dd + 1, :], (GRID_W, 128))
            row_b = jnp.broadcast_to(r_ref[dd + 1:dd + 2, :], (GRID_W, 128))
            both = jnp.where(lane < 64, row_a, pltpu.roll(row_b, 64, 1))
            t = pltpu.roll(both, 128 - 15, 1, stride=1, stride_axis=0)
            t_ref[dd] = jnp.where(valid, t, NEG)
        for off in range(NA_ROWS):
            for p in range(4):
                o_ref[off, :, p * 128:(p + 1) * 128] = t_ref[2 * p - off + 7]

    return pl.pallas_call(
        body, name="na_bias", grid=(8,),
        in_specs=[pl.BlockSpec((None, 16, 128), lambda h: (h, 0, 0))] + dep_specs,
        out_specs=pl.BlockSpec((None, NA_ROWS, GRID_W, NA_ROWS * GRID_W), lambda h: (h, 0, 0, 0)),
        out_shape=jax.ShapeDtypeStruct((8, NA_ROWS, GRID_W, NA_ROWS * GRID_W), F32),
        scratch_shapes=[pltpu.VMEM((14, GRID_W, 128), F32)],
        compiler_params=_params(("parallel",)))(rpb_rows, *dep_args)


def _na_bias_bwd(dbias, dep=None):
    dep_specs, dep_args = _dep_operand(dep)

    def body(d_ref, *rest):
        o_ref = rest[-1]
        lane, valid = _na_geometry()
        reverse = (_iota((GRID_W, GRID_W), 0) + _iota((GRID_W, GRID_W), 1) == GRID_W - 1).astype(F32)
        o_ref[...] = jnp.zeros((16, 128), F32)
        for dd in range(14):
            t = jnp.zeros((GRID_W, 128), F32)
            for off in range(NA_ROWS):
                for p in range(4):
                    if 2 * p - off + 7 == dd:
                        t = t + d_ref[off, :, p * 128:(p + 1) * 128]
            t = jnp.dot(reverse, jnp.where(valid, t, 0.0), precision=lax.Precision.HIGHEST,
                        preferred_element_type=F32)
            t = pltpu.roll(t, 128 - (GRID_W - 16), 1, stride=1, stride_axis=0)
            o_ref[dd:dd + 1, :] = jnp.sum(t, axis=0, keepdims=True)

    return pl.pallas_call(
        body, name="na_bias_bwd", grid=(8,),
        in_specs=[pl.BlockSpec((None, NA_ROWS, GRID_W, NA_ROWS * GRID_W), lambda h: (h, 0, 0, 0))] + dep_specs,
        out_specs=pl.BlockSpec((None, 16, 128), lambda h: (h, 0, 0)),
        out_shape=jax.ShapeDtypeStruct((8, 16, 128), F32),
        compiler_params=_params(("parallel",)))(dbias, *dep_args)


GATE_ROWS = 128


def _group_weights(l0, l1, l2):
    m = jnp.maximum(jnp.maximum(l0, l1), l2)
    e0, e1, e2 = jnp.exp(l0 - m), jnp.exp(l1 - m), jnp.exp(l2 - m)
    inv = 1.0 / (e0 + e1 + e2)
    return e0 * inv, e1 * inv, e2 * inv


def _gate_block(o_grp, l_grp, out_b, out_c, parts, x, target, merge_bias, branch_rows, out_rows, gain, head_sum):
    rows = GATE_ROWS
    r512 = pl.BlockSpec((rows, 512), lambda i: (i, 0))
    r1024 = pl.BlockSpec((rows, D_MODEL), lambda i: (i, 0))
    silu_cols = [pl.BlockSpec((rows, 512), functools.partial(lambda b, i: (i, b), 13 + b)) for b in range(3)]
    logit_cols = [pl.BlockSpec((rows, D_MODEL), functools.partial(lambda b, i: (i, b), 8 + b)) for b in range(3)]

    def body(o0, o1, o2, l0, l1, l2, ob, oc, ga, gb, gc, la, lb, lc, x_ref, t_ref, mb, wa, wb, wc, wo_ref, gn_ref,
             hs_ref, dout_ref, dla, dlb, dlc, dga, dgb, dgc, do0, do1, do2, dp0, dp1, dp2, dob, doc, err_ref, gg_ref,
             gmb, gwa, gwb, gwc, gwo, acc_a, acc_b, acc_c, acc_o):
        step = pl.program_id(0)
        whole = lambda w_ref: w_ref[...].reshape(D_MODEL, w_ref.shape[-1])
        ws = _group_weights(l0[...], l1[...], l2[...])
        out_a = ws[0] * o0[...] + ws[1] * o1[...] + ws[2] * o2[...]
        branches = ((out_a, ga, la, wa, acc_a, dla, dga), (ob[...], gb, lb, wb, acc_b, dlb, dgb),
                    (oc[...], gc, lc, wc, acc_c, dlc, dgc))

        @pl.when(step == 0)
        def _():
            for acc in (acc_a, acc_b, acc_c, acc_o):
                acc[...] = jnp.zeros(acc.shape, F32)
            err_ref[...] = jnp.zeros((1, D_MODEL), F32)
            gg_ref[...] = jnp.zeros((1, D_MODEL), F32)
            gmb[...] = jnp.zeros((3, D_MODEL), F32)

        y = jnp.zeros((rows, D_MODEL), F32)
        zs, gates, silus, dsilus, us = [], [], [], [], []
        for b, (ov, g_ref, l_ref, w_ref, _, _, _) in enumerate(branches):
            g = g_ref[...].astype(F32)
            sg = _sigmoid(g)
            silus.append(g * sg)
            dsilus.append(sg * (1.0 + g * (1.0 - sg)))
            us.append((ov * silus[b]).astype(BF16))
            zs.append(lax.dot_general(us[b], whole(w_ref), NT, preferred_element_type=F32))
            gates.append(_sigmoid(l_ref[...].astype(F32) + mb[b:b + 1, :]))
            y = y + gates[b] * zs[b]
        yb = y.astype(BF16)
        y2 = jnp.dot(yb, whole(wo_ref), preferred_element_type=F32)
        rstd = lax.rsqrt(jnp.mean(y2 * y2, axis=1, keepdims=True) + EPS)
        yn = y2 * rstd
        gv = gn_ref[...]
        err = x_ref[...] + yn * gv - t_ref[...]
        dout = err * (1.0 / D_MODEL)
        dout_ref[...] = dout
        dn = dout * gv
        dy2 = (rstd * (dn - yn * jnp.mean(dn * yn, axis=1, keepdims=True))).astype(BF16)
        acc_o[...] += lax.dot_general(yb, dy2, TN, preferred_element_type=F32)
        err_ref[...] += jnp.sum(err * err, axis=0, keepdims=True)
        gg_ref[...] += jnp.sum(dout * yn, axis=0, keepdims=True)
        dy = lax.dot_general(dy2, whole(wo_ref), NT, preferred_element_type=F32)
        dos = []
        for b, (ov, _, _, w_ref, acc, dl_ref, dg_ref) in enumerate(branches):
            dl = dy * zs[b] * gates[b] * (1.0 - gates[b])
            dl_ref[...] = dl.astype(BF16)
            gmb[b:b + 1, :] += jnp.sum(dl, axis=0, keepdims=True)
            dz = (dy * gates[b]).astype(BF16)
            acc[...] += lax.dot_general(dz, us[b], TN, preferred_element_type=F32)
            du = jnp.dot(dz, whole(w_ref), preferred_element_type=F32)
            dos.append(du * silus[b])
            dg_ref[...] = (du * ov * dsilus[b]).astype(BF16)
        dob[...] = dos[1].astype(BF16)
        doc[...] = dos[2].astype(BF16)
        row_term = jnp.dot(dos[0] * out_a, hs_ref[...], precision=lax.Precision.HIGHEST, preferred_element_type=F32)
        for wg, do_ref, dp_ref in zip(ws, (do0, do1, do2), (dp0, dp1, dp2)):
            do_ref[...] = wg * dos[0]
            dp_ref[...] = wg * row_term

        @pl.when(step == SEQ // rows - 1)
        def _():
            for acc, out in ((acc_a, gwa), (acc_b, gwb), (acc_c, gwc), (acc_o, gwo)):
                out[...] = acc[...].astype(BF16)

    full = lambda shape: pl.BlockSpec(shape, lambda i: (0,) * len(shape))
    vec = pl.BlockSpec((1, D_MODEL), lambda i: (0, 0))
    acc3 = pl.BlockSpec((3, D_MODEL), lambda i: (0, 0))
    shard = D_MODEL // N_DEV
    dev_rows = lambda width, k: pl.BlockSpec((N_DEV, shard, width), lambda i: (0, k, 0))
    in_specs = ([r512] * 8 + silu_cols + logit_cols + [r1024, r1024, full((3, D_MODEL))]
                + [dev_rows(512, k) for k in range(3)] + [dev_rows(D_MODEL, 1), vec, full((512, 512))])
    out_specs = ([r1024] + [r1024] * 3 + [r512] * 3 + [r512] * 6 + [r512] * 2 + [vec, vec, acc3]
                 + [full((D_MODEL, 512))] * 3 + [full((D_MODEL, D_MODEL))])
    bf, f32 = BF16, F32
    sds = jax.ShapeDtypeStruct
    out_shape = ([sds((SEQ, D_MODEL), f32)] + [sds((SEQ, D_MODEL), bf)] * 3 + [sds((SEQ, 512), bf)] * 3
                 + [sds((SEQ, 512), f32)] * 6 + [sds((SEQ, 512), bf)] * 2 + [sds((1, D_MODEL), f32)] * 2
                 + [sds((3, D_MODEL), f32)] + [sds((D_MODEL, 512), bf)] * 3 + [sds((D_MODEL, D_MODEL), bf)])
    res = pl.pallas_call(
        body, name="gate_block", grid=(SEQ // rows,), in_specs=in_specs, out_specs=out_specs, out_shape=out_shape,
        scratch_shapes=[pltpu.VMEM((D_MODEL, 512), F32)] * 3 + [pltpu.VMEM((D_MODEL, D_MODEL), F32)],
        compiler_params=_params(("arbitrary",)))(
            *o_grp, *l_grp, out_b, out_c, parts, parts, parts, parts, parts, parts, x, target, merge_bias,
            branch_rows, branch_rows, branch_rows, out_rows, gain, head_sum)
    return dict(dout=res[0], dlog=res[1:4], dg=res[4:7], do_grp=res[7:10], dp_grp=res[10:13], do_b=res[13],
                do_c=res[14], err_sq=res[15], g_post=res[16], g_mb=res[17], g_wt=res[18:21], g_w_out=res[21])


def _local_step(x, hst, parts, tabs, bias, mem, target, pre_norm, mem_norm, post_norm, wt_in, late_weights,
                reduce_start=None):
    o_grp, l_grp = [], []
    for g, d in enumerate(DILATIONS):
        o, l = _attn_fwd("dil_fwd_%d" % g, "dil", parts, parts, parts, 12 * g, 12 * g + 4, 12 * g + 8, d=d)
        o_grp.append(o)
        l_grp.append(l)
    out_b, lse_b = _attn_fwd("na_fwd", "na", parts, parts, parts, 36, 40, 44, bias=bias)
    merge_bias, w_kv, branch_rows, out_rows = late_weights(sum(a[:8, :128] for a in [out_b] + o_grp))
    memn = _rmsnorm_fwd("memnorm", mem, mem_norm, MEM_LEN)
    kv_m = _mm_simple("mem_kv", memn, w_kv, NN, BF16, MEM_LEN, 512, D_MODEL)
    out_c, lse_c = _attn_fwd("mem_fwd", "mem", parts, kv_m, kv_m, 48, 0, 4)

    rr = _iota((512, 512), 0) // HEAD_DIM
    cc = _iota((512, 512), 1) // HEAD_DIM
    head_sum = (rr == cc).astype(F32)
    gb = _gate_block(o_grp, l_grp, out_b, out_c, parts, x, target, merge_bias, branch_rows, out_rows, post_norm,
                     head_sum)
    dout, dlog, dg, g_wt, g_w_out = gb["dout"], gb["dlog"], gb["dg"], gb["g_wt"], gb["g_w_out"]
    do_grp, dp_grp, do_b, do_c, g_post, g_mb = (gb["do_grp"], gb["dp_grp"], gb["do_b"], gb["do_c"], gb["g_post"],
                                                gb["g_mb"])
    loss = 0.5 * jnp.sum(gb["err_sq"]) / D_MODEL

    dq_c, dk_m, dv_m = _attn_bwd("mem_bwd", "mem", parts, kv_m, kv_m, 48, 0, 4, do_c, lse_c, o=out_c)
    dkv = jnp.concatenate([dk_m, dv_m], axis=1).astype(BF16)
    g_w_kv = _mm_simple("mem_kv_dw", memn, dkv, TN, BF16, D_MODEL, 512, MEM_LEN)
    dmemn = _mm_simple("mem_kv_dx", dkv, w_kv, NT, F32, MEM_LEN, 512, D_MODEL)
    grads = dict(w_kv=g_w_kv, wt_a=g_wt[0], wt_b=g_wt[1], wt_c=g_wt[2], w_out=g_w_out, merge_bias=g_mb,
                 post_norm=g_post)
    dep = reduce_start("rest_sibling", grads) if reduce_start is not None else None

    dq_b, dk_b, dv_b, dbias = _attn_bwd("na_bwd", "na", parts, parts, parts, 36, 40, 44, do_b, lse_b, o=out_b,
                                        bias=bias, dep=dep)
    dqkv = []
    for g, d in enumerate(DILATIONS):
        dq, dk, dv = _attn_bwd("dil_bwd_%d" % g, "dil", parts, parts, parts, 12 * g, 12 * g + 4, 12 * g + 8,
                               do_grp[g], l_grp[g], dp=dp_grp[g], d=d, tabs=tabs[g])
        dqkv += [dq, dk, dv]
    if reduce_start is not None:
        dep = reduce_start("rest_chips", grads, sum(a[:8, :128] for a in (dqkv[0], dqkv[3], dqkv[6], dq_b)))
    dparts = dqkv + [dq_b, dk_b, dv_b, dq_c] + list(dg) + list(dlog)
    grads["wt_in"] = _in_proj_dw(dparts, hst, dep)
    dep = reduce_start("w_in", grads) if reduce_start is not None else None
    dh = _in_proj_dh(dparts, wt_in, dep)
    if reduce_start is not None:
        dep = reduce_start("w_in_second", grads, dh)
    grad_x, grads["pre_norm"] = _prenorm_bwd(x, pre_norm, dh, dout)
    g_rpb_t = _na_bias_bwd(dbias, dep)
    grads["na_rpb"] = g_rpb_t[:, :15, :31] + jnp.pad(g_rpb_t[:, :14, 64:95], ((0, 0), (1, 0), (0, 0)))
    grads["mem_norm"] = _memnorm_bwd(mem, dmemn, dep)
    return loss, grad_x, grads


ANY = pl.BlockSpec(memory_space=pl.ANY)


def _place():
    return lax.axis_index("x"), lax.axis_index("y"), lax.axis_index("c")


HBM = pl.BlockSpec(memory_space=pltpu.HBM)
SEM = pl.BlockSpec(memory_space=pltpu.SEMAPHORE)
DATAFLOW = pltpu.SideEffectType.DATAFLOW_SIDE_EFFECTING


def _split_copies(kind, srcs, lands, send_sems, recv_sems):
    nt = len(srcs)
    x, y, c = _place()
    copies = []
    if kind == "sibling":
        for q in range(4):
            for t in range(nt):
                k = q * nt + t
                copies.append(pltpu.make_async_remote_copy(
                    src_ref=srcs[t].at[2 * q + 1 - c], dst_ref=lands[t].at[q], send_sem=send_sems.at[k],
                    recv_sem=recv_sems.at[k], device_id=(x, y, 1 - c), device_id_type=MESH_ID))
    elif kind in ("rs_a", "rs_b"):
        half = lands[0].shape[1]
        xn, yn = (1 - x, y, c), (x, 1 - y, c)
        q_xn, q_yn, q_dg = 2 * (1 - x) + y, 2 * x + 1 - y, 2 * (1 - x) + 1 - y
        if kind == "rs_a":
            plan = [(srcs[0].at[q_yn].at[pl.ds(0, half)], 0, yn), (srcs[0].at[q_dg].at[pl.ds(0, half)], 1, yn),
                    (srcs[0].at[q_xn].at[pl.ds(half, half)], 2, xn), (srcs[0].at[q_dg].at[pl.ds(half, half)], 3, xn)]
        else:
            plan = [(srcs[0].at[0], 0, xn), (srcs[0].at[1], 1, yn)]
        for k, (src, slot, to) in enumerate(plan):
            copies.append(pltpu.make_async_remote_copy(
                src_ref=src, dst_ref=lands[0].at[slot], send_sem=send_sems.at[k], recv_sem=recv_sems.at[k],
                device_id=to, device_id_type=MESH_ID))
    elif kind == "gather":
        me = 4 * x + 2 * y + c
        for mask in range(1, 8):
            fx, fy, fc = (mask >> 2) & 1, (mask >> 1) & 1, mask & 1
            to = (1 - x if fx else x, 1 - y if fy else y, 1 - c if fc else c)
            for t in range(nt):
                k = (mask - 1) * nt + t
                copies.append(pltpu.make_async_remote_copy(
                    src_ref=srcs[t], dst_ref=lands[t].at[me], send_sem=send_sems.at[k], recv_sem=recv_sems.at[k],
                    device_id=to, device_id_type=MESH_ID))
        for t in range(nt):
            copies.append(pltpu.make_async_copy(srcs[t], lands[t].at[me], recv_sems.at[7 * nt + t]))
    else:
        for s, (tx, ty) in enumerate([(1 - x, y), (x, 1 - y), (1 - x, 1 - y)]):
            for t in range(nt):
                k = s * nt + t
                copies.append(pltpu.make_async_remote_copy(
                    src_ref=srcs[t].at[2 * tx + ty], dst_ref=lands[t].at[s], send_sem=send_sems.at[k],
                    recv_sem=recv_sems.at[k], device_id=(tx, ty, c), device_id_type=MESH_ID))
    return copies


def _split_count(kind, nt):
    return {"gather": 8, "chips": 3, "sibling": 4, "rs_a": 4, "rs_b": 2}[kind] * nt


def _exchange_start(name, kind, srcs, land_shapes, after=None):
    nt = len(srcs)
    n = _split_count(kind, nt)
    dep_specs, dep_args = _dep_operand(after)
    nd = len(dep_args)

    def body(*refs):
        src_refs, land_refs = refs[:nt], refs[nt:2 * nt]
        send_sems, recv_sems = refs[2 * nt + nd], refs[2 * nt + nd + 1]
        token = refs[-1]
        for cp in _split_copies(kind, src_refs, land_refs, send_sems, recv_sems):
            cp.start()
        token[...] = jnp.zeros_like(token)

    lands = [pltpu.with_memory_space_constraint(lax.empty(s.shape, s.dtype), pltpu.HBM) for s in land_shapes]
    res = pl.pallas_call(
        body, name=name,
        out_shape=(pltpu.SemaphoreType.DMA((n,)), pltpu.SemaphoreType.DMA((n,)),
                   *[pltpu.HBM(s.shape, s.dtype) for s in srcs], *[pltpu.HBM(s.shape, s.dtype) for s in land_shapes],
                   jax.ShapeDtypeStruct((8, 128), F32)),
        in_specs=[HBM] * (2 * nt) + dep_specs,
        out_specs=(SEM, SEM, *([HBM] * (2 * nt)), pl.BlockSpec(memory_space=pltpu.VMEM)),
        input_output_aliases={i: 2 + i for i in range(2 * nt)},
        compiler_params=pltpu.CompilerParams(has_side_effects=DATAFLOW))(
            *[pltpu.with_memory_space_constraint(s, pltpu.HBM) for s in srcs], *lands, *dep_args)
    return res[0], res[1], list(res[2:2 + nt]), list(res[2 + nt:2 + 2 * nt]), res[-1]


def _exchange_wait(name, kind, send_sems, recv_sems, srcs, lands, after):
    nt = len(srcs)

    def body(*refs):
        src_refs, land_refs = refs[:nt], refs[nt:2 * nt]
        s_sems, r_sems = refs[2 * nt], refs[2 * nt + 1]
        for cp in _split_copies(kind, src_refs, land_refs, s_sems, r_sems):
            if cp.is_remote:
                cp.wait_send()
                cp.wait_recv()
            else:
                cp.wait()

    res = pl.pallas_call(
        body, name=name,
        out_shape=tuple(pltpu.HBM(s.shape, s.dtype) for s in list(srcs) + list(lands)),
        in_specs=[HBM] * (2 * nt) + [SEM, SEM, pl.BlockSpec(memory_space=pl.ANY)],
        out_specs=tuple([HBM] * (2 * nt)),
        input_output_aliases={i: i for i in range(2 * nt)},
        compiler_params=pltpu.CompilerParams(has_side_effects=DATAFLOW))(
            *srcs, *lands, send_sems, recv_sems, after)
    return list(res[:nt]), list(res[nt:])


AG_GROUPS = ((0, 3), (3, 4), (7, 2))


def _ag_phase(name, own, land, sems, waits, starts, after=None):
    r = own.shape[0]
    half = r // 2
    ns = len(sems)
    dep_specs, dep_args = _dep_operand(after)
    nd = len(dep_args)
    new_group = None
    if starts:
        (new_group,) = [g for g, (first, n) in enumerate(AG_GROUPS) if first == starts[0]]
        assert list(starts) == list(range(AG_GROUPS[new_group][0], sum(AG_GROUPS[new_group])))

    def body(*refs):
        own_ref, land_ref = refs[0], refs[1]
        sem_refs = list(refs[2:2 + 2 * ns])
        outs = refs[2 + 2 * ns + nd:]
        if starts:
            sem_refs += [outs[0], outs[1]]
        x, y, c = _place()
        me, sib = (x, y, c), (x, y, 1 - c)
        xn, yn, dg = (1 - x, y, c), (x, 1 - y, c), (1 - x, 1 - y, c)

        def other(dev):
            return (dev[0], dev[1], 1 - dev[2])

        def rows(dev, part):
            blk = land_ref.at[4 * dev[0] + 2 * dev[1] + dev[2]]
            return blk if part is None else blk.at[pl.ds(part * half, half)]

        def sem_of(k):
            (g,) = [g for g, (first, n) in enumerate(AG_GROUPS) if first <= k < first + n]
            return sem_refs[2 * g].at[k - AG_GROUPS[g][0]], sem_refs[2 * g + 1].at[k - AG_GROUPS[g][0]]

        sent = {0: (me, None, sib), 1: (me, None, xn), 2: (me, None, yn), 3: (xn, 0, yn), 4: (yn, 1, xn),
                5: (xn, None, sib), 6: (yn, None, sib), 7: (dg, 0, sib), 8: (dg, 1, sib)}
        landed = {0: (sib, None), 1: (xn, None), 2: (yn, None), 3: (dg, 0), 4: (dg, 1), 5: (other(xn), None),
                  6: (other(yn), None), 7: (other(dg), 0), 8: (other(dg), 1)}

        def copy(k, receiving):
            send_sem, recv_sem = sem_of(k)
            dev, part, to = (*landed[k], me) if receiving else sent[k]
            src = own_ref if (dev is me and not receiving) else rows(dev, part)
            return pltpu.make_async_remote_copy(src_ref=src, dst_ref=rows(dev, part), send_sem=send_sem,
                                                recv_sem=recv_sem, device_id=to, device_id_type=MESH_ID)

        def own_copy():
            return pltpu.make_async_copy(own_ref, rows(me, None), sem_refs[1].at[AG_GROUPS[0][1]])

        for kind, k in waits:
            if kind == "recv":
                copy(k, True).wait_recv()
            elif kind == "own":
                own_copy().wait()
            else:
                copy(k, False).wait_send()
        for k in starts:
            copy(k, False).start()
        if new_group == 0:
            own_copy().start()
        if starts:
            outs[-1][...] = jnp.zeros_like(outs[-1])

    n_new = AG_GROUPS[new_group][1] if starts else 0
    sem_out = (pltpu.SemaphoreType.DMA((n_new + 1,)), pltpu.SemaphoreType.DMA((n_new + 1,))) if starts else ()
    token_out = (jax.ShapeDtypeStruct((8, 128), F32),) if starts else ()
    res = pl.pallas_call(
        body, name=name,
        out_shape=(*sem_out, pltpu.HBM(own.shape, own.dtype), pltpu.HBM(land.shape, land.dtype), *token_out),
        in_specs=[HBM, HBM] + [SEM] * (2 * ns) + dep_specs,
        out_specs=(*([SEM] * len(sem_out)), HBM, HBM, *([pl.BlockSpec(memory_space=pltpu.VMEM)] * len(token_out))),
        input_output_aliases={0: len(sem_out), 1: len(sem_out) + 1},
        compiler_params=pltpu.CompilerParams(has_side_effects=DATAFLOW))(
            own, land, *[a for pair in sems for a in pair], *dep_args)
    if starts:
        return (res[0], res[1]), res[2], res[3], res[4]
    return None, res[0], res[1], None


def _add_sibling(name, term, recv, rows):
    _, r, w = term.shape
    cidx = lax.axis_index("c").astype(jnp.int32).reshape(1)
    like_term = recv.shape[0] == N_DEV

    def body(c_ref, a_ref, b_ref, o_ref):
        o_ref[...] = (a_ref[...].astype(F32) + b_ref[...].astype(F32)).astype(o_ref.dtype)

    grid_spec = pltpu.PrefetchScalarGridSpec(
        num_scalar_prefetch=1, grid=(4, r // rows),
        in_specs=[pl.BlockSpec((None, rows, w), lambda q, i, c_ref: (2 * q + c_ref[0], i, 0)),
                  pl.BlockSpec((None, rows, w), lambda q, i, c_ref: (2 * q + c_ref[0] if like_term else q, i, 0))],
        out_specs=pl.BlockSpec((None, rows, w), lambda q, i, c_ref: (q, i, 0)))
    return pl.pallas_call(
        body, name=name, grid_spec=grid_spec, out_shape=jax.ShapeDtypeStruct((4, r, w), term.dtype),
        compiler_params=_params(("parallel", "parallel")))(cidx, term, recv)


def _add_sibling_small(name, terms, recvs):
    nt = len(terms)

    def body(*refs):
        c = lax.axis_index("c")
        for t_ref, r_ref, o_ref in zip(refs[:nt], refs[nt:2 * nt], refs[2 * nt:]):
            for q in range(4):
                o_ref[q] = (t_ref[2 * q + c].astype(F32) + r_ref[q].astype(F32)).astype(o_ref.dtype)

    return pl.pallas_call(
        body, name=name, out_shape=[jax.ShapeDtypeStruct((4,) + t.shape[1:], t.dtype) for t in terms],
        compiler_params=_params())(*terms, *recvs)


def _reduce_scatter_start(tag, terms, recv1):
    sums = _add_sibling_small("add_sibling_" + tag, terms, recv1)
    lands = [jax.ShapeDtypeStruct((3,) + s.shape[1:], s.dtype) for s in sums]
    send_sems, recv_sems, sums, lands, token = _exchange_start("exchange_chips_start_" + tag, "chips", sums, lands)
    return (tag, send_sems, recv_sems, sums, lands), token


def _reduce_scatter_wait(state, after):
    tag, send_sems, recv_sems, sums, lands = state
    return _exchange_wait("exchange_chips_wait_" + tag, "chips", send_sems, recv_sems, sums, lands, after)


def _adam_math(w, g, m, v):
    nm = ADAM_B1 * m + (1.0 - ADAM_B1) * g
    nv = ADAM_B2 * v + (1.0 - ADAM_B2) * (g * g)
    c1 = 1.0 - ADAM_B1 ** ADAM_STEP
    c2 = 1.0 - ADAM_B2 ** ADAM_STEP
    return -ADAM_LR * ((nm / c1) / (jnp.sqrt(nv / c2) + ADAM_EPS) + ADAM_WD * w), nm, nv


def _presum_halves(sums, landed):
    _, r, w = sums.shape
    rows = r // 2
    x, y = lax.axis_index("x"), lax.axis_index("y")
    dest = jnp.stack([2 * (1 - x) + y, 2 * x + 1 - y]).astype(jnp.int32)

    def body(q_ref, a_ref, b_ref, o_ref):
        o_ref[...] = (a_ref[...].astype(F32) + b_ref[...].astype(F32)).astype(o_ref.dtype)

    grid_spec = pltpu.PrefetchScalarGridSpec(
        num_scalar_prefetch=1, grid=(2,),
        in_specs=[pl.BlockSpec((None, rows, w), lambda h, q_ref: (q_ref[h], h, 0)),
                  pl.BlockSpec((None, rows, w), lambda h, q_ref: (1 + 2 * h, 0, 0))],
        out_specs=pl.BlockSpec((None, rows, w), lambda h, q_ref: (h, 0, 0)))
    return pl.pallas_call(
        body, name="presum_halves", grid_spec=grid_spec, out_shape=jax.ShapeDtypeStruct((2, r // 2, w), sums.dtype),
        compiler_params=_params(("parallel",)))(dest, sums, landed)


def _adamw_halves(name, sums, landed_a, landed_b, w, m, v, rows):
    r, c = w.shape
    half = c // 2
    qidx = (2 * lax.axis_index("x") + lax.axis_index("y")).astype(jnp.int32).reshape(1)

    def body(q_ref, s_ref, a_ref, b_ref, w_ref, m_ref, v_ref, g_ref, d_ref, nm_ref, nv_ref):
        first = (s_ref[:half, :].astype(F32) + a_ref[0].astype(F32)) + b_ref[0].astype(F32)
        second = (s_ref[half:, :].astype(F32) + a_ref[2].astype(F32)) + b_ref[1].astype(F32)
        g = jnp.concatenate([first, second], axis=0).T
        g_ref[...] = g
        d_ref[...], nm_ref[...], nv_ref[...] = _adam_math(w_ref[...], g, m_ref[...], v_ref[...])

    row = pl.BlockSpec((rows, c), lambda i, q_ref: (i, 0))
    grid_spec = pltpu.PrefetchScalarGridSpec(
        num_scalar_prefetch=1, grid=(r // rows,),
        in_specs=[pl.BlockSpec((None, c, rows), lambda i, q_ref: (q_ref[0], 0, i)),
                  pl.BlockSpec((4, half, rows), lambda i, q_ref: (0, 0, i)),
                  pl.BlockSpec((2, half, rows), lambda i, q_ref: (0, 0, i)), row, row, row],
        out_specs=[row] * 4)
    return pl.pallas_call(
        body, name=name, grid_spec=grid_spec, out_shape=[jax.ShapeDtypeStruct((r, c), F32)] * 4,
        compiler_params=_params(("parallel",)))(qidx, sums, landed_a, landed_b, w, m, v)


def _adamw_chips_small(name, items):
    n = len(items)

    def body(*refs):
        q = 2 * lax.axis_index("x") + lax.axis_index("y")
        ins, outs = refs[:5 * n], refs[5 * n:]
        for i, (_, _, w, _, _, transposed) in enumerate(items):
            s_ref, r_ref, w_ref, m_ref, v_ref = ins[5 * i:5 * i + 5]
            g_ref, d_ref, nm_ref, nv_ref = outs[4 * i:4 * i + 4]
            g = (s_ref[q].astype(F32) + r_ref[0].astype(F32)) + (r_ref[1].astype(F32) + r_ref[2].astype(F32))
            g = g.T if transposed else g[:w.shape[0]]
            g_ref[...] = g
            d_ref[...], nm_ref[...], nv_ref[...] = _adam_math(w_ref[...], g, m_ref[...], v_ref[...])

    res = pl.pallas_call(
        body, name=name, out_shape=[jax.ShapeDtypeStruct(it[2].shape, F32) for it in items for _ in range(4)],
        compiler_params=_params())(*[a for it in items for a in it[:5]])
    return [res[4 * i:4 * i + 4] for i in range(n)]


def _adamw_replicated(gathered, items):
    n = len(items)

    def body(g_ref, *refs):
        ins, t_ref, outs = refs[:3 * n], refs[3 * n], refs[3 * n + 1:]
        acc = g_ref[0]
        for j in range(1, N_DEV):
            acc = acc + g_ref[j]
        t_ref[...] = acc
        for i, (first, w, _, _) in enumerate(items):
            w_ref, m_ref, v_ref = ins[3 * i:3 * i + 3]
            g = t_ref[first:first + w.shape[0], :]
            outs[3 * i][...], outs[3 * i + 1][...], outs[3 * i + 2][...] = _adam_math(w_ref[...], g, m_ref[...], v_ref[...])

    res = pl.pallas_call(
        body, name="adamw_replicated",
        out_shape=[jax.ShapeDtypeStruct(gathered.shape[1:], F32)]
        + [jax.ShapeDtypeStruct(it[1].shape, F32) for it in items for _ in range(3)],
        compiler_params=_params())(gathered, *[a for it in items for a in it[1:]])
    return res[0], [res[1 + 3 * i:4 + 3 * i] for i in range(n)]


def _wide_rows(a):
    rows = -(-a.size // D_MODEL)
    return jnp.pad(a.reshape(-1), (0, rows * D_MODEL - a.size)).reshape(rows, D_MODEL)


def kernel(x, mem, pre_norm, w_in, merge_bias, na_rpb, mem_norm, w_mem_kv, w_branch_a, w_branch_b, w_branch_c, w_out, post_norm, loss_target, m_pre_norm, m_w_in, m_merge_bias, m_na_rpb, m_mem_norm, m_w_mem_kv, m_w_branch_a, m_w_branch_b, m_w_branch_c, m_w_out, m_post_norm, v_pre_norm, v_w_in, v_merge_bias, v_na_rpb, v_mem_norm, v_w_mem_kv, v_w_branch_a, v_w_branch_b, v_w_branch_c, v_w_out, v_post_norm):
    wt_in_s = w_in[0].T.astype(BF16)
    rows_s = jnp.concatenate([w_mem_kv[0], w_out[0]], axis=0).astype(BF16)
    cols_s = jnp.concatenate([w_branch_a[0].T, w_branch_b[0].T, w_branch_c[0].T], axis=0).astype(BF16)
    mb_s = jnp.pad(merge_bias[0], ((0, 5), (0, 0)))

    chip = 2 * lax.axis_index("x") + lax.axis_index("y")

    def first_block(q):
        return jnp.where(q == 0, 0, jnp.where(q == 1, 6, jnp.where(q == 2, 11, 17)))

    five = jnp.arange(5, dtype=jnp.int32)
    near, far = jnp.where(chip < 2, 5, 16), jnp.where(chip < 2, 16, 5)
    order1 = (first_block(chip) + five).astype(jnp.int32)
    order2 = jnp.concatenate([first_block(chip ^ 1) + five, near[None], first_block(chip ^ 2) + five]).astype(jnp.int32)
    order3 = jnp.concatenate([first_block(chip ^ 3) + five, far[None]]).astype(jnp.int32)
    tabs = _rope_tables()

    def weights_of(land):
        return land.reshape(N_IN, D_MODEL)

    land = pltpu.with_memory_space_constraint(lax.empty((N_DEV,) + wt_in_s.shape, BF16), pltpu.HBM)
    own = pltpu.with_memory_space_constraint(wt_in_s, pltpu.HBM)
    sem_a, own, land, token = _ag_phase("ag_start", own, land, [], [], [0, 1, 2])
    hs, hst = _prenorm_fold(x[0], pre_norm, token)
    _, own, land, _ = _ag_phase("ag_wait0", own, land, [sem_a], [("recv", 0), ("own", 0)], [], hs)
    parts = _in_proj("in_proj_1", hs, weights_of(land), tabs, order1)
    bias = _na_bias(jnp.pad(na_rpb[0], ((0, 0), (0, 1), (0, 128 - 31))), parts)
    sem_b, own, land, _ = _ag_phase("ag_mid1", own, land, [sem_a], [("recv", 1), ("recv", 2)], [3, 4, 5, 6], bias)
    _, own, land, _ = _ag_phase("ag_wait1", own, land, [sem_a, sem_b], [("recv", 5), ("recv", 6)], [])
    parts = _in_proj("in_proj_2", hs, weights_of(land), tabs, order2, parts)
    sem_c, own, land, _ = _ag_phase("ag_mid2", own, land, [sem_a, sem_b], [("recv", 3), ("recv", 4)], [7, 8], parts)
    _, own, land, _ = _ag_phase("ag_end", own, land, [sem_a, sem_b, sem_c],
                                [("recv", 7), ("recv", 8)] + [("send", k) for k in range(9)], [])
    wt_in = weights_of(land)

    late_own = [rows_s, cols_s, mb_s]
    late_lands = [jax.ShapeDtypeStruct((N_DEV,) + s.shape, s.dtype) for s in late_own]
    l_send, l_recv, late_own, late_lands, late_token = _exchange_start("gather_late_start", "gather", late_own,
                                                                       late_lands, after=wt_in)
    parts = _in_proj("in_proj_3", hs, wt_in, tabs, order3, parts, late_token)

    def late_weights(after):
        _, lands = _exchange_wait("gather_late_wait", "gather", l_send, l_recv, late_own, late_lands, after)
        g_rows, g_cols, g_mb = lands
        return (g_mb[:, :3].transpose(1, 0, 2).reshape(3, D_MODEL), g_rows[:, :128].reshape(D_MODEL, D_MODEL),
                g_cols, g_rows)

    rest_state, rest_sibling, w_in_a, w_in_b = [], [], [], []

    def reduce_start(phase, grads, after=None):
        if phase == "rest_sibling":
            gmb_t = jnp.pad(grads["merge_bias"].reshape(3, N_DEV, 128).transpose(1, 0, 2), ((0, 0), (0, 5), (0, 0)))
            terms = [grads["w_kv"].reshape(N_DEV, 128, D_MODEL), grads["w_out"].reshape(N_DEV, 128, D_MODEL),
                     grads["wt_a"].reshape(N_DEV, 128, 512), grads["wt_b"].reshape(N_DEV, 128, 512),
                     grads["wt_c"].reshape(N_DEV, 128, 512), gmb_t]
            lands = [jax.ShapeDtypeStruct((4,) + t.shape[1:], t.dtype) for t in terms]
            started = _exchange_start("exchange_sibling_start_rest", "sibling", terms, lands)
            rest_sibling.extend(started[:4])
            return started[4]
        if phase == "rest_chips":
            s_send, s_recv, terms, lands = rest_sibling
            terms, recv1 = _exchange_wait("exchange_sibling_wait_rest", "sibling", s_send, s_recv, terms, lands, after)
            state, token = _reduce_scatter_start("rest", terms, recv1)
            rest_state.append(state)
            return token
        if phase == "w_in":
            own, sibling = [a.reshape(N_DEV, SHARD_IN, D_MODEL) for a in grads["wt_in"]]
            sums = _add_sibling("add_sibling_w_in", own, sibling, SHARD_IN)
            lands = [jax.ShapeDtypeStruct((4, SHARD_IN // 2, D_MODEL), BF16)]
            w_in_a.extend(_exchange_start("rs_a_start", "rs_a", [sums], lands))
            return w_in_a[4]
        (sums,), (landed_a,) = _exchange_wait("rs_a_wait", "rs_a", w_in_a[0], w_in_a[1], w_in_a[2], w_in_a[3], after)
        lands = [jax.ShapeDtypeStruct((2, SHARD_IN // 2, D_MODEL), BF16)]
        w_in_b.extend(_exchange_start("rs_b_start", "rs_b", [_presum_halves(sums, landed_a)], lands))
        w_in_b.extend([sums, landed_a])
        return w_in_b[4]

    loss_term, grad_x, grads = _local_step(
        x[0], hst, parts, tabs, bias, mem[0], loss_target[0], pre_norm, mem_norm, post_norm, wt_in, late_weights,
        reduce_start=reduce_start)

    replicated = ("pre_norm", "mem_norm", "post_norm", "na_rpb")
    pieces = [_wide_rows(grads[n]) for n in replicated] + [_wide_rows(loss_term)]
    first_rows = [sum(p.shape[0] for p in pieces[:i]) for i in range(len(pieces))]
    small = jnp.concatenate(pieces, axis=0)
    s_send, s_recv, s_own, s_land, s_token = _exchange_start(
        "gather_small_start", "gather", [small], [jax.ShapeDtypeStruct((N_DEV,) + small.shape, F32)])
    grad = {}
    weights = {
        "pre_norm": (pre_norm, m_pre_norm, v_pre_norm), "w_in": (w_in, m_w_in, v_w_in),
        "merge_bias": (merge_bias, m_merge_bias, v_merge_bias), "na_rpb": (na_rpb, m_na_rpb, v_na_rpb),
        "mem_norm": (mem_norm, m_mem_norm, v_mem_norm), "w_mem_kv": (w_mem_kv, m_w_mem_kv, v_w_mem_kv),
        "w_branch_a": (w_branch_a, m_w_branch_a, v_w_branch_a), "w_branch_b": (w_branch_b, m_w_branch_b, v_w_branch_b),
        "w_branch_c": (w_branch_c, m_w_branch_c, v_w_branch_c), "w_out": (w_out, m_w_out, v_w_out),
        "post_norm": (post_norm, m_post_norm, v_post_norm)}
    order = ["pre_norm", "w_in", "merge_bias", "na_rpb", "mem_norm", "w_mem_kv", "w_branch_a", "w_branch_b",
             "w_branch_c", "w_out", "post_norm"]
    delta, new_m, new_v = {}, {}, {}

    sums, recv2 = _reduce_scatter_wait(rest_state[0], s_token)
    rest = (("w_mem_kv", False), ("w_out", False), ("w_branch_a", True), ("w_branch_b", True), ("w_branch_c", True),
            ("merge_bias", False))
    items = [(sums[i], recv2[i], *[a[0] for a in weights[n]], transposed) for i, (n, transposed) in enumerate(rest)]
    for (n, _), (g, dl, nm, nv) in zip(rest, _adamw_chips_small("adamw_rest", items)):
        grad[n], delta[n], new_m[n], new_v[n] = g[None], dl[None], nm[None], nv[None]
    s_own, s_land = _exchange_wait("gather_small_wait", "gather", s_send, s_recv, s_own, s_land, delta["w_out"])
    items = [(first, *[_wide_rows(a) for a in weights[n]]) for n, first in zip(replicated, first_rows)]
    total, updates = _adamw_replicated(s_land[0], items)
    loss = total[first_rows[-1], 0]
    for n, first, it, (dl, nm, nv) in zip(replicated, first_rows, items, updates):
        w = weights[n][0]
        grad[n], delta[n], new_m[n], new_v[n] = [
            a.reshape(-1)[:w.size].reshape(w.shape) for a in (total[first:first + it[1].shape[0]], dl, nm, nv)]
    _, (landed_b,) = _exchange_wait("rs_b_wait", "rs_b", w_in_b[0], w_in_b[1], w_in_b[2], w_in_b[3], updates[-1][0])
    g, dl, nm, nv = _adamw_halves("adamw_w_in", w_in_b[5], w_in_b[6], landed_b, w_in[0], m_w_in[0], v_w_in[0], 256)
    grad["w_in"], delta["w_in"], new_m["w_in"], new_v["w_in"] = g[None], dl[None], nm[None], nv[None]

    return (loss, grad_x[None], *[grad[n] for n in order], *[delta[n] for n in order],
            *[new_m[n] for n in order], *[new_v[n] for n in order])
```

```python
import functools

import numpy as np
import jax
import jax.numpy as jnp
from jax import lax
from jax.experimental import pallas as pl
from jax.experimental.pallas import tpu as pltpu

F32 = jnp.float32
BF16 = jnp.bfloat16

SEQ = 2048
D_MODEL = 1024
N_IN = 11264
N_DEV = 8
SHARD_IN = N_IN // N_DEV
HEAD_DIM = 64
GRID_W = 64
NA_ROWS = 8
MEM_LEN = 256
DILATIONS = (1, 4, 16)
REACH = 64
ROPE_THETA = 500000.0
ROPE_DIM = 16
EPS = 1e-6
NEG = -1e30
ADAM_LR = 0.001
ADAM_B1 = 0.9
ADAM_B2 = 0.999
ADAM_EPS = 1e-08
ADAM_WD = 0.01
ADAM_STEP = 10

VMEM_LIMIT_BYTES = 56 * 1024 * 1024
MESH_ID = pl.DeviceIdType.MESH

NN = (((1,), (0,)), ((), ()))
NT = (((1,), (1,)), ((), ()))
TN = (((0,), (0,)), ((), ()))


def _params(sem=None):
    return pltpu.CompilerParams(dimension_semantics=sem, vmem_limit_bytes=VMEM_LIMIT_BYTES)


def _iota(shape, dim):
    return lax.broadcasted_iota(jnp.int32, shape, dim)


def _sigmoid(x):
    return 1.0 / (1.0 + jnp.exp(-x))


def _rope_tables():
    half = ROPE_DIM // 2
    inv = (ROPE_THETA ** (-np.arange(half, dtype=np.float64) * 2.0 / ROPE_DIM)).astype(np.float32)
    pos = np.arange(SEQ, dtype=np.float32)
    ang = pos[:, None] * inv[None, :]
    cos, sin = np.cos(ang), np.sin(ang)
    zeros = np.zeros_like(cos)
    rest = HEAD_DIM - ROPE_DIM
    c64 = np.concatenate([cos, cos, np.ones((SEQ, rest), np.float32)], axis=1)
    s1 = np.concatenate([zeros, sin, np.zeros((SEQ, rest), np.float32)], axis=1)
    s2 = np.concatenate([-sin, zeros, np.zeros((SEQ, rest), np.float32)], axis=1)

    def fold(t, d):
        return t.reshape(SEQ // d, d, t.shape[1]).transpose(1, 0, 2).reshape(SEQ, t.shape[1])

    tabs = [np.stack([np.tile(fold(t, d), (1, 2)) for t in (c64, s1, s2)], axis=0) for d in DILATIONS]
    return jnp.asarray(np.stack(tabs, axis=0), dtype=F32)


def _rope(a, c, s1, s2):
    return a * c + pltpu.roll(a, 8, 1) * s1 + pltpu.roll(a, 120, 1) * s2


def _rope_t(a, c, s1, s2):
    return a * c + pltpu.roll(a * s1, 120, 1) + pltpu.roll(a * s2, 8, 1)


def _perm_of_block(j):
    return jnp.where(j < 3, 0, jnp.where(j < 6, 1, jnp.where(j < 9, 2, 0)))


def _mm(name, a, b, out_shape, out_dtype, grid, a_spec, b_spec, o_spec, acc_shape, dims, k_axis, nk):
    def body(a_ref, b_ref, o_ref, acc_ref):
        k = pl.program_id(k_axis)

        @pl.when(k == 0)
        def _():
            acc_ref[...] = jnp.zeros(acc_shape, F32)

        acc_ref[...] += lax.dot_general(a_ref[...], b_ref[...], dims, preferred_element_type=F32)

        @pl.when(k == nk - 1)
        def _():
            o_ref[...] = acc_ref[...].astype(out_dtype)

    sem = tuple("arbitrary" if ax == k_axis else "parallel" for ax in range(len(grid)))
    return pl.pallas_call(
        body, name=name, grid=grid, in_specs=[a_spec, b_spec], out_specs=o_spec,
        out_shape=jax.ShapeDtypeStruct(out_shape, out_dtype),
        scratch_shapes=[pltpu.VMEM(acc_shape, F32)], compiler_params=_params(sem))(a, b)


def _mm_simple(name, a, b, dims, out_dtype, tm, tn, tk):
    if dims is NN:
        m, kk = a.shape
        n = b.shape[1]
        a_spec = pl.BlockSpec((tm, tk), lambda i, j, k: (i, k))
        b_spec = pl.BlockSpec((tk, tn), lambda i, j, k: (k, j))
    elif dims is NT:
        m, kk = a.shape
        n = b.shape[0]
        a_spec = pl.BlockSpec((tm, tk), lambda i, j, k: (i, k))
        b_spec = pl.BlockSpec((tn, tk), lambda i, j, k: (j, k))
    else:
        kk, m = a.shape
        n = b.shape[1]
        a_spec = pl.BlockSpec((tk, tm), lambda i, j, k: (k, i))
        b_spec = pl.BlockSpec((tk, tn), lambda i, j, k: (k, j))
    grid = (m // tm, n // tn, kk // tk)
    o_spec = pl.BlockSpec((tm, tn), lambda i, j, k: (i, j))
    return _mm(name, a, b, (m, n), out_dtype, grid, a_spec, b_spec, o_spec, (tm, tn), dims, 2, kk // tk)


def _rmsnorm_fwd(name, x, gain, rows):
    n, d = x.shape

    def body(x_ref, g_ref, o_ref):
        xv = x_ref[...]
        rstd = lax.rsqrt(jnp.mean(xv * xv, axis=1, keepdims=True) + EPS)
        o_ref[...] = (xv * rstd * g_ref[...]).astype(BF16)

    return pl.pallas_call(
        body, name=name, grid=(n // rows,),
        in_specs=[pl.BlockSpec((rows, d), lambda i: (i, 0)), pl.BlockSpec((1, d), lambda i: (0, 0))],
        out_specs=pl.BlockSpec((rows, d), lambda i: (i, 0)),
        out_shape=jax.ShapeDtypeStruct((n, d), BF16), compiler_params=_params(("parallel",)))(x, gain)


def _folded_rows(first, rows, d):
    if d == 1:
        return pl.ds(pl.multiple_of(first, rows), rows)
    mlen = SEQ // d
    return pl.ds((first % mlen) * d + first // mlen, rows, stride=d)


def _prenorm_fold(x, gain, dep=None):
    rows = 128
    nchunk = D_MODEL // 128
    dep_specs, dep_args = _dep_operand(dep)

    def body(*refs):
        x_refs, g_ref, hs_ref, hst_ref = refs[:nchunk], refs[nchunk], refs[-2], refs[-1]
        first = pl.program_id(0) * rows
        for p, d in enumerate(DILATIONS):
            idx = _folded_rows(first, rows, d)
            xv = jnp.concatenate([r[idx, :] for r in x_refs], axis=1)
            rstd = lax.rsqrt(jnp.mean(xv * xv, axis=1, keepdims=True) + EPS)
            h = xv * rstd * g_ref[...]
            hs_ref[p] = h.astype(BF16)
            hst_ref[p] = h.T.astype(BF16)

    x_specs = [pl.BlockSpec((SEQ, 128), functools.partial(lambda c, i: (0, c), c)) for c in range(nchunk)]
    return pl.pallas_call(
        body, name="prenorm", grid=(SEQ // rows,),
        in_specs=x_specs + [pl.BlockSpec((1, D_MODEL), lambda i: (0, 0))] + dep_specs,
        out_specs=[pl.BlockSpec((3, rows, D_MODEL), lambda i: (0, i, 0)),
                   pl.BlockSpec((3, D_MODEL, rows), lambda i: (0, 0, i))],
        out_shape=[jax.ShapeDtypeStruct((3, SEQ, D_MODEL), BF16), jax.ShapeDtypeStruct((3, D_MODEL, SEQ), BF16)],
        compiler_params=_params(("parallel",)))(*([x] * nchunk), gain, *dep_args)


def _prenorm_bwd(x, gain, dh, dout):
    rows = 512

    def body(x_ref, g_ref, a_ref, do_ref, dx_ref, gg_ref):
        xv = x_ref[...]
        rstd = lax.rsqrt(jnp.mean(xv * xv, axis=1, keepdims=True) + EPS)
        xn = xv * rstd
        dh = jnp.concatenate([a_ref[c] for c in range(D_MODEL // 128)], axis=1)
        gdh = dh * g_ref[...]
        dx_ref[...] = rstd * (gdh - xn * jnp.mean(gdh * xn, axis=1, keepdims=True)) + do_ref[...]

        @pl.when(pl.program_id(0) == 0)
        def _():
            gg_ref[...] = jnp.zeros((1, D_MODEL), F32)

        gg_ref[...] += jnp.sum(dh * xn, axis=0, keepdims=True)

    row = pl.BlockSpec((rows, D_MODEL), lambda i: (i, 0))
    vec = pl.BlockSpec((1, D_MODEL), lambda i: (0, 0))
    return pl.pallas_call(
        body, name="prenorm_bwd", grid=(SEQ // rows,),
        in_specs=[row, vec, pl.BlockSpec((D_MODEL // 128, rows, 128), lambda i: (0, i, 0)), row], out_specs=[row, vec],
        out_shape=[jax.ShapeDtypeStruct((SEQ, D_MODEL), F32), jax.ShapeDtypeStruct((1, D_MODEL), F32)],
        compiler_params=_params(("arbitrary",)))(x, gain, dh, dout)


def _memnorm_bwd(mem, dmemn, dep=None):
    dep_specs, dep_args = _dep_operand(dep)

    def body(m_ref, d_ref, *rest):
        mv = m_ref[...]
        rstd = lax.rsqrt(jnp.mean(mv * mv, axis=1, keepdims=True) + EPS)
        rest[-1][...] = jnp.sum(d_ref[...] * mv * rstd, axis=0, keepdims=True)

    whole = pl.BlockSpec(memory_space=pltpu.VMEM)
    return pl.pallas_call(
        body, name="memnorm_bwd", in_specs=[whole, whole] + dep_specs,
        out_shape=jax.ShapeDtypeStruct((1, D_MODEL), F32), compiler_params=_params())(mem, dmemn, *dep_args)


def _dep_operand(dep):
    return ([], []) if dep is None else ([pl.BlockSpec(memory_space=pl.ANY)], [dep])


def _in_proj(name, hs, wt, tabs, order, prev=None, dep=None):
    tm, tn = 512, 512
    prev_specs, prev_args = ([], []) if prev is None else ([ANY], [prev])
    dep_specs, dep_args = _dep_operand(dep)

    def body(order_ref, h_ref, w_ref, t_ref, *rest):
        o_ref = rest[-1]
        j = order_ref[pl.program_id(0)]
        is_rope = jnp.logical_and(j < 9, j % 3 != 2)
        row_slices = [slice(r * tm, (r + 1) * tm) for r in range(SEQ // tm)]

        def product(rs):
            return lax.dot_general(h_ref[rs, :], w_ref[...], NT, preferred_element_type=F32)

        @pl.when(is_rope)
        def _():
            for rs in row_slices:
                acc = product(rs)
                c, s1, s2 = t_ref[0, rs, :], t_ref[1, rs, :], t_ref[2, rs, :]
                for q in range(tn // 128):
                    a = acc[:, q * 128:(q + 1) * 128]
                    o_ref[rs, q * 128:(q + 1) * 128] = _rope(a, c, s1, s2).astype(BF16)

        @pl.when(jnp.logical_not(is_rope))
        def _():
            for rs in row_slices:
                o_ref[rs, :] = product(rs).astype(BF16)

    grid_spec = pltpu.PrefetchScalarGridSpec(
        num_scalar_prefetch=1, grid=(order.shape[0],),
        in_specs=[pl.BlockSpec((None, SEQ, D_MODEL), lambda t, o: (_perm_of_block(o[t]), 0, 0)),
                  pl.BlockSpec((tn, D_MODEL), lambda t, o: (o[t], 0)),
                  pl.BlockSpec((None, 3, SEQ, 128), lambda t, o: (_perm_of_block(o[t]), 0, 0, 0))] + prev_specs
        + dep_specs,
        out_specs=pl.BlockSpec((SEQ, tn), lambda t, o: (0, o[t])))
    return pl.pallas_call(
        body, name=name, grid_spec=grid_spec, out_shape=jax.ShapeDtypeStruct((SEQ, N_IN), BF16),
        input_output_aliases={} if prev is None else {4: 0},
        compiler_params=_params(("arbitrary",)))(order, hs, wt, tabs, *prev_args, *dep_args)


def _piece_blocks(pieces):
    return [(a, h * 512) for a, p in enumerate(pieces) for h in range(p.shape[1] // 512)]


def _block_fetch(piece_refs, blocks, buf, sem):
    def start(block, slot):
        for b, (a, col) in enumerate(blocks):
            @pl.when(block == b)
            def _():
                pltpu.make_async_copy(piece_refs[a].at[:, pl.ds(col, 512)], buf.at[slot], sem.at[slot]).start()

    def wait(slot):
        pltpu.make_async_copy(piece_refs[0].at[:, pl.ds(0, 512)], buf.at[slot], sem.at[slot]).wait()

    return start, wait


def _in_proj_dw(pieces, hst, dep=None):
    tn = 512
    blocks = _piece_blocks(pieces)
    nblk = len(blocks)
    npc = len(pieces)
    dep_specs, dep_args = _dep_operand(dep)

    def body(h_ref, *rest):
        piece_refs = rest[:npc]
        own_out, mirror, buf, sem, out_buf, send_sems, recv_sem, local_sems = rest[-8:]
        j = pl.program_id(0)
        slot = j % 2
        start, wait = _block_fetch(piece_refs, blocks, buf, sem)
        x, y, c = _place()

        def rows_of(step):
            return pl.ds(pl.multiple_of(step * tn, tn), tn)

        def to_sibling(step, slot_):
            return pltpu.make_async_remote_copy(
                src_ref=out_buf.at[slot_], dst_ref=mirror.at[rows_of(step)],
                send_sem=send_sems.at[slot_], recv_sem=recv_sem, device_id=(x, y, 1 - c), device_id_type=MESH_ID)

        def to_own(step, slot_):
            return pltpu.make_async_copy(out_buf.at[slot_], own_out.at[rows_of(step)], local_sems.at[slot_])

        @pl.when(j == 0)
        def _():
            start(j, slot)

        wait(slot)

        @pl.when(j + 1 < nblk)
        def _():
            start(j + 1, 1 - slot)

        acc = jnp.dot(h_ref[...], buf[slot], preferred_element_type=F32)

        @pl.when(j >= 2)
        def _():
            to_sibling(j - 2, slot).wait_send()
            to_own(j - 2, slot).wait()

        out_buf[slot] = acc.T.astype(BF16)
        to_sibling(j, slot).start()
        to_own(j, slot).start()

        @pl.when(j == nblk - 1)
        def _():
            to_sibling(j - 1, 1 - slot).wait_send()
            to_own(j - 1, 1 - slot).wait()
            to_sibling(j, slot).wait_send()
            to_own(j, slot).wait()
            pltpu.make_async_remote_copy(src_ref=mirror, dst_ref=mirror, send_sem=send_sems.at[0], recv_sem=recv_sem,
                                         device_id=(x, y, 1 - c), device_id_type=MESH_ID).wait_recv()

    return pl.pallas_call(
        body, name="in_proj_dw", grid=(nblk,),
        in_specs=[pl.BlockSpec((None, D_MODEL, SEQ), lambda j: (_perm_of_block(j), 0, 0))] + [ANY] * npc + dep_specs,
        out_specs=[ANY, ANY],
        out_shape=[jax.ShapeDtypeStruct((N_IN, D_MODEL), BF16), jax.ShapeDtypeStruct((N_IN, D_MODEL), BF16)],
        scratch_shapes=[pltpu.VMEM((2, SEQ, tn), BF16), pltpu.SemaphoreType.DMA((2,)),
                        pltpu.VMEM((2, tn, D_MODEL), BF16), pltpu.SemaphoreType.DMA((2,)), pltpu.SemaphoreType.DMA,
                        pltpu.SemaphoreType.DMA((2,))],
        compiler_params=_params(("arbitrary",)))(hst, *pieces, *dep_args)


def _in_proj_dh(pieces, wt, dep=None):
    tk = 512
    blocks = _piece_blocks(pieces)
    nblk = len(blocks)
    npc = len(pieces)
    nchunk = D_MODEL // 128

    def col(s):
        return jnp.where(s < 3, s, jnp.where(s < 16, s + 6, s - 13))

    dep_specs, dep_args = _dep_operand(dep)

    def body(w_ref, *rest):
        piece_refs = rest[:npc]
        o_ref, acc_ref, buf, sem = rest[-4:]
        s = pl.program_id(0)
        slot = s % 2
        start, wait = _block_fetch(piece_refs, blocks, buf, sem)

        @pl.when(s == 0)
        def _():
            start(col(s), slot)

        wait(slot)

        @pl.when(s + 1 < nblk)
        def _():
            start(col(s + 1), 1 - slot)

        row_slices = [slice(r * 512, (r + 1) * 512) for r in range(SEQ // 512)]

        def product(rs):
            return jnp.dot(buf[slot, rs, :], w_ref[...], preferred_element_type=F32)

        def accumulate(cond, to_out, init):
            @pl.when(cond)
            def _():
                for rs in row_slices:
                    prod = product(rs)
                    if not to_out:
                        if init:
                            acc_ref[rs, :] = prod
                        else:
                            acc_ref[rs, :] += prod
                        continue
                    for c in range(nchunk):
                        if init:
                            o_ref[c, rs, :] = prod[:, c * 128:(c + 1) * 128]
                        else:
                            o_ref[c, rs, :] += prod[:, c * 128:(c + 1) * 128]

        accumulate(s == 0, True, True)
        accumulate(jnp.logical_and(s > 0, s < 16), True, False)
        accumulate(jnp.logical_or(s == 16, s == 19), False, True)
        accumulate(jnp.logical_and(s > 16, s != 19), False, False)
        for last, d in ((18, 4), (21, 16)):
            @pl.when(s == last)
            def _():
                mlen = SEQ // d
                for r in range(d):
                    for c in range(nchunk):
                        o_ref[c, pl.ds(r, mlen, stride=d), :] += acc_ref[r * mlen:(r + 1) * mlen,
                                                                         c * 128:(c + 1) * 128]

    return pl.pallas_call(
        body, name="in_proj_dh", grid=(nblk,),
        in_specs=[pl.BlockSpec((tk, D_MODEL), lambda s: (col(s), 0))] + [ANY] * npc + dep_specs,
        out_specs=pl.BlockSpec((nchunk, SEQ, 128), lambda s: (0, 0, 0)),
        out_shape=jax.ShapeDtypeStruct((nchunk, SEQ, 128), F32),
        scratch_shapes=[pltpu.VMEM((SEQ, D_MODEL), F32), pltpu.VMEM((2, SEQ, tk), BF16),
                        pltpu.SemaphoreType.DMA((2,))],
        compiler_params=_params(("arbitrary",)))(wt, *pieces, *dep_args)


def _head_lanes(lanes, hh):
    return lanes >= 64 if hh == 1 else lanes < 64


def _head_rows(x, lanes, hh, pair):
    if not pair:
        return jnp.max(x, axis=1, keepdims=True)
    return jnp.max(jnp.where(_head_lanes(lanes, hh), x, -jnp.inf), axis=1, keepdims=True)


def _mask_head(x, lanes, hh, pair, scale=1.0):
    if not pair:
        return x
    xf = x.astype(F32) if scale == 1.0 else x.astype(F32) * scale
    return jnp.where(_head_lanes(lanes, hh), xf, 0.0).astype(BF16)


def _window(mode, qi, tq, mlen, tk):
    if mode == "dil":
        q0 = qi * tq
        seg = (q0 // mlen) * mlen
        ks = jnp.clip(q0 - REACH, seg, seg + mlen - tk)
        return pl.multiple_of(ks, 64)
    if mode == "na":
        r_start = jnp.clip(qi - NA_ROWS // 2, 0, SEQ // GRID_W - NA_ROWS)
        return pl.multiple_of(r_start * GRID_W, 64)
    return 0


def _band_mask(qi, tq, tk, ks):
    qpos = qi * tq + _iota((tq, tk), 0)
    kpos = ks + _iota((tq, tk), 1)
    return jnp.where(jnp.abs(qpos - kpos) <= REACH, 0.0, NEG).astype(F32)


def _stack_heads(x, lanes, pair, scale=1.0):
    if not pair:
        return x
    return jnp.concatenate([_mask_head(x, lanes, hh, pair, scale) for hh in range(2)], axis=0)


def _stack_rows(x, lanes, pair):
    if not pair:
        return _head_rows(x, lanes, 0, pair)
    return jnp.concatenate([_head_rows(x, lanes, hh, pair) for hh in range(2)], axis=0)


def _unstack_heads(x, lanes, pair, tq):
    if not pair:
        return x
    return jnp.where(lanes < 64, x[:tq], x[tq:])


def _scores(mode, qst, k, sscale, band, qi, bias_ref, pair):
    s = lax.dot_general(qst, k, NT, preferred_element_type=F32)
    if sscale != 1.0:
        s = s * sscale
    if mode == "dil":
        s = s + jnp.concatenate([band, band], axis=0)
    elif mode == "na":
        off = qi - jnp.clip(qi - NA_ROWS // 2, 0, SEQ // GRID_W - NA_ROWS)
        s = s + jnp.concatenate([bias_ref[0, off], bias_ref[1, off]], axis=0)
    return s


def _attn_cfg(mode, d):
    if mode == "dil":
        mlen = SEQ // d
        return dict(pair=True, tq=128, tk=min(256, mlen), mlen=mlen, lk=SEQ, scale=HEAD_DIM ** -0.5, units=4,
                    nsub=ATTN_SUBTILES)
    if mode == "na":
        return dict(pair=True, tq=GRID_W, tk=NA_ROWS * GRID_W, mlen=SEQ, lk=SEQ, scale=HEAD_DIM ** -0.5, units=4,
                    nsub=2 * ATTN_SUBTILES)
    return dict(pair=False, tq=128, tk=MEM_LEN, mlen=SEQ, lk=MEM_LEN, scale=128 ** -0.5, units=4,
                nsub=ATTN_SUBTILES)


ATTN_SUBTILES = 16


def _attn_fwd(name, mode, q_arr, k_arr, v_arr, qcol, kcol, vcol, d=1, bias=None):
    cfg = _attn_cfg(mode, d)
    pair, tq, tk, mlen, lk, scale = cfg["pair"], cfg["tq"], cfg["tk"], cfg["mlen"], cfg["lk"], cfg["scale"]
    qscale, sscale = (scale, 1.0) if pair else (1.0, scale)
    nsub = cfg["nsub"]
    rows = nsub * tq

    def body(*refs):
        if mode == "na":
            q_ref, k_ref, v_ref, bias_ref, o_ref, l_ref = refs
        else:
            q_ref, k_ref, v_ref, o_ref, l_ref = refs
            bias_ref = None
        lanes = _iota((tq, 128), 1)
        qis = [pl.program_id(1) * nsub + sub for sub in range(nsub)]
        kss = [_window(mode, qi, tq, mlen, tk) for qi in qis]
        vs = [v_ref[pl.ds(ks, tk), :] for ks in kss]
        bands = [_band_mask(qi, tq, tk, ks) if mode == "dil" else None for qi, ks in zip(qis, kss)]
        ss = []
        for sub in range(nsub):
            qst = _stack_heads(q_ref[sub * tq:(sub + 1) * tq, :], lanes, pair, qscale)
            k = k_ref[pl.ds(kss[sub], tk), :]
            ss.append(_scores(mode, qst, k, sscale, bands[sub], qis[sub], bias_ref, pair))
        ms = [jnp.max(s_, axis=1, keepdims=True) for s_ in ss]
        ps = [jnp.exp(s_ - m) for s_, m in zip(ss, ms)]
        ls = [jnp.sum(p, axis=1, keepdims=True) for p in ps]
        os_ = [jnp.dot(p.astype(BF16), v, preferred_element_type=F32) for p, v in zip(ps, vs)]
        for sub in range(nsub):
            out = _unstack_heads(os_[sub] / ls[sub], lanes, pair, tq)
            lse = ms[sub] + jnp.log(ls[sub])
            lse = _unstack_heads(jnp.broadcast_to(lse, (lse.shape[0], 128)), lanes, pair, tq)
            dst = _folded_rows(qis[sub] * tq, tq, d) if mode == "dil" else slice(sub * tq, (sub + 1) * tq)
            o_ref[dst, :] = out
            l_ref[dst, :] = lse

    in_specs = [pl.BlockSpec((rows, 128), lambda u, i: (i, qcol + u)),
                pl.BlockSpec((lk, 128), lambda u, i: (0, kcol + u)),
                pl.BlockSpec((lk, 128), lambda u, i: (0, vcol + u))]
    args = [q_arr, k_arr, v_arr]
    if mode == "na":
        in_specs.append(pl.BlockSpec((2, NA_ROWS, GRID_W, NA_ROWS * GRID_W), lambda u, i: (u, 0, 0, 0)))
        args.append(bias)
    if mode == "dil":
        out_spec = pl.BlockSpec((SEQ, 128), lambda u, i: (0, u))
    else:
        out_spec = pl.BlockSpec((rows, 128), lambda u, i: (i, u))
    return pl.pallas_call(
        body, name=name, grid=(cfg["units"], SEQ // rows), in_specs=in_specs, out_specs=[out_spec, out_spec],
        out_shape=[jax.ShapeDtypeStruct((SEQ, 512), F32), jax.ShapeDtypeStruct((SEQ, 512), F32)],
        compiler_params=_params(("parallel", "arbitrary")))(*args)


def _attn_bwd(name, mode, q_arr, k_arr, v_arr, qcol, kcol, vcol, do, lse, dp=None, o=None, d=1, bias=None,
              tabs=None, dep=None):
    cfg = _attn_cfg(mode, d)
    pair, tq, tk, mlen, lk, scale = cfg["pair"], cfg["tq"], cfg["tk"], cfg["mlen"], cfg["lk"], cfg["scale"]
    qscale, sscale = (scale, 1.0) if pair else (1.0, scale)
    nsub = cfg["nsub"]
    rows = nsub * tq
    nq = SEQ // rows
    kv_dtype = F32 if mode == "mem" else BF16
    dep_specs, dep_args = _dep_operand(dep)
    mode_inputs = {"dil": 3, "na": 2, "mem": 1}[mode]

    def body(*refs):
        refs = list(refs)
        q_ref, k_ref, v_ref, do_ref, l_ref = refs[:5]
        rest = refs[5:5 + mode_inputs] + refs[5 + mode_inputs + len(dep_args):]
        bias_ref = tq_ref = tk_ref = db_ref = None
        if mode == "dil":
            dp_ref, tq_ref, tk_ref, dq_ref, dk_ref, dv_ref, dk_acc, dv_acc = rest
        elif mode == "na":
            o_ref, bias_ref, dq_ref, dk_ref, dv_ref, db_ref, dk_acc, dv_acc = rest
        else:
            o_ref, dq_ref, dk_ref, dv_ref, dk_acc, dv_acc = rest
        step = pl.program_id(1)

        @pl.when(step == 0)
        def _():
            dk_acc[...] = jnp.zeros((lk, 128), F32)
            dv_acc[...] = jnp.zeros((lk, 128), F32)
            if mode == "na":
                db_ref[...] = jnp.zeros(db_ref.shape, F32)

        lanes = _iota((tq, 128), 1)
        qis = [step * nsub + sub for sub in range(nsub)]
        sls = [slice(sub * tq, (sub + 1) * tq) for sub in range(nsub)]
        kss = [_window(mode, qi, tq, mlen, tk) for qi in qis]
        ks_ = [k_ref[pl.ds(ks, tk), :] for ks in kss]
        vs = [v_ref[pl.ds(ks, tk), :] for ks in kss]
        qsts, dosts, lses, dphs = [], [], [], []
        for sub in range(nsub):
            if mode == "dil":
                src = _folded_rows(qis[sub] * tq, tq, d)
                dov = do_ref[src, :].astype(BF16)
                lsev = l_ref[src, :]
                dphs.append(_stack_rows(dp_ref[src, :], lanes, pair))
            else:
                dov = do_ref[sls[sub], :]
                lsev = l_ref[sls[sub], :]
                dpv = dov.astype(F32) * o_ref[sls[sub], :]
                if pair:
                    dphs.append(jnp.concatenate(
                        [jnp.sum(jnp.where(_head_lanes(lanes, hh), dpv, 0.0), axis=1, keepdims=True)
                         for hh in range(2)], axis=0))
                else:
                    dphs.append(jnp.sum(dpv, axis=1, keepdims=True))
            qsts.append(_stack_heads(q_ref[sls[sub], :], lanes, pair, qscale))
            dosts.append(_stack_heads(dov, lanes, pair))
            lses.append(_stack_rows(lsev, lanes, pair))
        bands = [_band_mask(qi, tq, tk, ks) if mode == "dil" else None for qi, ks in zip(qis, kss)]
        ss = [_scores(mode, qsts[sub], ks_[sub], sscale, bands[sub], qis[sub], bias_ref, pair) for sub in range(nsub)]
        dpms = [lax.dot_general(dosts[sub], vs[sub], NT, preferred_element_type=F32) for sub in range(nsub)]
        ps = [jnp.exp(s_ - lse) for s_, lse in zip(ss, lses)]
        dss = [p * (dpm - dph) for p, dpm, dph in zip(ps, dpms, dphs)]
        if mode == "na":
            for sub, ds in enumerate(dss):
                off = qis[sub] - jnp.clip(qis[sub] - NA_ROWS // 2, 0, SEQ // GRID_W - NA_ROWS)
                db_ref[0, off] += ds[:tq]
                db_ref[1, off] += ds[tq:]
        dsbs = [ds.astype(BF16) for ds in dss]
        dvs = [lax.dot_general(p.astype(BF16), dosts[sub], TN, preferred_element_type=F32)
               for sub, p in enumerate(ps)]
        dqs = [jnp.dot(dsb, ks_[sub], preferred_element_type=F32) * scale for sub, dsb in enumerate(dsbs)]
        dks = [lax.dot_general(dsb, qsts[sub], TN, preferred_element_type=F32) for sub, dsb in enumerate(dsbs)]
        for sub in range(nsub):
            sl = sls[sub]
            dq = _unstack_heads(dqs[sub], lanes, pair, tq)
            if mode == "dil":
                dq = _rope_t(dq, tq_ref[0, sl, :], tq_ref[1, sl, :], tq_ref[2, sl, :])
            dq_ref[sl, :] = dq.astype(BF16)
            dk_acc[pl.ds(kss[sub], tk), :] += dks[sub] if pair else dks[sub] * scale
            dv_acc[pl.ds(kss[sub], tk), :] += dvs[sub]

        @pl.when(step == nq - 1)
        def _():
            dkv = dk_acc[...]
            if mode == "dil":
                dkv = _rope_t(dkv, tk_ref[0], tk_ref[1], tk_ref[2])
            dk_ref[...] = dkv.astype(kv_dtype)
            dv_ref[...] = dv_acc[...].astype(kv_dtype)

    q_spec = pl.BlockSpec((rows, 128), lambda u, i: (i, qcol + u))
    row_spec = pl.BlockSpec((rows, 128), lambda u, i: (i, u))
    kv_out = pl.BlockSpec((lk, 128), lambda u, i: (0, u))
    whole = pl.BlockSpec((SEQ, 128), lambda u, i: (0, u))
    nat_spec = whole if mode == "dil" else row_spec
    in_specs = [q_spec,
                pl.BlockSpec((lk, 128), lambda u, i: (0, kcol + u)),
                pl.BlockSpec((lk, 128), lambda u, i: (0, vcol + u)),
                nat_spec, nat_spec]
    args = [q_arr, k_arr, v_arr, do, lse]
    out_specs = [row_spec, kv_out, kv_out]
    out_shape = [jax.ShapeDtypeStruct((SEQ, 512), BF16), jax.ShapeDtypeStruct((lk, 512), kv_dtype),
                 jax.ShapeDtypeStruct((lk, 512), kv_dtype)]
    if mode == "dil":
        in_specs += [whole, pl.BlockSpec((3, rows, 128), lambda u, i: (0, i, 0)),
                     pl.BlockSpec((3, SEQ, 128), lambda u, i: (0, 0, 0))]
        args += [dp, tabs, tabs]
    elif mode == "na":
        b_spec = pl.BlockSpec((2, NA_ROWS, GRID_W, NA_ROWS * GRID_W), lambda u, i: (u, 0, 0, 0))
        in_specs += [row_spec, b_spec]
        args += [o, bias]
        out_specs.append(b_spec)
        out_shape.append(jax.ShapeDtypeStruct((8, NA_ROWS, GRID_W, NA_ROWS * GRID_W), F32))
    else:
        in_specs.append(row_spec)
        args.append(o)
    return pl.pallas_call(
        body, name=name, grid=(cfg["units"], nq), in_specs=in_specs + dep_specs, out_specs=out_specs,
        out_shape=out_shape, scratch_shapes=[pltpu.VMEM((lk, 128), F32), pltpu.VMEM((lk, 128), F32)],
        compiler_params=_params(("parallel", "arbitrary")))(*args, *dep_args)


def _na_geometry():
    qc = _iota((GRID_W, 128), 0)
    lane = _iota((GRID_W, 128), 1)
    kc = lane & 63
    c_start = jnp.clip(qc - 8, 0, GRID_W - 16)
    valid = jnp.logical_and(kc >= c_start, kc < c_start + 16)
    return lane, valid


def _na_bias(rpb_rows, dep=None):
    dep_specs, dep_args = _dep_operand(dep)

    def body(r_ref, *rest):
        o_ref, t_ref = rest[-2:]
        lane, valid = _na_geometry()
        for dd in range(14):
            row_a = jnp.broadcast_to(r_ref[dd:dd + 1, :], (GRID_W, 128))
            row_b = jnp.broadcast_to(r_ref[dd + 1:dd + 2, :], (GRID_W, 128))
            both = jnp.where(lane < 64, row_a, pltpu.roll(row_b, 64, 1))
            t = pltpu.roll(both, 128 - 15, 1, stride=1, stride_axis=0)
            t_ref[dd] = jnp.where(valid, t, NEG)
        for off in range(NA_ROWS):
            for p in range(4):
                o_ref[off, :, p * 128:(p + 1) * 128] = t_ref[2 * p - off + 7]

    return pl.pallas_call(
        body, name="na_bias", grid=(8,),
        in_specs=[pl.BlockSpec((None, 16, 128), lambda h: (h, 0, 0))] + dep_specs,
        out_specs=pl.BlockSpec((None, NA_ROWS, GRID_W, NA_ROWS * GRID_W), lambda h: (h, 0, 0, 0)),
        out_shape=jax.ShapeDtypeStruct((8, NA_ROWS, GRID_W, NA_ROWS * GRID_W), F32),
        scratch_shapes=[pltpu.VMEM((14, GRID_W, 128), F32)],
        compiler_params=_params(("parallel",)))(rpb_rows, *dep_args)


def _na_bias_bwd(dbias, dep=None):
    dep_specs, dep_args = _dep_operand(dep)

    def body(d_ref, *rest):
        o_ref = rest[-1]
        lane, valid = _na_geometry()
        reverse = (_iota((GRID_W, GRID_W), 0) + _iota((GRID_W, GRID_W), 1) == GRID_W - 1).astype(F32)
        o_ref[...] = jnp.zeros((16, 128), F32)
        for dd in range(14):
            t = jnp.zeros((GRID_W, 128), F32)
            for off in range(NA_ROWS):
                for p in range(4):
                    if 2 * p - off + 7 == dd:
                        t = t + d_ref[off, :, p * 128:(p + 1) * 128]
            t = jnp.dot(reverse, jnp.where(valid, t, 0.0), precision=lax.Precision.HIGHEST,
                        preferred_element_type=F32)
            t = pltpu.roll(t, 128 - (GRID_W - 16), 1, stride=1, stride_axis=0)
            o_ref[dd:dd + 1, :] = jnp.sum(t, axis=0, keepdims=True)

    return pl.pallas_call(
        body, name="na_bias_bwd", grid=(8,),
        in_specs=[pl.BlockSpec((None, NA_ROWS, GRID_W, NA_ROWS * GRID_W), lambda h: (h, 0, 0, 0))] + dep_specs,
        out_specs=pl.BlockSpec((None, 16, 128), lambda h: (h, 0, 0)),
        out_shape=jax.ShapeDtypeStruct((8, 16, 128), F32),
        compiler_params=_params(("parallel",)))(dbias, *dep_args)


GATE_ROWS = 128


def _group_weights(l0, l1, l2):
    m = jnp.maximum(jnp.maximum(l0, l1), l2)
    e0, e1, e2 = jnp.exp(l0 - m), jnp.exp(l1 - m), jnp.exp(l2 - m)
    inv = 1.0 / (e0 + e1 + e2)
    return e0 * inv, e1 * inv, e2 * inv


def _gate_block(o_grp, l_grp, out_b, out_c, parts, x, target, merge_bias, branch_rows, out_rows, gain, head_sum):
    rows = GATE_ROWS
    r512 = pl.BlockSpec((rows, 512), lambda i: (i, 0))
    r1024 = pl.BlockSpec((rows, D_MODEL), lambda i: (i, 0))
    silu_cols = [pl.BlockSpec((rows, 512), functools.partial(lambda b, i: (i, b), 13 + b)) for b in range(3)]
    logit_cols = [pl.BlockSpec((rows, D_MODEL), functools.partial(lambda b, i: (i, b), 8 + b)) for b in range(3)]

    def body(o0, o1, o2, l0, l1, l2, ob, oc, ga, gb, gc, la, lb, lc, x_ref, t_ref, mb, wa, wb, wc, wo_ref, gn_ref,
             hs_ref, dout_ref, dla, dlb, dlc, dga, dgb, dgc, do0, do1, do2, dp0, dp1, dp2, dob, doc, err_ref, gg_ref,
             gmb, gwa, gwb, gwc, gwo, acc_a, acc_b, acc_c, acc_o):
        step = pl.program_id(0)
        whole = lambda w_ref: w_ref[...].reshape(D_MODEL, w_ref.shape[-1])
        ws = _group_weights(l0[...], l1[...], l2[...])
        out_a = ws[0] * o0[...] + ws[1] * o1[...] + ws[2] * o2[...]
        branches = ((out_a, ga, la, wa, acc_a, dla, dga), (ob[...], gb, lb, wb, acc_b, dlb, dgb),
                    (oc[...], gc, lc, wc, acc_c, dlc, dgc))

        @pl.when(step == 0)
        def _():
            for acc in (acc_a, acc_b, acc_c, acc_o):
                acc[...] = jnp.zeros(acc.shape, F32)
            err_ref[...] = jnp.zeros((1, D_MODEL), F32)
            gg_ref[...] = jnp.zeros((1, D_MODEL), F32)
            gmb[...] = jnp.zeros((3, D_MODEL), F32)

        y = jnp.zeros((rows, D_MODEL), F32)
        zs, gates, silus, dsilus, us = [], [], [], [], []
        for b, (ov, g_ref, l_ref, w_ref, _, _, _) in enumerate(branches):
            g = g_ref[...].astype(F32)
            sg = _sigmoid(g)
            silus.append(g * sg)
            dsilus.append(sg * (1.0 + g * (1.0 - sg)))
            us.append((ov * silus[b]).astype(BF16))
            zs.append(lax.dot_general(us[b], whole(w_ref), NT, preferred_element_type=F32))
            gates.append(_sigmoid(l_ref[...].astype(F32) + mb[b:b + 1, :]))
            y = y + gates[b] * zs[b]
        yb = y.astype(BF16)
        y2 = jnp.dot(yb, whole(wo_ref), preferred_element_type=F32)
        rstd = lax.rsqrt(jnp.mean(y2 * y2, axis=1, keepdims=True) + EPS)
        yn = y2 * rstd
        gv = gn_ref[...]
        err = x_ref[...] + yn * gv - t_ref[...]
        dout = err * (1.0 / D_MODEL)
        dout_ref[...] = dout
        dn = dout * gv
        dy2 = (rstd * (dn - yn * jnp.mean(dn * yn, axis=1, keepdims=True))).astype(BF16)
        acc_o[...] += lax.dot_general(yb, dy2, TN, preferred_element_type=F32)
        err_ref[...] += jnp.sum(err * err, axis=0, keepdims=True)
        gg_ref[...] += jnp.sum(dout * yn, axis=0, keepdims=True)
        dy = lax.dot_general(dy2, whole(wo_ref), NT, preferred_element_type=F32)
        dos = []
        for b, (ov, _, _, w_ref, acc, dl_ref, dg_ref) in enumerate(branches):
            dl = dy * zs[b] * gates[b] * (1.0 - gates[b])
            dl_ref[...] = dl.astype(BF16)
            gmb[b:b + 1, :] += jnp.sum(dl, axis=0, keepdims=True)
            dz = (dy * gates[b]).astype(BF16)
            acc[...] += lax.dot_general(dz, us[b], TN, preferred_element_type=F32)
            du = jnp.dot(dz, whole(w_ref), preferred_element_type=F32)
            dos.append(du * silus[b])
            dg_ref[...] = (du * ov * dsilus[b]).astype(BF16)
        dob[...] = dos[1].astype(BF16)
        doc[...] = dos[2].astype(BF16)
        row_term = jnp.dot(dos[0] * out_a, hs_ref[...], precision=lax.Precision.HIGHEST, preferred_element_type=F32)
        for wg, do_ref, dp_ref in zip(ws, (do0, do1, do2), (dp0, dp1, dp2)):
            do_ref[...] = wg * dos[0]
            dp_ref[...] = wg * row_term

        @pl.when(step == SEQ // rows - 1)
        def _():
            for acc, out in ((acc_a, gwa), (acc_b, gwb), (acc_c, gwc), (acc_o, gwo)):
                out[...] = acc[...].astype(BF16)

    full = lambda shape: pl.BlockSpec(shape, lambda i: (0,) * len(shape))
    vec = pl.BlockSpec((1, D_MODEL), lambda i: (0, 0))
    acc3 = pl.BlockSpec((3, D_MODEL), lambda i: (0, 0))
    shard = D_MODEL // N_DEV
    dev_rows = lambda width, k: pl.BlockSpec((N_DEV, shard, width), lambda i: (0, k, 0))
    in_specs = ([r512] * 8 + silu_cols + logit_cols + [r1024, r1024, full((3, D_MODEL))]
                + [dev_rows(512, k) for k in range(3)] + [dev_rows(D_MODEL, 1), vec, full((512, 512))])
    out_specs = ([r1024] + [r1024] * 3 + [r512] * 3 + [r512] * 6 + [r512] * 2 + [vec, vec, acc3]
                 + [full((D_MODEL, 512))] * 3 + [full((D_MODEL, D_MODEL))])
    bf, f32 = BF16, F32
    sds = jax.ShapeDtypeStruct
    out_shape = ([sds((SEQ, D_MODEL), f32)] + [sds((SEQ, D_MODEL), bf)] * 3 + [sds((SEQ, 512), bf)] * 3
                 + [sds((SEQ, 512), f32)] * 6 + [sds((SEQ, 512), bf)] * 2 + [sds((1, D_MODEL), f32)] * 2
                 + [sds((3, D_MODEL), f32)] + [sds((D_MODEL, 512), bf)] * 3 + [sds((D_MODEL, D_MODEL), bf)])
    res = pl.pallas_call(
        body, name="gate_block", grid=(SEQ // rows,), in_specs=in_specs, out_specs=out_specs, out_shape=out_shape,
        scratch_shapes=[pltpu.VMEM((D_MODEL, 512), F32)] * 3 + [pltpu.VMEM((D_MODEL, D_MODEL), F32)],
        compiler_params=_params(("arbitrary",)))(
            *o_grp, *l_grp, out_b, out_c, parts, parts, parts, parts, parts, parts, x, target, merge_bias,
            branch_rows, branch_rows, branch_rows, out_rows, gain, head_sum)
    return dict(dout=res[0], dlog=res[1:4], dg=res[4:7], do_grp=res[7:10], dp_grp=res[10:13], do_b=res[13],
                do_c=res[14], err_sq=res[15], g_post=res[16], g_mb=res[17], g_wt=res[18:21], g_w_out=res[21])


def _local_step(x, hst, parts, tabs, bias, mem, target, pre_norm, mem_norm, post_norm, wt_in, late_weights,
                reduce_start=None):
    o_grp, l_grp = [], []
    for g, d in enumerate(DILATIONS):
        o, l = _attn_fwd("dil_fwd_%d" % g, "dil", parts, parts, parts, 12 * g, 12 * g + 4, 12 * g + 8, d=d)
        o_grp.append(o)
        l_grp.append(l)
    out_b, lse_b = _attn_fwd("na_fwd", "na", parts, parts, parts, 36, 40, 44, bias=bias)
    merge_bias, w_kv, branch_rows, out_rows = late_weights(sum(a[:8, :128] for a in [out_b] + o_grp))
    memn = _rmsnorm_fwd("memnorm", mem, mem_norm, MEM_LEN)
    kv_m = _mm_simple("mem_kv", memn, w_kv, NN, BF16, MEM_LEN, 512, D_MODEL)
    out_c, lse_c = _attn_fwd("mem_fwd", "mem", parts, kv_m, kv_m, 48, 0, 4)

    rr = _iota((512, 512), 0) // HEAD_DIM
    cc = _iota((512, 512), 1) // HEAD_DIM
    head_sum = (rr == cc).astype(F32)
    gb = _gate_block(o_grp, l_grp, out_b, out_c, parts, x, target, merge_bias, branch_rows, out_rows, post_norm,
                     head_sum)
    dout, dlog, dg, g_wt, g_w_out = gb["dout"], gb["dlog"], gb["dg"], gb["g_wt"], gb["g_w_out"]
    do_grp, dp_grp, do_b, do_c, g_post, g_mb = (gb["do_grp"], gb["dp_grp"], gb["do_b"], gb["do_c"], gb["g_post"],
                                                gb["g_mb"])
    loss = 0.5 * jnp.sum(gb["err_sq"]) / D_MODEL

    dq_c, dk_m, dv_m = _attn_bwd("mem_bwd", "mem", parts, kv_m, kv_m, 48, 0, 4, do_c, lse_c, o=out_c)
    dkv = jnp.concatenate([dk_m, dv_m], axis=1).astype(BF16)
    g_w_kv = _mm_simple("mem_kv_dw", memn, dkv, TN, BF16, D_MODEL, 512, MEM_LEN)
    dmemn = _mm_simple("mem_kv_dx", dkv, w_kv, NT, F32, MEM_LEN, 512, D_MODEL)
    grads = dict(w_kv=g_w_kv, wt_a=g_wt[0], wt_b=g_wt[1], wt_c=g_wt[2], w_out=g_w_out, merge_bias=g_mb,
                 post_norm=g_post)
    dep = reduce_start("rest_sibling", grads) if reduce_start is not None else None

    dq_b, dk_b, dv_b, dbias = _attn_bwd("na_bwd", "na", parts, parts, parts, 36, 40, 44, do_b, lse_b, o=out_b,
                                        bias=bias, dep=dep)
    dqkv = []
    for g, d in enumerate(DILATIONS):
        dq, dk, dv = _attn_bwd("dil_bwd_%d" % g, "dil", parts, parts, parts, 12 * g, 12 * g + 4, 12 * g + 8,
                               do_grp[g], l_grp[g], dp=dp_grp[g], d=d, tabs=tabs[g])
        dqkv += [dq, dk, dv]
    if reduce_start is not None:
        dep = reduce_start("rest_chips", grads, sum(a[:8, :128] for a in (dqkv[0], dqkv[3], dqkv[6], dq_b)))
    dparts = dqkv + [dq_b, dk_b, dv_b, dq_c] + list(dg) + list(dlog)
    grads["wt_in"] = _in_proj_dw(dparts, hst, dep)
    dep = reduce_start("w_in", grads) if reduce_start is not None else None
    dh = _in_proj_dh(dparts, wt_in, dep)
    if reduce_start is not None:
        dep = reduce_start("w_in_second", grads, dh)
    grad_x, grads["pre_norm"] = _prenorm_bwd(x, pre_norm, dh, dout)
    g_rpb_t = _na_bias_bwd(dbias, dep)
    grads["na_rpb"] = g_rpb_t[:, :15, :31] + jnp.pad(g_rpb_t[:, :14, 64:95], ((0, 0), (1, 0), (0, 0)))
    grads["mem_norm"] = _memnorm_bwd(mem, dmemn, dep)
    return loss, grad_x, grads


ANY = pl.BlockSpec(memory_space=pl.ANY)


def _place():
    return lax.axis_index("x"), lax.axis_index("y"), lax.axis_index("c")


HBM = pl.BlockSpec(memory_space=pltpu.HBM)
SEM = pl.BlockSpec(memory_space=pltpu.SEMAPHORE)
DATAFLOW = pltpu.SideEffectType.DATAFLOW_SIDE_EFFECTING


def _split_copies(kind, srcs, lands, send_sems, recv_sems):
    nt = len(srcs)
    x, y, c = _place()
    copies = []
    if kind == "sibling":
        for q in range(4):
            for t in range(nt):
                k = q * nt + t
                copies.append(pltpu.make_async_remote_copy(
                    src_ref=srcs[t].at[2 * q + 1 - c], dst_ref=lands[t].at[q], send_sem=send_sems.at[k],
                    recv_sem=recv_sems.at[k], device_id=(x, y, 1 - c), device_id_type=MESH_ID))
    elif kind in ("rs_a", "rs_b"):
        half = lands[0].shape[1]
        xn, yn = (1 - x, y, c), (x, 1 - y, c)
        q_xn, q_yn, q_dg = 2 * (1 - x) + y, 2 * x + 1 - y, 2 * (1 - x) + 1 - y
        if kind == "rs_a":
            plan = [(srcs[0].at[q_yn].at[pl.ds(0, half)], 0, yn), (srcs[0].at[q_dg].at[pl.ds(0, half)], 1, yn),
                    (srcs[0].at[q_xn].at[pl.ds(half, half)], 2, xn), (srcs[0].at[q_dg].at[pl.ds(half, half)], 3, xn)]
        else:
            plan = [(srcs[0].at[0], 0, xn), (srcs[0].at[1], 1, yn)]
        for k, (src, slot, to) in enumerate(plan):
            copies.append(pltpu.make_async_remote_copy(
                src_ref=src, dst_ref=lands[0].at[slot], send_sem=send_sems.at[k], recv_sem=recv_sems.at[k],
                device_id=to, device_id_type=MESH_ID))
    elif kind == "gather":
        me = 4 * x + 2 * y + c
        for mask in range(1, 8):
            fx, fy, fc = (mask >> 2) & 1, (mask >> 1) & 1, mask & 1
            to = (1 - x if fx else x, 1 - y if fy else y, 1 - c if fc else c)
            for t in range(nt):
                k = (mask - 1) * nt + t
                copies.append(pltpu.make_async_remote_copy(
                    src_ref=srcs[t], dst_ref=lands[t].at[me], send_sem=send_sems.at[k], recv_sem=recv_sems.at[k],
                    device_id=to, device_id_type=MESH_ID))
        for t in range(nt):
            copies.append(pltpu.make_async_copy(srcs[t], lands[t].at[me], recv_sems.at[7 * nt + t]))
    else:
        for s, (tx, ty) in enumerate([(1 - x, y), (x, 1 - y), (1 - x, 1 - y)]):
            for t in range(nt):
                k = s * nt + t
                copies.append(pltpu.make_async_remote_copy(
                    src_ref=srcs[t].at[2 * tx + ty], dst_ref=lands[t].at[s], send_sem=send_sems.at[k],
                    recv_sem=recv_sems.at[k], device_id=(tx, ty, c), device_id_type=MESH_ID))
    return copies


def _split_count(kind, nt):
    return {"gather": 8, "chips": 3, "sibling": 4, "rs_a": 4, "rs_b": 2}[kind] * nt


def _exchange_start(name, kind, srcs, land_shapes, after=None):
    nt = len(srcs)
    n = _split_count(kind, nt)
    dep_specs, dep_args = _dep_operand(after)
    nd = len(dep_args)

    def body(*refs):
        src_refs, land_refs = refs[:nt], refs[nt:2 * nt]
        send_sems, recv_sems = refs[2 * nt + nd], refs[2 * nt + nd + 1]
        token = refs[-1]
        for cp in _split_copies(kind, src_refs, land_refs, send_sems, recv_sems):
            cp.start()
        token[...] = jnp.zeros_like(token)

    lands = [pltpu.with_memory_space_constraint(lax.empty(s.shape, s.dtype), pltpu.HBM) for s in land_shapes]
    res = pl.pallas_call(
        body, name=name,
        out_shape=(pltpu.SemaphoreType.DMA((n,)), pltpu.SemaphoreType.DMA((n,)),
                   *[pltpu.HBM(s.shape, s.dtype) for s in srcs], *[pltpu.HBM(s.shape, s.dtype) for s in land_shapes],
                   jax.ShapeDtypeStruct((8, 128), F32)),
        in_specs=[HBM] * (2 * nt) + dep_specs,
        out_specs=(SEM, SEM, *([HBM] * (2 * nt)), pl.BlockSpec(memory_space=pltpu.VMEM)),
        input_output_aliases={i: 2 + i for i in range(2 * nt)},
        compiler_params=pltpu.CompilerParams(has_side_effects=DATAFLOW))(
            *[pltpu.with_memory_space_constraint(s, pltpu.HBM) for s in srcs], *lands, *dep_args)
    return res[0], res[1], list(res[2:2 + nt]), list(res[2 + nt:2 + 2 * nt]), res[-1]


def _exchange_wait(name, kind, send_sems, recv_sems, srcs, lands, after):
    nt = len(srcs)

    def body(*refs):
        src_refs, land_refs = refs[:nt], refs[nt:2 * nt]
        s_sems, r_sems = refs[2 * nt], refs[2 * nt + 1]
        for cp in _split_copies(kind, src_refs, land_refs, s_sems, r_sems):
            if cp.is_remote:
                cp.wait_send()
                cp.wait_recv()
            else:
                cp.wait()

    res = pl.pallas_call(
        body, name=name,
        out_shape=tuple(pltpu.HBM(s.shape, s.dtype) for s in list(srcs) + list(lands)),
        in_specs=[HBM] * (2 * nt) + [SEM, SEM, pl.BlockSpec(memory_space=pl.ANY)],
        out_specs=tuple([HBM] * (2 * nt)),
        input_output_aliases={i: i for i in range(2 * nt)},
        compiler_params=pltpu.CompilerParams(has_side_effects=DATAFLOW))(
            *srcs, *lands, send_sems, recv_sems, after)
    return list(res[:nt]), list(res[nt:])


AG_GROUPS = ((0, 3), (3, 4), (7, 2))


def _ag_phase(name, own, land, sems, waits, starts, after=None):
    r = own.shape[0]
    half = r // 2
    ns = len(sems)
    dep_specs, dep_args = _dep_operand(after)
    nd = len(dep_args)
    new_group = None
    if starts:
        (new_group,) = [g for g, (first, n) in enumerate(AG_GROUPS) if first == starts[0]]
        assert list(starts) == list(range(AG_GROUPS[new_group][0], sum(AG_GROUPS[new_group])))

    def body(*refs):
        own_ref, land_ref = refs[0], refs[1]
        sem_refs = list(refs[2:2 + 2 * ns])
        outs = refs[2 + 2 * ns + nd:]
        if starts:
            sem_refs += [outs[0], outs[1]]
        x, y, c = _place()
        me, sib = (x, y, c), (x, y, 1 - c)
        xn, yn, dg = (1 - x, y, c), (x, 1 - y, c), (1 - x, 1 - y, c)

        def other(dev):
            return (dev[0], dev[1], 1 - dev[2])

        def rows(dev, part):
            blk = land_ref.at[4 * dev[0] + 2 * dev[1] + dev[2]]
            return blk if part is None else blk.at[pl.ds(part * half, half)]

        def sem_of(k):
            (g,) = [g for g, (first, n) in enumerate(AG_GROUPS) if first <= k < first + n]
            return sem_refs[2 * g].at[k - AG_GROUPS[g][0]], sem_refs[2 * g + 1].at[k - AG_GROUPS[g][0]]

        sent = {0: (me, None, sib), 1: (me, None, xn), 2: (me, None, yn), 3: (xn, 0, yn), 4: (yn, 1, xn),
                5: (xn, None, sib), 6: (yn, None, sib), 7: (dg, 0, sib), 8: (dg, 1, sib)}
        landed = {0: (sib, None), 1: (xn, None), 2: (yn, None), 3: (dg, 0), 4: (dg, 1), 5: (other(xn), None),
                  6: (other(yn), None), 7: (other(dg), 0), 8: (other(dg), 1)}

        def copy(k, receiving):
            send_sem, recv_sem = sem_of(k)
            dev, part, to = (*landed[k], me) if receiving else sent[k]
            src = own_ref if (dev is me and not receiving) else rows(dev, part)
            return pltpu.make_async_remote_copy(src_ref=src, dst_ref=rows(dev, part), send_sem=send_sem,
                                                recv_sem=recv_sem, device_id=to, device_id_type=MESH_ID)

        def own_copy():
            return pltpu.make_async_copy(own_ref, rows(me, None), sem_refs[1].at[AG_GROUPS[0][1]])

        if new_group == 0:
            own_copy().start(priority=1)
        for kind, k in waits:
            if kind == "recv":
                copy(k, True).wait_recv()
            elif kind == "own":
                own_copy().wait()
            else:
                copy(k, False).wait_send()
        for k in starts:
            copy(k, False).start()
        if starts:
            outs[-1][...] = jnp.zeros_like(outs[-1])

    n_new = AG_GROUPS[new_group][1] if starts else 0
    sem_out = (pltpu.SemaphoreType.DMA((n_new + 1,)), pltpu.SemaphoreType.DMA((n_new + 1,))) if starts else ()
    token_out = (jax.ShapeDtypeStruct((8, 128), F32),) if starts else ()
    res = pl.pallas_call(
        body, name=name,
        out_shape=(*sem_out, pltpu.HBM(own.shape, own.dtype), pltpu.HBM(land.shape, land.dtype), *token_out),
        in_specs=[HBM, HBM] + [SEM] * (2 * ns) + dep_specs,
        out_specs=(*([SEM] * len(sem_out)), HBM, HBM, *([pl.BlockSpec(memory_space=pltpu.VMEM)] * len(token_out))),
        input_output_aliases={0: len(sem_out), 1: len(sem_out) + 1},
        compiler_params=pltpu.CompilerParams(has_side_effects=DATAFLOW))(
            own, land, *[a for pair in sems for a in pair], *dep_args)
    if starts:
        return (res[0], res[1]), res[2], res[3], res[4]
    return None, res[0], res[1], None


def _add_sibling(name, term, recv, rows):
    _, r, w = term.shape
    cidx = lax.axis_index("c").astype(jnp.int32).reshape(1)
    like_term = recv.shape[0] == N_DEV

    def body(c_ref, a_ref, b_ref, o_ref):
        o_ref[...] = (a_ref[...].astype(F32) + b_ref[...].astype(F32)).astype(o_ref.dtype)

    grid_spec = pltpu.PrefetchScalarGridSpec(
        num_scalar_prefetch=1, grid=(4, r // rows),
        in_specs=[pl.BlockSpec((None, rows, w), lambda q, i, c_ref: (2 * q + c_ref[0], i, 0)),
                  pl.BlockSpec((None, rows, w), lambda q, i, c_ref: (2 * q + c_ref[0] if like_term else q, i, 0))],
        out_specs=pl.BlockSpec((None, rows, w), lambda q, i, c_ref: (q, i, 0)))
    return pl.pallas_call(
        body, name=name, grid_spec=grid_spec, out_shape=jax.ShapeDtypeStruct((4, r, w), term.dtype),
        compiler_params=_params(("parallel", "parallel")))(cidx, term, recv)


def _add_sibling_small(name, terms, recvs):
    nt = len(terms)

    def body(*refs):
        c = lax.axis_index("c")
        for t_ref, r_ref, o_ref in zip(refs[:nt], refs[nt:2 * nt], refs[2 * nt:]):
            for q in range(4):
                o_ref[q] = (t_ref[2 * q + c].astype(F32) + r_ref[q].astype(F32)).astype(o_ref.dtype)

    return pl.pallas_call(
        body, name=name, out_shape=[jax.ShapeDtypeStruct((4,) + t.shape[1:], t.dtype) for t in terms],
        compiler_params=_params())(*terms, *recvs)


def _reduce_scatter_start(tag, terms, recv1):
    sums = _add_sibling_small("add_sibling_" + tag, terms, recv1)
    lands = [jax.ShapeDtypeStruct((3,) + s.shape[1:], s.dtype) for s in sums]
    send_sems, recv_sems, sums, lands, token = _exchange_start("exchange_chips_start_" + tag, "chips", sums, lands)
    return (tag, send_sems, recv_sems, sums, lands), token


def _reduce_scatter_wait(state, after):
    tag, send_sems, recv_sems, sums, lands = state
    return _exchange_wait("exchange_chips_wait_" + tag, "chips", send_sems, recv_sems, sums, lands, after)


def _adam_math(w, g, m, v):
    nm = ADAM_B1 * m + (1.0 - ADAM_B1) * g
    nv = ADAM_B2 * v + (1.0 - ADAM_B2) * (g * g)
    c1 = 1.0 - ADAM_B1 ** ADAM_STEP
    c2 = 1.0 - ADAM_B2 ** ADAM_STEP
    return -ADAM_LR * ((nm / c1) / (jnp.sqrt(nv / c2) + ADAM_EPS) + ADAM_WD * w), nm, nv


def _presum_halves(sums, landed):
    _, r, w = sums.shape
    rows = r // 2
    x, y = lax.axis_index("x"), lax.axis_index("y")
    dest = jnp.stack([2 * (1 - x) + y, 2 * x + 1 - y]).astype(jnp.int32)

    def body(q_ref, a_ref, b_ref, o_ref):
        o_ref[...] = (a_ref[...].astype(F32) + b_ref[...].astype(F32)).astype(o_ref.dtype)

    grid_spec = pltpu.PrefetchScalarGridSpec(
        num_scalar_prefetch=1, grid=(2,),
        in_specs=[pl.BlockSpec((None, rows, w), lambda h, q_ref: (q_ref[h], h, 0)),
                  pl.BlockSpec((None, rows, w), lambda h, q_ref: (1 + 2 * h, 0, 0))],
        out_specs=pl.BlockSpec((None, rows, w), lambda h, q_ref: (h, 0, 0)))
    return pl.pallas_call(
        body, name="presum_halves", grid_spec=grid_spec, out_shape=jax.ShapeDtypeStruct((2, r // 2, w), sums.dtype),
        compiler_params=_params(("parallel",)))(dest, sums, landed)


def _adamw_halves(name, sums, landed_a, landed_b, w, m, v, rows):
    r, c = w.shape
    half = c // 2
    qidx = (2 * lax.axis_index("x") + lax.axis_index("y")).astype(jnp.int32).reshape(1)

    def body(q_ref, s_ref, a_ref, b_ref, w_ref, m_ref, v_ref, g_ref, d_ref, nm_ref, nv_ref):
        first = (s_ref[:half, :].astype(F32) + a_ref[0].astype(F32)) + b_ref[0].astype(F32)
        second = (s_ref[half:, :].astype(F32) + a_ref[2].astype(F32)) + b_ref[1].astype(F32)
        g = jnp.concatenate([first, second], axis=0).T
        g_ref[...] = g
        d_ref[...], nm_ref[...], nv_ref[...] = _adam_math(w_ref[...], g, m_ref[...], v_ref[...])

    row = pl.BlockSpec((rows, c), lambda i, q_ref: (i, 0))
    grid_spec = pltpu.PrefetchScalarGridSpec(
        num_scalar_prefetch=1, grid=(r // rows,),
        in_specs=[pl.BlockSpec((None, c, rows), lambda i, q_ref: (q_ref[0], 0, i)),
                  pl.BlockSpec((4, half, rows), lambda i, q_ref: (0, 0, i)),
                  pl.BlockSpec((2, half, rows), lambda i, q_ref: (0, 0, i)), row, row, row],
        out_specs=[row] * 4)
    return pl.pallas_call(
        body, name=name, grid_spec=grid_spec, out_shape=[jax.ShapeDtypeStruct((r, c), F32)] * 4,
        compiler_params=_params(("parallel",)))(qidx, sums, landed_a, landed_b, w, m, v)


def _adamw_chips_small(name, items):
    n = len(items)

    def body(*refs):
        q = 2 * lax.axis_index("x") + lax.axis_index("y")
        ins, outs = refs[:5 * n], refs[5 * n:]
        for i, (_, _, w, _, _, transposed) in enumerate(items):
            s_ref, r_ref, w_ref, m_ref, v_ref = ins[5 * i:5 * i + 5]
            g_ref, d_ref, nm_ref, nv_ref = outs[4 * i:4 * i + 4]
            g = (s_ref[q].astype(F32) + r_ref[0].astype(F32)) + (r_ref[1].astype(F32) + r_ref[2].astype(F32))
            g = g.T if transposed else g[:w.shape[0]]
            g_ref[...] = g
            d_ref[...], nm_ref[...], nv_ref[...] = _adam_math(w_ref[...], g, m_ref[...], v_ref[...])

    res = pl.pallas_call(
        body, name=name, out_shape=[jax.ShapeDtypeStruct(it[2].shape, F32) for it in items for _ in range(4)],
        compiler_params=_params())(*[a for it in items for a in it[:5]])
    return [res[4 * i:4 * i + 4] for i in range(n)]


def _adamw_replicated(gathered, items):
    n = len(items)

    def body(g_ref, *refs):
        ins, t_ref, outs = refs[:3 * n], refs[3 * n], refs[3 * n + 1:]
        acc = g_ref[0]
        for j in range(1, N_DEV):
            acc = acc + g_ref[j]
        t_ref[...] = acc
        for i, (first, w, _, _) in enumerate(items):
            w_ref, m_ref, v_ref = ins[3 * i:3 * i + 3]
            g = t_ref[first:first + w.shape[0], :]
            outs[3 * i][...], outs[3 * i + 1][...], outs[3 * i + 2][...] = _adam_math(w_ref[...], g, m_ref[...], v_ref[...])

    res = pl.pallas_call(
        body, name="adamw_replicated",
        out_shape=[jax.ShapeDtypeStruct(gathered.shape[1:], F32)]
        + [jax.ShapeDtypeStruct(it[1].shape, F32) for it in items for _ in range(3)],
        compiler_params=_params())(gathered, *[a for it in items for a in it[1:]])
    return res[0], [res[1 + 3 * i:4 + 3 * i] for i in range(n)]


def _wide_rows(a):
    rows = -(-a.size // D_MODEL)
    return jnp.pad(a.reshape(-1), (0, rows * D_MODEL - a.size)).reshape(rows, D_MODEL)


def kernel(x, mem, pre_norm, w_in, merge_bias, na_rpb, mem_norm, w_mem_kv, w_branch_a, w_branch_b, w_branch_c, w_out, post_norm, loss_target, m_pre_norm, m_w_in, m_merge_bias, m_na_rpb, m_mem_norm, m_w_mem_kv, m_w_branch_a, m_w_branch_b, m_w_branch_c, m_w_out, m_post_norm, v_pre_norm, v_w_in, v_merge_bias, v_na_rpb, v_mem_norm, v_w_mem_kv, v_w_branch_a, v_w_branch_b, v_w_branch_c, v_w_out, v_post_norm):
    wt_in_s = w_in[0].T.astype(BF16)
    rows_s = jnp.concatenate([w_mem_kv[0], w_out[0]], axis=0).astype(BF16)
    cols_s = jnp.concatenate([w_branch_a[0].T, w_branch_b[0].T, w_branch_c[0].T], axis=0).astype(BF16)
    mb_s = jnp.pad(merge_bias[0], ((0, 5), (0, 0)))

    chip = 2 * lax.axis_index("x") + lax.axis_index("y")

    def first_block(q):
        return jnp.where(q == 0, 0, jnp.where(q == 1, 6, jnp.where(q == 2, 11, 17)))

    five = jnp.arange(5, dtype=jnp.int32)
    near, far = jnp.where(chip < 2, 5, 16), jnp.where(chip < 2, 16, 5)
    order1 = (first_block(chip) + five).astype(jnp.int32)
    order2 = jnp.concatenate([first_block(chip ^ 1) + five, near[None], first_block(chip ^ 2) + five]).astype(jnp.int32)
    order3 = jnp.concatenate([first_block(chip ^ 3) + five, far[None]]).astype(jnp.int32)
    tabs = _rope_tables()

    def weights_of(land):
        return land.reshape(N_IN, D_MODEL)

    land = pltpu.with_memory_space_constraint(lax.empty((N_DEV,) + wt_in_s.shape, BF16), pltpu.HBM)
    own = pltpu.with_memory_space_constraint(wt_in_s, pltpu.HBM)
    sem_a, own, land, token = _ag_phase("ag_start", own, land, [], [], [0, 1, 2])
    hs, hst = _prenorm_fold(x[0], pre_norm, token)
    _, own, land, _ = _ag_phase("ag_wait0", own, land, [sem_a], [("recv", 0), ("own", 0)], [], hs)
    parts = _in_proj("in_proj_1", hs, weights_of(land), tabs, order1)
    bias = _na_bias(jnp.pad(na_rpb[0], ((0, 0), (0, 1), (0, 128 - 31))), parts)
    sem_b, own, land, _ = _ag_phase("ag_mid1", own, land, [sem_a], [("recv", 1), ("recv", 2)], [3, 4, 5, 6], bias)
    _, own, land, _ = _ag_phase("ag_wait1", own, land, [sem_a, sem_b], [("recv", 5), ("recv", 6)], [])
    parts = _in_proj("in_proj_2", hs, weights_of(land), tabs, order2, parts)
    sem_c, own, land, _ = _ag_phase("ag_mid2", own, land, [sem_a, sem_b], [("recv", 3), ("recv", 4)], [7, 8], parts)
    _, own, land, _ = _ag_phase("ag_end", own, land, [sem_a, sem_b, sem_c],
                                [("recv", 7), ("recv", 8)] + [("send", k) for k in range(9)], [])
    wt_in = weights_of(land)

    late_own = [rows_s, cols_s, mb_s]
    late_lands = [jax.ShapeDtypeStruct((N_DEV,) + s.shape, s.dtype) for s in late_own]
    l_send, l_recv, late_own, late_lands, late_token = _exchange_start("gather_late_start", "gather", late_own,
                                                                       late_lands, after=wt_in)
    parts = _in_proj("in_proj_3", hs, wt_in, tabs, order3, parts, late_token)

    def late_weights(after):
        _, lands = _exchange_wait("gather_late_wait", "gather", l_send, l_recv, late_own, late_lands, after)
        g_rows, g_cols, g_mb = lands
        return (g_mb[:, :3].transpose(1, 0, 2).reshape(3, D_MODEL), g_rows[:, :128].reshape(D_MODEL, D_MODEL),
                g_cols, g_rows)

    rest_state, rest_sibling, w_in_a, w_in_b = [], [], [], []

    def reduce_start(phase, grads, after=None):
        if phase == "rest_sibling":
            gmb_t = jnp.pad(grads["merge_bias"].reshape(3, N_DEV, 128).transpose(1, 0, 2), ((0, 0), (0, 5), (0, 0)))
            terms = [grads["w_kv"].reshape(N_DEV, 128, D_MODEL), grads["w_out"].reshape(N_DEV, 128, D_MODEL),
                     grads["wt_a"].reshape(N_DEV, 128, 512), grads["wt_b"].reshape(N_DEV, 128, 512),
                     grads["wt_c"].reshape(N_DEV, 128, 512), gmb_t]
            lands = [jax.ShapeDtypeStruct((4,) + t.shape[1:], t.dtype) for t in terms]
            started = _exchange_start("exchange_sibling_start_rest", "sibling", terms, lands)
            rest_sibling.extend(started[:4])
            return started[4]
        if phase == "rest_chips":
            s_send, s_recv, terms, lands = rest_sibling
            terms, recv1 = _exchange_wait("exchange_sibling_wait_rest", "sibling", s_send, s_recv, terms, lands, after)
            state, token = _reduce_scatter_start("rest", terms, recv1)
            rest_state.append(state)
            return token
        if phase == "w_in":
            own, sibling = [a.reshape(N_DEV, SHARD_IN, D_MODEL) for a in grads["wt_in"]]
            sums = _add_sibling("add_sibling_w_in", own, sibling, SHARD_IN)
            lands = [jax.ShapeDtypeStruct((4, SHARD_IN // 2, D_MODEL), BF16)]
            w_in_a.extend(_exchange_start("rs_a_start", "rs_a", [sums], lands))
            return w_in_a[4]
        (sums,), (landed_a,) = _exchange_wait("rs_a_wait", "rs_a", w_in_a[0], w_in_a[1], w_in_a[2], w_in_a[3], after)
        lands = [jax.ShapeDtypeStruct((2, SHARD_IN // 2, D_MODEL), BF16)]
        w_in_b.extend(_exchange_start("rs_b_start", "rs_b", [_presum_halves(sums, landed_a)], lands))
        w_in_b.extend([sums, landed_a])
        return w_in_b[4]

    loss_term, grad_x, grads = _local_step(
        x[0], hst, parts, tabs, bias, mem[0], loss_target[0], pre_norm, mem_norm, post_norm, wt_in, late_weights,
        reduce_start=reduce_start)

    replicated = ("pre_norm", "mem_norm", "post_norm", "na_rpb")
    pieces = [_wide_rows(grads[n]) for n in replicated] + [_wide_rows(loss_term)]
    first_rows = [sum(p.shape[0] for p in pieces[:i]) for i in range(len(pieces))]
    small = jnp.concatenate(pieces, axis=0)
    s_send, s_recv, s_own, s_land, s_token = _exchange_start(
        "gather_small_start", "gather", [small], [jax.ShapeDtypeStruct((N_DEV,) + small.shape, F32)])
    grad = {}
    weights = {
        "pre_norm": (pre_norm, m_pre_norm, v_pre_norm), "w_in": (w_in, m_w_in, v_w_in),
        "merge_bias": (merge_bias, m_merge_bias, v_merge_bias), "na_rpb": (na_rpb, m_na_rpb, v_na_rpb),
        "mem_norm": (mem_norm, m_mem_norm, v_mem_norm), "w_mem_kv": (w_mem_kv, m_w_mem_kv, v_w_mem_kv),
        "w_branch_a": (w_branch_a, m_w_branch_a, v_w_branch_a), "w_branch_b": (w_branch_b, m_w_branch_b, v_w_branch_b),
        "w_branch_c": (w_branch_c, m_w_branch_c, v_w_branch_c), "w_out": (w_out, m_w_out, v_w_out),
        "post_norm": (post_norm, m_post_norm, v_post_norm)}
    order = ["pre_norm", "w_in", "merge_bias", "na_rpb", "mem_norm", "w_mem_kv", "w_branch_a", "w_branch_b",
             "w_branch_c", "w_out", "post_norm"]
    delta, new_m, new_v = {}, {}, {}

    sums, recv2 = _reduce_scatter_wait(rest_state[0], s_token)
    rest = (("w_mem_kv", False), ("w_out", False), ("w_branch_a", True), ("w_branch_b", True), ("w_branch_c", True),
            ("merge_bias", False))
    items = [(sums[i], recv2[i], *[a[0] for a in weights[n]], transposed) for i, (n, transposed) in enumerate(rest)]
    for (n, _), (g, dl, nm, nv) in zip(rest, _adamw_chips_small("adamw_rest", items)):
        grad[n], delta[n], new_m[n], new_v[n] = g[None], dl[None], nm[None], nv[None]
    s_own, s_land = _exchange_wait("gather_small_wait", "gather", s_send, s_recv, s_own, s_land, delta["w_out"])
    items = [(first, *[_wide_rows(a) for a in weights[n]]) for n, first in zip(replicated, first_rows)]
    total, updates = _adamw_replicated(s_land[0], items)
    loss = total[first_rows[-1], 0]
    for n, first, it, (dl, nm, nv) in zip(replicated, first_rows, items, updates):
        w = weights[n][0]
        grad[n], delta[n], new_m[n], new_v[n] = [
            a.reshape(-1)[:w.size].reshape(w.shape) for a in (total[first:first + it[1].shape[0]], dl, nm, nv)]
    _, (landed_b,) = _exchange_wait("rs_b_wait", "rs_b", w_in_b[0], w_in_b[1], w_in_b[2], w_in_b[3], updates[-1][0])
    g, dl, nm, nv = _adamw_halves("adamw_w_in", w_in_b[5], w_in_b[6], landed_b, w_in[0], m_w_in[0], v_w_in[0], 256)
    grad["w_in"], delta["w_in"], new_m["w_in"], new_v["w_in"] = g[None], dl[None], nm[None], nv[None]

    return (loss, grad_x[None], *[grad[n] for n in order], *[delta[n] for n in order],
            *[new_m[n] for n in order], *[new_v[n] for n in order])
```
